```python
import jax, jax.numpy as jnp
from jax import lax
import numpy as np

D_MODEL = 1024
BATCH = 8
SEQ = 4096
DEPTH = 2

CHUNK = 128
A_GROUPS = 4
A_GROUP_DIM = 128
A_WIDTH = A_GROUPS * A_GROUP_DIM
DIL_WINDOWS = (128, 512, 2048)
DIL_RATES = (1, 4, 16)
N_DIL = 3
B_HEADS = 8
B_HEAD_DIM = 64
B_WIDTH = B_HEADS * B_HEAD_DIM
ROPE_DIM = B_HEAD_DIM // 4
ROPE_THETA = 500000.0
C_WIDTH = 512
C_KERNEL = 31
D_WIDTH = 512
D_KERNEL = 3
D_FF = -(-8 * D_MODEL // (3 * 256)) * 256
EPS = 1e-6
NEG_INF = -1e30

AB_IN = 2 * A_WIDTH + 3 * N_DIL * B_HEADS * B_HEAD_DIM
AB_OUT = A_WIDTH + B_WIDTH
CD_IN = 2 * C_WIDTH + 3 * D_WIDTH
CD_OUT = C_WIDTH + D_WIDTH
N_AB = (DEPTH + 1) // 2
N_CD = DEPTH // 2

kernel_name = 'hybrid_gmlp_dilattn_conformer_shortconv'


def rms_norm(x, g):
    xf = x.astype(jnp.float32)
    y = xf * lax.rsqrt(jnp.mean(xf * xf, axis=-1, keepdims=True) + EPS)
    return (y * g.astype(jnp.float32)).astype(x.dtype)


def layer_norm(x, g, b):
    xf = x.astype(jnp.float32)
    xc = xf - jnp.mean(xf, axis=-1, keepdims=True)
    y = xc * lax.rsqrt(jnp.mean(xc * xc, axis=-1, keepdims=True) + EPS)
    return (y * g.astype(jnp.float32) + b.astype(jnp.float32)).astype(x.dtype)


def apply_rope(x, pos):
    half = ROPE_DIM // 2
    inv_freq = ROPE_THETA ** (-jnp.arange(half, dtype=jnp.float32) * 2.0 / ROPE_DIM)
    ang = pos[:, None] * inv_freq[None, :]
    cos = jnp.cos(ang)[None, :, None, :]
    sin = jnp.sin(ang)[None, :, None, :]
    xf = x.astype(jnp.float32)
    x1 = xf[..., :half]
    x2 = xf[..., half:ROPE_DIM]
    out = jnp.concatenate([x1 * cos - x2 * sin, x1 * sin + x2 * cos, xf[..., ROPE_DIM:]], axis=-1)
    return out.astype(x.dtype)


def causal_depthwise_conv(x, w):
    k, c = w.shape
    return lax.conv_general_dilated(
        x, w.astype(x.dtype)[:, None, :], window_strides=(1,), padding=[(k - 1, 0)],
        dimension_numbers=('NWC', 'WIO', 'NWC'), feature_group_count=c)


def chunked_spatial_gating(u, v, w_s, b_s):
    bn, s, _ = u.shape
    nc = s // CHUNK
    vg = v.reshape(bn, nc, CHUNK, A_GROUPS, A_GROUP_DIM)
    causal = jnp.tril(jnp.ones((CHUNK, CHUNK), dtype=bool))
    w = jnp.where(causal[None], w_s, 0.0).astype(v.dtype)
    mixed = jnp.einsum('gts,bnsgc->bntgc', w, vg) + b_s.T.astype(v.dtype)[None, None, :, :, None]
    return u * mixed.reshape(bn, s, A_WIDTH)


def dilated_group_attention(q, k, v, rate, n_back):
    bn, s, h, dh = q.shape
    span = rate * n_back
    s_pad = -(-s // span) * span
    length = s_pad // rate
    nb = length // n_back

    def to_blocks(t):
        t = jnp.pad(t, ((0, 0), (0, s_pad - s), (0, 0), (0, 0)))
        t = t.reshape(bn, length, rate, h, dh).transpose(0, 2, 1, 3, 4)
        return t.reshape(bn, rate, nb, n_back, h, dh)

    def with_prev(t):
        prev = jnp.pad(t[:, :, :-1], ((0, 0), (0, 0), (1, 0), (0, 0), (0, 0), (0, 0)))
        return jnp.concatenate([prev, t], axis=3)

    qb = to_blocks(q)
    kk = with_prev(to_blocks(k))
    vv = with_prev(to_blocks(v))
    scores = jnp.einsum('brnqhd,brnkhd->brnhqk', qb.astype(jnp.float32),
                        kk.astype(jnp.float32)) * (dh ** -0.5)
    qi = jnp.arange(n_back)[:, None]
    kj = jnp.arange(2 * n_back)[None, :]
    dist = qi + n_back - kj
    band = (dist >= 0) & (dist <= n_back)
    exists = (jnp.arange(nb)[:, None, None] > 0) | (kj[None] >= n_back)
    mask = band[None] & exists
    scores = jnp.where(mask[None, None, :, None], scores, NEG_INF)
    lse = jax.nn.logsumexp(scores, axis=-1)
    probs = jnp.exp(scores - lse[..., None])
    out = jnp.einsum('brnhqk,brnkhd->brnqhd', probs, vv.astype(jnp.float32))
    out = out.reshape(bn, rate, length, h, dh).transpose(0, 2, 1, 3, 4).reshape(bn, s_pad, h, dh)[:, :s]
    lse = lse.transpose(0, 1, 2, 4, 3).reshape(bn, rate, length, h)
    lse = lse.transpose(0, 2, 1, 3).reshape(bn, s_pad, h)[:, :s]
    return out, lse


def mixer_ab(h, w_in, sgu_g, sgu_b, sgu_w, sgu_bias, q_g, k_g, w_out):
    bn, s, _ = h.shape
    proj = h @ w_in
    a = jax.nn.gelu(proj[..., :2 * A_WIDTH], approximate=False)
    u, v = a[..., :A_WIDTH], a[..., A_WIDTH:]
    a_out = chunked_spatial_gating(u, layer_norm(v, sgu_g, sgu_b), sgu_w, sgu_bias)
    qkv = proj[..., 2 * A_WIDTH:].reshape(bn, s, 3, N_DIL, B_HEADS, B_HEAD_DIM)
    pos = jnp.arange(s, dtype=jnp.float32)
    outs, lses = [], []
    for g in range(N_DIL):
        q = apply_rope(rms_norm(qkv[:, :, 0, g], q_g[g]), pos)
        k = apply_rope(rms_norm(qkv[:, :, 1, g], k_g[g]), pos)
        o, l = dilated_group_attention(q, k, qkv[:, :, 2, g], DIL_RATES[g], DIL_WINDOWS[g] // DIL_RATES[g])
        outs.append(o)
        lses.append(l)
    wts = jax.nn.softmax(jnp.stack(lses), axis=0)
    b_out = jnp.sum(wts[..., None] * jnp.stack(outs), axis=0).astype(h.dtype).reshape(bn, s, B_WIDTH)
    return jnp.concatenate([a_out, b_out], axis=-1) @ w_out


def mixer_cd(h, w_in, conv_c_w, conv_c_b, c_ln_g, c_ln_b, conv_d_w, w_out):
    proj = h @ w_in
    c = proj[..., :C_WIDTH] * jax.nn.sigmoid(proj[..., C_WIDTH:2 * C_WIDTH])
    c = causal_depthwise_conv(c, conv_c_w) + conv_c_b.astype(c.dtype)
    c = jax.nn.silu(layer_norm(c, c_ln_g, c_ln_b))
    gate_b, gate_c, hv = jnp.split(proj[..., 2 * C_WIDTH:], 3, axis=-1)
    d = gate_b * causal_depthwise_conv(gate_c * hv, conv_d_w)
    return jnp.concatenate([c, d], axis=-1) @ w_out


def swiglu(h, w_gate, w_up, w_down):
    return (jax.nn.silu(h @ w_gate) * (h @ w_up)) @ w_down


def _fwd_setup_inputs(seed: int = 0) -> dict:
    key = jax.random.key(seed)
    ks = jax.random.split(key, 24)
    f32 = jnp.float32

    def nrm(k, shape, scale):
        return jax.random.normal(k, shape, f32) * scale

    def gain(k, shape):
        return 1.0 + 0.05 * jax.random.normal(k, shape, f32)

    return {
        'x': jax.random.normal(ks[0], (BATCH, SEQ, D_MODEL), f32),
        'ab_norm_g': gain(ks[1], (N_AB, D_MODEL)),
        'ab_w_in': nrm(ks[2], (N_AB, D_MODEL, AB_IN), D_MODEL ** -0.5),
        'sgu_norm_g': gain(ks[3], (N_AB, A_WIDTH)),
        'sgu_norm_b': nrm(ks[4], (N_AB, A_WIDTH), 0.02),
        'sgu_w': nrm(ks[5], (N_AB, A_GROUPS, CHUNK, CHUNK), 0.5 * CHUNK ** -0.5),
        'sgu_bias': gain(ks[6], (N_AB, A_GROUPS, CHUNK)),
        'q_norm_g': gain(ks[7], (N_AB, N_DIL, B_HEAD_DIM)),
        'k_norm_g': gain(ks[8], (N_AB, N_DIL, B_HEAD_DIM)),
        'ab_w_out': nrm(ks[9], (N_AB, AB_OUT, D_MODEL), AB_OUT ** -0.5),
        'cd_norm_g': gain(ks[10], (N_CD, D_MODEL)),
        'cd_w_in': nrm(ks[11], (N_CD, D_MODEL, CD_IN), D_MODEL ** -0.5),
        'conv_c_w': nrm(ks[12], (N_CD, C_KERNEL, C_WIDTH), C_KERNEL ** -0.5),
        'conv_c_b': nrm(ks[13], (N_CD, C_WIDTH), 0.02),
        'c_ln_g': gain(ks[14], (N_CD, C_WIDTH)),
        'c_ln_b': nrm(ks[15], (N_CD, C_WIDTH), 0.02),
        'conv_d_w': nrm(ks[16], (N_CD, D_KERNEL, D_WIDTH), D_KERNEL ** -0.5),
        'cd_w_out': nrm(ks[17], (N_CD, CD_OUT, D_MODEL), CD_OUT ** -0.5),
        'ffn_norm_g': gain(ks[18], (DEPTH, D_MODEL)),
        'ffn_w_gate': nrm(ks[19], (DEPTH, D_MODEL, D_FF), D_MODEL ** -0.5),
        'ffn_w_up': nrm(ks[20], (DEPTH, D_MODEL, D_FF), D_MODEL ** -0.5),
        'ffn_w_down': nrm(ks[21], (DEPTH, D_FF, D_MODEL), D_FF ** -0.5),
    }


def _fwd_reference(x, ab_norm_g, ab_w_in, sgu_norm_g, sgu_norm_b, sgu_w, sgu_bias, q_norm_g, k_norm_g,
              ab_w_out, cd_norm_g, cd_w_in, conv_c_w, conv_c_b, c_ln_g, c_ln_b, conv_d_w, cd_w_out,
              ffn_norm_g, ffn_w_gate, ffn_w_up, ffn_w_down):
    for layer in range(DEPTH):
        i = layer // 2
        if layer % 2 == 0:
            h = rms_norm(x, ab_norm_g[i])
            x = x + mixer_ab(h, ab_w_in[i], sgu_norm_g[i], sgu_norm_b[i], sgu_w[i], sgu_bias[i],
                             q_norm_g[i], k_norm_g[i], ab_w_out[i])
        else:
            h = rms_norm(x, cd_norm_g[i])
            x = x + mixer_cd(h, cd_w_in[i], conv_c_w[i], conv_c_b[i], c_ln_g[i], c_ln_b[i],
                             conv_d_w[i], cd_w_out[i])
        h = rms_norm(x, ffn_norm_g[layer])
        x = x + swiglu(h, ffn_w_gate[layer], ffn_w_up[layer], ffn_w_down[layer])
    return x


import jax as _jax
import jax.numpy as _jnp

TWIN_FORMAT = 'train_step'
FWD_PARAMS = ['x', 'ab_norm_g', 'ab_w_in', 'sgu_norm_g', 'sgu_norm_b', 'sgu_w', 'sgu_bias', 'q_norm_g', 'k_norm_g', 'ab_w_out', 'cd_norm_g', 'cd_w_in', 'conv_c_w', 'conv_c_b', 'c_ln_g', 'c_ln_b', 'conv_d_w', 'cd_w_out', 'ffn_norm_g', 'ffn_w_gate', 'ffn_w_up', 'ffn_w_down']
TWIN_WEIGHTS = ['ab_norm_g', 'ab_w_in', 'sgu_norm_g', 'sgu_norm_b', 'sgu_w', 'sgu_bias', 'q_norm_g', 'k_norm_g', 'ab_w_out', 'cd_norm_g', 'cd_w_in', 'conv_c_w', 'conv_c_b', 'c_ln_g', 'c_ln_b', 'conv_d_w', 'cd_w_out', 'ffn_norm_g', 'ffn_w_gate', 'ffn_w_up', 'ffn_w_down']
TWIN_DIFF_INPUT = 'x'
TWIN_INPUTS = ['x', 'ab_norm_g', 'ab_w_in', 'sgu_norm_g', 'sgu_norm_b', 'sgu_w', 'sgu_bias', 'q_norm_g', 'k_norm_g', 'ab_w_out', 'cd_norm_g', 'cd_w_in', 'conv_c_w', 'conv_c_b', 'c_ln_g', 'c_ln_b', 'conv_d_w', 'cd_w_out', 'ffn_norm_g', 'ffn_w_gate', 'ffn_w_up', 'ffn_w_down', 'loss_target', 'm_ab_norm_g', 'm_ab_w_in', 'm_sgu_norm_g', 'm_sgu_norm_b', 'm_sgu_w', 'm_sgu_bias', 'm_q_norm_g', 'm_k_norm_g', 'm_ab_w_out', 'm_cd_norm_g', 'm_cd_w_in', 'm_conv_c_w', 'm_conv_c_b', 'm_c_ln_g', 'm_c_ln_b', 'm_conv_d_w', 'm_cd_w_out', 'm_ffn_norm_g', 'm_ffn_w_gate', 'm_ffn_w_up', 'm_ffn_w_down', 'v_ab_norm_g', 'v_ab_w_in', 'v_sgu_norm_g', 'v_sgu_norm_b', 'v_sgu_w', 'v_sgu_bias', 'v_q_norm_g', 'v_k_norm_g', 'v_ab_w_out', 'v_cd_norm_g', 'v_cd_w_in', 'v_conv_c_w', 'v_conv_c_b', 'v_c_ln_g', 'v_c_ln_b', 'v_conv_d_w', 'v_cd_w_out', 'v_ffn_norm_g', 'v_ffn_w_gate', 'v_ffn_w_up', 'v_ffn_w_down']
TWIN_OUTPUTS = ['loss', 'grad_x', 'grad_ab_norm_g', 'grad_ab_w_in', 'grad_sgu_norm_g', 'grad_sgu_norm_b', 'grad_sgu_w', 'grad_sgu_bias', 'grad_q_norm_g', 'grad_k_norm_g', 'grad_ab_w_out', 'grad_cd_norm_g', 'grad_cd_w_in', 'grad_conv_c_w', 'grad_conv_c_b', 'grad_c_ln_g', 'grad_c_ln_b', 'grad_conv_d_w', 'grad_cd_w_out', 'grad_ffn_norm_g', 'grad_ffn_w_gate', 'grad_ffn_w_up', 'grad_ffn_w_down', 'delta_ab_norm_g', 'delta_ab_w_in', 'delta_sgu_norm_g', 'delta_sgu_norm_b', 'delta_sgu_w', 'delta_sgu_bias', 'delta_q_norm_g', 'delta_k_norm_g', 'delta_ab_w_out', 'delta_cd_norm_g', 'delta_cd_w_in', 'delta_conv_c_w', 'delta_conv_c_b', 'delta_c_ln_g', 'delta_c_ln_b', 'delta_conv_d_w', 'delta_cd_w_out', 'delta_ffn_norm_g', 'delta_ffn_w_gate', 'delta_ffn_w_up', 'delta_ffn_w_down', 'new_m_ab_norm_g', 'new_m_ab_w_in', 'new_m_sgu_norm_g', 'new_m_sgu_norm_b', 'new_m_sgu_w', 'new_m_sgu_bias', 'new_m_q_norm_g', 'new_m_k_norm_g', 'new_m_ab_w_out', 'new_m_cd_norm_g', 'new_m_cd_w_in', 'new_m_conv_c_w', 'new_m_conv_c_b', 'new_m_c_ln_g', 'new_m_c_ln_b', 'new_m_conv_d_w', 'new_m_cd_w_out', 'new_m_ffn_norm_g', 'new_m_ffn_w_gate', 'new_m_ffn_w_up', 'new_m_ffn_w_down', 'new_v_ab_norm_g', 'new_v_ab_w_in', 'new_v_sgu_norm_g', 'new_v_sgu_norm_b', 'new_v_sgu_w', 'new_v_sgu_bias', 'new_v_q_norm_g', 'new_v_k_norm_g', 'new_v_ab_w_out', 'new_v_cd_norm_g', 'new_v_cd_w_in', 'new_v_conv_c_w', 'new_v_conv_c_b', 'new_v_c_ln_g', 'new_v_c_ln_b', 'new_v_conv_d_w', 'new_v_cd_w_out', 'new_v_ffn_norm_g', 'new_v_ffn_w_gate', 'new_v_ffn_w_up', 'new_v_ffn_w_down']
TWIN_LEAF_KINDS = {'loss': 'loss', 'grad_x': 'grad_x', 'grad_ab_norm_g': 'grad_w', 'grad_ab_w_in': 'grad_w', 'grad_sgu_norm_g': 'grad_w', 'grad_sgu_norm_b': 'grad_w', 'grad_sgu_w': 'grad_w', 'grad_sgu_bias': 'grad_w', 'grad_q_norm_g': 'grad_w', 'grad_k_norm_g': 'grad_w', 'grad_ab_w_out': 'grad_w', 'grad_cd_norm_g': 'grad_w', 'grad_cd_w_in': 'grad_w', 'grad_conv_c_w': 'grad_w', 'grad_conv_c_b': 'grad_w', 'grad_c_ln_g': 'grad_w', 'grad_c_ln_b': 'grad_w', 'grad_conv_d_w': 'grad_w', 'grad_cd_w_out': 'grad_w', 'grad_ffn_norm_g': 'grad_w', 'grad_ffn_w_gate': 'grad_w', 'grad_ffn_w_up': 'grad_w', 'grad_ffn_w_down': 'grad_w', 'delta_ab_norm_g': 'delta_w', 'delta_ab_w_in': 'delta_w', 'delta_sgu_norm_g': 'delta_w', 'delta_sgu_norm_b': 'delta_w', 'delta_sgu_w': 'delta_w', 'delta_sgu_bias': 'delta_w', 'delta_q_norm_g': 'delta_w', 'delta_k_norm_g': 'delta_w', 'delta_ab_w_out': 'delta_w', 'delta_cd_norm_g': 'delta_w', 'delta_cd_w_in': 'delta_w', 'delta_conv_c_w': 'delta_w', 'delta_conv_c_b': 'delta_w', 'delta_c_ln_g': 'delta_w', 'delta_c_ln_b': 'delta_w', 'delta_conv_d_w': 'delta_w', 'delta_cd_w_out': 'delta_w', 'delta_ffn_norm_g': 'delta_w', 'delta_ffn_w_gate': 'delta_w', 'delta_ffn_w_up': 'delta_w', 'delta_ffn_w_down': 'delta_w', 'new_m_ab_norm_g': 'new_m', 'new_m_ab_w_in': 'new_m', 'new_m_sgu_norm_g': 'new_m', 'new_m_sgu_norm_b': 'new_m', 'new_m_sgu_w': 'new_m', 'new_m_sgu_bias': 'new_m', 'new_m_q_norm_g': 'new_m', 'new_m_k_norm_g': 'new_m', 'new_m_ab_w_out': 'new_m', 'new_m_cd_norm_g': 'new_m', 'new_m_cd_w_in': 'new_m', 'new_m_conv_c_w': 'new_m', 'new_m_conv_c_b': 'new_m', 'new_m_c_ln_g': 'new_m', 'new_m_c_ln_b': 'new_m', 'new_m_conv_d_w': 'new_m', 'new_m_cd_w_out': 'new_m', 'new_m_ffn_norm_g': 'new_m', 'new_m_ffn_w_gate': 'new_m', 'new_m_ffn_w_up': 'new_m', 'new_m_ffn_w_down': 'new_m', 'new_v_ab_norm_g': 'new_v', 'new_v_ab_w_in': 'new_v', 'new_v_sgu_norm_g': 'new_v', 'new_v_sgu_norm_b': 'new_v', 'new_v_sgu_w': 'new_v', 'new_v_sgu_bias': 'new_v', 'new_v_q_norm_g': 'new_v', 'new_v_k_norm_g': 'new_v', 'new_v_ab_w_out': 'new_v', 'new_v_cd_norm_g': 'new_v', 'new_v_cd_w_in': 'new_v', 'new_v_conv_c_w': 'new_v', 'new_v_conv_c_b': 'new_v', 'new_v_c_ln_g': 'new_v', 'new_v_c_ln_b': 'new_v', 'new_v_conv_d_w': 'new_v', 'new_v_cd_w_out': 'new_v', 'new_v_ffn_norm_g': 'new_v', 'new_v_ffn_w_gate': 'new_v', 'new_v_ffn_w_up': 'new_v', 'new_v_ffn_w_down': 'new_v'}


def _forward(args):
    return _fwd_reference(*[args[k] for k in FWD_PARAMS])


def _output_shape():
    out = _jax.eval_shape(lambda: _forward(_fwd_setup_inputs(0)))
    return out.shape, out.dtype

N_MICROBATCH = 1
ADAM_LR = 0.001
ADAM_B1 = 0.9
ADAM_B2 = 0.999
ADAM_EPS = 1e-08
ADAM_WD = 0.01
ADAM_STEP = 10
PER_EXAMPLE_BATCH_AXIS = {'x': 0, 'loss_target': 0}
SHARED_INPUTS = []
_WEIGHT_DTYPES = {'ab_norm_g': _jnp.float32, 'ab_w_in': _jnp.float32, 'sgu_norm_g': _jnp.float32, 'sgu_norm_b': _jnp.float32, 'sgu_w': _jnp.float32, 'sgu_bias': _jnp.float32, 'q_norm_g': _jnp.float32, 'k_norm_g': _jnp.float32, 'ab_w_out': _jnp.float32, 'cd_norm_g': _jnp.float32, 'cd_w_in': _jnp.float32, 'conv_c_w': _jnp.float32, 'conv_c_b': _jnp.float32, 'c_ln_g': _jnp.float32, 'c_ln_b': _jnp.float32, 'conv_d_w': _jnp.float32, 'cd_w_out': _jnp.float32, 'ffn_norm_g': _jnp.float32, 'ffn_w_gate': _jnp.float32, 'ffn_w_up': _jnp.float32, 'ffn_w_down': _jnp.float32}
MOMENT_SCALE = {'ab_norm_g': 7.963523e+00, 'ab_w_in': 2.961015e-01, 'sgu_norm_g': 1.424647e+00, 'sgu_norm_b': 5.949403e-01, 'sgu_w': 5.742847e-01, 'sgu_bias': 1.290752e+01, 'q_norm_g': 5.557424e-01, 'k_norm_g': 5.527109e-01, 'ab_w_out': 2.011038e+00, 'cd_norm_g': 4.760943e+01, 'cd_w_in': 1.039927e+00, 'conv_c_w': 7.624547e-01, 'conv_c_b': 8.577930e+00, 'c_ln_g': 1.494959e+01, 'c_ln_b': 1.031832e+01, 'conv_d_w': 1.859644e+01, 'cd_w_out': 1.370671e+00, 'ffn_norm_g': 2.473348e+01, 'ffn_w_gate': 4.691302e-01, 'ffn_w_up': 3.389844e-01, 'ffn_w_down': 5.412167e-01}


def _to_microbatches(a, axis):
    t = _jnp.moveaxis(a, axis, 0)
    t = t.reshape((N_MICROBATCH, t.shape[0] // N_MICROBATCH) + t.shape[1:])
    return _jnp.moveaxis(t, 1, axis + 1)


def setup_inputs(seed: int = 0) -> dict:
    inp = _fwd_setup_inputs(seed)
    key = _jax.random.fold_in(_jax.random.key(seed), 7919)
    shape, _ = _output_shape()
    out = dict(inp)
    out["loss_target"] = _jax.random.normal(_jax.random.fold_in(key, 0), shape, _jnp.float32)
    for i, name in enumerate(TWIN_WEIGHTS):
        w = inp[name].astype(_jnp.float32)
        if MOMENT_SCALE is None:
            s = _jnp.sqrt(_jnp.mean(_jnp.square(w)) + 1e-30)
        else:
            s = MOMENT_SCALE[name]
        km, kv = _jax.random.split(_jax.random.fold_in(key, i + 1))
        out[name] = w
        out["m_" + name] = s * _jax.random.normal(km, w.shape, _jnp.float32)
        out["v_" + name] = (s * s) * _jax.random.uniform(kv, w.shape, _jnp.float32, 0.5, 1.5)
    if N_MICROBATCH > 1:
        for name, axis in PER_EXAMPLE_BATCH_AXIS.items():
            out[name] = _to_microbatches(out[name], axis)
    return {'x': out['x'], 'ab_norm_g': out['ab_norm_g'], 'ab_w_in': out['ab_w_in'], 'sgu_norm_g': out['sgu_norm_g'], 'sgu_norm_b': out['sgu_norm_b'], 'sgu_w': out['sgu_w'], 'sgu_bias': out['sgu_bias'], 'q_norm_g': out['q_norm_g'], 'k_norm_g': out['k_norm_g'], 'ab_w_out': out['ab_w_out'], 'cd_norm_g': out['cd_norm_g'], 'cd_w_in': out['cd_w_in'], 'conv_c_w': out['conv_c_w'], 'conv_c_b': out['conv_c_b'], 'c_ln_g': out['c_ln_g'], 'c_ln_b': out['c_ln_b'], 'conv_d_w': out['conv_d_w'], 'cd_w_out': out['cd_w_out'], 'ffn_norm_g': out['ffn_norm_g'], 'ffn_w_gate': out['ffn_w_gate'], 'ffn_w_up': out['ffn_w_up'], 'ffn_w_down': out['ffn_w_down'], 'loss_target': out['loss_target'], 'm_ab_norm_g': out['m_ab_norm_g'], 'm_ab_w_in': out['m_ab_w_in'], 'm_sgu_norm_g': out['m_sgu_norm_g'], 'm_sgu_norm_b': out['m_sgu_norm_b'], 'm_sgu_w': out['m_sgu_w'], 'm_sgu_bias': out['m_sgu_bias'], 'm_q_norm_g': out['m_q_norm_g'], 'm_k_norm_g': out['m_k_norm_g'], 'm_ab_w_out': out['m_ab_w_out'], 'm_cd_norm_g': out['m_cd_norm_g'], 'm_cd_w_in': out['m_cd_w_in'], 'm_conv_c_w': out['m_conv_c_w'], 'm_conv_c_b': out['m_conv_c_b'], 'm_c_ln_g': out['m_c_ln_g'], 'm_c_ln_b': out['m_c_ln_b'], 'm_conv_d_w': out['m_conv_d_w'], 'm_cd_w_out': out['m_cd_w_out'], 'm_ffn_norm_g': out['m_ffn_norm_g'], 'm_ffn_w_gate': out['m_ffn_w_gate'], 'm_ffn_w_up': out['m_ffn_w_up'], 'm_ffn_w_down': out['m_ffn_w_down'], 'v_ab_norm_g': out['v_ab_norm_g'], 'v_ab_w_in': out['v_ab_w_in'], 'v_sgu_norm_g': out['v_sgu_norm_g'], 'v_sgu_norm_b': out['v_sgu_norm_b'], 'v_sgu_w': out['v_sgu_w'], 'v_sgu_bias': out['v_sgu_bias'], 'v_q_norm_g': out['v_q_norm_g'], 'v_k_norm_g': out['v_k_norm_g'], 'v_ab_w_out': out['v_ab_w_out'], 'v_cd_norm_g': out['v_cd_norm_g'], 'v_cd_w_in': out['v_cd_w_in'], 'v_conv_c_w': out['v_conv_c_w'], 'v_conv_c_b': out['v_conv_c_b'], 'v_c_ln_g': out['v_c_ln_g'], 'v_c_ln_b': out['v_c_ln_b'], 'v_conv_d_w': out['v_conv_d_w'], 'v_cd_w_out': out['v_cd_w_out'], 'v_ffn_norm_g': out['v_ffn_norm_g'], 'v_ffn_w_gate': out['v_ffn_w_gate'], 'v_ffn_w_up': out['v_ffn_w_up'], 'v_ffn_w_down': out['v_ffn_w_down']}


def _loss(weights, diff, rest, loss_target):
    with _jax.named_scope("forward"):
        args = {**rest, TWIN_DIFF_INPUT: diff, **{k: w.astype(_WEIGHT_DTYPES[k]) for k, w in weights.items()}}
        y = _forward(args)
    with _jax.named_scope("loss_head"):
        err = _jnp.square(y.astype(_jnp.float32) - loss_target)
        return 0.5 * _jnp.sum(_jnp.mean(err, axis=-1)) if err.ndim else 0.5 * err


def _adamw(w, g, m, v):
    m = ADAM_B1 * m + (1.0 - ADAM_B1) * g
    v = ADAM_B2 * v + (1.0 - ADAM_B2) * _jnp.square(g)
    m_hat = m / (1.0 - ADAM_B1 ** ADAM_STEP)
    v_hat = v / (1.0 - ADAM_B2 ** ADAM_STEP)
    delta = -ADAM_LR * (m_hat / (_jnp.sqrt(v_hat) + ADAM_EPS) + ADAM_WD * w)
    return delta, m, v


def reference(x, ab_norm_g, ab_w_in, sgu_norm_g, sgu_norm_b, sgu_w, sgu_bias, q_norm_g, k_norm_g, ab_w_out, cd_norm_g, cd_w_in, conv_c_w, conv_c_b, c_ln_g, c_ln_b, conv_d_w, cd_w_out, ffn_norm_g, ffn_w_gate, ffn_w_up, ffn_w_down, loss_target, m_ab_norm_g, m_ab_w_in, m_sgu_norm_g, m_sgu_norm_b, m_sgu_w, m_sgu_bias, m_q_norm_g, m_k_norm_g, m_ab_w_out, m_cd_norm_g, m_cd_w_in, m_conv_c_w, m_conv_c_b, m_c_ln_g, m_c_ln_b, m_conv_d_w, m_cd_w_out, m_ffn_norm_g, m_ffn_w_gate, m_ffn_w_up, m_ffn_w_down, v_ab_norm_g, v_ab_w_in, v_sgu_norm_g, v_sgu_norm_b, v_sgu_w, v_sgu_bias, v_q_norm_g, v_k_norm_g, v_ab_w_out, v_cd_norm_g, v_cd_w_in, v_conv_c_w, v_conv_c_b, v_c_ln_g, v_c_ln_b, v_conv_d_w, v_cd_w_out, v_ffn_norm_g, v_ffn_w_gate, v_ffn_w_up, v_ffn_w_down):
    given = dict(x=x, ab_norm_g=ab_norm_g, ab_w_in=ab_w_in, sgu_norm_g=sgu_norm_g, sgu_norm_b=sgu_norm_b, sgu_w=sgu_w, sgu_bias=sgu_bias, q_norm_g=q_norm_g, k_norm_g=k_norm_g, ab_w_out=ab_w_out, cd_norm_g=cd_norm_g, cd_w_in=cd_w_in, conv_c_w=conv_c_w, conv_c_b=conv_c_b, c_ln_g=c_ln_g, c_ln_b=c_ln_b, conv_d_w=conv_d_w, cd_w_out=cd_w_out, ffn_norm_g=ffn_norm_g, ffn_w_gate=ffn_w_gate, ffn_w_up=ffn_w_up, ffn_w_down=ffn_w_down, loss_target=loss_target, m_ab_norm_g=m_ab_norm_g, m_ab_w_in=m_ab_w_in, m_sgu_norm_g=m_sgu_norm_g, m_sgu_norm_b=m_sgu_norm_b, m_sgu_w=m_sgu_w, m_sgu_bias=m_sgu_bias, m_q_norm_g=m_q_norm_g, m_k_norm_g=m_k_norm_g, m_ab_w_out=m_ab_w_out, m_cd_norm_g=m_cd_norm_g, m_cd_w_in=m_cd_w_in, m_conv_c_w=m_conv_c_w, m_conv_c_b=m_conv_c_b, m_c_ln_g=m_c_ln_g, m_c_ln_b=m_c_ln_b, m_conv_d_w=m_conv_d_w, m_cd_w_out=m_cd_w_out, m_ffn_norm_g=m_ffn_norm_g, m_ffn_w_gate=m_ffn_w_gate, m_ffn_w_up=m_ffn_w_up, m_ffn_w_down=m_ffn_w_down, v_ab_norm_g=v_ab_norm_g, v_ab_w_in=v_ab_w_in, v_sgu_norm_g=v_sgu_norm_g, v_sgu_norm_b=v_sgu_norm_b, v_sgu_w=v_sgu_w, v_sgu_bias=v_sgu_bias, v_q_norm_g=v_q_norm_g, v_k_norm_g=v_k_norm_g, v_ab_w_out=v_ab_w_out, v_cd_norm_g=v_cd_norm_g, v_cd_w_in=v_cd_w_in, v_conv_c_w=v_conv_c_w, v_conv_c_b=v_conv_c_b, v_c_ln_g=v_c_ln_g, v_c_ln_b=v_c_ln_b, v_conv_d_w=v_conv_d_w, v_cd_w_out=v_cd_w_out, v_ffn_norm_g=v_ffn_norm_g, v_ffn_w_gate=v_ffn_w_gate, v_ffn_w_up=v_ffn_w_up, v_ffn_w_down=v_ffn_w_down)
    weights = {n: given[n] for n in TWIN_WEIGHTS}
    shared = {n: given[n] for n in SHARED_INPUTS}
    per_example = {n: given[n] for n in ['x']}
    grad_fn = _jax.value_and_grad(_loss, argnums=(0, 1))

    def one_microbatch(ex, loss_target):
        ex = dict(ex)
        diff = ex.pop(TWIN_DIFF_INPUT)
        return grad_fn(weights, diff, {**shared, **ex}, loss_target)

    if N_MICROBATCH == 1:
        loss, (grad_w, grad_x) = one_microbatch(per_example, given["loss_target"])
    else:
        def body(carry, xs):
            loss_sum, grad_sum = carry
            l_k, (gw_k, gx_k) = one_microbatch(xs[0], xs[1])
            with _jax.named_scope("update"):
                return (loss_sum + l_k, _jax.tree.map(_jnp.add, grad_sum, gw_k)), gx_k

        init = (_jnp.zeros((), _jnp.float32), _jax.tree.map(_jnp.zeros_like, weights))
        (loss, grad_w), grad_x = _jax.lax.scan(body, init, (per_example, given["loss_target"]))
    with _jax.named_scope("update"):
        delta_w, new_m, new_v = {}, {}, {}
        for n in TWIN_WEIGHTS:
            delta_w[n], new_m[n], new_v[n] = _adamw(weights[n], grad_w[n], given["m_" + n], given["v_" + n])
    return (loss, grad_x, *[grad_w[n] for n in TWIN_WEIGHTS], *[delta_w[n] for n in TWIN_WEIGHTS],
            *[new_m[n] for n in TWIN_WEIGHTS], *[new_v[n] for n in TWIN_WEIGHTS])
```

```python
import functools
import math

import jax
import jax.numpy as jnp
from jax import lax
from jax.experimental import pallas as pl
from jax.experimental.pallas import tpu as pltpu

F32 = jnp.float32
BF16 = jnp.bfloat16
SDS = jax.ShapeDtypeStruct

N_CHIPS = 4
EPS = 1e-6
NEG_INF = -1e30
CHUNK = 128
A_GROUPS = 4
A_WIDTH = 512
N_DIL = 3
DIL_RATES = (1, 4, 16)
HEAD_DIM = 64
B_WIDTH = 512
ROPE_DIM = 16
ROPE_THETA = 500000.0
C_WIDTH = 512
C_KERNEL = 31
D_KERNEL = 3
HALO = 32
ATT_BLOCK = 128
LANES = 128

ADAM_LR = 0.001
ADAM_B1 = 0.9
ADAM_B2 = 0.999
ADAM_EPS = 1e-08
ADAM_WD = 0.01
ADAM_STEP = 10

VMEM_LIMIT = 56 * 1024 * 1024

NN = (((1,), (0,)), ((), ()))
NT = (((1,), (1,)), ((), ()))
TN = (((0,), (0,)), ((), ()))


def _params(sem=None):
    return pltpu.CompilerParams(dimension_semantics=sem, vmem_limit_bytes=VMEM_LIMIT)


def _bf(v):
    return v if v.dtype == BF16 else v.astype(BF16)


def _dot(a, b, dims):
    return lax.dot_general(_bf(a), _bf(b), dims, preferred_element_type=F32)


def _dot_hi(a, b):
    return jnp.dot(a, b, precision=lax.Precision.HIGHEST, preferred_element_type=F32)


def _sigmoid(v):
    return 1.0 / (1.0 + jnp.exp(-v))


def _gelu(v):
    return 0.5 * v * (1.0 + lax.erf(v * (1.0 / math.sqrt(2.0))))


def _gelu_grad(v):
    cdf = 0.5 * (1.0 + lax.erf(v * (1.0 / math.sqrt(2.0))))
    return cdf + v * jnp.exp(-0.5 * v * v) * (1.0 / math.sqrt(2.0 * math.pi))


def _segment_mean_matrix(seg):
    r = lax.broadcasted_iota(jnp.int32, (LANES, LANES), 0) // seg
    c = lax.broadcasted_iota(jnp.int32, (LANES, LANES), 1) // seg
    return jnp.where(r == c, 1.0 / seg, 0.0).astype(F32)


def _matmul(name, grid, pairs, extras, outs, dims, epi, *, k_axis=None, acc_shapes=(), sem=None):
    n_pairs, n_ex, n_out, n_acc = len(pairs), len(extras), len(outs), len(acc_shapes)
    pair_acc = [p[4] for p in pairs]
    nk = grid[k_axis] if k_axis is not None else 1

    def body(*refs):
        ab = refs[:2 * n_pairs]
        ex = refs[2 * n_pairs:2 * n_pairs + n_ex]
        out_refs = refs[2 * n_pairs + n_ex:2 * n_pairs + n_ex + n_out]
        acc_refs = refs[2 * n_pairs + n_ex + n_out:]
        pids = tuple(pl.program_id(a) for a in range(len(grid)))
        parts = [None] * n_acc
        for j in range(n_pairs):
            d = _dot(ab[2 * j][...], ab[2 * j + 1][...], dims)
            parts[pair_acc[j]] = d if parts[pair_acc[j]] is None else parts[pair_acc[j]] + d
        if k_axis is None:
            epi(parts, ex, out_refs, pids)
        else:
            k = pids[k_axis]

            @pl.when(k == 0)
            def _():
                for a in range(n_acc):
                    acc_refs[a][...] = parts[a]

            @pl.when(k > 0)
            def _():
                for a in range(n_acc):
                    acc_refs[a][...] += parts[a]

            @pl.when(k == nk - 1)
            def _():
                epi([r[...] for r in acc_refs], ex, out_refs, pids)

    operands, in_specs = [], []
    for a, a_spec, b, b_spec, _ in pairs:
        operands += [a, b]
        in_specs += [a_spec, b_spec]
    for e, e_spec in extras:
        operands.append(e)
        in_specs.append(e_spec)
    scratch = [pltpu.VMEM(s, F32) for s in acc_shapes] if k_axis is not None else []
    res = pl.pallas_call(
        body, grid=grid, in_specs=in_specs, out_specs=[o[1] for o in outs],
        out_shape=[o[0] for o in outs], scratch_shapes=scratch, name=name,
        compiler_params=_params(sem))(*operands)
    return res


def _rms_rows(v, g):
    r = lax.rsqrt(jnp.mean(v * v, axis=-1, keepdims=True) + EPS)
    return v * r * g


def _rms_fwd(name, x, g):
    t, d = x.shape
    tm = 512

    def body(x_ref, g_ref, o_ref):
        o_ref[...] = _rms_rows(x_ref[...], g_ref[...]).astype(BF16)

    return pl.pallas_call(
        body, grid=(t // tm,),
        in_specs=[pl.BlockSpec((tm, d), lambda i: (i, 0)), pl.BlockSpec((1, d), lambda i: (0, 0))],
        out_specs=pl.BlockSpec((tm, d), lambda i: (i, 0)), out_shape=SDS((t, d), BF16), name=name,
        compiler_params=_params())(x, g)


def _epi_residual_norm(accs, ex, outs, pids):
    x_new = accs[0] + ex[0][...]
    outs[0][...] = x_new
    outs[1][...] = _rms_rows(x_new, ex[1][...]).astype(BF16)


def _epi_residual_loss(accs, ex, outs, pids):
    y = accs[0] + ex[0][...]
    err = y - ex[1][...]
    outs[0][...] = err * (1.0 / err.shape[-1])

    @pl.when(pids[0] == 0)
    def _():
        outs[1][...] = jnp.zeros_like(outs[1])

    outs[1][...] += jnp.sum(err * err)


def _epi_rms_bwd(accs, ex, outs, pids):
    dh = accs[0]
    xv, g, res = ex[0][...], ex[1][...], ex[2][...]
    r = lax.rsqrt(jnp.mean(xv * xv, axis=-1, keepdims=True) + EPS)
    xh = xv * r
    dy = dh * g
    outs[0][...] = res + r * (dy - xh * jnp.mean(dy * xh, axis=-1, keepdims=True))

    @pl.when(pids[0] == 0)
    def _():
        outs[1][...] = jnp.zeros_like(outs[1])

    outs[1][...] += jnp.sum(dh * xh, axis=0, keepdims=True)


def _row_spec(tm, d):
    return pl.BlockSpec((tm, d), lambda i, *_: (i, 0))


def _const_spec(shape):
    nd = len(shape)
    return pl.BlockSpec(shape, lambda *_: (0,) * nd)


def _proj_in(name, h, w, layer):
    t, d = h.shape
    n4 = w.shape[-1]
    tm = 512

    def epi(accs, ex, outs, pids):
        outs[0][...] = accs[0].astype(BF16)

    return _matmul(
        name, (N_CHIPS, t // tm),
        [(h, pl.BlockSpec((tm, d), lambda p, i: (i, 0)),
          w, pl.BlockSpec((None, None, d, n4), lambda p, i: (p, layer, 0, 0)), 0)],
        [], [(SDS((t, N_CHIPS * n4), BF16), pl.BlockSpec((tm, n4), lambda p, i: (i, p)))],
        NN, epi, acc_shapes=((tm, n4),))[0]


def _proj_out(name, a, w, x, g_next=None, target=None):
    t, k = a.shape
    d = w.shape[-1]
    tm = 512
    if target is None:
        extras = [(x, _row_spec(tm, d)), (g_next, _const_spec((1, d)))]
        outs = [(SDS((t, d), F32), _row_spec(tm, d)), (SDS((t, d), BF16), _row_spec(tm, d))]
        epi = _epi_residual_norm
    else:
        extras = [(x, _row_spec(tm, d)), (target, _row_spec(tm, d))]
        outs = [(SDS((t, d), F32), _row_spec(tm, d)), (SDS((8, LANES), F32), _const_spec((8, LANES)))]
        epi = _epi_residual_loss
    return _matmul(name, (t // tm,), [(a, _row_spec(tm, k), w, _const_spec((k, d)), 0)], extras, outs, NN, epi,
                   acc_shapes=((tm, d),))


def _ffn_in(name, h, wg, wu, layer):
    t, d = h.shape
    n4 = wg.shape[-1]
    tm = 512

    def epi(accs, ex, outs, pids):
        gate, up = accs
        outs[0][...] = gate.astype(BF16)
        outs[1][...] = up.astype(BF16)
        outs[2][...] = (gate * _sigmoid(gate) * up).astype(BF16)

    w_spec = pl.BlockSpec((None, None, d, n4), lambda p, i: (p, layer, 0, 0))
    h_spec = pl.BlockSpec((tm, d), lambda p, i: (i, 0))
    o = (SDS((N_CHIPS, t, n4), BF16), pl.BlockSpec((None, tm, n4), lambda p, i: (p, i, 0)))
    return _matmul(name, (N_CHIPS, t // tm), [(h, h_spec, wg, w_spec, 0), (h, h_spec, wu, w_spec, 1)], [],
                   [o, o, o], NN, epi, acc_shapes=((tm, n4), (tm, n4)))


def _ffn_out(name, act, wd, layer, x, g_next=None, target=None):
    _, t, n4 = act.shape
    d = wd.shape[-1]
    tm = 1024
    xs = pl.BlockSpec((tm, d), lambda i, k: (i, 0))
    if target is None:
        extras = [(x, xs), (g_next, _const_spec((1, d)))]
        outs = [(SDS((t, d), F32), xs), (SDS((t, d), BF16), xs)]
        epi = _epi_residual_norm
    else:
        extras = [(x, xs), (target, xs)]
        outs = [(SDS((t, d), F32), xs), (SDS((8, LANES), F32), _const_spec((8, LANES)))]
        epi = _epi_residual_loss
    return _matmul(
        name, (t // tm, N_CHIPS),
        [(act, pl.BlockSpec((None, tm, n4), lambda i, k: (k, i, 0)),
          wd, pl.BlockSpec((None, None, n4, d), lambda i, k: (k, layer, 0, 0)), 0)],
        extras, outs, NN, epi, k_axis=1, acc_shapes=((tm, d),))


def _ffn_dact(name, g, wd, layer, gate, up):
    t, d = g.shape
    n4 = wd.shape[-2]
    tm = 512

    def epi(accs, ex, outs, pids):
        dact = accs[0]
        gt = ex[0][...].astype(F32)
        upv = ex[1][...].astype(F32)
        s = _sigmoid(gt)
        outs[0][...] = (dact * upv * (s * (1.0 + gt * (1.0 - s)))).astype(BF16)
        outs[1][...] = (dact * gt * s).astype(BF16)

    blk = pl.BlockSpec((None, tm, n4), lambda p, i: (p, i, 0))
    o = (SDS((N_CHIPS, t, n4), BF16), blk)
    return _matmul(
        name, (N_CHIPS, t // tm),
        [(g, pl.BlockSpec((tm, d), lambda p, i: (i, 0)),
          wd, pl.BlockSpec((None, None, n4, d), lambda p, i: (p, layer, 0, 0)), 0)],
        [(gate, blk), (up, blk)], [o, o], NT, epi, acc_shapes=((tm, n4),))


def _copy_epi(accs, ex, outs, pids):
    for a, o in zip(accs, outs):
        o[...] = a.astype(o.dtype)


def _dgrad_cols(name, dz_list, w_list, layer, three_d, x, g, res):
    t, d = x.shape
    n4 = w_list[0].shape[-1]
    tm = 512
    if three_d:
        zs = pl.BlockSpec((None, tm, n4), lambda i, k: (k, i, 0))
    else:
        zs = pl.BlockSpec((tm, n4), lambda i, k: (i, k))
    ws = pl.BlockSpec((None, None, d, n4), lambda i, k: (k, layer, 0, 0))
    xs = pl.BlockSpec((tm, d), lambda i, k: (i, 0))
    return _matmul(
        name, (t // tm, N_CHIPS), [(dz, zs, w, ws, 0) for dz, w in zip(dz_list, w_list)],
        [(x, xs), (g, _const_spec((1, d))), (res, xs)],
        [(SDS((t, d), F32), xs), (SDS((1, d), F32), _const_spec((1, d)))],
        NT, _epi_rms_bwd, k_axis=1, acc_shapes=((tm, d),))


def _dgrad_rows(name, g, w):
    t, d = g.shape
    k = w.shape[0]
    tm = 512
    return _matmul(name, (t // tm,), [(g, _row_spec(tm, d), w, _const_spec((k, d)), 0)], [],
                   [(SDS((t, k), F32), _row_spec(tm, k))], NT, _copy_epi, acc_shapes=((tm, k),))[0]


A_TILE = 256


def _a_common(p_ref, lg_ref, lb_ref):
    pv = p_ref[...].astype(F32)
    a = _gelu(pv)
    u, v = a[:, :A_WIDTH], a[:, A_WIDTH:]
    vc = v - jnp.mean(v, axis=-1, keepdims=True)
    rs = lax.rsqrt(jnp.mean(vc * vc, axis=-1, keepdims=True) + EPS)
    vhat = vc * rs
    vn = vhat * lg_ref[...] + lb_ref[...]
    return pv, u, vhat, rs, vn.astype(BF16)


def _tril_weights(w_ref, g):
    r = lax.broadcasted_iota(jnp.int32, (CHUNK, CHUNK), 0)
    c = lax.broadcasted_iota(jnp.int32, (CHUNK, CHUNK), 1)
    return jnp.where(c <= r, w_ref[g], 0.0).astype(BF16), c <= r


def _mixer_a_fwd(proj, lg, lb, w, bias_t):
    t = proj.shape[0]

    def body(p_ref, lg_ref, lb_ref, w_ref, bt_ref, o_ref):
        _, u, _, _, vnb = _a_common(p_ref, lg_ref, lb_ref)
        for g in range(A_GROUPS):
            wt, _ = _tril_weights(w_ref, g)
            cs = slice(g * CHUNK, (g + 1) * CHUNK)
            for ch in range(A_TILE // CHUNK):
                rs_ = slice(ch * CHUNK, (ch + 1) * CHUNK)
                mixed = _dot(wt, vnb[rs_, cs], NN) + bt_ref[:, g:g + 1]
                o_ref[rs_, cs] = (u[rs_, cs] * mixed).astype(BF16)

    return pl.pallas_call(
        body, grid=(t // A_TILE,),
        in_specs=[pl.BlockSpec((A_TILE, 2 * A_WIDTH), lambda i: (i, 0)), _const_spec((1, A_WIDTH)),
                  _const_spec((1, A_WIDTH)), _const_spec((A_GROUPS, CHUNK, CHUNK)), _const_spec((CHUNK, A_GROUPS))],
        out_specs=pl.BlockSpec((A_TILE, A_WIDTH), lambda i: (i, 0)), out_shape=SDS((t, A_WIDTH), BF16),
        name="mixer_a_fwd", compiler_params=_params())(proj, lg, lb, w, bias_t)


def _mixer_a_bwd(proj, dcat, lg, lb, w, bias_t):
    t = proj.shape[0]

    def body(p_ref, da_ref, lg_ref, lb_ref, w_ref, bt_ref, dp_ref, dw_ref, dbt_ref, dlg_ref, dlb_ref, du_scr, dvn_scr):
        @pl.when(pl.program_id(0) == 0)
        def _():
            dw_ref[...] = jnp.zeros_like(dw_ref)
            dbt_ref[...] = jnp.zeros_like(dbt_ref)
            dlg_ref[...] = jnp.zeros_like(dlg_ref)
            dlb_ref[...] = jnp.zeros_like(dlb_ref)

        pv, u, vhat, rs, vnb = _a_common(p_ref, lg_ref, lb_ref)
        da = da_ref[...]
        for g in range(A_GROUPS):
            wt, keep = _tril_weights(w_ref, g)
            cs = slice(g * CHUNK, (g + 1) * CHUNK)
            for ch in range(A_TILE // CHUNK):
                rs_ = slice(ch * CHUNK, (ch + 1) * CHUNK)
                vg = vnb[rs_, cs]
                mixed = _dot(wt, vg, NN) + bt_ref[:, g:g + 1]
                du_scr[rs_, cs] = da[rs_, cs] * mixed
                dmx = da[rs_, cs] * u[rs_, cs]
                dw_ref[g] += jnp.where(keep, _dot(dmx, vg, NT), 0.0)
                dvn_scr[rs_, cs] = _dot(wt, dmx, TN)
                dbt_ref[:, g:g + 1] += jnp.sum(dmx, axis=1, keepdims=True)
        dvn = dvn_scr[...]
        dlg_ref[...] += jnp.sum(dvn * vhat, axis=0, keepdims=True)
        dlb_ref[...] += jnp.sum(dvn, axis=0, keepdims=True)
        dvh = dvn * lg_ref[...]
        dv = rs * (dvh - jnp.mean(dvh, axis=-1, keepdims=True) - vhat * jnp.mean(dvh * vhat, axis=-1, keepdims=True))
        gp = _gelu_grad(pv)
        dp_ref[:, :A_WIDTH] = (du_scr[...] * gp[:, :A_WIDTH]).astype(BF16)
        dp_ref[:, A_WIDTH:] = (dv * gp[:, A_WIDTH:]).astype(BF16)

    return pl.pallas_call(
        body, grid=(t // A_TILE,),
        in_specs=[pl.BlockSpec((A_TILE, 2 * A_WIDTH), lambda i: (i, 0)), pl.BlockSpec((A_TILE, A_WIDTH), lambda i: (i, 0)),
                  _const_spec((1, A_WIDTH)), _const_spec((1, A_WIDTH)), _const_spec((A_GROUPS, CHUNK, CHUNK)),
                  _const_spec((CHUNK, A_GROUPS))],
        out_specs=[pl.BlockSpec((A_TILE, 2 * A_WIDTH), lambda i: (i, 0)), _const_spec((A_GROUPS, CHUNK, CHUNK)),
                   _const_spec((CHUNK, A_GROUPS)), _const_spec((1, A_WIDTH)), _const_spec((1, A_WIDTH))],
        out_shape=[SDS((t, 2 * A_WIDTH), BF16), SDS((A_GROUPS, CHUNK, CHUNK), F32), SDS((CHUNK, A_GROUPS), F32),
                   SDS((1, A_WIDTH), F32), SDS((1, A_WIDTH), F32)],
        scratch_shapes=[pltpu.VMEM((A_TILE, A_WIDTH), F32), pltpu.VMEM((A_TILE, A_WIDTH), F32)],
        name="mixer_a_bwd", compiler_params=_params())(proj, dcat, lg, lb, w, bias_t)


def _rope_tables(t):
    half = ROPE_DIM // 2
    inv_freq = ROPE_THETA ** (-jnp.arange(half, dtype=F32) * 2.0 / ROPE_DIM)
    ang = jnp.arange(t, dtype=F32)[:, None] * inv_freq[None, :]
    cos, sin = jnp.cos(ang), jnp.sin(ang)
    one = jnp.ones((t, HEAD_DIM - ROPE_DIM), F32)
    zero = jnp.zeros((t, HEAD_DIM - ROPE_DIM), F32)
    zh = jnp.zeros((t, half), F32)
    c = jnp.concatenate([cos, cos, one], axis=1)
    s1 = jnp.concatenate([-sin, zh, zero], axis=1)
    s2 = jnp.concatenate([zh, sin, zero], axis=1)
    return tuple(jnp.tile(a, (1, LANES // HEAD_DIM)) for a in (c, s1, s2))


QK_TILE = 512
QK_COLS = 2 * N_DIL * B_WIDTH


def _qk_fwd(proj, gains, tabs):
    t = proj.shape[0]
    col0 = 2 * A_WIDTH // 1024

    def body(p_ref, g_ref, c_ref, s1_ref, s2_ref, o_ref):
        seg = _segment_mean_matrix(HEAD_DIM)
        c, s1, s2 = c_ref[...], s1_ref[...], s2_ref[...]
        for ci in range(1024 // LANES):
            ls = slice(ci * LANES, (ci + 1) * LANES)
            xv = p_ref[:, ls].astype(F32)
            r = lax.rsqrt(_dot_hi(xv * xv, seg) + EPS)
            y = xv * r * g_ref[:, ls]
            o_ref[:, ls] = (y * c + pltpu.roll(y, LANES - 8, axis=1) * s1 + pltpu.roll(y, 8, axis=1) * s2).astype(BF16)

    tab = pl.BlockSpec((QK_TILE, LANES), lambda i, j: (i, 0))
    return pl.pallas_call(
        body, grid=(t // QK_TILE, QK_COLS // 1024),
        in_specs=[pl.BlockSpec((QK_TILE, 1024), lambda i, j: (i, col0 + j)), pl.BlockSpec((1, 1024), lambda i, j: (0, j)),
                  tab, tab, tab],
        out_specs=pl.BlockSpec((QK_TILE, 1024), lambda i, j: (i, j)), out_shape=SDS((t, QK_COLS), BF16),
        name="qk_norm_rope_fwd", compiler_params=_params())(proj, gains, *tabs)


PERM_TILE = 512


def _permute(name, items, rate):
    t = items[0][0].shape[0]
    n = len(items)
    rows = PERM_TILE // rate

    def body(*refs):
        scr = refs[-1]
        for x_ref, o_ref in zip(refs[:n], refs[n:2 * n]):
            for ci in range(B_WIDTH // LANES):
                scr[ci] = x_ref[:, ci * LANES:(ci + 1) * LANES].astype(F32)
            for rho in range(rate):
                for ci in range(B_WIDTH // LANES):
                    o_ref[rho, :, ci * LANES:(ci + 1) * LANES] = scr[ci, pl.ds(rho, rows, stride=rate), :].astype(o_ref.dtype)

    return pl.pallas_call(
        body, grid=(t // PERM_TILE,),
        in_specs=[pl.BlockSpec((PERM_TILE, B_WIDTH), functools.partial(lambda cb, i: (i, cb), cb)) for _, cb in items],
        out_specs=[pl.BlockSpec((rate, rows, B_WIDTH), lambda i: (0, i, 0)) for _ in items],
        out_shape=[SDS((rate, t // rate, B_WIDTH), a.dtype) for a, _ in items],
        scratch_shapes=[pltpu.VMEM((B_WIDTH // LANES, PERM_TILE, LANES), F32)],
        name=name, compiler_params=_params())(*[a for a, _ in items])


def _unpermute(name, arrays, rate):
    t = arrays[0].shape[1] * rate
    n = len(arrays)
    rows = PERM_TILE // rate

    def body(*refs):
        scr = refs[-1]
        for x_ref, o_ref in zip(refs[:n], refs[n:2 * n]):
            for rho in range(rate):
                for ci in range(B_WIDTH // LANES):
                    scr[ci, pl.ds(rho, rows, stride=rate), :] = x_ref[rho, :, ci * LANES:(ci + 1) * LANES].astype(F32)
            for ci in range(B_WIDTH // LANES):
                o_ref[:, ci * LANES:(ci + 1) * LANES] = scr[ci].astype(o_ref.dtype)

    return pl.pallas_call(
        body, grid=(t // PERM_TILE,),
        in_specs=[pl.BlockSpec((rate, rows, B_WIDTH), lambda i: (0, i, 0)) for _ in arrays],
        out_specs=[pl.BlockSpec((PERM_TILE, B_WIDTH), lambda i: (i, 0)) for _ in arrays],
        out_shape=[SDS((t, B_WIDTH), a.dtype) for a in arrays],
        scratch_shapes=[pltpu.VMEM((B_WIDTH // LANES, PERM_TILE, LANES), F32)],
        name=name, compiler_params=_params())(*arrays)


def _head_lane_mask(h):
    lane = lax.broadcasted_iota(jnp.int32, (1, LANES), 1)
    return (lane < HEAD_DIM) if h == 0 else (lane >= HEAD_DIM)


def _attn_fwd(name, q, k, v):
    rate, length = q[0].shape[0], q[0].shape[1]
    nb = length // ATT_BLOCK
    scale = HEAD_DIM ** -0.5

    def body(q_ref, kc_ref, kp_ref, vc_ref, vp_ref, o_ref, l_ref):
        n = pl.program_id(1)
        qi = lax.broadcasted_iota(jnp.int32, (ATT_BLOCK, 2 * ATT_BLOCK), 0)
        cj = lax.broadcasted_iota(jnp.int32, (ATT_BLOCK, 2 * ATT_BLOCK), 1)
        has_prev = jnp.where(n > 0, 0, 2 * ATT_BLOCK)
        mask = ((cj < ATT_BLOCK) & (cj >= qi + has_prev)) | ((cj >= ATT_BLOCK) & (cj - ATT_BLOCK <= qi))
        for hp in range(B_WIDTH // LANES):
            ls = slice(hp * LANES, (hp + 1) * LANES)
            q2 = q_ref[:, ls]
            k2 = jnp.concatenate([kp_ref[:, ls], kc_ref[:, ls]], axis=0)
            v2 = jnp.concatenate([vp_ref[:, ls], vc_ref[:, ls]], axis=0)
            o_acc, lse2 = None, None
            for h in range(2):
                hm = _head_lane_mask(h)
                s = _dot(jnp.where(hm, q2, jnp.zeros_like(q2)), k2, NT) * scale
                s = jnp.where(mask, s, NEG_INF)
                m = jnp.max(s, axis=1, keepdims=True)
                p = jnp.exp(s - m)
                den = jnp.sum(p, axis=1, keepdims=True)
                lse = m + jnp.log(den)
                o = _dot(p / den, jnp.where(hm, v2, jnp.zeros_like(v2)), NN)
                o_acc = o if h == 0 else o_acc + o
                lse_b = lse + jnp.zeros((ATT_BLOCK, LANES), F32)
                lse2 = lse_b if h == 0 else jnp.where(hm, lse_b, lse2)
            o_ref[:, ls] = o_acc
            l_ref[:, ls] = lse2

    def cur(cb):
        return pl.BlockSpec((None, ATT_BLOCK, B_WIDTH), lambda r, n: (r, n, cb))

    def prev(cb):
        return pl.BlockSpec((None, ATT_BLOCK, B_WIDTH), lambda r, n: (r, jnp.maximum(n - 1, 0), cb))

    out = pl.BlockSpec((None, ATT_BLOCK, B_WIDTH), lambda r, n: (r, n, 0))
    return pl.pallas_call(
        body, grid=(rate, nb),
        in_specs=[cur(q[1]), cur(k[1]), prev(k[1]), cur(v[1]), prev(v[1])],
        out_specs=[out, out], out_shape=[SDS((rate, length, B_WIDTH), F32)] * 2,
        name=name, compiler_params=_params())(q[0], k[0], k[0], v[0], v[0])


def _attn_merge(a_out, o_list, l_list):
    t = a_out.shape[0]
    tm = 512

    def body(a_ref, o0, o1, o2, l0, l1, l2, cat_ref, lt_ref):
        ls = [l0[...], l1[...], l2[...]]
        m = jnp.maximum(jnp.maximum(ls[0], ls[1]), ls[2])
        es = [jnp.exp(l - m) for l in ls]
        den = es[0] + es[1] + es[2]
        b = (es[0] * o0[...] + es[1] * o1[...] + es[2] * o2[...]) / den
        cat_ref[:, :A_WIDTH] = a_ref[...]
        cat_ref[:, A_WIDTH:] = b.astype(BF16)
        lt_ref[...] = m + jnp.log(den)

    blk = _row_spec(tm, B_WIDTH)
    return pl.pallas_call(
        body, grid=(t // tm,), in_specs=[blk] * 7,
        out_specs=[_row_spec(tm, A_WIDTH + B_WIDTH), blk],
        out_shape=[SDS((t, A_WIDTH + B_WIDTH), BF16), SDS((t, B_WIDTH), F32)],
        name="attn_merge", compiler_params=_params())(a_out, *o_list, *l_list)


def _attn_bwd_prep(dcat, cat):
    t = dcat.shape[0]
    tm = 512

    def body(d_ref, b_ref, db_ref, dd_ref):
        seg = _segment_mean_matrix(HEAD_DIM) * float(HEAD_DIM)
        for ci in range(B_WIDTH // LANES):
            ls = slice(ci * LANES, (ci + 1) * LANES)
            d = d_ref[:, ls]
            db_ref[:, ls] = d.astype(BF16)
            dd_ref[:, ls] = _dot_hi(d * b_ref[:, ls].astype(F32), seg)

    right = pl.BlockSpec((tm, B_WIDTH), lambda i: (i, 1))
    blk = _row_spec(tm, B_WIDTH)
    return pl.pallas_call(
        body, grid=(t // tm,), in_specs=[right, right], out_specs=[blk, blk],
        out_shape=[SDS((t, B_WIDTH), BF16), SDS((t, B_WIDTH), F32)],
        name="attn_bwd_prep", compiler_params=_params())(dcat, cat)


def _attn_bwd(name, q, k, v, db, lse, dd):
    rate, length = db.shape[0], db.shape[1]
    nb = length // ATT_BLOCK
    scale = HEAD_DIM ** -0.5

    def body(qa_ref, qb_ref, k_ref, v_ref, dba_ref, dbb_ref, la_ref, lb_ref, da_ref, dbd_ref, dq_ref, dk_ref, dv_ref, carry):
        m = pl.program_id(1)

        @pl.when(m == 0)
        def _():
            carry[...] = jnp.zeros_like(carry)

        qi = lax.broadcasted_iota(jnp.int32, (ATT_BLOCK, ATT_BLOCK), 0)
        kj = lax.broadcasted_iota(jnp.int32, (ATT_BLOCK, ATT_BLOCK), 1)
        masks = (kj <= qi, (kj >= qi) & (m + 1 < nb))
        for hp in range(B_WIDTH // LANES):
            ls = slice(hp * LANES, (hp + 1) * LANES)
            k2, v2 = k_ref[:, ls], v_ref[:, ls]
            sides = ((qa_ref[:, ls], dba_ref[:, ls], la_ref[:, ls], da_ref[:, ls]),
                     (qb_ref[:, ls], dbb_ref[:, ls], lb_ref[:, ls], dbd_ref[:, ls]))
            dq = [None, None]
            dk_acc, dv_acc = None, None
            for h in range(2):
                hm = _head_lane_mask(h)
                km = jnp.where(hm, k2, jnp.zeros_like(k2))
                vm = jnp.where(hm, v2, jnp.zeros_like(v2))
                for side in range(2):
                    q2, db2, lse2, dd2 = sides[side]
                    lse_col = jnp.max(jnp.where(hm, lse2, NEG_INF), axis=1, keepdims=True)
                    dd_col = jnp.max(jnp.where(hm, dd2, NEG_INF), axis=1, keepdims=True)
                    s = _dot(q2, km, NT) * scale
                    p = jnp.where(masks[side], jnp.exp(s - lse_col), 0.0)
                    dvc = _dot(p, jnp.where(hm, db2, jnp.zeros_like(db2)), TN)
                    dp = _dot(db2, vm, NT)
                    ds = (p * (dp - dd_col) * scale).astype(BF16)
                    dqc = _dot(ds, km, NN)
                    dkc = _dot(ds, jnp.where(hm, q2, jnp.zeros_like(q2)), TN)
                    dq[side] = dqc if dq[side] is None else dq[side] + dqc
                    dk_acc = dkc if dk_acc is None else dk_acc + dkc
                    dv_acc = dvc if dv_acc is None else dv_acc + dvc
            dq_ref[:, ls] = (dq[0] + carry[:, ls]).astype(BF16)
            carry[:, ls] = dq[1]
            dk_ref[:, ls] = dk_acc.astype(BF16)
            dv_ref[:, ls] = dv_acc.astype(BF16)

    def cur(cb):
        return pl.BlockSpec((None, ATT_BLOCK, B_WIDTH), lambda r, n: (r, n, cb))

    def nxt(cb):
        return pl.BlockSpec((None, ATT_BLOCK, B_WIDTH), lambda r, n: (r, jnp.minimum(n + 1, nb - 1), cb))

    out = cur(0)
    return pl.pallas_call(
        body, grid=(rate, nb),
        in_specs=[cur(q[1]), nxt(q[1]), cur(k[1]), cur(v[1]), cur(0), nxt(0), cur(0), nxt(0), cur(0), nxt(0)],
        out_specs=[out, out, out], out_shape=[SDS((rate, length, B_WIDTH), BF16)] * 3,
        scratch_shapes=[pltpu.VMEM((ATT_BLOCK, B_WIDTH), F32)],
        name=name, compiler_params=_params())(q[0], q[0], k[0], v[0], db, db, lse, lse, dd, dd)


AB_IN = 2 * A_WIDTH + 3 * N_DIL * B_WIDTH
ASM_TILE = 256


def _dproj_assemble(proj, d_a, dqk, dv, gains, tabs):
    t = proj.shape[0]
    n_qk = 2 * N_DIL

    def body(p_ref, da_ref, *rest):
        dqk_refs = rest[:n_qk]
        dv_refs = rest[n_qk:n_qk + N_DIL]
        g_ref, c_ref, s1_ref, s2_ref, o_ref, dg_ref = rest[n_qk + N_DIL:]

        @pl.when(pl.program_id(0) == 0)
        def _():
            dg_ref[...] = jnp.zeros_like(dg_ref)

        seg = _segment_mean_matrix(HEAD_DIM)
        c, s1, s2 = c_ref[...], s1_ref[...], s2_ref[...]
        o_ref[:, :2 * A_WIDTH] = da_ref[...]
        for jg in range(n_qk):
            for ci in range(B_WIDTH // LANES):
                col = jg * B_WIDTH + ci * LANES
                src = slice(2 * A_WIDTH + col, 2 * A_WIDTH + col + LANES)
                xv = p_ref[:, src].astype(F32)
                r = lax.rsqrt(_dot_hi(xv * xv, seg) + EPS)
                xh = xv * r
                gain = g_ref[:, col:col + LANES]
                do = dqk_refs[jg][:, ci * LANES:(ci + 1) * LANES].astype(F32)
                dy = do * c + pltpu.roll(do * s1, 8, axis=1) + pltpu.roll(do * s2, LANES - 8, axis=1)
                dg_ref[:, col:col + LANES] += jnp.sum(dy * xh, axis=0, keepdims=True)
                dxh = dy * gain
                o_ref[:, src] = (r * (dxh - xh * _dot_hi(dxh * xh, seg))).astype(BF16)
        v0 = 2 * A_WIDTH + QK_COLS
        for g in range(N_DIL):
            o_ref[:, v0 + g * B_WIDTH:v0 + (g + 1) * B_WIDTH] = dv_refs[g][...]

    blk = _row_spec(ASM_TILE, B_WIDTH)
    tab = _row_spec(ASM_TILE, LANES)
    return pl.pallas_call(
        body, grid=(t // ASM_TILE,),
        in_specs=[_row_spec(ASM_TILE, AB_IN), _row_spec(ASM_TILE, 2 * A_WIDTH)] + [blk] * (n_qk + N_DIL)
        + [_const_spec((1, QK_COLS)), tab, tab, tab],
        out_specs=[_row_spec(ASM_TILE, AB_IN), _const_spec((1, QK_COLS))],
        out_shape=[SDS((t, AB_IN), BF16), SDS((1, QK_COLS), F32)],
        name="dproj_assemble", compiler_params=_params())(proj, d_a, *dqk, *dv, gains, *tabs)


def _fold_heads(dg_lane):
    n = dg_lane.shape[1]

    def body(x_ref, o_ref):
        r = lax.broadcasted_iota(jnp.int32, (B_WIDTH, B_WIDTH), 0) % HEAD_DIM
        c = lax.broadcasted_iota(jnp.int32, (B_WIDTH, B_WIDTH), 1) % HEAD_DIM
        fold = jnp.where(r == c, 1.0, 0.0).astype(F32)
        for jg in range(n // B_WIDTH):
            ls = slice(jg * B_WIDTH, (jg + 1) * B_WIDTH)
            o_ref[:, ls] = _dot_hi(jnp.broadcast_to(x_ref[:, ls], (8, B_WIDTH)), fold)

    return pl.pallas_call(body, out_shape=SDS((8, n), F32), name="fold_heads", compiler_params=_params())(dg_lane)


CD_TILE = 256
CD_IN = 2 * C_WIDTH + 3 * 512


def _cd_split(pv):
    w = C_WIDTH
    return pv[:, :w], pv[:, w:2 * w], pv[:, 2 * w:3 * w], pv[:, 3 * w:4 * w], pv[:, 4 * w:5 * w]


def _mixer_cd_fwd(proj, cw, cb, lg, lb, dw):
    t = proj.shape[0]
    per = CD_TILE // HALO

    def body(h_ref, m_ref, cw_ref, cb_ref, lg_ref, lb_ref, dw_ref, o_ref, c_scr, e_scr):
        not_first = (pl.program_id(0) > 0).astype(F32)
        ha, hg, _, hgc, hhv = _cd_split(h_ref[...].astype(F32))
        ma, mg, mgb, mgc, mhv = _cd_split(m_ref[...].astype(F32))
        c_scr[:HALO] = ha * _sigmoid(hg) * not_first
        c_scr[HALO:] = ma * _sigmoid(mg)
        e_scr[:HALO] = hgc * hhv * not_first
        e_scr[HALO:] = mgc * mhv
        acc = jnp.zeros((CD_TILE, C_WIDTH), F32)
        for k in range(C_KERNEL):
            acc = acc + cw_ref[k:k + 1, :] * c_scr[pl.ds(HALO - (C_KERNEL - 1) + k, CD_TILE), :]
        c1 = acc + cb_ref[...]
        cc = c1 - jnp.mean(c1, axis=-1, keepdims=True)
        c2 = cc * lax.rsqrt(jnp.mean(cc * cc, axis=-1, keepdims=True) + EPS) * lg_ref[...] + lb_ref[...]
        o_ref[:, :C_WIDTH] = (c2 * _sigmoid(c2)).astype(BF16)
        d1 = jnp.zeros((CD_TILE, C_WIDTH), F32)
        for k in range(D_KERNEL):
            d1 = d1 + dw_ref[k:k + 1, :] * e_scr[pl.ds(HALO - (D_KERNEL - 1) + k, CD_TILE), :]
        o_ref[:, C_WIDTH:] = (mgb * d1).astype(BF16)

    return pl.pallas_call(
        body, grid=(t // CD_TILE,),
        in_specs=[pl.BlockSpec((HALO, CD_IN), lambda i: (jnp.maximum(i * per - 1, 0), 0)), _row_spec(CD_TILE, CD_IN),
                  _const_spec((32, C_WIDTH)), _const_spec((1, C_WIDTH)), _const_spec((1, C_WIDTH)), _const_spec((1, C_WIDTH)),
                  _const_spec((8, C_WIDTH))],
        out_specs=_row_spec(CD_TILE, 2 * C_WIDTH), out_shape=SDS((t, 2 * C_WIDTH), BF16),
        scratch_shapes=[pltpu.VMEM((HALO + CD_TILE, C_WIDTH), F32)] * 2,
        name="mixer_cd_fwd", compiler_params=_params())(proj, proj, cw, cb, lg, lb, dw)


def _mixer_cd_bwd(proj, dcat, cw, cb, lg, lb, dw):
    t = proj.shape[0]
    per = CD_TILE // HALO
    nt = t // CD_TILE
    ext = CD_TILE + HALO

    def body(hp_ref, m_ref, hn_ref, dm_ref, dn_ref, cw_ref, cb_ref, lg_ref, lb_ref, dw_ref,
             dp_ref, dcw_ref, dcb_ref, dlg_ref, dlb_ref, ddw_ref, c_scr, e_scr, dc1_scr, dd1_scr):
        i = pl.program_id(0)

        @pl.when(i == 0)
        def _():
            for r in (dcw_ref, dcb_ref, dlg_ref, dlb_ref, ddw_ref):
                r[...] = jnp.zeros_like(r)

        not_first = (i > 0).astype(F32)
        not_last = (i < nt - 1).astype(F32)
        pa, pg, _, pgc, phv = _cd_split(hp_ref[...].astype(F32))
        ma, mg, mgb, mgc, mhv = _cd_split(m_ref[...].astype(F32))
        na, ng, ngb, ngc, nhv = _cd_split(hn_ref[...].astype(F32))
        sig_m = _sigmoid(mg)
        c_scr[:HALO] = pa * _sigmoid(pg) * not_first
        c_scr[HALO:HALO + CD_TILE] = ma * sig_m
        c_scr[HALO + CD_TILE:] = na * _sigmoid(ng) * not_last
        e_scr[:HALO] = pgc * phv * not_first
        e_scr[HALO:HALO + CD_TILE] = mgc * mhv
        e_scr[HALO + CD_TILE:] = ngc * nhv * not_last

        acc = jnp.zeros((ext, C_WIDTH), F32)
        for k in range(C_KERNEL):
            acc = acc + cw_ref[k:k + 1, :] * c_scr[pl.ds(HALO - (C_KERNEL - 1) + k, ext), :]
        c1 = acc + cb_ref[...]
        cc = c1 - jnp.mean(c1, axis=-1, keepdims=True)
        rs = lax.rsqrt(jnp.mean(cc * cc, axis=-1, keepdims=True) + EPS)
        vhat = cc * rs
        c2 = vhat * lg_ref[...] + lb_ref[...]
        sig = _sigmoid(c2)
        dc = jnp.concatenate([dm_ref[:, :C_WIDTH], dn_ref[:, :C_WIDTH] * not_last], axis=0)
        dc2 = dc * (sig * (1.0 + c2 * (1.0 - sig)))
        dvh = dc2 * lg_ref[...]
        dc1 = rs * (dvh - jnp.mean(dvh, axis=-1, keepdims=True) - vhat * jnp.mean(dvh * vhat, axis=-1, keepdims=True))
        dc1_scr[...] = dc1
        dlg_ref[...] += jnp.sum((dc2 * vhat)[:CD_TILE], axis=0, keepdims=True)
        dlb_ref[...] += jnp.sum(dc2[:CD_TILE], axis=0, keepdims=True)
        dc1_m = dc1[:CD_TILE]
        dcb_ref[...] += jnp.sum(dc1_m, axis=0, keepdims=True)
        dc0 = jnp.zeros((CD_TILE, C_WIDTH), F32)
        for k in range(C_KERNEL):
            dc0 = dc0 + cw_ref[k:k + 1, :] * dc1_scr[pl.ds(C_KERNEL - 1 - k, CD_TILE), :]
            dcw_ref[k:k + 1, :] += jnp.sum(dc1_m * c_scr[pl.ds(HALO - (C_KERNEL - 1) + k, CD_TILE), :], axis=0, keepdims=True)
        dp_ref[:, :C_WIDTH] = (dc0 * sig_m).astype(BF16)
        dp_ref[:, C_WIDTH:2 * C_WIDTH] = (dc0 * ma * sig_m * (1.0 - sig_m)).astype(BF16)

        d1 = jnp.zeros((CD_TILE, C_WIDTH), F32)
        for k in range(D_KERNEL):
            d1 = d1 + dw_ref[k:k + 1, :] * e_scr[pl.ds(HALO - (D_KERNEL - 1) + k, CD_TILE), :]
        dd_m = dm_ref[:, C_WIDTH:]
        dd1 = jnp.concatenate([dd_m * mgb, dn_ref[:, C_WIDTH:] * ngb * not_last], axis=0)
        dd1_scr[...] = dd1
        dp_ref[:, 2 * C_WIDTH:3 * C_WIDTH] = (dd_m * d1).astype(BF16)
        de = jnp.zeros((CD_TILE, C_WIDTH), F32)
        for k in range(D_KERNEL):
            de = de + dw_ref[k:k + 1, :] * dd1_scr[pl.ds(D_KERNEL - 1 - k, CD_TILE), :]
            ddw_ref[k:k + 1, :] += jnp.sum(dd1[:CD_TILE] * e_scr[pl.ds(HALO - (D_KERNEL - 1) + k, CD_TILE), :], axis=0, keepdims=True)
        dp_ref[:, 3 * C_WIDTH:4 * C_WIDTH] = (de * mhv).astype(BF16)
        dp_ref[:, 4 * C_WIDTH:] = (de * mgc).astype(BF16)

    halo_prev = lambda i: (jnp.maximum(i * per - 1, 0), 0)
    halo_next = lambda i: (jnp.minimum((i + 1) * per, t // HALO - 1), 0)
    vec = _const_spec((1, C_WIDTH))
    return pl.pallas_call(
        body, grid=(nt,),
        in_specs=[pl.BlockSpec((HALO, CD_IN), halo_prev), _row_spec(CD_TILE, CD_IN), pl.BlockSpec((HALO, CD_IN), halo_next),
                  _row_spec(CD_TILE, 2 * C_WIDTH), pl.BlockSpec((HALO, 2 * C_WIDTH), halo_next),
                  _const_spec((32, C_WIDTH)), vec, vec, vec, _const_spec((8, C_WIDTH))],
        out_specs=[_row_spec(CD_TILE, CD_IN), _const_spec((32, C_WIDTH)), vec, vec, vec, _const_spec((8, C_WIDTH))],
        out_shape=[SDS((t, CD_IN), BF16), SDS((32, C_WIDTH), F32), SDS((1, C_WIDTH), F32), SDS((1, C_WIDTH), F32),
                   SDS((1, C_WIDTH), F32), SDS((8, C_WIDTH), F32)],
        scratch_shapes=[pltpu.VMEM((2 * HALO + CD_TILE, C_WIDTH), F32)] * 2 + [pltpu.VMEM((ext, C_WIDTH), F32)] * 2,
        name="mixer_cd_bwd", compiler_params=_params())(proj, proj, proj, dcat, dcat, cw, cb, lg, lb, dw)


def _wgrad(name, lhs, lhs_spec, rhs_list, rhs_spec, out_rc, t, layers, layer, bufs):
    tk = 512
    r, c = out_rc
    n = len(rhs_list)
    o_spec = pl.BlockSpec((None, None, r, c), lambda p, k: (p, layer, 0, 0))
    o = (SDS((N_CHIPS, layers, r, c), F32), o_spec)
    pairs = [(lhs, lhs_spec, rhs, rhs_spec, j) for j, rhs in enumerate(rhs_list)]
    n_pairs = len(pairs)

    def body(*refs):
        ab = refs[:2 * n_pairs]
        skip = 0 if bufs is None else n
        out_refs = refs[2 * n_pairs + skip:2 * n_pairs + skip + n]
        acc_refs = refs[2 * n_pairs + skip + n:]
        k = pl.program_id(1)
        parts = [_dot(ab[2 * j][...], ab[2 * j + 1][...], TN) for j in range(n_pairs)]

        @pl.when(k == 0)
        def _():
            for a in range(n):
                acc_refs[a][...] = parts[a]

        @pl.when(k > 0)
        def _():
            for a in range(n):
                acc_refs[a][...] += parts[a]

        @pl.when(k == t // tk - 1)
        def _():
            for a in range(n):
                out_refs[a][...] = acc_refs[a][...]

    operands, in_specs = [], []
    for a, a_spec, b, b_spec, _ in pairs:
        operands += [a, b]
        in_specs += [a_spec, b_spec]
    aliases = {}
    if bufs is not None:
        for j, buf in enumerate(bufs):
            aliases[len(operands)] = j
            operands.append(buf)
            in_specs.append(pl.BlockSpec(memory_space=pl.ANY))
    return pl.pallas_call(
        body, grid=(N_CHIPS, t // tk), in_specs=in_specs, out_specs=[o[1]] * n, out_shape=[o[0]] * n,
        scratch_shapes=[pltpu.VMEM((r, c), F32)] * n, input_output_aliases=aliases, name=name,
        compiler_params=_params())(*operands)


def _wgrad_col_sharded(name, h, dz_list, three_d, layers=1, layer=0, bufs=None):
    t, d = h.shape
    tk = 512
    n4 = dz_list[0].shape[-1] if three_d else dz_list[0].shape[-1] // N_CHIPS
    hs = pl.BlockSpec((tk, d), lambda p, k: (k, 0))
    zs = pl.BlockSpec((None, tk, n4), lambda p, k: (p, k, 0)) if three_d else pl.BlockSpec((tk, n4), lambda p, k: (k, p))
    return _wgrad(name, h, hs, dz_list, zs, (d, n4), t, layers, layer, bufs)


def _wgrad_row_sharded(name, a, g, three_d, layers=1, layer=0, buf=None):
    t, d = g.shape
    tk = 512
    k4 = a.shape[-1] if three_d else a.shape[-1] // N_CHIPS
    a_spec = pl.BlockSpec((None, tk, k4), lambda p, k: (p, k, 0)) if three_d else pl.BlockSpec((tk, k4), lambda p, k: (k, p))
    gs = pl.BlockSpec((tk, d), lambda p, k: (k, 0))
    return _wgrad(name, a, a_spec, [g], gs, (k4, d), t, layers, layer, None if buf is None else [buf])[0]


MESH = pl.DeviceIdType.MESH
ANY = pl.BlockSpec(memory_space=pl.ANY)


def _position():
    x, y, c = lax.axis_index("x"), lax.axis_index("y"), lax.axis_index("c")
    others = [(1 - x, y), (x, 1 - y), (1 - x, 1 - y)]
    return x, y, c, 2 * x + y, others


def _half_rows(ref, lead, c, h):
    nd = len(ref.shape)
    idx = tuple(lead) + (slice(None),) * (nd - 2 - len(lead)) + (pl.ds(c * h, h), slice(None))
    return ref.at[idx]


def _gather_weights(shards):
    n = len(shards)

    def body(*refs):
        ins, outs = refs[:n], refs[n:2 * n]
        send_sems, recv_sems, local_sems = refs[2 * n:]
        x, y, c, p, others = _position()
        sibling = (x, y, 1 - c)
        sends, locals_ = [], []
        for t in range(n):
            h = ins[t].shape[-2] // 2
            cp = pltpu.make_async_copy(ins[t], outs[t].at[p], local_sems.at[t])
            cp.start()
            locals_.append(cp)
            for j, (qx, qy) in enumerate(others):
                cp = pltpu.make_async_remote_copy(
                    _half_rows(ins[t], (), c, h), _half_rows(outs[t], (p,), c, h), send_sems.at[t, j], recv_sems.at[t, j],
                    device_id=(qx, qy, c), device_id_type=MESH)
                cp.start()
                sends.append(cp)
        for t in range(n):
            h = ins[t].shape[-2] // 2
            for j, (qx, qy) in enumerate(others):
                landed = _half_rows(outs[t], (2 * qx + qy,), c, h)
                pltpu.make_async_remote_copy(landed, landed, send_sems.at[t, j], recv_sems.at[t, j],
                                             device_id=(qx, qy, c), device_id_type=MESH).wait_recv()
                cp = pltpu.make_async_remote_copy(landed, landed, send_sems.at[t, 3 + j], recv_sems.at[t, 3 + j],
                                                  device_id=sibling, device_id_type=MESH)
                cp.start()
                sends.append(cp)
        for t in range(n):
            h = ins[t].shape[-2] // 2
            for j, (qx, qy) in enumerate(others):
                passed = _half_rows(outs[t], (2 * qx + qy,), 1 - c, h)
                pltpu.make_async_remote_copy(passed, passed, send_sems.at[t, 3 + j], recv_sems.at[t, 3 + j],
                                             device_id=sibling, device_id_type=MESH).wait_recv()
        for cp in sends:
            cp.wait_send()
        for cp in locals_:
            cp.wait()

    return pl.pallas_call(
        body, in_specs=[ANY] * n, out_specs=[ANY] * n,
        out_shape=[SDS((N_CHIPS,) + s.shape, s.dtype) for s in shards],
        scratch_shapes=[pltpu.SemaphoreType.DMA((n, 6)), pltpu.SemaphoreType.DMA((n, 6)), pltpu.SemaphoreType.DMA((n,))],
        name="gather_weights")(*shards)


def _swap_halves(tensors):
    n = len(tensors)

    def body(*refs):
        ins, outs = refs[:n], refs[n:2 * n]
        send_sems, recv_sems = refs[2 * n:]
        x, y, c, _, _ = _position()
        copies = []
        for t in range(n):
            h = ins[t].shape[-2] // 2
            cp = pltpu.make_async_remote_copy(_half_rows(ins[t], (), 1 - c, h), outs[t], send_sems.at[t], recv_sems.at[t],
                                              device_id=(x, y, 1 - c), device_id_type=MESH)
            cp.start()
            copies.append(cp)
        for cp in copies:
            cp.wait()

    return pl.pallas_call(
        body, in_specs=[ANY] * n, out_specs=[ANY] * n,
        out_shape=[SDS(s.shape[:-2] + (s.shape[-2] // 2, s.shape[-1]), s.dtype) for s in tensors],
        scratch_shapes=[pltpu.SemaphoreType.DMA((n,)), pltpu.SemaphoreType.DMA((n,))],
        name="swap_halves")(*tensors)


def _scatter_chips(tensors):
    n = len(tensors)

    def body(*refs):
        ins, outs = refs[:n], refs[n:2 * n]
        send_sems, recv_sems, local_sems = refs[2 * n:]
        x, y, c, p, others = _position()
        copies, locals_ = [], []
        for t in range(n):
            cp = pltpu.make_async_copy(ins[t].at[p], outs[t].at[p], local_sems.at[t])
            cp.start()
            locals_.append(cp)
            for j, (qx, qy) in enumerate(others):
                cp = pltpu.make_async_remote_copy(ins[t].at[2 * qx + qy], outs[t].at[p], send_sems.at[t, j], recv_sems.at[t, j],
                                                  device_id=(qx, qy, c), device_id_type=MESH)
                cp.start()
                copies.append((cp, t, j, 2 * qx + qy))
        for cp, t, j, q in copies:
            cp.wait_send()
            pltpu.make_async_remote_copy(outs[t].at[q], outs[t].at[q], send_sems.at[t, j], recv_sems.at[t, j],
                                         device_id=(x, y, c), device_id_type=MESH).wait_recv()
        for cp in locals_:
            cp.wait()

    return pl.pallas_call(
        body, in_specs=[ANY] * n, out_specs=[ANY] * n, out_shape=[SDS(s.shape, s.dtype) for s in tensors],
        scratch_shapes=[pltpu.SemaphoreType.DMA((n, 3)), pltpu.SemaphoreType.DMA((n, 3)), pltpu.SemaphoreType.DMA((n,))],
        name="scatter_chips")(*tensors)


def _join_halves(tensors):
    n = len(tensors)

    def body(*refs):
        ins, outs = refs[:n], refs[n:2 * n]
        send_sems, recv_sems, local_sems = refs[2 * n:]
        x, y, c, _, _ = _position()
        copies, locals_ = [], []
        for t in range(n):
            h = ins[t].shape[-2]
            mine = _half_rows(outs[t], (), c, h)
            cp = pltpu.make_async_copy(ins[t], mine, local_sems.at[t])
            cp.start()
            locals_.append(cp)
            cp = pltpu.make_async_remote_copy(ins[t], mine, send_sems.at[t], recv_sems.at[t],
                                              device_id=(x, y, 1 - c), device_id_type=MESH)
            cp.start()
            copies.append((cp, t, h))
        for cp, t, h in copies:
            cp.wait_send()
            theirs = _half_rows(outs[t], (), 1 - c, h)
            pltpu.make_async_remote_copy(theirs, theirs, send_sems.at[t], recv_sems.at[t],
                                         device_id=(x, y, 1 - c), device_id_type=MESH).wait_recv()
        for cp in locals_:
            cp.wait()

    return pl.pallas_call(
        body, in_specs=[ANY] * n, out_specs=[ANY] * n,
        out_shape=[SDS(s.shape[:-2] + (2 * s.shape[-2], s.shape[-1]), s.dtype) for s in tensors],
        scratch_shapes=[pltpu.SemaphoreType.DMA((n,)), pltpu.SemaphoreType.DMA((n,)), pltpu.SemaphoreType.DMA((n,))],
        name="join_halves")(*tensors)


def _add_own_half(name, full, recv, out_dtype):
    n4, layers, r, cols = full.shape
    h = r // 2
    core = lax.axis_index("c").astype(jnp.int32).reshape(1)

    def body(c_ref, a_ref, b_ref, o_ref):
        o_ref[...] = (a_ref[...] + b_ref[...]).astype(out_dtype)

    blk = (None, None, h, cols)
    return pl.pallas_call(
        body,
        grid_spec=pltpu.PrefetchScalarGridSpec(
            num_scalar_prefetch=1, grid=(n4, layers),
            in_specs=[pl.BlockSpec(blk, lambda p, l, c_ref: (p, l, c_ref[0], 0)), pl.BlockSpec(blk, lambda p, l, c_ref: (p, l, 0, 0))],
            out_specs=pl.BlockSpec(blk, lambda p, l, c_ref: (p, l, 0, 0))),
        out_shape=SDS((n4, layers, h, cols), out_dtype), name=name, compiler_params=_params())(core, full, recv)


def _sum_chips(name, parts):
    n4, layers, h, cols = parts.shape

    def body(a_ref, o_ref):
        acc = a_ref[0].astype(F32)
        for q in range(1, n4):
            acc = acc + a_ref[q].astype(F32)
        o_ref[...] = acc

    return pl.pallas_call(
        body, grid=(layers,), in_specs=[pl.BlockSpec((n4, None, h, cols), lambda l: (0, l, 0, 0))],
        out_specs=pl.BlockSpec((None, h, cols), lambda l: (l, 0, 0)), out_shape=SDS((layers, h, cols), F32),
        name=name, compiler_params=_params())(parts)


def _adamw_math(w, g, m, v):
    m2 = ADAM_B1 * m + (1.0 - ADAM_B1) * g
    v2 = ADAM_B2 * v + (1.0 - ADAM_B2) * (g * g)
    m_hat = m2 / (1.0 - ADAM_B1 ** ADAM_STEP)
    v_hat = v2 / (1.0 - ADAM_B2 ** ADAM_STEP)
    delta = -ADAM_LR * (m_hat / (jnp.sqrt(v_hat) + ADAM_EPS) + ADAM_WD * w)
    return delta, m2, v2


def _row_tile(rows, cols):
    cap = max(8, (1 << 18) // cols)
    best = 8
    for cand in range(8, min(rows, cap) + 1, 8):
        if rows % cand == 0:
            best = cand
    return best


def _adamw_big(name, w, g, m, v):
    shape = w.shape
    cols = shape[-1]
    rows = math.prod(shape[:-1])
    w2, g2, m2, v2 = (a.reshape(rows, cols) for a in (w, g, m, v))
    tr = _row_tile(rows, cols)

    def body(w_ref, g_ref, m_ref, v_ref, d_ref, mo_ref, vo_ref):
        d, mm, vv = _adamw_math(w_ref[...], g_ref[...], m_ref[...], v_ref[...])
        d_ref[...] = d
        mo_ref[...] = mm
        vo_ref[...] = vv

    blk = _row_spec(tr, cols)
    outs = pl.pallas_call(
        body, grid=(rows // tr,), in_specs=[blk] * 4, out_specs=[blk] * 3, out_shape=[SDS((rows, cols), F32)] * 3,
        name=name, compiler_params=_params())(w2, g2, m2, v2)
    return tuple(o.reshape(shape) for o in outs)


def _adamw_small(ws, gs, ms, vs):
    n = len(ws)
    flat = []
    for group in (ws, gs, ms, vs):
        flat += [a.reshape(-1, a.shape[-1]) for a in group]

    def body(*refs):
        w_r, g_r, m_r, v_r = refs[:n], refs[n:2 * n], refs[2 * n:3 * n], refs[3 * n:4 * n]
        d_o, m_o, v_o = refs[4 * n:5 * n], refs[5 * n:6 * n], refs[6 * n:7 * n]
        for j in range(n):
            d, mm, vv = _adamw_math(w_r[j][...], g_r[j][...], m_r[j][...], v_r[j][...])
            d_o[j][...] = d
            m_o[j][...] = mm
            v_o[j][...] = vv

    shapes = [SDS(a.shape, F32) for a in flat[:n]]
    outs = pl.pallas_call(body, out_shape=shapes * 3, name="adamw_small", compiler_params=_params())(*flat)
    res = []
    for k in range(3):
        res.append([outs[k * n + j].reshape(ws[j].shape) for j in range(n)])
    return res


BIG = ("ab_w_in", "ab_w_out", "cd_w_in", "cd_w_out", "ffn_w_gate", "ffn_w_up", "ffn_w_down")
V_BLOCK = (2 * A_WIDTH + QK_COLS) // B_WIDTH


def _pad_rows(a, rows):
    return jnp.pad(a, ((0, rows - a.shape[0]), (0, 0)))


def _local_step(x, target, w, sp):
    t, d = x.shape
    tabs = _rope_tables(t)
    gains = jnp.concatenate([jnp.tile(sp["q_norm_g"][g], HEAD_DIM // 8) for g in range(N_DIL)]
                            + [jnp.tile(sp["k_norm_g"][g], HEAD_DIM // 8) for g in range(N_DIL)]).reshape(1, QK_COLS)
    bias_t = sp["sgu_bias"].T
    cw = _pad_rows(sp["conv_c_w"], 32)
    dw = _pad_rows(sp["conv_d_w"], 8)
    cb, clg, clb = (sp[k].reshape(1, C_WIDTH) for k in ("conv_c_b", "c_ln_g", "c_ln_b"))
    slg, slb = sp["sgu_norm_g"].reshape(1, A_WIDTH), sp["sgu_norm_b"].reshape(1, A_WIDTH)
    g_ab, g_cd = sp["ab_norm_g"].reshape(1, d), sp["cd_norm_g"].reshape(1, d)
    g_f0, g_f1 = sp["ffn_norm_g"][0:1], sp["ffn_norm_g"][1:2]
    wo = w["ab_w_out"].reshape(-1, d)
    wo2 = w["cd_w_out"].reshape(-1, d)
    wg, wu, wd = w["ffn_w_gate"], w["ffn_w_up"], w["ffn_w_down"]

    h0 = _rms_fwd("rms_ab", x, g_ab)
    proj = _proj_in("proj_ab", h0, w["ab_w_in"], 0)
    a_out = _mixer_a_fwd(proj, slg, slb, sp["sgu_w"], bias_t)
    qk = _qk_fwd(proj, gains, tabs)
    qkv, o_list, l_list = [], [], []
    for g, rate in enumerate(DIL_RATES):
        if rate == 1:
            qk3, proj3 = qk.reshape(1, t, QK_COLS), proj.reshape(1, t, AB_IN)
            q, k, v = (qk3, g), (qk3, N_DIL + g), (proj3, V_BLOCK + g)
        else:
            qp, kp, vp = _permute(f"perm_fwd_{g}", [(qk, g), (qk, N_DIL + g), (proj, V_BLOCK + g)], rate)
            q, k, v = (qp, 0), (kp, 0), (vp, 0)
        qkv.append((q, k, v))
        o, l = _attn_fwd(f"attn_fwd_{g}", q, k, v)
        if rate == 1:
            o, l = o.reshape(t, B_WIDTH), l.reshape(t, B_WIDTH)
        else:
            o, l = _unpermute(f"unperm_fwd_{g}", [o, l], rate)
        o_list.append(o)
        l_list.append(l)
    cat, lse_tot = _attn_merge(a_out, o_list, l_list)
    x1, hf0 = _proj_out("out_ab", cat, wo, x, g_next=g_f0)
    gate0, up0, act0 = _ffn_in("ffn_in_0", hf0, wg, wu, 0)
    x2, h1 = _ffn_out("ffn_out_0", act0, wd, 0, x1, g_next=g_cd)
    projcd = _proj_in("proj_cd", h1, w["cd_w_in"], 0)
    cat2 = _mixer_cd_fwd(projcd, cw, cb, clg, clb, dw)
    x3, hf1 = _proj_out("out_cd", cat2, wo2, x2, g_next=g_f1)
    gate1, up1, act1 = _ffn_in("ffn_in_1", hf1, wg, wu, 1)
    dy, loss_acc = _ffn_out("ffn_out_1", act1, wd, 1, x3, target=target)
    loss = 0.5 * loss_acc[0, 0] / d

    grads = {}
    dgate, dup = _ffn_dact("ffn_dact_1", dy, wd, 1, gate1, up1)
    d_down = _wgrad_row_sharded("wgrad_down_1", act1, dy, True, layers=2, layer=1)
    d_gate, d_up = _wgrad_col_sharded("wgrad_gate_up_1", hf1, [dgate, dup], True, layers=2, layer=1)
    g3, d_f1 = _dgrad_cols("dgrad_ffn_1", [dgate, dup], [wg, wu], 1, True, x3, g_f1, dy)

    dcat2 = _dgrad_rows("dgrad_out_cd", g3, wo2)
    grads["cd_w_out"] = _wgrad_row_sharded("wgrad_out_cd", cat2, g3, False)
    dprojcd, d_cw, d_cb, d_clg, d_clb, d_dw = _mixer_cd_bwd(projcd, dcat2, cw, cb, clg, clb, dw)
    grads["cd_w_in"] = _wgrad_col_sharded("wgrad_in_cd", h1, [dprojcd], False)[0]
    g2, d_cdn = _dgrad_cols("dgrad_in_cd", [dprojcd], [w["cd_w_in"]], 0, False, x2, g_cd, g3)

    dgate, dup = _ffn_dact("ffn_dact_0", g2, wd, 0, gate0, up0)
    grads["ffn_w_down"] = _wgrad_row_sharded("wgrad_down_0", act0, g2, True, layers=2, layer=0, buf=d_down)
    grads["ffn_w_gate"], grads["ffn_w_up"] = _wgrad_col_sharded(
        "wgrad_gate_up_0", hf0, [dgate, dup], True, layers=2, layer=0, bufs=[d_gate, d_up])
    g1, d_f0 = _dgrad_cols("dgrad_ffn_0", [dgate, dup], [wg, wu], 0, True, x1, g_f0, g2)

    dcat = _dgrad_rows("dgrad_out_ab", g1, wo)
    grads["ab_w_out"] = _wgrad_row_sharded("wgrad_out_ab", cat, g1, False)
    d_a, d_sw, d_sbt, d_slg, d_slb = _mixer_a_bwd(proj, dcat, slg, slb, sp["sgu_w"], bias_t)
    dbb, dd = _attn_bwd_prep(dcat, cat)
    dqs, dks, dvs = [], [], []
    for g, rate in enumerate(DIL_RATES):
        q, k, v = qkv[g]
        if rate == 1:
            db3, l3, dd3 = (a.reshape(1, t, B_WIDTH) for a in (dbb, lse_tot, dd))
        else:
            db3, l3, dd3 = _permute(f"perm_bwd_{g}", [(dbb, 0), (lse_tot, 0), (dd, 0)], rate)
        dq, dk, dv = _attn_bwd(f"attn_bwd_{g}", q, k, v, db3, l3, dd3)
        if rate == 1:
            dq, dk, dv = (a.reshape(t, B_WIDTH) for a in (dq, dk, dv))
        else:
            dq, dk, dv = _unpermute(f"unperm_bwd_{g}", [dq, dk, dv], rate)
        dqs.append(dq)
        dks.append(dk)
        dvs.append(dv)
    dproj, d_gains = _dproj_assemble(proj, d_a, dqs + dks, dvs, gains, tabs)
    d_gains = _fold_heads(d_gains)[0].reshape(2, N_DIL, B_WIDTH)[:, :, :HEAD_DIM]
    grads["ab_w_in"] = _wgrad_col_sharded("wgrad_in_ab", h0, [dproj], False)[0]
    gx, d_abn = _dgrad_cols("dgrad_in_ab", [dproj], [w["ab_w_in"]], 0, False, x, g_ab, g1)

    small = {
        "ab_norm_g": d_abn, "sgu_norm_g": d_slg, "sgu_norm_b": d_slb, "sgu_w": d_sw, "sgu_bias": d_sbt.T,
        "q_norm_g": d_gains[0], "k_norm_g": d_gains[1], "cd_norm_g": d_cdn, "conv_c_w": d_cw[:C_KERNEL],
        "conv_c_b": d_cb, "c_ln_g": d_clg, "c_ln_b": d_clb, "conv_d_w": d_dw[:D_KERNEL],
        "ffn_norm_g": jnp.concatenate([d_f0, d_f1], axis=0),
    }
    return loss, gx, grads, small


SHARDED_SMALL = ("cd_norm_g", "conv_c_w", "conv_c_b", "c_ln_g", "c_ln_b", "conv_d_w")
SHARDED_ROWS = 48
REPLICATED_SMALL = ("ab_norm_g", "sgu_norm_g", "sgu_norm_b", "sgu_w", "sgu_bias", "q_norm_g", "k_norm_g", "ffn_norm_g")
REPLICATED_ROWS = 560


def _pack_sharded(parts):
    rows = [parts[k].reshape(-1, LANES) for k in SHARDED_SMALL]
    return _pad_rows(jnp.concatenate(rows, axis=0), SHARDED_ROWS)


def _split_full_small(small):
    per_chip = []
    for q in range(N_CHIPS):
        parts = {}
        for k in SHARDED_SMALL:
            a = small[k]
            a = a.reshape(-1, a.shape[-1])
            n = a.shape[-1] // N_CHIPS
            parts[k] = a[:, q * n:(q + 1) * n]
        per_chip.append(_pack_sharded(parts))
    return jnp.stack(per_chip)


def _unpack_sharded(pack, shapes):
    out, r = {}, 0
    for k in SHARDED_SMALL:
        n = math.prod(shapes[k]) // LANES
        out[k] = pack[r:r + n].reshape(shapes[k])
        r += n
    return out


def _gathered_small(packs, shapes):
    per_chip = [_unpack_sharded(packs[q], shapes) for q in range(N_CHIPS)]
    return {k: jnp.concatenate([pc[k] for pc in per_chip], axis=-1) for k in SHARDED_SMALL}


def _pack_replicated(small):
    rows = []
    for k in REPLICATED_SMALL:
        a = small[k].reshape(-1)
        a = jnp.pad(a, (0, (-a.shape[0]) % LANES))
        rows.append(a.reshape(-1, LANES))
    return _pad_rows(jnp.concatenate(rows, axis=0), REPLICATED_ROWS)


def _unpack_replicated(pack, shapes):
    out, r = {}, 0
    for k in REPLICATED_SMALL:
        size = math.prod(shapes[k])
        n = -(-size // LANES)
        out[k] = pack[r:r + n].reshape(-1)[:size].reshape(shapes[k])
        r += n
    return out


def _reduce_gradients(tensors, bf16_payload):
    recv = _swap_halves(tensors)
    halves = [_add_own_half(f"pair_sum_{j}", a, b, BF16 if bf else F32)
              for j, (a, b, bf) in enumerate(zip(tensors, recv, bf16_payload))]
    parts = _scatter_chips(halves)
    sums = [_sum_chips(f"chip_sum_{j}", a) for j, a in enumerate(parts)]
    return _join_halves(sums)


WEIGHT_ORDER = ("ab_norm_g", "ab_w_in", "sgu_norm_g", "sgu_norm_b", "sgu_w", "sgu_bias", "q_norm_g", "k_norm_g", "ab_w_out",
                "cd_norm_g", "cd_w_in", "conv_c_w", "conv_c_b", "c_ln_g", "c_ln_b", "conv_d_w", "cd_w_out", "ffn_norm_g",
                "ffn_w_gate", "ffn_w_up", "ffn_w_down")


def kernel(x, ab_norm_g, ab_w_in, sgu_norm_g, sgu_norm_b, sgu_w, sgu_bias, q_norm_g, k_norm_g, ab_w_out, cd_norm_g, cd_w_in, conv_c_w, conv_c_b, c_ln_g, c_ln_b, conv_d_w, cd_w_out, ffn_norm_g, ffn_w_gate, ffn_w_up, ffn_w_down, loss_target, m_ab_norm_g, m_ab_w_in, m_sgu_norm_g, m_sgu_norm_b, m_sgu_w, m_sgu_bias, m_q_norm_g, m_k_norm_g, m_ab_w_out, m_cd_norm_g, m_cd_w_in, m_conv_c_w, m_conv_c_b, m_c_ln_g, m_c_ln_b, m_conv_d_w, m_cd_w_out, m_ffn_norm_g, m_ffn_w_gate, m_ffn_w_up, m_ffn_w_down, v_ab_norm_g, v_ab_w_in, v_sgu_norm_g, v_sgu_norm_b, v_sgu_w, v_sgu_bias, v_q_norm_g, v_k_norm_g, v_ab_w_out, v_cd_norm_g, v_cd_w_in, v_conv_c_w, v_conv_c_b, v_c_ln_g, v_c_ln_b, v_conv_d_w, v_cd_w_out, v_ffn_norm_g, v_ffn_w_gate, v_ffn_w_up, v_ffn_w_down):
    args = dict(locals())
    ws = {k: args[k] for k in WEIGHT_ORDER}
    ms = {k: args["m_" + k] for k in WEIGHT_ORDER}
    vs = {k: args["v_" + k] for k in WEIGHT_ORDER}
    small_names = [k for k in WEIGHT_ORDER if k not in BIG]

    own_small = _pack_sharded({k: ws[k][0] for k in SHARDED_SMALL}).reshape(1, SHARDED_ROWS, LANES)
    gathered = _gather_weights([ws[k].astype(BF16) for k in BIG] + [own_small])
    w_full = dict(zip(BIG, gathered[:len(BIG)]))
    sp = _gathered_small(gathered[-1][:, 0], {k: ws[k].shape[1:] for k in SHARDED_SMALL})
    for k in REPLICATED_SMALL:
        sp[k] = ws[k] if k == "ffn_norm_g" else ws[k][0]

    loss, grad_x, g_big, g_small = _local_step(x[0], loss_target[0], w_full, sp)

    g_rep = jnp.broadcast_to(_pack_replicated(g_small), (N_CHIPS, 1, REPLICATED_ROWS, LANES))
    g_sh = _split_full_small(g_small).reshape(N_CHIPS, 1, SHARDED_ROWS, LANES)
    reduced = _reduce_gradients([g_big[k] for k in BIG] + [g_sh, g_rep], [True] * len(BIG) + [False, False])
    grad = dict(zip(BIG, reduced[:len(BIG)]))
    grad.update(_unpack_sharded(reduced[-2][0], {k: ws[k].shape for k in SHARDED_SMALL}))
    grad.update(_unpack_replicated(reduced[-1][0], {k: ws[k].shape for k in REPLICATED_SMALL}))

    delta, new_m, new_v = {}, {}, {}
    for k in BIG:
        delta[k], new_m[k], new_v[k] = _adamw_big("adamw_" + k, ws[k], grad[k], ms[k], vs[k])
    d_s, m_s, v_s = _adamw_small([ws[k] for k in small_names], [grad[k] for k in small_names],
                                 [ms[k] for k in small_names], [vs[k] for k in small_names])
    for j, k in enumerate(small_names):
        delta[k], new_m[k], new_v[k] = d_s[j], m_s[j], v_s[j]

    loss = lax.psum(loss, ("x", "y", "c"))
    return (loss, grad_x[None], *[grad[k] for k in WEIGHT_ORDER], *[delta[k] for k in WEIGHT_ORDER],
            *[new_m[k] for k in WEIGHT_ORDER], *[new_v[k] for k in WEIGHT_ORDER])
```

```python
import functools
import math

import jax
import jax.numpy as jnp
from jax import lax
from jax.experimental import pallas as pl
from jax.experimental.pallas import tpu as pltpu

F32 = jnp.float32
BF16 = jnp.bfloat16
SDS = jax.ShapeDtypeStruct

N_CHIPS = 4
EPS = 1e-6
NEG_INF = -1e30
CHUNK = 128
A_GROUPS = 4
A_WIDTH = 512
N_DIL = 3
DIL_RATES = (1, 4, 16)
HEAD_DIM = 64
B_WIDTH = 512
ROPE_DIM = 16
ROPE_THETA = 500000.0
C_WIDTH = 512
C_KERNEL = 31
D_KERNEL = 3
HALO = 32
ATT_BLOCK = 128
LANES = 128

ADAM_LR = 0.001
ADAM_B1 = 0.9
ADAM_B2 = 0.999
ADAM_EPS = 1e-08
ADAM_WD = 0.01
ADAM_STEP = 10

VMEM_LIMIT = 56 * 1024 * 1024

NN = (((1,), (0,)), ((), ()))
NT = (((1,), (1,)), ((), ()))
TN = (((0,), (0,)), ((), ()))


def _params(sem=None):
    return pltpu.CompilerParams(dimension_semantics=sem, vmem_limit_bytes=VMEM_LIMIT)


def _bf(v):
    return v if v.dtype == BF16 else v.astype(BF16)


def _dot(a, b, dims):
    return lax.dot_general(_bf(a), _bf(b), dims, preferred_element_type=F32)


def _dot_hi(a, b):
    return jnp.dot(a, b, precision=lax.Precision.HIGHEST, preferred_element_type=F32)


def _sigmoid(v):
    return 1.0 / (1.0 + jnp.exp(-v))


def _gelu(v):
    return 0.5 * v * (1.0 + lax.erf(v * (1.0 / math.sqrt(2.0))))


def _gelu_grad(v):
    cdf = 0.5 * (1.0 + lax.erf(v * (1.0 / math.sqrt(2.0))))
    return cdf + v * jnp.exp(-0.5 * v * v) * (1.0 / math.sqrt(2.0 * math.pi))


def _segment_mean_matrix(seg):
    r = lax.broadcasted_iota(jnp.int32, (LANES, LANES), 0) // seg
    c = lax.broadcasted_iota(jnp.int32, (LANES, LANES), 1) // seg
    return jnp.where(r == c, 1.0 / seg, 0.0).astype(F32)


def _matmul(name, grid, pairs, extras, outs, dims, epi, *, k_axis=None, acc_shapes=(), sem=None):
    n_pairs, n_ex, n_out, n_acc = len(pairs), len(extras), len(outs), len(acc_shapes)
    pair_acc = [p[4] for p in pairs]
    nk = grid[k_axis] if k_axis is not None else 1

    def body(*refs):
        ab = refs[:2 * n_pairs]
        ex = refs[2 * n_pairs:2 * n_pairs + n_ex]
        out_refs = refs[2 * n_pairs + n_ex:2 * n_pairs + n_ex + n_out]
        acc_refs = refs[2 * n_pairs + n_ex + n_out:]
        pids = tuple(pl.program_id(a) for a in range(len(grid)))
        parts = [None] * n_acc
        for j in range(n_pairs):
            d = _dot(ab[2 * j][...], ab[2 * j + 1][...], dims)
            parts[pair_acc[j]] = d if parts[pair_acc[j]] is None else parts[pair_acc[j]] + d
        if k_axis is None:
            epi(parts, ex, out_refs, pids)
        else:
            k = pids[k_axis]

            @pl.when(k == 0)
            def _():
                for a in range(n_acc):
                    acc_refs[a][...] = parts[a]

            @pl.when(k > 0)
            def _():
                for a in range(n_acc):
                    acc_refs[a][...] += parts[a]

            @pl.when(k == nk - 1)
            def _():
                epi([r[...] for r in acc_refs], ex, out_refs, pids)

    operands, in_specs = [], []
    for a, a_spec, b, b_spec, _ in pairs:
        operands += [a, b]
        in_specs += [a_spec, b_spec]
    for e, e_spec in extras:
        operands.append(e)
        in_specs.append(e_spec)
    scratch = [pltpu.VMEM(s, F32) for s in acc_shapes] if k_axis is not None else []
    res = pl.pallas_call(
        body, grid=grid, in_specs=in_specs, out_specs=[o[1] for o in outs],
        out_shape=[o[0] for o in outs], scratch_shapes=scratch, name=name,
        compiler_params=_params(sem))(*operands)
    return res


def _rms_rows(v, g):
    r = lax.rsqrt(jnp.mean(v * v, axis=-1, keepdims=True) + EPS)
    return v * r * g


def _rms_fwd(name, x, g):
    t, d = x.shape
    tm = 512

    def body(x_ref, g_ref, o_ref):
        o_ref[...] = _rms_rows(x_ref[...], g_ref[...]).astype(BF16)

    return pl.pallas_call(
        body, grid=(t // tm,),
        in_specs=[pl.BlockSpec((tm, d), lambda i: (i, 0)), pl.BlockSpec((1, d), lambda i: (0, 0))],
        out_specs=pl.BlockSpec((tm, d), lambda i: (i, 0)), out_shape=SDS((t, d), BF16), name=name,
        compiler_params=_params())(x, g)


def _epi_residual_norm(accs, ex, outs, pids):
    x_new = accs[0] + ex[0][...]
    outs[0][...] = x_new
    outs[1][...] = _rms_rows(x_new, ex[1][...]).astype(BF16)


def _epi_residual_loss(accs, ex, outs, pids):
    y = accs[0] + ex[0][...]
    err = y - ex[1][...]
    outs[0][...] = err * (1.0 / err.shape[-1])

    @pl.when(pids[0] == 0)
    def _():
        outs[1][...] = jnp.zeros_like(outs[1])

    outs[1][...] += jnp.sum(err * err)


def _epi_rms_bwd(accs, ex, outs, pids):
    dh = accs[0]
    xv, g, res = ex[0][...], ex[1][...], ex[2][...]
    r = lax.rsqrt(jnp.mean(xv * xv, axis=-1, keepdims=True) + EPS)
    xh = xv * r
    dy = dh * g
    outs[0][...] = res + r * (dy - xh * jnp.mean(dy * xh, axis=-1, keepdims=True))

    @pl.when(pids[0] == 0)
    def _():
        outs[1][...] = jnp.zeros_like(outs[1])

    outs[1][...] += jnp.sum(dh * xh, axis=0, keepdims=True)


def _row_spec(tm, d):
    return pl.BlockSpec((tm, d), lambda i, *_: (i, 0))


def _const_spec(shape):
    nd = len(shape)
    return pl.BlockSpec(shape, lambda *_: (0,) * nd)


def _proj_in(name, h, w, layer):
    t, d = h.shape
    n4 = w.shape[-1]
    tm = 512

    def epi(accs, ex, outs, pids):
        outs[0][...] = accs[0].astype(BF16)

    return _matmul(
        name, (N_CHIPS, t // tm),
        [(h, pl.BlockSpec((tm, d), lambda p, i: (i, 0)),
          w, pl.BlockSpec((None, None, d, n4), lambda p, i: (p, layer, 0, 0)), 0)],
        [], [(SDS((t, N_CHIPS * n4), BF16), pl.BlockSpec((tm, n4), lambda p, i: (i, p)))],
        NN, epi, acc_shapes=((tm, n4),))[0]


def _proj_out(name, a, w, x, g_next=None, target=None):
    t, k = a.shape
    d = w.shape[-1]
    tm = 512
    if target is None:
        extras = [(x, _row_spec(tm, d)), (g_next, _const_spec((1, d)))]
        outs = [(SDS((t, d), F32), _row_spec(tm, d)), (SDS((t, d), BF16), _row_spec(tm, d))]
        epi = _epi_residual_norm
    else:
        extras = [(x, _row_spec(tm, d)), (target, _row_spec(tm, d))]
        outs = [(SDS((t, d), F32), _row_spec(tm, d)), (SDS((8, LANES), F32), _const_spec((8, LANES)))]
        epi = _epi_residual_loss
    return _matmul(name, (t // tm,), [(a, _row_spec(tm, k), w, _const_spec((k, d)), 0)], extras, outs, NN, epi,
                   acc_shapes=((tm, d),))


def _ffn_in(name, h, wg, wu, layer):
    t, d = h.shape
    n4 = wg.shape[-1]
    tm = 512

    def epi(accs, ex, outs, pids):
        gate, up = accs
        outs[0][...] = gate.astype(BF16)
        outs[1][...] = up.astype(BF16)
        outs[2][...] = (gate * _sigmoid(gate) * up).astype(BF16)

    w_spec = pl.BlockSpec((None, None, d, n4), lambda p, i: (p, layer, 0, 0))
    h_spec = pl.BlockSpec((tm, d), lambda p, i: (i, 0))
    o = (SDS((N_CHIPS, t, n4), BF16), pl.BlockSpec((None, tm, n4), lambda p, i: (p, i, 0)))
    return _matmul(name, (N_CHIPS, t // tm), [(h, h_spec, wg, w_spec, 0), (h, h_spec, wu, w_spec, 1)], [],
                   [o, o, o], NN, epi, acc_shapes=((tm, n4), (tm, n4)))


def _ffn_out(name, act, wd, layer, x, g_next=None, target=None):
    _, t, n4 = act.shape
    d = wd.shape[-1]
    tm = 1024
    xs = pl.BlockSpec((tm, d), lambda i, k: (i, 0))
    if target is None:
        extras = [(x, xs), (g_next, _const_spec((1, d)))]
        outs = [(SDS((t, d), F32), xs), (SDS((t, d), BF16), xs)]
        epi = _epi_residual_norm
    else:
        extras = [(x, xs), (target, xs)]
        outs = [(SDS((t, d), F32), xs), (SDS((8, LANES), F32), _const_spec((8, LANES)))]
        epi = _epi_residual_loss
    return _matmul(
        name, (t // tm, N_CHIPS),
        [(act, pl.BlockSpec((None, tm, n4), lambda i, k: (k, i, 0)),
          wd, pl.BlockSpec((None, None, n4, d), lambda i, k: (k, layer, 0, 0)), 0)],
        extras, outs, NN, epi, k_axis=1, acc_shapes=((tm, d),))


def _ffn_dact(name, g, wd, layer, gate, up):
    t, d = g.shape
    n4 = wd.shape[-2]
    tm = 512

    def epi(accs, ex, outs, pids):
        dact = accs[0]
        gt = ex[0][...].astype(F32)
        upv = ex[1][...].astype(F32)
        s = _sigmoid(gt)
        outs[0][...] = (dact * upv * (s * (1.0 + gt * (1.0 - s)))).astype(BF16)
        outs[1][...] = (dact * gt * s).astype(BF16)

    blk = pl.BlockSpec((None, tm, n4), lambda p, i: (p, i, 0))
    o = (SDS((N_CHIPS, t, n4), BF16), blk)
    return _matmul(
        name, (N_CHIPS, t // tm),
        [(g, pl.BlockSpec((tm, d), lambda p, i: (i, 0)),
          wd, pl.BlockSpec((None, None, n4, d), lambda p, i: (p, layer, 0, 0)), 0)],
        [(gate, blk), (up, blk)], [o, o], NT, epi, acc_shapes=((tm, n4),))


def _copy_epi(accs, ex, outs, pids):
    for a, o in zip(accs, outs):
        o[...] = a.astype(o.dtype)


def _dgrad_cols(name, dz_list, w_list, layer, three_d, x, g, res):
    t, d = x.shape
    n4 = w_list[0].shape[-1]
    tm = 512
    if three_d:
        zs = pl.BlockSpec((None, tm, n4), lambda i, k: (k, i, 0))
    else:
        zs = pl.BlockSpec((tm, n4), lambda i, k: (i, k))
    ws = pl.BlockSpec((None, None, d, n4), lambda i, k: (k, layer, 0, 0))
    xs = pl.BlockSpec((tm, d), lambda i, k: (i, 0))
    return _matmul(
        name, (t // tm, N_CHIPS), [(dz, zs, w, ws, 0) for dz, w in zip(dz_list, w_list)],
        [(x, xs), (g, _const_spec((1, d))), (res, xs)],
        [(SDS((t, d), F32), xs), (SDS((1, d), F32), _const_spec((1, d)))],
        NT, _epi_rms_bwd, k_axis=1, acc_shapes=((tm, d),))


def _dgrad_rows(name, g, w):
    t, d = g.shape
    k = w.shape[0]
    tm = 512
    return _matmul(name, (t // tm,), [(g, _row_spec(tm, d), w, _const_spec((k, d)), 0)], [],
                   [(SDS((t, k), F32), _row_spec(tm, k))], NT, _copy_epi, acc_shapes=((tm, k),))[0]


A_TILE = 256


def _a_common(p_ref, lg_ref, lb_ref):
    pv = p_ref[...].astype(F32)
    a = _gelu(pv)
    u, v = a[:, :A_WIDTH], a[:, A_WIDTH:]
    vc = v - jnp.mean(v, axis=-1, keepdims=True)
    rs = lax.rsqrt(jnp.mean(vc * vc, axis=-1, keepdims=True) + EPS)
    vhat = vc * rs
    vn = vhat * lg_ref[...] + lb_ref[...]
    return pv, u, vhat, rs, vn.astype(BF16)


def _tril_weights(w_ref, g):
    r = lax.broadcasted_iota(jnp.int32, (CHUNK, CHUNK), 0)
    c = lax.broadcasted_iota(jnp.int32, (CHUNK, CHUNK), 1)
    return jnp.where(c <= r, w_ref[g], 0.0).astype(BF16), c <= r


def _mixer_a_fwd(proj, lg, lb, w, bias_t):
    t = proj.shape[0]

    def body(p_ref, lg_ref, lb_ref, w_ref, bt_ref, o_ref):
        _, u, _, _, vnb = _a_common(p_ref, lg_ref, lb_ref)
        for g in range(A_GROUPS):
            wt, _ = _tril_weights(w_ref, g)
            cs = slice(g * CHUNK, (g + 1) * CHUNK)
            for ch in range(A_TILE // CHUNK):
                rs_ = slice(ch * CHUNK, (ch + 1) * CHUNK)
                mixed = _dot(wt, vnb[rs_, cs], NN) + bt_ref[:, g:g + 1]
                o_ref[rs_, cs] = (u[rs_, cs] * mixed).astype(BF16)

    return pl.pallas_call(
        body, grid=(t // A_TILE,),
        in_specs=[pl.BlockSpec((A_TILE, 2 * A_WIDTH), lambda i: (i, 0)), _const_spec((1, A_WIDTH)),
                  _const_spec((1, A_WIDTH)), _const_spec((A_GROUPS, CHUNK, CHUNK)), _const_spec((CHUNK, A_GROUPS))],
        out_specs=pl.BlockSpec((A_TILE, A_WIDTH), lambda i: (i, 0)), out_shape=SDS((t, A_WIDTH), BF16),
        name="mixer_a_fwd", compiler_params=_params())(proj, lg, lb, w, bias_t)


def _mixer_a_bwd(proj, dcat, lg, lb, w, bias_t):
    t = proj.shape[0]

    def body(p_ref, da_ref, lg_ref, lb_ref, w_ref, bt_ref, dp_ref, dw_ref, dbt_ref, dlg_ref, dlb_ref, du_scr, dvn_scr):
        @pl.when(pl.program_id(0) == 0)
        def _():
            dw_ref[...] = jnp.zeros_like(dw_ref)
            dbt_ref[...] = jnp.zeros_like(dbt_ref)
            dlg_ref[...] = jnp.zeros_like(dlg_ref)
            dlb_ref[...] = jnp.zeros_like(dlb_ref)

        pv, u, vhat, rs, vnb = _a_common(p_ref, lg_ref, lb_ref)
        da = da_ref[...]
        for g in range(A_GROUPS):
            wt, keep = _tril_weights(w_ref, g)
            cs = slice(g * CHUNK, (g + 1) * CHUNK)
            for ch in range(A_TILE // CHUNK):
                rs_ = slice(ch * CHUNK, (ch + 1) * CHUNK)
                vg = vnb[rs_, cs]
                mixed = _dot(wt, vg, NN) + bt_ref[:, g:g + 1]
                du_scr[rs_, cs] = da[rs_, cs] * mixed
                dmx = da[rs_, cs] * u[rs_, cs]
                dw_ref[g] += jnp.where(keep, _dot(dmx, vg, NT), 0.0)
                dvn_scr[rs_, cs] = _dot(wt, dmx, TN)
                dbt_ref[:, g:g + 1] += jnp.sum(dmx, axis=1, keepdims=True)
        dvn = dvn_scr[...]
        dlg_ref[...] += jnp.sum(dvn * vhat, axis=0, keepdims=True)
        dlb_ref[...] += jnp.sum(dvn, axis=0, keepdims=True)
        dvh = dvn * lg_ref[...]
        dv = rs * (dvh - jnp.mean(dvh, axis=-1, keepdims=True) - vhat * jnp.mean(dvh * vhat, axis=-1, keepdims=True))
        gp = _gelu_grad(pv)
        dp_ref[:, :A_WIDTH] = (du_scr[...] * gp[:, :A_WIDTH]).astype(BF16)
        dp_ref[:, A_WIDTH:] = (dv * gp[:, A_WIDTH:]).astype(BF16)

    return pl.pallas_call(
        body, grid=(t // A_TILE,),
        in_specs=[pl.BlockSpec((A_TILE, 2 * A_WIDTH), lambda i: (i, 0)), pl.BlockSpec((A_TILE, A_WIDTH), lambda i: (i, 0)),
                  _const_spec((1, A_WIDTH)), _const_spec((1, A_WIDTH)), _const_spec((A_GROUPS, CHUNK, CHUNK)),
                  _const_spec((CHUNK, A_GROUPS))],
        out_specs=[pl.BlockSpec((A_TILE, 2 * A_WIDTH), lambda i: (i, 0)), _const_spec((A_GROUPS, CHUNK, CHUNK)),
                   _const_spec((CHUNK, A_GROUPS)), _const_spec((1, A_WIDTH)), _const_spec((1, A_WIDTH))],
        out_shape=[SDS((t, 2 * A_WIDTH), BF16), SDS((A_GROUPS, CHUNK, CHUNK), F32), SDS((CHUNK, A_GROUPS), F32),
                   SDS((1, A_WIDTH), F32), SDS((1, A_WIDTH), F32)],
        scratch_shapes=[pltpu.VMEM((A_TILE, A_WIDTH), F32), pltpu.VMEM((A_TILE, A_WIDTH), F32)],
        name="mixer_a_bwd", compiler_params=_params())(proj, dcat, lg, lb, w, bias_t)


def _rope_tables(t):
    half = ROPE_DIM // 2
    inv_freq = ROPE_THETA ** (-jnp.arange(half, dtype=F32) * 2.0 / ROPE_DIM)
    ang = jnp.arange(t, dtype=F32)[:, None] * inv_freq[None, :]
    cos, sin = jnp.cos(ang), jnp.sin(ang)
    one = jnp.ones((t, HEAD_DIM - ROPE_DIM), F32)
    zero = jnp.zeros((t, HEAD_DIM - ROPE_DIM), F32)
    zh = jnp.zeros((t, half), F32)
    c = jnp.concatenate([cos, cos, one], axis=1)
    s1 = jnp.concatenate([-sin, zh, zero], axis=1)
    s2 = jnp.concatenate([zh, sin, zero], axis=1)
    return tuple(jnp.tile(a, (1, LANES // HEAD_DIM)) for a in (c, s1, s2))


QK_TILE = 512
QK_COLS = 2 * N_DIL * B_WIDTH


def _qk_fwd(proj, gains, tabs):
    t = proj.shape[0]
    col0 = 2 * A_WIDTH // 1024

    def body(p_ref, g_ref, c_ref, s1_ref, s2_ref, o_ref):
        seg = _segment_mean_matrix(HEAD_DIM)
        c, s1, s2 = c_ref[...], s1_ref[...], s2_ref[...]
        for ci in range(1024 // LANES):
            ls = slice(ci * LANES, (ci + 1) * LANES)
            xv = p_ref[:, ls].astype(F32)
            r = lax.rsqrt(_dot_hi(xv * xv, seg) + EPS)
            y = xv * r * g_ref[:, ls]
            o_ref[:, ls] = (y * c + pltpu.roll(y, LANES - 8, axis=1) * s1 + pltpu.roll(y, 8, axis=1) * s2).astype(BF16)

    tab = pl.BlockSpec((QK_TILE, LANES), lambda i, j: (i, 0))
    return pl.pallas_call(
        body, grid=(t // QK_TILE, QK_COLS // 1024),
        in_specs=[pl.BlockSpec((QK_TILE, 1024), lambda i, j: (i, col0 + j)), pl.BlockSpec((1, 1024), lambda i, j: (0, j)),
                  tab, tab, tab],
        out_specs=pl.BlockSpec((QK_TILE, 1024), lambda i, j: (i, j)), out_shape=SDS((t, QK_COLS), BF16),
        name="qk_norm_rope_fwd", compiler_params=_params())(proj, gains, *tabs)


PERM_TILE = 512


def _permute(name, items, rate):
    t = items[0][0].shape[0]
    n = len(items)
    rows = PERM_TILE // rate

    def body(*refs):
        scr = refs[-1]
        for x_ref, o_ref in zip(refs[:n], refs[n:2 * n]):
            for ci in range(B_WIDTH // LANES):
                scr[ci] = x_ref[:, ci * LANES:(ci + 1) * LANES].astype(F32)
            for rho in range(rate):
                for ci in range(B_WIDTH // LANES):
                    o_ref[rho, :, ci * LANES:(ci + 1) * LANES] = scr[ci, pl.ds(rho, rows, stride=rate), :].astype(o_ref.dtype)

    return pl.pallas_call(
        body, grid=(t // PERM_TILE,),
        in_specs=[pl.BlockSpec((PERM_TILE, B_WIDTH), functools.partial(lambda cb, i: (i, cb), cb)) for _, cb in items],
        out_specs=[pl.BlockSpec((rate, rows, B_WIDTH), lambda i: (0, i, 0)) for _ in items],
        out_shape=[SDS((rate, t // rate, B_WIDTH), a.dtype) for a, _ in items],
        scratch_shapes=[pltpu.VMEM((B_WIDTH // LANES, PERM_TILE, LANES), F32)],
        name=name, compiler_params=_params())(*[a for a, _ in items])


def _unpermute(name, arrays, rate):
    t = arrays[0].shape[1] * rate
    n = len(arrays)
    rows = PERM_TILE // rate

    def body(*refs):
        scr = refs[-1]
        for x_ref, o_ref in zip(refs[:n], refs[n:2 * n]):
            for rho in range(rate):
                for ci in range(B_WIDTH // LANES):
                    scr[ci, pl.ds(rho, rows, stride=rate), :] = x_ref[rho, :, ci * LANES:(ci + 1) * LANES].astype(F32)
            for ci in range(B_WIDTH // LANES):
                o_ref[:, ci * LANES:(ci + 1) * LANES] = scr[ci].astype(o_ref.dtype)

    return pl.pallas_call(
        body, grid=(t // PERM_TILE,),
        in_specs=[pl.BlockSpec((rate, rows, B_WIDTH), lambda i: (0, i, 0)) for _ in arrays],
        out_specs=[pl.BlockSpec((PERM_TILE, B_WIDTH), lambda i: (i, 0)) for _ in arrays],
        out_shape=[SDS((t, B_WIDTH), a.dtype) for a in arrays],
        scratch_shapes=[pltpu.VMEM((B_WIDTH // LANES, PERM_TILE, LANES), F32)],
        name=name, compiler_params=_params())(*arrays)


def _head_lane_mask(h):
    lane = lax.broadcasted_iota(jnp.int32, (1, LANES), 1)
    return (lane < HEAD_DIM) if h == 0 else (lane >= HEAD_DIM)


def _attn_fwd(name, q, k, v):
    rate, length = q[0].shape[0], q[0].shape[1]
    nb = length // ATT_BLOCK
    scale = HEAD_DIM ** -0.5

    def body(q_ref, kc_ref, kp_ref, vc_ref, vp_ref, o_ref, l_ref):
        n = pl.program_id(1)
        qi = lax.broadcasted_iota(jnp.int32, (ATT_BLOCK, 2 * ATT_BLOCK), 0)
        cj = lax.broadcasted_iota(jnp.int32, (ATT_BLOCK, 2 * ATT_BLOCK), 1)
        has_prev = jnp.where(n > 0, 0, 2 * ATT_BLOCK)
        mask = ((cj < ATT_BLOCK) & (cj >= qi + has_prev)) | ((cj >= ATT_BLOCK) & (cj - ATT_BLOCK <= qi))
        for hp in range(B_WIDTH // LANES):
            ls = slice(hp * LANES, (hp + 1) * LANES)
            q2 = q_ref[:, ls]
            k2 = jnp.concatenate([kp_ref[:, ls], kc_ref[:, ls]], axis=0)
            v2 = jnp.concatenate([vp_ref[:, ls], vc_ref[:, ls]], axis=0)
            o_acc, lse2 = None, None
            for h in range(2):
                hm = _head_lane_mask(h)
                s = _dot(jnp.where(hm, q2, jnp.zeros_like(q2)), k2, NT) * scale
                s = jnp.where(mask, s, NEG_INF)
                m = jnp.max(s, axis=1, keepdims=True)
                p = jnp.exp(s - m)
                den = jnp.sum(p, axis=1, keepdims=True)
                lse = m + jnp.log(den)
                o = _dot(p / den, jnp.where(hm, v2, jnp.zeros_like(v2)), NN)
                o_acc = o if h == 0 else o_acc + o
                lse_b = lse + jnp.zeros((ATT_BLOCK, LANES), F32)
                lse2 = lse_b if h == 0 else jnp.where(hm, lse_b, lse2)
            o_ref[:, ls] = o_acc
            l_ref[:, ls] = lse2

    def cur(cb):
        return pl.BlockSpec((None, ATT_BLOCK, B_WIDTH), lambda r, n: (r, n, cb))

    def prev(cb):
        return pl.BlockSpec((None, ATT_BLOCK, B_WIDTH), lambda r, n: (r, jnp.maximum(n - 1, 0), cb))

    out = pl.BlockSpec((None, ATT_BLOCK, B_WIDTH), lambda r, n: (r, n, 0))
    return pl.pallas_call(
        body, grid=(rate, nb),
        in_specs=[cur(q[1]), cur(k[1]), prev(k[1]), cur(v[1]), prev(v[1])],
        out_specs=[out, out], out_shape=[SDS((rate, length, B_WIDTH), F32)] * 2,
        name=name, compiler_params=_params())(q[0], k[0], k[0], v[0], v[0])


def _attn_merge(a_out, o_list, l_list):
    t = a_out.shape[0]
    tm = 512

    def body(a_ref, o0, o1, o2, l0, l1, l2, cat_ref, lt_ref):
        ls = [l0[...], l1[...], l2[...]]
        m = jnp.maximum(jnp.maximum(ls[0], ls[1]), ls[2])
        es = [jnp.exp(l - m) for l in ls]
        den = es[0] + es[1] + es[2]
        b = (es[0] * o0[...] + es[1] * o1[...] + es[2] * o2[...]) / den
        cat_ref[:, :A_WIDTH] = a_ref[...]
        cat_ref[:, A_WIDTH:] = b.astype(BF16)
        lt_ref[...] = m + jnp.log(den)

    blk = _row_spec(tm, B_WIDTH)
    return pl.pallas_call(
        body, grid=(t // tm,), in_specs=[blk] * 7,
        out_specs=[_row_spec(tm, A_WIDTH + B_WIDTH), blk],
        out_shape=[SDS((t, A_WIDTH + B_WIDTH), BF16), SDS((t, B_WIDTH), F32)],
        name="attn_merge", compiler_params=_params())(a_out, *o_list, *l_list)


def _attn_bwd_prep(dcat, cat):
    t = dcat.shape[0]
    tm = 512

    def body(d_ref, b_ref, db_ref, dd_ref):
        seg = _segment_mean_matrix(HEAD_DIM) * float(HEAD_DIM)
        for ci in range(B_WIDTH // LANES):
            ls = slice(ci * LANES, (ci + 1) * LANES)
            d = d_ref[:, ls]
            db_ref[:, ls] = d.astype(BF16)
            dd_ref[:, ls] = _dot_hi(d * b_ref[:, ls].astype(F32), seg)

    right = pl.BlockSpec((tm, B_WIDTH), lambda i: (i, 1))
    blk = _row_spec(tm, B_WIDTH)
    return pl.pallas_call(
        body, grid=(t // tm,), in_specs=[right, right], out_specs=[blk, blk],
        out_shape=[SDS((t, B_WIDTH), BF16), SDS((t, B_WIDTH), F32)],
        name="attn_bwd_prep", compiler_params=_params())(dcat, cat)


def _attn_bwd(name, q, k, v, db, lse, dd):
    rate, length = db.shape[0], db.shape[1]
    nb = length // ATT_BLOCK
    scale = HEAD_DIM ** -0.5

    def body(qa_ref, qb_ref, k_ref, v_ref, dba_ref, dbb_ref, la_ref, lb_ref, da_ref, dbd_ref, dq_ref, dk_ref, dv_ref, carry):
        m = pl.program_id(1)

        @pl.when(m == 0)
        def _():
            carry[...] = jnp.zeros_like(carry)

        qi = lax.broadcasted_iota(jnp.int32, (ATT_BLOCK, ATT_BLOCK), 0)
        kj = lax.broadcasted_iota(jnp.int32, (ATT_BLOCK, ATT_BLOCK), 1)
        masks = (kj <= qi, (kj >= qi) & (m + 1 < nb))
        for hp in range(B_WIDTH // LANES):
            ls = slice(hp * LANES, (hp + 1) * LANES)
            k2, v2 = k_ref[:, ls], v_ref[:, ls]
            sides = ((qa_ref[:, ls], dba_ref[:, ls], la_ref[:, ls], da_ref[:, ls]),
                     (qb_ref[:, ls], dbb_ref[:, ls], lb_ref[:, ls], dbd_ref[:, ls]))
            dq = [None, None]
            dk_acc, dv_acc = None, None
            for h in range(2):
                hm = _head_lane_mask(h)
                km = jnp.where(hm, k2, jnp.zeros_like(k2))
                vm = jnp.where(hm, v2, jnp.zeros_like(v2))
                for side in range(2):
                    q2, db2, lse2, dd2 = sides[side]
                    lse_col = jnp.max(jnp.where(hm, lse2, NEG_INF), axis=1, keepdims=True)
                    dd_col = jnp.max(jnp.where(hm, dd2, NEG_INF), axis=1, keepdims=True)
                    s = _dot(q2, km, NT) * scale
                    p = jnp.where(masks[side], jnp.exp(s - lse_col), 0.0)
                    dvc = _dot(p, jnp.where(hm, db2, jnp.zeros_like(db2)), TN)
                    dp = _dot(db2, vm, NT)
                    ds = (p * (dp - dd_col) * scale).astype(BF16)
                    dqc = _dot(ds, km, NN)
                    dkc = _dot(ds, jnp.where(hm, q2, jnp.zeros_like(q2)), TN)
                    dq[side] = dqc if dq[side] is None else dq[side] + dqc
                    dk_acc = dkc if dk_acc is None else dk_acc + dkc
                    dv_acc = dvc if dv_acc is None else dv_acc + dvc
            dq_ref[:, ls] = (dq[0] + carry[:, ls]).astype(BF16)
            carry[:, ls] = dq[1]
            dk_ref[:, ls] = dk_acc.astype(BF16)
            dv_ref[:, ls] = dv_acc.astype(BF16)

    def cur(cb):
        return pl.BlockSpec((None, ATT_BLOCK, B_WIDTH), lambda r, n: (r, n, cb))

    def nxt(cb):
        return pl.BlockSpec((None, ATT_BLOCK, B_WIDTH), lambda r, n: (r, jnp.minimum(n + 1, nb - 1), cb))

    out = cur(0)
    return pl.pallas_call(
        body, grid=(rate, nb),
        in_specs=[cur(q[1]), nxt(q[1]), cur(k[1]), cur(v[1]), cur(0), nxt(0), cur(0), nxt(0), cur(0), nxt(0)],
        out_specs=[out, out, out], out_shape=[SDS((rate, length, B_WIDTH), BF16)] * 3,
        scratch_shapes=[pltpu.VMEM((ATT_BLOCK, B_WIDTH), F32)],
        name=name, compiler_params=_params())(q[0], q[0], k[0], v[0], db, db, lse, lse, dd, dd)


AB_IN = 2 * A_WIDTH + 3 * N_DIL * B_WIDTH
ASM_TILE = 256


def _dproj_assemble(proj, d_a, dqk, dv, gains, tabs):
    t = proj.shape[0]
    n_qk = 2 * N_DIL

    def body(p_ref, da_ref, *rest):
        dqk_refs = rest[:n_qk]
        dv_refs = rest[n_qk:n_qk + N_DIL]
        g_ref, c_ref, s1_ref, s2_ref, o_ref, dg_ref = rest[n_qk + N_DIL:]

        @pl.when(pl.program_id(0) == 0)
        def _():
            dg_ref[...] = jnp.zeros_like(dg_ref)

        seg = _segment_mean_matrix(HEAD_DIM)
        c, s1, s2 = c_ref[...], s1_ref[...], s2_ref[...]
        o_ref[:, :2 * A_WIDTH] = da_ref[...]
        for jg in range(n_qk):
            for ci in range(B_WIDTH // LANES):
                col = jg * B_WIDTH + ci * LANES
                src = slice(2 * A_WIDTH + col, 2 * A_WIDTH + col + LANES)
                xv = p_ref[:, src].astype(F32)
                r = lax.rsqrt(_dot_hi(xv * xv, seg) + EPS)
                xh = xv * r
                gain = g_ref[:, col:col + LANES]
                do = dqk_refs[jg][:, ci * LANES:(ci + 1) * LANES].astype(F32)
                dy = do * c + pltpu.roll(do * s1, 8, axis=1) + pltpu.roll(do * s2, LANES - 8, axis=1)
                dg_ref[:, col:col + LANES] += jnp.sum(dy * xh, axis=0, keepdims=True)
                dxh = dy * gain
                o_ref[:, src] = (r * (dxh - xh * _dot_hi(dxh * xh, seg))).astype(BF16)
        v0 = 2 * A_WIDTH + QK_COLS
        for g in range(N_DIL):
            o_ref[:, v0 + g * B_WIDTH:v0 + (g + 1) * B_WIDTH] = dv_refs[g][...]

    blk = _row_spec(ASM_TILE, B_WIDTH)
    tab = _row_spec(ASM_TILE, LANES)
    return pl.pallas_call(
        body, grid=(t // ASM_TILE,),
        in_specs=[_row_spec(ASM_TILE, AB_IN), _row_spec(ASM_TILE, 2 * A_WIDTH)] + [blk] * (n_qk + N_DIL)
        + [_const_spec((1, QK_COLS)), tab, tab, tab],
        out_specs=[_row_spec(ASM_TILE, AB_IN), _const_spec((1, QK_COLS))],
        out_shape=[SDS((t, AB_IN), BF16), SDS((1, QK_COLS), F32)],
        name="dproj_assemble", compiler_params=_params())(proj, d_a, *dqk, *dv, gains, *tabs)


def _fold_heads(dg_lane):
    n = dg_lane.shape[1]

    def body(x_ref, o_ref):
        r = lax.broadcasted_iota(jnp.int32, (B_WIDTH, B_WIDTH), 0) % HEAD_DIM
        c = lax.broadcasted_iota(jnp.int32, (B_WIDTH, B_WIDTH), 1) % HEAD_DIM
        fold = jnp.where(r == c, 1.0, 0.0).astype(F32)
        for jg in range(n // B_WIDTH):
            ls = slice(jg * B_WIDTH, (jg + 1) * B_WIDTH)
            o_ref[:, ls] = _dot_hi(jnp.broadcast_to(x_ref[:, ls], (8, B_WIDTH)), fold)

    return pl.pallas_call(body, out_shape=SDS((8, n), F32), name="fold_heads", compiler_params=_params())(dg_lane)


CD_TILE = 256
CD_IN = 2 * C_WIDTH + 3 * 512


def _cd_split(pv):
    w = C_WIDTH
    return pv[:, :w], pv[:, w:2 * w], pv[:, 2 * w:3 * w], pv[:, 3 * w:4 * w], pv[:, 4 * w:5 * w]


def _mixer_cd_fwd(proj, cw, cb, lg, lb, dw):
    t = proj.shape[0]
    per = CD_TILE // HALO

    def body(h_ref, m_ref, cw_ref, cb_ref, lg_ref, lb_ref, dw_ref, o_ref, c_scr, e_scr):
        not_first = (pl.program_id(0) > 0).astype(F32)
        ha, hg, _, hgc, hhv = _cd_split(h_ref[...].astype(F32))
        ma, mg, mgb, mgc, mhv = _cd_split(m_ref[...].astype(F32))
        c_scr[:HALO] = ha * _sigmoid(hg) * not_first
        c_scr[HALO:] = ma * _sigmoid(mg)
        e_scr[:HALO] = hgc * hhv * not_first
        e_scr[HALO:] = mgc * mhv
        acc = jnp.zeros((CD_TILE, C_WIDTH), F32)
        for k in range(C_KERNEL):
            acc = acc + cw_ref[k:k + 1, :] * c_scr[pl.ds(HALO - (C_KERNEL - 1) + k, CD_TILE), :]
        c1 = acc + cb_ref[...]
        cc = c1 - jnp.mean(c1, axis=-1, keepdims=True)
        c2 = cc * lax.rsqrt(jnp.mean(cc * cc, axis=-1, keepdims=True) + EPS) * lg_ref[...] + lb_ref[...]
        o_ref[:, :C_WIDTH] = (c2 * _sigmoid(c2)).astype(BF16)
        d1 = jnp.zeros((CD_TILE, C_WIDTH), F32)
        for k in range(D_KERNEL):
            d1 = d1 + dw_ref[k:k + 1, :] * e_scr[pl.ds(HALO - (D_KERNEL - 1) + k, CD_TILE), :]
        o_ref[:, C_WIDTH:] = (mgb * d1).astype(BF16)

    return pl.pallas_call(
        body, grid=(t // CD_TILE,),
        in_specs=[pl.BlockSpec((HALO, CD_IN), lambda i: (jnp.maximum(i * per - 1, 0), 0)), _row_spec(CD_TILE, CD_IN),
                  _const_spec((32, C_WIDTH)), _const_spec((1, C_WIDTH)), _const_spec((1, C_WIDTH)), _const_spec((1, C_WIDTH)),
                  _const_spec((8, C_WIDTH))],
        out_specs=_row_spec(CD_TILE, 2 * C_WIDTH), out_shape=SDS((t, 2 * C_WIDTH), BF16),
        scratch_shapes=[pltpu.VMEM((HALO + CD_TILE, C_WIDTH), F32)] * 2,
        name="mixer_cd_fwd", compiler_params=_params())(proj, proj, cw, cb, lg, lb, dw)


def _mixer_cd_bwd(proj, dcat, cw, cb, lg, lb, dw):
    t = proj.shape[0]
    per = CD_TILE // HALO
    nt = t // CD_TILE
    ext = CD_TILE + HALO

    def body(hp_ref, m_ref, hn_ref, dm_ref, dn_ref, cw_ref, cb_ref, lg_ref, lb_ref, dw_ref,
             dp_ref, dcw_ref, dcb_ref, dlg_ref, dlb_ref, ddw_ref, c_scr, e_scr, dc1_scr, dd1_scr):
        i = pl.program_id(0)

        @pl.when(i == 0)
        def _():
            for r in (dcw_ref, dcb_ref, dlg_ref, dlb_ref, ddw_ref):
                r[...] = jnp.zeros_like(r)

        not_first = (i > 0).astype(F32)
        not_last = (i < nt - 1).astype(F32)
        pa, pg, _, pgc, phv = _cd_split(hp_ref[...].astype(F32))
        ma, mg, mgb, mgc, mhv = _cd_split(m_ref[...].astype(F32))
        na, ng, ngb, ngc, nhv = _cd_split(hn_ref[...].astype(F32))
        sig_m = _sigmoid(mg)
        c_scr[:HALO] = pa * _sigmoid(pg) * not_first
        c_scr[HALO:HALO + CD_TILE] = ma * sig_m
        c_scr[HALO + CD_TILE:] = na * _sigmoid(ng) * not_last
        e_scr[:HALO] = pgc * phv * not_first
        e_scr[HALO:HALO + CD_TILE] = mgc * mhv
        e_scr[HALO + CD_TILE:] = ngc * nhv * not_last

        acc = jnp.zeros((ext, C_WIDTH), F32)
        for k in range(C_KERNEL):
            acc = acc + cw_ref[k:k + 1, :] * c_scr[pl.ds(HALO - (C_KERNEL - 1) + k, ext), :]
        c1 = acc + cb_ref[...]
        cc = c1 - jnp.mean(c1, axis=-1, keepdims=True)
        rs = lax.rsqrt(jnp.mean(cc * cc, axis=-1, keepdims=True) + EPS)
        vhat = cc * rs
        c2 = vhat * lg_ref[...] + lb_ref[...]
        sig = _sigmoid(c2)
        dc = jnp.concatenate([dm_ref[:, :C_WIDTH], dn_ref[:, :C_WIDTH] * not_last], axis=0)
        dc2 = dc * (sig * (1.0 + c2 * (1.0 - sig)))
        dvh = dc2 * lg_ref[...]
        dc1 = rs * (dvh - jnp.mean(dvh, axis=-1, keepdims=True) - vhat * jnp.mean(dvh * vhat, axis=-1, keepdims=True))
        dc1_scr[...] = dc1
        dlg_ref[...] += jnp.sum((dc2 * vhat)[:CD_TILE], axis=0, keepdims=True)
        dlb_ref[...] += jnp.sum(dc2[:CD_TILE], axis=0, keepdims=True)
        dc1_m = dc1[:CD_TILE]
        dcb_ref[...] += jnp.sum(dc1_m, axis=0, keepdims=True)
        dc0 = jnp.zeros((CD_TILE, C_WIDTH), F32)
        for k in range(C_KERNEL):
            dc0 = dc0 + cw_ref[k:k + 1, :] * dc1_scr[pl.ds(C_KERNEL - 1 - k, CD_TILE), :]
            dcw_ref[k:k + 1, :] += jnp.sum(dc1_m * c_scr[pl.ds(HALO - (C_KERNEL - 1) + k, CD_TILE), :], axis=0, keepdims=True)
        dp_ref[:, :C_WIDTH] = (dc0 * sig_m).astype(BF16)
        dp_ref[:, C_WIDTH:2 * C_WIDTH] = (dc0 * ma * sig_m * (1.0 - sig_m)).astype(BF16)

        d1 = jnp.zeros((CD_TILE, C_WIDTH), F32)
        for k in range(D_KERNEL):
            d1 = d1 + dw_ref[k:k + 1, :] * e_scr[pl.ds(HALO - (D_KERNEL - 1) + k, CD_TILE), :]
        dd_m = dm_ref[:, C_WIDTH:]
        dd1 = jnp.concatenate([dd_m * mgb, dn_ref[:, C_WIDTH:] * ngb * not_last], axis=0)
        dd1_scr[...] = dd1
        dp_ref[:, 2 * C_WIDTH:3 * C_WIDTH] = (dd_m * d1).astype(BF16)
        de = jnp.zeros((CD_TILE, C_WIDTH), F32)
        for k in range(D_KERNEL):
            de = de + dw_ref[k:k + 1, :] * dd1_scr[pl.ds(D_KERNEL - 1 - k, CD_TILE), :]
            ddw_ref[k:k + 1, :] += jnp.sum(dd1[:CD_TILE] * e_scr[pl.ds(HALO - (D_KERNEL - 1) + k, CD_TILE), :], axis=0, keepdims=True)
        dp_ref[:, 3 * C_WIDTH:4 * C_WIDTH] = (de * mhv).astype(BF16)
        dp_ref[:, 4 * C_WIDTH:] = (de * mgc).astype(BF16)

    halo_prev = lambda i: (jnp.maximum(i * per - 1, 0), 0)
    halo_next = lambda i: (jnp.minimum((i + 1) * per, t // HALO - 1), 0)
    vec = _const_spec((1, C_WIDTH))
    return pl.pallas_call(
        body, grid=(nt,),
        in_specs=[pl.BlockSpec((HALO, CD_IN), halo_prev), _row_spec(CD_TILE, CD_IN), pl.BlockSpec((HALO, CD_IN), halo_next),
                  _row_spec(CD_TILE, 2 * C_WIDTH), pl.BlockSpec((HALO, 2 * C_WIDTH), halo_next),
                  _const_spec((32, C_WIDTH)), vec, vec, vec, _const_spec((8, C_WIDTH))],
        out_specs=[_row_spec(CD_TILE, CD_IN), _const_spec((32, C_WIDTH)), vec, vec, vec, _const_spec((8, C_WIDTH))],
        out_shape=[SDS((t, CD_IN), BF16), SDS((32, C_WIDTH), F32), SDS((1, C_WIDTH), F32), SDS((1, C_WIDTH), F32),
                   SDS((1, C_WIDTH), F32), SDS((8, C_WIDTH), F32)],
        scratch_shapes=[pltpu.VMEM((2 * HALO + CD_TILE, C_WIDTH), F32)] * 2 + [pltpu.VMEM((ext, C_WIDTH), F32)] * 2,
        name="mixer_cd_bwd", compiler_params=_params())(proj, proj, proj, dcat, dcat, cw, cb, lg, lb, dw)


def _wgrad(name, lhs, lhs_spec, rhs_list, rhs_spec, out_rc, t, layers, layer, bufs):
    tk = 512
    r, c = out_rc
    n = len(rhs_list)
    o_spec = pl.BlockSpec((None, None, r, c), lambda p, k: (p, layer, 0, 0))
    o = (SDS((N_CHIPS, layers, r, c), F32), o_spec)
    pairs = [(lhs, lhs_spec, rhs, rhs_spec, j) for j, rhs in enumerate(rhs_list)]
    n_pairs = len(pairs)

    def body(*refs):
        ab = refs[:2 * n_pairs]
        skip = 0 if bufs is None else n
        out_refs = refs[2 * n_pairs + skip:2 * n_pairs + skip + n]
        acc_refs = refs[2 * n_pairs + skip + n:]
        k = pl.program_id(1)
        parts = [_dot(ab[2 * j][...], ab[2 * j + 1][...], TN) for j in range(n_pairs)]

        @pl.when(k == 0)
        def _():
            for a in range(n):
                acc_refs[a][...] = parts[a]

        @pl.when(k > 0)
        def _():
            for a in range(n):
                acc_refs[a][...] += parts[a]

        @pl.when(k == t // tk - 1)
        def _():
            for a in range(n):
                out_refs[a][...] = acc_refs[a][...]

    operands, in_specs = [], []
    for a, a_spec, b, b_spec, _ in pairs:
        operands += [a, b]
        in_specs += [a_spec, b_spec]
    aliases = {}
    if bufs is not None:
        for j, buf in enumerate(bufs):
            aliases[len(operands)] = j
            operands.append(buf)
            in_specs.append(pl.BlockSpec(memory_space=pl.ANY))
    return pl.pallas_call(
        body, grid=(N_CHIPS, t // tk), in_specs=in_specs, out_specs=[o[1]] * n, out_shape=[o[0]] * n,
        scratch_shapes=[pltpu.VMEM((r, c), F32)] * n, input_output_aliases=aliases, name=name,
        compiler_params=_params())(*operands)


def _wgrad_col_sharded(name, h, dz_list, three_d, layers=1, layer=0, bufs=None):
    t, d = h.shape
    tk = 512
    n4 = dz_list[0].shape[-1] if three_d else dz_list[0].shape[-1] // N_CHIPS
    hs = pl.BlockSpec((tk, d), lambda p, k: (k, 0))
    zs = pl.BlockSpec((None, tk, n4), lambda p, k: (p, k, 0)) if three_d else pl.BlockSpec((tk, n4), lambda p, k: (k, p))
    return _wgrad(name, h, hs, dz_list, zs, (d, n4), t, layers, layer, bufs)


def _wgrad_row_sharded(name, a, g, three_d, layers=1, layer=0, buf=None):
    t, d = g.shape
    tk = 512
    k4 = a.shape[-1] if three_d else a.shape[-1] // N_CHIPS
    a_spec = pl.BlockSpec((None, tk, k4), lambda p, k: (p, k, 0)) if three_d else pl.BlockSpec((tk, k4), lambda p, k: (k, p))
    gs = pl.BlockSpec((tk, d), lambda p, k: (k, 0))
    return _wgrad(name, a, a_spec, [g], gs, (k4, d), t, layers, layer, None if buf is None else [buf])[0]


MESH = pl.DeviceIdType.MESH
ANY = pl.BlockSpec(memory_space=pl.ANY)


def _position():
    x, y, c = lax.axis_index("x"), lax.axis_index("y"), lax.axis_index("c")
    others = [(1 - x, y), (x, 1 - y), (1 - x, 1 - y)]
    return x, y, c, 2 * x + y, others


def _gather_weights(shards):
    n = len(shards)

    def body(*refs):
        ins, outs = refs[:n], refs[n:2 * n]
        send_sems, recv_sems, local_sems = refs[2 * n:]
        x, y, c, p, others = _position()
        sibling = (x, y, 1 - c)
        sends, locals_ = [], []
        for t in range(n):
            cp = pltpu.make_async_copy(ins[t], outs[t].at[p], local_sems.at[t])
            cp.start()
            locals_.append(cp)
            for j, (qx, qy) in enumerate(others):
                cp = pltpu.make_async_remote_copy(ins[t].at[c], outs[t].at[p, c], send_sems.at[t, j], recv_sems.at[t, j],
                                                  device_id=(qx, qy, c), device_id_type=MESH)
                cp.start()
                sends.append(cp)
        for t in range(n):
            for j, (qx, qy) in enumerate(others):
                landed = outs[t].at[2 * qx + qy, c]
                pltpu.make_async_remote_copy(landed, landed, send_sems.at[t, j], recv_sems.at[t, j],
                                             device_id=(qx, qy, c), device_id_type=MESH).wait_recv()
                cp = pltpu.make_async_remote_copy(landed, landed, send_sems.at[t, 3 + j], recv_sems.at[t, 3 + j],
                                                  device_id=sibling, device_id_type=MESH)
                cp.start()
                sends.append(cp)
        for t in range(n):
            for j, (qx, qy) in enumerate(others):
                passed = outs[t].at[2 * qx + qy, 1 - c]
                pltpu.make_async_remote_copy(passed, passed, send_sems.at[t, 3 + j], recv_sems.at[t, 3 + j],
                                             device_id=sibling, device_id_type=MESH).wait_recv()
        for cp in sends:
            cp.wait_send()
        for cp in locals_:
            cp.wait()

    return pl.pallas_call(
        body, in_specs=[ANY] * n, out_specs=[ANY] * n,
        out_shape=[SDS((N_CHIPS,) + s.shape, s.dtype) for s in shards],
        scratch_shapes=[pltpu.SemaphoreType.DMA((n, 6)), pltpu.SemaphoreType.DMA((n, 6)), pltpu.SemaphoreType.DMA((n,))],
        name="gather_weights")(*shards)


def _swap_halves(tensors):
    n = len(tensors)

    def body(*refs):
        ins, outs = refs[:n], refs[n:2 * n]
        send_sems, recv_sems = refs[2 * n:]
        x, y, c, _, _ = _position()
        copies = []
        for t in range(n):
            cp = pltpu.make_async_remote_copy(ins[t].at[:, 1 - c], outs[t], send_sems.at[t], recv_sems.at[t],
                                              device_id=(x, y, 1 - c), device_id_type=MESH)
            cp.start()
            copies.append(cp)
        for cp in copies:
            cp.wait()

    return pl.pallas_call(
        body, in_specs=[ANY] * n, out_specs=[ANY] * n,
        out_shape=[SDS((s.shape[0],) + s.shape[2:], s.dtype) for s in tensors],
        scratch_shapes=[pltpu.SemaphoreType.DMA((n,)), pltpu.SemaphoreType.DMA((n,))],
        name="swap_halves")(*tensors)


def _scatter_chips(tensors):
    n = len(tensors)

    def body(*refs):
        ins, outs = refs[:n], refs[n:2 * n]
        send_sems, recv_sems, local_sems = refs[2 * n:]
        x, y, c, p, others = _position()
        copies, locals_ = [], []
        for t in range(n):
            cp = pltpu.make_async_copy(ins[t].at[p], outs[t].at[p], local_sems.at[t])
            cp.start()
            locals_.append(cp)
            for j, (qx, qy) in enumerate(others):
                cp = pltpu.make_async_remote_copy(ins[t].at[2 * qx + qy], outs[t].at[p], send_sems.at[t, j], recv_sems.at[t, j],
                                                  device_id=(qx, qy, c), device_id_type=MESH)
                cp.start()
                copies.append((cp, t, j, 2 * qx + qy))
        for cp, t, j, q in copies:
            cp.wait_send()
            pltpu.make_async_remote_copy(outs[t].at[q], outs[t].at[q], send_sems.at[t, j], recv_sems.at[t, j],
                                         device_id=(x, y, c), device_id_type=MESH).wait_recv()
        for cp in locals_:
            cp.wait()

    return pl.pallas_call(
        body, in_specs=[ANY] * n, out_specs=[ANY] * n, out_shape=[SDS(s.shape, s.dtype) for s in tensors],
        scratch_shapes=[pltpu.SemaphoreType.DMA((n, 3)), pltpu.SemaphoreType.DMA((n, 3)), pltpu.SemaphoreType.DMA((n,))],
        name="scatter_chips")(*tensors)


def _join_halves(tensors):
    n = len(tensors)

    def body(*refs):
        ins, outs = refs[:n], refs[n:2 * n]
        send_sems, recv_sems, local_sems = refs[2 * n:]
        x, y, c, _, _ = _position()
        copies, locals_ = [], []
        for t in range(n):
            cp = pltpu.make_async_copy(ins[t], outs[t].at[c], local_sems.at[t])
            cp.start()
            locals_.append(cp)
            cp = pltpu.make_async_remote_copy(ins[t], outs[t].at[c], send_sems.at[t], recv_sems.at[t],
                                              device_id=(x, y, 1 - c), device_id_type=MESH)
            cp.start()
            copies.append((cp, t))
        for cp, t in copies:
            cp.wait_send()
            theirs = outs[t].at[1 - c]
            pltpu.make_async_remote_copy(theirs, theirs, send_sems.at[t], recv_sems.at[t],
                                         device_id=(x, y, 1 - c), device_id_type=MESH).wait_recv()
        for cp in locals_:
            cp.wait()

    return pl.pallas_call(
        body, in_specs=[ANY] * n, out_specs=[ANY] * n,
        out_shape=[SDS((2,) + s.shape, s.dtype) for s in tensors],
        scratch_shapes=[pltpu.SemaphoreType.DMA((n,)), pltpu.SemaphoreType.DMA((n,)), pltpu.SemaphoreType.DMA((n,))],
        name="join_halves")(*tensors)


def _add_own_half(name, full, recv, out_dtype):
    n4, _, h, cols = full.shape
    core = lax.axis_index("c").astype(jnp.int32).reshape(1)

    def body(c_ref, a_ref, b_ref, o_ref):
        o_ref[...] = (a_ref[...] + b_ref[...]).astype(out_dtype)

    return pl.pallas_call(
        body,
        grid_spec=pltpu.PrefetchScalarGridSpec(
            num_scalar_prefetch=1, grid=(n4,),
            in_specs=[pl.BlockSpec((None, None, h, cols), lambda p, c_ref: (p, c_ref[0], 0, 0)),
                      pl.BlockSpec((None, h, cols), lambda p, c_ref: (p, 0, 0))],
            out_specs=pl.BlockSpec((None, h, cols), lambda p, c_ref: (p, 0, 0))),
        out_shape=SDS((n4, h, cols), out_dtype), name=name, compiler_params=_params())(core, full, recv)


def _sum_chips(name, parts):
    n4, h, cols = parts.shape
    th = h // 4 if h % 64 == 0 else h

    def body(a_ref, o_ref):
        acc = a_ref[0].astype(F32)
        for q in range(1, n4):
            acc = acc + a_ref[q].astype(F32)
        o_ref[...] = acc

    return pl.pallas_call(
        body, grid=(h // th,), in_specs=[pl.BlockSpec((n4, th, cols), lambda i: (0, i, 0))],
        out_specs=pl.BlockSpec((th, cols), lambda i: (i, 0)), out_shape=SDS((h, cols), F32),
        name=name, compiler_params=_params())(parts)


def _adamw_math(w, g, m, v):
    m2 = ADAM_B1 * m + (1.0 - ADAM_B1) * g
    v2 = ADAM_B2 * v + (1.0 - ADAM_B2) * (g * g)
    m_hat = m2 / (1.0 - ADAM_B1 ** ADAM_STEP)
    v_hat = v2 / (1.0 - ADAM_B2 ** ADAM_STEP)
    delta = -ADAM_LR * (m_hat / (jnp.sqrt(v_hat) + ADAM_EPS) + ADAM_WD * w)
    return delta, m2, v2


def _row_tile(rows, cols):
    cap = max(8, (1 << 18) // cols)
    best = 8
    for cand in range(8, min(rows, cap) + 1, 8):
        if rows % cand == 0:
            best = cand
    return best


def _adamw_big(name, w, g, m, v):
    shape = w.shape
    cols = shape[-1]
    rows = math.prod(shape[:-1])
    w2, g2, m2, v2 = (a.reshape(rows, cols) for a in (w, g, m, v))
    tr = _row_tile(rows, cols)

    def body(w_ref, g_ref, m_ref, v_ref, d_ref, mo_ref, vo_ref):
        d, mm, vv = _adamw_math(w_ref[...], g_ref[...], m_ref[...], v_ref[...])
        d_ref[...] = d
        mo_ref[...] = mm
        vo_ref[...] = vv

    blk = _row_spec(tr, cols)
    outs = pl.pallas_call(
        body, grid=(rows // tr,), in_specs=[blk] * 4, out_specs=[blk] * 3, out_shape=[SDS((rows, cols), F32)] * 3,
        name=name, compiler_params=_params())(w2, g2, m2, v2)
    return tuple(o.reshape(shape) for o in outs)


def _adamw_small(ws, gs, ms, vs):
    n = len(ws)
    flat = []
    for group in (ws, gs, ms, vs):
        flat += [a.reshape(-1, a.shape[-1]) for a in group]

    def body(*refs):
        w_r, g_r, m_r, v_r = refs[:n], refs[n:2 * n], refs[2 * n:3 * n], refs[3 * n:4 * n]
        d_o, m_o, v_o = refs[4 * n:5 * n], refs[5 * n:6 * n], refs[6 * n:7 * n]
        for j in range(n):
            d, mm, vv = _adamw_math(w_r[j][...], g_r[j][...], m_r[j][...], v_r[j][...])
            d_o[j][...] = d
            m_o[j][...] = mm
            v_o[j][...] = vv

    shapes = [SDS(a.shape, F32) for a in flat[:n]]
    outs = pl.pallas_call(body, out_shape=shapes * 3, name="adamw_small", compiler_params=_params())(*flat)
    res = []
    for k in range(3):
        res.append([outs[k * n + j].reshape(ws[j].shape) for j in range(n)])
    return res


BIG = ("ab_w_in", "ab_w_out", "cd_w_in", "cd_w_out", "ffn_w_gate", "ffn_w_up", "ffn_w_down")
V_BLOCK = (2 * A_WIDTH + QK_COLS) // B_WIDTH


def _pad_rows(a, rows):
    return jnp.pad(a, ((0, rows - a.shape[0]), (0, 0)))


def _local_step(x, target, w, sp):
    t, d = x.shape
    tabs = _rope_tables(t)
    gains = jnp.concatenate([jnp.tile(sp["q_norm_g"][g], HEAD_DIM // 8) for g in range(N_DIL)]
                            + [jnp.tile(sp["k_norm_g"][g], HEAD_DIM // 8) for g in range(N_DIL)]).reshape(1, QK_COLS)
    bias_t = sp["sgu_bias"].T
    cw = _pad_rows(sp["conv_c_w"], 32)
    dw = _pad_rows(sp["conv_d_w"], 8)
    cb, clg, clb = (sp[k].reshape(1, C_WIDTH) for k in ("conv_c_b", "c_ln_g", "c_ln_b"))
    slg, slb = sp["sgu_norm_g"].reshape(1, A_WIDTH), sp["sgu_norm_b"].reshape(1, A_WIDTH)
    g_ab, g_cd = sp["ab_norm_g"].reshape(1, d), sp["cd_norm_g"].reshape(1, d)
    g_f0, g_f1 = sp["ffn_norm_g"][0:1], sp["ffn_norm_g"][1:2]
    wo = w["ab_w_out"].reshape(-1, d)
    wo2 = w["cd_w_out"].reshape(-1, d)
    wg, wu, wd = w["ffn_w_gate"], w["ffn_w_up"], w["ffn_w_down"]

    h0 = _rms_fwd("rms_ab", x, g_ab)
    proj = _proj_in("proj_ab", h0, w["ab_w_in"], 0)
    a_out = _mixer_a_fwd(proj, slg, slb, sp["sgu_w"], bias_t)
    qk = _qk_fwd(proj, gains, tabs)
    qkv, o_list, l_list = [], [], []
    for g, rate in enumerate(DIL_RATES):
        if rate == 1:
            qk3, proj3 = qk.reshape(1, t, QK_COLS), proj.reshape(1, t, AB_IN)
            q, k, v = (qk3, g), (qk3, N_DIL + g), (proj3, V_BLOCK + g)
        else:
            qp, kp, vp = _permute(f"perm_fwd_{g}", [(qk, g), (qk, N_DIL + g), (proj, V_BLOCK + g)], rate)
            q, k, v = (qp, 0), (kp, 0), (vp, 0)
        qkv.append((q, k, v))
        o, l = _attn_fwd(f"attn_fwd_{g}", q, k, v)
        if rate == 1:
            o, l = o.reshape(t, B_WIDTH), l.reshape(t, B_WIDTH)
        else:
            o, l = _unpermute(f"unperm_fwd_{g}", [o, l], rate)
        o_list.append(o)
        l_list.append(l)
    cat, lse_tot = _attn_merge(a_out, o_list, l_list)
    x1, hf0 = _proj_out("out_ab", cat, wo, x, g_next=g_f0)
    gate0, up0, act0 = _ffn_in("ffn_in_0", hf0, wg, wu, 0)
    x2, h1 = _ffn_out("ffn_out_0", act0, wd, 0, x1, g_next=g_cd)
    projcd = _proj_in("proj_cd", h1, w["cd_w_in"], 0)
    cat2 = _mixer_cd_fwd(projcd, cw, cb, clg, clb, dw)
    x3, hf1 = _proj_out("out_cd", cat2, wo2, x2, g_next=g_f1)
    gate1, up1, act1 = _ffn_in("ffn_in_1", hf1, wg, wu, 1)
    dy, loss_acc = _ffn_out("ffn_out_1", act1, wd, 1, x3, target=target)
    loss = 0.5 * loss_acc[0, 0] / d

    grads = {}
    dgate, dup = _ffn_dact("ffn_dact_1", dy, wd, 1, gate1, up1)
    d_down = _wgrad_row_sharded("wgrad_down_1", act1, dy, True, layers=2, layer=1)
    d_gate, d_up = _wgrad_col_sharded("wgrad_gate_up_1", hf1, [dgate, dup], True, layers=2, layer=1)
    g3, d_f1 = _dgrad_cols("dgrad_ffn_1", [dgate, dup], [wg, wu], 1, True, x3, g_f1, dy)

    dcat2 = _dgrad_rows("dgrad_out_cd", g3, wo2)
    grads["cd_w_out"] = _wgrad_row_sharded("wgrad_out_cd", cat2, g3, False)
    dprojcd, d_cw, d_cb, d_clg, d_clb, d_dw = _mixer_cd_bwd(projcd, dcat2, cw, cb, clg, clb, dw)
    grads["cd_w_in"] = _wgrad_col_sharded("wgrad_in_cd", h1, [dprojcd], False)[0]
    g2, d_cdn = _dgrad_cols("dgrad_in_cd", [dprojcd], [w["cd_w_in"]], 0, False, x2, g_cd, g3)

    dgate, dup = _ffn_dact("ffn_dact_0", g2, wd, 0, gate0, up0)
    grads["ffn_w_down"] = _wgrad_row_sharded("wgrad_down_0", act0, g2, True, layers=2, layer=0, buf=d_down)
    grads["ffn_w_gate"], grads["ffn_w_up"] = _wgrad_col_sharded(
        "wgrad_gate_up_0", hf0, [dgate, dup], True, layers=2, layer=0, bufs=[d_gate, d_up])
    g1, d_f0 = _dgrad_cols("dgrad_ffn_0", [dgate, dup], [wg, wu], 0, True, x1, g_f0, g2)

    dcat = _dgrad_rows("dgrad_out_ab", g1, wo)
    grads["ab_w_out"] = _wgrad_row_sharded("wgrad_out_ab", cat, g1, False)
    d_a, d_sw, d_sbt, d_slg, d_slb = _mixer_a_bwd(proj, dcat, slg, slb, sp["sgu_w"], bias_t)
    dbb, dd = _attn_bwd_prep(dcat, cat)
    dqs, dks, dvs = [], [], []
    for g, rate in enumerate(DIL_RATES):
        q, k, v = qkv[g]
        if rate == 1:
            db3, l3, dd3 = (a.reshape(1, t, B_WIDTH) for a in (dbb, lse_tot, dd))
        else:
            db3, l3, dd3 = _permute(f"perm_bwd_{g}", [(dbb, 0), (lse_tot, 0), (dd, 0)], rate)
        dq, dk, dv = _attn_bwd(f"attn_bwd_{g}", q, k, v, db3, l3, dd3)
        if rate == 1:
            dq, dk, dv = (a.reshape(t, B_WIDTH) for a in (dq, dk, dv))
        else:
            dq, dk, dv = _unpermute(f"unperm_bwd_{g}", [dq, dk, dv], rate)
        dqs.append(dq)
        dks.append(dk)
        dvs.append(dv)
    dproj, d_gains = _dproj_assemble(proj, d_a, dqs + dks, dvs, gains, tabs)
    d_gains = _fold_heads(d_gains)[0].reshape(2, N_DIL, B_WIDTH)[:, :, :HEAD_DIM]
    grads["ab_w_in"] = _wgrad_col_sharded("wgrad_in_ab", h0, [dproj], False)[0]
    gx, d_abn = _dgrad_cols("dgrad_in_ab", [dproj], [w["ab_w_in"]], 0, False, x, g_ab, g1)

    small = {
        "ab_norm_g": d_abn, "sgu_norm_g": d_slg, "sgu_norm_b": d_slb, "sgu_w": d_sw, "sgu_bias": d_sbt.T,
        "q_norm_g": d_gains[0], "k_norm_g": d_gains[1], "cd_norm_g": d_cdn, "conv_c_w": d_cw[:C_KERNEL],
        "conv_c_b": d_cb, "c_ln_g": d_clg, "c_ln_b": d_clb, "conv_d_w": d_dw[:D_KERNEL],
        "ffn_norm_g": jnp.concatenate([d_f0, d_f1], axis=0),
    }
    return loss, gx, grads, small


SHARDED_SMALL = ("cd_norm_g", "conv_c_w", "conv_c_b", "c_ln_g", "c_ln_b", "conv_d_w")
SHARDED_ROWS = 48
REPLICATED_SMALL = ("ab_norm_g", "sgu_norm_g", "sgu_norm_b", "sgu_w", "sgu_bias", "q_norm_g", "k_norm_g", "ffn_norm_g")
REPLICATED_ROWS = 560


def _pack_sharded(parts):
    rows = [parts[k].reshape(-1, LANES) for k in SHARDED_SMALL]
    return _pad_rows(jnp.concatenate(rows, axis=0), SHARDED_ROWS)


def _split_full_small(small):
    per_chip = []
    for q in range(N_CHIPS):
        parts = {}
        for k in SHARDED_SMALL:
            a = small[k]
            a = a.reshape(-1, a.shape[-1])
            n = a.shape[-1] // N_CHIPS
            parts[k] = a[:, q * n:(q + 1) * n]
        per_chip.append(_pack_sharded(parts))
    return jnp.stack(per_chip)


def _unpack_sharded(pack, shapes):
    out, r = {}, 0
    for k in SHARDED_SMALL:
        n = math.prod(shapes[k]) // LANES
        out[k] = pack[r:r + n].reshape(shapes[k])
        r += n
    return out


def _gathered_small(packs, shapes):
    per_chip = [_unpack_sharded(packs[q], shapes) for q in range(N_CHIPS)]
    return {k: jnp.concatenate([pc[k] for pc in per_chip], axis=-1) for k in SHARDED_SMALL}


def _pack_replicated(small):
    rows = []
    for k in REPLICATED_SMALL:
        a = small[k].reshape(-1)
        a = jnp.pad(a, (0, (-a.shape[0]) % LANES))
        rows.append(a.reshape(-1, LANES))
    return _pad_rows(jnp.concatenate(rows, axis=0), REPLICATED_ROWS)


def _unpack_replicated(pack, shapes):
    out, r = {}, 0
    for k in REPLICATED_SMALL:
        size = math.prod(shapes[k])
        n = -(-size // LANES)
        out[k] = pack[r:r + n].reshape(-1)[:size].reshape(shapes[k])
        r += n
    return out


def _two_halves(a):
    rows = math.prod(a.shape[:-1])
    return a.reshape(2, rows // 2, a.shape[-1])


def _reduce_gradients(tensors, bf16_payload):
    tensors = [a.reshape((N_CHIPS, 2, math.prod(a.shape[1:-1]) // 2, a.shape[-1])) for a in tensors]
    recv = _swap_halves(tensors)
    halves = [_add_own_half(f"pair_sum_{j}", a, b, BF16 if bf else F32)
              for j, (a, b, bf) in enumerate(zip(tensors, recv, bf16_payload))]
    parts = _scatter_chips(halves)
    sums = [_sum_chips(f"chip_sum_{j}", a) for j, a in enumerate(parts)]
    return _join_halves(sums)


WEIGHT_ORDER = ("ab_norm_g", "ab_w_in", "sgu_norm_g", "sgu_norm_b", "sgu_w", "sgu_bias", "q_norm_g", "k_norm_g", "ab_w_out",
                "cd_norm_g", "cd_w_in", "conv_c_w", "conv_c_b", "c_ln_g", "c_ln_b", "conv_d_w", "cd_w_out", "ffn_norm_g",
                "ffn_w_gate", "ffn_w_up", "ffn_w_down")


def kernel(x, ab_norm_g, ab_w_in, sgu_norm_g, sgu_norm_b, sgu_w, sgu_bias, q_norm_g, k_norm_g, ab_w_out, cd_norm_g, cd_w_in, conv_c_w, conv_c_b, c_ln_g, c_ln_b, conv_d_w, cd_w_out, ffn_norm_g, ffn_w_gate, ffn_w_up, ffn_w_down, loss_target, m_ab_norm_g, m_ab_w_in, m_sgu_norm_g, m_sgu_norm_b, m_sgu_w, m_sgu_bias, m_q_norm_g, m_k_norm_g, m_ab_w_out, m_cd_norm_g, m_cd_w_in, m_conv_c_w, m_conv_c_b, m_c_ln_g, m_c_ln_b, m_conv_d_w, m_cd_w_out, m_ffn_norm_g, m_ffn_w_gate, m_ffn_w_up, m_ffn_w_down, v_ab_norm_g, v_ab_w_in, v_sgu_norm_g, v_sgu_norm_b, v_sgu_w, v_sgu_bias, v_q_norm_g, v_k_norm_g, v_ab_w_out, v_cd_norm_g, v_cd_w_in, v_conv_c_w, v_conv_c_b, v_c_ln_g, v_c_ln_b, v_conv_d_w, v_cd_w_out, v_ffn_norm_g, v_ffn_w_gate, v_ffn_w_up, v_ffn_w_down):
    args = dict(locals())
    ws = {k: args[k] for k in WEIGHT_ORDER}
    ms = {k: args["m_" + k] for k in WEIGHT_ORDER}
    vs = {k: args["v_" + k] for k in WEIGHT_ORDER}
    small_names = [k for k in WEIGHT_ORDER if k not in BIG]

    own_small = _pack_sharded({k: ws[k][0] for k in SHARDED_SMALL})
    gathered = _gather_weights([_two_halves(ws[k].astype(BF16)) for k in BIG] + [_two_halves(own_small)])
    w_full = {k: g.reshape((N_CHIPS,) + ws[k].shape) for k, g in zip(BIG, gathered)}
    sp = _gathered_small(gathered[-1].reshape(N_CHIPS, SHARDED_ROWS, LANES), {k: ws[k].shape[1:] for k in SHARDED_SMALL})
    for k in REPLICATED_SMALL:
        sp[k] = ws[k] if k == "ffn_norm_g" else ws[k][0]

    loss, grad_x, g_big, g_small = _local_step(x[0], loss_target[0], w_full, sp)

    g_rep = jnp.broadcast_to(_pack_replicated(g_small), (N_CHIPS, REPLICATED_ROWS, LANES))
    g_sh = _split_full_small(g_small)
    reduced = _reduce_gradients([g_big[k] for k in BIG] + [g_sh, g_rep], [True] * len(BIG) + [False, False])
    grad = {k: r.reshape(ws[k].shape) for k, r in zip(BIG, reduced)}
    grad.update(_unpack_sharded(reduced[-2].reshape(SHARDED_ROWS, LANES), {k: ws[k].shape for k in SHARDED_SMALL}))
    grad.update(_unpack_replicated(reduced[-1].reshape(REPLICATED_ROWS, LANES), {k: ws[k].shape for k in REPLICATED_SMALL}))

    delta, new_m, new_v = {}, {}, {}
    for k in BIG:
        delta[k], new_m[k], new_v[k] = _adamw_big("adamw_" + k, ws[k], grad[k], ms[k], vs[k])
    d_s, m_s, v_s = _adamw_small([ws[k] for k in small_names], [grad[k] for k in small_names],
                                 [ms[k] for k in small_names], [vs[k] for k in small_names])
    for j, k in enumerate(small_names):
        delta[k], new_m[k], new_v[k] = d_s[j], m_s[j], v_s[j]

    loss = lax.psum(loss, ("x", "y", "c"))
    return (loss, grad_x[None], *[grad[k] for k in WEIGHT_ORDER], *[delta[k] for k in WEIGHT_ORDER],
            *[new_m[k] for k in WEIGHT_ORDER], *[new_v[k] for k in WEIGHT_ORDER])
```

```python
import functools
import math

import jax
import jax.numpy as jnp
from jax import lax
from jax.experimental import pallas as pl
from jax.experimental.pallas import tpu as pltpu

F32 = jnp.float32
BF16 = jnp.bfloat16
SDS = jax.ShapeDtypeStruct

N_CHIPS = 4
EPS = 1e-6
NEG_INF = -1e30
CHUNK = 128
A_GROUPS = 4
A_WIDTH = 512
N_DIL = 3
DIL_RATES = (1, 4, 16)
HEAD_DIM = 64
B_WIDTH = 512
ROPE_DIM = 16
ROPE_THETA = 500000.0
C_WIDTH = 512
C_KERNEL = 31
D_KERNEL = 3
HALO = 32
ATT_BLOCK = 128
LANES = 128

ADAM_LR = 0.001
ADAM_B1 = 0.9
ADAM_B2 = 0.999
ADAM_EPS = 1e-08
ADAM_WD = 0.01
ADAM_STEP = 10

VMEM_LIMIT = 56 * 1024 * 1024

NN = (((1,), (0,)), ((), ()))
NT = (((1,), (1,)), ((), ()))
TN = (((0,), (0,)), ((), ()))


def _params(sem=None):
    return pltpu.CompilerParams(dimension_semantics=sem, vmem_limit_bytes=VMEM_LIMIT)


def _bf(v):
    return v if v.dtype == BF16 else v.astype(BF16)


def _dot(a, b, dims):
    return lax.dot_general(_bf(a), _bf(b), dims, preferred_element_type=F32)


def _dot_hi(a, b):
    return jnp.dot(a, b, precision=lax.Precision.HIGHEST, preferred_element_type=F32)


def _sigmoid(v):
    return 1.0 / (1.0 + jnp.exp(-v))


def _gelu(v):
    return 0.5 * v * (1.0 + lax.erf(v * (1.0 / math.sqrt(2.0))))


def _gelu_grad(v):
    cdf = 0.5 * (1.0 + lax.erf(v * (1.0 / math.sqrt(2.0))))
    return cdf + v * jnp.exp(-0.5 * v * v) * (1.0 / math.sqrt(2.0 * math.pi))


def _segment_mean_matrix(seg):
    r = lax.broadcasted_iota(jnp.int32, (LANES, LANES), 0) // seg
    c = lax.broadcasted_iota(jnp.int32, (LANES, LANES), 1) // seg
    return jnp.where(r == c, 1.0 / seg, 0.0).astype(F32)


def _matmul(name, grid, pairs, extras, outs, dims, epi, *, k_axis=None, acc_shapes=(), sem=None):
    n_pairs, n_ex, n_out, n_acc = len(pairs), len(extras), len(outs), len(acc_shapes)
    pair_acc = [p[4] for p in pairs]
    nk = grid[k_axis] if k_axis is not None else 1

    def body(*refs):
        ab = refs[:2 * n_pairs]
        ex = refs[2 * n_pairs:2 * n_pairs + n_ex]
        out_refs = refs[2 * n_pairs + n_ex:2 * n_pairs + n_ex + n_out]
        acc_refs = refs[2 * n_pairs + n_ex + n_out:]
        pids = tuple(pl.program_id(a) for a in range(len(grid)))
        parts = [None] * n_acc
        for j in range(n_pairs):
            d = _dot(ab[2 * j][...], ab[2 * j + 1][...], dims)
            parts[pair_acc[j]] = d if parts[pair_acc[j]] is None else parts[pair_acc[j]] + d
        if k_axis is None:
            epi(parts, ex, out_refs, pids)
        else:
            k = pids[k_axis]

            @pl.when(k == 0)
            def _():
                for a in range(n_acc):
                    acc_refs[a][...] = parts[a]

            @pl.when(k > 0)
            def _():
                for a in range(n_acc):
                    acc_refs[a][...] += parts[a]

            @pl.when(k == nk - 1)
            def _():
                epi([r[...] for r in acc_refs], ex, out_refs, pids)

    operands, in_specs = [], []
    for a, a_spec, b, b_spec, _ in pairs:
        operands += [a, b]
        in_specs += [a_spec, b_spec]
    for e, e_spec in extras:
        operands.append(e)
        in_specs.append(e_spec)
    scratch = [pltpu.VMEM(s, F32) for s in acc_shapes] if k_axis is not None else []
    res = pl.pallas_call(
        body, grid=grid, in_specs=in_specs, out_specs=[o[1] for o in outs],
        out_shape=[o[0] for o in outs], scratch_shapes=scratch, name=name,
        compiler_params=_params(sem))(*operands)
    return res


def _rms_rows(v, g):
    r = lax.rsqrt(jnp.mean(v * v, axis=-1, keepdims=True) + EPS)
    return v * r * g


def _rms_fwd(name, x, g):
    t, d = x.shape
    tm = 512

    def body(x_ref, g_ref, o_ref):
        o_ref[...] = _rms_rows(x_ref[...], g_ref[...]).astype(BF16)

    return pl.pallas_call(
        body, grid=(t // tm,),
        in_specs=[pl.BlockSpec((tm, d), lambda i: (i, 0)), pl.BlockSpec((1, d), lambda i: (0, 0))],
        out_specs=pl.BlockSpec((tm, d), lambda i: (i, 0)), out_shape=SDS((t, d), BF16), name=name,
        compiler_params=_params())(x, g)


def _epi_residual_norm(accs, ex, outs, pids):
    x_new = accs[0] + ex[0][...]
    outs[0][...] = x_new
    outs[1][...] = _rms_rows(x_new, ex[1][...]).astype(BF16)


def _epi_residual_loss(accs, ex, outs, pids):
    y = accs[0] + ex[0][...]
    err = y - ex[1][...]
    outs[0][...] = err * (1.0 / err.shape[-1])

    @pl.when(pids[0] == 0)
    def _():
        outs[1][...] = jnp.zeros_like(outs[1])

    outs[1][...] += jnp.sum(err * err)


def _epi_rms_bwd(accs, ex, outs, pids):
    dh = accs[0]
    xv, g, res = ex[0][...], ex[1][...], ex[2][...]
    r = lax.rsqrt(jnp.mean(xv * xv, axis=-1, keepdims=True) + EPS)
    xh = xv * r
    dy = dh * g
    outs[0][...] = res + r * (dy - xh * jnp.mean(dy * xh, axis=-1, keepdims=True))

    @pl.when(pids[0] == 0)
    def _():
        outs[1][...] = jnp.zeros_like(outs[1])

    outs[1][...] += jnp.sum(dh * xh, axis=0, keepdims=True)


def _row_spec(tm, d):
    return pl.BlockSpec((tm, d), lambda i, *_: (i, 0))


def _const_spec(shape):
    nd = len(shape)
    return pl.BlockSpec(shape, lambda *_: (0,) * nd)


def _proj_in(name, h, w, layer):
    t, d = h.shape
    n4 = w.shape[-1]
    tm = 512

    def epi(accs, ex, outs, pids):
        outs[0][...] = accs[0].astype(BF16)

    return _matmul(
        name, (N_CHIPS, t // tm),
        [(h, pl.BlockSpec((tm, d), lambda p, i: (i, 0)),
          w, pl.BlockSpec((None, None, d, n4), lambda p, i: (p, layer, 0, 0)), 0)],
        [], [(SDS((t, N_CHIPS * n4), BF16), pl.BlockSpec((tm, n4), lambda p, i: (i, p)))],
        NN, epi, acc_shapes=((tm, n4),))[0]


def _proj_out(name, a, w, x, g_next=None, target=None):
    t, k = a.shape
    d = w.shape[-1]
    tm = 512
    if target is None:
        extras = [(x, _row_spec(tm, d)), (g_next, _const_spec((1, d)))]
        outs = [(SDS((t, d), F32), _row_spec(tm, d)), (SDS((t, d), BF16), _row_spec(tm, d))]
        epi = _epi_residual_norm
    else:
        extras = [(x, _row_spec(tm, d)), (target, _row_spec(tm, d))]
        outs = [(SDS((t, d), F32), _row_spec(tm, d)), (SDS((8, LANES), F32), _const_spec((8, LANES)))]
        epi = _epi_residual_loss
    return _matmul(name, (t // tm,), [(a, _row_spec(tm, k), w, _const_spec((k, d)), 0)], extras, outs, NN, epi,
                   acc_shapes=((tm, d),))


def _ffn_in(name, h, wg, wu, layer):
    t, d = h.shape
    n4 = wg.shape[-1]
    tm = 512

    def epi(accs, ex, outs, pids):
        gate, up = accs
        outs[0][...] = gate.astype(BF16)
        outs[1][...] = up.astype(BF16)
        outs[2][...] = (gate * _sigmoid(gate) * up).astype(BF16)

    w_spec = pl.BlockSpec((None, None, d, n4), lambda p, i: (p, layer, 0, 0))
    h_spec = pl.BlockSpec((tm, d), lambda p, i: (i, 0))
    o = (SDS((N_CHIPS, t, n4), BF16), pl.BlockSpec((None, tm, n4), lambda p, i: (p, i, 0)))
    return _matmul(name, (N_CHIPS, t // tm), [(h, h_spec, wg, w_spec, 0), (h, h_spec, wu, w_spec, 1)], [],
                   [o, o, o], NN, epi, acc_shapes=((tm, n4), (tm, n4)))


def _ffn_out(name, act, wd, layer, x, g_next=None, target=None):
    _, t, n4 = act.shape
    d = wd.shape[-1]
    tm = 1024
    xs = pl.BlockSpec((tm, d), lambda i, k: (i, 0))
    if target is None:
        extras = [(x, xs), (g_next, _const_spec((1, d)))]
        outs = [(SDS((t, d), F32), xs), (SDS((t, d), BF16), xs)]
        epi = _epi_residual_norm
    else:
        extras = [(x, xs), (target, xs)]
        outs = [(SDS((t, d), F32), xs), (SDS((8, LANES), F32), _const_spec((8, LANES)))]
        epi = _epi_residual_loss
    return _matmul(
        name, (t // tm, N_CHIPS),
        [(act, pl.BlockSpec((None, tm, n4), lambda i, k: (k, i, 0)),
          wd, pl.BlockSpec((None, None, n4, d), lambda i, k: (k, layer, 0, 0)), 0)],
        extras, outs, NN, epi, k_axis=1, acc_shapes=((tm, d),))


def _ffn_dact(name, g, wd, layer, gate, up):
    t, d = g.shape
    n4 = wd.shape[-2]
    tm = 512

    def epi(accs, ex, outs, pids):
        dact = accs[0]
        gt = ex[0][...].astype(F32)
        upv = ex[1][...].astype(F32)
        s = _sigmoid(gt)
        outs[0][...] = (dact * upv * (s * (1.0 + gt * (1.0 - s)))).astype(BF16)
        outs[1][...] = (dact * gt * s).astype(BF16)

    blk = pl.BlockSpec((None, tm, n4), lambda p, i: (p, i, 0))
    o = (SDS((N_CHIPS, t, n4), BF16), blk)
    return _matmul(
        name, (N_CHIPS, t // tm),
        [(g, pl.BlockSpec((tm, d), lambda p, i: (i, 0)),
          wd, pl.BlockSpec((None, None, n4, d), lambda p, i: (p, layer, 0, 0)), 0)],
        [(gate, blk), (up, blk)], [o, o], NT, epi, acc_shapes=((tm, n4),))


def _copy_epi(accs, ex, outs, pids):
    for a, o in zip(accs, outs):
        o[...] = a.astype(o.dtype)


def _dgrad_cols(name, dz_list, w_list, layer, three_d, x, g, res):
    t, d = x.shape
    n4 = w_list[0].shape[-1]
    tm = 512
    if three_d:
        zs = pl.BlockSpec((None, tm, n4), lambda i, k: (k, i, 0))
    else:
        zs = pl.BlockSpec((tm, n4), lambda i, k: (i, k))
    ws = pl.BlockSpec((None, None, d, n4), lambda i, k: (k, layer, 0, 0))
    xs = pl.BlockSpec((tm, d), lambda i, k: (i, 0))
    return _matmul(
        name, (t // tm, N_CHIPS), [(dz, zs, w, ws, 0) for dz, w in zip(dz_list, w_list)],
        [(x, xs), (g, _const_spec((1, d))), (res, xs)],
        [(SDS((t, d), F32), xs), (SDS((1, d), F32), _const_spec((1, d)))],
        NT, _epi_rms_bwd, k_axis=1, acc_shapes=((tm, d),))


def _dgrad_rows(name, g, w):
    t, d = g.shape
    k = w.shape[0]
    tm = 512
    return _matmul(name, (t // tm,), [(g, _row_spec(tm, d), w, _const_spec((k, d)), 0)], [],
                   [(SDS((t, k), F32), _row_spec(tm, k))], NT, _copy_epi, acc_shapes=((tm, k),))[0]


A_TILE = 256


def _a_common(p_ref, lg_ref, lb_ref):
    pv = p_ref[...].astype(F32)
    a = _gelu(pv)
    u, v = a[:, :A_WIDTH], a[:, A_WIDTH:]
    vc = v - jnp.mean(v, axis=-1, keepdims=True)
    rs = lax.rsqrt(jnp.mean(vc * vc, axis=-1, keepdims=True) + EPS)
    vhat = vc * rs
    vn = vhat * lg_ref[...] + lb_ref[...]
    return pv, u, vhat, rs, vn.astype(BF16)


def _tril_weights(w_ref, g):
    r = lax.broadcasted_iota(jnp.int32, (CHUNK, CHUNK), 0)
    c = lax.broadcasted_iota(jnp.int32, (CHUNK, CHUNK), 1)
    return jnp.where(c <= r, w_ref[g], 0.0).astype(BF16), c <= r


def _mixer_a_fwd(proj, lg, lb, w, bias_t):
    t = proj.shape[0]

    def body(p_ref, lg_ref, lb_ref, w_ref, bt_ref, o_ref):
        _, u, _, _, vnb = _a_common(p_ref, lg_ref, lb_ref)
        for g in range(A_GROUPS):
            wt, _ = _tril_weights(w_ref, g)
            cs = slice(g * CHUNK, (g + 1) * CHUNK)
            for ch in range(A_TILE // CHUNK):
                rs_ = slice(ch * CHUNK, (ch + 1) * CHUNK)
                mixed = _dot(wt, vnb[rs_, cs], NN) + bt_ref[:, g:g + 1]
                o_ref[rs_, cs] = (u[rs_, cs] * mixed).astype(BF16)

    return pl.pallas_call(
        body, grid=(t // A_TILE,),
        in_specs=[pl.BlockSpec((A_TILE, 2 * A_WIDTH), lambda i: (i, 0)), _const_spec((1, A_WIDTH)),
                  _const_spec((1, A_WIDTH)), _const_spec((A_GROUPS, CHUNK, CHUNK)), _const_spec((CHUNK, A_GROUPS))],
        out_specs=pl.BlockSpec((A_TILE, A_WIDTH), lambda i: (i, 0)), out_shape=SDS((t, A_WIDTH), BF16),
        name="mixer_a_fwd", compiler_params=_params())(proj, lg, lb, w, bias_t)


def _mixer_a_bwd(proj, dcat, lg, lb, w, bias_t):
    t = proj.shape[0]

    def body(p_ref, da_ref, lg_ref, lb_ref, w_ref, bt_ref, dp_ref, dw_ref, dbt_ref, dlg_ref, dlb_ref, du_scr, dvn_scr):
        @pl.when(pl.program_id(0) == 0)
        def _():
            dw_ref[...] = jnp.zeros_like(dw_ref)
            dbt_ref[...] = jnp.zeros_like(dbt_ref)
            dlg_ref[...] = jnp.zeros_like(dlg_ref)
            dlb_ref[...] = jnp.zeros_like(dlb_ref)

        pv, u, vhat, rs, vnb = _a_common(p_ref, lg_ref, lb_ref)
        da = da_ref[...]
        for g in range(A_GROUPS):
            wt, keep = _tril_weights(w_ref, g)
            cs = slice(g * CHUNK, (g + 1) * CHUNK)
            for ch in range(A_TILE // CHUNK):
                rs_ = slice(ch * CHUNK, (ch + 1) * CHUNK)
                vg = vnb[rs_, cs]
                mixed = _dot(wt, vg, NN) + bt_ref[:, g:g + 1]
                du_scr[rs_, cs] = da[rs_, cs] * mixed
                dmx = da[rs_, cs] * u[rs_, cs]
                dw_ref[g] += jnp.where(keep, _dot(dmx, vg, NT), 0.0)
                dvn_scr[rs_, cs] = _dot(wt, dmx, TN)
                dbt_ref[:, g:g + 1] += jnp.sum(dmx, axis=1, keepdims=True)
        dvn = dvn_scr[...]
        dlg_ref[...] += jnp.sum(dvn * vhat, axis=0, keepdims=True)
        dlb_ref[...] += jnp.sum(dvn, axis=0, keepdims=True)
        dvh = dvn * lg_ref[...]
        dv = rs * (dvh - jnp.mean(dvh, axis=-1, keepdims=True) - vhat * jnp.mean(dvh * vhat, axis=-1, keepdims=True))
        gp = _gelu_grad(pv)
        dp_ref[:, :A_WIDTH] = (du_scr[...] * gp[:, :A_WIDTH]).astype(BF16)
        dp_ref[:, A_WIDTH:] = (dv * gp[:, A_WIDTH:]).astype(BF16)

    return pl.pallas_call(
        body, grid=(t // A_TILE,),
        in_specs=[pl.BlockSpec((A_TILE, 2 * A_WIDTH), lambda i: (i, 0)), pl.BlockSpec((A_TILE, A_WIDTH), lambda i: (i, 0)),
                  _const_spec((1, A_WIDTH)), _const_spec((1, A_WIDTH)), _const_spec((A_GROUPS, CHUNK, CHUNK)),
                  _const_spec((CHUNK, A_GROUPS))],
        out_specs=[pl.BlockSpec((A_TILE, 2 * A_WIDTH), lambda i: (i, 0)), _const_spec((A_GROUPS, CHUNK, CHUNK)),
                   _const_spec((CHUNK, A_GROUPS)), _const_spec((1, A_WIDTH)), _const_spec((1, A_WIDTH))],
        out_shape=[SDS((t, 2 * A_WIDTH), BF16), SDS((A_GROUPS, CHUNK, CHUNK), F32), SDS((CHUNK, A_GROUPS), F32),
                   SDS((1, A_WIDTH), F32), SDS((1, A_WIDTH), F32)],
        scratch_shapes=[pltpu.VMEM((A_TILE, A_WIDTH), F32), pltpu.VMEM((A_TILE, A_WIDTH), F32)],
        name="mixer_a_bwd", compiler_params=_params())(proj, dcat, lg, lb, w, bias_t)


def _rope_tables(t):
    half = ROPE_DIM // 2
    inv_freq = ROPE_THETA ** (-jnp.arange(half, dtype=F32) * 2.0 / ROPE_DIM)
    ang = jnp.arange(t, dtype=F32)[:, None] * inv_freq[None, :]
    cos, sin = jnp.cos(ang), jnp.sin(ang)
    one = jnp.ones((t, HEAD_DIM - ROPE_DIM), F32)
    zero = jnp.zeros((t, HEAD_DIM - ROPE_DIM), F32)
    zh = jnp.zeros((t, half), F32)
    c = jnp.concatenate([cos, cos, one], axis=1)
    s1 = jnp.concatenate([-sin, zh, zero], axis=1)
    s2 = jnp.concatenate([zh, sin, zero], axis=1)
    return tuple(jnp.tile(a, (1, LANES // HEAD_DIM)) for a in (c, s1, s2))


QK_TILE = 512
QK_COLS = 2 * N_DIL * B_WIDTH


def _qk_fwd(proj, gains, tabs):
    t = proj.shape[0]
    col0 = 2 * A_WIDTH // 1024

    def body(p_ref, g_ref, c_ref, s1_ref, s2_ref, o_ref):
        seg = _segment_mean_matrix(HEAD_DIM)
        c, s1, s2 = c_ref[...], s1_ref[...], s2_ref[...]
        for ci in range(1024 // LANES):
            ls = slice(ci * LANES, (ci + 1) * LANES)
            xv = p_ref[:, ls].astype(F32)
            r = lax.rsqrt(_dot_hi(xv * xv, seg) + EPS)
            y = xv * r * g_ref[:, ls]
            o_ref[:, ls] = (y * c + pltpu.roll(y, LANES - 8, axis=1) * s1 + pltpu.roll(y, 8, axis=1) * s2).astype(BF16)

    tab = pl.BlockSpec((QK_TILE, LANES), lambda i, j: (i, 0))
    return pl.pallas_call(
        body, grid=(t // QK_TILE, QK_COLS // 1024),
        in_specs=[pl.BlockSpec((QK_TILE, 1024), lambda i, j: (i, col0 + j)), pl.BlockSpec((1, 1024), lambda i, j: (0, j)),
                  tab, tab, tab],
        out_specs=pl.BlockSpec((QK_TILE, 1024), lambda i, j: (i, j)), out_shape=SDS((t, QK_COLS), BF16),
        name="qk_norm_rope_fwd", compiler_params=_params())(proj, gains, *tabs)


PERM_TILE = 512


def _permute(name, items, rate):
    t = items[0][0].shape[0]
    n = len(items)
    rows = PERM_TILE // rate

    def body(*refs):
        scr = refs[-1]
        for x_ref, o_ref in zip(refs[:n], refs[n:2 * n]):
            for ci in range(B_WIDTH // LANES):
                scr[ci] = x_ref[:, ci * LANES:(ci + 1) * LANES].astype(F32)
            for rho in range(rate):
                for ci in range(B_WIDTH // LANES):
                    o_ref[rho, :, ci * LANES:(ci + 1) * LANES] = scr[ci, pl.ds(rho, rows, stride=rate), :].astype(o_ref.dtype)

    return pl.pallas_call(
        body, grid=(t // PERM_TILE,),
        in_specs=[pl.BlockSpec((PERM_TILE, B_WIDTH), functools.partial(lambda cb, i: (i, cb), cb)) for _, cb in items],
        out_specs=[pl.BlockSpec((rate, rows, B_WIDTH), lambda i: (0, i, 0)) for _ in items],
        out_shape=[SDS((rate, t // rate, B_WIDTH), a.dtype) for a, _ in items],
        scratch_shapes=[pltpu.VMEM((B_WIDTH // LANES, PERM_TILE, LANES), F32)],
        name=name, compiler_params=_params())(*[a for a, _ in items])


def _unpermute(name, arrays, rate):
    t = arrays[0].shape[1] * rate
    n = len(arrays)
    rows = PERM_TILE // rate

    def body(*refs):
        scr = refs[-1]
        for x_ref, o_ref in zip(refs[:n], refs[n:2 * n]):
            for rho in range(rate):
                for ci in range(B_WIDTH // LANES):
                    scr[ci, pl.ds(rho, rows, stride=rate), :] = x_ref[rho, :, ci * LANES:(ci + 1) * LANES].astype(F32)
            for ci in range(B_WIDTH // LANES):
                o_ref[:, ci * LANES:(ci + 1) * LANES] = scr[ci].astype(o_ref.dtype)

    return pl.pallas_call(
        body, grid=(t // PERM_TILE,),
        in_specs=[pl.BlockSpec((rate, rows, B_WIDTH), lambda i: (0, i, 0)) for _ in arrays],
        out_specs=[pl.BlockSpec((PERM_TILE, B_WIDTH), lambda i: (i, 0)) for _ in arrays],
        out_shape=[SDS((t, B_WIDTH), a.dtype) for a in arrays],
        scratch_shapes=[pltpu.VMEM((B_WIDTH // LANES, PERM_TILE, LANES), F32)],
        name=name, compiler_params=_params())(*arrays)


def _head_lane_mask(h):
    lane = lax.broadcasted_iota(jnp.int32, (1, LANES), 1)
    return (lane < HEAD_DIM) if h == 0 else (lane >= HEAD_DIM)


def _attn_fwd(name, q, k, v):
    rate, length = q[0].shape[0], q[0].shape[1]
    nb = length // ATT_BLOCK
    scale = HEAD_DIM ** -0.5

    def body(q_ref, kc_ref, kp_ref, vc_ref, vp_ref, o_ref, l_ref):
        n = pl.program_id(1)
        qi = lax.broadcasted_iota(jnp.int32, (ATT_BLOCK, 2 * ATT_BLOCK), 0)
        cj = lax.broadcasted_iota(jnp.int32, (ATT_BLOCK, 2 * ATT_BLOCK), 1)
        has_prev = jnp.where(n > 0, 0, 2 * ATT_BLOCK)
        mask = ((cj < ATT_BLOCK) & (cj >= qi + has_prev)) | ((cj >= ATT_BLOCK) & (cj - ATT_BLOCK <= qi))
        for hp in range(B_WIDTH // LANES):
            ls = slice(hp * LANES, (hp + 1) * LANES)
            q2 = q_ref[:, ls]
            k2 = jnp.concatenate([kp_ref[:, ls], kc_ref[:, ls]], axis=0)
            v2 = jnp.concatenate([vp_ref[:, ls], vc_ref[:, ls]], axis=0)
            o_acc, lse2 = None, None
            for h in range(2):
                hm = _head_lane_mask(h)
                s = _dot(jnp.where(hm, q2, jnp.zeros_like(q2)), k2, NT) * scale
                s = jnp.where(mask, s, NEG_INF)
                m = jnp.max(s, axis=1, keepdims=True)
                p = jnp.exp(s - m)
                den = jnp.sum(p, axis=1, keepdims=True)
                lse = m + jnp.log(den)
                o = _dot(p / den, jnp.where(hm, v2, jnp.zeros_like(v2)), NN)
                o_acc = o if h == 0 else o_acc + o
                lse_b = lse + jnp.zeros((ATT_BLOCK, LANES), F32)
                lse2 = lse_b if h == 0 else jnp.where(hm, lse_b, lse2)
            o_ref[:, ls] = o_acc
            l_ref[:, ls] = lse2

    def cur(cb):
        return pl.BlockSpec((None, ATT_BLOCK, B_WIDTH), lambda r, n: (r, n, cb))

    def prev(cb):
        return pl.BlockSpec((None, ATT_BLOCK, B_WIDTH), lambda r, n: (r, jnp.maximum(n - 1, 0), cb))

    out = pl.BlockSpec((None, ATT_BLOCK, B_WIDTH), lambda r, n: (r, n, 0))
    return pl.pallas_call(
        body, grid=(rate, nb),
        in_specs=[cur(q[1]), cur(k[1]), prev(k[1]), cur(v[1]), prev(v[1])],
        out_specs=[out, out], out_shape=[SDS((rate, length, B_WIDTH), F32)] * 2,
        name=name, compiler_params=_params())(q[0], k[0], k[0], v[0], v[0])


def _attn_merge(a_out, o_list, l_list):
    t = a_out.shape[0]
    tm = 512

    def body(a_ref, o0, o1, o2, l0, l1, l2, cat_ref, lt_ref):
        ls = [l0[...], l1[...], l2[...]]
        m = jnp.maximum(jnp.maximum(ls[0], ls[1]), ls[2])
        es = [jnp.exp(l - m) for l in ls]
        den = es[0] + es[1] + es[2]
        b = (es[0] * o0[...] + es[1] * o1[...] + es[2] * o2[...]) / den
        cat_ref[:, :A_WIDTH] = a_ref[...]
        cat_ref[:, A_WIDTH:] = b.astype(BF16)
        lt_ref[...] = m + jnp.log(den)

    blk = _row_spec(tm, B_WIDTH)
    return pl.pallas_call(
        body, grid=(t // tm,), in_specs=[blk] * 7,
        out_specs=[_row_spec(tm, A_WIDTH + B_WIDTH), blk],
        out_shape=[SDS((t, A_WIDTH + B_WIDTH), BF16), SDS((t, B_WIDTH), F32)],
        name="attn_merge", compiler_params=_params())(a_out, *o_list, *l_list)


def _attn_bwd_prep(dcat, cat):
    t = dcat.shape[0]
    tm = 512

    def body(d_ref, b_ref, db_ref, dd_ref):
        seg = _segment_mean_matrix(HEAD_DIM) * float(HEAD_DIM)
        for ci in range(B_WIDTH // LANES):
            ls = slice(ci * LANES, (ci + 1) * LANES)
            d = d_ref[:, ls]
            db_ref[:, ls] = d.astype(BF16)
            dd_ref[:, ls] = _dot_hi(d * b_ref[:, ls].astype(F32), seg)

    right = pl.BlockSpec((tm, B_WIDTH), lambda i: (i, 1))
    blk = _row_spec(tm, B_WIDTH)
    return pl.pallas_call(
        body, grid=(t // tm,), in_specs=[right, right], out_specs=[blk, blk],
        out_shape=[SDS((t, B_WIDTH), BF16), SDS((t, B_WIDTH), F32)],
        name="attn_bwd_prep", compiler_params=_params())(dcat, cat)


def _attn_bwd(name, q, k, v, db, lse, dd):
    rate, length = db.shape[0], db.shape[1]
    nb = length // ATT_BLOCK
    scale = HEAD_DIM ** -0.5

    def body(qa_ref, qb_ref, k_ref, v_ref, dba_ref, dbb_ref, la_ref, lb_ref, da_ref, dbd_ref, dq_ref, dk_ref, dv_ref, carry):
        m = pl.program_id(1)

        @pl.when(m == 0)
        def _():
            carry[...] = jnp.zeros_like(carry)

        qi = lax.broadcasted_iota(jnp.int32, (ATT_BLOCK, ATT_BLOCK), 0)
        kj = lax.broadcasted_iota(jnp.int32, (ATT_BLOCK, ATT_BLOCK), 1)
        masks = (kj <= qi, (kj >= qi) & (m + 1 < nb))
        for hp in range(B_WIDTH // LANES):
            ls = slice(hp * LANES, (hp + 1) * LANES)
            k2, v2 = k_ref[:, ls], v_ref[:, ls]
            sides = ((qa_ref[:, ls], dba_ref[:, ls], la_ref[:, ls], da_ref[:, ls]),
                     (qb_ref[:, ls], dbb_ref[:, ls], lb_ref[:, ls], dbd_ref[:, ls]))
            dq = [None, None]
            dk_acc, dv_acc = None, None
            for h in range(2):
                hm = _head_lane_mask(h)
                km = jnp.where(hm, k2, jnp.zeros_like(k2))
                vm = jnp.where(hm, v2, jnp.zeros_like(v2))
                for side in range(2):
                    q2, db2, lse2, dd2 = sides[side]
                    lse_col = jnp.max(jnp.where(hm, lse2, NEG_INF), axis=1, keepdims=True)
                    dd_col = jnp.max(jnp.where(hm, dd2, NEG_INF), axis=1, keepdims=True)
                    s = _dot(q2, km, NT) * scale
                    p = jnp.where(masks[side], jnp.exp(s - lse_col), 0.0)
                    dvc = _dot(p, jnp.where(hm, db2, jnp.zeros_like(db2)), TN)
                    dp = _dot(db2, vm, NT)
                    ds = (p * (dp - dd_col) * scale).astype(BF16)
                    dqc = _dot(ds, km, NN)
                    dkc = _dot(ds, jnp.where(hm, q2, jnp.zeros_like(q2)), TN)
                    dq[side] = dqc if dq[side] is None else dq[side] + dqc
                    dk_acc = dkc if dk_acc is None else dk_acc + dkc
                    dv_acc = dvc if dv_acc is None else dv_acc + dvc
            dq_ref[:, ls] = (dq[0] + carry[:, ls]).astype(BF16)
            carry[:, ls] = dq[1]
            dk_ref[:, ls] = dk_acc.astype(BF16)
            dv_ref[:, ls] = dv_acc.astype(BF16)

    def cur(cb):
        return pl.BlockSpec((None, ATT_BLOCK, B_WIDTH), lambda r, n: (r, n, cb))

    def nxt(cb):
        return pl.BlockSpec((None, ATT_BLOCK, B_WIDTH), lambda r, n: (r, jnp.minimum(n + 1, nb - 1), cb))

    out = cur(0)
    return pl.pallas_call(
        body, grid=(rate, nb),
        in_specs=[cur(q[1]), nxt(q[1]), cur(k[1]), cur(v[1]), cur(0), nxt(0), cur(0), nxt(0), cur(0), nxt(0)],
        out_specs=[out, out, out], out_shape=[SDS((rate, length, B_WIDTH), BF16)] * 3,
        scratch_shapes=[pltpu.VMEM((ATT_BLOCK, B_WIDTH), F32)],
        name=name, compiler_params=_params())(q[0], q[0], k[0], v[0], db, db, lse, lse, dd, dd)


AB_IN = 2 * A_WIDTH + 3 * N_DIL * B_WIDTH
ASM_TILE = 256


def _dproj_assemble(proj, d_a, dqk, dv, gains, tabs):
    t = proj.shape[0]
    n_qk = 2 * N_DIL

    def body(p_ref, da_ref, *rest):
        dqk_refs = rest[:n_qk]
        dv_refs = rest[n_qk:n_qk + N_DIL]
        g_ref, c_ref, s1_ref, s2_ref, o_ref, dg_ref = rest[n_qk + N_DIL:]

        @pl.when(pl.program_id(0) == 0)
        def _():
            dg_ref[...] = jnp.zeros_like(dg_ref)

        seg = _segment_mean_matrix(HEAD_DIM)
        c, s1, s2 = c_ref[...], s1_ref[...], s2_ref[...]
        o_ref[:, :2 * A_WIDTH] = da_ref[...]
        for jg in range(n_qk):
            for ci in range(B_WIDTH // LANES):
                col = jg * B_WIDTH + ci * LANES
                src = slice(2 * A_WIDTH + col, 2 * A_WIDTH + col + LANES)
                xv = p_ref[:, src].astype(F32)
                r = lax.rsqrt(_dot_hi(xv * xv, seg) + EPS)
                xh = xv * r
                gain = g_ref[:, col:col + LANES]
                do = dqk_refs[jg][:, ci * LANES:(ci + 1) * LANES].astype(F32)
                dy = do * c + pltpu.roll(do * s1, 8, axis=1) + pltpu.roll(do * s2, LANES - 8, axis=1)
                dg_ref[:, col:col + LANES] += jnp.sum(dy * xh, axis=0, keepdims=True)
                dxh = dy * gain
                o_ref[:, src] = (r * (dxh - xh * _dot_hi(dxh * xh, seg))).astype(BF16)
        v0 = 2 * A_WIDTH + QK_COLS
        for g in range(N_DIL):
            o_ref[:, v0 + g * B_WIDTH:v0 + (g + 1) * B_WIDTH] = dv_refs[g][...]

    blk = _row_spec(ASM_TILE, B_WIDTH)
    tab = _row_spec(ASM_TILE, LANES)
    return pl.pallas_call(
        body, grid=(t // ASM_TILE,),
        in_specs=[_row_spec(ASM_TILE, AB_IN), _row_spec(ASM_TILE, 2 * A_WIDTH)] + [blk] * (n_qk + N_DIL)
        + [_const_spec((1, QK_COLS)), tab, tab, tab],
        out_specs=[_row_spec(ASM_TILE, AB_IN), _const_spec((1, QK_COLS))],
        out_shape=[SDS((t, AB_IN), BF16), SDS((1, QK_COLS), F32)],
        name="dproj_assemble", compiler_params=_params())(proj, d_a, *dqk, *dv, gains, *tabs)


def _fold_heads(dg_lane):
    n = dg_lane.shape[1]

    def body(x_ref, o_ref):
        r = lax.broadcasted_iota(jnp.int32, (B_WIDTH, B_WIDTH), 0) % HEAD_DIM
        c = lax.broadcasted_iota(jnp.int32, (B_WIDTH, B_WIDTH), 1) % HEAD_DIM
        fold = jnp.where(r == c, 1.0, 0.0).astype(F32)
        for jg in range(n // B_WIDTH):
            ls = slice(jg * B_WIDTH, (jg + 1) * B_WIDTH)
            o_ref[:, ls] = _dot_hi(jnp.broadcast_to(x_ref[:, ls], (8, B_WIDTH)), fold)

    return pl.pallas_call(body, out_shape=SDS((8, n), F32), name="fold_heads", compiler_params=_params())(dg_lane)


CD_TILE = 256
CD_IN = 2 * C_WIDTH + 3 * 512


def _cd_split(pv):
    w = C_WIDTH
    return pv[:, :w], pv[:, w:2 * w], pv[:, 2 * w:3 * w], pv[:, 3 * w:4 * w], pv[:, 4 * w:5 * w]


def _mixer_cd_fwd(proj, cw, cb, lg, lb, dw):
    t = proj.shape[0]
    per = CD_TILE // HALO

    def body(h_ref, m_ref, cw_ref, cb_ref, lg_ref, lb_ref, dw_ref, o_ref, c_scr, e_scr):
        not_first = (pl.program_id(0) > 0).astype(F32)
        ha, hg, _, hgc, hhv = _cd_split(h_ref[...].astype(F32))
        ma, mg, mgb, mgc, mhv = _cd_split(m_ref[...].astype(F32))
        c_scr[:HALO] = ha * _sigmoid(hg) * not_first
        c_scr[HALO:] = ma * _sigmoid(mg)
        e_scr[:HALO] = hgc * hhv * not_first
        e_scr[HALO:] = mgc * mhv
        acc = jnp.zeros((CD_TILE, C_WIDTH), F32)
        for k in range(C_KERNEL):
            acc = acc + cw_ref[k:k + 1, :] * c_scr[pl.ds(HALO - (C_KERNEL - 1) + k, CD_TILE), :]
        c1 = acc + cb_ref[...]
        cc = c1 - jnp.mean(c1, axis=-1, keepdims=True)
        c2 = cc * lax.rsqrt(jnp.mean(cc * cc, axis=-1, keepdims=True) + EPS) * lg_ref[...] + lb_ref[...]
        o_ref[:, :C_WIDTH] = (c2 * _sigmoid(c2)).astype(BF16)
        d1 = jnp.zeros((CD_TILE, C_WIDTH), F32)
        for k in range(D_KERNEL):
            d1 = d1 + dw_ref[k:k + 1, :] * e_scr[pl.ds(HALO - (D_KERNEL - 1) + k, CD_TILE), :]
        o_ref[:, C_WIDTH:] = (mgb * d1).astype(BF16)

    return pl.pallas_call(
        body, grid=(t // CD_TILE,),
        in_specs=[pl.BlockSpec((HALO, CD_IN), lambda i: (jnp.maximum(i * per - 1, 0), 0)), _row_spec(CD_TILE, CD_IN),
                  _const_spec((32, C_WIDTH)), _const_spec((1, C_WIDTH)), _const_spec((1, C_WIDTH)), _const_spec((1, C_WIDTH)),
                  _const_spec((8, C_WIDTH))],
        out_specs=_row_spec(CD_TILE, 2 * C_WIDTH), out_shape=SDS((t, 2 * C_WIDTH), BF16),
        scratch_shapes=[pltpu.VMEM((HALO + CD_TILE, C_WIDTH), F32)] * 2,
        name="mixer_cd_fwd", compiler_params=_params())(proj, proj, cw, cb, lg, lb, dw)


def _mixer_cd_bwd(proj, dcat, cw, cb, lg, lb, dw):
    t = proj.shape[0]
    per = CD_TILE // HALO
    nt = t // CD_TILE
    ext = CD_TILE + HALO

    def body(hp_ref, m_ref, hn_ref, dm_ref, dn_ref, cw_ref, cb_ref, lg_ref, lb_ref, dw_ref,
             dp_ref, dcw_ref, dcb_ref, dlg_ref, dlb_ref, ddw_ref, c_scr, e_scr, dc1_scr, dd1_scr):
        i = pl.program_id(0)

        @pl.when(i == 0)
        def _():
            for r in (dcw_ref, dcb_ref, dlg_ref, dlb_ref, ddw_ref):
                r[...] = jnp.zeros_like(r)

        not_first = (i > 0).astype(F32)
        not_last = (i < nt - 1).astype(F32)
        pa, pg, _, pgc, phv = _cd_split(hp_ref[...].astype(F32))
        ma, mg, mgb, mgc, mhv = _cd_split(m_ref[...].astype(F32))
        na, ng, ngb, ngc, nhv = _cd_split(hn_ref[...].astype(F32))
        sig_m = _sigmoid(mg)
        c_scr[:HALO] = pa * _sigmoid(pg) * not_first
        c_scr[HALO:HALO + CD_TILE] = ma * sig_m
        c_scr[HALO + CD_TILE:] = na * _sigmoid(ng) * not_last
        e_scr[:HALO] = pgc * phv * not_first
        e_scr[HALO:HALO + CD_TILE] = mgc * mhv
        e_scr[HALO + CD_TILE:] = ngc * nhv * not_last

        acc = jnp.zeros((ext, C_WIDTH), F32)
        for k in range(C_KERNEL):
            acc = acc + cw_ref[k:k + 1, :] * c_scr[pl.ds(HALO - (C_KERNEL - 1) + k, ext), :]
        c1 = acc + cb_ref[...]
        cc = c1 - jnp.mean(c1, axis=-1, keepdims=True)
        rs = lax.rsqrt(jnp.mean(cc * cc, axis=-1, keepdims=True) + EPS)
        vhat = cc * rs
        c2 = vhat * lg_ref[...] + lb_ref[...]
        sig = _sigmoid(c2)
        dc = jnp.concatenate([dm_ref[:, :C_WIDTH], dn_ref[:, :C_WIDTH] * not_last], axis=0)
        dc2 = dc * (sig * (1.0 + c2 * (1.0 - sig)))
        dvh = dc2 * lg_ref[...]
        dc1 = rs * (dvh - jnp.mean(dvh, axis=-1, keepdims=True) - vhat * jnp.mean(dvh * vhat, axis=-1, keepdims=True))
        dc1_scr[...] = dc1
        dlg_ref[...] += jnp.sum((dc2 * vhat)[:CD_TILE], axis=0, keepdims=True)
        dlb_ref[...] += jnp.sum(dc2[:CD_TILE], axis=0, keepdims=True)
        dc1_m = dc1[:CD_TILE]
        dcb_ref[...] += jnp.sum(dc1_m, axis=0, keepdims=True)
        dc0 = jnp.zeros((CD_TILE, C_WIDTH), F32)
        for k in range(C_KERNEL):
            dc0 = dc0 + cw_ref[k:k + 1, :] * dc1_scr[pl.ds(C_KERNEL - 1 - k, CD_TILE), :]
            dcw_ref[k:k + 1, :] += jnp.sum(dc1_m * c_scr[pl.ds(HALO - (C_KERNEL - 1) + k, CD_TILE), :], axis=0, keepdims=True)
        dp_ref[:, :C_WIDTH] = (dc0 * sig_m).astype(BF16)
        dp_ref[:, C_WIDTH:2 * C_WIDTH] = (dc0 * ma * sig_m * (1.0 - sig_m)).astype(BF16)

        d1 = jnp.zeros((CD_TILE, C_WIDTH), F32)
        for k in range(D_KERNEL):
            d1 = d1 + dw_ref[k:k + 1, :] * e_scr[pl.ds(HALO - (D_KERNEL - 1) + k, CD_TILE), :]
        dd_m = dm_ref[:, C_WIDTH:]
        dd1 = jnp.concatenate([dd_m * mgb, dn_ref[:, C_WIDTH:] * ngb * not_last], axis=0)
        dd1_scr[...] = dd1
        dp_ref[:, 2 * C_WIDTH:3 * C_WIDTH] = (dd_m * d1).astype(BF16)
        de = jnp.zeros((CD_TILE, C_WIDTH), F32)
        for k in range(D_KERNEL):
            de = de + dw_ref[k:k + 1, :] * dd1_scr[pl.ds(D_KERNEL - 1 - k, CD_TILE), :]
            ddw_ref[k:k + 1, :] += jnp.sum(dd1[:CD_TILE] * e_scr[pl.ds(HALO - (D_KERNEL - 1) + k, CD_TILE), :], axis=0, keepdims=True)
        dp_ref[:, 3 * C_WIDTH:4 * C_WIDTH] = (de * mhv).astype(BF16)
        dp_ref[:, 4 * C_WIDTH:] = (de * mgc).astype(BF16)

    halo_prev = lambda i: (jnp.maximum(i * per - 1, 0), 0)
    halo_next = lambda i: (jnp.minimum((i + 1) * per, t // HALO - 1), 0)
    vec = _const_spec((1, C_WIDTH))
    return pl.pallas_call(
        body, grid=(nt,),
        in_specs=[pl.BlockSpec((HALO, CD_IN), halo_prev), _row_spec(CD_TILE, CD_IN), pl.BlockSpec((HALO, CD_IN), halo_next),
                  _row_spec(CD_TILE, 2 * C_WIDTH), pl.BlockSpec((HALO, 2 * C_WIDTH), halo_next),
                  _const_spec((32, C_WIDTH)), vec, vec, vec, _const_spec((8, C_WIDTH))],
        out_specs=[_row_spec(CD_TILE, CD_IN), _const_spec((32, C_WIDTH)), vec, vec, vec, _const_spec((8, C_WIDTH))],
        out_shape=[SDS((t, CD_IN), BF16), SDS((32, C_WIDTH), F32), SDS((1, C_WIDTH), F32), SDS((1, C_WIDTH), F32),
                   SDS((1, C_WIDTH), F32), SDS((8, C_WIDTH), F32)],
        scratch_shapes=[pltpu.VMEM((2 * HALO + CD_TILE, C_WIDTH), F32)] * 2 + [pltpu.VMEM((ext, C_WIDTH), F32)] * 2,
        name="mixer_cd_bwd", compiler_params=_params())(proj, proj, proj, dcat, dcat, cw, cb, lg, lb, dw)


def _wgrad(name, lhs, lhs_spec, rhs_list, rhs_spec, out_rc, t, layers, layer, bufs):
    tk = 512
    r, c = out_rc
    n = len(rhs_list)
    o_spec = pl.BlockSpec((None, None, r, c), lambda p, k: (p, layer, 0, 0))
    o = (SDS((N_CHIPS, layers, r, c), F32), o_spec)
    pairs = [(lhs, lhs_spec, rhs, rhs_spec, j) for j, rhs in enumerate(rhs_list)]
    n_pairs = len(pairs)

    def body(*refs):
        ab = refs[:2 * n_pairs]
        skip = 0 if bufs is None else n
        out_refs = refs[2 * n_pairs + skip:2 * n_pairs + skip + n]
        acc_refs = refs[2 * n_pairs + skip + n:]
        k = pl.program_id(1)
        parts = [_dot(ab[2 * j][...], ab[2 * j + 1][...], TN) for j in range(n_pairs)]

        @pl.when(k == 0)
        def _():
            for a in range(n):
                acc_refs[a][...] = parts[a]

        @pl.when(k > 0)
        def _():
            for a in range(n):
                acc_refs[a][...] += parts[a]

        @pl.when(k == t // tk - 1)
        def _():
            for a in range(n):
                out_refs[a][...] = acc_refs[a][...]

    operands, in_specs = [], []
    for a, a_spec, b, b_spec, _ in pairs:
        operands += [a, b]
        in_specs += [a_spec, b_spec]
    aliases = {}
    if bufs is not None:
        for j, buf in enumerate(bufs):
            aliases[len(operands)] = j
            operands.append(buf)
            in_specs.append(pl.BlockSpec(memory_space=pl.ANY))
    return pl.pallas_call(
        body, grid=(N_CHIPS, t // tk), in_specs=in_specs, out_specs=[o[1]] * n, out_shape=[o[0]] * n,
        scratch_shapes=[pltpu.VMEM((r, c), F32)] * n, input_output_aliases=aliases, name=name,
        compiler_params=_params())(*operands)


def _wgrad_col_sharded(name, h, dz_list, three_d, layers=1, layer=0, bufs=None):
    t, d = h.shape
    tk = 512
    n4 = dz_list[0].shape[-1] if three_d else dz_list[0].shape[-1] // N_CHIPS
    hs = pl.BlockSpec((tk, d), lambda p, k: (k, 0))
    zs = pl.BlockSpec((None, tk, n4), lambda p, k: (p, k, 0)) if three_d else pl.BlockSpec((tk, n4), lambda p, k: (k, p))
    return _wgrad(name, h, hs, dz_list, zs, (d, n4), t, layers, layer, bufs)


def _wgrad_row_sharded(name, a, g, three_d, layers=1, layer=0, buf=None):
    t, d = g.shape
    tk = 512
    k4 = a.shape[-1] if three_d else a.shape[-1] // N_CHIPS
    a_spec = pl.BlockSpec((None, tk, k4), lambda p, k: (p, k, 0)) if three_d else pl.BlockSpec((tk, k4), lambda p, k: (k, p))
    gs = pl.BlockSpec((tk, d), lambda p, k: (k, 0))
    return _wgrad(name, a, a_spec, [g], gs, (k4, d), t, layers, layer, None if buf is None else [buf])[0]


MESH = pl.DeviceIdType.MESH
ANY = pl.BlockSpec(memory_space=pl.ANY)


def _position():
    x, y, c = lax.axis_index("x"), lax.axis_index("y"), lax.axis_index("c")
    others = [(1 - x, y), (x, 1 - y), (1 - x, 1 - y)]
    return x, y, c, 2 * x + y, others


def _mesh_scalars():
    return jnp.stack([lax.axis_index("c"), 2 * lax.axis_index("x") + lax.axis_index("y")]).astype(jnp.int32)


def _stage_own(name, shard, dtype):
    _, h, cols = shard.shape

    def body(s_ref, x_ref, o_ref):
        o_ref[...] = x_ref[...].astype(dtype)

    return pl.pallas_call(
        body,
        grid_spec=pltpu.PrefetchScalarGridSpec(
            num_scalar_prefetch=1, grid=(2,),
            in_specs=[pl.BlockSpec((None, h, cols), lambda i, s: (i, 0, 0))],
            out_specs=pl.BlockSpec((None, None, h, cols), lambda i, s: (s[1], i, 0, 0))),
        out_shape=SDS((N_CHIPS, 2, h, cols), dtype), name=name, compiler_params=_params())(_mesh_scalars(), shard)


def _gather_weights(bufs):
    n = len(bufs)

    def body(*refs):
        outs = refs[n:2 * n]
        send_sems, recv_sems = refs[2 * n:]
        x, y, c, p, others = _position()
        sibling = (x, y, 1 - c)
        sends = []
        for t in range(n):
            for j, (qx, qy) in enumerate(others):
                mine = outs[t].at[p, c]
                cp = pltpu.make_async_remote_copy(mine, mine, send_sems.at[t, j], recv_sems.at[t, j],
                                                  device_id=(qx, qy, c), device_id_type=MESH)
                cp.start()
                sends.append(cp)
        for t in range(n):
            for j, (qx, qy) in enumerate(others):
                landed = outs[t].at[2 * qx + qy, c]
                pltpu.make_async_remote_copy(landed, landed, send_sems.at[t, j], recv_sems.at[t, j],
                                             device_id=(qx, qy, c), device_id_type=MESH).wait_recv()
                cp = pltpu.make_async_remote_copy(landed, landed, send_sems.at[t, 3 + j], recv_sems.at[t, 3 + j],
                                                  device_id=sibling, device_id_type=MESH)
                cp.start()
                sends.append(cp)
        for t in range(n):
            for j, (qx, qy) in enumerate(others):
                passed = outs[t].at[2 * qx + qy, 1 - c]
                pltpu.make_async_remote_copy(passed, passed, send_sems.at[t, 3 + j], recv_sems.at[t, 3 + j],
                                             device_id=sibling, device_id_type=MESH).wait_recv()
        for cp in sends:
            cp.wait_send()

    return pl.pallas_call(
        body, in_specs=[ANY] * n, out_specs=[ANY] * n, out_shape=[SDS(s.shape, s.dtype) for s in bufs],
        input_output_aliases={t: t for t in range(n)},
        scratch_shapes=[pltpu.SemaphoreType.DMA((n, 6)), pltpu.SemaphoreType.DMA((n, 6))],
        name="gather_weights")(*bufs)


def _swap_halves(tensors):
    n = len(tensors)

    def body(*refs):
        ins, outs = refs[:n], refs[n:2 * n]
        send_sems, recv_sems = refs[2 * n:]
        x, y, c, _, _ = _position()
        copies = []
        for t in range(n):
            cp = pltpu.make_async_remote_copy(ins[t].at[:, 1 - c], outs[t], send_sems.at[t], recv_sems.at[t],
                                              device_id=(x, y, 1 - c), device_id_type=MESH)
            cp.start()
            copies.append(cp)
        for cp in copies:
            cp.wait()

    return pl.pallas_call(
        body, in_specs=[ANY] * n, out_specs=[ANY] * n,
        out_shape=[SDS((s.shape[0],) + s.shape[2:], s.dtype) for s in tensors],
        scratch_shapes=[pltpu.SemaphoreType.DMA((n,)), pltpu.SemaphoreType.DMA((n,))],
        name="swap_halves")(*tensors)


def _scatter_chips(tensors, landing):
    n = len(tensors)

    def body(*refs):
        ins, outs = refs[:n], refs[2 * n:3 * n]
        send_sems, recv_sems = refs[3 * n:]
        x, y, c, p, others = _position()
        copies = []
        for t in range(n):
            for j, (qx, qy) in enumerate(others):
                cp = pltpu.make_async_remote_copy(ins[t].at[2 * qx + qy], outs[t].at[p], send_sems.at[t, j], recv_sems.at[t, j],
                                                  device_id=(qx, qy, c), device_id_type=MESH)
                cp.start()
                copies.append((cp, t, j, 2 * qx + qy))
        for cp, t, j, q in copies:
            cp.wait_send()
            pltpu.make_async_remote_copy(outs[t].at[q], outs[t].at[q], send_sems.at[t, j], recv_sems.at[t, j],
                                         device_id=(x, y, c), device_id_type=MESH).wait_recv()

    return pl.pallas_call(
        body, in_specs=[ANY] * (2 * n), out_specs=[ANY] * n, out_shape=[SDS(s.shape, s.dtype) for s in landing],
        input_output_aliases={n + t: t for t in range(n)},
        scratch_shapes=[pltpu.SemaphoreType.DMA((n, 3)), pltpu.SemaphoreType.DMA((n, 3))],
        name="scatter_chips")(*tensors, *landing)


def _join_halves(bufs):
    n = len(bufs)

    def body(*refs):
        outs = refs[n:2 * n]
        send_sems, recv_sems = refs[2 * n:]
        x, y, c, _, _ = _position()
        copies = []
        for t in range(n):
            cp = pltpu.make_async_remote_copy(outs[t].at[c], outs[t].at[c], send_sems.at[t], recv_sems.at[t],
                                              device_id=(x, y, 1 - c), device_id_type=MESH)
            cp.start()
            copies.append((cp, t))
        for cp, t in copies:
            cp.wait_send()
            theirs = outs[t].at[1 - c]
            pltpu.make_async_remote_copy(theirs, theirs, send_sems.at[t], recv_sems.at[t],
                                         device_id=(x, y, 1 - c), device_id_type=MESH).wait_recv()

    return pl.pallas_call(
        body, in_specs=[ANY] * n, out_specs=[ANY] * n, out_shape=[SDS(s.shape, s.dtype) for s in bufs],
        input_output_aliases={t: t for t in range(n)},
        scratch_shapes=[pltpu.SemaphoreType.DMA((n,)), pltpu.SemaphoreType.DMA((n,))],
        name="join_halves")(*bufs)


def _add_own_half(name, full, recv, out_dtype):
    n4, _, h, cols = full.shape

    def body(s_ref, a_ref, b_ref, o_ref, own_ref):
        v = (a_ref[...] + b_ref[...]).astype(out_dtype)
        o_ref[...] = v

        @pl.when(pl.program_id(0) == s_ref[1])
        def _():
            own_ref[...] = v

    return pl.pallas_call(
        body,
        grid_spec=pltpu.PrefetchScalarGridSpec(
            num_scalar_prefetch=1, grid=(n4,),
            in_specs=[pl.BlockSpec((None, None, h, cols), lambda q, s: (q, s[0], 0, 0)),
                      pl.BlockSpec((None, h, cols), lambda q, s: (q, 0, 0))],
            out_specs=[pl.BlockSpec((None, h, cols), lambda q, s: (q, 0, 0)),
                       pl.BlockSpec((None, h, cols), lambda q, s: (s[1], 0, 0))]),
        out_shape=[SDS((n4, h, cols), out_dtype)] * 2, name=name, compiler_params=_params())(_mesh_scalars(), full, recv)


def _sum_chips(name, parts):
    n4, h, cols = parts.shape
    th = h // 4 if h % 64 == 0 else h

    def body(s_ref, a_ref, o_ref):
        acc = a_ref[0].astype(F32)
        for q in range(1, n4):
            acc = acc + a_ref[q].astype(F32)
        o_ref[...] = acc

    return pl.pallas_call(
        body,
        grid_spec=pltpu.PrefetchScalarGridSpec(
            num_scalar_prefetch=1, grid=(h // th,),
            in_specs=[pl.BlockSpec((n4, th, cols), lambda i, s: (0, i, 0))],
            out_specs=pl.BlockSpec((None, th, cols), lambda i, s: (s[0], i, 0))),
        out_shape=SDS((2, h, cols), F32), name=name, compiler_params=_params())(_mesh_scalars(), parts)


def _adamw_math(w, g, m, v):
    m2 = ADAM_B1 * m + (1.0 - ADAM_B1) * g
    v2 = ADAM_B2 * v + (1.0 - ADAM_B2) * (g * g)
    m_hat = m2 / (1.0 - ADAM_B1 ** ADAM_STEP)
    v_hat = v2 / (1.0 - ADAM_B2 ** ADAM_STEP)
    delta = -ADAM_LR * (m_hat / (jnp.sqrt(v_hat) + ADAM_EPS) + ADAM_WD * w)
    return delta, m2, v2


def _row_tile(rows, cols):
    cap = max(8, (1 << 18) // cols)
    best = 8
    for cand in range(8, min(rows, cap) + 1, 8):
        if rows % cand == 0:
            best = cand
    return best


def _adamw_big(name, w, g, m, v):
    shape = w.shape
    cols = shape[-1]
    rows = math.prod(shape[:-1])
    w2, g2, m2, v2 = (a.reshape(rows, cols) for a in (w, g, m, v))
    tr = _row_tile(rows, cols)

    def body(w_ref, g_ref, m_ref, v_ref, d_ref, mo_ref, vo_ref):
        d, mm, vv = _adamw_math(w_ref[...], g_ref[...], m_ref[...], v_ref[...])
        d_ref[...] = d
        mo_ref[...] = mm
        vo_ref[...] = vv

    blk = _row_spec(tr, cols)
    outs = pl.pallas_call(
        body, grid=(rows // tr,), in_specs=[blk] * 4, out_specs=[blk] * 3, out_shape=[SDS((rows, cols), F32)] * 3,
        name=name, compiler_params=_params())(w2, g2, m2, v2)
    return tuple(o.reshape(shape) for o in outs)


def _adamw_small(ws, gs, ms, vs):
    n = len(ws)
    flat = []
    for group in (ws, gs, ms, vs):
        flat += [a.reshape(-1, a.shape[-1]) for a in group]

    def body(*refs):
        w_r, g_r, m_r, v_r = refs[:n], refs[n:2 * n], refs[2 * n:3 * n], refs[3 * n:4 * n]
        d_o, m_o, v_o = refs[4 * n:5 * n], refs[5 * n:6 * n], refs[6 * n:7 * n]
        for j in range(n):
            d, mm, vv = _adamw_math(w_r[j][...], g_r[j][...], m_r[j][...], v_r[j][...])
            d_o[j][...] = d
            m_o[j][...] = mm
            v_o[j][...] = vv

    shapes = [SDS(a.shape, F32) for a in flat[:n]]
    outs = pl.pallas_call(body, out_shape=shapes * 3, name="adamw_small", compiler_params=_params())(*flat)
    res = []
    for k in range(3):
        res.append([outs[k * n + j].reshape(ws[j].shape) for j in range(n)])
    return res


BIG = ("ab_w_in", "ab_w_out", "cd_w_in", "cd_w_out", "ffn_w_gate", "ffn_w_up", "ffn_w_down")
V_BLOCK = (2 * A_WIDTH + QK_COLS) // B_WIDTH


def _pad_rows(a, rows):
    return jnp.pad(a, ((0, rows - a.shape[0]), (0, 0)))


def _local_step(x, target, w, sp):
    t, d = x.shape
    tabs = _rope_tables(t)
    gains = jnp.concatenate([jnp.tile(sp["q_norm_g"][g], HEAD_DIM // 8) for g in range(N_DIL)]
                            + [jnp.tile(sp["k_norm_g"][g], HEAD_DIM // 8) for g in range(N_DIL)]).reshape(1, QK_COLS)
    bias_t = sp["sgu_bias"].T
    cw = _pad_rows(sp["conv_c_w"], 32)
    dw = _pad_rows(sp["conv_d_w"], 8)
    cb, clg, clb = (sp[k].reshape(1, C_WIDTH) for k in ("conv_c_b", "c_ln_g", "c_ln_b"))
    slg, slb = sp["sgu_norm_g"].reshape(1, A_WIDTH), sp["sgu_norm_b"].reshape(1, A_WIDTH)
    g_ab, g_cd = sp["ab_norm_g"].reshape(1, d), sp["cd_norm_g"].reshape(1, d)
    g_f0, g_f1 = sp["ffn_norm_g"][0:1], sp["ffn_norm_g"][1:2]
    wo = w["ab_w_out"].reshape(-1, d)
    wo2 = w["cd_w_out"].reshape(-1, d)
    wg, wu, wd = w["ffn_w_gate"], w["ffn_w_up"], w["ffn_w_down"]

    h0 = _rms_fwd("rms_ab", x, g_ab)
    proj = _proj_in("proj_ab", h0, w["ab_w_in"], 0)
    a_out = _mixer_a_fwd(proj, slg, slb, sp["sgu_w"], bias_t)
    qk = _qk_fwd(proj, gains, tabs)
    qkv, o_list, l_list = [], [], []
    for g, rate in enumerate(DIL_RATES):
        if rate == 1:
            qk3, proj3 = qk.reshape(1, t, QK_COLS), proj.reshape(1, t, AB_IN)
            q, k, v = (qk3, g), (qk3, N_DIL + g), (proj3, V_BLOCK + g)
        else:
            qp, kp, vp = _permute(f"perm_fwd_{g}", [(qk, g), (qk, N_DIL + g), (proj, V_BLOCK + g)], rate)
            q, k, v = (qp, 0), (kp, 0), (vp, 0)
        qkv.append((q, k, v))
        o, l = _attn_fwd(f"attn_fwd_{g}", q, k, v)
        if rate == 1:
            o, l = o.reshape(t, B_WIDTH), l.reshape(t, B_WIDTH)
        else:
            o, l = _unpermute(f"unperm_fwd_{g}", [o, l], rate)
        o_list.append(o)
        l_list.append(l)
    cat, lse_tot = _attn_merge(a_out, o_list, l_list)
    x1, hf0 = _proj_out("out_ab", cat, wo, x, g_next=g_f0)
    gate0, up0, act0 = _ffn_in("ffn_in_0", hf0, wg, wu, 0)
    x2, h1 = _ffn_out("ffn_out_0", act0, wd, 0, x1, g_next=g_cd)
    projcd = _proj_in("proj_cd", h1, w["cd_w_in"], 0)
    cat2 = _mixer_cd_fwd(projcd, cw, cb, clg, clb, dw)
    x3, hf1 = _proj_out("out_cd", cat2, wo2, x2, g_next=g_f1)
    gate1, up1, act1 = _ffn_in("ffn_in_1", hf1, wg, wu, 1)
    dy, loss_acc = _ffn_out("ffn_out_1", act1, wd, 1, x3, target=target)
    loss = 0.5 * loss_acc[0, 0] / d

    grads = {}
    dgate, dup = _ffn_dact("ffn_dact_1", dy, wd, 1, gate1, up1)
    d_down = _wgrad_row_sharded("wgrad_down_1", act1, dy, True, layers=2, layer=1)
    d_gate, d_up = _wgrad_col_sharded("wgrad_gate_up_1", hf1, [dgate, dup], True, layers=2, layer=1)
    g3, d_f1 = _dgrad_cols("dgrad_ffn_1", [dgate, dup], [wg, wu], 1, True, x3, g_f1, dy)

    dcat2 = _dgrad_rows("dgrad_out_cd", g3, wo2)
    grads["cd_w_out"] = _wgrad_row_sharded("wgrad_out_cd", cat2, g3, False)
    dprojcd, d_cw, d_cb, d_clg, d_clb, d_dw = _mixer_cd_bwd(projcd, dcat2, cw, cb, clg, clb, dw)
    grads["cd_w_in"] = _wgrad_col_sharded("wgrad_in_cd", h1, [dprojcd], False)[0]
    g2, d_cdn = _dgrad_cols("dgrad_in_cd", [dprojcd], [w["cd_w_in"]], 0, False, x2, g_cd, g3)

    dgate, dup = _ffn_dact("ffn_dact_0", g2, wd, 0, gate0, up0)
    grads["ffn_w_down"] = _wgrad_row_sharded("wgrad_down_0", act0, g2, True, layers=2, layer=0, buf=d_down)
    grads["ffn_w_gate"], grads["ffn_w_up"] = _wgrad_col_sharded(
        "wgrad_gate_up_0", hf0, [dgate, dup], True, layers=2, layer=0, bufs=[d_gate, d_up])
    g1, d_f0 = _dgrad_cols("dgrad_ffn_0", [dgate, dup], [wg, wu], 0, True, x1, g_f0, g2)

    dcat = _dgrad_rows("dgrad_out_ab", g1, wo)
    grads["ab_w_out"] = _wgrad_row_sharded("wgrad_out_ab", cat, g1, False)
    d_a, d_sw, d_sbt, d_slg, d_slb = _mixer_a_bwd(proj, dcat, slg, slb, sp["sgu_w"], bias_t)
    dbb, dd = _attn_bwd_prep(dcat, cat)
    dqs, dks, dvs = [], [], []
    for g, rate in enumerate(DIL_RATES):
        q, k, v = qkv[g]
        if rate == 1:
            db3, l3, dd3 = (a.reshape(1, t, B_WIDTH) for a in (dbb, lse_tot, dd))
        else:
            db3, l3, dd3 = _permute(f"perm_bwd_{g}", [(dbb, 0), (lse_tot, 0), (dd, 0)], rate)
        dq, dk, dv = _attn_bwd(f"attn_bwd_{g}", q, k, v, db3, l3, dd3)
        if rate == 1:
            dq, dk, dv = (a.reshape(t, B_WIDTH) for a in (dq, dk, dv))
        else:
            dq, dk, dv = _unpermute(f"unperm_bwd_{g}", [dq, dk, dv], rate)
        dqs.append(dq)
        dks.append(dk)
        dvs.append(dv)
    dproj, d_gains = _dproj_assemble(proj, d_a, dqs + dks, dvs, gains, tabs)
    d_gains = _fold_heads(d_gains)[0].reshape(2, N_DIL, B_WIDTH)[:, :, :HEAD_DIM]
    grads["ab_w_in"] = _wgrad_col_sharded("wgrad_in_ab", h0, [dproj], False)[0]
    gx, d_abn = _dgrad_cols("dgrad_in_ab", [dproj], [w["ab_w_in"]], 0, False, x, g_ab, g1)

    small = {
        "ab_norm_g": d_abn, "sgu_norm_g": d_slg, "sgu_norm_b": d_slb, "sgu_w": d_sw, "sgu_bias": d_sbt.T,
        "q_norm_g": d_gains[0], "k_norm_g": d_gains[1], "cd_norm_g": d_cdn, "conv_c_w": d_cw[:C_KERNEL],
        "conv_c_b": d_cb, "c_ln_g": d_clg, "c_ln_b": d_clb, "conv_d_w": d_dw[:D_KERNEL],
        "ffn_norm_g": jnp.concatenate([d_f0, d_f1], axis=0),
    }
    return loss, gx, grads, small


SHARDED_SMALL = ("cd_norm_g", "conv_c_w", "conv_c_b", "c_ln_g", "c_ln_b", "conv_d_w")
SHARDED_ROWS = 48
REPLICATED_SMALL = ("ab_norm_g", "sgu_norm_g", "sgu_norm_b", "sgu_w", "sgu_bias", "q_norm_g", "k_norm_g", "ffn_norm_g")
REPLICATED_ROWS = 560


def _pack_sharded(parts):
    rows = [parts[k].reshape(-1, LANES) for k in SHARDED_SMALL]
    return _pad_rows(jnp.concatenate(rows, axis=0), SHARDED_ROWS)


def _split_full_small(small):
    per_chip = []
    for q in range(N_CHIPS):
        parts = {}
        for k in SHARDED_SMALL:
            a = small[k]
            a = a.reshape(-1, a.shape[-1])
            n = a.shape[-1] // N_CHIPS
            parts[k] = a[:, q * n:(q + 1) * n]
        per_chip.append(_pack_sharded(parts))
    return jnp.stack(per_chip)


def _unpack_sharded(pack, shapes):
    out, r = {}, 0
    for k in SHARDED_SMALL:
        n = math.prod(shapes[k]) // LANES
        out[k] = pack[r:r + n].reshape(shapes[k])
        r += n
    return out


def _gathered_small(packs, shapes):
    per_chip = [_unpack_sharded(packs[q], shapes) for q in range(N_CHIPS)]
    return {k: jnp.concatenate([pc[k] for pc in per_chip], axis=-1) for k in SHARDED_SMALL}


def _pack_replicated(small):
    rows = []
    for k in REPLICATED_SMALL:
        a = small[k].reshape(-1)
        a = jnp.pad(a, (0, (-a.shape[0]) % LANES))
        rows.append(a.reshape(-1, LANES))
    return _pad_rows(jnp.concatenate(rows, axis=0), REPLICATED_ROWS)


def _unpack_replicated(pack, shapes):
    out, r = {}, 0
    for k in REPLICATED_SMALL:
        size = math.prod(shapes[k])
        n = -(-size // LANES)
        out[k] = pack[r:r + n].reshape(-1)[:size].reshape(shapes[k])
        r += n
    return out


def _two_halves(a):
    rows = math.prod(a.shape[:-1])
    return a.reshape(2, rows // 2, a.shape[-1])


def _reduce_gradients(tensors, bf16_payload):
    tensors = [a.reshape((N_CHIPS, 2, math.prod(a.shape[1:-1]) // 2, a.shape[-1])) for a in tensors]
    recv = _swap_halves(tensors)
    halves = [_add_own_half(f"pair_sum_{j}", a, b, BF16 if bf else F32)
              for j, (a, b, bf) in enumerate(zip(tensors, recv, bf16_payload))]
    parts = _scatter_chips([h[0] for h in halves], [h[1] for h in halves])
    sums = [_sum_chips(f"chip_sum_{j}", a) for j, a in enumerate(parts)]
    return _join_halves(sums)


WEIGHT_ORDER = ("ab_norm_g", "ab_w_in", "sgu_norm_g", "sgu_norm_b", "sgu_w", "sgu_bias", "q_norm_g", "k_norm_g", "ab_w_out",
                "cd_norm_g", "cd_w_in", "conv_c_w", "conv_c_b", "c_ln_g", "c_ln_b", "conv_d_w", "cd_w_out", "ffn_norm_g",
                "ffn_w_gate", "ffn_w_up", "ffn_w_down")


def kernel(x, ab_norm_g, ab_w_in, sgu_norm_g, sgu_norm_b, sgu_w, sgu_bias, q_norm_g, k_norm_g, ab_w_out, cd_norm_g, cd_w_in, conv_c_w, conv_c_b, c_ln_g, c_ln_b, conv_d_w, cd_w_out, ffn_norm_g, ffn_w_gate, ffn_w_up, ffn_w_down, loss_target, m_ab_norm_g, m_ab_w_in, m_sgu_norm_g, m_sgu_norm_b, m_sgu_w, m_sgu_bias, m_q_norm_g, m_k_norm_g, m_ab_w_out, m_cd_norm_g, m_cd_w_in, m_conv_c_w, m_conv_c_b, m_c_ln_g, m_c_ln_b, m_conv_d_w, m_cd_w_out, m_ffn_norm_g, m_ffn_w_gate, m_ffn_w_up, m_ffn_w_down, v_ab_norm_g, v_ab_w_in, v_sgu_norm_g, v_sgu_norm_b, v_sgu_w, v_sgu_bias, v_q_norm_g, v_k_norm_g, v_ab_w_out, v_cd_norm_g, v_cd_w_in, v_conv_c_w, v_conv_c_b, v_c_ln_g, v_c_ln_b, v_conv_d_w, v_cd_w_out, v_ffn_norm_g, v_ffn_w_gate, v_ffn_w_up, v_ffn_w_down):
    args = dict(locals())
    ws = {k: args[k] for k in WEIGHT_ORDER}
    ms = {k: args["m_" + k] for k in WEIGHT_ORDER}
    vs = {k: args["v_" + k] for k in WEIGHT_ORDER}
    small_names = [k for k in WEIGHT_ORDER if k not in BIG]

    own_small = _pack_sharded({k: ws[k][0] for k in SHARDED_SMALL})
    gathered = _gather_weights([_stage_own("stage_" + k, _two_halves(ws[k]), BF16) for k in BIG]
                               + [_stage_own("stage_small", _two_halves(own_small), F32)])
    w_full = {k: g.reshape((N_CHIPS,) + ws[k].shape) for k, g in zip(BIG, gathered)}
    sp = _gathered_small(gathered[-1].reshape(N_CHIPS, SHARDED_ROWS, LANES), {k: ws[k].shape[1:] for k in SHARDED_SMALL})
    for k in REPLICATED_SMALL:
        sp[k] = ws[k] if k == "ffn_norm_g" else ws[k][0]

    loss, grad_x, g_big, g_small = _local_step(x[0], loss_target[0], w_full, sp)

    g_rep = jnp.broadcast_to(_pack_replicated(g_small), (N_CHIPS, REPLICATED_ROWS, LANES))
    g_sh = _split_full_small(g_small)
    reduced = _reduce_gradients([g_big[k] for k in BIG] + [g_sh, g_rep], [True] * len(BIG) + [False, False])
    grad = {k: r.reshape(ws[k].shape) for k, r in zip(BIG, reduced)}
    grad.update(_unpack_sharded(reduced[-2].reshape(SHARDED_ROWS, LANES), {k: ws[k].shape for k in SHARDED_SMALL}))
    grad.update(_unpack_replicated(reduced[-1].reshape(REPLICATED_ROWS, LANES), {k: ws[k].shape for k in REPLICATED_SMALL}))

    delta, new_m, new_v = {}, {}, {}
    for k in BIG:
        delta[k], new_m[k], new_v[k] = _adamw_big("adamw_" + k, ws[k], grad[k], ms[k], vs[k])
    d_s, m_s, v_s = _adamw_small([ws[k] for k in small_names], [grad[k] for k in small_names],
                                 [ms[k] for k in small_names], [vs[k] for k in small_names])
    for j, k in enumerate(small_names):
        delta[k], new_m[k], new_v[k] = d_s[j], m_s[j], v_s[j]

    loss = lax.psum(loss, ("x", "y", "c"))
    return (loss, grad_x[None], *[grad[k] for k in WEIGHT_ORDER], *[delta[k] for k in WEIGHT_ORDER],
            *[new_m[k] for k in WEIGHT_ORDER], *[new_v[k] for k in WEIGHT_ORDER])
```

```python
import functools
import math

import jax
import jax.numpy as jnp
from jax import lax
from jax.experimental import pallas as pl
from jax.experimental.pallas import tpu as pltpu

F32 = jnp.float32
BF16 = jnp.bfloat16
SDS = jax.ShapeDtypeStruct

N_CHIPS = 4
EPS = 1e-6
NEG_INF = -1e30
CHUNK = 128
A_GROUPS = 4
A_WIDTH = 512
N_DIL = 3
DIL_RATES = (1, 4, 16)
HEAD_DIM = 64
B_WIDTH = 512
ROPE_DIM = 16
ROPE_THETA = 500000.0
C_WIDTH = 512
C_KERNEL = 31
D_KERNEL = 3
HALO = 32
ATT_BLOCK = 128
LANES = 128

ADAM_LR = 0.001
ADAM_B1 = 0.9
ADAM_B2 = 0.999
ADAM_EPS = 1e-08
ADAM_WD = 0.01
ADAM_STEP = 10

VMEM_LIMIT = 56 * 1024 * 1024

NN = (((1,), (0,)), ((), ()))
NT = (((1,), (1,)), ((), ()))
TN = (((0,), (0,)), ((), ()))

TILES = {"proj_in": 1024, "proj_out": 1024, "ffn_in": 1024, "ffn_out": 512, "ffn_dact": 512, "dgrad_cols": 512,
         "dgrad_rows": 1024, "wgrad": 2048}


def _params(sem=None):
    return pltpu.CompilerParams(dimension_semantics=sem, vmem_limit_bytes=VMEM_LIMIT)


def _bf(v):
    return v if v.dtype == BF16 else v.astype(BF16)


def _dot(a, b, dims):
    return lax.dot_general(_bf(a), _bf(b), dims, preferred_element_type=F32)


def _dot_hi(a, b):
    return jnp.dot(a, b, precision=lax.Precision.HIGHEST, preferred_element_type=F32)


def _sigmoid(v):
    return 0.5 * jnp.tanh(0.5 * v) + 0.5


def _gelu(v):
    return 0.5 * v * (1.0 + lax.erf(v * (1.0 / math.sqrt(2.0))))


def _gelu_grad(v):
    cdf = 0.5 * (1.0 + lax.erf(v * (1.0 / math.sqrt(2.0))))
    return cdf + v * jnp.exp(-0.5 * v * v) * (1.0 / math.sqrt(2.0 * math.pi))


def _segment_mean_matrix(seg):
    r = lax.broadcasted_iota(jnp.int32, (LANES, LANES), 0) // seg
    c = lax.broadcasted_iota(jnp.int32, (LANES, LANES), 1) // seg
    return jnp.where(r == c, 1.0 / seg, 0.0).astype(F32)


def _whole(ref, p):
    return ref[...]


def _slab(ref, p):
    return ref[p]


def _matmul(name, grid, pairs, extras, outs, dims, epi, *, slabs=1, n_acc=1, sem=None):
    n_pairs, n_ex, n_out = len(pairs), len(extras), len(outs)

    def body(*refs):
        ab = refs[:2 * n_pairs]
        ex = refs[2 * n_pairs:2 * n_pairs + n_ex]
        out_refs = refs[2 * n_pairs + n_ex:2 * n_pairs + n_ex + n_out]
        pids = tuple(pl.program_id(a) for a in range(len(grid)))
        parts = [None] * n_acc
        for p in range(slabs):
            for j, (_, _, a_pick, _, _, b_pick, acc) in enumerate(pairs):
                d = _dot(a_pick(ab[2 * j], p), b_pick(ab[2 * j + 1], p), dims)
                parts[acc] = d if parts[acc] is None else parts[acc] + d
        epi(parts, ex, out_refs, pids)

    operands, in_specs = [], []
    for a, a_spec, _, b, b_spec, _, _ in pairs:
        operands += [a, b]
        in_specs += [a_spec, b_spec]
    for e, e_spec in extras:
        operands.append(e)
        in_specs.append(e_spec)
    return pl.pallas_call(
        body, grid=grid, in_specs=in_specs, out_specs=[o[1] for o in outs],
        out_shape=[o[0] for o in outs], name=name, compiler_params=_params(sem))(*operands)


def _rms_rows(v, g):
    r = lax.rsqrt(jnp.mean(v * v, axis=-1, keepdims=True) + EPS)
    return v * r * g


def _rms_fwd(name, x, g):
    t, d = x.shape
    tm = 512

    def body(x_ref, g_ref, o_ref):
        o_ref[...] = _rms_rows(x_ref[...], g_ref[...]).astype(BF16)

    return pl.pallas_call(
        body, grid=(t // tm,),
        in_specs=[pl.BlockSpec((tm, d), lambda i: (i, 0)), pl.BlockSpec((1, d), lambda i: (0, 0))],
        out_specs=pl.BlockSpec((tm, d), lambda i: (i, 0)), out_shape=SDS((t, d), BF16), name=name,
        compiler_params=_params())(x, g)


def _epi_residual_norm(accs, ex, outs, pids):
    x_new = accs[0] + ex[0][...]
    outs[0][...] = x_new
    outs[1][...] = _rms_rows(x_new, ex[1][...]).astype(BF16)


def _epi_residual_loss(accs, ex, outs, pids):
    y = accs[0] + ex[0][...]
    err = y - ex[1][...]
    dy = err * (1.0 / err.shape[-1])
    outs[0][...] = dy
    outs[2][...] = dy.astype(BF16)

    @pl.when(pids[0] == 0)
    def _():
        outs[1][...] = jnp.zeros_like(outs[1])

    outs[1][...] += jnp.sum(err * err)


def _epi_rms_bwd(accs, ex, outs, pids):
    dh = accs[0]
    xv, g, res = ex[0][...], ex[1][...], ex[2][...]
    r = lax.rsqrt(jnp.mean(xv * xv, axis=-1, keepdims=True) + EPS)
    xh = xv * r
    dy = dh * g
    dx = res + r * (dy - xh * jnp.mean(dy * xh, axis=-1, keepdims=True))
    outs[0][...] = dx
    if len(outs) > 2:
        outs[2][...] = dx.astype(BF16)

    @pl.when(pids[0] == 0)
    def _():
        outs[1][...] = jnp.zeros_like(outs[1])

    outs[1][...] += jnp.sum(dh * xh, axis=0, keepdims=True)


def _row_spec(tm, d):
    return pl.BlockSpec((tm, d), lambda i, *_: (i, 0))


def _const_spec(shape):
    nd = len(shape)
    return pl.BlockSpec(shape, lambda *_: (0,) * nd)


def _proj_in(name, h, w, layer):
    t, d = h.shape
    n4 = w.shape[-1]
    tm = TILES["proj_in"]

    def epi(accs, ex, outs, pids):
        outs[0][...] = accs[0].astype(BF16)

    return _matmul(
        name, (N_CHIPS, t // tm),
        [(h, pl.BlockSpec((tm, d), lambda p, i: (i, 0)), _whole,
          w, pl.BlockSpec((None, None, d, n4), lambda p, i: (p, layer, 0, 0)), _whole, 0)],
        [], [(SDS((t, N_CHIPS * n4), BF16), pl.BlockSpec((tm, n4), lambda p, i: (i, p)))],
        NN, epi)[0]


def _proj_out(name, a, w, x, g_next=None, target=None):
    t, k = a.shape
    d = w.shape[-1]
    tm = TILES["proj_out"]
    if target is None:
        extras = [(x, _row_spec(tm, d)), (g_next, _const_spec((1, d)))]
        outs = [(SDS((t, d), F32), _row_spec(tm, d)), (SDS((t, d), BF16), _row_spec(tm, d))]
        epi = _epi_residual_norm
    else:
        extras = [(x, _row_spec(tm, d)), (target, _row_spec(tm, d))]
        outs = [(SDS((t, d), F32), _row_spec(tm, d)), (SDS((8, LANES), F32), _const_spec((8, LANES))),
                (SDS((t, d), BF16), _row_spec(tm, d))]
        epi = _epi_residual_loss
    return _matmul(name, (t // tm,), [(a, _row_spec(tm, k), _whole, w, _const_spec((k, d)), _whole, 0)], extras, outs, NN, epi)


def _ffn_in(name, h, wg, wu, layer):
    t, d = h.shape
    n4 = wg.shape[-1]
    tm = TILES["ffn_in"]

    def epi(accs, ex, outs, pids):
        gate, up = accs
        outs[0][...] = gate.astype(BF16)
        outs[1][...] = up.astype(BF16)
        outs[2][...] = (gate * _sigmoid(gate) * up).astype(BF16)

    w_spec = pl.BlockSpec((None, None, d, n4), lambda p, i: (p, layer, 0, 0))
    h_spec = pl.BlockSpec((tm, d), lambda p, i: (i, 0))
    o = (SDS((N_CHIPS, t, n4), BF16), pl.BlockSpec((None, tm, n4), lambda p, i: (p, i, 0)))
    return _matmul(name, (N_CHIPS, t // tm),
                   [(h, h_spec, _whole, wg, w_spec, _whole, 0), (h, h_spec, _whole, wu, w_spec, _whole, 1)], [],
                   [o, o, o], NN, epi, n_acc=2)


def _ffn_out(name, act, wd, layer, x, g_next=None, target=None):
    _, t, n4 = act.shape
    d = wd.shape[-1]
    tm = TILES["ffn_out"]
    xs = _row_spec(tm, d)
    if target is None:
        extras = [(x, xs), (g_next, _const_spec((1, d)))]
        outs = [(SDS((t, d), F32), xs), (SDS((t, d), BF16), xs)]
        epi = _epi_residual_norm
    else:
        extras = [(x, xs), (target, xs)]
        outs = [(SDS((t, d), F32), xs), (SDS((8, LANES), F32), _const_spec((8, LANES))), (SDS((t, d), BF16), xs)]
        epi = _epi_residual_loss
    return _matmul(
        name, (t // tm,),
        [(act, pl.BlockSpec((N_CHIPS, tm, n4), lambda i: (0, i, 0)), _slab,
          wd, pl.BlockSpec((N_CHIPS, None, n4, d), lambda i: (0, layer, 0, 0)), _slab, 0)],
        extras, outs, NN, epi, slabs=N_CHIPS)


def _ffn_dact(name, g, wd, layer, gate, up):
    t, d = g.shape
    n4 = wd.shape[-2]
    tm = TILES["ffn_dact"]

    def epi(accs, ex, outs, pids):
        dact = accs[0]
        gt = ex[0][...].astype(F32)
        upv = ex[1][...].astype(F32)
        s = _sigmoid(gt)
        silu = gt * s
        outs[0][...] = (dact * upv * (s + silu - silu * s)).astype(BF16)
        outs[1][...] = (dact * silu).astype(BF16)

    blk = pl.BlockSpec((None, tm, n4), lambda p, i: (p, i, 0))
    o = (SDS((N_CHIPS, t, n4), BF16), blk)
    return _matmul(
        name, (N_CHIPS, t // tm),
        [(g, pl.BlockSpec((tm, d), lambda p, i: (i, 0)), _whole,
          wd, pl.BlockSpec((None, None, n4, d), lambda p, i: (p, layer, 0, 0)), _whole, 0)],
        [(gate, blk), (up, blk)], [o, o], NT, epi)


def _copy_epi(accs, ex, outs, pids):
    for a, o in zip(accs, outs):
        o[...] = a.astype(o.dtype)


def _dgrad_cols(name, dz_list, w_list, layer, three_d, x, g, res, bf16_copy=True):
    t, d = x.shape
    n4 = w_list[0].shape[-1]
    tm = TILES["dgrad_cols"]
    if three_d:
        zs, z_pick = pl.BlockSpec((N_CHIPS, tm, n4), lambda i: (0, i, 0)), _slab
    else:
        zs, z_pick = _row_spec(tm, N_CHIPS * n4), (lambda ref, p: ref[:, p * n4:(p + 1) * n4])
    ws = pl.BlockSpec((N_CHIPS, None, d, n4), lambda i: (0, layer, 0, 0))
    xs = _row_spec(tm, d)
    return _matmul(
        name, (t // tm,), [(dz, zs, z_pick, w, ws, _slab, 0) for dz, w in zip(dz_list, w_list)],
        [(x, xs), (g, _const_spec((1, d))), (res, xs)],
        [(SDS((t, d), F32), xs), (SDS((1, d), F32), _const_spec((1, d)))] + ([(SDS((t, d), BF16), xs)] if bf16_copy else []),
        NT, _epi_rms_bwd, slabs=N_CHIPS)


def _dgrad_rows(name, g, w):
    t, d = g.shape
    k = w.shape[0]
    tm = TILES["dgrad_rows"]
    return _matmul(name, (t // tm,), [(g, _row_spec(tm, d), _whole, w, _const_spec((k, d)), _whole, 0)], [],
                   [(SDS((t, k), F32), _row_spec(tm, k))], NT, _copy_epi)[0]


A_TILE = 256


def _a_common(p_ref, lg_ref, lb_ref):
    pv = p_ref[...].astype(F32)
    a = _gelu(pv)
    u, v = a[:, :A_WIDTH], a[:, A_WIDTH:]
    vc = v - jnp.mean(v, axis=-1, keepdims=True)
    rs = lax.rsqrt(jnp.mean(vc * vc, axis=-1, keepdims=True) + EPS)
    vhat = vc * rs
    vn = vhat * lg_ref[...] + lb_ref[...]
    return pv, u, vhat, rs, vn.astype(BF16)


def _tril_weights(w_ref, g):
    r = lax.broadcasted_iota(jnp.int32, (CHUNK, CHUNK), 0)
    c = lax.broadcasted_iota(jnp.int32, (CHUNK, CHUNK), 1)
    return jnp.where(c <= r, w_ref[g], 0.0).astype(BF16), c <= r


def _mixer_a_fwd(proj, lg, lb, w, bias_t):
    t = proj.shape[0]

    def body(p_ref, lg_ref, lb_ref, w_ref, bt_ref, o_ref):
        _, u, _, _, vnb = _a_common(p_ref, lg_ref, lb_ref)
        for g in range(A_GROUPS):
            wt, _ = _tril_weights(w_ref, g)
            cs = slice(g * CHUNK, (g + 1) * CHUNK)
            for ch in range(A_TILE // CHUNK):
                rs_ = slice(ch * CHUNK, (ch + 1) * CHUNK)
                mixed = _dot(wt, vnb[rs_, cs], NN) + bt_ref[:, g:g + 1]
                o_ref[rs_, cs] = (u[rs_, cs] * mixed).astype(BF16)

    return pl.pallas_call(
        body, grid=(t // A_TILE,),
        in_specs=[pl.BlockSpec((A_TILE, 2 * A_WIDTH), lambda i: (i, 0)), _const_spec((1, A_WIDTH)),
                  _const_spec((1, A_WIDTH)), _const_spec((A_GROUPS, CHUNK, CHUNK)), _const_spec((CHUNK, A_GROUPS))],
        out_specs=pl.BlockSpec((A_TILE, A_WIDTH), lambda i: (i, 0)), out_shape=SDS((t, A_WIDTH), BF16),
        name="mixer_a_fwd", compiler_params=_params())(proj, lg, lb, w, bias_t)


def _mixer_a_bwd(proj, dcat, lg, lb, w, bias_t):
    t = proj.shape[0]

    def body(p_ref, da_ref, lg_ref, lb_ref, w_ref, bt_ref, dp_ref, dw_ref, dbt_ref, dlg_ref, dlb_ref, du_scr, dvn_scr):
        @pl.when(pl.program_id(0) == 0)
        def _():
            dw_ref[...] = jnp.zeros_like(dw_ref)
            dbt_ref[...] = jnp.zeros_like(dbt_ref)
            dlg_ref[...] = jnp.zeros_like(dlg_ref)
            dlb_ref[...] = jnp.zeros_like(dlb_ref)

        pv, u, vhat, rs, vnb = _a_common(p_ref, lg_ref, lb_ref)
        da = da_ref[...]
        for g in range(A_GROUPS):
            wt, keep = _tril_weights(w_ref, g)
            cs = slice(g * CHUNK, (g + 1) * CHUNK)
            for ch in range(A_TILE // CHUNK):
                rs_ = slice(ch * CHUNK, (ch + 1) * CHUNK)
                vg = vnb[rs_, cs]
                mixed = _dot(wt, vg, NN) + bt_ref[:, g:g + 1]
                du_scr[rs_, cs] = da[rs_, cs] * mixed
                dmx = da[rs_, cs] * u[rs_, cs]
                dw_ref[g] += jnp.where(keep, _dot(dmx, vg, NT), 0.0)
                dvn_scr[rs_, cs] = _dot(wt, dmx, TN)
                dbt_ref[:, g:g + 1] += jnp.sum(dmx, axis=1, keepdims=True)
        dvn = dvn_scr[...]
        dlg_ref[...] += jnp.sum(dvn * vhat, axis=0, keepdims=True)
        dlb_ref[...] += jnp.sum(dvn, axis=0, keepdims=True)
        dvh = dvn * lg_ref[...]
        dv = rs * (dvh - jnp.mean(dvh, axis=-1, keepdims=True) - vhat * jnp.mean(dvh * vhat, axis=-1, keepdims=True))
        gp = _gelu_grad(pv)
        dp_ref[:, :A_WIDTH] = (du_scr[...] * gp[:, :A_WIDTH]).astype(BF16)
        dp_ref[:, A_WIDTH:] = (dv * gp[:, A_WIDTH:]).astype(BF16)

    return pl.pallas_call(
        body, grid=(t // A_TILE,),
        in_specs=[pl.BlockSpec((A_TILE, 2 * A_WIDTH), lambda i: (i, 0)), pl.BlockSpec((A_TILE, A_WIDTH), lambda i: (i, 0)),
                  _const_spec((1, A_WIDTH)), _const_spec((1, A_WIDTH)), _const_spec((A_GROUPS, CHUNK, CHUNK)),
                  _const_spec((CHUNK, A_GROUPS))],
        out_specs=[pl.BlockSpec((A_TILE, 2 * A_WIDTH), lambda i: (i, 0)), _const_spec((A_GROUPS, CHUNK, CHUNK)),
                   _const_spec((CHUNK, A_GROUPS)), _const_spec((1, A_WIDTH)), _const_spec((1, A_WIDTH))],
        out_shape=[SDS((t, 2 * A_WIDTH), BF16), SDS((A_GROUPS, CHUNK, CHUNK), F32), SDS((CHUNK, A_GROUPS), F32),
                   SDS((1, A_WIDTH), F32), SDS((1, A_WIDTH), F32)],
        scratch_shapes=[pltpu.VMEM((A_TILE, A_WIDTH), F32), pltpu.VMEM((A_TILE, A_WIDTH), F32)],
        name="mixer_a_bwd", compiler_params=_params())(proj, dcat, lg, lb, w, bias_t)


def _rope_tables(t):
    half = ROPE_DIM // 2
    inv_freq = ROPE_THETA ** (-jnp.arange(half, dtype=F32) * 2.0 / ROPE_DIM)
    ang = jnp.arange(t, dtype=F32)[:, None] * inv_freq[None, :]
    cos, sin = jnp.cos(ang), jnp.sin(ang)
    one = jnp.ones((t, HEAD_DIM - ROPE_DIM), F32)
    zero = jnp.zeros((t, HEAD_DIM - ROPE_DIM), F32)
    zh = jnp.zeros((t, half), F32)
    c = jnp.concatenate([cos, cos, one], axis=1)
    s1 = jnp.concatenate([-sin, zh, zero], axis=1)
    s2 = jnp.concatenate([zh, sin, zero], axis=1)
    return tuple(jnp.tile(a, (1, LANES // HEAD_DIM)) for a in (c, s1, s2))


QK_TILE = 512
QK_COLS = 2 * N_DIL * B_WIDTH


def _qk_fwd(proj, gains, tabs):
    t = proj.shape[0]
    col0 = 2 * A_WIDTH // 1024

    def body(p_ref, g_ref, c_ref, s1_ref, s2_ref, o_ref):
        seg = _segment_mean_matrix(HEAD_DIM)
        c, s1, s2 = c_ref[...], s1_ref[...], s2_ref[...]
        for ci in range(1024 // LANES):
            ls = slice(ci * LANES, (ci + 1) * LANES)
            xv = p_ref[:, ls].astype(F32)
            r = lax.rsqrt(_dot_hi(xv * xv, seg) + EPS)
            y = xv * r * g_ref[:, ls]
            o_ref[:, ls] = (y * c + pltpu.roll(y, LANES - 8, axis=1) * s1 + pltpu.roll(y, 8, axis=1) * s2).astype(BF16)

    tab = pl.BlockSpec((QK_TILE, LANES), lambda i, j: (i, 0))
    return pl.pallas_call(
        body, grid=(t // QK_TILE, QK_COLS // 1024),
        in_specs=[pl.BlockSpec((QK_TILE, 1024), lambda i, j: (i, col0 + j)), pl.BlockSpec((1, 1024), lambda i, j: (0, j)),
                  tab, tab, tab],
        out_specs=pl.BlockSpec((QK_TILE, 1024), lambda i, j: (i, j)), out_shape=SDS((t, QK_COLS), BF16),
        name="qk_norm_rope_fwd", compiler_params=_params())(proj, gains, *tabs)


PERM_TILE = 512


def _permute(name, items, rate):
    t = items[0][0].shape[0]
    n = len(items)
    rows = PERM_TILE // rate

    def body(*refs):
        scr = refs[-1]
        for x_ref, o_ref in zip(refs[:n], refs[n:2 * n]):
            for ci in range(B_WIDTH // LANES):
                scr[ci] = x_ref[:, ci * LANES:(ci + 1) * LANES].astype(F32)
            for rho in range(rate):
                for ci in range(B_WIDTH // LANES):
                    o_ref[rho, :, ci * LANES:(ci + 1) * LANES] = scr[ci, pl.ds(rho, rows, stride=rate), :].astype(o_ref.dtype)

    return pl.pallas_call(
        body, grid=(t // PERM_TILE,),
        in_specs=[pl.BlockSpec((PERM_TILE, B_WIDTH), functools.partial(lambda cb, i: (i, cb), cb)) for _, cb in items],
        out_specs=[pl.BlockSpec((rate, rows, B_WIDTH), lambda i: (0, i, 0)) for _ in items],
        out_shape=[SDS((rate, t // rate, B_WIDTH), a.dtype) for a, _ in items],
        scratch_shapes=[pltpu.VMEM((B_WIDTH // LANES, PERM_TILE, LANES), F32)],
        name=name, compiler_params=_params())(*[a for a, _ in items])


def _unpermute(name, arrays, rate):
    t = arrays[0].shape[1] * rate
    n = len(arrays)
    rows = PERM_TILE // rate

    def body(*refs):
        scr = refs[-1]
        for x_ref, o_ref in zip(refs[:n], refs[n:2 * n]):
            for rho in range(rate):
                for ci in range(B_WIDTH // LANES):
                    scr[ci, pl.ds(rho, rows, stride=rate), :] = x_ref[rho, :, ci * LANES:(ci + 1) * LANES].astype(F32)
            for ci in range(B_WIDTH // LANES):
                o_ref[:, ci * LANES:(ci + 1) * LANES] = scr[ci].astype(o_ref.dtype)

    return pl.pallas_call(
        body, grid=(t // PERM_TILE,),
        in_specs=[pl.BlockSpec((rate, rows, B_WIDTH), lambda i: (0, i, 0)) for _ in arrays],
        out_specs=[pl.BlockSpec((PERM_TILE, B_WIDTH), lambda i: (i, 0)) for _ in arrays],
        out_shape=[SDS((t, B_WIDTH), a.dtype) for a in arrays],
        scratch_shapes=[pltpu.VMEM((B_WIDTH // LANES, PERM_TILE, LANES), F32)],
        name=name, compiler_params=_params())(*arrays)


def _head_lane_mask(h):
    lane = lax.broadcasted_iota(jnp.int32, (1, LANES), 1)
    return (lane < HEAD_DIM) if h == 0 else (lane >= HEAD_DIM)


def _attn_fwd(name, q, k, v):
    rate, length = q[0].shape[0], q[0].shape[1]
    nb = length // ATT_BLOCK
    scale = HEAD_DIM ** -0.5

    def body(q_ref, kc_ref, kp_ref, vc_ref, vp_ref, o_ref, l_ref):
        n = pl.program_id(1)
        qi = lax.broadcasted_iota(jnp.int32, (ATT_BLOCK, 2 * ATT_BLOCK), 0)
        cj = lax.broadcasted_iota(jnp.int32, (ATT_BLOCK, 2 * ATT_BLOCK), 1)
        has_prev = jnp.where(n > 0, 0, 2 * ATT_BLOCK)
        mask = ((cj < ATT_BLOCK) & (cj >= qi + has_prev)) | ((cj >= ATT_BLOCK) & (cj - ATT_BLOCK <= qi))
        for hp in range(B_WIDTH // LANES):
            ls = slice(hp * LANES, (hp + 1) * LANES)
            q2 = q_ref[:, ls]
            k2 = jnp.concatenate([kp_ref[:, ls], kc_ref[:, ls]], axis=0)
            v2 = jnp.concatenate([vp_ref[:, ls], vc_ref[:, ls]], axis=0)
            o_acc, lse2 = None, None
            for h in range(2):
                hm = _head_lane_mask(h)
                s = _dot(jnp.where(hm, q2, jnp.zeros_like(q2)), k2, NT) * scale
                s = jnp.where(mask, s, NEG_INF)
                m = jnp.max(s, axis=1, keepdims=True)
                p = jnp.exp(s - m)
                den = jnp.sum(p, axis=1, keepdims=True)
                lse = m + jnp.log(den)
                o = _dot(p / den, jnp.where(hm, v2, jnp.zeros_like(v2)), NN)
                o_acc = o if h == 0 else o_acc + o
                lse_b = lse + jnp.zeros((ATT_BLOCK, LANES), F32)
                lse2 = lse_b if h == 0 else jnp.where(hm, lse_b, lse2)
            o_ref[:, ls] = o_acc
            l_ref[:, ls] = lse2

    def cur(cb):
        return pl.BlockSpec((None, ATT_BLOCK, B_WIDTH), lambda r, n: (r, n, cb))

    def prev(cb):
        return pl.BlockSpec((None, ATT_BLOCK, B_WIDTH), lambda r, n: (r, jnp.maximum(n - 1, 0), cb))

    out = pl.BlockSpec((None, ATT_BLOCK, B_WIDTH), lambda r, n: (r, n, 0))
    return pl.pallas_call(
        body, grid=(rate, nb),
        in_specs=[cur(q[1]), cur(k[1]), prev(k[1]), cur(v[1]), prev(v[1])],
        out_specs=[out, out], out_shape=[SDS((rate, length, B_WIDTH), F32)] * 2,
        name=name, compiler_params=_params())(q[0], k[0], k[0], v[0], v[0])


def _attn_merge(a_out, o_list, l_list):
    t = a_out.shape[0]
    tm = 512

    def body(a_ref, o0, o1, o2, l0, l1, l2, cat_ref, lt_ref):
        ls = [l0[...], l1[...], l2[...]]
        m = jnp.maximum(jnp.maximum(ls[0], ls[1]), ls[2])
        es = [jnp.exp(l - m) for l in ls]
        den = es[0] + es[1] + es[2]
        b = (es[0] * o0[...] + es[1] * o1[...] + es[2] * o2[...]) / den
        cat_ref[:, :A_WIDTH] = a_ref[...]
        cat_ref[:, A_WIDTH:] = b.astype(BF16)
        lt_ref[...] = m + jnp.log(den)

    blk = _row_spec(tm, B_WIDTH)
    return pl.pallas_call(
        body, grid=(t // tm,), in_specs=[blk] * 7,
        out_specs=[_row_spec(tm, A_WIDTH + B_WIDTH), blk],
        out_shape=[SDS((t, A_WIDTH + B_WIDTH), BF16), SDS((t, B_WIDTH), F32)],
        name="attn_merge", compiler_params=_params())(a_out, *o_list, *l_list)


def _attn_bwd_prep(dcat, cat):
    t = dcat.shape[0]
    tm = 512

    def body(d_ref, b_ref, db_ref, dd_ref):
        seg = _segment_mean_matrix(HEAD_DIM) * float(HEAD_DIM)
        for ci in range(B_WIDTH // LANES):
            ls = slice(ci * LANES, (ci + 1) * LANES)
            d = d_ref[:, ls]
            db_ref[:, ls] = d.astype(BF16)
            dd_ref[:, ls] = _dot_hi(d * b_ref[:, ls].astype(F32), seg)

    right = pl.BlockSpec((tm, B_WIDTH), lambda i: (i, 1))
    blk = _row_spec(tm, B_WIDTH)
    return pl.pallas_call(
        body, grid=(t // tm,), in_specs=[right, right], out_specs=[blk, blk],
        out_shape=[SDS((t, B_WIDTH), BF16), SDS((t, B_WIDTH), F32)],
        name="attn_bwd_prep", compiler_params=_params())(dcat, cat)


def _attn_bwd(name, q, k, v, db, lse, dd):
    rate, length = db.shape[0], db.shape[1]
    nb = length // ATT_BLOCK
    scale = HEAD_DIM ** -0.5

    def body(qa_ref, qb_ref, k_ref, v_ref, dba_ref, dbb_ref, la_ref, lb_ref, da_ref, dbd_ref, dq_ref, dk_ref, dv_ref, carry):
        m = pl.program_id(1)

        @pl.when(m == 0)
        def _():
            carry[...] = jnp.zeros_like(carry)

        qi = lax.broadcasted_iota(jnp.int32, (ATT_BLOCK, ATT_BLOCK), 0)
        kj = lax.broadcasted_iota(jnp.int32, (ATT_BLOCK, ATT_BLOCK), 1)
        masks = (kj <= qi, (kj >= qi) & (m + 1 < nb))
        for hp in range(B_WIDTH // LANES):
            ls = slice(hp * LANES, (hp + 1) * LANES)
            k2, v2 = k_ref[:, ls], v_ref[:, ls]
            sides = ((qa_ref[:, ls], dba_ref[:, ls], la_ref[:, ls], da_ref[:, ls]),
                     (qb_ref[:, ls], dbb_ref[:, ls], lb_ref[:, ls], dbd_ref[:, ls]))
            dq = [None, None]
            dk_acc, dv_acc = None, None
            for h in range(2):
                hm = _head_lane_mask(h)
                km = jnp.where(hm, k2, jnp.zeros_like(k2))
                vm = jnp.where(hm, v2, jnp.zeros_like(v2))
                for side in range(2):
                    q2, db2, lse2, dd2 = sides[side]
                    lse_col = jnp.max(jnp.where(hm, lse2, NEG_INF), axis=1, keepdims=True)
                    dd_col = jnp.max(jnp.where(hm, dd2, NEG_INF), axis=1, keepdims=True)
                    s = _dot(q2, km, NT) * scale
                    p = jnp.where(masks[side], jnp.exp(s - lse_col), 0.0)
                    dvc = _dot(p, jnp.where(hm, db2, jnp.zeros_like(db2)), TN)
                    dp = _dot(db2, vm, NT)
                    ds = (p * (dp - dd_col) * scale).astype(BF16)
                    dqc = _dot(ds, km, NN)
                    dkc = _dot(ds, jnp.where(hm, q2, jnp.zeros_like(q2)), TN)
                    dq[side] = dqc if dq[side] is None else dq[side] + dqc
                    dk_acc = dkc if dk_acc is None else dk_acc + dkc
                    dv_acc = dvc if dv_acc is None else dv_acc + dvc
            dq_ref[:, ls] = (dq[0] + carry[:, ls]).astype(BF16)
            carry[:, ls] = dq[1]
            dk_ref[:, ls] = dk_acc.astype(BF16)
            dv_ref[:, ls] = dv_acc.astype(BF16)

    def cur(cb):
        return pl.BlockSpec((None, ATT_BLOCK, B_WIDTH), lambda r, n: (r, n, cb))

    def nxt(cb):
        return pl.BlockSpec((None, ATT_BLOCK, B_WIDTH), lambda r, n: (r, jnp.minimum(n + 1, nb - 1), cb))

    out = cur(0)
    return pl.pallas_call(
        body, grid=(rate, nb),
        in_specs=[cur(q[1]), nxt(q[1]), cur(k[1]), cur(v[1]), cur(0), nxt(0), cur(0), nxt(0), cur(0), nxt(0)],
        out_specs=[out, out, out], out_shape=[SDS((rate, length, B_WIDTH), BF16)] * 3,
        scratch_shapes=[pltpu.VMEM((ATT_BLOCK, B_WIDTH), F32)],
        name=name, compiler_params=_params())(q[0], q[0], k[0], v[0], db, db, lse, lse, dd, dd)


AB_IN = 2 * A_WIDTH + 3 * N_DIL * B_WIDTH
ASM_TILE = 256


def _dproj_assemble(proj, d_a, dqk, dv, gains, tabs):
    t = proj.shape[0]
    n_qk = 2 * N_DIL

    def body(p_ref, da_ref, *rest):
        dqk_refs = rest[:n_qk]
        dv_refs = rest[n_qk:n_qk + N_DIL]
        g_ref, c_ref, s1_ref, s2_ref, o_ref, dg_ref = rest[n_qk + N_DIL:]

        @pl.when(pl.program_id(0) == 0)
        def _():
            dg_ref[...] = jnp.zeros_like(dg_ref)

        seg = _segment_mean_matrix(HEAD_DIM)
        c, s1, s2 = c_ref[...], s1_ref[...], s2_ref[...]
        o_ref[:, :2 * A_WIDTH] = da_ref[...]
        for jg in range(n_qk):
            for ci in range(B_WIDTH // LANES):
                col = jg * B_WIDTH + ci * LANES
                src = slice(2 * A_WIDTH + col, 2 * A_WIDTH + col + LANES)
                xv = p_ref[:, src].astype(F32)
                r = lax.rsqrt(_dot_hi(xv * xv, seg) + EPS)
                xh = xv * r
                gain = g_ref[:, col:col + LANES]
                do = dqk_refs[jg][:, ci * LANES:(ci + 1) * LANES].astype(F32)
                dy = do * c + pltpu.roll(do * s1, 8, axis=1) + pltpu.roll(do * s2, LANES - 8, axis=1)
                dg_ref[:, col:col + LANES] += jnp.sum(dy * xh, axis=0, keepdims=True)
                dxh = dy * gain
                o_ref[:, src] = (r * (dxh - xh * _dot_hi(dxh * xh, seg))).astype(BF16)
        v0 = 2 * A_WIDTH + QK_COLS
        for g in range(N_DIL):
            o_ref[:, v0 + g * B_WIDTH:v0 + (g + 1) * B_WIDTH] = dv_refs[g][...]

    blk = _row_spec(ASM_TILE, B_WIDTH)
    tab = _row_spec(ASM_TILE, LANES)
    return pl.pallas_call(
        body, grid=(t // ASM_TILE,),
        in_specs=[_row_spec(ASM_TILE, AB_IN), _row_spec(ASM_TILE, 2 * A_WIDTH)] + [blk] * (n_qk + N_DIL)
        + [_const_spec((1, QK_COLS)), tab, tab, tab],
        out_specs=[_row_spec(ASM_TILE, AB_IN), _const_spec((1, QK_COLS))],
        out_shape=[SDS((t, AB_IN), BF16), SDS((1, QK_COLS), F32)],
        name="dproj_assemble", compiler_params=_params())(proj, d_a, *dqk, *dv, gains, *tabs)


def _fold_heads(dg_lane):
    n = dg_lane.shape[1]

    def body(x_ref, o_ref):
        r = lax.broadcasted_iota(jnp.int32, (B_WIDTH, B_WIDTH), 0) % HEAD_DIM
        c = lax.broadcasted_iota(jnp.int32, (B_WIDTH, B_WIDTH), 1) % HEAD_DIM
        fold = jnp.where(r == c, 1.0, 0.0).astype(F32)
        for jg in range(n // B_WIDTH):
            ls = slice(jg * B_WIDTH, (jg + 1) * B_WIDTH)
            o_ref[:, ls] = _dot_hi(jnp.broadcast_to(x_ref[:, ls], (8, B_WIDTH)), fold)

    return pl.pallas_call(body, out_shape=SDS((8, n), F32), name="fold_heads", compiler_params=_params())(dg_lane)


CD_TILE = 256
CD_IN = 2 * C_WIDTH + 3 * 512


def _cd_split(pv):
    w = C_WIDTH
    return pv[:, :w], pv[:, w:2 * w], pv[:, 2 * w:3 * w], pv[:, 3 * w:4 * w], pv[:, 4 * w:5 * w]


def _mixer_cd_fwd(proj, cw, cb, lg, lb, dw):
    t = proj.shape[0]
    per = CD_TILE // HALO

    def body(h_ref, m_ref, cw_ref, cb_ref, lg_ref, lb_ref, dw_ref, o_ref, c_scr, e_scr):
        not_first = (pl.program_id(0) > 0).astype(F32)
        ha, hg, _, hgc, hhv = _cd_split(h_ref[...].astype(F32))
        ma, mg, mgb, mgc, mhv = _cd_split(m_ref[...].astype(F32))
        c_scr[:HALO] = ha * _sigmoid(hg) * not_first
        c_scr[HALO:] = ma * _sigmoid(mg)
        e_scr[:HALO] = hgc * hhv * not_first
        e_scr[HALO:] = mgc * mhv
        acc = jnp.zeros((CD_TILE, C_WIDTH), F32)
        for k in range(C_KERNEL):
            acc = acc + cw_ref[k:k + 1, :] * c_scr[pl.ds(HALO - (C_KERNEL - 1) + k, CD_TILE), :]
        c1 = acc + cb_ref[...]
        cc = c1 - jnp.mean(c1, axis=-1, keepdims=True)
        c2 = cc * lax.rsqrt(jnp.mean(cc * cc, axis=-1, keepdims=True) + EPS) * lg_ref[...] + lb_ref[...]
        o_ref[:, :C_WIDTH] = (c2 * _sigmoid(c2)).astype(BF16)
        d1 = jnp.zeros((CD_TILE, C_WIDTH), F32)
        for k in range(D_KERNEL):
            d1 = d1 + dw_ref[k:k + 1, :] * e_scr[pl.ds(HALO - (D_KERNEL - 1) + k, CD_TILE), :]
        o_ref[:, C_WIDTH:] = (mgb * d1).astype(BF16)

    return pl.pallas_call(
        body, grid=(t // CD_TILE,),
        in_specs=[pl.BlockSpec((HALO, CD_IN), lambda i: (jnp.maximum(i * per - 1, 0), 0)), _row_spec(CD_TILE, CD_IN),
                  _const_spec((32, C_WIDTH)), _const_spec((1, C_WIDTH)), _const_spec((1, C_WIDTH)), _const_spec((1, C_WIDTH)),
                  _const_spec((8, C_WIDTH))],
        out_specs=_row_spec(CD_TILE, 2 * C_WIDTH), out_shape=SDS((t, 2 * C_WIDTH), BF16),
        scratch_shapes=[pltpu.VMEM((HALO + CD_TILE, C_WIDTH), F32)] * 2,
        name="mixer_cd_fwd", compiler_params=_params())(proj, proj, cw, cb, lg, lb, dw)


def _mixer_cd_bwd(proj, dcat, cw, cb, lg, lb, dw):
    t = proj.shape[0]
    per = CD_TILE // HALO
    nt = t // CD_TILE
    ext = CD_TILE + HALO

    def body(hp_ref, m_ref, hn_ref, dm_ref, dn_ref, cw_ref, cb_ref, lg_ref, lb_ref, dw_ref,
             dp_ref, dcw_ref, dcb_ref, dlg_ref, dlb_ref, ddw_ref, c_scr, e_scr, dc1_scr, dd1_scr):
        i = pl.program_id(0)

        @pl.when(i == 0)
        def _():
            for r in (dcw_ref, dcb_ref, dlg_ref, dlb_ref, ddw_ref):
                r[...] = jnp.zeros_like(r)

        not_first = (i > 0).astype(F32)
        not_last = (i < nt - 1).astype(F32)
        pa, pg, _, pgc, phv = _cd_split(hp_ref[...].astype(F32))
        ma, mg, mgb, mgc, mhv = _cd_split(m_ref[...].astype(F32))
        na, ng, ngb, ngc, nhv = _cd_split(hn_ref[...].astype(F32))
        sig_m = _sigmoid(mg)
        c_scr[:HALO] = pa * _sigmoid(pg) * not_first
        c_scr[HALO:HALO + CD_TILE] = ma * sig_m
        c_scr[HALO + CD_TILE:] = na * _sigmoid(ng) * not_last
        e_scr[:HALO] = pgc * phv * not_first
        e_scr[HALO:HALO + CD_TILE] = mgc * mhv
        e_scr[HALO + CD_TILE:] = ngc * nhv * not_last

        acc = jnp.zeros((ext, C_WIDTH), F32)
        for k in range(C_KERNEL):
            acc = acc + cw_ref[k:k + 1, :] * c_scr[pl.ds(HALO - (C_KERNEL - 1) + k, ext), :]
        c1 = acc + cb_ref[...]
        cc = c1 - jnp.mean(c1, axis=-1, keepdims=True)
        rs = lax.rsqrt(jnp.mean(cc * cc, axis=-1, keepdims=True) + EPS)
        vhat = cc * rs
        c2 = vhat * lg_ref[...] + lb_ref[...]
        sig = _sigmoid(c2)
        dc = jnp.concatenate([dm_ref[:, :C_WIDTH], dn_ref[:, :C_WIDTH] * not_last], axis=0)
        dc2 = dc * (sig * (1.0 + c2 * (1.0 - sig)))
        dvh = dc2 * lg_ref[...]
        dc1 = rs * (dvh - jnp.mean(dvh, axis=-1, keepdims=True) - vhat * jnp.mean(dvh * vhat, axis=-1, keepdims=True))
        dc1_scr[...] = dc1
        dlg_ref[...] += jnp.sum((dc2 * vhat)[:CD_TILE], axis=0, keepdims=True)
        dlb_ref[...] += jnp.sum(dc2[:CD_TILE], axis=0, keepdims=True)
        dc1_m = dc1[:CD_TILE]
        dcb_ref[...] += jnp.sum(dc1_m, axis=0, keepdims=True)
        dc0 = jnp.zeros((CD_TILE, C_WIDTH), F32)
        for k in range(C_KERNEL):
            dc0 = dc0 + cw_ref[k:k + 1, :] * dc1_scr[pl.ds(C_KERNEL - 1 - k, CD_TILE), :]
            dcw_ref[k:k + 1, :] += jnp.sum(dc1_m * c_scr[pl.ds(HALO - (C_KERNEL - 1) + k, CD_TILE), :], axis=0, keepdims=True)
        dp_ref[:, :C_WIDTH] = (dc0 * sig_m).astype(BF16)
        dp_ref[:, C_WIDTH:2 * C_WIDTH] = (dc0 * ma * sig_m * (1.0 - sig_m)).astype(BF16)

        d1 = jnp.zeros((CD_TILE, C_WIDTH), F32)
        for k in range(D_KERNEL):
            d1 = d1 + dw_ref[k:k + 1, :] * e_scr[pl.ds(HALO - (D_KERNEL - 1) + k, CD_TILE), :]
        dd_m = dm_ref[:, C_WIDTH:]
        dd1 = jnp.concatenate([dd_m * mgb, dn_ref[:, C_WIDTH:] * ngb * not_last], axis=0)
        dd1_scr[...] = dd1
        dp_ref[:, 2 * C_WIDTH:3 * C_WIDTH] = (dd_m * d1).astype(BF16)
        de = jnp.zeros((CD_TILE, C_WIDTH), F32)
        for k in range(D_KERNEL):
            de = de + dw_ref[k:k + 1, :] * dd1_scr[pl.ds(D_KERNEL - 1 - k, CD_TILE), :]
            ddw_ref[k:k + 1, :] += jnp.sum(dd1[:CD_TILE] * e_scr[pl.ds(HALO - (D_KERNEL - 1) + k, CD_TILE), :], axis=0, keepdims=True)
        dp_ref[:, 3 * C_WIDTH:4 * C_WIDTH] = (de * mhv).astype(BF16)
        dp_ref[:, 4 * C_WIDTH:] = (de * mgc).astype(BF16)

    halo_prev = lambda i: (jnp.maximum(i * per - 1, 0), 0)
    halo_next = lambda i: (jnp.minimum((i + 1) * per, t // HALO - 1), 0)
    vec = _const_spec((1, C_WIDTH))
    return pl.pallas_call(
        body, grid=(nt,),
        in_specs=[pl.BlockSpec((HALO, CD_IN), halo_prev), _row_spec(CD_TILE, CD_IN), pl.BlockSpec((HALO, CD_IN), halo_next),
                  _row_spec(CD_TILE, 2 * C_WIDTH), pl.BlockSpec((HALO, 2 * C_WIDTH), halo_next),
                  _const_spec((32, C_WIDTH)), vec, vec, vec, _const_spec((8, C_WIDTH))],
        out_specs=[_row_spec(CD_TILE, CD_IN), _const_spec((32, C_WIDTH)), vec, vec, vec, _const_spec((8, C_WIDTH))],
        out_shape=[SDS((t, CD_IN), BF16), SDS((32, C_WIDTH), F32), SDS((1, C_WIDTH), F32), SDS((1, C_WIDTH), F32),
                   SDS((1, C_WIDTH), F32), SDS((8, C_WIDTH), F32)],
        scratch_shapes=[pltpu.VMEM((2 * HALO + CD_TILE, C_WIDTH), F32)] * 2 + [pltpu.VMEM((ext, C_WIDTH), F32)] * 2,
        name="mixer_cd_bwd", compiler_params=_params())(proj, proj, proj, dcat, dcat, cw, cb, lg, lb, dw)


def _wgrad(name, lhs, lhs_spec, rhs_list, rhs_spec, out_rc, t, layers, layer, bufs):
    tk = TILES["wgrad"]
    r, c = out_rc
    n = len(rhs_list)
    o_spec = pl.BlockSpec((None, None, r, c), lambda p, k: (p, layer, 0, 0))
    o = (SDS((N_CHIPS, layers, r, c), F32), o_spec)
    pairs = [(lhs, lhs_spec, rhs, rhs_spec, j) for j, rhs in enumerate(rhs_list)]
    n_pairs = len(pairs)

    def body(*refs):
        ab = refs[:2 * n_pairs]
        skip = 0 if bufs is None else n
        out_refs = refs[2 * n_pairs + skip:2 * n_pairs + skip + n]
        k = pl.program_id(1)
        parts = [_dot(ab[2 * j][...], ab[2 * j + 1][...], TN) for j in range(n_pairs)]

        @pl.when(k == 0)
        def _():
            for a in range(n):
                out_refs[a][...] = parts[a]

        @pl.when(k > 0)
        def _():
            for a in range(n):
                out_refs[a][...] += parts[a]

    operands, in_specs = [], []
    for a, a_spec, b, b_spec, _ in pairs:
        operands += [a, b]
        in_specs += [a_spec, b_spec]
    aliases = {}
    if bufs is not None:
        for j, buf in enumerate(bufs):
            aliases[len(operands)] = j
            operands.append(buf)
            in_specs.append(pl.BlockSpec(memory_space=pl.ANY))
    return pl.pallas_call(
        body, grid=(N_CHIPS, t // tk), in_specs=in_specs, out_specs=[o[1]] * n, out_shape=[o[0]] * n,
        input_output_aliases=aliases, name=name, compiler_params=_params())(*operands)


def _wgrad_col_sharded(name, h, dz_list, three_d, layers=1, layer=0, bufs=None):
    t, d = h.shape
    tk = TILES["wgrad"]
    n4 = dz_list[0].shape[-1] if three_d else dz_list[0].shape[-1] // N_CHIPS
    hs = pl.BlockSpec((tk, d), lambda p, k: (k, 0))
    zs = pl.BlockSpec((None, tk, n4), lambda p, k: (p, k, 0)) if three_d else pl.BlockSpec((tk, n4), lambda p, k: (k, p))
    return _wgrad(name, h, hs, dz_list, zs, (d, n4), t, layers, layer, bufs)


def _wgrad_row_sharded(name, a, g, three_d, layers=1, layer=0, buf=None):
    t, d = g.shape
    tk = TILES["wgrad"]
    k4 = a.shape[-1] if three_d else a.shape[-1] // N_CHIPS
    a_spec = pl.BlockSpec((None, tk, k4), lambda p, k: (p, k, 0)) if three_d else pl.BlockSpec((tk, k4), lambda p, k: (k, p))
    gs = pl.BlockSpec((tk, d), lambda p, k: (k, 0))
    return _wgrad(name, a, a_spec, [g], gs, (k4, d), t, layers, layer, None if buf is None else [buf])[0]


MESH = pl.DeviceIdType.MESH
ANY = pl.BlockSpec(memory_space=pl.ANY)


def _position():
    x, y, c = lax.axis_index("x"), lax.axis_index("y"), lax.axis_index("c")
    others = [(1 - x, y), (x, 1 - y), (1 - x, 1 - y)]
    return x, y, c, 2 * x + y, others


def _mesh_scalars():
    return jnp.stack([lax.axis_index("c"), 2 * lax.axis_index("x") + lax.axis_index("y")]).astype(jnp.int32)


def _stage_own(name, shard, dtype):
    _, h, cols = shard.shape

    def body(s_ref, x_ref, o_ref):
        o_ref[...] = x_ref[...].astype(dtype)

    return pl.pallas_call(
        body,
        grid_spec=pltpu.PrefetchScalarGridSpec(
            num_scalar_prefetch=1, grid=(2,),
            in_specs=[pl.BlockSpec((None, h, cols), lambda i, s: (i, 0, 0))],
            out_specs=pl.BlockSpec((None, None, h, cols), lambda i, s: (s[1], i, 0, 0))),
        out_shape=SDS((N_CHIPS, 2, h, cols), dtype), name=name, compiler_params=_params())(_mesh_scalars(), shard)


def _gather_weights(bufs):
    n = len(bufs)

    def body(*refs):
        outs = refs[n:2 * n]
        send_sems, recv_sems = refs[2 * n:]
        x, y, c, p, others = _position()
        sibling = (x, y, 1 - c)
        sends = []
        for t in range(n):
            for j, (qx, qy) in enumerate(others):
                mine = outs[t].at[p, c]
                cp = pltpu.make_async_remote_copy(mine, mine, send_sems.at[t, j], recv_sems.at[t, j],
                                                  device_id=(qx, qy, c), device_id_type=MESH)
                cp.start()
                sends.append(cp)
        for t in range(n):
            for j, (qx, qy) in enumerate(others):
                landed = outs[t].at[2 * qx + qy, c]
                pltpu.make_async_remote_copy(landed, landed, send_sems.at[t, j], recv_sems.at[t, j],
                                             device_id=(qx, qy, c), device_id_type=MESH).wait_recv()
                cp = pltpu.make_async_remote_copy(landed, landed, send_sems.at[t, 3 + j], recv_sems.at[t, 3 + j],
                                                  device_id=sibling, device_id_type=MESH)
                cp.start()
                sends.append(cp)
        for t in range(n):
            for j, (qx, qy) in enumerate(others):
                passed = outs[t].at[2 * qx + qy, 1 - c]
                pltpu.make_async_remote_copy(passed, passed, send_sems.at[t, 3 + j], recv_sems.at[t, 3 + j],
                                             device_id=sibling, device_id_type=MESH).wait_recv()
        for cp in sends:
            cp.wait_send()

    return pl.pallas_call(
        body, in_specs=[ANY] * n, out_specs=[ANY] * n, out_shape=[SDS(s.shape, s.dtype) for s in bufs],
        input_output_aliases={t: t for t in range(n)},
        scratch_shapes=[pltpu.SemaphoreType.DMA((n, 6)), pltpu.SemaphoreType.DMA((n, 6))],
        name="gather_weights")(*bufs)


def _swap_halves(tensors):
    n = len(tensors)

    def body(*refs):
        ins, outs = refs[:n], refs[n:2 * n]
        send_sems, recv_sems = refs[2 * n:]
        x, y, c, _, _ = _position()
        copies = []
        for t in range(n):
            cp = pltpu.make_async_remote_copy(ins[t].at[:, 1 - c], outs[t], send_sems.at[t], recv_sems.at[t],
                                              device_id=(x, y, 1 - c), device_id_type=MESH)
            cp.start()
            copies.append(cp)
        for cp in copies:
            cp.wait()

    return pl.pallas_call(
        body, in_specs=[ANY] * n, out_specs=[ANY] * n,
        out_shape=[SDS((s.shape[0],) + s.shape[2:], s.dtype) for s in tensors],
        scratch_shapes=[pltpu.SemaphoreType.DMA((n,)), pltpu.SemaphoreType.DMA((n,))],
        name="swap_halves")(*tensors)


def _scatter_chips(tensors, landing):
    n = len(tensors)

    def body(*refs):
        ins, outs = refs[:n], refs[2 * n:3 * n]
        send_sems, recv_sems = refs[3 * n:]
        x, y, c, p, others = _position()
        copies = []
        for t in range(n):
            for j, (qx, qy) in enumerate(others):
                cp = pltpu.make_async_remote_copy(ins[t].at[2 * qx + qy], outs[t].at[p], send_sems.at[t, j], recv_sems.at[t, j],
                                                  device_id=(qx, qy, c), device_id_type=MESH)
                cp.start()
                copies.append((cp, t, j, 2 * qx + qy))
        for cp, t, j, q in copies:
            cp.wait_send()
            pltpu.make_async_remote_copy(outs[t].at[q], outs[t].at[q], send_sems.at[t, j], recv_sems.at[t, j],
                                         device_id=(x, y, c), device_id_type=MESH).wait_recv()

    return pl.pallas_call(
        body, in_specs=[ANY] * (2 * n), out_specs=[ANY] * n, out_shape=[SDS(s.shape, s.dtype) for s in landing],
        input_output_aliases={n + t: t for t in range(n)},
        scratch_shapes=[pltpu.SemaphoreType.DMA((n, 3)), pltpu.SemaphoreType.DMA((n, 3))],
        name="scatter_chips")(*tensors, *landing)


def _join_halves(bufs):
    n = len(bufs)

    def body(*refs):
        outs = refs[n:2 * n]
        send_sems, recv_sems = refs[2 * n:]
        x, y, c, _, _ = _position()
        copies = []
        for t in range(n):
            cp = pltpu.make_async_remote_copy(outs[t].at[c], outs[t].at[c], send_sems.at[t], recv_sems.at[t],
                                              device_id=(x, y, 1 - c), device_id_type=MESH)
            cp.start()
            copies.append((cp, t))
        for cp, t in copies:
            cp.wait_send()
            theirs = outs[t].at[1 - c]
            pltpu.make_async_remote_copy(theirs, theirs, send_sems.at[t], recv_sems.at[t],
                                         device_id=(x, y, 1 - c), device_id_type=MESH).wait_recv()

    return pl.pallas_call(
        body, in_specs=[ANY] * n, out_specs=[ANY] * n, out_shape=[SDS(s.shape, s.dtype) for s in bufs],
        input_output_aliases={t: t for t in range(n)},
        scratch_shapes=[pltpu.SemaphoreType.DMA((n,)), pltpu.SemaphoreType.DMA((n,))],
        name="join_halves")(*bufs)


def _add_own_half(name, full, recv, out_dtype):
    n4, _, h, cols = full.shape

    def body(s_ref, a_ref, b_ref, o_ref, own_ref):
        v = (a_ref[...] + b_ref[...]).astype(out_dtype)
        o_ref[...] = v

        @pl.when(pl.program_id(0) == s_ref[1])
        def _():
            own_ref[...] = v

    return pl.pallas_call(
        body,
        grid_spec=pltpu.PrefetchScalarGridSpec(
            num_scalar_prefetch=1, grid=(n4,),
            in_specs=[pl.BlockSpec((None, None, h, cols), lambda q, s: (q, s[0], 0, 0)),
                      pl.BlockSpec((None, h, cols), lambda q, s: (q, 0, 0))],
            out_specs=[pl.BlockSpec((None, h, cols), lambda q, s: (q, 0, 0)),
                       pl.BlockSpec((None, h, cols), lambda q, s: (s[1], 0, 0))]),
        out_shape=[SDS((n4, h, cols), out_dtype)] * 2, name=name, compiler_params=_params())(_mesh_scalars(), full, recv)


def _sum_chips(name, parts):
    n4, h, cols = parts.shape
    th = h // 4 if h % 64 == 0 else h

    def body(s_ref, a_ref, o_ref):
        acc = a_ref[0].astype(F32)
        for q in range(1, n4):
            acc = acc + a_ref[q].astype(F32)
        o_ref[...] = acc

    return pl.pallas_call(
        body,
        grid_spec=pltpu.PrefetchScalarGridSpec(
            num_scalar_prefetch=1, grid=(h // th,),
            in_specs=[pl.BlockSpec((n4, th, cols), lambda i, s: (0, i, 0))],
            out_specs=pl.BlockSpec((None, th, cols), lambda i, s: (s[0], i, 0))),
        out_shape=SDS((2, h, cols), F32), name=name, compiler_params=_params())(_mesh_scalars(), parts)


def _adamw_math(w, g, m, v):
    m2 = ADAM_B1 * m + (1.0 - ADAM_B1) * g
    v2 = ADAM_B2 * v + (1.0 - ADAM_B2) * (g * g)
    m_hat = m2 / (1.0 - ADAM_B1 ** ADAM_STEP)
    v_hat = v2 / (1.0 - ADAM_B2 ** ADAM_STEP)
    delta = -ADAM_LR * (m_hat / (jnp.sqrt(v_hat) + ADAM_EPS) + ADAM_WD * w)
    return delta, m2, v2


def _row_tile(rows, cols):
    cap = max(8, (1 << 18) // cols)
    best = 8
    for cand in range(8, min(rows, cap) + 1, 8):
        if rows % cand == 0:
            best = cand
    return best


def _adamw_big(name, w, g, m, v):
    shape = w.shape
    cols = shape[-1]
    rows = math.prod(shape[:-1])
    w2, g2, m2, v2 = (a.reshape(rows, cols) for a in (w, g, m, v))
    tr = _row_tile(rows, cols)

    def body(w_ref, g_ref, m_ref, v_ref, d_ref, mo_ref, vo_ref):
        d, mm, vv = _adamw_math(w_ref[...], g_ref[...], m_ref[...], v_ref[...])
        d_ref[...] = d
        mo_ref[...] = mm
        vo_ref[...] = vv

    blk = _row_spec(tr, cols)
    outs = pl.pallas_call(
        body, grid=(rows // tr,), in_specs=[blk] * 4, out_specs=[blk] * 3, out_shape=[SDS((rows, cols), F32)] * 3,
        name=name, compiler_params=_params())(w2, g2, m2, v2)
    return tuple(o.reshape(shape) for o in outs)


def _adamw_small(ws, gs, ms, vs):
    n = len(ws)
    flat = []
    for group in (ws, gs, ms, vs):
        flat += [a.reshape(-1, a.shape[-1]) for a in group]

    def body(*refs):
        w_r, g_r, m_r, v_r = refs[:n], refs[n:2 * n], refs[2 * n:3 * n], refs[3 * n:4 * n]
        d_o, m_o, v_o = refs[4 * n:5 * n], refs[5 * n:6 * n], refs[6 * n:7 * n]
        for j in range(n):
            d, mm, vv = _adamw_math(w_r[j][...], g_r[j][...], m_r[j][...], v_r[j][...])
            d_o[j][...] = d
            m_o[j][...] = mm
            v_o[j][...] = vv

    shapes = [SDS(a.shape, F32) for a in flat[:n]]
    outs = pl.pallas_call(body, out_shape=shapes * 3, name="adamw_small", compiler_params=_params())(*flat)
    res = []
    for k in range(3):
        res.append([outs[k * n + j].reshape(ws[j].shape) for j in range(n)])
    return res


BIG = ("ab_w_in", "ab_w_out", "cd_w_in", "cd_w_out", "ffn_w_gate", "ffn_w_up", "ffn_w_down")
V_BLOCK = (2 * A_WIDTH + QK_COLS) // B_WIDTH


def _pad_rows(a, rows):
    return jnp.pad(a, ((0, rows - a.shape[0]), (0, 0)))


def _local_step(x, target, w, sp):
    t, d = x.shape
    tabs = _rope_tables(t)
    gains = jnp.concatenate([jnp.tile(sp["q_norm_g"][g], HEAD_DIM // 8) for g in range(N_DIL)]
                            + [jnp.tile(sp["k_norm_g"][g], HEAD_DIM // 8) for g in range(N_DIL)]).reshape(1, QK_COLS)
    bias_t = sp["sgu_bias"].T
    cw = _pad_rows(sp["conv_c_w"], 32)
    dw = _pad_rows(sp["conv_d_w"], 8)
    cb, clg, clb = (sp[k].reshape(1, C_WIDTH) for k in ("conv_c_b", "c_ln_g", "c_ln_b"))
    slg, slb = sp["sgu_norm_g"].reshape(1, A_WIDTH), sp["sgu_norm_b"].reshape(1, A_WIDTH)
    g_ab, g_cd = sp["ab_norm_g"].reshape(1, d), sp["cd_norm_g"].reshape(1, d)
    g_f0, g_f1 = sp["ffn_norm_g"][0:1], sp["ffn_norm_g"][1:2]
    wo = w["ab_w_out"].reshape(-1, d)
    wo2 = w["cd_w_out"].reshape(-1, d)
    wg, wu, wd = w["ffn_w_gate"], w["ffn_w_up"], w["ffn_w_down"]

    h0 = _rms_fwd("rms_ab", x, g_ab)
    proj = _proj_in("proj_ab", h0, w["ab_w_in"], 0)
    a_out = _mixer_a_fwd(proj, slg, slb, sp["sgu_w"], bias_t)
    qk = _qk_fwd(proj, gains, tabs)
    qkv, o_list, l_list = [], [], []
    for g, rate in enumerate(DIL_RATES):
        if rate == 1:
            qk3, proj3 = qk.reshape(1, t, QK_COLS), proj.reshape(1, t, AB_IN)
            q, k, v = (qk3, g), (qk3, N_DIL + g), (proj3, V_BLOCK + g)
        else:
            qp, kp, vp = _permute(f"perm_fwd_{g}", [(qk, g), (qk, N_DIL + g), (proj, V_BLOCK + g)], rate)
            q, k, v = (qp, 0), (kp, 0), (vp, 0)
        qkv.append((q, k, v))
        o, l = _attn_fwd(f"attn_fwd_{g}", q, k, v)
        if rate == 1:
            o, l = o.reshape(t, B_WIDTH), l.reshape(t, B_WIDTH)
        else:
            o, l = _unpermute(f"unperm_fwd_{g}", [o, l], rate)
        o_list.append(o)
        l_list.append(l)
    cat, lse_tot = _attn_merge(a_out, o_list, l_list)
    x1, hf0 = _proj_out("out_ab", cat, wo, x, g_next=g_f0)
    gate0, up0, act0 = _ffn_in("ffn_in_0", hf0, wg, wu, 0)
    x2, h1 = _ffn_out("ffn_out_0", act0, wd, 0, x1, g_next=g_cd)
    projcd = _proj_in("proj_cd", h1, w["cd_w_in"], 0)
    cat2 = _mixer_cd_fwd(projcd, cw, cb, clg, clb, dw)
    x3, hf1 = _proj_out("out_cd", cat2, wo2, x2, g_next=g_f1)
    gate1, up1, act1 = _ffn_in("ffn_in_1", hf1, wg, wu, 1)
    dy, loss_acc, dy_b = _ffn_out("ffn_out_1", act1, wd, 1, x3, target=target)
    loss = 0.5 * loss_acc[0, 0] / d

    grads = {}
    dgate, dup = _ffn_dact("ffn_dact_1", dy_b, wd, 1, gate1, up1)
    d_down = _wgrad_row_sharded("wgrad_down_1", act1, dy_b, True, layers=2, layer=1)
    d_gate, d_up = _wgrad_col_sharded("wgrad_gate_up_1", hf1, [dgate, dup], True, layers=2, layer=1)
    g3, d_f1, g3_b = _dgrad_cols("dgrad_ffn_1", [dgate, dup], [wg, wu], 1, True, x3, g_f1, dy)

    dcat2 = _dgrad_rows("dgrad_out_cd", g3_b, wo2)
    grads["cd_w_out"] = _wgrad_row_sharded("wgrad_out_cd", cat2, g3_b, False)
    dprojcd, d_cw, d_cb, d_clg, d_clb, d_dw = _mixer_cd_bwd(projcd, dcat2, cw, cb, clg, clb, dw)
    grads["cd_w_in"] = _wgrad_col_sharded("wgrad_in_cd", h1, [dprojcd], False)[0]
    g2, d_cdn, g2_b = _dgrad_cols("dgrad_in_cd", [dprojcd], [w["cd_w_in"]], 0, False, x2, g_cd, g3)

    dgate, dup = _ffn_dact("ffn_dact_0", g2_b, wd, 0, gate0, up0)
    grads["ffn_w_down"] = _wgrad_row_sharded("wgrad_down_0", act0, g2_b, True, layers=2, layer=0, buf=d_down)
    grads["ffn_w_gate"], grads["ffn_w_up"] = _wgrad_col_sharded(
        "wgrad_gate_up_0", hf0, [dgate, dup], True, layers=2, layer=0, bufs=[d_gate, d_up])
    g1, d_f0, g1_b = _dgrad_cols("dgrad_ffn_0", [dgate, dup], [wg, wu], 0, True, x1, g_f0, g2)

    dcat = _dgrad_rows("dgrad_out_ab", g1_b, wo)
    grads["ab_w_out"] = _wgrad_row_sharded("wgrad_out_ab", cat, g1_b, False)
    d_a, d_sw, d_sbt, d_slg, d_slb = _mixer_a_bwd(proj, dcat, slg, slb, sp["sgu_w"], bias_t)
    dbb, dd = _attn_bwd_prep(dcat, cat)
    dqs, dks, dvs = [], [], []
    for g, rate in enumerate(DIL_RATES):
        q, k, v = qkv[g]
        if rate == 1:
            db3, l3, dd3 = (a.reshape(1, t, B_WIDTH) for a in (dbb, lse_tot, dd))
        else:
            db3, l3, dd3 = _permute(f"perm_bwd_{g}", [(dbb, 0), (lse_tot, 0), (dd, 0)], rate)
        dq, dk, dv = _attn_bwd(f"attn_bwd_{g}", q, k, v, db3, l3, dd3)
        if rate == 1:
            dq, dk, dv = (a.reshape(t, B_WIDTH) for a in (dq, dk, dv))
        else:
            dq, dk, dv = _unpermute(f"unperm_bwd_{g}", [dq, dk, dv], rate)
        dqs.append(dq)
        dks.append(dk)
        dvs.append(dv)
    dproj, d_gains = _dproj_assemble(proj, d_a, dqs + dks, dvs, gains, tabs)
    d_gains = _fold_heads(d_gains)[0].reshape(2, N_DIL, B_WIDTH)[:, :, :HEAD_DIM]
    grads["ab_w_in"] = _wgrad_col_sharded("wgrad_in_ab", h0, [dproj], False)[0]
    gx, d_abn = _dgrad_cols("dgrad_in_ab", [dproj], [w["ab_w_in"]], 0, False, x, g_ab, g1, bf16_copy=False)

    small = {
        "ab_norm_g": d_abn, "sgu_norm_g": d_slg, "sgu_norm_b": d_slb, "sgu_w": d_sw, "sgu_bias": d_sbt.T,
        "q_norm_g": d_gains[0], "k_norm_g": d_gains[1], "cd_norm_g": d_cdn, "conv_c_w": d_cw[:C_KERNEL],
        "conv_c_b": d_cb, "c_ln_g": d_clg, "c_ln_b": d_clb, "conv_d_w": d_dw[:D_KERNEL],
        "ffn_norm_g": jnp.concatenate([d_f0, d_f1], axis=0),
    }
    return loss, gx, grads, small


SHARDED_SMALL = ("cd_norm_g", "conv_c_w", "conv_c_b", "c_ln_g", "c_ln_b", "conv_d_w")
SHARDED_ROWS = 48
REPLICATED_SMALL = ("ab_norm_g", "sgu_norm_g", "sgu_norm_b", "sgu_w", "sgu_bias", "q_norm_g", "k_norm_g", "ffn_norm_g")
REPLICATED_ROWS = 560


def _pack_sharded(parts):
    rows = [parts[k].reshape(-1, LANES) for k in SHARDED_SMALL]
    return _pad_rows(jnp.concatenate(rows, axis=0), SHARDED_ROWS)


def _split_full_small(small):
    per_chip = []
    for q in range(N_CHIPS):
        parts = {}
        for k in SHARDED_SMALL:
            a = small[k]
            a = a.reshape(-1, a.shape[-1])
            n = a.shape[-1] // N_CHIPS
            parts[k] = a[:, q * n:(q + 1) * n]
        per_chip.append(_pack_sharded(parts))
    return jnp.stack(per_chip)


def _unpack_sharded(pack, shapes):
    out, r = {}, 0
    for k in SHARDED_SMALL:
        n = math.prod(shapes[k]) // LANES
        out[k] = pack[r:r + n].reshape(shapes[k])
        r += n
    return out


def _gathered_small(packs, shapes):
    per_chip = [_unpack_sharded(packs[q], shapes) for q in range(N_CHIPS)]
    return {k: jnp.concatenate([pc[k] for pc in per_chip], axis=-1) for k in SHARDED_SMALL}


def _pack_replicated(small):
    rows = []
    for k in REPLICATED_SMALL:
        a = small[k].reshape(-1)
        a = jnp.pad(a, (0, (-a.shape[0]) % LANES))
        rows.append(a.reshape(-1, LANES))
    return _pad_rows(jnp.concatenate(rows, axis=0), REPLICATED_ROWS)


def _unpack_replicated(pack, shapes):
    out, r = {}, 0
    for k in REPLICATED_SMALL:
        size = math.prod(shapes[k])
        n = -(-size // LANES)
        out[k] = pack[r:r + n].reshape(-1)[:size].reshape(shapes[k])
        r += n
    return out


def _two_halves(a):
    rows = math.prod(a.shape[:-1])
    return a.reshape(2, rows // 2, a.shape[-1])


def _reduce_gradients(tensors, bf16_payload):
    tensors = [a.reshape((N_CHIPS, 2, math.prod(a.shape[1:-1]) // 2, a.shape[-1])) for a in tensors]
    recv = _swap_halves(tensors)
    halves = [_add_own_half(f"pair_sum_{j}", a, b, BF16 if bf else F32)
              for j, (a, b, bf) in enumerate(zip(tensors, recv, bf16_payload))]
    parts = _scatter_chips([h[0] for h in halves], [h[1] for h in halves])
    sums = [_sum_chips(f"chip_sum_{j}", a) for j, a in enumerate(parts)]
    return _join_halves(sums)


WEIGHT_ORDER = ("ab_norm_g", "ab_w_in", "sgu_norm_g", "sgu_norm_b", "sgu_w", "sgu_bias", "q_norm_g", "k_norm_g", "ab_w_out",
                "cd_norm_g", "cd_w_in", "conv_c_w", "conv_c_b", "c_ln_g", "c_ln_b", "conv_d_w", "cd_w_out", "ffn_norm_g",
                "ffn_w_gate", "ffn_w_up", "ffn_w_down")


def kernel(x, ab_norm_g, ab_w_in, sgu_norm_g, sgu_norm_b, sgu_w, sgu_bias, q_norm_g, k_norm_g, ab_w_out, cd_norm_g, cd_w_in, conv_c_w, conv_c_b, c_ln_g, c_ln_b, conv_d_w, cd_w_out, ffn_norm_g, ffn_w_gate, ffn_w_up, ffn_w_down, loss_target, m_ab_norm_g, m_ab_w_in, m_sgu_norm_g, m_sgu_norm_b, m_sgu_w, m_sgu_bias, m_q_norm_g, m_k_norm_g, m_ab_w_out, m_cd_norm_g, m_cd_w_in, m_conv_c_w, m_conv_c_b, m_c_ln_g, m_c_ln_b, m_conv_d_w, m_cd_w_out, m_ffn_norm_g, m_ffn_w_gate, m_ffn_w_up, m_ffn_w_down, v_ab_norm_g, v_ab_w_in, v_sgu_norm_g, v_sgu_norm_b, v_sgu_w, v_sgu_bias, v_q_norm_g, v_k_norm_g, v_ab_w_out, v_cd_norm_g, v_cd_w_in, v_conv_c_w, v_conv_c_b, v_c_ln_g, v_c_ln_b, v_conv_d_w, v_cd_w_out, v_ffn_norm_g, v_ffn_w_gate, v_ffn_w_up, v_ffn_w_down):
    args = dict(locals())
    ws = {k: args[k] for k in WEIGHT_ORDER}
    ms = {k: args["m_" + k] for k in WEIGHT_ORDER}
    vs = {k: args["v_" + k] for k in WEIGHT_ORDER}
    small_names = [k for k in WEIGHT_ORDER if k not in BIG]

    own_small = _pack_sharded({k: ws[k][0] for k in SHARDED_SMALL})
    gathered = _gather_weights([_stage_own("stage_" + k, _two_halves(ws[k]), BF16) for k in BIG]
                               + [_stage_own("stage_small", _two_halves(own_small), F32)])
    w_full = {k: g.reshape((N_CHIPS,) + ws[k].shape) for k, g in zip(BIG, gathered)}
    sp = _gathered_small(gathered[-1].reshape(N_CHIPS, SHARDED_ROWS, LANES), {k: ws[k].shape[1:] for k in SHARDED_SMALL})
    for k in REPLICATED_SMALL:
        sp[k] = ws[k] if k == "ffn_norm_g" else ws[k][0]

    loss, grad_x, g_big, g_small = _local_step(x[0], loss_target[0], w_full, sp)

    g_rep = jnp.broadcast_to(_pack_replicated(g_small), (N_CHIPS, REPLICATED_ROWS, LANES))
    g_sh = _split_full_small(g_small)
    reduced = _reduce_gradients([g_big[k] for k in BIG] + [g_sh, g_rep], [True] * len(BIG) + [False, False])
    grad = {k: r.reshape(ws[k].shape) for k, r in zip(BIG, reduced)}
    grad.update(_unpack_sharded(reduced[-2].reshape(SHARDED_ROWS, LANES), {k: ws[k].shape for k in SHARDED_SMALL}))
    grad.update(_unpack_replicated(reduced[-1].reshape(REPLICATED_ROWS, LANES), {k: ws[k].shape for k in REPLICATED_SMALL}))

    delta, new_m, new_v = {}, {}, {}
    for k in BIG:
        delta[k], new_m[k], new_v[k] = _adamw_big("adamw_" + k, ws[k], grad[k], ms[k], vs[k])
    d_s, m_s, v_s = _adamw_small([ws[k] for k in small_names], [grad[k] for k in small_names],
                                 [ms[k] for k in small_names], [vs[k] for k in small_names])
    for j, k in enumerate(small_names):
        delta[k], new_m[k], new_v[k] = d_s[j], m_s[j], v_s[j]

    loss = lax.psum(loss, ("x", "y", "c"))
    return (loss, grad_x[None], *[grad[k] for k in WEIGHT_ORDER], *[delta[k] for k in WEIGHT_ORDER],
            *[new_m[k] for k in WEIGHT_ORDER], *[new_v[k] for k in WEIGHT_ORDER])
```

```python
import functools
import math

import jax
import jax.numpy as jnp
from jax import lax
from jax.experimental import pallas as pl
from jax.experimental.pallas import tpu as pltpu

F32 = jnp.float32
BF16 = jnp.bfloat16
SDS = jax.ShapeDtypeStruct

N_CHIPS = 4
EPS = 1e-6
NEG_INF = -1e30
CHUNK = 128
A_GROUPS = 4
A_WIDTH = 512
N_DIL = 3
DIL_RATES = (1, 4, 16)
HEAD_DIM = 64
B_WIDTH = 512
ROPE_DIM = 16
ROPE_THETA = 500000.0
C_WIDTH = 512
C_KERNEL = 31
D_KERNEL = 3
HALO = 32
ATT_BLOCK = 128
LANES = 128

ADAM_LR = 0.001
ADAM_B1 = 0.9
ADAM_B2 = 0.999
ADAM_EPS = 1e-08
ADAM_WD = 0.01
ADAM_STEP = 10

VMEM_LIMIT = 56 * 1024 * 1024

NN = (((1,), (0,)), ((), ()))
NT = (((1,), (1,)), ((), ()))
TN = (((0,), (0,)), ((), ()))

TILES = {"proj_in": 1024, "proj_out": 1024, "ffn_in": 1024, "ffn_out": 512, "ffn_dact": 512, "dgrad_cols": 512,
         "dgrad_rows": 1024, "wgrad": 2048}


def _params(sem=None):
    return pltpu.CompilerParams(dimension_semantics=sem, vmem_limit_bytes=VMEM_LIMIT)


def _bf(v):
    return v if v.dtype == BF16 else v.astype(BF16)


def _dot(a, b, dims):
    return lax.dot_general(_bf(a), _bf(b), dims, preferred_element_type=F32)


def _dot_hi(a, b):
    return jnp.dot(a, b, precision=lax.Precision.HIGHEST, preferred_element_type=F32)


def _sigmoid(v):
    return 0.5 * jnp.tanh(0.5 * v) + 0.5


def _gelu(v):
    return 0.5 * v * (1.0 + lax.erf(v * (1.0 / math.sqrt(2.0))))


def _gelu_grad(v):
    cdf = 0.5 * (1.0 + lax.erf(v * (1.0 / math.sqrt(2.0))))
    return cdf + v * jnp.exp(-0.5 * v * v) * (1.0 / math.sqrt(2.0 * math.pi))


def _segment_mean_matrix(seg):
    r = lax.broadcasted_iota(jnp.int32, (LANES, LANES), 0) // seg
    c = lax.broadcasted_iota(jnp.int32, (LANES, LANES), 1) // seg
    return jnp.where(r == c, 1.0 / seg, 0.0).astype(F32)


MESH = pl.DeviceIdType.MESH
ANY = pl.BlockSpec(memory_space=pl.ANY)


def _position():
    x, y, c = lax.axis_index("x"), lax.axis_index("y"), lax.axis_index("c")
    others = [(1 - x, y), (x, 1 - y), (1 - x, 1 - y)]
    return x, y, c, 2 * x + y, others


class _Ride:
    def __init__(self, ins, bufs, new_outs, sem_shapes, start, finish):
        self.ins, self.bufs, self.new_outs, self.sem_shapes = list(ins), list(bufs), list(new_outs), list(sem_shapes)
        self.start, self.finish = start, finish


def _ride_both(a, b):
    na = (len(a.ins), len(a.bufs), len(a.new_outs), len(a.sem_shapes))

    def split(ins, bufs, new, sems):
        return ((ins[:na[0]], bufs[:na[1]], new[:na[2]], sems[:na[3]]), (ins[na[0]:], bufs[na[1]:], new[na[2]:], sems[na[3]:]))

    def start(*refs):
        ra, rb = split(*refs)
        a.start(*ra)
        b.start(*rb)

    def finish(*refs):
        ra, rb = split(*refs)
        a.finish(*ra)
        b.finish(*rb)

    return _Ride(a.ins + b.ins, a.bufs + b.bufs, a.new_outs + b.new_outs, a.sem_shapes + b.sem_shapes, start, finish)


def _call(body, *, grid, in_specs, out_specs, out_shape, operands, name, scratch_shapes=(), aliases=None, ride=None):
    if ride is None:
        return pl.pallas_call(body, grid=grid, in_specs=in_specs, out_specs=out_specs, out_shape=out_shape,
                              scratch_shapes=list(scratch_shapes), input_output_aliases=aliases or {}, name=name,
                              compiler_params=_params())(*operands)
    multi = isinstance(out_shape, (list, tuple))
    out_shapes = list(out_shape) if multi else [out_shape]
    o_specs = list(out_specs) if multi else [out_specs]
    n_in, n_out, n_scr = len(operands), len(out_shapes), len(scratch_shapes)
    n_ri, n_rb, n_rn = len(ride.ins), len(ride.bufs), len(ride.new_outs)

    def carrying(*refs):
        k = n_in
        r_ins = refs[k:k + n_ri]
        k += n_ri + n_rb
        outs = refs[k:k + n_out]
        k += n_out
        r_bufs = refs[k:k + n_rb]
        k += n_rb
        r_new = refs[k:k + n_rn]
        k += n_rn
        scratch = refs[k:k + n_scr]
        sems = refs[k + n_scr:]
        first, last = None, None
        for axis, size in enumerate(grid):
            pid = pl.program_id(axis)
            first = (pid == 0) if first is None else first & (pid == 0)
            last = (pid == size - 1) if last is None else last & (pid == size - 1)

        @pl.when(first)
        def _():
            ride.start(r_ins, r_bufs, r_new, sems)

        body(*refs[:n_in], *outs, *scratch)

        @pl.when(last)
        def _():
            ride.finish(r_ins, r_bufs, r_new, sems)

    all_aliases = dict(aliases or {})
    for j in range(n_rb):
        all_aliases[n_in + n_ri + j] = n_out + j
    res = pl.pallas_call(
        carrying, grid=grid, in_specs=list(in_specs) + [ANY] * (n_ri + n_rb), out_specs=o_specs + [ANY] * (n_rb + n_rn),
        out_shape=out_shapes + [SDS(b.shape, b.dtype) for b in ride.bufs] + ride.new_outs,
        scratch_shapes=list(scratch_shapes) + [pltpu.SemaphoreType.DMA(s) for s in ride.sem_shapes],
        input_output_aliases=all_aliases, name=name, compiler_params=_params())(*operands, *ride.ins, *ride.bufs)
    outs = res[:n_out]
    return (list(outs) if multi else outs[0]), list(res[n_out:])


def _run_ride(name, ride):
    n_ri, n_rb, n_rn = len(ride.ins), len(ride.bufs), len(ride.new_outs)

    def body(*refs):
        r_ins = refs[:n_ri]
        r_bufs = refs[n_ri + n_rb:n_ri + 2 * n_rb]
        r_new = refs[n_ri + 2 * n_rb:n_ri + 2 * n_rb + n_rn]
        sems = refs[n_ri + 2 * n_rb + n_rn:]
        ride.start(r_ins, r_bufs, r_new, sems)
        ride.finish(r_ins, r_bufs, r_new, sems)

    return list(pl.pallas_call(
        body, in_specs=[ANY] * (n_ri + n_rb), out_specs=[ANY] * (n_rb + n_rn),
        out_shape=[SDS(b.shape, b.dtype) for b in ride.bufs] + ride.new_outs,
        scratch_shapes=[pltpu.SemaphoreType.DMA(s) for s in ride.sem_shapes],
        input_output_aliases={n_ri + j: j for j in range(n_rb)}, name=name)(*ride.ins, *ride.bufs))


def _whole(ref, p):
    return ref[...]


def _slab(ref, p):
    return ref[p]


def _matmul(name, grid, pairs, extras, outs, dims, epi, *, slabs=1, n_acc=1, ride=None):
    n_pairs, n_ex, n_out = len(pairs), len(extras), len(outs)

    def body(*refs):
        ab = refs[:2 * n_pairs]
        ex = refs[2 * n_pairs:2 * n_pairs + n_ex]
        out_refs = refs[2 * n_pairs + n_ex:2 * n_pairs + n_ex + n_out]
        pids = tuple(pl.program_id(a) for a in range(len(grid)))
        parts = [None] * n_acc
        for p in range(slabs):
            for j, (_, _, a_pick, _, _, b_pick, acc) in enumerate(pairs):
                d = _dot(a_pick(ab[2 * j], p), b_pick(ab[2 * j + 1], p), dims)
                parts[acc] = d if parts[acc] is None else parts[acc] + d
        epi(parts, ex, out_refs, pids)

    operands, in_specs = [], []
    for a, a_spec, _, b, b_spec, _, _ in pairs:
        operands += [a, b]
        in_specs += [a_spec, b_spec]
    for e, e_spec in extras:
        operands.append(e)
        in_specs.append(e_spec)
    return _call(body, grid=grid, in_specs=in_specs, out_specs=[o[1] for o in outs], out_shape=[o[0] for o in outs],
                 operands=operands, name=name, ride=ride)


def _rms_rows(v, g):
    r = lax.rsqrt(jnp.mean(v * v, axis=-1, keepdims=True) + EPS)
    return v * r * g


def _rms_fwd(name, x, g):
    t, d = x.shape
    tm = 512

    def body(x_ref, g_ref, o_ref):
        o_ref[...] = _rms_rows(x_ref[...], g_ref[...]).astype(BF16)

    return pl.pallas_call(
        body, grid=(t // tm,),
        in_specs=[pl.BlockSpec((tm, d), lambda i: (i, 0)), pl.BlockSpec((1, d), lambda i: (0, 0))],
        out_specs=pl.BlockSpec((tm, d), lambda i: (i, 0)), out_shape=SDS((t, d), BF16), name=name,
        compiler_params=_params())(x, g)


def _epi_residual_norm(accs, ex, outs, pids):
    x_new = accs[0] + ex[0][...]
    outs[0][...] = x_new
    outs[1][...] = _rms_rows(x_new, ex[1][...]).astype(BF16)


def _epi_residual_loss(accs, ex, outs, pids):
    y = accs[0] + ex[0][...]
    err = y - ex[1][...]
    dy = err * (1.0 / err.shape[-1])
    outs[0][...] = dy
    outs[2][...] = dy.astype(BF16)

    @pl.when(pids[0] == 0)
    def _():
        outs[1][...] = jnp.zeros_like(outs[1])

    outs[1][...] += jnp.sum(err * err)


def _epi_rms_bwd(accs, ex, outs, pids):
    dh = accs[0]
    xv, g, res = ex[0][...], ex[1][...], ex[2][...]
    r = lax.rsqrt(jnp.mean(xv * xv, axis=-1, keepdims=True) + EPS)
    xh = xv * r
    dy = dh * g
    dx = res + r * (dy - xh * jnp.mean(dy * xh, axis=-1, keepdims=True))
    outs[0][...] = dx
    if len(outs) > 2:
        outs[2][...] = dx.astype(BF16)

    @pl.when(pids[0] == 0)
    def _():
        outs[1][...] = jnp.zeros_like(outs[1])

    outs[1][...] += jnp.sum(dh * xh, axis=0, keepdims=True)


def _row_spec(tm, d):
    return pl.BlockSpec((tm, d), lambda i, *_: (i, 0))


def _const_spec(shape):
    nd = len(shape)
    return pl.BlockSpec(shape, lambda *_: (0,) * nd)


def _proj_in(name, h, w, layer, ride=None):
    t, d = h.shape
    n4 = w.shape[-1]
    tm = TILES["proj_in"]

    def epi(accs, ex, outs, pids):
        outs[0][...] = accs[0].astype(BF16)

    res = _matmul(
        name, (N_CHIPS, t // tm),
        [(h, pl.BlockSpec((tm, d), lambda p, i: (i, 0)), _whole,
          w, pl.BlockSpec((None, None, d, n4), lambda p, i: (p, layer, 0, 0)), _whole, 0)],
        [], [(SDS((t, N_CHIPS * n4), BF16), pl.BlockSpec((tm, n4), lambda p, i: (i, p)))],
        NN, epi, ride=ride)
    return res[0] if ride is None else (res[0][0], res[1])


def _proj_out(name, a, w, x, g_next=None, target=None):
    t, k = a.shape
    d = w.shape[-1]
    tm = TILES["proj_out"]
    if target is None:
        extras = [(x, _row_spec(tm, d)), (g_next, _const_spec((1, d)))]
        outs = [(SDS((t, d), F32), _row_spec(tm, d)), (SDS((t, d), BF16), _row_spec(tm, d))]
        epi = _epi_residual_norm
    else:
        extras = [(x, _row_spec(tm, d)), (target, _row_spec(tm, d))]
        outs = [(SDS((t, d), F32), _row_spec(tm, d)), (SDS((8, LANES), F32), _const_spec((8, LANES))),
                (SDS((t, d), BF16), _row_spec(tm, d))]
        epi = _epi_residual_loss
    return _matmul(name, (t // tm,), [(a, _row_spec(tm, k), _whole, w, _const_spec((k, d)), _whole, 0)], extras, outs, NN, epi)


def _ffn_in(name, h, wg, wu, layer, ride=None):
    t, d = h.shape
    n4 = wg.shape[-1]
    tm = TILES["ffn_in"]

    def epi(accs, ex, outs, pids):
        gate, up = accs
        outs[0][...] = gate.astype(BF16)
        outs[1][...] = up.astype(BF16)
        outs[2][...] = (gate * _sigmoid(gate) * up).astype(BF16)

    w_spec = pl.BlockSpec((None, None, d, n4), lambda p, i: (p, layer, 0, 0))
    h_spec = pl.BlockSpec((tm, d), lambda p, i: (i, 0))
    o = (SDS((N_CHIPS, t, n4), BF16), pl.BlockSpec((None, tm, n4), lambda p, i: (p, i, 0)))
    return _matmul(name, (N_CHIPS, t // tm),
                   [(h, h_spec, _whole, wg, w_spec, _whole, 0), (h, h_spec, _whole, wu, w_spec, _whole, 1)], [],
                   [o, o, o], NN, epi, n_acc=2, ride=ride)


def _ffn_out(name, act, wd, layer, x, g_next=None, target=None, ride=None):
    _, t, n4 = act.shape
    d = wd.shape[-1]
    tm = TILES["ffn_out"]
    xs = _row_spec(tm, d)
    if target is None:
        extras = [(x, xs), (g_next, _const_spec((1, d)))]
        outs = [(SDS((t, d), F32), xs), (SDS((t, d), BF16), xs)]
        epi = _epi_residual_norm
    else:
        extras = [(x, xs), (target, xs)]
        outs = [(SDS((t, d), F32), xs), (SDS((8, LANES), F32), _const_spec((8, LANES))), (SDS((t, d), BF16), xs)]
        epi = _epi_residual_loss
    return _matmul(
        name, (t // tm,),
        [(act, pl.BlockSpec((N_CHIPS, tm, n4), lambda i: (0, i, 0)), _slab,
          wd, pl.BlockSpec((N_CHIPS, None, n4, d), lambda i: (0, layer, 0, 0)), _slab, 0)],
        extras, outs, NN, epi, slabs=N_CHIPS, ride=ride)


def _ffn_dact(name, g, wd, layer, gate, up, ride=None):
    t, d = g.shape
    n4 = wd.shape[-2]
    tm = TILES["ffn_dact"]

    def epi(accs, ex, outs, pids):
        dact = accs[0]
        gt = ex[0][...].astype(F32)
        upv = ex[1][...].astype(F32)
        s = _sigmoid(gt)
        silu = gt * s
        outs[0][...] = (dact * upv * (s + silu - silu * s)).astype(BF16)
        outs[1][...] = (dact * silu).astype(BF16)

    blk = pl.BlockSpec((None, tm, n4), lambda p, i: (p, i, 0))
    o = (SDS((N_CHIPS, t, n4), BF16), blk)
    return _matmul(
        name, (N_CHIPS, t // tm),
        [(g, pl.BlockSpec((tm, d), lambda p, i: (i, 0)), _whole,
          wd, pl.BlockSpec((None, None, n4, d), lambda p, i: (p, layer, 0, 0)), _whole, 0)],
        [(gate, blk), (up, blk)], [o, o], NT, epi, ride=ride)


def _copy_epi(accs, ex, outs, pids):
    for a, o in zip(accs, outs):
        o[...] = a.astype(o.dtype)


def _dgrad_cols(name, dz_list, w_list, layer, three_d, x, g, res, bf16_copy=True, ride=None):
    t, d = x.shape
    n4 = w_list[0].shape[-1]
    tm = TILES["dgrad_cols"]
    if three_d:
        zs, z_pick = pl.BlockSpec((N_CHIPS, tm, n4), lambda i: (0, i, 0)), _slab
    else:
        zs, z_pick = _row_spec(tm, N_CHIPS * n4), (lambda ref, p: ref[:, p * n4:(p + 1) * n4])
    ws = pl.BlockSpec((N_CHIPS, None, d, n4), lambda i: (0, layer, 0, 0))
    xs = _row_spec(tm, d)
    return _matmul(
        name, (t // tm,), [(dz, zs, z_pick, w, ws, _slab, 0) for dz, w in zip(dz_list, w_list)],
        [(x, xs), (g, _const_spec((1, d))), (res, xs)],
        [(SDS((t, d), F32), xs), (SDS((1, d), F32), _const_spec((1, d)))] + ([(SDS((t, d), BF16), xs)] if bf16_copy else []),
        NT, _epi_rms_bwd, slabs=N_CHIPS, ride=ride)


def _dgrad_rows(name, g, w):
    t, d = g.shape
    k = w.shape[0]
    tm = TILES["dgrad_rows"]
    return _matmul(name, (t // tm,), [(g, _row_spec(tm, d), _whole, w, _const_spec((k, d)), _whole, 0)], [],
                   [(SDS((t, k), F32), _row_spec(tm, k))], NT, _copy_epi)[0]


A_TILE = 256


def _a_common(p_ref, lg_ref, lb_ref):
    pv = p_ref[...].astype(F32)
    a = _gelu(pv)
    u, v = a[:, :A_WIDTH], a[:, A_WIDTH:]
    vc = v - jnp.mean(v, axis=-1, keepdims=True)
    rs = lax.rsqrt(jnp.mean(vc * vc, axis=-1, keepdims=True) + EPS)
    vhat = vc * rs
    vn = vhat * lg_ref[...] + lb_ref[...]
    return pv, u, vhat, rs, vn.astype(BF16)


def _tril_weights(w_ref, g):
    r = lax.broadcasted_iota(jnp.int32, (CHUNK, CHUNK), 0)
    c = lax.broadcasted_iota(jnp.int32, (CHUNK, CHUNK), 1)
    return jnp.where(c <= r, w_ref[g], 0.0).astype(BF16), c <= r


def _mixer_a_fwd(proj, lg, lb, w, bias_t):
    t = proj.shape[0]

    def body(p_ref, lg_ref, lb_ref, w_ref, bt_ref, o_ref):
        _, u, _, _, vnb = _a_common(p_ref, lg_ref, lb_ref)
        for g in range(A_GROUPS):
            wt, _ = _tril_weights(w_ref, g)
            cs = slice(g * CHUNK, (g + 1) * CHUNK)
            for ch in range(A_TILE // CHUNK):
                rs_ = slice(ch * CHUNK, (ch + 1) * CHUNK)
                mixed = _dot(wt, vnb[rs_, cs], NN) + bt_ref[:, g:g + 1]
                o_ref[rs_, cs] = (u[rs_, cs] * mixed).astype(BF16)

    return pl.pallas_call(
        body, grid=(t // A_TILE,),
        in_specs=[pl.BlockSpec((A_TILE, 2 * A_WIDTH), lambda i: (i, 0)), _const_spec((1, A_WIDTH)),
                  _const_spec((1, A_WIDTH)), _const_spec((A_GROUPS, CHUNK, CHUNK)), _const_spec((CHUNK, A_GROUPS))],
        out_specs=pl.BlockSpec((A_TILE, A_WIDTH), lambda i: (i, 0)), out_shape=SDS((t, A_WIDTH), BF16),
        name="mixer_a_fwd", compiler_params=_params())(proj, lg, lb, w, bias_t)


def _mixer_a_bwd(proj, dcat, lg, lb, w, bias_t):
    t = proj.shape[0]

    def body(p_ref, da_ref, lg_ref, lb_ref, w_ref, bt_ref, dp_ref, dw_ref, dbt_ref, dlg_ref, dlb_ref, du_scr, dvn_scr):
        @pl.when(pl.program_id(0) == 0)
        def _():
            dw_ref[...] = jnp.zeros_like(dw_ref)
            dbt_ref[...] = jnp.zeros_like(dbt_ref)
            dlg_ref[...] = jnp.zeros_like(dlg_ref)
            dlb_ref[...] = jnp.zeros_like(dlb_ref)

        pv, u, vhat, rs, vnb = _a_common(p_ref, lg_ref, lb_ref)
        da = da_ref[...]
        for g in range(A_GROUPS):
            wt, keep = _tril_weights(w_ref, g)
            cs = slice(g * CHUNK, (g + 1) * CHUNK)
            for ch in range(A_TILE // CHUNK):
                rs_ = slice(ch * CHUNK, (ch + 1) * CHUNK)
                vg = vnb[rs_, cs]
                mixed = _dot(wt, vg, NN) + bt_ref[:, g:g + 1]
                du_scr[rs_, cs] = da[rs_, cs] * mixed
                dmx = da[rs_, cs] * u[rs_, cs]
                dw_ref[g] += jnp.where(keep, _dot(dmx, vg, NT), 0.0)
                dvn_scr[rs_, cs] = _dot(wt, dmx, TN)
                dbt_ref[:, g:g + 1] += jnp.sum(dmx, axis=1, keepdims=True)
        dvn = dvn_scr[...]
        dlg_ref[...] += jnp.sum(dvn * vhat, axis=0, keepdims=True)
        dlb_ref[...] += jnp.sum(dvn, axis=0, keepdims=True)
        dvh = dvn * lg_ref[...]
        dv = rs * (dvh - jnp.mean(dvh, axis=-1, keepdims=True) - vhat * jnp.mean(dvh * vhat, axis=-1, keepdims=True))
        gp = _gelu_grad(pv)
        dp_ref[:, :A_WIDTH] = (du_scr[...] * gp[:, :A_WIDTH]).astype(BF16)
        dp_ref[:, A_WIDTH:] = (dv * gp[:, A_WIDTH:]).astype(BF16)

    return pl.pallas_call(
        body, grid=(t // A_TILE,),
        in_specs=[pl.BlockSpec((A_TILE, 2 * A_WIDTH), lambda i: (i, 0)), pl.BlockSpec((A_TILE, A_WIDTH), lambda i: (i, 0)),
                  _const_spec((1, A_WIDTH)), _const_spec((1, A_WIDTH)), _const_spec((A_GROUPS, CHUNK, CHUNK)),
                  _const_spec((CHUNK, A_GROUPS))],
        out_specs=[pl.BlockSpec((A_TILE, 2 * A_WIDTH), lambda i: (i, 0)), _const_spec((A_GROUPS, CHUNK, CHUNK)),
                   _const_spec((CHUNK, A_GROUPS)), _const_spec((1, A_WIDTH)), _const_spec((1, A_WIDTH))],
        out_shape=[SDS((t, 2 * A_WIDTH), BF16), SDS((A_GROUPS, CHUNK, CHUNK), F32), SDS((CHUNK, A_GROUPS), F32),
                   SDS((1, A_WIDTH), F32), SDS((1, A_WIDTH), F32)],
        scratch_shapes=[pltpu.VMEM((A_TILE, A_WIDTH), F32), pltpu.VMEM((A_TILE, A_WIDTH), F32)],
        name="mixer_a_bwd", compiler_params=_params())(proj, dcat, lg, lb, w, bias_t)


def _rope_tables(t):
    half = ROPE_DIM // 2
    inv_freq = ROPE_THETA ** (-jnp.arange(half, dtype=F32) * 2.0 / ROPE_DIM)
    ang = jnp.arange(t, dtype=F32)[:, None] * inv_freq[None, :]
    cos, sin = jnp.cos(ang), jnp.sin(ang)
    one = jnp.ones((t, HEAD_DIM - ROPE_DIM), F32)
    zero = jnp.zeros((t, HEAD_DIM - ROPE_DIM), F32)
    zh = jnp.zeros((t, half), F32)
    c = jnp.concatenate([cos, cos, one], axis=1)
    s1 = jnp.concatenate([-sin, zh, zero], axis=1)
    s2 = jnp.concatenate([zh, sin, zero], axis=1)
    return tuple(jnp.tile(a, (1, LANES // HEAD_DIM)) for a in (c, s1, s2))


QK_TILE = 512
QK_COLS = 2 * N_DIL * B_WIDTH


def _qk_fwd(proj, gains, tabs, ride=None):
    t = proj.shape[0]
    col0 = 2 * A_WIDTH // 1024

    def body(p_ref, g_ref, c_ref, s1_ref, s2_ref, o_ref):
        seg = _segment_mean_matrix(HEAD_DIM)
        c, s1, s2 = c_ref[...], s1_ref[...], s2_ref[...]
        for ci in range(1024 // LANES):
            ls = slice(ci * LANES, (ci + 1) * LANES)
            xv = p_ref[:, ls].astype(F32)
            r = lax.rsqrt(_dot_hi(xv * xv, seg) + EPS)
            y = xv * r * g_ref[:, ls]
            o_ref[:, ls] = (y * c + pltpu.roll(y, LANES - 8, axis=1) * s1 + pltpu.roll(y, 8, axis=1) * s2).astype(BF16)

    tab = pl.BlockSpec((QK_TILE, LANES), lambda i, j: (i, 0))
    return _call(
        body, grid=(t // QK_TILE, QK_COLS // 1024),
        in_specs=[pl.BlockSpec((QK_TILE, 1024), lambda i, j: (i, col0 + j)), pl.BlockSpec((1, 1024), lambda i, j: (0, j)),
                  tab, tab, tab],
        out_specs=pl.BlockSpec((QK_TILE, 1024), lambda i, j: (i, j)), out_shape=SDS((t, QK_COLS), BF16),
        operands=[proj, gains, *tabs], name="qk_norm_rope_fwd", ride=ride)


PERM_TILE = 512


def _permute(name, items, rate):
    t = items[0][0].shape[0]
    n = len(items)
    rows = PERM_TILE // rate

    def body(*refs):
        scr = refs[-1]
        for x_ref, o_ref in zip(refs[:n], refs[n:2 * n]):
            for ci in range(B_WIDTH // LANES):
                scr[ci] = x_ref[:, ci * LANES:(ci + 1) * LANES].astype(F32)
            for rho in range(rate):
                for ci in range(B_WIDTH // LANES):
                    o_ref[rho, :, ci * LANES:(ci + 1) * LANES] = scr[ci, pl.ds(rho, rows, stride=rate), :].astype(o_ref.dtype)

    return pl.pallas_call(
        body, grid=(t // PERM_TILE,),
        in_specs=[pl.BlockSpec((PERM_TILE, B_WIDTH), functools.partial(lambda cb, i: (i, cb), cb)) for _, cb in items],
        out_specs=[pl.BlockSpec((rate, rows, B_WIDTH), lambda i: (0, i, 0)) for _ in items],
        out_shape=[SDS((rate, t // rate, B_WIDTH), a.dtype) for a, _ in items],
        scratch_shapes=[pltpu.VMEM((B_WIDTH // LANES, PERM_TILE, LANES), F32)],
        name=name, compiler_params=_params())(*[a for a, _ in items])


def _unpermute(name, arrays, rate):
    t = arrays[0].shape[1] * rate
    n = len(arrays)
    rows = PERM_TILE // rate

    def body(*refs):
        scr = refs[-1]
        for x_ref, o_ref in zip(refs[:n], refs[n:2 * n]):
            for rho in range(rate):
                for ci in range(B_WIDTH // LANES):
                    scr[ci, pl.ds(rho, rows, stride=rate), :] = x_ref[rho, :, ci * LANES:(ci + 1) * LANES].astype(F32)
            for ci in range(B_WIDTH // LANES):
                o_ref[:, ci * LANES:(ci + 1) * LANES] = scr[ci].astype(o_ref.dtype)

    return pl.pallas_call(
        body, grid=(t // PERM_TILE,),
        in_specs=[pl.BlockSpec((rate, rows, B_WIDTH), lambda i: (0, i, 0)) for _ in arrays],
        out_specs=[pl.BlockSpec((PERM_TILE, B_WIDTH), lambda i: (i, 0)) for _ in arrays],
        out_shape=[SDS((t, B_WIDTH), a.dtype) for a in arrays],
        scratch_shapes=[pltpu.VMEM((B_WIDTH // LANES, PERM_TILE, LANES), F32)],
        name=name, compiler_params=_params())(*arrays)


def _head_lane_mask(h):
    lane = lax.broadcasted_iota(jnp.int32, (1, LANES), 1)
    return (lane < HEAD_DIM) if h == 0 else (lane >= HEAD_DIM)


def _attn_fwd(name, q, k, v, ride=None):
    rate, length = q[0].shape[0], q[0].shape[1]
    nb = length // ATT_BLOCK
    scale = HEAD_DIM ** -0.5

    def body(q_ref, kc_ref, kp_ref, vc_ref, vp_ref, o_ref, l_ref):
        n = pl.program_id(1)
        qi = lax.broadcasted_iota(jnp.int32, (ATT_BLOCK, 2 * ATT_BLOCK), 0)
        cj = lax.broadcasted_iota(jnp.int32, (ATT_BLOCK, 2 * ATT_BLOCK), 1)
        has_prev = jnp.where(n > 0, 0, 2 * ATT_BLOCK)
        mask = ((cj < ATT_BLOCK) & (cj >= qi + has_prev)) | ((cj >= ATT_BLOCK) & (cj - ATT_BLOCK <= qi))
        for hp in range(B_WIDTH // LANES):
            ls = slice(hp * LANES, (hp + 1) * LANES)
            q2 = q_ref[:, ls]
            k2 = jnp.concatenate([kp_ref[:, ls], kc_ref[:, ls]], axis=0)
            v2 = jnp.concatenate([vp_ref[:, ls], vc_ref[:, ls]], axis=0)
            o_acc, lse2 = None, None
            for h in range(2):
                hm = _head_lane_mask(h)
                s = _dot(jnp.where(hm, q2, jnp.zeros_like(q2)), k2, NT) * scale
                s = jnp.where(mask, s, NEG_INF)
                m = jnp.max(s, axis=1, keepdims=True)
                p = jnp.exp(s - m)
                den = jnp.sum(p, axis=1, keepdims=True)
                lse = m + jnp.log(den)
                o = _dot(p / den, jnp.where(hm, v2, jnp.zeros_like(v2)), NN)
                o_acc = o if h == 0 else o_acc + o
                lse_b = lse + jnp.zeros((ATT_BLOCK, LANES), F32)
                lse2 = lse_b if h == 0 else jnp.where(hm, lse_b, lse2)
            o_ref[:, ls] = o_acc
            l_ref[:, ls] = lse2

    def cur(cb):
        return pl.BlockSpec((None, ATT_BLOCK, B_WIDTH), lambda r, n: (r, n, cb))

    def prev(cb):
        return pl.BlockSpec((None, ATT_BLOCK, B_WIDTH), lambda r, n: (r, jnp.maximum(n - 1, 0), cb))

    out = pl.BlockSpec((None, ATT_BLOCK, B_WIDTH), lambda r, n: (r, n, 0))
    return _call(
        body, grid=(rate, nb),
        in_specs=[cur(q[1]), cur(k[1]), prev(k[1]), cur(v[1]), prev(v[1])],
        out_specs=[out, out], out_shape=[SDS((rate, length, B_WIDTH), F32)] * 2,
        operands=[q[0], k[0], k[0], v[0], v[0]], name=name, ride=ride)


def _attn_merge(a_out, o_list, l_list):
    t = a_out.shape[0]
    tm = 512

    def body(a_ref, o0, o1, o2, l0, l1, l2, cat_ref, lt_ref):
        ls = [l0[...], l1[...], l2[...]]
        m = jnp.maximum(jnp.maximum(ls[0], ls[1]), ls[2])
        es = [jnp.exp(l - m) for l in ls]
        den = es[0] + es[1] + es[2]
        b = (es[0] * o0[...] + es[1] * o1[...] + es[2] * o2[...]) / den
        cat_ref[:, :A_WIDTH] = a_ref[...]
        cat_ref[:, A_WIDTH:] = b.astype(BF16)
        lt_ref[...] = m + jnp.log(den)

    blk = _row_spec(tm, B_WIDTH)
    return pl.pallas_call(
        body, grid=(t // tm,), in_specs=[blk] * 7,
        out_specs=[_row_spec(tm, A_WIDTH + B_WIDTH), blk],
        out_shape=[SDS((t, A_WIDTH + B_WIDTH), BF16), SDS((t, B_WIDTH), F32)],
        name="attn_merge", compiler_params=_params())(a_out, *o_list, *l_list)


def _attn_bwd_prep(dcat, cat):
    t = dcat.shape[0]
    tm = 512

    def body(d_ref, b_ref, db_ref, dd_ref):
        seg = _segment_mean_matrix(HEAD_DIM) * float(HEAD_DIM)
        for ci in range(B_WIDTH // LANES):
            ls = slice(ci * LANES, (ci + 1) * LANES)
            d = d_ref[:, ls]
            db_ref[:, ls] = d.astype(BF16)
            dd_ref[:, ls] = _dot_hi(d * b_ref[:, ls].astype(F32), seg)

    right = pl.BlockSpec((tm, B_WIDTH), lambda i: (i, 1))
    blk = _row_spec(tm, B_WIDTH)
    return pl.pallas_call(
        body, grid=(t // tm,), in_specs=[right, right], out_specs=[blk, blk],
        out_shape=[SDS((t, B_WIDTH), BF16), SDS((t, B_WIDTH), F32)],
        name="attn_bwd_prep", compiler_params=_params())(dcat, cat)


def _attn_bwd(name, q, k, v, db, lse, dd, ride=None):
    rate, length = db.shape[0], db.shape[1]
    nb = length // ATT_BLOCK
    scale = HEAD_DIM ** -0.5

    def body(qa_ref, qb_ref, k_ref, v_ref, dba_ref, dbb_ref, la_ref, lb_ref, da_ref, dbd_ref, dq_ref, dk_ref, dv_ref, carry):
        m = pl.program_id(1)

        @pl.when(m == 0)
        def _():
            carry[...] = jnp.zeros_like(carry)

        qi = lax.broadcasted_iota(jnp.int32, (ATT_BLOCK, ATT_BLOCK), 0)
        kj = lax.broadcasted_iota(jnp.int32, (ATT_BLOCK, ATT_BLOCK), 1)
        masks = (kj <= qi, (kj >= qi) & (m + 1 < nb))
        for hp in range(B_WIDTH // LANES):
            ls = slice(hp * LANES, (hp + 1) * LANES)
            k2, v2 = k_ref[:, ls], v_ref[:, ls]
            sides = ((qa_ref[:, ls], dba_ref[:, ls], la_ref[:, ls], da_ref[:, ls]),
                     (qb_ref[:, ls], dbb_ref[:, ls], lb_ref[:, ls], dbd_ref[:, ls]))
            dq = [None, None]
            dk_acc, dv_acc = None, None
            for h in range(2):
                hm = _head_lane_mask(h)
                km = jnp.where(hm, k2, jnp.zeros_like(k2))
                vm = jnp.where(hm, v2, jnp.zeros_like(v2))
                for side in range(2):
                    q2, db2, lse2, dd2 = sides[side]
                    lse_col = jnp.max(jnp.where(hm, lse2, NEG_INF), axis=1, keepdims=True)
                    dd_col = jnp.max(jnp.where(hm, dd2, NEG_INF), axis=1, keepdims=True)
                    s = _dot(q2, km, NT) * scale
                    p = jnp.where(masks[side], jnp.exp(s - lse_col), 0.0)
                    dvc = _dot(p, jnp.where(hm, db2, jnp.zeros_like(db2)), TN)
                    dp = _dot(db2, vm, NT)
                    ds = (p * (dp - dd_col) * scale).astype(BF16)
                    dqc = _dot(ds, km, NN)
                    dkc = _dot(ds, jnp.where(hm, q2, jnp.zeros_like(q2)), TN)
                    dq[side] = dqc if dq[side] is None else dq[side] + dqc
                    dk_acc = dkc if dk_acc is None else dk_acc + dkc
                    dv_acc = dvc if dv_acc is None else dv_acc + dvc
            dq_ref[:, ls] = (dq[0] + carry[:, ls]).astype(BF16)
            carry[:, ls] = dq[1]
            dk_ref[:, ls] = dk_acc.astype(BF16)
            dv_ref[:, ls] = dv_acc.astype(BF16)

    def cur(cb):
        return pl.BlockSpec((None, ATT_BLOCK, B_WIDTH), lambda r, n: (r, n, cb))

    def nxt(cb):
        return pl.BlockSpec((None, ATT_BLOCK, B_WIDTH), lambda r, n: (r, jnp.minimum(n + 1, nb - 1), cb))

    out = cur(0)
    return _call(
        body, grid=(rate, nb),
        in_specs=[cur(q[1]), nxt(q[1]), cur(k[1]), cur(v[1]), cur(0), nxt(0), cur(0), nxt(0), cur(0), nxt(0)],
        out_specs=[out, out, out], out_shape=[SDS((rate, length, B_WIDTH), BF16)] * 3,
        scratch_shapes=[pltpu.VMEM((ATT_BLOCK, B_WIDTH), F32)],
        operands=[q[0], q[0], k[0], v[0], db, db, lse, lse, dd, dd], name=name, ride=ride)


AB_IN = 2 * A_WIDTH + 3 * N_DIL * B_WIDTH
ASM_TILE = 256


def _dproj_assemble(proj, d_a, dqk, dv, gains, tabs):
    t = proj.shape[0]
    n_qk = 2 * N_DIL

    def body(p_ref, da_ref, *rest):
        dqk_refs = rest[:n_qk]
        dv_refs = rest[n_qk:n_qk + N_DIL]
        g_ref, c_ref, s1_ref, s2_ref, o_ref, dg_ref = rest[n_qk + N_DIL:]

        @pl.when(pl.program_id(0) == 0)
        def _():
            dg_ref[...] = jnp.zeros_like(dg_ref)

        seg = _segment_mean_matrix(HEAD_DIM)
        c, s1, s2 = c_ref[...], s1_ref[...], s2_ref[...]
        o_ref[:, :2 * A_WIDTH] = da_ref[...]
        for jg in range(n_qk):
            for ci in range(B_WIDTH // LANES):
                col = jg * B_WIDTH + ci * LANES
                src = slice(2 * A_WIDTH + col, 2 * A_WIDTH + col + LANES)
                xv = p_ref[:, src].astype(F32)
                r = lax.rsqrt(_dot_hi(xv * xv, seg) + EPS)
                xh = xv * r
                gain = g_ref[:, col:col + LANES]
                do = dqk_refs[jg][:, ci * LANES:(ci + 1) * LANES].astype(F32)
                dy = do * c + pltpu.roll(do * s1, 8, axis=1) + pltpu.roll(do * s2, LANES - 8, axis=1)
                dg_ref[:, col:col + LANES] += jnp.sum(dy * xh, axis=0, keepdims=True)
                dxh = dy * gain
                o_ref[:, src] = (r * (dxh - xh * _dot_hi(dxh * xh, seg))).astype(BF16)
        v0 = 2 * A_WIDTH + QK_COLS
        for g in range(N_DIL):
            o_ref[:, v0 + g * B_WIDTH:v0 + (g + 1) * B_WIDTH] = dv_refs[g][...]

    blk = _row_spec(ASM_TILE, B_WIDTH)
    tab = _row_spec(ASM_TILE, LANES)
    return pl.pallas_call(
        body, grid=(t // ASM_TILE,),
        in_specs=[_row_spec(ASM_TILE, AB_IN), _row_spec(ASM_TILE, 2 * A_WIDTH)] + [blk] * (n_qk + N_DIL)
        + [_const_spec((1, QK_COLS)), tab, tab, tab],
        out_specs=[_row_spec(ASM_TILE, AB_IN), _const_spec((1, QK_COLS))],
        out_shape=[SDS((t, AB_IN), BF16), SDS((1, QK_COLS), F32)],
        name="dproj_assemble", compiler_params=_params())(proj, d_a, *dqk, *dv, gains, *tabs)


def _fold_heads(dg_lane):
    n = dg_lane.shape[1]

    def body(x_ref, o_ref):
        r = lax.broadcasted_iota(jnp.int32, (B_WIDTH, B_WIDTH), 0) % HEAD_DIM
        c = lax.broadcasted_iota(jnp.int32, (B_WIDTH, B_WIDTH), 1) % HEAD_DIM
        fold = jnp.where(r == c, 1.0, 0.0).astype(F32)
        for jg in range(n // B_WIDTH):
            ls = slice(jg * B_WIDTH, (jg + 1) * B_WIDTH)
            o_ref[:, ls] = _dot_hi(jnp.broadcast_to(x_ref[:, ls], (8, B_WIDTH)), fold)

    return pl.pallas_call(body, out_shape=SDS((8, n), F32), name="fold_heads", compiler_params=_params())(dg_lane)


CD_TILE = 256
CD_IN = 2 * C_WIDTH + 3 * 512


def _cd_split(pv):
    w = C_WIDTH
    return pv[:, :w], pv[:, w:2 * w], pv[:, 2 * w:3 * w], pv[:, 3 * w:4 * w], pv[:, 4 * w:5 * w]


def _mixer_cd_fwd(proj, cw, cb, lg, lb, dw):
    t = proj.shape[0]
    per = CD_TILE // HALO

    def body(h_ref, m_ref, cw_ref, cb_ref, lg_ref, lb_ref, dw_ref, o_ref, c_scr, e_scr):
        not_first = (pl.program_id(0) > 0).astype(F32)
        ha, hg, _, hgc, hhv = _cd_split(h_ref[...].astype(F32))
        ma, mg, mgb, mgc, mhv = _cd_split(m_ref[...].astype(F32))
        c_scr[:HALO] = ha * _sigmoid(hg) * not_first
        c_scr[HALO:] = ma * _sigmoid(mg)
        e_scr[:HALO] = hgc * hhv * not_first
        e_scr[HALO:] = mgc * mhv
        acc = jnp.zeros((CD_TILE, C_WIDTH), F32)
        for k in range(C_KERNEL):
            acc = acc + cw_ref[k:k + 1, :] * c_scr[pl.ds(HALO - (C_KERNEL - 1) + k, CD_TILE), :]
        c1 = acc + cb_ref[...]
        cc = c1 - jnp.mean(c1, axis=-1, keepdims=True)
        c2 = cc * lax.rsqrt(jnp.mean(cc * cc, axis=-1, keepdims=True) + EPS) * lg_ref[...] + lb_ref[...]
        o_ref[:, :C_WIDTH] = (c2 * _sigmoid(c2)).astype(BF16)
        d1 = jnp.zeros((CD_TILE, C_WIDTH), F32)
        for k in range(D_KERNEL):
            d1 = d1 + dw_ref[k:k + 1, :] * e_scr[pl.ds(HALO - (D_KERNEL - 1) + k, CD_TILE), :]
        o_ref[:, C_WIDTH:] = (mgb * d1).astype(BF16)

    return pl.pallas_call(
        body, grid=(t // CD_TILE,),
        in_specs=[pl.BlockSpec((HALO, CD_IN), lambda i: (jnp.maximum(i * per - 1, 0), 0)), _row_spec(CD_TILE, CD_IN),
                  _const_spec((32, C_WIDTH)), _const_spec((1, C_WIDTH)), _const_spec((1, C_WIDTH)), _const_spec((1, C_WIDTH)),
                  _const_spec((8, C_WIDTH))],
        out_specs=_row_spec(CD_TILE, 2 * C_WIDTH), out_shape=SDS((t, 2 * C_WIDTH), BF16),
        scratch_shapes=[pltpu.VMEM((HALO + CD_TILE, C_WIDTH), F32)] * 2,
        name="mixer_cd_fwd", compiler_params=_params())(proj, proj, cw, cb, lg, lb, dw)


def _mixer_cd_bwd(proj, dcat, cw, cb, lg, lb, dw, ride=None):
    t = proj.shape[0]
    per = CD_TILE // HALO
    nt = t // CD_TILE
    ext = CD_TILE + HALO

    def body(hp_ref, m_ref, hn_ref, dm_ref, dn_ref, cw_ref, cb_ref, lg_ref, lb_ref, dw_ref,
             dp_ref, dcw_ref, dcb_ref, dlg_ref, dlb_ref, ddw_ref, c_scr, e_scr, dc1_scr, dd1_scr):
        i = pl.program_id(0)

        @pl.when(i == 0)
        def _():
            for r in (dcw_ref, dcb_ref, dlg_ref, dlb_ref, ddw_ref):
                r[...] = jnp.zeros_like(r)

        not_first = (i > 0).astype(F32)
        not_last = (i < nt - 1).astype(F32)
        pa, pg, _, pgc, phv = _cd_split(hp_ref[...].astype(F32))
        ma, mg, mgb, mgc, mhv = _cd_split(m_ref[...].astype(F32))
        na, ng, ngb, ngc, nhv = _cd_split(hn_ref[...].astype(F32))
        sig_m = _sigmoid(mg)
        c_scr[:HALO] = pa * _sigmoid(pg) * not_first
        c_scr[HALO:HALO + CD_TILE] = ma * sig_m
        c_scr[HALO + CD_TILE:] = na * _sigmoid(ng) * not_last
        e_scr[:HALO] = pgc * phv * not_first
        e_scr[HALO:HALO + CD_TILE] = mgc * mhv
        e_scr[HALO + CD_TILE:] = ngc * nhv * not_last

        acc = jnp.zeros((ext, C_WIDTH), F32)
        for k in range(C_KERNEL):
            acc = acc + cw_ref[k:k + 1, :] * c_scr[pl.ds(HALO - (C_KERNEL - 1) + k, ext), :]
        c1 = acc + cb_ref[...]
        cc = c1 - jnp.mean(c1, axis=-1, keepdims=True)
        rs = lax.rsqrt(jnp.mean(cc * cc, axis=-1, keepdims=True) + EPS)
        vhat = cc * rs
        c2 = vhat * lg_ref[...] + lb_ref[...]
        sig = _sigmoid(c2)
        dc = jnp.concatenate([dm_ref[:, :C_WIDTH], dn_ref[:, :C_WIDTH] * not_last], axis=0)
        dc2 = dc * (sig * (1.0 + c2 * (1.0 - sig)))
        dvh = dc2 * lg_ref[...]
        dc1 = rs * (dvh - jnp.mean(dvh, axis=-1, keepdims=True) - vhat * jnp.mean(dvh * vhat, axis=-1, keepdims=True))
        dc1_scr[...] = dc1
        dlg_ref[...] += jnp.sum((dc2 * vhat)[:CD_TILE], axis=0, keepdims=True)
        dlb_ref[...] += jnp.sum(dc2[:CD_TILE], axis=0, keepdims=True)
        dc1_m = dc1[:CD_TILE]
        dcb_ref[...] += jnp.sum(dc1_m, axis=0, keepdims=True)
        dc0 = jnp.zeros((CD_TILE, C_WIDTH), F32)
        for k in range(C_KERNEL):
            dc0 = dc0 + cw_ref[k:k + 1, :] * dc1_scr[pl.ds(C_KERNEL - 1 - k, CD_TILE), :]
            dcw_ref[k:k + 1, :] += jnp.sum(dc1_m * c_scr[pl.ds(HALO - (C_KERNEL - 1) + k, CD_TILE), :], axis=0, keepdims=True)
        dp_ref[:, :C_WIDTH] = (dc0 * sig_m).astype(BF16)
        dp_ref[:, C_WIDTH:2 * C_WIDTH] = (dc0 * ma * sig_m * (1.0 - sig_m)).astype(BF16)

        d1 = jnp.zeros((CD_TILE, C_WIDTH), F32)
        for k in range(D_KERNEL):
            d1 = d1 + dw_ref[k:k + 1, :] * e_scr[pl.ds(HALO - (D_KERNEL - 1) + k, CD_TILE), :]
        dd_m = dm_ref[:, C_WIDTH:]
        dd1 = jnp.concatenate([dd_m * mgb, dn_ref[:, C_WIDTH:] * ngb * not_last], axis=0)
        dd1_scr[...] = dd1
        dp_ref[:, 2 * C_WIDTH:3 * C_WIDTH] = (dd_m * d1).astype(BF16)
        de = jnp.zeros((CD_TILE, C_WIDTH), F32)
        for k in range(D_KERNEL):
            de = de + dw_ref[k:k + 1, :] * dd1_scr[pl.ds(D_KERNEL - 1 - k, CD_TILE), :]
            ddw_ref[k:k + 1, :] += jnp.sum(dd1[:CD_TILE] * e_scr[pl.ds(HALO - (D_KERNEL - 1) + k, CD_TILE), :], axis=0, keepdims=True)
        dp_ref[:, 3 * C_WIDTH:4 * C_WIDTH] = (de * mhv).astype(BF16)
        dp_ref[:, 4 * C_WIDTH:] = (de * mgc).astype(BF16)

    halo_prev = lambda i: (jnp.maximum(i * per - 1, 0), 0)
    halo_next = lambda i: (jnp.minimum((i + 1) * per, t // HALO - 1), 0)
    vec = _const_spec((1, C_WIDTH))
    return _call(
        body, grid=(nt,),
        in_specs=[pl.BlockSpec((HALO, CD_IN), halo_prev), _row_spec(CD_TILE, CD_IN), pl.BlockSpec((HALO, CD_IN), halo_next),
                  _row_spec(CD_TILE, 2 * C_WIDTH), pl.BlockSpec((HALO, 2 * C_WIDTH), halo_next),
                  _const_spec((32, C_WIDTH)), vec, vec, vec, _const_spec((8, C_WIDTH))],
        out_specs=[_row_spec(CD_TILE, CD_IN), _const_spec((32, C_WIDTH)), vec, vec, vec, _const_spec((8, C_WIDTH))],
        out_shape=[SDS((t, CD_IN), BF16), SDS((32, C_WIDTH), F32), SDS((1, C_WIDTH), F32), SDS((1, C_WIDTH), F32),
                   SDS((1, C_WIDTH), F32), SDS((8, C_WIDTH), F32)],
        scratch_shapes=[pltpu.VMEM((2 * HALO + CD_TILE, C_WIDTH), F32)] * 2 + [pltpu.VMEM((ext, C_WIDTH), F32)] * 2,
        operands=[proj, proj, proj, dcat, dcat, cw, cb, lg, lb, dw], name="mixer_cd_bwd", ride=ride)


def _wgrad(name, lhs, lhs_spec, rhs_list, rhs_spec, out_rc, t, ride):
    tk = TILES["wgrad"]
    r, c = out_rc
    n = len(rhs_list)

    def body(*refs):
        ab, out_refs = refs[:2 * n], refs[2 * n:]
        k = pl.program_id(1)
        parts = [_dot(ab[2 * j][...], ab[2 * j + 1][...], TN) for j in range(n)]

        @pl.when(k == 0)
        def _():
            for a in range(n):
                out_refs[a][...] = parts[a]

        @pl.when(k > 0)
        def _():
            for a in range(n):
                out_refs[a][...] += parts[a]

    operands, in_specs = [], []
    for rhs in rhs_list:
        operands += [lhs, rhs]
        in_specs += [lhs_spec, rhs_spec]
    res = _call(body, grid=(N_CHIPS, t // tk), in_specs=in_specs,
                out_specs=[pl.BlockSpec((None, r, c), lambda p, k: (p, 0, 0))] * n,
                out_shape=[SDS((N_CHIPS, r, c), F32)] * n, operands=operands, name=name, ride=ride)
    outs, ride_res = (res, None) if ride is None else res
    outs = [o.reshape(N_CHIPS, 2, r // 2, c) for o in outs]
    return outs if ride is None else (outs, ride_res)


def _wgrad_col_sharded(name, h, dz_list, three_d, ride=None):
    t, d = h.shape
    tk = TILES["wgrad"]
    n4 = dz_list[0].shape[-1] if three_d else dz_list[0].shape[-1] // N_CHIPS
    hs = pl.BlockSpec((tk, d), lambda p, k: (k, 0))
    zs = pl.BlockSpec((None, tk, n4), lambda p, k: (p, k, 0)) if three_d else pl.BlockSpec((tk, n4), lambda p, k: (k, p))
    return _wgrad(name, h, hs, dz_list, zs, (d, n4), t, ride)


def _wgrad_row_sharded(name, a, g, three_d, ride=None):
    t, d = g.shape
    tk = TILES["wgrad"]
    k4 = a.shape[-1] if three_d else a.shape[-1] // N_CHIPS
    a_spec = pl.BlockSpec((None, tk, k4), lambda p, k: (p, k, 0)) if three_d else pl.BlockSpec((tk, k4), lambda p, k: (k, p))
    gs = pl.BlockSpec((tk, d), lambda p, k: (k, 0))
    res = _wgrad(name, a, a_spec, [g], gs, (k4, d), t, ride)
    return res[0] if ride is None else (res[0][0], res[1])


def _mesh_scalars():
    return jnp.stack([lax.axis_index("c"), 2 * lax.axis_index("x") + lax.axis_index("y")]).astype(jnp.int32)


def _stage_own(name, w, layer, dtype):
    layers, r, cols = w.shape
    h = r // 2

    def body(s_ref, x_ref, o_ref):
        o_ref[...] = x_ref[...].astype(dtype)

    return pl.pallas_call(
        body,
        grid_spec=pltpu.PrefetchScalarGridSpec(
            num_scalar_prefetch=1, grid=(2,),
            in_specs=[pl.BlockSpec((None, h, cols), lambda i, s: (2 * layer + i, 0, 0))],
            out_specs=pl.BlockSpec((None, None, h, cols), lambda i, s: (s[1], i, 0, 0))),
        out_shape=SDS((N_CHIPS, 2, h, cols), dtype), name=name,
        compiler_params=_params())(_mesh_scalars(), w.reshape(2 * layers, h, cols))


def _remote(src, dst, send_sem, recv_sem, device):
    return pltpu.make_async_remote_copy(src, dst, send_sem, recv_sem, device_id=device, device_id_type=MESH)


def _ride_gather_send(bufs):
    n = len(bufs)

    def each(b, sems, act):
        send, recv = sems
        x, y, c, p, others = _position()
        for t in range(n):
            for j, (qx, qy) in enumerate(others):
                act(b[t].at[p, c], b[t].at[2 * qx + qy, c], send.at[t, j], recv.at[t, j], (qx, qy, c))

    def start(ins, b, new, sems):
        each(b, sems, lambda mine, landed, s, r, dev: _remote(mine, mine, s, r, dev).start())

    def finish(ins, b, new, sems):
        def act(mine, landed, s, r, dev):
            _remote(mine, mine, s, r, dev).wait_send()
            _remote(landed, landed, s, r, dev).wait_recv()
        each(b, sems, act)

    return _Ride([], bufs, [], [(n, 3), (n, 3)], start, finish)


def _ride_gather_pass(bufs):
    n = len(bufs)

    def each(b, sems, act):
        send, recv = sems
        x, y, c, p, others = _position()
        for t in range(n):
            for j, (qx, qy) in enumerate(others):
                act(b[t].at[2 * qx + qy, c], b[t].at[2 * qx + qy, 1 - c], send.at[t, j], recv.at[t, j], (x, y, 1 - c))

    def start(ins, b, new, sems):
        each(b, sems, lambda landed, passed, s, r, dev: _remote(landed, landed, s, r, dev).start())

    def finish(ins, b, new, sems):
        def act(landed, passed, s, r, dev):
            _remote(landed, landed, s, r, dev).wait_send()
            _remote(passed, passed, s, r, dev).wait_recv()
        each(b, sems, act)

    return _Ride([], bufs, [], [(n, 3), (n, 3)], start, finish)


def _ride_swap(tensors):
    n = len(tensors)

    def each(ins, new, sems, act):
        send, recv = sems
        x, y, c, _, _ = _position()
        for t in range(n):
            act(_remote(ins[t].at[:, 1 - c], new[t], send.at[t], recv.at[t], (x, y, 1 - c)))

    def start(ins, b, new, sems):
        each(ins, new, sems, lambda cp: cp.start())

    def finish(ins, b, new, sems):
        each(ins, new, sems, lambda cp: cp.wait())

    return _Ride(tensors, [], [SDS((s.shape[0],) + s.shape[2:], s.dtype) for s in tensors], [(n,), (n,)], start, finish)


def _ride_scatter(tensors, landing):
    n = len(tensors)

    def each(ins, b, sems, act):
        send, recv = sems
        x, y, c, p, others = _position()
        for t in range(n):
            for j, (qx, qy) in enumerate(others):
                q = 2 * qx + qy
                act(ins[t].at[q], b[t].at[p], b[t].at[q], send.at[t, j], recv.at[t, j], (qx, qy, c))

    def start(ins, b, new, sems):
        each(ins, b, sems, lambda src, dst, landed, s, r, dev: _remote(src, dst, s, r, dev).start())

    def finish(ins, b, new, sems):
        def act(src, dst, landed, s, r, dev):
            _remote(src, dst, s, r, dev).wait_send()
            _remote(landed, landed, s, r, dev).wait_recv()
        each(ins, b, sems, act)

    return _Ride(tensors, landing, [], [(n, 3), (n, 3)], start, finish)


def _ride_join(bufs):
    n = len(bufs)

    def each(b, sems, act):
        send, recv = sems
        x, y, c, _, _ = _position()
        for t in range(n):
            act(b[t].at[c], b[t].at[1 - c], send.at[t], recv.at[t], (x, y, 1 - c))

    def start(ins, b, new, sems):
        each(b, sems, lambda mine, theirs, s, r, dev: _remote(mine, mine, s, r, dev).start())

    def finish(ins, b, new, sems):
        def act(mine, theirs, s, r, dev):
            _remote(mine, mine, s, r, dev).wait_send()
            _remote(theirs, theirs, s, r, dev).wait_recv()
        each(b, sems, act)

    return _Ride([], bufs, [], [(n,), (n,)], start, finish)


def _all_reduce_small(pack):
    rows = pack.shape[0]
    n_dev = 2 * N_CHIPS

    def body(x_ref, o_ref, land, send, recv):
        x, y, c, p, _ = _position()
        me = 2 * p + c
        land[me] = x_ref[...]
        peers = [(dx, dy, dc) for dx in range(2) for dy in range(2) for dc in range(2) if (dx, dy, dc) != (0, 0, 0)]
        for j, (dx, dy, dc) in enumerate(peers):
            _remote(land.at[me], land.at[me], send.at[j], recv.at[j], (x ^ dx, y ^ dy, c ^ dc)).start()
        for j, (dx, dy, dc) in enumerate(peers):
            src = 4 * (x ^ dx) + 2 * (y ^ dy) + (c ^ dc)
            _remote(land.at[me], land.at[me], send.at[j], recv.at[j], (x ^ dx, y ^ dy, c ^ dc)).wait_send()
            _remote(land.at[src], land.at[src], send.at[j], recv.at[j], (x ^ dx, y ^ dy, c ^ dc)).wait_recv()
        acc = land[0]
        for dev in range(1, n_dev):
            acc = acc + land[dev]
        o_ref[...] = acc

    return pl.pallas_call(
        body, out_shape=SDS((rows, LANES), F32),
        scratch_shapes=[pltpu.VMEM((n_dev, rows, LANES), F32), pltpu.SemaphoreType.DMA((n_dev - 1,)),
                        pltpu.SemaphoreType.DMA((n_dev - 1,))],
        name="all_reduce_small", compiler_params=_params())(pack)


def _add_own_half(name, full, recv, out_dtype):
    n4, _, h, cols = full.shape

    def body(s_ref, a_ref, b_ref, o_ref, own_ref):
        v = (a_ref[...] + b_ref[...]).astype(out_dtype)
        o_ref[...] = v

        @pl.when(pl.program_id(0) == s_ref[1])
        def _():
            own_ref[...] = v

    return pl.pallas_call(
        body,
        grid_spec=pltpu.PrefetchScalarGridSpec(
            num_scalar_prefetch=1, grid=(n4,),
            in_specs=[pl.BlockSpec((None, None, h, cols), lambda q, s: (q, s[0], 0, 0)),
                      pl.BlockSpec((None, h, cols), lambda q, s: (q, 0, 0))],
            out_specs=[pl.BlockSpec((None, h, cols), lambda q, s: (q, 0, 0)),
                       pl.BlockSpec((None, h, cols), lambda q, s: (s[1], 0, 0))]),
        out_shape=[SDS((n4, h, cols), out_dtype)] * 2, name=name, compiler_params=_params())(_mesh_scalars(), full, recv)


def _sum_chips(name, parts):
    n4, h, cols = parts.shape
    th = h // 4 if h % 64 == 0 else h

    def body(s_ref, a_ref, o_ref):
        acc = a_ref[0].astype(F32)
        for q in range(1, n4):
            acc = acc + a_ref[q].astype(F32)
        o_ref[...] = acc

    return pl.pallas_call(
        body,
        grid_spec=pltpu.PrefetchScalarGridSpec(
            num_scalar_prefetch=1, grid=(h // th,),
            in_specs=[pl.BlockSpec((n4, th, cols), lambda i, s: (0, i, 0))],
            out_specs=pl.BlockSpec((None, th, cols), lambda i, s: (s[0], i, 0))),
        out_shape=SDS((2, h, cols), F32), name=name, compiler_params=_params())(_mesh_scalars(), parts)


def _adamw_math(w, g, m, v):
    m2 = ADAM_B1 * m + (1.0 - ADAM_B1) * g
    v2 = ADAM_B2 * v + (1.0 - ADAM_B2) * (g * g)
    m_hat = m2 / (1.0 - ADAM_B1 ** ADAM_STEP)
    v_hat = v2 / (1.0 - ADAM_B2 ** ADAM_STEP)
    delta = -ADAM_LR * (m_hat / (jnp.sqrt(v_hat) + ADAM_EPS) + ADAM_WD * w)
    return delta, m2, v2


def _row_tile(rows, cols):
    cap = max(8, (1 << 18) // cols)
    best = 8
    for cand in range(8, min(rows, cap) + 1, 8):
        if rows % cand == 0:
            best = cand
    return best


def _adamw_big(name, w, g_layers, m, v):
    layers, rows, cols = w.shape
    tr = _row_tile(rows, cols)
    outs = None
    for layer, g in enumerate(g_layers):
        def body(w_ref, g_ref, m_ref, v_ref, *rest):
            g_o, d_o, m_o, v_o = rest[-4:]
            gv = g_ref[...]
            d, mm, vv = _adamw_math(w_ref[...], gv, m_ref[...], v_ref[...])
            g_o[...] = gv
            d_o[...] = d
            m_o[...] = mm
            v_o[...] = vv

        blk = pl.BlockSpec((None, tr, cols), functools.partial(lambda l, i: (l, i, 0), layer))
        prev = [] if outs is None else list(outs)
        outs = pl.pallas_call(
            body, grid=(rows // tr,), in_specs=[blk, _row_spec(tr, cols), blk, blk] + [ANY] * len(prev),
            out_specs=[blk] * 4, out_shape=[SDS((layers, rows, cols), F32)] * 4,
            input_output_aliases={4 + j: j for j in range(len(prev))}, name=f"{name}_{layer}",
            compiler_params=_params())(w, g.reshape(rows, cols), m, v, *prev)
    return tuple(outs)


def _adamw_small(ws, gs, ms, vs):
    n = len(ws)
    flat = []
    for group in (ws, gs, ms, vs):
        flat += [a.reshape(-1, a.shape[-1]) for a in group]

    def body(*refs):
        w_r, g_r, m_r, v_r = refs[:n], refs[n:2 * n], refs[2 * n:3 * n], refs[3 * n:4 * n]
        d_o, m_o, v_o = refs[4 * n:5 * n], refs[5 * n:6 * n], refs[6 * n:7 * n]
        for j in range(n):
            d, mm, vv = _adamw_math(w_r[j][...], g_r[j][...], m_r[j][...], v_r[j][...])
            d_o[j][...] = d
            m_o[j][...] = mm
            v_o[j][...] = vv

    shapes = [SDS(a.shape, F32) for a in flat[:n]]
    outs = pl.pallas_call(body, out_shape=shapes * 3, name="adamw_small", compiler_params=_params())(*flat)
    res = []
    for k in range(3):
        res.append([outs[k * n + j].reshape(ws[j].shape) for j in range(n)])
    return res


BIG = ("ab_w_in", "ab_w_out", "cd_w_in", "cd_w_out", "ffn_w_gate", "ffn_w_up", "ffn_w_down")
V_BLOCK = (2 * A_WIDTH + QK_COLS) // B_WIDTH


def _pad_rows(a, rows):
    return jnp.pad(a, ((0, rows - a.shape[0]), (0, 0)))


A_IN, A_OUT, C_IN, C_OUT = ("ab_w_in", 0), ("ab_w_out", 0), ("cd_w_in", 0), ("cd_w_out", 0)
G0, U0, D0 = ("ffn_w_gate", 0), ("ffn_w_up", 0), ("ffn_w_down", 0)
G1, U1, D1 = ("ffn_w_gate", 1), ("ffn_w_up", 1), ("ffn_w_down", 1)
UNITS = (A_IN, A_OUT, G0, U0, D0, C_IN, C_OUT, G1, U1, D1)
SMALL_SHARDED = ("small", 0)


class _Exchange:
    def __init__(self, enabled):
        self.enabled = enabled
        self.w, self.grad, self.recv, self.half, self.land, self.done = {}, {}, {}, {}, {}, {}

    def full(self, unit):
        b = self.w[unit]
        return b.reshape(N_CHIPS, 1, 2 * b.shape[2], b.shape[3])

    def _ride(self, phases):
        rides, sinks = [], []
        for kind, units in phases:
            if kind == "send":
                rides.append(_ride_gather_send([self.w[u] for u in units]))
                sinks.append(self.w)
            elif kind == "pass":
                rides.append(_ride_gather_pass([self.w[u] for u in units]))
                sinks.append(self.w)
            elif kind == "swap":
                rides.append(_ride_swap([self.grad[u] for u in units]))
                sinks.append(self.recv)
            elif kind == "scatter":
                rides.append(_ride_scatter([self.half[u] for u in units], [self.land[u] for u in units]))
                sinks.append(self.land)
            else:
                rides.append(_ride_join([self.done[u] for u in units]))
                sinks.append(self.done)
        ride = functools.reduce(_ride_both, rides)

        def settle(res):
            n_bufs = sum(len(r.bufs) for r in rides)
            bufs, new = list(res[:n_bufs]), list(res[n_bufs:])
            for r, sink, (_, units) in zip(rides, sinks, phases):
                vals = [bufs.pop(0) for _ in r.bufs] + [new.pop(0) for _ in r.new_outs]
                for u, v in zip(units, vals):
                    sink[u] = v

        return ride, settle

    def run(self, fn, *args, phases=(), **kw):
        if not self.enabled or not phases:
            return fn(*args, **kw)
        ride, settle = self._ride(phases)
        out, res = fn(*args, ride=ride, **kw)
        settle(res)
        return out

    def alone(self, name, phases):
        if self.enabled:
            ride, settle = self._ride(phases)
            settle(_run_ride(name, ride))

    def pair_sum(self, units):
        if self.enabled:
            for u in units:
                dtype = F32 if u == SMALL_SHARDED else BF16
                self.half[u], self.land[u] = _add_own_half(f"pair_sum_{u[0]}_{u[1]}", self.grad[u], self.recv[u], dtype)

    def chip_sum(self, units):
        if self.enabled:
            for u in units:
                self.done[u] = _sum_chips(f"chip_sum_{u[0]}_{u[1]}", self.land[u])


def _local_step(x, target, ex, sp):
    t, d = x.shape
    tabs = _rope_tables(t)
    gains = jnp.concatenate([jnp.tile(sp["q_norm_g"][g], HEAD_DIM // 8) for g in range(N_DIL)]
                            + [jnp.tile(sp["k_norm_g"][g], HEAD_DIM // 8) for g in range(N_DIL)]).reshape(1, QK_COLS)
    bias_t = sp["sgu_bias"].T
    cw = _pad_rows(sp["conv_c_w"], 32)
    dw = _pad_rows(sp["conv_d_w"], 8)
    cb, clg, clb = (sp[k].reshape(1, C_WIDTH) for k in ("conv_c_b", "c_ln_g", "c_ln_b"))
    slg, slb = sp["sgu_norm_g"].reshape(1, A_WIDTH), sp["sgu_norm_b"].reshape(1, A_WIDTH)
    g_ab, g_cd = sp["ab_norm_g"].reshape(1, d), sp["cd_norm_g"].reshape(1, d)
    g_f0, g_f1 = sp["ffn_norm_g"][0:1], sp["ffn_norm_g"][1:2]
    run = ex.run

    def w2d(unit):
        return ex.full(unit).reshape(-1, d)

    h0 = _rms_fwd("rms_ab", x, g_ab)
    proj = run(_proj_in, "proj_ab", h0, ex.full(A_IN), 0, phases=[("send", [A_OUT, G0])])
    a_out = _mixer_a_fwd(proj, slg, slb, sp["sgu_w"], bias_t)
    qk = run(_qk_fwd, proj, gains, tabs, phases=[("pass", [A_OUT, G0]), ("send", [U0])])
    fwd_phases = ([("pass", [U0]), ("send", [D0])], [("pass", [D0]), ("send", [C_IN])],
                  [("pass", [C_IN]), ("send", [C_OUT, G1])])
    qkv, o_list, l_list = [], [], []
    for g, rate in enumerate(DIL_RATES):
        if rate == 1:
            qk3, proj3 = qk.reshape(1, t, QK_COLS), proj.reshape(1, t, AB_IN)
            q, k, v = (qk3, g), (qk3, N_DIL + g), (proj3, V_BLOCK + g)
        else:
            qp, kp, vp = _permute(f"perm_fwd_{g}", [(qk, g), (qk, N_DIL + g), (proj, V_BLOCK + g)], rate)
            q, k, v = (qp, 0), (kp, 0), (vp, 0)
        qkv.append((q, k, v))
        o, l = run(_attn_fwd, f"attn_fwd_{g}", q, k, v, phases=fwd_phases[g])
        if rate == 1:
            o, l = o.reshape(t, B_WIDTH), l.reshape(t, B_WIDTH)
        else:
            o, l = _unpermute(f"unperm_fwd_{g}", [o, l], rate)
        o_list.append(o)
        l_list.append(l)
    cat, lse_tot = _attn_merge(a_out, o_list, l_list)
    x1, hf0 = _proj_out("out_ab", cat, w2d(A_OUT), x, g_next=g_f0)
    gate0, up0, act0 = run(_ffn_in, "ffn_in_0", hf0, ex.full(G0), ex.full(U0), 0,
                           phases=[("pass", [C_OUT, G1]), ("send", [U1])])
    x2, h1 = run(_ffn_out, "ffn_out_0", act0, ex.full(D0), 0, x1, g_next=g_cd, phases=[("pass", [U1]), ("send", [D1])])
    projcd = run(_proj_in, "proj_cd", h1, ex.full(C_IN), 0, phases=[("pass", [D1])])
    cat2 = _mixer_cd_fwd(projcd, cw, cb, clg, clb, dw)
    x3, hf1 = _proj_out("out_cd", cat2, w2d(C_OUT), x2, g_next=g_f1)
    gate1, up1, act1 = _ffn_in("ffn_in_1", hf1, ex.full(G1), ex.full(U1), 0)
    dy, loss_acc, dy_b = _ffn_out("ffn_out_1", act1, ex.full(D1), 0, x3, target=target)
    loss = 0.5 * loss_acc[0, 0] / d

    late = [D1, G1, U1]
    dgate, dup = _ffn_dact("ffn_dact_1", dy_b, ex.full(D1), 0, gate1, up1)
    ex.grad[D1] = _wgrad_row_sharded("wgrad_down_1", act1, dy_b, True)
    ex.grad[G1], ex.grad[U1] = _wgrad_col_sharded("wgrad_gate_up_1", hf1, [dgate, dup], True)
    g3, d_f1, g3_b = run(_dgrad_cols, "dgrad_ffn_1", [dgate, dup], [ex.full(G1), ex.full(U1)], 0, True, x3, g_f1, dy,
                         phases=[("swap", late)])
    ex.pair_sum(late)

    dcat2 = _dgrad_rows("dgrad_out_cd", g3_b, w2d(C_OUT))
    ex.grad[C_OUT] = _wgrad_row_sharded("wgrad_out_cd", cat2, g3_b, False)
    dprojcd, d_cw, d_cb, d_clg, d_clb, d_dw = run(_mixer_cd_bwd, projcd, dcat2, cw, cb, clg, clb, dw, phases=[("scatter", late)])
    ex.chip_sum(late)
    ex.grad[C_IN] = run(_wgrad_col_sharded, "wgrad_in_cd", h1, [dprojcd], False, phases=[("join", late)])[0]
    g2, d_cdn, g2_b = run(_dgrad_cols, "dgrad_in_cd", [dprojcd], [ex.full(C_IN)], 0, False, x2, g_cd, g3,
                          phases=[("swap", [C_OUT, C_IN])])
    ex.pair_sum([C_OUT, C_IN])

    dgate, dup = run(_ffn_dact, "ffn_dact_0", g2_b, ex.full(D0), 0, gate0, up0, phases=[("scatter", [C_OUT, C_IN])])
    ex.chip_sum([C_OUT, C_IN])
    ex.grad[D0] = run(_wgrad_row_sharded, "wgrad_down_0", act0, g2_b, True, phases=[("join", [C_OUT, C_IN])])
    ex.grad[G0], ex.grad[U0] = _wgrad_col_sharded("wgrad_gate_up_0", hf0, [dgate, dup], True)
    small = {"cd_norm_g": d_cdn, "conv_c_w": d_cw[:C_KERNEL], "conv_c_b": d_cb, "c_ln_g": d_clg, "c_ln_b": d_clb,
             "conv_d_w": d_dw[:D_KERNEL]}
    ex.grad[SMALL_SHARDED] = _split_full_small(small).reshape(N_CHIPS, 2, SHARDED_ROWS // 2, LANES)
    mid = [D0, G0, U0, SMALL_SHARDED]
    g1, d_f0, g1_b = run(_dgrad_cols, "dgrad_ffn_0", [dgate, dup], [ex.full(G0), ex.full(U0)], 0, True, x1, g_f0, g2,
                         phases=[("swap", mid)])
    ex.pair_sum(mid)

    dcat = _dgrad_rows("dgrad_out_ab", g1_b, w2d(A_OUT))
    ex.grad[A_OUT] = _wgrad_row_sharded("wgrad_out_ab", cat, g1_b, False)
    d_a, d_sw, d_sbt, d_slg, d_slb = _mixer_a_bwd(proj, dcat, slg, slb, sp["sgu_w"], bias_t)
    dbb, dd = _attn_bwd_prep(dcat, cat)
    bwd_phases = ([("scatter", mid)], [("join", mid), ("swap", [A_OUT])], [("scatter", [A_OUT])])
    dqs, dks, dvs = [], [], []
    for g, rate in enumerate(DIL_RATES):
        q, k, v = qkv[g]
        if rate == 1:
            db3, l3, dd3 = (a.reshape(1, t, B_WIDTH) for a in (dbb, lse_tot, dd))
        else:
            db3, l3, dd3 = _permute(f"perm_bwd_{g}", [(dbb, 0), (lse_tot, 0), (dd, 0)], rate)
        dq, dk, dv = run(_attn_bwd, f"attn_bwd_{g}", q, k, v, db3, l3, dd3, phases=bwd_phases[g])
        if g == 0:
            ex.chip_sum(mid)
        elif g == 1:
            ex.pair_sum([A_OUT])
        else:
            ex.chip_sum([A_OUT])
        if rate == 1:
            dq, dk, dv = (a.reshape(t, B_WIDTH) for a in (dq, dk, dv))
        else:
            dq, dk, dv = _unpermute(f"unperm_bwd_{g}", [dq, dk, dv], rate)
        dqs.append(dq)
        dks.append(dk)
        dvs.append(dv)
    dproj, d_gains = _dproj_assemble(proj, d_a, dqs + dks, dvs, gains, tabs)
    d_gains = _fold_heads(d_gains)[0].reshape(2, N_DIL, B_WIDTH)[:, :, :HEAD_DIM]
    ex.grad[A_IN] = run(_wgrad_col_sharded, "wgrad_in_ab", h0, [dproj], False, phases=[("join", [A_OUT])])[0]
    ex.alone("swap_last", [("swap", [A_IN])])
    ex.pair_sum([A_IN])
    gx, d_abn = run(_dgrad_cols, "dgrad_in_ab", [dproj], [ex.full(A_IN)], 0, False, x, g_ab, g1, bf16_copy=False,
                    phases=[("scatter", [A_IN])])
    ex.chip_sum([A_IN])
    ex.alone("join_last", [("join", [A_IN])])

    small.update({
        "ab_norm_g": d_abn, "sgu_norm_g": d_slg, "sgu_norm_b": d_slb, "sgu_w": d_sw, "sgu_bias": d_sbt.T,
        "q_norm_g": d_gains[0], "k_norm_g": d_gains[1], "ffn_norm_g": jnp.concatenate([d_f0, d_f1], axis=0),
    })
    return loss, gx, small


SHARDED_SMALL = ("cd_norm_g", "conv_c_w", "conv_c_b", "c_ln_g", "c_ln_b", "conv_d_w")
SHARDED_ROWS = 48
REPLICATED_SMALL = ("ab_norm_g", "sgu_norm_g", "sgu_norm_b", "sgu_w", "sgu_bias", "q_norm_g", "k_norm_g", "ffn_norm_g")
REPLICATED_ROWS = 560


def _pack_sharded(parts):
    rows = [parts[k].reshape(-1, LANES) for k in SHARDED_SMALL]
    return _pad_rows(jnp.concatenate(rows, axis=0), SHARDED_ROWS)


def _split_full_small(small):
    per_chip = []
    for q in range(N_CHIPS):
        parts = {}
        for k in SHARDED_SMALL:
            a = small[k]
            a = a.reshape(-1, a.shape[-1])
            n = a.shape[-1] // N_CHIPS
            parts[k] = a[:, q * n:(q + 1) * n]
        per_chip.append(_pack_sharded(parts))
    return jnp.stack(per_chip)


def _unpack_sharded(pack, shapes):
    out, r = {}, 0
    for k in SHARDED_SMALL:
        n = math.prod(shapes[k]) // LANES
        out[k] = pack[r:r + n].reshape(shapes[k])
        r += n
    return out


def _gathered_small(packs, shapes):
    per_chip = [_unpack_sharded(packs[q], shapes) for q in range(N_CHIPS)]
    return {k: jnp.concatenate([pc[k] for pc in per_chip], axis=-1) for k in SHARDED_SMALL}


def _pack_replicated(small):
    rows = []
    for k in REPLICATED_SMALL:
        a = small[k].reshape(-1)
        a = jnp.pad(a, (0, (-a.shape[0]) % LANES))
        rows.append(a.reshape(-1, LANES))
    return _pad_rows(jnp.concatenate(rows, axis=0), REPLICATED_ROWS)


def _unpack_replicated(pack, shapes):
    out, r = {}, 0
    for k in REPLICATED_SMALL:
        size = math.prod(shapes[k])
        n = -(-size // LANES)
        out[k] = pack[r:r + n].reshape(-1)[:size].reshape(shapes[k])
        r += n
    return out


WEIGHT_ORDER = ("ab_norm_g", "ab_w_in", "sgu_norm_g", "sgu_norm_b", "sgu_w", "sgu_bias", "q_norm_g", "k_norm_g", "ab_w_out",
                "cd_norm_g", "cd_w_in", "conv_c_w", "conv_c_b", "c_ln_g", "c_ln_b", "conv_d_w", "cd_w_out", "ffn_norm_g",
                "ffn_w_gate", "ffn_w_up", "ffn_w_down")


def kernel(x, ab_norm_g, ab_w_in, sgu_norm_g, sgu_norm_b, sgu_w, sgu_bias, q_norm_g, k_norm_g, ab_w_out, cd_norm_g, cd_w_in, conv_c_w, conv_c_b, c_ln_g, c_ln_b, conv_d_w, cd_w_out, ffn_norm_g, ffn_w_gate, ffn_w_up, ffn_w_down, loss_target, m_ab_norm_g, m_ab_w_in, m_sgu_norm_g, m_sgu_norm_b, m_sgu_w, m_sgu_bias, m_q_norm_g, m_k_norm_g, m_ab_w_out, m_cd_norm_g, m_cd_w_in, m_conv_c_w, m_conv_c_b, m_c_ln_g, m_c_ln_b, m_conv_d_w, m_cd_w_out, m_ffn_norm_g, m_ffn_w_gate, m_ffn_w_up, m_ffn_w_down, v_ab_norm_g, v_ab_w_in, v_sgu_norm_g, v_sgu_norm_b, v_sgu_w, v_sgu_bias, v_q_norm_g, v_k_norm_g, v_ab_w_out, v_cd_norm_g, v_cd_w_in, v_conv_c_w, v_conv_c_b, v_c_ln_g, v_c_ln_b, v_conv_d_w, v_cd_w_out, v_ffn_norm_g, v_ffn_w_gate, v_ffn_w_up, v_ffn_w_down):
    args = dict(locals())
    ws = {k: args[k] for k in WEIGHT_ORDER}
    ms = {k: args["m_" + k] for k in WEIGHT_ORDER}
    vs = {k: args["v_" + k] for k in WEIGHT_ORDER}
    small_names = [k for k in WEIGHT_ORDER if k not in BIG]
    t, d = x.shape[1:]

    ex = _Exchange(enabled=True)
    for name, layer in UNITS:
        ex.w[(name, layer)] = _stage_own(f"stage_{name}_{layer}", ws[name], layer, BF16)
    own_small = _pack_sharded({k: ws[k][0] for k in SHARDED_SMALL})
    ex.w[SMALL_SHARDED] = _stage_own("stage_small", own_small[None], 0, F32)
    ex.alone("gather_first", [("send", [A_IN, SMALL_SHARDED])])
    ex.alone("gather_first_pass", [("pass", [A_IN, SMALL_SHARDED])])
    sp = _gathered_small(ex.w[SMALL_SHARDED].reshape(N_CHIPS, SHARDED_ROWS, LANES), {k: ws[k].shape[1:] for k in SHARDED_SMALL})
    for k in REPLICATED_SMALL:
        sp[k] = ws[k] if k == "ffn_norm_g" else ws[k][0]

    loss, grad_x, g_small = _local_step(x.reshape(t, d), loss_target.reshape(t, d), ex, sp)

    grad = _unpack_sharded(ex.done[SMALL_SHARDED].reshape(SHARDED_ROWS, LANES), {k: ws[k].shape for k in SHARDED_SMALL})
    grad.update(_unpack_replicated(_all_reduce_small(_pack_replicated(g_small)), {k: ws[k].shape for k in REPLICATED_SMALL}))

    delta, new_m, new_v = {}, {}, {}
    for k in BIG:
        g_layers = [ex.done[(k, layer)] for layer in range(ws[k].shape[0])]
        grad[k], delta[k], new_m[k], new_v[k] = _adamw_big("adamw_" + k, ws[k], g_layers, ms[k], vs[k])
    d_s, m_s, v_s = _adamw_small([ws[k] for k in small_names], [grad[k] for k in small_names],
                                 [ms[k] for k in small_names], [vs[k] for k in small_names])
    for j, k in enumerate(small_names):
        delta[k], new_m[k], new_v[k] = d_s[j], m_s[j], v_s[j]

    loss = lax.psum(loss, ("x", "y", "c"))
    return (loss, grad_x[None], *[grad[k] for k in WEIGHT_ORDER], *[delta[k] for k in WEIGHT_ORDER],
            *[new_m[k] for k in WEIGHT_ORDER], *[new_v[k] for k in WEIGHT_ORDER])
```

```python
import functools
import math

import jax
import jax.numpy as jnp
from jax import lax
from jax.experimental import pallas as pl
from jax.experimental.pallas import tpu as pltpu

F32 = jnp.float32
BF16 = jnp.bfloat16
SDS = jax.ShapeDtypeStruct

N_CHIPS = 4
EPS = 1e-6
NEG_INF = -1e30
CHUNK = 128
A_GROUPS = 4
A_WIDTH = 512
N_DIL = 3
DIL_RATES = (1, 4, 16)
HEAD_DIM = 64
B_WIDTH = 512
ROPE_DIM = 16
ROPE_THETA = 500000.0
C_WIDTH = 512
C_KERNEL = 31
D_KERNEL = 3
HALO = 32
ATT_BLOCK = 128
LANES = 128

ADAM_LR = 0.001
ADAM_B1 = 0.9
ADAM_B2 = 0.999
ADAM_EPS = 1e-08
ADAM_WD = 0.01
ADAM_STEP = 10

VMEM_LIMIT = 56 * 1024 * 1024

NN = (((1,), (0,)), ((), ()))
NT = (((1,), (1,)), ((), ()))
TN = (((0,), (0,)), ((), ()))

TILES = {"proj_in": 1024, "proj_out": 1024, "ffn_in": 1024, "ffn_out": 512, "ffn_dact": 512, "dgrad_cols": 512,
         "dgrad_rows": 1024, "wgrad": 2048}


def _params(sem=None):
    return pltpu.CompilerParams(dimension_semantics=sem, vmem_limit_bytes=VMEM_LIMIT)


def _bf(v):
    return v if v.dtype == BF16 else v.astype(BF16)


def _dot(a, b, dims):
    return lax.dot_general(_bf(a), _bf(b), dims, preferred_element_type=F32)


def _dot_hi(a, b):
    return jnp.dot(a, b, precision=lax.Precision.HIGHEST, preferred_element_type=F32)


def _sigmoid(v):
    return 0.5 * jnp.tanh(0.5 * v) + 0.5


def _gelu(v):
    return 0.5 * v * (1.0 + lax.erf(v * (1.0 / math.sqrt(2.0))))


def _gelu_grad(v):
    cdf = 0.5 * (1.0 + lax.erf(v * (1.0 / math.sqrt(2.0))))
    return cdf + v * jnp.exp(-0.5 * v * v) * (1.0 / math.sqrt(2.0 * math.pi))


def _segment_mean_matrix(seg, scale=None):
    r = lax.broadcasted_iota(jnp.int32, (LANES, LANES), 0) // seg
    c = lax.broadcasted_iota(jnp.int32, (LANES, LANES), 1) // seg
    return jnp.where(r == c, (1.0 / seg) if scale is None else scale, 0.0).astype(BF16)


def _segment_dot(v, seg):
    hi = v.astype(BF16)
    lo = (v - hi.astype(F32)).astype(BF16)
    return jnp.dot(hi, seg, preferred_element_type=F32) + jnp.dot(lo, seg, preferred_element_type=F32)


MESH = pl.DeviceIdType.MESH
ANY = pl.BlockSpec(memory_space=pl.ANY)


def _position():
    x, y, c = lax.axis_index("x"), lax.axis_index("y"), lax.axis_index("c")
    others = [(1 - x, y), (x, 1 - y), (1 - x, 1 - y)]
    return x, y, c, 2 * x + y, others


class _Ride:
    def __init__(self, ins, bufs, new_outs, sem_shapes, start, finish):
        self.ins, self.bufs, self.new_outs, self.sem_shapes = list(ins), list(bufs), list(new_outs), list(sem_shapes)
        self.start, self.finish = start, finish


def _ride_both(a, b):
    na = (len(a.ins), len(a.bufs), len(a.new_outs), len(a.sem_shapes))

    def split(ins, bufs, new, sems):
        return ((ins[:na[0]], bufs[:na[1]], new[:na[2]], sems[:na[3]]), (ins[na[0]:], bufs[na[1]:], new[na[2]:], sems[na[3]:]))

    def start(*refs):
        ra, rb = split(*refs)
        a.start(*ra)
        b.start(*rb)

    def finish(*refs):
        ra, rb = split(*refs)
        a.finish(*ra)
        b.finish(*rb)

    return _Ride(a.ins + b.ins, a.bufs + b.bufs, a.new_outs + b.new_outs, a.sem_shapes + b.sem_shapes, start, finish)


def _call(body, *, grid, in_specs, out_specs, out_shape, operands, name, scratch_shapes=(), aliases=None, ride=None):
    if ride is None:
        return pl.pallas_call(body, grid=grid, in_specs=in_specs, out_specs=out_specs, out_shape=out_shape,
                              scratch_shapes=list(scratch_shapes), input_output_aliases=aliases or {}, name=name,
                              compiler_params=_params())(*operands)
    multi = isinstance(out_shape, (list, tuple))
    out_shapes = list(out_shape) if multi else [out_shape]
    o_specs = list(out_specs) if multi else [out_specs]
    n_in, n_out, n_scr = len(operands), len(out_shapes), len(scratch_shapes)
    n_ri, n_rb, n_rn = len(ride.ins), len(ride.bufs), len(ride.new_outs)

    def carrying(*refs):
        k = n_in
        r_ins = refs[k:k + n_ri]
        k += n_ri + n_rb
        outs = refs[k:k + n_out]
        k += n_out
        r_bufs = refs[k:k + n_rb]
        k += n_rb
        r_new = refs[k:k + n_rn]
        k += n_rn
        scratch = refs[k:k + n_scr]
        sems = refs[k + n_scr:]
        first, last = None, None
        for axis, size in enumerate(grid):
            pid = pl.program_id(axis)
            first = (pid == 0) if first is None else first & (pid == 0)
            last = (pid == size - 1) if last is None else last & (pid == size - 1)

        @pl.when(first)
        def _():
            ride.start(r_ins, r_bufs, r_new, sems)

        body(*refs[:n_in], *outs, *scratch)

        @pl.when(last)
        def _():
            ride.finish(r_ins, r_bufs, r_new, sems)

    all_aliases = dict(aliases or {})
    for j in range(n_rb):
        all_aliases[n_in + n_ri + j] = n_out + j
    res = pl.pallas_call(
        carrying, grid=grid, in_specs=list(in_specs) + [ANY] * (n_ri + n_rb), out_specs=o_specs + [ANY] * (n_rb + n_rn),
        out_shape=out_shapes + [SDS(b.shape, b.dtype) for b in ride.bufs] + ride.new_outs,
        scratch_shapes=list(scratch_shapes) + [pltpu.SemaphoreType.DMA(s) for s in ride.sem_shapes],
        input_output_aliases=all_aliases, name=name, compiler_params=_params())(*operands, *ride.ins, *ride.bufs)
    outs = res[:n_out]
    return (list(outs) if multi else outs[0]), list(res[n_out:])


def _run_ride(name, ride):
    n_ri, n_rb, n_rn = len(ride.ins), len(ride.bufs), len(ride.new_outs)

    def body(*refs):
        r_ins = refs[:n_ri]
        r_bufs = refs[n_ri + n_rb:n_ri + 2 * n_rb]
        r_new = refs[n_ri + 2 * n_rb:n_ri + 2 * n_rb + n_rn]
        sems = refs[n_ri + 2 * n_rb + n_rn:]
        ride.start(r_ins, r_bufs, r_new, sems)
        ride.finish(r_ins, r_bufs, r_new, sems)

    return list(pl.pallas_call(
        body, in_specs=[ANY] * (n_ri + n_rb), out_specs=[ANY] * (n_rb + n_rn),
        out_shape=[SDS(b.shape, b.dtype) for b in ride.bufs] + ride.new_outs,
        scratch_shapes=[pltpu.SemaphoreType.DMA(s) for s in ride.sem_shapes],
        input_output_aliases={n_ri + j: j for j in range(n_rb)}, name=name)(*ride.ins, *ride.bufs))


def _whole(ref, p):
    return ref[...]


def _slab(ref, p):
    return ref[p]


def _matmul(name, grid, pairs, extras, outs, dims, epi, *, slabs=1, n_acc=1, ride=None):
    n_pairs, n_ex, n_out = len(pairs), len(extras), len(outs)

    def body(*refs):
        ab = refs[:2 * n_pairs]
        ex = refs[2 * n_pairs:2 * n_pairs + n_ex]
        out_refs = refs[2 * n_pairs + n_ex:2 * n_pairs + n_ex + n_out]
        pids = tuple(pl.program_id(a) for a in range(len(grid)))
        parts = [None] * n_acc
        for p in range(slabs):
            for j, (_, _, a_pick, _, _, b_pick, acc) in enumerate(pairs):
                d = _dot(a_pick(ab[2 * j], p), b_pick(ab[2 * j + 1], p), dims)
                parts[acc] = d if parts[acc] is None else parts[acc] + d
        epi(parts, ex, out_refs, pids)

    operands, in_specs = [], []
    for a, a_spec, _, b, b_spec, _, _ in pairs:
        operands += [a, b]
        in_specs += [a_spec, b_spec]
    for e, e_spec in extras:
        operands.append(e)
        in_specs.append(e_spec)
    return _call(body, grid=grid, in_specs=in_specs, out_specs=[o[1] for o in outs], out_shape=[o[0] for o in outs],
                 operands=operands, name=name, ride=ride)


def _rms_rows(v, g):
    r = lax.rsqrt(jnp.mean(v * v, axis=-1, keepdims=True) + EPS)
    return v * r * g


def _rms_fwd(name, x, g):
    t, d = x.shape
    tm = 512

    def body(x_ref, g_ref, o_ref):
        o_ref[...] = _rms_rows(x_ref[...], g_ref[...]).astype(BF16)

    return pl.pallas_call(
        body, grid=(t // tm,),
        in_specs=[pl.BlockSpec((tm, d), lambda i: (i, 0)), pl.BlockSpec((1, d), lambda i: (0, 0))],
        out_specs=pl.BlockSpec((tm, d), lambda i: (i, 0)), out_shape=SDS((t, d), BF16), name=name,
        compiler_params=_params())(x, g)


def _epi_residual_norm(accs, ex, outs, pids):
    x_new = accs[0] + ex[0][...]
    outs[0][...] = x_new
    outs[1][...] = _rms_rows(x_new, ex[1][...]).astype(BF16)


def _epi_residual_loss(accs, ex, outs, pids):
    y = accs[0] + ex[0][...]
    err = y - ex[1][...]
    dy = err * (1.0 / err.shape[-1])
    outs[0][...] = dy
    outs[2][...] = dy.astype(BF16)

    @pl.when(pids[0] == 0)
    def _():
        outs[1][...] = jnp.zeros_like(outs[1])

    outs[1][...] += jnp.sum(err * err)


def _epi_rms_bwd(accs, ex, outs, pids):
    dh = accs[0]
    xv, g, res = ex[0][...], ex[1][...], ex[2][...]
    r = lax.rsqrt(jnp.mean(xv * xv, axis=-1, keepdims=True) + EPS)
    xh = xv * r
    dy = dh * g
    dx = res + r * (dy - xh * jnp.mean(dy * xh, axis=-1, keepdims=True))
    outs[0][...] = dx
    if len(outs) > 2:
        outs[2][...] = dx.astype(BF16)

    @pl.when(pids[0] == 0)
    def _():
        outs[1][...] = jnp.zeros_like(outs[1])

    outs[1][...] += jnp.sum(dh * xh, axis=0, keepdims=True)


def _row_spec(tm, d):
    return pl.BlockSpec((tm, d), lambda i, *_: (i, 0))


def _const_spec(shape):
    nd = len(shape)
    return pl.BlockSpec(shape, lambda *_: (0,) * nd)


def _proj_in(name, h, w, layer, ride=None):
    t, d = h.shape
    n4 = w.shape[-1]
    tm = TILES["proj_in"]

    def epi(accs, ex, outs, pids):
        outs[0][...] = accs[0].astype(BF16)

    res = _matmul(
        name, (N_CHIPS, t // tm),
        [(h, pl.BlockSpec((tm, d), lambda p, i: (i, 0)), _whole,
          w, pl.BlockSpec((None, None, d, n4), lambda p, i: (p, layer, 0, 0)), _whole, 0)],
        [], [(SDS((t, N_CHIPS * n4), BF16), pl.BlockSpec((tm, n4), lambda p, i: (i, p)))],
        NN, epi, ride=ride)
    return res[0] if ride is None else (res[0][0], res[1])


def _proj_out(name, a, w, x, g_next=None, target=None):
    t, k = a.shape
    d = w.shape[-1]
    tm = TILES["proj_out"]
    if target is None:
        extras = [(x, _row_spec(tm, d)), (g_next, _const_spec((1, d)))]
        outs = [(SDS((t, d), F32), _row_spec(tm, d)), (SDS((t, d), BF16), _row_spec(tm, d))]
        epi = _epi_residual_norm
    else:
        extras = [(x, _row_spec(tm, d)), (target, _row_spec(tm, d))]
        outs = [(SDS((t, d), F32), _row_spec(tm, d)), (SDS((8, LANES), F32), _const_spec((8, LANES))),
                (SDS((t, d), BF16), _row_spec(tm, d))]
        epi = _epi_residual_loss
    return _matmul(name, (t // tm,), [(a, _row_spec(tm, k), _whole, w, _const_spec((k, d)), _whole, 0)], extras, outs, NN, epi)


def _ffn_in(name, h, wg, wu, layer, ride=None):
    t, d = h.shape
    n4 = wg.shape[-1]
    tm = TILES["ffn_in"]

    def epi(accs, ex, outs, pids):
        gate, up = accs
        outs[0][...] = gate.astype(BF16)
        outs[1][...] = up.astype(BF16)
        outs[2][...] = (gate * _sigmoid(gate) * up).astype(BF16)

    w_spec = pl.BlockSpec((None, None, d, n4), lambda p, i: (p, layer, 0, 0))
    h_spec = pl.BlockSpec((tm, d), lambda p, i: (i, 0))
    o = (SDS((N_CHIPS, t, n4), BF16), pl.BlockSpec((None, tm, n4), lambda p, i: (p, i, 0)))
    return _matmul(name, (N_CHIPS, t // tm),
                   [(h, h_spec, _whole, wg, w_spec, _whole, 0), (h, h_spec, _whole, wu, w_spec, _whole, 1)], [],
                   [o, o, o], NN, epi, n_acc=2, ride=ride)


def _ffn_out(name, act, wd, layer, x, g_next=None, target=None, ride=None):
    _, t, n4 = act.shape
    d = wd.shape[-1]
    tm = TILES["ffn_out"]
    xs = _row_spec(tm, d)
    if target is None:
        extras = [(x, xs), (g_next, _const_spec((1, d)))]
        outs = [(SDS((t, d), F32), xs), (SDS((t, d), BF16), xs)]
        epi = _epi_residual_norm
    else:
        extras = [(x, xs), (target, xs)]
        outs = [(SDS((t, d), F32), xs), (SDS((8, LANES), F32), _const_spec((8, LANES))), (SDS((t, d), BF16), xs)]
        epi = _epi_residual_loss
    return _matmul(
        name, (t // tm,),
        [(act, pl.BlockSpec((N_CHIPS, tm, n4), lambda i: (0, i, 0)), _slab,
          wd, pl.BlockSpec((N_CHIPS, None, n4, d), lambda i: (0, layer, 0, 0)), _slab, 0)],
        extras, outs, NN, epi, slabs=N_CHIPS, ride=ride)


def _ffn_dact(name, g, wd, layer, gate, up, ride=None):
    t, d = g.shape
    n4 = wd.shape[-2]
    tm = TILES["ffn_dact"]

    def epi(accs, ex, outs, pids):
        dact = accs[0]
        gt = ex[0][...].astype(F32)
        upv = ex[1][...].astype(F32)
        s = _sigmoid(gt)
        silu = gt * s
        outs[0][...] = (dact * upv * (s + silu - silu * s)).astype(BF16)
        outs[1][...] = (dact * silu).astype(BF16)

    blk = pl.BlockSpec((None, tm, n4), lambda p, i: (p, i, 0))
    o = (SDS((N_CHIPS, t, n4), BF16), blk)
    return _matmul(
        name, (N_CHIPS, t // tm),
        [(g, pl.BlockSpec((tm, d), lambda p, i: (i, 0)), _whole,
          wd, pl.BlockSpec((None, None, n4, d), lambda p, i: (p, layer, 0, 0)), _whole, 0)],
        [(gate, blk), (up, blk)], [o, o], NT, epi, ride=ride)


def _copy_epi(accs, ex, outs, pids):
    for a, o in zip(accs, outs):
        o[...] = a.astype(o.dtype)


def _dgrad_cols(name, dz_list, w_list, layer, three_d, x, g, res, bf16_copy=True, ride=None):
    t, d = x.shape
    n4 = w_list[0].shape[-1]
    tm = TILES["dgrad_cols"]
    if three_d:
        zs, z_pick = pl.BlockSpec((N_CHIPS, tm, n4), lambda i: (0, i, 0)), _slab
    else:
        zs, z_pick = _row_spec(tm, N_CHIPS * n4), (lambda ref, p: ref[:, p * n4:(p + 1) * n4])
    ws = pl.BlockSpec((N_CHIPS, None, d, n4), lambda i: (0, layer, 0, 0))
    xs = _row_spec(tm, d)
    return _matmul(
        name, (t // tm,), [(dz, zs, z_pick, w, ws, _slab, 0) for dz, w in zip(dz_list, w_list)],
        [(x, xs), (g, _const_spec((1, d))), (res, xs)],
        [(SDS((t, d), F32), xs), (SDS((1, d), F32), _const_spec((1, d)))] + ([(SDS((t, d), BF16), xs)] if bf16_copy else []),
        NT, _epi_rms_bwd, slabs=N_CHIPS, ride=ride)


def _dgrad_rows(name, g, w):
    t, d = g.shape
    k = w.shape[0]
    tm = TILES["dgrad_rows"]
    return _matmul(name, (t // tm,), [(g, _row_spec(tm, d), _whole, w, _const_spec((k, d)), _whole, 0)], [],
                   [(SDS((t, k), F32), _row_spec(tm, k))], NT, _copy_epi)[0]


A_TILE = 256


def _a_common(p_ref, lg_ref, lb_ref):
    pv = p_ref[...].astype(F32)
    a = _gelu(pv)
    u, v = a[:, :A_WIDTH], a[:, A_WIDTH:]
    vc = v - jnp.mean(v, axis=-1, keepdims=True)
    rs = lax.rsqrt(jnp.mean(vc * vc, axis=-1, keepdims=True) + EPS)
    vhat = vc * rs
    vn = vhat * lg_ref[...] + lb_ref[...]
    return pv, u, vhat, rs, vn.astype(BF16)


def _tril_weights(w_ref, g):
    r = lax.broadcasted_iota(jnp.int32, (CHUNK, CHUNK), 0)
    c = lax.broadcasted_iota(jnp.int32, (CHUNK, CHUNK), 1)
    return jnp.where(c <= r, w_ref[g], 0.0).astype(BF16), c <= r


def _mixer_a_fwd(proj, lg, lb, w, bias_t):
    t = proj.shape[0]

    def body(p_ref, lg_ref, lb_ref, w_ref, bt_ref, o_ref):
        _, u, _, _, vnb = _a_common(p_ref, lg_ref, lb_ref)
        for g in range(A_GROUPS):
            wt, _ = _tril_weights(w_ref, g)
            cs = slice(g * CHUNK, (g + 1) * CHUNK)
            for ch in range(A_TILE // CHUNK):
                rs_ = slice(ch * CHUNK, (ch + 1) * CHUNK)
                mixed = _dot(wt, vnb[rs_, cs], NN) + bt_ref[:, g:g + 1]
                o_ref[rs_, cs] = (u[rs_, cs] * mixed).astype(BF16)

    return pl.pallas_call(
        body, grid=(t // A_TILE,),
        in_specs=[pl.BlockSpec((A_TILE, 2 * A_WIDTH), lambda i: (i, 0)), _const_spec((1, A_WIDTH)),
                  _const_spec((1, A_WIDTH)), _const_spec((A_GROUPS, CHUNK, CHUNK)), _const_spec((CHUNK, A_GROUPS))],
        out_specs=pl.BlockSpec((A_TILE, A_WIDTH), lambda i: (i, 0)), out_shape=SDS((t, A_WIDTH), BF16),
        name="mixer_a_fwd", compiler_params=_params())(proj, lg, lb, w, bias_t)


def _mixer_a_bwd(proj, dcat, lg, lb, w, bias_t):
    t = proj.shape[0]

    def body(p_ref, da_ref, lg_ref, lb_ref, w_ref, bt_ref, dp_ref, dw_ref, dbt_ref, dlg_ref, dlb_ref, du_scr, dvn_scr):
        @pl.when(pl.program_id(0) == 0)
        def _():
            dw_ref[...] = jnp.zeros_like(dw_ref)
            dbt_ref[...] = jnp.zeros_like(dbt_ref)
            dlg_ref[...] = jnp.zeros_like(dlg_ref)
            dlb_ref[...] = jnp.zeros_like(dlb_ref)

        pv, u, vhat, rs, vnb = _a_common(p_ref, lg_ref, lb_ref)
        da = da_ref[...]
        for g in range(A_GROUPS):
            wt, keep = _tril_weights(w_ref, g)
            cs = slice(g * CHUNK, (g + 1) * CHUNK)
            for ch in range(A_TILE // CHUNK):
                rs_ = slice(ch * CHUNK, (ch + 1) * CHUNK)
                vg = vnb[rs_, cs]
                mixed = _dot(wt, vg, NN) + bt_ref[:, g:g + 1]
                du_scr[rs_, cs] = da[rs_, cs] * mixed
                dmx = da[rs_, cs] * u[rs_, cs]
                dw_ref[g] += jnp.where(keep, _dot(dmx, vg, NT), 0.0)
                dvn_scr[rs_, cs] = _dot(wt, dmx, TN)
                dbt_ref[:, g:g + 1] += jnp.sum(dmx, axis=1, keepdims=True)
        dvn = dvn_scr[...]
        dlg_ref[...] += jnp.sum(dvn * vhat, axis=0, keepdims=True)
        dlb_ref[...] += jnp.sum(dvn, axis=0, keepdims=True)
        dvh = dvn * lg_ref[...]
        dv = rs * (dvh - jnp.mean(dvh, axis=-1, keepdims=True) - vhat * jnp.mean(dvh * vhat, axis=-1, keepdims=True))
        gp = _gelu_grad(pv)
        dp_ref[:, :A_WIDTH] = (du_scr[...] * gp[:, :A_WIDTH]).astype(BF16)
        dp_ref[:, A_WIDTH:] = (dv * gp[:, A_WIDTH:]).astype(BF16)

    return pl.pallas_call(
        body, grid=(t // A_TILE,),
        in_specs=[pl.BlockSpec((A_TILE, 2 * A_WIDTH), lambda i: (i, 0)), pl.BlockSpec((A_TILE, A_WIDTH), lambda i: (i, 0)),
                  _const_spec((1, A_WIDTH)), _const_spec((1, A_WIDTH)), _const_spec((A_GROUPS, CHUNK, CHUNK)),
                  _const_spec((CHUNK, A_GROUPS))],
        out_specs=[pl.BlockSpec((A_TILE, 2 * A_WIDTH), lambda i: (i, 0)), _const_spec((A_GROUPS, CHUNK, CHUNK)),
                   _const_spec((CHUNK, A_GROUPS)), _const_spec((1, A_WIDTH)), _const_spec((1, A_WIDTH))],
        out_shape=[SDS((t, 2 * A_WIDTH), BF16), SDS((A_GROUPS, CHUNK, CHUNK), F32), SDS((CHUNK, A_GROUPS), F32),
                   SDS((1, A_WIDTH), F32), SDS((1, A_WIDTH), F32)],
        scratch_shapes=[pltpu.VMEM((A_TILE, A_WIDTH), F32), pltpu.VMEM((A_TILE, A_WIDTH), F32)],
        name="mixer_a_bwd", compiler_params=_params())(proj, dcat, lg, lb, w, bias_t)


def _rope_tables(t):
    half = ROPE_DIM // 2
    inv_freq = ROPE_THETA ** (-jnp.arange(half, dtype=F32) * 2.0 / ROPE_DIM)
    ang = jnp.arange(t, dtype=F32)[:, None] * inv_freq[None, :]
    cos, sin = jnp.cos(ang), jnp.sin(ang)
    one = jnp.ones((t, HEAD_DIM - ROPE_DIM), F32)
    zero = jnp.zeros((t, HEAD_DIM - ROPE_DIM), F32)
    zh = jnp.zeros((t, half), F32)
    c = jnp.concatenate([cos, cos, one], axis=1)
    s1 = jnp.concatenate([-sin, zh, zero], axis=1)
    s2 = jnp.concatenate([zh, sin, zero], axis=1)
    return tuple(jnp.tile(a, (1, LANES // HEAD_DIM)) for a in (c, s1, s2))


QK_TILE = 512
QK_COLS = 2 * N_DIL * B_WIDTH


def _qk_fwd(proj, gains, tabs, ride=None):
    t = proj.shape[0]
    col0 = 2 * A_WIDTH // 1024

    def body(p_ref, g_ref, c_ref, s1_ref, s2_ref, o_ref):
        seg = _segment_mean_matrix(HEAD_DIM)
        c, s1, s2 = c_ref[...], s1_ref[...], s2_ref[...]
        for ci in range(1024 // LANES):
            ls = slice(ci * LANES, (ci + 1) * LANES)
            xv = p_ref[:, ls].astype(F32)
            r = lax.rsqrt(_segment_dot(xv * xv, seg) + EPS)
            y = xv * r * g_ref[:, ls]
            o_ref[:, ls] = (y * c + pltpu.roll(y, LANES - 8, axis=1) * s1 + pltpu.roll(y, 8, axis=1) * s2).astype(BF16)

    tab = pl.BlockSpec((QK_TILE, LANES), lambda i, j: (i, 0))
    return _call(
        body, grid=(t // QK_TILE, QK_COLS // 1024),
        in_specs=[pl.BlockSpec((QK_TILE, 1024), lambda i, j: (i, col0 + j)), pl.BlockSpec((1, 1024), lambda i, j: (0, j)),
                  tab, tab, tab],
        out_specs=pl.BlockSpec((QK_TILE, 1024), lambda i, j: (i, j)), out_shape=SDS((t, QK_COLS), BF16),
        operands=[proj, gains, *tabs], name="qk_norm_rope_fwd", ride=ride)


PERM_TILE = 512


def _permute(name, items, rate):
    t = items[0][0].shape[0]
    n = len(items)
    rows = PERM_TILE // rate

    def body(*refs):
        scr = refs[-1]
        for x_ref, o_ref in zip(refs[:n], refs[n:2 * n]):
            for ci in range(B_WIDTH // LANES):
                scr[ci] = x_ref[:, ci * LANES:(ci + 1) * LANES].astype(F32)
            for rho in range(rate):
                for ci in range(B_WIDTH // LANES):
                    o_ref[rho, :, ci * LANES:(ci + 1) * LANES] = scr[ci, pl.ds(rho, rows, stride=rate), :].astype(o_ref.dtype)

    return pl.pallas_call(
        body, grid=(t // PERM_TILE,),
        in_specs=[pl.BlockSpec((PERM_TILE, B_WIDTH), functools.partial(lambda cb, i: (i, cb), cb)) for _, cb in items],
        out_specs=[pl.BlockSpec((rate, rows, B_WIDTH), lambda i: (0, i, 0)) for _ in items],
        out_shape=[SDS((rate, t // rate, B_WIDTH), a.dtype) for a, _ in items],
        scratch_shapes=[pltpu.VMEM((B_WIDTH // LANES, PERM_TILE, LANES), F32)],
        name=name, compiler_params=_params())(*[a for a, _ in items])


def _unpermute(name, arrays, rate):
    t = arrays[0].shape[1] * rate
    n = len(arrays)
    rows = PERM_TILE // rate

    def body(*refs):
        scr = refs[-1]
        for x_ref, o_ref in zip(refs[:n], refs[n:2 * n]):
            for rho in range(rate):
                for ci in range(B_WIDTH // LANES):
                    scr[ci, pl.ds(rho, rows, stride=rate), :] = x_ref[rho, :, ci * LANES:(ci + 1) * LANES].astype(F32)
            for ci in range(B_WIDTH // LANES):
                o_ref[:, ci * LANES:(ci + 1) * LANES] = scr[ci].astype(o_ref.dtype)

    return pl.pallas_call(
        body, grid=(t // PERM_TILE,),
        in_specs=[pl.BlockSpec((rate, rows, B_WIDTH), lambda i: (0, i, 0)) for _ in arrays],
        out_specs=[pl.BlockSpec((PERM_TILE, B_WIDTH), lambda i: (i, 0)) for _ in arrays],
        out_shape=[SDS((t, B_WIDTH), a.dtype) for a in arrays],
        scratch_shapes=[pltpu.VMEM((B_WIDTH // LANES, PERM_TILE, LANES), F32)],
        name=name, compiler_params=_params())(*arrays)


def _head_lane_mask(h):
    lane = lax.broadcasted_iota(jnp.int32, (1, LANES), 1)
    return (lane < HEAD_DIM) if h == 0 else (lane >= HEAD_DIM)


def _attn_fwd(name, q, k, v, ride=None):
    rate, length = q[0].shape[0], q[0].shape[1]
    nb = length // ATT_BLOCK
    scale = HEAD_DIM ** -0.5

    def body(q_ref, kc_ref, kp_ref, vc_ref, vp_ref, o_ref, l_ref):
        n = pl.program_id(1)
        qi = lax.broadcasted_iota(jnp.int32, (ATT_BLOCK, 2 * ATT_BLOCK), 0)
        cj = lax.broadcasted_iota(jnp.int32, (ATT_BLOCK, 2 * ATT_BLOCK), 1)
        has_prev = jnp.where(n > 0, 0, 2 * ATT_BLOCK)
        mask = ((cj < ATT_BLOCK) & (cj >= qi + has_prev)) | ((cj >= ATT_BLOCK) & (cj - ATT_BLOCK <= qi))
        for hp in range(B_WIDTH // LANES):
            ls = slice(hp * LANES, (hp + 1) * LANES)
            q2 = q_ref[:, ls]
            k2 = jnp.concatenate([kp_ref[:, ls], kc_ref[:, ls]], axis=0)
            v2 = jnp.concatenate([vp_ref[:, ls], vc_ref[:, ls]], axis=0)
            o_acc, lse2 = None, None
            for h in range(2):
                hm = _head_lane_mask(h)
                s = _dot(jnp.where(hm, q2, jnp.zeros_like(q2)), k2, NT) * scale
                s = jnp.where(mask, s, NEG_INF)
                m = jnp.max(s, axis=1, keepdims=True)
                p = jnp.exp(s - m)
                den = jnp.sum(p, axis=1, keepdims=True)
                lse = m + jnp.log(den)
                o = _dot(p / den, jnp.where(hm, v2, jnp.zeros_like(v2)), NN)
                o_acc = o if h == 0 else o_acc + o
                lse_b = lse + jnp.zeros((ATT_BLOCK, LANES), F32)
                lse2 = lse_b if h == 0 else jnp.where(hm, lse_b, lse2)
            o_ref[:, ls] = o_acc
            l_ref[:, ls] = lse2

    def cur(cb):
        return pl.BlockSpec((None, ATT_BLOCK, B_WIDTH), lambda r, n: (r, n, cb))

    def prev(cb):
        return pl.BlockSpec((None, ATT_BLOCK, B_WIDTH), lambda r, n: (r, jnp.maximum(n - 1, 0), cb))

    out = pl.BlockSpec((None, ATT_BLOCK, B_WIDTH), lambda r, n: (r, n, 0))
    return _call(
        body, grid=(rate, nb),
        in_specs=[cur(q[1]), cur(k[1]), prev(k[1]), cur(v[1]), prev(v[1])],
        out_specs=[out, out], out_shape=[SDS((rate, length, B_WIDTH), F32)] * 2,
        operands=[q[0], k[0], k[0], v[0], v[0]], name=name, ride=ride)


def _attn_merge(a_out, o_list, l_list):
    t = a_out.shape[0]
    tm = 512

    def body(a_ref, o0, o1, o2, l0, l1, l2, cat_ref, lt_ref):
        ls = [l0[...], l1[...], l2[...]]
        m = jnp.maximum(jnp.maximum(ls[0], ls[1]), ls[2])
        es = [jnp.exp(l - m) for l in ls]
        den = es[0] + es[1] + es[2]
        b = (es[0] * o0[...] + es[1] * o1[...] + es[2] * o2[...]) / den
        cat_ref[:, :A_WIDTH] = a_ref[...]
        cat_ref[:, A_WIDTH:] = b.astype(BF16)
        lt_ref[...] = m + jnp.log(den)

    blk = _row_spec(tm, B_WIDTH)
    return pl.pallas_call(
        body, grid=(t // tm,), in_specs=[blk] * 7,
        out_specs=[_row_spec(tm, A_WIDTH + B_WIDTH), blk],
        out_shape=[SDS((t, A_WIDTH + B_WIDTH), BF16), SDS((t, B_WIDTH), F32)],
        name="attn_merge", compiler_params=_params())(a_out, *o_list, *l_list)


def _attn_bwd_prep(dcat, cat):
    t = dcat.shape[0]
    tm = 512

    def body(d_ref, b_ref, db_ref, dd_ref):
        seg = _segment_mean_matrix(HEAD_DIM, scale=1.0)
        for ci in range(B_WIDTH // LANES):
            ls = slice(ci * LANES, (ci + 1) * LANES)
            d = d_ref[:, ls]
            db_ref[:, ls] = d.astype(BF16)
            dd_ref[:, ls] = _segment_dot(d * b_ref[:, ls].astype(F32), seg)

    right = pl.BlockSpec((tm, B_WIDTH), lambda i: (i, 1))
    blk = _row_spec(tm, B_WIDTH)
    return pl.pallas_call(
        body, grid=(t // tm,), in_specs=[right, right], out_specs=[blk, blk],
        out_shape=[SDS((t, B_WIDTH), BF16), SDS((t, B_WIDTH), F32)],
        name="attn_bwd_prep", compiler_params=_params())(dcat, cat)


def _attn_bwd(name, q, k, v, db, lse, dd, ride=None):
    rate, length = db.shape[0], db.shape[1]
    nb = length // ATT_BLOCK
    scale = HEAD_DIM ** -0.5

    def body(qa_ref, qb_ref, k_ref, v_ref, dba_ref, dbb_ref, la_ref, lb_ref, da_ref, dbd_ref, dq_ref, dk_ref, dv_ref, carry):
        m = pl.program_id(1)

        @pl.when(m == 0)
        def _():
            carry[...] = jnp.zeros_like(carry)

        row = lax.broadcasted_iota(jnp.int32, (2 * ATT_BLOCK, ATT_BLOCK), 0)
        kj = lax.broadcasted_iota(jnp.int32, (2 * ATT_BLOCK, ATT_BLOCK), 1)
        no_next = jnp.where(m + 1 < nb, 0, 2 * ATT_BLOCK)
        mask = ((row < ATT_BLOCK) & (kj <= row)) | ((row >= ATT_BLOCK) & (kj >= row - ATT_BLOCK + no_next))
        for hp in range(B_WIDTH // LANES):
            ls = slice(hp * LANES, (hp + 1) * LANES)
            k2, v2 = k_ref[:, ls], v_ref[:, ls]
            q2 = jnp.concatenate([qa_ref[:, ls], qb_ref[:, ls]], axis=0)
            db2 = jnp.concatenate([dba_ref[:, ls], dbb_ref[:, ls]], axis=0)
            lse2 = jnp.concatenate([la_ref[:, ls], lb_ref[:, ls]], axis=0)
            dd2 = jnp.concatenate([da_ref[:, ls], dbd_ref[:, ls]], axis=0)
            dq_acc, dk_acc, dv_acc = None, None, None
            for h in range(2):
                hm = _head_lane_mask(h)
                km = jnp.where(hm, k2, jnp.zeros_like(k2))
                vm = jnp.where(hm, v2, jnp.zeros_like(v2))
                lse_col = jnp.max(jnp.where(hm, lse2, NEG_INF), axis=1, keepdims=True)
                dd_col = jnp.max(jnp.where(hm, dd2, NEG_INF), axis=1, keepdims=True)
                s = _dot(q2, km, NT) * scale
                p = jnp.where(mask, jnp.exp(s - lse_col), 0.0)
                dvc = _dot(p, jnp.where(hm, db2, jnp.zeros_like(db2)), TN)
                dp = _dot(db2, vm, NT)
                ds = (p * (dp - dd_col) * scale).astype(BF16)
                dqc = _dot(ds, km, NN)
                dkc = _dot(ds, jnp.where(hm, q2, jnp.zeros_like(q2)), TN)
                dq_acc = dqc if dq_acc is None else dq_acc + dqc
                dk_acc = dkc if dk_acc is None else dk_acc + dkc
                dv_acc = dvc if dv_acc is None else dv_acc + dvc
            dq_ref[:, ls] = (dq_acc[:ATT_BLOCK] + carry[:, ls]).astype(BF16)
            carry[:, ls] = dq_acc[ATT_BLOCK:]
            dk_ref[:, ls] = dk_acc.astype(BF16)
            dv_ref[:, ls] = dv_acc.astype(BF16)

    def cur(cb):
        return pl.BlockSpec((None, ATT_BLOCK, B_WIDTH), lambda r, n: (r, n, cb))

    def nxt(cb):
        return pl.BlockSpec((None, ATT_BLOCK, B_WIDTH), lambda r, n: (r, jnp.minimum(n + 1, nb - 1), cb))

    out = cur(0)
    return _call(
        body, grid=(rate, nb),
        in_specs=[cur(q[1]), nxt(q[1]), cur(k[1]), cur(v[1]), cur(0), nxt(0), cur(0), nxt(0), cur(0), nxt(0)],
        out_specs=[out, out, out], out_shape=[SDS((rate, length, B_WIDTH), BF16)] * 3,
        scratch_shapes=[pltpu.VMEM((ATT_BLOCK, B_WIDTH), F32)],
        operands=[q[0], q[0], k[0], v[0], db, db, lse, lse, dd, dd], name=name, ride=ride)


AB_IN = 2 * A_WIDTH + 3 * N_DIL * B_WIDTH
ASM_TILE = 256


def _dproj_assemble(proj, d_a, dqk, dv, gains, tabs):
    t = proj.shape[0]
    n_qk = 2 * N_DIL

    def body(p_ref, da_ref, *rest):
        dqk_refs = rest[:n_qk]
        dv_refs = rest[n_qk:n_qk + N_DIL]
        g_ref, c_ref, s1_ref, s2_ref, o_ref, dg_ref = rest[n_qk + N_DIL:]

        @pl.when(pl.program_id(0) == 0)
        def _():
            dg_ref[...] = jnp.zeros_like(dg_ref)

        seg = _segment_mean_matrix(HEAD_DIM)
        c, s1, s2 = c_ref[...], s1_ref[...], s2_ref[...]
        o_ref[:, :2 * A_WIDTH] = da_ref[...]
        for jg in range(n_qk):
            for ci in range(B_WIDTH // LANES):
                col = jg * B_WIDTH + ci * LANES
                src = slice(2 * A_WIDTH + col, 2 * A_WIDTH + col + LANES)
                xv = p_ref[:, src].astype(F32)
                r = lax.rsqrt(_segment_dot(xv * xv, seg) + EPS)
                xh = xv * r
                gain = g_ref[:, col:col + LANES]
                do = dqk_refs[jg][:, ci * LANES:(ci + 1) * LANES].astype(F32)
                dy = do * c + pltpu.roll(do * s1, 8, axis=1) + pltpu.roll(do * s2, LANES - 8, axis=1)
                dg_ref[:, col:col + LANES] += jnp.sum(dy * xh, axis=0, keepdims=True)
                dxh = dy * gain
                o_ref[:, src] = (r * (dxh - xh * _segment_dot(dxh * xh, seg))).astype(BF16)
        v0 = 2 * A_WIDTH + QK_COLS
        for g in range(N_DIL):
            o_ref[:, v0 + g * B_WIDTH:v0 + (g + 1) * B_WIDTH] = dv_refs[g][...]

    blk = _row_spec(ASM_TILE, B_WIDTH)
    tab = _row_spec(ASM_TILE, LANES)
    return pl.pallas_call(
        body, grid=(t // ASM_TILE,),
        in_specs=[_row_spec(ASM_TILE, AB_IN), _row_spec(ASM_TILE, 2 * A_WIDTH)] + [blk] * (n_qk + N_DIL)
        + [_const_spec((1, QK_COLS)), tab, tab, tab],
        out_specs=[_row_spec(ASM_TILE, AB_IN), _const_spec((1, QK_COLS))],
        out_shape=[SDS((t, AB_IN), BF16), SDS((1, QK_COLS), F32)],
        name="dproj_assemble", compiler_params=_params())(proj, d_a, *dqk, *dv, gains, *tabs)


def _fold_heads(dg_lane):
    n = dg_lane.shape[1]

    def body(x_ref, o_ref):
        r = lax.broadcasted_iota(jnp.int32, (B_WIDTH, B_WIDTH), 0) % HEAD_DIM
        c = lax.broadcasted_iota(jnp.int32, (B_WIDTH, B_WIDTH), 1) % HEAD_DIM
        fold = jnp.where(r == c, 1.0, 0.0).astype(F32)
        for jg in range(n // B_WIDTH):
            ls = slice(jg * B_WIDTH, (jg + 1) * B_WIDTH)
            o_ref[:, ls] = _dot_hi(jnp.broadcast_to(x_ref[:, ls], (8, B_WIDTH)), fold)

    return pl.pallas_call(body, out_shape=SDS((8, n), F32), name="fold_heads", compiler_params=_params())(dg_lane)


CD_TILE = 256
CD_IN = 2 * C_WIDTH + 3 * 512


def _cd_split(pv):
    w = C_WIDTH
    return pv[:, :w], pv[:, w:2 * w], pv[:, 2 * w:3 * w], pv[:, 3 * w:4 * w], pv[:, 4 * w:5 * w]


def _shifted_copies(src, dst, rows):
    dst[0, :rows] = src[...]
    for b in range(1, 8):
        dst[b, :rows - 8] = src[pl.ds(b, rows - 8), :]


def _rows_from(shifted, start, n):
    b = start % 8
    return shifted[b, pl.ds(start - b, n), :]


def _mixer_cd_fwd(proj, cw, cb, lg, lb, dw):
    t = proj.shape[0]
    per = CD_TILE // HALO

    def body(h_ref, m_ref, cw_ref, cb_ref, lg_ref, lb_ref, dw_ref, o_ref, c_scr, e_scr, c_sh):
        not_first = (pl.program_id(0) > 0).astype(F32)
        ha, hg, _, hgc, hhv = _cd_split(h_ref[...].astype(F32))
        ma, mg, mgb, mgc, mhv = _cd_split(m_ref[...].astype(F32))
        c_scr[:HALO] = ha * _sigmoid(hg) * not_first
        c_scr[HALO:] = ma * _sigmoid(mg)
        e_scr[:HALO] = hgc * hhv * not_first
        e_scr[HALO:] = mgc * mhv
        _shifted_copies(c_scr, c_sh, HALO + CD_TILE)
        acc = jnp.zeros((CD_TILE, C_WIDTH), F32)
        for k in range(C_KERNEL):
            acc = acc + cw_ref[k:k + 1, :] * _rows_from(c_sh, HALO - (C_KERNEL - 1) + k, CD_TILE)
        c1 = acc + cb_ref[...]
        cc = c1 - jnp.mean(c1, axis=-1, keepdims=True)
        c2 = cc * lax.rsqrt(jnp.mean(cc * cc, axis=-1, keepdims=True) + EPS) * lg_ref[...] + lb_ref[...]
        o_ref[:, :C_WIDTH] = (c2 * _sigmoid(c2)).astype(BF16)
        d1 = jnp.zeros((CD_TILE, C_WIDTH), F32)
        for k in range(D_KERNEL):
            d1 = d1 + dw_ref[k:k + 1, :] * e_scr[pl.ds(HALO - (D_KERNEL - 1) + k, CD_TILE), :]
        o_ref[:, C_WIDTH:] = (mgb * d1).astype(BF16)

    return pl.pallas_call(
        body, grid=(t // CD_TILE,),
        in_specs=[pl.BlockSpec((HALO, CD_IN), lambda i: (jnp.maximum(i * per - 1, 0), 0)), _row_spec(CD_TILE, CD_IN),
                  _const_spec((32, C_WIDTH)), _const_spec((1, C_WIDTH)), _const_spec((1, C_WIDTH)), _const_spec((1, C_WIDTH)),
                  _const_spec((8, C_WIDTH))],
        out_specs=_row_spec(CD_TILE, 2 * C_WIDTH), out_shape=SDS((t, 2 * C_WIDTH), BF16),
        scratch_shapes=[pltpu.VMEM((HALO + CD_TILE, C_WIDTH), F32)] * 2 + [pltpu.VMEM((8, HALO + CD_TILE, C_WIDTH), F32)],
        name="mixer_cd_fwd", compiler_params=_params())(proj, proj, cw, cb, lg, lb, dw)


def _mixer_cd_bwd(proj, dcat, cw, cb, lg, lb, dw, ride=None):
    t = proj.shape[0]
    per = CD_TILE // HALO
    nt = t // CD_TILE
    ext = CD_TILE + HALO

    def body(hp_ref, m_ref, hn_ref, dm_ref, dn_ref, cw_ref, cb_ref, lg_ref, lb_ref, dw_ref,
             dp_ref, dcw_ref, dcb_ref, dlg_ref, dlb_ref, ddw_ref, c_scr, e_scr, dc1_scr, dd1_scr, c_sh, dc1_sh):
        i = pl.program_id(0)

        @pl.when(i == 0)
        def _():
            for r in (dcw_ref, dcb_ref, dlg_ref, dlb_ref, ddw_ref):
                r[...] = jnp.zeros_like(r)

        not_first = (i > 0).astype(F32)
        not_last = (i < nt - 1).astype(F32)
        pa, pg, _, pgc, phv = _cd_split(hp_ref[...].astype(F32))
        ma, mg, mgb, mgc, mhv = _cd_split(m_ref[...].astype(F32))
        na, ng, ngb, ngc, nhv = _cd_split(hn_ref[...].astype(F32))
        sig_m = _sigmoid(mg)
        c_scr[:HALO] = pa * _sigmoid(pg) * not_first
        c_scr[HALO:HALO + CD_TILE] = ma * sig_m
        c_scr[HALO + CD_TILE:] = na * _sigmoid(ng) * not_last
        e_scr[:HALO] = pgc * phv * not_first
        e_scr[HALO:HALO + CD_TILE] = mgc * mhv
        e_scr[HALO + CD_TILE:] = ngc * nhv * not_last

        _shifted_copies(c_scr, c_sh, 2 * HALO + CD_TILE)
        acc = jnp.zeros((ext, C_WIDTH), F32)
        for k in range(C_KERNEL):
            acc = acc + cw_ref[k:k + 1, :] * _rows_from(c_sh, HALO - (C_KERNEL - 1) + k, ext)
        c1 = acc + cb_ref[...]
        cc = c1 - jnp.mean(c1, axis=-1, keepdims=True)
        rs = lax.rsqrt(jnp.mean(cc * cc, axis=-1, keepdims=True) + EPS)
        vhat = cc * rs
        c2 = vhat * lg_ref[...] + lb_ref[...]
        sig = _sigmoid(c2)
        dc = jnp.concatenate([dm_ref[:, :C_WIDTH], dn_ref[:, :C_WIDTH] * not_last], axis=0)
        dc2 = dc * (sig * (1.0 + c2 * (1.0 - sig)))
        dvh = dc2 * lg_ref[...]
        dc1 = rs * (dvh - jnp.mean(dvh, axis=-1, keepdims=True) - vhat * jnp.mean(dvh * vhat, axis=-1, keepdims=True))
        dc1_scr[...] = dc1
        _shifted_copies(dc1_scr, dc1_sh, ext)
        dlg_ref[...] += jnp.sum((dc2 * vhat)[:CD_TILE], axis=0, keepdims=True)
        dlb_ref[...] += jnp.sum(dc2[:CD_TILE], axis=0, keepdims=True)
        dc1_m = dc1[:CD_TILE]
        dcb_ref[...] += jnp.sum(dc1_m, axis=0, keepdims=True)
        dc0 = jnp.zeros((CD_TILE, C_WIDTH), F32)
        for k in range(C_KERNEL):
            dc0 = dc0 + cw_ref[k:k + 1, :] * _rows_from(dc1_sh, C_KERNEL - 1 - k, CD_TILE)
            dcw_ref[k:k + 1, :] += jnp.sum(dc1_m * _rows_from(c_sh, HALO - (C_KERNEL - 1) + k, CD_TILE), axis=0, keepdims=True)
        dp_ref[:, :C_WIDTH] = (dc0 * sig_m).astype(BF16)
        dp_ref[:, C_WIDTH:2 * C_WIDTH] = (dc0 * ma * sig_m * (1.0 - sig_m)).astype(BF16)

        d1 = jnp.zeros((CD_TILE, C_WIDTH), F32)
        for k in range(D_KERNEL):
            d1 = d1 + dw_ref[k:k + 1, :] * e_scr[pl.ds(HALO - (D_KERNEL - 1) + k, CD_TILE), :]
        dd_m = dm_ref[:, C_WIDTH:]
        dd1 = jnp.concatenate([dd_m * mgb, dn_ref[:, C_WIDTH:] * ngb * not_last], axis=0)
        dd1_scr[...] = dd1
        dp_ref[:, 2 * C_WIDTH:3 * C_WIDTH] = (dd_m * d1).astype(BF16)
        de = jnp.zeros((CD_TILE, C_WIDTH), F32)
        for k in range(D_KERNEL):
            de = de + dw_ref[k:k + 1, :] * dd1_scr[pl.ds(D_KERNEL - 1 - k, CD_TILE), :]
            ddw_ref[k:k + 1, :] += jnp.sum(dd1[:CD_TILE] * e_scr[pl.ds(HALO - (D_KERNEL - 1) + k, CD_TILE), :], axis=0, keepdims=True)
        dp_ref[:, 3 * C_WIDTH:4 * C_WIDTH] = (de * mhv).astype(BF16)
        dp_ref[:, 4 * C_WIDTH:] = (de * mgc).astype(BF16)

    halo_prev = lambda i: (jnp.maximum(i * per - 1, 0), 0)
    halo_next = lambda i: (jnp.minimum((i + 1) * per, t // HALO - 1), 0)
    vec = _const_spec((1, C_WIDTH))
    return _call(
        body, grid=(nt,),
        in_specs=[pl.BlockSpec((HALO, CD_IN), halo_prev), _row_spec(CD_TILE, CD_IN), pl.BlockSpec((HALO, CD_IN), halo_next),
                  _row_spec(CD_TILE, 2 * C_WIDTH), pl.BlockSpec((HALO, 2 * C_WIDTH), halo_next),
                  _const_spec((32, C_WIDTH)), vec, vec, vec, _const_spec((8, C_WIDTH))],
        out_specs=[_row_spec(CD_TILE, CD_IN), _const_spec((32, C_WIDTH)), vec, vec, vec, _const_spec((8, C_WIDTH))],
        out_shape=[SDS((t, CD_IN), BF16), SDS((32, C_WIDTH), F32), SDS((1, C_WIDTH), F32), SDS((1, C_WIDTH), F32),
                   SDS((1, C_WIDTH), F32), SDS((8, C_WIDTH), F32)],
        scratch_shapes=[pltpu.VMEM((2 * HALO + CD_TILE, C_WIDTH), F32)] * 2 + [pltpu.VMEM((ext, C_WIDTH), F32)] * 2
        + [pltpu.VMEM((8, 2 * HALO + CD_TILE, C_WIDTH), F32), pltpu.VMEM((8, ext, C_WIDTH), F32)],
        operands=[proj, proj, proj, dcat, dcat, cw, cb, lg, lb, dw], name="mixer_cd_bwd", ride=ride)


def _wgrad(name, lhs, lhs_spec, rhs_list, rhs_spec, out_rc, t, ride):
    tk = TILES["wgrad"]
    r, c = out_rc
    n = len(rhs_list)

    def body(*refs):
        ab, out_refs = refs[:2 * n], refs[2 * n:]
        k = pl.program_id(1)
        parts = [_dot(ab[2 * j][...], ab[2 * j + 1][...], TN) for j in range(n)]

        @pl.when(k == 0)
        def _():
            for a in range(n):
                out_refs[a][...] = parts[a]

        @pl.when(k > 0)
        def _():
            for a in range(n):
                out_refs[a][...] += parts[a]

    operands, in_specs = [], []
    for rhs in rhs_list:
        operands += [lhs, rhs]
        in_specs += [lhs_spec, rhs_spec]
    res = _call(body, grid=(N_CHIPS, t // tk), in_specs=in_specs,
                out_specs=[pl.BlockSpec((None, r, c), lambda p, k: (p, 0, 0))] * n,
                out_shape=[SDS((N_CHIPS, r, c), F32)] * n, operands=operands, name=name, ride=ride)
    outs, ride_res = (res, None) if ride is None else res
    outs = [o.reshape(N_CHIPS, 2, r // 2, c) for o in outs]
    return outs if ride is None else (outs, ride_res)


def _wgrad_col_sharded(name, h, dz_list, three_d, ride=None):
    t, d = h.shape
    tk = TILES["wgrad"]
    n4 = dz_list[0].shape[-1] if three_d else dz_list[0].shape[-1] // N_CHIPS
    hs = pl.BlockSpec((tk, d), lambda p, k: (k, 0))
    zs = pl.BlockSpec((None, tk, n4), lambda p, k: (p, k, 0)) if three_d else pl.BlockSpec((tk, n4), lambda p, k: (k, p))
    return _wgrad(name, h, hs, dz_list, zs, (d, n4), t, ride)


def _wgrad_row_sharded(name, a, g, three_d, ride=None):
    t, d = g.shape
    tk = TILES["wgrad"]
    k4 = a.shape[-1] if three_d else a.shape[-1] // N_CHIPS
    a_spec = pl.BlockSpec((None, tk, k4), lambda p, k: (p, k, 0)) if three_d else pl.BlockSpec((tk, k4), lambda p, k: (k, p))
    gs = pl.BlockSpec((tk, d), lambda p, k: (k, 0))
    res = _wgrad(name, a, a_spec, [g], gs, (k4, d), t, ride)
    return res[0] if ride is None else (res[0][0], res[1])


def _mesh_scalars():
    return jnp.stack([lax.axis_index("c"), 2 * lax.axis_index("x") + lax.axis_index("y")]).astype(jnp.int32)


def _stage_own(name, w, layer, dtype):
    layers, r, cols = w.shape
    h = r // 2

    def body(s_ref, x_ref, o_ref):
        o_ref[...] = x_ref[...].astype(dtype)

    return pl.pallas_call(
        body,
        grid_spec=pltpu.PrefetchScalarGridSpec(
            num_scalar_prefetch=1, grid=(2,),
            in_specs=[pl.BlockSpec((None, h, cols), lambda i, s: (2 * layer + i, 0, 0))],
            out_specs=pl.BlockSpec((None, None, h, cols), lambda i, s: (s[1], i, 0, 0))),
        out_shape=SDS((N_CHIPS, 2, h, cols), dtype), name=name,
        compiler_params=_params())(_mesh_scalars(), w.reshape(2 * layers, h, cols))


def _remote(src, dst, send_sem, recv_sem, device):
    return pltpu.make_async_remote_copy(src, dst, send_sem, recv_sem, device_id=device, device_id_type=MESH)


def _ride_gather_send(bufs):
    n = len(bufs)

    def each(b, sems, act):
        send, recv = sems
        x, y, c, p, others = _position()
        for t in range(n):
            for j, (qx, qy) in enumerate(others):
                act(b[t].at[p, c], b[t].at[2 * qx + qy, c], send.at[t, j], recv.at[t, j], (qx, qy, c))

    def start(ins, b, new, sems):
        each(b, sems, lambda mine, landed, s, r, dev: _remote(mine, mine, s, r, dev).start())

    def finish(ins, b, new, sems):
        def act(mine, landed, s, r, dev):
            _remote(mine, mine, s, r, dev).wait_send()
            _remote(landed, landed, s, r, dev).wait_recv()
        each(b, sems, act)

    return _Ride([], bufs, [], [(n, 3), (n, 3)], start, finish)


def _ride_gather_pass(bufs):
    n = len(bufs)

    def each(b, sems, act):
        send, recv = sems
        x, y, c, p, others = _position()
        for t in range(n):
            for j, (qx, qy) in enumerate(others):
                act(b[t].at[2 * qx + qy, c], b[t].at[2 * qx + qy, 1 - c], send.at[t, j], recv.at[t, j], (x, y, 1 - c))

    def start(ins, b, new, sems):
        each(b, sems, lambda landed, passed, s, r, dev: _remote(landed, landed, s, r, dev).start())

    def finish(ins, b, new, sems):
        def act(landed, passed, s, r, dev):
            _remote(landed, landed, s, r, dev).wait_send()
            _remote(passed, passed, s, r, dev).wait_recv()
        each(b, sems, act)

    return _Ride([], bufs, [], [(n, 3), (n, 3)], start, finish)


def _ride_swap(tensors):
    n = len(tensors)

    def each(ins, new, sems, act):
        send, recv = sems
        x, y, c, _, _ = _position()
        for t in range(n):
            act(_remote(ins[t].at[:, 1 - c], new[t], send.at[t], recv.at[t], (x, y, 1 - c)))

    def start(ins, b, new, sems):
        each(ins, new, sems, lambda cp: cp.start())

    def finish(ins, b, new, sems):
        each(ins, new, sems, lambda cp: cp.wait())

    return _Ride(tensors, [], [SDS((s.shape[0],) + s.shape[2:], s.dtype) for s in tensors], [(n,), (n,)], start, finish)


def _ride_scatter(tensors, landing):
    n = len(tensors)

    def each(ins, b, sems, act):
        send, recv = sems
        x, y, c, p, others = _position()
        for t in range(n):
            for j, (qx, qy) in enumerate(others):
                q = 2 * qx + qy
                act(ins[t].at[q], b[t].at[p], b[t].at[q], send.at[t, j], recv.at[t, j], (qx, qy, c))

    def start(ins, b, new, sems):
        each(ins, b, sems, lambda src, dst, landed, s, r, dev: _remote(src, dst, s, r, dev).start())

    def finish(ins, b, new, sems):
        def act(src, dst, landed, s, r, dev):
            _remote(src, dst, s, r, dev).wait_send()
            _remote(landed, landed, s, r, dev).wait_recv()
        each(ins, b, sems, act)

    return _Ride(tensors, landing, [], [(n, 3), (n, 3)], start, finish)


def _ride_join(bufs):
    n = len(bufs)

    def each(b, sems, act):
        send, recv = sems
        x, y, c, _, _ = _position()
        for t in range(n):
            act(b[t].at[c], b[t].at[1 - c], send.at[t], recv.at[t], (x, y, 1 - c))

    def start(ins, b, new, sems):
        each(b, sems, lambda mine, theirs, s, r, dev: _remote(mine, mine, s, r, dev).start())

    def finish(ins, b, new, sems):
        def act(mine, theirs, s, r, dev):
            _remote(mine, mine, s, r, dev).wait_send()
            _remote(theirs, theirs, s, r, dev).wait_recv()
        each(b, sems, act)

    return _Ride([], bufs, [], [(n,), (n,)], start, finish)


def _all_reduce_small(pack):
    rows = pack.shape[0]
    n_dev = 2 * N_CHIPS

    def body(x_ref, o_ref, land, send, recv):
        x, y, c, p, _ = _position()
        me = 2 * p + c
        land[me] = x_ref[...]
        peers = [(dx, dy, dc) for dx in range(2) for dy in range(2) for dc in range(2) if (dx, dy, dc) != (0, 0, 0)]
        for j, (dx, dy, dc) in enumerate(peers):
            _remote(land.at[me], land.at[me], send.at[j], recv.at[j], (x ^ dx, y ^ dy, c ^ dc)).start()
        for j, (dx, dy, dc) in enumerate(peers):
            src = 4 * (x ^ dx) + 2 * (y ^ dy) + (c ^ dc)
            _remote(land.at[me], land.at[me], send.at[j], recv.at[j], (x ^ dx, y ^ dy, c ^ dc)).wait_send()
            _remote(land.at[src], land.at[src], send.at[j], recv.at[j], (x ^ dx, y ^ dy, c ^ dc)).wait_recv()
        acc = land[0]
        for dev in range(1, n_dev):
            acc = acc + land[dev]
        o_ref[...] = acc

    return pl.pallas_call(
        body, out_shape=SDS((rows, LANES), F32),
        scratch_shapes=[pltpu.VMEM((n_dev, rows, LANES), F32), pltpu.SemaphoreType.DMA((n_dev - 1,)),
                        pltpu.SemaphoreType.DMA((n_dev - 1,))],
        name="all_reduce_small", compiler_params=_params())(pack)


def _add_own_half(name, full, recv, out_dtype):
    n4, _, h, cols = full.shape

    def body(s_ref, a_ref, b_ref, o_ref, own_ref):
        v = (a_ref[...] + b_ref[...]).astype(out_dtype)
        o_ref[...] = v

        @pl.when(pl.program_id(0) == s_ref[1])
        def _():
            own_ref[...] = v

    return pl.pallas_call(
        body,
        grid_spec=pltpu.PrefetchScalarGridSpec(
            num_scalar_prefetch=1, grid=(n4,),
            in_specs=[pl.BlockSpec((None, None, h, cols), lambda q, s: (q, s[0], 0, 0)),
                      pl.BlockSpec((None, h, cols), lambda q, s: (q, 0, 0))],
            out_specs=[pl.BlockSpec((None, h, cols), lambda q, s: (q, 0, 0)),
                       pl.BlockSpec((None, h, cols), lambda q, s: (s[1], 0, 0))]),
        out_shape=[SDS((n4, h, cols), out_dtype)] * 2, name=name, compiler_params=_params())(_mesh_scalars(), full, recv)


def _sum_chips(name, parts):
    n4, h, cols = parts.shape
    th = h // 4 if h % 64 == 0 else h

    def body(s_ref, a_ref, o_ref):
        acc = a_ref[0].astype(F32)
        for q in range(1, n4):
            acc = acc + a_ref[q].astype(F32)
        o_ref[...] = acc

    return pl.pallas_call(
        body,
        grid_spec=pltpu.PrefetchScalarGridSpec(
            num_scalar_prefetch=1, grid=(h // th,),
            in_specs=[pl.BlockSpec((n4, th, cols), lambda i, s: (0, i, 0))],
            out_specs=pl.BlockSpec((None, th, cols), lambda i, s: (s[0], i, 0))),
        out_shape=SDS((2, h, cols), F32), name=name, compiler_params=_params())(_mesh_scalars(), parts)


def _adamw_math(w, g, m, v):
    m2 = ADAM_B1 * m + (1.0 - ADAM_B1) * g
    v2 = ADAM_B2 * v + (1.0 - ADAM_B2) * (g * g)
    m_hat = m2 / (1.0 - ADAM_B1 ** ADAM_STEP)
    v_hat = v2 / (1.0 - ADAM_B2 ** ADAM_STEP)
    delta = -ADAM_LR * (m_hat / (jnp.sqrt(v_hat) + ADAM_EPS) + ADAM_WD * w)
    return delta, m2, v2


def _row_tile(rows, cols):
    cap = max(8, (1 << 18) // cols)
    best = 8
    for cand in range(8, min(rows, cap) + 1, 8):
        if rows % cand == 0:
            best = cand
    return best


def _adamw_big(name, w, g_layers, m, v):
    layers, rows, cols = w.shape
    tr = _row_tile(rows, cols)
    outs = None
    for layer, g in enumerate(g_layers):
        def body(w_ref, g_ref, m_ref, v_ref, *rest):
            g_o, d_o, m_o, v_o = rest[-4:]
            gv = g_ref[...]
            d, mm, vv = _adamw_math(w_ref[...], gv, m_ref[...], v_ref[...])
            g_o[...] = gv
            d_o[...] = d
            m_o[...] = mm
            v_o[...] = vv

        blk = pl.BlockSpec((None, tr, cols), functools.partial(lambda l, i: (l, i, 0), layer))
        prev = [] if outs is None else list(outs)
        outs = pl.pallas_call(
            body, grid=(rows // tr,), in_specs=[blk, _row_spec(tr, cols), blk, blk] + [ANY] * len(prev),
            out_specs=[blk] * 4, out_shape=[SDS((layers, rows, cols), F32)] * 4,
            input_output_aliases={4 + j: j for j in range(len(prev))}, name=f"{name}_{layer}",
            compiler_params=_params())(w, g.reshape(rows, cols), m, v, *prev)
    return tuple(outs)


def _adamw_small(ws, gs, ms, vs):
    n = len(ws)
    flat = []
    for group in (ws, gs, ms, vs):
        flat += [a.reshape(-1, a.shape[-1]) for a in group]

    def body(*refs):
        w_r, g_r, m_r, v_r = refs[:n], refs[n:2 * n], refs[2 * n:3 * n], refs[3 * n:4 * n]
        d_o, m_o, v_o = refs[4 * n:5 * n], refs[5 * n:6 * n], refs[6 * n:7 * n]
        for j in range(n):
            d, mm, vv = _adamw_math(w_r[j][...], g_r[j][...], m_r[j][...], v_r[j][...])
            d_o[j][...] = d
            m_o[j][...] = mm
            v_o[j][...] = vv

    shapes = [SDS(a.shape, F32) for a in flat[:n]]
    outs = pl.pallas_call(body, out_shape=shapes * 3, name="adamw_small", compiler_params=_params())(*flat)
    res = []
    for k in range(3):
        res.append([outs[k * n + j].reshape(ws[j].shape) for j in range(n)])
    return res


BIG = ("ab_w_in", "ab_w_out", "cd_w_in", "cd_w_out", "ffn_w_gate", "ffn_w_up", "ffn_w_down")
V_BLOCK = (2 * A_WIDTH + QK_COLS) // B_WIDTH


def _pad_rows(a, rows):
    return jnp.pad(a, ((0, rows - a.shape[0]), (0, 0)))


A_IN, A_OUT, C_IN, C_OUT = ("ab_w_in", 0), ("ab_w_out", 0), ("cd_w_in", 0), ("cd_w_out", 0)
G0, U0, D0 = ("ffn_w_gate", 0), ("ffn_w_up", 0), ("ffn_w_down", 0)
G1, U1, D1 = ("ffn_w_gate", 1), ("ffn_w_up", 1), ("ffn_w_down", 1)
UNITS = (A_IN, A_OUT, G0, U0, D0, C_IN, C_OUT, G1, U1, D1)
SMALL_SHARDED = ("small", 0)


class _Exchange:
    def __init__(self, enabled):
        self.enabled = enabled
        self.w, self.grad, self.recv, self.half, self.land, self.done = {}, {}, {}, {}, {}, {}

    def full(self, unit):
        b = self.w[unit]
        return b.reshape(N_CHIPS, 1, 2 * b.shape[2], b.shape[3])

    def _ride(self, phases):
        rides, sinks = [], []
        for kind, units in phases:
            if kind == "send":
                rides.append(_ride_gather_send([self.w[u] for u in units]))
                sinks.append(self.w)
            elif kind == "pass":
                rides.append(_ride_gather_pass([self.w[u] for u in units]))
                sinks.append(self.w)
            elif kind == "swap":
                rides.append(_ride_swap([self.grad[u] for u in units]))
                sinks.append(self.recv)
            elif kind == "scatter":
                rides.append(_ride_scatter([self.half[u] for u in units], [self.land[u] for u in units]))
                sinks.append(self.land)
            else:
                rides.append(_ride_join([self.done[u] for u in units]))
                sinks.append(self.done)
        ride = functools.reduce(_ride_both, rides)

        def settle(res):
            n_bufs = sum(len(r.bufs) for r in rides)
            bufs, new = list(res[:n_bufs]), list(res[n_bufs:])
            for r, sink, (_, units) in zip(rides, sinks, phases):
                vals = [bufs.pop(0) for _ in r.bufs] + [new.pop(0) for _ in r.new_outs]
                for u, v in zip(units, vals):
                    sink[u] = v

        return ride, settle

    def run(self, fn, *args, phases=(), **kw):
        if not self.enabled or not phases:
            return fn(*args, **kw)
        ride, settle = self._ride(phases)
        out, res = fn(*args, ride=ride, **kw)
        settle(res)
        return out

    def alone(self, name, phases):
        if self.enabled:
            ride, settle = self._ride(phases)
            settle(_run_ride(name, ride))

    def pair_sum(self, units):
        if self.enabled:
            for u in units:
                dtype = F32 if u == SMALL_SHARDED else BF16
                self.half[u], self.land[u] = _add_own_half(f"pair_sum_{u[0]}_{u[1]}", self.grad[u], self.recv[u], dtype)

    def chip_sum(self, units):
        if self.enabled:
            for u in units:
                self.done[u] = _sum_chips(f"chip_sum_{u[0]}_{u[1]}", self.land[u])


def _local_step(x, target, ex, sp):
    t, d = x.shape
    tabs = _rope_tables(t)
    gains = jnp.concatenate([jnp.tile(sp["q_norm_g"][g], HEAD_DIM // 8) for g in range(N_DIL)]
                            + [jnp.tile(sp["k_norm_g"][g], HEAD_DIM // 8) for g in range(N_DIL)]).reshape(1, QK_COLS)
    bias_t = sp["sgu_bias"].T
    cw = _pad_rows(sp["conv_c_w"], 32)
    dw = _pad_rows(sp["conv_d_w"], 8)
    cb, clg, clb = (sp[k].reshape(1, C_WIDTH) for k in ("conv_c_b", "c_ln_g", "c_ln_b"))
    slg, slb = sp["sgu_norm_g"].reshape(1, A_WIDTH), sp["sgu_norm_b"].reshape(1, A_WIDTH)
    g_ab, g_cd = sp["ab_norm_g"].reshape(1, d), sp["cd_norm_g"].reshape(1, d)
    g_f0, g_f1 = sp["ffn_norm_g"][0:1], sp["ffn_norm_g"][1:2]
    run = ex.run

    def w2d(unit):
        return ex.full(unit).reshape(-1, d)

    h0 = _rms_fwd("rms_ab", x, g_ab)
    proj = run(_proj_in, "proj_ab", h0, ex.full(A_IN), 0, phases=[("send", [A_OUT, G0])])
    a_out = _mixer_a_fwd(proj, slg, slb, sp["sgu_w"], bias_t)
    qk = run(_qk_fwd, proj, gains, tabs, phases=[("pass", [A_OUT, G0]), ("send", [U0])])
    fwd_phases = ([("pass", [U0]), ("send", [D0])], [("pass", [D0]), ("send", [C_IN])],
                  [("pass", [C_IN]), ("send", [C_OUT, G1])])
    qkv, o_list, l_list = [], [], []
    for g, rate in enumerate(DIL_RATES):
        if rate == 1:
            qk3, proj3 = qk.reshape(1, t, QK_COLS), proj.reshape(1, t, AB_IN)
            q, k, v = (qk3, g), (qk3, N_DIL + g), (proj3, V_BLOCK + g)
        else:
            qp, kp, vp = _permute(f"perm_fwd_{g}", [(qk, g), (qk, N_DIL + g), (proj, V_BLOCK + g)], rate)
            q, k, v = (qp, 0), (kp, 0), (vp, 0)
        qkv.append((q, k, v))
        o, l = run(_attn_fwd, f"attn_fwd_{g}", q, k, v, phases=fwd_phases[g])
        if rate == 1:
            o, l = o.reshape(t, B_WIDTH), l.reshape(t, B_WIDTH)
        else:
            o, l = _unpermute(f"unperm_fwd_{g}", [o, l], rate)
        o_list.append(o)
        l_list.append(l)
    cat, lse_tot = _attn_merge(a_out, o_list, l_list)
    x1, hf0 = _proj_out("out_ab", cat, w2d(A_OUT), x, g_next=g_f0)
    gate0, up0, act0 = run(_ffn_in, "ffn_in_0", hf0, ex.full(G0), ex.full(U0), 0,
                           phases=[("pass", [C_OUT, G1]), ("send", [U1])])
    x2, h1 = run(_ffn_out, "ffn_out_0", act0, ex.full(D0), 0, x1, g_next=g_cd, phases=[("pass", [U1]), ("send", [D1])])
    projcd = run(_proj_in, "proj_cd", h1, ex.full(C_IN), 0, phases=[("pass", [D1])])
    cat2 = _mixer_cd_fwd(projcd, cw, cb, clg, clb, dw)
    x3, hf1 = _proj_out("out_cd", cat2, w2d(C_OUT), x2, g_next=g_f1)
    gate1, up1, act1 = _ffn_in("ffn_in_1", hf1, ex.full(G1), ex.full(U1), 0)
    dy, loss_acc, dy_b = _ffn_out("ffn_out_1", act1, ex.full(D1), 0, x3, target=target)
    loss = 0.5 * loss_acc[0, 0] / d

    late = [D1, G1, U1]
    dgate, dup = _ffn_dact("ffn_dact_1", dy_b, ex.full(D1), 0, gate1, up1)
    ex.grad[D1] = _wgrad_row_sharded("wgrad_down_1", act1, dy_b, True)
    ex.grad[G1], ex.grad[U1] = _wgrad_col_sharded("wgrad_gate_up_1", hf1, [dgate, dup], True)
    g3, d_f1, g3_b = run(_dgrad_cols, "dgrad_ffn_1", [dgate, dup], [ex.full(G1), ex.full(U1)], 0, True, x3, g_f1, dy,
                         phases=[("swap", late)])
    ex.pair_sum(late)

    dcat2 = _dgrad_rows("dgrad_out_cd", g3_b, w2d(C_OUT))
    ex.grad[C_OUT] = _wgrad_row_sharded("wgrad_out_cd", cat2, g3_b, False)
    dprojcd, d_cw, d_cb, d_clg, d_clb, d_dw = run(_mixer_cd_bwd, projcd, dcat2, cw, cb, clg, clb, dw, phases=[("scatter", late)])
    ex.chip_sum(late)
    ex.grad[C_IN] = run(_wgrad_col_sharded, "wgrad_in_cd", h1, [dprojcd], False, phases=[("join", late)])[0]
    g2, d_cdn, g2_b = run(_dgrad_cols, "dgrad_in_cd", [dprojcd], [ex.full(C_IN)], 0, False, x2, g_cd, g3,
                          phases=[("swap", [C_OUT, C_IN])])
    ex.pair_sum([C_OUT, C_IN])

    dgate, dup = run(_ffn_dact, "ffn_dact_0", g2_b, ex.full(D0), 0, gate0, up0, phases=[("scatter", [C_OUT, C_IN])])
    ex.chip_sum([C_OUT, C_IN])
    ex.grad[D0] = run(_wgrad_row_sharded, "wgrad_down_0", act0, g2_b, True, phases=[("join", [C_OUT, C_IN])])
    ex.grad[G0], ex.grad[U0] = _wgrad_col_sharded("wgrad_gate_up_0", hf0, [dgate, dup], True)
    small = {"cd_norm_g": d_cdn, "conv_c_w": d_cw[:C_KERNEL], "conv_c_b": d_cb, "c_ln_g": d_clg, "c_ln_b": d_clb,
             "conv_d_w": d_dw[:D_KERNEL]}
    ex.grad[SMALL_SHARDED] = _split_full_small(small).reshape(N_CHIPS, 2, SHARDED_ROWS // 2, LANES)
    mid = [D0, G0, U0, SMALL_SHARDED]
    g1, d_f0, g1_b = run(_dgrad_cols, "dgrad_ffn_0", [dgate, dup], [ex.full(G0), ex.full(U0)], 0, True, x1, g_f0, g2,
                         phases=[("swap", mid)])
    ex.pair_sum(mid)

    dcat = _dgrad_rows("dgrad_out_ab", g1_b, w2d(A_OUT))
    ex.grad[A_OUT] = _wgrad_row_sharded("wgrad_out_ab", cat, g1_b, False)
    d_a, d_sw, d_sbt, d_slg, d_slb = _mixer_a_bwd(proj, dcat, slg, slb, sp["sgu_w"], bias_t)
    dbb, dd = _attn_bwd_prep(dcat, cat)
    bwd_phases = ([("scatter", mid)], [("join", mid), ("swap", [A_OUT])], [("scatter", [A_OUT])])
    dqs, dks, dvs = [], [], []
    for g, rate in enumerate(DIL_RATES):
        q, k, v = qkv[g]
        if rate == 1:
            db3, l3, dd3 = (a.reshape(1, t, B_WIDTH) for a in (dbb, lse_tot, dd))
        else:
            db3, l3, dd3 = _permute(f"perm_bwd_{g}", [(dbb, 0), (lse_tot, 0), (dd, 0)], rate)
        dq, dk, dv = run(_attn_bwd, f"attn_bwd_{g}", q, k, v, db3, l3, dd3, phases=bwd_phases[g])
        if g == 0:
            ex.chip_sum(mid)
        elif g == 1:
            ex.pair_sum([A_OUT])
        else:
            ex.chip_sum([A_OUT])
        if rate == 1:
            dq, dk, dv = (a.reshape(t, B_WIDTH) for a in (dq, dk, dv))
        else:
            dq, dk, dv = _unpermute(f"unperm_bwd_{g}", [dq, dk, dv], rate)
        dqs.append(dq)
        dks.append(dk)
        dvs.append(dv)
    dproj, d_gains = _dproj_assemble(proj, d_a, dqs + dks, dvs, gains, tabs)
    d_gains = _fold_heads(d_gains)[0].reshape(2, N_DIL, B_WIDTH)[:, :, :HEAD_DIM]
    ex.grad[A_IN] = run(_wgrad_col_sharded, "wgrad_in_ab", h0, [dproj], False, phases=[("join", [A_OUT])])[0]
    ex.alone("swap_last", [("swap", [A_IN])])
    ex.pair_sum([A_IN])
    gx, d_abn = run(_dgrad_cols, "dgrad_in_ab", [dproj], [ex.full(A_IN)], 0, False, x, g_ab, g1, bf16_copy=False,
                    phases=[("scatter", [A_IN])])
    ex.chip_sum([A_IN])
    ex.alone("join_last", [("join", [A_IN])])

    small.update({
        "ab_norm_g": d_abn, "sgu_norm_g": d_slg, "sgu_norm_b": d_slb, "sgu_w": d_sw, "sgu_bias": d_sbt.T,
        "q_norm_g": d_gains[0], "k_norm_g": d_gains[1], "ffn_norm_g": jnp.concatenate([d_f0, d_f1], axis=0),
    })
    return loss, gx, small


SHARDED_SMALL = ("cd_norm_g", "conv_c_w", "conv_c_b", "c_ln_g", "c_ln_b", "conv_d_w")
SHARDED_ROWS = 48
REPLICATED_SMALL = ("ab_norm_g", "sgu_norm_g", "sgu_norm_b", "sgu_w", "sgu_bias", "q_norm_g", "k_norm_g", "ffn_norm_g")
REPLICATED_ROWS = 560


def _pack_sharded(parts):
    rows = [parts[k].reshape(-1, LANES) for k in SHARDED_SMALL]
    return _pad_rows(jnp.concatenate(rows, axis=0), SHARDED_ROWS)


def _split_full_small(small):
    per_chip = []
    for q in range(N_CHIPS):
        parts = {}
        for k in SHARDED_SMALL:
            a = small[k]
            a = a.reshape(-1, a.shape[-1])
            n = a.shape[-1] // N_CHIPS
            parts[k] = a[:, q * n:(q + 1) * n]
        per_chip.append(_pack_sharded(parts))
    return jnp.stack(per_chip)


def _unpack_sharded(pack, shapes):
    out, r = {}, 0
    for k in SHARDED_SMALL:
        n = math.prod(shapes[k]) // LANES
        out[k] = pack[r:r + n].reshape(shapes[k])
        r += n
    return out


def _gathered_small(packs, shapes):
    per_chip = [_unpack_sharded(packs[q], shapes) for q in range(N_CHIPS)]
    return {k: jnp.concatenate([pc[k] for pc in per_chip], axis=-1) for k in SHARDED_SMALL}


def _pack_replicated(small):
    rows = []
    for k in REPLICATED_SMALL:
        a = small[k].reshape(-1)
        a = jnp.pad(a, (0, (-a.shape[0]) % LANES))
        rows.append(a.reshape(-1, LANES))
    return _pad_rows(jnp.concatenate(rows, axis=0), REPLICATED_ROWS)


def _unpack_replicated(pack, shapes):
    out, r = {}, 0
    for k in REPLICATED_SMALL:
        size = math.prod(shapes[k])
        n = -(-size // LANES)
        out[k] = pack[r:r + n].reshape(-1)[:size].reshape(shapes[k])
        r += n
    return out


WEIGHT_ORDER = ("ab_norm_g", "ab_w_in", "sgu_norm_g", "sgu_norm_b", "sgu_w", "sgu_bias", "q_norm_g", "k_norm_g", "ab_w_out",
                "cd_norm_g", "cd_w_in", "conv_c_w", "conv_c_b", "c_ln_g", "c_ln_b", "conv_d_w", "cd_w_out", "ffn_norm_g",
                "ffn_w_gate", "ffn_w_up", "ffn_w_down")


def kernel(x, ab_norm_g, ab_w_in, sgu_norm_g, sgu_norm_b, sgu_w, sgu_bias, q_norm_g, k_norm_g, ab_w_out, cd_norm_g, cd_w_in, conv_c_w, conv_c_b, c_ln_g, c_ln_b, conv_d_w, cd_w_out, ffn_norm_g, ffn_w_gate, ffn_w_up, ffn_w_down, loss_target, m_ab_norm_g, m_ab_w_in, m_sgu_norm_g, m_sgu_norm_b, m_sgu_w, m_sgu_bias, m_q_norm_g, m_k_norm_g, m_ab_w_out, m_cd_norm_g, m_cd_w_in, m_conv_c_w, m_conv_c_b, m_c_ln_g, m_c_ln_b, m_conv_d_w, m_cd_w_out, m_ffn_norm_g, m_ffn_w_gate, m_ffn_w_up, m_ffn_w_down, v_ab_norm_g, v_ab_w_in, v_sgu_norm_g, v_sgu_norm_b, v_sgu_w, v_sgu_bias, v_q_norm_g, v_k_norm_g, v_ab_w_out, v_cd_norm_g, v_cd_w_in, v_conv_c_w, v_conv_c_b, v_c_ln_g, v_c_ln_b, v_conv_d_w, v_cd_w_out, v_ffn_norm_g, v_ffn_w_gate, v_ffn_w_up, v_ffn_w_down):
    args = dict(locals())
    ws = {k: args[k] for k in WEIGHT_ORDER}
    ms = {k: args["m_" + k] for k in WEIGHT_ORDER}
    vs = {k: args["v_" + k] for k in WEIGHT_ORDER}
    small_names = [k for k in WEIGHT_ORDER if k not in BIG]
    t, d = x.shape[1:]

    ex = _Exchange(enabled=True)
    for name, layer in UNITS:
        ex.w[(name, layer)] = _stage_own(f"stage_{name}_{layer}", ws[name], layer, BF16)
    own_small = _pack_sharded({k: ws[k][0] for k in SHARDED_SMALL})
    ex.w[SMALL_SHARDED] = _stage_own("stage_small", own_small[None], 0, F32)
    ex.alone("gather_first", [("send", [A_IN, SMALL_SHARDED])])
    ex.alone("gather_first_pass", [("pass", [A_IN, SMALL_SHARDED])])
    sp = _gathered_small(ex.w[SMALL_SHARDED].reshape(N_CHIPS, SHARDED_ROWS, LANES), {k: ws[k].shape[1:] for k in SHARDED_SMALL})
    for k in REPLICATED_SMALL:
        sp[k] = ws[k] if k == "ffn_norm_g" else ws[k][0]

    loss, grad_x, g_small = _local_step(x.reshape(t, d), loss_target.reshape(t, d), ex, sp)

    grad = _unpack_sharded(ex.done[SMALL_SHARDED].reshape(SHARDED_ROWS, LANES), {k: ws[k].shape for k in SHARDED_SMALL})
    grad.update(_unpack_replicated(_all_reduce_small(_pack_replicated(g_small)), {k: ws[k].shape for k in REPLICATED_SMALL}))

    delta, new_m, new_v = {}, {}, {}
    for k in BIG:
        g_layers = [ex.done[(k, layer)] for layer in range(ws[k].shape[0])]
        grad[k], delta[k], new_m[k], new_v[k] = _adamw_big("adamw_" + k, ws[k], g_layers, ms[k], vs[k])
    d_s, m_s, v_s = _adamw_small([ws[k] for k in small_names], [grad[k] for k in small_names],
                                 [ms[k] for k in small_names], [vs[k] for k in small_names])
    for j, k in enumerate(small_names):
        delta[k], new_m[k], new_v[k] = d_s[j], m_s[j], v_s[j]

    loss = lax.psum(loss, ("x", "y", "c"))
    return (loss, grad_x[None], *[grad[k] for k in WEIGHT_ORDER], *[delta[k] for k in WEIGHT_ORDER],
            *[new_m[k] for k in WEIGHT_ORDER], *[new_v[k] for k in WEIGHT_ORDER])
```

```python
import functools
import math

import jax
import jax.numpy as jnp
from jax import lax
from jax.experimental import pallas as pl
from jax.experimental.pallas import tpu as pltpu

F32 = jnp.float32
BF16 = jnp.bfloat16
SDS = jax.ShapeDtypeStruct

N_CHIPS = 4
EPS = 1e-6
NEG_INF = -1e30
CHUNK = 128
A_GROUPS = 4
A_WIDTH = 512
N_DIL = 3
DIL_RATES = (1, 4, 16)
HEAD_DIM = 64
B_WIDTH = 512
ROPE_DIM = 16
ROPE_THETA = 500000.0
C_WIDTH = 512
C_KERNEL = 31
D_KERNEL = 3
HALO = 32
ATT_BLOCK = 128
LANES = 128

ADAM_LR = 0.001
ADAM_B1 = 0.9
ADAM_B2 = 0.999
ADAM_EPS = 1e-08
ADAM_WD = 0.01
ADAM_STEP = 10

VMEM_LIMIT = 56 * 1024 * 1024

NN = (((1,), (0,)), ((), ()))
NT = (((1,), (1,)), ((), ()))
TN = (((0,), (0,)), ((), ()))

TILES = {"proj_in": 1024, "proj_out": 1024, "ffn_in": 1024, "ffn_out": 512, "ffn_dact": 512, "dgrad_cols": 512,
         "dgrad_rows": 1024, "wgrad": 2048}


def _params(sem=None):
    return pltpu.CompilerParams(dimension_semantics=sem, vmem_limit_bytes=VMEM_LIMIT)


def _bf(v):
    return v if v.dtype == BF16 else v.astype(BF16)


def _dot(a, b, dims):
    return lax.dot_general(_bf(a), _bf(b), dims, preferred_element_type=F32)


def _dot_hi(a, b):
    return jnp.dot(a, b, precision=lax.Precision.HIGHEST, preferred_element_type=F32)


def _sigmoid(v):
    return 0.5 * jnp.tanh(0.5 * v) + 0.5


def _gelu(v):
    return 0.5 * v * (1.0 + lax.erf(v * (1.0 / math.sqrt(2.0))))


def _gelu_grad(v):
    cdf = 0.5 * (1.0 + lax.erf(v * (1.0 / math.sqrt(2.0))))
    return cdf + v * jnp.exp(-0.5 * v * v) * (1.0 / math.sqrt(2.0 * math.pi))


def _segment_mean_matrix(seg, scale=None):
    r = lax.broadcasted_iota(jnp.int32, (LANES, LANES), 0) // seg
    c = lax.broadcasted_iota(jnp.int32, (LANES, LANES), 1) // seg
    return jnp.where(r == c, (1.0 / seg) if scale is None else scale, 0.0).astype(BF16)


def _segment_dot(v, seg):
    hi = v.astype(BF16)
    lo = (v - hi.astype(F32)).astype(BF16)
    return jnp.dot(hi, seg, preferred_element_type=F32) + jnp.dot(lo, seg, preferred_element_type=F32)


MESH = pl.DeviceIdType.MESH
ANY = pl.BlockSpec(memory_space=pl.ANY)


def _position():
    x, y, c = lax.axis_index("x"), lax.axis_index("y"), lax.axis_index("c")
    others = [(1 - x, y), (x, 1 - y), (1 - x, 1 - y)]
    return x, y, c, 2 * x + y, others


class _Ride:
    def __init__(self, ins, bufs, new_outs, sem_shapes, start, finish):
        self.ins, self.bufs, self.new_outs, self.sem_shapes = list(ins), list(bufs), list(new_outs), list(sem_shapes)
        self.start, self.finish = start, finish


def _ride_both(a, b):
    na = (len(a.ins), len(a.bufs), len(a.new_outs), len(a.sem_shapes))

    def split(ins, bufs, new, sems):
        return ((ins[:na[0]], bufs[:na[1]], new[:na[2]], sems[:na[3]]), (ins[na[0]:], bufs[na[1]:], new[na[2]:], sems[na[3]:]))

    def start(*refs):
        ra, rb = split(*refs)
        a.start(*ra)
        b.start(*rb)

    def finish(*refs):
        ra, rb = split(*refs)
        a.finish(*ra)
        b.finish(*rb)

    return _Ride(a.ins + b.ins, a.bufs + b.bufs, a.new_outs + b.new_outs, a.sem_shapes + b.sem_shapes, start, finish)


def _call(body, *, grid, in_specs, out_specs, out_shape, operands, name, scratch_shapes=(), aliases=None, ride=None):
    if ride is None:
        return pl.pallas_call(body, grid=grid, in_specs=in_specs, out_specs=out_specs, out_shape=out_shape,
                              scratch_shapes=list(scratch_shapes), input_output_aliases=aliases or {}, name=name,
                              compiler_params=_params())(*operands)
    multi = isinstance(out_shape, (list, tuple))
    out_shapes = list(out_shape) if multi else [out_shape]
    o_specs = list(out_specs) if multi else [out_specs]
    n_in, n_out, n_scr = len(operands), len(out_shapes), len(scratch_shapes)
    n_ri, n_rb, n_rn = len(ride.ins), len(ride.bufs), len(ride.new_outs)

    def carrying(*refs):
        k = n_in
        r_ins = refs[k:k + n_ri]
        k += n_ri + n_rb
        outs = refs[k:k + n_out]
        k += n_out
        r_bufs = refs[k:k + n_rb]
        k += n_rb
        r_new = refs[k:k + n_rn]
        k += n_rn
        scratch = refs[k:k + n_scr]
        sems = refs[k + n_scr:]
        first, last = None, None
        for axis, size in enumerate(grid):
            pid = pl.program_id(axis)
            first = (pid == 0) if first is None else first & (pid == 0)
            last = (pid == size - 1) if last is None else last & (pid == size - 1)

        @pl.when(first)
        def _():
            ride.start(r_ins, r_bufs, r_new, sems)

        body(*refs[:n_in], *outs, *scratch)

        @pl.when(last)
        def _():
            ride.finish(r_ins, r_bufs, r_new, sems)

    all_aliases = dict(aliases or {})
    for j in range(n_rb):
        all_aliases[n_in + n_ri + j] = n_out + j
    res = pl.pallas_call(
        carrying, grid=grid, in_specs=list(in_specs) + [ANY] * (n_ri + n_rb), out_specs=o_specs + [ANY] * (n_rb + n_rn),
        out_shape=out_shapes + [SDS(b.shape, b.dtype) for b in ride.bufs] + ride.new_outs,
        scratch_shapes=list(scratch_shapes) + [pltpu.SemaphoreType.DMA(s) for s in ride.sem_shapes],
        input_output_aliases=all_aliases, name=name, compiler_params=_params())(*operands, *ride.ins, *ride.bufs)
    outs = res[:n_out]
    return (list(outs) if multi else outs[0]), list(res[n_out:])


def _run_ride(name, ride):
    n_ri, n_rb, n_rn = len(ride.ins), len(ride.bufs), len(ride.new_outs)

    def body(*refs):
        r_ins = refs[:n_ri]
        r_bufs = refs[n_ri + n_rb:n_ri + 2 * n_rb]
        r_new = refs[n_ri + 2 * n_rb:n_ri + 2 * n_rb + n_rn]
        sems = refs[n_ri + 2 * n_rb + n_rn:]
        ride.start(r_ins, r_bufs, r_new, sems)
        ride.finish(r_ins, r_bufs, r_new, sems)

    return list(pl.pallas_call(
        body, in_specs=[ANY] * (n_ri + n_rb), out_specs=[ANY] * (n_rb + n_rn),
        out_shape=[SDS(b.shape, b.dtype) for b in ride.bufs] + ride.new_outs,
        scratch_shapes=[pltpu.SemaphoreType.DMA(s) for s in ride.sem_shapes],
        input_output_aliases={n_ri + j: j for j in range(n_rb)}, name=name)(*ride.ins, *ride.bufs))


def _whole(ref, p):
    return ref[...]


def _slab(ref, p):
    return ref[p]


def _matmul(name, grid, pairs, extras, outs, dims, epi, *, slabs=1, n_acc=1, ride=None):
    n_pairs, n_ex, n_out = len(pairs), len(extras), len(outs)

    def body(*refs):
        ab = refs[:2 * n_pairs]
        ex = refs[2 * n_pairs:2 * n_pairs + n_ex]
        out_refs = refs[2 * n_pairs + n_ex:2 * n_pairs + n_ex + n_out]
        pids = tuple(pl.program_id(a) for a in range(len(grid)))
        parts = [None] * n_acc
        for p in range(slabs):
            for j, (_, _, a_pick, _, _, b_pick, acc) in enumerate(pairs):
                d = _dot(a_pick(ab[2 * j], p), b_pick(ab[2 * j + 1], p), dims)
                parts[acc] = d if parts[acc] is None else parts[acc] + d
        epi(parts, ex, out_refs, pids)

    operands, in_specs = [], []
    for a, a_spec, _, b, b_spec, _, _ in pairs:
        operands += [a, b]
        in_specs += [a_spec, b_spec]
    for e, e_spec in extras:
        operands.append(e)
        in_specs.append(e_spec)
    return _call(body, grid=grid, in_specs=in_specs, out_specs=[o[1] for o in outs], out_shape=[o[0] for o in outs],
                 operands=operands, name=name, ride=ride)


def _rms_rows(v, g):
    r = lax.rsqrt(jnp.mean(v * v, axis=-1, keepdims=True) + EPS)
    return v * r * g


def _rms_fwd(name, x, g):
    t, d = x.shape
    tm = 512

    def body(x_ref, g_ref, o_ref):
        o_ref[...] = _rms_rows(x_ref[...], g_ref[...]).astype(BF16)

    return pl.pallas_call(
        body, grid=(t // tm,),
        in_specs=[pl.BlockSpec((tm, d), lambda i: (i, 0)), pl.BlockSpec((1, d), lambda i: (0, 0))],
        out_specs=pl.BlockSpec((tm, d), lambda i: (i, 0)), out_shape=SDS((t, d), BF16), name=name,
        compiler_params=_params())(x, g)


def _epi_residual_norm(accs, ex, outs, pids):
    x_new = accs[0] + ex[0][...]
    outs[0][...] = x_new
    outs[1][...] = _rms_rows(x_new, ex[1][...]).astype(BF16)


def _epi_residual_loss(accs, ex, outs, pids):
    y = accs[0] + ex[0][...]
    err = y - ex[1][...]
    dy = err * (1.0 / err.shape[-1])
    outs[0][...] = dy
    outs[2][...] = dy.astype(BF16)

    @pl.when(pids[0] == 0)
    def _():
        outs[1][...] = jnp.zeros_like(outs[1])

    outs[1][...] += jnp.sum(err * err)


def _epi_rms_bwd(accs, ex, outs, pids):
    dh = accs[0]
    xv, g, res = ex[0][...], ex[1][...], ex[2][...]
    r = lax.rsqrt(jnp.mean(xv * xv, axis=-1, keepdims=True) + EPS)
    xh = xv * r
    dy = dh * g
    dx = res + r * (dy - xh * jnp.mean(dy * xh, axis=-1, keepdims=True))
    outs[0][...] = dx
    if len(outs) > 2:
        outs[2][...] = dx.astype(BF16)

    @pl.when(pids[0] == 0)
    def _():
        outs[1][...] = jnp.zeros_like(outs[1])

    outs[1][...] += jnp.sum(dh * xh, axis=0, keepdims=True)


def _row_spec(tm, d):
    return pl.BlockSpec((tm, d), lambda i, *_: (i, 0))


def _const_spec(shape):
    nd = len(shape)
    return pl.BlockSpec(shape, lambda *_: (0,) * nd)


def _proj_in(name, h, w, layer, ride=None):
    t, d = h.shape
    n4 = w.shape[-1]
    tm = TILES["proj_in"]

    def epi(accs, ex, outs, pids):
        outs[0][...] = accs[0].astype(BF16)

    res = _matmul(
        name, (N_CHIPS, t // tm),
        [(h, pl.BlockSpec((tm, d), lambda p, i: (i, 0)), _whole,
          w, pl.BlockSpec((None, None, d, n4), lambda p, i: (p, layer, 0, 0)), _whole, 0)],
        [], [(SDS((t, N_CHIPS * n4), BF16), pl.BlockSpec((tm, n4), lambda p, i: (i, p)))],
        NN, epi, ride=ride)
    return res[0] if ride is None else (res[0][0], res[1])


def _proj_out(name, a, w, x, g_next=None, target=None):
    t, k = a.shape
    d = w.shape[-1]
    tm = TILES["proj_out"]
    if target is None:
        extras = [(x, _row_spec(tm, d)), (g_next, _const_spec((1, d)))]
        outs = [(SDS((t, d), F32), _row_spec(tm, d)), (SDS((t, d), BF16), _row_spec(tm, d))]
        epi = _epi_residual_norm
    else:
        extras = [(x, _row_spec(tm, d)), (target, _row_spec(tm, d))]
        outs = [(SDS((t, d), F32), _row_spec(tm, d)), (SDS((8, LANES), F32), _const_spec((8, LANES))),
                (SDS((t, d), BF16), _row_spec(tm, d))]
        epi = _epi_residual_loss
    return _matmul(name, (t // tm,), [(a, _row_spec(tm, k), _whole, w, _const_spec((k, d)), _whole, 0)], extras, outs, NN, epi)


def _ffn_in(name, h, wg, wu, layer, ride=None):
    t, d = h.shape
    n4 = wg.shape[-1]
    tm = TILES["ffn_in"]

    def epi(accs, ex, outs, pids):
        gate, up = accs
        outs[0][...] = gate.astype(BF16)
        outs[1][...] = up.astype(BF16)
        outs[2][...] = (gate * _sigmoid(gate) * up).astype(BF16)

    w_spec = pl.BlockSpec((None, None, d, n4), lambda p, i: (p, layer, 0, 0))
    h_spec = pl.BlockSpec((tm, d), lambda p, i: (i, 0))
    o = (SDS((N_CHIPS, t, n4), BF16), pl.BlockSpec((None, tm, n4), lambda p, i: (p, i, 0)))
    return _matmul(name, (N_CHIPS, t // tm),
                   [(h, h_spec, _whole, wg, w_spec, _whole, 0), (h, h_spec, _whole, wu, w_spec, _whole, 1)], [],
                   [o, o, o], NN, epi, n_acc=2, ride=ride)


def _ffn_out(name, act, wd, layer, x, g_next=None, target=None, ride=None):
    _, t, n4 = act.shape
    d = wd.shape[-1]
    tm = TILES["ffn_out"]
    xs = _row_spec(tm, d)
    if target is None:
        extras = [(x, xs), (g_next, _const_spec((1, d)))]
        outs = [(SDS((t, d), F32), xs), (SDS((t, d), BF16), xs)]
        epi = _epi_residual_norm
    else:
        extras = [(x, xs), (target, xs)]
        outs = [(SDS((t, d), F32), xs), (SDS((8, LANES), F32), _const_spec((8, LANES))), (SDS((t, d), BF16), xs)]
        epi = _epi_residual_loss
    return _matmul(
        name, (t // tm,),
        [(act, pl.BlockSpec((N_CHIPS, tm, n4), lambda i: (0, i, 0)), _slab,
          wd, pl.BlockSpec((N_CHIPS, None, n4, d), lambda i: (0, layer, 0, 0)), _slab, 0)],
        extras, outs, NN, epi, slabs=N_CHIPS, ride=ride)


def _ffn_dact(name, g, wd, layer, gate, up, ride=None):
    t, d = g.shape
    n4 = wd.shape[-2]
    tm = TILES["ffn_dact"]

    def epi(accs, ex, outs, pids):
        dact = accs[0]
        gt = ex[0][...].astype(F32)
        upv = ex[1][...].astype(F32)
        s = _sigmoid(gt)
        silu = gt * s
        outs[0][...] = (dact * upv * (s + silu - silu * s)).astype(BF16)
        outs[1][...] = (dact * silu).astype(BF16)

    blk = pl.BlockSpec((None, tm, n4), lambda p, i: (p, i, 0))
    o = (SDS((N_CHIPS, t, n4), BF16), blk)
    return _matmul(
        name, (N_CHIPS, t // tm),
        [(g, pl.BlockSpec((tm, d), lambda p, i: (i, 0)), _whole,
          wd, pl.BlockSpec((None, None, n4, d), lambda p, i: (p, layer, 0, 0)), _whole, 0)],
        [(gate, blk), (up, blk)], [o, o], NT, epi, ride=ride)


def _copy_epi(accs, ex, outs, pids):
    for a, o in zip(accs, outs):
        o[...] = a.astype(o.dtype)


def _dgrad_cols(name, dz_list, w_list, layer, three_d, x, g, res, bf16_copy=True, ride=None):
    t, d = x.shape
    n4 = w_list[0].shape[-1]
    tm = TILES["dgrad_cols"]
    if three_d:
        zs, z_pick = pl.BlockSpec((N_CHIPS, tm, n4), lambda i: (0, i, 0)), _slab
    else:
        zs, z_pick = _row_spec(tm, N_CHIPS * n4), (lambda ref, p: ref[:, p * n4:(p + 1) * n4])
    ws = pl.BlockSpec((N_CHIPS, None, d, n4), lambda i: (0, layer, 0, 0))
    xs = _row_spec(tm, d)
    return _matmul(
        name, (t // tm,), [(dz, zs, z_pick, w, ws, _slab, 0) for dz, w in zip(dz_list, w_list)],
        [(x, xs), (g, _const_spec((1, d))), (res, xs)],
        [(SDS((t, d), F32), xs), (SDS((1, d), F32), _const_spec((1, d)))] + ([(SDS((t, d), BF16), xs)] if bf16_copy else []),
        NT, _epi_rms_bwd, slabs=N_CHIPS, ride=ride)


def _dgrad_rows(name, g, w):
    t, d = g.shape
    k = w.shape[0]
    tm = TILES["dgrad_rows"]
    return _matmul(name, (t // tm,), [(g, _row_spec(tm, d), _whole, w, _const_spec((k, d)), _whole, 0)], [],
                   [(SDS((t, k), F32), _row_spec(tm, k))], NT, _copy_epi)[0]


A_TILE = 256


def _a_common(p_ref, lg_ref, lb_ref):
    pv = p_ref[...].astype(F32)
    a = _gelu(pv)
    u, v = a[:, :A_WIDTH], a[:, A_WIDTH:]
    vc = v - jnp.mean(v, axis=-1, keepdims=True)
    rs = lax.rsqrt(jnp.mean(vc * vc, axis=-1, keepdims=True) + EPS)
    vhat = vc * rs
    vn = vhat * lg_ref[...] + lb_ref[...]
    return pv, u, vhat, rs, vn.astype(BF16)


def _tril_weights(w_ref, g):
    r = lax.broadcasted_iota(jnp.int32, (CHUNK, CHUNK), 0)
    c = lax.broadcasted_iota(jnp.int32, (CHUNK, CHUNK), 1)
    return jnp.where(c <= r, w_ref[g], 0.0).astype(BF16), c <= r


def _mixer_a_fwd(proj, lg, lb, w, bias_t):
    t = proj.shape[0]

    def body(p_ref, lg_ref, lb_ref, w_ref, bt_ref, o_ref):
        _, u, _, _, vnb = _a_common(p_ref, lg_ref, lb_ref)
        for g in range(A_GROUPS):
            wt, _ = _tril_weights(w_ref, g)
            cs = slice(g * CHUNK, (g + 1) * CHUNK)
            for ch in range(A_TILE // CHUNK):
                rs_ = slice(ch * CHUNK, (ch + 1) * CHUNK)
                mixed = _dot(wt, vnb[rs_, cs], NN) + bt_ref[:, g:g + 1]
                o_ref[rs_, cs] = (u[rs_, cs] * mixed).astype(BF16)

    return pl.pallas_call(
        body, grid=(t // A_TILE,),
        in_specs=[pl.BlockSpec((A_TILE, 2 * A_WIDTH), lambda i: (i, 0)), _const_spec((1, A_WIDTH)),
                  _const_spec((1, A_WIDTH)), _const_spec((A_GROUPS, CHUNK, CHUNK)), _const_spec((CHUNK, A_GROUPS))],
        out_specs=pl.BlockSpec((A_TILE, A_WIDTH), lambda i: (i, 0)), out_shape=SDS((t, A_WIDTH), BF16),
        name="mixer_a_fwd", compiler_params=_params())(proj, lg, lb, w, bias_t)


def _mixer_a_bwd(proj, dcat, lg, lb, w, bias_t):
    t = proj.shape[0]

    def body(p_ref, da_ref, lg_ref, lb_ref, w_ref, bt_ref, dp_ref, dw_ref, dbt_ref, dlg_ref, dlb_ref, du_scr, dvn_scr):
        @pl.when(pl.program_id(0) == 0)
        def _():
            dw_ref[...] = jnp.zeros_like(dw_ref)
            dbt_ref[...] = jnp.zeros_like(dbt_ref)
            dlg_ref[...] = jnp.zeros_like(dlg_ref)
            dlb_ref[...] = jnp.zeros_like(dlb_ref)

        pv, u, vhat, rs, vnb = _a_common(p_ref, lg_ref, lb_ref)
        da = da_ref[...]
        for g in range(A_GROUPS):
            wt, keep = _tril_weights(w_ref, g)
            cs = slice(g * CHUNK, (g + 1) * CHUNK)
            for ch in range(A_TILE // CHUNK):
                rs_ = slice(ch * CHUNK, (ch + 1) * CHUNK)
                vg = vnb[rs_, cs]
                mixed = _dot(wt, vg, NN) + bt_ref[:, g:g + 1]
                du_scr[rs_, cs] = da[rs_, cs] * mixed
                dmx = da[rs_, cs] * u[rs_, cs]
                dw_ref[g] += jnp.where(keep, _dot(dmx, vg, NT), 0.0)
                dvn_scr[rs_, cs] = _dot(wt, dmx, TN)
                dbt_ref[:, g:g + 1] += jnp.sum(dmx, axis=1, keepdims=True)
        dvn = dvn_scr[...]
        dlg_ref[...] += jnp.sum(dvn * vhat, axis=0, keepdims=True)
        dlb_ref[...] += jnp.sum(dvn, axis=0, keepdims=True)
        dvh = dvn * lg_ref[...]
        dv = rs * (dvh - jnp.mean(dvh, axis=-1, keepdims=True) - vhat * jnp.mean(dvh * vhat, axis=-1, keepdims=True))
        gp = _gelu_grad(pv)
        dp_ref[:, :A_WIDTH] = (du_scr[...] * gp[:, :A_WIDTH]).astype(BF16)
        dp_ref[:, A_WIDTH:] = (dv * gp[:, A_WIDTH:]).astype(BF16)

    return pl.pallas_call(
        body, grid=(t // A_TILE,),
        in_specs=[pl.BlockSpec((A_TILE, 2 * A_WIDTH), lambda i: (i, 0)), pl.BlockSpec((A_TILE, A_WIDTH), lambda i: (i, 0)),
                  _const_spec((1, A_WIDTH)), _const_spec((1, A_WIDTH)), _const_spec((A_GROUPS, CHUNK, CHUNK)),
                  _const_spec((CHUNK, A_GROUPS))],
        out_specs=[pl.BlockSpec((A_TILE, 2 * A_WIDTH), lambda i: (i, 0)), _const_spec((A_GROUPS, CHUNK, CHUNK)),
                   _const_spec((CHUNK, A_GROUPS)), _const_spec((1, A_WIDTH)), _const_spec((1, A_WIDTH))],
        out_shape=[SDS((t, 2 * A_WIDTH), BF16), SDS((A_GROUPS, CHUNK, CHUNK), F32), SDS((CHUNK, A_GROUPS), F32),
                   SDS((1, A_WIDTH), F32), SDS((1, A_WIDTH), F32)],
        scratch_shapes=[pltpu.VMEM((A_TILE, A_WIDTH), F32), pltpu.VMEM((A_TILE, A_WIDTH), F32)],
        name="mixer_a_bwd", compiler_params=_params())(proj, dcat, lg, lb, w, bias_t)


def _rope_tables(t):
    half = ROPE_DIM // 2
    inv_freq = ROPE_THETA ** (-jnp.arange(half, dtype=F32) * 2.0 / ROPE_DIM)
    ang = jnp.arange(t, dtype=F32)[:, None] * inv_freq[None, :]
    cos, sin = jnp.cos(ang), jnp.sin(ang)
    one = jnp.ones((t, HEAD_DIM - ROPE_DIM), F32)
    zero = jnp.zeros((t, HEAD_DIM - ROPE_DIM), F32)
    zh = jnp.zeros((t, half), F32)
    c = jnp.concatenate([cos, cos, one], axis=1)
    s1 = jnp.concatenate([-sin, zh, zero], axis=1)
    s2 = jnp.concatenate([zh, sin, zero], axis=1)
    return tuple(jnp.tile(a, (1, LANES // HEAD_DIM)) for a in (c, s1, s2))


QK_TILE = 512
QK_COLS = 2 * N_DIL * B_WIDTH


def _qk_fwd(proj, gains, tabs, ride=None):
    t = proj.shape[0]
    col0 = 2 * A_WIDTH // 1024

    def body(p_ref, g_ref, c_ref, s1_ref, s2_ref, o_ref):
        seg = _segment_mean_matrix(HEAD_DIM)
        c, s1, s2 = c_ref[...], s1_ref[...], s2_ref[...]
        for ci in range(1024 // LANES):
            ls = slice(ci * LANES, (ci + 1) * LANES)
            xv = p_ref[:, ls].astype(F32)
            r = lax.rsqrt(_segment_dot(xv * xv, seg) + EPS)
            y = xv * r * g_ref[:, ls]
            o_ref[:, ls] = (y * c + pltpu.roll(y, LANES - 8, axis=1) * s1 + pltpu.roll(y, 8, axis=1) * s2).astype(BF16)

    tab = pl.BlockSpec((QK_TILE, LANES), lambda i, j: (i, 0))
    return _call(
        body, grid=(t // QK_TILE, QK_COLS // 1024),
        in_specs=[pl.BlockSpec((QK_TILE, 1024), lambda i, j: (i, col0 + j)), pl.BlockSpec((1, 1024), lambda i, j: (0, j)),
                  tab, tab, tab],
        out_specs=pl.BlockSpec((QK_TILE, 1024), lambda i, j: (i, j)), out_shape=SDS((t, QK_COLS), BF16),
        operands=[proj, gains, *tabs], name="qk_norm_rope_fwd", ride=ride)


PERM_TILE = 512


def _permute(name, items, rate):
    t = items[0][0].shape[0]
    n = len(items)
    rows = PERM_TILE // rate

    def body(*refs):
        scr = refs[-1]
        for x_ref, o_ref in zip(refs[:n], refs[n:2 * n]):
            for ci in range(B_WIDTH // LANES):
                scr[ci] = x_ref[:, ci * LANES:(ci + 1) * LANES].astype(F32)
            for rho in range(rate):
                for ci in range(B_WIDTH // LANES):
                    o_ref[rho, :, ci * LANES:(ci + 1) * LANES] = scr[ci, pl.ds(rho, rows, stride=rate), :].astype(o_ref.dtype)

    return pl.pallas_call(
        body, grid=(t // PERM_TILE,),
        in_specs=[pl.BlockSpec((PERM_TILE, B_WIDTH), functools.partial(lambda cb, i: (i, cb), cb)) for _, cb in items],
        out_specs=[pl.BlockSpec((rate, rows, B_WIDTH), lambda i: (0, i, 0)) for _ in items],
        out_shape=[SDS((rate, t // rate, B_WIDTH), a.dtype) for a, _ in items],
        scratch_shapes=[pltpu.VMEM((B_WIDTH // LANES, PERM_TILE, LANES), F32)],
        name=name, compiler_params=_params())(*[a for a, _ in items])


def _unpermute(name, arrays, rate):
    t = arrays[0].shape[1] * rate
    n = len(arrays)
    rows = PERM_TILE // rate

    def body(*refs):
        scr = refs[-1]
        for x_ref, o_ref in zip(refs[:n], refs[n:2 * n]):
            for rho in range(rate):
                for ci in range(B_WIDTH // LANES):
                    scr[ci, pl.ds(rho, rows, stride=rate), :] = x_ref[rho, :, ci * LANES:(ci + 1) * LANES].astype(F32)
            for ci in range(B_WIDTH // LANES):
                o_ref[:, ci * LANES:(ci + 1) * LANES] = scr[ci].astype(o_ref.dtype)

    return pl.pallas_call(
        body, grid=(t // PERM_TILE,),
        in_specs=[pl.BlockSpec((rate, rows, B_WIDTH), lambda i: (0, i, 0)) for _ in arrays],
        out_specs=[pl.BlockSpec((PERM_TILE, B_WIDTH), lambda i: (i, 0)) for _ in arrays],
        out_shape=[SDS((t, B_WIDTH), a.dtype) for a in arrays],
        scratch_shapes=[pltpu.VMEM((B_WIDTH // LANES, PERM_TILE, LANES), F32)],
        name=name, compiler_params=_params())(*arrays)


def _head_lane_mask(h):
    lane = lax.broadcasted_iota(jnp.int32, (1, LANES), 1)
    return (lane < HEAD_DIM) if h == 0 else (lane >= HEAD_DIM)


def _attn_fwd(name, q, k, v, ride=None):
    rate, length = q[0].shape[0], q[0].shape[1]
    nb = length // ATT_BLOCK
    scale = HEAD_DIM ** -0.5

    def body(q_ref, kc_ref, kp_ref, vc_ref, vp_ref, o_ref, l_ref):
        n = pl.program_id(1)
        qi = lax.broadcasted_iota(jnp.int32, (ATT_BLOCK, 2 * ATT_BLOCK), 0)
        cj = lax.broadcasted_iota(jnp.int32, (ATT_BLOCK, 2 * ATT_BLOCK), 1)
        has_prev = jnp.where(n > 0, 0, 2 * ATT_BLOCK)
        mask = ((cj < ATT_BLOCK) & (cj >= qi + has_prev)) | ((cj >= ATT_BLOCK) & (cj - ATT_BLOCK <= qi))
        for hp in range(B_WIDTH // LANES):
            ls = slice(hp * LANES, (hp + 1) * LANES)
            q2 = q_ref[:, ls]
            k2 = jnp.concatenate([kp_ref[:, ls], kc_ref[:, ls]], axis=0)
            v2 = jnp.concatenate([vp_ref[:, ls], vc_ref[:, ls]], axis=0)
            o_acc, lse2 = None, None
            for h in range(2):
                hm = _head_lane_mask(h)
                s = _dot(jnp.where(hm, q2, jnp.zeros_like(q2)), k2, NT) * scale
                s = jnp.where(mask, s, NEG_INF)
                m = jnp.max(s, axis=1, keepdims=True)
                p = jnp.exp(s - m)
                den = jnp.sum(p, axis=1, keepdims=True)
                lse = m + jnp.log(den)
                o = _dot(p / den, jnp.where(hm, v2, jnp.zeros_like(v2)), NN)
                o_acc = o if h == 0 else o_acc + o
                lse_b = lse + jnp.zeros((ATT_BLOCK, LANES), F32)
                lse2 = lse_b if h == 0 else jnp.where(hm, lse_b, lse2)
            o_ref[:, ls] = o_acc
            l_ref[:, ls] = lse2

    def cur(cb):
        return pl.BlockSpec((None, ATT_BLOCK, B_WIDTH), lambda r, n: (r, n, cb))

    def prev(cb):
        return pl.BlockSpec((None, ATT_BLOCK, B_WIDTH), lambda r, n: (r, jnp.maximum(n - 1, 0), cb))

    out = pl.BlockSpec((None, ATT_BLOCK, B_WIDTH), lambda r, n: (r, n, 0))
    return _call(
        body, grid=(rate, nb),
        in_specs=[cur(q[1]), cur(k[1]), prev(k[1]), cur(v[1]), prev(v[1])],
        out_specs=[out, out], out_shape=[SDS((rate, length, B_WIDTH), F32)] * 2,
        operands=[q[0], k[0], k[0], v[0], v[0]], name=name, ride=ride)


def _attn_merge(a_out, o_list, l_list):
    t = a_out.shape[0]
    tm = 512

    def body(a_ref, o0, o1, o2, l0, l1, l2, cat_ref, lt_ref):
        ls = [l0[...], l1[...], l2[...]]
        m = jnp.maximum(jnp.maximum(ls[0], ls[1]), ls[2])
        es = [jnp.exp(l - m) for l in ls]
        den = es[0] + es[1] + es[2]
        b = (es[0] * o0[...] + es[1] * o1[...] + es[2] * o2[...]) / den
        cat_ref[:, :A_WIDTH] = a_ref[...]
        cat_ref[:, A_WIDTH:] = b.astype(BF16)
        lt_ref[...] = m + jnp.log(den)

    blk = _row_spec(tm, B_WIDTH)
    return pl.pallas_call(
        body, grid=(t // tm,), in_specs=[blk] * 7,
        out_specs=[_row_spec(tm, A_WIDTH + B_WIDTH), blk],
        out_shape=[SDS((t, A_WIDTH + B_WIDTH), BF16), SDS((t, B_WIDTH), F32)],
        name="attn_merge", compiler_params=_params())(a_out, *o_list, *l_list)


def _attn_bwd_prep(dcat, cat):
    t = dcat.shape[0]
    tm = 512

    def body(d_ref, b_ref, db_ref, dd_ref):
        seg = _segment_mean_matrix(HEAD_DIM, scale=1.0)
        for ci in range(B_WIDTH // LANES):
            ls = slice(ci * LANES, (ci + 1) * LANES)
            d = d_ref[:, ls]
            db_ref[:, ls] = d.astype(BF16)
            dd_ref[:, ls] = _segment_dot(d * b_ref[:, ls].astype(F32), seg)

    right = pl.BlockSpec((tm, B_WIDTH), lambda i: (i, 1))
    blk = _row_spec(tm, B_WIDTH)
    return pl.pallas_call(
        body, grid=(t // tm,), in_specs=[right, right], out_specs=[blk, blk],
        out_shape=[SDS((t, B_WIDTH), BF16), SDS((t, B_WIDTH), F32)],
        name="attn_bwd_prep", compiler_params=_params())(dcat, cat)


def _attn_bwd(name, q, k, v, db, lse, dd, ride=None):
    rate, length = db.shape[0], db.shape[1]
    nb = length // ATT_BLOCK
    scale = HEAD_DIM ** -0.5

    def body(qa_ref, qb_ref, k_ref, v_ref, dba_ref, dbb_ref, la_ref, lb_ref, da_ref, dbd_ref, dq_ref, dk_ref, dv_ref, carry):
        m = pl.program_id(1)

        @pl.when(m == 0)
        def _():
            carry[...] = jnp.zeros_like(carry)

        row = lax.broadcasted_iota(jnp.int32, (2 * ATT_BLOCK, ATT_BLOCK), 0)
        kj = lax.broadcasted_iota(jnp.int32, (2 * ATT_BLOCK, ATT_BLOCK), 1)
        no_next = jnp.where(m + 1 < nb, 0, 2 * ATT_BLOCK)
        mask = ((row < ATT_BLOCK) & (kj <= row)) | ((row >= ATT_BLOCK) & (kj >= row - ATT_BLOCK + no_next))
        for hp in range(B_WIDTH // LANES):
            ls = slice(hp * LANES, (hp + 1) * LANES)
            k2, v2 = k_ref[:, ls], v_ref[:, ls]
            q2 = jnp.concatenate([qa_ref[:, ls], qb_ref[:, ls]], axis=0)
            db2 = jnp.concatenate([dba_ref[:, ls], dbb_ref[:, ls]], axis=0)
            lse2 = jnp.concatenate([la_ref[:, ls], lb_ref[:, ls]], axis=0)
            dd2 = jnp.concatenate([da_ref[:, ls], dbd_ref[:, ls]], axis=0)
            dq_acc, dk_acc, dv_acc = None, None, None
            for h in range(2):
                hm = _head_lane_mask(h)
                km = jnp.where(hm, k2, jnp.zeros_like(k2))
                vm = jnp.where(hm, v2, jnp.zeros_like(v2))
                lse_col = jnp.max(jnp.where(hm, lse2, NEG_INF), axis=1, keepdims=True)
                dd_col = jnp.max(jnp.where(hm, dd2, NEG_INF), axis=1, keepdims=True)
                s = _dot(q2, km, NT) * scale
                p = jnp.where(mask, jnp.exp(s - lse_col), 0.0)
                dvc = _dot(p, jnp.where(hm, db2, jnp.zeros_like(db2)), TN)
                dp = _dot(db2, vm, NT)
                ds = (p * (dp - dd_col) * scale).astype(BF16)
                dqc = _dot(ds, km, NN)
                dkc = _dot(ds, jnp.where(hm, q2, jnp.zeros_like(q2)), TN)
                dq_acc = dqc if dq_acc is None else dq_acc + dqc
                dk_acc = dkc if dk_acc is None else dk_acc + dkc
                dv_acc = dvc if dv_acc is None else dv_acc + dvc
            dq_ref[:, ls] = (dq_acc[:ATT_BLOCK] + carry[:, ls]).astype(BF16)
            carry[:, ls] = dq_acc[ATT_BLOCK:]
            dk_ref[:, ls] = dk_acc.astype(BF16)
            dv_ref[:, ls] = dv_acc.astype(BF16)

    def cur(cb):
        return pl.BlockSpec((None, ATT_BLOCK, B_WIDTH), lambda r, n: (r, n, cb))

    def nxt(cb):
        return pl.BlockSpec((None, ATT_BLOCK, B_WIDTH), lambda r, n: (r, jnp.minimum(n + 1, nb - 1), cb))

    out = cur(0)
    return _call(
        body, grid=(rate, nb),
        in_specs=[cur(q[1]), nxt(q[1]), cur(k[1]), cur(v[1]), cur(0), nxt(0), cur(0), nxt(0), cur(0), nxt(0)],
        out_specs=[out, out, out], out_shape=[SDS((rate, length, B_WIDTH), BF16)] * 3,
        scratch_shapes=[pltpu.VMEM((ATT_BLOCK, B_WIDTH), F32)],
        operands=[q[0], q[0], k[0], v[0], db, db, lse, lse, dd, dd], name=name, ride=ride)


AB_IN = 2 * A_WIDTH + 3 * N_DIL * B_WIDTH
ASM_TILE = 256


def _dproj_assemble(proj, d_a, dqk, dv, gains, tabs):
    t = proj.shape[0]
    n_qk = 2 * N_DIL

    def body(p_ref, da_ref, *rest):
        dqk_refs = rest[:n_qk]
        dv_refs = rest[n_qk:n_qk + N_DIL]
        g_ref, c_ref, s1_ref, s2_ref, o_ref, dg_ref = rest[n_qk + N_DIL:]

        @pl.when(pl.program_id(0) == 0)
        def _():
            dg_ref[...] = jnp.zeros_like(dg_ref)

        seg = _segment_mean_matrix(HEAD_DIM)
        c, s1, s2 = c_ref[...], s1_ref[...], s2_ref[...]
        o_ref[:, :2 * A_WIDTH] = da_ref[...]
        for jg in range(n_qk):
            for ci in range(B_WIDTH // LANES):
                col = jg * B_WIDTH + ci * LANES
                src = slice(2 * A_WIDTH + col, 2 * A_WIDTH + col + LANES)
                xv = p_ref[:, src].astype(F32)
                r = lax.rsqrt(_segment_dot(xv * xv, seg) + EPS)
                xh = xv * r
                gain = g_ref[:, col:col + LANES]
                do = dqk_refs[jg][:, ci * LANES:(ci + 1) * LANES].astype(F32)
                dy = do * c + pltpu.roll(do * s1, 8, axis=1) + pltpu.roll(do * s2, LANES - 8, axis=1)
                dg_ref[:, col:col + LANES] += jnp.sum(dy * xh, axis=0, keepdims=True)
                dxh = dy * gain
                o_ref[:, src] = (r * (dxh - xh * _segment_dot(dxh * xh, seg))).astype(BF16)
        v0 = 2 * A_WIDTH + QK_COLS
        for g in range(N_DIL):
            o_ref[:, v0 + g * B_WIDTH:v0 + (g + 1) * B_WIDTH] = dv_refs[g][...]

    blk = _row_spec(ASM_TILE, B_WIDTH)
    tab = _row_spec(ASM_TILE, LANES)
    return pl.pallas_call(
        body, grid=(t // ASM_TILE,),
        in_specs=[_row_spec(ASM_TILE, AB_IN), _row_spec(ASM_TILE, 2 * A_WIDTH)] + [blk] * (n_qk + N_DIL)
        + [_const_spec((1, QK_COLS)), tab, tab, tab],
        out_specs=[_row_spec(ASM_TILE, AB_IN), _const_spec((1, QK_COLS))],
        out_shape=[SDS((t, AB_IN), BF16), SDS((1, QK_COLS), F32)],
        name="dproj_assemble", compiler_params=_params())(proj, d_a, *dqk, *dv, gains, *tabs)


def _fold_heads(dg_lane):
    n = dg_lane.shape[1]

    def body(x_ref, o_ref):
        r = lax.broadcasted_iota(jnp.int32, (B_WIDTH, B_WIDTH), 0) % HEAD_DIM
        c = lax.broadcasted_iota(jnp.int32, (B_WIDTH, B_WIDTH), 1) % HEAD_DIM
        fold = jnp.where(r == c, 1.0, 0.0).astype(F32)
        for jg in range(n // B_WIDTH):
            ls = slice(jg * B_WIDTH, (jg + 1) * B_WIDTH)
            o_ref[:, ls] = _dot_hi(jnp.broadcast_to(x_ref[:, ls], (8, B_WIDTH)), fold)

    return pl.pallas_call(body, out_shape=SDS((8, n), F32), name="fold_heads", compiler_params=_params())(dg_lane)


CD_TILE = 256
CD_IN = 2 * C_WIDTH + 3 * 512


def _cd_split(pv):
    w = C_WIDTH
    return pv[:, :w], pv[:, w:2 * w], pv[:, 2 * w:3 * w], pv[:, 3 * w:4 * w], pv[:, 4 * w:5 * w]


def _shifted_copies(src, dst, rows):
    dst[0, :rows] = src[...]
    for b in range(1, 8):
        dst[b, :rows - 8] = src[pl.ds(b, rows - 8), :]


def _rows_from(shifted, start, n):
    b = start % 8
    return shifted[b, pl.ds(start - b, n), :]


def _mixer_cd_fwd(proj, cw, cb, lg, lb, dw):
    t = proj.shape[0]
    per = CD_TILE // HALO

    def body(h_ref, m_ref, cw_ref, cb_ref, lg_ref, lb_ref, dw_ref, o_ref, c1_ref, c_scr, e_scr, c_sh):
        not_first = (pl.program_id(0) > 0).astype(F32)
        ha, hg, _, hgc, hhv = _cd_split(h_ref[...].astype(F32))
        ma, mg, mgb, mgc, mhv = _cd_split(m_ref[...].astype(F32))
        c_scr[:HALO] = ha * _sigmoid(hg) * not_first
        c_scr[HALO:] = ma * _sigmoid(mg)
        e_scr[:HALO] = hgc * hhv * not_first
        e_scr[HALO:] = mgc * mhv
        _shifted_copies(c_scr, c_sh, HALO + CD_TILE)
        acc = jnp.zeros((CD_TILE, C_WIDTH), F32)
        for k in range(C_KERNEL):
            acc = acc + cw_ref[k:k + 1, :] * _rows_from(c_sh, HALO - (C_KERNEL - 1) + k, CD_TILE)
        c1 = acc + cb_ref[...]
        c1_ref[...] = c1
        cc = c1 - jnp.mean(c1, axis=-1, keepdims=True)
        c2 = cc * lax.rsqrt(jnp.mean(cc * cc, axis=-1, keepdims=True) + EPS) * lg_ref[...] + lb_ref[...]
        o_ref[:, :C_WIDTH] = (c2 * _sigmoid(c2)).astype(BF16)
        d1 = jnp.zeros((CD_TILE, C_WIDTH), F32)
        for k in range(D_KERNEL):
            d1 = d1 + dw_ref[k:k + 1, :] * e_scr[pl.ds(HALO - (D_KERNEL - 1) + k, CD_TILE), :]
        o_ref[:, C_WIDTH:] = (mgb * d1).astype(BF16)

    return pl.pallas_call(
        body, grid=(t // CD_TILE,),
        in_specs=[pl.BlockSpec((HALO, CD_IN), lambda i: (jnp.maximum(i * per - 1, 0), 0)), _row_spec(CD_TILE, CD_IN),
                  _const_spec((32, C_WIDTH)), _const_spec((1, C_WIDTH)), _const_spec((1, C_WIDTH)), _const_spec((1, C_WIDTH)),
                  _const_spec((8, C_WIDTH))],
        out_specs=[_row_spec(CD_TILE, 2 * C_WIDTH), _row_spec(CD_TILE, C_WIDTH)],
        out_shape=[SDS((t, 2 * C_WIDTH), BF16), SDS((t, C_WIDTH), F32)],
        scratch_shapes=[pltpu.VMEM((HALO + CD_TILE, C_WIDTH), F32)] * 2 + [pltpu.VMEM((8, HALO + CD_TILE, C_WIDTH), F32)],
        name="mixer_cd_fwd", compiler_params=_params())(proj, proj, cw, cb, lg, lb, dw)


def _mixer_cd_bwd(proj, dcat, c1, cw, lg, lb, dw, ride=None):
    t = proj.shape[0]
    per = CD_TILE // HALO
    nt = t // CD_TILE
    ext = CD_TILE + HALO

    def body(hp_ref, m_ref, hn_ref, dm_ref, dn_ref, c1m_ref, c1n_ref, cw_ref, lg_ref, lb_ref, dw_ref,
             dp_ref, dcw_ref, dcb_ref, dlg_ref, dlb_ref, ddw_ref, c_scr, e_scr, dc1_scr, dd1_scr, c_sh, dc1_sh):
        i = pl.program_id(0)

        @pl.when(i == 0)
        def _():
            for r in (dcw_ref, dcb_ref, dlg_ref, dlb_ref, ddw_ref):
                r[...] = jnp.zeros_like(r)

        not_first = (i > 0).astype(F32)
        not_last = (i < nt - 1).astype(F32)
        pa, pg, _, pgc, phv = _cd_split(hp_ref[...].astype(F32))
        ma, mg, mgb, mgc, mhv = _cd_split(m_ref[...].astype(F32))
        na, ng, ngb, ngc, nhv = _cd_split(hn_ref[...].astype(F32))
        sig_m = _sigmoid(mg)
        c_scr[:HALO] = pa * _sigmoid(pg) * not_first
        c_scr[HALO:HALO + CD_TILE] = ma * sig_m
        c_scr[HALO + CD_TILE:] = na * _sigmoid(ng) * not_last
        e_scr[:HALO] = pgc * phv * not_first
        e_scr[HALO:HALO + CD_TILE] = mgc * mhv
        e_scr[HALO + CD_TILE:] = ngc * nhv * not_last

        _shifted_copies(c_scr, c_sh, 2 * HALO + CD_TILE)
        c1 = jnp.concatenate([c1m_ref[...], c1n_ref[...]], axis=0)
        cc = c1 - jnp.mean(c1, axis=-1, keepdims=True)
        rs = lax.rsqrt(jnp.mean(cc * cc, axis=-1, keepdims=True) + EPS)
        vhat = cc * rs
        c2 = vhat * lg_ref[...] + lb_ref[...]
        sig = _sigmoid(c2)
        dc = jnp.concatenate([dm_ref[:, :C_WIDTH], dn_ref[:, :C_WIDTH] * not_last], axis=0)
        dc2 = dc * (sig * (1.0 + c2 * (1.0 - sig)))
        dvh = dc2 * lg_ref[...]
        dc1 = rs * (dvh - jnp.mean(dvh, axis=-1, keepdims=True) - vhat * jnp.mean(dvh * vhat, axis=-1, keepdims=True))
        dc1_scr[...] = dc1
        _shifted_copies(dc1_scr, dc1_sh, ext)
        dlg_ref[...] += jnp.sum((dc2 * vhat)[:CD_TILE], axis=0, keepdims=True)
        dlb_ref[...] += jnp.sum(dc2[:CD_TILE], axis=0, keepdims=True)
        dc1_m = dc1[:CD_TILE]
        dcb_ref[...] += jnp.sum(dc1_m, axis=0, keepdims=True)
        dc0 = jnp.zeros((CD_TILE, C_WIDTH), F32)
        for k in range(C_KERNEL):
            dc0 = dc0 + cw_ref[k:k + 1, :] * _rows_from(dc1_sh, C_KERNEL - 1 - k, CD_TILE)
            dcw_ref[k:k + 1, :] += jnp.sum(dc1_m * _rows_from(c_sh, HALO - (C_KERNEL - 1) + k, CD_TILE), axis=0, keepdims=True)
        dp_ref[:, :C_WIDTH] = (dc0 * sig_m).astype(BF16)
        dp_ref[:, C_WIDTH:2 * C_WIDTH] = (dc0 * ma * sig_m * (1.0 - sig_m)).astype(BF16)

        d1 = jnp.zeros((CD_TILE, C_WIDTH), F32)
        for k in range(D_KERNEL):
            d1 = d1 + dw_ref[k:k + 1, :] * e_scr[pl.ds(HALO - (D_KERNEL - 1) + k, CD_TILE), :]
        dd_m = dm_ref[:, C_WIDTH:]
        dd1 = jnp.concatenate([dd_m * mgb, dn_ref[:, C_WIDTH:] * ngb * not_last], axis=0)
        dd1_scr[...] = dd1
        dp_ref[:, 2 * C_WIDTH:3 * C_WIDTH] = (dd_m * d1).astype(BF16)
        de = jnp.zeros((CD_TILE, C_WIDTH), F32)
        for k in range(D_KERNEL):
            de = de + dw_ref[k:k + 1, :] * dd1_scr[pl.ds(D_KERNEL - 1 - k, CD_TILE), :]
            ddw_ref[k:k + 1, :] += jnp.sum(dd1[:CD_TILE] * e_scr[pl.ds(HALO - (D_KERNEL - 1) + k, CD_TILE), :], axis=0, keepdims=True)
        dp_ref[:, 3 * C_WIDTH:4 * C_WIDTH] = (de * mhv).astype(BF16)
        dp_ref[:, 4 * C_WIDTH:] = (de * mgc).astype(BF16)

    halo_prev = lambda i: (jnp.maximum(i * per - 1, 0), 0)
    halo_next = lambda i: (jnp.minimum((i + 1) * per, t // HALO - 1), 0)
    vec = _const_spec((1, C_WIDTH))
    return _call(
        body, grid=(nt,),
        in_specs=[pl.BlockSpec((HALO, CD_IN), halo_prev), _row_spec(CD_TILE, CD_IN), pl.BlockSpec((HALO, CD_IN), halo_next),
                  _row_spec(CD_TILE, 2 * C_WIDTH), pl.BlockSpec((HALO, 2 * C_WIDTH), halo_next),
                  _row_spec(CD_TILE, C_WIDTH), pl.BlockSpec((HALO, C_WIDTH), halo_next),
                  _const_spec((32, C_WIDTH)), vec, vec, _const_spec((8, C_WIDTH))],
        out_specs=[_row_spec(CD_TILE, CD_IN), _const_spec((32, C_WIDTH)), vec, vec, vec, _const_spec((8, C_WIDTH))],
        out_shape=[SDS((t, CD_IN), BF16), SDS((32, C_WIDTH), F32), SDS((1, C_WIDTH), F32), SDS((1, C_WIDTH), F32),
                   SDS((1, C_WIDTH), F32), SDS((8, C_WIDTH), F32)],
        scratch_shapes=[pltpu.VMEM((2 * HALO + CD_TILE, C_WIDTH), F32)] * 2 + [pltpu.VMEM((ext, C_WIDTH), F32)] * 2
        + [pltpu.VMEM((8, 2 * HALO + CD_TILE, C_WIDTH), F32), pltpu.VMEM((8, ext, C_WIDTH), F32)],
        operands=[proj, proj, proj, dcat, dcat, c1, c1, cw, lg, lb, dw], name="mixer_cd_bwd", ride=ride)


def _wgrad(name, lhs, lhs_spec, rhs_list, rhs_spec, out_rc, t, ride):
    tk = TILES["wgrad"]
    r, c = out_rc
    n = len(rhs_list)

    def body(*refs):
        ab, out_refs = refs[:2 * n], refs[2 * n:]
        k = pl.program_id(1)
        parts = [_dot(ab[2 * j][...], ab[2 * j + 1][...], TN) for j in range(n)]

        @pl.when(k == 0)
        def _():
            for a in range(n):
                out_refs[a][...] = parts[a]

        @pl.when(k > 0)
        def _():
            for a in range(n):
                out_refs[a][...] += parts[a]

    operands, in_specs = [], []
    for rhs in rhs_list:
        operands += [lhs, rhs]
        in_specs += [lhs_spec, rhs_spec]
    res = _call(body, grid=(N_CHIPS, t // tk), in_specs=in_specs,
                out_specs=[pl.BlockSpec((None, r, c), lambda p, k: (p, 0, 0))] * n,
                out_shape=[SDS((N_CHIPS, r, c), F32)] * n, operands=operands, name=name, ride=ride)
    outs, ride_res = (res, None) if ride is None else res
    outs = [o.reshape(N_CHIPS, 2, r // 2, c) for o in outs]
    return outs if ride is None else (outs, ride_res)


def _wgrad_col_sharded(name, h, dz_list, three_d, ride=None):
    t, d = h.shape
    tk = TILES["wgrad"]
    n4 = dz_list[0].shape[-1] if three_d else dz_list[0].shape[-1] // N_CHIPS
    hs = pl.BlockSpec((tk, d), lambda p, k: (k, 0))
    zs = pl.BlockSpec((None, tk, n4), lambda p, k: (p, k, 0)) if three_d else pl.BlockSpec((tk, n4), lambda p, k: (k, p))
    return _wgrad(name, h, hs, dz_list, zs, (d, n4), t, ride)


def _wgrad_row_sharded(name, a, g, three_d, ride=None):
    t, d = g.shape
    tk = TILES["wgrad"]
    k4 = a.shape[-1] if three_d else a.shape[-1] // N_CHIPS
    a_spec = pl.BlockSpec((None, tk, k4), lambda p, k: (p, k, 0)) if three_d else pl.BlockSpec((tk, k4), lambda p, k: (k, p))
    gs = pl.BlockSpec((tk, d), lambda p, k: (k, 0))
    res = _wgrad(name, a, a_spec, [g], gs, (k4, d), t, ride)
    return res[0] if ride is None else (res[0][0], res[1])


def _mesh_scalars():
    return jnp.stack([lax.axis_index("c"), 2 * lax.axis_index("x") + lax.axis_index("y")]).astype(jnp.int32)


def _stage_own(name, w, layer, dtype):
    layers, r, cols = w.shape
    h = r // 2

    def body(s_ref, x_ref, o_ref):
        o_ref[...] = x_ref[...].astype(dtype)

    return pl.pallas_call(
        body,
        grid_spec=pltpu.PrefetchScalarGridSpec(
            num_scalar_prefetch=1, grid=(2,),
            in_specs=[pl.BlockSpec((None, h, cols), lambda i, s: (2 * layer + i, 0, 0))],
            out_specs=pl.BlockSpec((None, None, h, cols), lambda i, s: (s[1], i, 0, 0))),
        out_shape=SDS((N_CHIPS, 2, h, cols), dtype), name=name,
        compiler_params=_params())(_mesh_scalars(), w.reshape(2 * layers, h, cols))


def _remote(src, dst, send_sem, recv_sem, device):
    return pltpu.make_async_remote_copy(src, dst, send_sem, recv_sem, device_id=device, device_id_type=MESH)


def _ride_gather_send(bufs):
    n = len(bufs)

    def each(b, sems, act):
        send, recv = sems
        x, y, c, p, others = _position()
        for t in range(n):
            for j, (qx, qy) in enumerate(others):
                act(b[t].at[p, c], b[t].at[2 * qx + qy, c], send.at[t, j], recv.at[t, j], (qx, qy, c))

    def start(ins, b, new, sems):
        each(b, sems, lambda mine, landed, s, r, dev: _remote(mine, mine, s, r, dev).start())

    def finish(ins, b, new, sems):
        def act(mine, landed, s, r, dev):
            _remote(mine, mine, s, r, dev).wait_send()
            _remote(landed, landed, s, r, dev).wait_recv()
        each(b, sems, act)

    return _Ride([], bufs, [], [(n, 3), (n, 3)], start, finish)


def _ride_gather_pass(bufs):
    n = len(bufs)

    def each(b, sems, act):
        send, recv = sems
        x, y, c, p, others = _position()
        for t in range(n):
            for j, (qx, qy) in enumerate(others):
                act(b[t].at[2 * qx + qy, c], b[t].at[2 * qx + qy, 1 - c], send.at[t, j], recv.at[t, j], (x, y, 1 - c))

    def start(ins, b, new, sems):
        each(b, sems, lambda landed, passed, s, r, dev: _remote(landed, landed, s, r, dev).start())

    def finish(ins, b, new, sems):
        def act(landed, passed, s, r, dev):
            _remote(landed, landed, s, r, dev).wait_send()
            _remote(passed, passed, s, r, dev).wait_recv()
        each(b, sems, act)

    return _Ride([], bufs, [], [(n, 3), (n, 3)], start, finish)


def _ride_swap(tensors):
    n = len(tensors)

    def each(ins, new, sems, act):
        send, recv = sems
        x, y, c, _, _ = _position()
        for t in range(n):
            act(_remote(ins[t].at[:, 1 - c], new[t], send.at[t], recv.at[t], (x, y, 1 - c)))

    def start(ins, b, new, sems):
        each(ins, new, sems, lambda cp: cp.start())

    def finish(ins, b, new, sems):
        each(ins, new, sems, lambda cp: cp.wait())

    return _Ride(tensors, [], [SDS((s.shape[0],) + s.shape[2:], s.dtype) for s in tensors], [(n,), (n,)], start, finish)


def _ride_scatter(tensors, landing):
    n = len(tensors)

    def each(ins, b, sems, act):
        send, recv = sems
        x, y, c, p, others = _position()
        for t in range(n):
            for j, (qx, qy) in enumerate(others):
                q = 2 * qx + qy
                act(ins[t].at[q], b[t].at[p], b[t].at[q], send.at[t, j], recv.at[t, j], (qx, qy, c))

    def start(ins, b, new, sems):
        each(ins, b, sems, lambda src, dst, landed, s, r, dev: _remote(src, dst, s, r, dev).start())

    def finish(ins, b, new, sems):
        def act(src, dst, landed, s, r, dev):
            _remote(src, dst, s, r, dev).wait_send()
            _remote(landed, landed, s, r, dev).wait_recv()
        each(ins, b, sems, act)

    return _Ride(tensors, landing, [], [(n, 3), (n, 3)], start, finish)


def _ride_join(bufs):
    n = len(bufs)

    def each(b, sems, act):
        send, recv = sems
        x, y, c, _, _ = _position()
        for t in range(n):
            act(b[t].at[c], b[t].at[1 - c], send.at[t], recv.at[t], (x, y, 1 - c))

    def start(ins, b, new, sems):
        each(b, sems, lambda mine, theirs, s, r, dev: _remote(mine, mine, s, r, dev).start())

    def finish(ins, b, new, sems):
        def act(mine, theirs, s, r, dev):
            _remote(mine, mine, s, r, dev).wait_send()
            _remote(theirs, theirs, s, r, dev).wait_recv()
        each(b, sems, act)

    return _Ride([], bufs, [], [(n,), (n,)], start, finish)


def _all_reduce_small(pack):
    rows = pack.shape[0]
    n_dev = 2 * N_CHIPS

    def body(x_ref, o_ref, land, send, recv):
        x, y, c, p, _ = _position()
        me = 2 * p + c
        land[me] = x_ref[...]
        peers = [(dx, dy, dc) for dx in range(2) for dy in range(2) for dc in range(2) if (dx, dy, dc) != (0, 0, 0)]
        for j, (dx, dy, dc) in enumerate(peers):
            _remote(land.at[me], land.at[me], send.at[j], recv.at[j], (x ^ dx, y ^ dy, c ^ dc)).start()
        for j, (dx, dy, dc) in enumerate(peers):
            src = 4 * (x ^ dx) + 2 * (y ^ dy) + (c ^ dc)
            _remote(land.at[me], land.at[me], send.at[j], recv.at[j], (x ^ dx, y ^ dy, c ^ dc)).wait_send()
            _remote(land.at[src], land.at[src], send.at[j], recv.at[j], (x ^ dx, y ^ dy, c ^ dc)).wait_recv()
        acc = land[0]
        for dev in range(1, n_dev):
            acc = acc + land[dev]
        o_ref[...] = acc

    return pl.pallas_call(
        body, out_shape=SDS((rows, LANES), F32),
        scratch_shapes=[pltpu.VMEM((n_dev, rows, LANES), F32), pltpu.SemaphoreType.DMA((n_dev - 1,)),
                        pltpu.SemaphoreType.DMA((n_dev - 1,))],
        name="all_reduce_small", compiler_params=_params())(pack)


def _add_own_half(name, full, recv, out_dtype):
    n4, _, h, cols = full.shape

    def body(s_ref, a_ref, b_ref, o_ref, own_ref):
        v = (a_ref[...] + b_ref[...]).astype(out_dtype)
        o_ref[...] = v

        @pl.when(pl.program_id(0) == s_ref[1])
        def _():
            own_ref[...] = v

    return pl.pallas_call(
        body,
        grid_spec=pltpu.PrefetchScalarGridSpec(
            num_scalar_prefetch=1, grid=(n4,),
            in_specs=[pl.BlockSpec((None, None, h, cols), lambda q, s: (q, s[0], 0, 0)),
                      pl.BlockSpec((None, h, cols), lambda q, s: (q, 0, 0))],
            out_specs=[pl.BlockSpec((None, h, cols), lambda q, s: (q, 0, 0)),
                       pl.BlockSpec((None, h, cols), lambda q, s: (s[1], 0, 0))]),
        out_shape=[SDS((n4, h, cols), out_dtype)] * 2, name=name, compiler_params=_params())(_mesh_scalars(), full, recv)


def _sum_chips(name, parts):
    n4, h, cols = parts.shape
    th = h // 4 if h % 64 == 0 else h

    def body(s_ref, a_ref, o_ref):
        acc = a_ref[0].astype(F32)
        for q in range(1, n4):
            acc = acc + a_ref[q].astype(F32)
        o_ref[...] = acc

    return pl.pallas_call(
        body,
        grid_spec=pltpu.PrefetchScalarGridSpec(
            num_scalar_prefetch=1, grid=(h // th,),
            in_specs=[pl.BlockSpec((n4, th, cols), lambda i, s: (0, i, 0))],
            out_specs=pl.BlockSpec((None, th, cols), lambda i, s: (s[0], i, 0))),
        out_shape=SDS((2, h, cols), F32), name=name, compiler_params=_params())(_mesh_scalars(), parts)


def _adamw_math(w, g, m, v):
    m2 = ADAM_B1 * m + (1.0 - ADAM_B1) * g
    v2 = ADAM_B2 * v + (1.0 - ADAM_B2) * (g * g)
    m_hat = m2 / (1.0 - ADAM_B1 ** ADAM_STEP)
    v_hat = v2 / (1.0 - ADAM_B2 ** ADAM_STEP)
    delta = -ADAM_LR * (m_hat / (jnp.sqrt(v_hat) + ADAM_EPS) + ADAM_WD * w)
    return delta, m2, v2


def _row_tile(rows, cols):
    cap = max(8, (1 << 18) // cols)
    best = 8
    for cand in range(8, min(rows, cap) + 1, 8):
        if rows % cand == 0:
            best = cand
    return best


def _adamw_big(name, w, g_layers, m, v):
    layers, rows, cols = w.shape
    tr = _row_tile(rows, cols)

    def body(w_ref, m_ref, v_ref, *rest):
        g_refs, (g_o, d_o, m_o, v_o) = rest[:layers], rest[layers:]
        gv = g_refs[0][...]
        for layer in range(1, layers):
            gv = jnp.where(pl.program_id(0) == layer, g_refs[layer][...], gv)
        d, mm, vv = _adamw_math(w_ref[...], gv, m_ref[...], v_ref[...])
        g_o[...] = gv
        d_o[...] = d
        m_o[...] = mm
        v_o[...] = vv

    blk = pl.BlockSpec((None, tr, cols), lambda l, i: (l, i, 0))
    g_blk = pl.BlockSpec((tr, cols), lambda l, i: (i, 0))
    return tuple(pl.pallas_call(
        body, grid=(layers, rows // tr), in_specs=[blk] * 3 + [g_blk] * layers, out_specs=[blk] * 4,
        out_shape=[SDS((layers, rows, cols), F32)] * 4, name=name,
        compiler_params=_params())(w, m, v, *[g.reshape(rows, cols) for g in g_layers]))


def _adamw_small(ws, gs, ms, vs):
    n = len(ws)
    flat = []
    for group in (ws, gs, ms, vs):
        flat += [a.reshape(-1, a.shape[-1]) for a in group]

    def body(*refs):
        w_r, g_r, m_r, v_r = refs[:n], refs[n:2 * n], refs[2 * n:3 * n], refs[3 * n:4 * n]
        d_o, m_o, v_o = refs[4 * n:5 * n], refs[5 * n:6 * n], refs[6 * n:7 * n]
        for j in range(n):
            d, mm, vv = _adamw_math(w_r[j][...], g_r[j][...], m_r[j][...], v_r[j][...])
            d_o[j][...] = d
            m_o[j][...] = mm
            v_o[j][...] = vv

    shapes = [SDS(a.shape, F32) for a in flat[:n]]
    outs = pl.pallas_call(body, out_shape=shapes * 3, name="adamw_small", compiler_params=_params())(*flat)
    res = []
    for k in range(3):
        res.append([outs[k * n + j].reshape(ws[j].shape) for j in range(n)])
    return res


BIG = ("ab_w_in", "ab_w_out", "cd_w_in", "cd_w_out", "ffn_w_gate", "ffn_w_up", "ffn_w_down")
V_BLOCK = (2 * A_WIDTH + QK_COLS) // B_WIDTH


def _pad_rows(a, rows):
    return jnp.pad(a, ((0, rows - a.shape[0]), (0, 0)))


A_IN, A_OUT, C_IN, C_OUT = ("ab_w_in", 0), ("ab_w_out", 0), ("cd_w_in", 0), ("cd_w_out", 0)
G0, U0, D0 = ("ffn_w_gate", 0), ("ffn_w_up", 0), ("ffn_w_down", 0)
G1, U1, D1 = ("ffn_w_gate", 1), ("ffn_w_up", 1), ("ffn_w_down", 1)
UNITS = (A_IN, A_OUT, G0, U0, D0, C_IN, C_OUT, G1, U1, D1)
SMALL_SHARDED = ("small", 0)


class _Exchange:
    def __init__(self, enabled):
        self.enabled = enabled
        self.w, self.grad, self.recv, self.half, self.land, self.done = {}, {}, {}, {}, {}, {}

    def full(self, unit):
        b = self.w[unit]
        return b.reshape(N_CHIPS, 1, 2 * b.shape[2], b.shape[3])

    def _ride(self, phases):
        rides, sinks = [], []
        for kind, units in phases:
            if kind == "send":
                rides.append(_ride_gather_send([self.w[u] for u in units]))
                sinks.append(self.w)
            elif kind == "pass":
                rides.append(_ride_gather_pass([self.w[u] for u in units]))
                sinks.append(self.w)
            elif kind == "swap":
                rides.append(_ride_swap([self.grad[u] for u in units]))
                sinks.append(self.recv)
            elif kind == "scatter":
                rides.append(_ride_scatter([self.half[u] for u in units], [self.land[u] for u in units]))
                sinks.append(self.land)
            else:
                rides.append(_ride_join([self.done[u] for u in units]))
                sinks.append(self.done)
        ride = functools.reduce(_ride_both, rides)

        def settle(res):
            n_bufs = sum(len(r.bufs) for r in rides)
            bufs, new = list(res[:n_bufs]), list(res[n_bufs:])
            for r, sink, (_, units) in zip(rides, sinks, phases):
                vals = [bufs.pop(0) for _ in r.bufs] + [new.pop(0) for _ in r.new_outs]
                for u, v in zip(units, vals):
                    sink[u] = v

        return ride, settle

    def run(self, fn, *args, phases=(), **kw):
        if not self.enabled or not phases:
            return fn(*args, **kw)
        ride, settle = self._ride(phases)
        out, res = fn(*args, ride=ride, **kw)
        settle(res)
        return out

    def alone(self, name, phases):
        if self.enabled:
            ride, settle = self._ride(phases)
            settle(_run_ride(name, ride))

    def pair_sum(self, units):
        if self.enabled:
            for u in units:
                dtype = F32 if u == SMALL_SHARDED else BF16
                self.half[u], self.land[u] = _add_own_half(f"pair_sum_{u[0]}_{u[1]}", self.grad[u], self.recv[u], dtype)

    def chip_sum(self, units):
        if self.enabled:
            for u in units:
                self.done[u] = _sum_chips(f"chip_sum_{u[0]}_{u[1]}", self.land[u])


def _local_step(x, target, ex, sp):
    t, d = x.shape
    tabs = _rope_tables(t)
    gains = jnp.concatenate([jnp.tile(sp["q_norm_g"][g], HEAD_DIM // 8) for g in range(N_DIL)]
                            + [jnp.tile(sp["k_norm_g"][g], HEAD_DIM // 8) for g in range(N_DIL)]).reshape(1, QK_COLS)
    bias_t = sp["sgu_bias"].T
    cw = _pad_rows(sp["conv_c_w"], 32)
    dw = _pad_rows(sp["conv_d_w"], 8)
    cb, clg, clb = (sp[k].reshape(1, C_WIDTH) for k in ("conv_c_b", "c_ln_g", "c_ln_b"))
    slg, slb = sp["sgu_norm_g"].reshape(1, A_WIDTH), sp["sgu_norm_b"].reshape(1, A_WIDTH)
    g_ab, g_cd = sp["ab_norm_g"].reshape(1, d), sp["cd_norm_g"].reshape(1, d)
    g_f0, g_f1 = sp["ffn_norm_g"][0:1], sp["ffn_norm_g"][1:2]
    run = ex.run

    def w2d(unit):
        return ex.full(unit).reshape(-1, d)

    h0 = _rms_fwd("rms_ab", x, g_ab)
    proj = run(_proj_in, "proj_ab", h0, ex.full(A_IN), 0, phases=[("send", [A_OUT, G0])])
    a_out = _mixer_a_fwd(proj, slg, slb, sp["sgu_w"], bias_t)
    qk = run(_qk_fwd, proj, gains, tabs, phases=[("pass", [A_OUT, G0]), ("send", [U0])])
    fwd_phases = ([("pass", [U0]), ("send", [D0])], [("pass", [D0]), ("send", [C_IN])],
                  [("pass", [C_IN]), ("send", [C_OUT, G1])])
    qkv, o_list, l_list = [], [], []
    for g, rate in enumerate(DIL_RATES):
        if rate == 1:
            qk3, proj3 = qk.reshape(1, t, QK_COLS), proj.reshape(1, t, AB_IN)
            q, k, v = (qk3, g), (qk3, N_DIL + g), (proj3, V_BLOCK + g)
        else:
            qp, kp, vp = _permute(f"perm_fwd_{g}", [(qk, g), (qk, N_DIL + g), (proj, V_BLOCK + g)], rate)
            q, k, v = (qp, 0), (kp, 0), (vp, 0)
        qkv.append((q, k, v))
        o, l = run(_attn_fwd, f"attn_fwd_{g}", q, k, v, phases=fwd_phases[g])
        if rate == 1:
            o, l = o.reshape(t, B_WIDTH), l.reshape(t, B_WIDTH)
        else:
            o, l = _unpermute(f"unperm_fwd_{g}", [o, l], rate)
        o_list.append(o)
        l_list.append(l)
    cat, lse_tot = _attn_merge(a_out, o_list, l_list)
    x1, hf0 = _proj_out("out_ab", cat, w2d(A_OUT), x, g_next=g_f0)
    gate0, up0, act0 = run(_ffn_in, "ffn_in_0", hf0, ex.full(G0), ex.full(U0), 0,
                           phases=[("pass", [C_OUT, G1]), ("send", [U1])])
    x2, h1 = run(_ffn_out, "ffn_out_0", act0, ex.full(D0), 0, x1, g_next=g_cd, phases=[("pass", [U1]), ("send", [D1])])
    projcd = run(_proj_in, "proj_cd", h1, ex.full(C_IN), 0, phases=[("pass", [D1])])
    cat2, c1 = _mixer_cd_fwd(projcd, cw, cb, clg, clb, dw)
    x3, hf1 = _proj_out("out_cd", cat2, w2d(C_OUT), x2, g_next=g_f1)
    gate1, up1, act1 = _ffn_in("ffn_in_1", hf1, ex.full(G1), ex.full(U1), 0)
    dy, loss_acc, dy_b = _ffn_out("ffn_out_1", act1, ex.full(D1), 0, x3, target=target)
    loss = 0.5 * loss_acc[0, 0] / d

    late = [D1, G1, U1]
    dgate, dup = _ffn_dact("ffn_dact_1", dy_b, ex.full(D1), 0, gate1, up1)
    ex.grad[D1] = _wgrad_row_sharded("wgrad_down_1", act1, dy_b, True)
    ex.grad[G1], ex.grad[U1] = _wgrad_col_sharded("wgrad_gate_up_1", hf1, [dgate, dup], True)
    g3, d_f1, g3_b = run(_dgrad_cols, "dgrad_ffn_1", [dgate, dup], [ex.full(G1), ex.full(U1)], 0, True, x3, g_f1, dy,
                         phases=[("swap", late)])
    ex.pair_sum(late)

    dcat2 = _dgrad_rows("dgrad_out_cd", g3_b, w2d(C_OUT))
    ex.grad[C_OUT] = _wgrad_row_sharded("wgrad_out_cd", cat2, g3_b, False)
    dprojcd, d_cw, d_cb, d_clg, d_clb, d_dw = run(_mixer_cd_bwd, projcd, dcat2, c1, cw, clg, clb, dw, phases=[("scatter", late)])
    ex.chip_sum(late)
    ex.grad[C_IN] = run(_wgrad_col_sharded, "wgrad_in_cd", h1, [dprojcd], False, phases=[("join", late)])[0]
    g2, d_cdn, g2_b = run(_dgrad_cols, "dgrad_in_cd", [dprojcd], [ex.full(C_IN)], 0, False, x2, g_cd, g3,
                          phases=[("swap", [C_OUT, C_IN])])
    ex.pair_sum([C_OUT, C_IN])

    dgate, dup = run(_ffn_dact, "ffn_dact_0", g2_b, ex.full(D0), 0, gate0, up0, phases=[("scatter", [C_OUT, C_IN])])
    ex.chip_sum([C_OUT, C_IN])
    ex.grad[D0] = run(_wgrad_row_sharded, "wgrad_down_0", act0, g2_b, True, phases=[("join", [C_OUT, C_IN])])
    ex.grad[G0], ex.grad[U0] = _wgrad_col_sharded("wgrad_gate_up_0", hf0, [dgate, dup], True)
    small = {"cd_norm_g": d_cdn, "conv_c_w": d_cw[:C_KERNEL], "conv_c_b": d_cb, "c_ln_g": d_clg, "c_ln_b": d_clb,
             "conv_d_w": d_dw[:D_KERNEL]}
    ex.grad[SMALL_SHARDED] = _split_full_small(small).reshape(N_CHIPS, 2, SHARDED_ROWS // 2, LANES)
    mid = [D0, G0, U0, SMALL_SHARDED]
    g1, d_f0, g1_b = run(_dgrad_cols, "dgrad_ffn_0", [dgate, dup], [ex.full(G0), ex.full(U0)], 0, True, x1, g_f0, g2,
                         phases=[("swap", mid)])
    ex.pair_sum(mid)

    dcat = _dgrad_rows("dgrad_out_ab", g1_b, w2d(A_OUT))
    ex.grad[A_OUT] = _wgrad_row_sharded("wgrad_out_ab", cat, g1_b, False)
    d_a, d_sw, d_sbt, d_slg, d_slb = _mixer_a_bwd(proj, dcat, slg, slb, sp["sgu_w"], bias_t)
    dbb, dd = _attn_bwd_prep(dcat, cat)
    bwd_phases = ([("scatter", [D0, G0])], [("scatter", [U0, SMALL_SHARDED]), ("join", [D0, G0]), ("swap", [A_OUT])],
                  [("join", [U0, SMALL_SHARDED]), ("scatter", [A_OUT])])
    dqs, dks, dvs = [], [], []
    for g, rate in enumerate(DIL_RATES):
        q, k, v = qkv[g]
        if rate == 1:
            db3, l3, dd3 = (a.reshape(1, t, B_WIDTH) for a in (dbb, lse_tot, dd))
        else:
            db3, l3, dd3 = _permute(f"perm_bwd_{g}", [(dbb, 0), (lse_tot, 0), (dd, 0)], rate)
        if g == 1:
            ex.chip_sum([D0, G0])
        elif g == 2:
            ex.chip_sum([U0, SMALL_SHARDED])
            ex.pair_sum([A_OUT])
        dq, dk, dv = run(_attn_bwd, f"attn_bwd_{g}", q, k, v, db3, l3, dd3, phases=bwd_phases[g])
        if g == 2:
            ex.chip_sum([A_OUT])
        if rate == 1:
            dq, dk, dv = (a.reshape(t, B_WIDTH) for a in (dq, dk, dv))
        else:
            dq, dk, dv = _unpermute(f"unperm_bwd_{g}", [dq, dk, dv], rate)
        dqs.append(dq)
        dks.append(dk)
        dvs.append(dv)
    dproj, d_gains = _dproj_assemble(proj, d_a, dqs + dks, dvs, gains, tabs)
    d_gains = _fold_heads(d_gains)[0].reshape(2, N_DIL, B_WIDTH)[:, :, :HEAD_DIM]
    ex.grad[A_IN] = run(_wgrad_col_sharded, "wgrad_in_ab", h0, [dproj], False, phases=[("join", [A_OUT])])[0]
    ex.alone("swap_last", [("swap", [A_IN])])
    ex.pair_sum([A_IN])
    gx, d_abn = run(_dgrad_cols, "dgrad_in_ab", [dproj], [ex.full(A_IN)], 0, False, x, g_ab, g1, bf16_copy=False,
                    phases=[("scatter", [A_IN])])
    ex.chip_sum([A_IN])
    ex.alone("join_last", [("join", [A_IN])])

    small.update({
        "ab_norm_g": d_abn, "sgu_norm_g": d_slg, "sgu_norm_b": d_slb, "sgu_w": d_sw, "sgu_bias": d_sbt.T,
        "q_norm_g": d_gains[0], "k_norm_g": d_gains[1], "ffn_norm_g": jnp.concatenate([d_f0, d_f1], axis=0),
    })
    return loss, gx, small


SHARDED_SMALL = ("cd_norm_g", "conv_c_w", "conv_c_b", "c_ln_g", "c_ln_b", "conv_d_w")
SHARDED_ROWS = 48
REPLICATED_SMALL = ("ab_norm_g", "sgu_norm_g", "sgu_norm_b", "sgu_w", "sgu_bias", "q_norm_g", "k_norm_g", "ffn_norm_g")
REPLICATED_ROWS = 560


def _pack_sharded(parts):
    rows = [parts[k].reshape(-1, LANES) for k in SHARDED_SMALL]
    return _pad_rows(jnp.concatenate(rows, axis=0), SHARDED_ROWS)


def _split_full_small(small):
    per_chip = []
    for q in range(N_CHIPS):
        parts = {}
        for k in SHARDED_SMALL:
            a = small[k]
            a = a.reshape(-1, a.shape[-1])
            n = a.shape[-1] // N_CHIPS
            parts[k] = a[:, q * n:(q + 1) * n]
        per_chip.append(_pack_sharded(parts))
    return jnp.stack(per_chip)


def _unpack_sharded(pack, shapes):
    out, r = {}, 0
    for k in SHARDED_SMALL:
        n = math.prod(shapes[k]) // LANES
        out[k] = pack[r:r + n].reshape(shapes[k])
        r += n
    return out


def _gathered_small(packs, shapes):
    per_chip = [_unpack_sharded(packs[q], shapes) for q in range(N_CHIPS)]
    return {k: jnp.concatenate([pc[k] for pc in per_chip], axis=-1) for k in SHARDED_SMALL}


def _pack_replicated(small):
    rows = []
    for k in REPLICATED_SMALL:
        a = small[k].reshape(-1)
        a = jnp.pad(a, (0, (-a.shape[0]) % LANES))
        rows.append(a.reshape(-1, LANES))
    return _pad_rows(jnp.concatenate(rows, axis=0), REPLICATED_ROWS)


def _unpack_replicated(pack, shapes):
    out, r = {}, 0
    for k in REPLICATED_SMALL:
        size = math.prod(shapes[k])
        n = -(-size // LANES)
        out[k] = pack[r:r + n].reshape(-1)[:size].reshape(shapes[k])
        r += n
    return out


WEIGHT_ORDER = ("ab_norm_g", "ab_w_in", "sgu_norm_g", "sgu_norm_b", "sgu_w", "sgu_bias", "q_norm_g", "k_norm_g", "ab_w_out",
                "cd_norm_g", "cd_w_in", "conv_c_w", "conv_c_b", "c_ln_g", "c_ln_b", "conv_d_w", "cd_w_out", "ffn_norm_g",
                "ffn_w_gate", "ffn_w_up", "ffn_w_down")


def kernel(x, ab_norm_g, ab_w_in, sgu_norm_g, sgu_norm_b, sgu_w, sgu_bias, q_norm_g, k_norm_g, ab_w_out, cd_norm_g, cd_w_in, conv_c_w, conv_c_b, c_ln_g, c_ln_b, conv_d_w, cd_w_out, ffn_norm_g, ffn_w_gate, ffn_w_up, ffn_w_down, loss_target, m_ab_norm_g, m_ab_w_in, m_sgu_norm_g, m_sgu_norm_b, m_sgu_w, m_sgu_bias, m_q_norm_g, m_k_norm_g, m_ab_w_out, m_cd_norm_g, m_cd_w_in, m_conv_c_w, m_conv_c_b, m_c_ln_g, m_c_ln_b, m_conv_d_w, m_cd_w_out, m_ffn_norm_g, m_ffn_w_gate, m_ffn_w_up, m_ffn_w_down, v_ab_norm_g, v_ab_w_in, v_sgu_norm_g, v_sgu_norm_b, v_sgu_w, v_sgu_bias, v_q_norm_g, v_k_norm_g, v_ab_w_out, v_cd_norm_g, v_cd_w_in, v_conv_c_w, v_conv_c_b, v_c_ln_g, v_c_ln_b, v_conv_d_w, v_cd_w_out, v_ffn_norm_g, v_ffn_w_gate, v_ffn_w_up, v_ffn_w_down):
    args = dict(locals())
    ws = {k: args[k] for k in WEIGHT_ORDER}
    ms = {k: args["m_" + k] for k in WEIGHT_ORDER}
    vs = {k: args["v_" + k] for k in WEIGHT_ORDER}
    small_names = [k for k in WEIGHT_ORDER if k not in BIG]
    t, d = x.shape[1:]

    ex = _Exchange(enabled=True)
    for name, layer in UNITS:
        ex.w[(name, layer)] = _stage_own(f"stage_{name}_{layer}", ws[name], layer, BF16)
    own_small = _pack_sharded({k: ws[k][0] for k in SHARDED_SMALL})
    ex.w[SMALL_SHARDED] = _stage_own("stage_small", own_small[None], 0, F32)
    ex.alone("gather_first", [("send", [A_IN, SMALL_SHARDED])])
    ex.alone("gather_first_pass", [("pass", [A_IN, SMALL_SHARDED])])
    sp = _gathered_small(ex.w[SMALL_SHARDED].reshape(N_CHIPS, SHARDED_ROWS, LANES), {k: ws[k].shape[1:] for k in SHARDED_SMALL})
    for k in REPLICATED_SMALL:
        sp[k] = ws[k] if k == "ffn_norm_g" else ws[k][0]

    loss, grad_x, g_small = _local_step(x.reshape(t, d), loss_target.reshape(t, d), ex, sp)

    grad = _unpack_sharded(ex.done[SMALL_SHARDED].reshape(SHARDED_ROWS, LANES), {k: ws[k].shape for k in SHARDED_SMALL})
    grad.update(_unpack_replicated(_all_reduce_small(_pack_replicated(g_small)), {k: ws[k].shape for k in REPLICATED_SMALL}))

    delta, new_m, new_v = {}, {}, {}
    for k in BIG:
        g_layers = [ex.done[(k, layer)] for layer in range(ws[k].shape[0])]
        grad[k], delta[k], new_m[k], new_v[k] = _adamw_big("adamw_" + k, ws[k], g_layers, ms[k], vs[k])
    d_s, m_s, v_s = _adamw_small([ws[k] for k in small_names], [grad[k] for k in small_names],
                                 [ms[k] for k in small_names], [vs[k] for k in small_names])
    for j, k in enumerate(small_names):
        delta[k], new_m[k], new_v[k] = d_s[j], m_s[j], v_s[j]

    loss = lax.psum(loss, ("x", "y", "c"))
    return (loss, grad_x[None], *[grad[k] for k in WEIGHT_ORDER], *[delta[k] for k in WEIGHT_ORDER],
            *[new_m[k] for k in WEIGHT_ORDER], *[new_v[k] for k in WEIGHT_ORDER])
```

```python
import functools
import math

import jax
import jax.numpy as jnp
from jax import lax
from jax.experimental import pallas as pl
from jax.experimental.pallas import tpu as pltpu

F32 = jnp.float32
BF16 = jnp.bfloat16
SDS = jax.ShapeDtypeStruct

N_CHIPS = 4
EPS = 1e-6
NEG_INF = -1e30
CHUNK = 128
A_GROUPS = 4
A_WIDTH = 512
N_DIL = 3
DIL_RATES = (1, 4, 16)
HEAD_DIM = 64
B_WIDTH = 512
ROPE_DIM = 16
ROPE_THETA = 500000.0
C_WIDTH = 512
C_KERNEL = 31
D_KERNEL = 3
HALO = 32
ATT_BLOCK = 128
LANES = 128

ADAM_LR = 0.001
ADAM_B1 = 0.9
ADAM_B2 = 0.999
ADAM_EPS = 1e-08
ADAM_WD = 0.01
ADAM_STEP = 10

VMEM_LIMIT = 56 * 1024 * 1024

NN = (((1,), (0,)), ((), ()))
NT = (((1,), (1,)), ((), ()))
TN = (((0,), (0,)), ((), ()))

TILES = {"proj_in": 1024, "proj_out": 1024, "ffn_in": 1024, "ffn_out": 512, "ffn_dact": 512, "dgrad_cols": 512,
         "dgrad_rows": 1024, "wgrad": 2048}


def _params(sem=None):
    return pltpu.CompilerParams(dimension_semantics=sem, vmem_limit_bytes=VMEM_LIMIT)


def _bf(v):
    return v if v.dtype == BF16 else v.astype(BF16)


def _dot(a, b, dims):
    return lax.dot_general(_bf(a), _bf(b), dims, preferred_element_type=F32)


def _dot_hi(a, b):
    return jnp.dot(a, b, precision=lax.Precision.HIGHEST, preferred_element_type=F32)


def _sigmoid(v):
    return 0.5 * jnp.tanh(0.5 * v) + 0.5


def _gelu(v):
    return 0.5 * v * (1.0 + lax.erf(v * (1.0 / math.sqrt(2.0))))


def _gelu_grad(v):
    cdf = 0.5 * (1.0 + lax.erf(v * (1.0 / math.sqrt(2.0))))
    return cdf + v * jnp.exp(-0.5 * v * v) * (1.0 / math.sqrt(2.0 * math.pi))


def _segment_mean_matrix(seg, scale=None):
    r = lax.broadcasted_iota(jnp.int32, (LANES, LANES), 0) // seg
    c = lax.broadcasted_iota(jnp.int32, (LANES, LANES), 1) // seg
    return jnp.where(r == c, (1.0 / seg) if scale is None else scale, 0.0).astype(BF16)


def _segment_dot(v, seg):
    hi = v.astype(BF16)
    lo = (v - hi.astype(F32)).astype(BF16)
    return jnp.dot(hi, seg, preferred_element_type=F32) + jnp.dot(lo, seg, preferred_element_type=F32)


MESH = pl.DeviceIdType.MESH
ANY = pl.BlockSpec(memory_space=pl.ANY)


def _position():
    x, y, c = lax.axis_index("x"), lax.axis_index("y"), lax.axis_index("c")
    others = [(1 - x, y), (x, 1 - y), (1 - x, 1 - y)]
    return x, y, c, 2 * x + y, others


class _Ride:
    def __init__(self, ins, bufs, new_outs, sem_shapes, start, finish):
        self.ins, self.bufs, self.new_outs, self.sem_shapes = list(ins), list(bufs), list(new_outs), list(sem_shapes)
        self.start, self.finish = start, finish


def _ride_both(a, b):
    na = (len(a.ins), len(a.bufs), len(a.new_outs), len(a.sem_shapes))

    def split(ins, bufs, new, sems):
        return ((ins[:na[0]], bufs[:na[1]], new[:na[2]], sems[:na[3]]), (ins[na[0]:], bufs[na[1]:], new[na[2]:], sems[na[3]:]))

    def start(*refs):
        ra, rb = split(*refs)
        a.start(*ra)
        b.start(*rb)

    def finish(*refs):
        ra, rb = split(*refs)
        a.finish(*ra)
        b.finish(*rb)

    return _Ride(a.ins + b.ins, a.bufs + b.bufs, a.new_outs + b.new_outs, a.sem_shapes + b.sem_shapes, start, finish)


def _call(body, *, grid, in_specs, out_specs, out_shape, operands, name, scratch_shapes=(), aliases=None, ride=None):
    if ride is None:
        return pl.pallas_call(body, grid=grid, in_specs=in_specs, out_specs=out_specs, out_shape=out_shape,
                              scratch_shapes=list(scratch_shapes), input_output_aliases=aliases or {}, name=name,
                              compiler_params=_params())(*operands)
    multi = isinstance(out_shape, (list, tuple))
    out_shapes = list(out_shape) if multi else [out_shape]
    o_specs = list(out_specs) if multi else [out_specs]
    n_in, n_out, n_scr = len(operands), len(out_shapes), len(scratch_shapes)
    n_ri, n_rb, n_rn = len(ride.ins), len(ride.bufs), len(ride.new_outs)

    def carrying(*refs):
        k = n_in
        r_ins = refs[k:k + n_ri]
        k += n_ri + n_rb
        outs = refs[k:k + n_out]
        k += n_out
        r_bufs = refs[k:k + n_rb]
        k += n_rb
        r_new = refs[k:k + n_rn]
        k += n_rn
        scratch = refs[k:k + n_scr]
        sems = refs[k + n_scr:]
        first, last = None, None
        for axis, size in enumerate(grid):
            pid = pl.program_id(axis)
            first = (pid == 0) if first is None else first & (pid == 0)
            last = (pid == size - 1) if last is None else last & (pid == size - 1)

        @pl.when(first)
        def _():
            ride.start(r_ins, r_bufs, r_new, sems)

        body(*refs[:n_in], *outs, *scratch)

        @pl.when(last)
        def _():
            ride.finish(r_ins, r_bufs, r_new, sems)

    all_aliases = dict(aliases or {})
    for j in range(n_rb):
        all_aliases[n_in + n_ri + j] = n_out + j
    res = pl.pallas_call(
        carrying, grid=grid, in_specs=list(in_specs) + [ANY] * (n_ri + n_rb), out_specs=o_specs + [ANY] * (n_rb + n_rn),
        out_shape=out_shapes + [SDS(b.shape, b.dtype) for b in ride.bufs] + ride.new_outs,
        scratch_shapes=list(scratch_shapes) + [pltpu.SemaphoreType.DMA(s) for s in ride.sem_shapes],
        input_output_aliases=all_aliases, name=name, compiler_params=_params())(*operands, *ride.ins, *ride.bufs)
    outs = res[:n_out]
    return (list(outs) if multi else outs[0]), list(res[n_out:])


def _run_ride(name, ride):
    n_ri, n_rb, n_rn = len(ride.ins), len(ride.bufs), len(ride.new_outs)

    def body(*refs):
        r_ins = refs[:n_ri]
        r_bufs = refs[n_ri + n_rb:n_ri + 2 * n_rb]
        r_new = refs[n_ri + 2 * n_rb:n_ri + 2 * n_rb + n_rn]
        sems = refs[n_ri + 2 * n_rb + n_rn:]
        ride.start(r_ins, r_bufs, r_new, sems)
        ride.finish(r_ins, r_bufs, r_new, sems)

    return list(pl.pallas_call(
        body, in_specs=[ANY] * (n_ri + n_rb), out_specs=[ANY] * (n_rb + n_rn),
        out_shape=[SDS(b.shape, b.dtype) for b in ride.bufs] + ride.new_outs,
        scratch_shapes=[pltpu.SemaphoreType.DMA(s) for s in ride.sem_shapes],
        input_output_aliases={n_ri + j: j for j in range(n_rb)}, name=name)(*ride.ins, *ride.bufs))


def _whole(ref, p):
    return ref[...]


def _slab(ref, p):
    return ref[p]


def _matmul(name, grid, pairs, extras, outs, dims, epi, *, slabs=1, n_acc=1, ride=None):
    n_pairs, n_ex, n_out = len(pairs), len(extras), len(outs)

    def body(*refs):
        ab = refs[:2 * n_pairs]
        ex = refs[2 * n_pairs:2 * n_pairs + n_ex]
        out_refs = refs[2 * n_pairs + n_ex:2 * n_pairs + n_ex + n_out]
        pids = tuple(pl.program_id(a) for a in range(len(grid)))
        parts = [None] * n_acc
        for p in range(slabs):
            for j, (_, _, a_pick, _, _, b_pick, acc) in enumerate(pairs):
                d = _dot(a_pick(ab[2 * j], p), b_pick(ab[2 * j + 1], p), dims)
                parts[acc] = d if parts[acc] is None else parts[acc] + d
        epi(parts, ex, out_refs, pids)

    operands, in_specs = [], []
    for a, a_spec, _, b, b_spec, _, _ in pairs:
        operands += [a, b]
        in_specs += [a_spec, b_spec]
    for e, e_spec in extras:
        operands.append(e)
        in_specs.append(e_spec)
    return _call(body, grid=grid, in_specs=in_specs, out_specs=[o[1] for o in outs], out_shape=[o[0] for o in outs],
                 operands=operands, name=name, ride=ride)


def _rms_rows(v, g):
    r = lax.rsqrt(jnp.mean(v * v, axis=-1, keepdims=True) + EPS)
    return v * r * g


def _rms_fwd(name, x, g):
    t, d = x.shape
    tm = 512

    def body(x_ref, g_ref, o_ref):
        o_ref[...] = _rms_rows(x_ref[...], g_ref[...]).astype(BF16)

    return pl.pallas_call(
        body, grid=(t // tm,),
        in_specs=[pl.BlockSpec((tm, d), lambda i: (i, 0)), pl.BlockSpec((1, d), lambda i: (0, 0))],
        out_specs=pl.BlockSpec((tm, d), lambda i: (i, 0)), out_shape=SDS((t, d), BF16), name=name,
        compiler_params=_params())(x, g)


def _epi_residual_norm(accs, ex, outs, pids):
    x_new = accs[0] + ex[0][...]
    outs[0][...] = x_new
    outs[1][...] = _rms_rows(x_new, ex[1][...]).astype(BF16)


def _epi_residual_loss(accs, ex, outs, pids):
    y = accs[0] + ex[0][...]
    err = y - ex[1][...]
    dy = err * (1.0 / err.shape[-1])
    outs[0][...] = dy
    outs[2][...] = dy.astype(BF16)

    @pl.when(pids[0] == 0)
    def _():
        outs[1][...] = jnp.zeros_like(outs[1])

    outs[1][...] += jnp.sum(err * err)


def _epi_rms_bwd(accs, ex, outs, pids):
    dh = accs[0]
    xv, g, res = ex[0][...], ex[1][...], ex[2][...]
    r = lax.rsqrt(jnp.mean(xv * xv, axis=-1, keepdims=True) + EPS)
    xh = xv * r
    dy = dh * g
    dx = res + r * (dy - xh * jnp.mean(dy * xh, axis=-1, keepdims=True))
    outs[0][...] = dx
    if len(outs) > 2:
        outs[2][...] = dx.astype(BF16)

    @pl.when(pids[0] == 0)
    def _():
        outs[1][...] = jnp.zeros_like(outs[1])

    outs[1][...] += jnp.sum(dh * xh, axis=0, keepdims=True)


def _row_spec(tm, d):
    return pl.BlockSpec((tm, d), lambda i, *_: (i, 0))


def _const_spec(shape):
    nd = len(shape)
    return pl.BlockSpec(shape, lambda *_: (0,) * nd)


def _proj_in(name, h, w, layer, ride=None):
    t, d = h.shape
    n4 = w.shape[-1]
    tm = TILES["proj_in"]

    def epi(accs, ex, outs, pids):
        outs[0][...] = accs[0].astype(BF16)

    res = _matmul(
        name, (N_CHIPS, t // tm),
        [(h, pl.BlockSpec((tm, d), lambda p, i: (i, 0)), _whole,
          w, pl.BlockSpec((None, None, d, n4), lambda p, i: (p, layer, 0, 0)), _whole, 0)],
        [], [(SDS((t, N_CHIPS * n4), BF16), pl.BlockSpec((tm, n4), lambda p, i: (i, p)))],
        NN, epi, ride=ride)
    return res[0] if ride is None else (res[0][0], res[1])


def _proj_out(name, a, w, x, g_next=None, target=None):
    t, k = a.shape
    d = w.shape[-1]
    tm = TILES["proj_out"]
    if target is None:
        extras = [(x, _row_spec(tm, d)), (g_next, _const_spec((1, d)))]
        outs = [(SDS((t, d), F32), _row_spec(tm, d)), (SDS((t, d), BF16), _row_spec(tm, d))]
        epi = _epi_residual_norm
    else:
        extras = [(x, _row_spec(tm, d)), (target, _row_spec(tm, d))]
        outs = [(SDS((t, d), F32), _row_spec(tm, d)), (SDS((8, LANES), F32), _const_spec((8, LANES))),
                (SDS((t, d), BF16), _row_spec(tm, d))]
        epi = _epi_residual_loss
    return _matmul(name, (t // tm,), [(a, _row_spec(tm, k), _whole, w, _const_spec((k, d)), _whole, 0)], extras, outs, NN, epi)


def _ffn_in(name, h, wg, wu, layer, ride=None):
    t, d = h.shape
    n4 = wg.shape[-2]
    tm = TILES["ffn_in"]

    def epi(accs, ex, outs, pids):
        gate, up = accs
        outs[0][...] = gate.astype(BF16)
        outs[1][...] = up.astype(BF16)
        outs[2][...] = (gate * _sigmoid(gate) * up).astype(BF16)

    w_spec = pl.BlockSpec((None, None, n4, d), lambda p, i: (p, layer, 0, 0))
    h_spec = pl.BlockSpec((tm, d), lambda p, i: (i, 0))
    o = (SDS((N_CHIPS, t, n4), BF16), pl.BlockSpec((None, tm, n4), lambda p, i: (p, i, 0)))
    return _matmul(name, (N_CHIPS, t // tm),
                   [(h, h_spec, _whole, wg, w_spec, _whole, 0), (h, h_spec, _whole, wu, w_spec, _whole, 1)], [],
                   [o, o, o], NT, epi, n_acc=2, ride=ride)


def _ffn_out(name, act, wd, layer, x, g_next=None, target=None, ride=None):
    _, t, n4 = act.shape
    d = wd.shape[-1]
    tm = TILES["ffn_out"]
    xs = _row_spec(tm, d)
    if target is None:
        extras = [(x, xs), (g_next, _const_spec((1, d)))]
        outs = [(SDS((t, d), F32), xs), (SDS((t, d), BF16), xs)]
        epi = _epi_residual_norm
    else:
        extras = [(x, xs), (target, xs)]
        outs = [(SDS((t, d), F32), xs), (SDS((8, LANES), F32), _const_spec((8, LANES))), (SDS((t, d), BF16), xs)]
        epi = _epi_residual_loss
    return _matmul(
        name, (t // tm,),
        [(act, pl.BlockSpec((N_CHIPS, tm, n4), lambda i: (0, i, 0)), _slab,
          wd, pl.BlockSpec((N_CHIPS, None, n4, d), lambda i: (0, layer, 0, 0)), _slab, 0)],
        extras, outs, NN, epi, slabs=N_CHIPS, ride=ride)


def _ffn_dact(name, g, wd, layer, gate, up, ride=None):
    t, d = g.shape
    n4 = wd.shape[-2]
    tm = TILES["ffn_dact"]

    def epi(accs, ex, outs, pids):
        dact = accs[0]
        gt = ex[0][...].astype(F32)
        upv = ex[1][...].astype(F32)
        s = _sigmoid(gt)
        silu = gt * s
        outs[0][...] = (dact * upv * (s + silu - silu * s)).astype(BF16)
        outs[1][...] = (dact * silu).astype(BF16)

    blk = pl.BlockSpec((None, tm, n4), lambda p, i: (p, i, 0))
    o = (SDS((N_CHIPS, t, n4), BF16), blk)
    return _matmul(
        name, (N_CHIPS, t // tm),
        [(g, pl.BlockSpec((tm, d), lambda p, i: (i, 0)), _whole,
          wd, pl.BlockSpec((None, None, n4, d), lambda p, i: (p, layer, 0, 0)), _whole, 0)],
        [(gate, blk), (up, blk)], [o, o], NT, epi, ride=ride)


def _copy_epi(accs, ex, outs, pids):
    for a, o in zip(accs, outs):
        o[...] = a.astype(o.dtype)


def _dgrad_cols(name, dz_list, w_list, layer, three_d, x, g, res, bf16_copy=True, w_rows=False, ride=None):
    t, d = x.shape
    n4 = w_list[0].shape[-2 if w_rows else -1]
    tm = TILES["dgrad_cols"]
    if three_d:
        zs, z_pick = pl.BlockSpec((N_CHIPS, tm, n4), lambda i: (0, i, 0)), _slab
    else:
        zs, z_pick = _row_spec(tm, N_CHIPS * n4), (lambda ref, p: ref[:, p * n4:(p + 1) * n4])
    ws = pl.BlockSpec((N_CHIPS, None) + ((n4, d) if w_rows else (d, n4)), lambda i: (0, layer, 0, 0))
    xs = _row_spec(tm, d)
    return _matmul(
        name, (t // tm,), [(dz, zs, z_pick, w, ws, _slab, 0) for dz, w in zip(dz_list, w_list)],
        [(x, xs), (g, _const_spec((1, d))), (res, xs)],
        [(SDS((t, d), F32), xs), (SDS((1, d), F32), _const_spec((1, d)))] + ([(SDS((t, d), BF16), xs)] if bf16_copy else []),
        NN if w_rows else NT, _epi_rms_bwd, slabs=N_CHIPS, ride=ride)


def _dgrad_rows(name, g, w):
    t, d = g.shape
    k = w.shape[0]
    tm = TILES["dgrad_rows"]
    return _matmul(name, (t // tm,), [(g, _row_spec(tm, d), _whole, w, _const_spec((k, d)), _whole, 0)], [],
                   [(SDS((t, k), F32), _row_spec(tm, k))], NT, _copy_epi)[0]


A_TILE = 256


def _a_common(p_ref, lg_ref, lb_ref):
    pv = p_ref[...].astype(F32)
    a = _gelu(pv)
    u, v = a[:, :A_WIDTH], a[:, A_WIDTH:]
    vc = v - jnp.mean(v, axis=-1, keepdims=True)
    rs = lax.rsqrt(jnp.mean(vc * vc, axis=-1, keepdims=True) + EPS)
    vhat = vc * rs
    vn = vhat * lg_ref[...] + lb_ref[...]
    return pv, u, vhat, rs, vn.astype(BF16)


def _tril_weights(w_ref, g):
    r = lax.broadcasted_iota(jnp.int32, (CHUNK, CHUNK), 0)
    c = lax.broadcasted_iota(jnp.int32, (CHUNK, CHUNK), 1)
    return jnp.where(c <= r, w_ref[g], 0.0).astype(BF16), c <= r


def _mixer_a_fwd(proj, lg, lb, w, bias_t):
    t = proj.shape[0]

    def body(p_ref, lg_ref, lb_ref, w_ref, bt_ref, o_ref):
        _, u, _, _, vnb = _a_common(p_ref, lg_ref, lb_ref)
        for g in range(A_GROUPS):
            wt, _ = _tril_weights(w_ref, g)
            cs = slice(g * CHUNK, (g + 1) * CHUNK)
            for ch in range(A_TILE // CHUNK):
                rs_ = slice(ch * CHUNK, (ch + 1) * CHUNK)
                mixed = _dot(wt, vnb[rs_, cs], NN) + bt_ref[:, g:g + 1]
                o_ref[rs_, cs] = (u[rs_, cs] * mixed).astype(BF16)

    return pl.pallas_call(
        body, grid=(t // A_TILE,),
        in_specs=[pl.BlockSpec((A_TILE, 2 * A_WIDTH), lambda i: (i, 0)), _const_spec((1, A_WIDTH)),
                  _const_spec((1, A_WIDTH)), _const_spec((A_GROUPS, CHUNK, CHUNK)), _const_spec((CHUNK, A_GROUPS))],
        out_specs=pl.BlockSpec((A_TILE, A_WIDTH), lambda i: (i, 0)), out_shape=SDS((t, A_WIDTH), BF16),
        name="mixer_a_fwd", compiler_params=_params())(proj, lg, lb, w, bias_t)


def _mixer_a_bwd(proj, dcat, lg, lb, w, bias_t):
    t = proj.shape[0]

    def body(p_ref, da_ref, lg_ref, lb_ref, w_ref, bt_ref, dp_ref, dw_ref, dbt_ref, dlg_ref, dlb_ref, du_scr, dvn_scr):
        @pl.when(pl.program_id(0) == 0)
        def _():
            dw_ref[...] = jnp.zeros_like(dw_ref)
            dbt_ref[...] = jnp.zeros_like(dbt_ref)
            dlg_ref[...] = jnp.zeros_like(dlg_ref)
            dlb_ref[...] = jnp.zeros_like(dlb_ref)

        pv, u, vhat, rs, vnb = _a_common(p_ref, lg_ref, lb_ref)
        da = da_ref[...]
        for g in range(A_GROUPS):
            wt, keep = _tril_weights(w_ref, g)
            cs = slice(g * CHUNK, (g + 1) * CHUNK)
            for ch in range(A_TILE // CHUNK):
                rs_ = slice(ch * CHUNK, (ch + 1) * CHUNK)
                vg = vnb[rs_, cs]
                mixed = _dot(wt, vg, NN) + bt_ref[:, g:g + 1]
                du_scr[rs_, cs] = da[rs_, cs] * mixed
                dmx = da[rs_, cs] * u[rs_, cs]
                dw_ref[g] += jnp.where(keep, _dot(dmx, vg, NT), 0.0)
                dvn_scr[rs_, cs] = _dot(wt, dmx, TN)
                dbt_ref[:, g:g + 1] += jnp.sum(dmx, axis=1, keepdims=True)
        dvn = dvn_scr[...]
        dlg_ref[...] += jnp.sum(dvn * vhat, axis=0, keepdims=True)
        dlb_ref[...] += jnp.sum(dvn, axis=0, keepdims=True)
        dvh = dvn * lg_ref[...]
        dv = rs * (dvh - jnp.mean(dvh, axis=-1, keepdims=True) - vhat * jnp.mean(dvh * vhat, axis=-1, keepdims=True))
        gp = _gelu_grad(pv)
        dp_ref[:, :A_WIDTH] = (du_scr[...] * gp[:, :A_WIDTH]).astype(BF16)
        dp_ref[:, A_WIDTH:] = (dv * gp[:, A_WIDTH:]).astype(BF16)

    return pl.pallas_call(
        body, grid=(t // A_TILE,),
        in_specs=[pl.BlockSpec((A_TILE, 2 * A_WIDTH), lambda i: (i, 0)), pl.BlockSpec((A_TILE, A_WIDTH), lambda i: (i, 0)),
                  _const_spec((1, A_WIDTH)), _const_spec((1, A_WIDTH)), _const_spec((A_GROUPS, CHUNK, CHUNK)),
                  _const_spec((CHUNK, A_GROUPS))],
        out_specs=[pl.BlockSpec((A_TILE, 2 * A_WIDTH), lambda i: (i, 0)), _const_spec((A_GROUPS, CHUNK, CHUNK)),
                   _const_spec((CHUNK, A_GROUPS)), _const_spec((1, A_WIDTH)), _const_spec((1, A_WIDTH))],
        out_shape=[SDS((t, 2 * A_WIDTH), BF16), SDS((A_GROUPS, CHUNK, CHUNK), F32), SDS((CHUNK, A_GROUPS), F32),
                   SDS((1, A_WIDTH), F32), SDS((1, A_WIDTH), F32)],
        scratch_shapes=[pltpu.VMEM((A_TILE, A_WIDTH), F32), pltpu.VMEM((A_TILE, A_WIDTH), F32)],
        name="mixer_a_bwd", compiler_params=_params())(proj, dcat, lg, lb, w, bias_t)


def _rope_tables(t):
    half = ROPE_DIM // 2
    inv_freq = ROPE_THETA ** (-jnp.arange(half, dtype=F32) * 2.0 / ROPE_DIM)
    ang = jnp.arange(t, dtype=F32)[:, None] * inv_freq[None, :]
    cos, sin = jnp.cos(ang), jnp.sin(ang)
    one = jnp.ones((t, HEAD_DIM - ROPE_DIM), F32)
    zero = jnp.zeros((t, HEAD_DIM - ROPE_DIM), F32)
    zh = jnp.zeros((t, half), F32)
    c = jnp.concatenate([cos, cos, one], axis=1)
    s1 = jnp.concatenate([-sin, zh, zero], axis=1)
    s2 = jnp.concatenate([zh, sin, zero], axis=1)
    return tuple(jnp.tile(a, (1, LANES // HEAD_DIM)) for a in (c, s1, s2))


QK_TILE = 512
QK_COLS = 2 * N_DIL * B_WIDTH


def _qk_fwd(proj, gains, tabs, ride=None):
    t = proj.shape[0]
    col0 = 2 * A_WIDTH // 1024

    def body(p_ref, g_ref, c_ref, s1_ref, s2_ref, o_ref):
        seg = _segment_mean_matrix(HEAD_DIM)
        c, s1, s2 = c_ref[...], s1_ref[...], s2_ref[...]
        for ci in range(1024 // LANES):
            ls = slice(ci * LANES, (ci + 1) * LANES)
            xv = p_ref[:, ls].astype(F32)
            r = lax.rsqrt(_segment_dot(xv * xv, seg) + EPS)
            y = xv * r * g_ref[:, ls]
            o_ref[:, ls] = (y * c + pltpu.roll(y, LANES - 8, axis=1) * s1 + pltpu.roll(y, 8, axis=1) * s2).astype(BF16)

    tab = pl.BlockSpec((QK_TILE, LANES), lambda i, j: (i, 0))
    return _call(
        body, grid=(t // QK_TILE, QK_COLS // 1024),
        in_specs=[pl.BlockSpec((QK_TILE, 1024), lambda i, j: (i, col0 + j)), pl.BlockSpec((1, 1024), lambda i, j: (0, j)),
                  tab, tab, tab],
        out_specs=pl.BlockSpec((QK_TILE, 1024), lambda i, j: (i, j)), out_shape=SDS((t, QK_COLS), BF16),
        operands=[proj, gains, *tabs], name="qk_norm_rope_fwd", ride=ride)


PERM_TILE = 512


def _permute(name, items, rate):
    t = items[0][0].shape[0]
    n = len(items)
    rows = PERM_TILE // rate

    def body(*refs):
        scr = refs[-1]
        for x_ref, o_ref in zip(refs[:n], refs[n:2 * n]):
            for ci in range(B_WIDTH // LANES):
                scr[ci] = x_ref[:, ci * LANES:(ci + 1) * LANES].astype(F32)
            for rho in range(rate):
                for ci in range(B_WIDTH // LANES):
                    o_ref[rho, :, ci * LANES:(ci + 1) * LANES] = scr[ci, pl.ds(rho, rows, stride=rate), :].astype(o_ref.dtype)

    return pl.pallas_call(
        body, grid=(t // PERM_TILE,),
        in_specs=[pl.BlockSpec((PERM_TILE, B_WIDTH), functools.partial(lambda cb, i: (i, cb), cb)) for _, cb in items],
        out_specs=[pl.BlockSpec((rate, rows, B_WIDTH), lambda i: (0, i, 0)) for _ in items],
        out_shape=[SDS((rate, t // rate, B_WIDTH), a.dtype) for a, _ in items],
        scratch_shapes=[pltpu.VMEM((B_WIDTH // LANES, PERM_TILE, LANES), F32)],
        name=name, compiler_params=_params())(*[a for a, _ in items])


def _unpermute(name, arrays, rate):
    t = arrays[0].shape[1] * rate
    n = len(arrays)
    rows = PERM_TILE // rate

    def body(*refs):
        scr = refs[-1]
        for x_ref, o_ref in zip(refs[:n], refs[n:2 * n]):
            for rho in range(rate):
                for ci in range(B_WIDTH // LANES):
                    scr[ci, pl.ds(rho, rows, stride=rate), :] = x_ref[rho, :, ci * LANES:(ci + 1) * LANES].astype(F32)
            for ci in range(B_WIDTH // LANES):
                o_ref[:, ci * LANES:(ci + 1) * LANES] = scr[ci].astype(o_ref.dtype)

    return pl.pallas_call(
        body, grid=(t // PERM_TILE,),
        in_specs=[pl.BlockSpec((rate, rows, B_WIDTH), lambda i: (0, i, 0)) for _ in arrays],
        out_specs=[pl.BlockSpec((PERM_TILE, B_WIDTH), lambda i: (i, 0)) for _ in arrays],
        out_shape=[SDS((t, B_WIDTH), a.dtype) for a in arrays],
        scratch_shapes=[pltpu.VMEM((B_WIDTH // LANES, PERM_TILE, LANES), F32)],
        name=name, compiler_params=_params())(*arrays)


def _head_lane_mask(h):
    lane = lax.broadcasted_iota(jnp.int32, (1, LANES), 1)
    return (lane < HEAD_DIM) if h == 0 else (lane >= HEAD_DIM)


def _attn_fwd(name, q, k, v, ride=None):
    rate, length = q[0].shape[0], q[0].shape[1]
    nb = length // ATT_BLOCK
    scale = HEAD_DIM ** -0.5

    def body(q_ref, kc_ref, kp_ref, vc_ref, vp_ref, o_ref, l_ref):
        n = pl.program_id(1)
        qi = lax.broadcasted_iota(jnp.int32, (ATT_BLOCK, 2 * ATT_BLOCK), 0)
        cj = lax.broadcasted_iota(jnp.int32, (ATT_BLOCK, 2 * ATT_BLOCK), 1)
        has_prev = jnp.where(n > 0, 0, 2 * ATT_BLOCK)
        mask = ((cj < ATT_BLOCK) & (cj >= qi + has_prev)) | ((cj >= ATT_BLOCK) & (cj - ATT_BLOCK <= qi))
        for hp in range(B_WIDTH // LANES):
            ls = slice(hp * LANES, (hp + 1) * LANES)
            q2 = q_ref[:, ls]
            k2 = jnp.concatenate([kp_ref[:, ls], kc_ref[:, ls]], axis=0)
            v2 = jnp.concatenate([vp_ref[:, ls], vc_ref[:, ls]], axis=0)
            o_acc, lse2 = None, None
            for h in range(2):
                hm = _head_lane_mask(h)
                s = _dot(jnp.where(hm, q2, jnp.zeros_like(q2)), k2, NT) * scale
                s = jnp.where(mask, s, NEG_INF)
                m = jnp.max(s, axis=1, keepdims=True)
                p = jnp.exp(s - m)
                den = jnp.sum(p, axis=1, keepdims=True)
                lse = m + jnp.log(den)
                o = _dot(p / den, jnp.where(hm, v2, jnp.zeros_like(v2)), NN)
                o_acc = o if h == 0 else o_acc + o
                lse_b = lse + jnp.zeros((ATT_BLOCK, LANES), F32)
                lse2 = lse_b if h == 0 else jnp.where(hm, lse_b, lse2)
            o_ref[:, ls] = o_acc
            l_ref[:, ls] = lse2

    def cur(cb):
        return pl.BlockSpec((None, ATT_BLOCK, B_WIDTH), lambda r, n: (r, n, cb))

    def prev(cb):
        return pl.BlockSpec((None, ATT_BLOCK, B_WIDTH), lambda r, n: (r, jnp.maximum(n - 1, 0), cb))

    out = pl.BlockSpec((None, ATT_BLOCK, B_WIDTH), lambda r, n: (r, n, 0))
    return _call(
        body, grid=(rate, nb),
        in_specs=[cur(q[1]), cur(k[1]), prev(k[1]), cur(v[1]), prev(v[1])],
        out_specs=[out, out], out_shape=[SDS((rate, length, B_WIDTH), F32)] * 2,
        operands=[q[0], k[0], k[0], v[0], v[0]], name=name, ride=ride)


def _attn_merge(a_out, o_list, l_list):
    t = a_out.shape[0]
    tm = 512

    def body(a_ref, o0, o1, o2, l0, l1, l2, cat_ref, lt_ref):
        ls = [l0[...], l1[...], l2[...]]
        m = jnp.maximum(jnp.maximum(ls[0], ls[1]), ls[2])
        es = [jnp.exp(l - m) for l in ls]
        den = es[0] + es[1] + es[2]
        b = (es[0] * o0[...] + es[1] * o1[...] + es[2] * o2[...]) / den
        cat_ref[:, :A_WIDTH] = a_ref[...]
        cat_ref[:, A_WIDTH:] = b.astype(BF16)
        lt_ref[...] = m + jnp.log(den)

    blk = _row_spec(tm, B_WIDTH)
    return pl.pallas_call(
        body, grid=(t // tm,), in_specs=[blk] * 7,
        out_specs=[_row_spec(tm, A_WIDTH + B_WIDTH), blk],
        out_shape=[SDS((t, A_WIDTH + B_WIDTH), BF16), SDS((t, B_WIDTH), F32)],
        name="attn_merge", compiler_params=_params())(a_out, *o_list, *l_list)


def _attn_bwd_prep(dcat, cat):
    t = dcat.shape[0]
    tm = 512

    def body(d_ref, b_ref, db_ref, dd_ref):
        seg = _segment_mean_matrix(HEAD_DIM, scale=1.0)
        for ci in range(B_WIDTH // LANES):
            ls = slice(ci * LANES, (ci + 1) * LANES)
            d = d_ref[:, ls]
            db_ref[:, ls] = d.astype(BF16)
            dd_ref[:, ls] = _segment_dot(d * b_ref[:, ls].astype(F32), seg)

    right = pl.BlockSpec((tm, B_WIDTH), lambda i: (i, 1))
    blk = _row_spec(tm, B_WIDTH)
    return pl.pallas_call(
        body, grid=(t // tm,), in_specs=[right, right], out_specs=[blk, blk],
        out_shape=[SDS((t, B_WIDTH), BF16), SDS((t, B_WIDTH), F32)],
        name="attn_bwd_prep", compiler_params=_params())(dcat, cat)


def _attn_bwd(name, q, k, v, db, lse, dd, ride=None):
    rate, length = db.shape[0], db.shape[1]
    nb = length // ATT_BLOCK
    scale = HEAD_DIM ** -0.5

    def body(qa_ref, qb_ref, k_ref, v_ref, dba_ref, dbb_ref, la_ref, lb_ref, da_ref, dbd_ref, dq_ref, dk_ref, dv_ref, carry):
        m = pl.program_id(1)

        @pl.when(m == 0)
        def _():
            carry[...] = jnp.zeros_like(carry)

        row = lax.broadcasted_iota(jnp.int32, (2 * ATT_BLOCK, ATT_BLOCK), 0)
        kj = lax.broadcasted_iota(jnp.int32, (2 * ATT_BLOCK, ATT_BLOCK), 1)
        no_next = jnp.where(m + 1 < nb, 0, 2 * ATT_BLOCK)
        mask = ((row < ATT_BLOCK) & (kj <= row)) | ((row >= ATT_BLOCK) & (kj >= row - ATT_BLOCK + no_next))
        for hp in range(B_WIDTH // LANES):
            ls = slice(hp * LANES, (hp + 1) * LANES)
            k2, v2 = k_ref[:, ls], v_ref[:, ls]
            q2 = jnp.concatenate([qa_ref[:, ls], qb_ref[:, ls]], axis=0)
            db2 = jnp.concatenate([dba_ref[:, ls], dbb_ref[:, ls]], axis=0)
            lse2 = jnp.concatenate([la_ref[:, ls], lb_ref[:, ls]], axis=0)
            dd2 = jnp.concatenate([da_ref[:, ls], dbd_ref[:, ls]], axis=0)
            dq_acc, dk_acc, dv_acc = None, None, None
            for h in range(2):
                hm = _head_lane_mask(h)
                km = jnp.where(hm, k2, jnp.zeros_like(k2))
                vm = jnp.where(hm, v2, jnp.zeros_like(v2))
                lse_col = jnp.max(jnp.where(hm, lse2, NEG_INF), axis=1, keepdims=True)
                dd_col = jnp.max(jnp.where(hm, dd2, NEG_INF), axis=1, keepdims=True)
                s = _dot(q2, km, NT) * scale
                p = jnp.where(mask, jnp.exp(s - lse_col), 0.0)
                dvc = _dot(p, jnp.where(hm, db2, jnp.zeros_like(db2)), TN)
                dp = _dot(db2, vm, NT)
                ds = (p * (dp - dd_col) * scale).astype(BF16)
                dqc = _dot(ds, km, NN)
                dkc = _dot(ds, jnp.where(hm, q2, jnp.zeros_like(q2)), TN)
                dq_acc = dqc if dq_acc is None else dq_acc + dqc
                dk_acc = dkc if dk_acc is None else dk_acc + dkc
                dv_acc = dvc if dv_acc is None else dv_acc + dvc
            dq_ref[:, ls] = (dq_acc[:ATT_BLOCK] + carry[:, ls]).astype(BF16)
            carry[:, ls] = dq_acc[ATT_BLOCK:]
            dk_ref[:, ls] = dk_acc.astype(BF16)
            dv_ref[:, ls] = dv_acc.astype(BF16)

    def cur(cb):
        return pl.BlockSpec((None, ATT_BLOCK, B_WIDTH), lambda r, n: (r, n, cb))

    def nxt(cb):
        return pl.BlockSpec((None, ATT_BLOCK, B_WIDTH), lambda r, n: (r, jnp.minimum(n + 1, nb - 1), cb))

    out = cur(0)
    return _call(
        body, grid=(rate, nb),
        in_specs=[cur(q[1]), nxt(q[1]), cur(k[1]), cur(v[1]), cur(0), nxt(0), cur(0), nxt(0), cur(0), nxt(0)],
        out_specs=[out, out, out], out_shape=[SDS((rate, length, B_WIDTH), BF16)] * 3,
        scratch_shapes=[pltpu.VMEM((ATT_BLOCK, B_WIDTH), F32)],
        operands=[q[0], q[0], k[0], v[0], db, db, lse, lse, dd, dd], name=name, ride=ride)


AB_IN = 2 * A_WIDTH + 3 * N_DIL * B_WIDTH
ASM_TILE = 256


def _dproj_assemble(proj, d_a, dqk, dv, gains, tabs):
    t = proj.shape[0]
    n_qk = 2 * N_DIL

    def body(p_ref, da_ref, *rest):
        dqk_refs = rest[:n_qk]
        dv_refs = rest[n_qk:n_qk + N_DIL]
        g_ref, c_ref, s1_ref, s2_ref, o_ref, dg_ref = rest[n_qk + N_DIL:]

        @pl.when(pl.program_id(0) == 0)
        def _():
            dg_ref[...] = jnp.zeros_like(dg_ref)

        seg = _segment_mean_matrix(HEAD_DIM)
        c, s1, s2 = c_ref[...], s1_ref[...], s2_ref[...]
        o_ref[:, :2 * A_WIDTH] = da_ref[...]
        for jg in range(n_qk):
            for ci in range(B_WIDTH // LANES):
                col = jg * B_WIDTH + ci * LANES
                src = slice(2 * A_WIDTH + col, 2 * A_WIDTH + col + LANES)
                xv = p_ref[:, src].astype(F32)
                r = lax.rsqrt(_segment_dot(xv * xv, seg) + EPS)
                xh = xv * r
                gain = g_ref[:, col:col + LANES]
                do = dqk_refs[jg][:, ci * LANES:(ci + 1) * LANES].astype(F32)
                dy = do * c + pltpu.roll(do * s1, 8, axis=1) + pltpu.roll(do * s2, LANES - 8, axis=1)
                dg_ref[:, col:col + LANES] += jnp.sum(dy * xh, axis=0, keepdims=True)
                dxh = dy * gain
                o_ref[:, src] = (r * (dxh - xh * _segment_dot(dxh * xh, seg))).astype(BF16)
        v0 = 2 * A_WIDTH + QK_COLS
        for g in range(N_DIL):
            o_ref[:, v0 + g * B_WIDTH:v0 + (g + 1) * B_WIDTH] = dv_refs[g][...]

    blk = _row_spec(ASM_TILE, B_WIDTH)
    tab = _row_spec(ASM_TILE, LANES)
    return pl.pallas_call(
        body, grid=(t // ASM_TILE,),
        in_specs=[_row_spec(ASM_TILE, AB_IN), _row_spec(ASM_TILE, 2 * A_WIDTH)] + [blk] * (n_qk + N_DIL)
        + [_const_spec((1, QK_COLS)), tab, tab, tab],
        out_specs=[_row_spec(ASM_TILE, AB_IN), _const_spec((1, QK_COLS))],
        out_shape=[SDS((t, AB_IN), BF16), SDS((1, QK_COLS), F32)],
        name="dproj_assemble", compiler_params=_params())(proj, d_a, *dqk, *dv, gains, *tabs)


def _fold_heads(dg_lane):
    n = dg_lane.shape[1]

    def body(x_ref, o_ref):
        r = lax.broadcasted_iota(jnp.int32, (B_WIDTH, B_WIDTH), 0) % HEAD_DIM
        c = lax.broadcasted_iota(jnp.int32, (B_WIDTH, B_WIDTH), 1) % HEAD_DIM
        fold = jnp.where(r == c, 1.0, 0.0).astype(F32)
        for jg in range(n // B_WIDTH):
            ls = slice(jg * B_WIDTH, (jg + 1) * B_WIDTH)
            o_ref[:, ls] = _dot_hi(jnp.broadcast_to(x_ref[:, ls], (8, B_WIDTH)), fold)

    return pl.pallas_call(body, out_shape=SDS((8, n), F32), name="fold_heads", compiler_params=_params())(dg_lane)


CD_TILE = 256
CD_IN = 2 * C_WIDTH + 3 * 512


def _cd_split(pv):
    w = C_WIDTH
    return pv[:, :w], pv[:, w:2 * w], pv[:, 2 * w:3 * w], pv[:, 3 * w:4 * w], pv[:, 4 * w:5 * w]


def _shifted_copies(src, dst, rows):
    dst[0, :rows] = src[...]
    for b in range(1, 8):
        dst[b, :rows - 8] = src[pl.ds(b, rows - 8), :]


def _rows_from(shifted, start, n):
    b = start % 8
    return shifted[b, pl.ds(start - b, n), :]


def _mixer_cd_fwd(proj, cw, cb, lg, lb, dw):
    t = proj.shape[0]
    per = CD_TILE // HALO

    def body(h_ref, m_ref, cw_ref, cb_ref, lg_ref, lb_ref, dw_ref, o_ref, c1_ref, c_scr, e_scr, c_sh):
        not_first = (pl.program_id(0) > 0).astype(F32)
        ha, hg, _, hgc, hhv = _cd_split(h_ref[...].astype(F32))
        ma, mg, mgb, mgc, mhv = _cd_split(m_ref[...].astype(F32))
        c_scr[:HALO] = ha * _sigmoid(hg) * not_first
        c_scr[HALO:] = ma * _sigmoid(mg)
        e_scr[:HALO] = hgc * hhv * not_first
        e_scr[HALO:] = mgc * mhv
        _shifted_copies(c_scr, c_sh, HALO + CD_TILE)
        acc = jnp.zeros((CD_TILE, C_WIDTH), F32)
        for k in range(C_KERNEL):
            acc = acc + cw_ref[k:k + 1, :] * _rows_from(c_sh, HALO - (C_KERNEL - 1) + k, CD_TILE)
        c1 = acc + cb_ref[...]
        c1_ref[...] = c1
        cc = c1 - jnp.mean(c1, axis=-1, keepdims=True)
        c2 = cc * lax.rsqrt(jnp.mean(cc * cc, axis=-1, keepdims=True) + EPS) * lg_ref[...] + lb_ref[...]
        o_ref[:, :C_WIDTH] = (c2 * _sigmoid(c2)).astype(BF16)
        d1 = jnp.zeros((CD_TILE, C_WIDTH), F32)
        for k in range(D_KERNEL):
            d1 = d1 + dw_ref[k:k + 1, :] * e_scr[pl.ds(HALO - (D_KERNEL - 1) + k, CD_TILE), :]
        o_ref[:, C_WIDTH:] = (mgb * d1).astype(BF16)

    return pl.pallas_call(
        body, grid=(t // CD_TILE,),
        in_specs=[pl.BlockSpec((HALO, CD_IN), lambda i: (jnp.maximum(i * per - 1, 0), 0)), _row_spec(CD_TILE, CD_IN),
                  _const_spec((32, C_WIDTH)), _const_spec((1, C_WIDTH)), _const_spec((1, C_WIDTH)), _const_spec((1, C_WIDTH)),
                  _const_spec((8, C_WIDTH))],
        out_specs=[_row_spec(CD_TILE, 2 * C_WIDTH), _row_spec(CD_TILE, C_WIDTH)],
        out_shape=[SDS((t, 2 * C_WIDTH), BF16), SDS((t, C_WIDTH), F32)],
        scratch_shapes=[pltpu.VMEM((HALO + CD_TILE, C_WIDTH), F32)] * 2 + [pltpu.VMEM((8, HALO + CD_TILE, C_WIDTH), F32)],
        name="mixer_cd_fwd", compiler_params=_params())(proj, proj, cw, cb, lg, lb, dw)


def _mixer_cd_bwd(proj, dcat, c1, cw, lg, lb, dw, ride=None):
    t = proj.shape[0]
    per = CD_TILE // HALO
    nt = t // CD_TILE
    ext = CD_TILE + HALO

    def body(hp_ref, m_ref, hn_ref, dm_ref, dn_ref, c1m_ref, c1n_ref, cw_ref, lg_ref, lb_ref, dw_ref,
             dp_ref, dcw_ref, dcb_ref, dlg_ref, dlb_ref, ddw_ref, c_scr, e_scr, dc1_scr, dd1_scr, c_sh, dc1_sh):
        i = pl.program_id(0)

        @pl.when(i == 0)
        def _():
            for r in (dcw_ref, dcb_ref, dlg_ref, dlb_ref, ddw_ref):
                r[...] = jnp.zeros_like(r)

        not_first = (i > 0).astype(F32)
        not_last = (i < nt - 1).astype(F32)
        pa, pg, _, pgc, phv = _cd_split(hp_ref[...].astype(F32))
        ma, mg, mgb, mgc, mhv = _cd_split(m_ref[...].astype(F32))
        na, ng, ngb, ngc, nhv = _cd_split(hn_ref[...].astype(F32))
        sig_m = _sigmoid(mg)
        c_scr[:HALO] = pa * _sigmoid(pg) * not_first
        c_scr[HALO:HALO + CD_TILE] = ma * sig_m
        c_scr[HALO + CD_TILE:] = na * _sigmoid(ng) * not_last
        e_scr[:HALO] = pgc * phv * not_first
        e_scr[HALO:HALO + CD_TILE] = mgc * mhv
        e_scr[HALO + CD_TILE:] = ngc * nhv * not_last

        _shifted_copies(c_scr, c_sh, 2 * HALO + CD_TILE)
        c1 = jnp.concatenate([c1m_ref[...], c1n_ref[...]], axis=0)
        cc = c1 - jnp.mean(c1, axis=-1, keepdims=True)
        rs = lax.rsqrt(jnp.mean(cc * cc, axis=-1, keepdims=True) + EPS)
        vhat = cc * rs
        c2 = vhat * lg_ref[...] + lb_ref[...]
        sig = _sigmoid(c2)
        dc = jnp.concatenate([dm_ref[:, :C_WIDTH], dn_ref[:, :C_WIDTH] * not_last], axis=0)
        dc2 = dc * (sig * (1.0 + c2 * (1.0 - sig)))
        dvh = dc2 * lg_ref[...]
        dc1 = rs * (dvh - jnp.mean(dvh, axis=-1, keepdims=True) - vhat * jnp.mean(dvh * vhat, axis=-1, keepdims=True))
        dc1_scr[...] = dc1
        _shifted_copies(dc1_scr, dc1_sh, ext)
        dlg_ref[...] += jnp.sum((dc2 * vhat)[:CD_TILE], axis=0, keepdims=True)
        dlb_ref[...] += jnp.sum(dc2[:CD_TILE], axis=0, keepdims=True)
        dc1_m = dc1[:CD_TILE]
        dcb_ref[...] += jnp.sum(dc1_m, axis=0, keepdims=True)
        dc0 = jnp.zeros((CD_TILE, C_WIDTH), F32)
        for k in range(C_KERNEL):
            dc0 = dc0 + cw_ref[k:k + 1, :] * _rows_from(dc1_sh, C_KERNEL - 1 - k, CD_TILE)
            dcw_ref[k:k + 1, :] += jnp.sum(dc1_m * _rows_from(c_sh, HALO - (C_KERNEL - 1) + k, CD_TILE), axis=0, keepdims=True)
        dp_ref[:, :C_WIDTH] = (dc0 * sig_m).astype(BF16)
        dp_ref[:, C_WIDTH:2 * C_WIDTH] = (dc0 * ma * sig_m * (1.0 - sig_m)).astype(BF16)

        d1 = jnp.zeros((CD_TILE, C_WIDTH), F32)
        for k in range(D_KERNEL):
            d1 = d1 + dw_ref[k:k + 1, :] * e_scr[pl.ds(HALO - (D_KERNEL - 1) + k, CD_TILE), :]
        dd_m = dm_ref[:, C_WIDTH:]
        dd1 = jnp.concatenate([dd_m * mgb, dn_ref[:, C_WIDTH:] * ngb * not_last], axis=0)
        dd1_scr[...] = dd1
        dp_ref[:, 2 * C_WIDTH:3 * C_WIDTH] = (dd_m * d1).astype(BF16)
        de = jnp.zeros((CD_TILE, C_WIDTH), F32)
        for k in range(D_KERNEL):
            de = de + dw_ref[k:k + 1, :] * dd1_scr[pl.ds(D_KERNEL - 1 - k, CD_TILE), :]
            ddw_ref[k:k + 1, :] += jnp.sum(dd1[:CD_TILE] * e_scr[pl.ds(HALO - (D_KERNEL - 1) + k, CD_TILE), :], axis=0, keepdims=True)
        dp_ref[:, 3 * C_WIDTH:4 * C_WIDTH] = (de * mhv).astype(BF16)
        dp_ref[:, 4 * C_WIDTH:] = (de * mgc).astype(BF16)

    halo_prev = lambda i: (jnp.maximum(i * per - 1, 0), 0)
    halo_next = lambda i: (jnp.minimum((i + 1) * per, t // HALO - 1), 0)
    vec = _const_spec((1, C_WIDTH))
    return _call(
        body, grid=(nt,),
        in_specs=[pl.BlockSpec((HALO, CD_IN), halo_prev), _row_spec(CD_TILE, CD_IN), pl.BlockSpec((HALO, CD_IN), halo_next),
                  _row_spec(CD_TILE, 2 * C_WIDTH), pl.BlockSpec((HALO, 2 * C_WIDTH), halo_next),
                  _row_spec(CD_TILE, C_WIDTH), pl.BlockSpec((HALO, C_WIDTH), halo_next),
                  _const_spec((32, C_WIDTH)), vec, vec, _const_spec((8, C_WIDTH))],
        out_specs=[_row_spec(CD_TILE, CD_IN), _const_spec((32, C_WIDTH)), vec, vec, vec, _const_spec((8, C_WIDTH))],
        out_shape=[SDS((t, CD_IN), BF16), SDS((32, C_WIDTH), F32), SDS((1, C_WIDTH), F32), SDS((1, C_WIDTH), F32),
                   SDS((1, C_WIDTH), F32), SDS((8, C_WIDTH), F32)],
        scratch_shapes=[pltpu.VMEM((2 * HALO + CD_TILE, C_WIDTH), F32)] * 2 + [pltpu.VMEM((ext, C_WIDTH), F32)] * 2
        + [pltpu.VMEM((8, 2 * HALO + CD_TILE, C_WIDTH), F32), pltpu.VMEM((8, ext, C_WIDTH), F32)],
        operands=[proj, proj, proj, dcat, dcat, c1, c1, cw, lg, lb, dw], name="mixer_cd_bwd", ride=ride)


def _wgrad(name, pairs, out_rc, t, ride):
    tk = TILES["wgrad"]
    r, c = out_rc
    n = len(pairs)

    def body(*refs):
        ab, out_refs = refs[:2 * n], refs[2 * n:]
        k = pl.program_id(1)
        parts = [_dot(ab[2 * j][...], ab[2 * j + 1][...], TN) for j in range(n)]

        @pl.when(k == 0)
        def _():
            for a in range(n):
                out_refs[a][...] = parts[a]

        @pl.when(k > 0)
        def _():
            for a in range(n):
                out_refs[a][...] += parts[a]

    operands, in_specs = [], []
    for lhs, lhs_spec, rhs, rhs_spec in pairs:
        operands += [lhs, rhs]
        in_specs += [lhs_spec, rhs_spec]
    res = _call(body, grid=(N_CHIPS, t // tk), in_specs=in_specs,
                out_specs=[pl.BlockSpec((None, r, c), lambda p, k: (p, 0, 0))] * n,
                out_shape=[SDS((N_CHIPS, r, c), F32)] * n, operands=operands, name=name, ride=ride)
    outs, ride_res = (res, None) if ride is None else res
    outs = [o.reshape(N_CHIPS, 2, r // 2, c) for o in outs]
    return outs if ride is None else (outs, ride_res)


def _wgrad_col_sharded(name, h, dz_list, three_d, ride=None):
    t, d = h.shape
    tk = TILES["wgrad"]
    n4 = dz_list[0].shape[-1] if three_d else dz_list[0].shape[-1] // N_CHIPS
    hs = pl.BlockSpec((tk, d), lambda p, k: (k, 0))
    zs = pl.BlockSpec((None, tk, n4), lambda p, k: (p, k, 0)) if three_d else pl.BlockSpec((tk, n4), lambda p, k: (k, p))
    return _wgrad(name, [(h, hs, dz, zs) for dz in dz_list], (d, n4), t, ride)


def _wgrad_row_sharded(name, a, g, three_d, ride=None):
    many = isinstance(a, (list, tuple))
    a_list = list(a) if many else [a]
    t, d = g.shape
    tk = TILES["wgrad"]
    k4 = a_list[0].shape[-1] if three_d else a_list[0].shape[-1] // N_CHIPS
    a_spec = pl.BlockSpec((None, tk, k4), lambda p, k: (p, k, 0)) if three_d else pl.BlockSpec((tk, k4), lambda p, k: (k, p))
    gs = pl.BlockSpec((tk, d), lambda p, k: (k, 0))
    res = _wgrad(name, [(a_j, a_spec, g, gs) for a_j in a_list], (k4, d), t, ride)
    if many:
        return res
    return res[0] if ride is None else (res[0][0], res[1])


def _mesh_scalars():
    return jnp.stack([lax.axis_index("c"), 2 * lax.axis_index("x") + lax.axis_index("y")]).astype(jnp.int32)


def _stage_own(name, w, layer, dtype):
    layers, r, cols = w.shape
    h = r // 2

    def body(s_ref, x_ref, o_ref):
        o_ref[...] = x_ref[...].astype(dtype)

    return pl.pallas_call(
        body,
        grid_spec=pltpu.PrefetchScalarGridSpec(
            num_scalar_prefetch=1, grid=(2,),
            in_specs=[pl.BlockSpec((None, h, cols), lambda i, s: (2 * layer + i, 0, 0))],
            out_specs=pl.BlockSpec((None, None, h, cols), lambda i, s: (s[1], i, 0, 0))),
        out_shape=SDS((N_CHIPS, 2, h, cols), dtype), name=name,
        compiler_params=_params())(_mesh_scalars(), w.reshape(2 * layers, h, cols))


def _remote(src, dst, send_sem, recv_sem, device):
    return pltpu.make_async_remote_copy(src, dst, send_sem, recv_sem, device_id=device, device_id_type=MESH)


def _ride_gather_send(bufs):
    n = len(bufs)

    def each(b, sems, act):
        send, recv = sems
        x, y, c, p, others = _position()
        for t in range(n):
            for j, (qx, qy) in enumerate(others):
                act(b[t].at[p, c], b[t].at[2 * qx + qy, c], send.at[t, j], recv.at[t, j], (qx, qy, c))

    def start(ins, b, new, sems):
        each(b, sems, lambda mine, landed, s, r, dev: _remote(mine, mine, s, r, dev).start())

    def finish(ins, b, new, sems):
        def act(mine, landed, s, r, dev):
            _remote(mine, mine, s, r, dev).wait_send()
            _remote(landed, landed, s, r, dev).wait_recv()
        each(b, sems, act)

    return _Ride([], bufs, [], [(n, 3), (n, 3)], start, finish)


def _ride_gather_pass(bufs):
    n = len(bufs)

    def each(b, sems, act):
        send, recv = sems
        x, y, c, p, others = _position()
        for t in range(n):
            for j, (qx, qy) in enumerate(others):
                act(b[t].at[2 * qx + qy, c], b[t].at[2 * qx + qy, 1 - c], send.at[t, j], recv.at[t, j], (x, y, 1 - c))

    def start(ins, b, new, sems):
        each(b, sems, lambda landed, passed, s, r, dev: _remote(landed, landed, s, r, dev).start())

    def finish(ins, b, new, sems):
        def act(landed, passed, s, r, dev):
            _remote(landed, landed, s, r, dev).wait_send()
            _remote(passed, passed, s, r, dev).wait_recv()
        each(b, sems, act)

    return _Ride([], bufs, [], [(n, 3), (n, 3)], start, finish)


def _ride_swap(tensors):
    n = len(tensors)

    def each(ins, new, sems, act):
        send, recv = sems
        x, y, c, _, _ = _position()
        for t in range(n):
            act(_remote(ins[t].at[:, 1 - c], new[t], send.at[t], recv.at[t], (x, y, 1 - c)))

    def start(ins, b, new, sems):
        each(ins, new, sems, lambda cp: cp.start())

    def finish(ins, b, new, sems):
        each(ins, new, sems, lambda cp: cp.wait())

    return _Ride(tensors, [], [SDS((s.shape[0],) + s.shape[2:], s.dtype) for s in tensors], [(n,), (n,)], start, finish)


def _ride_scatter(tensors, landing):
    n = len(tensors)

    def each(ins, b, sems, act):
        send, recv = sems
        x, y, c, p, others = _position()
        for t in range(n):
            for j, (qx, qy) in enumerate(others):
                q = 2 * qx + qy
                act(ins[t].at[q], b[t].at[p], b[t].at[q], send.at[t, j], recv.at[t, j], (qx, qy, c))

    def start(ins, b, new, sems):
        each(ins, b, sems, lambda src, dst, landed, s, r, dev: _remote(src, dst, s, r, dev).start())

    def finish(ins, b, new, sems):
        def act(src, dst, landed, s, r, dev):
            _remote(src, dst, s, r, dev).wait_send()
            _remote(landed, landed, s, r, dev).wait_recv()
        each(ins, b, sems, act)

    return _Ride(tensors, landing, [], [(n, 3), (n, 3)], start, finish)


def _ride_join(bufs):
    n = len(bufs)

    def each(b, sems, act):
        send, recv = sems
        x, y, c, _, _ = _position()
        for t in range(n):
            act(b[t].at[c], b[t].at[1 - c], send.at[t], recv.at[t], (x, y, 1 - c))

    def start(ins, b, new, sems):
        each(b, sems, lambda mine, theirs, s, r, dev: _remote(mine, mine, s, r, dev).start())

    def finish(ins, b, new, sems):
        def act(mine, theirs, s, r, dev):
            _remote(mine, mine, s, r, dev).wait_send()
            _remote(theirs, theirs, s, r, dev).wait_recv()
        each(b, sems, act)

    return _Ride([], bufs, [], [(n,), (n,)], start, finish)


def _all_reduce_small(pack):
    rows = pack.shape[0]
    n_dev = 2 * N_CHIPS

    def body(x_ref, o_ref, land, send, recv):
        x, y, c, p, _ = _position()
        me = 2 * p + c
        land[me] = x_ref[...]
        peers = [(dx, dy, dc) for dx in range(2) for dy in range(2) for dc in range(2) if (dx, dy, dc) != (0, 0, 0)]
        for j, (dx, dy, dc) in enumerate(peers):
            _remote(land.at[me], land.at[me], send.at[j], recv.at[j], (x ^ dx, y ^ dy, c ^ dc)).start()
        for j, (dx, dy, dc) in enumerate(peers):
            src = 4 * (x ^ dx) + 2 * (y ^ dy) + (c ^ dc)
            _remote(land.at[me], land.at[me], send.at[j], recv.at[j], (x ^ dx, y ^ dy, c ^ dc)).wait_send()
            _remote(land.at[src], land.at[src], send.at[j], recv.at[j], (x ^ dx, y ^ dy, c ^ dc)).wait_recv()
        acc = land[0]
        for dev in range(1, n_dev):
            acc = acc + land[dev]
        o_ref[...] = acc

    return pl.pallas_call(
        body, out_shape=SDS((rows, LANES), F32),
        scratch_shapes=[pltpu.VMEM((n_dev, rows, LANES), F32), pltpu.SemaphoreType.DMA((n_dev - 1,)),
                        pltpu.SemaphoreType.DMA((n_dev - 1,))],
        name="all_reduce_small", compiler_params=_params())(pack)


def _add_own_half(name, full, recv, out_dtype):
    n4, _, h, cols = full.shape

    def body(s_ref, a_ref, b_ref, o_ref, own_ref):
        v = (a_ref[...] + b_ref[...]).astype(out_dtype)
        o_ref[...] = v

        @pl.when(pl.program_id(0) == s_ref[1])
        def _():
            own_ref[...] = v

    return pl.pallas_call(
        body,
        grid_spec=pltpu.PrefetchScalarGridSpec(
            num_scalar_prefetch=1, grid=(n4,),
            in_specs=[pl.BlockSpec((None, None, h, cols), lambda q, s: (q, s[0], 0, 0)),
                      pl.BlockSpec((None, h, cols), lambda q, s: (q, 0, 0))],
            out_specs=[pl.BlockSpec((None, h, cols), lambda q, s: (q, 0, 0)),
                       pl.BlockSpec((None, h, cols), lambda q, s: (s[1], 0, 0))]),
        out_shape=[SDS((n4, h, cols), out_dtype)] * 2, name=name, compiler_params=_params())(_mesh_scalars(), full, recv)


def _sum_chips(name, parts):
    n4, h, cols = parts.shape
    th = h // 4 if h % 64 == 0 else h

    def body(s_ref, a_ref, o_ref):
        acc = a_ref[0].astype(F32)
        for q in range(1, n4):
            acc = acc + a_ref[q].astype(F32)
        o_ref[...] = acc

    return pl.pallas_call(
        body,
        grid_spec=pltpu.PrefetchScalarGridSpec(
            num_scalar_prefetch=1, grid=(h // th,),
            in_specs=[pl.BlockSpec((n4, th, cols), lambda i, s: (0, i, 0))],
            out_specs=pl.BlockSpec((None, th, cols), lambda i, s: (s[0], i, 0))),
        out_shape=SDS((2, h, cols), F32), name=name, compiler_params=_params())(_mesh_scalars(), parts)


def _adamw_math(w, g, m, v):
    m2 = ADAM_B1 * m + (1.0 - ADAM_B1) * g
    v2 = ADAM_B2 * v + (1.0 - ADAM_B2) * (g * g)
    m_hat = m2 / (1.0 - ADAM_B1 ** ADAM_STEP)
    v_hat = v2 / (1.0 - ADAM_B2 ** ADAM_STEP)
    delta = -ADAM_LR * (m_hat / (jnp.sqrt(v_hat) + ADAM_EPS) + ADAM_WD * w)
    return delta, m2, v2


def _row_tile(rows, cols):
    cap = max(8, (1 << 18) // cols)
    best = 8
    for cand in range(8, min(rows, cap) + 1, 8):
        if rows % cand == 0:
            best = cand
    return best


def _adamw_big(name, w, g_layers, m, v):
    layers, rows, cols = w.shape
    tr = _row_tile(rows, cols)

    def body(w_ref, m_ref, v_ref, *rest):
        g_refs, (g_o, d_o, m_o, v_o) = rest[:layers], rest[layers:]
        gv = g_refs[0][...]
        for layer in range(1, layers):
            gv = jnp.where(pl.program_id(0) == layer, g_refs[layer][...], gv)
        d, mm, vv = _adamw_math(w_ref[...], gv, m_ref[...], v_ref[...])
        g_o[...] = gv
        d_o[...] = d
        m_o[...] = mm
        v_o[...] = vv

    blk = pl.BlockSpec((None, tr, cols), lambda l, i: (l, i, 0))
    g_blk = pl.BlockSpec((tr, cols), lambda l, i: (i, 0))
    return tuple(pl.pallas_call(
        body, grid=(layers, rows // tr), in_specs=[blk] * 3 + [g_blk] * layers, out_specs=[blk] * 4,
        out_shape=[SDS((layers, rows, cols), F32)] * 4, name=name,
        compiler_params=_params())(w, m, v, *[g.reshape(rows, cols) for g in g_layers]))


def _adamw_small(ws, gs, ms, vs):
    n = len(ws)
    flat = []
    for group in (ws, gs, ms, vs):
        flat += [a.reshape(-1, a.shape[-1]) for a in group]

    def body(*refs):
        w_r, g_r, m_r, v_r = refs[:n], refs[n:2 * n], refs[2 * n:3 * n], refs[3 * n:4 * n]
        d_o, m_o, v_o = refs[4 * n:5 * n], refs[5 * n:6 * n], refs[6 * n:7 * n]
        for j in range(n):
            d, mm, vv = _adamw_math(w_r[j][...], g_r[j][...], m_r[j][...], v_r[j][...])
            d_o[j][...] = d
            m_o[j][...] = mm
            v_o[j][...] = vv

    shapes = [SDS(a.shape, F32) for a in flat[:n]]
    outs = pl.pallas_call(body, out_shape=shapes * 3, name="adamw_small", compiler_params=_params())(*flat)
    res = []
    for k in range(3):
        res.append([outs[k * n + j].reshape(ws[j].shape) for j in range(n)])
    return res


BIG = ("ab_w_in", "ab_w_out", "cd_w_in", "cd_w_out", "ffn_w_gate", "ffn_w_up", "ffn_w_down")
V_BLOCK = (2 * A_WIDTH + QK_COLS) // B_WIDTH


def _pad_rows(a, rows):
    return jnp.pad(a, ((0, rows - a.shape[0]), (0, 0)))


A_IN, A_OUT, C_IN, C_OUT = ("ab_w_in", 0), ("ab_w_out", 0), ("cd_w_in", 0), ("cd_w_out", 0)
G0, U0, D0 = ("ffn_w_gate", 0), ("ffn_w_up", 0), ("ffn_w_down", 0)
G1, U1, D1 = ("ffn_w_gate", 1), ("ffn_w_up", 1), ("ffn_w_down", 1)
UNITS = (A_IN, A_OUT, G0, U0, D0, C_IN, C_OUT, G1, U1, D1)
ROWS_MINOR = ("ffn_w_gate", "ffn_w_up")
SMALL_SHARDED = ("small", 0)


class _Exchange:
    def __init__(self, enabled):
        self.enabled = enabled
        self.w, self.grad, self.recv, self.half, self.land, self.done = {}, {}, {}, {}, {}, {}

    def full(self, unit):
        b = self.w[unit]
        return b.reshape(N_CHIPS, 1, 2 * b.shape[2], b.shape[3])

    def _ride(self, phases):
        rides, sinks = [], []
        for kind, units in phases:
            if kind == "send":
                rides.append(_ride_gather_send([self.w[u] for u in units]))
                sinks.append(self.w)
            elif kind == "pass":
                rides.append(_ride_gather_pass([self.w[u] for u in units]))
                sinks.append(self.w)
            elif kind == "swap":
                rides.append(_ride_swap([self.grad[u] for u in units]))
                sinks.append(self.recv)
            elif kind == "scatter":
                rides.append(_ride_scatter([self.half[u] for u in units], [self.land[u] for u in units]))
                sinks.append(self.land)
            else:
                rides.append(_ride_join([self.done[u] for u in units]))
                sinks.append(self.done)
        ride = functools.reduce(_ride_both, rides)

        def settle(res):
            n_bufs = sum(len(r.bufs) for r in rides)
            bufs, new = list(res[:n_bufs]), list(res[n_bufs:])
            for r, sink, (_, units) in zip(rides, sinks, phases):
                vals = [bufs.pop(0) for _ in r.bufs] + [new.pop(0) for _ in r.new_outs]
                for u, v in zip(units, vals):
                    sink[u] = v

        return ride, settle

    def run(self, fn, *args, phases=(), **kw):
        if not self.enabled or not phases:
            return fn(*args, **kw)
        ride, settle = self._ride(phases)
        out, res = fn(*args, ride=ride, **kw)
        settle(res)
        return out

    def alone(self, name, phases):
        if self.enabled:
            ride, settle = self._ride(phases)
            settle(_run_ride(name, ride))

    def pair_sum(self, units):
        if self.enabled:
            for u in units:
                dtype = F32 if u == SMALL_SHARDED else BF16
                self.half[u], self.land[u] = _add_own_half(f"pair_sum_{u[0]}_{u[1]}", self.grad[u], self.recv[u], dtype)

    def chip_sum(self, units):
        if self.enabled:
            for u in units:
                self.done[u] = _sum_chips(f"chip_sum_{u[0]}_{u[1]}", self.land[u])


def _local_step(x, target, ex, sp):
    t, d = x.shape
    tabs = _rope_tables(t)
    gains = jnp.concatenate([jnp.tile(sp["q_norm_g"][g], HEAD_DIM // 8) for g in range(N_DIL)]
                            + [jnp.tile(sp["k_norm_g"][g], HEAD_DIM // 8) for g in range(N_DIL)]).reshape(1, QK_COLS)
    bias_t = sp["sgu_bias"].T
    cw = _pad_rows(sp["conv_c_w"], 32)
    dw = _pad_rows(sp["conv_d_w"], 8)
    cb, clg, clb = (sp[k].reshape(1, C_WIDTH) for k in ("conv_c_b", "c_ln_g", "c_ln_b"))
    slg, slb = sp["sgu_norm_g"].reshape(1, A_WIDTH), sp["sgu_norm_b"].reshape(1, A_WIDTH)
    g_ab, g_cd = sp["ab_norm_g"].reshape(1, d), sp["cd_norm_g"].reshape(1, d)
    g_f0, g_f1 = sp["ffn_norm_g"][0:1], sp["ffn_norm_g"][1:2]
    run = ex.run

    def w2d(unit):
        return ex.full(unit).reshape(-1, d)

    h0 = _rms_fwd("rms_ab", x, g_ab)
    proj = run(_proj_in, "proj_ab", h0, ex.full(A_IN), 0, phases=[("send", [A_OUT, G0])])
    a_out = _mixer_a_fwd(proj, slg, slb, sp["sgu_w"], bias_t)
    qk = run(_qk_fwd, proj, gains, tabs, phases=[("pass", [A_OUT, G0]), ("send", [U0])])
    fwd_phases = ([("pass", [U0]), ("send", [D0])], [("pass", [D0]), ("send", [C_IN])],
                  [("pass", [C_IN]), ("send", [C_OUT, G1])])
    qkv, o_list, l_list = [], [], []
    for g, rate in enumerate(DIL_RATES):
        if rate == 1:
            qk3, proj3 = qk.reshape(1, t, QK_COLS), proj.reshape(1, t, AB_IN)
            q, k, v = (qk3, g), (qk3, N_DIL + g), (proj3, V_BLOCK + g)
        else:
            qp, kp, vp = _permute(f"perm_fwd_{g}", [(qk, g), (qk, N_DIL + g), (proj, V_BLOCK + g)], rate)
            q, k, v = (qp, 0), (kp, 0), (vp, 0)
        qkv.append((q, k, v))
        o, l = run(_attn_fwd, f"attn_fwd_{g}", q, k, v, phases=fwd_phases[g])
        if rate == 1:
            o, l = o.reshape(t, B_WIDTH), l.reshape(t, B_WIDTH)
        else:
            o, l = _unpermute(f"unperm_fwd_{g}", [o, l], rate)
        o_list.append(o)
        l_list.append(l)
    cat, lse_tot = _attn_merge(a_out, o_list, l_list)
    x1, hf0 = _proj_out("out_ab", cat, w2d(A_OUT), x, g_next=g_f0)
    gate0, up0, act0 = run(_ffn_in, "ffn_in_0", hf0, ex.full(G0), ex.full(U0), 0,
                           phases=[("pass", [C_OUT, G1]), ("send", [U1])])
    x2, h1 = run(_ffn_out, "ffn_out_0", act0, ex.full(D0), 0, x1, g_next=g_cd, phases=[("pass", [U1]), ("send", [D1])])
    projcd = run(_proj_in, "proj_cd", h1, ex.full(C_IN), 0, phases=[("pass", [D1])])
    cat2, c1 = _mixer_cd_fwd(projcd, cw, cb, clg, clb, dw)
    x3, hf1 = _proj_out("out_cd", cat2, w2d(C_OUT), x2, g_next=g_f1)
    gate1, up1, act1 = _ffn_in("ffn_in_1", hf1, ex.full(G1), ex.full(U1), 0)
    dy, loss_acc, dy_b = _ffn_out("ffn_out_1", act1, ex.full(D1), 0, x3, target=target)
    loss = 0.5 * loss_acc[0, 0] / d

    late = [D1, G1, U1]
    dgate, dup = _ffn_dact("ffn_dact_1", dy_b, ex.full(D1), 0, gate1, up1)
    ex.grad[D1] = _wgrad_row_sharded("wgrad_down_1", act1, dy_b, True)
    ex.grad[G1], ex.grad[U1] = _wgrad_row_sharded("wgrad_gate_up_1", [dgate, dup], hf1, True)
    g3, d_f1, g3_b = run(_dgrad_cols, "dgrad_ffn_1", [dgate, dup], [ex.full(G1), ex.full(U1)], 0, True, x3, g_f1, dy,
                         w_rows=True, phases=[("swap", late)])
    ex.pair_sum(late)

    dcat2 = _dgrad_rows("dgrad_out_cd", g3_b, w2d(C_OUT))
    ex.grad[C_OUT] = _wgrad_row_sharded("wgrad_out_cd", cat2, g3_b, False)
    dprojcd, d_cw, d_cb, d_clg, d_clb, d_dw = run(_mixer_cd_bwd, projcd, dcat2, c1, cw, clg, clb, dw, phases=[("scatter", late)])
    ex.chip_sum(late)
    ex.grad[C_IN] = run(_wgrad_col_sharded, "wgrad_in_cd", h1, [dprojcd], False, phases=[("join", late)])[0]
    g2, d_cdn, g2_b = run(_dgrad_cols, "dgrad_in_cd", [dprojcd], [ex.full(C_IN)], 0, False, x2, g_cd, g3,
                          phases=[("swap", [C_OUT, C_IN])])
    ex.pair_sum([C_OUT, C_IN])

    dgate, dup = run(_ffn_dact, "ffn_dact_0", g2_b, ex.full(D0), 0, gate0, up0, phases=[("scatter", [C_OUT, C_IN])])
    ex.chip_sum([C_OUT, C_IN])
    ex.grad[D0] = run(_wgrad_row_sharded, "wgrad_down_0", act0, g2_b, True, phases=[("join", [C_OUT, C_IN])])
    ex.grad[G0], ex.grad[U0] = _wgrad_row_sharded("wgrad_gate_up_0", [dgate, dup], hf0, True)
    small = {"cd_norm_g": d_cdn, "conv_c_w": d_cw[:C_KERNEL], "conv_c_b": d_cb, "c_ln_g": d_clg, "c_ln_b": d_clb,
             "conv_d_w": d_dw[:D_KERNEL]}
    ex.grad[SMALL_SHARDED] = _split_full_small(small).reshape(N_CHIPS, 2, SHARDED_ROWS // 2, LANES)
    mid = [D0, G0, U0, SMALL_SHARDED]
    g1, d_f0, g1_b = run(_dgrad_cols, "dgrad_ffn_0", [dgate, dup], [ex.full(G0), ex.full(U0)], 0, True, x1, g_f0, g2,
                         w_rows=True, phases=[("swap", mid)])
    ex.pair_sum(mid)

    dcat = _dgrad_rows("dgrad_out_ab", g1_b, w2d(A_OUT))
    ex.grad[A_OUT] = _wgrad_row_sharded("wgrad_out_ab", cat, g1_b, False)
    d_a, d_sw, d_sbt, d_slg, d_slb = _mixer_a_bwd(proj, dcat, slg, slb, sp["sgu_w"], bias_t)
    dbb, dd = _attn_bwd_prep(dcat, cat)
    bwd_phases = ([("scatter", [D0, G0])], [("scatter", [U0, SMALL_SHARDED]), ("join", [D0, G0]), ("swap", [A_OUT])],
                  [("join", [U0, SMALL_SHARDED]), ("scatter", [A_OUT])])
    dqs, dks, dvs = [], [], []
    for g, rate in enumerate(DIL_RATES):
        q, k, v = qkv[g]
        if rate == 1:
            db3, l3, dd3 = (a.reshape(1, t, B_WIDTH) for a in (dbb, lse_tot, dd))
        else:
            db3, l3, dd3 = _permute(f"perm_bwd_{g}", [(dbb, 0), (lse_tot, 0), (dd, 0)], rate)
        if g == 1:
            ex.chip_sum([D0, G0])
        elif g == 2:
            ex.chip_sum([U0, SMALL_SHARDED])
            ex.pair_sum([A_OUT])
        dq, dk, dv = run(_attn_bwd, f"attn_bwd_{g}", q, k, v, db3, l3, dd3, phases=bwd_phases[g])
        if g == 2:
            ex.chip_sum([A_OUT])
        if rate == 1:
            dq, dk, dv = (a.reshape(t, B_WIDTH) for a in (dq, dk, dv))
        else:
            dq, dk, dv = _unpermute(f"unperm_bwd_{g}", [dq, dk, dv], rate)
        dqs.append(dq)
        dks.append(dk)
        dvs.append(dv)
    dproj, d_gains = _dproj_assemble(proj, d_a, dqs + dks, dvs, gains, tabs)
    d_gains = _fold_heads(d_gains)[0].reshape(2, N_DIL, B_WIDTH)[:, :, :HEAD_DIM]
    ex.grad[A_IN] = run(_wgrad_col_sharded, "wgrad_in_ab", h0, [dproj], False, phases=[("join", [A_OUT])])[0]
    ex.alone("swap_last", [("swap", [A_IN])])
    ex.pair_sum([A_IN])
    gx, d_abn = run(_dgrad_cols, "dgrad_in_ab", [dproj], [ex.full(A_IN)], 0, False, x, g_ab, g1, bf16_copy=False,
                    phases=[("scatter", [A_IN])])
    ex.chip_sum([A_IN])
    ex.alone("join_last", [("join", [A_IN])])

    small.update({
        "ab_norm_g": d_abn, "sgu_norm_g": d_slg, "sgu_norm_b": d_slb, "sgu_w": d_sw, "sgu_bias": d_sbt.T,
        "q_norm_g": d_gains[0], "k_norm_g": d_gains[1], "ffn_norm_g": jnp.concatenate([d_f0, d_f1], axis=0),
    })
    return loss, gx, small


SHARDED_SMALL = ("cd_norm_g", "conv_c_w", "conv_c_b", "c_ln_g", "c_ln_b", "conv_d_w")
SHARDED_ROWS = 48
REPLICATED_SMALL = ("ab_norm_g", "sgu_norm_g", "sgu_norm_b", "sgu_w", "sgu_bias", "q_norm_g", "k_norm_g", "ffn_norm_g")
REPLICATED_ROWS = 560


def _pack_sharded(parts):
    rows = [parts[k].reshape(-1, LANES) for k in SHARDED_SMALL]
    return _pad_rows(jnp.concatenate(rows, axis=0), SHARDED_ROWS)


def _split_full_small(small):
    per_chip = []
    for q in range(N_CHIPS):
        parts = {}
        for k in SHARDED_SMALL:
            a = small[k]
            a = a.reshape(-1, a.shape[-1])
            n = a.shape[-1] // N_CHIPS
            parts[k] = a[:, q * n:(q + 1) * n]
        per_chip.append(_pack_sharded(parts))
    return jnp.stack(per_chip)


def _unpack_sharded(pack, shapes):
    out, r = {}, 0
    for k in SHARDED_SMALL:
        n = math.prod(shapes[k]) // LANES
        out[k] = pack[r:r + n].reshape(shapes[k])
        r += n
    return out


def _gathered_small(packs, shapes):
    per_chip = [_unpack_sharded(packs[q], shapes) for q in range(N_CHIPS)]
    return {k: jnp.concatenate([pc[k] for pc in per_chip], axis=-1) for k in SHARDED_SMALL}


def _pack_replicated(small):
    rows = []
    for k in REPLICATED_SMALL:
        a = small[k].reshape(-1)
        a = jnp.pad(a, (0, (-a.shape[0]) % LANES))
        rows.append(a.reshape(-1, LANES))
    return _pad_rows(jnp.concatenate(rows, axis=0), REPLICATED_ROWS)


def _unpack_replicated(pack, shapes):
    out, r = {}, 0
    for k in REPLICATED_SMALL:
        size = math.prod(shapes[k])
        n = -(-size // LANES)
        out[k] = pack[r:r + n].reshape(-1)[:size].reshape(shapes[k])
        r += n
    return out


WEIGHT_ORDER = ("ab_norm_g", "ab_w_in", "sgu_norm_g", "sgu_norm_b", "sgu_w", "sgu_bias", "q_norm_g", "k_norm_g", "ab_w_out",
                "cd_norm_g", "cd_w_in", "conv_c_w", "conv_c_b", "c_ln_g", "c_ln_b", "conv_d_w", "cd_w_out", "ffn_norm_g",
                "ffn_w_gate", "ffn_w_up", "ffn_w_down")


def kernel(x, ab_norm_g, ab_w_in, sgu_norm_g, sgu_norm_b, sgu_w, sgu_bias, q_norm_g, k_norm_g, ab_w_out, cd_norm_g, cd_w_in, conv_c_w, conv_c_b, c_ln_g, c_ln_b, conv_d_w, cd_w_out, ffn_norm_g, ffn_w_gate, ffn_w_up, ffn_w_down, loss_target, m_ab_norm_g, m_ab_w_in, m_sgu_norm_g, m_sgu_norm_b, m_sgu_w, m_sgu_bias, m_q_norm_g, m_k_norm_g, m_ab_w_out, m_cd_norm_g, m_cd_w_in, m_conv_c_w, m_conv_c_b, m_c_ln_g, m_c_ln_b, m_conv_d_w, m_cd_w_out, m_ffn_norm_g, m_ffn_w_gate, m_ffn_w_up, m_ffn_w_down, v_ab_norm_g, v_ab_w_in, v_sgu_norm_g, v_sgu_norm_b, v_sgu_w, v_sgu_bias, v_q_norm_g, v_k_norm_g, v_ab_w_out, v_cd_norm_g, v_cd_w_in, v_conv_c_w, v_conv_c_b, v_c_ln_g, v_c_ln_b, v_conv_d_w, v_cd_w_out, v_ffn_norm_g, v_ffn_w_gate, v_ffn_w_up, v_ffn_w_down):
    args = dict(locals())
    ws = {k: args[k] for k in WEIGHT_ORDER}
    ms = {k: args["m_" + k] for k in WEIGHT_ORDER}
    vs = {k: args["v_" + k] for k in WEIGHT_ORDER}
    small_names = [k for k in WEIGHT_ORDER if k not in BIG]
    t, d = x.shape[1:]

    for group in (ws, ms, vs):
        for k in ROWS_MINOR:
            group[k] = jnp.swapaxes(group[k], 1, 2)
    ex = _Exchange(enabled=True)
    for name, layer in UNITS:
        ex.w[(name, layer)] = _stage_own(f"stage_{name}_{layer}", ws[name], layer, BF16)
    own_small = _pack_sharded({k: ws[k][0] for k in SHARDED_SMALL})
    ex.w[SMALL_SHARDED] = _stage_own("stage_small", own_small[None], 0, F32)
    ex.alone("gather_first", [("send", [A_IN, SMALL_SHARDED])])
    ex.alone("gather_first_pass", [("pass", [A_IN, SMALL_SHARDED])])
    sp = _gathered_small(ex.w[SMALL_SHARDED].reshape(N_CHIPS, SHARDED_ROWS, LANES), {k: ws[k].shape[1:] for k in SHARDED_SMALL})
    for k in REPLICATED_SMALL:
        sp[k] = ws[k] if k == "ffn_norm_g" else ws[k][0]

    loss, grad_x, g_small = _local_step(x.reshape(t, d), loss_target.reshape(t, d), ex, sp)

    grad = _unpack_sharded(ex.done[SMALL_SHARDED].reshape(SHARDED_ROWS, LANES), {k: ws[k].shape for k in SHARDED_SMALL})
    grad.update(_unpack_replicated(_all_reduce_small(_pack_replicated(g_small)), {k: ws[k].shape for k in REPLICATED_SMALL}))

    delta, new_m, new_v = {}, {}, {}
    for k in BIG:
        g_layers = [ex.done[(k, layer)] for layer in range(ws[k].shape[0])]
        outs = _adamw_big("adamw_" + k, ws[k], g_layers, ms[k], vs[k])
        if k in ROWS_MINOR:
            outs = [jnp.swapaxes(o, 1, 2) for o in outs]
        grad[k], delta[k], new_m[k], new_v[k] = outs
    d_s, m_s, v_s = _adamw_small([ws[k] for k in small_names], [grad[k] for k in small_names],
                                 [ms[k] for k in small_names], [vs[k] for k in small_names])
    for j, k in enumerate(small_names):
        delta[k], new_m[k], new_v[k] = d_s[j], m_s[j], v_s[j]

    loss = lax.psum(loss, ("x", "y", "c"))
    return (loss, grad_x[None], *[grad[k] for k in WEIGHT_ORDER], *[delta[k] for k in WEIGHT_ORDER],
            *[new_m[k] for k in WEIGHT_ORDER], *[new_v[k] for k in WEIGHT_ORDER])
```

```python
import functools
import math

import jax
import jax.numpy as jnp
from jax import lax
from jax.experimental import pallas as pl
from jax.experimental.pallas import tpu as pltpu

F32 = jnp.float32
BF16 = jnp.bfloat16
SDS = jax.ShapeDtypeStruct

N_CHIPS = 4
EPS = 1e-6
NEG_INF = -1e30
CHUNK = 128
A_GROUPS = 4
A_WIDTH = 512
N_DIL = 3
DIL_RATES = (1, 4, 16)
HEAD_DIM = 64
B_WIDTH = 512
ROPE_DIM = 16
ROPE_THETA = 500000.0
C_WIDTH = 512
C_KERNEL = 31
D_KERNEL = 3
HALO = 32
ATT_BLOCK = 128
LANES = 128

ADAM_LR = 0.001
ADAM_B1 = 0.9
ADAM_B2 = 0.999
ADAM_EPS = 1e-08
ADAM_WD = 0.01
ADAM_STEP = 10

VMEM_LIMIT = 56 * 1024 * 1024

NN = (((1,), (0,)), ((), ()))
NT = (((1,), (1,)), ((), ()))
TN = (((0,), (0,)), ((), ()))

TILES = {"proj_in": 1024, "proj_out": 1024, "ffn_in": 1024, "ffn_out": 512, "ffn_dact": 512, "dgrad_cols": 512,
         "dgrad_rows": 1024, "wgrad": 2048}


def _params(sem=None):
    return pltpu.CompilerParams(dimension_semantics=sem, vmem_limit_bytes=VMEM_LIMIT)


def _bf(v):
    return v if v.dtype == BF16 else v.astype(BF16)


def _dot(a, b, dims):
    return lax.dot_general(_bf(a), _bf(b), dims, preferred_element_type=F32)


def _dot_hi(a, b):
    return jnp.dot(a, b, precision=lax.Precision.HIGHEST, preferred_element_type=F32)


def _sigmoid(v):
    return 0.5 * jnp.tanh(0.5 * v) + 0.5


def _gelu(v):
    return 0.5 * v * (1.0 + lax.erf(v * (1.0 / math.sqrt(2.0))))


def _gelu_grad(v):
    cdf = 0.5 * (1.0 + lax.erf(v * (1.0 / math.sqrt(2.0))))
    return cdf + v * jnp.exp(-0.5 * v * v) * (1.0 / math.sqrt(2.0 * math.pi))


def _segment_mean_matrix(seg, scale=None):
    r = lax.broadcasted_iota(jnp.int32, (LANES, LANES), 0) // seg
    c = lax.broadcasted_iota(jnp.int32, (LANES, LANES), 1) // seg
    return jnp.where(r == c, (1.0 / seg) if scale is None else scale, 0.0).astype(BF16)


def _segment_dot(v, seg):
    hi = v.astype(BF16)
    lo = (v - hi.astype(F32)).astype(BF16)
    return jnp.dot(hi, seg, preferred_element_type=F32) + jnp.dot(lo, seg, preferred_element_type=F32)


MESH = pl.DeviceIdType.MESH
ANY = pl.BlockSpec(memory_space=pl.ANY)


def _position():
    x, y, c = lax.axis_index("x"), lax.axis_index("y"), lax.axis_index("c")
    others = [(1 - x, y), (x, 1 - y), (1 - x, 1 - y)]
    return x, y, c, 2 * x + y, others


class _Ride:
    def __init__(self, ins, bufs, new_outs, sem_shapes, start, finish):
        self.ins, self.bufs, self.new_outs, self.sem_shapes = list(ins), list(bufs), list(new_outs), list(sem_shapes)
        self.start, self.finish = start, finish


def _ride_both(a, b):
    na = (len(a.ins), len(a.bufs), len(a.new_outs), len(a.sem_shapes))

    def split(ins, bufs, new, sems):
        return ((ins[:na[0]], bufs[:na[1]], new[:na[2]], sems[:na[3]]), (ins[na[0]:], bufs[na[1]:], new[na[2]:], sems[na[3]:]))

    def start(*refs):
        ra, rb = split(*refs)
        a.start(*ra)
        b.start(*rb)

    def finish(*refs):
        ra, rb = split(*refs)
        a.finish(*ra)
        b.finish(*rb)

    return _Ride(a.ins + b.ins, a.bufs + b.bufs, a.new_outs + b.new_outs, a.sem_shapes + b.sem_shapes, start, finish)


def _call(body, *, grid, in_specs, out_specs, out_shape, operands, name, scratch_shapes=(), aliases=None, ride=None):
    if ride is None:
        return pl.pallas_call(body, grid=grid, in_specs=in_specs, out_specs=out_specs, out_shape=out_shape,
                              scratch_shapes=list(scratch_shapes), input_output_aliases=aliases or {}, name=name,
                              compiler_params=_params())(*operands)
    multi = isinstance(out_shape, (list, tuple))
    out_shapes = list(out_shape) if multi else [out_shape]
    o_specs = list(out_specs) if multi else [out_specs]
    n_in, n_out, n_scr = len(operands), len(out_shapes), len(scratch_shapes)
    n_ri, n_rb, n_rn = len(ride.ins), len(ride.bufs), len(ride.new_outs)

    def carrying(*refs):
        k = n_in
        r_ins = refs[k:k + n_ri]
        k += n_ri + n_rb
        outs = refs[k:k + n_out]
        k += n_out
        r_bufs = refs[k:k + n_rb]
        k += n_rb
        r_new = refs[k:k + n_rn]
        k += n_rn
        scratch = refs[k:k + n_scr]
        sems = refs[k + n_scr:]
        first, last = None, None
        for axis, size in enumerate(grid):
            pid = pl.program_id(axis)
            first = (pid == 0) if first is None else first & (pid == 0)
            last = (pid == size - 1) if last is None else last & (pid == size - 1)

        @pl.when(first)
        def _():
            ride.start(r_ins, r_bufs, r_new, sems)

        body(*refs[:n_in], *outs, *scratch)

        @pl.when(last)
        def _():
            ride.finish(r_ins, r_bufs, r_new, sems)

    all_aliases = dict(aliases or {})
    for j in range(n_rb):
        all_aliases[n_in + n_ri + j] = n_out + j
    res = pl.pallas_call(
        carrying, grid=grid, in_specs=list(in_specs) + [ANY] * (n_ri + n_rb), out_specs=o_specs + [ANY] * (n_rb + n_rn),
        out_shape=out_shapes + [SDS(b.shape, b.dtype) for b in ride.bufs] + ride.new_outs,
        scratch_shapes=list(scratch_shapes) + [pltpu.SemaphoreType.DMA(s) for s in ride.sem_shapes],
        input_output_aliases=all_aliases, name=name, compiler_params=_params())(*operands, *ride.ins, *ride.bufs)
    outs = res[:n_out]
    return (list(outs) if multi else outs[0]), list(res[n_out:])


def _run_ride(name, ride):
    n_ri, n_rb, n_rn = len(ride.ins), len(ride.bufs), len(ride.new_outs)

    def body(*refs):
        r_ins = refs[:n_ri]
        r_bufs = refs[n_ri + n_rb:n_ri + 2 * n_rb]
        r_new = refs[n_ri + 2 * n_rb:n_ri + 2 * n_rb + n_rn]
        sems = refs[n_ri + 2 * n_rb + n_rn:]
        ride.start(r_ins, r_bufs, r_new, sems)
        ride.finish(r_ins, r_bufs, r_new, sems)

    return list(pl.pallas_call(
        body, in_specs=[ANY] * (n_ri + n_rb), out_specs=[ANY] * (n_rb + n_rn),
        out_shape=[SDS(b.shape, b.dtype) for b in ride.bufs] + ride.new_outs,
        scratch_shapes=[pltpu.SemaphoreType.DMA(s) for s in ride.sem_shapes],
        input_output_aliases={n_ri + j: j for j in range(n_rb)}, name=name)(*ride.ins, *ride.bufs))


def _whole(ref, p):
    return ref[...]


def _slab(ref, p):
    return ref[p]


def _matmul(name, grid, pairs, extras, outs, dims, epi, *, slabs=1, n_acc=1, ride=None):
    n_pairs, n_ex, n_out = len(pairs), len(extras), len(outs)

    def body(*refs):
        ab = refs[:2 * n_pairs]
        ex = refs[2 * n_pairs:2 * n_pairs + n_ex]
        out_refs = refs[2 * n_pairs + n_ex:2 * n_pairs + n_ex + n_out]
        pids = tuple(pl.program_id(a) for a in range(len(grid)))
        parts = [None] * n_acc
        for p in range(slabs):
            for j, (_, _, a_pick, _, _, b_pick, acc) in enumerate(pairs):
                d = _dot(a_pick(ab[2 * j], p), b_pick(ab[2 * j + 1], p), dims)
                parts[acc] = d if parts[acc] is None else parts[acc] + d
        epi(parts, ex, out_refs, pids)

    operands, in_specs = [], []
    for a, a_spec, _, b, b_spec, _, _ in pairs:
        operands += [a, b]
        in_specs += [a_spec, b_spec]
    for e, e_spec in extras:
        operands.append(e)
        in_specs.append(e_spec)
    return _call(body, grid=grid, in_specs=in_specs, out_specs=[o[1] for o in outs], out_shape=[o[0] for o in outs],
                 operands=operands, name=name, ride=ride)


def _rms_rows(v, g):
    r = lax.rsqrt(jnp.mean(v * v, axis=-1, keepdims=True) + EPS)
    return v * r * g


def _rms_fwd(name, x, g):
    t, d = x.shape
    tm = 512

    def body(x_ref, g_ref, o_ref):
        o_ref[...] = _rms_rows(x_ref[...], g_ref[...]).astype(BF16)

    return pl.pallas_call(
        body, grid=(t // tm,),
        in_specs=[pl.BlockSpec((tm, d), lambda i: (i, 0)), pl.BlockSpec((1, d), lambda i: (0, 0))],
        out_specs=pl.BlockSpec((tm, d), lambda i: (i, 0)), out_shape=SDS((t, d), BF16), name=name,
        compiler_params=_params())(x, g)


def _epi_residual_norm(accs, ex, outs, pids):
    x_new = accs[0] + ex[0][...]
    outs[0][...] = x_new
    outs[1][...] = _rms_rows(x_new, ex[1][...]).astype(BF16)


def _epi_residual_loss(accs, ex, outs, pids):
    y = accs[0] + ex[0][...]
    err = y - ex[1][...]
    dy = err * (1.0 / err.shape[-1])
    outs[0][...] = dy
    outs[2][...] = dy.astype(BF16)

    @pl.when(pids[0] == 0)
    def _():
        outs[1][...] = jnp.zeros_like(outs[1])

    outs[1][...] += jnp.sum(err * err)


def _epi_rms_bwd(accs, ex, outs, pids):
    dh = accs[0]
    xv, g, res = ex[0][...], ex[1][...], ex[2][...]
    r = lax.rsqrt(jnp.mean(xv * xv, axis=-1, keepdims=True) + EPS)
    xh = xv * r
    dy = dh * g
    dx = res + r * (dy - xh * jnp.mean(dy * xh, axis=-1, keepdims=True))
    outs[0][...] = dx
    if len(outs) > 2:
        outs[2][...] = dx.astype(BF16)

    @pl.when(pids[0] == 0)
    def _():
        outs[1][...] = jnp.zeros_like(outs[1])

    outs[1][...] += jnp.sum(dh * xh, axis=0, keepdims=True)


def _row_spec(tm, d):
    return pl.BlockSpec((tm, d), lambda i, *_: (i, 0))


def _const_spec(shape):
    nd = len(shape)
    return pl.BlockSpec(shape, lambda *_: (0,) * nd)


def _proj_in(name, h, w, layer, ride=None):
    t, d = h.shape
    n4 = w.shape[-1]
    tm = TILES["proj_in"]

    def epi(accs, ex, outs, pids):
        outs[0][...] = accs[0].astype(BF16)

    res = _matmul(
        name, (N_CHIPS, t // tm),
        [(h, pl.BlockSpec((tm, d), lambda p, i: (i, 0)), _whole,
          w, pl.BlockSpec((None, None, d, n4), lambda p, i: (p, layer, 0, 0)), _whole, 0)],
        [], [(SDS((t, N_CHIPS * n4), BF16), pl.BlockSpec((tm, n4), lambda p, i: (i, p)))],
        NN, epi, ride=ride)
    return res[0] if ride is None else (res[0][0], res[1])


def _proj_out(name, a, w, x, g_next=None, target=None):
    t, k = a.shape
    d = w.shape[-1]
    tm = TILES["proj_out"]
    if target is None:
        extras = [(x, _row_spec(tm, d)), (g_next, _const_spec((1, d)))]
        outs = [(SDS((t, d), F32), _row_spec(tm, d)), (SDS((t, d), BF16), _row_spec(tm, d))]
        epi = _epi_residual_norm
    else:
        extras = [(x, _row_spec(tm, d)), (target, _row_spec(tm, d))]
        outs = [(SDS((t, d), F32), _row_spec(tm, d)), (SDS((8, LANES), F32), _const_spec((8, LANES))),
                (SDS((t, d), BF16), _row_spec(tm, d))]
        epi = _epi_residual_loss
    return _matmul(name, (t // tm,), [(a, _row_spec(tm, k), _whole, w, _const_spec((k, d)), _whole, 0)], extras, outs, NN, epi)


def _ffn_in(name, h, wg, wu, layer, ride=None):
    t, d = h.shape
    n4 = wg.shape[-2]
    tm = TILES["ffn_in"]

    def epi(accs, ex, outs, pids):
        gate, up = accs
        outs[0][...] = gate.astype(BF16)
        outs[1][...] = up.astype(BF16)
        outs[2][...] = (gate * _sigmoid(gate) * up).astype(BF16)

    w_spec = pl.BlockSpec((None, None, n4, d), lambda p, i: (p, layer, 0, 0))
    h_spec = pl.BlockSpec((tm, d), lambda p, i: (i, 0))
    o = (SDS((N_CHIPS, t, n4), BF16), pl.BlockSpec((None, tm, n4), lambda p, i: (p, i, 0)))
    return _matmul(name, (N_CHIPS, t // tm),
                   [(h, h_spec, _whole, wg, w_spec, _whole, 0), (h, h_spec, _whole, wu, w_spec, _whole, 1)], [],
                   [o, o, o], NT, epi, n_acc=2, ride=ride)


def _ffn_out(name, act, wd, layer, x, g_next=None, target=None, ride=None):
    _, t, n4 = act.shape
    d = wd.shape[-1]
    tm = TILES["ffn_out"]
    xs = _row_spec(tm, d)
    if target is None:
        extras = [(x, xs), (g_next, _const_spec((1, d)))]
        outs = [(SDS((t, d), F32), xs), (SDS((t, d), BF16), xs)]
        epi = _epi_residual_norm
    else:
        extras = [(x, xs), (target, xs)]
        outs = [(SDS((t, d), F32), xs), (SDS((8, LANES), F32), _const_spec((8, LANES))), (SDS((t, d), BF16), xs)]
        epi = _epi_residual_loss
    return _matmul(
        name, (t // tm,),
        [(act, pl.BlockSpec((N_CHIPS, tm, n4), lambda i: (0, i, 0)), _slab,
          wd, pl.BlockSpec((N_CHIPS, None, n4, d), lambda i: (0, layer, 0, 0)), _slab, 0)],
        extras, outs, NN, epi, slabs=N_CHIPS, ride=ride)


def _ffn_dact(name, g, wd, layer, gate, up, ride=None):
    t, d = g.shape
    n4 = wd.shape[-2]
    tm = TILES["ffn_dact"]

    def epi(accs, ex, outs, pids):
        dact = accs[0]
        gt = ex[0][...].astype(F32)
        upv = ex[1][...].astype(F32)
        s = _sigmoid(gt)
        silu = gt * s
        outs[0][...] = (dact * upv * (s + silu - silu * s)).astype(BF16)
        outs[1][...] = (dact * silu).astype(BF16)

    blk = pl.BlockSpec((None, tm, n4), lambda p, i: (p, i, 0))
    o = (SDS((N_CHIPS, t, n4), BF16), blk)
    return _matmul(
        name, (N_CHIPS, t // tm),
        [(g, pl.BlockSpec((tm, d), lambda p, i: (i, 0)), _whole,
          wd, pl.BlockSpec((None, None, n4, d), lambda p, i: (p, layer, 0, 0)), _whole, 0)],
        [(gate, blk), (up, blk)], [o, o], NT, epi, ride=ride)


def _copy_epi(accs, ex, outs, pids):
    for a, o in zip(accs, outs):
        o[...] = a.astype(o.dtype)


def _dgrad_cols(name, dz_list, w_list, layer, three_d, x, g, res, bf16_copy=True, w_rows=False, ride=None):
    t, d = x.shape
    n4 = w_list[0].shape[-2 if w_rows else -1]
    tm = TILES["dgrad_cols"]
    if three_d:
        zs, z_pick = pl.BlockSpec((N_CHIPS, tm, n4), lambda i: (0, i, 0)), _slab
    else:
        zs, z_pick = _row_spec(tm, N_CHIPS * n4), (lambda ref, p: ref[:, p * n4:(p + 1) * n4])
    ws = pl.BlockSpec((N_CHIPS, None) + ((n4, d) if w_rows else (d, n4)), lambda i: (0, layer, 0, 0))
    xs = _row_spec(tm, d)
    return _matmul(
        name, (t // tm,), [(dz, zs, z_pick, w, ws, _slab, 0) for dz, w in zip(dz_list, w_list)],
        [(x, xs), (g, _const_spec((1, d))), (res, xs)],
        [(SDS((t, d), F32), xs), (SDS((1, d), F32), _const_spec((1, d)))] + ([(SDS((t, d), BF16), xs)] if bf16_copy else []),
        NN if w_rows else NT, _epi_rms_bwd, slabs=N_CHIPS, ride=ride)


def _dgrad_rows(name, g, w):
    t, d = g.shape
    k = w.shape[0]
    tm = TILES["dgrad_rows"]
    return _matmul(name, (t // tm,), [(g, _row_spec(tm, d), _whole, w, _const_spec((k, d)), _whole, 0)], [],
                   [(SDS((t, k), F32), _row_spec(tm, k))], NT, _copy_epi)[0]


A_TILE = 256


def _a_common(p_ref, lg_ref, lb_ref):
    pv = p_ref[...].astype(F32)
    a = _gelu(pv)
    u, v = a[:, :A_WIDTH], a[:, A_WIDTH:]
    vc = v - jnp.mean(v, axis=-1, keepdims=True)
    rs = lax.rsqrt(jnp.mean(vc * vc, axis=-1, keepdims=True) + EPS)
    vhat = vc * rs
    vn = vhat * lg_ref[...] + lb_ref[...]
    return pv, u, vhat, rs, vn.astype(BF16)


def _tril_weights(w_ref, g):
    r = lax.broadcasted_iota(jnp.int32, (CHUNK, CHUNK), 0)
    c = lax.broadcasted_iota(jnp.int32, (CHUNK, CHUNK), 1)
    return jnp.where(c <= r, w_ref[g], 0.0).astype(BF16), c <= r


def _mixer_a_fwd(proj, lg, lb, w, bias_t):
    t = proj.shape[0]

    def body(p_ref, lg_ref, lb_ref, w_ref, bt_ref, o_ref):
        _, u, _, _, vnb = _a_common(p_ref, lg_ref, lb_ref)
        for g in range(A_GROUPS):
            wt, _ = _tril_weights(w_ref, g)
            cs = slice(g * CHUNK, (g + 1) * CHUNK)
            for ch in range(A_TILE // CHUNK):
                rs_ = slice(ch * CHUNK, (ch + 1) * CHUNK)
                mixed = _dot(wt, vnb[rs_, cs], NN) + bt_ref[:, g:g + 1]
                o_ref[rs_, cs] = (u[rs_, cs] * mixed).astype(BF16)

    return pl.pallas_call(
        body, grid=(t // A_TILE,),
        in_specs=[pl.BlockSpec((A_TILE, 2 * A_WIDTH), lambda i: (i, 0)), _const_spec((1, A_WIDTH)),
                  _const_spec((1, A_WIDTH)), _const_spec((A_GROUPS, CHUNK, CHUNK)), _const_spec((CHUNK, A_GROUPS))],
        out_specs=pl.BlockSpec((A_TILE, A_WIDTH), lambda i: (i, 0)), out_shape=SDS((t, A_WIDTH), BF16),
        name="mixer_a_fwd", compiler_params=_params())(proj, lg, lb, w, bias_t)


def _mixer_a_bwd(proj, dcat, lg, lb, w, bias_t):
    t = proj.shape[0]

    def body(p_ref, da_ref, lg_ref, lb_ref, w_ref, bt_ref, dp_ref, dw_ref, dbt_ref, dlg_ref, dlb_ref, du_scr, dvn_scr):
        @pl.when(pl.program_id(0) == 0)
        def _():
            dw_ref[...] = jnp.zeros_like(dw_ref)
            dbt_ref[...] = jnp.zeros_like(dbt_ref)
            dlg_ref[...] = jnp.zeros_like(dlg_ref)
            dlb_ref[...] = jnp.zeros_like(dlb_ref)

        pv, u, vhat, rs, vnb = _a_common(p_ref, lg_ref, lb_ref)
        da = da_ref[...]
        for g in range(A_GROUPS):
            wt, keep = _tril_weights(w_ref, g)
            cs = slice(g * CHUNK, (g + 1) * CHUNK)
            for ch in range(A_TILE // CHUNK):
                rs_ = slice(ch * CHUNK, (ch + 1) * CHUNK)
                vg = vnb[rs_, cs]
                mixed = _dot(wt, vg, NN) + bt_ref[:, g:g + 1]
                du_scr[rs_, cs] = da[rs_, cs] * mixed
                dmx = da[rs_, cs] * u[rs_, cs]
                dw_ref[g] += jnp.where(keep, _dot(dmx, vg, NT), 0.0)
                dvn_scr[rs_, cs] = _dot(wt, dmx, TN)
                dbt_ref[:, g:g + 1] += jnp.sum(dmx, axis=1, keepdims=True)
        dvn = dvn_scr[...]
        dlg_ref[...] += jnp.sum(dvn * vhat, axis=0, keepdims=True)
        dlb_ref[...] += jnp.sum(dvn, axis=0, keepdims=True)
        dvh = dvn * lg_ref[...]
        dv = rs * (dvh - jnp.mean(dvh, axis=-1, keepdims=True) - vhat * jnp.mean(dvh * vhat, axis=-1, keepdims=True))
        gp = _gelu_grad(pv)
        dp_ref[:, :A_WIDTH] = (du_scr[...] * gp[:, :A_WIDTH]).astype(BF16)
        dp_ref[:, A_WIDTH:] = (dv * gp[:, A_WIDTH:]).astype(BF16)

    return pl.pallas_call(
        body, grid=(t // A_TILE,),
        in_specs=[pl.BlockSpec((A_TILE, 2 * A_WIDTH), lambda i: (i, 0)), pl.BlockSpec((A_TILE, A_WIDTH), lambda i: (i, 0)),
                  _const_spec((1, A_WIDTH)), _const_spec((1, A_WIDTH)), _const_spec((A_GROUPS, CHUNK, CHUNK)),
                  _const_spec((CHUNK, A_GROUPS))],
        out_specs=[pl.BlockSpec((A_TILE, 2 * A_WIDTH), lambda i: (i, 0)), _const_spec((A_GROUPS, CHUNK, CHUNK)),
                   _const_spec((CHUNK, A_GROUPS)), _const_spec((1, A_WIDTH)), _const_spec((1, A_WIDTH))],
        out_shape=[SDS((t, 2 * A_WIDTH), BF16), SDS((A_GROUPS, CHUNK, CHUNK), F32), SDS((CHUNK, A_GROUPS), F32),
                   SDS((1, A_WIDTH), F32), SDS((1, A_WIDTH), F32)],
        scratch_shapes=[pltpu.VMEM((A_TILE, A_WIDTH), F32), pltpu.VMEM((A_TILE, A_WIDTH), F32)],
        name="mixer_a_bwd", compiler_params=_params())(proj, dcat, lg, lb, w, bias_t)


def _rope_tables(t):
    half = ROPE_DIM // 2
    inv_freq = ROPE_THETA ** (-jnp.arange(half, dtype=F32) * 2.0 / ROPE_DIM)
    ang = jnp.arange(t, dtype=F32)[:, None] * inv_freq[None, :]
    cos, sin = jnp.cos(ang), jnp.sin(ang)
    one = jnp.ones((t, HEAD_DIM - ROPE_DIM), F32)
    zero = jnp.zeros((t, HEAD_DIM - ROPE_DIM), F32)
    zh = jnp.zeros((t, half), F32)
    c = jnp.concatenate([cos, cos, one], axis=1)
    s1 = jnp.concatenate([-sin, zh, zero], axis=1)
    s2 = jnp.concatenate([zh, sin, zero], axis=1)
    return tuple(jnp.tile(a, (1, LANES // HEAD_DIM)) for a in (c, s1, s2))


QK_TILE = 512
QK_COLS = 2 * N_DIL * B_WIDTH


def _qk_fwd(proj, gains, tabs, ride=None):
    t = proj.shape[0]
    col0 = 2 * A_WIDTH // 1024

    def body(p_ref, g_ref, c_ref, s1_ref, s2_ref, o_ref):
        seg = _segment_mean_matrix(HEAD_DIM)
        c, s1, s2 = c_ref[...], s1_ref[...], s2_ref[...]
        for ci in range(1024 // LANES):
            ls = slice(ci * LANES, (ci + 1) * LANES)
            xv = p_ref[:, ls].astype(F32)
            r = lax.rsqrt(_segment_dot(xv * xv, seg) + EPS)
            y = xv * r * g_ref[:, ls]
            o_ref[:, ls] = (y * c + pltpu.roll(y, LANES - 8, axis=1) * s1 + pltpu.roll(y, 8, axis=1) * s2).astype(BF16)

    tab = pl.BlockSpec((QK_TILE, LANES), lambda i, j: (i, 0))
    return _call(
        body, grid=(t // QK_TILE, QK_COLS // 1024),
        in_specs=[pl.BlockSpec((QK_TILE, 1024), lambda i, j: (i, col0 + j)), pl.BlockSpec((1, 1024), lambda i, j: (0, j)),
                  tab, tab, tab],
        out_specs=pl.BlockSpec((QK_TILE, 1024), lambda i, j: (i, j)), out_shape=SDS((t, QK_COLS), BF16),
        operands=[proj, gains, *tabs], name="qk_norm_rope_fwd", ride=ride)


PERM_TILE = 512


def _permute(name, items, rate):
    t = items[0][0].shape[0]
    n = len(items)
    rows = PERM_TILE // rate

    def body(*refs):
        scr = refs[-1]
        for x_ref, o_ref in zip(refs[:n], refs[n:2 * n]):
            for ci in range(B_WIDTH // LANES):
                scr[ci] = x_ref[:, ci * LANES:(ci + 1) * LANES].astype(F32)
            for rho in range(rate):
                for ci in range(B_WIDTH // LANES):
                    o_ref[rho, :, ci * LANES:(ci + 1) * LANES] = scr[ci, pl.ds(rho, rows, stride=rate), :].astype(o_ref.dtype)

    return pl.pallas_call(
        body, grid=(t // PERM_TILE,),
        in_specs=[pl.BlockSpec((PERM_TILE, B_WIDTH), functools.partial(lambda cb, i: (i, cb), cb)) for _, cb in items],
        out_specs=[pl.BlockSpec((rate, rows, B_WIDTH), lambda i: (0, i, 0)) for _ in items],
        out_shape=[SDS((rate, t // rate, B_WIDTH), a.dtype) for a, _ in items],
        scratch_shapes=[pltpu.VMEM((B_WIDTH // LANES, PERM_TILE, LANES), F32)],
        name=name, compiler_params=_params())(*[a for a, _ in items])


def _unpermute(name, arrays, rate):
    t = arrays[0].shape[1] * rate
    n = len(arrays)
    rows = PERM_TILE // rate

    def body(*refs):
        scr = refs[-1]
        for x_ref, o_ref in zip(refs[:n], refs[n:2 * n]):
            for rho in range(rate):
                for ci in range(B_WIDTH // LANES):
                    scr[ci, pl.ds(rho, rows, stride=rate), :] = x_ref[rho, :, ci * LANES:(ci + 1) * LANES].astype(F32)
            for ci in range(B_WIDTH // LANES):
                o_ref[:, ci * LANES:(ci + 1) * LANES] = scr[ci].astype(o_ref.dtype)

    return pl.pallas_call(
        body, grid=(t // PERM_TILE,),
        in_specs=[pl.BlockSpec((rate, rows, B_WIDTH), lambda i: (0, i, 0)) for _ in arrays],
        out_specs=[pl.BlockSpec((PERM_TILE, B_WIDTH), lambda i: (i, 0)) for _ in arrays],
        out_shape=[SDS((t, B_WIDTH), a.dtype) for a in arrays],
        scratch_shapes=[pltpu.VMEM((B_WIDTH // LANES, PERM_TILE, LANES), F32)],
        name=name, compiler_params=_params())(*arrays)


def _head_lane_mask(h):
    lane = lax.broadcasted_iota(jnp.int32, (1, LANES), 1)
    return (lane < HEAD_DIM) if h == 0 else (lane >= HEAD_DIM)


def _attn_fwd(name, q, k, v, ride=None):
    rate, length = q[0].shape[0], q[0].shape[1]
    nb = length // ATT_BLOCK
    scale = HEAD_DIM ** -0.5

    def body(q_ref, kc_ref, kp_ref, vc_ref, vp_ref, o_ref, l_ref):
        n = pl.program_id(1)
        qi = lax.broadcasted_iota(jnp.int32, (ATT_BLOCK, 2 * ATT_BLOCK), 0)
        cj = lax.broadcasted_iota(jnp.int32, (ATT_BLOCK, 2 * ATT_BLOCK), 1)
        has_prev = jnp.where(n > 0, 0, 2 * ATT_BLOCK)
        mask = ((cj < ATT_BLOCK) & (cj >= qi + has_prev)) | ((cj >= ATT_BLOCK) & (cj - ATT_BLOCK <= qi))
        for hp in range(B_WIDTH // LANES):
            ls = slice(hp * LANES, (hp + 1) * LANES)
            q2 = q_ref[:, ls]
            k2 = jnp.concatenate([kp_ref[:, ls], kc_ref[:, ls]], axis=0)
            v2 = jnp.concatenate([vp_ref[:, ls], vc_ref[:, ls]], axis=0)
            o_acc, lse2 = None, None
            for h in range(2):
                hm = _head_lane_mask(h)
                s = _dot(jnp.where(hm, q2, jnp.zeros_like(q2)), k2, NT) * scale
                s = jnp.where(mask, s, NEG_INF)
                m = jnp.max(s, axis=1, keepdims=True)
                p = jnp.exp(s - m)
                den = jnp.sum(p, axis=1, keepdims=True)
                lse = m + jnp.log(den)
                o = _dot(p / den, jnp.where(hm, v2, jnp.zeros_like(v2)), NN)
                o_acc = o if h == 0 else o_acc + o
                lse_b = lse + jnp.zeros((ATT_BLOCK, LANES), F32)
                lse2 = lse_b if h == 0 else jnp.where(hm, lse_b, lse2)
            o_ref[:, ls] = o_acc
            l_ref[:, ls] = lse2

    def cur(cb):
        return pl.BlockSpec((None, ATT_BLOCK, B_WIDTH), lambda r, n: (r, n, cb))

    def prev(cb):
        return pl.BlockSpec((None, ATT_BLOCK, B_WIDTH), lambda r, n: (r, jnp.maximum(n - 1, 0), cb))

    out = pl.BlockSpec((None, ATT_BLOCK, B_WIDTH), lambda r, n: (r, n, 0))
    return _call(
        body, grid=(rate, nb),
        in_specs=[cur(q[1]), cur(k[1]), prev(k[1]), cur(v[1]), prev(v[1])],
        out_specs=[out, out], out_shape=[SDS((rate, length, B_WIDTH), F32)] * 2,
        operands=[q[0], k[0], k[0], v[0], v[0]], name=name, ride=ride)


def _attn_merge(a_out, o_list, l_list):
    t = a_out.shape[0]
    tm = 512

    def body(a_ref, o0, o1, o2, l0, l1, l2, cat_ref, lt_ref):
        ls = [l0[...], l1[...], l2[...]]
        m = jnp.maximum(jnp.maximum(ls[0], ls[1]), ls[2])
        es = [jnp.exp(l - m) for l in ls]
        den = es[0] + es[1] + es[2]
        b = (es[0] * o0[...] + es[1] * o1[...] + es[2] * o2[...]) / den
        cat_ref[:, :A_WIDTH] = a_ref[...]
        cat_ref[:, A_WIDTH:] = b.astype(BF16)
        lt_ref[...] = m + jnp.log(den)

    blk = _row_spec(tm, B_WIDTH)
    return pl.pallas_call(
        body, grid=(t // tm,), in_specs=[blk] * 7,
        out_specs=[_row_spec(tm, A_WIDTH + B_WIDTH), blk],
        out_shape=[SDS((t, A_WIDTH + B_WIDTH), BF16), SDS((t, B_WIDTH), F32)],
        name="attn_merge", compiler_params=_params())(a_out, *o_list, *l_list)


def _attn_bwd_prep(dcat, cat):
    t = dcat.shape[0]
    tm = 512

    def body(d_ref, b_ref, db_ref, dd_ref):
        seg = _segment_mean_matrix(HEAD_DIM, scale=1.0)
        for ci in range(B_WIDTH // LANES):
            ls = slice(ci * LANES, (ci + 1) * LANES)
            d = d_ref[:, ls]
            db_ref[:, ls] = d.astype(BF16)
            dd_ref[:, ls] = _segment_dot(d * b_ref[:, ls].astype(F32), seg)

    right = pl.BlockSpec((tm, B_WIDTH), lambda i: (i, 1))
    blk = _row_spec(tm, B_WIDTH)
    return pl.pallas_call(
        body, grid=(t // tm,), in_specs=[right, right], out_specs=[blk, blk],
        out_shape=[SDS((t, B_WIDTH), BF16), SDS((t, B_WIDTH), F32)],
        name="attn_bwd_prep", compiler_params=_params())(dcat, cat)


def _attn_bwd(name, q, k, v, db, lse, dd, ride=None):
    rate, length = db.shape[0], db.shape[1]
    nb = length // ATT_BLOCK
    scale = HEAD_DIM ** -0.5

    def body(qa_ref, qb_ref, k_ref, v_ref, dba_ref, dbb_ref, la_ref, lb_ref, da_ref, dbd_ref, dq_ref, dk_ref, dv_ref, carry):
        m = pl.program_id(1)

        @pl.when(m == 0)
        def _():
            carry[...] = jnp.zeros_like(carry)

        row = lax.broadcasted_iota(jnp.int32, (2 * ATT_BLOCK, ATT_BLOCK), 0)
        kj = lax.broadcasted_iota(jnp.int32, (2 * ATT_BLOCK, ATT_BLOCK), 1)
        no_next = jnp.where(m + 1 < nb, 0, 2 * ATT_BLOCK)
        mask = ((row < ATT_BLOCK) & (kj <= row)) | ((row >= ATT_BLOCK) & (kj >= row - ATT_BLOCK + no_next))
        for hp in range(B_WIDTH // LANES):
            ls = slice(hp * LANES, (hp + 1) * LANES)
            k2, v2 = k_ref[:, ls], v_ref[:, ls]
            q2 = jnp.concatenate([qa_ref[:, ls], qb_ref[:, ls]], axis=0)
            db2 = jnp.concatenate([dba_ref[:, ls], dbb_ref[:, ls]], axis=0)
            lse2 = jnp.concatenate([la_ref[:, ls], lb_ref[:, ls]], axis=0)
            dd2 = jnp.concatenate([da_ref[:, ls], dbd_ref[:, ls]], axis=0)
            dq_acc, dk_acc, dv_acc = None, None, None
            for h in range(2):
                hm = _head_lane_mask(h)
                km = jnp.where(hm, k2, jnp.zeros_like(k2))
                vm = jnp.where(hm, v2, jnp.zeros_like(v2))
                lse_col = jnp.max(jnp.where(hm, lse2, NEG_INF), axis=1, keepdims=True)
                dd_col = jnp.max(jnp.where(hm, dd2, NEG_INF), axis=1, keepdims=True)
                s = _dot(q2, km, NT) * scale
                p = jnp.where(mask, jnp.exp(s - lse_col), 0.0)
                dvc = _dot(p, jnp.where(hm, db2, jnp.zeros_like(db2)), TN)
                dp = _dot(db2, vm, NT)
                ds = (p * (dp - dd_col) * scale).astype(BF16)
                dqc = _dot(ds, km, NN)
                dkc = _dot(ds, jnp.where(hm, q2, jnp.zeros_like(q2)), TN)
                dq_acc = dqc if dq_acc is None else dq_acc + dqc
                dk_acc = dkc if dk_acc is None else dk_acc + dkc
                dv_acc = dvc if dv_acc is None else dv_acc + dvc
            dq_ref[:, ls] = (dq_acc[:ATT_BLOCK] + carry[:, ls]).astype(BF16)
            carry[:, ls] = dq_acc[ATT_BLOCK:]
            dk_ref[:, ls] = dk_acc.astype(BF16)
            dv_ref[:, ls] = dv_acc.astype(BF16)

    def cur(cb):
        return pl.BlockSpec((None, ATT_BLOCK, B_WIDTH), lambda r, n: (r, n, cb))

    def nxt(cb):
        return pl.BlockSpec((None, ATT_BLOCK, B_WIDTH), lambda r, n: (r, jnp.minimum(n + 1, nb - 1), cb))

    out = cur(0)
    return _call(
        body, grid=(rate, nb),
        in_specs=[cur(q[1]), nxt(q[1]), cur(k[1]), cur(v[1]), cur(0), nxt(0), cur(0), nxt(0), cur(0), nxt(0)],
        out_specs=[out, out, out], out_shape=[SDS((rate, length, B_WIDTH), BF16)] * 3,
        scratch_shapes=[pltpu.VMEM((ATT_BLOCK, B_WIDTH), F32)],
        operands=[q[0], q[0], k[0], v[0], db, db, lse, lse, dd, dd], name=name, ride=ride)


AB_IN = 2 * A_WIDTH + 3 * N_DIL * B_WIDTH
ASM_TILE = 256


def _dproj_assemble(proj, d_a, dqk, dv, gains, tabs):
    t = proj.shape[0]
    n_qk = 2 * N_DIL

    def body(p_ref, da_ref, *rest):
        dqk_refs = rest[:n_qk]
        dv_refs = rest[n_qk:n_qk + N_DIL]
        g_ref, c_ref, s1_ref, s2_ref, o_ref, dg_ref = rest[n_qk + N_DIL:]

        @pl.when(pl.program_id(0) == 0)
        def _():
            dg_ref[...] = jnp.zeros_like(dg_ref)

        seg = _segment_mean_matrix(HEAD_DIM)
        c, s1, s2 = c_ref[...], s1_ref[...], s2_ref[...]
        o_ref[:, :2 * A_WIDTH] = da_ref[...]
        for jg in range(n_qk):
            for ci in range(B_WIDTH // LANES):
                col = jg * B_WIDTH + ci * LANES
                src = slice(2 * A_WIDTH + col, 2 * A_WIDTH + col + LANES)
                xv = p_ref[:, src].astype(F32)
                r = lax.rsqrt(_segment_dot(xv * xv, seg) + EPS)
                xh = xv * r
                gain = g_ref[:, col:col + LANES]
                do = dqk_refs[jg][:, ci * LANES:(ci + 1) * LANES].astype(F32)
                dy = do * c + pltpu.roll(do * s1, 8, axis=1) + pltpu.roll(do * s2, LANES - 8, axis=1)
                dg_ref[:, col:col + LANES] += jnp.sum(dy * xh, axis=0, keepdims=True)
                dxh = dy * gain
                o_ref[:, src] = (r * (dxh - xh * _segment_dot(dxh * xh, seg))).astype(BF16)
        v0 = 2 * A_WIDTH + QK_COLS
        for g in range(N_DIL):
            o_ref[:, v0 + g * B_WIDTH:v0 + (g + 1) * B_WIDTH] = dv_refs[g][...]

    blk = _row_spec(ASM_TILE, B_WIDTH)
    tab = _row_spec(ASM_TILE, LANES)
    return pl.pallas_call(
        body, grid=(t // ASM_TILE,),
        in_specs=[_row_spec(ASM_TILE, AB_IN), _row_spec(ASM_TILE, 2 * A_WIDTH)] + [blk] * (n_qk + N_DIL)
        + [_const_spec((1, QK_COLS)), tab, tab, tab],
        out_specs=[_row_spec(ASM_TILE, AB_IN), _const_spec((1, QK_COLS))],
        out_shape=[SDS((t, AB_IN), BF16), SDS((1, QK_COLS), F32)],
        name="dproj_assemble", compiler_params=_params())(proj, d_a, *dqk, *dv, gains, *tabs)


def _fold_heads(dg_lane):
    n = dg_lane.shape[1]

    def body(x_ref, o_ref):
        r = lax.broadcasted_iota(jnp.int32, (B_WIDTH, B_WIDTH), 0) % HEAD_DIM
        c = lax.broadcasted_iota(jnp.int32, (B_WIDTH, B_WIDTH), 1) % HEAD_DIM
        fold = jnp.where(r == c, 1.0, 0.0).astype(F32)
        for jg in range(n // B_WIDTH):
            ls = slice(jg * B_WIDTH, (jg + 1) * B_WIDTH)
            o_ref[:, ls] = _dot_hi(jnp.broadcast_to(x_ref[:, ls], (8, B_WIDTH)), fold)

    return pl.pallas_call(body, out_shape=SDS((8, n), F32), name="fold_heads", compiler_params=_params())(dg_lane)


CD_TILE = 256
CD_IN = 2 * C_WIDTH + 3 * 512


def _cd_split(pv):
    w = C_WIDTH
    return pv[:, :w], pv[:, w:2 * w], pv[:, 2 * w:3 * w], pv[:, 3 * w:4 * w], pv[:, 4 * w:5 * w]


def _shifted_copies(src, dst, rows):
    dst[0, :rows] = src[...]
    for b in range(1, 8):
        dst[b, :rows - 8] = src[pl.ds(b, rows - 8), :]


def _rows_from(shifted, start, n):
    b = start % 8
    return shifted[b, pl.ds(start - b, n), :]


def _mixer_cd_fwd(proj, cw, cb, lg, lb, dw):
    t = proj.shape[0]
    per = CD_TILE // HALO

    def body(h_ref, m_ref, cw_ref, cb_ref, lg_ref, lb_ref, dw_ref, o_ref, c1_ref, c_scr, e_scr, c_sh):
        not_first = (pl.program_id(0) > 0).astype(F32)
        ha, hg, _, hgc, hhv = _cd_split(h_ref[...].astype(F32))
        ma, mg, mgb, mgc, mhv = _cd_split(m_ref[...].astype(F32))
        c_scr[:HALO] = ha * _sigmoid(hg) * not_first
        c_scr[HALO:] = ma * _sigmoid(mg)
        e_scr[:HALO] = hgc * hhv * not_first
        e_scr[HALO:] = mgc * mhv
        _shifted_copies(c_scr, c_sh, HALO + CD_TILE)
        acc = jnp.zeros((CD_TILE, C_WIDTH), F32)
        for k in range(C_KERNEL):
            acc = acc + cw_ref[k:k + 1, :] * _rows_from(c_sh, HALO - (C_KERNEL - 1) + k, CD_TILE)
        c1 = acc + cb_ref[...]
        c1_ref[...] = c1
        cc = c1 - jnp.mean(c1, axis=-1, keepdims=True)
        c2 = cc * lax.rsqrt(jnp.mean(cc * cc, axis=-1, keepdims=True) + EPS) * lg_ref[...] + lb_ref[...]
        o_ref[:, :C_WIDTH] = (c2 * _sigmoid(c2)).astype(BF16)
        d1 = jnp.zeros((CD_TILE, C_WIDTH), F32)
        for k in range(D_KERNEL):
            d1 = d1 + dw_ref[k:k + 1, :] * e_scr[pl.ds(HALO - (D_KERNEL - 1) + k, CD_TILE), :]
        o_ref[:, C_WIDTH:] = (mgb * d1).astype(BF16)

    return pl.pallas_call(
        body, grid=(t // CD_TILE,),
        in_specs=[pl.BlockSpec((HALO, CD_IN), lambda i: (jnp.maximum(i * per - 1, 0), 0)), _row_spec(CD_TILE, CD_IN),
                  _const_spec((32, C_WIDTH)), _const_spec((1, C_WIDTH)), _const_spec((1, C_WIDTH)), _const_spec((1, C_WIDTH)),
                  _const_spec((8, C_WIDTH))],
        out_specs=[_row_spec(CD_TILE, 2 * C_WIDTH), _row_spec(CD_TILE, C_WIDTH)],
        out_shape=[SDS((t, 2 * C_WIDTH), BF16), SDS((t, C_WIDTH), F32)],
        scratch_shapes=[pltpu.VMEM((HALO + CD_TILE, C_WIDTH), F32)] * 2 + [pltpu.VMEM((8, HALO + CD_TILE, C_WIDTH), F32)],
        name="mixer_cd_fwd", compiler_params=_params())(proj, proj, cw, cb, lg, lb, dw)


def _mixer_cd_bwd(proj, dcat, c1, cw, lg, lb, dw, ride=None):
    t = proj.shape[0]
    per = CD_TILE // HALO
    nt = t // CD_TILE
    ext = CD_TILE + HALO

    def body(hp_ref, m_ref, hn_ref, dm_ref, dn_ref, c1m_ref, c1n_ref, cw_ref, lg_ref, lb_ref, dw_ref,
             dp_ref, dcw_ref, dcb_ref, dlg_ref, dlb_ref, ddw_ref, c_scr, e_scr, dc1_scr, dd1_scr, c_sh, dc1_sh):
        i = pl.program_id(0)

        @pl.when(i == 0)
        def _():
            for r in (dcw_ref, dcb_ref, dlg_ref, dlb_ref, ddw_ref):
                r[...] = jnp.zeros_like(r)

        not_first = (i > 0).astype(F32)
        not_last = (i < nt - 1).astype(F32)
        pa, pg, _, pgc, phv = _cd_split(hp_ref[...].astype(F32))
        ma, mg, mgb, mgc, mhv = _cd_split(m_ref[...].astype(F32))
        na, ng, ngb, ngc, nhv = _cd_split(hn_ref[...].astype(F32))
        sig_m = _sigmoid(mg)
        c_scr[:HALO] = pa * _sigmoid(pg) * not_first
        c_scr[HALO:HALO + CD_TILE] = ma * sig_m
        c_scr[HALO + CD_TILE:] = na * _sigmoid(ng) * not_last
        e_scr[:HALO] = pgc * phv * not_first
        e_scr[HALO:HALO + CD_TILE] = mgc * mhv
        e_scr[HALO + CD_TILE:] = ngc * nhv * not_last

        _shifted_copies(c_scr, c_sh, 2 * HALO + CD_TILE)
        c1 = jnp.concatenate([c1m_ref[...], c1n_ref[...]], axis=0)
        cc = c1 - jnp.mean(c1, axis=-1, keepdims=True)
        rs = lax.rsqrt(jnp.mean(cc * cc, axis=-1, keepdims=True) + EPS)
        vhat = cc * rs
        c2 = vhat * lg_ref[...] + lb_ref[...]
        sig = _sigmoid(c2)
        dc = jnp.concatenate([dm_ref[:, :C_WIDTH], dn_ref[:, :C_WIDTH] * not_last], axis=0)
        dc2 = dc * (sig * (1.0 + c2 * (1.0 - sig)))
        dvh = dc2 * lg_ref[...]
        dc1 = rs * (dvh - jnp.mean(dvh, axis=-1, keepdims=True) - vhat * jnp.mean(dvh * vhat, axis=-1, keepdims=True))
        dc1_scr[...] = dc1
        _shifted_copies(dc1_scr, dc1_sh, ext)
        dlg_ref[...] += jnp.sum((dc2 * vhat)[:CD_TILE], axis=0, keepdims=True)
        dlb_ref[...] += jnp.sum(dc2[:CD_TILE], axis=0, keepdims=True)
        dc1_m = dc1[:CD_TILE]
        dcb_ref[...] += jnp.sum(dc1_m, axis=0, keepdims=True)
        dc0 = jnp.zeros((CD_TILE, C_WIDTH), F32)
        for k in range(C_KERNEL):
            dc0 = dc0 + cw_ref[k:k + 1, :] * _rows_from(dc1_sh, C_KERNEL - 1 - k, CD_TILE)
            dcw_ref[k:k + 1, :] += jnp.sum(dc1_m * _rows_from(c_sh, HALO - (C_KERNEL - 1) + k, CD_TILE), axis=0, keepdims=True)
        dp_ref[:, :C_WIDTH] = (dc0 * sig_m).astype(BF16)
        dp_ref[:, C_WIDTH:2 * C_WIDTH] = (dc0 * ma * sig_m * (1.0 - sig_m)).astype(BF16)

        d1 = jnp.zeros((CD_TILE, C_WIDTH), F32)
        for k in range(D_KERNEL):
            d1 = d1 + dw_ref[k:k + 1, :] * e_scr[pl.ds(HALO - (D_KERNEL - 1) + k, CD_TILE), :]
        dd_m = dm_ref[:, C_WIDTH:]
        dd1 = jnp.concatenate([dd_m * mgb, dn_ref[:, C_WIDTH:] * ngb * not_last], axis=0)
        dd1_scr[...] = dd1
        dp_ref[:, 2 * C_WIDTH:3 * C_WIDTH] = (dd_m * d1).astype(BF16)
        de = jnp.zeros((CD_TILE, C_WIDTH), F32)
        for k in range(D_KERNEL):
            de = de + dw_ref[k:k + 1, :] * dd1_scr[pl.ds(D_KERNEL - 1 - k, CD_TILE), :]
            ddw_ref[k:k + 1, :] += jnp.sum(dd1[:CD_TILE] * e_scr[pl.ds(HALO - (D_KERNEL - 1) + k, CD_TILE), :], axis=0, keepdims=True)
        dp_ref[:, 3 * C_WIDTH:4 * C_WIDTH] = (de * mhv).astype(BF16)
        dp_ref[:, 4 * C_WIDTH:] = (de * mgc).astype(BF16)

    halo_prev = lambda i: (jnp.maximum(i * per - 1, 0), 0)
    halo_next = lambda i: (jnp.minimum((i + 1) * per, t // HALO - 1), 0)
    vec = _const_spec((1, C_WIDTH))
    return _call(
        body, grid=(nt,),
        in_specs=[pl.BlockSpec((HALO, CD_IN), halo_prev), _row_spec(CD_TILE, CD_IN), pl.BlockSpec((HALO, CD_IN), halo_next),
                  _row_spec(CD_TILE, 2 * C_WIDTH), pl.BlockSpec((HALO, 2 * C_WIDTH), halo_next),
                  _row_spec(CD_TILE, C_WIDTH), pl.BlockSpec((HALO, C_WIDTH), halo_next),
                  _const_spec((32, C_WIDTH)), vec, vec, _const_spec((8, C_WIDTH))],
        out_specs=[_row_spec(CD_TILE, CD_IN), _const_spec((32, C_WIDTH)), vec, vec, vec, _const_spec((8, C_WIDTH))],
        out_shape=[SDS((t, CD_IN), BF16), SDS((32, C_WIDTH), F32), SDS((1, C_WIDTH), F32), SDS((1, C_WIDTH), F32),
                   SDS((1, C_WIDTH), F32), SDS((8, C_WIDTH), F32)],
        scratch_shapes=[pltpu.VMEM((2 * HALO + CD_TILE, C_WIDTH), F32)] * 2 + [pltpu.VMEM((ext, C_WIDTH), F32)] * 2
        + [pltpu.VMEM((8, 2 * HALO + CD_TILE, C_WIDTH), F32), pltpu.VMEM((8, ext, C_WIDTH), F32)],
        operands=[proj, proj, proj, dcat, dcat, c1, c1, cw, lg, lb, dw], name="mixer_cd_bwd", ride=ride)


def _wgrad(name, pairs, out_rc, t, ride):
    tk = TILES["wgrad"]
    r, c = out_rc
    n = len(pairs)

    def body(*refs):
        ab, out_refs = refs[:2 * n], refs[2 * n:]
        k = pl.program_id(1)
        parts = [_dot(ab[2 * j][...], ab[2 * j + 1][...], TN) for j in range(n)]

        @pl.when(k == 0)
        def _():
            for a in range(n):
                out_refs[a][...] = parts[a]

        @pl.when(k > 0)
        def _():
            for a in range(n):
                out_refs[a][...] += parts[a]

    operands, in_specs = [], []
    for lhs, lhs_spec, rhs, rhs_spec in pairs:
        operands += [lhs, rhs]
        in_specs += [lhs_spec, rhs_spec]
    res = _call(body, grid=(N_CHIPS, t // tk), in_specs=in_specs,
                out_specs=[pl.BlockSpec((None, r, c), lambda p, k: (p, 0, 0))] * n,
                out_shape=[SDS((N_CHIPS, r, c), F32)] * n, operands=operands, name=name, ride=ride)
    outs, ride_res = (res, None) if ride is None else res
    outs = [o.reshape(N_CHIPS, 2, r // 2, c) for o in outs]
    return outs if ride is None else (outs, ride_res)


def _wgrad_col_sharded(name, h, dz_list, three_d, ride=None):
    t, d = h.shape
    tk = TILES["wgrad"]
    n4 = dz_list[0].shape[-1] if three_d else dz_list[0].shape[-1] // N_CHIPS
    hs = pl.BlockSpec((tk, d), lambda p, k: (k, 0))
    zs = pl.BlockSpec((None, tk, n4), lambda p, k: (p, k, 0)) if three_d else pl.BlockSpec((tk, n4), lambda p, k: (k, p))
    return _wgrad(name, [(h, hs, dz, zs) for dz in dz_list], (d, n4), t, ride)


def _wgrad_row_sharded(name, a, g, three_d, ride=None):
    many = isinstance(a, (list, tuple))
    a_list = list(a) if many else [a]
    t, d = g.shape
    tk = TILES["wgrad"]
    k4 = a_list[0].shape[-1] if three_d else a_list[0].shape[-1] // N_CHIPS
    a_spec = pl.BlockSpec((None, tk, k4), lambda p, k: (p, k, 0)) if three_d else pl.BlockSpec((tk, k4), lambda p, k: (k, p))
    gs = pl.BlockSpec((tk, d), lambda p, k: (k, 0))
    res = _wgrad(name, [(a_j, a_spec, g, gs) for a_j in a_list], (k4, d), t, ride)
    if many:
        return res
    return res[0] if ride is None else (res[0][0], res[1])


def _mesh_scalars():
    return jnp.stack([lax.axis_index("c"), 2 * lax.axis_index("x") + lax.axis_index("y")]).astype(jnp.int32)


def _stage_own(name, w, layer, dtype):
    layers, r, cols = w.shape
    h = r // 2

    def body(s_ref, x_ref, o_ref):
        o_ref[...] = x_ref[...].astype(dtype)

    return pl.pallas_call(
        body,
        grid_spec=pltpu.PrefetchScalarGridSpec(
            num_scalar_prefetch=1, grid=(2,),
            in_specs=[pl.BlockSpec((None, h, cols), lambda i, s: (2 * layer + i, 0, 0))],
            out_specs=pl.BlockSpec((None, None, h, cols), lambda i, s: (s[1], i, 0, 0))),
        out_shape=SDS((N_CHIPS, 2, h, cols), dtype), name=name,
        compiler_params=_params())(_mesh_scalars(), w.reshape(2 * layers, h, cols))


def _remote(src, dst, send_sem, recv_sem, device):
    return pltpu.make_async_remote_copy(src, dst, send_sem, recv_sem, device_id=device, device_id_type=MESH)


def _ride_gather_send(bufs):
    n = len(bufs)

    def each(b, sems, act):
        send, recv = sems
        x, y, c, p, others = _position()
        for t in range(n):
            for j, (qx, qy) in enumerate(others):
                act(b[t].at[p, c], b[t].at[2 * qx + qy, c], send.at[t, j], recv.at[t, j], (qx, qy, c))

    def start(ins, b, new, sems):
        each(b, sems, lambda mine, landed, s, r, dev: _remote(mine, mine, s, r, dev).start())

    def finish(ins, b, new, sems):
        def act(mine, landed, s, r, dev):
            _remote(mine, mine, s, r, dev).wait_send()
            _remote(landed, landed, s, r, dev).wait_recv()
        each(b, sems, act)

    return _Ride([], bufs, [], [(n, 3), (n, 3)], start, finish)


def _ride_gather_pass(bufs):
    n = len(bufs)

    def each(b, sems, act):
        send, recv = sems
        x, y, c, p, others = _position()
        for t in range(n):
            for j, (qx, qy) in enumerate(others):
                act(b[t].at[2 * qx + qy, c], b[t].at[2 * qx + qy, 1 - c], send.at[t, j], recv.at[t, j], (x, y, 1 - c))

    def start(ins, b, new, sems):
        each(b, sems, lambda landed, passed, s, r, dev: _remote(landed, landed, s, r, dev).start())

    def finish(ins, b, new, sems):
        def act(landed, passed, s, r, dev):
            _remote(landed, landed, s, r, dev).wait_send()
            _remote(passed, passed, s, r, dev).wait_recv()
        each(b, sems, act)

    return _Ride([], bufs, [], [(n, 3), (n, 3)], start, finish)


def _ride_swap(tensors):
    n = len(tensors)

    def each(ins, new, sems, act):
        send, recv = sems
        x, y, c, _, _ = _position()
        for t in range(n):
            act(_remote(ins[t].at[:, 1 - c], new[t], send.at[t], recv.at[t], (x, y, 1 - c)))

    def start(ins, b, new, sems):
        each(ins, new, sems, lambda cp: cp.start())

    def finish(ins, b, new, sems):
        each(ins, new, sems, lambda cp: cp.wait())

    return _Ride(tensors, [], [SDS((s.shape[0],) + s.shape[2:], s.dtype) for s in tensors], [(n,), (n,)], start, finish)


def _ride_scatter(tensors, landing):
    n = len(tensors)

    def each(ins, b, sems, act):
        send, recv = sems
        x, y, c, p, others = _position()
        for t in range(n):
            for j, (qx, qy) in enumerate(others):
                q = 2 * qx + qy
                act(ins[t].at[q], b[t].at[p], b[t].at[q], send.at[t, j], recv.at[t, j], (qx, qy, c))

    def start(ins, b, new, sems):
        each(ins, b, sems, lambda src, dst, landed, s, r, dev: _remote(src, dst, s, r, dev).start())

    def finish(ins, b, new, sems):
        def act(src, dst, landed, s, r, dev):
            _remote(src, dst, s, r, dev).wait_send()
            _remote(landed, landed, s, r, dev).wait_recv()
        each(ins, b, sems, act)

    return _Ride(tensors, landing, [], [(n, 3), (n, 3)], start, finish)


def _ride_join(bufs):
    n = len(bufs)

    def each(b, sems, act):
        send, recv = sems
        x, y, c, _, _ = _position()
        for t in range(n):
            act(b[t].at[c], b[t].at[1 - c], send.at[t], recv.at[t], (x, y, 1 - c))

    def start(ins, b, new, sems):
        each(b, sems, lambda mine, theirs, s, r, dev: _remote(mine, mine, s, r, dev).start())

    def finish(ins, b, new, sems):
        def act(mine, theirs, s, r, dev):
            _remote(mine, mine, s, r, dev).wait_send()
            _remote(theirs, theirs, s, r, dev).wait_recv()
        each(b, sems, act)

    return _Ride([], bufs, [], [(n,), (n,)], start, finish)


def _all_reduce_small(pack):
    rows = pack.shape[0]
    n_dev = 2 * N_CHIPS

    def body(x_ref, o_ref, land, send, recv):
        x, y, c, p, _ = _position()
        me = 2 * p + c
        land[me] = x_ref[...]
        peers = [(dx, dy, dc) for dx in range(2) for dy in range(2) for dc in range(2) if (dx, dy, dc) != (0, 0, 0)]
        for j, (dx, dy, dc) in enumerate(peers):
            _remote(land.at[me], land.at[me], send.at[j], recv.at[j], (x ^ dx, y ^ dy, c ^ dc)).start()
        for j, (dx, dy, dc) in enumerate(peers):
            src = 4 * (x ^ dx) + 2 * (y ^ dy) + (c ^ dc)
            _remote(land.at[me], land.at[me], send.at[j], recv.at[j], (x ^ dx, y ^ dy, c ^ dc)).wait_send()
            _remote(land.at[src], land.at[src], send.at[j], recv.at[j], (x ^ dx, y ^ dy, c ^ dc)).wait_recv()
        acc = land[0]
        for dev in range(1, n_dev):
            acc = acc + land[dev]
        o_ref[...] = acc

    return pl.pallas_call(
        body, out_shape=SDS((rows, LANES), F32),
        scratch_shapes=[pltpu.VMEM((n_dev, rows, LANES), F32), pltpu.SemaphoreType.DMA((n_dev - 1,)),
                        pltpu.SemaphoreType.DMA((n_dev - 1,))],
        name="all_reduce_small", compiler_params=_params())(pack)


def _add_own_half(name, full, recv, out_dtype):
    n4, _, h, cols = full.shape

    def body(s_ref, a_ref, b_ref, o_ref, own_ref):
        v = (a_ref[...] + b_ref[...]).astype(out_dtype)
        o_ref[...] = v

        @pl.when(pl.program_id(0) == s_ref[1])
        def _():
            own_ref[...] = v

    return pl.pallas_call(
        body,
        grid_spec=pltpu.PrefetchScalarGridSpec(
            num_scalar_prefetch=1, grid=(n4,),
            in_specs=[pl.BlockSpec((None, None, h, cols), lambda q, s: (q, s[0], 0, 0)),
                      pl.BlockSpec((None, h, cols), lambda q, s: (q, 0, 0))],
            out_specs=[pl.BlockSpec((None, h, cols), lambda q, s: (q, 0, 0)),
                       pl.BlockSpec((None, h, cols), lambda q, s: (s[1], 0, 0))]),
        out_shape=[SDS((n4, h, cols), out_dtype)] * 2, name=name, compiler_params=_params())(_mesh_scalars(), full, recv)


def _sum_chips(name, parts):
    n4, h, cols = parts.shape
    th = h // 4 if h % 64 == 0 else h

    def body(s_ref, a_ref, o_ref):
        acc = a_ref[0].astype(F32)
        for q in range(1, n4):
            acc = acc + a_ref[q].astype(F32)
        o_ref[...] = acc

    return pl.pallas_call(
        body,
        grid_spec=pltpu.PrefetchScalarGridSpec(
            num_scalar_prefetch=1, grid=(h // th,),
            in_specs=[pl.BlockSpec((n4, th, cols), lambda i, s: (0, i, 0))],
            out_specs=pl.BlockSpec((None, th, cols), lambda i, s: (s[0], i, 0))),
        out_shape=SDS((2, h, cols), F32), name=name, compiler_params=_params())(_mesh_scalars(), parts)


def _adamw_math(w, g, m, v):
    m2 = ADAM_B1 * m + (1.0 - ADAM_B1) * g
    v2 = ADAM_B2 * v + (1.0 - ADAM_B2) * (g * g)
    m_hat = m2 / (1.0 - ADAM_B1 ** ADAM_STEP)
    v_hat = v2 / (1.0 - ADAM_B2 ** ADAM_STEP)
    delta = -ADAM_LR * (m_hat / (jnp.sqrt(v_hat) + ADAM_EPS) + ADAM_WD * w)
    return delta, m2, v2


def _row_tile(rows, cols):
    cap = max(8, (1 << 18) // cols)
    best = 8
    for cand in range(8, min(rows, cap) + 1, 8):
        if rows % cand == 0:
            best = cand
    return best


def _adamw_big(name, w, g_layers, m, v):
    layers, rows, cols = w.shape
    tr = _row_tile(rows, cols)

    def body(w_ref, m_ref, v_ref, *rest):
        g_refs, (g_o, d_o, m_o, v_o) = rest[:layers], rest[layers:]
        gv = g_refs[0][...]
        for layer in range(1, layers):
            gv = jnp.where(pl.program_id(0) == layer, g_refs[layer][...], gv)
        d, mm, vv = _adamw_math(w_ref[...], gv, m_ref[...], v_ref[...])
        g_o[...] = gv
        d_o[...] = d
        m_o[...] = mm
        v_o[...] = vv

    blk = pl.BlockSpec((None, tr, cols), lambda l, i: (l, i, 0))
    g_blk = pl.BlockSpec((tr, cols), lambda l, i: (i, 0))
    return tuple(pl.pallas_call(
        body, grid=(layers, rows // tr), in_specs=[blk] * 3 + [g_blk] * layers, out_specs=[blk] * 4,
        out_shape=[SDS((layers, rows, cols), F32)] * 4, name=name,
        compiler_params=_params())(w, m, v, *[g.reshape(rows, cols) for g in g_layers]))


def _adamw_small(ws, gs, ms, vs):
    n = len(ws)
    flat = []
    for group in (ws, gs, ms, vs):
        flat += [a.reshape(-1, a.shape[-1]) for a in group]

    def body(*refs):
        w_r, g_r, m_r, v_r = refs[:n], refs[n:2 * n], refs[2 * n:3 * n], refs[3 * n:4 * n]
        d_o, m_o, v_o = refs[4 * n:5 * n], refs[5 * n:6 * n], refs[6 * n:7 * n]
        for j in range(n):
            d, mm, vv = _adamw_math(w_r[j][...], g_r[j][...], m_r[j][...], v_r[j][...])
            d_o[j][...] = d
            m_o[j][...] = mm
            v_o[j][...] = vv

    shapes = [SDS(a.shape, F32) for a in flat[:n]]
    outs = pl.pallas_call(body, out_shape=shapes * 3, name="adamw_small", compiler_params=_params())(*flat)
    res = []
    for k in range(3):
        res.append([outs[k * n + j].reshape(ws[j].shape) for j in range(n)])
    return res


BIG = ("ab_w_in", "ab_w_out", "cd_w_in", "cd_w_out", "ffn_w_gate", "ffn_w_up", "ffn_w_down")
V_BLOCK = (2 * A_WIDTH + QK_COLS) // B_WIDTH


def _pad_rows(a, rows):
    return jnp.pad(a, ((0, rows - a.shape[0]), (0, 0)))


A_IN, A_OUT, C_IN, C_OUT = ("ab_w_in", 0), ("ab_w_out", 0), ("cd_w_in", 0), ("cd_w_out", 0)
G0, U0, D0 = ("ffn_w_gate", 0), ("ffn_w_up", 0), ("ffn_w_down", 0)
G1, U1, D1 = ("ffn_w_gate", 1), ("ffn_w_up", 1), ("ffn_w_down", 1)
UNITS = (A_IN, A_OUT, G0, U0, D0, C_IN, C_OUT, G1, U1, D1)
ROWS_MINOR = ("ffn_w_gate", "ffn_w_up")
SMALL_SHARDED = ("small", 0)
REPLICATED_UNIT = ("replicated", 0)


class _Exchange:
    def __init__(self, enabled):
        self.enabled = enabled
        self.w, self.grad, self.recv, self.half, self.land, self.done = {}, {}, {}, {}, {}, {}

    def full(self, unit):
        b = self.w[unit]
        return b.reshape(N_CHIPS, 1, 2 * b.shape[2], b.shape[3])

    def _ride(self, phases):
        rides, sinks = [], []
        for kind, units in phases:
            if kind == "send":
                rides.append(_ride_gather_send([self.w[u] for u in units]))
                sinks.append(self.w)
            elif kind == "pass":
                rides.append(_ride_gather_pass([self.w[u] for u in units]))
                sinks.append(self.w)
            elif kind == "swap":
                rides.append(_ride_swap([self.grad[u] for u in units]))
                sinks.append(self.recv)
            elif kind == "scatter":
                rides.append(_ride_scatter([self.half[u] for u in units], [self.land[u] for u in units]))
                sinks.append(self.land)
            else:
                rides.append(_ride_join([self.done[u] for u in units]))
                sinks.append(self.done)
        ride = functools.reduce(_ride_both, rides)

        def settle(res):
            n_bufs = sum(len(r.bufs) for r in rides)
            bufs, new = list(res[:n_bufs]), list(res[n_bufs:])
            for r, sink, (_, units) in zip(rides, sinks, phases):
                vals = [bufs.pop(0) for _ in r.bufs] + [new.pop(0) for _ in r.new_outs]
                for u, v in zip(units, vals):
                    sink[u] = v

        return ride, settle

    def run(self, fn, *args, phases=(), **kw):
        if not self.enabled or not phases:
            return fn(*args, **kw)
        ride, settle = self._ride(phases)
        out, res = fn(*args, ride=ride, **kw)
        settle(res)
        return out

    def alone(self, name, phases):
        if self.enabled:
            ride, settle = self._ride(phases)
            settle(_run_ride(name, ride))

    def pair_sum(self, units):
        if self.enabled:
            for u in units:
                dtype = F32 if u in (SMALL_SHARDED, REPLICATED_UNIT) else BF16
                self.half[u], self.land[u] = _add_own_half(f"pair_sum_{u[0]}_{u[1]}", self.grad[u], self.recv[u], dtype)

    def chip_sum(self, units):
        if self.enabled:
            for u in units:
                self.done[u] = _sum_chips(f"chip_sum_{u[0]}_{u[1]}", self.land[u])


def _local_step(x, target, ex, sp):
    t, d = x.shape
    tabs = _rope_tables(t)
    gains = jnp.concatenate([jnp.tile(sp["q_norm_g"][g], HEAD_DIM // 8) for g in range(N_DIL)]
                            + [jnp.tile(sp["k_norm_g"][g], HEAD_DIM // 8) for g in range(N_DIL)]).reshape(1, QK_COLS)
    bias_t = sp["sgu_bias"].T
    cw = _pad_rows(sp["conv_c_w"], 32)
    dw = _pad_rows(sp["conv_d_w"], 8)
    cb, clg, clb = (sp[k].reshape(1, C_WIDTH) for k in ("conv_c_b", "c_ln_g", "c_ln_b"))
    slg, slb = sp["sgu_norm_g"].reshape(1, A_WIDTH), sp["sgu_norm_b"].reshape(1, A_WIDTH)
    g_ab, g_cd = sp["ab_norm_g"].reshape(1, d), sp["cd_norm_g"].reshape(1, d)
    g_f0, g_f1 = sp["ffn_norm_g"][0:1], sp["ffn_norm_g"][1:2]
    run = ex.run

    def w2d(unit):
        return ex.full(unit).reshape(-1, d)

    h0 = _rms_fwd("rms_ab", x, g_ab)
    proj = run(_proj_in, "proj_ab", h0, ex.full(A_IN), 0, phases=[("send", [A_OUT, G0])])
    a_out = _mixer_a_fwd(proj, slg, slb, sp["sgu_w"], bias_t)
    qk = run(_qk_fwd, proj, gains, tabs, phases=[("pass", [A_OUT, G0]), ("send", [U0])])
    fwd_phases = ([("pass", [U0]), ("send", [D0])], [("pass", [D0]), ("send", [C_IN])],
                  [("pass", [C_IN]), ("send", [C_OUT, G1])])
    qkv, o_list, l_list = [], [], []
    for g, rate in enumerate(DIL_RATES):
        if rate == 1:
            qk3, proj3 = qk.reshape(1, t, QK_COLS), proj.reshape(1, t, AB_IN)
            q, k, v = (qk3, g), (qk3, N_DIL + g), (proj3, V_BLOCK + g)
        else:
            qp, kp, vp = _permute(f"perm_fwd_{g}", [(qk, g), (qk, N_DIL + g), (proj, V_BLOCK + g)], rate)
            q, k, v = (qp, 0), (kp, 0), (vp, 0)
        qkv.append((q, k, v))
        o, l = run(_attn_fwd, f"attn_fwd_{g}", q, k, v, phases=fwd_phases[g])
        if rate == 1:
            o, l = o.reshape(t, B_WIDTH), l.reshape(t, B_WIDTH)
        else:
            o, l = _unpermute(f"unperm_fwd_{g}", [o, l], rate)
        o_list.append(o)
        l_list.append(l)
    cat, lse_tot = _attn_merge(a_out, o_list, l_list)
    x1, hf0 = _proj_out("out_ab", cat, w2d(A_OUT), x, g_next=g_f0)
    gate0, up0, act0 = run(_ffn_in, "ffn_in_0", hf0, ex.full(G0), ex.full(U0), 0,
                           phases=[("pass", [C_OUT, G1]), ("send", [U1])])
    x2, h1 = run(_ffn_out, "ffn_out_0", act0, ex.full(D0), 0, x1, g_next=g_cd, phases=[("pass", [U1]), ("send", [D1])])
    projcd = run(_proj_in, "proj_cd", h1, ex.full(C_IN), 0, phases=[("pass", [D1])])
    cat2, c1 = _mixer_cd_fwd(projcd, cw, cb, clg, clb, dw)
    x3, hf1 = _proj_out("out_cd", cat2, w2d(C_OUT), x2, g_next=g_f1)
    gate1, up1, act1 = _ffn_in("ffn_in_1", hf1, ex.full(G1), ex.full(U1), 0)
    dy, loss_acc, dy_b = _ffn_out("ffn_out_1", act1, ex.full(D1), 0, x3, target=target)
    loss = 0.5 * loss_acc[0, 0] / d

    late = [D1, G1, U1]
    dgate, dup = _ffn_dact("ffn_dact_1", dy_b, ex.full(D1), 0, gate1, up1)
    ex.grad[D1] = _wgrad_row_sharded("wgrad_down_1", act1, dy_b, True)
    ex.grad[G1], ex.grad[U1] = _wgrad_row_sharded("wgrad_gate_up_1", [dgate, dup], hf1, True)
    g3, d_f1, g3_b = run(_dgrad_cols, "dgrad_ffn_1", [dgate, dup], [ex.full(G1), ex.full(U1)], 0, True, x3, g_f1, dy,
                         w_rows=True, phases=[("swap", late)])
    ex.pair_sum(late)

    dcat2 = _dgrad_rows("dgrad_out_cd", g3_b, w2d(C_OUT))
    ex.grad[C_OUT] = _wgrad_row_sharded("wgrad_out_cd", cat2, g3_b, False)
    dprojcd, d_cw, d_cb, d_clg, d_clb, d_dw = run(_mixer_cd_bwd, projcd, dcat2, c1, cw, clg, clb, dw, phases=[("scatter", late)])
    ex.chip_sum(late)
    ex.grad[C_IN] = run(_wgrad_col_sharded, "wgrad_in_cd", h1, [dprojcd], False, phases=[("join", late)])[0]
    g2, d_cdn, g2_b = run(_dgrad_cols, "dgrad_in_cd", [dprojcd], [ex.full(C_IN)], 0, False, x2, g_cd, g3,
                          phases=[("swap", [C_OUT, C_IN])])
    ex.pair_sum([C_OUT, C_IN])

    dgate, dup = run(_ffn_dact, "ffn_dact_0", g2_b, ex.full(D0), 0, gate0, up0, phases=[("scatter", [C_OUT, C_IN])])
    ex.chip_sum([C_OUT, C_IN])
    ex.grad[D0] = run(_wgrad_row_sharded, "wgrad_down_0", act0, g2_b, True, phases=[("join", [C_OUT, C_IN])])
    ex.grad[G0], ex.grad[U0] = _wgrad_row_sharded("wgrad_gate_up_0", [dgate, dup], hf0, True)
    small = {"cd_norm_g": d_cdn, "conv_c_w": d_cw[:C_KERNEL], "conv_c_b": d_cb, "c_ln_g": d_clg, "c_ln_b": d_clb,
             "conv_d_w": d_dw[:D_KERNEL]}
    ex.grad[SMALL_SHARDED] = _split_full_small(small).reshape(N_CHIPS, 2, SHARDED_ROWS // 2, LANES)
    mid = [D0, G0, U0, SMALL_SHARDED]
    g1, d_f0, g1_b = run(_dgrad_cols, "dgrad_ffn_0", [dgate, dup], [ex.full(G0), ex.full(U0)], 0, True, x1, g_f0, g2,
                         w_rows=True, phases=[("swap", mid)])
    ex.pair_sum(mid)

    dcat = _dgrad_rows("dgrad_out_ab", g1_b, w2d(A_OUT))
    ex.grad[A_OUT] = _wgrad_row_sharded("wgrad_out_ab", cat, g1_b, False)
    d_a, d_sw, d_sbt, d_slg, d_slb = _mixer_a_bwd(proj, dcat, slg, slb, sp["sgu_w"], bias_t)
    early = {"sgu_norm_g": d_slg, "sgu_norm_b": d_slb, "sgu_w": d_sw, "sgu_bias": d_sbt.T}
    ex.grad[REPLICATED_UNIT] = jnp.broadcast_to(
        _pack_replicated(early, REPLICATED_EARLY, REPLICATED_EARLY_ROWS).reshape(2, REPLICATED_EARLY_ROWS // 2, LANES),
        (N_CHIPS, 2, REPLICATED_EARLY_ROWS // 2, LANES))
    last = [A_OUT, REPLICATED_UNIT]
    dbb, dd = _attn_bwd_prep(dcat, cat)
    bwd_phases = ([("scatter", [D0, G0])], [("scatter", [U0, SMALL_SHARDED]), ("join", [D0, G0]), ("swap", last)],
                  [("join", [U0, SMALL_SHARDED]), ("scatter", last)])
    dqs, dks, dvs = [], [], []
    for g, rate in enumerate(DIL_RATES):
        q, k, v = qkv[g]
        if rate == 1:
            db3, l3, dd3 = (a.reshape(1, t, B_WIDTH) for a in (dbb, lse_tot, dd))
        else:
            db3, l3, dd3 = _permute(f"perm_bwd_{g}", [(dbb, 0), (lse_tot, 0), (dd, 0)], rate)
        if g == 1:
            ex.chip_sum([D0, G0])
        elif g == 2:
            ex.chip_sum([U0, SMALL_SHARDED])
            ex.pair_sum(last)
        dq, dk, dv = run(_attn_bwd, f"attn_bwd_{g}", q, k, v, db3, l3, dd3, phases=bwd_phases[g])
        if g == 2:
            ex.chip_sum(last)
        if rate == 1:
            dq, dk, dv = (a.reshape(t, B_WIDTH) for a in (dq, dk, dv))
        else:
            dq, dk, dv = _unpermute(f"unperm_bwd_{g}", [dq, dk, dv], rate)
        dqs.append(dq)
        dks.append(dk)
        dvs.append(dv)
    dproj, d_gains = _dproj_assemble(proj, d_a, dqs + dks, dvs, gains, tabs)
    d_gains = _fold_heads(d_gains)[0].reshape(2, N_DIL, B_WIDTH)[:, :, :HEAD_DIM]
    ex.grad[A_IN] = run(_wgrad_col_sharded, "wgrad_in_ab", h0, [dproj], False, phases=[("join", last)])[0]
    ex.alone("swap_last", [("swap", [A_IN])])
    ex.pair_sum([A_IN])
    gx, d_abn = run(_dgrad_cols, "dgrad_in_ab", [dproj], [ex.full(A_IN)], 0, False, x, g_ab, g1, bf16_copy=False,
                    phases=[("scatter", [A_IN])])
    ex.chip_sum([A_IN])
    ex.alone("join_last", [("join", [A_IN])])

    small.update({
        "ab_norm_g": d_abn, "sgu_norm_g": d_slg, "sgu_norm_b": d_slb, "sgu_w": d_sw, "sgu_bias": d_sbt.T,
        "q_norm_g": d_gains[0], "k_norm_g": d_gains[1], "ffn_norm_g": jnp.concatenate([d_f0, d_f1], axis=0),
    })
    return loss, gx, small


SHARDED_SMALL = ("cd_norm_g", "conv_c_w", "conv_c_b", "c_ln_g", "c_ln_b", "conv_d_w")
SHARDED_ROWS = 48
REPLICATED_EARLY = ("sgu_norm_g", "sgu_norm_b", "sgu_w", "sgu_bias")
REPLICATED_EARLY_ROWS = 528
REPLICATED_LATE = ("ab_norm_g", "q_norm_g", "k_norm_g", "ffn_norm_g", "loss")
REPLICATED_LATE_ROWS = 32
REPLICATED_SMALL = REPLICATED_EARLY + REPLICATED_LATE[:-1]


def _pack_sharded(parts):
    rows = [parts[k].reshape(-1, LANES) for k in SHARDED_SMALL]
    return _pad_rows(jnp.concatenate(rows, axis=0), SHARDED_ROWS)


def _split_full_small(small):
    per_chip = []
    for q in range(N_CHIPS):
        parts = {}
        for k in SHARDED_SMALL:
            a = small[k]
            a = a.reshape(-1, a.shape[-1])
            n = a.shape[-1] // N_CHIPS
            parts[k] = a[:, q * n:(q + 1) * n]
        per_chip.append(_pack_sharded(parts))
    return jnp.stack(per_chip)


def _unpack_sharded(pack, shapes):
    out, r = {}, 0
    for k in SHARDED_SMALL:
        n = math.prod(shapes[k]) // LANES
        out[k] = pack[r:r + n].reshape(shapes[k])
        r += n
    return out


def _gathered_small(packs, shapes):
    per_chip = [_unpack_sharded(packs[q], shapes) for q in range(N_CHIPS)]
    return {k: jnp.concatenate([pc[k] for pc in per_chip], axis=-1) for k in SHARDED_SMALL}


def _pack_replicated(small, names, total_rows):
    rows = []
    for k in names:
        a = small[k].reshape(-1)
        a = jnp.pad(a, (0, (-a.shape[0]) % LANES))
        rows.append(a.reshape(-1, LANES))
    return _pad_rows(jnp.concatenate(rows, axis=0), total_rows)


def _unpack_replicated(pack, shapes, names):
    out, r = {}, 0
    for k in names:
        size = math.prod(shapes[k])
        n = -(-size // LANES)
        out[k] = pack[r:r + n].reshape(-1)[:size].reshape(shapes[k])
        r += n
    return out


WEIGHT_ORDER = ("ab_norm_g", "ab_w_in", "sgu_norm_g", "sgu_norm_b", "sgu_w", "sgu_bias", "q_norm_g", "k_norm_g", "ab_w_out",
                "cd_norm_g", "cd_w_in", "conv_c_w", "conv_c_b", "c_ln_g", "c_ln_b", "conv_d_w", "cd_w_out", "ffn_norm_g",
                "ffn_w_gate", "ffn_w_up", "ffn_w_down")


def kernel(x, ab_norm_g, ab_w_in, sgu_norm_g, sgu_norm_b, sgu_w, sgu_bias, q_norm_g, k_norm_g, ab_w_out, cd_norm_g, cd_w_in, conv_c_w, conv_c_b, c_ln_g, c_ln_b, conv_d_w, cd_w_out, ffn_norm_g, ffn_w_gate, ffn_w_up, ffn_w_down, loss_target, m_ab_norm_g, m_ab_w_in, m_sgu_norm_g, m_sgu_norm_b, m_sgu_w, m_sgu_bias, m_q_norm_g, m_k_norm_g, m_ab_w_out, m_cd_norm_g, m_cd_w_in, m_conv_c_w, m_conv_c_b, m_c_ln_g, m_c_ln_b, m_conv_d_w, m_cd_w_out, m_ffn_norm_g, m_ffn_w_gate, m_ffn_w_up, m_ffn_w_down, v_ab_norm_g, v_ab_w_in, v_sgu_norm_g, v_sgu_norm_b, v_sgu_w, v_sgu_bias, v_q_norm_g, v_k_norm_g, v_ab_w_out, v_cd_norm_g, v_cd_w_in, v_conv_c_w, v_conv_c_b, v_c_ln_g, v_c_ln_b, v_conv_d_w, v_cd_w_out, v_ffn_norm_g, v_ffn_w_gate, v_ffn_w_up, v_ffn_w_down):
    args = dict(locals())
    ws = {k: args[k] for k in WEIGHT_ORDER}
    ms = {k: args["m_" + k] for k in WEIGHT_ORDER}
    vs = {k: args["v_" + k] for k in WEIGHT_ORDER}
    small_names = [k for k in WEIGHT_ORDER if k not in BIG]
    t, d = x.shape[1:]

    for group in (ws, ms, vs):
        for k in ROWS_MINOR:
            group[k] = jnp.swapaxes(group[k], 1, 2)
    ex = _Exchange(enabled=True)
    for name, layer in UNITS:
        ex.w[(name, layer)] = _stage_own(f"stage_{name}_{layer}", ws[name], layer, BF16)
    own_small = _pack_sharded({k: ws[k][0] for k in SHARDED_SMALL})
    ex.w[SMALL_SHARDED] = _stage_own("stage_small", own_small[None], 0, F32)
    ex.alone("gather_first", [("send", [A_IN, SMALL_SHARDED])])
    ex.alone("gather_first_pass", [("pass", [A_IN, SMALL_SHARDED])])
    sp = _gathered_small(ex.w[SMALL_SHARDED].reshape(N_CHIPS, SHARDED_ROWS, LANES), {k: ws[k].shape[1:] for k in SHARDED_SMALL})
    for k in REPLICATED_SMALL:
        sp[k] = ws[k] if k == "ffn_norm_g" else ws[k][0]

    loss, grad_x, g_small = _local_step(x.reshape(t, d), loss_target.reshape(t, d), ex, sp)

    shapes = {k: ws[k].shape for k in REPLICATED_SMALL}
    shapes["loss"] = (1,)
    g_small["loss"] = loss
    late = _all_reduce_small(_pack_replicated(g_small, REPLICATED_LATE, REPLICATED_LATE_ROWS))
    grad = _unpack_sharded(ex.done[SMALL_SHARDED].reshape(SHARDED_ROWS, LANES), {k: ws[k].shape for k in SHARDED_SMALL})
    grad.update(_unpack_replicated(ex.done[REPLICATED_UNIT].reshape(REPLICATED_EARLY_ROWS, LANES), shapes, REPLICATED_EARLY))
    grad.update(_unpack_replicated(late, shapes, REPLICATED_LATE))
    loss = grad.pop("loss")[0]

    delta, new_m, new_v = {}, {}, {}
    for k in BIG:
        g_layers = [ex.done[(k, layer)] for layer in range(ws[k].shape[0])]
        outs = _adamw_big("adamw_" + k, ws[k], g_layers, ms[k], vs[k])
        if k in ROWS_MINOR:
            outs = [jnp.swapaxes(o, 1, 2) for o in outs]
        grad[k], delta[k], new_m[k], new_v[k] = outs
    d_s, m_s, v_s = _adamw_small([ws[k] for k in small_names], [grad[k] for k in small_names],
                                 [ms[k] for k in small_names], [vs[k] for k in small_names])
    for j, k in enumerate(small_names):
        delta[k], new_m[k], new_v[k] = d_s[j], m_s[j], v_s[j]

    return (loss, grad_x[None], *[grad[k] for k in WEIGHT_ORDER], *[delta[k] for k in WEIGHT_ORDER],
            *[new_m[k] for k in WEIGHT_ORDER], *[new_v[k] for k in WEIGHT_ORDER])
```

```python
import functools
import math

import jax
import jax.numpy as jnp
from jax import lax
from jax.experimental import pallas as pl
from jax.experimental.pallas import tpu as pltpu

F32 = jnp.float32
BF16 = jnp.bfloat16
SDS = jax.ShapeDtypeStruct

N_CHIPS = 4
EPS = 1e-6
NEG_INF = -1e30
CHUNK = 128
A_GROUPS = 4
A_WIDTH = 512
N_DIL = 3
DIL_RATES = (1, 4, 16)
HEAD_DIM = 64
B_WIDTH = 512
ROPE_DIM = 16
ROPE_THETA = 500000.0
C_WIDTH = 512
C_KERNEL = 31
D_KERNEL = 3
HALO = 32
ATT_BLOCK = 128
LANES = 128

ADAM_LR = 0.001
ADAM_B1 = 0.9
ADAM_B2 = 0.999
ADAM_EPS = 1e-08
ADAM_WD = 0.01
ADAM_STEP = 10

VMEM_LIMIT = 56 * 1024 * 1024

NN = (((1,), (0,)), ((), ()))
NT = (((1,), (1,)), ((), ()))
TN = (((0,), (0,)), ((), ()))

TILES = {"proj_in": 1024, "proj_out": 1024, "ffn_in": 1024, "ffn_out": 512, "ffn_dact": 512, "dgrad_cols": 512,
         "dgrad_rows": 1024, "wgrad": 2048}


def _params(sem=None):
    return pltpu.CompilerParams(dimension_semantics=sem, vmem_limit_bytes=VMEM_LIMIT)


def _bf(v):
    return v if v.dtype == BF16 else v.astype(BF16)


def _dot(a, b, dims):
    return lax.dot_general(_bf(a), _bf(b), dims, preferred_element_type=F32)


def _dot_hi(a, b):
    return jnp.dot(a, b, precision=lax.Precision.HIGHEST, preferred_element_type=F32)


def _sigmoid(v):
    return 0.5 * jnp.tanh(0.5 * v) + 0.5


def _gelu(v):
    return 0.5 * v * (1.0 + lax.erf(v * (1.0 / math.sqrt(2.0))))


def _gelu_grad(v):
    cdf = 0.5 * (1.0 + lax.erf(v * (1.0 / math.sqrt(2.0))))
    return cdf + v * jnp.exp(-0.5 * v * v) * (1.0 / math.sqrt(2.0 * math.pi))


def _segment_mean_matrix(seg, scale=None):
    r = lax.broadcasted_iota(jnp.int32, (LANES, LANES), 0) // seg
    c = lax.broadcasted_iota(jnp.int32, (LANES, LANES), 1) // seg
    return jnp.where(r == c, (1.0 / seg) if scale is None else scale, 0.0).astype(BF16)


def _segment_dot(v, seg):
    hi = v.astype(BF16)
    lo = (v - hi.astype(F32)).astype(BF16)
    return jnp.dot(hi, seg, preferred_element_type=F32) + jnp.dot(lo, seg, preferred_element_type=F32)


MESH = pl.DeviceIdType.MESH
ANY = pl.BlockSpec(memory_space=pl.ANY)


def _position():
    x, y, c = lax.axis_index("x"), lax.axis_index("y"), lax.axis_index("c")
    others = [(1 - x, y), (x, 1 - y), (1 - x, 1 - y)]
    return x, y, c, 2 * x + y, others


class _Ride:
    def __init__(self, ins, bufs, new_outs, sem_shapes, start, finish):
        self.ins, self.bufs, self.new_outs, self.sem_shapes = list(ins), list(bufs), list(new_outs), list(sem_shapes)
        self.start, self.finish = start, finish


def _ride_both(a, b):
    na = (len(a.ins), len(a.bufs), len(a.new_outs), len(a.sem_shapes))

    def split(ins, bufs, new, sems):
        return ((ins[:na[0]], bufs[:na[1]], new[:na[2]], sems[:na[3]]), (ins[na[0]:], bufs[na[1]:], new[na[2]:], sems[na[3]:]))

    def start(*refs):
        ra, rb = split(*refs)
        a.start(*ra)
        b.start(*rb)

    def finish(*refs):
        ra, rb = split(*refs)
        a.finish(*ra)
        b.finish(*rb)

    return _Ride(a.ins + b.ins, a.bufs + b.bufs, a.new_outs + b.new_outs, a.sem_shapes + b.sem_shapes, start, finish)


def _call(body, *, grid, in_specs, out_specs, out_shape, operands, name, scratch_shapes=(), aliases=None, ride=None):
    if ride is None:
        return pl.pallas_call(body, grid=grid, in_specs=in_specs, out_specs=out_specs, out_shape=out_shape,
                              scratch_shapes=list(scratch_shapes), input_output_aliases=aliases or {}, name=name,
                              compiler_params=_params())(*operands)
    multi = isinstance(out_shape, (list, tuple))
    out_shapes = list(out_shape) if multi else [out_shape]
    o_specs = list(out_specs) if multi else [out_specs]
    n_in, n_out, n_scr = len(operands), len(out_shapes), len(scratch_shapes)
    n_ri, n_rb, n_rn = len(ride.ins), len(ride.bufs), len(ride.new_outs)

    def carrying(*refs):
        k = n_in
        r_ins = refs[k:k + n_ri]
        k += n_ri + n_rb
        outs = refs[k:k + n_out]
        k += n_out
        r_bufs = refs[k:k + n_rb]
        k += n_rb
        r_new = refs[k:k + n_rn]
        k += n_rn
        scratch = refs[k:k + n_scr]
        sems = refs[k + n_scr:]
        first, last = None, None
        for axis, size in enumerate(grid):
            pid = pl.program_id(axis)
            first = (pid == 0) if first is None else first & (pid == 0)
            last = (pid == size - 1) if last is None else last & (pid == size - 1)

        @pl.when(first)
        def _():
            ride.start(r_ins, r_bufs, r_new, sems)

        body(*refs[:n_in], *outs, *scratch)

        @pl.when(last)
        def _():
            ride.finish(r_ins, r_bufs, r_new, sems)

    all_aliases = dict(aliases or {})
    for j in range(n_rb):
        all_aliases[n_in + n_ri + j] = n_out + j
    res = pl.pallas_call(
        carrying, grid=grid, in_specs=list(in_specs) + [ANY] * (n_ri + n_rb), out_specs=o_specs + [ANY] * (n_rb + n_rn),
        out_shape=out_shapes + [SDS(b.shape, b.dtype) for b in ride.bufs] + ride.new_outs,
        scratch_shapes=list(scratch_shapes) + [pltpu.SemaphoreType.DMA(s) for s in ride.sem_shapes],
        input_output_aliases=all_aliases, name=name, compiler_params=_params())(*operands, *ride.ins, *ride.bufs)
    outs = res[:n_out]
    return (list(outs) if multi else outs[0]), list(res[n_out:])


def _run_ride(name, ride):
    n_ri, n_rb, n_rn = len(ride.ins), len(ride.bufs), len(ride.new_outs)

    def body(*refs):
        r_ins = refs[:n_ri]
        r_bufs = refs[n_ri + n_rb:n_ri + 2 * n_rb]
        r_new = refs[n_ri + 2 * n_rb:n_ri + 2 * n_rb + n_rn]
        sems = refs[n_ri + 2 * n_rb + n_rn:]
        ride.start(r_ins, r_bufs, r_new, sems)
        ride.finish(r_ins, r_bufs, r_new, sems)

    return list(pl.pallas_call(
        body, in_specs=[ANY] * (n_ri + n_rb), out_specs=[ANY] * (n_rb + n_rn),
        out_shape=[SDS(b.shape, b.dtype) for b in ride.bufs] + ride.new_outs,
        scratch_shapes=[pltpu.SemaphoreType.DMA(s) for s in ride.sem_shapes],
        input_output_aliases={n_ri + j: j for j in range(n_rb)}, name=name)(*ride.ins, *ride.bufs))


def _whole(ref, p):
    return ref[...]


def _slab(ref, p):
    return ref[p]


def _matmul(name, grid, pairs, extras, outs, dims, epi, *, slabs=1, n_acc=1, ride=None):
    n_pairs, n_ex, n_out = len(pairs), len(extras), len(outs)

    def body(*refs):
        ab = refs[:2 * n_pairs]
        ex = refs[2 * n_pairs:2 * n_pairs + n_ex]
        out_refs = refs[2 * n_pairs + n_ex:2 * n_pairs + n_ex + n_out]
        pids = tuple(pl.program_id(a) for a in range(len(grid)))
        parts = [None] * n_acc
        for p in range(slabs):
            for j, (_, _, a_pick, _, _, b_pick, acc) in enumerate(pairs):
                d = _dot(a_pick(ab[2 * j], p), b_pick(ab[2 * j + 1], p), dims)
                parts[acc] = d if parts[acc] is None else parts[acc] + d
        epi(parts, ex, out_refs, pids)

    operands, in_specs = [], []
    for a, a_spec, _, b, b_spec, _, _ in pairs:
        operands += [a, b]
        in_specs += [a_spec, b_spec]
    for e, e_spec in extras:
        operands.append(e)
        in_specs.append(e_spec)
    return _call(body, grid=grid, in_specs=in_specs, out_specs=[o[1] for o in outs], out_shape=[o[0] for o in outs],
                 operands=operands, name=name, ride=ride)


def _rms_rows(v, g):
    r = lax.rsqrt(jnp.mean(v * v, axis=-1, keepdims=True) + EPS)
    return v * r * g


def _rms_fwd(name, x, g):
    t, d = x.shape
    tm = 512

    def body(x_ref, g_ref, o_ref):
        o_ref[...] = _rms_rows(x_ref[...], g_ref[...]).astype(BF16)

    return pl.pallas_call(
        body, grid=(t // tm,),
        in_specs=[pl.BlockSpec((tm, d), lambda i: (i, 0)), pl.BlockSpec((1, d), lambda i: (0, 0))],
        out_specs=pl.BlockSpec((tm, d), lambda i: (i, 0)), out_shape=SDS((t, d), BF16), name=name,
        compiler_params=_params())(x, g)


def _epi_residual_norm(accs, ex, outs, pids):
    x_new = accs[0] + ex[0][...]
    outs[0][...] = x_new
    outs[1][...] = _rms_rows(x_new, ex[1][...]).astype(BF16)


def _epi_residual_loss(accs, ex, outs, pids):
    y = accs[0] + ex[0][...]
    err = y - ex[1][...]
    dy = err * (1.0 / err.shape[-1])
    outs[0][...] = dy
    outs[2][...] = dy.astype(BF16)

    @pl.when(pids[0] == 0)
    def _():
        outs[1][...] = jnp.zeros_like(outs[1])

    outs[1][...] += jnp.sum(err * err)


def _epi_rms_bwd(accs, ex, outs, pids):
    dh = accs[0]
    xv, g, res = ex[0][...], ex[1][...], ex[2][...]
    r = lax.rsqrt(jnp.mean(xv * xv, axis=-1, keepdims=True) + EPS)
    xh = xv * r
    dy = dh * g
    dx = res + r * (dy - xh * jnp.mean(dy * xh, axis=-1, keepdims=True))
    outs[0][...] = dx
    if len(outs) > 2:
        outs[2][...] = dx.astype(BF16)

    @pl.when(pids[0] == 0)
    def _():
        outs[1][...] = jnp.zeros_like(outs[1])

    outs[1][...] += jnp.sum(dh * xh, axis=0, keepdims=True)


def _row_spec(tm, d):
    return pl.BlockSpec((tm, d), lambda i, *_: (i, 0))


def _const_spec(shape):
    nd = len(shape)
    return pl.BlockSpec(shape, lambda *_: (0,) * nd)


def _proj_in(name, h, w, layer, ride=None):
    t, d = h.shape
    n4 = w.shape[-1]
    tm = TILES["proj_in"]

    def epi(accs, ex, outs, pids):
        outs[0][...] = accs[0].astype(BF16)

    res = _matmul(
        name, (N_CHIPS, t // tm),
        [(h, pl.BlockSpec((tm, d), lambda p, i: (i, 0)), _whole,
          w, pl.BlockSpec((None, None, d, n4), lambda p, i: (p, layer, 0, 0)), _whole, 0)],
        [], [(SDS((t, N_CHIPS * n4), BF16), pl.BlockSpec((tm, n4), lambda p, i: (i, p)))],
        NN, epi, ride=ride)
    return res[0] if ride is None else (res[0][0], res[1])


def _proj_out(name, a, w, x, g_next=None, target=None):
    t, k = a.shape
    d = w.shape[-1]
    tm = TILES["proj_out"]
    if target is None:
        extras = [(x, _row_spec(tm, d)), (g_next, _const_spec((1, d)))]
        outs = [(SDS((t, d), F32), _row_spec(tm, d)), (SDS((t, d), BF16), _row_spec(tm, d))]
        epi = _epi_residual_norm
    else:
        extras = [(x, _row_spec(tm, d)), (target, _row_spec(tm, d))]
        outs = [(SDS((t, d), F32), _row_spec(tm, d)), (SDS((8, LANES), F32), _const_spec((8, LANES))),
                (SDS((t, d), BF16), _row_spec(tm, d))]
        epi = _epi_residual_loss
    return _matmul(name, (t // tm,), [(a, _row_spec(tm, k), _whole, w, _const_spec((k, d)), _whole, 0)], extras, outs, NN, epi)


def _ffn_in(name, h, wg, wu, layer, ride=None):
    t, d = h.shape
    n4 = wg.shape[-2]
    tm = TILES["ffn_in"]

    def epi(accs, ex, outs, pids):
        gate, up = accs
        outs[0][...] = gate.astype(BF16)
        outs[1][...] = up.astype(BF16)
        outs[2][...] = (gate * _sigmoid(gate) * up).astype(BF16)

    w_spec = pl.BlockSpec((None, None, n4, d), lambda p, i: (p, layer, 0, 0))
    h_spec = pl.BlockSpec((tm, d), lambda p, i: (i, 0))
    o = (SDS((N_CHIPS, t, n4), BF16), pl.BlockSpec((None, tm, n4), lambda p, i: (p, i, 0)))
    return _matmul(name, (N_CHIPS, t // tm),
                   [(h, h_spec, _whole, wg, w_spec, _whole, 0), (h, h_spec, _whole, wu, w_spec, _whole, 1)], [],
                   [o, o, o], NT, epi, n_acc=2, ride=ride)


def _ffn_out(name, act, wd, layer, x, g_next=None, target=None, ride=None):
    _, t, n4 = act.shape
    d = wd.shape[-1]
    tm = TILES["ffn_out"]
    xs = _row_spec(tm, d)
    if target is None:
        extras = [(x, xs), (g_next, _const_spec((1, d)))]
        outs = [(SDS((t, d), F32), xs), (SDS((t, d), BF16), xs)]
        epi = _epi_residual_norm
    else:
        extras = [(x, xs), (target, xs)]
        outs = [(SDS((t, d), F32), xs), (SDS((8, LANES), F32), _const_spec((8, LANES))), (SDS((t, d), BF16), xs)]
        epi = _epi_residual_loss
    return _matmul(
        name, (t // tm,),
        [(act, pl.BlockSpec((N_CHIPS, tm, n4), lambda i: (0, i, 0)), _slab,
          wd, pl.BlockSpec((N_CHIPS, None, n4, d), lambda i: (0, layer, 0, 0)), _slab, 0)],
        extras, outs, NN, epi, slabs=N_CHIPS, ride=ride)


def _ffn_dact(name, g, wd, layer, gate, up, ride=None):
    t, d = g.shape
    n4 = wd.shape[-2]
    tm = TILES["ffn_dact"]

    def epi(accs, ex, outs, pids):
        dact = accs[0]
        gt = ex[0][...].astype(F32)
        upv = ex[1][...].astype(F32)
        s = _sigmoid(gt)
        silu = gt * s
        outs[0][...] = (dact * upv * (s + silu - silu * s)).astype(BF16)
        outs[1][...] = (dact * silu).astype(BF16)

    blk = pl.BlockSpec((None, tm, n4), lambda p, i: (p, i, 0))
    o = (SDS((N_CHIPS, t, n4), BF16), blk)
    return _matmul(
        name, (N_CHIPS, t // tm),
        [(g, pl.BlockSpec((tm, d), lambda p, i: (i, 0)), _whole,
          wd, pl.BlockSpec((None, None, n4, d), lambda p, i: (p, layer, 0, 0)), _whole, 0)],
        [(gate, blk), (up, blk)], [o, o], NT, epi, ride=ride)


def _copy_epi(accs, ex, outs, pids):
    for a, o in zip(accs, outs):
        o[...] = a.astype(o.dtype)


def _dgrad_cols(name, dz_list, w_list, layer, three_d, x, g, res, bf16_copy=True, w_rows=False, ride=None):
    t, d = x.shape
    n4 = w_list[0].shape[-2 if w_rows else -1]
    tm = TILES["dgrad_cols"]
    if three_d:
        zs, z_pick = pl.BlockSpec((N_CHIPS, tm, n4), lambda i: (0, i, 0)), _slab
    else:
        zs, z_pick = _row_spec(tm, N_CHIPS * n4), (lambda ref, p: ref[:, p * n4:(p + 1) * n4])
    ws = pl.BlockSpec((N_CHIPS, None) + ((n4, d) if w_rows else (d, n4)), lambda i: (0, layer, 0, 0))
    xs = _row_spec(tm, d)
    return _matmul(
        name, (t // tm,), [(dz, zs, z_pick, w, ws, _slab, 0) for dz, w in zip(dz_list, w_list)],
        [(x, xs), (g, _const_spec((1, d))), (res, xs)],
        [(SDS((t, d), F32), xs), (SDS((1, d), F32), _const_spec((1, d)))] + ([(SDS((t, d), BF16), xs)] if bf16_copy else []),
        NN if w_rows else NT, _epi_rms_bwd, slabs=N_CHIPS, ride=ride)


def _dgrad_rows(name, g, w):
    t, d = g.shape
    k = w.shape[0]
    tm = TILES["dgrad_rows"]
    return _matmul(name, (t // tm,), [(g, _row_spec(tm, d), _whole, w, _const_spec((k, d)), _whole, 0)], [],
                   [(SDS((t, k), F32), _row_spec(tm, k))], NT, _copy_epi)[0]


A_TILE = 256


def _a_common(p_ref, lg_ref, lb_ref):
    pv = p_ref[...].astype(F32)
    a = _gelu(pv)
    u, v = a[:, :A_WIDTH], a[:, A_WIDTH:]
    vc = v - jnp.mean(v, axis=-1, keepdims=True)
    rs = lax.rsqrt(jnp.mean(vc * vc, axis=-1, keepdims=True) + EPS)
    vhat = vc * rs
    vn = vhat * lg_ref[...] + lb_ref[...]
    return pv, u, vhat, rs, vn.astype(BF16)


def _tril_weights(w_ref, g):
    r = lax.broadcasted_iota(jnp.int32, (CHUNK, CHUNK), 0)
    c = lax.broadcasted_iota(jnp.int32, (CHUNK, CHUNK), 1)
    return jnp.where(c <= r, w_ref[g], 0.0).astype(BF16), c <= r


def _mixer_a_fwd(proj, lg, lb, w, bias_t):
    t = proj.shape[0]

    def body(p_ref, lg_ref, lb_ref, w_ref, bt_ref, o_ref):
        _, u, _, _, vnb = _a_common(p_ref, lg_ref, lb_ref)
        for g in range(A_GROUPS):
            wt, _ = _tril_weights(w_ref, g)
            cs = slice(g * CHUNK, (g + 1) * CHUNK)
            for ch in range(A_TILE // CHUNK):
                rs_ = slice(ch * CHUNK, (ch + 1) * CHUNK)
                mixed = _dot(wt, vnb[rs_, cs], NN) + bt_ref[:, g:g + 1]
                o_ref[rs_, cs] = (u[rs_, cs] * mixed).astype(BF16)

    return pl.pallas_call(
        body, grid=(t // A_TILE,),
        in_specs=[pl.BlockSpec((A_TILE, 2 * A_WIDTH), lambda i: (i, 0)), _const_spec((1, A_WIDTH)),
                  _const_spec((1, A_WIDTH)), _const_spec((A_GROUPS, CHUNK, CHUNK)), _const_spec((CHUNK, A_GROUPS))],
        out_specs=pl.BlockSpec((A_TILE, A_WIDTH), lambda i: (i, 0)), out_shape=SDS((t, A_WIDTH), BF16),
        name="mixer_a_fwd", compiler_params=_params())(proj, lg, lb, w, bias_t)


def _mixer_a_bwd(proj, dcat, lg, lb, w, bias_t):
    t = proj.shape[0]

    def body(p_ref, da_ref, lg_ref, lb_ref, w_ref, bt_ref, dp_ref, dw_ref, dbt_ref, dlg_ref, dlb_ref, du_scr, dvn_scr):
        @pl.when(pl.program_id(0) == 0)
        def _():
            dw_ref[...] = jnp.zeros_like(dw_ref)
            dbt_ref[...] = jnp.zeros_like(dbt_ref)
            dlg_ref[...] = jnp.zeros_like(dlg_ref)
            dlb_ref[...] = jnp.zeros_like(dlb_ref)

        pv, u, vhat, rs, vnb = _a_common(p_ref, lg_ref, lb_ref)
        da = da_ref[...]
        for g in range(A_GROUPS):
            wt, keep = _tril_weights(w_ref, g)
            cs = slice(g * CHUNK, (g + 1) * CHUNK)
            for ch in range(A_TILE // CHUNK):
                rs_ = slice(ch * CHUNK, (ch + 1) * CHUNK)
                vg = vnb[rs_, cs]
                mixed = _dot(wt, vg, NN) + bt_ref[:, g:g + 1]
                du_scr[rs_, cs] = da[rs_, cs] * mixed
                dmx = da[rs_, cs] * u[rs_, cs]
                dw_ref[g] += jnp.where(keep, _dot(dmx, vg, NT), 0.0)
                dvn_scr[rs_, cs] = _dot(wt, dmx, TN)
                dbt_ref[:, g:g + 1] += jnp.sum(dmx, axis=1, keepdims=True)
        dvn = dvn_scr[...]
        dlg_ref[...] += jnp.sum(dvn * vhat, axis=0, keepdims=True)
        dlb_ref[...] += jnp.sum(dvn, axis=0, keepdims=True)
        dvh = dvn * lg_ref[...]
        dv = rs * (dvh - jnp.mean(dvh, axis=-1, keepdims=True) - vhat * jnp.mean(dvh * vhat, axis=-1, keepdims=True))
        gp = _gelu_grad(pv)
        dp_ref[:, :A_WIDTH] = (du_scr[...] * gp[:, :A_WIDTH]).astype(BF16)
        dp_ref[:, A_WIDTH:] = (dv * gp[:, A_WIDTH:]).astype(BF16)

    return pl.pallas_call(
        body, grid=(t // A_TILE,),
        in_specs=[pl.BlockSpec((A_TILE, 2 * A_WIDTH), lambda i: (i, 0)), pl.BlockSpec((A_TILE, A_WIDTH), lambda i: (i, 0)),
                  _const_spec((1, A_WIDTH)), _const_spec((1, A_WIDTH)), _const_spec((A_GROUPS, CHUNK, CHUNK)),
                  _const_spec((CHUNK, A_GROUPS))],
        out_specs=[pl.BlockSpec((A_TILE, 2 * A_WIDTH), lambda i: (i, 0)), _const_spec((A_GROUPS, CHUNK, CHUNK)),
                   _const_spec((CHUNK, A_GROUPS)), _const_spec((1, A_WIDTH)), _const_spec((1, A_WIDTH))],
        out_shape=[SDS((t, 2 * A_WIDTH), BF16), SDS((A_GROUPS, CHUNK, CHUNK), F32), SDS((CHUNK, A_GROUPS), F32),
                   SDS((1, A_WIDTH), F32), SDS((1, A_WIDTH), F32)],
        scratch_shapes=[pltpu.VMEM((A_TILE, A_WIDTH), F32), pltpu.VMEM((A_TILE, A_WIDTH), F32)],
        name="mixer_a_bwd", compiler_params=_params())(proj, dcat, lg, lb, w, bias_t)


def _rope_tables(t):
    half = ROPE_DIM // 2
    inv_freq = ROPE_THETA ** (-jnp.arange(half, dtype=F32) * 2.0 / ROPE_DIM)
    ang = jnp.arange(t, dtype=F32)[:, None] * inv_freq[None, :]
    cos, sin = jnp.cos(ang), jnp.sin(ang)
    one = jnp.ones((t, HEAD_DIM - ROPE_DIM), F32)
    zero = jnp.zeros((t, HEAD_DIM - ROPE_DIM), F32)
    zh = jnp.zeros((t, half), F32)
    c = jnp.concatenate([cos, cos, one], axis=1)
    s1 = jnp.concatenate([-sin, zh, zero], axis=1)
    s2 = jnp.concatenate([zh, sin, zero], axis=1)
    return tuple(jnp.tile(a, (1, LANES // HEAD_DIM)) for a in (c, s1, s2))


QK_TILE = 512
QK_COLS = 2 * N_DIL * B_WIDTH


CHUNKS = B_WIDTH // LANES


def _regroup_out(scr, first, out_ref, rate, tile):
    rows = tile // rate
    for rho in range(rate):
        for c in range(CHUNKS):
            out_ref[rho, :, c * LANES:(c + 1) * LANES] = scr[first + c, pl.ds(rho, rows, stride=rate), :].astype(out_ref.dtype)


def _regroup_in(x_ref, scr, rate, tile):
    rows = tile // rate
    for rho in range(rate):
        for c in range(CHUNKS):
            scr[c, pl.ds(rho, rows, stride=rate), :] = x_ref[rho, :, c * LANES:(c + 1) * LANES].astype(F32)


def _regrouped_spec(rate, tile):
    return pl.BlockSpec((rate, tile // rate, B_WIDTH), lambda i, *_: (0, i, 0))


def _qk_fwd(proj, gains, tabs, ride=None):
    t = proj.shape[0]
    col0 = 2 * A_WIDTH // 1024
    r1, r2 = DIL_RATES[1], DIL_RATES[2]

    def body(p_ref, g_ref, c_ref, s1_ref, s2_ref, o_ref, q1_ref, q2_ref, k1_ref, k2_ref, scr):
        seg = _segment_mean_matrix(HEAD_DIM)
        c, s1, s2 = c_ref[...], s1_ref[...], s2_ref[...]
        for ci in range(1024 // LANES):
            ls = slice(ci * LANES, (ci + 1) * LANES)
            xv = p_ref[:, ls].astype(F32)
            r = lax.rsqrt(_segment_dot(xv * xv, seg) + EPS)
            y = xv * r * g_ref[:, ls]
            val = y * c + pltpu.roll(y, LANES - 8, axis=1) * s1 + pltpu.roll(y, 8, axis=1) * s2
            o_ref[:, ls] = val.astype(BF16)
            scr[ci] = val

        j = pl.program_id(1)

        @pl.when(j == 0)
        def _():
            _regroup_out(scr, CHUNKS, q1_ref, r1, QK_TILE)

        @pl.when(j == 1)
        def _():
            _regroup_out(scr, 0, q2_ref, r2, QK_TILE)

        @pl.when(j == 2)
        def _():
            _regroup_out(scr, 0, k1_ref, r1, QK_TILE)
            _regroup_out(scr, CHUNKS, k2_ref, r2, QK_TILE)

    tab = pl.BlockSpec((QK_TILE, LANES), lambda i, j: (i, 0))
    g1, g2 = SDS((r1, t // r1, B_WIDTH), BF16), SDS((r2, t // r2, B_WIDTH), BF16)
    s1_, s2_ = _regrouped_spec(r1, QK_TILE), _regrouped_spec(r2, QK_TILE)
    return _call(
        body, grid=(t // QK_TILE, QK_COLS // 1024),
        in_specs=[pl.BlockSpec((QK_TILE, 1024), lambda i, j: (i, col0 + j)), pl.BlockSpec((1, 1024), lambda i, j: (0, j)),
                  tab, tab, tab],
        out_specs=[pl.BlockSpec((QK_TILE, 1024), lambda i, j: (i, j)), s1_, s2_, s1_, s2_],
        out_shape=[SDS((t, QK_COLS), BF16), g1, g2, g1, g2],
        scratch_shapes=[pltpu.VMEM((2 * CHUNKS, QK_TILE, LANES), F32)],
        operands=[proj, gains, *tabs], name="qk_norm_rope_fwd", ride=ride)


PERM_TILE = 512


def _permute(name, items, rate):
    t = items[0][0].shape[0]
    n = len(items)

    def body(*refs):
        scr = refs[-1]
        for x_ref, o_ref in zip(refs[:n], refs[n:2 * n]):
            for ci in range(CHUNKS):
                scr[ci] = x_ref[:, ci * LANES:(ci + 1) * LANES].astype(F32)
            _regroup_out(scr, 0, o_ref, rate, PERM_TILE)

    return pl.pallas_call(
        body, grid=(t // PERM_TILE,),
        in_specs=[pl.BlockSpec((PERM_TILE, B_WIDTH), functools.partial(lambda cb, i: (i, cb), cb)) for _, cb in items],
        out_specs=[_regrouped_spec(rate, PERM_TILE) for _ in items],
        out_shape=[SDS((rate, t // rate, B_WIDTH), a.dtype) for a, _ in items],
        scratch_shapes=[pltpu.VMEM((CHUNKS, PERM_TILE, LANES), F32)],
        name=name, compiler_params=_params())(*[a for a, _ in items])


def _head_lane_mask(h):
    lane = lax.broadcasted_iota(jnp.int32, (1, LANES), 1)
    return (lane < HEAD_DIM) if h == 0 else (lane >= HEAD_DIM)


def _attn_fwd(name, q, k, v, ride=None):
    rate, length = q[0].shape[0], q[0].shape[1]
    nb = length // ATT_BLOCK
    scale = HEAD_DIM ** -0.5

    def body(q_ref, kc_ref, kp_ref, vc_ref, vp_ref, o_ref, l_ref):
        n = pl.program_id(1)
        qi = lax.broadcasted_iota(jnp.int32, (ATT_BLOCK, 2 * ATT_BLOCK), 0)
        cj = lax.broadcasted_iota(jnp.int32, (ATT_BLOCK, 2 * ATT_BLOCK), 1)
        has_prev = jnp.where(n > 0, 0, 2 * ATT_BLOCK)
        mask = ((cj < ATT_BLOCK) & (cj >= qi + has_prev)) | ((cj >= ATT_BLOCK) & (cj - ATT_BLOCK <= qi))
        for hp in range(B_WIDTH // LANES):
            ls = slice(hp * LANES, (hp + 1) * LANES)
            q2 = q_ref[:, ls]
            k2 = jnp.concatenate([kp_ref[:, ls], kc_ref[:, ls]], axis=0)
            v2 = jnp.concatenate([vp_ref[:, ls], vc_ref[:, ls]], axis=0)
            o_acc, lse2 = None, None
            for h in range(2):
                hm = _head_lane_mask(h)
                s = _dot(jnp.where(hm, q2, jnp.zeros_like(q2)), k2, NT) * scale
                s = jnp.where(mask, s, NEG_INF)
                m = jnp.max(s, axis=1, keepdims=True)
                p = jnp.exp(s - m)
                den = jnp.sum(p, axis=1, keepdims=True)
                lse = m + jnp.log(den)
                o = _dot(p / den, jnp.where(hm, v2, jnp.zeros_like(v2)), NN)
                o_acc = o if h == 0 else o_acc + o
                lse_b = lse + jnp.zeros((ATT_BLOCK, LANES), F32)
                lse2 = lse_b if h == 0 else jnp.where(hm, lse_b, lse2)
            o_ref[:, ls] = o_acc
            l_ref[:, ls] = lse2

    def cur(cb):
        return pl.BlockSpec((None, ATT_BLOCK, B_WIDTH), lambda r, n: (r, n, cb))

    def prev(cb):
        return pl.BlockSpec((None, ATT_BLOCK, B_WIDTH), lambda r, n: (r, jnp.maximum(n - 1, 0), cb))

    out = pl.BlockSpec((None, ATT_BLOCK, B_WIDTH), lambda r, n: (r, n, 0))
    return _call(
        body, grid=(rate, nb),
        in_specs=[cur(q[1]), cur(k[1]), prev(k[1]), cur(v[1]), prev(v[1])],
        out_specs=[out, out], out_shape=[SDS((rate, length, B_WIDTH), F32)] * 2,
        operands=[q[0], k[0], k[0], v[0], v[0]], name=name, ride=ride)


def _attn_merge(a_out, o_list, l_list):
    t = a_out.shape[0]
    tm = PERM_TILE
    r1, r2 = DIL_RATES[1], DIL_RATES[2]

    def body(a_ref, o0, o1, o2, l0, l1, l2, cat_ref, lt_ref, lt1_ref, lt2_ref, so1, so2, sl1, sl2, slt):
        _regroup_in(o1, so1, r1, tm)
        _regroup_in(l1, sl1, r1, tm)
        _regroup_in(o2, so2, r2, tm)
        _regroup_in(l2, sl2, r2, tm)
        cat_ref[:, :A_WIDTH] = a_ref[...]
        for c in range(CHUNKS):
            ls = slice(c * LANES, (c + 1) * LANES)
            lg = [l0[:, ls], sl1[c], sl2[c]]
            m = jnp.maximum(jnp.maximum(lg[0], lg[1]), lg[2])
            es = [jnp.exp(l - m) for l in lg]
            den = es[0] + es[1] + es[2]
            b = (es[0] * o0[:, ls] + es[1] * so1[c] + es[2] * so2[c]) / den
            cat_ref[:, A_WIDTH + c * LANES:A_WIDTH + (c + 1) * LANES] = b.astype(BF16)
            lt = m + jnp.log(den)
            lt_ref[:, ls] = lt
            slt[c] = lt
        _regroup_out(slt, 0, lt1_ref, r1, tm)
        _regroup_out(slt, 0, lt2_ref, r2, tm)

    blk = _row_spec(tm, B_WIDTH)
    g1, g2 = _regrouped_spec(r1, tm), _regrouped_spec(r2, tm)
    return pl.pallas_call(
        body, grid=(t // tm,), in_specs=[blk, blk, g1, g2, blk, g1, g2],
        out_specs=[_row_spec(tm, A_WIDTH + B_WIDTH), blk, g1, g2],
        out_shape=[SDS((t, A_WIDTH + B_WIDTH), BF16), SDS((t, B_WIDTH), F32), SDS((r1, t // r1, B_WIDTH), F32),
                   SDS((r2, t // r2, B_WIDTH), F32)],
        scratch_shapes=[pltpu.VMEM((CHUNKS, tm, LANES), F32)] * 5,
        name="attn_merge", compiler_params=_params())(a_out, *o_list, *l_list)


def _attn_bwd_prep(dcat, cat):
    t = dcat.shape[0]
    tm = PERM_TILE
    r1, r2 = DIL_RATES[1], DIL_RATES[2]

    def body(d_ref, b_ref, db_ref, dd_ref, db1_ref, dd1_ref, db2_ref, dd2_ref, sdb, sdd):
        seg = _segment_mean_matrix(HEAD_DIM, scale=1.0)
        for c in range(CHUNKS):
            ls = slice(c * LANES, (c + 1) * LANES)
            d = d_ref[:, ls]
            dsum = _segment_dot(d * b_ref[:, ls].astype(F32), seg)
            db_ref[:, ls] = d.astype(BF16)
            dd_ref[:, ls] = dsum
            sdb[c] = d
            sdd[c] = dsum
        _regroup_out(sdb, 0, db1_ref, r1, tm)
        _regroup_out(sdd, 0, dd1_ref, r1, tm)
        _regroup_out(sdb, 0, db2_ref, r2, tm)
        _regroup_out(sdd, 0, dd2_ref, r2, tm)

    right = pl.BlockSpec((tm, B_WIDTH), lambda i: (i, 1))
    blk = _row_spec(tm, B_WIDTH)
    g1, g2 = _regrouped_spec(r1, tm), _regrouped_spec(r2, tm)
    return pl.pallas_call(
        body, grid=(t // tm,), in_specs=[right, right], out_specs=[blk, blk, g1, g1, g2, g2],
        out_shape=[SDS((t, B_WIDTH), BF16), SDS((t, B_WIDTH), F32), SDS((r1, t // r1, B_WIDTH), BF16),
                   SDS((r1, t // r1, B_WIDTH), F32), SDS((r2, t // r2, B_WIDTH), BF16), SDS((r2, t // r2, B_WIDTH), F32)],
        scratch_shapes=[pltpu.VMEM((CHUNKS, tm, LANES), F32)] * 2,
        name="attn_bwd_prep", compiler_params=_params())(dcat, cat)


def _attn_bwd(name, q, k, v, db, lse, dd, ride=None):
    rate, length = db.shape[0], db.shape[1]
    nb = length // ATT_BLOCK
    scale = HEAD_DIM ** -0.5

    def body(qa_ref, qb_ref, k_ref, v_ref, dba_ref, dbb_ref, la_ref, lb_ref, da_ref, dbd_ref, dq_ref, dk_ref, dv_ref, carry):
        m = pl.program_id(1)

        @pl.when(m == 0)
        def _():
            carry[...] = jnp.zeros_like(carry)

        row = lax.broadcasted_iota(jnp.int32, (2 * ATT_BLOCK, ATT_BLOCK), 0)
        kj = lax.broadcasted_iota(jnp.int32, (2 * ATT_BLOCK, ATT_BLOCK), 1)
        no_next = jnp.where(m + 1 < nb, 0, 2 * ATT_BLOCK)
        mask = ((row < ATT_BLOCK) & (kj <= row)) | ((row >= ATT_BLOCK) & (kj >= row - ATT_BLOCK + no_next))
        for hp in range(B_WIDTH // LANES):
            ls = slice(hp * LANES, (hp + 1) * LANES)
            k2, v2 = k_ref[:, ls], v_ref[:, ls]
            q2 = jnp.concatenate([qa_ref[:, ls], qb_ref[:, ls]], axis=0)
            db2 = jnp.concatenate([dba_ref[:, ls], dbb_ref[:, ls]], axis=0)
            lse2 = jnp.concatenate([la_ref[:, ls], lb_ref[:, ls]], axis=0)
            dd2 = jnp.concatenate([da_ref[:, ls], dbd_ref[:, ls]], axis=0)
            dq_acc, dk_acc, dv_acc = None, None, None
            for h in range(2):
                hm = _head_lane_mask(h)
                km = jnp.where(hm, k2, jnp.zeros_like(k2))
                vm = jnp.where(hm, v2, jnp.zeros_like(v2))
                lse_col = jnp.max(jnp.where(hm, lse2, NEG_INF), axis=1, keepdims=True)
                dd_col = jnp.max(jnp.where(hm, dd2, NEG_INF), axis=1, keepdims=True)
                s = _dot(q2, km, NT) * scale
                p = jnp.where(mask, jnp.exp(s - lse_col), 0.0)
                dvc = _dot(p, jnp.where(hm, db2, jnp.zeros_like(db2)), TN)
                dp = _dot(db2, vm, NT)
                ds = (p * (dp - dd_col) * scale).astype(BF16)
                dqc = _dot(ds, km, NN)
                dkc = _dot(ds, jnp.where(hm, q2, jnp.zeros_like(q2)), TN)
                dq_acc = dqc if dq_acc is None else dq_acc + dqc
                dk_acc = dkc if dk_acc is None else dk_acc + dkc
                dv_acc = dvc if dv_acc is None else dv_acc + dvc
            dq_ref[:, ls] = (dq_acc[:ATT_BLOCK] + carry[:, ls]).astype(BF16)
            carry[:, ls] = dq_acc[ATT_BLOCK:]
            dk_ref[:, ls] = dk_acc.astype(BF16)
            dv_ref[:, ls] = dv_acc.astype(BF16)

    def cur(cb):
        return pl.BlockSpec((None, ATT_BLOCK, B_WIDTH), lambda r, n: (r, n, cb))

    def nxt(cb):
        return pl.BlockSpec((None, ATT_BLOCK, B_WIDTH), lambda r, n: (r, jnp.minimum(n + 1, nb - 1), cb))

    out = cur(0)
    return _call(
        body, grid=(rate, nb),
        in_specs=[cur(q[1]), nxt(q[1]), cur(k[1]), cur(v[1]), cur(0), nxt(0), cur(0), nxt(0), cur(0), nxt(0)],
        out_specs=[out, out, out], out_shape=[SDS((rate, length, B_WIDTH), BF16)] * 3,
        scratch_shapes=[pltpu.VMEM((ATT_BLOCK, B_WIDTH), F32)],
        operands=[q[0], q[0], k[0], v[0], db, db, lse, lse, dd, dd], name=name, ride=ride)


AB_IN = 2 * A_WIDTH + 3 * N_DIL * B_WIDTH
ASM_TILE = 256


def _dproj_assemble(proj, d_a, dq, dk, dv, gains, tabs):
    t = proj.shape[0]
    n_in = 3 * N_DIL

    def body(p_ref, da_ref, *rest):
        grads = rest[:n_in]
        g_ref, c_ref, s1_ref, s2_ref, o_ref, dg_ref = rest[n_in:n_in + 6]
        scratch = rest[n_in + 6:]

        @pl.when(pl.program_id(0) == 0)
        def _():
            dg_ref[...] = jnp.zeros_like(dg_ref)

        chunk = {}
        k_scr = 0
        for j in range(n_in):
            g = j % N_DIL
            if DIL_RATES[g] == 1:
                for ci in range(CHUNKS):
                    chunk[j, ci] = functools.partial(lambda r, ci: r[:, ci * LANES:(ci + 1) * LANES].astype(F32), grads[j], ci)
            else:
                scr = scratch[k_scr]
                k_scr += 1
                _regroup_in(grads[j], scr, DIL_RATES[g], ASM_TILE)
                for ci in range(CHUNKS):
                    chunk[j, ci] = functools.partial(lambda s, ci: s[ci], scr, ci)

        seg = _segment_mean_matrix(HEAD_DIM)
        c, s1, s2 = c_ref[...], s1_ref[...], s2_ref[...]
        o_ref[:, :2 * A_WIDTH] = da_ref[...]
        for jg in range(2 * N_DIL):
            for ci in range(CHUNKS):
                col = jg * B_WIDTH + ci * LANES
                src = slice(2 * A_WIDTH + col, 2 * A_WIDTH + col + LANES)
                xv = p_ref[:, src].astype(F32)
                r = lax.rsqrt(_segment_dot(xv * xv, seg) + EPS)
                xh = xv * r
                gain = g_ref[:, col:col + LANES]
                do = chunk[jg, ci]()
                dy = do * c + pltpu.roll(do * s1, 8, axis=1) + pltpu.roll(do * s2, LANES - 8, axis=1)
                dg_ref[:, col:col + LANES] += jnp.sum(dy * xh, axis=0, keepdims=True)
                dxh = dy * gain
                o_ref[:, src] = (r * (dxh - xh * _segment_dot(dxh * xh, seg))).astype(BF16)
        v0 = 2 * A_WIDTH + QK_COLS
        for g in range(N_DIL):
            for ci in range(CHUNKS):
                col = v0 + g * B_WIDTH + ci * LANES
                o_ref[:, col:col + LANES] = chunk[2 * N_DIL + g, ci]().astype(BF16)

    specs = [_row_spec(ASM_TILE, B_WIDTH) if r == 1 else _regrouped_spec(r, ASM_TILE) for r in DIL_RATES] * 3
    n_scr = 3 * sum(1 for r in DIL_RATES if r > 1)
    tab = _row_spec(ASM_TILE, LANES)
    return pl.pallas_call(
        body, grid=(t // ASM_TILE,),
        in_specs=[_row_spec(ASM_TILE, AB_IN), _row_spec(ASM_TILE, 2 * A_WIDTH)] + specs
        + [_const_spec((1, QK_COLS)), tab, tab, tab],
        out_specs=[_row_spec(ASM_TILE, AB_IN), _const_spec((1, QK_COLS))],
        out_shape=[SDS((t, AB_IN), BF16), SDS((1, QK_COLS), F32)],
        scratch_shapes=[pltpu.VMEM((CHUNKS, ASM_TILE, LANES), F32)] * n_scr,
        name="dproj_assemble", compiler_params=_params())(proj, d_a, *dq, *dk, *dv, gains, *tabs)


def _fold_heads(dg_lane):
    n = dg_lane.shape[1]

    def body(x_ref, o_ref):
        r = lax.broadcasted_iota(jnp.int32, (B_WIDTH, B_WIDTH), 0) % HEAD_DIM
        c = lax.broadcasted_iota(jnp.int32, (B_WIDTH, B_WIDTH), 1) % HEAD_DIM
        fold = jnp.where(r == c, 1.0, 0.0).astype(F32)
        for jg in range(n // B_WIDTH):
            ls = slice(jg * B_WIDTH, (jg + 1) * B_WIDTH)
            o_ref[:, ls] = _dot_hi(jnp.broadcast_to(x_ref[:, ls], (8, B_WIDTH)), fold)

    return pl.pallas_call(body, out_shape=SDS((8, n), F32), name="fold_heads", compiler_params=_params())(dg_lane)


CD_TILE = 256
CD_IN = 2 * C_WIDTH + 3 * 512


def _cd_split(pv):
    w = C_WIDTH
    return pv[:, :w], pv[:, w:2 * w], pv[:, 2 * w:3 * w], pv[:, 3 * w:4 * w], pv[:, 4 * w:5 * w]


def _shifted_copies(src, dst, rows):
    dst[0, :rows] = src[...]
    for b in range(1, 8):
        dst[b, :rows - 8] = src[pl.ds(b, rows - 8), :]


def _rows_from(shifted, start, n):
    b = start % 8
    return shifted[b, pl.ds(start - b, n), :]


def _mixer_cd_fwd(proj, cw, cb, lg, lb, dw):
    t = proj.shape[0]
    per = CD_TILE // HALO

    def body(h_ref, m_ref, cw_ref, cb_ref, lg_ref, lb_ref, dw_ref, o_ref, c1_ref, c_scr, e_scr, c_sh):
        not_first = (pl.program_id(0) > 0).astype(F32)
        ha, hg, _, hgc, hhv = _cd_split(h_ref[...].astype(F32))
        ma, mg, mgb, mgc, mhv = _cd_split(m_ref[...].astype(F32))
        c_scr[:HALO] = ha * _sigmoid(hg) * not_first
        c_scr[HALO:] = ma * _sigmoid(mg)
        e_scr[:HALO] = hgc * hhv * not_first
        e_scr[HALO:] = mgc * mhv
        _shifted_copies(c_scr, c_sh, HALO + CD_TILE)
        acc = jnp.zeros((CD_TILE, C_WIDTH), F32)
        for k in range(C_KERNEL):
            acc = acc + cw_ref[k:k + 1, :] * _rows_from(c_sh, HALO - (C_KERNEL - 1) + k, CD_TILE)
        c1 = acc + cb_ref[...]
        c1_ref[...] = c1
        cc = c1 - jnp.mean(c1, axis=-1, keepdims=True)
        c2 = cc * lax.rsqrt(jnp.mean(cc * cc, axis=-1, keepdims=True) + EPS) * lg_ref[...] + lb_ref[...]
        o_ref[:, :C_WIDTH] = (c2 * _sigmoid(c2)).astype(BF16)
        d1 = jnp.zeros((CD_TILE, C_WIDTH), F32)
        for k in range(D_KERNEL):
            d1 = d1 + dw_ref[k:k + 1, :] * e_scr[pl.ds(HALO - (D_KERNEL - 1) + k, CD_TILE), :]
        o_ref[:, C_WIDTH:] = (mgb * d1).astype(BF16)

    return pl.pallas_call(
        body, grid=(t // CD_TILE,),
        in_specs=[pl.BlockSpec((HALO, CD_IN), lambda i: (jnp.maximum(i * per - 1, 0), 0)), _row_spec(CD_TILE, CD_IN),
                  _const_spec((32, C_WIDTH)), _const_spec((1, C_WIDTH)), _const_spec((1, C_WIDTH)), _const_spec((1, C_WIDTH)),
                  _const_spec((8, C_WIDTH))],
        out_specs=[_row_spec(CD_TILE, 2 * C_WIDTH), _row_spec(CD_TILE, C_WIDTH)],
        out_shape=[SDS((t, 2 * C_WIDTH), BF16), SDS((t, C_WIDTH), F32)],
        scratch_shapes=[pltpu.VMEM((HALO + CD_TILE, C_WIDTH), F32)] * 2 + [pltpu.VMEM((8, HALO + CD_TILE, C_WIDTH), F32)],
        name="mixer_cd_fwd", compiler_params=_params())(proj, proj, cw, cb, lg, lb, dw)


def _mixer_cd_bwd(proj, dcat, c1, cw, lg, lb, dw, ride=None):
    t = proj.shape[0]
    per = CD_TILE // HALO
    nt = t // CD_TILE
    ext = CD_TILE + HALO

    def body(hp_ref, m_ref, hn_ref, dm_ref, dn_ref, c1m_ref, c1n_ref, cw_ref, lg_ref, lb_ref, dw_ref,
             dp_ref, dcw_ref, dcb_ref, dlg_ref, dlb_ref, ddw_ref, c_scr, e_scr, dc1_scr, dd1_scr, c_sh, dc1_sh):
        i = pl.program_id(0)

        @pl.when(i == 0)
        def _():
            for r in (dcw_ref, dcb_ref, dlg_ref, dlb_ref, ddw_ref):
                r[...] = jnp.zeros_like(r)

        not_first = (i > 0).astype(F32)
        not_last = (i < nt - 1).astype(F32)
        pa, pg, _, pgc, phv = _cd_split(hp_ref[...].astype(F32))
        ma, mg, mgb, mgc, mhv = _cd_split(m_ref[...].astype(F32))
        na, ng, ngb, ngc, nhv = _cd_split(hn_ref[...].astype(F32))
        sig_m = _sigmoid(mg)
        c_scr[:HALO] = pa * _sigmoid(pg) * not_first
        c_scr[HALO:HALO + CD_TILE] = ma * sig_m
        c_scr[HALO + CD_TILE:] = na * _sigmoid(ng) * not_last
        e_scr[:HALO] = pgc * phv * not_first
        e_scr[HALO:HALO + CD_TILE] = mgc * mhv
        e_scr[HALO + CD_TILE:] = ngc * nhv * not_last

        _shifted_copies(c_scr, c_sh, 2 * HALO + CD_TILE)
        c1 = jnp.concatenate([c1m_ref[...], c1n_ref[...]], axis=0)
        cc = c1 - jnp.mean(c1, axis=-1, keepdims=True)
        rs = lax.rsqrt(jnp.mean(cc * cc, axis=-1, keepdims=True) + EPS)
        vhat = cc * rs
        c2 = vhat * lg_ref[...] + lb_ref[...]
        sig = _sigmoid(c2)
        dc = jnp.concatenate([dm_ref[:, :C_WIDTH], dn_ref[:, :C_WIDTH] * not_last], axis=0)
        dc2 = dc * (sig * (1.0 + c2 * (1.0 - sig)))
        dvh = dc2 * lg_ref[...]
        dc1 = rs * (dvh - jnp.mean(dvh, axis=-1, keepdims=True) - vhat * jnp.mean(dvh * vhat, axis=-1, keepdims=True))
        dc1_scr[...] = dc1
        _shifted_copies(dc1_scr, dc1_sh, ext)
        dlg_ref[...] += jnp.sum((dc2 * vhat)[:CD_TILE], axis=0, keepdims=True)
        dlb_ref[...] += jnp.sum(dc2[:CD_TILE], axis=0, keepdims=True)
        dc1_m = dc1[:CD_TILE]
        dcb_ref[...] += jnp.sum(dc1_m, axis=0, keepdims=True)
        dc0 = jnp.zeros((CD_TILE, C_WIDTH), F32)
        for k in range(C_KERNEL):
            dc0 = dc0 + cw_ref[k:k + 1, :] * _rows_from(dc1_sh, C_KERNEL - 1 - k, CD_TILE)
            dcw_ref[k:k + 1, :] += jnp.sum(dc1_m * _rows_from(c_sh, HALO - (C_KERNEL - 1) + k, CD_TILE), axis=0, keepdims=True)
        dp_ref[:, :C_WIDTH] = (dc0 * sig_m).astype(BF16)
        dp_ref[:, C_WIDTH:2 * C_WIDTH] = (dc0 * ma * sig_m * (1.0 - sig_m)).astype(BF16)

        d1 = jnp.zeros((CD_TILE, C_WIDTH), F32)
        for k in range(D_KERNEL):
            d1 = d1 + dw_ref[k:k + 1, :] * e_scr[pl.ds(HALO - (D_KERNEL - 1) + k, CD_TILE), :]
        dd_m = dm_ref[:, C_WIDTH:]
        dd1 = jnp.concatenate([dd_m * mgb, dn_ref[:, C_WIDTH:] * ngb * not_last], axis=0)
        dd1_scr[...] = dd1
        dp_ref[:, 2 * C_WIDTH:3 * C_WIDTH] = (dd_m * d1).astype(BF16)
        de = jnp.zeros((CD_TILE, C_WIDTH), F32)
        for k in range(D_KERNEL):
            de = de + dw_ref[k:k + 1, :] * dd1_scr[pl.ds(D_KERNEL - 1 - k, CD_TILE), :]
            ddw_ref[k:k + 1, :] += jnp.sum(dd1[:CD_TILE] * e_scr[pl.ds(HALO - (D_KERNEL - 1) + k, CD_TILE), :], axis=0, keepdims=True)
        dp_ref[:, 3 * C_WIDTH:4 * C_WIDTH] = (de * mhv).astype(BF16)
        dp_ref[:, 4 * C_WIDTH:] = (de * mgc).astype(BF16)

    halo_prev = lambda i: (jnp.maximum(i * per - 1, 0), 0)
    halo_next = lambda i: (jnp.minimum((i + 1) * per, t // HALO - 1), 0)
    vec = _const_spec((1, C_WIDTH))
    return _call(
        body, grid=(nt,),
        in_specs=[pl.BlockSpec((HALO, CD_IN), halo_prev), _row_spec(CD_TILE, CD_IN), pl.BlockSpec((HALO, CD_IN), halo_next),
                  _row_spec(CD_TILE, 2 * C_WIDTH), pl.BlockSpec((HALO, 2 * C_WIDTH), halo_next),
                  _row_spec(CD_TILE, C_WIDTH), pl.BlockSpec((HALO, C_WIDTH), halo_next),
                  _const_spec((32, C_WIDTH)), vec, vec, _const_spec((8, C_WIDTH))],
        out_specs=[_row_spec(CD_TILE, CD_IN), _const_spec((32, C_WIDTH)), vec, vec, vec, _const_spec((8, C_WIDTH))],
        out_shape=[SDS((t, CD_IN), BF16), SDS((32, C_WIDTH), F32), SDS((1, C_WIDTH), F32), SDS((1, C_WIDTH), F32),
                   SDS((1, C_WIDTH), F32), SDS((8, C_WIDTH), F32)],
        scratch_shapes=[pltpu.VMEM((2 * HALO + CD_TILE, C_WIDTH), F32)] * 2 + [pltpu.VMEM((ext, C_WIDTH), F32)] * 2
        + [pltpu.VMEM((8, 2 * HALO + CD_TILE, C_WIDTH), F32), pltpu.VMEM((8, ext, C_WIDTH), F32)],
        operands=[proj, proj, proj, dcat, dcat, c1, c1, cw, lg, lb, dw], name="mixer_cd_bwd", ride=ride)


def _wgrad(name, pairs, out_rc, t, ride):
    tk = TILES["wgrad"]
    r, c = out_rc
    n = len(pairs)

    def body(*refs):
        ab, out_refs = refs[:2 * n], refs[2 * n:]
        k = pl.program_id(1)
        parts = [_dot(ab[2 * j][...], ab[2 * j + 1][...], TN) for j in range(n)]

        @pl.when(k == 0)
        def _():
            for a in range(n):
                out_refs[a][...] = parts[a]

        @pl.when(k > 0)
        def _():
            for a in range(n):
                out_refs[a][...] += parts[a]

    operands, in_specs = [], []
    for lhs, lhs_spec, rhs, rhs_spec in pairs:
        operands += [lhs, rhs]
        in_specs += [lhs_spec, rhs_spec]
    res = _call(body, grid=(N_CHIPS, t // tk), in_specs=in_specs,
                out_specs=[pl.BlockSpec((None, r, c), lambda p, k: (p, 0, 0))] * n,
                out_shape=[SDS((N_CHIPS, r, c), F32)] * n, operands=operands, name=name, ride=ride)
    outs, ride_res = (res, None) if ride is None else res
    outs = [o.reshape(N_CHIPS, 2, r // 2, c) for o in outs]
    return outs if ride is None else (outs, ride_res)


def _wgrad_col_sharded(name, h, dz_list, three_d, ride=None):
    t, d = h.shape
    tk = TILES["wgrad"]
    n4 = dz_list[0].shape[-1] if three_d else dz_list[0].shape[-1] // N_CHIPS
    hs = pl.BlockSpec((tk, d), lambda p, k: (k, 0))
    zs = pl.BlockSpec((None, tk, n4), lambda p, k: (p, k, 0)) if three_d else pl.BlockSpec((tk, n4), lambda p, k: (k, p))
    return _wgrad(name, [(h, hs, dz, zs) for dz in dz_list], (d, n4), t, ride)


def _wgrad_row_sharded(name, a, g, three_d, ride=None):
    many = isinstance(a, (list, tuple))
    a_list = list(a) if many else [a]
    t, d = g.shape
    tk = TILES["wgrad"]
    k4 = a_list[0].shape[-1] if three_d else a_list[0].shape[-1] // N_CHIPS
    a_spec = pl.BlockSpec((None, tk, k4), lambda p, k: (p, k, 0)) if three_d else pl.BlockSpec((tk, k4), lambda p, k: (k, p))
    gs = pl.BlockSpec((tk, d), lambda p, k: (k, 0))
    res = _wgrad(name, [(a_j, a_spec, g, gs) for a_j in a_list], (k4, d), t, ride)
    if many:
        return res
    return res[0] if ride is None else (res[0][0], res[1])


def _mesh_scalars():
    return jnp.stack([lax.axis_index("c"), 2 * lax.axis_index("x") + lax.axis_index("y")]).astype(jnp.int32)


def _stage_own(name, w, layer, dtype):
    layers, r, cols = w.shape
    h = r // 2

    def body(s_ref, x_ref, o_ref):
        o_ref[...] = x_ref[...].astype(dtype)

    return pl.pallas_call(
        body,
        grid_spec=pltpu.PrefetchScalarGridSpec(
            num_scalar_prefetch=1, grid=(2,),
            in_specs=[pl.BlockSpec((None, h, cols), lambda i, s: (2 * layer + i, 0, 0))],
            out_specs=pl.BlockSpec((None, None, h, cols), lambda i, s: (s[1], i, 0, 0))),
        out_shape=SDS((N_CHIPS, 2, h, cols), dtype), name=name,
        compiler_params=_params())(_mesh_scalars(), w.reshape(2 * layers, h, cols))


def _remote(src, dst, send_sem, recv_sem, device):
    return pltpu.make_async_remote_copy(src, dst, send_sem, recv_sem, device_id=device, device_id_type=MESH)


def _ride_gather_send(bufs):
    n = len(bufs)

    def each(b, sems, act):
        send, recv = sems
        x, y, c, p, others = _position()
        for t in range(n):
            for j, (qx, qy) in enumerate(others):
                act(b[t].at[p, c], b[t].at[2 * qx + qy, c], send.at[t, j], recv.at[t, j], (qx, qy, c))

    def start(ins, b, new, sems):
        each(b, sems, lambda mine, landed, s, r, dev: _remote(mine, mine, s, r, dev).start())

    def finish(ins, b, new, sems):
        def act(mine, landed, s, r, dev):
            _remote(mine, mine, s, r, dev).wait_send()
            _remote(landed, landed, s, r, dev).wait_recv()
        each(b, sems, act)

    return _Ride([], bufs, [], [(n, 3), (n, 3)], start, finish)


def _ride_gather_pass(bufs):
    n = len(bufs)

    def each(b, sems, act):
        send, recv = sems
        x, y, c, p, others = _position()
        for t in range(n):
            for j, (qx, qy) in enumerate(others):
                act(b[t].at[2 * qx + qy, c], b[t].at[2 * qx + qy, 1 - c], send.at[t, j], recv.at[t, j], (x, y, 1 - c))

    def start(ins, b, new, sems):
        each(b, sems, lambda landed, passed, s, r, dev: _remote(landed, landed, s, r, dev).start())

    def finish(ins, b, new, sems):
        def act(landed, passed, s, r, dev):
            _remote(landed, landed, s, r, dev).wait_send()
            _remote(passed, passed, s, r, dev).wait_recv()
        each(b, sems, act)

    return _Ride([], bufs, [], [(n, 3), (n, 3)], start, finish)


def _ride_swap(tensors):
    n = len(tensors)

    def each(ins, new, sems, act):
        send, recv = sems
        x, y, c, _, _ = _position()
        for t in range(n):
            act(_remote(ins[t].at[:, 1 - c], new[t], send.at[t], recv.at[t], (x, y, 1 - c)))

    def start(ins, b, new, sems):
        each(ins, new, sems, lambda cp: cp.start())

    def finish(ins, b, new, sems):
        each(ins, new, sems, lambda cp: cp.wait())

    return _Ride(tensors, [], [SDS((s.shape[0],) + s.shape[2:], s.dtype) for s in tensors], [(n,), (n,)], start, finish)


def _ride_scatter(tensors, landing):
    n = len(tensors)

    def each(ins, b, sems, act):
        send, recv = sems
        x, y, c, p, others = _position()
        for t in range(n):
            for j, (qx, qy) in enumerate(others):
                q = 2 * qx + qy
                act(ins[t].at[q], b[t].at[p], b[t].at[q], send.at[t, j], recv.at[t, j], (qx, qy, c))

    def start(ins, b, new, sems):
        each(ins, b, sems, lambda src, dst, landed, s, r, dev: _remote(src, dst, s, r, dev).start())

    def finish(ins, b, new, sems):
        def act(src, dst, landed, s, r, dev):
            _remote(src, dst, s, r, dev).wait_send()
            _remote(landed, landed, s, r, dev).wait_recv()
        each(ins, b, sems, act)

    return _Ride(tensors, landing, [], [(n, 3), (n, 3)], start, finish)


def _ride_join(bufs):
    n = len(bufs)

    def each(b, sems, act):
        send, recv = sems
        x, y, c, _, _ = _position()
        for t in range(n):
            act(b[t].at[c], b[t].at[1 - c], send.at[t], recv.at[t], (x, y, 1 - c))

    def start(ins, b, new, sems):
        each(b, sems, lambda mine, theirs, s, r, dev: _remote(mine, mine, s, r, dev).start())

    def finish(ins, b, new, sems):
        def act(mine, theirs, s, r, dev):
            _remote(mine, mine, s, r, dev).wait_send()
            _remote(theirs, theirs, s, r, dev).wait_recv()
        each(b, sems, act)

    return _Ride([], bufs, [], [(n,), (n,)], start, finish)


def _all_reduce_small(pack):
    rows = pack.shape[0]
    n_dev = 2 * N_CHIPS

    def body(x_ref, o_ref, land, send, recv):
        x, y, c, p, _ = _position()
        me = 2 * p + c
        land[me] = x_ref[...]
        peers = [(dx, dy, dc) for dx in range(2) for dy in range(2) for dc in range(2) if (dx, dy, dc) != (0, 0, 0)]
        for j, (dx, dy, dc) in enumerate(peers):
            _remote(land.at[me], land.at[me], send.at[j], recv.at[j], (x ^ dx, y ^ dy, c ^ dc)).start()
        for j, (dx, dy, dc) in enumerate(peers):
            src = 4 * (x ^ dx) + 2 * (y ^ dy) + (c ^ dc)
            _remote(land.at[me], land.at[me], send.at[j], recv.at[j], (x ^ dx, y ^ dy, c ^ dc)).wait_send()
            _remote(land.at[src], land.at[src], send.at[j], recv.at[j], (x ^ dx, y ^ dy, c ^ dc)).wait_recv()
        acc = land[0]
        for dev in range(1, n_dev):
            acc = acc + land[dev]
        o_ref[...] = acc

    return pl.pallas_call(
        body, out_shape=SDS((rows, LANES), F32),
        scratch_shapes=[pltpu.VMEM((n_dev, rows, LANES), F32), pltpu.SemaphoreType.DMA((n_dev - 1,)),
                        pltpu.SemaphoreType.DMA((n_dev - 1,))],
        name="all_reduce_small", compiler_params=_params())(pack)


def _add_own_half(name, full, recv, out_dtype):
    n4, _, h, cols = full.shape

    def body(s_ref, a_ref, b_ref, o_ref, own_ref):
        v = (a_ref[...] + b_ref[...]).astype(out_dtype)
        o_ref[...] = v

        @pl.when(pl.program_id(0) == s_ref[1])
        def _():
            own_ref[...] = v

    return pl.pallas_call(
        body,
        grid_spec=pltpu.PrefetchScalarGridSpec(
            num_scalar_prefetch=1, grid=(n4,),
            in_specs=[pl.BlockSpec((None, None, h, cols), lambda q, s: (q, s[0], 0, 0)),
                      pl.BlockSpec((None, h, cols), lambda q, s: (q, 0, 0))],
            out_specs=[pl.BlockSpec((None, h, cols), lambda q, s: (q, 0, 0)),
                       pl.BlockSpec((None, h, cols), lambda q, s: (s[1], 0, 0))]),
        out_shape=[SDS((n4, h, cols), out_dtype)] * 2, name=name, compiler_params=_params())(_mesh_scalars(), full, recv)


def _sum_chips(name, parts):
    n4, h, cols = parts.shape
    th = h // 4 if h % 64 == 0 else h

    def body(s_ref, a_ref, o_ref):
        acc = a_ref[0].astype(F32)
        for q in range(1, n4):
            acc = acc + a_ref[q].astype(F32)
        o_ref[...] = acc

    return pl.pallas_call(
        body,
        grid_spec=pltpu.PrefetchScalarGridSpec(
            num_scalar_prefetch=1, grid=(h // th,),
            in_specs=[pl.BlockSpec((n4, th, cols), lambda i, s: (0, i, 0))],
            out_specs=pl.BlockSpec((None, th, cols), lambda i, s: (s[0], i, 0))),
        out_shape=SDS((2, h, cols), F32), name=name, compiler_params=_params())(_mesh_scalars(), parts)


def _adamw_math(w, g, m, v):
    m2 = ADAM_B1 * m + (1.0 - ADAM_B1) * g
    v2 = ADAM_B2 * v + (1.0 - ADAM_B2) * (g * g)
    m_hat = m2 / (1.0 - ADAM_B1 ** ADAM_STEP)
    v_hat = v2 / (1.0 - ADAM_B2 ** ADAM_STEP)
    delta = -ADAM_LR * (m_hat / (jnp.sqrt(v_hat) + ADAM_EPS) + ADAM_WD * w)
    return delta, m2, v2


def _row_tile(rows, cols):
    cap = max(8, (1 << 18) // cols)
    best = 8
    for cand in range(8, min(rows, cap) + 1, 8):
        if rows % cand == 0:
            best = cand
    return best


def _adamw_big(name, w, g_layers, m, v):
    layers, rows, cols = w.shape
    tr = _row_tile(rows, cols)

    def body(w_ref, m_ref, v_ref, *rest):
        g_refs, (g_o, d_o, m_o, v_o) = rest[:layers], rest[layers:]
        gv = g_refs[0][...]
        for layer in range(1, layers):
            gv = jnp.where(pl.program_id(0) == layer, g_refs[layer][...], gv)
        d, mm, vv = _adamw_math(w_ref[...], gv, m_ref[...], v_ref[...])
        g_o[...] = gv
        d_o[...] = d
        m_o[...] = mm
        v_o[...] = vv

    blk = pl.BlockSpec((None, tr, cols), lambda l, i: (l, i, 0))
    g_blk = pl.BlockSpec((tr, cols), lambda l, i: (i, 0))
    return tuple(pl.pallas_call(
        body, grid=(layers, rows // tr), in_specs=[blk] * 3 + [g_blk] * layers, out_specs=[blk] * 4,
        out_shape=[SDS((layers, rows, cols), F32)] * 4, name=name,
        compiler_params=_params())(w, m, v, *[g.reshape(rows, cols) for g in g_layers]))


def _adamw_small(ws, gs, ms, vs):
    n = len(ws)
    flat = []
    for group in (ws, gs, ms, vs):
        flat += [a.reshape(-1, a.shape[-1]) for a in group]

    def body(*refs):
        w_r, g_r, m_r, v_r = refs[:n], refs[n:2 * n], refs[2 * n:3 * n], refs[3 * n:4 * n]
        d_o, m_o, v_o = refs[4 * n:5 * n], refs[5 * n:6 * n], refs[6 * n:7 * n]
        for j in range(n):
            d, mm, vv = _adamw_math(w_r[j][...], g_r[j][...], m_r[j][...], v_r[j][...])
            d_o[j][...] = d
            m_o[j][...] = mm
            v_o[j][...] = vv

    shapes = [SDS(a.shape, F32) for a in flat[:n]]
    outs = pl.pallas_call(body, out_shape=shapes * 3, name="adamw_small", compiler_params=_params())(*flat)
    res = []
    for k in range(3):
        res.append([outs[k * n + j].reshape(ws[j].shape) for j in range(n)])
    return res


BIG = ("ab_w_in", "ab_w_out", "cd_w_in", "cd_w_out", "ffn_w_gate", "ffn_w_up", "ffn_w_down")
V_BLOCK = (2 * A_WIDTH + QK_COLS) // B_WIDTH


def _pad_rows(a, rows):
    return jnp.pad(a, ((0, rows - a.shape[0]), (0, 0)))


A_IN, A_OUT, C_IN, C_OUT = ("ab_w_in", 0), ("ab_w_out", 0), ("cd_w_in", 0), ("cd_w_out", 0)
G0, U0, D0 = ("ffn_w_gate", 0), ("ffn_w_up", 0), ("ffn_w_down", 0)
G1, U1, D1 = ("ffn_w_gate", 1), ("ffn_w_up", 1), ("ffn_w_down", 1)
UNITS = (A_IN, A_OUT, G0, U0, D0, C_IN, C_OUT, G1, U1, D1)
ROWS_MINOR = ("ffn_w_gate", "ffn_w_up")
SMALL_SHARDED = ("small", 0)
REPLICATED_UNIT = ("replicated", 0)


class _Exchange:
    def __init__(self, enabled):
        self.enabled = enabled
        self.w, self.grad, self.recv, self.half, self.land, self.done = {}, {}, {}, {}, {}, {}

    def full(self, unit):
        b = self.w[unit]
        return b.reshape(N_CHIPS, 1, 2 * b.shape[2], b.shape[3])

    def _ride(self, phases):
        rides, sinks = [], []
        for kind, units in phases:
            if kind == "send":
                rides.append(_ride_gather_send([self.w[u] for u in units]))
                sinks.append(self.w)
            elif kind == "pass":
                rides.append(_ride_gather_pass([self.w[u] for u in units]))
                sinks.append(self.w)
            elif kind == "swap":
                rides.append(_ride_swap([self.grad[u] for u in units]))
                sinks.append(self.recv)
            elif kind == "scatter":
                rides.append(_ride_scatter([self.half[u] for u in units], [self.land[u] for u in units]))
                sinks.append(self.land)
            else:
                rides.append(_ride_join([self.done[u] for u in units]))
                sinks.append(self.done)
        ride = functools.reduce(_ride_both, rides)

        def settle(res):
            n_bufs = sum(len(r.bufs) for r in rides)
            bufs, new = list(res[:n_bufs]), list(res[n_bufs:])
            for r, sink, (_, units) in zip(rides, sinks, phases):
                vals = [bufs.pop(0) for _ in r.bufs] + [new.pop(0) for _ in r.new_outs]
                for u, v in zip(units, vals):
                    sink[u] = v

        return ride, settle

    def run(self, fn, *args, phases=(), **kw):
        if not self.enabled or not phases:
            return fn(*args, **kw)
        ride, settle = self._ride(phases)
        out, res = fn(*args, ride=ride, **kw)
        settle(res)
        return out

    def alone(self, name, phases):
        if self.enabled:
            ride, settle = self._ride(phases)
            settle(_run_ride(name, ride))

    def pair_sum(self, units):
        if self.enabled:
            for u in units:
                dtype = F32 if u in (SMALL_SHARDED, REPLICATED_UNIT) else BF16
                self.half[u], self.land[u] = _add_own_half(f"pair_sum_{u[0]}_{u[1]}", self.grad[u], self.recv[u], dtype)

    def chip_sum(self, units):
        if self.enabled:
            for u in units:
                self.done[u] = _sum_chips(f"chip_sum_{u[0]}_{u[1]}", self.land[u])


def _local_step(x, target, ex, sp):
    t, d = x.shape
    tabs = _rope_tables(t)
    gains = jnp.concatenate([jnp.tile(sp["q_norm_g"][g], HEAD_DIM // 8) for g in range(N_DIL)]
                            + [jnp.tile(sp["k_norm_g"][g], HEAD_DIM // 8) for g in range(N_DIL)]).reshape(1, QK_COLS)
    bias_t = sp["sgu_bias"].T
    cw = _pad_rows(sp["conv_c_w"], 32)
    dw = _pad_rows(sp["conv_d_w"], 8)
    cb, clg, clb = (sp[k].reshape(1, C_WIDTH) for k in ("conv_c_b", "c_ln_g", "c_ln_b"))
    slg, slb = sp["sgu_norm_g"].reshape(1, A_WIDTH), sp["sgu_norm_b"].reshape(1, A_WIDTH)
    g_ab, g_cd = sp["ab_norm_g"].reshape(1, d), sp["cd_norm_g"].reshape(1, d)
    g_f0, g_f1 = sp["ffn_norm_g"][0:1], sp["ffn_norm_g"][1:2]
    run = ex.run

    def w2d(unit):
        return ex.full(unit).reshape(-1, d)

    h0 = _rms_fwd("rms_ab", x, g_ab)
    proj = run(_proj_in, "proj_ab", h0, ex.full(A_IN), 0, phases=[("send", [A_OUT, G0])])
    a_out = _mixer_a_fwd(proj, slg, slb, sp["sgu_w"], bias_t)
    qk, q1, q2, k1, k2 = run(_qk_fwd, proj, gains, tabs, phases=[("pass", [A_OUT, G0]), ("send", [U0])])
    regrouped_qk = {1: (q1, k1), 2: (q2, k2)}
    fwd_phases = ([("pass", [U0]), ("send", [D0])], [("pass", [D0]), ("send", [C_IN])],
                  [("pass", [C_IN]), ("send", [C_OUT, G1])])
    qkv, o_list, l_list = [], [], []
    for g, rate in enumerate(DIL_RATES):
        if rate == 1:
            qk3, proj3 = qk.reshape(1, t, QK_COLS), proj.reshape(1, t, AB_IN)
            q, k, v = (qk3, g), (qk3, N_DIL + g), (proj3, V_BLOCK + g)
        else:
            vp, = _permute(f"regroup_v_{g}", [(proj, V_BLOCK + g)], rate)
            q, k, v = (regrouped_qk[g][0], 0), (regrouped_qk[g][1], 0), (vp, 0)
        qkv.append((q, k, v))
        o, l = run(_attn_fwd, f"attn_fwd_{g}", q, k, v, phases=fwd_phases[g])
        if rate == 1:
            o, l = o.reshape(t, B_WIDTH), l.reshape(t, B_WIDTH)
        o_list.append(o)
        l_list.append(l)
    cat, lse_tot, lse_1, lse_2 = _attn_merge(a_out, o_list, l_list)
    x1, hf0 = _proj_out("out_ab", cat, w2d(A_OUT), x, g_next=g_f0)
    gate0, up0, act0 = run(_ffn_in, "ffn_in_0", hf0, ex.full(G0), ex.full(U0), 0,
                           phases=[("pass", [C_OUT, G1]), ("send", [U1])])
    x2, h1 = run(_ffn_out, "ffn_out_0", act0, ex.full(D0), 0, x1, g_next=g_cd, phases=[("pass", [U1]), ("send", [D1])])
    projcd = run(_proj_in, "proj_cd", h1, ex.full(C_IN), 0, phases=[("pass", [D1])])
    cat2, c1 = _mixer_cd_fwd(projcd, cw, cb, clg, clb, dw)
    x3, hf1 = _proj_out("out_cd", cat2, w2d(C_OUT), x2, g_next=g_f1)
    gate1, up1, act1 = _ffn_in("ffn_in_1", hf1, ex.full(G1), ex.full(U1), 0)
    dy, loss_acc, dy_b = _ffn_out("ffn_out_1", act1, ex.full(D1), 0, x3, target=target)
    loss = 0.5 * loss_acc[0, 0] / d

    late = [D1, G1, U1]
    dgate, dup = _ffn_dact("ffn_dact_1", dy_b, ex.full(D1), 0, gate1, up1)
    ex.grad[D1] = _wgrad_row_sharded("wgrad_down_1", act1, dy_b, True)
    ex.grad[G1], ex.grad[U1] = _wgrad_row_sharded("wgrad_gate_up_1", [dgate, dup], hf1, True)
    g3, d_f1, g3_b = run(_dgrad_cols, "dgrad_ffn_1", [dgate, dup], [ex.full(G1), ex.full(U1)], 0, True, x3, g_f1, dy,
                         w_rows=True, phases=[("swap", late)])
    ex.pair_sum(late)

    dcat2 = _dgrad_rows("dgrad_out_cd", g3_b, w2d(C_OUT))
    ex.grad[C_OUT] = _wgrad_row_sharded("wgrad_out_cd", cat2, g3_b, False)
    dprojcd, d_cw, d_cb, d_clg, d_clb, d_dw = run(_mixer_cd_bwd, projcd, dcat2, c1, cw, clg, clb, dw, phases=[("scatter", late)])
    ex.chip_sum(late)
    ex.grad[C_IN] = run(_wgrad_col_sharded, "wgrad_in_cd", h1, [dprojcd], False, phases=[("join", late)])[0]
    g2, d_cdn, g2_b = run(_dgrad_cols, "dgrad_in_cd", [dprojcd], [ex.full(C_IN)], 0, False, x2, g_cd, g3,
                          phases=[("swap", [C_OUT, C_IN])])
    ex.pair_sum([C_OUT, C_IN])

    dgate, dup = run(_ffn_dact, "ffn_dact_0", g2_b, ex.full(D0), 0, gate0, up0, phases=[("scatter", [C_OUT, C_IN])])
    ex.chip_sum([C_OUT, C_IN])
    ex.grad[D0] = run(_wgrad_row_sharded, "wgrad_down_0", act0, g2_b, True, phases=[("join", [C_OUT, C_IN])])
    ex.grad[G0], ex.grad[U0] = _wgrad_row_sharded("wgrad_gate_up_0", [dgate, dup], hf0, True)
    small = {"cd_norm_g": d_cdn, "conv_c_w": d_cw[:C_KERNEL], "conv_c_b": d_cb, "c_ln_g": d_clg, "c_ln_b": d_clb,
             "conv_d_w": d_dw[:D_KERNEL]}
    ex.grad[SMALL_SHARDED] = _split_full_small(small).reshape(N_CHIPS, 2, SHARDED_ROWS // 2, LANES)
    mid = [D0, G0, U0, SMALL_SHARDED]
    g1, d_f0, g1_b = run(_dgrad_cols, "dgrad_ffn_0", [dgate, dup], [ex.full(G0), ex.full(U0)], 0, True, x1, g_f0, g2,
                         w_rows=True, phases=[("swap", mid)])
    ex.pair_sum(mid)

    dcat = _dgrad_rows("dgrad_out_ab", g1_b, w2d(A_OUT))
    ex.grad[A_OUT] = _wgrad_row_sharded("wgrad_out_ab", cat, g1_b, False)
    d_a, d_sw, d_sbt, d_slg, d_slb = _mixer_a_bwd(proj, dcat, slg, slb, sp["sgu_w"], bias_t)
    early = {"sgu_norm_g": d_slg, "sgu_norm_b": d_slb, "sgu_w": d_sw, "sgu_bias": d_sbt.T}
    ex.grad[REPLICATED_UNIT] = jnp.broadcast_to(
        _pack_replicated(early, REPLICATED_EARLY, REPLICATED_EARLY_ROWS).reshape(2, REPLICATED_EARLY_ROWS // 2, LANES),
        (N_CHIPS, 2, REPLICATED_EARLY_ROWS // 2, LANES))
    last = [A_OUT, REPLICATED_UNIT]
    dbb, dd, db_1, dd_1, db_2, dd_2 = _attn_bwd_prep(dcat, cat)
    regrouped_bwd = {1: (db_1, lse_1, dd_1), 2: (db_2, lse_2, dd_2)}
    bwd_phases = ([("scatter", [D0, G0])], [("scatter", [U0, SMALL_SHARDED]), ("join", [D0, G0]), ("swap", last)],
                  [("join", [U0, SMALL_SHARDED]), ("scatter", last)])
    dqs, dks, dvs = [], [], []
    for g, rate in enumerate(DIL_RATES):
        q, k, v = qkv[g]
        if rate == 1:
            db3, l3, dd3 = (a.reshape(1, t, B_WIDTH) for a in (dbb, lse_tot, dd))
        else:
            db3, l3, dd3 = regrouped_bwd[g]
        if g == 1:
            ex.chip_sum([D0, G0])
        elif g == 2:
            ex.chip_sum([U0, SMALL_SHARDED])
            ex.pair_sum(last)
        dq, dk, dv = run(_attn_bwd, f"attn_bwd_{g}", q, k, v, db3, l3, dd3, phases=bwd_phases[g])
        if g == 2:
            ex.chip_sum(last)
        if rate == 1:
            dq, dk, dv = (a.reshape(t, B_WIDTH) for a in (dq, dk, dv))
        dqs.append(dq)
        dks.append(dk)
        dvs.append(dv)
    dproj, d_gains = _dproj_assemble(proj, d_a, dqs, dks, dvs, gains, tabs)
    d_gains = _fold_heads(d_gains)[0].reshape(2, N_DIL, B_WIDTH)[:, :, :HEAD_DIM]
    ex.grad[A_IN] = run(_wgrad_col_sharded, "wgrad_in_ab", h0, [dproj], False, phases=[("join", last)])[0]
    ex.alone("swap_last", [("swap", [A_IN])])
    ex.pair_sum([A_IN])
    gx, d_abn = run(_dgrad_cols, "dgrad_in_ab", [dproj], [ex.full(A_IN)], 0, False, x, g_ab, g1, bf16_copy=False,
                    phases=[("scatter", [A_IN])])
    ex.chip_sum([A_IN])
    ex.alone("join_last", [("join", [A_IN])])

    small.update({
        "ab_norm_g": d_abn, "sgu_norm_g": d_slg, "sgu_norm_b": d_slb, "sgu_w": d_sw, "sgu_bias": d_sbt.T,
        "q_norm_g": d_gains[0], "k_norm_g": d_gains[1], "ffn_norm_g": jnp.concatenate([d_f0, d_f1], axis=0),
    })
    return loss, gx, small


SHARDED_SMALL = ("cd_norm_g", "conv_c_w", "conv_c_b", "c_ln_g", "c_ln_b", "conv_d_w")
SHARDED_ROWS = 48
REPLICATED_EARLY = ("sgu_norm_g", "sgu_norm_b", "sgu_w", "sgu_bias")
REPLICATED_EARLY_ROWS = 528
REPLICATED_LATE = ("ab_norm_g", "q_norm_g", "k_norm_g", "ffn_norm_g", "loss")
REPLICATED_LATE_ROWS = 32
REPLICATED_SMALL = REPLICATED_EARLY + REPLICATED_LATE[:-1]


def _pack_sharded(parts):
    rows = [parts[k].reshape(-1, LANES) for k in SHARDED_SMALL]
    return _pad_rows(jnp.concatenate(rows, axis=0), SHARDED_ROWS)


def _split_full_small(small):
    per_chip = []
    for q in range(N_CHIPS):
        parts = {}
        for k in SHARDED_SMALL:
            a = small[k]
            a = a.reshape(-1, a.shape[-1])
            n = a.shape[-1] // N_CHIPS
            parts[k] = a[:, q * n:(q + 1) * n]
        per_chip.append(_pack_sharded(parts))
    return jnp.stack(per_chip)


def _unpack_sharded(pack, shapes):
    out, r = {}, 0
    for k in SHARDED_SMALL:
        n = math.prod(shapes[k]) // LANES
        out[k] = pack[r:r + n].reshape(shapes[k])
        r += n
    return out


def _gathered_small(packs, shapes):
    per_chip = [_unpack_sharded(packs[q], shapes) for q in range(N_CHIPS)]
    return {k: jnp.concatenate([pc[k] for pc in per_chip], axis=-1) for k in SHARDED_SMALL}


def _pack_replicated(small, names, total_rows):
    rows = []
    for k in names:
        a = small[k].reshape(-1)
        a = jnp.pad(a, (0, (-a.shape[0]) % LANES))
        rows.append(a.reshape(-1, LANES))
    return _pad_rows(jnp.concatenate(rows, axis=0), total_rows)


def _unpack_replicated(pack, shapes, names):
    out, r = {}, 0
    for k in names:
        size = math.prod(shapes[k])
        n = -(-size // LANES)
        out[k] = pack[r:r + n].reshape(-1)[:size].reshape(shapes[k])
        r += n
    return out


WEIGHT_ORDER = ("ab_norm_g", "ab_w_in", "sgu_norm_g", "sgu_norm_b", "sgu_w", "sgu_bias", "q_norm_g", "k_norm_g", "ab_w_out",
                "cd_norm_g", "cd_w_in", "conv_c_w", "conv_c_b", "c_ln_g", "c_ln_b", "conv_d_w", "cd_w_out", "ffn_norm_g",
                "ffn_w_gate", "ffn_w_up", "ffn_w_down")


def kernel(x, ab_norm_g, ab_w_in, sgu_norm_g, sgu_norm_b, sgu_w, sgu_bias, q_norm_g, k_norm_g, ab_w_out, cd_norm_g, cd_w_in, conv_c_w, conv_c_b, c_ln_g, c_ln_b, conv_d_w, cd_w_out, ffn_norm_g, ffn_w_gate, ffn_w_up, ffn_w_down, loss_target, m_ab_norm_g, m_ab_w_in, m_sgu_norm_g, m_sgu_norm_b, m_sgu_w, m_sgu_bias, m_q_norm_g, m_k_norm_g, m_ab_w_out, m_cd_norm_g, m_cd_w_in, m_conv_c_w, m_conv_c_b, m_c_ln_g, m_c_ln_b, m_conv_d_w, m_cd_w_out, m_ffn_norm_g, m_ffn_w_gate, m_ffn_w_up, m_ffn_w_down, v_ab_norm_g, v_ab_w_in, v_sgu_norm_g, v_sgu_norm_b, v_sgu_w, v_sgu_bias, v_q_norm_g, v_k_norm_g, v_ab_w_out, v_cd_norm_g, v_cd_w_in, v_conv_c_w, v_conv_c_b, v_c_ln_g, v_c_ln_b, v_conv_d_w, v_cd_w_out, v_ffn_norm_g, v_ffn_w_gate, v_ffn_w_up, v_ffn_w_down):
    args = dict(locals())
    ws = {k: args[k] for k in WEIGHT_ORDER}
    ms = {k: args["m_" + k] for k in WEIGHT_ORDER}
    vs = {k: args["v_" + k] for k in WEIGHT_ORDER}
    small_names = [k for k in WEIGHT_ORDER if k not in BIG]
    t, d = x.shape[1:]

    for group in (ws, ms, vs):
        for k in ROWS_MINOR:
            group[k] = jnp.swapaxes(group[k], 1, 2)
    ex = _Exchange(enabled=True)
    for name, layer in UNITS:
        ex.w[(name, layer)] = _stage_own(f"stage_{name}_{layer}", ws[name], layer, BF16)
    own_small = _pack_sharded({k: ws[k][0] for k in SHARDED_SMALL})
    ex.w[SMALL_SHARDED] = _stage_own("stage_small", own_small[None], 0, F32)
    ex.alone("gather_first", [("send", [A_IN, SMALL_SHARDED])])
    ex.alone("gather_first_pass", [("pass", [A_IN, SMALL_SHARDED])])
    sp = _gathered_small(ex.w[SMALL_SHARDED].reshape(N_CHIPS, SHARDED_ROWS, LANES), {k: ws[k].shape[1:] for k in SHARDED_SMALL})
    for k in REPLICATED_SMALL:
        sp[k] = ws[k] if k == "ffn_norm_g" else ws[k][0]

    loss, grad_x, g_small = _local_step(x.reshape(t, d), loss_target.reshape(t, d), ex, sp)

    shapes = {k: ws[k].shape for k in REPLICATED_SMALL}
    shapes["loss"] = (1,)
    g_small["loss"] = loss
    late = _all_reduce_small(_pack_replicated(g_small, REPLICATED_LATE, REPLICATED_LATE_ROWS))
    grad = _unpack_sharded(ex.done[SMALL_SHARDED].reshape(SHARDED_ROWS, LANES), {k: ws[k].shape for k in SHARDED_SMALL})
    grad.update(_unpack_replicated(ex.done[REPLICATED_UNIT].reshape(REPLICATED_EARLY_ROWS, LANES), shapes, REPLICATED_EARLY))
    grad.update(_unpack_replicated(late, shapes, REPLICATED_LATE))
    loss = grad.pop("loss")[0]

    delta, new_m, new_v = {}, {}, {}
    for k in BIG:
        g_layers = [ex.done[(k, layer)] for layer in range(ws[k].shape[0])]
        outs = _adamw_big("adamw_" + k, ws[k], g_layers, ms[k], vs[k])
        if k in ROWS_MINOR:
            outs = [jnp.swapaxes(o, 1, 2) for o in outs]
        grad[k], delta[k], new_m[k], new_v[k] = outs
    d_s, m_s, v_s = _adamw_small([ws[k] for k in small_names], [grad[k] for k in small_names],
                                 [ms[k] for k in small_names], [vs[k] for k in small_names])
    for j, k in enumerate(small_names):
        delta[k], new_m[k], new_v[k] = d_s[j], m_s[j], v_s[j]

    return (loss, grad_x[None], *[grad[k] for k in WEIGHT_ORDER], *[delta[k] for k in WEIGHT_ORDER],
            *[new_m[k] for k in WEIGHT_ORDER], *[new_v[k] for k in WEIGHT_ORDER])
```

```python
import functools
import math

import jax
import jax.numpy as jnp
from jax import lax
from jax.experimental import pallas as pl
from jax.experimental.pallas import tpu as pltpu

F32 = jnp.float32
BF16 = jnp.bfloat16
SDS = jax.ShapeDtypeStruct

N_CHIPS = 4
EPS = 1e-6
NEG_INF = -1e30
CHUNK = 128
A_GROUPS = 4
A_WIDTH = 512
N_DIL = 3
DIL_RATES = (1, 4, 16)
HEAD_DIM = 64
B_WIDTH = 512
ROPE_DIM = 16
ROPE_THETA = 500000.0
C_WIDTH = 512
C_KERNEL = 31
D_KERNEL = 3
HALO = 32
ATT_BLOCK = 128
LANES = 128

ADAM_LR = 0.001
ADAM_B1 = 0.9
ADAM_B2 = 0.999
ADAM_EPS = 1e-08
ADAM_WD = 0.01
ADAM_STEP = 10

VMEM_LIMIT = 56 * 1024 * 1024

NN = (((1,), (0,)), ((), ()))
NT = (((1,), (1,)), ((), ()))
TN = (((0,), (0,)), ((), ()))

TILES = {"proj_in": 1024, "proj_out": 1024, "ffn_in": 1024, "ffn_out": 512, "ffn_dact": 512, "dgrad_cols": 512,
         "dgrad_rows": 1024, "wgrad": 2048}


def _params(sem=None):
    return pltpu.CompilerParams(dimension_semantics=sem, vmem_limit_bytes=VMEM_LIMIT)


def _bf(v):
    return v if v.dtype == BF16 else v.astype(BF16)


def _dot(a, b, dims):
    return lax.dot_general(_bf(a), _bf(b), dims, preferred_element_type=F32)


def _dot_hi(a, b):
    return jnp.dot(a, b, precision=lax.Precision.HIGHEST, preferred_element_type=F32)


def _sigmoid(v):
    return 0.5 * jnp.tanh(0.5 * v) + 0.5


def _gelu(v):
    return 0.5 * v * (1.0 + lax.erf(v * (1.0 / math.sqrt(2.0))))


def _gelu_grad(v):
    cdf = 0.5 * (1.0 + lax.erf(v * (1.0 / math.sqrt(2.0))))
    return cdf + v * jnp.exp(-0.5 * v * v) * (1.0 / math.sqrt(2.0 * math.pi))


def _segment_mean_matrix(seg, scale=None):
    r = lax.broadcasted_iota(jnp.int32, (LANES, LANES), 0) // seg
    c = lax.broadcasted_iota(jnp.int32, (LANES, LANES), 1) // seg
    return jnp.where(r == c, (1.0 / seg) if scale is None else scale, 0.0).astype(BF16)


def _segment_dot(v, seg):
    hi = v.astype(BF16)
    lo = (v - hi.astype(F32)).astype(BF16)
    return jnp.dot(hi, seg, preferred_element_type=F32) + jnp.dot(lo, seg, preferred_element_type=F32)


MESH = pl.DeviceIdType.MESH
ANY = pl.BlockSpec(memory_space=pl.ANY)


def _position():
    x, y, c = lax.axis_index("x"), lax.axis_index("y"), lax.axis_index("c")
    others = [(1 - x, y), (x, 1 - y), (1 - x, 1 - y)]
    return x, y, c, 2 * x + y, others


class _Ride:
    def __init__(self, ins, bufs, new_outs, sem_shapes, start, finish):
        self.ins, self.bufs, self.new_outs, self.sem_shapes = list(ins), list(bufs), list(new_outs), list(sem_shapes)
        self.start, self.finish = start, finish


def _ride_both(a, b):
    na = (len(a.ins), len(a.bufs), len(a.new_outs), len(a.sem_shapes))

    def split(ins, bufs, new, sems):
        return ((ins[:na[0]], bufs[:na[1]], new[:na[2]], sems[:na[3]]), (ins[na[0]:], bufs[na[1]:], new[na[2]:], sems[na[3]:]))

    def start(*refs):
        ra, rb = split(*refs)
        a.start(*ra)
        b.start(*rb)

    def finish(*refs):
        ra, rb = split(*refs)
        a.finish(*ra)
        b.finish(*rb)

    return _Ride(a.ins + b.ins, a.bufs + b.bufs, a.new_outs + b.new_outs, a.sem_shapes + b.sem_shapes, start, finish)


def _call(body, *, grid, in_specs, out_specs, out_shape, operands, name, scratch_shapes=(), aliases=None, ride=None):
    if ride is None:
        return pl.pallas_call(body, grid=grid, in_specs=in_specs, out_specs=out_specs, out_shape=out_shape,
                              scratch_shapes=list(scratch_shapes), input_output_aliases=aliases or {}, name=name,
                              compiler_params=_params())(*operands)
    multi = isinstance(out_shape, (list, tuple))
    out_shapes = list(out_shape) if multi else [out_shape]
    o_specs = list(out_specs) if multi else [out_specs]
    n_in, n_out, n_scr = len(operands), len(out_shapes), len(scratch_shapes)
    n_ri, n_rb, n_rn = len(ride.ins), len(ride.bufs), len(ride.new_outs)

    def carrying(*refs):
        k = n_in
        r_ins = refs[k:k + n_ri]
        k += n_ri + n_rb
        outs = refs[k:k + n_out]
        k += n_out
        r_bufs = refs[k:k + n_rb]
        k += n_rb
        r_new = refs[k:k + n_rn]
        k += n_rn
        scratch = refs[k:k + n_scr]
        sems = refs[k + n_scr:]
        first, last = None, None
        for axis, size in enumerate(grid):
            pid = pl.program_id(axis)
            first = (pid == 0) if first is None else first & (pid == 0)
            last = (pid == size - 1) if last is None else last & (pid == size - 1)

        @pl.when(first)
        def _():
            ride.start(r_ins, r_bufs, r_new, sems)

        body(*refs[:n_in], *outs, *scratch)

        @pl.when(last)
        def _():
            ride.finish(r_ins, r_bufs, r_new, sems)

    all_aliases = dict(aliases or {})
    for j in range(n_rb):
        all_aliases[n_in + n_ri + j] = n_out + j
    res = pl.pallas_call(
        carrying, grid=grid, in_specs=list(in_specs) + [ANY] * (n_ri + n_rb), out_specs=o_specs + [ANY] * (n_rb + n_rn),
        out_shape=out_shapes + [SDS(b.shape, b.dtype) for b in ride.bufs] + ride.new_outs,
        scratch_shapes=list(scratch_shapes) + [pltpu.SemaphoreType.DMA(s) for s in ride.sem_shapes],
        input_output_aliases=all_aliases, name=name, compiler_params=_params())(*operands, *ride.ins, *ride.bufs)
    outs = res[:n_out]
    return (list(outs) if multi else outs[0]), list(res[n_out:])


def _run_ride(name, ride):
    n_ri, n_rb, n_rn = len(ride.ins), len(ride.bufs), len(ride.new_outs)

    def body(*refs):
        r_ins = refs[:n_ri]
        r_bufs = refs[n_ri + n_rb:n_ri + 2 * n_rb]
        r_new = refs[n_ri + 2 * n_rb:n_ri + 2 * n_rb + n_rn]
        sems = refs[n_ri + 2 * n_rb + n_rn:]
        ride.start(r_ins, r_bufs, r_new, sems)
        ride.finish(r_ins, r_bufs, r_new, sems)

    return list(pl.pallas_call(
        body, in_specs=[ANY] * (n_ri + n_rb), out_specs=[ANY] * (n_rb + n_rn),
        out_shape=[SDS(b.shape, b.dtype) for b in ride.bufs] + ride.new_outs,
        scratch_shapes=[pltpu.SemaphoreType.DMA(s) for s in ride.sem_shapes],
        input_output_aliases={n_ri + j: j for j in range(n_rb)}, name=name)(*ride.ins, *ride.bufs))


def _whole(ref, p):
    return ref[...]


def _slab(ref, p):
    return ref[p]


def _matmul(name, grid, pairs, extras, outs, dims, epi, *, slabs=1, n_acc=1, ride=None):
    n_pairs, n_ex, n_out = len(pairs), len(extras), len(outs)

    def body(*refs):
        ab = refs[:2 * n_pairs]
        ex = refs[2 * n_pairs:2 * n_pairs + n_ex]
        out_refs = refs[2 * n_pairs + n_ex:2 * n_pairs + n_ex + n_out]
        pids = tuple(pl.program_id(a) for a in range(len(grid)))
        parts = [None] * n_acc
        for p in range(slabs):
            for j, (_, _, a_pick, _, _, b_pick, acc) in enumerate(pairs):
                d = _dot(a_pick(ab[2 * j], p), b_pick(ab[2 * j + 1], p), dims)
                parts[acc] = d if parts[acc] is None else parts[acc] + d
        epi(parts, ex, out_refs, pids)

    operands, in_specs = [], []
    for a, a_spec, _, b, b_spec, _, _ in pairs:
        operands += [a, b]
        in_specs += [a_spec, b_spec]
    for e, e_spec in extras:
        operands.append(e)
        in_specs.append(e_spec)
    return _call(body, grid=grid, in_specs=in_specs, out_specs=[o[1] for o in outs], out_shape=[o[0] for o in outs],
                 operands=operands, name=name, ride=ride)


def _rms_rows(v, g):
    r = lax.rsqrt(jnp.mean(v * v, axis=-1, keepdims=True) + EPS)
    return v * r * g


def _rms_fwd(name, x, g):
    t, d = x.shape
    tm = 512

    def body(x_ref, g_ref, o_ref):
        o_ref[...] = _rms_rows(x_ref[...], g_ref[...]).astype(BF16)

    return pl.pallas_call(
        body, grid=(t // tm,),
        in_specs=[pl.BlockSpec((tm, d), lambda i: (i, 0)), pl.BlockSpec((1, d), lambda i: (0, 0))],
        out_specs=pl.BlockSpec((tm, d), lambda i: (i, 0)), out_shape=SDS((t, d), BF16), name=name,
        compiler_params=_params())(x, g)


def _epi_residual_norm(accs, ex, outs, pids):
    x_new = accs[0] + ex[0][...]
    outs[0][...] = x_new
    outs[1][...] = _rms_rows(x_new, ex[1][...]).astype(BF16)


def _epi_residual_loss(accs, ex, outs, pids):
    y = accs[0] + ex[0][...]
    err = y - ex[1][...]
    dy = err * (1.0 / err.shape[-1])
    outs[0][...] = dy
    outs[2][...] = dy.astype(BF16)

    @pl.when(pids[0] == 0)
    def _():
        outs[1][...] = jnp.zeros_like(outs[1])

    outs[1][...] += jnp.sum(err * err)


def _epi_rms_bwd(accs, ex, outs, pids):
    dh = accs[0]
    xv, g, res = ex[0][...], ex[1][...], ex[2][...]
    r = lax.rsqrt(jnp.mean(xv * xv, axis=-1, keepdims=True) + EPS)
    xh = xv * r
    dy = dh * g
    dx = res + r * (dy - xh * jnp.mean(dy * xh, axis=-1, keepdims=True))
    outs[0][...] = dx
    if len(outs) > 2:
        outs[2][...] = dx.astype(BF16)

    @pl.when(pids[0] == 0)
    def _():
        outs[1][...] = jnp.zeros_like(outs[1])

    outs[1][...] += jnp.sum(dh * xh, axis=0, keepdims=True)


def _row_spec(tm, d):
    return pl.BlockSpec((tm, d), lambda i, *_: (i, 0))


def _const_spec(shape):
    nd = len(shape)
    return pl.BlockSpec(shape, lambda *_: (0,) * nd)


def _proj_in(name, h, w, layer, ride=None):
    t, d = h.shape
    n4 = w.shape[-1]
    tm = TILES["proj_in"]

    def epi(accs, ex, outs, pids):
        outs[0][...] = accs[0].astype(BF16)

    res = _matmul(
        name, (N_CHIPS, t // tm),
        [(h, pl.BlockSpec((tm, d), lambda p, i: (i, 0)), _whole,
          w, pl.BlockSpec((None, None, d, n4), lambda p, i: (p, layer, 0, 0)), _whole, 0)],
        [], [(SDS((t, N_CHIPS * n4), BF16), pl.BlockSpec((tm, n4), lambda p, i: (i, p)))],
        NN, epi, ride=ride)
    return res[0] if ride is None else (res[0][0], res[1])


def _proj_out(name, a, w, x, g_next=None, target=None):
    t, k = a.shape
    d = w.shape[-1]
    tm = TILES["proj_out"]
    if target is None:
        extras = [(x, _row_spec(tm, d)), (g_next, _const_spec((1, d)))]
        outs = [(SDS((t, d), F32), _row_spec(tm, d)), (SDS((t, d), BF16), _row_spec(tm, d))]
        epi = _epi_residual_norm
    else:
        extras = [(x, _row_spec(tm, d)), (target, _row_spec(tm, d))]
        outs = [(SDS((t, d), F32), _row_spec(tm, d)), (SDS((8, LANES), F32), _const_spec((8, LANES))),
                (SDS((t, d), BF16), _row_spec(tm, d))]
        epi = _epi_residual_loss
    return _matmul(name, (t // tm,), [(a, _row_spec(tm, k), _whole, w, _const_spec((k, d)), _whole, 0)], extras, outs, NN, epi)


def _ffn_in(name, h, wg, wu, layer, ride=None):
    t, d = h.shape
    n4 = wg.shape[-2]
    tm = TILES["ffn_in"]

    def epi(accs, ex, outs, pids):
        gate, up = accs
        s = _sigmoid(gate)
        silu = gate * s
        outs[0][...] = (up * (s + silu - silu * s)).astype(BF16)
        outs[1][...] = silu.astype(BF16)
        outs[2][...] = (silu * up).astype(BF16)

    w_spec = pl.BlockSpec((None, None, n4, d), lambda p, i: (p, layer, 0, 0))
    h_spec = pl.BlockSpec((tm, d), lambda p, i: (i, 0))
    o = (SDS((N_CHIPS, t, n4), BF16), pl.BlockSpec((None, tm, n4), lambda p, i: (p, i, 0)))
    return _matmul(name, (N_CHIPS, t // tm),
                   [(h, h_spec, _whole, wg, w_spec, _whole, 0), (h, h_spec, _whole, wu, w_spec, _whole, 1)], [],
                   [o, o, o], NT, epi, n_acc=2, ride=ride)


def _ffn_out(name, act, wd, layer, x, g_next=None, target=None, ride=None):
    _, t, n4 = act.shape
    d = wd.shape[-1]
    tm = TILES["ffn_out"]
    xs = _row_spec(tm, d)
    if target is None:
        extras = [(x, xs), (g_next, _const_spec((1, d)))]
        outs = [(SDS((t, d), F32), xs), (SDS((t, d), BF16), xs)]
        epi = _epi_residual_norm
    else:
        extras = [(x, xs), (target, xs)]
        outs = [(SDS((t, d), F32), xs), (SDS((8, LANES), F32), _const_spec((8, LANES))), (SDS((t, d), BF16), xs)]
        epi = _epi_residual_loss
    return _matmul(
        name, (t // tm,),
        [(act, pl.BlockSpec((N_CHIPS, tm, n4), lambda i: (0, i, 0)), _slab,
          wd, pl.BlockSpec((N_CHIPS, None, n4, d), lambda i: (0, layer, 0, 0)), _slab, 0)],
        extras, outs, NN, epi, slabs=N_CHIPS, ride=ride)


def _ffn_dact(name, g, wd, layer, gate, up, ride=None):
    t, d = g.shape
    n4 = wd.shape[-2]
    tm = TILES["ffn_dact"]

    def epi(accs, ex, outs, pids):
        dact = accs[0]
        outs[0][...] = (dact * ex[0][...].astype(F32)).astype(BF16)
        outs[1][...] = (dact * ex[1][...].astype(F32)).astype(BF16)

    blk = pl.BlockSpec((None, tm, n4), lambda p, i: (p, i, 0))
    o = (SDS((N_CHIPS, t, n4), BF16), blk)
    return _matmul(
        name, (N_CHIPS, t // tm),
        [(g, pl.BlockSpec((tm, d), lambda p, i: (i, 0)), _whole,
          wd, pl.BlockSpec((None, None, n4, d), lambda p, i: (p, layer, 0, 0)), _whole, 0)],
        [(gate, blk), (up, blk)], [o, o], NT, epi, ride=ride)


def _copy_epi(accs, ex, outs, pids):
    for a, o in zip(accs, outs):
        o[...] = a.astype(o.dtype)


def _dgrad_cols(name, dz_list, w_list, layer, three_d, x, g, res, bf16_copy=True, w_rows=False, ride=None):
    t, d = x.shape
    n4 = w_list[0].shape[-2 if w_rows else -1]
    tm = TILES["dgrad_cols"]
    if three_d:
        zs, z_pick = pl.BlockSpec((N_CHIPS, tm, n4), lambda i: (0, i, 0)), _slab
    else:
        zs, z_pick = _row_spec(tm, N_CHIPS * n4), (lambda ref, p: ref[:, p * n4:(p + 1) * n4])
    ws = pl.BlockSpec((N_CHIPS, None) + ((n4, d) if w_rows else (d, n4)), lambda i: (0, layer, 0, 0))
    xs = _row_spec(tm, d)
    return _matmul(
        name, (t // tm,), [(dz, zs, z_pick, w, ws, _slab, 0) for dz, w in zip(dz_list, w_list)],
        [(x, xs), (g, _const_spec((1, d))), (res, xs)],
        [(SDS((t, d), F32), xs), (SDS((1, d), F32), _const_spec((1, d)))] + ([(SDS((t, d), BF16), xs)] if bf16_copy else []),
        NN if w_rows else NT, _epi_rms_bwd, slabs=N_CHIPS, ride=ride)


def _dgrad_rows(name, g, w):
    t, d = g.shape
    k = w.shape[0]
    tm = TILES["dgrad_rows"]
    return _matmul(name, (t // tm,), [(g, _row_spec(tm, d), _whole, w, _const_spec((k, d)), _whole, 0)], [],
                   [(SDS((t, k), F32), _row_spec(tm, k))], NT, _copy_epi)[0]


A_TILE = 256


def _a_common(p_ref, lg_ref, lb_ref):
    pv = p_ref[...].astype(F32)
    a = _gelu(pv)
    u, v = a[:, :A_WIDTH], a[:, A_WIDTH:]
    vc = v - jnp.mean(v, axis=-1, keepdims=True)
    rs = lax.rsqrt(jnp.mean(vc * vc, axis=-1, keepdims=True) + EPS)
    vhat = vc * rs
    vn = vhat * lg_ref[...] + lb_ref[...]
    return pv, u, vhat, rs, vn.astype(BF16)


def _tril_weights(w_ref, g):
    r = lax.broadcasted_iota(jnp.int32, (CHUNK, CHUNK), 0)
    c = lax.broadcasted_iota(jnp.int32, (CHUNK, CHUNK), 1)
    return jnp.where(c <= r, w_ref[g], 0.0).astype(BF16), c <= r


def _mixer_a_fwd(proj, lg, lb, w, bias_t):
    t = proj.shape[0]

    def body(p_ref, lg_ref, lb_ref, w_ref, bt_ref, o_ref):
        _, u, _, _, vnb = _a_common(p_ref, lg_ref, lb_ref)
        for g in range(A_GROUPS):
            wt, _ = _tril_weights(w_ref, g)
            cs = slice(g * CHUNK, (g + 1) * CHUNK)
            for ch in range(A_TILE // CHUNK):
                rs_ = slice(ch * CHUNK, (ch + 1) * CHUNK)
                mixed = _dot(wt, vnb[rs_, cs], NN) + bt_ref[:, g:g + 1]
                o_ref[rs_, cs] = (u[rs_, cs] * mixed).astype(BF16)

    return pl.pallas_call(
        body, grid=(t // A_TILE,),
        in_specs=[pl.BlockSpec((A_TILE, 2 * A_WIDTH), lambda i: (i, 0)), _const_spec((1, A_WIDTH)),
                  _const_spec((1, A_WIDTH)), _const_spec((A_GROUPS, CHUNK, CHUNK)), _const_spec((CHUNK, A_GROUPS))],
        out_specs=pl.BlockSpec((A_TILE, A_WIDTH), lambda i: (i, 0)), out_shape=SDS((t, A_WIDTH), BF16),
        name="mixer_a_fwd", compiler_params=_params())(proj, lg, lb, w, bias_t)


def _mixer_a_bwd(proj, dcat, lg, lb, w, bias_t):
    t = proj.shape[0]

    def body(p_ref, da_ref, lg_ref, lb_ref, w_ref, bt_ref, dp_ref, dw_ref, dbt_ref, dlg_ref, dlb_ref, du_scr, dvn_scr):
        @pl.when(pl.program_id(0) == 0)
        def _():
            dw_ref[...] = jnp.zeros_like(dw_ref)
            dbt_ref[...] = jnp.zeros_like(dbt_ref)
            dlg_ref[...] = jnp.zeros_like(dlg_ref)
            dlb_ref[...] = jnp.zeros_like(dlb_ref)

        pv, u, vhat, rs, vnb = _a_common(p_ref, lg_ref, lb_ref)
        da = da_ref[...]
        for g in range(A_GROUPS):
            wt, keep = _tril_weights(w_ref, g)
            cs = slice(g * CHUNK, (g + 1) * CHUNK)
            for ch in range(A_TILE // CHUNK):
                rs_ = slice(ch * CHUNK, (ch + 1) * CHUNK)
                vg = vnb[rs_, cs]
                mixed = _dot(wt, vg, NN) + bt_ref[:, g:g + 1]
                du_scr[rs_, cs] = da[rs_, cs] * mixed
                dmx = da[rs_, cs] * u[rs_, cs]
                dw_ref[g] += jnp.where(keep, _dot(dmx, vg, NT), 0.0)
                dvn_scr[rs_, cs] = _dot(wt, dmx, TN)
                dbt_ref[:, g:g + 1] += jnp.sum(dmx, axis=1, keepdims=True)
        dvn = dvn_scr[...]
        dlg_ref[...] += jnp.sum(dvn * vhat, axis=0, keepdims=True)
        dlb_ref[...] += jnp.sum(dvn, axis=0, keepdims=True)
        dvh = dvn * lg_ref[...]
        dv = rs * (dvh - jnp.mean(dvh, axis=-1, keepdims=True) - vhat * jnp.mean(dvh * vhat, axis=-1, keepdims=True))
        gp = _gelu_grad(pv)
        dp_ref[:, :A_WIDTH] = (du_scr[...] * gp[:, :A_WIDTH]).astype(BF16)
        dp_ref[:, A_WIDTH:] = (dv * gp[:, A_WIDTH:]).astype(BF16)

    return pl.pallas_call(
        body, grid=(t // A_TILE,),
        in_specs=[pl.BlockSpec((A_TILE, 2 * A_WIDTH), lambda i: (i, 0)), pl.BlockSpec((A_TILE, A_WIDTH), lambda i: (i, 0)),
                  _const_spec((1, A_WIDTH)), _const_spec((1, A_WIDTH)), _const_spec((A_GROUPS, CHUNK, CHUNK)),
                  _const_spec((CHUNK, A_GROUPS))],
        out_specs=[pl.BlockSpec((A_TILE, 2 * A_WIDTH), lambda i: (i, 0)), _const_spec((A_GROUPS, CHUNK, CHUNK)),
                   _const_spec((CHUNK, A_GROUPS)), _const_spec((1, A_WIDTH)), _const_spec((1, A_WIDTH))],
        out_shape=[SDS((t, 2 * A_WIDTH), BF16), SDS((A_GROUPS, CHUNK, CHUNK), F32), SDS((CHUNK, A_GROUPS), F32),
                   SDS((1, A_WIDTH), F32), SDS((1, A_WIDTH), F32)],
        scratch_shapes=[pltpu.VMEM((A_TILE, A_WIDTH), F32), pltpu.VMEM((A_TILE, A_WIDTH), F32)],
        name="mixer_a_bwd", compiler_params=_params())(proj, dcat, lg, lb, w, bias_t)


def _rope_tables(t):
    half = ROPE_DIM // 2
    inv_freq = ROPE_THETA ** (-jnp.arange(half, dtype=F32) * 2.0 / ROPE_DIM)
    ang = jnp.arange(t, dtype=F32)[:, None] * inv_freq[None, :]
    cos, sin = jnp.cos(ang), jnp.sin(ang)
    one = jnp.ones((t, HEAD_DIM - ROPE_DIM), F32)
    zero = jnp.zeros((t, HEAD_DIM - ROPE_DIM), F32)
    zh = jnp.zeros((t, half), F32)
    c = jnp.concatenate([cos, cos, one], axis=1)
    s1 = jnp.concatenate([-sin, zh, zero], axis=1)
    s2 = jnp.concatenate([zh, sin, zero], axis=1)
    return tuple(jnp.tile(a, (1, LANES // HEAD_DIM)) for a in (c, s1, s2))


QK_TILE = 512
QK_COLS = 2 * N_DIL * B_WIDTH


CHUNKS = B_WIDTH // LANES


def _regroup_out(scr, first, out_ref, rate, tile):
    rows = tile // rate
    for rho in range(rate):
        for c in range(CHUNKS):
            out_ref[rho, :, c * LANES:(c + 1) * LANES] = scr[first + c, pl.ds(rho, rows, stride=rate), :].astype(out_ref.dtype)


def _regroup_in(x_ref, scr, rate, tile):
    rows = tile // rate
    for rho in range(rate):
        for c in range(CHUNKS):
            scr[c, pl.ds(rho, rows, stride=rate), :] = x_ref[rho, :, c * LANES:(c + 1) * LANES].astype(F32)


def _regrouped_spec(rate, tile):
    return pl.BlockSpec((rate, tile // rate, B_WIDTH), lambda i, *_: (0, i, 0))


def _qk_fwd(proj, gains, tabs, ride=None):
    t = proj.shape[0]
    col0 = 2 * A_WIDTH // 1024
    r1, r2 = DIL_RATES[1], DIL_RATES[2]

    def body(p_ref, g_ref, c_ref, s1_ref, s2_ref, o_ref, q1_ref, q2_ref, k1_ref, k2_ref, scr):
        seg = _segment_mean_matrix(HEAD_DIM)
        c, s1, s2 = c_ref[...], s1_ref[...], s2_ref[...]
        for ci in range(1024 // LANES):
            ls = slice(ci * LANES, (ci + 1) * LANES)
            xv = p_ref[:, ls].astype(F32)
            r = lax.rsqrt(_segment_dot(xv * xv, seg) + EPS)
            y = xv * r * g_ref[:, ls]
            val = y * c + pltpu.roll(y, LANES - 8, axis=1) * s1 + pltpu.roll(y, 8, axis=1) * s2
            o_ref[:, ls] = val.astype(BF16)
            scr[ci] = val

        j = pl.program_id(1)

        @pl.when(j == 0)
        def _():
            _regroup_out(scr, CHUNKS, q1_ref, r1, QK_TILE)

        @pl.when(j == 1)
        def _():
            _regroup_out(scr, 0, q2_ref, r2, QK_TILE)

        @pl.when(j == 2)
        def _():
            _regroup_out(scr, 0, k1_ref, r1, QK_TILE)
            _regroup_out(scr, CHUNKS, k2_ref, r2, QK_TILE)

    tab = pl.BlockSpec((QK_TILE, LANES), lambda i, j: (i, 0))
    g1, g2 = SDS((r1, t // r1, B_WIDTH), BF16), SDS((r2, t // r2, B_WIDTH), BF16)
    s1_, s2_ = _regrouped_spec(r1, QK_TILE), _regrouped_spec(r2, QK_TILE)
    return _call(
        body, grid=(t // QK_TILE, QK_COLS // 1024),
        in_specs=[pl.BlockSpec((QK_TILE, 1024), lambda i, j: (i, col0 + j)), pl.BlockSpec((1, 1024), lambda i, j: (0, j)),
                  tab, tab, tab],
        out_specs=[pl.BlockSpec((QK_TILE, 1024), lambda i, j: (i, j)), s1_, s2_, s1_, s2_],
        out_shape=[SDS((t, QK_COLS), BF16), g1, g2, g1, g2],
        scratch_shapes=[pltpu.VMEM((2 * CHUNKS, QK_TILE, LANES), F32)],
        operands=[proj, gains, *tabs], name="qk_norm_rope_fwd", ride=ride)


PERM_TILE = 512


def _permute(name, items, rate):
    t = items[0][0].shape[0]
    n = len(items)

    def body(*refs):
        scr = refs[-1]
        for x_ref, o_ref in zip(refs[:n], refs[n:2 * n]):
            for ci in range(CHUNKS):
                scr[ci] = x_ref[:, ci * LANES:(ci + 1) * LANES].astype(F32)
            _regroup_out(scr, 0, o_ref, rate, PERM_TILE)

    return pl.pallas_call(
        body, grid=(t // PERM_TILE,),
        in_specs=[pl.BlockSpec((PERM_TILE, B_WIDTH), functools.partial(lambda cb, i: (i, cb), cb)) for _, cb in items],
        out_specs=[_regrouped_spec(rate, PERM_TILE) for _ in items],
        out_shape=[SDS((rate, t // rate, B_WIDTH), a.dtype) for a, _ in items],
        scratch_shapes=[pltpu.VMEM((CHUNKS, PERM_TILE, LANES), F32)],
        name=name, compiler_params=_params())(*[a for a, _ in items])


def _head_lane_mask(h):
    lane = lax.broadcasted_iota(jnp.int32, (1, LANES), 1)
    return (lane < HEAD_DIM) if h == 0 else (lane >= HEAD_DIM)


def _attn_fwd(name, q, k, v, ride=None):
    rate, length = q[0].shape[0], q[0].shape[1]
    nb = length // ATT_BLOCK
    scale = HEAD_DIM ** -0.5

    def body(q_ref, kc_ref, kp_ref, vc_ref, vp_ref, o_ref, l_ref):
        n = pl.program_id(1)
        qi = lax.broadcasted_iota(jnp.int32, (ATT_BLOCK, 2 * ATT_BLOCK), 0)
        cj = lax.broadcasted_iota(jnp.int32, (ATT_BLOCK, 2 * ATT_BLOCK), 1)
        has_prev = jnp.where(n > 0, 0, 2 * ATT_BLOCK)
        mask = ((cj < ATT_BLOCK) & (cj >= qi + has_prev)) | ((cj >= ATT_BLOCK) & (cj - ATT_BLOCK <= qi))
        for hp in range(B_WIDTH // LANES):
            ls = slice(hp * LANES, (hp + 1) * LANES)
            q2 = q_ref[:, ls]
            k2 = jnp.concatenate([kp_ref[:, ls], kc_ref[:, ls]], axis=0)
            v2 = jnp.concatenate([vp_ref[:, ls], vc_ref[:, ls]], axis=0)
            o_acc, lse2 = None, None
            for h in range(2):
                hm = _head_lane_mask(h)
                s = _dot(jnp.where(hm, q2, jnp.zeros_like(q2)), k2, NT) * scale
                s = jnp.where(mask, s, NEG_INF)
                m = jnp.max(s, axis=1, keepdims=True)
                p = jnp.exp(s - m)
                den = jnp.sum(p, axis=1, keepdims=True)
                lse = m + jnp.log(den)
                o = _dot(p / den, jnp.where(hm, v2, jnp.zeros_like(v2)), NN)
                o_acc = o if h == 0 else o_acc + o
                lse_b = lse + jnp.zeros((ATT_BLOCK, LANES), F32)
                lse2 = lse_b if h == 0 else jnp.where(hm, lse_b, lse2)
            o_ref[:, ls] = o_acc
            l_ref[:, ls] = lse2

    def cur(cb):
        return pl.BlockSpec((None, ATT_BLOCK, B_WIDTH), lambda r, n: (r, n, cb))

    def prev(cb):
        return pl.BlockSpec((None, ATT_BLOCK, B_WIDTH), lambda r, n: (r, jnp.maximum(n - 1, 0), cb))

    out = pl.BlockSpec((None, ATT_BLOCK, B_WIDTH), lambda r, n: (r, n, 0))
    return _call(
        body, grid=(rate, nb),
        in_specs=[cur(q[1]), cur(k[1]), prev(k[1]), cur(v[1]), prev(v[1])],
        out_specs=[out, out], out_shape=[SDS((rate, length, B_WIDTH), F32)] * 2,
        operands=[q[0], k[0], k[0], v[0], v[0]], name=name, ride=ride)


def _attn_merge(a_out, o_list, l_list):
    t = a_out.shape[0]
    tm = PERM_TILE
    r1, r2 = DIL_RATES[1], DIL_RATES[2]

    def body(a_ref, o0, o1, o2, l0, l1, l2, cat_ref, lt_ref, lt1_ref, lt2_ref, so1, so2, sl1, sl2, slt):
        _regroup_in(o1, so1, r1, tm)
        _regroup_in(l1, sl1, r1, tm)
        _regroup_in(o2, so2, r2, tm)
        _regroup_in(l2, sl2, r2, tm)
        cat_ref[:, :A_WIDTH] = a_ref[...]
        for c in range(CHUNKS):
            ls = slice(c * LANES, (c + 1) * LANES)
            lg = [l0[:, ls], sl1[c], sl2[c]]
            m = jnp.maximum(jnp.maximum(lg[0], lg[1]), lg[2])
            es = [jnp.exp(l - m) for l in lg]
            den = es[0] + es[1] + es[2]
            b = (es[0] * o0[:, ls] + es[1] * so1[c] + es[2] * so2[c]) / den
            cat_ref[:, A_WIDTH + c * LANES:A_WIDTH + (c + 1) * LANES] = b.astype(BF16)
            lt = m + jnp.log(den)
            lt_ref[:, ls] = lt
            slt[c] = lt
        _regroup_out(slt, 0, lt1_ref, r1, tm)
        _regroup_out(slt, 0, lt2_ref, r2, tm)

    blk = _row_spec(tm, B_WIDTH)
    g1, g2 = _regrouped_spec(r1, tm), _regrouped_spec(r2, tm)
    return pl.pallas_call(
        body, grid=(t // tm,), in_specs=[blk, blk, g1, g2, blk, g1, g2],
        out_specs=[_row_spec(tm, A_WIDTH + B_WIDTH), blk, g1, g2],
        out_shape=[SDS((t, A_WIDTH + B_WIDTH), BF16), SDS((t, B_WIDTH), F32), SDS((r1, t // r1, B_WIDTH), F32),
                   SDS((r2, t // r2, B_WIDTH), F32)],
        scratch_shapes=[pltpu.VMEM((CHUNKS, tm, LANES), F32)] * 5,
        name="attn_merge", compiler_params=_params())(a_out, *o_list, *l_list)


def _attn_bwd_prep(dcat, cat):
    t = dcat.shape[0]
    tm = PERM_TILE
    r1, r2 = DIL_RATES[1], DIL_RATES[2]

    def body(d_ref, b_ref, db_ref, dd_ref, db1_ref, dd1_ref, db2_ref, dd2_ref, sdb, sdd):
        seg = _segment_mean_matrix(HEAD_DIM, scale=1.0)
        for c in range(CHUNKS):
            ls = slice(c * LANES, (c + 1) * LANES)
            d = d_ref[:, ls]
            dsum = _segment_dot(d * b_ref[:, ls].astype(F32), seg)
            db_ref[:, ls] = d.astype(BF16)
            dd_ref[:, ls] = dsum
            sdb[c] = d
            sdd[c] = dsum
        _regroup_out(sdb, 0, db1_ref, r1, tm)
        _regroup_out(sdd, 0, dd1_ref, r1, tm)
        _regroup_out(sdb, 0, db2_ref, r2, tm)
        _regroup_out(sdd, 0, dd2_ref, r2, tm)

    right = pl.BlockSpec((tm, B_WIDTH), lambda i: (i, 1))
    blk = _row_spec(tm, B_WIDTH)
    g1, g2 = _regrouped_spec(r1, tm), _regrouped_spec(r2, tm)
    return pl.pallas_call(
        body, grid=(t // tm,), in_specs=[right, right], out_specs=[blk, blk, g1, g1, g2, g2],
        out_shape=[SDS((t, B_WIDTH), BF16), SDS((t, B_WIDTH), F32), SDS((r1, t // r1, B_WIDTH), BF16),
                   SDS((r1, t // r1, B_WIDTH), F32), SDS((r2, t // r2, B_WIDTH), BF16), SDS((r2, t // r2, B_WIDTH), F32)],
        scratch_shapes=[pltpu.VMEM((CHUNKS, tm, LANES), F32)] * 2,
        name="attn_bwd_prep", compiler_params=_params())(dcat, cat)


def _attn_bwd(name, q, k, v, db, lse, dd, ride=None):
    rate, length = db.shape[0], db.shape[1]
    nb = length // ATT_BLOCK
    scale = HEAD_DIM ** -0.5

    def body(qa_ref, qb_ref, k_ref, v_ref, dba_ref, dbb_ref, la_ref, lb_ref, da_ref, dbd_ref, dq_ref, dk_ref, dv_ref, carry):
        m = pl.program_id(1)

        @pl.when(m == 0)
        def _():
            carry[...] = jnp.zeros_like(carry)

        row = lax.broadcasted_iota(jnp.int32, (2 * ATT_BLOCK, ATT_BLOCK), 0)
        kj = lax.broadcasted_iota(jnp.int32, (2 * ATT_BLOCK, ATT_BLOCK), 1)
        no_next = jnp.where(m + 1 < nb, 0, 2 * ATT_BLOCK)
        mask = ((row < ATT_BLOCK) & (kj <= row)) | ((row >= ATT_BLOCK) & (kj >= row - ATT_BLOCK + no_next))
        for hp in range(B_WIDTH // LANES):
            ls = slice(hp * LANES, (hp + 1) * LANES)
            k2, v2 = k_ref[:, ls], v_ref[:, ls]
            q2 = jnp.concatenate([qa_ref[:, ls], qb_ref[:, ls]], axis=0)
            db2 = jnp.concatenate([dba_ref[:, ls], dbb_ref[:, ls]], axis=0)
            lse2 = jnp.concatenate([la_ref[:, ls], lb_ref[:, ls]], axis=0)
            dd2 = jnp.concatenate([da_ref[:, ls], dbd_ref[:, ls]], axis=0)
            dq_acc, dk_acc, dv_acc = None, None, None
            for h in range(2):
                hm = _head_lane_mask(h)
                km = jnp.where(hm, k2, jnp.zeros_like(k2))
                vm = jnp.where(hm, v2, jnp.zeros_like(v2))
                lse_col = jnp.max(jnp.where(hm, lse2, NEG_INF), axis=1, keepdims=True)
                dd_col = jnp.max(jnp.where(hm, dd2, NEG_INF), axis=1, keepdims=True)
                s = _dot(q2, km, NT) * scale
                p = jnp.where(mask, jnp.exp(s - lse_col), 0.0)
                dvc = _dot(p, jnp.where(hm, db2, jnp.zeros_like(db2)), TN)
                dp = _dot(db2, vm, NT)
                ds = (p * (dp - dd_col) * scale).astype(BF16)
                dqc = _dot(ds, km, NN)
                dkc = _dot(ds, jnp.where(hm, q2, jnp.zeros_like(q2)), TN)
                dq_acc = dqc if dq_acc is None else dq_acc + dqc
                dk_acc = dkc if dk_acc is None else dk_acc + dkc
                dv_acc = dvc if dv_acc is None else dv_acc + dvc
            dq_ref[:, ls] = (dq_acc[:ATT_BLOCK] + carry[:, ls]).astype(BF16)
            carry[:, ls] = dq_acc[ATT_BLOCK:]
            dk_ref[:, ls] = dk_acc.astype(BF16)
            dv_ref[:, ls] = dv_acc.astype(BF16)

    def cur(cb):
        return pl.BlockSpec((None, ATT_BLOCK, B_WIDTH), lambda r, n: (r, n, cb))

    def nxt(cb):
        return pl.BlockSpec((None, ATT_BLOCK, B_WIDTH), lambda r, n: (r, jnp.minimum(n + 1, nb - 1), cb))

    out = cur(0)
    return _call(
        body, grid=(rate, nb),
        in_specs=[cur(q[1]), nxt(q[1]), cur(k[1]), cur(v[1]), cur(0), nxt(0), cur(0), nxt(0), cur(0), nxt(0)],
        out_specs=[out, out, out], out_shape=[SDS((rate, length, B_WIDTH), BF16)] * 3,
        scratch_shapes=[pltpu.VMEM((ATT_BLOCK, B_WIDTH), F32)],
        operands=[q[0], q[0], k[0], v[0], db, db, lse, lse, dd, dd], name=name, ride=ride)


AB_IN = 2 * A_WIDTH + 3 * N_DIL * B_WIDTH
ASM_TILE = 256


def _dproj_assemble(proj, d_a, dq, dk, dv, gains, tabs):
    t = proj.shape[0]
    n_in = 3 * N_DIL

    def body(p_ref, da_ref, *rest):
        grads = rest[:n_in]
        g_ref, c_ref, s1_ref, s2_ref, o_ref, dg_ref = rest[n_in:n_in + 6]
        scratch = rest[n_in + 6:]

        @pl.when(pl.program_id(0) == 0)
        def _():
            dg_ref[...] = jnp.zeros_like(dg_ref)

        chunk = {}
        k_scr = 0
        for j in range(n_in):
            g = j % N_DIL
            if DIL_RATES[g] == 1:
                for ci in range(CHUNKS):
                    chunk[j, ci] = functools.partial(lambda r, ci: r[:, ci * LANES:(ci + 1) * LANES].astype(F32), grads[j], ci)
            else:
                scr = scratch[k_scr]
                k_scr += 1
                _regroup_in(grads[j], scr, DIL_RATES[g], ASM_TILE)
                for ci in range(CHUNKS):
                    chunk[j, ci] = functools.partial(lambda s, ci: s[ci], scr, ci)

        seg = _segment_mean_matrix(HEAD_DIM)
        c, s1, s2 = c_ref[...], s1_ref[...], s2_ref[...]
        o_ref[:, :2 * A_WIDTH] = da_ref[...]
        for jg in range(2 * N_DIL):
            for ci in range(CHUNKS):
                col = jg * B_WIDTH + ci * LANES
                src = slice(2 * A_WIDTH + col, 2 * A_WIDTH + col + LANES)
                xv = p_ref[:, src].astype(F32)
                r = lax.rsqrt(_segment_dot(xv * xv, seg) + EPS)
                xh = xv * r
                gain = g_ref[:, col:col + LANES]
                do = chunk[jg, ci]()
                dy = do * c + pltpu.roll(do * s1, 8, axis=1) + pltpu.roll(do * s2, LANES - 8, axis=1)
                dg_ref[:, col:col + LANES] += jnp.sum(dy * xh, axis=0, keepdims=True)
                dxh = dy * gain
                o_ref[:, src] = (r * (dxh - xh * _segment_dot(dxh * xh, seg))).astype(BF16)
        v0 = 2 * A_WIDTH + QK_COLS
        for g in range(N_DIL):
            for ci in range(CHUNKS):
                col = v0 + g * B_WIDTH + ci * LANES
                o_ref[:, col:col + LANES] = chunk[2 * N_DIL + g, ci]().astype(BF16)

    specs = [_row_spec(ASM_TILE, B_WIDTH) if r == 1 else _regrouped_spec(r, ASM_TILE) for r in DIL_RATES] * 3
    n_scr = 3 * sum(1 for r in DIL_RATES if r > 1)
    tab = _row_spec(ASM_TILE, LANES)
    return pl.pallas_call(
        body, grid=(t // ASM_TILE,),
        in_specs=[_row_spec(ASM_TILE, AB_IN), _row_spec(ASM_TILE, 2 * A_WIDTH)] + specs
        + [_const_spec((1, QK_COLS)), tab, tab, tab],
        out_specs=[_row_spec(ASM_TILE, AB_IN), _const_spec((1, QK_COLS))],
        out_shape=[SDS((t, AB_IN), BF16), SDS((1, QK_COLS), F32)],
        scratch_shapes=[pltpu.VMEM((CHUNKS, ASM_TILE, LANES), F32)] * n_scr,
        name="dproj_assemble", compiler_params=_params())(proj, d_a, *dq, *dk, *dv, gains, *tabs)


def _fold_heads(dg_lane):
    n = dg_lane.shape[1]

    def body(x_ref, o_ref):
        r = lax.broadcasted_iota(jnp.int32, (B_WIDTH, B_WIDTH), 0) % HEAD_DIM
        c = lax.broadcasted_iota(jnp.int32, (B_WIDTH, B_WIDTH), 1) % HEAD_DIM
        fold = jnp.where(r == c, 1.0, 0.0).astype(F32)
        for jg in range(n // B_WIDTH):
            ls = slice(jg * B_WIDTH, (jg + 1) * B_WIDTH)
            o_ref[:, ls] = _dot_hi(jnp.broadcast_to(x_ref[:, ls], (8, B_WIDTH)), fold)

    return pl.pallas_call(body, out_shape=SDS((8, n), F32), name="fold_heads", compiler_params=_params())(dg_lane)


CD_TILE = 256
CD_IN = 2 * C_WIDTH + 3 * 512


def _cd_split(pv):
    w = C_WIDTH
    return pv[:, :w], pv[:, w:2 * w], pv[:, 2 * w:3 * w], pv[:, 3 * w:4 * w], pv[:, 4 * w:5 * w]


def _shifted_copies(src, dst, rows):
    dst[0, :rows] = src[...]
    for b in range(1, 8):
        dst[b, :rows - 8] = src[pl.ds(b, rows - 8), :]


def _rows_from(shifted, start, n):
    b = start % 8
    return shifted[b, pl.ds(start - b, n), :]


def _mixer_cd_fwd(proj, cw, cb, lg, lb, dw):
    t = proj.shape[0]
    per = CD_TILE // HALO

    def body(h_ref, m_ref, cw_ref, cb_ref, lg_ref, lb_ref, dw_ref, o_ref, c1_ref, c_scr, e_scr, c_sh):
        not_first = (pl.program_id(0) > 0).astype(F32)
        ha, hg, _, hgc, hhv = _cd_split(h_ref[...].astype(F32))
        ma, mg, mgb, mgc, mhv = _cd_split(m_ref[...].astype(F32))
        c_scr[:HALO] = ha * _sigmoid(hg) * not_first
        c_scr[HALO:] = ma * _sigmoid(mg)
        e_scr[:HALO] = hgc * hhv * not_first
        e_scr[HALO:] = mgc * mhv
        _shifted_copies(c_scr, c_sh, HALO + CD_TILE)
        acc = jnp.zeros((CD_TILE, C_WIDTH), F32)
        for k in range(C_KERNEL):
            acc = acc + cw_ref[k:k + 1, :] * _rows_from(c_sh, HALO - (C_KERNEL - 1) + k, CD_TILE)
        c1 = acc + cb_ref[...]
        c1_ref[...] = c1
        cc = c1 - jnp.mean(c1, axis=-1, keepdims=True)
        c2 = cc * lax.rsqrt(jnp.mean(cc * cc, axis=-1, keepdims=True) + EPS) * lg_ref[...] + lb_ref[...]
        o_ref[:, :C_WIDTH] = (c2 * _sigmoid(c2)).astype(BF16)
        d1 = jnp.zeros((CD_TILE, C_WIDTH), F32)
        for k in range(D_KERNEL):
            d1 = d1 + dw_ref[k:k + 1, :] * e_scr[pl.ds(HALO - (D_KERNEL - 1) + k, CD_TILE), :]
        o_ref[:, C_WIDTH:] = (mgb * d1).astype(BF16)

    return pl.pallas_call(
        body, grid=(t // CD_TILE,),
        in_specs=[pl.BlockSpec((HALO, CD_IN), lambda i: (jnp.maximum(i * per - 1, 0), 0)), _row_spec(CD_TILE, CD_IN),
                  _const_spec((32, C_WIDTH)), _const_spec((1, C_WIDTH)), _const_spec((1, C_WIDTH)), _const_spec((1, C_WIDTH)),
                  _const_spec((8, C_WIDTH))],
        out_specs=[_row_spec(CD_TILE, 2 * C_WIDTH), _row_spec(CD_TILE, C_WIDTH)],
        out_shape=[SDS((t, 2 * C_WIDTH), BF16), SDS((t, C_WIDTH), F32)],
        scratch_shapes=[pltpu.VMEM((HALO + CD_TILE, C_WIDTH), F32)] * 2 + [pltpu.VMEM((8, HALO + CD_TILE, C_WIDTH), F32)],
        name="mixer_cd_fwd", compiler_params=_params())(proj, proj, cw, cb, lg, lb, dw)


def _mixer_cd_bwd(proj, dcat, c1, cw, lg, lb, dw, ride=None):
    t = proj.shape[0]
    per = CD_TILE // HALO
    nt = t // CD_TILE
    ext = CD_TILE + HALO

    def body(hp_ref, m_ref, hn_ref, dm_ref, dn_ref, c1m_ref, c1n_ref, cw_ref, lg_ref, lb_ref, dw_ref,
             dp_ref, dcw_ref, dcb_ref, dlg_ref, dlb_ref, ddw_ref, c_scr, e_scr, dc1_scr, dd1_scr, c_sh, dc1_sh, dcw_acc):
        i = pl.program_id(0)

        @pl.when(i == 0)
        def _():
            for r in (dcw_acc, dcb_ref, dlg_ref, dlb_ref, ddw_ref):
                r[...] = jnp.zeros_like(r)

        not_first = (i > 0).astype(F32)
        not_last = (i < nt - 1).astype(F32)
        pa, pg, _, pgc, phv = _cd_split(hp_ref[...].astype(F32))
        ma, mg, mgb, mgc, mhv = _cd_split(m_ref[...].astype(F32))
        na, ng, ngb, ngc, nhv = _cd_split(hn_ref[...].astype(F32))
        sig_m = _sigmoid(mg)
        c_scr[:HALO] = pa * _sigmoid(pg) * not_first
        c_scr[HALO:HALO + CD_TILE] = ma * sig_m
        c_scr[HALO + CD_TILE:] = na * _sigmoid(ng) * not_last
        e_scr[:HALO] = pgc * phv * not_first
        e_scr[HALO:HALO + CD_TILE] = mgc * mhv
        e_scr[HALO + CD_TILE:] = ngc * nhv * not_last

        _shifted_copies(c_scr, c_sh, 2 * HALO + CD_TILE)
        c1 = jnp.concatenate([c1m_ref[...], c1n_ref[...]], axis=0)
        cc = c1 - jnp.mean(c1, axis=-1, keepdims=True)
        rs = lax.rsqrt(jnp.mean(cc * cc, axis=-1, keepdims=True) + EPS)
        vhat = cc * rs
        c2 = vhat * lg_ref[...] + lb_ref[...]
        sig = _sigmoid(c2)
        dc = jnp.concatenate([dm_ref[:, :C_WIDTH], dn_ref[:, :C_WIDTH] * not_last], axis=0)
        dc2 = dc * (sig * (1.0 + c2 * (1.0 - sig)))
        dvh = dc2 * lg_ref[...]
        dc1 = rs * (dvh - jnp.mean(dvh, axis=-1, keepdims=True) - vhat * jnp.mean(dvh * vhat, axis=-1, keepdims=True))
        dc1_scr[...] = dc1
        _shifted_copies(dc1_scr, dc1_sh, ext)
        dlg_ref[...] += jnp.sum((dc2 * vhat)[:CD_TILE], axis=0, keepdims=True)
        dlb_ref[...] += jnp.sum(dc2[:CD_TILE], axis=0, keepdims=True)
        dc1_m = dc1[:CD_TILE]
        dcb_ref[...] += jnp.sum(dc1_m, axis=0, keepdims=True)
        dc0 = jnp.zeros((CD_TILE, C_WIDTH), F32)
        for k in range(C_KERNEL):
            dc0 = dc0 + cw_ref[k:k + 1, :] * _rows_from(dc1_sh, C_KERNEL - 1 - k, CD_TILE)
            prod = dc1_m * _rows_from(c_sh, HALO - (C_KERNEL - 1) + k, CD_TILE)
            dcw_acc[k] += prod.reshape(CD_TILE // 8, 8, C_WIDTH).sum(axis=0)

        @pl.when(i == nt - 1)
        def _():
            dcw_ref[...] = jnp.sum(dcw_acc[...], axis=1)

        dp_ref[:, :C_WIDTH] = (dc0 * sig_m).astype(BF16)
        dp_ref[:, C_WIDTH:2 * C_WIDTH] = (dc0 * ma * sig_m * (1.0 - sig_m)).astype(BF16)

        d1 = jnp.zeros((CD_TILE, C_WIDTH), F32)
        for k in range(D_KERNEL):
            d1 = d1 + dw_ref[k:k + 1, :] * e_scr[pl.ds(HALO - (D_KERNEL - 1) + k, CD_TILE), :]
        dd_m = dm_ref[:, C_WIDTH:]
        dd1 = jnp.concatenate([dd_m * mgb, dn_ref[:, C_WIDTH:] * ngb * not_last], axis=0)
        dd1_scr[...] = dd1
        dp_ref[:, 2 * C_WIDTH:3 * C_WIDTH] = (dd_m * d1).astype(BF16)
        de = jnp.zeros((CD_TILE, C_WIDTH), F32)
        for k in range(D_KERNEL):
            de = de + dw_ref[k:k + 1, :] * dd1_scr[pl.ds(D_KERNEL - 1 - k, CD_TILE), :]
            ddw_ref[k:k + 1, :] += jnp.sum(dd1[:CD_TILE] * e_scr[pl.ds(HALO - (D_KERNEL - 1) + k, CD_TILE), :], axis=0, keepdims=True)
        dp_ref[:, 3 * C_WIDTH:4 * C_WIDTH] = (de * mhv).astype(BF16)
        dp_ref[:, 4 * C_WIDTH:] = (de * mgc).astype(BF16)

    halo_prev = lambda i: (jnp.maximum(i * per - 1, 0), 0)
    halo_next = lambda i: (jnp.minimum((i + 1) * per, t // HALO - 1), 0)
    vec = _const_spec((1, C_WIDTH))
    return _call(
        body, grid=(nt,),
        in_specs=[pl.BlockSpec((HALO, CD_IN), halo_prev), _row_spec(CD_TILE, CD_IN), pl.BlockSpec((HALO, CD_IN), halo_next),
                  _row_spec(CD_TILE, 2 * C_WIDTH), pl.BlockSpec((HALO, 2 * C_WIDTH), halo_next),
                  _row_spec(CD_TILE, C_WIDTH), pl.BlockSpec((HALO, C_WIDTH), halo_next),
                  _const_spec((32, C_WIDTH)), vec, vec, _const_spec((8, C_WIDTH))],
        out_specs=[_row_spec(CD_TILE, CD_IN), _const_spec((32, C_WIDTH)), vec, vec, vec, _const_spec((8, C_WIDTH))],
        out_shape=[SDS((t, CD_IN), BF16), SDS((32, C_WIDTH), F32), SDS((1, C_WIDTH), F32), SDS((1, C_WIDTH), F32),
                   SDS((1, C_WIDTH), F32), SDS((8, C_WIDTH), F32)],
        scratch_shapes=[pltpu.VMEM((2 * HALO + CD_TILE, C_WIDTH), F32)] * 2 + [pltpu.VMEM((ext, C_WIDTH), F32)] * 2
        + [pltpu.VMEM((8, 2 * HALO + CD_TILE, C_WIDTH), F32), pltpu.VMEM((8, ext, C_WIDTH), F32),
           pltpu.VMEM((32, 8, C_WIDTH), F32)],
        operands=[proj, proj, proj, dcat, dcat, c1, c1, cw, lg, lb, dw], name="mixer_cd_bwd", ride=ride)


def _wgrad(name, pairs, out_rc, t, ride):
    tk = TILES["wgrad"]
    r, c = out_rc
    n = len(pairs)

    def body(*refs):
        ab, out_refs = refs[:2 * n], refs[2 * n:]
        k = pl.program_id(1)
        parts = [_dot(ab[2 * j][...], ab[2 * j + 1][...], TN) for j in range(n)]

        @pl.when(k == 0)
        def _():
            for a in range(n):
                out_refs[a][...] = parts[a]

        @pl.when(k > 0)
        def _():
            for a in range(n):
                out_refs[a][...] += parts[a]

    operands, in_specs = [], []
    for lhs, lhs_spec, rhs, rhs_spec in pairs:
        operands += [lhs, rhs]
        in_specs += [lhs_spec, rhs_spec]
    res = _call(body, grid=(N_CHIPS, t // tk), in_specs=in_specs,
                out_specs=[pl.BlockSpec((None, r, c), lambda p, k: (p, 0, 0))] * n,
                out_shape=[SDS((N_CHIPS, r, c), F32)] * n, operands=operands, name=name, ride=ride)
    outs, ride_res = (res, None) if ride is None else res
    outs = [o.reshape(N_CHIPS, 2, r // 2, c) for o in outs]
    return outs if ride is None else (outs, ride_res)


def _wgrad_col_sharded(name, h, dz_list, three_d, ride=None):
    t, d = h.shape
    tk = TILES["wgrad"]
    n4 = dz_list[0].shape[-1] if three_d else dz_list[0].shape[-1] // N_CHIPS
    hs = pl.BlockSpec((tk, d), lambda p, k: (k, 0))
    zs = pl.BlockSpec((None, tk, n4), lambda p, k: (p, k, 0)) if three_d else pl.BlockSpec((tk, n4), lambda p, k: (k, p))
    return _wgrad(name, [(h, hs, dz, zs) for dz in dz_list], (d, n4), t, ride)


def _wgrad_row_sharded(name, a, g, three_d, ride=None):
    many = isinstance(a, (list, tuple))
    a_list = list(a) if many else [a]
    t, d = g.shape
    tk = TILES["wgrad"]
    k4 = a_list[0].shape[-1] if three_d else a_list[0].shape[-1] // N_CHIPS
    a_spec = pl.BlockSpec((None, tk, k4), lambda p, k: (p, k, 0)) if three_d else pl.BlockSpec((tk, k4), lambda p, k: (k, p))
    gs = pl.BlockSpec((tk, d), lambda p, k: (k, 0))
    res = _wgrad(name, [(a_j, a_spec, g, gs) for a_j in a_list], (k4, d), t, ride)
    if many:
        return res
    return res[0] if ride is None else (res[0][0], res[1])


def _mesh_scalars():
    return jnp.stack([lax.axis_index("c"), 2 * lax.axis_index("x") + lax.axis_index("y")]).astype(jnp.int32)


def _stage_own(name, w, layer, dtype):
    layers, r, cols = w.shape
    h = r // 2

    def body(s_ref, x_ref, o_ref):
        o_ref[...] = x_ref[...].astype(dtype)

    return pl.pallas_call(
        body,
        grid_spec=pltpu.PrefetchScalarGridSpec(
            num_scalar_prefetch=1, grid=(2,),
            in_specs=[pl.BlockSpec((None, h, cols), lambda i, s: (2 * layer + i, 0, 0))],
            out_specs=pl.BlockSpec((None, None, h, cols), lambda i, s: (s[1], i, 0, 0))),
        out_shape=SDS((N_CHIPS, 2, h, cols), dtype), name=name,
        compiler_params=_params())(_mesh_scalars(), w.reshape(2 * layers, h, cols))


def _remote(src, dst, send_sem, recv_sem, device):
    return pltpu.make_async_remote_copy(src, dst, send_sem, recv_sem, device_id=device, device_id_type=MESH)


def _ride_gather_send(bufs):
    n = len(bufs)

    def each(b, sems, act):
        send, recv = sems
        x, y, c, p, others = _position()
        for t in range(n):
            for j, (qx, qy) in enumerate(others):
                act(b[t].at[p, c], b[t].at[2 * qx + qy, c], send.at[t, j], recv.at[t, j], (qx, qy, c))

    def start(ins, b, new, sems):
        each(b, sems, lambda mine, landed, s, r, dev: _remote(mine, mine, s, r, dev).start())

    def finish(ins, b, new, sems):
        def act(mine, landed, s, r, dev):
            _remote(mine, mine, s, r, dev).wait_send()
            _remote(landed, landed, s, r, dev).wait_recv()
        each(b, sems, act)

    return _Ride([], bufs, [], [(n, 3), (n, 3)], start, finish)


def _ride_gather_pass(bufs):
    n = len(bufs)

    def each(b, sems, act):
        send, recv = sems
        x, y, c, p, others = _position()
        for t in range(n):
            for j, (qx, qy) in enumerate(others):
                act(b[t].at[2 * qx + qy, c], b[t].at[2 * qx + qy, 1 - c], send.at[t, j], recv.at[t, j], (x, y, 1 - c))

    def start(ins, b, new, sems):
        each(b, sems, lambda landed, passed, s, r, dev: _remote(landed, landed, s, r, dev).start())

    def finish(ins, b, new, sems):
        def act(landed, passed, s, r, dev):
            _remote(landed, landed, s, r, dev).wait_send()
            _remote(passed, passed, s, r, dev).wait_recv()
        each(b, sems, act)

    return _Ride([], bufs, [], [(n, 3), (n, 3)], start, finish)


def _ride_swap(tensors):
    n = len(tensors)

    def each(ins, new, sems, act):
        send, recv = sems
        x, y, c, _, _ = _position()
        for t in range(n):
            act(_remote(ins[t].at[:, 1 - c], new[t], send.at[t], recv.at[t], (x, y, 1 - c)))

    def start(ins, b, new, sems):
        each(ins, new, sems, lambda cp: cp.start())

    def finish(ins, b, new, sems):
        each(ins, new, sems, lambda cp: cp.wait())

    return _Ride(tensors, [], [SDS((s.shape[0],) + s.shape[2:], s.dtype) for s in tensors], [(n,), (n,)], start, finish)


def _ride_scatter(tensors, landing):
    n = len(tensors)

    def each(ins, b, sems, act):
        send, recv = sems
        x, y, c, p, others = _position()
        for t in range(n):
            for j, (qx, qy) in enumerate(others):
                q = 2 * qx + qy
                act(ins[t].at[q], b[t].at[p], b[t].at[q], send.at[t, j], recv.at[t, j], (qx, qy, c))

    def start(ins, b, new, sems):
        each(ins, b, sems, lambda src, dst, landed, s, r, dev: _remote(src, dst, s, r, dev).start())

    def finish(ins, b, new, sems):
        def act(src, dst, landed, s, r, dev):
            _remote(src, dst, s, r, dev).wait_send()
            _remote(landed, landed, s, r, dev).wait_recv()
        each(ins, b, sems, act)

    return _Ride(tensors, landing, [], [(n, 3), (n, 3)], start, finish)


def _ride_join(bufs):
    n = len(bufs)

    def each(b, sems, act):
        send, recv = sems
        x, y, c, _, _ = _position()
        for t in range(n):
            act(b[t].at[c], b[t].at[1 - c], send.at[t], recv.at[t], (x, y, 1 - c))

    def start(ins, b, new, sems):
        each(b, sems, lambda mine, theirs, s, r, dev: _remote(mine, mine, s, r, dev).start())

    def finish(ins, b, new, sems):
        def act(mine, theirs, s, r, dev):
            _remote(mine, mine, s, r, dev).wait_send()
            _remote(theirs, theirs, s, r, dev).wait_recv()
        each(b, sems, act)

    return _Ride([], bufs, [], [(n,), (n,)], start, finish)


def _all_reduce_small(pack):
    rows = pack.shape[0]
    n_dev = 2 * N_CHIPS

    def body(x_ref, o_ref, land, send, recv):
        x, y, c, p, _ = _position()
        me = 2 * p + c
        land[me] = x_ref[...]
        peers = [(dx, dy, dc) for dx in range(2) for dy in range(2) for dc in range(2) if (dx, dy, dc) != (0, 0, 0)]
        for j, (dx, dy, dc) in enumerate(peers):
            _remote(land.at[me], land.at[me], send.at[j], recv.at[j], (x ^ dx, y ^ dy, c ^ dc)).start()
        for j, (dx, dy, dc) in enumerate(peers):
            src = 4 * (x ^ dx) + 2 * (y ^ dy) + (c ^ dc)
            _remote(land.at[me], land.at[me], send.at[j], recv.at[j], (x ^ dx, y ^ dy, c ^ dc)).wait_send()
            _remote(land.at[src], land.at[src], send.at[j], recv.at[j], (x ^ dx, y ^ dy, c ^ dc)).wait_recv()
        acc = land[0]
        for dev in range(1, n_dev):
            acc = acc + land[dev]
        o_ref[...] = acc

    return pl.pallas_call(
        body, out_shape=SDS((rows, LANES), F32),
        scratch_shapes=[pltpu.VMEM((n_dev, rows, LANES), F32), pltpu.SemaphoreType.DMA((n_dev - 1,)),
                        pltpu.SemaphoreType.DMA((n_dev - 1,))],
        name="all_reduce_small", compiler_params=_params())(pack)


def _add_own_half(name, full, recv, out_dtype):
    n4, _, h, cols = full.shape

    def body(s_ref, a_ref, b_ref, o_ref, own_ref):
        v = (a_ref[...] + b_ref[...]).astype(out_dtype)
        o_ref[...] = v

        @pl.when(pl.program_id(0) == s_ref[1])
        def _():
            own_ref[...] = v

    return pl.pallas_call(
        body,
        grid_spec=pltpu.PrefetchScalarGridSpec(
            num_scalar_prefetch=1, grid=(n4,),
            in_specs=[pl.BlockSpec((None, None, h, cols), lambda q, s: (q, s[0], 0, 0)),
                      pl.BlockSpec((None, h, cols), lambda q, s: (q, 0, 0))],
            out_specs=[pl.BlockSpec((None, h, cols), lambda q, s: (q, 0, 0)),
                       pl.BlockSpec((None, h, cols), lambda q, s: (s[1], 0, 0))]),
        out_shape=[SDS((n4, h, cols), out_dtype)] * 2, name=name, compiler_params=_params())(_mesh_scalars(), full, recv)


def _sum_chips(name, parts):
    n4, h, cols = parts.shape
    th = h // 4 if h % 64 == 0 else h

    def body(s_ref, a_ref, o_ref):
        acc = a_ref[0].astype(F32)
        for q in range(1, n4):
            acc = acc + a_ref[q].astype(F32)
        o_ref[...] = acc

    return pl.pallas_call(
        body,
        grid_spec=pltpu.PrefetchScalarGridSpec(
            num_scalar_prefetch=1, grid=(h // th,),
            in_specs=[pl.BlockSpec((n4, th, cols), lambda i, s: (0, i, 0))],
            out_specs=pl.BlockSpec((None, th, cols), lambda i, s: (s[0], i, 0))),
        out_shape=SDS((2, h, cols), F32), name=name, compiler_params=_params())(_mesh_scalars(), parts)


def _adamw_math(w, g, m, v):
    m2 = ADAM_B1 * m + (1.0 - ADAM_B1) * g
    v2 = ADAM_B2 * v + (1.0 - ADAM_B2) * (g * g)
    m_hat = m2 / (1.0 - ADAM_B1 ** ADAM_STEP)
    v_hat = v2 / (1.0 - ADAM_B2 ** ADAM_STEP)
    delta = -ADAM_LR * (m_hat / (jnp.sqrt(v_hat) + ADAM_EPS) + ADAM_WD * w)
    return delta, m2, v2


def _row_tile(rows, cols):
    cap = max(8, (1 << 18) // cols)
    best = 8
    for cand in range(8, min(rows, cap) + 1, 8):
        if rows % cand == 0:
            best = cand
    return best


def _adamw_big(name, w, g_layers, m, v):
    layers, rows, cols = w.shape
    tr = _row_tile(rows, cols)

    def body(w_ref, m_ref, v_ref, *rest):
        g_refs, (g_o, d_o, m_o, v_o) = rest[:layers], rest[layers:]
        gv = g_refs[0][...]
        for layer in range(1, layers):
            gv = jnp.where(pl.program_id(0) == layer, g_refs[layer][...], gv)
        d, mm, vv = _adamw_math(w_ref[...], gv, m_ref[...], v_ref[...])
        g_o[...] = gv
        d_o[...] = d
        m_o[...] = mm
        v_o[...] = vv

    blk = pl.BlockSpec((None, tr, cols), lambda l, i: (l, i, 0))
    g_blk = pl.BlockSpec((tr, cols), lambda l, i: (i, 0))
    return tuple(pl.pallas_call(
        body, grid=(layers, rows // tr), in_specs=[blk] * 3 + [g_blk] * layers, out_specs=[blk] * 4,
        out_shape=[SDS((layers, rows, cols), F32)] * 4, name=name,
        compiler_params=_params())(w, m, v, *[g.reshape(rows, cols) for g in g_layers]))


def _adamw_small(ws, gs, ms, vs):
    n = len(ws)
    flat = []
    for group in (ws, gs, ms, vs):
        flat += [a.reshape(-1, a.shape[-1]) for a in group]

    def body(*refs):
        w_r, g_r, m_r, v_r = refs[:n], refs[n:2 * n], refs[2 * n:3 * n], refs[3 * n:4 * n]
        d_o, m_o, v_o = refs[4 * n:5 * n], refs[5 * n:6 * n], refs[6 * n:7 * n]
        for j in range(n):
            d, mm, vv = _adamw_math(w_r[j][...], g_r[j][...], m_r[j][...], v_r[j][...])
            d_o[j][...] = d
            m_o[j][...] = mm
            v_o[j][...] = vv

    shapes = [SDS(a.shape, F32) for a in flat[:n]]
    outs = pl.pallas_call(body, out_shape=shapes * 3, name="adamw_small", compiler_params=_params())(*flat)
    res = []
    for k in range(3):
        res.append([outs[k * n + j].reshape(ws[j].shape) for j in range(n)])
    return res


BIG = ("ab_w_in", "ab_w_out", "cd_w_in", "cd_w_out", "ffn_w_gate", "ffn_w_up", "ffn_w_down")
V_BLOCK = (2 * A_WIDTH + QK_COLS) // B_WIDTH


def _pad_rows(a, rows):
    return jnp.pad(a, ((0, rows - a.shape[0]), (0, 0)))


A_IN, A_OUT, C_IN, C_OUT = ("ab_w_in", 0), ("ab_w_out", 0), ("cd_w_in", 0), ("cd_w_out", 0)
G0, U0, D0 = ("ffn_w_gate", 0), ("ffn_w_up", 0), ("ffn_w_down", 0)
G1, U1, D1 = ("ffn_w_gate", 1), ("ffn_w_up", 1), ("ffn_w_down", 1)
UNITS = (A_IN, A_OUT, G0, U0, D0, C_IN, C_OUT, G1, U1, D1)
ROWS_MINOR = ("ffn_w_gate", "ffn_w_up")
SMALL_SHARDED = ("small", 0)
REPLICATED_UNIT = ("replicated", 0)


class _Exchange:
    def __init__(self, enabled):
        self.enabled = enabled
        self.w, self.grad, self.recv, self.half, self.land, self.done = {}, {}, {}, {}, {}, {}

    def full(self, unit):
        b = self.w[unit]
        return b.reshape(N_CHIPS, 1, 2 * b.shape[2], b.shape[3])

    def _ride(self, phases):
        rides, sinks = [], []
        for kind, units in phases:
            if kind == "send":
                rides.append(_ride_gather_send([self.w[u] for u in units]))
                sinks.append(self.w)
            elif kind == "pass":
                rides.append(_ride_gather_pass([self.w[u] for u in units]))
                sinks.append(self.w)
            elif kind == "swap":
                rides.append(_ride_swap([self.grad[u] for u in units]))
                sinks.append(self.recv)
            elif kind == "scatter":
                rides.append(_ride_scatter([self.half[u] for u in units], [self.land[u] for u in units]))
                sinks.append(self.land)
            else:
                rides.append(_ride_join([self.done[u] for u in units]))
                sinks.append(self.done)
        ride = functools.reduce(_ride_both, rides)

        def settle(res):
            n_bufs = sum(len(r.bufs) for r in rides)
            bufs, new = list(res[:n_bufs]), list(res[n_bufs:])
            for r, sink, (_, units) in zip(rides, sinks, phases):
                vals = [bufs.pop(0) for _ in r.bufs] + [new.pop(0) for _ in r.new_outs]
                for u, v in zip(units, vals):
                    sink[u] = v

        return ride, settle

    def run(self, fn, *args, phases=(), **kw):
        if not self.enabled or not phases:
            return fn(*args, **kw)
        ride, settle = self._ride(phases)
        out, res = fn(*args, ride=ride, **kw)
        settle(res)
        return out

    def alone(self, name, phases):
        if self.enabled:
            ride, settle = self._ride(phases)
            settle(_run_ride(name, ride))

    def pair_sum(self, units):
        if self.enabled:
            for u in units:
                dtype = F32 if u in (SMALL_SHARDED, REPLICATED_UNIT) else BF16
                self.half[u], self.land[u] = _add_own_half(f"pair_sum_{u[0]}_{u[1]}", self.grad[u], self.recv[u], dtype)

    def chip_sum(self, units):
        if self.enabled:
            for u in units:
                self.done[u] = _sum_chips(f"chip_sum_{u[0]}_{u[1]}", self.land[u])


def _local_step(x, target, ex, sp):
    t, d = x.shape
    tabs = _rope_tables(t)
    gains = jnp.concatenate([jnp.tile(sp["q_norm_g"][g], HEAD_DIM // 8) for g in range(N_DIL)]
                            + [jnp.tile(sp["k_norm_g"][g], HEAD_DIM // 8) for g in range(N_DIL)]).reshape(1, QK_COLS)
    bias_t = sp["sgu_bias"].T
    cw = _pad_rows(sp["conv_c_w"], 32)
    dw = _pad_rows(sp["conv_d_w"], 8)
    cb, clg, clb = (sp[k].reshape(1, C_WIDTH) for k in ("conv_c_b", "c_ln_g", "c_ln_b"))
    slg, slb = sp["sgu_norm_g"].reshape(1, A_WIDTH), sp["sgu_norm_b"].reshape(1, A_WIDTH)
    g_ab, g_cd = sp["ab_norm_g"].reshape(1, d), sp["cd_norm_g"].reshape(1, d)
    g_f0, g_f1 = sp["ffn_norm_g"][0:1], sp["ffn_norm_g"][1:2]
    run = ex.run

    def w2d(unit):
        return ex.full(unit).reshape(-1, d)

    h0 = _rms_fwd("rms_ab", x, g_ab)
    proj = run(_proj_in, "proj_ab", h0, ex.full(A_IN), 0, phases=[("send", [A_OUT, G0])])
    a_out = _mixer_a_fwd(proj, slg, slb, sp["sgu_w"], bias_t)
    qk, q1, q2, k1, k2 = run(_qk_fwd, proj, gains, tabs, phases=[("pass", [A_OUT, G0]), ("send", [U0])])
    regrouped_qk = {1: (q1, k1), 2: (q2, k2)}
    fwd_phases = ([("pass", [U0]), ("send", [D0])], [("pass", [D0]), ("send", [C_IN])],
                  [("pass", [C_IN]), ("send", [C_OUT, G1])])
    qkv, o_list, l_list = [], [], []
    for g, rate in enumerate(DIL_RATES):
        if rate == 1:
            qk3, proj3 = qk.reshape(1, t, QK_COLS), proj.reshape(1, t, AB_IN)
            q, k, v = (qk3, g), (qk3, N_DIL + g), (proj3, V_BLOCK + g)
        else:
            vp, = _permute(f"regroup_v_{g}", [(proj, V_BLOCK + g)], rate)
            q, k, v = (regrouped_qk[g][0], 0), (regrouped_qk[g][1], 0), (vp, 0)
        qkv.append((q, k, v))
        o, l = run(_attn_fwd, f"attn_fwd_{g}", q, k, v, phases=fwd_phases[g])
        if rate == 1:
            o, l = o.reshape(t, B_WIDTH), l.reshape(t, B_WIDTH)
        o_list.append(o)
        l_list.append(l)
    cat, lse_tot, lse_1, lse_2 = _attn_merge(a_out, o_list, l_list)
    x1, hf0 = _proj_out("out_ab", cat, w2d(A_OUT), x, g_next=g_f0)
    gate0, up0, act0 = run(_ffn_in, "ffn_in_0", hf0, ex.full(G0), ex.full(U0), 0,
                           phases=[("pass", [C_OUT, G1]), ("send", [U1])])
    x2, h1 = run(_ffn_out, "ffn_out_0", act0, ex.full(D0), 0, x1, g_next=g_cd, phases=[("pass", [U1]), ("send", [D1])])
    projcd = run(_proj_in, "proj_cd", h1, ex.full(C_IN), 0, phases=[("pass", [D1])])
    cat2, c1 = _mixer_cd_fwd(projcd, cw, cb, clg, clb, dw)
    x3, hf1 = _proj_out("out_cd", cat2, w2d(C_OUT), x2, g_next=g_f1)
    gate1, up1, act1 = _ffn_in("ffn_in_1", hf1, ex.full(G1), ex.full(U1), 0)
    dy, loss_acc, dy_b = _ffn_out("ffn_out_1", act1, ex.full(D1), 0, x3, target=target)
    loss = 0.5 * loss_acc[0, 0] / d

    late = [D1, G1, U1]
    dgate, dup = _ffn_dact("ffn_dact_1", dy_b, ex.full(D1), 0, gate1, up1)
    ex.grad[D1] = _wgrad_row_sharded("wgrad_down_1", act1, dy_b, True)
    ex.grad[G1], ex.grad[U1] = _wgrad_row_sharded("wgrad_gate_up_1", [dgate, dup], hf1, True)
    g3, d_f1, g3_b = run(_dgrad_cols, "dgrad_ffn_1", [dgate, dup], [ex.full(G1), ex.full(U1)], 0, True, x3, g_f1, dy,
                         w_rows=True, phases=[("swap", late)])
    ex.pair_sum(late)

    dcat2 = _dgrad_rows("dgrad_out_cd", g3_b, w2d(C_OUT))
    ex.grad[C_OUT] = _wgrad_row_sharded("wgrad_out_cd", cat2, g3_b, False)
    dprojcd, d_cw, d_cb, d_clg, d_clb, d_dw = run(_mixer_cd_bwd, projcd, dcat2, c1, cw, clg, clb, dw, phases=[("scatter", late)])
    ex.chip_sum(late)
    ex.grad[C_IN] = run(_wgrad_col_sharded, "wgrad_in_cd", h1, [dprojcd], False, phases=[("join", late)])[0]
    g2, d_cdn, g2_b = run(_dgrad_cols, "dgrad_in_cd", [dprojcd], [ex.full(C_IN)], 0, False, x2, g_cd, g3,
                          phases=[("swap", [C_OUT, C_IN])])
    ex.pair_sum([C_OUT, C_IN])

    dgate, dup = run(_ffn_dact, "ffn_dact_0", g2_b, ex.full(D0), 0, gate0, up0, phases=[("scatter", [C_OUT, C_IN])])
    ex.chip_sum([C_OUT, C_IN])
    ex.grad[D0] = run(_wgrad_row_sharded, "wgrad_down_0", act0, g2_b, True, phases=[("join", [C_OUT, C_IN])])
    ex.grad[G0], ex.grad[U0] = _wgrad_row_sharded("wgrad_gate_up_0", [dgate, dup], hf0, True)
    small = {"cd_norm_g": d_cdn, "conv_c_w": d_cw[:C_KERNEL], "conv_c_b": d_cb, "c_ln_g": d_clg, "c_ln_b": d_clb,
             "conv_d_w": d_dw[:D_KERNEL]}
    ex.grad[SMALL_SHARDED] = _split_full_small(small).reshape(N_CHIPS, 2, SHARDED_ROWS // 2, LANES)
    mid = [D0, G0, U0, SMALL_SHARDED]
    g1, d_f0, g1_b = run(_dgrad_cols, "dgrad_ffn_0", [dgate, dup], [ex.full(G0), ex.full(U0)], 0, True, x1, g_f0, g2,
                         w_rows=True, phases=[("swap", mid)])
    ex.pair_sum(mid)

    dcat = _dgrad_rows("dgrad_out_ab", g1_b, w2d(A_OUT))
    ex.grad[A_OUT] = _wgrad_row_sharded("wgrad_out_ab", cat, g1_b, False)
    d_a, d_sw, d_sbt, d_slg, d_slb = _mixer_a_bwd(proj, dcat, slg, slb, sp["sgu_w"], bias_t)
    early = {"sgu_norm_g": d_slg, "sgu_norm_b": d_slb, "sgu_w": d_sw, "sgu_bias": d_sbt.T}
    ex.grad[REPLICATED_UNIT] = jnp.broadcast_to(
        _pack_replicated(early, REPLICATED_EARLY, REPLICATED_EARLY_ROWS).reshape(2, REPLICATED_EARLY_ROWS // 2, LANES),
        (N_CHIPS, 2, REPLICATED_EARLY_ROWS // 2, LANES))
    last = [A_OUT, REPLICATED_UNIT]
    dbb, dd, db_1, dd_1, db_2, dd_2 = _attn_bwd_prep(dcat, cat)
    regrouped_bwd = {1: (db_1, lse_1, dd_1), 2: (db_2, lse_2, dd_2)}
    bwd_phases = ([("scatter", [D0, G0])], [("scatter", [U0, SMALL_SHARDED]), ("join", [D0, G0]), ("swap", last)],
                  [("join", [U0, SMALL_SHARDED]), ("scatter", last)])
    dqs, dks, dvs = [], [], []
    for g, rate in enumerate(DIL_RATES):
        q, k, v = qkv[g]
        if rate == 1:
            db3, l3, dd3 = (a.reshape(1, t, B_WIDTH) for a in (dbb, lse_tot, dd))
        else:
            db3, l3, dd3 = regrouped_bwd[g]
        if g == 1:
            ex.chip_sum([D0, G0])
        elif g == 2:
            ex.chip_sum([U0, SMALL_SHARDED])
            ex.pair_sum(last)
        dq, dk, dv = run(_attn_bwd, f"attn_bwd_{g}", q, k, v, db3, l3, dd3, phases=bwd_phases[g])
        if g == 2:
            ex.chip_sum(last)
        if rate == 1:
            dq, dk, dv = (a.reshape(t, B_WIDTH) for a in (dq, dk, dv))
        dqs.append(dq)
        dks.append(dk)
        dvs.append(dv)
    dproj, d_gains = _dproj_assemble(proj, d_a, dqs, dks, dvs, gains, tabs)
    d_gains = _fold_heads(d_gains)[0].reshape(2, N_DIL, B_WIDTH)[:, :, :HEAD_DIM]
    ex.grad[A_IN] = run(_wgrad_col_sharded, "wgrad_in_ab", h0, [dproj], False, phases=[("join", last)])[0]
    ex.alone("swap_last", [("swap", [A_IN])])
    ex.pair_sum([A_IN])
    gx, d_abn = run(_dgrad_cols, "dgrad_in_ab", [dproj], [ex.full(A_IN)], 0, False, x, g_ab, g1, bf16_copy=False,
                    phases=[("scatter", [A_IN])])
    ex.chip_sum([A_IN])
    ex.alone("join_last", [("join", [A_IN])])

    small.update({
        "ab_norm_g": d_abn, "sgu_norm_g": d_slg, "sgu_norm_b": d_slb, "sgu_w": d_sw, "sgu_bias": d_sbt.T,
        "q_norm_g": d_gains[0], "k_norm_g": d_gains[1], "ffn_norm_g": jnp.concatenate([d_f0, d_f1], axis=0),
    })
    return loss, gx, small


SHARDED_SMALL = ("cd_norm_g", "conv_c_w", "conv_c_b", "c_ln_g", "c_ln_b", "conv_d_w")
SHARDED_ROWS = 48
REPLICATED_EARLY = ("sgu_norm_g", "sgu_norm_b", "sgu_w", "sgu_bias")
REPLICATED_EARLY_ROWS = 528
REPLICATED_LATE = ("ab_norm_g", "q_norm_g", "k_norm_g", "ffn_norm_g", "loss")
REPLICATED_LATE_ROWS = 32
REPLICATED_SMALL = REPLICATED_EARLY + REPLICATED_LATE[:-1]


def _pack_sharded(parts):
    rows = [parts[k].reshape(-1, LANES) for k in SHARDED_SMALL]
    return _pad_rows(jnp.concatenate(rows, axis=0), SHARDED_ROWS)


def _split_full_small(small):
    per_chip = []
    for q in range(N_CHIPS):
        parts = {}
        for k in SHARDED_SMALL:
            a = small[k]
            a = a.reshape(-1, a.shape[-1])
            n = a.shape[-1] // N_CHIPS
            parts[k] = a[:, q * n:(q + 1) * n]
        per_chip.append(_pack_sharded(parts))
    return jnp.stack(per_chip)


def _unpack_sharded(pack, shapes):
    out, r = {}, 0
    for k in SHARDED_SMALL:
        n = math.prod(shapes[k]) // LANES
        out[k] = pack[r:r + n].reshape(shapes[k])
        r += n
    return out


def _gathered_small(packs, shapes):
    per_chip = [_unpack_sharded(packs[q], shapes) for q in range(N_CHIPS)]
    return {k: jnp.concatenate([pc[k] for pc in per_chip], axis=-1) for k in SHARDED_SMALL}


def _pack_replicated(small, names, total_rows):
    rows = []
    for k in names:
        a = small[k].reshape(-1)
        a = jnp.pad(a, (0, (-a.shape[0]) % LANES))
        rows.append(a.reshape(-1, LANES))
    return _pad_rows(jnp.concatenate(rows, axis=0), total_rows)


def _unpack_replicated(pack, shapes, names):
    out, r = {}, 0
    for k in names:
        size = math.prod(shapes[k])
        n = -(-size // LANES)
        out[k] = pack[r:r + n].reshape(-1)[:size].reshape(shapes[k])
        r += n
    return out


WEIGHT_ORDER = ("ab_norm_g", "ab_w_in", "sgu_norm_g", "sgu_norm_b", "sgu_w", "sgu_bias", "q_norm_g", "k_norm_g", "ab_w_out",
                "cd_norm_g", "cd_w_in", "conv_c_w", "conv_c_b", "c_ln_g", "c_ln_b", "conv_d_w", "cd_w_out", "ffn_norm_g",
                "ffn_w_gate", "ffn_w_up", "ffn_w_down")


def kernel(x, ab_norm_g, ab_w_in, sgu_norm_g, sgu_norm_b, sgu_w, sgu_bias, q_norm_g, k_norm_g, ab_w_out, cd_norm_g, cd_w_in, conv_c_w, conv_c_b, c_ln_g, c_ln_b, conv_d_w, cd_w_out, ffn_norm_g, ffn_w_gate, ffn_w_up, ffn_w_down, loss_target, m_ab_norm_g, m_ab_w_in, m_sgu_norm_g, m_sgu_norm_b, m_sgu_w, m_sgu_bias, m_q_norm_g, m_k_norm_g, m_ab_w_out, m_cd_norm_g, m_cd_w_in, m_conv_c_w, m_conv_c_b, m_c_ln_g, m_c_ln_b, m_conv_d_w, m_cd_w_out, m_ffn_norm_g, m_ffn_w_gate, m_ffn_w_up, m_ffn_w_down, v_ab_norm_g, v_ab_w_in, v_sgu_norm_g, v_sgu_norm_b, v_sgu_w, v_sgu_bias, v_q_norm_g, v_k_norm_g, v_ab_w_out, v_cd_norm_g, v_cd_w_in, v_conv_c_w, v_conv_c_b, v_c_ln_g, v_c_ln_b, v_conv_d_w, v_cd_w_out, v_ffn_norm_g, v_ffn_w_gate, v_ffn_w_up, v_ffn_w_down):
    args = dict(locals())
    ws = {k: args[k] for k in WEIGHT_ORDER}
    ms = {k: args["m_" + k] for k in WEIGHT_ORDER}
    vs = {k: args["v_" + k] for k in WEIGHT_ORDER}
    small_names = [k for k in WEIGHT_ORDER if k not in BIG]
    t, d = x.shape[1:]

    for group in (ws, ms, vs):
        for k in ROWS_MINOR:
            group[k] = jnp.swapaxes(group[k], 1, 2)
    ex = _Exchange(enabled=True)
    for name, layer in UNITS:
        ex.w[(name, layer)] = _stage_own(f"stage_{name}_{layer}", ws[name], layer, BF16)
    own_small = _pack_sharded({k: ws[k][0] for k in SHARDED_SMALL})
    ex.w[SMALL_SHARDED] = _stage_own("stage_small", own_small[None], 0, F32)
    ex.alone("gather_first", [("send", [A_IN, SMALL_SHARDED])])
    ex.alone("gather_first_pass", [("pass", [A_IN, SMALL_SHARDED])])
    sp = _gathered_small(ex.w[SMALL_SHARDED].reshape(N_CHIPS, SHARDED_ROWS, LANES), {k: ws[k].shape[1:] for k in SHARDED_SMALL})
    for k in REPLICATED_SMALL:
        sp[k] = ws[k] if k == "ffn_norm_g" else ws[k][0]

    loss, grad_x, g_small = _local_step(x.reshape(t, d), loss_target.reshape(t, d), ex, sp)

    shapes = {k: ws[k].shape for k in REPLICATED_SMALL}
    shapes["loss"] = (1,)
    g_small["loss"] = loss
    late = _all_reduce_small(_pack_replicated(g_small, REPLICATED_LATE, REPLICATED_LATE_ROWS))
    grad = _unpack_sharded(ex.done[SMALL_SHARDED].reshape(SHARDED_ROWS, LANES), {k: ws[k].shape for k in SHARDED_SMALL})
    grad.update(_unpack_replicated(ex.done[REPLICATED_UNIT].reshape(REPLICATED_EARLY_ROWS, LANES), shapes, REPLICATED_EARLY))
    grad.update(_unpack_replicated(late, shapes, REPLICATED_LATE))
    loss = grad.pop("loss")[0]

    delta, new_m, new_v = {}, {}, {}
    for k in BIG:
        g_layers = [ex.done[(k, layer)] for layer in range(ws[k].shape[0])]
        outs = _adamw_big("adamw_" + k, ws[k], g_layers, ms[k], vs[k])
        if k in ROWS_MINOR:
            outs = [jnp.swapaxes(o, 1, 2) for o in outs]
        grad[k], delta[k], new_m[k], new_v[k] = outs
    d_s, m_s, v_s = _adamw_small([ws[k] for k in small_names], [grad[k] for k in small_names],
                                 [ms[k] for k in small_names], [vs[k] for k in small_names])
    for j, k in enumerate(small_names):
        delta[k], new_m[k], new_v[k] = d_s[j], m_s[j], v_s[j]

    return (loss, grad_x[None], *[grad[k] for k in WEIGHT_ORDER], *[delta[k] for k in WEIGHT_ORDER],
            *[new_m[k] for k in WEIGHT_ORDER], *[new_v[k] for k in WEIGHT_ORDER])
```

```python
import functools
import math

import jax
import jax.numpy as jnp
from jax import lax
from jax.experimental import pallas as pl
from jax.experimental.pallas import tpu as pltpu

F32 = jnp.float32
BF16 = jnp.bfloat16
SDS = jax.ShapeDtypeStruct

N_CHIPS = 4
EPS = 1e-6
NEG_INF = -1e30
CHUNK = 128
A_GROUPS = 4
A_WIDTH = 512
N_DIL = 3
DIL_RATES = (1, 4, 16)
HEAD_DIM = 64
B_WIDTH = 512
ROPE_DIM = 16
ROPE_THETA = 500000.0
C_WIDTH = 512
C_KERNEL = 31
D_KERNEL = 3
HALO = 32
ATT_BLOCK = 128
LANES = 128

ADAM_LR = 0.001
ADAM_B1 = 0.9
ADAM_B2 = 0.999
ADAM_EPS = 1e-08
ADAM_WD = 0.01
ADAM_STEP = 10

VMEM_LIMIT = 56 * 1024 * 1024

NN = (((1,), (0,)), ((), ()))
NT = (((1,), (1,)), ((), ()))
TN = (((0,), (0,)), ((), ()))

TILES = {"proj_in": 1024, "proj_out": 1024, "ffn_in": 1024, "ffn_out": 512, "ffn_dact": 512, "dgrad_cols": 512,
         "dgrad_rows": 1024, "wgrad": 2048}


def _params(sem=None):
    return pltpu.CompilerParams(dimension_semantics=sem, vmem_limit_bytes=VMEM_LIMIT)


def _bf(v):
    return v if v.dtype == BF16 else v.astype(BF16)


def _dot(a, b, dims):
    return lax.dot_general(_bf(a), _bf(b), dims, preferred_element_type=F32)


def _dot_hi(a, b):
    return jnp.dot(a, b, precision=lax.Precision.HIGHEST, preferred_element_type=F32)


def _sigmoid(v):
    return 0.5 * jnp.tanh(0.5 * v) + 0.5


def _gelu(v):
    return 0.5 * v * (1.0 + lax.erf(v * (1.0 / math.sqrt(2.0))))


def _gelu_grad(v):
    cdf = 0.5 * (1.0 + lax.erf(v * (1.0 / math.sqrt(2.0))))
    return cdf + v * jnp.exp(-0.5 * v * v) * (1.0 / math.sqrt(2.0 * math.pi))


def _segment_mean_matrix(seg, scale=None):
    r = lax.broadcasted_iota(jnp.int32, (LANES, LANES), 0) // seg
    c = lax.broadcasted_iota(jnp.int32, (LANES, LANES), 1) // seg
    return jnp.where(r == c, (1.0 / seg) if scale is None else scale, 0.0).astype(BF16)


def _segment_dot(v, seg):
    hi = v.astype(BF16)
    lo = (v - hi.astype(F32)).astype(BF16)
    return jnp.dot(hi, seg, preferred_element_type=F32) + jnp.dot(lo, seg, preferred_element_type=F32)


MESH = pl.DeviceIdType.MESH
ANY = pl.BlockSpec(memory_space=pl.ANY)


def _position():
    x, y, c = lax.axis_index("x"), lax.axis_index("y"), lax.axis_index("c")
    others = [(1 - x, y), (x, 1 - y), (1 - x, 1 - y)]
    return x, y, c, 2 * x + y, others


class _Ride:
    def __init__(self, ins, bufs, new_outs, sem_shapes, start, finish):
        self.ins, self.bufs, self.new_outs, self.sem_shapes = list(ins), list(bufs), list(new_outs), list(sem_shapes)
        self.start, self.finish = start, finish


def _ride_both(a, b):
    na = (len(a.ins), len(a.bufs), len(a.new_outs), len(a.sem_shapes))

    def split(ins, bufs, new, sems):
        return ((ins[:na[0]], bufs[:na[1]], new[:na[2]], sems[:na[3]]), (ins[na[0]:], bufs[na[1]:], new[na[2]:], sems[na[3]:]))

    def start(*refs):
        ra, rb = split(*refs)
        a.start(*ra)
        b.start(*rb)

    def finish(*refs):
        ra, rb = split(*refs)
        a.finish(*ra)
        b.finish(*rb)

    return _Ride(a.ins + b.ins, a.bufs + b.bufs, a.new_outs + b.new_outs, a.sem_shapes + b.sem_shapes, start, finish)


def _call(body, *, grid, in_specs, out_specs, out_shape, operands, name, scratch_shapes=(), aliases=None, ride=None):
    if ride is None:
        return pl.pallas_call(body, grid=grid, in_specs=in_specs, out_specs=out_specs, out_shape=out_shape,
                              scratch_shapes=list(scratch_shapes), input_output_aliases=aliases or {}, name=name,
                              compiler_params=_params())(*operands)
    multi = isinstance(out_shape, (list, tuple))
    out_shapes = list(out_shape) if multi else [out_shape]
    o_specs = list(out_specs) if multi else [out_specs]
    n_in, n_out, n_scr = len(operands), len(out_shapes), len(scratch_shapes)
    n_ri, n_rb, n_rn = len(ride.ins), len(ride.bufs), len(ride.new_outs)

    def carrying(*refs):
        k = n_in
        r_ins = refs[k:k + n_ri]
        k += n_ri + n_rb
        outs = refs[k:k + n_out]
        k += n_out
        r_bufs = refs[k:k + n_rb]
        k += n_rb
        r_new = refs[k:k + n_rn]
        k += n_rn
        scratch = refs[k:k + n_scr]
        sems = refs[k + n_scr:]
        first, last = None, None
        for axis, size in enumerate(grid):
            pid = pl.program_id(axis)
            first = (pid == 0) if first is None else first & (pid == 0)
            last = (pid == size - 1) if last is None else last & (pid == size - 1)

        @pl.when(first)
        def _():
            ride.start(r_ins, r_bufs, r_new, sems)

        body(*refs[:n_in], *outs, *scratch)

        @pl.when(last)
        def _():
            ride.finish(r_ins, r_bufs, r_new, sems)

    all_aliases = dict(aliases or {})
    for j in range(n_rb):
        all_aliases[n_in + n_ri + j] = n_out + j
    res = pl.pallas_call(
        carrying, grid=grid, in_specs=list(in_specs) + [ANY] * (n_ri + n_rb), out_specs=o_specs + [ANY] * (n_rb + n_rn),
        out_shape=out_shapes + [SDS(b.shape, b.dtype) for b in ride.bufs] + ride.new_outs,
        scratch_shapes=list(scratch_shapes) + [pltpu.SemaphoreType.DMA(s) for s in ride.sem_shapes],
        input_output_aliases=all_aliases, name=name, compiler_params=_params())(*operands, *ride.ins, *ride.bufs)
    outs = res[:n_out]
    return (list(outs) if multi else outs[0]), list(res[n_out:])


def _run_ride(name, ride):
    n_ri, n_rb, n_rn = len(ride.ins), len(ride.bufs), len(ride.new_outs)

    def body(*refs):
        r_ins = refs[:n_ri]
        r_bufs = refs[n_ri + n_rb:n_ri + 2 * n_rb]
        r_new = refs[n_ri + 2 * n_rb:n_ri + 2 * n_rb + n_rn]
        sems = refs[n_ri + 2 * n_rb + n_rn:]
        ride.start(r_ins, r_bufs, r_new, sems)
        ride.finish(r_ins, r_bufs, r_new, sems)

    return list(pl.pallas_call(
        body, in_specs=[ANY] * (n_ri + n_rb), out_specs=[ANY] * (n_rb + n_rn),
        out_shape=[SDS(b.shape, b.dtype) for b in ride.bufs] + ride.new_outs,
        scratch_shapes=[pltpu.SemaphoreType.DMA(s) for s in ride.sem_shapes],
        input_output_aliases={n_ri + j: j for j in range(n_rb)}, name=name)(*ride.ins, *ride.bufs))


def _whole(ref, p):
    return ref[...]


def _slab(ref, p):
    return ref[p]


def _matmul(name, grid, pairs, extras, outs, dims, epi, *, slabs=1, n_acc=1, ride=None):
    n_pairs, n_ex, n_out = len(pairs), len(extras), len(outs)

    def body(*refs):
        ab = refs[:2 * n_pairs]
        ex = refs[2 * n_pairs:2 * n_pairs + n_ex]
        out_refs = refs[2 * n_pairs + n_ex:2 * n_pairs + n_ex + n_out]
        pids = tuple(pl.program_id(a) for a in range(len(grid)))
        parts = [None] * n_acc
        for p in range(slabs):
            for j, (_, _, a_pick, _, _, b_pick, acc) in enumerate(pairs):
                d = _dot(a_pick(ab[2 * j], p), b_pick(ab[2 * j + 1], p), dims)
                parts[acc] = d if parts[acc] is None else parts[acc] + d
        epi(parts, ex, out_refs, pids)

    operands, in_specs = [], []
    for a, a_spec, _, b, b_spec, _, _ in pairs:
        operands += [a, b]
        in_specs += [a_spec, b_spec]
    for e, e_spec in extras:
        operands.append(e)
        in_specs.append(e_spec)
    return _call(body, grid=grid, in_specs=in_specs, out_specs=[o[1] for o in outs], out_shape=[o[0] for o in outs],
                 operands=operands, name=name, ride=ride)


def _rms_rows(v, g):
    r = lax.rsqrt(jnp.mean(v * v, axis=-1, keepdims=True) + EPS)
    return v * r * g


def _rms_fwd(name, x, g):
    t, d = x.shape
    tm = 512

    def body(x_ref, g_ref, o_ref):
        o_ref[...] = _rms_rows(x_ref[...], g_ref[...]).astype(BF16)

    return pl.pallas_call(
        body, grid=(t // tm,),
        in_specs=[pl.BlockSpec((tm, d), lambda i: (i, 0)), pl.BlockSpec((1, d), lambda i: (0, 0))],
        out_specs=pl.BlockSpec((tm, d), lambda i: (i, 0)), out_shape=SDS((t, d), BF16), name=name,
        compiler_params=_params())(x, g)


def _epi_residual_norm(accs, ex, outs, pids):
    x_new = accs[0] + ex[0][...]
    outs[0][...] = x_new
    outs[1][...] = _rms_rows(x_new, ex[1][...]).astype(BF16)


def _epi_residual_loss(accs, ex, outs, pids):
    y = accs[0] + ex[0][...]
    err = y - ex[1][...]
    dy = err * (1.0 / err.shape[-1])
    outs[0][...] = dy
    outs[2][...] = dy.astype(BF16)

    @pl.when(pids[0] == 0)
    def _():
        outs[1][...] = jnp.zeros_like(outs[1])

    outs[1][...] += jnp.sum(err * err)


def _epi_rms_bwd(accs, ex, outs, pids):
    dh = accs[0]
    xv, g, res = ex[0][...], ex[1][...], ex[2][...]
    r = lax.rsqrt(jnp.mean(xv * xv, axis=-1, keepdims=True) + EPS)
    xh = xv * r
    dy = dh * g
    dx = res + r * (dy - xh * jnp.mean(dy * xh, axis=-1, keepdims=True))
    outs[0][...] = dx
    if len(outs) > 2:
        outs[2][...] = dx.astype(BF16)

    @pl.when(pids[0] == 0)
    def _():
        outs[1][...] = jnp.zeros_like(outs[1])

    outs[1][...] += jnp.sum(dh * xh, axis=0, keepdims=True)


def _row_spec(tm, d):
    return pl.BlockSpec((tm, d), lambda i, *_: (i, 0))


def _const_spec(shape):
    nd = len(shape)
    return pl.BlockSpec(shape, lambda *_: (0,) * nd)


def _proj_in(name, h, w, layer, ride=None):
    t, d = h.shape
    n4 = w.shape[-1]
    tm = TILES["proj_in"]

    def epi(accs, ex, outs, pids):
        outs[0][...] = accs[0].astype(BF16)

    res = _matmul(
        name, (N_CHIPS, t // tm),
        [(h, pl.BlockSpec((tm, d), lambda p, i: (i, 0)), _whole,
          w, pl.BlockSpec((None, None, d, n4), lambda p, i: (p, layer, 0, 0)), _whole, 0)],
        [], [(SDS((t, N_CHIPS * n4), BF16), pl.BlockSpec((tm, n4), lambda p, i: (i, p)))],
        NN, epi, ride=ride)
    return res[0] if ride is None else (res[0][0], res[1])


def _proj_out(name, a, w, x, g_next=None, target=None):
    t, k = a.shape
    d = w.shape[-1]
    tm = TILES["proj_out"]
    if target is None:
        extras = [(x, _row_spec(tm, d)), (g_next, _const_spec((1, d)))]
        outs = [(SDS((t, d), F32), _row_spec(tm, d)), (SDS((t, d), BF16), _row_spec(tm, d))]
        epi = _epi_residual_norm
    else:
        extras = [(x, _row_spec(tm, d)), (target, _row_spec(tm, d))]
        outs = [(SDS((t, d), F32), _row_spec(tm, d)), (SDS((8, LANES), F32), _const_spec((8, LANES))),
                (SDS((t, d), BF16), _row_spec(tm, d))]
        epi = _epi_residual_loss
    return _matmul(name, (t // tm,), [(a, _row_spec(tm, k), _whole, w, _const_spec((k, d)), _whole, 0)], extras, outs, NN, epi)


def _ffn_in(name, h, wg, wu, layer, ride=None):
    t, d = h.shape
    n4 = wg.shape[-2]
    tm = TILES["ffn_in"]

    def epi(accs, ex, outs, pids):
        gate, up = accs
        s = _sigmoid(gate)
        silu = gate * s
        outs[0][...] = (up * (s + silu - silu * s)).astype(BF16)
        outs[1][...] = silu.astype(BF16)
        outs[2][...] = (silu * up).astype(BF16)

    w_spec = pl.BlockSpec((None, None, n4, d), lambda p, i: (p, layer, 0, 0))
    h_spec = pl.BlockSpec((tm, d), lambda p, i: (i, 0))
    o = (SDS((N_CHIPS, t, n4), BF16), pl.BlockSpec((None, tm, n4), lambda p, i: (p, i, 0)))
    return _matmul(name, (N_CHIPS, t // tm),
                   [(h, h_spec, _whole, wg, w_spec, _whole, 0), (h, h_spec, _whole, wu, w_spec, _whole, 1)], [],
                   [o, o, o], NT, epi, n_acc=2, ride=ride)


def _ffn_out(name, act, wd, layer, x, g_next=None, target=None, ride=None):
    _, t, n4 = act.shape
    d = wd.shape[-1]
    tm = TILES["ffn_out"]
    xs = _row_spec(tm, d)
    if target is None:
        extras = [(x, xs), (g_next, _const_spec((1, d)))]
        outs = [(SDS((t, d), F32), xs), (SDS((t, d), BF16), xs)]
        epi = _epi_residual_norm
    else:
        extras = [(x, xs), (target, xs)]
        outs = [(SDS((t, d), F32), xs), (SDS((8, LANES), F32), _const_spec((8, LANES))), (SDS((t, d), BF16), xs)]
        epi = _epi_residual_loss
    return _matmul(
        name, (t // tm,),
        [(act, pl.BlockSpec((N_CHIPS, tm, n4), lambda i: (0, i, 0)), _slab,
          wd, pl.BlockSpec((N_CHIPS, None, n4, d), lambda i: (0, layer, 0, 0)), _slab, 0)],
        extras, outs, NN, epi, slabs=N_CHIPS, ride=ride)


def _ffn_dact(name, g, wd, layer, gate, up, ride=None):
    t, d = g.shape
    n4 = wd.shape[-2]
    tm = TILES["ffn_dact"]

    def epi(accs, ex, outs, pids):
        dact = accs[0]
        outs[0][...] = (dact * ex[0][...].astype(F32)).astype(BF16)
        outs[1][...] = (dact * ex[1][...].astype(F32)).astype(BF16)

    blk = pl.BlockSpec((None, tm, n4), lambda p, i: (p, i, 0))
    o = (SDS((N_CHIPS, t, n4), BF16), blk)
    return _matmul(
        name, (N_CHIPS, t // tm),
        [(g, pl.BlockSpec((tm, d), lambda p, i: (i, 0)), _whole,
          wd, pl.BlockSpec((None, None, n4, d), lambda p, i: (p, layer, 0, 0)), _whole, 0)],
        [(gate, blk), (up, blk)], [o, o], NT, epi, ride=ride)


def _copy_epi(accs, ex, outs, pids):
    for a, o in zip(accs, outs):
        o[...] = a.astype(o.dtype)


def _dgrad_cols(name, dz_list, w_list, layer, three_d, x, g, res, bf16_copy=True, w_rows=False, ride=None):
    t, d = x.shape
    n4 = w_list[0].shape[-2 if w_rows else -1]
    tm = TILES["dgrad_cols"]
    if three_d:
        zs, z_pick = pl.BlockSpec((N_CHIPS, tm, n4), lambda i: (0, i, 0)), _slab
    else:
        zs, z_pick = _row_spec(tm, N_CHIPS * n4), (lambda ref, p: ref[:, p * n4:(p + 1) * n4])
    ws = pl.BlockSpec((N_CHIPS, None) + ((n4, d) if w_rows else (d, n4)), lambda i: (0, layer, 0, 0))
    xs = _row_spec(tm, d)
    return _matmul(
        name, (t // tm,), [(dz, zs, z_pick, w, ws, _slab, 0) for dz, w in zip(dz_list, w_list)],
        [(x, xs), (g, _const_spec((1, d))), (res, xs)],
        [(SDS((t, d), F32), xs), (SDS((1, d), F32), _const_spec((1, d)))] + ([(SDS((t, d), BF16), xs)] if bf16_copy else []),
        NN if w_rows else NT, _epi_rms_bwd, slabs=N_CHIPS, ride=ride)


def _dgrad_rows(name, g, w):
    t, d = g.shape
    k = w.shape[0]
    tm = TILES["dgrad_rows"]
    return _matmul(name, (t // tm,), [(g, _row_spec(tm, d), _whole, w, _const_spec((k, d)), _whole, 0)], [],
                   [(SDS((t, k), F32), _row_spec(tm, k))], NT, _copy_epi)[0]


A_TILE = 256


def _a_common(p_ref, lg_ref, lb_ref):
    pv = p_ref[...].astype(F32)
    a = _gelu(pv)
    u, v = a[:, :A_WIDTH], a[:, A_WIDTH:]
    vc = v - jnp.mean(v, axis=-1, keepdims=True)
    rs = lax.rsqrt(jnp.mean(vc * vc, axis=-1, keepdims=True) + EPS)
    vhat = vc * rs
    vn = vhat * lg_ref[...] + lb_ref[...]
    return pv, u, vhat, rs, vn.astype(BF16)


def _tril_weights(w_ref, g):
    r = lax.broadcasted_iota(jnp.int32, (CHUNK, CHUNK), 0)
    c = lax.broadcasted_iota(jnp.int32, (CHUNK, CHUNK), 1)
    return jnp.where(c <= r, w_ref[g], 0.0).astype(BF16), c <= r


def _mixer_a_fwd(proj, lg, lb, w, bias_t):
    t = proj.shape[0]

    def body(p_ref, lg_ref, lb_ref, w_ref, bt_ref, o_ref):
        _, u, _, _, vnb = _a_common(p_ref, lg_ref, lb_ref)
        for g in range(A_GROUPS):
            wt, _ = _tril_weights(w_ref, g)
            cs = slice(g * CHUNK, (g + 1) * CHUNK)
            for ch in range(A_TILE // CHUNK):
                rs_ = slice(ch * CHUNK, (ch + 1) * CHUNK)
                mixed = _dot(wt, vnb[rs_, cs], NN) + bt_ref[:, g:g + 1]
                o_ref[rs_, cs] = (u[rs_, cs] * mixed).astype(BF16)

    return pl.pallas_call(
        body, grid=(t // A_TILE,),
        in_specs=[pl.BlockSpec((A_TILE, 2 * A_WIDTH), lambda i: (i, 0)), _const_spec((1, A_WIDTH)),
                  _const_spec((1, A_WIDTH)), _const_spec((A_GROUPS, CHUNK, CHUNK)), _const_spec((CHUNK, A_GROUPS))],
        out_specs=pl.BlockSpec((A_TILE, A_WIDTH), lambda i: (i, 0)), out_shape=SDS((t, A_WIDTH), BF16),
        name="mixer_a_fwd", compiler_params=_params())(proj, lg, lb, w, bias_t)


def _mixer_a_bwd(proj, dcat, lg, lb, w, bias_t):
    t = proj.shape[0]

    def body(p_ref, da_ref, lg_ref, lb_ref, w_ref, bt_ref, dp_ref, dw_ref, dbt_ref, dlg_ref, dlb_ref, du_scr, dvn_scr):
        @pl.when(pl.program_id(0) == 0)
        def _():
            dw_ref[...] = jnp.zeros_like(dw_ref)
            dbt_ref[...] = jnp.zeros_like(dbt_ref)
            dlg_ref[...] = jnp.zeros_like(dlg_ref)
            dlb_ref[...] = jnp.zeros_like(dlb_ref)

        pv, u, vhat, rs, vnb = _a_common(p_ref, lg_ref, lb_ref)
        da = da_ref[...]
        for g in range(A_GROUPS):
            wt, keep = _tril_weights(w_ref, g)
            cs = slice(g * CHUNK, (g + 1) * CHUNK)
            for ch in range(A_TILE // CHUNK):
                rs_ = slice(ch * CHUNK, (ch + 1) * CHUNK)
                vg = vnb[rs_, cs]
                mixed = _dot(wt, vg, NN) + bt_ref[:, g:g + 1]
                du_scr[rs_, cs] = da[rs_, cs] * mixed
                dmx = da[rs_, cs] * u[rs_, cs]
                dw_ref[g] += jnp.where(keep, _dot(dmx, vg, NT), 0.0)
                dvn_scr[rs_, cs] = _dot(wt, dmx, TN)
                dbt_ref[:, g:g + 1] += jnp.sum(dmx, axis=1, keepdims=True)
        dvn = dvn_scr[...]
        dlg_ref[...] += jnp.sum(dvn * vhat, axis=0, keepdims=True)
        dlb_ref[...] += jnp.sum(dvn, axis=0, keepdims=True)
        dvh = dvn * lg_ref[...]
        dv = rs * (dvh - jnp.mean(dvh, axis=-1, keepdims=True) - vhat * jnp.mean(dvh * vhat, axis=-1, keepdims=True))
        gp = _gelu_grad(pv)
        dp_ref[:, :A_WIDTH] = (du_scr[...] * gp[:, :A_WIDTH]).astype(BF16)
        dp_ref[:, A_WIDTH:] = (dv * gp[:, A_WIDTH:]).astype(BF16)

    return pl.pallas_call(
        body, grid=(t // A_TILE,),
        in_specs=[pl.BlockSpec((A_TILE, 2 * A_WIDTH), lambda i: (i, 0)), pl.BlockSpec((A_TILE, A_WIDTH), lambda i: (i, 0)),
                  _const_spec((1, A_WIDTH)), _const_spec((1, A_WIDTH)), _const_spec((A_GROUPS, CHUNK, CHUNK)),
                  _const_spec((CHUNK, A_GROUPS))],
        out_specs=[pl.BlockSpec((A_TILE, 2 * A_WIDTH), lambda i: (i, 0)), _const_spec((A_GROUPS, CHUNK, CHUNK)),
                   _const_spec((CHUNK, A_GROUPS)), _const_spec((1, A_WIDTH)), _const_spec((1, A_WIDTH))],
        out_shape=[SDS((t, 2 * A_WIDTH), BF16), SDS((A_GROUPS, CHUNK, CHUNK), F32), SDS((CHUNK, A_GROUPS), F32),
                   SDS((1, A_WIDTH), F32), SDS((1, A_WIDTH), F32)],
        scratch_shapes=[pltpu.VMEM((A_TILE, A_WIDTH), F32), pltpu.VMEM((A_TILE, A_WIDTH), F32)],
        name="mixer_a_bwd", compiler_params=_params())(proj, dcat, lg, lb, w, bias_t)


def _rope_tables(t):
    half = ROPE_DIM // 2
    inv_freq = ROPE_THETA ** (-jnp.arange(half, dtype=F32) * 2.0 / ROPE_DIM)
    ang = jnp.arange(t, dtype=F32)[:, None] * inv_freq[None, :]
    cos, sin = jnp.cos(ang), jnp.sin(ang)
    one = jnp.ones((t, HEAD_DIM - ROPE_DIM), F32)
    zero = jnp.zeros((t, HEAD_DIM - ROPE_DIM), F32)
    zh = jnp.zeros((t, half), F32)
    c = jnp.concatenate([cos, cos, one], axis=1)
    s1 = jnp.concatenate([-sin, zh, zero], axis=1)
    s2 = jnp.concatenate([zh, sin, zero], axis=1)
    return tuple(jnp.tile(a, (1, LANES // HEAD_DIM)) for a in (c, s1, s2))


QK_TILE = 512
QK_COLS = 2 * N_DIL * B_WIDTH


CHUNKS = B_WIDTH // LANES


def _regroup_out(scr, first, out_ref, rate, tile):
    rows = tile // rate
    for rho in range(rate):
        for c in range(CHUNKS):
            out_ref[rho, :, c * LANES:(c + 1) * LANES] = scr[first + c, pl.ds(rho, rows, stride=rate), :].astype(out_ref.dtype)


def _regroup_in(x_ref, scr, rate, tile):
    rows = tile // rate
    for rho in range(rate):
        for c in range(CHUNKS):
            scr[c, pl.ds(rho, rows, stride=rate), :] = x_ref[rho, :, c * LANES:(c + 1) * LANES].astype(F32)


def _regrouped_spec(rate, tile):
    return pl.BlockSpec((rate, tile // rate, B_WIDTH), lambda i, *_: (0, i, 0))


def _qk_fwd(proj, gains, tabs, ride=None):
    t = proj.shape[0]
    col0 = 2 * A_WIDTH // 1024
    r1, r2 = DIL_RATES[1], DIL_RATES[2]

    def body(p_ref, g_ref, c_ref, s1_ref, s2_ref, o_ref, q1_ref, q2_ref, k1_ref, k2_ref, scr):
        seg = _segment_mean_matrix(HEAD_DIM)
        c, s1, s2 = c_ref[...], s1_ref[...], s2_ref[...]
        for ci in range(1024 // LANES):
            ls = slice(ci * LANES, (ci + 1) * LANES)
            xv = p_ref[:, ls].astype(F32)
            r = lax.rsqrt(_segment_dot(xv * xv, seg) + EPS)
            y = xv * r * g_ref[:, ls]
            val = y * c + pltpu.roll(y, LANES - 8, axis=1) * s1 + pltpu.roll(y, 8, axis=1) * s2
            o_ref[:, ls] = val.astype(BF16)
            scr[ci] = val

        j = pl.program_id(1)

        @pl.when(j == 0)
        def _():
            _regroup_out(scr, CHUNKS, q1_ref, r1, QK_TILE)

        @pl.when(j == 1)
        def _():
            _regroup_out(scr, 0, q2_ref, r2, QK_TILE)

        @pl.when(j == 2)
        def _():
            _regroup_out(scr, 0, k1_ref, r1, QK_TILE)
            _regroup_out(scr, CHUNKS, k2_ref, r2, QK_TILE)

    tab = pl.BlockSpec((QK_TILE, LANES), lambda i, j: (i, 0))
    g1, g2 = SDS((r1, t // r1, B_WIDTH), BF16), SDS((r2, t // r2, B_WIDTH), BF16)
    s1_, s2_ = _regrouped_spec(r1, QK_TILE), _regrouped_spec(r2, QK_TILE)
    return _call(
        body, grid=(t // QK_TILE, QK_COLS // 1024),
        in_specs=[pl.BlockSpec((QK_TILE, 1024), lambda i, j: (i, col0 + j)), pl.BlockSpec((1, 1024), lambda i, j: (0, j)),
                  tab, tab, tab],
        out_specs=[pl.BlockSpec((QK_TILE, 1024), lambda i, j: (i, j)), s1_, s2_, s1_, s2_],
        out_shape=[SDS((t, QK_COLS), BF16), g1, g2, g1, g2],
        scratch_shapes=[pltpu.VMEM((2 * CHUNKS, QK_TILE, LANES), F32)],
        operands=[proj, gains, *tabs], name="qk_norm_rope_fwd", ride=ride)


PERM_TILE = 512


def _permute(name, items, rate):
    t = items[0][0].shape[0]
    n = len(items)

    def body(*refs):
        scr = refs[-1]
        for x_ref, o_ref in zip(refs[:n], refs[n:2 * n]):
            for ci in range(CHUNKS):
                scr[ci] = x_ref[:, ci * LANES:(ci + 1) * LANES].astype(F32)
            _regroup_out(scr, 0, o_ref, rate, PERM_TILE)

    return pl.pallas_call(
        body, grid=(t // PERM_TILE,),
        in_specs=[pl.BlockSpec((PERM_TILE, B_WIDTH), functools.partial(lambda cb, i: (i, cb), cb)) for _, cb in items],
        out_specs=[_regrouped_spec(rate, PERM_TILE) for _ in items],
        out_shape=[SDS((rate, t // rate, B_WIDTH), a.dtype) for a, _ in items],
        scratch_shapes=[pltpu.VMEM((CHUNKS, PERM_TILE, LANES), F32)],
        name=name, compiler_params=_params())(*[a for a, _ in items])


def _head_lane_mask(h):
    lane = lax.broadcasted_iota(jnp.int32, (1, LANES), 1)
    return (lane < HEAD_DIM) if h == 0 else (lane >= HEAD_DIM)


def _attn_fwd(name, q, k, v, ride=None):
    rate, length = q[0].shape[0], q[0].shape[1]
    nb = length // ATT_BLOCK
    scale = HEAD_DIM ** -0.5

    def body(q_ref, kc_ref, kp_ref, vc_ref, vp_ref, o_ref, l_ref):
        n = pl.program_id(1)
        qi = lax.broadcasted_iota(jnp.int32, (ATT_BLOCK, 2 * ATT_BLOCK), 0)
        cj = lax.broadcasted_iota(jnp.int32, (ATT_BLOCK, 2 * ATT_BLOCK), 1)
        has_prev = jnp.where(n > 0, 0, 2 * ATT_BLOCK)
        mask = ((cj < ATT_BLOCK) & (cj >= qi + has_prev)) | ((cj >= ATT_BLOCK) & (cj - ATT_BLOCK <= qi))
        heads = [(hp, h) for hp in range(CHUNKS) for h in range(2)]
        q2, k2, v2 = {}, {}, {}
        for hp in range(CHUNKS):
            ls = slice(hp * LANES, (hp + 1) * LANES)
            q2[hp] = q_ref[:, ls]
            k2[hp] = jnp.concatenate([kp_ref[:, ls], kc_ref[:, ls]], axis=0)
            v2[hp] = jnp.concatenate([vp_ref[:, ls], vc_ref[:, ls]], axis=0)
        scores = {}
        for hp, h in heads:
            scores[hp, h] = _dot(jnp.where(_head_lane_mask(h), q2[hp], jnp.zeros_like(q2[hp])), k2[hp], NT) * scale
        probs, lses = {}, {}
        for hp, h in heads:
            s = jnp.where(mask, scores[hp, h], NEG_INF)
            m = jnp.max(s, axis=1, keepdims=True)
            p = jnp.exp(s - m)
            den = jnp.sum(p, axis=1, keepdims=True)
            lses[hp, h] = m + jnp.log(den)
            probs[hp, h] = (p / den).astype(BF16)
        for hp in range(CHUNKS):
            ls = slice(hp * LANES, (hp + 1) * LANES)
            o_acc = None
            for h in range(2):
                o = _dot(probs[hp, h], jnp.where(_head_lane_mask(h), v2[hp], jnp.zeros_like(v2[hp])), NN)
                o_acc = o if o_acc is None else o_acc + o
            o_ref[:, ls] = o_acc
            zeros = jnp.zeros((ATT_BLOCK, LANES), F32)
            l_ref[:, ls] = jnp.where(_head_lane_mask(1), lses[hp, 1] + zeros, lses[hp, 0] + zeros)

    def cur(cb):
        return pl.BlockSpec((None, ATT_BLOCK, B_WIDTH), lambda r, n: (r, n, cb))

    def prev(cb):
        return pl.BlockSpec((None, ATT_BLOCK, B_WIDTH), lambda r, n: (r, jnp.maximum(n - 1, 0), cb))

    out = pl.BlockSpec((None, ATT_BLOCK, B_WIDTH), lambda r, n: (r, n, 0))
    return _call(
        body, grid=(rate, nb),
        in_specs=[cur(q[1]), cur(k[1]), prev(k[1]), cur(v[1]), prev(v[1])],
        out_specs=[out, out], out_shape=[SDS((rate, length, B_WIDTH), F32)] * 2,
        operands=[q[0], k[0], k[0], v[0], v[0]], name=name, ride=ride)


def _attn_merge(a_out, o_list, l_list):
    t = a_out.shape[0]
    tm = PERM_TILE
    r1, r2 = DIL_RATES[1], DIL_RATES[2]

    def body(a_ref, o0, o1, o2, l0, l1, l2, cat_ref, lt_ref, lt1_ref, lt2_ref, so1, so2, sl1, sl2, slt):
        _regroup_in(o1, so1, r1, tm)
        _regroup_in(l1, sl1, r1, tm)
        _regroup_in(o2, so2, r2, tm)
        _regroup_in(l2, sl2, r2, tm)
        cat_ref[:, :A_WIDTH] = a_ref[...]
        for c in range(CHUNKS):
            ls = slice(c * LANES, (c + 1) * LANES)
            lg = [l0[:, ls], sl1[c], sl2[c]]
            m = jnp.maximum(jnp.maximum(lg[0], lg[1]), lg[2])
            es = [jnp.exp(l - m) for l in lg]
            den = es[0] + es[1] + es[2]
            b = (es[0] * o0[:, ls] + es[1] * so1[c] + es[2] * so2[c]) / den
            cat_ref[:, A_WIDTH + c * LANES:A_WIDTH + (c + 1) * LANES] = b.astype(BF16)
            lt = m + jnp.log(den)
            lt_ref[:, ls] = lt
            slt[c] = lt
        _regroup_out(slt, 0, lt1_ref, r1, tm)
        _regroup_out(slt, 0, lt2_ref, r2, tm)

    blk = _row_spec(tm, B_WIDTH)
    g1, g2 = _regrouped_spec(r1, tm), _regrouped_spec(r2, tm)
    return pl.pallas_call(
        body, grid=(t // tm,), in_specs=[blk, blk, g1, g2, blk, g1, g2],
        out_specs=[_row_spec(tm, A_WIDTH + B_WIDTH), blk, g1, g2],
        out_shape=[SDS((t, A_WIDTH + B_WIDTH), BF16), SDS((t, B_WIDTH), F32), SDS((r1, t // r1, B_WIDTH), F32),
                   SDS((r2, t // r2, B_WIDTH), F32)],
        scratch_shapes=[pltpu.VMEM((CHUNKS, tm, LANES), F32)] * 5,
        name="attn_merge", compiler_params=_params())(a_out, *o_list, *l_list)


def _attn_bwd_prep(dcat, cat):
    t = dcat.shape[0]
    tm = PERM_TILE
    r1, r2 = DIL_RATES[1], DIL_RATES[2]

    def body(d_ref, b_ref, db_ref, dd_ref, db1_ref, dd1_ref, db2_ref, dd2_ref, sdb, sdd):
        seg = _segment_mean_matrix(HEAD_DIM, scale=1.0)
        for c in range(CHUNKS):
            ls = slice(c * LANES, (c + 1) * LANES)
            d = d_ref[:, ls]
            dsum = _segment_dot(d * b_ref[:, ls].astype(F32), seg)
            db_ref[:, ls] = d.astype(BF16)
            dd_ref[:, ls] = dsum
            sdb[c] = d
            sdd[c] = dsum
        _regroup_out(sdb, 0, db1_ref, r1, tm)
        _regroup_out(sdd, 0, dd1_ref, r1, tm)
        _regroup_out(sdb, 0, db2_ref, r2, tm)
        _regroup_out(sdd, 0, dd2_ref, r2, tm)

    right = pl.BlockSpec((tm, B_WIDTH), lambda i: (i, 1))
    blk = _row_spec(tm, B_WIDTH)
    g1, g2 = _regrouped_spec(r1, tm), _regrouped_spec(r2, tm)
    return pl.pallas_call(
        body, grid=(t // tm,), in_specs=[right, right], out_specs=[blk, blk, g1, g1, g2, g2],
        out_shape=[SDS((t, B_WIDTH), BF16), SDS((t, B_WIDTH), F32), SDS((r1, t // r1, B_WIDTH), BF16),
                   SDS((r1, t // r1, B_WIDTH), F32), SDS((r2, t // r2, B_WIDTH), BF16), SDS((r2, t // r2, B_WIDTH), F32)],
        scratch_shapes=[pltpu.VMEM((CHUNKS, tm, LANES), F32)] * 2,
        name="attn_bwd_prep", compiler_params=_params())(dcat, cat)


def _attn_bwd(name, q, k, v, db, lse, dd, ride=None):
    rate, length = db.shape[0], db.shape[1]
    nb = length // ATT_BLOCK
    scale = HEAD_DIM ** -0.5

    def body(qa_ref, qb_ref, k_ref, v_ref, dba_ref, dbb_ref, la_ref, lb_ref, da_ref, dbd_ref, dq_ref, dk_ref, dv_ref, carry):
        m = pl.program_id(1)

        @pl.when(m == 0)
        def _():
            carry[...] = jnp.zeros_like(carry)

        row = lax.broadcasted_iota(jnp.int32, (2 * ATT_BLOCK, ATT_BLOCK), 0)
        kj = lax.broadcasted_iota(jnp.int32, (2 * ATT_BLOCK, ATT_BLOCK), 1)
        no_next = jnp.where(m + 1 < nb, 0, 2 * ATT_BLOCK)
        mask = ((row < ATT_BLOCK) & (kj <= row)) | ((row >= ATT_BLOCK) & (kj >= row - ATT_BLOCK + no_next))
        heads = [(hp, h) for hp in range(CHUNKS) for h in range(2)]
        q2, db2, lse2, dd2, k2, v2 = {}, {}, {}, {}, {}, {}
        for hp in range(CHUNKS):
            ls = slice(hp * LANES, (hp + 1) * LANES)
            k2[hp], v2[hp] = k_ref[:, ls], v_ref[:, ls]
            q2[hp] = jnp.concatenate([qa_ref[:, ls], qb_ref[:, ls]], axis=0)
            db2[hp] = jnp.concatenate([dba_ref[:, ls], dbb_ref[:, ls]], axis=0)
            lse2[hp] = jnp.concatenate([la_ref[:, ls], lb_ref[:, ls]], axis=0)
            dd2[hp] = jnp.concatenate([da_ref[:, ls], dbd_ref[:, ls]], axis=0)
        km, scores, dps = {}, {}, {}
        for hp, h in heads:
            hm = _head_lane_mask(h)
            km[hp, h] = jnp.where(hm, k2[hp], jnp.zeros_like(k2[hp]))
            scores[hp, h] = _dot(q2[hp], km[hp, h], NT) * scale
            dps[hp, h] = _dot(db2[hp], jnp.where(hm, v2[hp], jnp.zeros_like(v2[hp])), NT)
        probs, dss = {}, {}
        for hp, h in heads:
            hm = _head_lane_mask(h)
            lse_col = jnp.max(jnp.where(hm, lse2[hp], NEG_INF), axis=1, keepdims=True)
            dd_col = jnp.max(jnp.where(hm, dd2[hp], NEG_INF), axis=1, keepdims=True)
            p = jnp.where(mask, jnp.exp(scores[hp, h] - lse_col), 0.0)
            probs[hp, h] = p.astype(BF16)
            dss[hp, h] = (p * (dps[hp, h] - dd_col) * scale).astype(BF16)
        for hp in range(CHUNKS):
            ls = slice(hp * LANES, (hp + 1) * LANES)
            dq_acc, dk_acc, dv_acc = None, None, None
            for h in range(2):
                hm = _head_lane_mask(h)
                dvc = _dot(probs[hp, h], jnp.where(hm, db2[hp], jnp.zeros_like(db2[hp])), TN)
                dqc = _dot(dss[hp, h], km[hp, h], NN)
                dkc = _dot(dss[hp, h], jnp.where(hm, q2[hp], jnp.zeros_like(q2[hp])), TN)
                dq_acc = dqc if dq_acc is None else dq_acc + dqc
                dk_acc = dkc if dk_acc is None else dk_acc + dkc
                dv_acc = dvc if dv_acc is None else dv_acc + dvc
            dq_ref[:, ls] = (dq_acc[:ATT_BLOCK] + carry[:, ls]).astype(BF16)
            carry[:, ls] = dq_acc[ATT_BLOCK:]
            dk_ref[:, ls] = dk_acc.astype(BF16)
            dv_ref[:, ls] = dv_acc.astype(BF16)

    def cur(cb):
        return pl.BlockSpec((None, ATT_BLOCK, B_WIDTH), lambda r, n: (r, n, cb))

    def nxt(cb):
        return pl.BlockSpec((None, ATT_BLOCK, B_WIDTH), lambda r, n: (r, jnp.minimum(n + 1, nb - 1), cb))

    out = cur(0)
    return _call(
        body, grid=(rate, nb),
        in_specs=[cur(q[1]), nxt(q[1]), cur(k[1]), cur(v[1]), cur(0), nxt(0), cur(0), nxt(0), cur(0), nxt(0)],
        out_specs=[out, out, out], out_shape=[SDS((rate, length, B_WIDTH), BF16)] * 3,
        scratch_shapes=[pltpu.VMEM((ATT_BLOCK, B_WIDTH), F32)],
        operands=[q[0], q[0], k[0], v[0], db, db, lse, lse, dd, dd], name=name, ride=ride)


AB_IN = 2 * A_WIDTH + 3 * N_DIL * B_WIDTH
ASM_TILE = 256


def _dproj_assemble(proj, d_a, dq, dk, dv, gains, tabs):
    t = proj.shape[0]
    n_in = 3 * N_DIL

    def body(p_ref, da_ref, *rest):
        grads = rest[:n_in]
        g_ref, c_ref, s1_ref, s2_ref, o_ref, dg_ref = rest[n_in:n_in + 6]
        scratch = rest[n_in + 6:]

        @pl.when(pl.program_id(0) == 0)
        def _():
            dg_ref[...] = jnp.zeros_like(dg_ref)

        chunk = {}
        k_scr = 0
        for j in range(n_in):
            g = j % N_DIL
            if DIL_RATES[g] == 1:
                for ci in range(CHUNKS):
                    chunk[j, ci] = functools.partial(lambda r, ci: r[:, ci * LANES:(ci + 1) * LANES].astype(F32), grads[j], ci)
            else:
                scr = scratch[k_scr]
                k_scr += 1
                _regroup_in(grads[j], scr, DIL_RATES[g], ASM_TILE)
                for ci in range(CHUNKS):
                    chunk[j, ci] = functools.partial(lambda s, ci: s[ci], scr, ci)

        seg = _segment_mean_matrix(HEAD_DIM)
        c, s1, s2 = c_ref[...], s1_ref[...], s2_ref[...]
        o_ref[:, :2 * A_WIDTH] = da_ref[...]
        for jg in range(2 * N_DIL):
            for ci in range(CHUNKS):
                col = jg * B_WIDTH + ci * LANES
                src = slice(2 * A_WIDTH + col, 2 * A_WIDTH + col + LANES)
                xv = p_ref[:, src].astype(F32)
                r = lax.rsqrt(_segment_dot(xv * xv, seg) + EPS)
                xh = xv * r
                gain = g_ref[:, col:col + LANES]
                do = chunk[jg, ci]()
                dy = do * c + pltpu.roll(do * s1, 8, axis=1) + pltpu.roll(do * s2, LANES - 8, axis=1)
                dg_ref[:, col:col + LANES] += jnp.sum(dy * xh, axis=0, keepdims=True)
                dxh = dy * gain
                o_ref[:, src] = (r * (dxh - xh * _segment_dot(dxh * xh, seg))).astype(BF16)
        v0 = 2 * A_WIDTH + QK_COLS
        for g in range(N_DIL):
            for ci in range(CHUNKS):
                col = v0 + g * B_WIDTH + ci * LANES
                o_ref[:, col:col + LANES] = chunk[2 * N_DIL + g, ci]().astype(BF16)

    specs = [_row_spec(ASM_TILE, B_WIDTH) if r == 1 else _regrouped_spec(r, ASM_TILE) for r in DIL_RATES] * 3
    n_scr = 3 * sum(1 for r in DIL_RATES if r > 1)
    tab = _row_spec(ASM_TILE, LANES)
    return pl.pallas_call(
        body, grid=(t // ASM_TILE,),
        in_specs=[_row_spec(ASM_TILE, AB_IN), _row_spec(ASM_TILE, 2 * A_WIDTH)] + specs
        + [_const_spec((1, QK_COLS)), tab, tab, tab],
        out_specs=[_row_spec(ASM_TILE, AB_IN), _const_spec((1, QK_COLS))],
        out_shape=[SDS((t, AB_IN), BF16), SDS((1, QK_COLS), F32)],
        scratch_shapes=[pltpu.VMEM((CHUNKS, ASM_TILE, LANES), F32)] * n_scr,
        name="dproj_assemble", compiler_params=_params())(proj, d_a, *dq, *dk, *dv, gains, *tabs)


def _fold_heads(dg_lane):
    n = dg_lane.shape[1]

    def body(x_ref, o_ref):
        r = lax.broadcasted_iota(jnp.int32, (B_WIDTH, B_WIDTH), 0) % HEAD_DIM
        c = lax.broadcasted_iota(jnp.int32, (B_WIDTH, B_WIDTH), 1) % HEAD_DIM
        fold = jnp.where(r == c, 1.0, 0.0).astype(F32)
        for jg in range(n // B_WIDTH):
            ls = slice(jg * B_WIDTH, (jg + 1) * B_WIDTH)
            o_ref[:, ls] = _dot_hi(jnp.broadcast_to(x_ref[:, ls], (8, B_WIDTH)), fold)

    return pl.pallas_call(body, out_shape=SDS((8, n), F32), name="fold_heads", compiler_params=_params())(dg_lane)


CD_TILE = 256
TAP_ROWS = 64
CD_IN = 2 * C_WIDTH + 3 * 512


def _cd_split(pv):
    w = C_WIDTH
    return pv[:, :w], pv[:, w:2 * w], pv[:, 2 * w:3 * w], pv[:, 3 * w:4 * w], pv[:, 4 * w:5 * w]


def _shifted_copies(src, dst, rows):
    dst[0, :rows] = src[...]
    for b in range(1, 8):
        dst[b, :rows - 8] = src[pl.ds(b, rows - 8), :]


def _rows_from(shifted, start, n, lanes=slice(None)):
    b = start % 8
    return shifted[b, pl.ds(start - b, n), lanes]


def _mixer_cd_fwd(proj, cw, cb, lg, lb, dw):
    t = proj.shape[0]
    per = CD_TILE // HALO

    def body(h_ref, m_ref, cw_ref, cb_ref, lg_ref, lb_ref, dw_ref, o_ref, c1_ref, c_scr, e_scr, c_sh):
        not_first = (pl.program_id(0) > 0).astype(F32)
        lanes = [slice(c * LANES, (c + 1) * LANES) for c in range(C_WIDTH // LANES)]

        def col(ref, part, ls):
            return ref[:, part * C_WIDTH + ls.start:part * C_WIDTH + ls.stop].astype(F32)

        for ls in lanes:
            c_scr[:HALO, ls] = col(h_ref, 0, ls) * _sigmoid(col(h_ref, 1, ls)) * not_first
            c_scr[HALO:, ls] = col(m_ref, 0, ls) * _sigmoid(col(m_ref, 1, ls))
            e_scr[:HALO, ls] = col(h_ref, 3, ls) * col(h_ref, 4, ls) * not_first
            e_scr[HALO:, ls] = col(m_ref, 3, ls) * col(m_ref, 4, ls)
        _shifted_copies(c_scr, c_sh, HALO + CD_TILE)
        for ls in lanes:
            for r0 in range(0, CD_TILE, TAP_ROWS):
                acc = jnp.zeros((TAP_ROWS, LANES), F32)
                for k in range(C_KERNEL):
                    acc = acc + cw_ref[k:k + 1, ls] * _rows_from(c_sh, r0 + HALO - (C_KERNEL - 1) + k, TAP_ROWS, ls)
                c1_ref[r0:r0 + TAP_ROWS, ls] = acc + cb_ref[:, ls]
        mean = sum(jnp.sum(c1_ref[:, ls], axis=-1, keepdims=True) for ls in lanes) * (1.0 / C_WIDTH)
        var = sum(jnp.sum((c1_ref[:, ls] - mean) ** 2, axis=-1, keepdims=True) for ls in lanes) * (1.0 / C_WIDTH)
        rs = lax.rsqrt(var + EPS)
        for ls in lanes:
            c2 = (c1_ref[:, ls] - mean) * rs * lg_ref[:, ls] + lb_ref[:, ls]
            o_ref[:, ls] = (c2 * _sigmoid(c2)).astype(BF16)
            d1 = jnp.zeros((CD_TILE, LANES), F32)
            for k in range(D_KERNEL):
                d1 = d1 + dw_ref[k:k + 1, ls] * e_scr[pl.ds(HALO - (D_KERNEL - 1) + k, CD_TILE), ls]
            o_ref[:, C_WIDTH + ls.start:C_WIDTH + ls.stop] = (col(m_ref, 2, ls) * d1).astype(BF16)

    return pl.pallas_call(
        body, grid=(t // CD_TILE,),
        in_specs=[pl.BlockSpec((HALO, CD_IN), lambda i: (jnp.maximum(i * per - 1, 0), 0)), _row_spec(CD_TILE, CD_IN),
                  _const_spec((32, C_WIDTH)), _const_spec((1, C_WIDTH)), _const_spec((1, C_WIDTH)), _const_spec((1, C_WIDTH)),
                  _const_spec((8, C_WIDTH))],
        out_specs=[_row_spec(CD_TILE, 2 * C_WIDTH), _row_spec(CD_TILE, C_WIDTH)],
        out_shape=[SDS((t, 2 * C_WIDTH), BF16), SDS((t, C_WIDTH), F32)],
        scratch_shapes=[pltpu.VMEM((HALO + CD_TILE, C_WIDTH), F32)] * 2 + [pltpu.VMEM((8, HALO + CD_TILE, C_WIDTH), F32)],
        name="mixer_cd_fwd", compiler_params=_params())(proj, proj, cw, cb, lg, lb, dw)


def _mixer_cd_bwd(proj, dcat, c1, cw, lg, lb, dw, ride=None):
    t = proj.shape[0]
    per = CD_TILE // HALO
    nt = t // CD_TILE
    ext = CD_TILE + HALO

    def body(hp_ref, m_ref, hn_ref, dm_ref, dn_ref, c1m_ref, c1n_ref, cw_ref, lg_ref, lb_ref, dw_ref,
             dp_ref, dcw_ref, dcb_ref, dlg_ref, dlb_ref, ddw_ref, c_scr, e_scr, dc1_scr, dd1_scr, c_sh, dc1_sh, dcw_acc,
             dvh_scr, vhat_scr):
        i = pl.program_id(0)

        @pl.when(i == 0)
        def _():
            for r in (dcw_acc, dcb_ref, dlg_ref, dlb_ref, ddw_ref):
                r[...] = jnp.zeros_like(r)

        not_first = (i > 0).astype(F32)
        not_last = (i < nt - 1).astype(F32)
        main = slice(HALO, HALO + CD_TILE)
        lanes = [slice(c * LANES, (c + 1) * LANES) for c in range(C_WIDTH // LANES)]

        def col(ref, part, ls):
            return ref[:, part * C_WIDTH + ls.start:part * C_WIDTH + ls.stop].astype(F32)

        for ls in lanes:
            c_scr[:HALO, ls] = col(hp_ref, 0, ls) * _sigmoid(col(hp_ref, 1, ls)) * not_first
            c_scr[main, ls] = col(m_ref, 0, ls) * _sigmoid(col(m_ref, 1, ls))
            c_scr[HALO + CD_TILE:, ls] = col(hn_ref, 0, ls) * _sigmoid(col(hn_ref, 1, ls)) * not_last
            e_scr[:HALO, ls] = col(hp_ref, 3, ls) * col(hp_ref, 4, ls) * not_first
            e_scr[main, ls] = col(m_ref, 3, ls) * col(m_ref, 4, ls)
            e_scr[HALO + CD_TILE:, ls] = col(hn_ref, 3, ls) * col(hn_ref, 4, ls) * not_last
        _shifted_copies(c_scr, c_sh, 2 * HALO + CD_TILE)

        def c1_of(ls):
            return jnp.concatenate([c1m_ref[:, ls], c1n_ref[:, ls]], axis=0)

        mean = sum(jnp.sum(c1_of(ls), axis=-1, keepdims=True) for ls in lanes) * (1.0 / C_WIDTH)
        var = sum(jnp.sum((c1_of(ls) - mean) ** 2, axis=-1, keepdims=True) for ls in lanes) * (1.0 / C_WIDTH)
        rs = lax.rsqrt(var + EPS)
        sum_dvh, sum_dvh_vhat = 0.0, 0.0
        for ls in lanes:
            vhat = (c1_of(ls) - mean) * rs
            c2 = vhat * lg_ref[:, ls] + lb_ref[:, ls]
            sig = _sigmoid(c2)
            dc = jnp.concatenate([dm_ref[:, ls], dn_ref[:, ls] * not_last], axis=0)
            dc2 = dc * (sig * (1.0 + c2 * (1.0 - sig)))
            dvh = dc2 * lg_ref[:, ls]
            sum_dvh = sum_dvh + jnp.sum(dvh, axis=-1, keepdims=True)
            sum_dvh_vhat = sum_dvh_vhat + jnp.sum(dvh * vhat, axis=-1, keepdims=True)
            dvh_scr[:, ls] = dvh
            vhat_scr[:, ls] = vhat
            dlg_ref[:, ls] += jnp.sum((dc2 * vhat)[:CD_TILE], axis=0, keepdims=True)
            dlb_ref[:, ls] += jnp.sum(dc2[:CD_TILE], axis=0, keepdims=True)
        for ls in lanes:
            dc1 = rs * (dvh_scr[:, ls] - sum_dvh * (1.0 / C_WIDTH) - vhat_scr[:, ls] * (sum_dvh_vhat * (1.0 / C_WIDTH)))
            dc1_scr[:, ls] = dc1
            dcb_ref[:, ls] += jnp.sum(dc1[:CD_TILE], axis=0, keepdims=True)
        _shifted_copies(dc1_scr, dc1_sh, ext)
        for ls in lanes:
            for r0 in range(0, CD_TILE, TAP_ROWS):
                rows = slice(r0, r0 + TAP_ROWS)
                dc1_m = dc1_scr[rows, ls]
                dc0 = jnp.zeros((TAP_ROWS, LANES), F32)
                for k in range(C_KERNEL):
                    dc0 = dc0 + cw_ref[k:k + 1, ls] * _rows_from(dc1_sh, r0 + C_KERNEL - 1 - k, TAP_ROWS, ls)
                    prod = dc1_m * _rows_from(c_sh, r0 + HALO - (C_KERNEL - 1) + k, TAP_ROWS, ls)
                    dcw_acc[k, :, ls] += prod.reshape(TAP_ROWS // 8, 8, LANES).sum(axis=0)
                g_m = m_ref[rows, C_WIDTH + ls.start:C_WIDTH + ls.stop].astype(F32)
                a_m = m_ref[rows, ls].astype(F32)
                sig_m = _sigmoid(g_m)
                dp_ref[rows, ls] = (dc0 * sig_m).astype(BF16)
                dp_ref[rows, C_WIDTH + ls.start:C_WIDTH + ls.stop] = (dc0 * a_m * sig_m * (1.0 - sig_m)).astype(BF16)

        @pl.when(i == nt - 1)
        def _():
            dcw_ref[...] = jnp.sum(dcw_acc[...], axis=1)

        for ls in lanes:
            wide = slice(C_WIDTH + ls.start, C_WIDTH + ls.stop)
            d1 = jnp.zeros((CD_TILE, LANES), F32)
            for k in range(D_KERNEL):
                d1 = d1 + dw_ref[k:k + 1, ls] * e_scr[pl.ds(HALO - (D_KERNEL - 1) + k, CD_TILE), ls]
            dd_m = dm_ref[:, wide]
            dd1 = jnp.concatenate([dd_m * col(m_ref, 2, ls), dn_ref[:, wide] * col(hn_ref, 2, ls) * not_last], axis=0)
            dd1_scr[:, ls] = dd1
            dp_ref[:, 2 * C_WIDTH + ls.start:2 * C_WIDTH + ls.stop] = (dd_m * d1).astype(BF16)
            de = jnp.zeros((CD_TILE, LANES), F32)
            for k in range(D_KERNEL):
                de = de + dw_ref[k:k + 1, ls] * dd1_scr[pl.ds(D_KERNEL - 1 - k, CD_TILE), ls]
                ddw_ref[k:k + 1, ls] += jnp.sum(dd1[:CD_TILE] * e_scr[pl.ds(HALO - (D_KERNEL - 1) + k, CD_TILE), ls], axis=0, keepdims=True)
            dp_ref[:, 3 * C_WIDTH + ls.start:3 * C_WIDTH + ls.stop] = (de * col(m_ref, 4, ls)).astype(BF16)
            dp_ref[:, 4 * C_WIDTH + ls.start:4 * C_WIDTH + ls.stop] = (de * col(m_ref, 3, ls)).astype(BF16)

    halo_prev = lambda i: (jnp.maximum(i * per - 1, 0), 0)
    halo_next = lambda i: (jnp.minimum((i + 1) * per, t // HALO - 1), 0)
    vec = _const_spec((1, C_WIDTH))
    return _call(
        body, grid=(nt,),
        in_specs=[pl.BlockSpec((HALO, CD_IN), halo_prev), _row_spec(CD_TILE, CD_IN), pl.BlockSpec((HALO, CD_IN), halo_next),
                  _row_spec(CD_TILE, 2 * C_WIDTH), pl.BlockSpec((HALO, 2 * C_WIDTH), halo_next),
                  _row_spec(CD_TILE, C_WIDTH), pl.BlockSpec((HALO, C_WIDTH), halo_next),
                  _const_spec((32, C_WIDTH)), vec, vec, _const_spec((8, C_WIDTH))],
        out_specs=[_row_spec(CD_TILE, CD_IN), _const_spec((32, C_WIDTH)), vec, vec, vec, _const_spec((8, C_WIDTH))],
        out_shape=[SDS((t, CD_IN), BF16), SDS((32, C_WIDTH), F32), SDS((1, C_WIDTH), F32), SDS((1, C_WIDTH), F32),
                   SDS((1, C_WIDTH), F32), SDS((8, C_WIDTH), F32)],
        scratch_shapes=[pltpu.VMEM((2 * HALO + CD_TILE, C_WIDTH), F32)] * 2 + [pltpu.VMEM((ext, C_WIDTH), F32)] * 2
        + [pltpu.VMEM((8, 2 * HALO + CD_TILE, C_WIDTH), F32), pltpu.VMEM((8, ext, C_WIDTH), F32),
           pltpu.VMEM((32, 8, C_WIDTH), F32)] + [pltpu.VMEM((ext, C_WIDTH), F32)] * 2,
        operands=[proj, proj, proj, dcat, dcat, c1, c1, cw, lg, lb, dw], name="mixer_cd_bwd", ride=ride)


def _wgrad(name, pairs, out_rc, t, ride):
    tk = TILES["wgrad"]
    r, c = out_rc
    n = len(pairs)

    def body(*refs):
        ab, out_refs = refs[:2 * n], refs[2 * n:]
        k = pl.program_id(1)
        parts = [_dot(ab[2 * j][...], ab[2 * j + 1][...], TN) for j in range(n)]

        @pl.when(k == 0)
        def _():
            for a in range(n):
                out_refs[a][...] = parts[a]

        @pl.when(k > 0)
        def _():
            for a in range(n):
                out_refs[a][...] += parts[a]

    operands, in_specs = [], []
    for lhs, lhs_spec, rhs, rhs_spec in pairs:
        operands += [lhs, rhs]
        in_specs += [lhs_spec, rhs_spec]
    res = _call(body, grid=(N_CHIPS, t // tk), in_specs=in_specs,
                out_specs=[pl.BlockSpec((None, r, c), lambda p, k: (p, 0, 0))] * n,
                out_shape=[SDS((N_CHIPS, r, c), F32)] * n, operands=operands, name=name, ride=ride)
    outs, ride_res = (res, None) if ride is None else res
    outs = [o.reshape(N_CHIPS, 2, r // 2, c) for o in outs]
    return outs if ride is None else (outs, ride_res)


def _wgrad_col_sharded(name, h, dz_list, three_d, ride=None):
    t, d = h.shape
    tk = TILES["wgrad"]
    n4 = dz_list[0].shape[-1] if three_d else dz_list[0].shape[-1] // N_CHIPS
    hs = pl.BlockSpec((tk, d), lambda p, k: (k, 0))
    zs = pl.BlockSpec((None, tk, n4), lambda p, k: (p, k, 0)) if three_d else pl.BlockSpec((tk, n4), lambda p, k: (k, p))
    return _wgrad(name, [(h, hs, dz, zs) for dz in dz_list], (d, n4), t, ride)


def _wgrad_row_sharded(name, a, g, three_d, ride=None):
    many = isinstance(a, (list, tuple))
    a_list = list(a) if many else [a]
    t, d = g.shape
    tk = TILES["wgrad"]
    k4 = a_list[0].shape[-1] if three_d else a_list[0].shape[-1] // N_CHIPS
    a_spec = pl.BlockSpec((None, tk, k4), lambda p, k: (p, k, 0)) if three_d else pl.BlockSpec((tk, k4), lambda p, k: (k, p))
    gs = pl.BlockSpec((tk, d), lambda p, k: (k, 0))
    res = _wgrad(name, [(a_j, a_spec, g, gs) for a_j in a_list], (k4, d), t, ride)
    if many:
        return res
    return res[0] if ride is None else (res[0][0], res[1])


def _mesh_scalars():
    return jnp.stack([lax.axis_index("c"), 2 * lax.axis_index("x") + lax.axis_index("y")]).astype(jnp.int32)


def _stage_own(name, w, layer, dtype):
    layers, r, cols = w.shape
    h = r // 2

    def body(s_ref, x_ref, o_ref):
        o_ref[...] = x_ref[...].astype(dtype)

    return pl.pallas_call(
        body,
        grid_spec=pltpu.PrefetchScalarGridSpec(
            num_scalar_prefetch=1, grid=(2,),
            in_specs=[pl.BlockSpec((None, h, cols), lambda i, s: (2 * layer + i, 0, 0))],
            out_specs=pl.BlockSpec((None, None, h, cols), lambda i, s: (s[1], i, 0, 0))),
        out_shape=SDS((N_CHIPS, 2, h, cols), dtype), name=name,
        compiler_params=_params())(_mesh_scalars(), w.reshape(2 * layers, h, cols))


def _remote(src, dst, send_sem, recv_sem, device):
    return pltpu.make_async_remote_copy(src, dst, send_sem, recv_sem, device_id=device, device_id_type=MESH)


def _ride_gather_send(bufs):
    n = len(bufs)

    def each(b, sems, act):
        send, recv = sems
        x, y, c, p, others = _position()
        for t in range(n):
            for j, (qx, qy) in enumerate(others):
                act(b[t].at[p, c], b[t].at[2 * qx + qy, c], send.at[t, j], recv.at[t, j], (qx, qy, c))

    def start(ins, b, new, sems):
        each(b, sems, lambda mine, landed, s, r, dev: _remote(mine, mine, s, r, dev).start())

    def finish(ins, b, new, sems):
        def act(mine, landed, s, r, dev):
            _remote(mine, mine, s, r, dev).wait_send()
            _remote(landed, landed, s, r, dev).wait_recv()
        each(b, sems, act)

    return _Ride([], bufs, [], [(n, 3), (n, 3)], start, finish)


def _ride_gather_pass(bufs):
    n = len(bufs)

    def each(b, sems, act):
        send, recv = sems
        x, y, c, p, others = _position()
        for t in range(n):
            for j, (qx, qy) in enumerate(others):
                act(b[t].at[2 * qx + qy, c], b[t].at[2 * qx + qy, 1 - c], send.at[t, j], recv.at[t, j], (x, y, 1 - c))

    def start(ins, b, new, sems):
        each(b, sems, lambda landed, passed, s, r, dev: _remote(landed, landed, s, r, dev).start())

    def finish(ins, b, new, sems):
        def act(landed, passed, s, r, dev):
            _remote(landed, landed, s, r, dev).wait_send()
            _remote(passed, passed, s, r, dev).wait_recv()
        each(b, sems, act)

    return _Ride([], bufs, [], [(n, 3), (n, 3)], start, finish)


def _ride_swap(tensors):
    n = len(tensors)

    def each(ins, new, sems, act):
        send, recv = sems
        x, y, c, _, _ = _position()
        for t in range(n):
            act(_remote(ins[t].at[:, 1 - c], new[t], send.at[t], recv.at[t], (x, y, 1 - c)))

    def start(ins, b, new, sems):
        each(ins, new, sems, lambda cp: cp.start())

    def finish(ins, b, new, sems):
        each(ins, new, sems, lambda cp: cp.wait())

    return _Ride(tensors, [], [SDS((s.shape[0],) + s.shape[2:], s.dtype) for s in tensors], [(n,), (n,)], start, finish)


def _ride_scatter(tensors, landing):
    n = len(tensors)

    def each(ins, b, sems, act):
        send, recv = sems
        x, y, c, p, others = _position()
        for t in range(n):
            for j, (qx, qy) in enumerate(others):
                q = 2 * qx + qy
                act(ins[t].at[q], b[t].at[p], b[t].at[q], send.at[t, j], recv.at[t, j], (qx, qy, c))

    def start(ins, b, new, sems):
        each(ins, b, sems, lambda src, dst, landed, s, r, dev: _remote(src, dst, s, r, dev).start())

    def finish(ins, b, new, sems):
        def act(src, dst, landed, s, r, dev):
            _remote(src, dst, s, r, dev).wait_send()
            _remote(landed, landed, s, r, dev).wait_recv()
        each(ins, b, sems, act)

    return _Ride(tensors, landing, [], [(n, 3), (n, 3)], start, finish)


def _ride_join(bufs):
    n = len(bufs)

    def each(b, sems, act):
        send, recv = sems
        x, y, c, _, _ = _position()
        for t in range(n):
            act(b[t].at[c], b[t].at[1 - c], send.at[t], recv.at[t], (x, y, 1 - c))

    def start(ins, b, new, sems):
        each(b, sems, lambda mine, theirs, s, r, dev: _remote(mine, mine, s, r, dev).start())

    def finish(ins, b, new, sems):
        def act(mine, theirs, s, r, dev):
            _remote(mine, mine, s, r, dev).wait_send()
            _remote(theirs, theirs, s, r, dev).wait_recv()
        each(b, sems, act)

    return _Ride([], bufs, [], [(n,), (n,)], start, finish)


def _all_reduce_small(pack):
    rows = pack.shape[0]
    n_dev = 2 * N_CHIPS

    def body(x_ref, o_ref, land, send, recv):
        x, y, c, p, _ = _position()
        me = 2 * p + c
        land[me] = x_ref[...]
        peers = [(dx, dy, dc) for dx in range(2) for dy in range(2) for dc in range(2) if (dx, dy, dc) != (0, 0, 0)]
        for j, (dx, dy, dc) in enumerate(peers):
            _remote(land.at[me], land.at[me], send.at[j], recv.at[j], (x ^ dx, y ^ dy, c ^ dc)).start()
        for j, (dx, dy, dc) in enumerate(peers):
            src = 4 * (x ^ dx) + 2 * (y ^ dy) + (c ^ dc)
            _remote(land.at[me], land.at[me], send.at[j], recv.at[j], (x ^ dx, y ^ dy, c ^ dc)).wait_send()
            _remote(land.at[src], land.at[src], send.at[j], recv.at[j], (x ^ dx, y ^ dy, c ^ dc)).wait_recv()
        acc = land[0]
        for dev in range(1, n_dev):
            acc = acc + land[dev]
        o_ref[...] = acc

    return pl.pallas_call(
        body, out_shape=SDS((rows, LANES), F32),
        scratch_shapes=[pltpu.VMEM((n_dev, rows, LANES), F32), pltpu.SemaphoreType.DMA((n_dev - 1,)),
                        pltpu.SemaphoreType.DMA((n_dev - 1,))],
        name="all_reduce_small", compiler_params=_params())(pack)


def _add_own_half(name, full, recv, out_dtype):
    n4, _, h, cols = full.shape

    def body(s_ref, a_ref, b_ref, o_ref, own_ref):
        v = (a_ref[...] + b_ref[...]).astype(out_dtype)
        o_ref[...] = v

        @pl.when(pl.program_id(0) == s_ref[1])
        def _():
            own_ref[...] = v

    return pl.pallas_call(
        body,
        grid_spec=pltpu.PrefetchScalarGridSpec(
            num_scalar_prefetch=1, grid=(n4,),
            in_specs=[pl.BlockSpec((None, None, h, cols), lambda q, s: (q, s[0], 0, 0)),
                      pl.BlockSpec((None, h, cols), lambda q, s: (q, 0, 0))],
            out_specs=[pl.BlockSpec((None, h, cols), lambda q, s: (q, 0, 0)),
                       pl.BlockSpec((None, h, cols), lambda q, s: (s[1], 0, 0))]),
        out_shape=[SDS((n4, h, cols), out_dtype)] * 2, name=name, compiler_params=_params())(_mesh_scalars(), full, recv)


def _sum_chips(name, parts):
    n4, h, cols = parts.shape
    th = h // 4 if h % 64 == 0 else h

    def body(s_ref, a_ref, o_ref):
        acc = a_ref[0].astype(F32)
        for q in range(1, n4):
            acc = acc + a_ref[q].astype(F32)
        o_ref[...] = acc

    return pl.pallas_call(
        body,
        grid_spec=pltpu.PrefetchScalarGridSpec(
            num_scalar_prefetch=1, grid=(h // th,),
            in_specs=[pl.BlockSpec((n4, th, cols), lambda i, s: (0, i, 0))],
            out_specs=pl.BlockSpec((None, th, cols), lambda i, s: (s[0], i, 0))),
        out_shape=SDS((2, h, cols), F32), name=name, compiler_params=_params())(_mesh_scalars(), parts)


def _adamw_math(w, g, m, v):
    m2 = ADAM_B1 * m + (1.0 - ADAM_B1) * g
    v2 = ADAM_B2 * v + (1.0 - ADAM_B2) * (g * g)
    m_hat = m2 / (1.0 - ADAM_B1 ** ADAM_STEP)
    v_hat = v2 / (1.0 - ADAM_B2 ** ADAM_STEP)
    delta = -ADAM_LR * (m_hat / (jnp.sqrt(v_hat) + ADAM_EPS) + ADAM_WD * w)
    return delta, m2, v2


def _row_tile(rows, cols):
    cap = max(8, (1 << 18) // cols)
    best = 8
    for cand in range(8, min(rows, cap) + 1, 8):
        if rows % cand == 0:
            best = cand
    return best


def _adamw_big(name, w, g_layers, m, v):
    layers, rows, cols = w.shape
    tr = _row_tile(rows, cols)

    def body(w_ref, m_ref, v_ref, *rest):
        g_refs, (g_o, d_o, m_o, v_o) = rest[:layers], rest[layers:]
        gv = g_refs[0][...]
        for layer in range(1, layers):
            gv = jnp.where(pl.program_id(0) == layer, g_refs[layer][...], gv)
        d, mm, vv = _adamw_math(w_ref[...], gv, m_ref[...], v_ref[...])
        g_o[...] = gv
        d_o[...] = d
        m_o[...] = mm
        v_o[...] = vv

    blk = pl.BlockSpec((None, tr, cols), lambda l, i: (l, i, 0))
    g_blk = pl.BlockSpec((tr, cols), lambda l, i: (i, 0))
    return tuple(pl.pallas_call(
        body, grid=(layers, rows // tr), in_specs=[blk] * 3 + [g_blk] * layers, out_specs=[blk] * 4,
        out_shape=[SDS((layers, rows, cols), F32)] * 4, name=name,
        compiler_params=_params())(w, m, v, *[g.reshape(rows, cols) for g in g_layers]))


def _adamw_small(ws, gs, ms, vs):
    n = len(ws)
    flat = []
    for group in (ws, gs, ms, vs):
        flat += [a.reshape(-1, a.shape[-1]) for a in group]

    def body(*refs):
        w_r, g_r, m_r, v_r = refs[:n], refs[n:2 * n], refs[2 * n:3 * n], refs[3 * n:4 * n]
        d_o, m_o, v_o = refs[4 * n:5 * n], refs[5 * n:6 * n], refs[6 * n:7 * n]
        for j in range(n):
            d, mm, vv = _adamw_math(w_r[j][...], g_r[j][...], m_r[j][...], v_r[j][...])
            d_o[j][...] = d
            m_o[j][...] = mm
            v_o[j][...] = vv

    shapes = [SDS(a.shape, F32) for a in flat[:n]]
    outs = pl.pallas_call(body, out_shape=shapes * 3, name="adamw_small", compiler_params=_params())(*flat)
    res = []
    for k in range(3):
        res.append([outs[k * n + j].reshape(ws[j].shape) for j in range(n)])
    return res


BIG = ("ab_w_in", "ab_w_out", "cd_w_in", "cd_w_out", "ffn_w_gate", "ffn_w_up", "ffn_w_down")
V_BLOCK = (2 * A_WIDTH + QK_COLS) // B_WIDTH


def _pad_rows(a, rows):
    return jnp.pad(a, ((0, rows - a.shape[0]), (0, 0)))


A_IN, A_OUT, C_IN, C_OUT = ("ab_w_in", 0), ("ab_w_out", 0), ("cd_w_in", 0), ("cd_w_out", 0)
G0, U0, D0 = ("ffn_w_gate", 0), ("ffn_w_up", 0), ("ffn_w_down", 0)
G1, U1, D1 = ("ffn_w_gate", 1), ("ffn_w_up", 1), ("ffn_w_down", 1)
UNITS = (A_IN, A_OUT, G0, U0, D0, C_IN, C_OUT, G1, U1, D1)
ROWS_MINOR = ("ffn_w_gate", "ffn_w_up")
SMALL_SHARDED = ("small", 0)
REPLICATED_UNIT = ("replicated", 0)


class _Exchange:
    def __init__(self, enabled):
        self.enabled = enabled
        self.w, self.grad, self.recv, self.half, self.land, self.done = {}, {}, {}, {}, {}, {}

    def full(self, unit):
        b = self.w[unit]
        return b.reshape(N_CHIPS, 1, 2 * b.shape[2], b.shape[3])

    def _ride(self, phases):
        rides, sinks = [], []
        for kind, units in phases:
            if kind == "send":
                rides.append(_ride_gather_send([self.w[u] for u in units]))
                sinks.append(self.w)
            elif kind == "pass":
                rides.append(_ride_gather_pass([self.w[u] for u in units]))
                sinks.append(self.w)
            elif kind == "swap":
                rides.append(_ride_swap([self.grad[u] for u in units]))
                sinks.append(self.recv)
            elif kind == "scatter":
                rides.append(_ride_scatter([self.half[u] for u in units], [self.land[u] for u in units]))
                sinks.append(self.land)
            else:
                rides.append(_ride_join([self.done[u] for u in units]))
                sinks.append(self.done)
        ride = functools.reduce(_ride_both, rides)

        def settle(res):
            n_bufs = sum(len(r.bufs) for r in rides)
            bufs, new = list(res[:n_bufs]), list(res[n_bufs:])
            for r, sink, (_, units) in zip(rides, sinks, phases):
                vals = [bufs.pop(0) for _ in r.bufs] + [new.pop(0) for _ in r.new_outs]
                for u, v in zip(units, vals):
                    sink[u] = v

        return ride, settle

    def run(self, fn, *args, phases=(), **kw):
        if not self.enabled or not phases:
            return fn(*args, **kw)
        ride, settle = self._ride(phases)
        out, res = fn(*args, ride=ride, **kw)
        settle(res)
        return out

    def alone(self, name, phases):
        if self.enabled:
            ride, settle = self._ride(phases)
            settle(_run_ride(name, ride))

    def pair_sum(self, units):
        if self.enabled:
            for u in units:
                dtype = F32 if u in (SMALL_SHARDED, REPLICATED_UNIT) else BF16
                self.half[u], self.land[u] = _add_own_half(f"pair_sum_{u[0]}_{u[1]}", self.grad[u], self.recv[u], dtype)

    def chip_sum(self, units):
        if self.enabled:
            for u in units:
                self.done[u] = _sum_chips(f"chip_sum_{u[0]}_{u[1]}", self.land[u])


def _local_step(x, target, ex, sp):
    t, d = x.shape
    tabs = _rope_tables(t)
    gains = jnp.concatenate([jnp.tile(sp["q_norm_g"][g], HEAD_DIM // 8) for g in range(N_DIL)]
                            + [jnp.tile(sp["k_norm_g"][g], HEAD_DIM // 8) for g in range(N_DIL)]).reshape(1, QK_COLS)
    bias_t = sp["sgu_bias"].T
    cw = _pad_rows(sp["conv_c_w"], 32)
    dw = _pad_rows(sp["conv_d_w"], 8)
    cb, clg, clb = (sp[k].reshape(1, C_WIDTH) for k in ("conv_c_b", "c_ln_g", "c_ln_b"))
    slg, slb = sp["sgu_norm_g"].reshape(1, A_WIDTH), sp["sgu_norm_b"].reshape(1, A_WIDTH)
    g_ab, g_cd = sp["ab_norm_g"].reshape(1, d), sp["cd_norm_g"].reshape(1, d)
    g_f0, g_f1 = sp["ffn_norm_g"][0:1], sp["ffn_norm_g"][1:2]
    run = ex.run

    def w2d(unit):
        return ex.full(unit).reshape(-1, d)

    h0 = _rms_fwd("rms_ab", x, g_ab)
    proj = run(_proj_in, "proj_ab", h0, ex.full(A_IN), 0, phases=[("send", [A_OUT, G0])])
    a_out = _mixer_a_fwd(proj, slg, slb, sp["sgu_w"], bias_t)
    qk, q1, q2, k1, k2 = run(_qk_fwd, proj, gains, tabs, phases=[("pass", [A_OUT, G0]), ("send", [U0])])
    regrouped_qk = {1: (q1, k1), 2: (q2, k2)}
    fwd_phases = ([("pass", [U0]), ("send", [D0])], [("pass", [D0]), ("send", [C_IN])],
                  [("pass", [C_IN]), ("send", [C_OUT, G1])])
    qkv, o_list, l_list = [], [], []
    for g, rate in enumerate(DIL_RATES):
        if rate == 1:
            qk3, proj3 = qk.reshape(1, t, QK_COLS), proj.reshape(1, t, AB_IN)
            q, k, v = (qk3, g), (qk3, N_DIL + g), (proj3, V_BLOCK + g)
        else:
            vp, = _permute(f"regroup_v_{g}", [(proj, V_BLOCK + g)], rate)
            q, k, v = (regrouped_qk[g][0], 0), (regrouped_qk[g][1], 0), (vp, 0)
        qkv.append((q, k, v))
        o, l = run(_attn_fwd, f"attn_fwd_{g}", q, k, v, phases=fwd_phases[g])
        if rate == 1:
            o, l = o.reshape(t, B_WIDTH), l.reshape(t, B_WIDTH)
        o_list.append(o)
        l_list.append(l)
    cat, lse_tot, lse_1, lse_2 = _attn_merge(a_out, o_list, l_list)
    x1, hf0 = _proj_out("out_ab", cat, w2d(A_OUT), x, g_next=g_f0)
    gate0, up0, act0 = run(_ffn_in, "ffn_in_0", hf0, ex.full(G0), ex.full(U0), 0,
                           phases=[("pass", [C_OUT, G1]), ("send", [U1])])
    x2, h1 = run(_ffn_out, "ffn_out_0", act0, ex.full(D0), 0, x1, g_next=g_cd, phases=[("pass", [U1]), ("send", [D1])])
    projcd = run(_proj_in, "proj_cd", h1, ex.full(C_IN), 0, phases=[("pass", [D1])])
    cat2, c1 = _mixer_cd_fwd(projcd, cw, cb, clg, clb, dw)
    x3, hf1 = _proj_out("out_cd", cat2, w2d(C_OUT), x2, g_next=g_f1)
    gate1, up1, act1 = _ffn_in("ffn_in_1", hf1, ex.full(G1), ex.full(U1), 0)
    dy, loss_acc, dy_b = _ffn_out("ffn_out_1", act1, ex.full(D1), 0, x3, target=target)
    loss = 0.5 * loss_acc[0, 0] / d

    late = [D1, G1, U1]
    dgate, dup = _ffn_dact("ffn_dact_1", dy_b, ex.full(D1), 0, gate1, up1)
    ex.grad[D1] = _wgrad_row_sharded("wgrad_down_1", act1, dy_b, True)
    ex.grad[G1], ex.grad[U1] = _wgrad_row_sharded("wgrad_gate_up_1", [dgate, dup], hf1, True)
    g3, d_f1, g3_b = run(_dgrad_cols, "dgrad_ffn_1", [dgate, dup], [ex.full(G1), ex.full(U1)], 0, True, x3, g_f1, dy,
                         w_rows=True, phases=[("swap", late)])
    ex.pair_sum(late)

    dcat2 = _dgrad_rows("dgrad_out_cd", g3_b, w2d(C_OUT))
    ex.grad[C_OUT] = _wgrad_row_sharded("wgrad_out_cd", cat2, g3_b, False)
    dprojcd, d_cw, d_cb, d_clg, d_clb, d_dw = run(_mixer_cd_bwd, projcd, dcat2, c1, cw, clg, clb, dw, phases=[("scatter", late)])
    ex.chip_sum(late)
    ex.grad[C_IN] = run(_wgrad_col_sharded, "wgrad_in_cd", h1, [dprojcd], False, phases=[("join", late)])[0]
    g2, d_cdn, g2_b = run(_dgrad_cols, "dgrad_in_cd", [dprojcd], [ex.full(C_IN)], 0, False, x2, g_cd, g3,
                          phases=[("swap", [C_OUT, C_IN])])
    ex.pair_sum([C_OUT, C_IN])

    dgate, dup = run(_ffn_dact, "ffn_dact_0", g2_b, ex.full(D0), 0, gate0, up0, phases=[("scatter", [C_OUT, C_IN])])
    ex.chip_sum([C_OUT, C_IN])
    ex.grad[D0] = run(_wgrad_row_sharded, "wgrad_down_0", act0, g2_b, True, phases=[("join", [C_OUT, C_IN])])
    ex.grad[G0], ex.grad[U0] = _wgrad_row_sharded("wgrad_gate_up_0", [dgate, dup], hf0, True)
    small = {"cd_norm_g": d_cdn, "conv_c_w": d_cw[:C_KERNEL], "conv_c_b": d_cb, "c_ln_g": d_clg, "c_ln_b": d_clb,
             "conv_d_w": d_dw[:D_KERNEL]}
    ex.grad[SMALL_SHARDED] = _split_full_small(small).reshape(N_CHIPS, 2, SHARDED_ROWS // 2, LANES)
    mid = [D0, G0, U0, SMALL_SHARDED]
    g1, d_f0, g1_b = run(_dgrad_cols, "dgrad_ffn_0", [dgate, dup], [ex.full(G0), ex.full(U0)], 0, True, x1, g_f0, g2,
                         w_rows=True, phases=[("swap", mid)])
    ex.pair_sum(mid)

    dcat = _dgrad_rows("dgrad_out_ab", g1_b, w2d(A_OUT))
    ex.grad[A_OUT] = _wgrad_row_sharded("wgrad_out_ab", cat, g1_b, False)
    d_a, d_sw, d_sbt, d_slg, d_slb = _mixer_a_bwd(proj, dcat, slg, slb, sp["sgu_w"], bias_t)
    early = {"sgu_norm_g": d_slg, "sgu_norm_b": d_slb, "sgu_w": d_sw, "sgu_bias": d_sbt.T}
    ex.grad[REPLICATED_UNIT] = jnp.broadcast_to(
        _pack_replicated(early, REPLICATED_EARLY, REPLICATED_EARLY_ROWS).reshape(2, REPLICATED_EARLY_ROWS // 2, LANES),
        (N_CHIPS, 2, REPLICATED_EARLY_ROWS // 2, LANES))
    last = [A_OUT, REPLICATED_UNIT]
    dbb, dd, db_1, dd_1, db_2, dd_2 = _attn_bwd_prep(dcat, cat)
    regrouped_bwd = {1: (db_1, lse_1, dd_1), 2: (db_2, lse_2, dd_2)}
    bwd_phases = ([("scatter", [D0, G0])], [("scatter", [U0, SMALL_SHARDED]), ("join", [D0, G0]), ("swap", last)],
                  [("join", [U0, SMALL_SHARDED]), ("scatter", last)])
    dqs, dks, dvs = [], [], []
    for g, rate in enumerate(DIL_RATES):
        q, k, v = qkv[g]
        if rate == 1:
            db3, l3, dd3 = (a.reshape(1, t, B_WIDTH) for a in (dbb, lse_tot, dd))
        else:
            db3, l3, dd3 = regrouped_bwd[g]
        if g == 1:
            ex.chip_sum([D0, G0])
        elif g == 2:
            ex.chip_sum([U0, SMALL_SHARDED])
            ex.pair_sum(last)
        dq, dk, dv = run(_attn_bwd, f"attn_bwd_{g}", q, k, v, db3, l3, dd3, phases=bwd_phases[g])
        if g == 2:
            ex.chip_sum(last)
        if rate == 1:
            dq, dk, dv = (a.reshape(t, B_WIDTH) for a in (dq, dk, dv))
        dqs.append(dq)
        dks.append(dk)
        dvs.append(dv)
    dproj, d_gains = _dproj_assemble(proj, d_a, dqs, dks, dvs, gains, tabs)
    d_gains = _fold_heads(d_gains)[0].reshape(2, N_DIL, B_WIDTH)[:, :, :HEAD_DIM]
    ex.grad[A_IN] = run(_wgrad_col_sharded, "wgrad_in_ab", h0, [dproj], False, phases=[("join", last)])[0]
    ex.alone("swap_last", [("swap", [A_IN])])
    ex.pair_sum([A_IN])
    gx, d_abn = run(_dgrad_cols, "dgrad_in_ab", [dproj], [ex.full(A_IN)], 0, False, x, g_ab, g1, bf16_copy=False,
                    phases=[("scatter", [A_IN])])
    ex.chip_sum([A_IN])
    ex.alone("join_last", [("join", [A_IN])])

    small.update({
        "ab_norm_g": d_abn, "sgu_norm_g": d_slg, "sgu_norm_b": d_slb, "sgu_w": d_sw, "sgu_bias": d_sbt.T,
        "q_norm_g": d_gains[0], "k_norm_g": d_gains[1], "ffn_norm_g": jnp.concatenate([d_f0, d_f1], axis=0),
    })
    return loss, gx, small


SHARDED_SMALL = ("cd_norm_g", "conv_c_w", "conv_c_b", "c_ln_g", "c_ln_b", "conv_d_w")
SHARDED_ROWS = 48
REPLICATED_EARLY = ("sgu_norm_g", "sgu_norm_b", "sgu_w", "sgu_bias")
REPLICATED_EARLY_ROWS = 528
REPLICATED_LATE = ("ab_norm_g", "q_norm_g", "k_norm_g", "ffn_norm_g", "loss")
REPLICATED_LATE_ROWS = 32
REPLICATED_SMALL = REPLICATED_EARLY + REPLICATED_LATE[:-1]


def _pack_sharded(parts):
    rows = [parts[k].reshape(-1, LANES) for k in SHARDED_SMALL]
    return _pad_rows(jnp.concatenate(rows, axis=0), SHARDED_ROWS)


def _split_full_small(small):
    per_chip = []
    for q in range(N_CHIPS):
        parts = {}
        for k in SHARDED_SMALL:
            a = small[k]
            a = a.reshape(-1, a.shape[-1])
            n = a.shape[-1] // N_CHIPS
            parts[k] = a[:, q * n:(q + 1) * n]
        per_chip.append(_pack_sharded(parts))
    return jnp.stack(per_chip)


def _unpack_sharded(pack, shapes):
    out, r = {}, 0
    for k in SHARDED_SMALL:
        n = math.prod(shapes[k]) // LANES
        out[k] = pack[r:r + n].reshape(shapes[k])
        r += n
    return out


def _gathered_small(packs, shapes):
    per_chip = [_unpack_sharded(packs[q], shapes) for q in range(N_CHIPS)]
    return {k: jnp.concatenate([pc[k] for pc in per_chip], axis=-1) for k in SHARDED_SMALL}


def _pack_replicated(small, names, total_rows):
    rows = []
    for k in names:
        a = small[k].reshape(-1)
        a = jnp.pad(a, (0, (-a.shape[0]) % LANES))
        rows.append(a.reshape(-1, LANES))
    return _pad_rows(jnp.concatenate(rows, axis=0), total_rows)


def _unpack_replicated(pack, shapes, names):
    out, r = {}, 0
    for k in names:
        size = math.prod(shapes[k])
        n = -(-size // LANES)
        out[k] = pack[r:r + n].reshape(-1)[:size].reshape(shapes[k])
        r += n
    return out


WEIGHT_ORDER = ("ab_norm_g", "ab_w_in", "sgu_norm_g", "sgu_norm_b", "sgu_w", "sgu_bias", "q_norm_g", "k_norm_g", "ab_w_out",
                "cd_norm_g", "cd_w_in", "conv_c_w", "conv_c_b", "c_ln_g", "c_ln_b", "conv_d_w", "cd_w_out", "ffn_norm_g",
                "ffn_w_gate", "ffn_w_up", "ffn_w_down")


def kernel(x, ab_norm_g, ab_w_in, sgu_norm_g, sgu_norm_b, sgu_w, sgu_bias, q_norm_g, k_norm_g, ab_w_out, cd_norm_g, cd_w_in, conv_c_w, conv_c_b, c_ln_g, c_ln_b, conv_d_w, cd_w_out, ffn_norm_g, ffn_w_gate, ffn_w_up, ffn_w_down, loss_target, m_ab_norm_g, m_ab_w_in, m_sgu_norm_g, m_sgu_norm_b, m_sgu_w, m_sgu_bias, m_q_norm_g, m_k_norm_g, m_ab_w_out, m_cd_norm_g, m_cd_w_in, m_conv_c_w, m_conv_c_b, m_c_ln_g, m_c_ln_b, m_conv_d_w, m_cd_w_out, m_ffn_norm_g, m_ffn_w_gate, m_ffn_w_up, m_ffn_w_down, v_ab_norm_g, v_ab_w_in, v_sgu_norm_g, v_sgu_norm_b, v_sgu_w, v_sgu_bias, v_q_norm_g, v_k_norm_g, v_ab_w_out, v_cd_norm_g, v_cd_w_in, v_conv_c_w, v_conv_c_b, v_c_ln_g, v_c_ln_b, v_conv_d_w, v_cd_w_out, v_ffn_norm_g, v_ffn_w_gate, v_ffn_w_up, v_ffn_w_down):
    args = dict(locals())
    ws = {k: args[k] for k in WEIGHT_ORDER}
    ms = {k: args["m_" + k] for k in WEIGHT_ORDER}
    vs = {k: args["v_" + k] for k in WEIGHT_ORDER}
    small_names = [k for k in WEIGHT_ORDER if k not in BIG]
    t, d = x.shape[1:]

    for group in (ws, ms, vs):
        for k in ROWS_MINOR:
            group[k] = jnp.swapaxes(group[k], 1, 2)
    ex = _Exchange(enabled=True)
    for name, layer in UNITS:
        ex.w[(name, layer)] = _stage_own(f"stage_{name}_{layer}", ws[name], layer, BF16)
    own_small = _pack_sharded({k: ws[k][0] for k in SHARDED_SMALL})
    ex.w[SMALL_SHARDED] = _stage_own("stage_small", own_small[None], 0, F32)
    ex.alone("gather_first", [("send", [A_IN, SMALL_SHARDED])])
    ex.alone("gather_first_pass", [("pass", [A_IN, SMALL_SHARDED])])
    sp = _gathered_small(ex.w[SMALL_SHARDED].reshape(N_CHIPS, SHARDED_ROWS, LANES), {k: ws[k].shape[1:] for k in SHARDED_SMALL})
    for k in REPLICATED_SMALL:
        sp[k] = ws[k] if k == "ffn_norm_g" else ws[k][0]

    loss, grad_x, g_small = _local_step(x.reshape(t, d), loss_target.reshape(t, d), ex, sp)

    shapes = {k: ws[k].shape for k in REPLICATED_SMALL}
    shapes["loss"] = (1,)
    g_small["loss"] = loss
    late = _all_reduce_small(_pack_replicated(g_small, REPLICATED_LATE, REPLICATED_LATE_ROWS))
    grad = _unpack_sharded(ex.done[SMALL_SHARDED].reshape(SHARDED_ROWS, LANES), {k: ws[k].shape for k in SHARDED_SMALL})
    grad.update(_unpack_replicated(ex.done[REPLICATED_UNIT].reshape(REPLICATED_EARLY_ROWS, LANES), shapes, REPLICATED_EARLY))
    grad.update(_unpack_replicated(late, shapes, REPLICATED_LATE))
    loss = grad.pop("loss")[0]

    delta, new_m, new_v = {}, {}, {}
    for k in BIG:
        g_layers = [ex.done[(k, layer)] for layer in range(ws[k].shape[0])]
        outs = _adamw_big("adamw_" + k, ws[k], g_layers, ms[k], vs[k])
        if k in ROWS_MINOR:
            outs = [jnp.swapaxes(o, 1, 2) for o in outs]
        grad[k], delta[k], new_m[k], new_v[k] = outs
    d_s, m_s, v_s = _adamw_small([ws[k] for k in small_names], [grad[k] for k in small_names],
                                 [ms[k] for k in small_names], [vs[k] for k in small_names])
    for j, k in enumerate(small_names):
        delta[k], new_m[k], new_v[k] = d_s[j], m_s[j], v_s[j]

    return (loss, grad_x[None], *[grad[k] for k in WEIGHT_ORDER], *[delta[k] for k in WEIGHT_ORDER],
            *[new_m[k] for k in WEIGHT_ORDER], *[new_v[k] for k in WEIGHT_ORDER])
```

```python
import functools
import math

import jax
import jax.numpy as jnp
from jax import lax
from jax.experimental import pallas as pl
from jax.experimental.pallas import tpu as pltpu

F32 = jnp.float32
BF16 = jnp.bfloat16
SDS = jax.ShapeDtypeStruct

N_CHIPS = 4
EPS = 1e-6
NEG_INF = -1e30
CHUNK = 128
A_GROUPS = 4
A_WIDTH = 512
N_DIL = 3
DIL_RATES = (1, 4, 16)
HEAD_DIM = 64
B_WIDTH = 512
ROPE_DIM = 16
ROPE_THETA = 500000.0
C_WIDTH = 512
C_KERNEL = 31
D_KERNEL = 3
HALO = 32
ATT_BLOCK = 128
LANES = 128

ADAM_LR = 0.001
ADAM_B1 = 0.9
ADAM_B2 = 0.999
ADAM_EPS = 1e-08
ADAM_WD = 0.01
ADAM_STEP = 10

VMEM_LIMIT = 56 * 1024 * 1024

NN = (((1,), (0,)), ((), ()))
NT = (((1,), (1,)), ((), ()))
TN = (((0,), (0,)), ((), ()))

TILES = {"proj_in": 1024, "proj_out": 1024, "ffn_in": 1024, "ffn_out": 512, "ffn_dact": 512, "dgrad_cols": 512,
         "dgrad_rows": 1024, "wgrad": 2048}


def _params(sem=None):
    return pltpu.CompilerParams(dimension_semantics=sem, vmem_limit_bytes=VMEM_LIMIT)


def _bf(v):
    return v if v.dtype == BF16 else v.astype(BF16)


def _dot(a, b, dims):
    return lax.dot_general(_bf(a), _bf(b), dims, preferred_element_type=F32)


def _dot_hi(a, b):
    return jnp.dot(a, b, precision=lax.Precision.HIGHEST, preferred_element_type=F32)


def _sigmoid(v):
    return 0.5 * jnp.tanh(0.5 * v) + 0.5


def _gelu(v):
    return 0.5 * v * (1.0 + lax.erf(v * (1.0 / math.sqrt(2.0))))


def _gelu_grad(v):
    cdf = 0.5 * (1.0 + lax.erf(v * (1.0 / math.sqrt(2.0))))
    return cdf + v * jnp.exp(-0.5 * v * v) * (1.0 / math.sqrt(2.0 * math.pi))


def _segment_mean_matrix(seg, scale=None):
    r = lax.broadcasted_iota(jnp.int32, (LANES, LANES), 0) // seg
    c = lax.broadcasted_iota(jnp.int32, (LANES, LANES), 1) // seg
    return jnp.where(r == c, (1.0 / seg) if scale is None else scale, 0.0).astype(BF16)


def _segment_dot(v, seg):
    hi = v.astype(BF16)
    lo = (v - hi.astype(F32)).astype(BF16)
    return jnp.dot(hi, seg, preferred_element_type=F32) + jnp.dot(lo, seg, preferred_element_type=F32)


MESH = pl.DeviceIdType.MESH
ANY = pl.BlockSpec(memory_space=pl.ANY)


def _position():
    x, y, c = lax.axis_index("x"), lax.axis_index("y"), lax.axis_index("c")
    others = [(1 - x, y), (x, 1 - y), (1 - x, 1 - y)]
    return x, y, c, 2 * x + y, others


class _Ride:
    def __init__(self, ins, bufs, new_outs, sem_shapes, start, finish):
        self.ins, self.bufs, self.new_outs, self.sem_shapes = list(ins), list(bufs), list(new_outs), list(sem_shapes)
        self.start, self.finish = start, finish


def _ride_both(a, b):
    na = (len(a.ins), len(a.bufs), len(a.new_outs), len(a.sem_shapes))

    def split(ins, bufs, new, sems):
        return ((ins[:na[0]], bufs[:na[1]], new[:na[2]], sems[:na[3]]), (ins[na[0]:], bufs[na[1]:], new[na[2]:], sems[na[3]:]))

    def start(*refs):
        ra, rb = split(*refs)
        a.start(*ra)
        b.start(*rb)

    def finish(*refs):
        ra, rb = split(*refs)
        a.finish(*ra)
        b.finish(*rb)

    return _Ride(a.ins + b.ins, a.bufs + b.bufs, a.new_outs + b.new_outs, a.sem_shapes + b.sem_shapes, start, finish)


def _call(body, *, grid, in_specs, out_specs, out_shape, operands, name, scratch_shapes=(), aliases=None, ride=None):
    if ride is None:
        return pl.pallas_call(body, grid=grid, in_specs=in_specs, out_specs=out_specs, out_shape=out_shape,
                              scratch_shapes=list(scratch_shapes), input_output_aliases=aliases or {}, name=name,
                              compiler_params=_params())(*operands)
    multi = isinstance(out_shape, (list, tuple))
    out_shapes = list(out_shape) if multi else [out_shape]
    o_specs = list(out_specs) if multi else [out_specs]
    n_in, n_out, n_scr = len(operands), len(out_shapes), len(scratch_shapes)
    n_ri, n_rb, n_rn = len(ride.ins), len(ride.bufs), len(ride.new_outs)

    def carrying(*refs):
        k = n_in
        r_ins = refs[k:k + n_ri]
        k += n_ri + n_rb
        outs = refs[k:k + n_out]
        k += n_out
        r_bufs = refs[k:k + n_rb]
        k += n_rb
        r_new = refs[k:k + n_rn]
        k += n_rn
        scratch = refs[k:k + n_scr]
        sems = refs[k + n_scr:]
        first, last = None, None
        for axis, size in enumerate(grid):
            pid = pl.program_id(axis)
            first = (pid == 0) if first is None else first & (pid == 0)
            last = (pid == size - 1) if last is None else last & (pid == size - 1)

        @pl.when(first)
        def _():
            ride.start(r_ins, r_bufs, r_new, sems)

        body(*refs[:n_in], *outs, *scratch)

        @pl.when(last)
        def _():
            ride.finish(r_ins, r_bufs, r_new, sems)

    all_aliases = dict(aliases or {})
    for j in range(n_rb):
        all_aliases[n_in + n_ri + j] = n_out + j
    res = pl.pallas_call(
        carrying, grid=grid, in_specs=list(in_specs) + [ANY] * (n_ri + n_rb), out_specs=o_specs + [ANY] * (n_rb + n_rn),
        out_shape=out_shapes + [SDS(b.shape, b.dtype) for b in ride.bufs] + ride.new_outs,
        scratch_shapes=list(scratch_shapes) + [pltpu.SemaphoreType.DMA(s) for s in ride.sem_shapes],
        input_output_aliases=all_aliases, name=name, compiler_params=_params())(*operands, *ride.ins, *ride.bufs)
    outs = res[:n_out]
    return (list(outs) if multi else outs[0]), list(res[n_out:])


def _run_ride(name, ride):
    n_ri, n_rb, n_rn = len(ride.ins), len(ride.bufs), len(ride.new_outs)

    def body(*refs):
        r_ins = refs[:n_ri]
        r_bufs = refs[n_ri + n_rb:n_ri + 2 * n_rb]
        r_new = refs[n_ri + 2 * n_rb:n_ri + 2 * n_rb + n_rn]
        sems = refs[n_ri + 2 * n_rb + n_rn:]
        ride.start(r_ins, r_bufs, r_new, sems)
        ride.finish(r_ins, r_bufs, r_new, sems)

    return list(pl.pallas_call(
        body, in_specs=[ANY] * (n_ri + n_rb), out_specs=[ANY] * (n_rb + n_rn),
        out_shape=[SDS(b.shape, b.dtype) for b in ride.bufs] + ride.new_outs,
        scratch_shapes=[pltpu.SemaphoreType.DMA(s) for s in ride.sem_shapes],
        input_output_aliases={n_ri + j: j for j in range(n_rb)}, name=name)(*ride.ins, *ride.bufs))


def _whole(ref, p):
    return ref[...]


def _slab(ref, p):
    return ref[p]


def _matmul(name, grid, pairs, extras, outs, dims, epi, *, slabs=1, n_acc=1, ride=None):
    n_pairs, n_ex, n_out = len(pairs), len(extras), len(outs)

    def body(*refs):
        ab = refs[:2 * n_pairs]
        ex = refs[2 * n_pairs:2 * n_pairs + n_ex]
        out_refs = refs[2 * n_pairs + n_ex:2 * n_pairs + n_ex + n_out]
        pids = tuple(pl.program_id(a) for a in range(len(grid)))
        parts = [None] * n_acc
        for p in range(slabs):
            for j, (_, _, a_pick, _, _, b_pick, acc) in enumerate(pairs):
                d = _dot(a_pick(ab[2 * j], p), b_pick(ab[2 * j + 1], p), dims)
                parts[acc] = d if parts[acc] is None else parts[acc] + d
        epi(parts, ex, out_refs, pids)

    operands, in_specs = [], []
    for a, a_spec, _, b, b_spec, _, _ in pairs:
        operands += [a, b]
        in_specs += [a_spec, b_spec]
    for e, e_spec in extras:
        operands.append(e)
        in_specs.append(e_spec)
    return _call(body, grid=grid, in_specs=in_specs, out_specs=[o[1] for o in outs], out_shape=[o[0] for o in outs],
                 operands=operands, name=name, ride=ride)


def _rms_rows(v, g):
    r = lax.rsqrt(jnp.mean(v * v, axis=-1, keepdims=True) + EPS)
    return v * r * g


def _rms_fwd(name, x, g):
    t, d = x.shape
    tm = 512

    def body(x_ref, g_ref, o_ref):
        o_ref[...] = _rms_rows(x_ref[...], g_ref[...]).astype(BF16)

    return pl.pallas_call(
        body, grid=(t // tm,),
        in_specs=[pl.BlockSpec((tm, d), lambda i: (i, 0)), pl.BlockSpec((1, d), lambda i: (0, 0))],
        out_specs=pl.BlockSpec((tm, d), lambda i: (i, 0)), out_shape=SDS((t, d), BF16), name=name,
        compiler_params=_params())(x, g)


def _epi_residual_norm(accs, ex, outs, pids):
    x_new = accs[0] + ex[0][...]
    outs[0][...] = x_new
    outs[1][...] = _rms_rows(x_new, ex[1][...]).astype(BF16)


def _epi_residual_loss(accs, ex, outs, pids):
    y = accs[0] + ex[0][...]
    err = y - ex[1][...]
    dy = err * (1.0 / err.shape[-1])
    outs[0][...] = dy
    outs[2][...] = dy.astype(BF16)

    @pl.when(pids[0] == 0)
    def _():
        outs[1][...] = jnp.zeros_like(outs[1])

    outs[1][...] += jnp.sum(err * err)


def _epi_rms_bwd(accs, ex, outs, pids):
    dh = accs[0]
    xv, g, res = ex[0][...], ex[1][...], ex[2][...]
    r = lax.rsqrt(jnp.mean(xv * xv, axis=-1, keepdims=True) + EPS)
    xh = xv * r
    dy = dh * g
    dx = res + r * (dy - xh * jnp.mean(dy * xh, axis=-1, keepdims=True))
    outs[0][...] = dx
    if len(outs) > 2:
        outs[2][...] = dx.astype(BF16)

    @pl.when(pids[0] == 0)
    def _():
        outs[1][...] = jnp.zeros_like(outs[1])

    outs[1][...] += jnp.sum(dh * xh, axis=0, keepdims=True)


def _row_spec(tm, d):
    return pl.BlockSpec((tm, d), lambda i, *_: (i, 0))


def _const_spec(shape):
    nd = len(shape)
    return pl.BlockSpec(shape, lambda *_: (0,) * nd)


def _proj_in(name, h, w, layer, ride=None):
    t, d = h.shape
    n4 = w.shape[-1]
    tm = TILES["proj_in"]

    def epi(accs, ex, outs, pids):
        outs[0][...] = accs[0].astype(BF16)

    res = _matmul(
        name, (N_CHIPS, t // tm),
        [(h, pl.BlockSpec((tm, d), lambda p, i: (i, 0)), _whole,
          w, pl.BlockSpec((None, None, d, n4), lambda p, i: (p, layer, 0, 0)), _whole, 0)],
        [], [(SDS((t, N_CHIPS * n4), BF16), pl.BlockSpec((tm, n4), lambda p, i: (i, p)))],
        NN, epi, ride=ride)
    return res[0] if ride is None else (res[0][0], res[1])


def _proj_out(name, a, w, x, g_next=None, target=None):
    t, k = a.shape
    d = w.shape[-1]
    tm = TILES["proj_out"]
    if target is None:
        extras = [(x, _row_spec(tm, d)), (g_next, _const_spec((1, d)))]
        outs = [(SDS((t, d), F32), _row_spec(tm, d)), (SDS((t, d), BF16), _row_spec(tm, d))]
        epi = _epi_residual_norm
    else:
        extras = [(x, _row_spec(tm, d)), (target, _row_spec(tm, d))]
        outs = [(SDS((t, d), F32), _row_spec(tm, d)), (SDS((8, LANES), F32), _const_spec((8, LANES))),
                (SDS((t, d), BF16), _row_spec(tm, d))]
        epi = _epi_residual_loss
    return _matmul(name, (t // tm,), [(a, _row_spec(tm, k), _whole, w, _const_spec((k, d)), _whole, 0)], extras, outs, NN, epi)


def _ffn_in(name, h, wg, wu, layer, ride=None):
    t, d = h.shape
    n4 = wg.shape[-2]
    tm = TILES["ffn_in"]

    def epi(accs, ex, outs, pids):
        gate, up = accs
        s = _sigmoid(gate)
        silu = gate * s
        outs[0][...] = (up * (s + silu - silu * s)).astype(BF16)
        outs[1][...] = silu.astype(BF16)
        outs[2][...] = (silu * up).astype(BF16)

    w_spec = pl.BlockSpec((None, None, n4, d), lambda p, i: (p, layer, 0, 0))
    h_spec = pl.BlockSpec((tm, d), lambda p, i: (i, 0))
    o = (SDS((N_CHIPS, t, n4), BF16), pl.BlockSpec((None, tm, n4), lambda p, i: (p, i, 0)))
    return _matmul(name, (N_CHIPS, t // tm),
                   [(h, h_spec, _whole, wg, w_spec, _whole, 0), (h, h_spec, _whole, wu, w_spec, _whole, 1)], [],
                   [o, o, o], NT, epi, n_acc=2, ride=ride)


def _ffn_out(name, act, wd, layer, x, g_next=None, target=None, ride=None):
    _, t, n4 = act.shape
    d = wd.shape[-1]
    tm = TILES["ffn_out"]
    xs = _row_spec(tm, d)
    if target is None:
        extras = [(x, xs), (g_next, _const_spec((1, d)))]
        outs = [(SDS((t, d), F32), xs), (SDS((t, d), BF16), xs)]
        epi = _epi_residual_norm
    else:
        extras = [(x, xs), (target, xs)]
        outs = [(SDS((t, d), F32), xs), (SDS((8, LANES), F32), _const_spec((8, LANES))), (SDS((t, d), BF16), xs)]
        epi = _epi_residual_loss
    return _matmul(
        name, (t // tm,),
        [(act, pl.BlockSpec((N_CHIPS, tm, n4), lambda i: (0, i, 0)), _slab,
          wd, pl.BlockSpec((N_CHIPS, None, n4, d), lambda i: (0, layer, 0, 0)), _slab, 0)],
        extras, outs, NN, epi, slabs=N_CHIPS, ride=ride)


def _ffn_dact(name, g, wd, layer, gate, up, ride=None):
    t, d = g.shape
    n4 = wd.shape[-2]
    tm = TILES["ffn_dact"]

    def epi(accs, ex, outs, pids):
        dact = accs[0]
        outs[0][...] = (dact * ex[0][...].astype(F32)).astype(BF16)
        outs[1][...] = (dact * ex[1][...].astype(F32)).astype(BF16)

    blk = pl.BlockSpec((None, tm, n4), lambda p, i: (p, i, 0))
    o = (SDS((N_CHIPS, t, n4), BF16), blk)
    return _matmul(
        name, (N_CHIPS, t // tm),
        [(g, pl.BlockSpec((tm, d), lambda p, i: (i, 0)), _whole,
          wd, pl.BlockSpec((None, None, n4, d), lambda p, i: (p, layer, 0, 0)), _whole, 0)],
        [(gate, blk), (up, blk)], [o, o], NT, epi, ride=ride)


def _copy_epi(accs, ex, outs, pids):
    for a, o in zip(accs, outs):
        o[...] = a.astype(o.dtype)


def _dgrad_cols(name, dz_list, w_list, layer, three_d, x, g, res, bf16_copy=True, w_rows=False, ride=None):
    t, d = x.shape
    n4 = w_list[0].shape[-2 if w_rows else -1]
    tm = TILES["dgrad_cols"]
    if three_d:
        zs, z_pick = pl.BlockSpec((N_CHIPS, tm, n4), lambda i: (0, i, 0)), _slab
    else:
        zs, z_pick = _row_spec(tm, N_CHIPS * n4), (lambda ref, p: ref[:, p * n4:(p + 1) * n4])
    ws = pl.BlockSpec((N_CHIPS, None) + ((n4, d) if w_rows else (d, n4)), lambda i: (0, layer, 0, 0))
    xs = _row_spec(tm, d)
    return _matmul(
        name, (t // tm,), [(dz, zs, z_pick, w, ws, _slab, 0) for dz, w in zip(dz_list, w_list)],
        [(x, xs), (g, _const_spec((1, d))), (res, xs)],
        [(SDS((t, d), F32), xs), (SDS((1, d), F32), _const_spec((1, d)))] + ([(SDS((t, d), BF16), xs)] if bf16_copy else []),
        NN if w_rows else NT, _epi_rms_bwd, slabs=N_CHIPS, ride=ride)


def _dgrad_rows(name, g, w):
    t, d = g.shape
    k = w.shape[0]
    tm = TILES["dgrad_rows"]
    return _matmul(name, (t // tm,), [(g, _row_spec(tm, d), _whole, w, _const_spec((k, d)), _whole, 0)], [],
                   [(SDS((t, k), F32), _row_spec(tm, k))], NT, _copy_epi)[0]


A_TILE = 256


def _a_common(p_ref, lg_ref, lb_ref):
    pv = p_ref[...].astype(F32)
    a = _gelu(pv)
    u, v = a[:, :A_WIDTH], a[:, A_WIDTH:]
    vc = v - jnp.mean(v, axis=-1, keepdims=True)
    rs = lax.rsqrt(jnp.mean(vc * vc, axis=-1, keepdims=True) + EPS)
    vhat = vc * rs
    vn = vhat * lg_ref[...] + lb_ref[...]
    return pv, u, vhat, rs, vn.astype(BF16)


def _tril_weights(w_ref, g):
    r = lax.broadcasted_iota(jnp.int32, (CHUNK, CHUNK), 0)
    c = lax.broadcasted_iota(jnp.int32, (CHUNK, CHUNK), 1)
    return jnp.where(c <= r, w_ref[g], 0.0).astype(BF16), c <= r


def _mixer_a_fwd(proj, lg, lb, w, bias_t):
    t = proj.shape[0]

    def body(p_ref, lg_ref, lb_ref, w_ref, bt_ref, o_ref):
        _, u, _, _, vnb = _a_common(p_ref, lg_ref, lb_ref)
        for g in range(A_GROUPS):
            wt, _ = _tril_weights(w_ref, g)
            cs = slice(g * CHUNK, (g + 1) * CHUNK)
            for ch in range(A_TILE // CHUNK):
                rs_ = slice(ch * CHUNK, (ch + 1) * CHUNK)
                mixed = _dot(wt, vnb[rs_, cs], NN) + bt_ref[:, g:g + 1]
                o_ref[rs_, cs] = (u[rs_, cs] * mixed).astype(BF16)

    return pl.pallas_call(
        body, grid=(t // A_TILE,),
        in_specs=[pl.BlockSpec((A_TILE, 2 * A_WIDTH), lambda i: (i, 0)), _const_spec((1, A_WIDTH)),
                  _const_spec((1, A_WIDTH)), _const_spec((A_GROUPS, CHUNK, CHUNK)), _const_spec((CHUNK, A_GROUPS))],
        out_specs=pl.BlockSpec((A_TILE, A_WIDTH), lambda i: (i, 0)), out_shape=SDS((t, A_WIDTH), BF16),
        name="mixer_a_fwd", compiler_params=_params())(proj, lg, lb, w, bias_t)


def _mixer_a_bwd(proj, dcat, lg, lb, w, bias_t):
    t = proj.shape[0]

    def body(p_ref, da_ref, lg_ref, lb_ref, w_ref, bt_ref, dp_ref, dw_ref, dbt_ref, dlg_ref, dlb_ref, du_scr, dvn_scr):
        @pl.when(pl.program_id(0) == 0)
        def _():
            dw_ref[...] = jnp.zeros_like(dw_ref)
            dbt_ref[...] = jnp.zeros_like(dbt_ref)
            dlg_ref[...] = jnp.zeros_like(dlg_ref)
            dlb_ref[...] = jnp.zeros_like(dlb_ref)

        pv, u, vhat, rs, vnb = _a_common(p_ref, lg_ref, lb_ref)
        da = da_ref[...]
        for g in range(A_GROUPS):
            wt, keep = _tril_weights(w_ref, g)
            cs = slice(g * CHUNK, (g + 1) * CHUNK)
            for ch in range(A_TILE // CHUNK):
                rs_ = slice(ch * CHUNK, (ch + 1) * CHUNK)
                vg = vnb[rs_, cs]
                mixed = _dot(wt, vg, NN) + bt_ref[:, g:g + 1]
                du_scr[rs_, cs] = da[rs_, cs] * mixed
                dmx = da[rs_, cs] * u[rs_, cs]
                dw_ref[g] += jnp.where(keep, _dot(dmx, vg, NT), 0.0)
                dvn_scr[rs_, cs] = _dot(wt, dmx, TN)
                dbt_ref[:, g:g + 1] += jnp.sum(dmx, axis=1, keepdims=True)
        dvn = dvn_scr[...]
        dlg_ref[...] += jnp.sum(dvn * vhat, axis=0, keepdims=True)
        dlb_ref[...] += jnp.sum(dvn, axis=0, keepdims=True)
        dvh = dvn * lg_ref[...]
        dv = rs * (dvh - jnp.mean(dvh, axis=-1, keepdims=True) - vhat * jnp.mean(dvh * vhat, axis=-1, keepdims=True))
        gp = _gelu_grad(pv)
        dp_ref[:, :A_WIDTH] = (du_scr[...] * gp[:, :A_WIDTH]).astype(BF16)
        dp_ref[:, A_WIDTH:] = (dv * gp[:, A_WIDTH:]).astype(BF16)

    return pl.pallas_call(
        body, grid=(t // A_TILE,),
        in_specs=[pl.BlockSpec((A_TILE, 2 * A_WIDTH), lambda i: (i, 0)), pl.BlockSpec((A_TILE, A_WIDTH), lambda i: (i, 0)),
                  _const_spec((1, A_WIDTH)), _const_spec((1, A_WIDTH)), _const_spec((A_GROUPS, CHUNK, CHUNK)),
                  _const_spec((CHUNK, A_GROUPS))],
        out_specs=[pl.BlockSpec((A_TILE, 2 * A_WIDTH), lambda i: (i, 0)), _const_spec((A_GROUPS, CHUNK, CHUNK)),
                   _const_spec((CHUNK, A_GROUPS)), _const_spec((1, A_WIDTH)), _const_spec((1, A_WIDTH))],
        out_shape=[SDS((t, 2 * A_WIDTH), BF16), SDS((A_GROUPS, CHUNK, CHUNK), F32), SDS((CHUNK, A_GROUPS), F32),
                   SDS((1, A_WIDTH), F32), SDS((1, A_WIDTH), F32)],
        scratch_shapes=[pltpu.VMEM((A_TILE, A_WIDTH), F32), pltpu.VMEM((A_TILE, A_WIDTH), F32)],
        name="mixer_a_bwd", compiler_params=_params())(proj, dcat, lg, lb, w, bias_t)


def _rope_tables(t):
    half = ROPE_DIM // 2
    inv_freq = ROPE_THETA ** (-jnp.arange(half, dtype=F32) * 2.0 / ROPE_DIM)
    ang = jnp.arange(t, dtype=F32)[:, None] * inv_freq[None, :]
    cos, sin = jnp.cos(ang), jnp.sin(ang)
    one = jnp.ones((t, HEAD_DIM - ROPE_DIM), F32)
    zero = jnp.zeros((t, HEAD_DIM - ROPE_DIM), F32)
    zh = jnp.zeros((t, half), F32)
    c = jnp.concatenate([cos, cos, one], axis=1)
    s1 = jnp.concatenate([-sin, zh, zero], axis=1)
    s2 = jnp.concatenate([zh, sin, zero], axis=1)
    return tuple(jnp.tile(a, (1, LANES // HEAD_DIM)) for a in (c, s1, s2))


QK_TILE = 512
QK_COLS = 2 * N_DIL * B_WIDTH


CHUNKS = B_WIDTH // LANES


def _regroup_out(scr, first, out_ref, rate, tile):
    rows = tile // rate
    for rho in range(rate):
        for c in range(CHUNKS):
            out_ref[rho, :, c * LANES:(c + 1) * LANES] = scr[first + c, pl.ds(rho, rows, stride=rate), :].astype(out_ref.dtype)


def _regroup_in(x_ref, scr, rate, tile):
    rows = tile // rate
    for rho in range(rate):
        for c in range(CHUNKS):
            scr[c, pl.ds(rho, rows, stride=rate), :] = x_ref[rho, :, c * LANES:(c + 1) * LANES].astype(F32)


def _regrouped_spec(rate, tile):
    return pl.BlockSpec((rate, tile // rate, B_WIDTH), lambda i, *_: (0, i, 0))


def _qk_fwd(proj, gains, tabs, ride=None):
    t = proj.shape[0]
    col0 = 2 * A_WIDTH // 1024
    r1, r2 = DIL_RATES[1], DIL_RATES[2]

    def body(p_ref, g_ref, c_ref, s1_ref, s2_ref, o_ref, q1_ref, q2_ref, k1_ref, k2_ref, scr):
        seg = _segment_mean_matrix(HEAD_DIM)
        c, s1, s2 = c_ref[...], s1_ref[...], s2_ref[...]
        for ci in range(1024 // LANES):
            ls = slice(ci * LANES, (ci + 1) * LANES)
            xv = p_ref[:, ls].astype(F32)
            r = lax.rsqrt(_segment_dot(xv * xv, seg) + EPS)
            y = xv * r * g_ref[:, ls]
            val = y * c + pltpu.roll(y, LANES - 8, axis=1) * s1 + pltpu.roll(y, 8, axis=1) * s2
            o_ref[:, ls] = val.astype(BF16)
            scr[ci] = val

        j = pl.program_id(1)

        @pl.when(j == 0)
        def _():
            _regroup_out(scr, CHUNKS, q1_ref, r1, QK_TILE)

        @pl.when(j == 1)
        def _():
            _regroup_out(scr, 0, q2_ref, r2, QK_TILE)

        @pl.when(j == 2)
        def _():
            _regroup_out(scr, 0, k1_ref, r1, QK_TILE)
            _regroup_out(scr, CHUNKS, k2_ref, r2, QK_TILE)

    tab = pl.BlockSpec((QK_TILE, LANES), lambda i, j: (i, 0))
    g1, g2 = SDS((r1, t // r1, B_WIDTH), BF16), SDS((r2, t // r2, B_WIDTH), BF16)
    s1_, s2_ = _regrouped_spec(r1, QK_TILE), _regrouped_spec(r2, QK_TILE)
    return _call(
        body, grid=(t // QK_TILE, QK_COLS // 1024),
        in_specs=[pl.BlockSpec((QK_TILE, 1024), lambda i, j: (i, col0 + j)), pl.BlockSpec((1, 1024), lambda i, j: (0, j)),
                  tab, tab, tab],
        out_specs=[pl.BlockSpec((QK_TILE, 1024), lambda i, j: (i, j)), s1_, s2_, s1_, s2_],
        out_shape=[SDS((t, QK_COLS), BF16), g1, g2, g1, g2],
        scratch_shapes=[pltpu.VMEM((2 * CHUNKS, QK_TILE, LANES), F32)],
        operands=[proj, gains, *tabs], name="qk_norm_rope_fwd", ride=ride)


PERM_TILE = 512


def _permute(name, items, rate):
    t = items[0][0].shape[0]
    n = len(items)

    def body(*refs):
        scr = refs[-1]
        for x_ref, o_ref in zip(refs[:n], refs[n:2 * n]):
            for ci in range(CHUNKS):
                scr[ci] = x_ref[:, ci * LANES:(ci + 1) * LANES].astype(F32)
            _regroup_out(scr, 0, o_ref, rate, PERM_TILE)

    return pl.pallas_call(
        body, grid=(t // PERM_TILE,),
        in_specs=[pl.BlockSpec((PERM_TILE, B_WIDTH), functools.partial(lambda cb, i: (i, cb), cb)) for _, cb in items],
        out_specs=[_regrouped_spec(rate, PERM_TILE) for _ in items],
        out_shape=[SDS((rate, t // rate, B_WIDTH), a.dtype) for a, _ in items],
        scratch_shapes=[pltpu.VMEM((CHUNKS, PERM_TILE, LANES), F32)],
        name=name, compiler_params=_params())(*[a for a, _ in items])


def _head_lane_mask(h):
    lane = lax.broadcasted_iota(jnp.int32, (1, LANES), 1)
    return (lane < HEAD_DIM) if h == 0 else (lane >= HEAD_DIM)


def _attn_fwd(name, q, k, v, ride=None):
    rate, length = q[0].shape[0], q[0].shape[1]
    nb = length // ATT_BLOCK
    scale = HEAD_DIM ** -0.5

    def body(q_ref, kc_ref, kp_ref, vc_ref, vp_ref, o_ref, l_ref):
        n = pl.program_id(1)
        qi = lax.broadcasted_iota(jnp.int32, (ATT_BLOCK, 2 * ATT_BLOCK), 0)
        cj = lax.broadcasted_iota(jnp.int32, (ATT_BLOCK, 2 * ATT_BLOCK), 1)
        has_prev = jnp.where(n > 0, 0, 2 * ATT_BLOCK)
        mask = ((cj < ATT_BLOCK) & (cj >= qi + has_prev)) | ((cj >= ATT_BLOCK) & (cj - ATT_BLOCK <= qi))
        heads = [(hp, h) for hp in range(CHUNKS) for h in range(2)]
        q2, k2, v2 = {}, {}, {}
        for hp in range(CHUNKS):
            ls = slice(hp * LANES, (hp + 1) * LANES)
            q2[hp] = q_ref[:, ls]
            k2[hp] = jnp.concatenate([kp_ref[:, ls], kc_ref[:, ls]], axis=0)
            v2[hp] = jnp.concatenate([vp_ref[:, ls], vc_ref[:, ls]], axis=0)
        scores = {}
        for hp, h in heads:
            scores[hp, h] = _dot(jnp.where(_head_lane_mask(h), q2[hp], jnp.zeros_like(q2[hp])), k2[hp], NT) * scale
        probs, lses = {}, {}
        for hp, h in heads:
            s = jnp.where(mask, scores[hp, h], NEG_INF)
            m = jnp.max(s, axis=1, keepdims=True)
            p = jnp.exp(s - m)
            den = jnp.sum(p, axis=1, keepdims=True)
            lses[hp, h] = m + jnp.log(den)
            probs[hp, h] = (p / den).astype(BF16)
        for hp in range(CHUNKS):
            ls = slice(hp * LANES, (hp + 1) * LANES)
            o_acc = None
            for h in range(2):
                o = _dot(probs[hp, h], jnp.where(_head_lane_mask(h), v2[hp], jnp.zeros_like(v2[hp])), NN)
                o_acc = o if o_acc is None else o_acc + o
            o_ref[:, ls] = o_acc
            zeros = jnp.zeros((ATT_BLOCK, LANES), F32)
            l_ref[:, ls] = jnp.where(_head_lane_mask(1), lses[hp, 1] + zeros, lses[hp, 0] + zeros)

    def cur(cb):
        return pl.BlockSpec((None, ATT_BLOCK, B_WIDTH), lambda r, n: (r, n, cb))

    def prev(cb):
        return pl.BlockSpec((None, ATT_BLOCK, B_WIDTH), lambda r, n: (r, jnp.maximum(n - 1, 0), cb))

    out = pl.BlockSpec((None, ATT_BLOCK, B_WIDTH), lambda r, n: (r, n, 0))
    return _call(
        body, grid=(rate, nb),
        in_specs=[cur(q[1]), cur(k[1]), prev(k[1]), cur(v[1]), prev(v[1])],
        out_specs=[out, out], out_shape=[SDS((rate, length, B_WIDTH), F32)] * 2,
        operands=[q[0], k[0], k[0], v[0], v[0]], name=name, ride=ride)


def _attn_merge(a_out, o_list, l_list):
    t = a_out.shape[0]
    tm = PERM_TILE
    r1, r2 = DIL_RATES[1], DIL_RATES[2]

    def body(a_ref, o0, o1, o2, l0, l1, l2, cat_ref, lt_ref, lt1_ref, lt2_ref, so1, so2, sl1, sl2, slt):
        _regroup_in(o1, so1, r1, tm)
        _regroup_in(l1, sl1, r1, tm)
        _regroup_in(o2, so2, r2, tm)
        _regroup_in(l2, sl2, r2, tm)
        cat_ref[:, :A_WIDTH] = a_ref[...]
        for c in range(CHUNKS):
            ls = slice(c * LANES, (c + 1) * LANES)
            lg = [l0[:, ls], sl1[c], sl2[c]]
            m = jnp.maximum(jnp.maximum(lg[0], lg[1]), lg[2])
            es = [jnp.exp(l - m) for l in lg]
            den = es[0] + es[1] + es[2]
            b = (es[0] * o0[:, ls] + es[1] * so1[c] + es[2] * so2[c]) / den
            cat_ref[:, A_WIDTH + c * LANES:A_WIDTH + (c + 1) * LANES] = b.astype(BF16)
            lt = m + jnp.log(den)
            lt_ref[:, ls] = lt
            slt[c] = lt
        _regroup_out(slt, 0, lt1_ref, r1, tm)
        _regroup_out(slt, 0, lt2_ref, r2, tm)

    blk = _row_spec(tm, B_WIDTH)
    g1, g2 = _regrouped_spec(r1, tm), _regrouped_spec(r2, tm)
    return pl.pallas_call(
        body, grid=(t // tm,), in_specs=[blk, blk, g1, g2, blk, g1, g2],
        out_specs=[_row_spec(tm, A_WIDTH + B_WIDTH), blk, g1, g2],
        out_shape=[SDS((t, A_WIDTH + B_WIDTH), BF16), SDS((t, B_WIDTH), F32), SDS((r1, t // r1, B_WIDTH), F32),
                   SDS((r2, t // r2, B_WIDTH), F32)],
        scratch_shapes=[pltpu.VMEM((CHUNKS, tm, LANES), F32)] * 5,
        name="attn_merge", compiler_params=_params())(a_out, *o_list, *l_list)


def _attn_bwd_prep(dcat, cat):
    t = dcat.shape[0]
    tm = PERM_TILE
    r1, r2 = DIL_RATES[1], DIL_RATES[2]

    def body(d_ref, b_ref, db_ref, dd_ref, db1_ref, dd1_ref, db2_ref, dd2_ref, sdb, sdd):
        seg = _segment_mean_matrix(HEAD_DIM, scale=1.0)
        for c in range(CHUNKS):
            ls = slice(c * LANES, (c + 1) * LANES)
            d = d_ref[:, ls]
            dsum = _segment_dot(d * b_ref[:, ls].astype(F32), seg)
            db_ref[:, ls] = d.astype(BF16)
            dd_ref[:, ls] = dsum
            sdb[c] = d
            sdd[c] = dsum
        _regroup_out(sdb, 0, db1_ref, r1, tm)
        _regroup_out(sdd, 0, dd1_ref, r1, tm)
        _regroup_out(sdb, 0, db2_ref, r2, tm)
        _regroup_out(sdd, 0, dd2_ref, r2, tm)

    right = pl.BlockSpec((tm, B_WIDTH), lambda i: (i, 1))
    blk = _row_spec(tm, B_WIDTH)
    g1, g2 = _regrouped_spec(r1, tm), _regrouped_spec(r2, tm)
    return pl.pallas_call(
        body, grid=(t // tm,), in_specs=[right, right], out_specs=[blk, blk, g1, g1, g2, g2],
        out_shape=[SDS((t, B_WIDTH), BF16), SDS((t, B_WIDTH), F32), SDS((r1, t // r1, B_WIDTH), BF16),
                   SDS((r1, t // r1, B_WIDTH), F32), SDS((r2, t // r2, B_WIDTH), BF16), SDS((r2, t // r2, B_WIDTH), F32)],
        scratch_shapes=[pltpu.VMEM((CHUNKS, tm, LANES), F32)] * 2,
        name="attn_bwd_prep", compiler_params=_params())(dcat, cat)


def _attn_bwd(name, q, k, v, db, lse, dd, ride=None):
    rate, length = db.shape[0], db.shape[1]
    nb = length // ATT_BLOCK
    scale = HEAD_DIM ** -0.5

    def body(qa_ref, qb_ref, k_ref, v_ref, dba_ref, dbb_ref, la_ref, lb_ref, da_ref, dbd_ref, dq_ref, dk_ref, dv_ref, carry):
        m = pl.program_id(1)

        @pl.when(m == 0)
        def _():
            carry[...] = jnp.zeros_like(carry)

        row = lax.broadcasted_iota(jnp.int32, (2 * ATT_BLOCK, ATT_BLOCK), 0)
        kj = lax.broadcasted_iota(jnp.int32, (2 * ATT_BLOCK, ATT_BLOCK), 1)
        no_next = jnp.where(m + 1 < nb, 0, 2 * ATT_BLOCK)
        mask = ((row < ATT_BLOCK) & (kj <= row)) | ((row >= ATT_BLOCK) & (kj >= row - ATT_BLOCK + no_next))
        heads = [(hp, h) for hp in range(CHUNKS) for h in range(2)]
        q2, db2, lse2, dd2, k2, v2 = {}, {}, {}, {}, {}, {}
        for hp in range(CHUNKS):
            ls = slice(hp * LANES, (hp + 1) * LANES)
            k2[hp], v2[hp] = k_ref[:, ls], v_ref[:, ls]
            q2[hp] = jnp.concatenate([qa_ref[:, ls], qb_ref[:, ls]], axis=0)
            db2[hp] = jnp.concatenate([dba_ref[:, ls], dbb_ref[:, ls]], axis=0)
            lse2[hp] = jnp.concatenate([la_ref[:, ls], lb_ref[:, ls]], axis=0)
            dd2[hp] = jnp.concatenate([da_ref[:, ls], dbd_ref[:, ls]], axis=0)
        km, scores, dps = {}, {}, {}
        for hp, h in heads:
            hm = _head_lane_mask(h)
            km[hp, h] = jnp.where(hm, k2[hp], jnp.zeros_like(k2[hp]))
            scores[hp, h] = _dot(q2[hp], km[hp, h], NT) * scale
            dps[hp, h] = _dot(db2[hp], jnp.where(hm, v2[hp], jnp.zeros_like(v2[hp])), NT)
        probs, dss = {}, {}
        for hp, h in heads:
            hm = _head_lane_mask(h)
            lse_col = jnp.max(jnp.where(hm, lse2[hp], NEG_INF), axis=1, keepdims=True)
            dd_col = jnp.max(jnp.where(hm, dd2[hp], NEG_INF), axis=1, keepdims=True)
            p = jnp.where(mask, jnp.exp(scores[hp, h] - lse_col), 0.0)
            probs[hp, h] = p.astype(BF16)
            dss[hp, h] = (p * (dps[hp, h] - dd_col) * scale).astype(BF16)
        for hp in range(CHUNKS):
            ls = slice(hp * LANES, (hp + 1) * LANES)
            dq_acc, dk_acc, dv_acc = None, None, None
            for h in range(2):
                hm = _head_lane_mask(h)
                dvc = _dot(probs[hp, h], jnp.where(hm, db2[hp], jnp.zeros_like(db2[hp])), TN)
                dqc = _dot(dss[hp, h], km[hp, h], NN)
                dkc = _dot(dss[hp, h], jnp.where(hm, q2[hp], jnp.zeros_like(q2[hp])), TN)
                dq_acc = dqc if dq_acc is None else dq_acc + dqc
                dk_acc = dkc if dk_acc is None else dk_acc + dkc
                dv_acc = dvc if dv_acc is None else dv_acc + dvc
            dq_ref[:, ls] = (dq_acc[:ATT_BLOCK] + carry[:, ls]).astype(BF16)
            carry[:, ls] = dq_acc[ATT_BLOCK:]
            dk_ref[:, ls] = dk_acc.astype(BF16)
            dv_ref[:, ls] = dv_acc.astype(BF16)

    def cur(cb):
        return pl.BlockSpec((None, ATT_BLOCK, B_WIDTH), lambda r, n: (r, n, cb))

    def nxt(cb):
        return pl.BlockSpec((None, ATT_BLOCK, B_WIDTH), lambda r, n: (r, jnp.minimum(n + 1, nb - 1), cb))

    out = cur(0)
    return _call(
        body, grid=(rate, nb),
        in_specs=[cur(q[1]), nxt(q[1]), cur(k[1]), cur(v[1]), cur(0), nxt(0), cur(0), nxt(0), cur(0), nxt(0)],
        out_specs=[out, out, out], out_shape=[SDS((rate, length, B_WIDTH), BF16)] * 3,
        scratch_shapes=[pltpu.VMEM((ATT_BLOCK, B_WIDTH), F32)],
        operands=[q[0], q[0], k[0], v[0], db, db, lse, lse, dd, dd], name=name, ride=ride)


AB_IN = 2 * A_WIDTH + 3 * N_DIL * B_WIDTH
ASM_TILE = 256


def _dproj_assemble(proj, d_a, dq, dk, dv, gains, tabs, ride=None):
    t = proj.shape[0]
    n_in = 3 * N_DIL

    def body(p_ref, da_ref, *rest):
        grads = rest[:n_in]
        g_ref, c_ref, s1_ref, s2_ref, o_ref, dg_ref = rest[n_in:n_in + 6]
        scratch = rest[n_in + 6:]

        @pl.when(pl.program_id(0) == 0)
        def _():
            dg_ref[...] = jnp.zeros_like(dg_ref)

        chunk = {}
        k_scr = 0
        for j in range(n_in):
            g = j % N_DIL
            if DIL_RATES[g] == 1:
                for ci in range(CHUNKS):
                    chunk[j, ci] = functools.partial(lambda r, ci: r[:, ci * LANES:(ci + 1) * LANES].astype(F32), grads[j], ci)
            else:
                scr = scratch[k_scr]
                k_scr += 1
                _regroup_in(grads[j], scr, DIL_RATES[g], ASM_TILE)
                for ci in range(CHUNKS):
                    chunk[j, ci] = functools.partial(lambda s, ci: s[ci], scr, ci)

        seg = _segment_mean_matrix(HEAD_DIM)
        c, s1, s2 = c_ref[...], s1_ref[...], s2_ref[...]
        o_ref[:, :2 * A_WIDTH] = da_ref[...]
        for jg in range(2 * N_DIL):
            for ci in range(CHUNKS):
                col = jg * B_WIDTH + ci * LANES
                src = slice(2 * A_WIDTH + col, 2 * A_WIDTH + col + LANES)
                xv = p_ref[:, src].astype(F32)
                r = lax.rsqrt(_segment_dot(xv * xv, seg) + EPS)
                xh = xv * r
                gain = g_ref[:, col:col + LANES]
                do = chunk[jg, ci]()
                dy = do * c + pltpu.roll(do * s1, 8, axis=1) + pltpu.roll(do * s2, LANES - 8, axis=1)
                dg_ref[:, col:col + LANES] += jnp.sum(dy * xh, axis=0, keepdims=True)
                dxh = dy * gain
                o_ref[:, src] = (r * (dxh - xh * _segment_dot(dxh * xh, seg))).astype(BF16)
        v0 = 2 * A_WIDTH + QK_COLS
        for g in range(N_DIL):
            for ci in range(CHUNKS):
                col = v0 + g * B_WIDTH + ci * LANES
                o_ref[:, col:col + LANES] = chunk[2 * N_DIL + g, ci]().astype(BF16)

    specs = [_row_spec(ASM_TILE, B_WIDTH) if r == 1 else _regrouped_spec(r, ASM_TILE) for r in DIL_RATES] * 3
    n_scr = 3 * sum(1 for r in DIL_RATES if r > 1)
    tab = _row_spec(ASM_TILE, LANES)
    return _call(
        body, grid=(t // ASM_TILE,),
        in_specs=[_row_spec(ASM_TILE, AB_IN), _row_spec(ASM_TILE, 2 * A_WIDTH)] + specs
        + [_const_spec((1, QK_COLS)), tab, tab, tab],
        out_specs=[_row_spec(ASM_TILE, AB_IN), _const_spec((1, QK_COLS))],
        out_shape=[SDS((t, AB_IN), BF16), SDS((1, QK_COLS), F32)],
        scratch_shapes=[pltpu.VMEM((CHUNKS, ASM_TILE, LANES), F32)] * n_scr,
        operands=[proj, d_a, *dq, *dk, *dv, gains, *tabs], name="dproj_assemble", ride=ride)


def _fold_heads(dg_lane):
    n = dg_lane.shape[1]

    def body(x_ref, o_ref):
        r = lax.broadcasted_iota(jnp.int32, (B_WIDTH, B_WIDTH), 0) % HEAD_DIM
        c = lax.broadcasted_iota(jnp.int32, (B_WIDTH, B_WIDTH), 1) % HEAD_DIM
        fold = jnp.where(r == c, 1.0, 0.0).astype(F32)
        for jg in range(n // B_WIDTH):
            ls = slice(jg * B_WIDTH, (jg + 1) * B_WIDTH)
            o_ref[:, ls] = _dot_hi(jnp.broadcast_to(x_ref[:, ls], (8, B_WIDTH)), fold)

    return pl.pallas_call(body, out_shape=SDS((8, n), F32), name="fold_heads", compiler_params=_params())(dg_lane)


CD_TILE = 256
TAP_ROWS = 64
CD_IN = 2 * C_WIDTH + 3 * 512


def _shifted_copies(src, dst, rows):
    dst[0, :rows] = src[...]
    for b in range(1, 8):
        dst[b, :rows - 8] = src[pl.ds(b, rows - 8), :]


def _rows_from(shifted, start, n, lanes=slice(None)):
    b = start % 8
    return shifted[b, pl.ds(start - b, n), lanes]


def _mixer_cd_fwd(proj, cw, cb, lg, lb, dw):
    t = proj.shape[0]
    per = CD_TILE // HALO

    def body(h_ref, m_ref, cw_ref, cb_ref, lg_ref, lb_ref, dw_ref, o_ref, c1_ref, c_scr, e_scr, c_sh):
        not_first = (pl.program_id(0) > 0).astype(F32)
        lanes = [slice(c * LANES, (c + 1) * LANES) for c in range(C_WIDTH // LANES)]

        def col(ref, part, ls):
            return ref[:, part * C_WIDTH + ls.start:part * C_WIDTH + ls.stop].astype(F32)

        for ls in lanes:
            c_scr[:HALO, ls] = col(h_ref, 0, ls) * _sigmoid(col(h_ref, 1, ls)) * not_first
            c_scr[HALO:, ls] = col(m_ref, 0, ls) * _sigmoid(col(m_ref, 1, ls))
            e_scr[:HALO, ls] = col(h_ref, 3, ls) * col(h_ref, 4, ls) * not_first
            e_scr[HALO:, ls] = col(m_ref, 3, ls) * col(m_ref, 4, ls)
        _shifted_copies(c_scr, c_sh, HALO + CD_TILE)
        for ls in lanes:
            for r0 in range(0, CD_TILE, TAP_ROWS):
                acc = jnp.zeros((TAP_ROWS, LANES), F32)
                for k in range(C_KERNEL):
                    acc = acc + cw_ref[k:k + 1, ls] * _rows_from(c_sh, r0 + HALO - (C_KERNEL - 1) + k, TAP_ROWS, ls)
                c1_ref[r0:r0 + TAP_ROWS, ls] = acc + cb_ref[:, ls]
        mean = sum(jnp.sum(c1_ref[:, ls], axis=-1, keepdims=True) for ls in lanes) * (1.0 / C_WIDTH)
        var = sum(jnp.sum((c1_ref[:, ls] - mean) ** 2, axis=-1, keepdims=True) for ls in lanes) * (1.0 / C_WIDTH)
        rs = lax.rsqrt(var + EPS)
        for ls in lanes:
            c2 = (c1_ref[:, ls] - mean) * rs * lg_ref[:, ls] + lb_ref[:, ls]
            o_ref[:, ls] = (c2 * _sigmoid(c2)).astype(BF16)
            d1 = jnp.zeros((CD_TILE, LANES), F32)
            for k in range(D_KERNEL):
                d1 = d1 + dw_ref[k:k + 1, ls] * e_scr[pl.ds(HALO - (D_KERNEL - 1) + k, CD_TILE), ls]
            o_ref[:, C_WIDTH + ls.start:C_WIDTH + ls.stop] = (col(m_ref, 2, ls) * d1).astype(BF16)

    return pl.pallas_call(
        body, grid=(t // CD_TILE,),
        in_specs=[pl.BlockSpec((HALO, CD_IN), lambda i: (jnp.maximum(i * per - 1, 0), 0)), _row_spec(CD_TILE, CD_IN),
                  _const_spec((32, C_WIDTH)), _const_spec((1, C_WIDTH)), _const_spec((1, C_WIDTH)), _const_spec((1, C_WIDTH)),
                  _const_spec((8, C_WIDTH))],
        out_specs=[_row_spec(CD_TILE, 2 * C_WIDTH), _row_spec(CD_TILE, C_WIDTH)],
        out_shape=[SDS((t, 2 * C_WIDTH), BF16), SDS((t, C_WIDTH), F32)],
        scratch_shapes=[pltpu.VMEM((HALO + CD_TILE, C_WIDTH), F32)] * 2 + [pltpu.VMEM((8, HALO + CD_TILE, C_WIDTH), F32)],
        name="mixer_cd_fwd", compiler_params=_params())(proj, proj, cw, cb, lg, lb, dw)


def _mixer_cd_bwd(proj, dcat, c1, cw, lg, lb, dw, ride=None):
    t = proj.shape[0]
    per = CD_TILE // HALO
    nt = t // CD_TILE
    ext = CD_TILE + HALO

    def body(hp_ref, m_ref, hn_ref, dm_ref, dn_ref, c1m_ref, c1n_ref, cw_ref, lg_ref, lb_ref, dw_ref,
             dp_ref, dcw_ref, dcb_ref, dlg_ref, dlb_ref, ddw_ref, c_scr, e_scr, dc1_scr, dd1_scr, c_sh, dc1_sh, dcw_acc,
             dvh_scr, vhat_scr):
        i = pl.program_id(0)

        @pl.when(i == 0)
        def _():
            for r in (dcw_acc, dcb_ref, dlg_ref, dlb_ref, ddw_ref):
                r[...] = jnp.zeros_like(r)

        not_first = (i > 0).astype(F32)
        not_last = (i < nt - 1).astype(F32)
        main = slice(HALO, HALO + CD_TILE)
        lanes = [slice(c * LANES, (c + 1) * LANES) for c in range(C_WIDTH // LANES)]

        def col(ref, part, ls):
            return ref[:, part * C_WIDTH + ls.start:part * C_WIDTH + ls.stop].astype(F32)

        for ls in lanes:
            c_scr[:HALO, ls] = col(hp_ref, 0, ls) * _sigmoid(col(hp_ref, 1, ls)) * not_first
            c_scr[main, ls] = col(m_ref, 0, ls) * _sigmoid(col(m_ref, 1, ls))
            c_scr[HALO + CD_TILE:, ls] = col(hn_ref, 0, ls) * _sigmoid(col(hn_ref, 1, ls)) * not_last
            e_scr[:HALO, ls] = col(hp_ref, 3, ls) * col(hp_ref, 4, ls) * not_first
            e_scr[main, ls] = col(m_ref, 3, ls) * col(m_ref, 4, ls)
            e_scr[HALO + CD_TILE:, ls] = col(hn_ref, 3, ls) * col(hn_ref, 4, ls) * not_last
        _shifted_copies(c_scr, c_sh, 2 * HALO + CD_TILE)

        def c1_of(ls):
            return jnp.concatenate([c1m_ref[:, ls], c1n_ref[:, ls]], axis=0)

        mean = sum(jnp.sum(c1_of(ls), axis=-1, keepdims=True) for ls in lanes) * (1.0 / C_WIDTH)
        var = sum(jnp.sum((c1_of(ls) - mean) ** 2, axis=-1, keepdims=True) for ls in lanes) * (1.0 / C_WIDTH)
        rs = lax.rsqrt(var + EPS)
        sum_dvh, sum_dvh_vhat = 0.0, 0.0
        for ls in lanes:
            vhat = (c1_of(ls) - mean) * rs
            c2 = vhat * lg_ref[:, ls] + lb_ref[:, ls]
            sig = _sigmoid(c2)
            dc = jnp.concatenate([dm_ref[:, ls], dn_ref[:, ls] * not_last], axis=0)
            dc2 = dc * (sig * (1.0 + c2 * (1.0 - sig)))
            dvh = dc2 * lg_ref[:, ls]
            sum_dvh = sum_dvh + jnp.sum(dvh, axis=-1, keepdims=True)
            sum_dvh_vhat = sum_dvh_vhat + jnp.sum(dvh * vhat, axis=-1, keepdims=True)
            dvh_scr[:, ls] = dvh
            vhat_scr[:, ls] = vhat
            dlg_ref[:, ls] += jnp.sum((dc2 * vhat)[:CD_TILE], axis=0, keepdims=True)
            dlb_ref[:, ls] += jnp.sum(dc2[:CD_TILE], axis=0, keepdims=True)
        for ls in lanes:
            dc1 = rs * (dvh_scr[:, ls] - sum_dvh * (1.0 / C_WIDTH) - vhat_scr[:, ls] * (sum_dvh_vhat * (1.0 / C_WIDTH)))
            dc1_scr[:, ls] = dc1
            dcb_ref[:, ls] += jnp.sum(dc1[:CD_TILE], axis=0, keepdims=True)
        _shifted_copies(dc1_scr, dc1_sh, ext)
        for ls in lanes:
            for r0 in range(0, CD_TILE, TAP_ROWS):
                rows = slice(r0, r0 + TAP_ROWS)
                dc1_m = dc1_scr[rows, ls]
                dc0 = jnp.zeros((TAP_ROWS, LANES), F32)
                for k in range(C_KERNEL):
                    dc0 = dc0 + cw_ref[k:k + 1, ls] * _rows_from(dc1_sh, r0 + C_KERNEL - 1 - k, TAP_ROWS, ls)
                    prod = dc1_m * _rows_from(c_sh, r0 + HALO - (C_KERNEL - 1) + k, TAP_ROWS, ls)
                    dcw_acc[k, :, ls] += prod.reshape(TAP_ROWS // 8, 8, LANES).sum(axis=0)
                g_m = m_ref[rows, C_WIDTH + ls.start:C_WIDTH + ls.stop].astype(F32)
                a_m = m_ref[rows, ls].astype(F32)
                sig_m = _sigmoid(g_m)
                dp_ref[rows, ls] = (dc0 * sig_m).astype(BF16)
                dp_ref[rows, C_WIDTH + ls.start:C_WIDTH + ls.stop] = (dc0 * a_m * sig_m * (1.0 - sig_m)).astype(BF16)

        @pl.when(i == nt - 1)
        def _():
            dcw_ref[...] = jnp.sum(dcw_acc[...], axis=1)

        for ls in lanes:
            wide = slice(C_WIDTH + ls.start, C_WIDTH + ls.stop)
            d1 = jnp.zeros((CD_TILE, LANES), F32)
            for k in range(D_KERNEL):
                d1 = d1 + dw_ref[k:k + 1, ls] * e_scr[pl.ds(HALO - (D_KERNEL - 1) + k, CD_TILE), ls]
            dd_m = dm_ref[:, wide]
            dd1 = jnp.concatenate([dd_m * col(m_ref, 2, ls), dn_ref[:, wide] * col(hn_ref, 2, ls) * not_last], axis=0)
            dd1_scr[:, ls] = dd1
            dp_ref[:, 2 * C_WIDTH + ls.start:2 * C_WIDTH + ls.stop] = (dd_m * d1).astype(BF16)
            de = jnp.zeros((CD_TILE, LANES), F32)
            for k in range(D_KERNEL):
                de = de + dw_ref[k:k + 1, ls] * dd1_scr[pl.ds(D_KERNEL - 1 - k, CD_TILE), ls]
                ddw_ref[k:k + 1, ls] += jnp.sum(dd1[:CD_TILE] * e_scr[pl.ds(HALO - (D_KERNEL - 1) + k, CD_TILE), ls], axis=0, keepdims=True)
            dp_ref[:, 3 * C_WIDTH + ls.start:3 * C_WIDTH + ls.stop] = (de * col(m_ref, 4, ls)).astype(BF16)
            dp_ref[:, 4 * C_WIDTH + ls.start:4 * C_WIDTH + ls.stop] = (de * col(m_ref, 3, ls)).astype(BF16)

    halo_prev = lambda i: (jnp.maximum(i * per - 1, 0), 0)
    halo_next = lambda i: (jnp.minimum((i + 1) * per, t // HALO - 1), 0)
    vec = _const_spec((1, C_WIDTH))
    return _call(
        body, grid=(nt,),
        in_specs=[pl.BlockSpec((HALO, CD_IN), halo_prev), _row_spec(CD_TILE, CD_IN), pl.BlockSpec((HALO, CD_IN), halo_next),
                  _row_spec(CD_TILE, 2 * C_WIDTH), pl.BlockSpec((HALO, 2 * C_WIDTH), halo_next),
                  _row_spec(CD_TILE, C_WIDTH), pl.BlockSpec((HALO, C_WIDTH), halo_next),
                  _const_spec((32, C_WIDTH)), vec, vec, _const_spec((8, C_WIDTH))],
        out_specs=[_row_spec(CD_TILE, CD_IN), _const_spec((32, C_WIDTH)), vec, vec, vec, _const_spec((8, C_WIDTH))],
        out_shape=[SDS((t, CD_IN), BF16), SDS((32, C_WIDTH), F32), SDS((1, C_WIDTH), F32), SDS((1, C_WIDTH), F32),
                   SDS((1, C_WIDTH), F32), SDS((8, C_WIDTH), F32)],
        scratch_shapes=[pltpu.VMEM((2 * HALO + CD_TILE, C_WIDTH), F32)] * 2 + [pltpu.VMEM((ext, C_WIDTH), F32)] * 2
        + [pltpu.VMEM((8, 2 * HALO + CD_TILE, C_WIDTH), F32), pltpu.VMEM((8, ext, C_WIDTH), F32),
           pltpu.VMEM((32, 8, C_WIDTH), F32)] + [pltpu.VMEM((ext, C_WIDTH), F32)] * 2,
        operands=[proj, proj, proj, dcat, dcat, c1, c1, cw, lg, lb, dw], name="mixer_cd_bwd", ride=ride)


def _wgrad(name, pairs, out_rc, t, ride):
    tk = TILES["wgrad"]
    r, c = out_rc
    n = len(pairs)

    def body(*refs):
        ab, out_refs = refs[:2 * n], refs[2 * n:]
        k = pl.program_id(1)
        parts = [_dot(ab[2 * j][...], ab[2 * j + 1][...], TN) for j in range(n)]

        @pl.when(k == 0)
        def _():
            for a in range(n):
                out_refs[a][...] = parts[a]

        @pl.when(k > 0)
        def _():
            for a in range(n):
                out_refs[a][...] += parts[a]

    operands, in_specs = [], []
    for lhs, lhs_spec, rhs, rhs_spec in pairs:
        operands += [lhs, rhs]
        in_specs += [lhs_spec, rhs_spec]
    res = _call(body, grid=(N_CHIPS, t // tk), in_specs=in_specs,
                out_specs=[pl.BlockSpec((None, r, c), lambda p, k: (p, 0, 0))] * n,
                out_shape=[SDS((N_CHIPS, r, c), F32)] * n, operands=operands, name=name, ride=ride)
    outs, ride_res = (res, None) if ride is None else res
    outs = [o.reshape(N_CHIPS, 2, r // 2, c) for o in outs]
    return outs if ride is None else (outs, ride_res)


def _wgrad_col_sharded(name, h, dz_list, three_d, ride=None):
    t, d = h.shape
    tk = TILES["wgrad"]
    n4 = dz_list[0].shape[-1] if three_d else dz_list[0].shape[-1] // N_CHIPS
    hs = pl.BlockSpec((tk, d), lambda p, k: (k, 0))
    zs = pl.BlockSpec((None, tk, n4), lambda p, k: (p, k, 0)) if three_d else pl.BlockSpec((tk, n4), lambda p, k: (k, p))
    return _wgrad(name, [(h, hs, dz, zs) for dz in dz_list], (d, n4), t, ride)


def _wgrad_row_sharded(name, a, g, three_d, ride=None):
    many = isinstance(a, (list, tuple))
    a_list = list(a) if many else [a]
    t, d = g.shape
    tk = TILES["wgrad"]
    k4 = a_list[0].shape[-1] if three_d else a_list[0].shape[-1] // N_CHIPS
    a_spec = pl.BlockSpec((None, tk, k4), lambda p, k: (p, k, 0)) if three_d else pl.BlockSpec((tk, k4), lambda p, k: (k, p))
    gs = pl.BlockSpec((tk, d), lambda p, k: (k, 0))
    res = _wgrad(name, [(a_j, a_spec, g, gs) for a_j in a_list], (k4, d), t, ride)
    if many:
        return res
    return res[0] if ride is None else (res[0][0], res[1])


def _mesh_scalars():
    return jnp.stack([lax.axis_index("c"), 2 * lax.axis_index("x") + lax.axis_index("y")]).astype(jnp.int32)


def _stage_own(name, w, layer, dtype):
    layers, r, cols = w.shape
    h = r // 2

    def body(s_ref, x_ref, o_ref):
        o_ref[...] = x_ref[...].astype(dtype)

    return pl.pallas_call(
        body,
        grid_spec=pltpu.PrefetchScalarGridSpec(
            num_scalar_prefetch=1, grid=(2,),
            in_specs=[pl.BlockSpec((None, h, cols), lambda i, s: (2 * layer + i, 0, 0))],
            out_specs=pl.BlockSpec((None, None, h, cols), lambda i, s: (s[1], i, 0, 0))),
        out_shape=SDS((N_CHIPS, 2, h, cols), dtype), name=name,
        compiler_params=_params())(_mesh_scalars(), w.reshape(2 * layers, h, cols))


def _remote(src, dst, send_sem, recv_sem, device):
    return pltpu.make_async_remote_copy(src, dst, send_sem, recv_sem, device_id=device, device_id_type=MESH)


def _ride_gather_send(bufs):
    n = len(bufs)

    def each(b, sems, act):
        send, recv = sems
        x, y, c, p, others = _position()
        for t in range(n):
            for j, (qx, qy) in enumerate(others):
                act(b[t].at[p, c], b[t].at[2 * qx + qy, c], send.at[t, j], recv.at[t, j], (qx, qy, c))

    def start(ins, b, new, sems):
        each(b, sems, lambda mine, landed, s, r, dev: _remote(mine, mine, s, r, dev).start())

    def finish(ins, b, new, sems):
        def act(mine, landed, s, r, dev):
            _remote(mine, mine, s, r, dev).wait_send()
            _remote(landed, landed, s, r, dev).wait_recv()
        each(b, sems, act)

    return _Ride([], bufs, [], [(n, 3), (n, 3)], start, finish)


def _ride_gather_pass(bufs):
    n = len(bufs)

    def each(b, sems, act):
        send, recv = sems
        x, y, c, p, others = _position()
        for t in range(n):
            for j, (qx, qy) in enumerate(others):
                act(b[t].at[2 * qx + qy, c], b[t].at[2 * qx + qy, 1 - c], send.at[t, j], recv.at[t, j], (x, y, 1 - c))

    def start(ins, b, new, sems):
        each(b, sems, lambda landed, passed, s, r, dev: _remote(landed, landed, s, r, dev).start())

    def finish(ins, b, new, sems):
        def act(landed, passed, s, r, dev):
            _remote(landed, landed, s, r, dev).wait_send()
            _remote(passed, passed, s, r, dev).wait_recv()
        each(b, sems, act)

    return _Ride([], bufs, [], [(n, 3), (n, 3)], start, finish)


def _ride_swap(tensors):
    n = len(tensors)

    def each(ins, new, sems, act):
        send, recv = sems
        x, y, c, _, _ = _position()
        for t in range(n):
            act(_remote(ins[t].at[:, 1 - c], new[t], send.at[t], recv.at[t], (x, y, 1 - c)))

    def start(ins, b, new, sems):
        each(ins, new, sems, lambda cp: cp.start())

    def finish(ins, b, new, sems):
        each(ins, new, sems, lambda cp: cp.wait())

    return _Ride(tensors, [], [SDS((s.shape[0],) + s.shape[2:], s.dtype) for s in tensors], [(n,), (n,)], start, finish)


def _ride_scatter(tensors, landing):
    n = len(tensors)

    def each(ins, b, sems, act):
        send, recv = sems
        x, y, c, p, others = _position()
        for t in range(n):
            for j, (qx, qy) in enumerate(others):
                q = 2 * qx + qy
                act(ins[t].at[q], b[t].at[p], b[t].at[q], send.at[t, j], recv.at[t, j], (qx, qy, c))

    def start(ins, b, new, sems):
        each(ins, b, sems, lambda src, dst, landed, s, r, dev: _remote(src, dst, s, r, dev).start())

    def finish(ins, b, new, sems):
        def act(src, dst, landed, s, r, dev):
            _remote(src, dst, s, r, dev).wait_send()
            _remote(landed, landed, s, r, dev).wait_recv()
        each(ins, b, sems, act)

    return _Ride(tensors, landing, [], [(n, 3), (n, 3)], start, finish)


def _ride_join(bufs):
    n = len(bufs)

    def each(b, sems, act):
        send, recv = sems
        x, y, c, _, _ = _position()
        for t in range(n):
            act(b[t].at[c], b[t].at[1 - c], send.at[t], recv.at[t], (x, y, 1 - c))

    def start(ins, b, new, sems):
        each(b, sems, lambda mine, theirs, s, r, dev: _remote(mine, mine, s, r, dev).start())

    def finish(ins, b, new, sems):
        def act(mine, theirs, s, r, dev):
            _remote(mine, mine, s, r, dev).wait_send()
            _remote(theirs, theirs, s, r, dev).wait_recv()
        each(b, sems, act)

    return _Ride([], bufs, [], [(n,), (n,)], start, finish)


def _all_reduce_small(pack):
    rows = pack.shape[0]
    n_dev = 2 * N_CHIPS

    def body(x_ref, o_ref, land, send, recv):
        x, y, c, p, _ = _position()
        me = 2 * p + c
        land[me] = x_ref[...]
        peers = [(dx, dy, dc) for dx in range(2) for dy in range(2) for dc in range(2) if (dx, dy, dc) != (0, 0, 0)]
        for j, (dx, dy, dc) in enumerate(peers):
            _remote(land.at[me], land.at[me], send.at[j], recv.at[j], (x ^ dx, y ^ dy, c ^ dc)).start()
        for j, (dx, dy, dc) in enumerate(peers):
            src = 4 * (x ^ dx) + 2 * (y ^ dy) + (c ^ dc)
            _remote(land.at[me], land.at[me], send.at[j], recv.at[j], (x ^ dx, y ^ dy, c ^ dc)).wait_send()
            _remote(land.at[src], land.at[src], send.at[j], recv.at[j], (x ^ dx, y ^ dy, c ^ dc)).wait_recv()
        acc = land[0]
        for dev in range(1, n_dev):
            acc = acc + land[dev]
        o_ref[...] = acc

    return pl.pallas_call(
        body, out_shape=SDS((rows, LANES), F32),
        scratch_shapes=[pltpu.VMEM((n_dev, rows, LANES), F32), pltpu.SemaphoreType.DMA((n_dev - 1,)),
                        pltpu.SemaphoreType.DMA((n_dev - 1,))],
        name="all_reduce_small", compiler_params=_params())(pack)


def _add_own_half(name, full, recv, out_dtype):
    n4, _, h, cols = full.shape

    def body(s_ref, a_ref, b_ref, o_ref, own_ref):
        v = (a_ref[...] + b_ref[...]).astype(out_dtype)
        o_ref[...] = v

        @pl.when(pl.program_id(0) == s_ref[1])
        def _():
            own_ref[...] = v

    return pl.pallas_call(
        body,
        grid_spec=pltpu.PrefetchScalarGridSpec(
            num_scalar_prefetch=1, grid=(n4,),
            in_specs=[pl.BlockSpec((None, None, h, cols), lambda q, s: (q, s[0], 0, 0)),
                      pl.BlockSpec((None, h, cols), lambda q, s: (q, 0, 0))],
            out_specs=[pl.BlockSpec((None, h, cols), lambda q, s: (q, 0, 0)),
                       pl.BlockSpec((None, h, cols), lambda q, s: (s[1], 0, 0))]),
        out_shape=[SDS((n4, h, cols), out_dtype)] * 2, name=name, compiler_params=_params())(_mesh_scalars(), full, recv)


def _sum_chips(name, parts):
    n4, h, cols = parts.shape
    th = h // 4 if h % 64 == 0 else h

    def body(s_ref, a_ref, o_ref):
        acc = a_ref[0].astype(F32)
        for q in range(1, n4):
            acc = acc + a_ref[q].astype(F32)
        o_ref[...] = acc

    return pl.pallas_call(
        body,
        grid_spec=pltpu.PrefetchScalarGridSpec(
            num_scalar_prefetch=1, grid=(h // th,),
            in_specs=[pl.BlockSpec((n4, th, cols), lambda i, s: (0, i, 0))],
            out_specs=pl.BlockSpec((None, th, cols), lambda i, s: (s[0], i, 0))),
        out_shape=SDS((2, h, cols), F32), name=name, compiler_params=_params())(_mesh_scalars(), parts)


def _adamw_math(w, g, m, v):
    m2 = ADAM_B1 * m + (1.0 - ADAM_B1) * g
    v2 = ADAM_B2 * v + (1.0 - ADAM_B2) * (g * g)
    m_hat = m2 / (1.0 - ADAM_B1 ** ADAM_STEP)
    v_hat = v2 / (1.0 - ADAM_B2 ** ADAM_STEP)
    delta = -ADAM_LR * (m_hat / (jnp.sqrt(v_hat) + ADAM_EPS) + ADAM_WD * w)
    return delta, m2, v2


def _row_tile(rows, cols):
    cap = max(8, (1 << 18) // cols)
    best = 8
    for cand in range(8, min(rows, cap) + 1, 8):
        if rows % cand == 0:
            best = cand
    return best


def _adamw_big(name, w, g_layers, m, v):
    layers, rows, cols = w.shape
    tr = _row_tile(rows, cols)

    def body(w_ref, m_ref, v_ref, *rest):
        g_refs, (g_o, d_o, m_o, v_o) = rest[:layers], rest[layers:]
        gv = g_refs[0][...]
        for layer in range(1, layers):
            gv = jnp.where(pl.program_id(0) == layer, g_refs[layer][...], gv)
        d, mm, vv = _adamw_math(w_ref[...], gv, m_ref[...], v_ref[...])
        g_o[...] = gv
        d_o[...] = d
        m_o[...] = mm
        v_o[...] = vv

    blk = pl.BlockSpec((None, tr, cols), lambda l, i: (l, i, 0))
    g_blk = pl.BlockSpec((tr, cols), lambda l, i: (i, 0))
    return tuple(pl.pallas_call(
        body, grid=(layers, rows // tr), in_specs=[blk] * 3 + [g_blk] * layers, out_specs=[blk] * 4,
        out_shape=[SDS((layers, rows, cols), F32)] * 4, name=name,
        compiler_params=_params())(w, m, v, *[g.reshape(rows, cols) for g in g_layers]))


def _adamw_small(ws, gs, ms, vs):
    n = len(ws)
    flat = []
    for group in (ws, gs, ms, vs):
        flat += [a.reshape(-1, a.shape[-1]) for a in group]

    def body(*refs):
        w_r, g_r, m_r, v_r = refs[:n], refs[n:2 * n], refs[2 * n:3 * n], refs[3 * n:4 * n]
        d_o, m_o, v_o = refs[4 * n:5 * n], refs[5 * n:6 * n], refs[6 * n:7 * n]
        for j in range(n):
            d, mm, vv = _adamw_math(w_r[j][...], g_r[j][...], m_r[j][...], v_r[j][...])
            d_o[j][...] = d
            m_o[j][...] = mm
            v_o[j][...] = vv

    shapes = [SDS(a.shape, F32) for a in flat[:n]]
    outs = pl.pallas_call(body, out_shape=shapes * 3, name="adamw_small", compiler_params=_params())(*flat)
    res = []
    for k in range(3):
        res.append([outs[k * n + j].reshape(ws[j].shape) for j in range(n)])
    return res


BIG = ("ab_w_in", "ab_w_out", "cd_w_in", "cd_w_out", "ffn_w_gate", "ffn_w_up", "ffn_w_down")
V_BLOCK = (2 * A_WIDTH + QK_COLS) // B_WIDTH


def _pad_rows(a, rows):
    return jnp.pad(a, ((0, rows - a.shape[0]), (0, 0)))


A_IN, A_OUT, C_IN, C_OUT = ("ab_w_in", 0), ("ab_w_out", 0), ("cd_w_in", 0), ("cd_w_out", 0)
G0, U0, D0 = ("ffn_w_gate", 0), ("ffn_w_up", 0), ("ffn_w_down", 0)
G1, U1, D1 = ("ffn_w_gate", 1), ("ffn_w_up", 1), ("ffn_w_down", 1)
UNITS = (A_IN, A_OUT, G0, U0, D0, C_IN, C_OUT, G1, U1, D1)
ROWS_MINOR = ("ffn_w_gate", "ffn_w_up")
SMALL_SHARDED = ("small", 0)
REPLICATED_UNIT = ("replicated", 0)


class _Exchange:
    def __init__(self, enabled):
        self.enabled = enabled
        self.w, self.grad, self.recv, self.half, self.land, self.done = {}, {}, {}, {}, {}, {}

    def full(self, unit):
        b = self.w[unit]
        return b.reshape(N_CHIPS, 1, 2 * b.shape[2], b.shape[3])

    def _ride(self, phases):
        rides, sinks = [], []
        for kind, units in phases:
            if kind == "send":
                rides.append(_ride_gather_send([self.w[u] for u in units]))
                sinks.append(self.w)
            elif kind == "pass":
                rides.append(_ride_gather_pass([self.w[u] for u in units]))
                sinks.append(self.w)
            elif kind == "swap":
                rides.append(_ride_swap([self.grad[u] for u in units]))
                sinks.append(self.recv)
            elif kind == "scatter":
                rides.append(_ride_scatter([self.half[u] for u in units], [self.land[u] for u in units]))
                sinks.append(self.land)
            else:
                rides.append(_ride_join([self.done[u] for u in units]))
                sinks.append(self.done)
        ride = functools.reduce(_ride_both, rides)

        def settle(res):
            n_bufs = sum(len(r.bufs) for r in rides)
            bufs, new = list(res[:n_bufs]), list(res[n_bufs:])
            for r, sink, (_, units) in zip(rides, sinks, phases):
                vals = [bufs.pop(0) for _ in r.bufs] + [new.pop(0) for _ in r.new_outs]
                for u, v in zip(units, vals):
                    sink[u] = v

        return ride, settle

    def run(self, fn, *args, phases=(), **kw):
        if not self.enabled or not phases:
            return fn(*args, **kw)
        ride, settle = self._ride(phases)
        out, res = fn(*args, ride=ride, **kw)
        settle(res)
        return out

    def alone(self, name, phases):
        if self.enabled:
            ride, settle = self._ride(phases)
            settle(_run_ride(name, ride))

    def pair_sum(self, units):
        if self.enabled:
            for u in units:
                dtype = F32 if u in (SMALL_SHARDED, REPLICATED_UNIT) else BF16
                self.half[u], self.land[u] = _add_own_half(f"pair_sum_{u[0]}_{u[1]}", self.grad[u], self.recv[u], dtype)

    def chip_sum(self, units):
        if self.enabled:
            for u in units:
                self.done[u] = _sum_chips(f"chip_sum_{u[0]}_{u[1]}", self.land[u])


def _local_step(x, target, ex, sp):
    t, d = x.shape
    tabs = _rope_tables(t)
    gains = jnp.concatenate([jnp.tile(sp["q_norm_g"][g], HEAD_DIM // 8) for g in range(N_DIL)]
                            + [jnp.tile(sp["k_norm_g"][g], HEAD_DIM // 8) for g in range(N_DIL)]).reshape(1, QK_COLS)
    bias_t = sp["sgu_bias"].T
    cw = _pad_rows(sp["conv_c_w"], 32)
    dw = _pad_rows(sp["conv_d_w"], 8)
    cb, clg, clb = (sp[k].reshape(1, C_WIDTH) for k in ("conv_c_b", "c_ln_g", "c_ln_b"))
    slg, slb = sp["sgu_norm_g"].reshape(1, A_WIDTH), sp["sgu_norm_b"].reshape(1, A_WIDTH)
    g_ab, g_cd = sp["ab_norm_g"].reshape(1, d), sp["cd_norm_g"].reshape(1, d)
    g_f0, g_f1 = sp["ffn_norm_g"][0:1], sp["ffn_norm_g"][1:2]
    run = ex.run

    def w2d(unit):
        return ex.full(unit).reshape(-1, d)

    h0 = _rms_fwd("rms_ab", x, g_ab)
    proj = run(_proj_in, "proj_ab", h0, ex.full(A_IN), 0, phases=[("send", [A_OUT, G0])])
    a_out = _mixer_a_fwd(proj, slg, slb, sp["sgu_w"], bias_t)
    qk, q1, q2, k1, k2 = run(_qk_fwd, proj, gains, tabs, phases=[("pass", [A_OUT, G0]), ("send", [U0])])
    regrouped_qk = {1: (q1, k1), 2: (q2, k2)}
    fwd_phases = ([("pass", [U0]), ("send", [D0])], [("pass", [D0]), ("send", [C_IN])],
                  [("pass", [C_IN]), ("send", [C_OUT])])
    qkv, o_list, l_list = [], [], []
    for g, rate in enumerate(DIL_RATES):
        if rate == 1:
            qk3, proj3 = qk.reshape(1, t, QK_COLS), proj.reshape(1, t, AB_IN)
            q, k, v = (qk3, g), (qk3, N_DIL + g), (proj3, V_BLOCK + g)
        else:
            vp, = _permute(f"regroup_v_{g}", [(proj, V_BLOCK + g)], rate)
            q, k, v = (regrouped_qk[g][0], 0), (regrouped_qk[g][1], 0), (vp, 0)
        qkv.append((q, k, v))
        o, l = run(_attn_fwd, f"attn_fwd_{g}", q, k, v, phases=fwd_phases[g])
        if rate == 1:
            o, l = o.reshape(t, B_WIDTH), l.reshape(t, B_WIDTH)
        o_list.append(o)
        l_list.append(l)
    cat, lse_tot, lse_1, lse_2 = _attn_merge(a_out, o_list, l_list)
    x1, hf0 = _proj_out("out_ab", cat, w2d(A_OUT), x, g_next=g_f0)
    gate0, up0, act0 = run(_ffn_in, "ffn_in_0", hf0, ex.full(G0), ex.full(U0), 0,
                           phases=[("pass", [C_OUT]), ("send", [G1, U1])])
    x2, h1 = run(_ffn_out, "ffn_out_0", act0, ex.full(D0), 0, x1, g_next=g_cd, phases=[("pass", [G1, U1]), ("send", [D1])])
    projcd = run(_proj_in, "proj_cd", h1, ex.full(C_IN), 0, phases=[("pass", [D1])])
    cat2, c1 = _mixer_cd_fwd(projcd, cw, cb, clg, clb, dw)
    x3, hf1 = _proj_out("out_cd", cat2, w2d(C_OUT), x2, g_next=g_f1)
    gate1, up1, act1 = _ffn_in("ffn_in_1", hf1, ex.full(G1), ex.full(U1), 0)
    dy, loss_acc, dy_b = _ffn_out("ffn_out_1", act1, ex.full(D1), 0, x3, target=target)
    loss = 0.5 * loss_acc[0, 0] / d

    late = [D1, G1, U1]
    dgate, dup = _ffn_dact("ffn_dact_1", dy_b, ex.full(D1), 0, gate1, up1)
    ex.grad[D1] = _wgrad_row_sharded("wgrad_down_1", act1, dy_b, True)
    ex.grad[G1], ex.grad[U1] = _wgrad_row_sharded("wgrad_gate_up_1", [dgate, dup], hf1, True)
    g3, d_f1, g3_b = run(_dgrad_cols, "dgrad_ffn_1", [dgate, dup], [ex.full(G1), ex.full(U1)], 0, True, x3, g_f1, dy,
                         w_rows=True, phases=[("swap", late)])
    ex.pair_sum(late)

    dcat2 = _dgrad_rows("dgrad_out_cd", g3_b, w2d(C_OUT))
    ex.grad[C_OUT] = _wgrad_row_sharded("wgrad_out_cd", cat2, g3_b, False)
    dprojcd, d_cw, d_cb, d_clg, d_clb, d_dw = run(_mixer_cd_bwd, projcd, dcat2, c1, cw, clg, clb, dw, phases=[("scatter", late)])
    ex.chip_sum(late)
    ex.grad[C_IN] = run(_wgrad_col_sharded, "wgrad_in_cd", h1, [dprojcd], False, phases=[("join", late)])[0]
    g2, d_cdn, g2_b = run(_dgrad_cols, "dgrad_in_cd", [dprojcd], [ex.full(C_IN)], 0, False, x2, g_cd, g3,
                          phases=[("swap", [C_OUT, C_IN])])
    ex.pair_sum([C_OUT, C_IN])

    dgate, dup = run(_ffn_dact, "ffn_dact_0", g2_b, ex.full(D0), 0, gate0, up0, phases=[("scatter", [C_OUT, C_IN])])
    ex.chip_sum([C_OUT, C_IN])
    ex.grad[D0] = run(_wgrad_row_sharded, "wgrad_down_0", act0, g2_b, True, phases=[("join", [C_OUT, C_IN])])
    ex.grad[G0], ex.grad[U0] = _wgrad_row_sharded("wgrad_gate_up_0", [dgate, dup], hf0, True)
    small = {"cd_norm_g": d_cdn, "conv_c_w": d_cw[:C_KERNEL], "conv_c_b": d_cb, "c_ln_g": d_clg, "c_ln_b": d_clb,
             "conv_d_w": d_dw[:D_KERNEL]}
    ex.grad[SMALL_SHARDED] = _split_full_small(small).reshape(N_CHIPS, 2, SHARDED_ROWS // 2, LANES)
    mid = [D0, G0, U0, SMALL_SHARDED]
    g1, d_f0, g1_b = run(_dgrad_cols, "dgrad_ffn_0", [dgate, dup], [ex.full(G0), ex.full(U0)], 0, True, x1, g_f0, g2,
                         w_rows=True, phases=[("swap", mid)])
    ex.pair_sum(mid)

    dcat = _dgrad_rows("dgrad_out_ab", g1_b, w2d(A_OUT))
    ex.grad[A_OUT] = _wgrad_row_sharded("wgrad_out_ab", cat, g1_b, False)
    d_a, d_sw, d_sbt, d_slg, d_slb = _mixer_a_bwd(proj, dcat, slg, slb, sp["sgu_w"], bias_t)
    early = {"sgu_norm_g": d_slg, "sgu_norm_b": d_slb, "sgu_w": d_sw, "sgu_bias": d_sbt.T}
    ex.grad[REPLICATED_UNIT] = jnp.broadcast_to(
        _pack_replicated(early, REPLICATED_EARLY, REPLICATED_EARLY_ROWS).reshape(2, REPLICATED_EARLY_ROWS // 2, LANES),
        (N_CHIPS, 2, REPLICATED_EARLY_ROWS // 2, LANES))
    last = [A_OUT, REPLICATED_UNIT]
    dbb, dd, db_1, dd_1, db_2, dd_2 = _attn_bwd_prep(dcat, cat)
    regrouped_bwd = {1: (db_1, lse_1, dd_1), 2: (db_2, lse_2, dd_2)}
    bwd_phases = ([("scatter", [D0, SMALL_SHARDED])],
                  [("scatter", [G0]), ("join", [D0, SMALL_SHARDED]), ("swap", last)],
                  [("scatter", [U0]), ("join", [G0])])
    dqs, dks, dvs = [], [], []
    for g, rate in enumerate(DIL_RATES):
        q, k, v = qkv[g]
        if rate == 1:
            db3, l3, dd3 = (a.reshape(1, t, B_WIDTH) for a in (dbb, lse_tot, dd))
        else:
            db3, l3, dd3 = regrouped_bwd[g]
        if g == 1:
            ex.chip_sum([D0, SMALL_SHARDED])
        elif g == 2:
            ex.chip_sum([G0])
            ex.pair_sum(last)
        dq, dk, dv = run(_attn_bwd, f"attn_bwd_{g}", q, k, v, db3, l3, dd3, phases=bwd_phases[g])
        if rate == 1:
            dq, dk, dv = (a.reshape(t, B_WIDTH) for a in (dq, dk, dv))
        dqs.append(dq)
        dks.append(dk)
        dvs.append(dv)
    ex.chip_sum([U0])
    dproj, d_gains = run(_dproj_assemble, proj, d_a, dqs, dks, dvs, gains, tabs, phases=[("scatter", last), ("join", [U0])])
    ex.chip_sum(last)
    d_gains = _fold_heads(d_gains)[0].reshape(2, N_DIL, B_WIDTH)[:, :, :HEAD_DIM]
    ex.grad[A_IN] = run(_wgrad_col_sharded, "wgrad_in_ab", h0, [dproj], False, phases=[("join", last)])[0]
    ex.alone("swap_last", [("swap", [A_IN])])
    ex.pair_sum([A_IN])
    gx, d_abn = run(_dgrad_cols, "dgrad_in_ab", [dproj], [ex.full(A_IN)], 0, False, x, g_ab, g1, bf16_copy=False,
                    phases=[("scatter", [A_IN])])
    ex.chip_sum([A_IN])
    ex.alone("join_last", [("join", [A_IN])])

    small.update({
        "ab_norm_g": d_abn, "sgu_norm_g": d_slg, "sgu_norm_b": d_slb, "sgu_w": d_sw, "sgu_bias": d_sbt.T,
        "q_norm_g": d_gains[0], "k_norm_g": d_gains[1], "ffn_norm_g": jnp.concatenate([d_f0, d_f1], axis=0),
    })
    return loss, gx, small


SHARDED_SMALL = ("cd_norm_g", "conv_c_w", "conv_c_b", "c_ln_g", "c_ln_b", "conv_d_w")
SHARDED_ROWS = 48
REPLICATED_EARLY = ("sgu_norm_g", "sgu_norm_b", "sgu_w", "sgu_bias")
REPLICATED_EARLY_ROWS = 528
REPLICATED_LATE = ("ab_norm_g", "q_norm_g", "k_norm_g", "ffn_norm_g", "loss")
REPLICATED_LATE_ROWS = 32
REPLICATED_SMALL = REPLICATED_EARLY + REPLICATED_LATE[:-1]


def _pack_sharded(parts):
    rows = [parts[k].reshape(-1, LANES) for k in SHARDED_SMALL]
    return _pad_rows(jnp.concatenate(rows, axis=0), SHARDED_ROWS)


def _split_full_small(small):
    per_chip = []
    for q in range(N_CHIPS):
        parts = {}
        for k in SHARDED_SMALL:
            a = small[k]
            a = a.reshape(-1, a.shape[-1])
            n = a.shape[-1] // N_CHIPS
            parts[k] = a[:, q * n:(q + 1) * n]
        per_chip.append(_pack_sharded(parts))
    return jnp.stack(per_chip)


def _unpack_sharded(pack, shapes):
    out, r = {}, 0
    for k in SHARDED_SMALL:
        n = math.prod(shapes[k]) // LANES
        out[k] = pack[r:r + n].reshape(shapes[k])
        r += n
    return out


def _gathered_small(packs, shapes):
    per_chip = [_unpack_sharded(packs[q], shapes) for q in range(N_CHIPS)]
    return {k: jnp.concatenate([pc[k] for pc in per_chip], axis=-1) for k in SHARDED_SMALL}


def _pack_replicated(small, names, total_rows):
    rows = []
    for k in names:
        a = small[k].reshape(-1)
        a = jnp.pad(a, (0, (-a.shape[0]) % LANES))
        rows.append(a.reshape(-1, LANES))
    return _pad_rows(jnp.concatenate(rows, axis=0), total_rows)


def _unpack_replicated(pack, shapes, names):
    out, r = {}, 0
    for k in names:
        size = math.prod(shapes[k])
        n = -(-size // LANES)
        out[k] = pack[r:r + n].reshape(-1)[:size].reshape(shapes[k])
        r += n
    return out


WEIGHT_ORDER = ("ab_norm_g", "ab_w_in", "sgu_norm_g", "sgu_norm_b", "sgu_w", "sgu_bias", "q_norm_g", "k_norm_g", "ab_w_out",
                "cd_norm_g", "cd_w_in", "conv_c_w", "conv_c_b", "c_ln_g", "c_ln_b", "conv_d_w", "cd_w_out", "ffn_norm_g",
                "ffn_w_gate", "ffn_w_up", "ffn_w_down")


def kernel(x, ab_norm_g, ab_w_in, sgu_norm_g, sgu_norm_b, sgu_w, sgu_bias, q_norm_g, k_norm_g, ab_w_out, cd_norm_g, cd_w_in, conv_c_w, conv_c_b, c_ln_g, c_ln_b, conv_d_w, cd_w_out, ffn_norm_g, ffn_w_gate, ffn_w_up, ffn_w_down, loss_target, m_ab_norm_g, m_ab_w_in, m_sgu_norm_g, m_sgu_norm_b, m_sgu_w, m_sgu_bias, m_q_norm_g, m_k_norm_g, m_ab_w_out, m_cd_norm_g, m_cd_w_in, m_conv_c_w, m_conv_c_b, m_c_ln_g, m_c_ln_b, m_conv_d_w, m_cd_w_out, m_ffn_norm_g, m_ffn_w_gate, m_ffn_w_up, m_ffn_w_down, v_ab_norm_g, v_ab_w_in, v_sgu_norm_g, v_sgu_norm_b, v_sgu_w, v_sgu_bias, v_q_norm_g, v_k_norm_g, v_ab_w_out, v_cd_norm_g, v_cd_w_in, v_conv_c_w, v_conv_c_b, v_c_ln_g, v_c_ln_b, v_conv_d_w, v_cd_w_out, v_ffn_norm_g, v_ffn_w_gate, v_ffn_w_up, v_ffn_w_down):
    args = dict(locals())
    ws = {k: args[k] for k in WEIGHT_ORDER}
    ms = {k: args["m_" + k] for k in WEIGHT_ORDER}
    vs = {k: args["v_" + k] for k in WEIGHT_ORDER}
    small_names = [k for k in WEIGHT_ORDER if k not in BIG]
    t, d = x.shape[1:]

    for group in (ws, ms, vs):
        for k in ROWS_MINOR:
            group[k] = jnp.swapaxes(group[k], 1, 2)
    ex = _Exchange(enabled=True)
    for name, layer in UNITS:
        ex.w[(name, layer)] = _stage_own(f"stage_{name}_{layer}", ws[name], layer, BF16)
    own_small = _pack_sharded({k: ws[k][0] for k in SHARDED_SMALL})
    ex.w[SMALL_SHARDED] = _stage_own("stage_small", own_small[None], 0, F32)
    ex.alone("gather_first", [("send", [A_IN, SMALL_SHARDED])])
    ex.alone("gather_first_pass", [("pass", [A_IN, SMALL_SHARDED])])
    sp = _gathered_small(ex.w[SMALL_SHARDED].reshape(N_CHIPS, SHARDED_ROWS, LANES), {k: ws[k].shape[1:] for k in SHARDED_SMALL})
    for k in REPLICATED_SMALL:
        sp[k] = ws[k] if k == "ffn_norm_g" else ws[k][0]

    loss, grad_x, g_small = _local_step(x.reshape(t, d), loss_target.reshape(t, d), ex, sp)

    shapes = {k: ws[k].shape for k in REPLICATED_SMALL}
    shapes["loss"] = (1,)
    g_small["loss"] = loss
    late = _all_reduce_small(_pack_replicated(g_small, REPLICATED_LATE, REPLICATED_LATE_ROWS))
    grad = _unpack_sharded(ex.done[SMALL_SHARDED].reshape(SHARDED_ROWS, LANES), {k: ws[k].shape for k in SHARDED_SMALL})
    grad.update(_unpack_replicated(ex.done[REPLICATED_UNIT].reshape(REPLICATED_EARLY_ROWS, LANES), shapes, REPLICATED_EARLY))
    grad.update(_unpack_replicated(late, shapes, REPLICATED_LATE))
    loss = grad.pop("loss")[0]

    delta, new_m, new_v = {}, {}, {}
    for k in BIG:
        g_layers = [ex.done[(k, layer)] for layer in range(ws[k].shape[0])]
        outs = _adamw_big("adamw_" + k, ws[k], g_layers, ms[k], vs[k])
        if k in ROWS_MINOR:
            outs = [jnp.swapaxes(o, 1, 2) for o in outs]
        grad[k], delta[k], new_m[k], new_v[k] = outs
    d_s, m_s, v_s = _adamw_small([ws[k] for k in small_names], [grad[k] for k in small_names],
                                 [ms[k] for k in small_names], [vs[k] for k in small_names])
    for j, k in enumerate(small_names):
        delta[k], new_m[k], new_v[k] = d_s[j], m_s[j], v_s[j]

    return (loss, grad_x[None], *[grad[k] for k in WEIGHT_ORDER], *[delta[k] for k in WEIGHT_ORDER],
            *[new_m[k] for k in WEIGHT_ORDER], *[new_v[k] for k in WEIGHT_ORDER])
```

```python
import functools
import math

import jax
import jax.numpy as jnp
from jax import lax
from jax.experimental import pallas as pl
from jax.experimental.pallas import tpu as pltpu

F32 = jnp.float32
BF16 = jnp.bfloat16
SDS = jax.ShapeDtypeStruct

N_CHIPS = 4
EPS = 1e-6
NEG_INF = -1e30
CHUNK = 128
A_GROUPS = 4
A_WIDTH = 512
N_DIL = 3
DIL_RATES = (1, 4, 16)
HEAD_DIM = 64
B_WIDTH = 512
ROPE_DIM = 16
ROPE_THETA = 500000.0
C_WIDTH = 512
C_KERNEL = 31
D_KERNEL = 3
HALO = 32
ATT_BLOCK = 128
LANES = 128

ADAM_LR = 0.001
ADAM_B1 = 0.9
ADAM_B2 = 0.999
ADAM_EPS = 1e-08
ADAM_WD = 0.01
ADAM_STEP = 10

VMEM_LIMIT = 56 * 1024 * 1024

NN = (((1,), (0,)), ((), ()))
NT = (((1,), (1,)), ((), ()))
TN = (((0,), (0,)), ((), ()))

TILES = {"proj_in": 1024, "proj_out": 1024, "ffn_in": 1024, "ffn_out": 512, "ffn_dact": 512, "dgrad_cols": 512,
         "dgrad_rows": 1024, "wgrad": 4096}


def _params(sem=None):
    return pltpu.CompilerParams(dimension_semantics=sem, vmem_limit_bytes=VMEM_LIMIT)


def _bf(v):
    return v if v.dtype == BF16 else v.astype(BF16)


def _dot(a, b, dims):
    return lax.dot_general(_bf(a), _bf(b), dims, preferred_element_type=F32)


def _dot_hi(a, b):
    return jnp.dot(a, b, precision=lax.Precision.HIGHEST, preferred_element_type=F32)


def _sigmoid(v):
    return 0.5 * jnp.tanh(0.5 * v) + 0.5


def _gelu(v):
    return 0.5 * v * (1.0 + lax.erf(v * (1.0 / math.sqrt(2.0))))


def _gelu_grad(v):
    cdf = 0.5 * (1.0 + lax.erf(v * (1.0 / math.sqrt(2.0))))
    return cdf + v * jnp.exp(-0.5 * v * v) * (1.0 / math.sqrt(2.0 * math.pi))


def _segment_mean_matrix(seg, scale=None):
    r = lax.broadcasted_iota(jnp.int32, (LANES, LANES), 0) // seg
    c = lax.broadcasted_iota(jnp.int32, (LANES, LANES), 1) // seg
    return jnp.where(r == c, (1.0 / seg) if scale is None else scale, 0.0).astype(BF16)


def _segment_dot(v, seg):
    hi = v.astype(BF16)
    lo = (v - hi.astype(F32)).astype(BF16)
    return jnp.dot(hi, seg, preferred_element_type=F32) + jnp.dot(lo, seg, preferred_element_type=F32)


MESH = pl.DeviceIdType.MESH
ANY = pl.BlockSpec(memory_space=pl.ANY)


def _position():
    x, y, c = lax.axis_index("x"), lax.axis_index("y"), lax.axis_index("c")
    others = [(1 - x, y), (x, 1 - y), (1 - x, 1 - y)]
    return x, y, c, 2 * x + y, others


class _Ride:
    def __init__(self, ins, bufs, new_outs, sem_shapes, start, finish):
        self.ins, self.bufs, self.new_outs, self.sem_shapes = list(ins), list(bufs), list(new_outs), list(sem_shapes)
        self.start, self.finish = start, finish


def _ride_both(a, b):
    na = (len(a.ins), len(a.bufs), len(a.new_outs), len(a.sem_shapes))

    def split(ins, bufs, new, sems):
        return ((ins[:na[0]], bufs[:na[1]], new[:na[2]], sems[:na[3]]), (ins[na[0]:], bufs[na[1]:], new[na[2]:], sems[na[3]:]))

    def start(*refs):
        ra, rb = split(*refs)
        a.start(*ra)
        b.start(*rb)

    def finish(*refs):
        ra, rb = split(*refs)
        a.finish(*ra)
        b.finish(*rb)

    return _Ride(a.ins + b.ins, a.bufs + b.bufs, a.new_outs + b.new_outs, a.sem_shapes + b.sem_shapes, start, finish)


def _call(body, *, grid, in_specs, out_specs, out_shape, operands, name, scratch_shapes=(), aliases=None, ride=None):
    if ride is None:
        return pl.pallas_call(body, grid=grid, in_specs=in_specs, out_specs=out_specs, out_shape=out_shape,
                              scratch_shapes=list(scratch_shapes), input_output_aliases=aliases or {}, name=name,
                              compiler_params=_params())(*operands)
    multi = isinstance(out_shape, (list, tuple))
    out_shapes = list(out_shape) if multi else [out_shape]
    o_specs = list(out_specs) if multi else [out_specs]
    n_in, n_out, n_scr = len(operands), len(out_shapes), len(scratch_shapes)
    n_ri, n_rb, n_rn = len(ride.ins), len(ride.bufs), len(ride.new_outs)

    def carrying(*refs):
        k = n_in
        r_ins = refs[k:k + n_ri]
        k += n_ri + n_rb
        outs = refs[k:k + n_out]
        k += n_out
        r_bufs = refs[k:k + n_rb]
        k += n_rb
        r_new = refs[k:k + n_rn]
        k += n_rn
        scratch = refs[k:k + n_scr]
        sems = refs[k + n_scr:]
        first, last = None, None
        for axis, size in enumerate(grid):
            pid = pl.program_id(axis)
            first = (pid == 0) if first is None else first & (pid == 0)
            last = (pid == size - 1) if last is None else last & (pid == size - 1)

        @pl.when(first)
        def _():
            ride.start(r_ins, r_bufs, r_new, sems)

        body(*refs[:n_in], *outs, *scratch)

        @pl.when(last)
        def _():
            ride.finish(r_ins, r_bufs, r_new, sems)

    all_aliases = dict(aliases or {})
    for j in range(n_rb):
        all_aliases[n_in + n_ri + j] = n_out + j
    res = pl.pallas_call(
        carrying, grid=grid, in_specs=list(in_specs) + [ANY] * (n_ri + n_rb), out_specs=o_specs + [ANY] * (n_rb + n_rn),
        out_shape=out_shapes + [SDS(b.shape, b.dtype) for b in ride.bufs] + ride.new_outs,
        scratch_shapes=list(scratch_shapes) + [pltpu.SemaphoreType.DMA(s) for s in ride.sem_shapes],
        input_output_aliases=all_aliases, name=name, compiler_params=_params())(*operands, *ride.ins, *ride.bufs)
    outs = res[:n_out]
    return (list(outs) if multi else outs[0]), list(res[n_out:])


def _run_ride(name, ride):
    n_ri, n_rb, n_rn = len(ride.ins), len(ride.bufs), len(ride.new_outs)

    def body(*refs):
        r_ins = refs[:n_ri]
        r_bufs = refs[n_ri + n_rb:n_ri + 2 * n_rb]
        r_new = refs[n_ri + 2 * n_rb:n_ri + 2 * n_rb + n_rn]
        sems = refs[n_ri + 2 * n_rb + n_rn:]
        ride.start(r_ins, r_bufs, r_new, sems)
        ride.finish(r_ins, r_bufs, r_new, sems)

    return list(pl.pallas_call(
        body, in_specs=[ANY] * (n_ri + n_rb), out_specs=[ANY] * (n_rb + n_rn),
        out_shape=[SDS(b.shape, b.dtype) for b in ride.bufs] + ride.new_outs,
        scratch_shapes=[pltpu.SemaphoreType.DMA(s) for s in ride.sem_shapes],
        input_output_aliases={n_ri + j: j for j in range(n_rb)}, name=name)(*ride.ins, *ride.bufs))


def _whole(ref, p):
    return ref[...]


def _slab(ref, p):
    return ref[p]


def _matmul(name, grid, pairs, extras, outs, dims, epi, *, slabs=1, n_acc=1, ride=None):
    n_pairs, n_ex, n_out = len(pairs), len(extras), len(outs)

    def body(*refs):
        ab = refs[:2 * n_pairs]
        ex = refs[2 * n_pairs:2 * n_pairs + n_ex]
        out_refs = refs[2 * n_pairs + n_ex:2 * n_pairs + n_ex + n_out]
        pids = tuple(pl.program_id(a) for a in range(len(grid)))
        parts = [None] * n_acc
        for p in range(slabs):
            for j, (_, _, a_pick, _, _, b_pick, acc) in enumerate(pairs):
                d = _dot(a_pick(ab[2 * j], p), b_pick(ab[2 * j + 1], p), dims)
                parts[acc] = d if parts[acc] is None else parts[acc] + d
        epi(parts, ex, out_refs, pids)

    operands, in_specs = [], []
    for a, a_spec, _, b, b_spec, _, _ in pairs:
        operands += [a, b]
        in_specs += [a_spec, b_spec]
    for e, e_spec in extras:
        operands.append(e)
        in_specs.append(e_spec)
    return _call(body, grid=grid, in_specs=in_specs, out_specs=[o[1] for o in outs], out_shape=[o[0] for o in outs],
                 operands=operands, name=name, ride=ride)


def _rms_rows(v, g):
    r = lax.rsqrt(jnp.mean(v * v, axis=-1, keepdims=True) + EPS)
    return v * r * g


def _rms_fwd(name, x, g):
    t, d = x.shape
    tm = 512

    def body(x_ref, g_ref, o_ref):
        o_ref[...] = _rms_rows(x_ref[...], g_ref[...]).astype(BF16)

    return pl.pallas_call(
        body, grid=(t // tm,),
        in_specs=[pl.BlockSpec((tm, d), lambda i: (i, 0)), pl.BlockSpec((1, d), lambda i: (0, 0))],
        out_specs=pl.BlockSpec((tm, d), lambda i: (i, 0)), out_shape=SDS((t, d), BF16), name=name,
        compiler_params=_params())(x, g)


def _epi_residual_norm(accs, ex, outs, pids):
    x_new = accs[0] + ex[0][...]
    outs[0][...] = x_new
    outs[1][...] = _rms_rows(x_new, ex[1][...]).astype(BF16)


def _epi_residual_loss(accs, ex, outs, pids):
    y = accs[0] + ex[0][...]
    err = y - ex[1][...]
    dy = err * (1.0 / err.shape[-1])
    outs[0][...] = dy
    outs[2][...] = dy.astype(BF16)

    @pl.when(pids[0] == 0)
    def _():
        outs[1][...] = jnp.zeros_like(outs[1])

    outs[1][...] += jnp.sum(err * err)


def _epi_rms_bwd(accs, ex, outs, pids):
    dh = accs[0]
    xv, g, res = ex[0][...], ex[1][...], ex[2][...]
    r = lax.rsqrt(jnp.mean(xv * xv, axis=-1, keepdims=True) + EPS)
    xh = xv * r
    dy = dh * g
    dx = res + r * (dy - xh * jnp.mean(dy * xh, axis=-1, keepdims=True))
    outs[0][...] = dx
    if len(outs) > 2:
        outs[2][...] = dx.astype(BF16)

    @pl.when(pids[0] == 0)
    def _():
        outs[1][...] = jnp.zeros_like(outs[1])

    outs[1][...] += jnp.sum(dh * xh, axis=0, keepdims=True)


def _row_spec(tm, d):
    return pl.BlockSpec((tm, d), lambda i, *_: (i, 0))


def _const_spec(shape):
    nd = len(shape)
    return pl.BlockSpec(shape, lambda *_: (0,) * nd)


def _proj_in(name, h, w, layer, ride=None):
    t, d = h.shape
    n4 = w.shape[-1]
    tm = TILES["proj_in"]

    def epi(accs, ex, outs, pids):
        outs[0][...] = accs[0].astype(BF16)

    res = _matmul(
        name, (N_CHIPS, t // tm),
        [(h, pl.BlockSpec((tm, d), lambda p, i: (i, 0)), _whole,
          w, pl.BlockSpec((None, None, d, n4), lambda p, i: (p, layer, 0, 0)), _whole, 0)],
        [], [(SDS((t, N_CHIPS * n4), BF16), pl.BlockSpec((tm, n4), lambda p, i: (i, p)))],
        NN, epi, ride=ride)
    return res[0] if ride is None else (res[0][0], res[1])


def _proj_out(name, a, w, x, g_next=None, target=None):
    t, k = a.shape
    d = w.shape[-1]
    tm = TILES["proj_out"]
    if target is None:
        extras = [(x, _row_spec(tm, d)), (g_next, _const_spec((1, d)))]
        outs = [(SDS((t, d), F32), _row_spec(tm, d)), (SDS((t, d), BF16), _row_spec(tm, d))]
        epi = _epi_residual_norm
    else:
        extras = [(x, _row_spec(tm, d)), (target, _row_spec(tm, d))]
        outs = [(SDS((t, d), F32), _row_spec(tm, d)), (SDS((8, LANES), F32), _const_spec((8, LANES))),
                (SDS((t, d), BF16), _row_spec(tm, d))]
        epi = _epi_residual_loss
    return _matmul(name, (t // tm,), [(a, _row_spec(tm, k), _whole, w, _const_spec((k, d)), _whole, 0)], extras, outs, NN, epi)


def _ffn_in(name, h, wg, wu, layer, ride=None):
    t, d = h.shape
    n4 = wg.shape[-2]
    tm = TILES["ffn_in"]

    def epi(accs, ex, outs, pids):
        gate, up = accs
        s = _sigmoid(gate)
        silu = gate * s
        outs[0][...] = (up * (s + silu - silu * s)).astype(BF16)
        outs[1][...] = silu.astype(BF16)
        outs[2][...] = (silu * up).astype(BF16)

    w_spec = pl.BlockSpec((None, None, n4, d), lambda p, i: (p, layer, 0, 0))
    h_spec = pl.BlockSpec((tm, d), lambda p, i: (i, 0))
    o = (SDS((N_CHIPS, t, n4), BF16), pl.BlockSpec((None, tm, n4), lambda p, i: (p, i, 0)))
    return _matmul(name, (N_CHIPS, t // tm),
                   [(h, h_spec, _whole, wg, w_spec, _whole, 0), (h, h_spec, _whole, wu, w_spec, _whole, 1)], [],
                   [o, o, o], NT, epi, n_acc=2, ride=ride)


def _ffn_out(name, act, wd, layer, x, g_next=None, target=None, ride=None):
    _, t, n4 = act.shape
    d = wd.shape[-1]
    tm = TILES["ffn_out"]
    xs = _row_spec(tm, d)
    if target is None:
        extras = [(x, xs), (g_next, _const_spec((1, d)))]
        outs = [(SDS((t, d), F32), xs), (SDS((t, d), BF16), xs)]
        epi = _epi_residual_norm
    else:
        extras = [(x, xs), (target, xs)]
        outs = [(SDS((t, d), F32), xs), (SDS((8, LANES), F32), _const_spec((8, LANES))), (SDS((t, d), BF16), xs)]
        epi = _epi_residual_loss
    return _matmul(
        name, (t // tm,),
        [(act, pl.BlockSpec((N_CHIPS, tm, n4), lambda i: (0, i, 0)), _slab,
          wd, pl.BlockSpec((N_CHIPS, None, n4, d), lambda i: (0, layer, 0, 0)), _slab, 0)],
        extras, outs, NN, epi, slabs=N_CHIPS, ride=ride)


def _ffn_dact(name, g, wd, layer, gate, up, ride=None):
    t, d = g.shape
    n4 = wd.shape[-2]
    tm = TILES["ffn_dact"]

    def body(g_ref, w_ref, gate_ref, up_ref, dgate_ref, dup_ref):
        gv = g_ref[...]
        for p in range(N_CHIPS):
            dact = _dot(gv, w_ref[p], NT)
            dgate_ref[p] = (dact * gate_ref[p].astype(F32)).astype(BF16)
            dup_ref[p] = (dact * up_ref[p].astype(F32)).astype(BF16)

    blk = pl.BlockSpec((N_CHIPS, tm, n4), lambda i: (0, i, 0))
    return _call(
        body, grid=(t // tm,),
        in_specs=[_row_spec(tm, d), pl.BlockSpec((N_CHIPS, None, n4, d), lambda i: (0, layer, 0, 0)), blk, blk],
        out_specs=[blk, blk], out_shape=[SDS((N_CHIPS, t, n4), BF16)] * 2, operands=[g, wd, gate, up], name=name, ride=ride)


def _copy_epi(accs, ex, outs, pids):
    for a, o in zip(accs, outs):
        o[...] = a.astype(o.dtype)


def _dgrad_cols(name, dz_list, w_list, layer, three_d, x, g, res, bf16_copy=True, w_rows=False, ride=None):
    t, d = x.shape
    n4 = w_list[0].shape[-2 if w_rows else -1]
    tm = TILES["dgrad_cols"]
    if three_d:
        zs, z_pick = pl.BlockSpec((N_CHIPS, tm, n4), lambda i: (0, i, 0)), _slab
    else:
        zs, z_pick = _row_spec(tm, N_CHIPS * n4), (lambda ref, p: ref[:, p * n4:(p + 1) * n4])
    ws = pl.BlockSpec((N_CHIPS, None) + ((n4, d) if w_rows else (d, n4)), lambda i: (0, layer, 0, 0))
    xs = _row_spec(tm, d)
    return _matmul(
        name, (t // tm,), [(dz, zs, z_pick, w, ws, _slab, 0) for dz, w in zip(dz_list, w_list)],
        [(x, xs), (g, _const_spec((1, d))), (res, xs)],
        [(SDS((t, d), F32), xs), (SDS((1, d), F32), _const_spec((1, d)))] + ([(SDS((t, d), BF16), xs)] if bf16_copy else []),
        NN if w_rows else NT, _epi_rms_bwd, slabs=N_CHIPS, ride=ride)


def _dgrad_rows(name, g, w):
    t, d = g.shape
    k = w.shape[0]
    tm = TILES["dgrad_rows"]
    return _matmul(name, (t // tm,), [(g, _row_spec(tm, d), _whole, w, _const_spec((k, d)), _whole, 0)], [],
                   [(SDS((t, k), F32), _row_spec(tm, k))], NT, _copy_epi)[0]


A_TILE = 256


def _a_common(p_ref, lg_ref, lb_ref):
    pv = p_ref[...].astype(F32)
    a = _gelu(pv)
    u, v = a[:, :A_WIDTH], a[:, A_WIDTH:]
    vc = v - jnp.mean(v, axis=-1, keepdims=True)
    rs = lax.rsqrt(jnp.mean(vc * vc, axis=-1, keepdims=True) + EPS)
    vhat = vc * rs
    vn = vhat * lg_ref[...] + lb_ref[...]
    return pv, u, vhat, rs, vn.astype(BF16)


def _tril_weights(w_ref, g):
    r = lax.broadcasted_iota(jnp.int32, (CHUNK, CHUNK), 0)
    c = lax.broadcasted_iota(jnp.int32, (CHUNK, CHUNK), 1)
    return jnp.where(c <= r, w_ref[g], 0.0).astype(BF16), c <= r


def _mixer_a_fwd(proj, lg, lb, w, bias_t):
    t = proj.shape[0]

    def body(p_ref, lg_ref, lb_ref, w_ref, bt_ref, o_ref):
        _, u, _, _, vnb = _a_common(p_ref, lg_ref, lb_ref)
        for g in range(A_GROUPS):
            wt, _ = _tril_weights(w_ref, g)
            cs = slice(g * CHUNK, (g + 1) * CHUNK)
            for ch in range(A_TILE // CHUNK):
                rs_ = slice(ch * CHUNK, (ch + 1) * CHUNK)
                mixed = _dot(wt, vnb[rs_, cs], NN) + bt_ref[:, g:g + 1]
                o_ref[rs_, cs] = (u[rs_, cs] * mixed).astype(BF16)

    return pl.pallas_call(
        body, grid=(t // A_TILE,),
        in_specs=[pl.BlockSpec((A_TILE, 2 * A_WIDTH), lambda i: (i, 0)), _const_spec((1, A_WIDTH)),
                  _const_spec((1, A_WIDTH)), _const_spec((A_GROUPS, CHUNK, CHUNK)), _const_spec((CHUNK, A_GROUPS))],
        out_specs=pl.BlockSpec((A_TILE, A_WIDTH), lambda i: (i, 0)), out_shape=SDS((t, A_WIDTH), BF16),
        name="mixer_a_fwd", compiler_params=_params())(proj, lg, lb, w, bias_t)


def _mixer_a_bwd(proj, dcat, lg, lb, w, bias_t):
    t = proj.shape[0]

    def body(p_ref, da_ref, lg_ref, lb_ref, w_ref, bt_ref, dp_ref, dw_ref, dbt_ref, dlg_ref, dlb_ref, du_scr, dvn_scr):
        @pl.when(pl.program_id(0) == 0)
        def _():
            dw_ref[...] = jnp.zeros_like(dw_ref)
            dbt_ref[...] = jnp.zeros_like(dbt_ref)
            dlg_ref[...] = jnp.zeros_like(dlg_ref)
            dlb_ref[...] = jnp.zeros_like(dlb_ref)

        pv, u, vhat, rs, vnb = _a_common(p_ref, lg_ref, lb_ref)
        da = da_ref[...]
        for g in range(A_GROUPS):
            wt, keep = _tril_weights(w_ref, g)
            cs = slice(g * CHUNK, (g + 1) * CHUNK)
            for ch in range(A_TILE // CHUNK):
                rs_ = slice(ch * CHUNK, (ch + 1) * CHUNK)
                vg = vnb[rs_, cs]
                mixed = _dot(wt, vg, NN) + bt_ref[:, g:g + 1]
                du_scr[rs_, cs] = da[rs_, cs] * mixed
                dmx = da[rs_, cs] * u[rs_, cs]
                dw_ref[g] += jnp.where(keep, _dot(dmx, vg, NT), 0.0)
                dvn_scr[rs_, cs] = _dot(wt, dmx, TN)
                dbt_ref[:, g:g + 1] += jnp.sum(dmx, axis=1, keepdims=True)
        dvn = dvn_scr[...]
        dlg_ref[...] += jnp.sum(dvn * vhat, axis=0, keepdims=True)
        dlb_ref[...] += jnp.sum(dvn, axis=0, keepdims=True)
        dvh = dvn * lg_ref[...]
        dv = rs * (dvh - jnp.mean(dvh, axis=-1, keepdims=True) - vhat * jnp.mean(dvh * vhat, axis=-1, keepdims=True))
        gp = _gelu_grad(pv)
        dp_ref[:, :A_WIDTH] = (du_scr[...] * gp[:, :A_WIDTH]).astype(BF16)
        dp_ref[:, A_WIDTH:] = (dv * gp[:, A_WIDTH:]).astype(BF16)

    return pl.pallas_call(
        body, grid=(t // A_TILE,),
        in_specs=[pl.BlockSpec((A_TILE, 2 * A_WIDTH), lambda i: (i, 0)), pl.BlockSpec((A_TILE, A_WIDTH), lambda i: (i, 0)),
                  _const_spec((1, A_WIDTH)), _const_spec((1, A_WIDTH)), _const_spec((A_GROUPS, CHUNK, CHUNK)),
                  _const_spec((CHUNK, A_GROUPS))],
        out_specs=[pl.BlockSpec((A_TILE, 2 * A_WIDTH), lambda i: (i, 0)), _const_spec((A_GROUPS, CHUNK, CHUNK)),
                   _const_spec((CHUNK, A_GROUPS)), _const_spec((1, A_WIDTH)), _const_spec((1, A_WIDTH))],
        out_shape=[SDS((t, 2 * A_WIDTH), BF16), SDS((A_GROUPS, CHUNK, CHUNK), F32), SDS((CHUNK, A_GROUPS), F32),
                   SDS((1, A_WIDTH), F32), SDS((1, A_WIDTH), F32)],
        scratch_shapes=[pltpu.VMEM((A_TILE, A_WIDTH), F32), pltpu.VMEM((A_TILE, A_WIDTH), F32)],
        name="mixer_a_bwd", compiler_params=_params())(proj, dcat, lg, lb, w, bias_t)


def _rope_tables(t):
    half = ROPE_DIM // 2
    inv_freq = ROPE_THETA ** (-jnp.arange(half, dtype=F32) * 2.0 / ROPE_DIM)
    ang = jnp.arange(t, dtype=F32)[:, None] * inv_freq[None, :]
    cos, sin = jnp.cos(ang), jnp.sin(ang)
    one = jnp.ones((t, HEAD_DIM - ROPE_DIM), F32)
    zero = jnp.zeros((t, HEAD_DIM - ROPE_DIM), F32)
    zh = jnp.zeros((t, half), F32)
    c = jnp.concatenate([cos, cos, one], axis=1)
    s1 = jnp.concatenate([-sin, zh, zero], axis=1)
    s2 = jnp.concatenate([zh, sin, zero], axis=1)
    return tuple(jnp.tile(a, (1, LANES // HEAD_DIM)) for a in (c, s1, s2))


QK_TILE = 512
QK_ROWS = 64
QK_COLS = 2 * N_DIL * B_WIDTH


CHUNKS = B_WIDTH // LANES


def _regroup_out(scr, first, out_ref, rate, tile):
    rows = tile // rate
    for rho in range(rate):
        for c in range(CHUNKS):
            out_ref[rho, :, c * LANES:(c + 1) * LANES] = scr[first + c, pl.ds(rho, rows, stride=rate), :].astype(out_ref.dtype)


def _regroup_in(x_ref, scr, rate, tile):
    rows = tile // rate
    for rho in range(rate):
        for c in range(CHUNKS):
            scr[c, pl.ds(rho, rows, stride=rate), :] = x_ref[rho, :, c * LANES:(c + 1) * LANES].astype(F32)


def _regrouped_spec(rate, tile):
    return pl.BlockSpec((rate, tile // rate, B_WIDTH), lambda i, *_: (0, i, 0))


def _qk_fwd(proj, gains, tabs, ride=None):
    t = proj.shape[0]
    col0 = 2 * A_WIDTH // 1024
    r1, r2 = DIL_RATES[1], DIL_RATES[2]

    def body(p_ref, g_ref, c_ref, s1_ref, s2_ref, o_ref, q1_ref, q2_ref, k1_ref, k2_ref, scr):
        seg = _segment_mean_matrix(HEAD_DIM)
        for r0 in range(0, QK_TILE, QK_ROWS):
            rows = slice(r0, r0 + QK_ROWS)
            c, s1, s2 = c_ref[rows, :], s1_ref[rows, :], s2_ref[rows, :]
            for ci in range(1024 // LANES):
                ls = slice(ci * LANES, (ci + 1) * LANES)
                xv = p_ref[rows, ls].astype(F32)
                r = lax.rsqrt(_segment_dot(xv * xv, seg) + EPS)
                y = xv * r * g_ref[:, ls]
                val = y * c + pltpu.roll(y, LANES - 8, axis=1) * s1 + pltpu.roll(y, 8, axis=1) * s2
                o_ref[rows, ls] = val.astype(BF16)
                scr[ci, rows, :] = val

        j = pl.program_id(1)

        @pl.when(j == 0)
        def _():
            _regroup_out(scr, CHUNKS, q1_ref, r1, QK_TILE)

        @pl.when(j == 1)
        def _():
            _regroup_out(scr, 0, q2_ref, r2, QK_TILE)

        @pl.when(j == 2)
        def _():
            _regroup_out(scr, 0, k1_ref, r1, QK_TILE)
            _regroup_out(scr, CHUNKS, k2_ref, r2, QK_TILE)

    tab = pl.BlockSpec((QK_TILE, LANES), lambda i, j: (i, 0))
    g1, g2 = SDS((r1, t // r1, B_WIDTH), BF16), SDS((r2, t // r2, B_WIDTH), BF16)
    s1_, s2_ = _regrouped_spec(r1, QK_TILE), _regrouped_spec(r2, QK_TILE)
    return _call(
        body, grid=(t // QK_TILE, QK_COLS // 1024),
        in_specs=[pl.BlockSpec((QK_TILE, 1024), lambda i, j: (i, col0 + j)), pl.BlockSpec((1, 1024), lambda i, j: (0, j)),
                  tab, tab, tab],
        out_specs=[pl.BlockSpec((QK_TILE, 1024), lambda i, j: (i, j)), s1_, s2_, s1_, s2_],
        out_shape=[SDS((t, QK_COLS), BF16), g1, g2, g1, g2],
        scratch_shapes=[pltpu.VMEM((2 * CHUNKS, QK_TILE, LANES), F32)],
        operands=[proj, gains, *tabs], name="qk_norm_rope_fwd", ride=ride)


PERM_TILE = 512


def _permute(name, items, rate):
    t = items[0][0].shape[0]
    n = len(items)

    def body(*refs):
        scr = refs[-1]
        for x_ref, o_ref in zip(refs[:n], refs[n:2 * n]):
            for ci in range(CHUNKS):
                scr[ci] = x_ref[:, ci * LANES:(ci + 1) * LANES].astype(F32)
            _regroup_out(scr, 0, o_ref, rate, PERM_TILE)

    return pl.pallas_call(
        body, grid=(t // PERM_TILE,),
        in_specs=[pl.BlockSpec((PERM_TILE, B_WIDTH), functools.partial(lambda cb, i: (i, cb), cb)) for _, cb in items],
        out_specs=[_regrouped_spec(rate, PERM_TILE) for _ in items],
        out_shape=[SDS((rate, t // rate, B_WIDTH), a.dtype) for a, _ in items],
        scratch_shapes=[pltpu.VMEM((CHUNKS, PERM_TILE, LANES), F32)],
        name=name, compiler_params=_params())(*[a for a, _ in items])


def _head_lane_mask(h):
    lane = lax.broadcasted_iota(jnp.int32, (1, LANES), 1)
    return (lane < HEAD_DIM) if h == 0 else (lane >= HEAD_DIM)


def _attn_fwd(name, q, k, v, ride=None):
    rate, length = q[0].shape[0], q[0].shape[1]
    nb = length // ATT_BLOCK
    scale = HEAD_DIM ** -0.5

    def body(q_ref, kc_ref, kp_ref, vc_ref, vp_ref, o_ref, l_ref):
        n = pl.program_id(1)
        qi = lax.broadcasted_iota(jnp.int32, (ATT_BLOCK, 2 * ATT_BLOCK), 0)
        cj = lax.broadcasted_iota(jnp.int32, (ATT_BLOCK, 2 * ATT_BLOCK), 1)
        has_prev = jnp.where(n > 0, 0, 2 * ATT_BLOCK)
        mask = ((cj < ATT_BLOCK) & (cj >= qi + has_prev)) | ((cj >= ATT_BLOCK) & (cj - ATT_BLOCK <= qi))
        heads = [(hp, h) for hp in range(CHUNKS) for h in range(2)]
        q2, k2, v2 = {}, {}, {}
        for hp in range(CHUNKS):
            ls = slice(hp * LANES, (hp + 1) * LANES)
            q2[hp] = q_ref[:, ls]
            k2[hp] = jnp.concatenate([kp_ref[:, ls], kc_ref[:, ls]], axis=0)
            v2[hp] = jnp.concatenate([vp_ref[:, ls], vc_ref[:, ls]], axis=0)
        scores = {}
        for hp, h in heads:
            scores[hp, h] = _dot(jnp.where(_head_lane_mask(h), q2[hp], jnp.zeros_like(q2[hp])), k2[hp], NT) * scale
        probs, lses = {}, {}
        for hp, h in heads:
            s = jnp.where(mask, scores[hp, h], NEG_INF)
            m = jnp.max(s, axis=1, keepdims=True)
            p = jnp.exp(s - m)
            den = jnp.sum(p, axis=1, keepdims=True)
            lses[hp, h] = m + jnp.log(den)
            probs[hp, h] = (p / den).astype(BF16)
        for hp in range(CHUNKS):
            ls = slice(hp * LANES, (hp + 1) * LANES)
            o_acc = None
            for h in range(2):
                o = _dot(probs[hp, h], jnp.where(_head_lane_mask(h), v2[hp], jnp.zeros_like(v2[hp])), NN)
                o_acc = o if o_acc is None else o_acc + o
            o_ref[:, ls] = o_acc
            zeros = jnp.zeros((ATT_BLOCK, LANES), F32)
            l_ref[:, ls] = jnp.where(_head_lane_mask(1), lses[hp, 1] + zeros, lses[hp, 0] + zeros)

    def cur(cb):
        return pl.BlockSpec((None, ATT_BLOCK, B_WIDTH), lambda r, n: (r, n, cb))

    def prev(cb):
        return pl.BlockSpec((None, ATT_BLOCK, B_WIDTH), lambda r, n: (r, jnp.maximum(n - 1, 0), cb))

    out = pl.BlockSpec((None, ATT_BLOCK, B_WIDTH), lambda r, n: (r, n, 0))
    return _call(
        body, grid=(rate, nb),
        in_specs=[cur(q[1]), cur(k[1]), prev(k[1]), cur(v[1]), prev(v[1])],
        out_specs=[out, out], out_shape=[SDS((rate, length, B_WIDTH), F32)] * 2,
        operands=[q[0], k[0], k[0], v[0], v[0]], name=name, ride=ride)


def _attn_merge(a_out, o_list, l_list):
    t = a_out.shape[0]
    tm = PERM_TILE
    r1, r2 = DIL_RATES[1], DIL_RATES[2]

    def body(a_ref, o0, o1, o2, l0, l1, l2, cat_ref, lt_ref, lt1_ref, lt2_ref, so1, so2, sl1, sl2, slt):
        _regroup_in(o1, so1, r1, tm)
        _regroup_in(l1, sl1, r1, tm)
        _regroup_in(o2, so2, r2, tm)
        _regroup_in(l2, sl2, r2, tm)
        cat_ref[:, :A_WIDTH] = a_ref[...]
        for c in range(CHUNKS):
            ls = slice(c * LANES, (c + 1) * LANES)
            lg = [l0[:, ls], sl1[c], sl2[c]]
            m = jnp.maximum(jnp.maximum(lg[0], lg[1]), lg[2])
            es = [jnp.exp(l - m) for l in lg]
            den = es[0] + es[1] + es[2]
            b = (es[0] * o0[:, ls] + es[1] * so1[c] + es[2] * so2[c]) / den
            cat_ref[:, A_WIDTH + c * LANES:A_WIDTH + (c + 1) * LANES] = b.astype(BF16)
            lt = m + jnp.log(den)
            lt_ref[:, ls] = lt
            slt[c] = lt
        _regroup_out(slt, 0, lt1_ref, r1, tm)
        _regroup_out(slt, 0, lt2_ref, r2, tm)

    blk = _row_spec(tm, B_WIDTH)
    g1, g2 = _regrouped_spec(r1, tm), _regrouped_spec(r2, tm)
    return pl.pallas_call(
        body, grid=(t // tm,), in_specs=[blk, blk, g1, g2, blk, g1, g2],
        out_specs=[_row_spec(tm, A_WIDTH + B_WIDTH), blk, g1, g2],
        out_shape=[SDS((t, A_WIDTH + B_WIDTH), BF16), SDS((t, B_WIDTH), F32), SDS((r1, t // r1, B_WIDTH), F32),
                   SDS((r2, t // r2, B_WIDTH), F32)],
        scratch_shapes=[pltpu.VMEM((CHUNKS, tm, LANES), F32)] * 5,
        name="attn_merge", compiler_params=_params())(a_out, *o_list, *l_list)


def _attn_bwd_prep(dcat, cat):
    t = dcat.shape[0]
    tm = PERM_TILE
    r1, r2 = DIL_RATES[1], DIL_RATES[2]

    def body(d_ref, b_ref, db_ref, dd_ref, db1_ref, dd1_ref, db2_ref, dd2_ref, sdb, sdd):
        seg = _segment_mean_matrix(HEAD_DIM, scale=1.0)
        for c in range(CHUNKS):
            ls = slice(c * LANES, (c + 1) * LANES)
            d = d_ref[:, ls]
            dsum = _segment_dot(d * b_ref[:, ls].astype(F32), seg)
            db_ref[:, ls] = d.astype(BF16)
            dd_ref[:, ls] = dsum
            sdb[c] = d
            sdd[c] = dsum
        _regroup_out(sdb, 0, db1_ref, r1, tm)
        _regroup_out(sdd, 0, dd1_ref, r1, tm)
        _regroup_out(sdb, 0, db2_ref, r2, tm)
        _regroup_out(sdd, 0, dd2_ref, r2, tm)

    right = pl.BlockSpec((tm, B_WIDTH), lambda i: (i, 1))
    blk = _row_spec(tm, B_WIDTH)
    g1, g2 = _regrouped_spec(r1, tm), _regrouped_spec(r2, tm)
    return pl.pallas_call(
        body, grid=(t // tm,), in_specs=[right, right], out_specs=[blk, blk, g1, g1, g2, g2],
        out_shape=[SDS((t, B_WIDTH), BF16), SDS((t, B_WIDTH), F32), SDS((r1, t // r1, B_WIDTH), BF16),
                   SDS((r1, t // r1, B_WIDTH), F32), SDS((r2, t // r2, B_WIDTH), BF16), SDS((r2, t // r2, B_WIDTH), F32)],
        scratch_shapes=[pltpu.VMEM((CHUNKS, tm, LANES), F32)] * 2,
        name="attn_bwd_prep", compiler_params=_params())(dcat, cat)


def _attn_bwd(name, q, k, v, db, lse, dd, ride=None):
    rate, length = db.shape[0], db.shape[1]
    nb = length // ATT_BLOCK
    scale = HEAD_DIM ** -0.5

    def body(qa_ref, qb_ref, k_ref, v_ref, dba_ref, dbb_ref, la_ref, lb_ref, da_ref, dbd_ref, dq_ref, dk_ref, dv_ref, carry):
        m = pl.program_id(1)

        @pl.when(m == 0)
        def _():
            carry[...] = jnp.zeros_like(carry)

        row = lax.broadcasted_iota(jnp.int32, (2 * ATT_BLOCK, ATT_BLOCK), 0)
        kj = lax.broadcasted_iota(jnp.int32, (2 * ATT_BLOCK, ATT_BLOCK), 1)
        no_next = jnp.where(m + 1 < nb, 0, 2 * ATT_BLOCK)
        mask = ((row < ATT_BLOCK) & (kj <= row)) | ((row >= ATT_BLOCK) & (kj >= row - ATT_BLOCK + no_next))
        heads = [(hp, h) for hp in range(CHUNKS) for h in range(2)]
        q2, db2, lse2, dd2, k2, v2 = {}, {}, {}, {}, {}, {}
        for hp in range(CHUNKS):
            ls = slice(hp * LANES, (hp + 1) * LANES)
            k2[hp], v2[hp] = k_ref[:, ls], v_ref[:, ls]
            q2[hp] = jnp.concatenate([qa_ref[:, ls], qb_ref[:, ls]], axis=0)
            db2[hp] = jnp.concatenate([dba_ref[:, ls], dbb_ref[:, ls]], axis=0)
            lse2[hp] = jnp.concatenate([la_ref[:, ls], lb_ref[:, ls]], axis=0)
            dd2[hp] = jnp.concatenate([da_ref[:, ls], dbd_ref[:, ls]], axis=0)
        km, scores, dps = {}, {}, {}
        for hp, h in heads:
            hm = _head_lane_mask(h)
            km[hp, h] = jnp.where(hm, k2[hp], jnp.zeros_like(k2[hp]))
            scores[hp, h] = _dot(q2[hp], km[hp, h], NT) * scale
            dps[hp, h] = _dot(db2[hp], jnp.where(hm, v2[hp], jnp.zeros_like(v2[hp])), NT)
        probs, dss = {}, {}
        for hp, h in heads:
            hm = _head_lane_mask(h)
            lse_col = jnp.max(jnp.where(hm, lse2[hp], NEG_INF), axis=1, keepdims=True)
            dd_col = jnp.max(jnp.where(hm, dd2[hp], NEG_INF), axis=1, keepdims=True)
            p = jnp.where(mask, jnp.exp(scores[hp, h] - lse_col), 0.0)
            probs[hp, h] = p.astype(BF16)
            dss[hp, h] = (p * (dps[hp, h] - dd_col) * scale).astype(BF16)
        for hp in range(CHUNKS):
            ls = slice(hp * LANES, (hp + 1) * LANES)
            dq_acc, dk_acc, dv_acc = None, None, None
            for h in range(2):
                hm = _head_lane_mask(h)
                dvc = _dot(probs[hp, h], jnp.where(hm, db2[hp], jnp.zeros_like(db2[hp])), TN)
                dqc = _dot(dss[hp, h], km[hp, h], NN)
                dkc = _dot(dss[hp, h], jnp.where(hm, q2[hp], jnp.zeros_like(q2[hp])), TN)
                dq_acc = dqc if dq_acc is None else dq_acc + dqc
                dk_acc = dkc if dk_acc is None else dk_acc + dkc
                dv_acc = dvc if dv_acc is None else dv_acc + dvc
            dq_ref[:, ls] = (dq_acc[:ATT_BLOCK] + carry[:, ls]).astype(BF16)
            carry[:, ls] = dq_acc[ATT_BLOCK:]
            dk_ref[:, ls] = dk_acc.astype(BF16)
            dv_ref[:, ls] = dv_acc.astype(BF16)

    def cur(cb):
        return pl.BlockSpec((None, ATT_BLOCK, B_WIDTH), lambda r, n: (r, n, cb))

    def nxt(cb):
        return pl.BlockSpec((None, ATT_BLOCK, B_WIDTH), lambda r, n: (r, jnp.minimum(n + 1, nb - 1), cb))

    out = cur(0)
    return _call(
        body, grid=(rate, nb),
        in_specs=[cur(q[1]), nxt(q[1]), cur(k[1]), cur(v[1]), cur(0), nxt(0), cur(0), nxt(0), cur(0), nxt(0)],
        out_specs=[out, out, out], out_shape=[SDS((rate, length, B_WIDTH), BF16)] * 3,
        scratch_shapes=[pltpu.VMEM((ATT_BLOCK, B_WIDTH), F32)],
        operands=[q[0], q[0], k[0], v[0], db, db, lse, lse, dd, dd], name=name, ride=ride)


AB_IN = 2 * A_WIDTH + 3 * N_DIL * B_WIDTH
ASM_TILE = 256


def _dproj_assemble(proj, d_a, dq, dk, dv, gains, tabs, ride=None):
    t = proj.shape[0]
    n_in = 3 * N_DIL

    def body(p_ref, da_ref, *rest):
        grads = rest[:n_in]
        g_ref, c_ref, s1_ref, s2_ref, o_ref, dg_ref = rest[n_in:n_in + 6]
        scratch = rest[n_in + 6:]

        @pl.when(pl.program_id(0) == 0)
        def _():
            dg_ref[...] = jnp.zeros_like(dg_ref)

        chunk = {}
        k_scr = 0
        for j in range(n_in):
            g = j % N_DIL
            if DIL_RATES[g] == 1:
                for ci in range(CHUNKS):
                    chunk[j, ci] = functools.partial(lambda r, ci: r[:, ci * LANES:(ci + 1) * LANES].astype(F32), grads[j], ci)
            else:
                scr = scratch[k_scr]
                k_scr += 1
                _regroup_in(grads[j], scr, DIL_RATES[g], ASM_TILE)
                for ci in range(CHUNKS):
                    chunk[j, ci] = functools.partial(lambda s, ci: s[ci], scr, ci)

        seg = _segment_mean_matrix(HEAD_DIM)
        c, s1, s2 = c_ref[...], s1_ref[...], s2_ref[...]
        o_ref[:, :2 * A_WIDTH] = da_ref[...]
        for jg in range(2 * N_DIL):
            for ci in range(CHUNKS):
                col = jg * B_WIDTH + ci * LANES
                src = slice(2 * A_WIDTH + col, 2 * A_WIDTH + col + LANES)
                xv = p_ref[:, src].astype(F32)
                r = lax.rsqrt(_segment_dot(xv * xv, seg) + EPS)
                xh = xv * r
                gain = g_ref[:, col:col + LANES]
                do = chunk[jg, ci]()
                dy = do * c + pltpu.roll(do * s1, 8, axis=1) + pltpu.roll(do * s2, LANES - 8, axis=1)
                dg_ref[:, col:col + LANES] += jnp.sum(dy * xh, axis=0, keepdims=True)
                dxh = dy * gain
                o_ref[:, src] = (r * (dxh - xh * _segment_dot(dxh * xh, seg))).astype(BF16)
        v0 = 2 * A_WIDTH + QK_COLS
        for g in range(N_DIL):
            for ci in range(CHUNKS):
                col = v0 + g * B_WIDTH + ci * LANES
                o_ref[:, col:col + LANES] = chunk[2 * N_DIL + g, ci]().astype(BF16)

    specs = [_row_spec(ASM_TILE, B_WIDTH) if r == 1 else _regrouped_spec(r, ASM_TILE) for r in DIL_RATES] * 3
    n_scr = 3 * sum(1 for r in DIL_RATES if r > 1)
    tab = _row_spec(ASM_TILE, LANES)
    return _call(
        body, grid=(t // ASM_TILE,),
        in_specs=[_row_spec(ASM_TILE, AB_IN), _row_spec(ASM_TILE, 2 * A_WIDTH)] + specs
        + [_const_spec((1, QK_COLS)), tab, tab, tab],
        out_specs=[_row_spec(ASM_TILE, AB_IN), _const_spec((1, QK_COLS))],
        out_shape=[SDS((t, AB_IN), BF16), SDS((1, QK_COLS), F32)],
        scratch_shapes=[pltpu.VMEM((CHUNKS, ASM_TILE, LANES), F32)] * n_scr,
        operands=[proj, d_a, *dq, *dk, *dv, gains, *tabs], name="dproj_assemble", ride=ride)


def _fold_heads(dg_lane):
    n = dg_lane.shape[1]

    def body(x_ref, o_ref):
        r = lax.broadcasted_iota(jnp.int32, (B_WIDTH, B_WIDTH), 0) % HEAD_DIM
        c = lax.broadcasted_iota(jnp.int32, (B_WIDTH, B_WIDTH), 1) % HEAD_DIM
        fold = jnp.where(r == c, 1.0, 0.0).astype(F32)
        for jg in range(n // B_WIDTH):
            ls = slice(jg * B_WIDTH, (jg + 1) * B_WIDTH)
            o_ref[:, ls] = _dot_hi(jnp.broadcast_to(x_ref[:, ls], (8, B_WIDTH)), fold)

    return pl.pallas_call(body, out_shape=SDS((8, n), F32), name="fold_heads", compiler_params=_params())(dg_lane)


CD_TILE = 256
TAP_ROWS = 64
CD_IN = 2 * C_WIDTH + 3 * 512


def _shifted_copies(src, dst, rows):
    dst[0, :rows] = src[...]
    for b in range(1, 8):
        dst[b, :rows - 8] = src[pl.ds(b, rows - 8), :]


def _rows_from(shifted, start, n, lanes=slice(None)):
    b = start % 8
    return shifted[b, pl.ds(start - b, n), lanes]


def _mixer_cd_fwd(proj, cw, cb, lg, lb, dw):
    t = proj.shape[0]
    per = CD_TILE // HALO

    def body(h_ref, m_ref, cw_ref, cb_ref, lg_ref, lb_ref, dw_ref, o_ref, c1_ref, c_scr, e_scr, c_sh):
        not_first = (pl.program_id(0) > 0).astype(F32)
        lanes = [slice(c * LANES, (c + 1) * LANES) for c in range(C_WIDTH // LANES)]

        def col(ref, part, ls):
            return ref[:, part * C_WIDTH + ls.start:part * C_WIDTH + ls.stop].astype(F32)

        for ls in lanes:
            c_scr[:HALO, ls] = col(h_ref, 0, ls) * _sigmoid(col(h_ref, 1, ls)) * not_first
            c_scr[HALO:, ls] = col(m_ref, 0, ls) * _sigmoid(col(m_ref, 1, ls))
            e_scr[:HALO, ls] = col(h_ref, 3, ls) * col(h_ref, 4, ls) * not_first
            e_scr[HALO:, ls] = col(m_ref, 3, ls) * col(m_ref, 4, ls)
        _shifted_copies(c_scr, c_sh, HALO + CD_TILE)
        for ls in lanes:
            for r0 in range(0, CD_TILE, TAP_ROWS):
                acc = jnp.zeros((TAP_ROWS, LANES), F32)
                for k in range(C_KERNEL):
                    acc = acc + cw_ref[k:k + 1, ls] * _rows_from(c_sh, r0 + HALO - (C_KERNEL - 1) + k, TAP_ROWS, ls)
                c1_ref[r0:r0 + TAP_ROWS, ls] = acc + cb_ref[:, ls]
        mean = sum(jnp.sum(c1_ref[:, ls], axis=-1, keepdims=True) for ls in lanes) * (1.0 / C_WIDTH)
        var = sum(jnp.sum((c1_ref[:, ls] - mean) ** 2, axis=-1, keepdims=True) for ls in lanes) * (1.0 / C_WIDTH)
        rs = lax.rsqrt(var + EPS)
        for ls in lanes:
            c2 = (c1_ref[:, ls] - mean) * rs * lg_ref[:, ls] + lb_ref[:, ls]
            o_ref[:, ls] = (c2 * _sigmoid(c2)).astype(BF16)
            d1 = jnp.zeros((CD_TILE, LANES), F32)
            for k in range(D_KERNEL):
                d1 = d1 + dw_ref[k:k + 1, ls] * e_scr[pl.ds(HALO - (D_KERNEL - 1) + k, CD_TILE), ls]
            o_ref[:, C_WIDTH + ls.start:C_WIDTH + ls.stop] = (col(m_ref, 2, ls) * d1).astype(BF16)

    return pl.pallas_call(
        body, grid=(t // CD_TILE,),
        in_specs=[pl.BlockSpec((HALO, CD_IN), lambda i: (jnp.maximum(i * per - 1, 0), 0)), _row_spec(CD_TILE, CD_IN),
                  _const_spec((32, C_WIDTH)), _const_spec((1, C_WIDTH)), _const_spec((1, C_WIDTH)), _const_spec((1, C_WIDTH)),
                  _const_spec((8, C_WIDTH))],
        out_specs=[_row_spec(CD_TILE, 2 * C_WIDTH), _row_spec(CD_TILE, C_WIDTH)],
        out_shape=[SDS((t, 2 * C_WIDTH), BF16), SDS((t, C_WIDTH), F32)],
        scratch_shapes=[pltpu.VMEM((HALO + CD_TILE, C_WIDTH), F32)] * 2 + [pltpu.VMEM((8, HALO + CD_TILE, C_WIDTH), F32)],
        name="mixer_cd_fwd", compiler_params=_params())(proj, proj, cw, cb, lg, lb, dw)


def _mixer_cd_bwd(proj, dcat, c1, cw, lg, lb, dw, ride=None):
    t = proj.shape[0]
    per = CD_TILE // HALO
    nt = t // CD_TILE
    ext = CD_TILE + HALO

    def body(hp_ref, m_ref, hn_ref, dm_ref, dn_ref, c1m_ref, c1n_ref, cw_ref, lg_ref, lb_ref, dw_ref,
             dp_ref, dcw_ref, dcb_ref, dlg_ref, dlb_ref, ddw_ref, c_scr, e_scr, dc1_scr, dd1_scr, c_sh, dc1_sh, dcw_acc,
             dvh_scr, vhat_scr):
        i = pl.program_id(0)

        @pl.when(i == 0)
        def _():
            for r in (dcw_acc, dcb_ref, dlg_ref, dlb_ref, ddw_ref):
                r[...] = jnp.zeros_like(r)

        not_first = (i > 0).astype(F32)
        not_last = (i < nt - 1).astype(F32)
        main = slice(HALO, HALO + CD_TILE)
        lanes = [slice(c * LANES, (c + 1) * LANES) for c in range(C_WIDTH // LANES)]

        def col(ref, part, ls):
            return ref[:, part * C_WIDTH + ls.start:part * C_WIDTH + ls.stop].astype(F32)

        for ls in lanes:
            c_scr[:HALO, ls] = col(hp_ref, 0, ls) * _sigmoid(col(hp_ref, 1, ls)) * not_first
            c_scr[main, ls] = col(m_ref, 0, ls) * _sigmoid(col(m_ref, 1, ls))
            c_scr[HALO + CD_TILE:, ls] = col(hn_ref, 0, ls) * _sigmoid(col(hn_ref, 1, ls)) * not_last
            e_scr[:HALO, ls] = col(hp_ref, 3, ls) * col(hp_ref, 4, ls) * not_first
            e_scr[main, ls] = col(m_ref, 3, ls) * col(m_ref, 4, ls)
            e_scr[HALO + CD_TILE:, ls] = col(hn_ref, 3, ls) * col(hn_ref, 4, ls) * not_last
        _shifted_copies(c_scr, c_sh, 2 * HALO + CD_TILE)

        def c1_of(ls):
            return jnp.concatenate([c1m_ref[:, ls], c1n_ref[:, ls]], axis=0)

        mean = sum(jnp.sum(c1_of(ls), axis=-1, keepdims=True) for ls in lanes) * (1.0 / C_WIDTH)
        var = sum(jnp.sum((c1_of(ls) - mean) ** 2, axis=-1, keepdims=True) for ls in lanes) * (1.0 / C_WIDTH)
        rs = lax.rsqrt(var + EPS)
        sum_dvh, sum_dvh_vhat = 0.0, 0.0
        for ls in lanes:
            vhat = (c1_of(ls) - mean) * rs
            c2 = vhat * lg_ref[:, ls] + lb_ref[:, ls]
            sig = _sigmoid(c2)
            dc = jnp.concatenate([dm_ref[:, ls], dn_ref[:, ls] * not_last], axis=0)
            dc2 = dc * (sig * (1.0 + c2 * (1.0 - sig)))
            dvh = dc2 * lg_ref[:, ls]
            sum_dvh = sum_dvh + jnp.sum(dvh, axis=-1, keepdims=True)
            sum_dvh_vhat = sum_dvh_vhat + jnp.sum(dvh * vhat, axis=-1, keepdims=True)
            dvh_scr[:, ls] = dvh
            vhat_scr[:, ls] = vhat
            dlg_ref[:, ls] += jnp.sum((dc2 * vhat)[:CD_TILE], axis=0, keepdims=True)
            dlb_ref[:, ls] += jnp.sum(dc2[:CD_TILE], axis=0, keepdims=True)
        for ls in lanes:
            dc1 = rs * (dvh_scr[:, ls] - sum_dvh * (1.0 / C_WIDTH) - vhat_scr[:, ls] * (sum_dvh_vhat * (1.0 / C_WIDTH)))
            dc1_scr[:, ls] = dc1
            dcb_ref[:, ls] += jnp.sum(dc1[:CD_TILE], axis=0, keepdims=True)
        _shifted_copies(dc1_scr, dc1_sh, ext)
        for ls in lanes:
            for r0 in range(0, CD_TILE, TAP_ROWS):
                rows = slice(r0, r0 + TAP_ROWS)
                dc1_m = dc1_scr[rows, ls]
                dc0 = jnp.zeros((TAP_ROWS, LANES), F32)
                for k in range(C_KERNEL):
                    dc0 = dc0 + cw_ref[k:k + 1, ls] * _rows_from(dc1_sh, r0 + C_KERNEL - 1 - k, TAP_ROWS, ls)
                    prod = dc1_m * _rows_from(c_sh, r0 + HALO - (C_KERNEL - 1) + k, TAP_ROWS, ls)
                    dcw_acc[k, :, ls] += prod.reshape(TAP_ROWS // 8, 8, LANES).sum(axis=0)
                g_m = m_ref[rows, C_WIDTH + ls.start:C_WIDTH + ls.stop].astype(F32)
                a_m = m_ref[rows, ls].astype(F32)
                sig_m = _sigmoid(g_m)
                dp_ref[rows, ls] = (dc0 * sig_m).astype(BF16)
                dp_ref[rows, C_WIDTH + ls.start:C_WIDTH + ls.stop] = (dc0 * a_m * sig_m * (1.0 - sig_m)).astype(BF16)

        @pl.when(i == nt - 1)
        def _():
            dcw_ref[...] = jnp.sum(dcw_acc[...], axis=1)

        for ls in lanes:
            wide = slice(C_WIDTH + ls.start, C_WIDTH + ls.stop)
            d1 = jnp.zeros((CD_TILE, LANES), F32)
            for k in range(D_KERNEL):
                d1 = d1 + dw_ref[k:k + 1, ls] * e_scr[pl.ds(HALO - (D_KERNEL - 1) + k, CD_TILE), ls]
            dd_m = dm_ref[:, wide]
            dd1 = jnp.concatenate([dd_m * col(m_ref, 2, ls), dn_ref[:, wide] * col(hn_ref, 2, ls) * not_last], axis=0)
            dd1_scr[:, ls] = dd1
            dp_ref[:, 2 * C_WIDTH + ls.start:2 * C_WIDTH + ls.stop] = (dd_m * d1).astype(BF16)
            de = jnp.zeros((CD_TILE, LANES), F32)
            for k in range(D_KERNEL):
                de = de + dw_ref[k:k + 1, ls] * dd1_scr[pl.ds(D_KERNEL - 1 - k, CD_TILE), ls]
                ddw_ref[k:k + 1, ls] += jnp.sum(dd1[:CD_TILE] * e_scr[pl.ds(HALO - (D_KERNEL - 1) + k, CD_TILE), ls], axis=0, keepdims=True)
            dp_ref[:, 3 * C_WIDTH + ls.start:3 * C_WIDTH + ls.stop] = (de * col(m_ref, 4, ls)).astype(BF16)
            dp_ref[:, 4 * C_WIDTH + ls.start:4 * C_WIDTH + ls.stop] = (de * col(m_ref, 3, ls)).astype(BF16)

    halo_prev = lambda i: (jnp.maximum(i * per - 1, 0), 0)
    halo_next = lambda i: (jnp.minimum((i + 1) * per, t // HALO - 1), 0)
    vec = _const_spec((1, C_WIDTH))
    return _call(
        body, grid=(nt,),
        in_specs=[pl.BlockSpec((HALO, CD_IN), halo_prev), _row_spec(CD_TILE, CD_IN), pl.BlockSpec((HALO, CD_IN), halo_next),
                  _row_spec(CD_TILE, 2 * C_WIDTH), pl.BlockSpec((HALO, 2 * C_WIDTH), halo_next),
                  _row_spec(CD_TILE, C_WIDTH), pl.BlockSpec((HALO, C_WIDTH), halo_next),
                  _const_spec((32, C_WIDTH)), vec, vec, _const_spec((8, C_WIDTH))],
        out_specs=[_row_spec(CD_TILE, CD_IN), _const_spec((32, C_WIDTH)), vec, vec, vec, _const_spec((8, C_WIDTH))],
        out_shape=[SDS((t, CD_IN), BF16), SDS((32, C_WIDTH), F32), SDS((1, C_WIDTH), F32), SDS((1, C_WIDTH), F32),
                   SDS((1, C_WIDTH), F32), SDS((8, C_WIDTH), F32)],
        scratch_shapes=[pltpu.VMEM((2 * HALO + CD_TILE, C_WIDTH), F32)] * 2 + [pltpu.VMEM((ext, C_WIDTH), F32)] * 2
        + [pltpu.VMEM((8, 2 * HALO + CD_TILE, C_WIDTH), F32), pltpu.VMEM((8, ext, C_WIDTH), F32),
           pltpu.VMEM((32, 8, C_WIDTH), F32)] + [pltpu.VMEM((ext, C_WIDTH), F32)] * 2,
        operands=[proj, proj, proj, dcat, dcat, c1, c1, cw, lg, lb, dw], name="mixer_cd_bwd", ride=ride)


def _wgrad(name, pairs, out_rc, t, ride):
    tk = TILES["wgrad"]
    r, c = out_rc
    n = len(pairs)

    def body(*refs):
        ab, out_refs = refs[:2 * n], refs[2 * n:]
        k = pl.program_id(1)
        parts = [_dot(ab[2 * j][...], ab[2 * j + 1][...], TN) for j in range(n)]

        @pl.when(k == 0)
        def _():
            for a in range(n):
                out_refs[a][...] = parts[a]

        @pl.when(k > 0)
        def _():
            for a in range(n):
                out_refs[a][...] += parts[a]

    operands, in_specs = [], []
    for lhs, lhs_spec, rhs, rhs_spec in pairs:
        operands += [lhs, rhs]
        in_specs += [lhs_spec, rhs_spec]
    res = _call(body, grid=(N_CHIPS, t // tk), in_specs=in_specs,
                out_specs=[pl.BlockSpec((None, r, c), lambda p, k: (p, 0, 0))] * n,
                out_shape=[SDS((N_CHIPS, r, c), F32)] * n, operands=operands, name=name, ride=ride)
    outs, ride_res = (res, None) if ride is None else res
    outs = [o.reshape(N_CHIPS, 2, r // 2, c) for o in outs]
    return outs if ride is None else (outs, ride_res)


def _wgrad_col_sharded(name, h, dz_list, three_d, ride=None):
    t, d = h.shape
    tk = TILES["wgrad"]
    n4 = dz_list[0].shape[-1] if three_d else dz_list[0].shape[-1] // N_CHIPS
    hs = pl.BlockSpec((tk, d), lambda p, k: (k, 0))
    zs = pl.BlockSpec((None, tk, n4), lambda p, k: (p, k, 0)) if three_d else pl.BlockSpec((tk, n4), lambda p, k: (k, p))
    return _wgrad(name, [(h, hs, dz, zs) for dz in dz_list], (d, n4), t, ride)


def _wgrad_row_sharded(name, a, g, three_d, ride=None):
    many = isinstance(a, (list, tuple))
    a_list = list(a) if many else [a]
    t, d = g.shape
    tk = TILES["wgrad"]
    k4 = a_list[0].shape[-1] if three_d else a_list[0].shape[-1] // N_CHIPS
    a_spec = pl.BlockSpec((None, tk, k4), lambda p, k: (p, k, 0)) if three_d else pl.BlockSpec((tk, k4), lambda p, k: (k, p))
    gs = pl.BlockSpec((tk, d), lambda p, k: (k, 0))
    res = _wgrad(name, [(a_j, a_spec, g, gs) for a_j in a_list], (k4, d), t, ride)
    if many:
        return res
    return res[0] if ride is None else (res[0][0], res[1])


def _mesh_scalars():
    return jnp.stack([lax.axis_index("c"), 2 * lax.axis_index("x") + lax.axis_index("y")]).astype(jnp.int32)


def _stage_own(name, w, layer, dtype):
    layers, r, cols = w.shape
    h = r // 2

    def body(s_ref, x_ref, o_ref):
        o_ref[...] = x_ref[...].astype(dtype)

    return pl.pallas_call(
        body,
        grid_spec=pltpu.PrefetchScalarGridSpec(
            num_scalar_prefetch=1, grid=(2,),
            in_specs=[pl.BlockSpec((None, h, cols), lambda i, s: (2 * layer + i, 0, 0))],
            out_specs=pl.BlockSpec((None, None, h, cols), lambda i, s: (s[1], i, 0, 0))),
        out_shape=SDS((N_CHIPS, 2, h, cols), dtype), name=name,
        compiler_params=_params())(_mesh_scalars(), w.reshape(2 * layers, h, cols))


def _remote(src, dst, send_sem, recv_sem, device):
    return pltpu.make_async_remote_copy(src, dst, send_sem, recv_sem, device_id=device, device_id_type=MESH)


def _ride_gather_send(bufs):
    n = len(bufs)

    def each(b, sems, act):
        send, recv = sems
        x, y, c, p, others = _position()
        for t in range(n):
            for j, (qx, qy) in enumerate(others):
                act(b[t].at[p, c], b[t].at[2 * qx + qy, c], send.at[t, j], recv.at[t, j], (qx, qy, c))

    def start(ins, b, new, sems):
        each(b, sems, lambda mine, landed, s, r, dev: _remote(mine, mine, s, r, dev).start())

    def finish(ins, b, new, sems):
        def act(mine, landed, s, r, dev):
            _remote(mine, mine, s, r, dev).wait_send()
            _remote(landed, landed, s, r, dev).wait_recv()
        each(b, sems, act)

    return _Ride([], bufs, [], [(n, 3), (n, 3)], start, finish)


def _ride_gather_pass(bufs):
    n = len(bufs)

    def each(b, sems, act):
        send, recv = sems
        x, y, c, p, others = _position()
        for t in range(n):
            for j, (qx, qy) in enumerate(others):
                act(b[t].at[2 * qx + qy, c], b[t].at[2 * qx + qy, 1 - c], send.at[t, j], recv.at[t, j], (x, y, 1 - c))

    def start(ins, b, new, sems):
        each(b, sems, lambda landed, passed, s, r, dev: _remote(landed, landed, s, r, dev).start())

    def finish(ins, b, new, sems):
        def act(landed, passed, s, r, dev):
            _remote(landed, landed, s, r, dev).wait_send()
            _remote(passed, passed, s, r, dev).wait_recv()
        each(b, sems, act)

    return _Ride([], bufs, [], [(n, 3), (n, 3)], start, finish)


def _ride_swap(tensors):
    n = len(tensors)

    def each(ins, new, sems, act):
        send, recv = sems
        x, y, c, _, _ = _position()
        for t in range(n):
            act(_remote(ins[t].at[:, 1 - c], new[t], send.at[t], recv.at[t], (x, y, 1 - c)))

    def start(ins, b, new, sems):
        each(ins, new, sems, lambda cp: cp.start())

    def finish(ins, b, new, sems):
        each(ins, new, sems, lambda cp: cp.wait())

    return _Ride(tensors, [], [SDS((s.shape[0],) + s.shape[2:], s.dtype) for s in tensors], [(n,), (n,)], start, finish)


def _ride_scatter(tensors, landing):
    n = len(tensors)

    def each(ins, b, sems, act):
        send, recv = sems
        x, y, c, p, others = _position()
        for t in range(n):
            for j, (qx, qy) in enumerate(others):
                q = 2 * qx + qy
                act(ins[t].at[q], b[t].at[p], b[t].at[q], send.at[t, j], recv.at[t, j], (qx, qy, c))

    def start(ins, b, new, sems):
        each(ins, b, sems, lambda src, dst, landed, s, r, dev: _remote(src, dst, s, r, dev).start())

    def finish(ins, b, new, sems):
        def act(src, dst, landed, s, r, dev):
            _remote(src, dst, s, r, dev).wait_send()
            _remote(landed, landed, s, r, dev).wait_recv()
        each(ins, b, sems, act)

    return _Ride(tensors, landing, [], [(n, 3), (n, 3)], start, finish)


def _ride_join(bufs):
    n = len(bufs)

    def each(b, sems, act):
        send, recv = sems
        x, y, c, _, _ = _position()
        for t in range(n):
            act(b[t].at[c], b[t].at[1 - c], send.at[t], recv.at[t], (x, y, 1 - c))

    def start(ins, b, new, sems):
        each(b, sems, lambda mine, theirs, s, r, dev: _remote(mine, mine, s, r, dev).start())

    def finish(ins, b, new, sems):
        def act(mine, theirs, s, r, dev):
            _remote(mine, mine, s, r, dev).wait_send()
            _remote(theirs, theirs, s, r, dev).wait_recv()
        each(b, sems, act)

    return _Ride([], bufs, [], [(n,), (n,)], start, finish)


def _all_reduce_small(pack):
    rows = pack.shape[0]
    n_dev = 2 * N_CHIPS

    def body(x_ref, o_ref, land, send, recv):
        x, y, c, p, _ = _position()
        me = 2 * p + c
        land[me] = x_ref[...]
        peers = [(dx, dy, dc) for dx in range(2) for dy in range(2) for dc in range(2) if (dx, dy, dc) != (0, 0, 0)]
        for j, (dx, dy, dc) in enumerate(peers):
            _remote(land.at[me], land.at[me], send.at[j], recv.at[j], (x ^ dx, y ^ dy, c ^ dc)).start()
        for j, (dx, dy, dc) in enumerate(peers):
            src = 4 * (x ^ dx) + 2 * (y ^ dy) + (c ^ dc)
            _remote(land.at[me], land.at[me], send.at[j], recv.at[j], (x ^ dx, y ^ dy, c ^ dc)).wait_send()
            _remote(land.at[src], land.at[src], send.at[j], recv.at[j], (x ^ dx, y ^ dy, c ^ dc)).wait_recv()
        acc = land[0]
        for dev in range(1, n_dev):
            acc = acc + land[dev]
        o_ref[...] = acc

    return pl.pallas_call(
        body, out_shape=SDS((rows, LANES), F32),
        scratch_shapes=[pltpu.VMEM((n_dev, rows, LANES), F32), pltpu.SemaphoreType.DMA((n_dev - 1,)),
                        pltpu.SemaphoreType.DMA((n_dev - 1,))],
        name="all_reduce_small", compiler_params=_params())(pack)


def _add_own_half(name, full, recv, out_dtype):
    n4, _, h, cols = full.shape

    def body(s_ref, a_ref, b_ref, o_ref, own_ref):
        v = (a_ref[...] + b_ref[...]).astype(out_dtype)
        o_ref[...] = v

        @pl.when(pl.program_id(0) == s_ref[1])
        def _():
            own_ref[...] = v

    return pl.pallas_call(
        body,
        grid_spec=pltpu.PrefetchScalarGridSpec(
            num_scalar_prefetch=1, grid=(n4,),
            in_specs=[pl.BlockSpec((None, None, h, cols), lambda q, s: (q, s[0], 0, 0)),
                      pl.BlockSpec((None, h, cols), lambda q, s: (q, 0, 0))],
            out_specs=[pl.BlockSpec((None, h, cols), lambda q, s: (q, 0, 0)),
                       pl.BlockSpec((None, h, cols), lambda q, s: (s[1], 0, 0))]),
        out_shape=[SDS((n4, h, cols), out_dtype)] * 2, name=name, compiler_params=_params())(_mesh_scalars(), full, recv)


def _sum_chips(name, parts):
    n4, h, cols = parts.shape
    th = h // 4 if h % 64 == 0 else h

    def body(s_ref, a_ref, o_ref):
        acc = a_ref[0].astype(F32)
        for q in range(1, n4):
            acc = acc + a_ref[q].astype(F32)
        o_ref[...] = acc

    return pl.pallas_call(
        body,
        grid_spec=pltpu.PrefetchScalarGridSpec(
            num_scalar_prefetch=1, grid=(h // th,),
            in_specs=[pl.BlockSpec((n4, th, cols), lambda i, s: (0, i, 0))],
            out_specs=pl.BlockSpec((None, th, cols), lambda i, s: (s[0], i, 0))),
        out_shape=SDS((2, h, cols), F32), name=name, compiler_params=_params())(_mesh_scalars(), parts)


def _adamw_math(w, g, m, v):
    m2 = ADAM_B1 * m + (1.0 - ADAM_B1) * g
    v2 = ADAM_B2 * v + (1.0 - ADAM_B2) * (g * g)
    m_hat = m2 / (1.0 - ADAM_B1 ** ADAM_STEP)
    v_hat = v2 / (1.0 - ADAM_B2 ** ADAM_STEP)
    delta = -ADAM_LR * (m_hat / (jnp.sqrt(v_hat) + ADAM_EPS) + ADAM_WD * w)
    return delta, m2, v2


def _row_tile(rows, cols):
    cap = max(8, (1 << 18) // cols)
    best = 8
    for cand in range(8, min(rows, cap) + 1, 8):
        if rows % cand == 0:
            best = cand
    return best


def _adamw_big(name, w, g_layers, m, v):
    layers, rows, cols = w.shape
    tr = _row_tile(rows, cols)

    def body(w_ref, m_ref, v_ref, *rest):
        g_refs, (g_o, d_o, m_o, v_o) = rest[:layers], rest[layers:]
        gv = g_refs[0][...]
        for layer in range(1, layers):
            gv = jnp.where(pl.program_id(0) == layer, g_refs[layer][...], gv)
        d, mm, vv = _adamw_math(w_ref[...], gv, m_ref[...], v_ref[...])
        g_o[...] = gv
        d_o[...] = d
        m_o[...] = mm
        v_o[...] = vv

    blk = pl.BlockSpec((None, tr, cols), lambda l, i: (l, i, 0))
    g_blk = pl.BlockSpec((tr, cols), lambda l, i: (i, 0))
    return tuple(pl.pallas_call(
        body, grid=(layers, rows // tr), in_specs=[blk] * 3 + [g_blk] * layers, out_specs=[blk] * 4,
        out_shape=[SDS((layers, rows, cols), F32)] * 4, name=name,
        compiler_params=_params())(w, m, v, *[g.reshape(rows, cols) for g in g_layers]))


def _adamw_small(ws, gs, ms, vs):
    n = len(ws)
    flat = []
    for group in (ws, gs, ms, vs):
        flat += [a.reshape(-1, a.shape[-1]) for a in group]

    def body(*refs):
        w_r, g_r, m_r, v_r = refs[:n], refs[n:2 * n], refs[2 * n:3 * n], refs[3 * n:4 * n]
        d_o, m_o, v_o = refs[4 * n:5 * n], refs[5 * n:6 * n], refs[6 * n:7 * n]
        for j in range(n):
            d, mm, vv = _adamw_math(w_r[j][...], g_r[j][...], m_r[j][...], v_r[j][...])
            d_o[j][...] = d
            m_o[j][...] = mm
            v_o[j][...] = vv

    shapes = [SDS(a.shape, F32) for a in flat[:n]]
    outs = pl.pallas_call(body, out_shape=shapes * 3, name="adamw_small", compiler_params=_params())(*flat)
    res = []
    for k in range(3):
        res.append([outs[k * n + j].reshape(ws[j].shape) for j in range(n)])
    return res


BIG = ("ab_w_in", "ab_w_out", "cd_w_in", "cd_w_out", "ffn_w_gate", "ffn_w_up", "ffn_w_down")
V_BLOCK = (2 * A_WIDTH + QK_COLS) // B_WIDTH


def _pad_rows(a, rows):
    return jnp.pad(a, ((0, rows - a.shape[0]), (0, 0)))


A_IN, A_OUT, C_IN, C_OUT = ("ab_w_in", 0), ("ab_w_out", 0), ("cd_w_in", 0), ("cd_w_out", 0)
G0, U0, D0 = ("ffn_w_gate", 0), ("ffn_w_up", 0), ("ffn_w_down", 0)
G1, U1, D1 = ("ffn_w_gate", 1), ("ffn_w_up", 1), ("ffn_w_down", 1)
UNITS = (A_IN, A_OUT, G0, U0, D0, C_IN, C_OUT, G1, U1, D1)
ROWS_MINOR = ("ffn_w_gate", "ffn_w_up")
SMALL_SHARDED = ("small", 0)
REPLICATED_UNIT = ("replicated", 0)


class _Exchange:
    def __init__(self, enabled):
        self.enabled = enabled
        self.w, self.grad, self.recv, self.half, self.land, self.done = {}, {}, {}, {}, {}, {}

    def full(self, unit):
        b = self.w[unit]
        return b.reshape(N_CHIPS, 1, 2 * b.shape[2], b.shape[3])

    def _ride(self, phases):
        rides, sinks = [], []
        for kind, units in phases:
            if kind == "send":
                rides.append(_ride_gather_send([self.w[u] for u in units]))
                sinks.append(self.w)
            elif kind == "pass":
                rides.append(_ride_gather_pass([self.w[u] for u in units]))
                sinks.append(self.w)
            elif kind == "swap":
                rides.append(_ride_swap([self.grad[u] for u in units]))
                sinks.append(self.recv)
            elif kind == "scatter":
                rides.append(_ride_scatter([self.half[u] for u in units], [self.land[u] for u in units]))
                sinks.append(self.land)
            else:
                rides.append(_ride_join([self.done[u] for u in units]))
                sinks.append(self.done)
        ride = functools.reduce(_ride_both, rides)

        def settle(res):
            n_bufs = sum(len(r.bufs) for r in rides)
            bufs, new = list(res[:n_bufs]), list(res[n_bufs:])
            for r, sink, (_, units) in zip(rides, sinks, phases):
                vals = [bufs.pop(0) for _ in r.bufs] + [new.pop(0) for _ in r.new_outs]
                for u, v in zip(units, vals):
                    sink[u] = v

        return ride, settle

    def run(self, fn, *args, phases=(), **kw):
        if not self.enabled or not phases:
            return fn(*args, **kw)
        ride, settle = self._ride(phases)
        out, res = fn(*args, ride=ride, **kw)
        settle(res)
        return out

    def alone(self, name, phases):
        if self.enabled:
            ride, settle = self._ride(phases)
            settle(_run_ride(name, ride))

    def pair_sum(self, units):
        if self.enabled:
            for u in units:
                dtype = F32 if u in (SMALL_SHARDED, REPLICATED_UNIT) else BF16
                self.half[u], self.land[u] = _add_own_half(f"pair_sum_{u[0]}_{u[1]}", self.grad[u], self.recv[u], dtype)

    def chip_sum(self, units):
        if self.enabled:
            for u in units:
                self.done[u] = _sum_chips(f"chip_sum_{u[0]}_{u[1]}", self.land[u])


def _local_step(x, target, ex, sp):
    t, d = x.shape
    tabs = _rope_tables(t)
    gains = jnp.concatenate([jnp.tile(sp["q_norm_g"][g], HEAD_DIM // 8) for g in range(N_DIL)]
                            + [jnp.tile(sp["k_norm_g"][g], HEAD_DIM // 8) for g in range(N_DIL)]).reshape(1, QK_COLS)
    bias_t = sp["sgu_bias"].T
    cw = _pad_rows(sp["conv_c_w"], 32)
    dw = _pad_rows(sp["conv_d_w"], 8)
    cb, clg, clb = (sp[k].reshape(1, C_WIDTH) for k in ("conv_c_b", "c_ln_g", "c_ln_b"))
    slg, slb = sp["sgu_norm_g"].reshape(1, A_WIDTH), sp["sgu_norm_b"].reshape(1, A_WIDTH)
    g_ab, g_cd = sp["ab_norm_g"].reshape(1, d), sp["cd_norm_g"].reshape(1, d)
    g_f0, g_f1 = sp["ffn_norm_g"][0:1], sp["ffn_norm_g"][1:2]
    run = ex.run

    def w2d(unit):
        return ex.full(unit).reshape(-1, d)

    h0 = _rms_fwd("rms_ab", x, g_ab)
    proj = run(_proj_in, "proj_ab", h0, ex.full(A_IN), 0, phases=[("send", [A_OUT, G0])])
    a_out = _mixer_a_fwd(proj, slg, slb, sp["sgu_w"], bias_t)
    qk, q1, q2, k1, k2 = run(_qk_fwd, proj, gains, tabs, phases=[("pass", [A_OUT, G0]), ("send", [U0])])
    regrouped_qk = {1: (q1, k1), 2: (q2, k2)}
    fwd_phases = ([("pass", [U0]), ("send", [D0])], [("pass", [D0]), ("send", [C_IN])],
                  [("pass", [C_IN]), ("send", [C_OUT])])
    qkv, o_list, l_list = [], [], []
    for g, rate in enumerate(DIL_RATES):
        if rate == 1:
            qk3, proj3 = qk.reshape(1, t, QK_COLS), proj.reshape(1, t, AB_IN)
            q, k, v = (qk3, g), (qk3, N_DIL + g), (proj3, V_BLOCK + g)
        else:
            vp, = _permute(f"regroup_v_{g}", [(proj, V_BLOCK + g)], rate)
            q, k, v = (regrouped_qk[g][0], 0), (regrouped_qk[g][1], 0), (vp, 0)
        qkv.append((q, k, v))
        o, l = run(_attn_fwd, f"attn_fwd_{g}", q, k, v, phases=fwd_phases[g])
        if rate == 1:
            o, l = o.reshape(t, B_WIDTH), l.reshape(t, B_WIDTH)
        o_list.append(o)
        l_list.append(l)
    cat, lse_tot, lse_1, lse_2 = _attn_merge(a_out, o_list, l_list)
    x1, hf0 = _proj_out("out_ab", cat, w2d(A_OUT), x, g_next=g_f0)
    gate0, up0, act0 = run(_ffn_in, "ffn_in_0", hf0, ex.full(G0), ex.full(U0), 0,
                           phases=[("pass", [C_OUT]), ("send", [G1, U1])])
    x2, h1 = run(_ffn_out, "ffn_out_0", act0, ex.full(D0), 0, x1, g_next=g_cd, phases=[("pass", [G1, U1]), ("send", [D1])])
    projcd = run(_proj_in, "proj_cd", h1, ex.full(C_IN), 0, phases=[("pass", [D1])])
    cat2, c1 = _mixer_cd_fwd(projcd, cw, cb, clg, clb, dw)
    x3, hf1 = _proj_out("out_cd", cat2, w2d(C_OUT), x2, g_next=g_f1)
    gate1, up1, act1 = _ffn_in("ffn_in_1", hf1, ex.full(G1), ex.full(U1), 0)
    dy, loss_acc, dy_b = _ffn_out("ffn_out_1", act1, ex.full(D1), 0, x3, target=target)
    loss = 0.5 * loss_acc[0, 0] / d

    late = [D1, G1, U1]
    dgate, dup = _ffn_dact("ffn_dact_1", dy_b, ex.full(D1), 0, gate1, up1)
    ex.grad[D1] = _wgrad_row_sharded("wgrad_down_1", act1, dy_b, True)
    ex.grad[G1] = _wgrad_row_sharded("wgrad_gate_1", dgate, hf1, True)
    ex.grad[U1] = _wgrad_row_sharded("wgrad_up_1", dup, hf1, True)
    g3, d_f1, g3_b = run(_dgrad_cols, "dgrad_ffn_1", [dgate, dup], [ex.full(G1), ex.full(U1)], 0, True, x3, g_f1, dy,
                         w_rows=True, phases=[("swap", late)])
    ex.pair_sum(late)

    dcat2 = _dgrad_rows("dgrad_out_cd", g3_b, w2d(C_OUT))
    ex.grad[C_OUT] = _wgrad_row_sharded("wgrad_out_cd", cat2, g3_b, False)
    dprojcd, d_cw, d_cb, d_clg, d_clb, d_dw = run(_mixer_cd_bwd, projcd, dcat2, c1, cw, clg, clb, dw, phases=[("scatter", late)])
    ex.chip_sum(late)
    ex.grad[C_IN] = run(_wgrad_col_sharded, "wgrad_in_cd", h1, [dprojcd], False, phases=[("join", late)])[0]
    g2, d_cdn, g2_b = run(_dgrad_cols, "dgrad_in_cd", [dprojcd], [ex.full(C_IN)], 0, False, x2, g_cd, g3,
                          phases=[("swap", [C_OUT, C_IN])])
    ex.pair_sum([C_OUT, C_IN])

    dgate, dup = run(_ffn_dact, "ffn_dact_0", g2_b, ex.full(D0), 0, gate0, up0, phases=[("scatter", [C_OUT, C_IN])])
    ex.chip_sum([C_OUT, C_IN])
    ex.grad[D0] = run(_wgrad_row_sharded, "wgrad_down_0", act0, g2_b, True, phases=[("join", [C_OUT, C_IN])])
    ex.grad[G0] = _wgrad_row_sharded("wgrad_gate_0", dgate, hf0, True)
    ex.grad[U0] = _wgrad_row_sharded("wgrad_up_0", dup, hf0, True)
    small = {"cd_norm_g": d_cdn, "conv_c_w": d_cw[:C_KERNEL], "conv_c_b": d_cb, "c_ln_g": d_clg, "c_ln_b": d_clb,
             "conv_d_w": d_dw[:D_KERNEL]}
    ex.grad[SMALL_SHARDED] = _split_full_small(small).reshape(N_CHIPS, 2, SHARDED_ROWS // 2, LANES)
    mid = [D0, G0, U0, SMALL_SHARDED]
    g1, d_f0, g1_b = run(_dgrad_cols, "dgrad_ffn_0", [dgate, dup], [ex.full(G0), ex.full(U0)], 0, True, x1, g_f0, g2,
                         w_rows=True, phases=[("swap", mid)])
    ex.pair_sum(mid)

    dcat = _dgrad_rows("dgrad_out_ab", g1_b, w2d(A_OUT))
    ex.grad[A_OUT] = _wgrad_row_sharded("wgrad_out_ab", cat, g1_b, False)
    d_a, d_sw, d_sbt, d_slg, d_slb = _mixer_a_bwd(proj, dcat, slg, slb, sp["sgu_w"], bias_t)
    early = {"sgu_norm_g": d_slg, "sgu_norm_b": d_slb, "sgu_w": d_sw, "sgu_bias": d_sbt.T}
    ex.grad[REPLICATED_UNIT] = jnp.broadcast_to(
        _pack_replicated(early, REPLICATED_EARLY, REPLICATED_EARLY_ROWS).reshape(2, REPLICATED_EARLY_ROWS // 2, LANES),
        (N_CHIPS, 2, REPLICATED_EARLY_ROWS // 2, LANES))
    last = [A_OUT, REPLICATED_UNIT]
    dbb, dd, db_1, dd_1, db_2, dd_2 = _attn_bwd_prep(dcat, cat)
    regrouped_bwd = {1: (db_1, lse_1, dd_1), 2: (db_2, lse_2, dd_2)}
    bwd_phases = ([("scatter", [D0, SMALL_SHARDED])],
                  [("scatter", [G0]), ("join", [D0, SMALL_SHARDED]), ("swap", last)],
                  [("scatter", [U0]), ("join", [G0])])
    dqs, dks, dvs = [], [], []
    for g, rate in enumerate(DIL_RATES):
        q, k, v = qkv[g]
        if rate == 1:
            db3, l3, dd3 = (a.reshape(1, t, B_WIDTH) for a in (dbb, lse_tot, dd))
        else:
            db3, l3, dd3 = regrouped_bwd[g]
        if g == 1:
            ex.chip_sum([D0, SMALL_SHARDED])
        elif g == 2:
            ex.chip_sum([G0])
            ex.pair_sum(last)
        dq, dk, dv = run(_attn_bwd, f"attn_bwd_{g}", q, k, v, db3, l3, dd3, phases=bwd_phases[g])
        if rate == 1:
            dq, dk, dv = (a.reshape(t, B_WIDTH) for a in (dq, dk, dv))
        dqs.append(dq)
        dks.append(dk)
        dvs.append(dv)
    ex.chip_sum([U0])
    dproj, d_gains = run(_dproj_assemble, proj, d_a, dqs, dks, dvs, gains, tabs, phases=[("scatter", last), ("join", [U0])])
    ex.chip_sum(last)
    d_gains = _fold_heads(d_gains)[0].reshape(2, N_DIL, B_WIDTH)[:, :, :HEAD_DIM]
    ex.grad[A_IN] = run(_wgrad_col_sharded, "wgrad_in_ab", h0, [dproj], False, phases=[("join", last)])[0]
    ex.alone("swap_last", [("swap", [A_IN])])
    ex.pair_sum([A_IN])
    gx, d_abn = run(_dgrad_cols, "dgrad_in_ab", [dproj], [ex.full(A_IN)], 0, False, x, g_ab, g1, bf16_copy=False,
                    phases=[("scatter", [A_IN])])
    ex.chip_sum([A_IN])
    ex.alone("join_last", [("join", [A_IN])])

    small.update({
        "ab_norm_g": d_abn, "sgu_norm_g": d_slg, "sgu_norm_b": d_slb, "sgu_w": d_sw, "sgu_bias": d_sbt.T,
        "q_norm_g": d_gains[0], "k_norm_g": d_gains[1], "ffn_norm_g": jnp.concatenate([d_f0, d_f1], axis=0),
    })
    return loss, gx, small


SHARDED_SMALL = ("cd_norm_g", "conv_c_w", "conv_c_b", "c_ln_g", "c_ln_b", "conv_d_w")
SHARDED_ROWS = 48
REPLICATED_EARLY = ("sgu_norm_g", "sgu_norm_b", "sgu_w", "sgu_bias")
REPLICATED_EARLY_ROWS = 528
REPLICATED_LATE = ("ab_norm_g", "q_norm_g", "k_norm_g", "ffn_norm_g", "loss")
REPLICATED_LATE_ROWS = 32
REPLICATED_SMALL = REPLICATED_EARLY + REPLICATED_LATE[:-1]


def _pack_sharded(parts):
    rows = [parts[k].reshape(-1, LANES) for k in SHARDED_SMALL]
    return _pad_rows(jnp.concatenate(rows, axis=0), SHARDED_ROWS)


def _split_full_small(small):
    per_chip = []
    for q in range(N_CHIPS):
        parts = {}
        for k in SHARDED_SMALL:
            a = small[k]
            a = a.reshape(-1, a.shape[-1])
            n = a.shape[-1] // N_CHIPS
            parts[k] = a[:, q * n:(q + 1) * n]
        per_chip.append(_pack_sharded(parts))
    return jnp.stack(per_chip)


def _unpack_sharded(pack, shapes):
    out, r = {}, 0
    for k in SHARDED_SMALL:
        n = math.prod(shapes[k]) // LANES
        out[k] = pack[r:r + n].reshape(shapes[k])
        r += n
    return out


def _gathered_small(packs, shapes):
    per_chip = [_unpack_sharded(packs[q], shapes) for q in range(N_CHIPS)]
    return {k: jnp.concatenate([pc[k] for pc in per_chip], axis=-1) for k in SHARDED_SMALL}


def _pack_replicated(small, names, total_rows):
    rows = []
    for k in names:
        a = small[k].reshape(-1)
        a = jnp.pad(a, (0, (-a.shape[0]) % LANES))
        rows.append(a.reshape(-1, LANES))
    return _pad_rows(jnp.concatenate(rows, axis=0), total_rows)


def _unpack_replicated(pack, shapes, names):
    out, r = {}, 0
    for k in names:
        size = math.prod(shapes[k])
        n = -(-size // LANES)
        out[k] = pack[r:r + n].reshape(-1)[:size].reshape(shapes[k])
        r += n
    return out


WEIGHT_ORDER = ("ab_norm_g", "ab_w_in", "sgu_norm_g", "sgu_norm_b", "sgu_w", "sgu_bias", "q_norm_g", "k_norm_g", "ab_w_out",
                "cd_norm_g", "cd_w_in", "conv_c_w", "conv_c_b", "c_ln_g", "c_ln_b", "conv_d_w", "cd_w_out", "ffn_norm_g",
                "ffn_w_gate", "ffn_w_up", "ffn_w_down")


def kernel(x, ab_norm_g, ab_w_in, sgu_norm_g, sgu_norm_b, sgu_w, sgu_bias, q_norm_g, k_norm_g, ab_w_out, cd_norm_g, cd_w_in, conv_c_w, conv_c_b, c_ln_g, c_ln_b, conv_d_w, cd_w_out, ffn_norm_g, ffn_w_gate, ffn_w_up, ffn_w_down, loss_target, m_ab_norm_g, m_ab_w_in, m_sgu_norm_g, m_sgu_norm_b, m_sgu_w, m_sgu_bias, m_q_norm_g, m_k_norm_g, m_ab_w_out, m_cd_norm_g, m_cd_w_in, m_conv_c_w, m_conv_c_b, m_c_ln_g, m_c_ln_b, m_conv_d_w, m_cd_w_out, m_ffn_norm_g, m_ffn_w_gate, m_ffn_w_up, m_ffn_w_down, v_ab_norm_g, v_ab_w_in, v_sgu_norm_g, v_sgu_norm_b, v_sgu_w, v_sgu_bias, v_q_norm_g, v_k_norm_g, v_ab_w_out, v_cd_norm_g, v_cd_w_in, v_conv_c_w, v_conv_c_b, v_c_ln_g, v_c_ln_b, v_conv_d_w, v_cd_w_out, v_ffn_norm_g, v_ffn_w_gate, v_ffn_w_up, v_ffn_w_down):
    args = dict(locals())
    ws = {k: args[k] for k in WEIGHT_ORDER}
    ms = {k: args["m_" + k] for k in WEIGHT_ORDER}
    vs = {k: args["v_" + k] for k in WEIGHT_ORDER}
    small_names = [k for k in WEIGHT_ORDER if k not in BIG]
    t, d = x.shape[1:]

    for group in (ws, ms, vs):
        for k in ROWS_MINOR:
            group[k] = jnp.swapaxes(group[k], 1, 2)
    ex = _Exchange(enabled=True)
    for name, layer in UNITS:
        ex.w[(name, layer)] = _stage_own(f"stage_{name}_{layer}", ws[name], layer, BF16)
    own_small = _pack_sharded({k: ws[k][0] for k in SHARDED_SMALL})
    ex.w[SMALL_SHARDED] = _stage_own("stage_small", own_small[None], 0, F32)
    ex.alone("gather_first", [("send", [A_IN, SMALL_SHARDED])])
    ex.alone("gather_first_pass", [("pass", [A_IN, SMALL_SHARDED])])
    sp = _gathered_small(ex.w[SMALL_SHARDED].reshape(N_CHIPS, SHARDED_ROWS, LANES), {k: ws[k].shape[1:] for k in SHARDED_SMALL})
    for k in REPLICATED_SMALL:
        sp[k] = ws[k] if k == "ffn_norm_g" else ws[k][0]

    loss, grad_x, g_small = _local_step(x.reshape(t, d), loss_target.reshape(t, d), ex, sp)

    shapes = {k: ws[k].shape for k in REPLICATED_SMALL}
    shapes["loss"] = (1,)
    g_small["loss"] = loss
    late = _all_reduce_small(_pack_replicated(g_small, REPLICATED_LATE, REPLICATED_LATE_ROWS))
    grad = _unpack_sharded(ex.done[SMALL_SHARDED].reshape(SHARDED_ROWS, LANES), {k: ws[k].shape for k in SHARDED_SMALL})
    grad.update(_unpack_replicated(ex.done[REPLICATED_UNIT].reshape(REPLICATED_EARLY_ROWS, LANES), shapes, REPLICATED_EARLY))
    grad.update(_unpack_replicated(late, shapes, REPLICATED_LATE))
    loss = grad.pop("loss")[0]

    delta, new_m, new_v = {}, {}, {}
    for k in BIG:
        g_layers = [ex.done[(k, layer)] for layer in range(ws[k].shape[0])]
        outs = _adamw_big("adamw_" + k, ws[k], g_layers, ms[k], vs[k])
        if k in ROWS_MINOR:
            outs = [jnp.swapaxes(o, 1, 2) for o in outs]
        grad[k], delta[k], new_m[k], new_v[k] = outs
    d_s, m_s, v_s = _adamw_small([ws[k] for k in small_names], [grad[k] for k in small_names],
                                 [ms[k] for k in small_names], [vs[k] for k in small_names])
    for j, k in enumerate(small_names):
        delta[k], new_m[k], new_v[k] = d_s[j], m_s[j], v_s[j]

    return (loss, grad_x[None], *[grad[k] for k in WEIGHT_ORDER], *[delta[k] for k in WEIGHT_ORDER],
            *[new_m[k] for k in WEIGHT_ORDER], *[new_v[k] for k in WEIGHT_ORDER])
```

```python
import functools
import math

import jax
import jax.numpy as jnp
from jax import lax
from jax.experimental import pallas as pl
from jax.experimental.pallas import tpu as pltpu

F32 = jnp.float32
BF16 = jnp.bfloat16
SDS = jax.ShapeDtypeStruct

N_CHIPS = 4
EPS = 1e-6
NEG_INF = -1e30
CHUNK = 128
A_GROUPS = 4
A_WIDTH = 512
N_DIL = 3
DIL_RATES = (1, 4, 16)
HEAD_DIM = 64
B_WIDTH = 512
ROPE_DIM = 16
ROPE_THETA = 500000.0
C_WIDTH = 512
C_KERNEL = 31
D_KERNEL = 3
HALO = 32
ATT_BLOCK = 128
LANES = 128

ADAM_LR = 0.001
ADAM_B1 = 0.9
ADAM_B2 = 0.999
ADAM_EPS = 1e-08
ADAM_WD = 0.01
ADAM_STEP = 10

VMEM_LIMIT = 56 * 1024 * 1024

NN = (((1,), (0,)), ((), ()))
NT = (((1,), (1,)), ((), ()))
TN = (((0,), (0,)), ((), ()))

TILES = {"proj_in": 1024, "proj_out": 1024, "ffn_in": 1024, "ffn_out": 512, "ffn_dact": 512, "dgrad_cols": 512,
         "dgrad_rows": 1024, "wgrad": 4096}


def _params(sem=None):
    return pltpu.CompilerParams(dimension_semantics=sem, vmem_limit_bytes=VMEM_LIMIT)


def _bf(v):
    return v if v.dtype == BF16 else v.astype(BF16)


def _dot(a, b, dims):
    return lax.dot_general(_bf(a), _bf(b), dims, preferred_element_type=F32)


def _dot_hi(a, b):
    return jnp.dot(a, b, precision=lax.Precision.HIGHEST, preferred_element_type=F32)


def _sigmoid(v):
    return 0.5 * jnp.tanh(0.5 * v) + 0.5


def _gelu(v):
    return 0.5 * v * (1.0 + lax.erf(v * (1.0 / math.sqrt(2.0))))


def _gelu_grad(v):
    cdf = 0.5 * (1.0 + lax.erf(v * (1.0 / math.sqrt(2.0))))
    return cdf + v * jnp.exp(-0.5 * v * v) * (1.0 / math.sqrt(2.0 * math.pi))


def _segment_mean_matrix(seg, scale=None):
    r = lax.broadcasted_iota(jnp.int32, (LANES, LANES), 0) // seg
    c = lax.broadcasted_iota(jnp.int32, (LANES, LANES), 1) // seg
    return jnp.where(r == c, (1.0 / seg) if scale is None else scale, 0.0).astype(BF16)


def _segment_dot(v, seg):
    hi = v.astype(BF16)
    lo = (v - hi.astype(F32)).astype(BF16)
    return jnp.dot(hi, seg, preferred_element_type=F32) + jnp.dot(lo, seg, preferred_element_type=F32)


MESH = pl.DeviceIdType.MESH
ANY = pl.BlockSpec(memory_space=pl.ANY)


def _position():
    x, y, c = lax.axis_index("x"), lax.axis_index("y"), lax.axis_index("c")
    others = [(1 - x, y), (x, 1 - y), (1 - x, 1 - y)]
    return x, y, c, 2 * x + y, others


class _Ride:
    def __init__(self, ins, bufs, new_outs, sem_shapes, start, finish):
        self.ins, self.bufs, self.new_outs, self.sem_shapes = list(ins), list(bufs), list(new_outs), list(sem_shapes)
        self.start, self.finish = start, finish


def _ride_both(a, b):
    na = (len(a.ins), len(a.bufs), len(a.new_outs), len(a.sem_shapes))

    def split(ins, bufs, new, sems):
        return ((ins[:na[0]], bufs[:na[1]], new[:na[2]], sems[:na[3]]), (ins[na[0]:], bufs[na[1]:], new[na[2]:], sems[na[3]:]))

    def start(*refs):
        ra, rb = split(*refs)
        a.start(*ra)
        b.start(*rb)

    def finish(*refs):
        ra, rb = split(*refs)
        a.finish(*ra)
        b.finish(*rb)

    return _Ride(a.ins + b.ins, a.bufs + b.bufs, a.new_outs + b.new_outs, a.sem_shapes + b.sem_shapes, start, finish)


def _call(body, *, grid, in_specs, out_specs, out_shape, operands, name, scratch_shapes=(), aliases=None, ride=None):
    if ride is None:
        return pl.pallas_call(body, grid=grid, in_specs=in_specs, out_specs=out_specs, out_shape=out_shape,
                              scratch_shapes=list(scratch_shapes), input_output_aliases=aliases or {}, name=name,
                              compiler_params=_params())(*operands)
    multi = isinstance(out_shape, (list, tuple))
    out_shapes = list(out_shape) if multi else [out_shape]
    o_specs = list(out_specs) if multi else [out_specs]
    n_in, n_out, n_scr = len(operands), len(out_shapes), len(scratch_shapes)
    n_ri, n_rb, n_rn = len(ride.ins), len(ride.bufs), len(ride.new_outs)

    def carrying(*refs):
        k = n_in
        r_ins = refs[k:k + n_ri]
        k += n_ri + n_rb
        outs = refs[k:k + n_out]
        k += n_out
        r_bufs = refs[k:k + n_rb]
        k += n_rb
        r_new = refs[k:k + n_rn]
        k += n_rn
        scratch = refs[k:k + n_scr]
        sems = refs[k + n_scr:]
        first, last = None, None
        for axis, size in enumerate(grid):
            pid = pl.program_id(axis)
            first = (pid == 0) if first is None else first & (pid == 0)
            last = (pid == size - 1) if last is None else last & (pid == size - 1)

        @pl.when(first)
        def _():
            ride.start(r_ins, r_bufs, r_new, sems)

        body(*refs[:n_in], *outs, *scratch)

        @pl.when(last)
        def _():
            ride.finish(r_ins, r_bufs, r_new, sems)

    all_aliases = dict(aliases or {})
    for j in range(n_rb):
        all_aliases[n_in + n_ri + j] = n_out + j
    res = pl.pallas_call(
        carrying, grid=grid, in_specs=list(in_specs) + [ANY] * (n_ri + n_rb), out_specs=o_specs + [ANY] * (n_rb + n_rn),
        out_shape=out_shapes + [SDS(b.shape, b.dtype) for b in ride.bufs] + ride.new_outs,
        scratch_shapes=list(scratch_shapes) + [pltpu.SemaphoreType.DMA(s) for s in ride.sem_shapes],
        input_output_aliases=all_aliases, name=name, compiler_params=_params())(*operands, *ride.ins, *ride.bufs)
    outs = res[:n_out]
    return (list(outs) if multi else outs[0]), list(res[n_out:])


def _run_ride(name, ride):
    n_ri, n_rb, n_rn = len(ride.ins), len(ride.bufs), len(ride.new_outs)

    def body(*refs):
        r_ins = refs[:n_ri]
        r_bufs = refs[n_ri + n_rb:n_ri + 2 * n_rb]
        r_new = refs[n_ri + 2 * n_rb:n_ri + 2 * n_rb + n_rn]
        sems = refs[n_ri + 2 * n_rb + n_rn:]
        ride.start(r_ins, r_bufs, r_new, sems)
        ride.finish(r_ins, r_bufs, r_new, sems)

    return list(pl.pallas_call(
        body, in_specs=[ANY] * (n_ri + n_rb), out_specs=[ANY] * (n_rb + n_rn),
        out_shape=[SDS(b.shape, b.dtype) for b in ride.bufs] + ride.new_outs,
        scratch_shapes=[pltpu.SemaphoreType.DMA(s) for s in ride.sem_shapes],
        input_output_aliases={n_ri + j: j for j in range(n_rb)}, name=name)(*ride.ins, *ride.bufs))


def _whole(ref, p):
    return ref[...]


def _slab(ref, p):
    return ref[p]


def _matmul(name, grid, pairs, extras, outs, dims, epi, *, slabs=1, n_acc=1, ride=None):
    n_pairs, n_ex, n_out = len(pairs), len(extras), len(outs)

    def body(*refs):
        ab = refs[:2 * n_pairs]
        ex = refs[2 * n_pairs:2 * n_pairs + n_ex]
        out_refs = refs[2 * n_pairs + n_ex:2 * n_pairs + n_ex + n_out]
        pids = tuple(pl.program_id(a) for a in range(len(grid)))
        parts = [None] * n_acc
        for p in range(slabs):
            for j, (_, _, a_pick, _, _, b_pick, acc) in enumerate(pairs):
                d = _dot(a_pick(ab[2 * j], p), b_pick(ab[2 * j + 1], p), dims)
                parts[acc] = d if parts[acc] is None else parts[acc] + d
        epi(parts, ex, out_refs, pids)

    operands, in_specs = [], []
    for a, a_spec, _, b, b_spec, _, _ in pairs:
        operands += [a, b]
        in_specs += [a_spec, b_spec]
    for e, e_spec in extras:
        operands.append(e)
        in_specs.append(e_spec)
    return _call(body, grid=grid, in_specs=in_specs, out_specs=[o[1] for o in outs], out_shape=[o[0] for o in outs],
                 operands=operands, name=name, ride=ride)


def _rms_rows(v, g):
    r = lax.rsqrt(jnp.mean(v * v, axis=-1, keepdims=True) + EPS)
    return v * r * g


def _rms_fwd(name, x, g):
    t, d = x.shape
    tm = 512

    def body(x_ref, g_ref, o_ref):
        o_ref[...] = _rms_rows(x_ref[...], g_ref[...]).astype(BF16)

    return pl.pallas_call(
        body, grid=(t // tm,),
        in_specs=[pl.BlockSpec((tm, d), lambda i: (i, 0)), pl.BlockSpec((1, d), lambda i: (0, 0))],
        out_specs=pl.BlockSpec((tm, d), lambda i: (i, 0)), out_shape=SDS((t, d), BF16), name=name,
        compiler_params=_params())(x, g)


def _epi_residual_norm(accs, ex, outs, pids):
    x_new = accs[0] + ex[0][...]
    outs[0][...] = x_new
    outs[1][...] = _rms_rows(x_new, ex[1][...]).astype(BF16)


def _epi_residual_loss(accs, ex, outs, pids):
    y = accs[0] + ex[0][...]
    err = y - ex[1][...]
    dy = err * (1.0 / err.shape[-1])
    outs[0][...] = dy
    outs[2][...] = dy.astype(BF16)

    @pl.when(pids[0] == 0)
    def _():
        outs[1][...] = jnp.zeros_like(outs[1])

    outs[1][...] += jnp.sum(err * err)


def _epi_rms_bwd(accs, ex, outs, pids):
    dh = accs[0]
    xv, g, res = ex[0][...], ex[1][...], ex[2][...]
    r = lax.rsqrt(jnp.mean(xv * xv, axis=-1, keepdims=True) + EPS)
    xh = xv * r
    dy = dh * g
    dx = res + r * (dy - xh * jnp.mean(dy * xh, axis=-1, keepdims=True))
    outs[0][...] = dx
    if len(outs) > 2:
        outs[2][...] = dx.astype(BF16)

    @pl.when(pids[0] == 0)
    def _():
        outs[1][...] = jnp.zeros_like(outs[1])

    outs[1][...] += jnp.sum(dh * xh, axis=0, keepdims=True)


def _row_spec(tm, d):
    return pl.BlockSpec((tm, d), lambda i, *_: (i, 0))


def _const_spec(shape):
    nd = len(shape)
    return pl.BlockSpec(shape, lambda *_: (0,) * nd)


def _proj_in(name, h, w, layer, ride=None):
    t, d = h.shape
    n4 = w.shape[-1]
    tm = TILES["proj_in"]

    def epi(accs, ex, outs, pids):
        outs[0][...] = accs[0].astype(BF16)

    res = _matmul(
        name, (N_CHIPS, t // tm),
        [(h, pl.BlockSpec((tm, d), lambda p, i: (i, 0)), _whole,
          w, pl.BlockSpec((None, None, d, n4), lambda p, i: (p, layer, 0, 0)), _whole, 0)],
        [], [(SDS((t, N_CHIPS * n4), BF16), pl.BlockSpec((tm, n4), lambda p, i: (i, p)))],
        NN, epi, ride=ride)
    return res[0] if ride is None else (res[0][0], res[1])


def _proj_out(name, a, w, x, g_next=None, target=None):
    t, k = a.shape
    d = w.shape[-1]
    tm = TILES["proj_out"]
    if target is None:
        extras = [(x, _row_spec(tm, d)), (g_next, _const_spec((1, d)))]
        outs = [(SDS((t, d), F32), _row_spec(tm, d)), (SDS((t, d), BF16), _row_spec(tm, d))]
        epi = _epi_residual_norm
    else:
        extras = [(x, _row_spec(tm, d)), (target, _row_spec(tm, d))]
        outs = [(SDS((t, d), F32), _row_spec(tm, d)), (SDS((8, LANES), F32), _const_spec((8, LANES))),
                (SDS((t, d), BF16), _row_spec(tm, d))]
        epi = _epi_residual_loss
    return _matmul(name, (t // tm,), [(a, _row_spec(tm, k), _whole, w, _const_spec((k, d)), _whole, 0)], extras, outs, NN, epi)


def _ffn_in(name, h, wg, wu, layer, ride=None):
    t, d = h.shape
    n4 = wg.shape[-2]
    tm = TILES["ffn_in"]

    def epi(accs, ex, outs, pids):
        gate, up = accs
        s = _sigmoid(gate)
        silu = gate * s
        outs[0][...] = (up * (s + silu - silu * s)).astype(BF16)
        outs[1][...] = silu.astype(BF16)
        outs[2][...] = (silu * up).astype(BF16)

    w_spec = pl.BlockSpec((None, None, n4, d), lambda p, i: (p, layer, 0, 0))
    h_spec = pl.BlockSpec((tm, d), lambda p, i: (i, 0))
    o = (SDS((N_CHIPS, t, n4), BF16), pl.BlockSpec((None, tm, n4), lambda p, i: (p, i, 0)))
    return _matmul(name, (N_CHIPS, t // tm),
                   [(h, h_spec, _whole, wg, w_spec, _whole, 0), (h, h_spec, _whole, wu, w_spec, _whole, 1)], [],
                   [o, o, o], NT, epi, n_acc=2, ride=ride)


def _ffn_out(name, act, wd, layer, x, g_next=None, target=None, ride=None):
    _, t, n4 = act.shape
    d = wd.shape[-1]
    tm = TILES["ffn_out"]
    xs = _row_spec(tm, d)
    if target is None:
        extras = [(x, xs), (g_next, _const_spec((1, d)))]
        outs = [(SDS((t, d), F32), xs), (SDS((t, d), BF16), xs)]
        epi = _epi_residual_norm
    else:
        extras = [(x, xs), (target, xs)]
        outs = [(SDS((t, d), F32), xs), (SDS((8, LANES), F32), _const_spec((8, LANES))), (SDS((t, d), BF16), xs)]
        epi = _epi_residual_loss
    return _matmul(
        name, (t // tm,),
        [(act, pl.BlockSpec((N_CHIPS, tm, n4), lambda i: (0, i, 0)), _slab,
          wd, pl.BlockSpec((N_CHIPS, None, n4, d), lambda i: (0, layer, 0, 0)), _slab, 0)],
        extras, outs, NN, epi, slabs=N_CHIPS, ride=ride)


def _ffn_dact(name, g, wd, layer, gate, up, ride=None):
    t, d = g.shape
    n4 = wd.shape[-2]
    tm = TILES["ffn_dact"]

    def body(g_ref, w_ref, gate_ref, up_ref, dgate_ref, dup_ref):
        gv = g_ref[...]
        for p in range(N_CHIPS):
            dact = _dot(gv, w_ref[p], NT)
            dgate_ref[p] = (dact * gate_ref[p].astype(F32)).astype(BF16)
            dup_ref[p] = (dact * up_ref[p].astype(F32)).astype(BF16)

    blk = pl.BlockSpec((N_CHIPS, tm, n4), lambda i: (0, i, 0))
    return _call(
        body, grid=(t // tm,),
        in_specs=[_row_spec(tm, d), pl.BlockSpec((N_CHIPS, None, n4, d), lambda i: (0, layer, 0, 0)), blk, blk],
        out_specs=[blk, blk], out_shape=[SDS((N_CHIPS, t, n4), BF16)] * 2, operands=[g, wd, gate, up], name=name, ride=ride)


def _copy_epi(accs, ex, outs, pids):
    for a, o in zip(accs, outs):
        o[...] = a.astype(o.dtype)


def _dgrad_cols(name, dz_list, w_list, layer, three_d, x, g, res, bf16_copy=True, w_rows=False, ride=None):
    t, d = x.shape
    n4 = w_list[0].shape[-2 if w_rows else -1]
    tm = TILES["dgrad_cols"]
    if three_d:
        zs, z_pick = pl.BlockSpec((N_CHIPS, tm, n4), lambda i: (0, i, 0)), _slab
    else:
        zs, z_pick = _row_spec(tm, N_CHIPS * n4), (lambda ref, p: ref[:, p * n4:(p + 1) * n4])
    ws = pl.BlockSpec((N_CHIPS, None) + ((n4, d) if w_rows else (d, n4)), lambda i: (0, layer, 0, 0))
    xs = _row_spec(tm, d)
    return _matmul(
        name, (t // tm,), [(dz, zs, z_pick, w, ws, _slab, 0) for dz, w in zip(dz_list, w_list)],
        [(x, xs), (g, _const_spec((1, d))), (res, xs)],
        [(SDS((t, d), F32), xs), (SDS((1, d), F32), _const_spec((1, d)))] + ([(SDS((t, d), BF16), xs)] if bf16_copy else []),
        NN if w_rows else NT, _epi_rms_bwd, slabs=N_CHIPS, ride=ride)


def _dgrad_rows(name, g, w):
    t, d = g.shape
    k = w.shape[0]
    tm = TILES["dgrad_rows"]
    return _matmul(name, (t // tm,), [(g, _row_spec(tm, d), _whole, w, _const_spec((k, d)), _whole, 0)], [],
                   [(SDS((t, k), F32), _row_spec(tm, k))], NT, _copy_epi)[0]


A_TILE = 256


def _a_common(p_ref, lg_ref, lb_ref):
    pv = p_ref[...].astype(F32)
    a = _gelu(pv)
    u, v = a[:, :A_WIDTH], a[:, A_WIDTH:]
    vc = v - jnp.mean(v, axis=-1, keepdims=True)
    rs = lax.rsqrt(jnp.mean(vc * vc, axis=-1, keepdims=True) + EPS)
    vhat = vc * rs
    vn = vhat * lg_ref[...] + lb_ref[...]
    return pv, u, vhat, rs, vn.astype(BF16)


def _tril_weights(w_ref, g):
    r = lax.broadcasted_iota(jnp.int32, (CHUNK, CHUNK), 0)
    c = lax.broadcasted_iota(jnp.int32, (CHUNK, CHUNK), 1)
    return jnp.where(c <= r, w_ref[g], 0.0).astype(BF16), c <= r


def _mixer_a_fwd(proj, lg, lb, w, bias_t):
    t = proj.shape[0]

    def body(p_ref, lg_ref, lb_ref, w_ref, bt_ref, o_ref):
        _, u, _, _, vnb = _a_common(p_ref, lg_ref, lb_ref)
        for g in range(A_GROUPS):
            wt, _ = _tril_weights(w_ref, g)
            cs = slice(g * CHUNK, (g + 1) * CHUNK)
            for ch in range(A_TILE // CHUNK):
                rs_ = slice(ch * CHUNK, (ch + 1) * CHUNK)
                mixed = _dot(wt, vnb[rs_, cs], NN) + bt_ref[:, g:g + 1]
                o_ref[rs_, cs] = (u[rs_, cs] * mixed).astype(BF16)

    return pl.pallas_call(
        body, grid=(t // A_TILE,),
        in_specs=[pl.BlockSpec((A_TILE, 2 * A_WIDTH), lambda i: (i, 0)), _const_spec((1, A_WIDTH)),
                  _const_spec((1, A_WIDTH)), _const_spec((A_GROUPS, CHUNK, CHUNK)), _const_spec((CHUNK, A_GROUPS))],
        out_specs=pl.BlockSpec((A_TILE, A_WIDTH), lambda i: (i, 0)), out_shape=SDS((t, A_WIDTH), BF16),
        name="mixer_a_fwd", compiler_params=_params())(proj, lg, lb, w, bias_t)


def _mixer_a_bwd(proj, dcat, lg, lb, w, bias_t):
    t = proj.shape[0]

    def body(p_ref, da_ref, lg_ref, lb_ref, w_ref, bt_ref, dp_ref, dw_ref, dbt_ref, dlg_ref, dlb_ref, du_scr, dvn_scr):
        @pl.when(pl.program_id(0) == 0)
        def _():
            dw_ref[...] = jnp.zeros_like(dw_ref)
            dbt_ref[...] = jnp.zeros_like(dbt_ref)
            dlg_ref[...] = jnp.zeros_like(dlg_ref)
            dlb_ref[...] = jnp.zeros_like(dlb_ref)

        pv, u, vhat, rs, vnb = _a_common(p_ref, lg_ref, lb_ref)
        da = da_ref[...]
        for g in range(A_GROUPS):
            wt, keep = _tril_weights(w_ref, g)
            cs = slice(g * CHUNK, (g + 1) * CHUNK)
            for ch in range(A_TILE // CHUNK):
                rs_ = slice(ch * CHUNK, (ch + 1) * CHUNK)
                vg = vnb[rs_, cs]
                mixed = _dot(wt, vg, NN) + bt_ref[:, g:g + 1]
                du_scr[rs_, cs] = da[rs_, cs] * mixed
                dmx = da[rs_, cs] * u[rs_, cs]
                dw_ref[g] += jnp.where(keep, _dot(dmx, vg, NT), 0.0)
                dvn_scr[rs_, cs] = _dot(wt, dmx, TN)
                dbt_ref[:, g:g + 1] += jnp.sum(dmx, axis=1, keepdims=True)
        dvn = dvn_scr[...]
        dlg_ref[...] += jnp.sum(dvn * vhat, axis=0, keepdims=True)
        dlb_ref[...] += jnp.sum(dvn, axis=0, keepdims=True)
        dvh = dvn * lg_ref[...]
        dv = rs * (dvh - jnp.mean(dvh, axis=-1, keepdims=True) - vhat * jnp.mean(dvh * vhat, axis=-1, keepdims=True))
        gp = _gelu_grad(pv)
        dp_ref[:, :A_WIDTH] = (du_scr[...] * gp[:, :A_WIDTH]).astype(BF16)
        dp_ref[:, A_WIDTH:] = (dv * gp[:, A_WIDTH:]).astype(BF16)

    return pl.pallas_call(
        body, grid=(t // A_TILE,),
        in_specs=[pl.BlockSpec((A_TILE, 2 * A_WIDTH), lambda i: (i, 0)), pl.BlockSpec((A_TILE, A_WIDTH), lambda i: (i, 0)),
                  _const_spec((1, A_WIDTH)), _const_spec((1, A_WIDTH)), _const_spec((A_GROUPS, CHUNK, CHUNK)),
                  _const_spec((CHUNK, A_GROUPS))],
        out_specs=[pl.BlockSpec((A_TILE, 2 * A_WIDTH), lambda i: (i, 0)), _const_spec((A_GROUPS, CHUNK, CHUNK)),
                   _const_spec((CHUNK, A_GROUPS)), _const_spec((1, A_WIDTH)), _const_spec((1, A_WIDTH))],
        out_shape=[SDS((t, 2 * A_WIDTH), BF16), SDS((A_GROUPS, CHUNK, CHUNK), F32), SDS((CHUNK, A_GROUPS), F32),
                   SDS((1, A_WIDTH), F32), SDS((1, A_WIDTH), F32)],
        scratch_shapes=[pltpu.VMEM((A_TILE, A_WIDTH), F32), pltpu.VMEM((A_TILE, A_WIDTH), F32)],
        name="mixer_a_bwd", compiler_params=_params())(proj, dcat, lg, lb, w, bias_t)


def _rope_tables(t):
    half = ROPE_DIM // 2
    inv_freq = ROPE_THETA ** (-jnp.arange(half, dtype=F32) * 2.0 / ROPE_DIM)
    ang = jnp.arange(t, dtype=F32)[:, None] * inv_freq[None, :]
    cos, sin = jnp.cos(ang), jnp.sin(ang)
    one = jnp.ones((t, HEAD_DIM - ROPE_DIM), F32)
    zero = jnp.zeros((t, HEAD_DIM - ROPE_DIM), F32)
    zh = jnp.zeros((t, half), F32)
    c = jnp.concatenate([cos, cos, one], axis=1)
    s1 = jnp.concatenate([-sin, zh, zero], axis=1)
    s2 = jnp.concatenate([zh, sin, zero], axis=1)
    return tuple(jnp.tile(a, (1, LANES // HEAD_DIM)) for a in (c, s1, s2))


QK_TILE = 512
QK_ROWS = 64
QK_COLS = 2 * N_DIL * B_WIDTH


CHUNKS = B_WIDTH // LANES


def _regroup_out(scr, first, out_ref, rate, tile):
    rows = tile // rate
    for rho in range(rate):
        for c in range(CHUNKS):
            out_ref[rho, :, c * LANES:(c + 1) * LANES] = scr[first + c, pl.ds(rho, rows, stride=rate), :].astype(out_ref.dtype)


def _regroup_in(x_ref, scr, rate, tile):
    rows = tile // rate
    for rho in range(rate):
        for c in range(CHUNKS):
            scr[c, pl.ds(rho, rows, stride=rate), :] = x_ref[rho, :, c * LANES:(c + 1) * LANES].astype(F32)


def _regrouped_spec(rate, tile):
    return pl.BlockSpec((rate, tile // rate, B_WIDTH), lambda i, *_: (0, i, 0))


def _qk_fwd(proj, gains, tabs, ride=None):
    t = proj.shape[0]
    col0 = 2 * A_WIDTH // 1024
    r1, r2 = DIL_RATES[1], DIL_RATES[2]

    def body(p_ref, g_ref, c_ref, s1_ref, s2_ref, o_ref, q1_ref, q2_ref, k1_ref, k2_ref, scr):
        seg = _segment_mean_matrix(HEAD_DIM)
        for r0 in range(0, QK_TILE, QK_ROWS):
            rows = slice(r0, r0 + QK_ROWS)
            c, s1, s2 = c_ref[rows, :], s1_ref[rows, :], s2_ref[rows, :]
            for ci in range(1024 // LANES):
                ls = slice(ci * LANES, (ci + 1) * LANES)
                xv = p_ref[rows, ls].astype(F32)
                r = lax.rsqrt(_segment_dot(xv * xv, seg) + EPS)
                y = xv * r * g_ref[:, ls]
                val = y * c + pltpu.roll(y, LANES - 8, axis=1) * s1 + pltpu.roll(y, 8, axis=1) * s2
                o_ref[rows, ls] = val.astype(BF16)
                scr[ci, rows, :] = val

        j = pl.program_id(1)

        @pl.when(j == 0)
        def _():
            _regroup_out(scr, CHUNKS, q1_ref, r1, QK_TILE)

        @pl.when(j == 1)
        def _():
            _regroup_out(scr, 0, q2_ref, r2, QK_TILE)

        @pl.when(j == 2)
        def _():
            _regroup_out(scr, 0, k1_ref, r1, QK_TILE)
            _regroup_out(scr, CHUNKS, k2_ref, r2, QK_TILE)

    tab = pl.BlockSpec((QK_TILE, LANES), lambda i, j: (i, 0))
    g1, g2 = SDS((r1, t // r1, B_WIDTH), BF16), SDS((r2, t // r2, B_WIDTH), BF16)
    s1_, s2_ = _regrouped_spec(r1, QK_TILE), _regrouped_spec(r2, QK_TILE)
    return _call(
        body, grid=(t // QK_TILE, QK_COLS // 1024),
        in_specs=[pl.BlockSpec((QK_TILE, 1024), lambda i, j: (i, col0 + j)), pl.BlockSpec((1, 1024), lambda i, j: (0, j)),
                  tab, tab, tab],
        out_specs=[pl.BlockSpec((QK_TILE, 1024), lambda i, j: (i, j)), s1_, s2_, s1_, s2_],
        out_shape=[SDS((t, QK_COLS), BF16), g1, g2, g1, g2],
        scratch_shapes=[pltpu.VMEM((2 * CHUNKS, QK_TILE, LANES), F32)],
        operands=[proj, gains, *tabs], name="qk_norm_rope_fwd", ride=ride)


PERM_TILE = 512


def _permute(name, items, rate):
    t = items[0][0].shape[0]
    n = len(items)

    def body(*refs):
        scr = refs[-1]
        for x_ref, o_ref in zip(refs[:n], refs[n:2 * n]):
            for ci in range(CHUNKS):
                scr[ci] = x_ref[:, ci * LANES:(ci + 1) * LANES].astype(F32)
            _regroup_out(scr, 0, o_ref, rate, PERM_TILE)

    return pl.pallas_call(
        body, grid=(t // PERM_TILE,),
        in_specs=[pl.BlockSpec((PERM_TILE, B_WIDTH), functools.partial(lambda cb, i: (i, cb), cb)) for _, cb in items],
        out_specs=[_regrouped_spec(rate, PERM_TILE) for _ in items],
        out_shape=[SDS((rate, t // rate, B_WIDTH), a.dtype) for a, _ in items],
        scratch_shapes=[pltpu.VMEM((CHUNKS, PERM_TILE, LANES), F32)],
        name=name, compiler_params=_params())(*[a for a, _ in items])


def _head_lane_mask(h):
    lane = lax.broadcasted_iota(jnp.int32, (1, LANES), 1)
    return (lane < HEAD_DIM) if h == 0 else (lane >= HEAD_DIM)


def _attn_fwd(name, q, k, v, ride=None):
    rate, length = q[0].shape[0], q[0].shape[1]
    nb = length // ATT_BLOCK
    scale = HEAD_DIM ** -0.5

    def body(q_ref, kc_ref, kp_ref, vc_ref, vp_ref, o_ref, l_ref):
        n = pl.program_id(1)
        qi = lax.broadcasted_iota(jnp.int32, (ATT_BLOCK, 2 * ATT_BLOCK), 0)
        cj = lax.broadcasted_iota(jnp.int32, (ATT_BLOCK, 2 * ATT_BLOCK), 1)
        has_prev = jnp.where(n > 0, 0, 2 * ATT_BLOCK)
        mask = ((cj < ATT_BLOCK) & (cj >= qi + has_prev)) | ((cj >= ATT_BLOCK) & (cj - ATT_BLOCK <= qi))
        heads = [(hp, h) for hp in range(CHUNKS) for h in range(2)]
        q2, k2, v2 = {}, {}, {}
        for hp in range(CHUNKS):
            ls = slice(hp * LANES, (hp + 1) * LANES)
            q2[hp] = q_ref[:, ls]
            k2[hp] = jnp.concatenate([kp_ref[:, ls], kc_ref[:, ls]], axis=0)
            v2[hp] = jnp.concatenate([vp_ref[:, ls], vc_ref[:, ls]], axis=0)
        scores = {}
        for hp, h in heads:
            scores[hp, h] = _dot(jnp.where(_head_lane_mask(h), q2[hp], jnp.zeros_like(q2[hp])), k2[hp], NT) * scale
        probs, lses = {}, {}
        for hp, h in heads:
            s = jnp.where(mask, scores[hp, h], NEG_INF)
            m = jnp.max(s, axis=1, keepdims=True)
            p = jnp.exp(s - m)
            den = jnp.sum(p, axis=1, keepdims=True)
            lses[hp, h] = m + jnp.log(den)
            probs[hp, h] = (p / den).astype(BF16)
        for hp in range(CHUNKS):
            ls = slice(hp * LANES, (hp + 1) * LANES)
            o_acc = None
            for h in range(2):
                o = _dot(probs[hp, h], jnp.where(_head_lane_mask(h), v2[hp], jnp.zeros_like(v2[hp])), NN)
                o_acc = o if o_acc is None else o_acc + o
            o_ref[:, ls] = o_acc
            zeros = jnp.zeros((ATT_BLOCK, LANES), F32)
            l_ref[:, ls] = jnp.where(_head_lane_mask(1), lses[hp, 1] + zeros, lses[hp, 0] + zeros)

    def cur(cb):
        return pl.BlockSpec((None, ATT_BLOCK, B_WIDTH), lambda r, n: (r, n, cb))

    def prev(cb):
        return pl.BlockSpec((None, ATT_BLOCK, B_WIDTH), lambda r, n: (r, jnp.maximum(n - 1, 0), cb))

    out = pl.BlockSpec((None, ATT_BLOCK, B_WIDTH), lambda r, n: (r, n, 0))
    return _call(
        body, grid=(rate, nb),
        in_specs=[cur(q[1]), cur(k[1]), prev(k[1]), cur(v[1]), prev(v[1])],
        out_specs=[out, out], out_shape=[SDS((rate, length, B_WIDTH), F32)] * 2,
        operands=[q[0], k[0], k[0], v[0], v[0]], name=name, ride=ride)


def _attn_merge(a_out, o_list, l_list):
    t = a_out.shape[0]
    tm = PERM_TILE
    r1, r2 = DIL_RATES[1], DIL_RATES[2]

    def body(a_ref, o0, o1, o2, l0, l1, l2, cat_ref, lt_ref, lt1_ref, lt2_ref, so1, so2, sl1, sl2, slt):
        _regroup_in(o1, so1, r1, tm)
        _regroup_in(l1, sl1, r1, tm)
        _regroup_in(o2, so2, r2, tm)
        _regroup_in(l2, sl2, r2, tm)
        cat_ref[:, :A_WIDTH] = a_ref[...]
        for c in range(CHUNKS):
            ls = slice(c * LANES, (c + 1) * LANES)
            lg = [l0[:, ls], sl1[c], sl2[c]]
            m = jnp.maximum(jnp.maximum(lg[0], lg[1]), lg[2])
            es = [jnp.exp(l - m) for l in lg]
            den = es[0] + es[1] + es[2]
            b = (es[0] * o0[:, ls] + es[1] * so1[c] + es[2] * so2[c]) / den
            cat_ref[:, A_WIDTH + c * LANES:A_WIDTH + (c + 1) * LANES] = b.astype(BF16)
            lt = m + jnp.log(den)
            lt_ref[:, ls] = lt
            slt[c] = lt
        _regroup_out(slt, 0, lt1_ref, r1, tm)
        _regroup_out(slt, 0, lt2_ref, r2, tm)

    blk = _row_spec(tm, B_WIDTH)
    g1, g2 = _regrouped_spec(r1, tm), _regrouped_spec(r2, tm)
    return pl.pallas_call(
        body, grid=(t // tm,), in_specs=[blk, blk, g1, g2, blk, g1, g2],
        out_specs=[_row_spec(tm, A_WIDTH + B_WIDTH), blk, g1, g2],
        out_shape=[SDS((t, A_WIDTH + B_WIDTH), BF16), SDS((t, B_WIDTH), F32), SDS((r1, t // r1, B_WIDTH), F32),
                   SDS((r2, t // r2, B_WIDTH), F32)],
        scratch_shapes=[pltpu.VMEM((CHUNKS, tm, LANES), F32)] * 5,
        name="attn_merge", compiler_params=_params())(a_out, *o_list, *l_list)


def _attn_bwd_prep(dcat, cat):
    t = dcat.shape[0]
    tm = PERM_TILE
    r1, r2 = DIL_RATES[1], DIL_RATES[2]

    def body(d_ref, b_ref, db_ref, dd_ref, db1_ref, dd1_ref, db2_ref, dd2_ref, sdb, sdd):
        seg = _segment_mean_matrix(HEAD_DIM, scale=1.0)
        for c in range(CHUNKS):
            ls = slice(c * LANES, (c + 1) * LANES)
            d = d_ref[:, ls]
            dsum = _segment_dot(d * b_ref[:, ls].astype(F32), seg)
            db_ref[:, ls] = d.astype(BF16)
            dd_ref[:, ls] = dsum
            sdb[c] = d
            sdd[c] = dsum
        _regroup_out(sdb, 0, db1_ref, r1, tm)
        _regroup_out(sdd, 0, dd1_ref, r1, tm)
        _regroup_out(sdb, 0, db2_ref, r2, tm)
        _regroup_out(sdd, 0, dd2_ref, r2, tm)

    right = pl.BlockSpec((tm, B_WIDTH), lambda i: (i, 1))
    blk = _row_spec(tm, B_WIDTH)
    g1, g2 = _regrouped_spec(r1, tm), _regrouped_spec(r2, tm)
    return pl.pallas_call(
        body, grid=(t // tm,), in_specs=[right, right], out_specs=[blk, blk, g1, g1, g2, g2],
        out_shape=[SDS((t, B_WIDTH), BF16), SDS((t, B_WIDTH), F32), SDS((r1, t // r1, B_WIDTH), BF16),
                   SDS((r1, t // r1, B_WIDTH), F32), SDS((r2, t // r2, B_WIDTH), BF16), SDS((r2, t // r2, B_WIDTH), F32)],
        scratch_shapes=[pltpu.VMEM((CHUNKS, tm, LANES), F32)] * 2,
        name="attn_bwd_prep", compiler_params=_params())(dcat, cat)


def _attn_bwd(name, q, k, v, db, lse, dd, ride=None):
    rate, length = db.shape[0], db.shape[1]
    nb = length // ATT_BLOCK
    scale = HEAD_DIM ** -0.5

    def body(qa_ref, qb_ref, k_ref, v_ref, dba_ref, dbb_ref, la_ref, lb_ref, da_ref, dbd_ref, dq_ref, dk_ref, dv_ref, carry):
        m = pl.program_id(1)

        @pl.when(m == 0)
        def _():
            carry[...] = jnp.zeros_like(carry)

        row = lax.broadcasted_iota(jnp.int32, (2 * ATT_BLOCK, ATT_BLOCK), 0)
        kj = lax.broadcasted_iota(jnp.int32, (2 * ATT_BLOCK, ATT_BLOCK), 1)
        no_next = jnp.where(m + 1 < nb, 0, 2 * ATT_BLOCK)
        mask = ((row < ATT_BLOCK) & (kj <= row)) | ((row >= ATT_BLOCK) & (kj >= row - ATT_BLOCK + no_next))
        heads = [(hp, h) for hp in range(CHUNKS) for h in range(2)]
        q2, db2, lse2, dd2, k2, v2 = {}, {}, {}, {}, {}, {}
        for hp in range(CHUNKS):
            ls = slice(hp * LANES, (hp + 1) * LANES)
            k2[hp], v2[hp] = k_ref[:, ls], v_ref[:, ls]
            q2[hp] = jnp.concatenate([qa_ref[:, ls], qb_ref[:, ls]], axis=0)
            db2[hp] = jnp.concatenate([dba_ref[:, ls], dbb_ref[:, ls]], axis=0)
            lse2[hp] = jnp.concatenate([la_ref[:, ls], lb_ref[:, ls]], axis=0)
            dd2[hp] = jnp.concatenate([da_ref[:, ls], dbd_ref[:, ls]], axis=0)
        km, scores, dps = {}, {}, {}
        for hp, h in heads:
            hm = _head_lane_mask(h)
            km[hp, h] = jnp.where(hm, k2[hp], jnp.zeros_like(k2[hp]))
            scores[hp, h] = _dot(q2[hp], km[hp, h], NT) * scale
            dps[hp, h] = _dot(db2[hp], jnp.where(hm, v2[hp], jnp.zeros_like(v2[hp])), NT)
        probs, dss = {}, {}
        for hp, h in heads:
            hm = _head_lane_mask(h)
            lse_col = jnp.max(jnp.where(hm, lse2[hp], NEG_INF), axis=1, keepdims=True)
            dd_col = jnp.max(jnp.where(hm, dd2[hp], NEG_INF), axis=1, keepdims=True)
            p = jnp.where(mask, jnp.exp(scores[hp, h] - lse_col), 0.0)
            probs[hp, h] = p.astype(BF16)
            dss[hp, h] = (p * (dps[hp, h] - dd_col) * scale).astype(BF16)
        for hp in range(CHUNKS):
            ls = slice(hp * LANES, (hp + 1) * LANES)
            dq_acc, dk_acc, dv_acc = None, None, None
            for h in range(2):
                hm = _head_lane_mask(h)
                dvc = _dot(probs[hp, h], jnp.where(hm, db2[hp], jnp.zeros_like(db2[hp])), TN)
                dqc = _dot(dss[hp, h], km[hp, h], NN)
                dkc = _dot(dss[hp, h], jnp.where(hm, q2[hp], jnp.zeros_like(q2[hp])), TN)
                dq_acc = dqc if dq_acc is None else dq_acc + dqc
                dk_acc = dkc if dk_acc is None else dk_acc + dkc
                dv_acc = dvc if dv_acc is None else dv_acc + dvc
            dq_ref[:, ls] = (dq_acc[:ATT_BLOCK] + carry[:, ls]).astype(BF16)
            carry[:, ls] = dq_acc[ATT_BLOCK:]
            dk_ref[:, ls] = dk_acc.astype(BF16)
            dv_ref[:, ls] = dv_acc.astype(BF16)

    def cur(cb):
        return pl.BlockSpec((None, ATT_BLOCK, B_WIDTH), lambda r, n: (r, n, cb))

    def nxt(cb):
        return pl.BlockSpec((None, ATT_BLOCK, B_WIDTH), lambda r, n: (r, jnp.minimum(n + 1, nb - 1), cb))

    out = cur(0)
    return _call(
        body, grid=(rate, nb),
        in_specs=[cur(q[1]), nxt(q[1]), cur(k[1]), cur(v[1]), cur(0), nxt(0), cur(0), nxt(0), cur(0), nxt(0)],
        out_specs=[out, out, out], out_shape=[SDS((rate, length, B_WIDTH), BF16)] * 3,
        scratch_shapes=[pltpu.VMEM((ATT_BLOCK, B_WIDTH), F32)],
        operands=[q[0], q[0], k[0], v[0], db, db, lse, lse, dd, dd], name=name, ride=ride)


AB_IN = 2 * A_WIDTH + 3 * N_DIL * B_WIDTH
ASM_TILE = 256


def _dproj_assemble(proj, d_a, dq, dk, dv, gains, tabs, ride=None):
    t = proj.shape[0]
    n_in = 3 * N_DIL

    def body(p_ref, da_ref, *rest):
        grads = rest[:n_in]
        g_ref, c_ref, s1_ref, s2_ref, o_ref, dg_ref = rest[n_in:n_in + 6]
        scratch = rest[n_in + 6:]

        @pl.when(pl.program_id(0) == 0)
        def _():
            dg_ref[...] = jnp.zeros_like(dg_ref)

        chunk = {}
        k_scr = 0
        for j in range(n_in):
            g = j % N_DIL
            if DIL_RATES[g] == 1:
                for ci in range(CHUNKS):
                    chunk[j, ci] = functools.partial(lambda r, ci: r[:, ci * LANES:(ci + 1) * LANES].astype(F32), grads[j], ci)
            else:
                scr = scratch[k_scr]
                k_scr += 1
                _regroup_in(grads[j], scr, DIL_RATES[g], ASM_TILE)
                for ci in range(CHUNKS):
                    chunk[j, ci] = functools.partial(lambda s, ci: s[ci], scr, ci)

        seg = _segment_mean_matrix(HEAD_DIM)
        c, s1, s2 = c_ref[...], s1_ref[...], s2_ref[...]
        o_ref[:, :2 * A_WIDTH] = da_ref[...]
        for jg in range(2 * N_DIL):
            for ci in range(CHUNKS):
                col = jg * B_WIDTH + ci * LANES
                src = slice(2 * A_WIDTH + col, 2 * A_WIDTH + col + LANES)
                xv = p_ref[:, src].astype(F32)
                r = lax.rsqrt(_segment_dot(xv * xv, seg) + EPS)
                xh = xv * r
                gain = g_ref[:, col:col + LANES]
                do = chunk[jg, ci]()
                dy = do * c + pltpu.roll(do * s1, 8, axis=1) + pltpu.roll(do * s2, LANES - 8, axis=1)
                dg_ref[:, col:col + LANES] += jnp.sum(dy * xh, axis=0, keepdims=True)
                dxh = dy * gain
                o_ref[:, src] = (r * (dxh - xh * _segment_dot(dxh * xh, seg))).astype(BF16)
        v0 = 2 * A_WIDTH + QK_COLS
        for g in range(N_DIL):
            for ci in range(CHUNKS):
                col = v0 + g * B_WIDTH + ci * LANES
                o_ref[:, col:col + LANES] = chunk[2 * N_DIL + g, ci]().astype(BF16)

    specs = [_row_spec(ASM_TILE, B_WIDTH) if r == 1 else _regrouped_spec(r, ASM_TILE) for r in DIL_RATES] * 3
    n_scr = 3 * sum(1 for r in DIL_RATES if r > 1)
    tab = _row_spec(ASM_TILE, LANES)
    return _call(
        body, grid=(t // ASM_TILE,),
        in_specs=[_row_spec(ASM_TILE, AB_IN), _row_spec(ASM_TILE, 2 * A_WIDTH)] + specs
        + [_const_spec((1, QK_COLS)), tab, tab, tab],
        out_specs=[_row_spec(ASM_TILE, AB_IN), _const_spec((1, QK_COLS))],
        out_shape=[SDS((t, AB_IN), BF16), SDS((1, QK_COLS), F32)],
        scratch_shapes=[pltpu.VMEM((CHUNKS, ASM_TILE, LANES), F32)] * n_scr,
        operands=[proj, d_a, *dq, *dk, *dv, gains, *tabs], name="dproj_assemble", ride=ride)


def _fold_heads(dg_lane):
    n = dg_lane.shape[1]

    def body(x_ref, o_ref):
        r = lax.broadcasted_iota(jnp.int32, (B_WIDTH, B_WIDTH), 0) % HEAD_DIM
        c = lax.broadcasted_iota(jnp.int32, (B_WIDTH, B_WIDTH), 1) % HEAD_DIM
        fold = jnp.where(r == c, 1.0, 0.0).astype(F32)
        for jg in range(n // B_WIDTH):
            ls = slice(jg * B_WIDTH, (jg + 1) * B_WIDTH)
            o_ref[:, ls] = _dot_hi(jnp.broadcast_to(x_ref[:, ls], (8, B_WIDTH)), fold)

    return pl.pallas_call(body, out_shape=SDS((8, n), F32), name="fold_heads", compiler_params=_params())(dg_lane)


CD_TILE = 256
TAP_ROWS = 64
CD_IN = 2 * C_WIDTH + 3 * 512


def _shifted_copies(src, dst, rows):
    dst[0, :rows] = src[...]
    for b in range(1, 8):
        dst[b, :rows - 8] = src[pl.ds(b, rows - 8), :]


def _rows_from(shifted, start, n, lanes=slice(None)):
    b = start % 8
    return shifted[b, pl.ds(start - b, n), lanes]


def _mixer_cd_fwd(proj, cw, cb, lg, lb, dw):
    t = proj.shape[0]
    per = CD_TILE // HALO

    def body(h_ref, m_ref, cw_ref, cb_ref, lg_ref, lb_ref, dw_ref, o_ref, c1_ref, c_scr, e_scr, c_sh):
        not_first = (pl.program_id(0) > 0).astype(F32)
        lanes = [slice(c * LANES, (c + 1) * LANES) for c in range(C_WIDTH // LANES)]

        def col(ref, part, ls):
            return ref[:, part * C_WIDTH + ls.start:part * C_WIDTH + ls.stop].astype(F32)

        for ls in lanes:
            c_scr[:HALO, ls] = col(h_ref, 0, ls) * _sigmoid(col(h_ref, 1, ls)) * not_first
            c_scr[HALO:, ls] = col(m_ref, 0, ls) * _sigmoid(col(m_ref, 1, ls))
            e_scr[:HALO, ls] = col(h_ref, 3, ls) * col(h_ref, 4, ls) * not_first
            e_scr[HALO:, ls] = col(m_ref, 3, ls) * col(m_ref, 4, ls)
        _shifted_copies(c_scr, c_sh, HALO + CD_TILE)
        for ls in lanes:
            for r0 in range(0, CD_TILE, TAP_ROWS):
                acc = jnp.zeros((TAP_ROWS, LANES), F32)
                for k in range(C_KERNEL):
                    acc = acc + cw_ref[k:k + 1, ls] * _rows_from(c_sh, r0 + HALO - (C_KERNEL - 1) + k, TAP_ROWS, ls)
                c1_ref[r0:r0 + TAP_ROWS, ls] = acc + cb_ref[:, ls]
        mean = sum(jnp.sum(c1_ref[:, ls], axis=-1, keepdims=True) for ls in lanes) * (1.0 / C_WIDTH)
        var = sum(jnp.sum((c1_ref[:, ls] - mean) ** 2, axis=-1, keepdims=True) for ls in lanes) * (1.0 / C_WIDTH)
        rs = lax.rsqrt(var + EPS)
        for ls in lanes:
            c2 = (c1_ref[:, ls] - mean) * rs * lg_ref[:, ls] + lb_ref[:, ls]
            o_ref[:, ls] = (c2 * _sigmoid(c2)).astype(BF16)
            d1 = jnp.zeros((CD_TILE, LANES), F32)
            for k in range(D_KERNEL):
                d1 = d1 + dw_ref[k:k + 1, ls] * e_scr[pl.ds(HALO - (D_KERNEL - 1) + k, CD_TILE), ls]
            o_ref[:, C_WIDTH + ls.start:C_WIDTH + ls.stop] = (col(m_ref, 2, ls) * d1).astype(BF16)

    return pl.pallas_call(
        body, grid=(t // CD_TILE,),
        in_specs=[pl.BlockSpec((HALO, CD_IN), lambda i: (jnp.maximum(i * per - 1, 0), 0)), _row_spec(CD_TILE, CD_IN),
                  _const_spec((32, C_WIDTH)), _const_spec((1, C_WIDTH)), _const_spec((1, C_WIDTH)), _const_spec((1, C_WIDTH)),
                  _const_spec((8, C_WIDTH))],
        out_specs=[_row_spec(CD_TILE, 2 * C_WIDTH), _row_spec(CD_TILE, C_WIDTH)],
        out_shape=[SDS((t, 2 * C_WIDTH), BF16), SDS((t, C_WIDTH), F32)],
        scratch_shapes=[pltpu.VMEM((HALO + CD_TILE, C_WIDTH), F32)] * 2 + [pltpu.VMEM((8, HALO + CD_TILE, C_WIDTH), F32)],
        name="mixer_cd_fwd", compiler_params=_params())(proj, proj, cw, cb, lg, lb, dw)


def _mixer_cd_bwd(proj, dcat, c1, cw, lg, lb, dw, ride=None):
    t = proj.shape[0]
    per = CD_TILE // HALO
    nt = t // CD_TILE
    ext = CD_TILE + HALO

    def body(hp_ref, m_ref, hn_ref, dm_ref, dn_ref, c1m_ref, c1n_ref, cw_ref, lg_ref, lb_ref, dw_ref,
             dp_ref, dcw_ref, dcb_ref, dlg_ref, dlb_ref, ddw_ref, c_scr, e_scr, dc1_scr, dd1_scr, c_sh, dc1_sh, dcw_acc,
             dvh_scr, vhat_scr):
        i = pl.program_id(0)

        @pl.when(i == 0)
        def _():
            for r in (dcw_acc, dcb_ref, dlg_ref, dlb_ref, ddw_ref):
                r[...] = jnp.zeros_like(r)

        not_first = (i > 0).astype(F32)
        not_last = (i < nt - 1).astype(F32)
        main = slice(HALO, HALO + CD_TILE)
        lanes = [slice(c * LANES, (c + 1) * LANES) for c in range(C_WIDTH // LANES)]

        def col(ref, part, ls):
            return ref[:, part * C_WIDTH + ls.start:part * C_WIDTH + ls.stop].astype(F32)

        for ls in lanes:
            c_scr[:HALO, ls] = col(hp_ref, 0, ls) * _sigmoid(col(hp_ref, 1, ls)) * not_first
            c_scr[main, ls] = col(m_ref, 0, ls) * _sigmoid(col(m_ref, 1, ls))
            c_scr[HALO + CD_TILE:, ls] = col(hn_ref, 0, ls) * _sigmoid(col(hn_ref, 1, ls)) * not_last
            e_scr[:HALO, ls] = col(hp_ref, 3, ls) * col(hp_ref, 4, ls) * not_first
            e_scr[main, ls] = col(m_ref, 3, ls) * col(m_ref, 4, ls)
            e_scr[HALO + CD_TILE:, ls] = col(hn_ref, 3, ls) * col(hn_ref, 4, ls) * not_last
        _shifted_copies(c_scr, c_sh, 2 * HALO + CD_TILE)

        def c1_of(ls):
            return jnp.concatenate([c1m_ref[:, ls], c1n_ref[:, ls]], axis=0)

        mean = sum(jnp.sum(c1_of(ls), axis=-1, keepdims=True) for ls in lanes) * (1.0 / C_WIDTH)
        var = sum(jnp.sum((c1_of(ls) - mean) ** 2, axis=-1, keepdims=True) for ls in lanes) * (1.0 / C_WIDTH)
        rs = lax.rsqrt(var + EPS)
        sum_dvh, sum_dvh_vhat = 0.0, 0.0
        for ls in lanes:
            vhat = (c1_of(ls) - mean) * rs
            c2 = vhat * lg_ref[:, ls] + lb_ref[:, ls]
            sig = _sigmoid(c2)
            dc = jnp.concatenate([dm_ref[:, ls], dn_ref[:, ls] * not_last], axis=0)
            dc2 = dc * (sig * (1.0 + c2 * (1.0 - sig)))
            dvh = dc2 * lg_ref[:, ls]
            sum_dvh = sum_dvh + jnp.sum(dvh, axis=-1, keepdims=True)
            sum_dvh_vhat = sum_dvh_vhat + jnp.sum(dvh * vhat, axis=-1, keepdims=True)
            dvh_scr[:, ls] = dvh
            vhat_scr[:, ls] = vhat
            dlg_ref[:, ls] += jnp.sum((dc2 * vhat)[:CD_TILE], axis=0, keepdims=True)
            dlb_ref[:, ls] += jnp.sum(dc2[:CD_TILE], axis=0, keepdims=True)
        for ls in lanes:
            dc1 = rs * (dvh_scr[:, ls] - sum_dvh * (1.0 / C_WIDTH) - vhat_scr[:, ls] * (sum_dvh_vhat * (1.0 / C_WIDTH)))
            dc1_scr[:, ls] = dc1
            dcb_ref[:, ls] += jnp.sum(dc1[:CD_TILE], axis=0, keepdims=True)
        _shifted_copies(dc1_scr, dc1_sh, ext)
        for ls in lanes:
            for r0 in range(0, CD_TILE, TAP_ROWS):
                rows = slice(r0, r0 + TAP_ROWS)
                dc1_m = dc1_scr[rows, ls]
                dc0 = jnp.zeros((TAP_ROWS, LANES), F32)
                for k in range(C_KERNEL):
                    dc0 = dc0 + cw_ref[k:k + 1, ls] * _rows_from(dc1_sh, r0 + C_KERNEL - 1 - k, TAP_ROWS, ls)
                    prod = dc1_m * _rows_from(c_sh, r0 + HALO - (C_KERNEL - 1) + k, TAP_ROWS, ls)
                    dcw_acc[k, :, ls] += prod.reshape(TAP_ROWS // 8, 8, LANES).sum(axis=0)
                g_m = m_ref[rows, C_WIDTH + ls.start:C_WIDTH + ls.stop].astype(F32)
                a_m = m_ref[rows, ls].astype(F32)
                sig_m = _sigmoid(g_m)
                dp_ref[rows, ls] = (dc0 * sig_m).astype(BF16)
                dp_ref[rows, C_WIDTH + ls.start:C_WIDTH + ls.stop] = (dc0 * a_m * sig_m * (1.0 - sig_m)).astype(BF16)

        @pl.when(i == nt - 1)
        def _():
            dcw_ref[...] = jnp.sum(dcw_acc[...], axis=1)

        for ls in lanes:
            wide = slice(C_WIDTH + ls.start, C_WIDTH + ls.stop)
            d1 = jnp.zeros((CD_TILE, LANES), F32)
            for k in range(D_KERNEL):
                d1 = d1 + dw_ref[k:k + 1, ls] * e_scr[pl.ds(HALO - (D_KERNEL - 1) + k, CD_TILE), ls]
            dd_m = dm_ref[:, wide]
            dd1 = jnp.concatenate([dd_m * col(m_ref, 2, ls), dn_ref[:, wide] * col(hn_ref, 2, ls) * not_last], axis=0)
            dd1_scr[:, ls] = dd1
            dp_ref[:, 2 * C_WIDTH + ls.start:2 * C_WIDTH + ls.stop] = (dd_m * d1).astype(BF16)
            de = jnp.zeros((CD_TILE, LANES), F32)
            for k in range(D_KERNEL):
                de = de + dw_ref[k:k + 1, ls] * dd1_scr[pl.ds(D_KERNEL - 1 - k, CD_TILE), ls]
                ddw_ref[k:k + 1, ls] += jnp.sum(dd1[:CD_TILE] * e_scr[pl.ds(HALO - (D_KERNEL - 1) + k, CD_TILE), ls], axis=0, keepdims=True)
            dp_ref[:, 3 * C_WIDTH + ls.start:3 * C_WIDTH + ls.stop] = (de * col(m_ref, 4, ls)).astype(BF16)
            dp_ref[:, 4 * C_WIDTH + ls.start:4 * C_WIDTH + ls.stop] = (de * col(m_ref, 3, ls)).astype(BF16)

    halo_prev = lambda i: (jnp.maximum(i * per - 1, 0), 0)
    halo_next = lambda i: (jnp.minimum((i + 1) * per, t // HALO - 1), 0)
    vec = _const_spec((1, C_WIDTH))
    return _call(
        body, grid=(nt,),
        in_specs=[pl.BlockSpec((HALO, CD_IN), halo_prev), _row_spec(CD_TILE, CD_IN), pl.BlockSpec((HALO, CD_IN), halo_next),
                  _row_spec(CD_TILE, 2 * C_WIDTH), pl.BlockSpec((HALO, 2 * C_WIDTH), halo_next),
                  _row_spec(CD_TILE, C_WIDTH), pl.BlockSpec((HALO, C_WIDTH), halo_next),
                  _const_spec((32, C_WIDTH)), vec, vec, _const_spec((8, C_WIDTH))],
        out_specs=[_row_spec(CD_TILE, CD_IN), _const_spec((32, C_WIDTH)), vec, vec, vec, _const_spec((8, C_WIDTH))],
        out_shape=[SDS((t, CD_IN), BF16), SDS((32, C_WIDTH), F32), SDS((1, C_WIDTH), F32), SDS((1, C_WIDTH), F32),
                   SDS((1, C_WIDTH), F32), SDS((8, C_WIDTH), F32)],
        scratch_shapes=[pltpu.VMEM((2 * HALO + CD_TILE, C_WIDTH), F32)] * 2 + [pltpu.VMEM((ext, C_WIDTH), F32)] * 2
        + [pltpu.VMEM((8, 2 * HALO + CD_TILE, C_WIDTH), F32), pltpu.VMEM((8, ext, C_WIDTH), F32),
           pltpu.VMEM((32, 8, C_WIDTH), F32)] + [pltpu.VMEM((ext, C_WIDTH), F32)] * 2,
        operands=[proj, proj, proj, dcat, dcat, c1, c1, cw, lg, lb, dw], name="mixer_cd_bwd", ride=ride)


def _wgrad(name, pairs, out_rc, t, ride):
    tk = TILES["wgrad"]
    assert tk == t, "the whole contraction has to fit one grid step"
    r, c = out_rc
    n = len(pairs)

    def body(*refs):
        ab, out_refs = refs[:2 * n], refs[2 * n:]
        for j in range(n):
            out_refs[j][...] = _dot(ab[2 * j][...], ab[2 * j + 1][...], TN).astype(BF16)

    operands, in_specs = [], []
    for lhs, lhs_spec, rhs, rhs_spec in pairs:
        operands += [lhs, rhs]
        in_specs += [lhs_spec, rhs_spec]
    res = _call(body, grid=(N_CHIPS, t // tk), in_specs=in_specs,
                out_specs=[pl.BlockSpec((None, r, c), lambda p, k: (p, 0, 0))] * n,
                out_shape=[SDS((N_CHIPS, r, c), BF16)] * n, operands=operands, name=name, ride=ride)
    outs, ride_res = (res, None) if ride is None else res
    outs = [o.reshape(N_CHIPS, 2, r // 2, c) for o in outs]
    return outs if ride is None else (outs, ride_res)


def _wgrad_col_sharded(name, h, dz_list, three_d, ride=None):
    t, d = h.shape
    tk = TILES["wgrad"]
    n4 = dz_list[0].shape[-1] if three_d else dz_list[0].shape[-1] // N_CHIPS
    hs = pl.BlockSpec((tk, d), lambda p, k: (k, 0))
    zs = pl.BlockSpec((None, tk, n4), lambda p, k: (p, k, 0)) if three_d else pl.BlockSpec((tk, n4), lambda p, k: (k, p))
    return _wgrad(name, [(h, hs, dz, zs) for dz in dz_list], (d, n4), t, ride)


def _wgrad_row_sharded(name, a, g, three_d, ride=None):
    many = isinstance(a, (list, tuple))
    a_list = list(a) if many else [a]
    t, d = g.shape
    tk = TILES["wgrad"]
    k4 = a_list[0].shape[-1] if three_d else a_list[0].shape[-1] // N_CHIPS
    a_spec = pl.BlockSpec((None, tk, k4), lambda p, k: (p, k, 0)) if three_d else pl.BlockSpec((tk, k4), lambda p, k: (k, p))
    gs = pl.BlockSpec((tk, d), lambda p, k: (k, 0))
    res = _wgrad(name, [(a_j, a_spec, g, gs) for a_j in a_list], (k4, d), t, ride)
    if many:
        return res
    return res[0] if ride is None else (res[0][0], res[1])


def _mesh_scalars():
    return jnp.stack([lax.axis_index("c"), 2 * lax.axis_index("x") + lax.axis_index("y")]).astype(jnp.int32)


def _stage_own(name, w, layer, dtype):
    layers, r, cols = w.shape
    h = r // 2

    def body(s_ref, x_ref, o_ref):
        o_ref[...] = x_ref[...].astype(dtype)

    return pl.pallas_call(
        body,
        grid_spec=pltpu.PrefetchScalarGridSpec(
            num_scalar_prefetch=1, grid=(2,),
            in_specs=[pl.BlockSpec((None, h, cols), lambda i, s: (2 * layer + i, 0, 0))],
            out_specs=pl.BlockSpec((None, None, h, cols), lambda i, s: (s[1], i, 0, 0))),
        out_shape=SDS((N_CHIPS, 2, h, cols), dtype), name=name,
        compiler_params=_params())(_mesh_scalars(), w.reshape(2 * layers, h, cols))


def _remote(src, dst, send_sem, recv_sem, device):
    return pltpu.make_async_remote_copy(src, dst, send_sem, recv_sem, device_id=device, device_id_type=MESH)


def _ride_gather_send(bufs):
    n = len(bufs)

    def each(b, sems, act):
        send, recv = sems
        x, y, c, p, others = _position()
        for t in range(n):
            for j, (qx, qy) in enumerate(others):
                act(b[t].at[p, c], b[t].at[2 * qx + qy, c], send.at[t, j], recv.at[t, j], (qx, qy, c))

    def start(ins, b, new, sems):
        each(b, sems, lambda mine, landed, s, r, dev: _remote(mine, mine, s, r, dev).start())

    def finish(ins, b, new, sems):
        def act(mine, landed, s, r, dev):
            _remote(mine, mine, s, r, dev).wait_send()
            _remote(landed, landed, s, r, dev).wait_recv()
        each(b, sems, act)

    return _Ride([], bufs, [], [(n, 3), (n, 3)], start, finish)


def _ride_gather_pass(bufs):
    n = len(bufs)

    def each(b, sems, act):
        send, recv = sems
        x, y, c, p, others = _position()
        for t in range(n):
            for j, (qx, qy) in enumerate(others):
                act(b[t].at[2 * qx + qy, c], b[t].at[2 * qx + qy, 1 - c], send.at[t, j], recv.at[t, j], (x, y, 1 - c))

    def start(ins, b, new, sems):
        each(b, sems, lambda landed, passed, s, r, dev: _remote(landed, landed, s, r, dev).start())

    def finish(ins, b, new, sems):
        def act(landed, passed, s, r, dev):
            _remote(landed, landed, s, r, dev).wait_send()
            _remote(passed, passed, s, r, dev).wait_recv()
        each(b, sems, act)

    return _Ride([], bufs, [], [(n, 3), (n, 3)], start, finish)


def _ride_swap(tensors):
    n = len(tensors)

    def each(ins, new, sems, act):
        send, recv = sems
        x, y, c, _, _ = _position()
        for t in range(n):
            act(_remote(ins[t].at[:, 1 - c], new[t], send.at[t], recv.at[t], (x, y, 1 - c)))

    def start(ins, b, new, sems):
        each(ins, new, sems, lambda cp: cp.start())

    def finish(ins, b, new, sems):
        each(ins, new, sems, lambda cp: cp.wait())

    return _Ride(tensors, [], [SDS((s.shape[0],) + s.shape[2:], s.dtype) for s in tensors], [(n,), (n,)], start, finish)


def _ride_scatter(tensors, landing):
    n = len(tensors)

    def each(ins, b, sems, act):
        send, recv = sems
        x, y, c, p, others = _position()
        for t in range(n):
            for j, (qx, qy) in enumerate(others):
                q = 2 * qx + qy
                act(ins[t].at[q], b[t].at[p], b[t].at[q], send.at[t, j], recv.at[t, j], (qx, qy, c))

    def start(ins, b, new, sems):
        each(ins, b, sems, lambda src, dst, landed, s, r, dev: _remote(src, dst, s, r, dev).start())

    def finish(ins, b, new, sems):
        def act(src, dst, landed, s, r, dev):
            _remote(src, dst, s, r, dev).wait_send()
            _remote(landed, landed, s, r, dev).wait_recv()
        each(ins, b, sems, act)

    return _Ride(tensors, landing, [], [(n, 3), (n, 3)], start, finish)


def _ride_join(bufs):
    n = len(bufs)

    def each(b, sems, act):
        send, recv = sems
        x, y, c, _, _ = _position()
        for t in range(n):
            act(b[t].at[c], b[t].at[1 - c], send.at[t], recv.at[t], (x, y, 1 - c))

    def start(ins, b, new, sems):
        each(b, sems, lambda mine, theirs, s, r, dev: _remote(mine, mine, s, r, dev).start())

    def finish(ins, b, new, sems):
        def act(mine, theirs, s, r, dev):
            _remote(mine, mine, s, r, dev).wait_send()
            _remote(theirs, theirs, s, r, dev).wait_recv()
        each(b, sems, act)

    return _Ride([], bufs, [], [(n,), (n,)], start, finish)


def _all_reduce_small(pack):
    rows = pack.shape[0]
    n_dev = 2 * N_CHIPS

    def body(x_ref, o_ref, land, send, recv):
        x, y, c, p, _ = _position()
        me = 2 * p + c
        land[me] = x_ref[...]
        peers = [(dx, dy, dc) for dx in range(2) for dy in range(2) for dc in range(2) if (dx, dy, dc) != (0, 0, 0)]
        for j, (dx, dy, dc) in enumerate(peers):
            _remote(land.at[me], land.at[me], send.at[j], recv.at[j], (x ^ dx, y ^ dy, c ^ dc)).start()
        for j, (dx, dy, dc) in enumerate(peers):
            src = 4 * (x ^ dx) + 2 * (y ^ dy) + (c ^ dc)
            _remote(land.at[me], land.at[me], send.at[j], recv.at[j], (x ^ dx, y ^ dy, c ^ dc)).wait_send()
            _remote(land.at[src], land.at[src], send.at[j], recv.at[j], (x ^ dx, y ^ dy, c ^ dc)).wait_recv()
        acc = land[0]
        for dev in range(1, n_dev):
            acc = acc + land[dev]
        o_ref[...] = acc

    return pl.pallas_call(
        body, out_shape=SDS((rows, LANES), F32),
        scratch_shapes=[pltpu.VMEM((n_dev, rows, LANES), F32), pltpu.SemaphoreType.DMA((n_dev - 1,)),
                        pltpu.SemaphoreType.DMA((n_dev - 1,))],
        name="all_reduce_small", compiler_params=_params())(pack)


def _add_own_half(name, full, recv, out_dtype):
    n4, _, h, cols = full.shape

    def body(s_ref, a_ref, b_ref, o_ref, own_ref):
        v = (a_ref[...].astype(F32) + b_ref[...].astype(F32)).astype(out_dtype)
        o_ref[...] = v

        @pl.when(pl.program_id(0) == s_ref[1])
        def _():
            own_ref[...] = v

    return pl.pallas_call(
        body,
        grid_spec=pltpu.PrefetchScalarGridSpec(
            num_scalar_prefetch=1, grid=(n4,),
            in_specs=[pl.BlockSpec((None, None, h, cols), lambda q, s: (q, s[0], 0, 0)),
                      pl.BlockSpec((None, h, cols), lambda q, s: (q, 0, 0))],
            out_specs=[pl.BlockSpec((None, h, cols), lambda q, s: (q, 0, 0)),
                       pl.BlockSpec((None, h, cols), lambda q, s: (s[1], 0, 0))]),
        out_shape=[SDS((n4, h, cols), out_dtype)] * 2, name=name, compiler_params=_params())(_mesh_scalars(), full, recv)


def _sum_chips(name, parts):
    n4, h, cols = parts.shape
    th = h // 4 if h % 64 == 0 else h

    def body(s_ref, a_ref, o_ref):
        acc = a_ref[0].astype(F32)
        for q in range(1, n4):
            acc = acc + a_ref[q].astype(F32)
        o_ref[...] = acc

    return pl.pallas_call(
        body,
        grid_spec=pltpu.PrefetchScalarGridSpec(
            num_scalar_prefetch=1, grid=(h // th,),
            in_specs=[pl.BlockSpec((n4, th, cols), lambda i, s: (0, i, 0))],
            out_specs=pl.BlockSpec((None, th, cols), lambda i, s: (s[0], i, 0))),
        out_shape=SDS((2, h, cols), F32), name=name, compiler_params=_params())(_mesh_scalars(), parts)


def _adamw_math(w, g, m, v):
    m2 = ADAM_B1 * m + (1.0 - ADAM_B1) * g
    v2 = ADAM_B2 * v + (1.0 - ADAM_B2) * (g * g)
    m_hat = m2 / (1.0 - ADAM_B1 ** ADAM_STEP)
    v_hat = v2 / (1.0 - ADAM_B2 ** ADAM_STEP)
    delta = -ADAM_LR * (m_hat / (jnp.sqrt(v_hat) + ADAM_EPS) + ADAM_WD * w)
    return delta, m2, v2


def _row_tile(rows, cols):
    cap = max(8, (1 << 18) // cols)
    best = 8
    for cand in range(8, min(rows, cap) + 1, 8):
        if rows % cand == 0:
            best = cand
    return best


def _adamw_big(name, w, g_layers, m, v):
    layers, rows, cols = w.shape
    tr = _row_tile(rows, cols)

    def body(w_ref, m_ref, v_ref, *rest):
        g_refs, (g_o, d_o, m_o, v_o) = rest[:layers], rest[layers:]
        gv = g_refs[0][...]
        for layer in range(1, layers):
            gv = jnp.where(pl.program_id(0) == layer, g_refs[layer][...], gv)
        d, mm, vv = _adamw_math(w_ref[...], gv, m_ref[...], v_ref[...])
        g_o[...] = gv
        d_o[...] = d
        m_o[...] = mm
        v_o[...] = vv

    blk = pl.BlockSpec((None, tr, cols), lambda l, i: (l, i, 0))
    g_blk = pl.BlockSpec((tr, cols), lambda l, i: (i, 0))
    return tuple(pl.pallas_call(
        body, grid=(layers, rows // tr), in_specs=[blk] * 3 + [g_blk] * layers, out_specs=[blk] * 4,
        out_shape=[SDS((layers, rows, cols), F32)] * 4, name=name,
        compiler_params=_params())(w, m, v, *[g.reshape(rows, cols) for g in g_layers]))


def _adamw_small(ws, gs, ms, vs):
    n = len(ws)
    flat = []
    for group in (ws, gs, ms, vs):
        flat += [a.reshape(-1, a.shape[-1]) for a in group]

    def body(*refs):
        w_r, g_r, m_r, v_r = refs[:n], refs[n:2 * n], refs[2 * n:3 * n], refs[3 * n:4 * n]
        d_o, m_o, v_o = refs[4 * n:5 * n], refs[5 * n:6 * n], refs[6 * n:7 * n]
        for j in range(n):
            d, mm, vv = _adamw_math(w_r[j][...], g_r[j][...], m_r[j][...], v_r[j][...])
            d_o[j][...] = d
            m_o[j][...] = mm
            v_o[j][...] = vv

    shapes = [SDS(a.shape, F32) for a in flat[:n]]
    outs = pl.pallas_call(body, out_shape=shapes * 3, name="adamw_small", compiler_params=_params())(*flat)
    res = []
    for k in range(3):
        res.append([outs[k * n + j].reshape(ws[j].shape) for j in range(n)])
    return res


BIG = ("ab_w_in", "ab_w_out", "cd_w_in", "cd_w_out", "ffn_w_gate", "ffn_w_up", "ffn_w_down")
V_BLOCK = (2 * A_WIDTH + QK_COLS) // B_WIDTH


def _pad_rows(a, rows):
    return jnp.pad(a, ((0, rows - a.shape[0]), (0, 0)))


A_IN, A_OUT, C_IN, C_OUT = ("ab_w_in", 0), ("ab_w_out", 0), ("cd_w_in", 0), ("cd_w_out", 0)
G0, U0, D0 = ("ffn_w_gate", 0), ("ffn_w_up", 0), ("ffn_w_down", 0)
G1, U1, D1 = ("ffn_w_gate", 1), ("ffn_w_up", 1), ("ffn_w_down", 1)
UNITS = (A_IN, A_OUT, G0, U0, D0, C_IN, C_OUT, G1, U1, D1)
ROWS_MINOR = ("ffn_w_gate", "ffn_w_up")
SMALL_SHARDED = ("small", 0)
REPLICATED_UNIT = ("replicated", 0)


class _Exchange:
    def __init__(self, enabled):
        self.enabled = enabled
        self.w, self.grad, self.recv, self.half, self.land, self.done = {}, {}, {}, {}, {}, {}

    def full(self, unit):
        b = self.w[unit]
        return b.reshape(N_CHIPS, 1, 2 * b.shape[2], b.shape[3])

    def _ride(self, phases):
        rides, sinks = [], []
        for kind, units in phases:
            if kind == "send":
                rides.append(_ride_gather_send([self.w[u] for u in units]))
                sinks.append(self.w)
            elif kind == "pass":
                rides.append(_ride_gather_pass([self.w[u] for u in units]))
                sinks.append(self.w)
            elif kind == "swap":
                rides.append(_ride_swap([self.grad[u] for u in units]))
                sinks.append(self.recv)
            elif kind == "scatter":
                rides.append(_ride_scatter([self.half[u] for u in units], [self.land[u] for u in units]))
                sinks.append(self.land)
            else:
                rides.append(_ride_join([self.done[u] for u in units]))
                sinks.append(self.done)
        ride = functools.reduce(_ride_both, rides)

        def settle(res):
            n_bufs = sum(len(r.bufs) for r in rides)
            bufs, new = list(res[:n_bufs]), list(res[n_bufs:])
            for r, sink, (_, units) in zip(rides, sinks, phases):
                vals = [bufs.pop(0) for _ in r.bufs] + [new.pop(0) for _ in r.new_outs]
                for u, v in zip(units, vals):
                    sink[u] = v

        return ride, settle

    def run(self, fn, *args, phases=(), **kw):
        if not self.enabled or not phases:
            return fn(*args, **kw)
        ride, settle = self._ride(phases)
        out, res = fn(*args, ride=ride, **kw)
        settle(res)
        return out

    def alone(self, name, phases):
        if self.enabled:
            ride, settle = self._ride(phases)
            settle(_run_ride(name, ride))

    def pair_sum(self, units):
        if self.enabled:
            for u in units:
                dtype = F32 if u in (SMALL_SHARDED, REPLICATED_UNIT) else BF16
                self.half[u], self.land[u] = _add_own_half(f"pair_sum_{u[0]}_{u[1]}", self.grad[u], self.recv[u], dtype)

    def chip_sum(self, units):
        if self.enabled:
            for u in units:
                self.done[u] = _sum_chips(f"chip_sum_{u[0]}_{u[1]}", self.land[u])


def _local_step(x, target, ex, sp):
    t, d = x.shape
    tabs = _rope_tables(t)
    gains = jnp.concatenate([jnp.tile(sp["q_norm_g"][g], HEAD_DIM // 8) for g in range(N_DIL)]
                            + [jnp.tile(sp["k_norm_g"][g], HEAD_DIM // 8) for g in range(N_DIL)]).reshape(1, QK_COLS)
    bias_t = sp["sgu_bias"].T
    cw = _pad_rows(sp["conv_c_w"], 32)
    dw = _pad_rows(sp["conv_d_w"], 8)
    cb, clg, clb = (sp[k].reshape(1, C_WIDTH) for k in ("conv_c_b", "c_ln_g", "c_ln_b"))
    slg, slb = sp["sgu_norm_g"].reshape(1, A_WIDTH), sp["sgu_norm_b"].reshape(1, A_WIDTH)
    g_ab, g_cd = sp["ab_norm_g"].reshape(1, d), sp["cd_norm_g"].reshape(1, d)
    g_f0, g_f1 = sp["ffn_norm_g"][0:1], sp["ffn_norm_g"][1:2]
    run = ex.run

    def w2d(unit):
        return ex.full(unit).reshape(-1, d)

    h0 = _rms_fwd("rms_ab", x, g_ab)
    proj = run(_proj_in, "proj_ab", h0, ex.full(A_IN), 0, phases=[("send", [A_OUT, G0])])
    a_out = _mixer_a_fwd(proj, slg, slb, sp["sgu_w"], bias_t)
    qk, q1, q2, k1, k2 = run(_qk_fwd, proj, gains, tabs, phases=[("pass", [A_OUT, G0]), ("send", [U0])])
    regrouped_qk = {1: (q1, k1), 2: (q2, k2)}
    fwd_phases = ([("pass", [U0]), ("send", [D0])], [("pass", [D0]), ("send", [C_IN])],
                  [("pass", [C_IN]), ("send", [C_OUT])])
    qkv, o_list, l_list = [], [], []
    for g, rate in enumerate(DIL_RATES):
        if rate == 1:
            qk3, proj3 = qk.reshape(1, t, QK_COLS), proj.reshape(1, t, AB_IN)
            q, k, v = (qk3, g), (qk3, N_DIL + g), (proj3, V_BLOCK + g)
        else:
            vp, = _permute(f"regroup_v_{g}", [(proj, V_BLOCK + g)], rate)
            q, k, v = (regrouped_qk[g][0], 0), (regrouped_qk[g][1], 0), (vp, 0)
        qkv.append((q, k, v))
        o, l = run(_attn_fwd, f"attn_fwd_{g}", q, k, v, phases=fwd_phases[g])
        if rate == 1:
            o, l = o.reshape(t, B_WIDTH), l.reshape(t, B_WIDTH)
        o_list.append(o)
        l_list.append(l)
    cat, lse_tot, lse_1, lse_2 = _attn_merge(a_out, o_list, l_list)
    x1, hf0 = _proj_out("out_ab", cat, w2d(A_OUT), x, g_next=g_f0)
    fgate0, fup0, act0 = run(_ffn_in, "ffn_in_0", hf0, ex.full(G0), ex.full(U0), 0,
                           phases=[("pass", [C_OUT]), ("send", [G1, U1])])
    x2, h1 = run(_ffn_out, "ffn_out_0", act0, ex.full(D0), 0, x1, g_next=g_cd, phases=[("pass", [G1, U1]), ("send", [D1])])
    projcd = run(_proj_in, "proj_cd", h1, ex.full(C_IN), 0, phases=[("pass", [D1])])
    cat2, c1 = _mixer_cd_fwd(projcd, cw, cb, clg, clb, dw)
    x3, hf1 = _proj_out("out_cd", cat2, w2d(C_OUT), x2, g_next=g_f1)
    fgate1, fup1, act1 = _ffn_in("ffn_in_1", hf1, ex.full(G1), ex.full(U1), 0)
    dy, loss_acc, dy_b = _ffn_out("ffn_out_1", act1, ex.full(D1), 0, x3, target=target)
    loss = 0.5 * loss_acc[0, 0] / d

    late = [D1, G1, U1]
    dgate, dup = _ffn_dact("ffn_dact_1", dy_b, ex.full(D1), 0, fgate1, fup1)
    ex.grad[D1] = _wgrad_row_sharded("wgrad_down_1", act1, dy_b, True)
    ex.grad[G1] = _wgrad_row_sharded("wgrad_gate_1", dgate, hf1, True)
    ex.grad[U1] = _wgrad_row_sharded("wgrad_up_1", dup, hf1, True)
    g3, d_f1, g3_b = run(_dgrad_cols, "dgrad_ffn_1", [dgate, dup], [ex.full(G1), ex.full(U1)], 0, True, x3, g_f1, dy,
                         w_rows=True, phases=[("swap", late)])
    ex.pair_sum(late)

    dcat2 = _dgrad_rows("dgrad_out_cd", g3_b, w2d(C_OUT))
    ex.grad[C_OUT] = _wgrad_row_sharded("wgrad_out_cd", cat2, g3_b, False)
    dprojcd, d_cw, d_cb, d_clg, d_clb, d_dw = run(_mixer_cd_bwd, projcd, dcat2, c1, cw, clg, clb, dw, phases=[("scatter", late)])
    ex.chip_sum(late)
    ex.grad[C_IN] = run(_wgrad_col_sharded, "wgrad_in_cd", h1, [dprojcd], False, phases=[("join", late)])[0]
    g2, d_cdn, g2_b = run(_dgrad_cols, "dgrad_in_cd", [dprojcd], [ex.full(C_IN)], 0, False, x2, g_cd, g3,
                          phases=[("swap", [C_OUT, C_IN])])
    ex.pair_sum([C_OUT, C_IN])

    dgate, dup = run(_ffn_dact, "ffn_dact_0", g2_b, ex.full(D0), 0, fgate0, fup0, phases=[("scatter", [C_OUT, C_IN])])
    ex.chip_sum([C_OUT, C_IN])
    ex.grad[D0] = run(_wgrad_row_sharded, "wgrad_down_0", act0, g2_b, True, phases=[("join", [C_OUT, C_IN])])
    ex.grad[G0] = _wgrad_row_sharded("wgrad_gate_0", dgate, hf0, True)
    ex.grad[U0] = _wgrad_row_sharded("wgrad_up_0", dup, hf0, True)
    small = {"cd_norm_g": d_cdn, "conv_c_w": d_cw[:C_KERNEL], "conv_c_b": d_cb, "c_ln_g": d_clg, "c_ln_b": d_clb,
             "conv_d_w": d_dw[:D_KERNEL]}
    ex.grad[SMALL_SHARDED] = _split_full_small(small).reshape(N_CHIPS, 2, SHARDED_ROWS // 2, LANES)
    mid = [D0, G0, U0, SMALL_SHARDED]
    g1, d_f0, g1_b = run(_dgrad_cols, "dgrad_ffn_0", [dgate, dup], [ex.full(G0), ex.full(U0)], 0, True, x1, g_f0, g2,
                         w_rows=True, phases=[("swap", mid)])
    ex.pair_sum(mid)

    dcat = _dgrad_rows("dgrad_out_ab", g1_b, w2d(A_OUT))
    ex.grad[A_OUT] = _wgrad_row_sharded("wgrad_out_ab", cat, g1_b, False)
    d_a, d_sw, d_sbt, d_slg, d_slb = _mixer_a_bwd(proj, dcat, slg, slb, sp["sgu_w"], bias_t)
    early = {"sgu_norm_g": d_slg, "sgu_norm_b": d_slb, "sgu_w": d_sw, "sgu_bias": d_sbt.T}
    ex.grad[REPLICATED_UNIT] = jnp.broadcast_to(
        _pack_replicated(early, REPLICATED_EARLY, REPLICATED_EARLY_ROWS).reshape(2, REPLICATED_EARLY_ROWS // 2, LANES),
        (N_CHIPS, 2, REPLICATED_EARLY_ROWS // 2, LANES))
    last = [A_OUT, REPLICATED_UNIT]
    dbb, dd, db_1, dd_1, db_2, dd_2 = _attn_bwd_prep(dcat, cat)
    regrouped_bwd = {1: (db_1, lse_1, dd_1), 2: (db_2, lse_2, dd_2)}
    bwd_phases = ([("scatter", [D0, SMALL_SHARDED])],
                  [("scatter", [G0]), ("join", [D0, SMALL_SHARDED]), ("swap", last)],
                  [("scatter", [U0]), ("join", [G0])])
    dqs, dks, dvs = [], [], []
    for g, rate in enumerate(DIL_RATES):
        q, k, v = qkv[g]
        if rate == 1:
            db3, l3, dd3 = (a.reshape(1, t, B_WIDTH) for a in (dbb, lse_tot, dd))
        else:
            db3, l3, dd3 = regrouped_bwd[g]
        if g == 1:
            ex.chip_sum([D0, SMALL_SHARDED])
        elif g == 2:
            ex.chip_sum([G0])
            ex.pair_sum(last)
        dq, dk, dv = run(_attn_bwd, f"attn_bwd_{g}", q, k, v, db3, l3, dd3, phases=bwd_phases[g])
        if rate == 1:
            dq, dk, dv = (a.reshape(t, B_WIDTH) for a in (dq, dk, dv))
        dqs.append(dq)
        dks.append(dk)
        dvs.append(dv)
    ex.chip_sum([U0])
    dproj, d_gains = run(_dproj_assemble, proj, d_a, dqs, dks, dvs, gains, tabs, phases=[("scatter", last), ("join", [U0])])
    ex.chip_sum(last)
    d_gains = _fold_heads(d_gains)[0].reshape(2, N_DIL, B_WIDTH)[:, :, :HEAD_DIM]
    ex.grad[A_IN] = run(_wgrad_col_sharded, "wgrad_in_ab", h0, [dproj], False, phases=[("join", last)])[0]
    ex.alone("swap_last", [("swap", [A_IN])])
    ex.pair_sum([A_IN])
    gx, d_abn = run(_dgrad_cols, "dgrad_in_ab", [dproj], [ex.full(A_IN)], 0, False, x, g_ab, g1, bf16_copy=False,
                    phases=[("scatter", [A_IN])])
    ex.chip_sum([A_IN])
    ex.alone("join_last", [("join", [A_IN])])

    small.update({
        "ab_norm_g": d_abn, "sgu_norm_g": d_slg, "sgu_norm_b": d_slb, "sgu_w": d_sw, "sgu_bias": d_sbt.T,
        "q_norm_g": d_gains[0], "k_norm_g": d_gains[1], "ffn_norm_g": jnp.concatenate([d_f0, d_f1], axis=0),
    })
    return loss, gx, small


SHARDED_SMALL = ("cd_norm_g", "conv_c_w", "conv_c_b", "c_ln_g", "c_ln_b", "conv_d_w")
SHARDED_ROWS = 48
REPLICATED_EARLY = ("sgu_norm_g", "sgu_norm_b", "sgu_w", "sgu_bias")
REPLICATED_EARLY_ROWS = 528
REPLICATED_LATE = ("ab_norm_g", "q_norm_g", "k_norm_g", "ffn_norm_g", "loss")
REPLICATED_LATE_ROWS = 32
REPLICATED_SMALL = REPLICATED_EARLY + REPLICATED_LATE[:-1]


def _pack_sharded(parts):
    rows = [parts[k].reshape(-1, LANES) for k in SHARDED_SMALL]
    return _pad_rows(jnp.concatenate(rows, axis=0), SHARDED_ROWS)


def _split_full_small(small):
    per_chip = []
    for q in range(N_CHIPS):
        parts = {}
        for k in SHARDED_SMALL:
            a = small[k]
            a = a.reshape(-1, a.shape[-1])
            n = a.shape[-1] // N_CHIPS
            parts[k] = a[:, q * n:(q + 1) * n]
        per_chip.append(_pack_sharded(parts))
    return jnp.stack(per_chip)


def _unpack_sharded(pack, shapes):
    out, r = {}, 0
    for k in SHARDED_SMALL:
        n = math.prod(shapes[k]) // LANES
        out[k] = pack[r:r + n].reshape(shapes[k])
        r += n
    return out


def _gathered_small(packs, shapes):
    per_chip = [_unpack_sharded(packs[q], shapes) for q in range(N_CHIPS)]
    return {k: jnp.concatenate([pc[k] for pc in per_chip], axis=-1) for k in SHARDED_SMALL}


def _pack_replicated(small, names, total_rows):
    rows = []
    for k in names:
        a = small[k].reshape(-1)
        a = jnp.pad(a, (0, (-a.shape[0]) % LANES))
        rows.append(a.reshape(-1, LANES))
    return _pad_rows(jnp.concatenate(rows, axis=0), total_rows)


def _unpack_replicated(pack, shapes, names):
    out, r = {}, 0
    for k in names:
        size = math.prod(shapes[k])
        n = -(-size // LANES)
        out[k] = pack[r:r + n].reshape(-1)[:size].reshape(shapes[k])
        r += n
    return out


WEIGHT_ORDER = ("ab_norm_g", "ab_w_in", "sgu_norm_g", "sgu_norm_b", "sgu_w", "sgu_bias", "q_norm_g", "k_norm_g", "ab_w_out",
                "cd_norm_g", "cd_w_in", "conv_c_w", "conv_c_b", "c_ln_g", "c_ln_b", "conv_d_w", "cd_w_out", "ffn_norm_g",
                "ffn_w_gate", "ffn_w_up", "ffn_w_down")


def kernel(x, ab_norm_g, ab_w_in, sgu_norm_g, sgu_norm_b, sgu_w, sgu_bias, q_norm_g, k_norm_g, ab_w_out, cd_norm_g, cd_w_in, conv_c_w, conv_c_b, c_ln_g, c_ln_b, conv_d_w, cd_w_out, ffn_norm_g, ffn_w_gate, ffn_w_up, ffn_w_down, loss_target, m_ab_norm_g, m_ab_w_in, m_sgu_norm_g, m_sgu_norm_b, m_sgu_w, m_sgu_bias, m_q_norm_g, m_k_norm_g, m_ab_w_out, m_cd_norm_g, m_cd_w_in, m_conv_c_w, m_conv_c_b, m_c_ln_g, m_c_ln_b, m_conv_d_w, m_cd_w_out, m_ffn_norm_g, m_ffn_w_gate, m_ffn_w_up, m_ffn_w_down, v_ab_norm_g, v_ab_w_in, v_sgu_norm_g, v_sgu_norm_b, v_sgu_w, v_sgu_bias, v_q_norm_g, v_k_norm_g, v_ab_w_out, v_cd_norm_g, v_cd_w_in, v_conv_c_w, v_conv_c_b, v_c_ln_g, v_c_ln_b, v_conv_d_w, v_cd_w_out, v_ffn_norm_g, v_ffn_w_gate, v_ffn_w_up, v_ffn_w_down):
    args = dict(locals())
    ws = {k: args[k] for k in WEIGHT_ORDER}
    ms = {k: args["m_" + k] for k in WEIGHT_ORDER}
    vs = {k: args["v_" + k] for k in WEIGHT_ORDER}
    small_names = [k for k in WEIGHT_ORDER if k not in BIG]
    t, d = x.shape[1:]

    for group in (ws, ms, vs):
        for k in ROWS_MINOR:
            group[k] = jnp.swapaxes(group[k], 1, 2)
    ex = _Exchange(enabled=True)
    for name, layer in UNITS:
        ex.w[(name, layer)] = _stage_own(f"stage_{name}_{layer}", ws[name], layer, BF16)
    own_small = _pack_sharded({k: ws[k][0] for k in SHARDED_SMALL})
    ex.w[SMALL_SHARDED] = _stage_own("stage_small", own_small[None], 0, F32)
    ex.alone("gather_first", [("send", [A_IN, SMALL_SHARDED])])
    ex.alone("gather_first_pass", [("pass", [A_IN, SMALL_SHARDED])])
    sp = _gathered_small(ex.w[SMALL_SHARDED].reshape(N_CHIPS, SHARDED_ROWS, LANES), {k: ws[k].shape[1:] for k in SHARDED_SMALL})
    for k in REPLICATED_SMALL:
        sp[k] = ws[k] if k == "ffn_norm_g" else ws[k][0]

    loss, grad_x, g_small = _local_step(x.reshape(t, d), loss_target.reshape(t, d), ex, sp)

    shapes = {k: ws[k].shape for k in REPLICATED_SMALL}
    shapes["loss"] = (1,)
    g_small["loss"] = loss
    late = _all_reduce_small(_pack_replicated(g_small, REPLICATED_LATE, REPLICATED_LATE_ROWS))
    grad = _unpack_sharded(ex.done[SMALL_SHARDED].reshape(SHARDED_ROWS, LANES), {k: ws[k].shape for k in SHARDED_SMALL})
    grad.update(_unpack_replicated(ex.done[REPLICATED_UNIT].reshape(REPLICATED_EARLY_ROWS, LANES), shapes, REPLICATED_EARLY))
    grad.update(_unpack_replicated(late, shapes, REPLICATED_LATE))
    loss = grad.pop("loss")[0]

    delta, new_m, new_v = {}, {}, {}
    for k in BIG:
        g_layers = [ex.done[(k, layer)] for layer in range(ws[k].shape[0])]
        outs = _adamw_big("adamw_" + k, ws[k], g_layers, ms[k], vs[k])
        if k in ROWS_MINOR:
            outs = [jnp.swapaxes(o, 1, 2) for o in outs]
        grad[k], delta[k], new_m[k], new_v[k] = outs
    d_s, m_s, v_s = _adamw_small([ws[k] for k in small_names], [grad[k] for k in small_names],
                                 [ms[k] for k in small_names], [vs[k] for k in small_names])
    for j, k in enumerate(small_names):
        delta[k], new_m[k], new_v[k] = d_s[j], m_s[j], v_s[j]

    return (loss, grad_x[None], *[grad[k] for k in WEIGHT_ORDER], *[delta[k] for k in WEIGHT_ORDER],
            *[new_m[k] for k in WEIGHT_ORDER], *[new_v[k] for k in WEIGHT_ORDER])
```

```python
import functools
import math

import jax
import jax.numpy as jnp
from jax import lax
from jax.experimental import pallas as pl
from jax.experimental.pallas import tpu as pltpu

F32 = jnp.float32
BF16 = jnp.bfloat16
SDS = jax.ShapeDtypeStruct

N_CHIPS = 4
EPS = 1e-6
NEG_INF = -1e30
CHUNK = 128
A_GROUPS = 4
A_WIDTH = 512
N_DIL = 3
DIL_RATES = (1, 4, 16)
HEAD_DIM = 64
B_WIDTH = 512
ROPE_DIM = 16
ROPE_THETA = 500000.0
C_WIDTH = 512
C_KERNEL = 31
D_KERNEL = 3
HALO = 32
ATT_BLOCK = 128
LANES = 128

ADAM_LR = 0.001
ADAM_B1 = 0.9
ADAM_B2 = 0.999
ADAM_EPS = 1e-08
ADAM_WD = 0.01
ADAM_STEP = 10

VMEM_LIMIT = 56 * 1024 * 1024

NN = (((1,), (0,)), ((), ()))
NT = (((1,), (1,)), ((), ()))
TN = (((0,), (0,)), ((), ()))

TILES = {"proj_in": 1024, "proj_out": 1024, "ffn_in": 1024, "ffn_out": 512, "ffn_dact": 512, "dgrad_cols": 512,
         "dgrad_rows": 1024, "wgrad": 4096}


def _params(sem=None):
    return pltpu.CompilerParams(dimension_semantics=sem, vmem_limit_bytes=VMEM_LIMIT)


def _bf(v):
    return v if v.dtype == BF16 else v.astype(BF16)


def _dot(a, b, dims):
    return lax.dot_general(_bf(a), _bf(b), dims, preferred_element_type=F32)


def _dot_hi(a, b):
    return jnp.dot(a, b, precision=lax.Precision.HIGHEST, preferred_element_type=F32)


def _sigmoid(v):
    return 0.5 * jnp.tanh(0.5 * v) + 0.5


def _gelu(v):
    return 0.5 * v * (1.0 + lax.erf(v * (1.0 / math.sqrt(2.0))))


def _gelu_grad(v):
    cdf = 0.5 * (1.0 + lax.erf(v * (1.0 / math.sqrt(2.0))))
    return cdf + v * jnp.exp(-0.5 * v * v) * (1.0 / math.sqrt(2.0 * math.pi))


def _segment_mean_matrix(seg, scale=None):
    r = lax.broadcasted_iota(jnp.int32, (LANES, LANES), 0) // seg
    c = lax.broadcasted_iota(jnp.int32, (LANES, LANES), 1) // seg
    return jnp.where(r == c, (1.0 / seg) if scale is None else scale, 0.0).astype(BF16)


def _segment_dot(v, seg):
    hi = v.astype(BF16)
    lo = (v - hi.astype(F32)).astype(BF16)
    return jnp.dot(hi, seg, preferred_element_type=F32) + jnp.dot(lo, seg, preferred_element_type=F32)


MESH = pl.DeviceIdType.MESH
ANY = pl.BlockSpec(memory_space=pl.ANY)


def _position():
    x, y, c = lax.axis_index("x"), lax.axis_index("y"), lax.axis_index("c")
    others = [(1 - x, y), (x, 1 - y), (1 - x, 1 - y)]
    return x, y, c, 2 * x + y, others


class _Ride:
    def __init__(self, ins, bufs, new_outs, sem_shapes, start, finish):
        self.ins, self.bufs, self.new_outs, self.sem_shapes = list(ins), list(bufs), list(new_outs), list(sem_shapes)
        self.start, self.finish = start, finish


def _ride_both(a, b):
    na = (len(a.ins), len(a.bufs), len(a.new_outs), len(a.sem_shapes))

    def split(ins, bufs, new, sems):
        return ((ins[:na[0]], bufs[:na[1]], new[:na[2]], sems[:na[3]]), (ins[na[0]:], bufs[na[1]:], new[na[2]:], sems[na[3]:]))

    def start(*refs):
        ra, rb = split(*refs)
        a.start(*ra)
        b.start(*rb)

    def finish(*refs):
        ra, rb = split(*refs)
        a.finish(*ra)
        b.finish(*rb)

    return _Ride(a.ins + b.ins, a.bufs + b.bufs, a.new_outs + b.new_outs, a.sem_shapes + b.sem_shapes, start, finish)


def _call(body, *, grid, in_specs, out_specs, out_shape, operands, name, scratch_shapes=(), aliases=None, ride=None):
    if ride is None:
        return pl.pallas_call(body, grid=grid, in_specs=in_specs, out_specs=out_specs, out_shape=out_shape,
                              scratch_shapes=list(scratch_shapes), input_output_aliases=aliases or {}, name=name,
                              compiler_params=_params())(*operands)
    multi = isinstance(out_shape, (list, tuple))
    out_shapes = list(out_shape) if multi else [out_shape]
    o_specs = list(out_specs) if multi else [out_specs]
    n_in, n_out, n_scr = len(operands), len(out_shapes), len(scratch_shapes)
    n_ri, n_rb, n_rn = len(ride.ins), len(ride.bufs), len(ride.new_outs)

    def carrying(*refs):
        k = n_in
        r_ins = refs[k:k + n_ri]
        k += n_ri + n_rb
        outs = refs[k:k + n_out]
        k += n_out
        r_bufs = refs[k:k + n_rb]
        k += n_rb
        r_new = refs[k:k + n_rn]
        k += n_rn
        scratch = refs[k:k + n_scr]
        sems = refs[k + n_scr:]
        first, last = None, None
        for axis, size in enumerate(grid):
            pid = pl.program_id(axis)
            first = (pid == 0) if first is None else first & (pid == 0)
            last = (pid == size - 1) if last is None else last & (pid == size - 1)

        @pl.when(first)
        def _():
            ride.start(r_ins, r_bufs, r_new, sems)

        body(*refs[:n_in], *outs, *scratch)

        @pl.when(last)
        def _():
            ride.finish(r_ins, r_bufs, r_new, sems)

    all_aliases = dict(aliases or {})
    for j in range(n_rb):
        all_aliases[n_in + n_ri + j] = n_out + j
    res = pl.pallas_call(
        carrying, grid=grid, in_specs=list(in_specs) + [ANY] * (n_ri + n_rb), out_specs=o_specs + [ANY] * (n_rb + n_rn),
        out_shape=out_shapes + [SDS(b.shape, b.dtype) for b in ride.bufs] + ride.new_outs,
        scratch_shapes=list(scratch_shapes) + [pltpu.SemaphoreType.DMA(s) for s in ride.sem_shapes],
        input_output_aliases=all_aliases, name=name, compiler_params=_params())(*operands, *ride.ins, *ride.bufs)
    outs = res[:n_out]
    return (list(outs) if multi else outs[0]), list(res[n_out:])


def _run_ride(name, ride):
    n_ri, n_rb, n_rn = len(ride.ins), len(ride.bufs), len(ride.new_outs)

    def body(*refs):
        r_ins = refs[:n_ri]
        r_bufs = refs[n_ri + n_rb:n_ri + 2 * n_rb]
        r_new = refs[n_ri + 2 * n_rb:n_ri + 2 * n_rb + n_rn]
        sems = refs[n_ri + 2 * n_rb + n_rn:]
        ride.start(r_ins, r_bufs, r_new, sems)
        ride.finish(r_ins, r_bufs, r_new, sems)

    return list(pl.pallas_call(
        body, in_specs=[ANY] * (n_ri + n_rb), out_specs=[ANY] * (n_rb + n_rn),
        out_shape=[SDS(b.shape, b.dtype) for b in ride.bufs] + ride.new_outs,
        scratch_shapes=[pltpu.SemaphoreType.DMA(s) for s in ride.sem_shapes],
        input_output_aliases={n_ri + j: j for j in range(n_rb)}, name=name)(*ride.ins, *ride.bufs))


def _whole(ref, p):
    return ref[...]


def _slab(ref, p):
    return ref[p]


def _matmul(name, grid, pairs, extras, outs, dims, epi, *, slabs=1, n_acc=1, ride=None):
    n_pairs, n_ex, n_out = len(pairs), len(extras), len(outs)

    def body(*refs):
        ab = refs[:2 * n_pairs]
        ex = refs[2 * n_pairs:2 * n_pairs + n_ex]
        out_refs = refs[2 * n_pairs + n_ex:2 * n_pairs + n_ex + n_out]
        pids = tuple(pl.program_id(a) for a in range(len(grid)))
        parts = [None] * n_acc
        for p in range(slabs):
            for j, (_, _, a_pick, _, _, b_pick, acc) in enumerate(pairs):
                d = _dot(a_pick(ab[2 * j], p), b_pick(ab[2 * j + 1], p), dims)
                parts[acc] = d if parts[acc] is None else parts[acc] + d
        epi(parts, ex, out_refs, pids)

    operands, in_specs = [], []
    for a, a_spec, _, b, b_spec, _, _ in pairs:
        operands += [a, b]
        in_specs += [a_spec, b_spec]
    for e, e_spec in extras:
        operands.append(e)
        in_specs.append(e_spec)
    return _call(body, grid=grid, in_specs=in_specs, out_specs=[o[1] for o in outs], out_shape=[o[0] for o in outs],
                 operands=operands, name=name, ride=ride)


def _rms_rows(v, g):
    r = lax.rsqrt(jnp.mean(v * v, axis=-1, keepdims=True) + EPS)
    return v * r * g


def _rms_fwd(name, x, g):
    t, d = x.shape
    tm = 512

    def body(x_ref, g_ref, o_ref):
        o_ref[...] = _rms_rows(x_ref[...], g_ref[...]).astype(BF16)

    return pl.pallas_call(
        body, grid=(t // tm,),
        in_specs=[pl.BlockSpec((tm, d), lambda i: (i, 0)), pl.BlockSpec((1, d), lambda i: (0, 0))],
        out_specs=pl.BlockSpec((tm, d), lambda i: (i, 0)), out_shape=SDS((t, d), BF16), name=name,
        compiler_params=_params())(x, g)


def _epi_residual_norm(accs, ex, outs, pids):
    x_new = accs[0] + ex[0][...]
    outs[0][...] = x_new
    outs[1][...] = _rms_rows(x_new, ex[1][...]).astype(BF16)


def _epi_residual_loss(accs, ex, outs, pids):
    y = accs[0] + ex[0][...]
    err = y - ex[1][...]
    dy = err * (1.0 / err.shape[-1])
    outs[0][...] = dy
    outs[2][...] = dy.astype(BF16)

    @pl.when(pids[0] == 0)
    def _():
        outs[1][...] = jnp.zeros_like(outs[1])

    outs[1][...] += jnp.sum(err * err)


def _epi_rms_bwd(accs, ex, outs, pids):
    dh = accs[0]
    xv, g, res = ex[0][...], ex[1][...], ex[2][...]
    r = lax.rsqrt(jnp.mean(xv * xv, axis=-1, keepdims=True) + EPS)
    xh = xv * r
    dy = dh * g
    dx = res + r * (dy - xh * jnp.mean(dy * xh, axis=-1, keepdims=True))
    outs[0][...] = dx
    if len(outs) > 2:
        outs[2][...] = dx.astype(BF16)

    @pl.when(pids[0] == 0)
    def _():
        outs[1][...] = jnp.zeros_like(outs[1])

    outs[1][...] += jnp.sum(dh * xh, axis=0, keepdims=True)


def _row_spec(tm, d):
    return pl.BlockSpec((tm, d), lambda i, *_: (i, 0))


def _const_spec(shape):
    nd = len(shape)
    return pl.BlockSpec(shape, lambda *_: (0,) * nd)


def _proj_in(name, h, w, layer, ride=None):
    t, d = h.shape
    n4 = w.shape[-1]
    tm = TILES["proj_in"]

    def epi(accs, ex, outs, pids):
        outs[0][...] = accs[0].astype(BF16)

    res = _matmul(
        name, (N_CHIPS, t // tm),
        [(h, pl.BlockSpec((tm, d), lambda p, i: (i, 0)), _whole,
          w, pl.BlockSpec((None, None, d, n4), lambda p, i: (p, layer, 0, 0)), _whole, 0)],
        [], [(SDS((t, N_CHIPS * n4), BF16), pl.BlockSpec((tm, n4), lambda p, i: (i, p)))],
        NN, epi, ride=ride)
    return res[0] if ride is None else (res[0][0], res[1])


def _proj_out(name, a, w, x, g_next=None, target=None):
    t, k = a.shape
    d = w.shape[-1]
    tm = TILES["proj_out"]
    if target is None:
        extras = [(x, _row_spec(tm, d)), (g_next, _const_spec((1, d)))]
        outs = [(SDS((t, d), F32), _row_spec(tm, d)), (SDS((t, d), BF16), _row_spec(tm, d))]
        epi = _epi_residual_norm
    else:
        extras = [(x, _row_spec(tm, d)), (target, _row_spec(tm, d))]
        outs = [(SDS((t, d), F32), _row_spec(tm, d)), (SDS((8, LANES), F32), _const_spec((8, LANES))),
                (SDS((t, d), BF16), _row_spec(tm, d))]
        epi = _epi_residual_loss
    return _matmul(name, (t // tm,), [(a, _row_spec(tm, k), _whole, w, _const_spec((k, d)), _whole, 0)], extras, outs, NN, epi)


def _ffn_in(name, h, wg, wu, layer, ride=None):
    t, d = h.shape
    n4 = wg.shape[-2]
    tm = TILES["ffn_in"]

    def epi(accs, ex, outs, pids):
        gate, up = accs
        s = _sigmoid(gate)
        silu = gate * s
        outs[0][...] = (up * (s + silu - silu * s)).astype(BF16)
        outs[1][...] = silu.astype(BF16)
        outs[2][...] = (silu * up).astype(BF16)

    w_spec = pl.BlockSpec((None, None, n4, d), lambda p, i: (p, layer, 0, 0))
    h_spec = pl.BlockSpec((tm, d), lambda p, i: (i, 0))
    o = (SDS((N_CHIPS, t, n4), BF16), pl.BlockSpec((None, tm, n4), lambda p, i: (p, i, 0)))
    return _matmul(name, (N_CHIPS, t // tm),
                   [(h, h_spec, _whole, wg, w_spec, _whole, 0), (h, h_spec, _whole, wu, w_spec, _whole, 1)], [],
                   [o, o, o], NT, epi, n_acc=2, ride=ride)


def _ffn_out(name, act, wd, layer, x, g_next=None, target=None, ride=None):
    _, t, n4 = act.shape
    d = wd.shape[-1]
    tm = TILES["ffn_out"]
    xs = _row_spec(tm, d)
    if target is None:
        extras = [(x, xs), (g_next, _const_spec((1, d)))]
        outs = [(SDS((t, d), F32), xs), (SDS((t, d), BF16), xs)]
        epi = _epi_residual_norm
    else:
        extras = [(x, xs), (target, xs)]
        outs = [(SDS((t, d), F32), xs), (SDS((8, LANES), F32), _const_spec((8, LANES))), (SDS((t, d), BF16), xs)]
        epi = _epi_residual_loss
    return _matmul(
        name, (t // tm,),
        [(act, pl.BlockSpec((N_CHIPS, tm, n4), lambda i: (0, i, 0)), _slab,
          wd, pl.BlockSpec((N_CHIPS, None, n4, d), lambda i: (0, layer, 0, 0)), _slab, 0)],
        extras, outs, NN, epi, slabs=N_CHIPS, ride=ride)


def _ffn_dact(name, g, wd, layer, gate, up, ride=None):
    t, d = g.shape
    n4 = wd.shape[-2]
    tm = TILES["ffn_dact"]

    def body(g_ref, w_ref, gate_ref, up_ref, dgate_ref, dup_ref):
        gv = g_ref[...]
        for p in range(N_CHIPS):
            dact = _dot(gv, w_ref[p], NT)
            dgate_ref[p] = (dact * gate_ref[p].astype(F32)).astype(BF16)
            dup_ref[p] = (dact * up_ref[p].astype(F32)).astype(BF16)

    blk = pl.BlockSpec((N_CHIPS, tm, n4), lambda i: (0, i, 0))
    return _call(
        body, grid=(t // tm,),
        in_specs=[_row_spec(tm, d), pl.BlockSpec((N_CHIPS, None, n4, d), lambda i: (0, layer, 0, 0)), blk, blk],
        out_specs=[blk, blk], out_shape=[SDS((N_CHIPS, t, n4), BF16)] * 2, operands=[g, wd, gate, up], name=name, ride=ride)


def _copy_epi(accs, ex, outs, pids):
    for a, o in zip(accs, outs):
        o[...] = a.astype(o.dtype)


def _dgrad_cols(name, dz_list, w_list, layer, three_d, x, g, res, bf16_copy=True, w_rows=False, ride=None):
    t, d = x.shape
    n4 = w_list[0].shape[-2 if w_rows else -1]
    tm = TILES["dgrad_cols"]
    if three_d:
        zs, z_pick = pl.BlockSpec((N_CHIPS, tm, n4), lambda i: (0, i, 0)), _slab
    else:
        zs, z_pick = _row_spec(tm, N_CHIPS * n4), (lambda ref, p: ref[:, p * n4:(p + 1) * n4])
    ws = pl.BlockSpec((N_CHIPS, None) + ((n4, d) if w_rows else (d, n4)), lambda i: (0, layer, 0, 0))
    xs = _row_spec(tm, d)
    return _matmul(
        name, (t // tm,), [(dz, zs, z_pick, w, ws, _slab, 0) for dz, w in zip(dz_list, w_list)],
        [(x, xs), (g, _const_spec((1, d))), (res, xs)],
        [(SDS((t, d), F32), xs), (SDS((1, d), F32), _const_spec((1, d)))] + ([(SDS((t, d), BF16), xs)] if bf16_copy else []),
        NN if w_rows else NT, _epi_rms_bwd, slabs=N_CHIPS, ride=ride)


def _dgrad_rows(name, g, w):
    t, d = g.shape
    k = w.shape[0]
    tm = TILES["dgrad_rows"]
    return _matmul(name, (t // tm,), [(g, _row_spec(tm, d), _whole, w, _const_spec((k, d)), _whole, 0)], [],
                   [(SDS((t, k), F32), _row_spec(tm, k))], NT, _copy_epi)[0]


A_TILE = 256


def _a_common(p_ref, lg_ref, lb_ref):
    pv = p_ref[...].astype(F32)
    a = _gelu(pv)
    u, v = a[:, :A_WIDTH], a[:, A_WIDTH:]
    vc = v - jnp.mean(v, axis=-1, keepdims=True)
    rs = lax.rsqrt(jnp.mean(vc * vc, axis=-1, keepdims=True) + EPS)
    vhat = vc * rs
    vn = vhat * lg_ref[...] + lb_ref[...]
    return pv, u, vhat, rs, vn.astype(BF16)


def _tril_weights(w_ref, g):
    r = lax.broadcasted_iota(jnp.int32, (CHUNK, CHUNK), 0)
    c = lax.broadcasted_iota(jnp.int32, (CHUNK, CHUNK), 1)
    return jnp.where(c <= r, w_ref[g], 0.0).astype(BF16), c <= r


def _mixer_a_fwd(proj, lg, lb, w, bias_t):
    t = proj.shape[0]

    def body(p_ref, lg_ref, lb_ref, w_ref, bt_ref, o_ref):
        _, u, _, _, vnb = _a_common(p_ref, lg_ref, lb_ref)
        for g in range(A_GROUPS):
            wt, _ = _tril_weights(w_ref, g)
            cs = slice(g * CHUNK, (g + 1) * CHUNK)
            for ch in range(A_TILE // CHUNK):
                rs_ = slice(ch * CHUNK, (ch + 1) * CHUNK)
                mixed = _dot(wt, vnb[rs_, cs], NN) + bt_ref[:, g:g + 1]
                o_ref[rs_, cs] = (u[rs_, cs] * mixed).astype(BF16)

    return pl.pallas_call(
        body, grid=(t // A_TILE,),
        in_specs=[pl.BlockSpec((A_TILE, 2 * A_WIDTH), lambda i: (i, 0)), _const_spec((1, A_WIDTH)),
                  _const_spec((1, A_WIDTH)), _const_spec((A_GROUPS, CHUNK, CHUNK)), _const_spec((CHUNK, A_GROUPS))],
        out_specs=pl.BlockSpec((A_TILE, A_WIDTH), lambda i: (i, 0)), out_shape=SDS((t, A_WIDTH), BF16),
        name="mixer_a_fwd", compiler_params=_params())(proj, lg, lb, w, bias_t)


def _mixer_a_bwd(proj, dcat, lg, lb, w, bias_t):
    t = proj.shape[0]

    def body(p_ref, da_ref, lg_ref, lb_ref, w_ref, bt_ref, dp_ref, dw_ref, dbt_ref, dlg_ref, dlb_ref, du_scr, dvn_scr):
        @pl.when(pl.program_id(0) == 0)
        def _():
            dw_ref[...] = jnp.zeros_like(dw_ref)
            dbt_ref[...] = jnp.zeros_like(dbt_ref)
            dlg_ref[...] = jnp.zeros_like(dlg_ref)
            dlb_ref[...] = jnp.zeros_like(dlb_ref)

        pv, u, vhat, rs, vnb = _a_common(p_ref, lg_ref, lb_ref)
        da = da_ref[...]
        for g in range(A_GROUPS):
            wt, keep = _tril_weights(w_ref, g)
            cs = slice(g * CHUNK, (g + 1) * CHUNK)
            for ch in range(A_TILE // CHUNK):
                rs_ = slice(ch * CHUNK, (ch + 1) * CHUNK)
                vg = vnb[rs_, cs]
                mixed = _dot(wt, vg, NN) + bt_ref[:, g:g + 1]
                du_scr[rs_, cs] = da[rs_, cs] * mixed
                dmx = da[rs_, cs] * u[rs_, cs]
                dw_ref[g] += jnp.where(keep, _dot(dmx, vg, NT), 0.0)
                dvn_scr[rs_, cs] = _dot(wt, dmx, TN)
                dbt_ref[:, g:g + 1] += jnp.sum(dmx, axis=1, keepdims=True)
        dvn = dvn_scr[...]
        dlg_ref[...] += jnp.sum(dvn * vhat, axis=0, keepdims=True)
        dlb_ref[...] += jnp.sum(dvn, axis=0, keepdims=True)
        dvh = dvn * lg_ref[...]
        dv = rs * (dvh - jnp.mean(dvh, axis=-1, keepdims=True) - vhat * jnp.mean(dvh * vhat, axis=-1, keepdims=True))
        gp = _gelu_grad(pv)
        dp_ref[:, :A_WIDTH] = (du_scr[...] * gp[:, :A_WIDTH]).astype(BF16)
        dp_ref[:, A_WIDTH:] = (dv * gp[:, A_WIDTH:]).astype(BF16)

    return pl.pallas_call(
        body, grid=(t // A_TILE,),
        in_specs=[pl.BlockSpec((A_TILE, 2 * A_WIDTH), lambda i: (i, 0)), pl.BlockSpec((A_TILE, A_WIDTH), lambda i: (i, 0)),
                  _const_spec((1, A_WIDTH)), _const_spec((1, A_WIDTH)), _const_spec((A_GROUPS, CHUNK, CHUNK)),
                  _const_spec((CHUNK, A_GROUPS))],
        out_specs=[pl.BlockSpec((A_TILE, 2 * A_WIDTH), lambda i: (i, 0)), _const_spec((A_GROUPS, CHUNK, CHUNK)),
                   _const_spec((CHUNK, A_GROUPS)), _const_spec((1, A_WIDTH)), _const_spec((1, A_WIDTH))],
        out_shape=[SDS((t, 2 * A_WIDTH), BF16), SDS((A_GROUPS, CHUNK, CHUNK), F32), SDS((CHUNK, A_GROUPS), F32),
                   SDS((1, A_WIDTH), F32), SDS((1, A_WIDTH), F32)],
        scratch_shapes=[pltpu.VMEM((A_TILE, A_WIDTH), F32), pltpu.VMEM((A_TILE, A_WIDTH), F32)],
        name="mixer_a_bwd", compiler_params=_params())(proj, dcat, lg, lb, w, bias_t)


def _rope_tables(t):
    half = ROPE_DIM // 2
    inv_freq = ROPE_THETA ** (-jnp.arange(half, dtype=F32) * 2.0 / ROPE_DIM)
    ang = jnp.arange(t, dtype=F32)[:, None] * inv_freq[None, :]
    cos, sin = jnp.cos(ang), jnp.sin(ang)
    one = jnp.ones((t, HEAD_DIM - ROPE_DIM), F32)
    zero = jnp.zeros((t, HEAD_DIM - ROPE_DIM), F32)
    zh = jnp.zeros((t, half), F32)
    c = jnp.concatenate([cos, cos, one], axis=1)
    s1 = jnp.concatenate([-sin, zh, zero], axis=1)
    s2 = jnp.concatenate([zh, sin, zero], axis=1)
    return tuple(jnp.tile(a, (1, LANES // HEAD_DIM)) for a in (c, s1, s2))


QK_TILE = 512
QK_ROWS = 64
QK_COLS = 2 * N_DIL * B_WIDTH


CHUNKS = B_WIDTH // LANES


def _regroup_out(scr, first, out_ref, rate, tile):
    rows = tile // rate
    for rho in range(rate):
        for c in range(CHUNKS):
            out_ref[rho, :, c * LANES:(c + 1) * LANES] = scr[first + c, pl.ds(rho, rows, stride=rate), :].astype(out_ref.dtype)


def _regroup_in(x_ref, scr, rate, tile):
    rows = tile // rate
    for rho in range(rate):
        for c in range(CHUNKS):
            scr[c, pl.ds(rho, rows, stride=rate), :] = x_ref[rho, :, c * LANES:(c + 1) * LANES].astype(F32)


def _regrouped_spec(rate, tile):
    return pl.BlockSpec((rate, tile // rate, B_WIDTH), lambda i, *_: (0, i, 0))


def _qk_fwd(proj, gains, tabs, ride=None):
    t = proj.shape[0]
    col0 = 2 * A_WIDTH // 1024
    r1, r2 = DIL_RATES[1], DIL_RATES[2]

    def body(p_ref, g_ref, c_ref, s1_ref, s2_ref, o_ref, q1_ref, q2_ref, k1_ref, k2_ref, scr):
        seg = _segment_mean_matrix(HEAD_DIM)
        for r0 in range(0, QK_TILE, QK_ROWS):
            rows = slice(r0, r0 + QK_ROWS)
            c, s1, s2 = c_ref[rows, :], s1_ref[rows, :], s2_ref[rows, :]
            for ci in range(1024 // LANES):
                ls = slice(ci * LANES, (ci + 1) * LANES)
                xv = p_ref[rows, ls].astype(F32)
                r = lax.rsqrt(_segment_dot(xv * xv, seg) + EPS)
                y = xv * r * g_ref[:, ls]
                val = y * c + pltpu.roll(y, LANES - 8, axis=1) * s1 + pltpu.roll(y, 8, axis=1) * s2
                o_ref[rows, ls] = val.astype(BF16)
                scr[ci, rows, :] = val

        j = pl.program_id(1)

        @pl.when(j == 0)
        def _():
            _regroup_out(scr, CHUNKS, q1_ref, r1, QK_TILE)

        @pl.when(j == 1)
        def _():
            _regroup_out(scr, 0, q2_ref, r2, QK_TILE)

        @pl.when(j == 2)
        def _():
            _regroup_out(scr, 0, k1_ref, r1, QK_TILE)
            _regroup_out(scr, CHUNKS, k2_ref, r2, QK_TILE)

    tab = pl.BlockSpec((QK_TILE, LANES), lambda i, j: (i, 0))
    g1, g2 = SDS((r1, t // r1, B_WIDTH), BF16), SDS((r2, t // r2, B_WIDTH), BF16)
    s1_, s2_ = _regrouped_spec(r1, QK_TILE), _regrouped_spec(r2, QK_TILE)
    return _call(
        body, grid=(t // QK_TILE, QK_COLS // 1024),
        in_specs=[pl.BlockSpec((QK_TILE, 1024), lambda i, j: (i, col0 + j)), pl.BlockSpec((1, 1024), lambda i, j: (0, j)),
                  tab, tab, tab],
        out_specs=[pl.BlockSpec((QK_TILE, 1024), lambda i, j: (i, j)), s1_, s2_, s1_, s2_],
        out_shape=[SDS((t, QK_COLS), BF16), g1, g2, g1, g2],
        scratch_shapes=[pltpu.VMEM((2 * CHUNKS, QK_TILE, LANES), F32)],
        operands=[proj, gains, *tabs], name="qk_norm_rope_fwd", ride=ride)


PERM_TILE = 512


def _permute(name, items, rate):
    t = items[0][0].shape[0]
    n = len(items)

    def body(*refs):
        scr = refs[-1]
        for x_ref, o_ref in zip(refs[:n], refs[n:2 * n]):
            for ci in range(CHUNKS):
                scr[ci] = x_ref[:, ci * LANES:(ci + 1) * LANES].astype(F32)
            _regroup_out(scr, 0, o_ref, rate, PERM_TILE)

    return pl.pallas_call(
        body, grid=(t // PERM_TILE,),
        in_specs=[pl.BlockSpec((PERM_TILE, B_WIDTH), functools.partial(lambda cb, i: (i, cb), cb)) for _, cb in items],
        out_specs=[_regrouped_spec(rate, PERM_TILE) for _ in items],
        out_shape=[SDS((rate, t // rate, B_WIDTH), a.dtype) for a, _ in items],
        scratch_shapes=[pltpu.VMEM((CHUNKS, PERM_TILE, LANES), F32)],
        name=name, compiler_params=_params())(*[a for a, _ in items])


def _head_lane_mask(h):
    lane = lax.broadcasted_iota(jnp.int32, (1, LANES), 1)
    return (lane < HEAD_DIM) if h == 0 else (lane >= HEAD_DIM)


def _attn_fwd(name, q, k, v, ride=None):
    rate, length = q[0].shape[0], q[0].shape[1]
    nb = length // ATT_BLOCK
    scale = HEAD_DIM ** -0.5

    def body(q_ref, kc_ref, kp_ref, vc_ref, vp_ref, o_ref, l_ref):
        n = pl.program_id(1)
        qi = lax.broadcasted_iota(jnp.int32, (ATT_BLOCK, 2 * ATT_BLOCK), 0)
        cj = lax.broadcasted_iota(jnp.int32, (ATT_BLOCK, 2 * ATT_BLOCK), 1)
        has_prev = jnp.where(n > 0, 0, 2 * ATT_BLOCK)
        mask = ((cj < ATT_BLOCK) & (cj >= qi + has_prev)) | ((cj >= ATT_BLOCK) & (cj - ATT_BLOCK <= qi))
        heads = [(hp, h) for hp in range(CHUNKS) for h in range(2)]
        q2, k2, v2 = {}, {}, {}
        for hp in range(CHUNKS):
            ls = slice(hp * LANES, (hp + 1) * LANES)
            q2[hp] = q_ref[:, ls]
            k2[hp] = jnp.concatenate([kp_ref[:, ls], kc_ref[:, ls]], axis=0)
            v2[hp] = jnp.concatenate([vp_ref[:, ls], vc_ref[:, ls]], axis=0)
        scores = {}
        for hp, h in heads:
            scores[hp, h] = _dot(jnp.where(_head_lane_mask(h), q2[hp], jnp.zeros_like(q2[hp])), k2[hp], NT) * scale
        probs, lses = {}, {}
        for hp, h in heads:
            s = jnp.where(mask, scores[hp, h], NEG_INF)
            m = jnp.max(s, axis=1, keepdims=True)
            p = jnp.exp(s - m)
            den = jnp.sum(p, axis=1, keepdims=True)
            lses[hp, h] = m + jnp.log(den)
            probs[hp, h] = (p / den).astype(BF16)
        for hp in range(CHUNKS):
            ls = slice(hp * LANES, (hp + 1) * LANES)
            o_acc = None
            for h in range(2):
                o = _dot(probs[hp, h], jnp.where(_head_lane_mask(h), v2[hp], jnp.zeros_like(v2[hp])), NN)
                o_acc = o if o_acc is None else o_acc + o
            o_ref[:, ls] = o_acc
            zeros = jnp.zeros((ATT_BLOCK, LANES), F32)
            l_ref[:, ls] = jnp.where(_head_lane_mask(1), lses[hp, 1] + zeros, lses[hp, 0] + zeros)

    def cur(cb):
        return pl.BlockSpec((None, ATT_BLOCK, B_WIDTH), lambda r, n: (r, n, cb))

    def prev(cb):
        return pl.BlockSpec((None, ATT_BLOCK, B_WIDTH), lambda r, n: (r, jnp.maximum(n - 1, 0), cb))

    out = pl.BlockSpec((None, ATT_BLOCK, B_WIDTH), lambda r, n: (r, n, 0))
    return _call(
        body, grid=(rate, nb),
        in_specs=[cur(q[1]), cur(k[1]), prev(k[1]), cur(v[1]), prev(v[1])],
        out_specs=[out, out], out_shape=[SDS((rate, length, B_WIDTH), F32)] * 2,
        operands=[q[0], k[0], k[0], v[0], v[0]], name=name, ride=ride)


def _attn_merge(a_out, o_list, l_list):
    t = a_out.shape[0]
    tm = PERM_TILE
    r1, r2 = DIL_RATES[1], DIL_RATES[2]

    def body(a_ref, o0, o1, o2, l0, l1, l2, cat_ref, lt_ref, lt1_ref, lt2_ref, so1, so2, sl1, sl2, slt):
        _regroup_in(o1, so1, r1, tm)
        _regroup_in(l1, sl1, r1, tm)
        _regroup_in(o2, so2, r2, tm)
        _regroup_in(l2, sl2, r2, tm)
        cat_ref[:, :A_WIDTH] = a_ref[...]
        for c in range(CHUNKS):
            ls = slice(c * LANES, (c + 1) * LANES)
            lg = [l0[:, ls], sl1[c], sl2[c]]
            m = jnp.maximum(jnp.maximum(lg[0], lg[1]), lg[2])
            es = [jnp.exp(l - m) for l in lg]
            den = es[0] + es[1] + es[2]
            b = (es[0] * o0[:, ls] + es[1] * so1[c] + es[2] * so2[c]) / den
            cat_ref[:, A_WIDTH + c * LANES:A_WIDTH + (c + 1) * LANES] = b.astype(BF16)
            lt = m + jnp.log(den)
            lt_ref[:, ls] = lt
            slt[c] = lt
        _regroup_out(slt, 0, lt1_ref, r1, tm)
        _regroup_out(slt, 0, lt2_ref, r2, tm)

    blk = _row_spec(tm, B_WIDTH)
    g1, g2 = _regrouped_spec(r1, tm), _regrouped_spec(r2, tm)
    return pl.pallas_call(
        body, grid=(t // tm,), in_specs=[blk, blk, g1, g2, blk, g1, g2],
        out_specs=[_row_spec(tm, A_WIDTH + B_WIDTH), blk, g1, g2],
        out_shape=[SDS((t, A_WIDTH + B_WIDTH), BF16), SDS((t, B_WIDTH), F32), SDS((r1, t // r1, B_WIDTH), F32),
                   SDS((r2, t // r2, B_WIDTH), F32)],
        scratch_shapes=[pltpu.VMEM((CHUNKS, tm, LANES), F32)] * 5,
        name="attn_merge", compiler_params=_params())(a_out, *o_list, *l_list)


def _attn_bwd_prep(dcat, cat):
    t = dcat.shape[0]
    tm = PERM_TILE
    r1, r2 = DIL_RATES[1], DIL_RATES[2]

    def body(d_ref, b_ref, db_ref, dd_ref, db1_ref, dd1_ref, db2_ref, dd2_ref, sdb, sdd):
        seg = _segment_mean_matrix(HEAD_DIM, scale=1.0)
        for c in range(CHUNKS):
            ls = slice(c * LANES, (c + 1) * LANES)
            d = d_ref[:, ls]
            dsum = _segment_dot(d * b_ref[:, ls].astype(F32), seg)
            db_ref[:, ls] = d.astype(BF16)
            dd_ref[:, ls] = dsum
            sdb[c] = d
            sdd[c] = dsum
        _regroup_out(sdb, 0, db1_ref, r1, tm)
        _regroup_out(sdd, 0, dd1_ref, r1, tm)
        _regroup_out(sdb, 0, db2_ref, r2, tm)
        _regroup_out(sdd, 0, dd2_ref, r2, tm)

    right = pl.BlockSpec((tm, B_WIDTH), lambda i: (i, 1))
    blk = _row_spec(tm, B_WIDTH)
    g1, g2 = _regrouped_spec(r1, tm), _regrouped_spec(r2, tm)
    return pl.pallas_call(
        body, grid=(t // tm,), in_specs=[right, right], out_specs=[blk, blk, g1, g1, g2, g2],
        out_shape=[SDS((t, B_WIDTH), BF16), SDS((t, B_WIDTH), F32), SDS((r1, t // r1, B_WIDTH), BF16),
                   SDS((r1, t // r1, B_WIDTH), F32), SDS((r2, t // r2, B_WIDTH), BF16), SDS((r2, t // r2, B_WIDTH), F32)],
        scratch_shapes=[pltpu.VMEM((CHUNKS, tm, LANES), F32)] * 2,
        name="attn_bwd_prep", compiler_params=_params())(dcat, cat)


def _attn_bwd(name, q, k, v, db, lse, dd, ride=None):
    rate, length = db.shape[0], db.shape[1]
    nb = length // ATT_BLOCK
    scale = HEAD_DIM ** -0.5

    def body(qa_ref, qb_ref, k_ref, v_ref, dba_ref, dbb_ref, la_ref, lb_ref, da_ref, dbd_ref, dq_ref, dk_ref, dv_ref, carry):
        m = pl.program_id(1)

        @pl.when(m == 0)
        def _():
            carry[...] = jnp.zeros_like(carry)

        row = lax.broadcasted_iota(jnp.int32, (2 * ATT_BLOCK, ATT_BLOCK), 0)
        kj = lax.broadcasted_iota(jnp.int32, (2 * ATT_BLOCK, ATT_BLOCK), 1)
        no_next = jnp.where(m + 1 < nb, 0, 2 * ATT_BLOCK)
        mask = ((row < ATT_BLOCK) & (kj <= row)) | ((row >= ATT_BLOCK) & (kj >= row - ATT_BLOCK + no_next))
        heads = [(hp, h) for hp in range(CHUNKS) for h in range(2)]
        q2, db2, lse2, dd2, k2, v2 = {}, {}, {}, {}, {}, {}
        for hp in range(CHUNKS):
            ls = slice(hp * LANES, (hp + 1) * LANES)
            k2[hp], v2[hp] = k_ref[:, ls], v_ref[:, ls]
            q2[hp] = jnp.concatenate([qa_ref[:, ls], qb_ref[:, ls]], axis=0)
            db2[hp] = jnp.concatenate([dba_ref[:, ls], dbb_ref[:, ls]], axis=0)
            lse2[hp] = jnp.concatenate([la_ref[:, ls], lb_ref[:, ls]], axis=0)
            dd2[hp] = jnp.concatenate([da_ref[:, ls], dbd_ref[:, ls]], axis=0)
        km, scores, dps = {}, {}, {}
        for hp, h in heads:
            hm = _head_lane_mask(h)
            km[hp, h] = jnp.where(hm, k2[hp], jnp.zeros_like(k2[hp]))
            scores[hp, h] = _dot(q2[hp], km[hp, h], NT) * scale
            dps[hp, h] = _dot(db2[hp], jnp.where(hm, v2[hp], jnp.zeros_like(v2[hp])), NT)
        probs, dss = {}, {}
        for hp, h in heads:
            hm = _head_lane_mask(h)
            lse_col = jnp.max(jnp.where(hm, lse2[hp], NEG_INF), axis=1, keepdims=True)
            dd_col = jnp.max(jnp.where(hm, dd2[hp], NEG_INF), axis=1, keepdims=True)
            p = jnp.where(mask, jnp.exp(scores[hp, h] - lse_col), 0.0)
            probs[hp, h] = p.astype(BF16)
            dss[hp, h] = (p * (dps[hp, h] - dd_col) * scale).astype(BF16)
        for hp in range(CHUNKS):
            ls = slice(hp * LANES, (hp + 1) * LANES)
            dq_acc, dk_acc, dv_acc = None, None, None
            for h in range(2):
                hm = _head_lane_mask(h)
                dvc = _dot(probs[hp, h], jnp.where(hm, db2[hp], jnp.zeros_like(db2[hp])), TN)
                dqc = _dot(dss[hp, h], km[hp, h], NN)
                dkc = _dot(dss[hp, h], jnp.where(hm, q2[hp], jnp.zeros_like(q2[hp])), TN)
                dq_acc = dqc if dq_acc is None else dq_acc + dqc
                dk_acc = dkc if dk_acc is None else dk_acc + dkc
                dv_acc = dvc if dv_acc is None else dv_acc + dvc
            dq_ref[:, ls] = (dq_acc[:ATT_BLOCK] + carry[:, ls]).astype(BF16)
            carry[:, ls] = dq_acc[ATT_BLOCK:]
            dk_ref[:, ls] = dk_acc.astype(BF16)
            dv_ref[:, ls] = dv_acc.astype(BF16)

    def cur(cb):
        return pl.BlockSpec((None, ATT_BLOCK, B_WIDTH), lambda r, n: (r, n, cb))

    def nxt(cb):
        return pl.BlockSpec((None, ATT_BLOCK, B_WIDTH), lambda r, n: (r, jnp.minimum(n + 1, nb - 1), cb))

    out = cur(0)
    return _call(
        body, grid=(rate, nb),
        in_specs=[cur(q[1]), nxt(q[1]), cur(k[1]), cur(v[1]), cur(0), nxt(0), cur(0), nxt(0), cur(0), nxt(0)],
        out_specs=[out, out, out], out_shape=[SDS((rate, length, B_WIDTH), BF16)] * 3,
        scratch_shapes=[pltpu.VMEM((ATT_BLOCK, B_WIDTH), F32)],
        operands=[q[0], q[0], k[0], v[0], db, db, lse, lse, dd, dd], name=name, ride=ride)


AB_IN = 2 * A_WIDTH + 3 * N_DIL * B_WIDTH
ASM_TILE = 256


def _dproj_assemble(proj, d_a, dq, dk, dv, gains, tabs, ride=None):
    t = proj.shape[0]
    n_in = 3 * N_DIL

    def body(p_ref, da_ref, *rest):
        grads = rest[:n_in]
        g_ref, c_ref, s1_ref, s2_ref, o_ref, dg_ref = rest[n_in:n_in + 6]
        scratch = rest[n_in + 6:]

        @pl.when(pl.program_id(0) == 0)
        def _():
            dg_ref[...] = jnp.zeros_like(dg_ref)

        chunk = {}
        k_scr = 0
        for j in range(n_in):
            g = j % N_DIL
            if DIL_RATES[g] == 1:
                for ci in range(CHUNKS):
                    chunk[j, ci] = functools.partial(lambda r, ci: r[:, ci * LANES:(ci + 1) * LANES].astype(F32), grads[j], ci)
            else:
                scr = scratch[k_scr]
                k_scr += 1
                _regroup_in(grads[j], scr, DIL_RATES[g], ASM_TILE)
                for ci in range(CHUNKS):
                    chunk[j, ci] = functools.partial(lambda s, ci: s[ci], scr, ci)

        seg = _segment_mean_matrix(HEAD_DIM)
        c, s1, s2 = c_ref[...], s1_ref[...], s2_ref[...]
        o_ref[:, :2 * A_WIDTH] = da_ref[...]
        for jg in range(2 * N_DIL):
            for ci in range(CHUNKS):
                col = jg * B_WIDTH + ci * LANES
                src = slice(2 * A_WIDTH + col, 2 * A_WIDTH + col + LANES)
                xv = p_ref[:, src].astype(F32)
                r = lax.rsqrt(_segment_dot(xv * xv, seg) + EPS)
                xh = xv * r
                gain = g_ref[:, col:col + LANES]
                do = chunk[jg, ci]()
                dy = do * c + pltpu.roll(do * s1, 8, axis=1) + pltpu.roll(do * s2, LANES - 8, axis=1)
                dg_ref[:, col:col + LANES] += jnp.sum(dy * xh, axis=0, keepdims=True)
                dxh = dy * gain
                o_ref[:, src] = (r * (dxh - xh * _segment_dot(dxh * xh, seg))).astype(BF16)
        v0 = 2 * A_WIDTH + QK_COLS
        for g in range(N_DIL):
            for ci in range(CHUNKS):
                col = v0 + g * B_WIDTH + ci * LANES
                o_ref[:, col:col + LANES] = chunk[2 * N_DIL + g, ci]().astype(BF16)

    specs = [_row_spec(ASM_TILE, B_WIDTH) if r == 1 else _regrouped_spec(r, ASM_TILE) for r in DIL_RATES] * 3
    n_scr = 3 * sum(1 for r in DIL_RATES if r > 1)
    tab = _row_spec(ASM_TILE, LANES)
    return _call(
        body, grid=(t // ASM_TILE,),
        in_specs=[_row_spec(ASM_TILE, AB_IN), _row_spec(ASM_TILE, 2 * A_WIDTH)] + specs
        + [_const_spec((1, QK_COLS)), tab, tab, tab],
        out_specs=[_row_spec(ASM_TILE, AB_IN), _const_spec((1, QK_COLS))],
        out_shape=[SDS((t, AB_IN), BF16), SDS((1, QK_COLS), F32)],
        scratch_shapes=[pltpu.VMEM((CHUNKS, ASM_TILE, LANES), F32)] * n_scr,
        operands=[proj, d_a, *dq, *dk, *dv, gains, *tabs], name="dproj_assemble", ride=ride)


def _fold_heads(dg_lane):
    n = dg_lane.shape[1]

    def body(x_ref, o_ref):
        r = lax.broadcasted_iota(jnp.int32, (B_WIDTH, B_WIDTH), 0) % HEAD_DIM
        c = lax.broadcasted_iota(jnp.int32, (B_WIDTH, B_WIDTH), 1) % HEAD_DIM
        fold = jnp.where(r == c, 1.0, 0.0).astype(F32)
        for jg in range(n // B_WIDTH):
            ls = slice(jg * B_WIDTH, (jg + 1) * B_WIDTH)
            o_ref[:, ls] = _dot_hi(jnp.broadcast_to(x_ref[:, ls], (8, B_WIDTH)), fold)

    return pl.pallas_call(body, out_shape=SDS((8, n), F32), name="fold_heads", compiler_params=_params())(dg_lane)


CD_TILE = 256
TAP_ROWS = 64
CD_IN = 2 * C_WIDTH + 3 * 512


def _shifted_copies(src, dst, rows):
    dst[0, :rows] = src[...]
    for b in range(1, 8):
        dst[b, :rows - 8] = src[pl.ds(b, rows - 8), :]


def _rows_from(shifted, start, n, lanes=slice(None)):
    b = start % 8
    return shifted[b, pl.ds(start - b, n), lanes]


def _mixer_cd_fwd(proj, cw, cb, lg, lb, dw):
    t = proj.shape[0]
    per = CD_TILE // HALO

    def body(h_ref, m_ref, cw_ref, cb_ref, lg_ref, lb_ref, dw_ref, o_ref, c1_ref, c_scr, e_scr, c_sh):
        not_first = (pl.program_id(0) > 0).astype(F32)
        lanes = [slice(c * LANES, (c + 1) * LANES) for c in range(C_WIDTH // LANES)]

        def col(ref, part, ls):
            return ref[:, part * C_WIDTH + ls.start:part * C_WIDTH + ls.stop].astype(F32)

        for ls in lanes:
            c_scr[:HALO, ls] = col(h_ref, 0, ls) * _sigmoid(col(h_ref, 1, ls)) * not_first
            c_scr[HALO:, ls] = col(m_ref, 0, ls) * _sigmoid(col(m_ref, 1, ls))
            e_scr[:HALO, ls] = col(h_ref, 3, ls) * col(h_ref, 4, ls) * not_first
            e_scr[HALO:, ls] = col(m_ref, 3, ls) * col(m_ref, 4, ls)
        _shifted_copies(c_scr, c_sh, HALO + CD_TILE)
        for ls in lanes:
            for r0 in range(0, CD_TILE, TAP_ROWS):
                acc = jnp.zeros((TAP_ROWS, LANES), F32)
                for k in range(C_KERNEL):
                    acc = acc + cw_ref[k:k + 1, ls] * _rows_from(c_sh, r0 + HALO - (C_KERNEL - 1) + k, TAP_ROWS, ls)
                c1_ref[r0:r0 + TAP_ROWS, ls] = acc + cb_ref[:, ls]
        mean = sum(jnp.sum(c1_ref[:, ls], axis=-1, keepdims=True) for ls in lanes) * (1.0 / C_WIDTH)
        var = sum(jnp.sum((c1_ref[:, ls] - mean) ** 2, axis=-1, keepdims=True) for ls in lanes) * (1.0 / C_WIDTH)
        rs = lax.rsqrt(var + EPS)
        for ls in lanes:
            c2 = (c1_ref[:, ls] - mean) * rs * lg_ref[:, ls] + lb_ref[:, ls]
            o_ref[:, ls] = (c2 * _sigmoid(c2)).astype(BF16)
            d1 = jnp.zeros((CD_TILE, LANES), F32)
            for k in range(D_KERNEL):
                d1 = d1 + dw_ref[k:k + 1, ls] * e_scr[pl.ds(HALO - (D_KERNEL - 1) + k, CD_TILE), ls]
            o_ref[:, C_WIDTH + ls.start:C_WIDTH + ls.stop] = (col(m_ref, 2, ls) * d1).astype(BF16)

    return pl.pallas_call(
        body, grid=(t // CD_TILE,),
        in_specs=[pl.BlockSpec((HALO, CD_IN), lambda i: (jnp.maximum(i * per - 1, 0), 0)), _row_spec(CD_TILE, CD_IN),
                  _const_spec((32, C_WIDTH)), _const_spec((1, C_WIDTH)), _const_spec((1, C_WIDTH)), _const_spec((1, C_WIDTH)),
                  _const_spec((8, C_WIDTH))],
        out_specs=[_row_spec(CD_TILE, 2 * C_WIDTH), _row_spec(CD_TILE, C_WIDTH)],
        out_shape=[SDS((t, 2 * C_WIDTH), BF16), SDS((t, C_WIDTH), F32)],
        scratch_shapes=[pltpu.VMEM((HALO + CD_TILE, C_WIDTH), F32)] * 2 + [pltpu.VMEM((8, HALO + CD_TILE, C_WIDTH), F32)],
        name="mixer_cd_fwd", compiler_params=_params())(proj, proj, cw, cb, lg, lb, dw)


def _mixer_cd_bwd(proj, dcat, c1, cw, lg, lb, dw, ride=None):
    t = proj.shape[0]
    per = CD_TILE // HALO
    nt = t // CD_TILE
    ext = CD_TILE + HALO

    def body(hp_ref, m_ref, hn_ref, dm_ref, dn_ref, c1m_ref, c1n_ref, cw_ref, lg_ref, lb_ref, dw_ref,
             dp_ref, dcw_ref, dcb_ref, dlg_ref, dlb_ref, ddw_ref, c_scr, e_scr, dc1_scr, dd1_scr, c_sh, dc1_sh, dcw_acc,
             dvh_scr, vhat_scr):
        i = pl.program_id(0)

        @pl.when(i == 0)
        def _():
            for r in (dcw_acc, dcb_ref, dlg_ref, dlb_ref, ddw_ref):
                r[...] = jnp.zeros_like(r)

        not_first = (i > 0).astype(F32)
        not_last = (i < nt - 1).astype(F32)
        main = slice(HALO, HALO + CD_TILE)
        lanes = [slice(c * LANES, (c + 1) * LANES) for c in range(C_WIDTH // LANES)]

        def col(ref, part, ls):
            return ref[:, part * C_WIDTH + ls.start:part * C_WIDTH + ls.stop].astype(F32)

        for ls in lanes:
            c_scr[:HALO, ls] = col(hp_ref, 0, ls) * _sigmoid(col(hp_ref, 1, ls)) * not_first
            c_scr[main, ls] = col(m_ref, 0, ls) * _sigmoid(col(m_ref, 1, ls))
            c_scr[HALO + CD_TILE:, ls] = col(hn_ref, 0, ls) * _sigmoid(col(hn_ref, 1, ls)) * not_last
            e_scr[:HALO, ls] = col(hp_ref, 3, ls) * col(hp_ref, 4, ls) * not_first
            e_scr[main, ls] = col(m_ref, 3, ls) * col(m_ref, 4, ls)
            e_scr[HALO + CD_TILE:, ls] = col(hn_ref, 3, ls) * col(hn_ref, 4, ls) * not_last
        _shifted_copies(c_scr, c_sh, 2 * HALO + CD_TILE)

        def c1_of(ls):
            return jnp.concatenate([c1m_ref[:, ls], c1n_ref[:, ls]], axis=0)

        mean = sum(jnp.sum(c1_of(ls), axis=-1, keepdims=True) for ls in lanes) * (1.0 / C_WIDTH)
        var = sum(jnp.sum((c1_of(ls) - mean) ** 2, axis=-1, keepdims=True) for ls in lanes) * (1.0 / C_WIDTH)
        rs = lax.rsqrt(var + EPS)
        sum_dvh, sum_dvh_vhat = 0.0, 0.0
        for ls in lanes:
            vhat = (c1_of(ls) - mean) * rs
            c2 = vhat * lg_ref[:, ls] + lb_ref[:, ls]
            sig = _sigmoid(c2)
            dc = jnp.concatenate([dm_ref[:, ls], dn_ref[:, ls] * not_last], axis=0)
            dc2 = dc * (sig * (1.0 + c2 * (1.0 - sig)))
            dvh = dc2 * lg_ref[:, ls]
            sum_dvh = sum_dvh + jnp.sum(dvh, axis=-1, keepdims=True)
            sum_dvh_vhat = sum_dvh_vhat + jnp.sum(dvh * vhat, axis=-1, keepdims=True)
            dvh_scr[:, ls] = dvh
            vhat_scr[:, ls] = vhat
            dlg_ref[:, ls] += jnp.sum((dc2 * vhat)[:CD_TILE], axis=0, keepdims=True)
            dlb_ref[:, ls] += jnp.sum(dc2[:CD_TILE], axis=0, keepdims=True)
        for ls in lanes:
            dc1 = rs * (dvh_scr[:, ls] - sum_dvh * (1.0 / C_WIDTH) - vhat_scr[:, ls] * (sum_dvh_vhat * (1.0 / C_WIDTH)))
            dc1_scr[:, ls] = dc1
            dcb_ref[:, ls] += jnp.sum(dc1[:CD_TILE], axis=0, keepdims=True)
        _shifted_copies(dc1_scr, dc1_sh, ext)
        for ls in lanes:
            for r0 in range(0, CD_TILE, TAP_ROWS):
                rows = slice(r0, r0 + TAP_ROWS)
                dc1_m = dc1_scr[rows, ls]
                dc0 = jnp.zeros((TAP_ROWS, LANES), F32)
                for k in range(C_KERNEL):
                    dc0 = dc0 + cw_ref[k:k + 1, ls] * _rows_from(dc1_sh, r0 + C_KERNEL - 1 - k, TAP_ROWS, ls)
                    prod = dc1_m * _rows_from(c_sh, r0 + HALO - (C_KERNEL - 1) + k, TAP_ROWS, ls)
                    dcw_acc[k, :, ls] += prod.reshape(TAP_ROWS // 8, 8, LANES).sum(axis=0)
                g_m = m_ref[rows, C_WIDTH + ls.start:C_WIDTH + ls.stop].astype(F32)
                a_m = m_ref[rows, ls].astype(F32)
                sig_m = _sigmoid(g_m)
                dp_ref[rows, ls] = (dc0 * sig_m).astype(BF16)
                dp_ref[rows, C_WIDTH + ls.start:C_WIDTH + ls.stop] = (dc0 * a_m * sig_m * (1.0 - sig_m)).astype(BF16)

        @pl.when(i == nt - 1)
        def _():
            dcw_ref[...] = jnp.sum(dcw_acc[...], axis=1)

        for ls in lanes:
            wide = slice(C_WIDTH + ls.start, C_WIDTH + ls.stop)
            d1 = jnp.zeros((CD_TILE, LANES), F32)
            for k in range(D_KERNEL):
                d1 = d1 + dw_ref[k:k + 1, ls] * e_scr[pl.ds(HALO - (D_KERNEL - 1) + k, CD_TILE), ls]
            dd_m = dm_ref[:, wide]
            dd1 = jnp.concatenate([dd_m * col(m_ref, 2, ls), dn_ref[:, wide] * col(hn_ref, 2, ls) * not_last], axis=0)
            dd1_scr[:, ls] = dd1
            dp_ref[:, 2 * C_WIDTH + ls.start:2 * C_WIDTH + ls.stop] = (dd_m * d1).astype(BF16)
            de = jnp.zeros((CD_TILE, LANES), F32)
            for k in range(D_KERNEL):
                de = de + dw_ref[k:k + 1, ls] * dd1_scr[pl.ds(D_KERNEL - 1 - k, CD_TILE), ls]
                ddw_ref[k:k + 1, ls] += jnp.sum(dd1[:CD_TILE] * e_scr[pl.ds(HALO - (D_KERNEL - 1) + k, CD_TILE), ls], axis=0, keepdims=True)
            dp_ref[:, 3 * C_WIDTH + ls.start:3 * C_WIDTH + ls.stop] = (de * col(m_ref, 4, ls)).astype(BF16)
            dp_ref[:, 4 * C_WIDTH + ls.start:4 * C_WIDTH + ls.stop] = (de * col(m_ref, 3, ls)).astype(BF16)

    halo_prev = lambda i: (jnp.maximum(i * per - 1, 0), 0)
    halo_next = lambda i: (jnp.minimum((i + 1) * per, t // HALO - 1), 0)
    vec = _const_spec((1, C_WIDTH))
    return _call(
        body, grid=(nt,),
        in_specs=[pl.BlockSpec((HALO, CD_IN), halo_prev), _row_spec(CD_TILE, CD_IN), pl.BlockSpec((HALO, CD_IN), halo_next),
                  _row_spec(CD_TILE, 2 * C_WIDTH), pl.BlockSpec((HALO, 2 * C_WIDTH), halo_next),
                  _row_spec(CD_TILE, C_WIDTH), pl.BlockSpec((HALO, C_WIDTH), halo_next),
                  _const_spec((32, C_WIDTH)), vec, vec, _const_spec((8, C_WIDTH))],
        out_specs=[_row_spec(CD_TILE, CD_IN), _const_spec((32, C_WIDTH)), vec, vec, vec, _const_spec((8, C_WIDTH))],
        out_shape=[SDS((t, CD_IN), BF16), SDS((32, C_WIDTH), F32), SDS((1, C_WIDTH), F32), SDS((1, C_WIDTH), F32),
                   SDS((1, C_WIDTH), F32), SDS((8, C_WIDTH), F32)],
        scratch_shapes=[pltpu.VMEM((2 * HALO + CD_TILE, C_WIDTH), F32)] * 2 + [pltpu.VMEM((ext, C_WIDTH), F32)] * 2
        + [pltpu.VMEM((8, 2 * HALO + CD_TILE, C_WIDTH), F32), pltpu.VMEM((8, ext, C_WIDTH), F32),
           pltpu.VMEM((32, 8, C_WIDTH), F32)] + [pltpu.VMEM((ext, C_WIDTH), F32)] * 2,
        operands=[proj, proj, proj, dcat, dcat, c1, c1, cw, lg, lb, dw], name="mixer_cd_bwd", ride=ride)


def _wgrad(name, pairs, out_rc, t, ride):
    tk = TILES["wgrad"]
    assert tk == t, "the whole contraction has to fit one grid step"
    r, c = out_rc
    n = len(pairs)

    def body(*refs):
        ab, out_refs = refs[:2 * n], refs[2 * n:]
        for j in range(n):
            out_refs[j][...] = _dot(ab[2 * j][...], ab[2 * j + 1][...], TN).astype(BF16)

    operands, in_specs = [], []
    for lhs, lhs_spec, rhs, rhs_spec in pairs:
        operands += [lhs, rhs]
        in_specs += [lhs_spec, rhs_spec]
    res = _call(body, grid=(N_CHIPS, t // tk), in_specs=in_specs,
                out_specs=[pl.BlockSpec((None, r, c), lambda p, k: (p, 0, 0))] * n,
                out_shape=[SDS((N_CHIPS, r, c), BF16)] * n, operands=operands, name=name, ride=ride)
    outs, ride_res = (res, None) if ride is None else res
    outs = [o.reshape(N_CHIPS, 2, r // 2, c) for o in outs]
    return outs if ride is None else (outs, ride_res)


def _wgrad_col_sharded(name, h, dz_list, three_d, ride=None):
    t, d = h.shape
    tk = TILES["wgrad"]
    n4 = dz_list[0].shape[-1] if three_d else dz_list[0].shape[-1] // N_CHIPS
    hs = pl.BlockSpec((tk, d), lambda p, k: (k, 0))
    zs = pl.BlockSpec((None, tk, n4), lambda p, k: (p, k, 0)) if three_d else pl.BlockSpec((tk, n4), lambda p, k: (k, p))
    return _wgrad(name, [(h, hs, dz, zs) for dz in dz_list], (d, n4), t, ride)


def _wgrad_row_sharded(name, a, g, three_d, ride=None):
    many = isinstance(a, (list, tuple))
    a_list = list(a) if many else [a]
    t, d = g.shape
    tk = TILES["wgrad"]
    k4 = a_list[0].shape[-1] if three_d else a_list[0].shape[-1] // N_CHIPS
    a_spec = pl.BlockSpec((None, tk, k4), lambda p, k: (p, k, 0)) if three_d else pl.BlockSpec((tk, k4), lambda p, k: (k, p))
    gs = pl.BlockSpec((tk, d), lambda p, k: (k, 0))
    res = _wgrad(name, [(a_j, a_spec, g, gs) for a_j in a_list], (k4, d), t, ride)
    if many:
        return res
    return res[0] if ride is None else (res[0][0], res[1])


def _mesh_scalars():
    return jnp.stack([lax.axis_index("c"), 2 * lax.axis_index("x") + lax.axis_index("y")]).astype(jnp.int32)


def _stage_own(name, w, layer, dtype):
    layers, r, cols = w.shape
    h = r // 2

    def body(s_ref, x_ref, o_ref):
        o_ref[...] = x_ref[...].astype(dtype)

    return pl.pallas_call(
        body,
        grid_spec=pltpu.PrefetchScalarGridSpec(
            num_scalar_prefetch=1, grid=(2,),
            in_specs=[pl.BlockSpec((None, h, cols), lambda i, s: (2 * layer + i, 0, 0))],
            out_specs=pl.BlockSpec((None, None, h, cols), lambda i, s: (s[1], i, 0, 0))),
        out_shape=SDS((N_CHIPS, 2, h, cols), dtype), name=name,
        compiler_params=_params())(_mesh_scalars(), w.reshape(2 * layers, h, cols))


def _remote(src, dst, send_sem, recv_sem, device):
    return pltpu.make_async_remote_copy(src, dst, send_sem, recv_sem, device_id=device, device_id_type=MESH)


def _ride_gather_send(bufs):
    n = len(bufs)

    def each(b, sems, act):
        send, recv = sems
        x, y, c, p, others = _position()
        for t in range(n):
            for j, (qx, qy) in enumerate(others):
                act(b[t].at[p, c], b[t].at[2 * qx + qy, c], send.at[t, j], recv.at[t, j], (qx, qy, c))

    def start(ins, b, new, sems):
        each(b, sems, lambda mine, landed, s, r, dev: _remote(mine, mine, s, r, dev).start())

    def finish(ins, b, new, sems):
        def act(mine, landed, s, r, dev):
            _remote(mine, mine, s, r, dev).wait_send()
            _remote(landed, landed, s, r, dev).wait_recv()
        each(b, sems, act)

    return _Ride([], bufs, [], [(n, 3), (n, 3)], start, finish)


def _ride_gather_pass(bufs):
    n = len(bufs)

    def each(b, sems, act):
        send, recv = sems
        x, y, c, p, others = _position()
        for t in range(n):
            for j, (qx, qy) in enumerate(others):
                act(b[t].at[2 * qx + qy, c], b[t].at[2 * qx + qy, 1 - c], send.at[t, j], recv.at[t, j], (x, y, 1 - c))

    def start(ins, b, new, sems):
        each(b, sems, lambda landed, passed, s, r, dev: _remote(landed, landed, s, r, dev).start())

    def finish(ins, b, new, sems):
        def act(landed, passed, s, r, dev):
            _remote(landed, landed, s, r, dev).wait_send()
            _remote(passed, passed, s, r, dev).wait_recv()
        each(b, sems, act)

    return _Ride([], bufs, [], [(n, 3), (n, 3)], start, finish)


def _ride_swap(tensors):
    n = len(tensors)

    def each(ins, new, sems, act):
        send, recv = sems
        x, y, c, _, _ = _position()
        for t in range(n):
            act(_remote(ins[t].at[:, 1 - c], new[t], send.at[t], recv.at[t], (x, y, 1 - c)))

    def start(ins, b, new, sems):
        each(ins, new, sems, lambda cp: cp.start())

    def finish(ins, b, new, sems):
        each(ins, new, sems, lambda cp: cp.wait())

    return _Ride(tensors, [], [SDS((s.shape[0],) + s.shape[2:], s.dtype) for s in tensors], [(n,), (n,)], start, finish)


def _ride_scatter(tensors, landing):
    n = len(tensors)

    def each(ins, b, sems, act):
        send, recv = sems
        x, y, c, p, others = _position()
        for t in range(n):
            for j, (qx, qy) in enumerate(others):
                q = 2 * qx + qy
                act(ins[t].at[q], b[t].at[p], b[t].at[q], send.at[t, j], recv.at[t, j], (qx, qy, c))

    def start(ins, b, new, sems):
        each(ins, b, sems, lambda src, dst, landed, s, r, dev: _remote(src, dst, s, r, dev).start())

    def finish(ins, b, new, sems):
        def act(src, dst, landed, s, r, dev):
            _remote(src, dst, s, r, dev).wait_send()
            _remote(landed, landed, s, r, dev).wait_recv()
        each(ins, b, sems, act)

    return _Ride(tensors, landing, [], [(n, 3), (n, 3)], start, finish)


def _ride_join(bufs):
    n = len(bufs)

    def each(b, sems, act):
        send, recv = sems
        x, y, c, _, _ = _position()
        for t in range(n):
            act(b[t].at[c], b[t].at[1 - c], send.at[t], recv.at[t], (x, y, 1 - c))

    def start(ins, b, new, sems):
        each(b, sems, lambda mine, theirs, s, r, dev: _remote(mine, mine, s, r, dev).start())

    def finish(ins, b, new, sems):
        def act(mine, theirs, s, r, dev):
            _remote(mine, mine, s, r, dev).wait_send()
            _remote(theirs, theirs, s, r, dev).wait_recv()
        each(b, sems, act)

    return _Ride([], bufs, [], [(n,), (n,)], start, finish)


def _all_reduce_small(pack):
    rows = pack.shape[0]
    n_dev = 2 * N_CHIPS

    def body(x_ref, o_ref, land, send, recv):
        x, y, c, p, _ = _position()
        me = 2 * p + c
        land[me] = x_ref[...]
        peers = [(dx, dy, dc) for dx in range(2) for dy in range(2) for dc in range(2) if (dx, dy, dc) != (0, 0, 0)]
        for j, (dx, dy, dc) in enumerate(peers):
            _remote(land.at[me], land.at[me], send.at[j], recv.at[j], (x ^ dx, y ^ dy, c ^ dc)).start()
        for j, (dx, dy, dc) in enumerate(peers):
            src = 4 * (x ^ dx) + 2 * (y ^ dy) + (c ^ dc)
            _remote(land.at[me], land.at[me], send.at[j], recv.at[j], (x ^ dx, y ^ dy, c ^ dc)).wait_send()
            _remote(land.at[src], land.at[src], send.at[j], recv.at[j], (x ^ dx, y ^ dy, c ^ dc)).wait_recv()
        acc = land[0]
        for dev in range(1, n_dev):
            acc = acc + land[dev]
        o_ref[...] = acc

    return pl.pallas_call(
        body, out_shape=SDS((rows, LANES), F32),
        scratch_shapes=[pltpu.VMEM((n_dev, rows, LANES), F32), pltpu.SemaphoreType.DMA((n_dev - 1,)),
                        pltpu.SemaphoreType.DMA((n_dev - 1,))],
        name="all_reduce_small", compiler_params=_params())(pack)


def _add_own_half(name, full, recv, out_dtype):
    n4, _, h, cols = full.shape

    def body(s_ref, a_ref, b_ref, o_ref, own_ref):
        v = (a_ref[...].astype(F32) + b_ref[...].astype(F32)).astype(out_dtype)
        o_ref[...] = v

        @pl.when(pl.program_id(0) == s_ref[1])
        def _():
            own_ref[...] = v

    return pl.pallas_call(
        body,
        grid_spec=pltpu.PrefetchScalarGridSpec(
            num_scalar_prefetch=1, grid=(n4,),
            in_specs=[pl.BlockSpec((None, None, h, cols), lambda q, s: (q, s[0], 0, 0)),
                      pl.BlockSpec((None, h, cols), lambda q, s: (q, 0, 0))],
            out_specs=[pl.BlockSpec((None, h, cols), lambda q, s: (q, 0, 0)),
                       pl.BlockSpec((None, h, cols), lambda q, s: (s[1], 0, 0))]),
        out_shape=[SDS((n4, h, cols), out_dtype)] * 2, name=name, compiler_params=_params())(_mesh_scalars(), full, recv)


def _sum_chips(name, parts):
    n4, h, cols = parts.shape
    th = h // 4 if h % 64 == 0 else h

    def body(s_ref, a_ref, o_ref):
        acc = a_ref[0].astype(F32)
        for q in range(1, n4):
            acc = acc + a_ref[q].astype(F32)
        o_ref[...] = acc

    return pl.pallas_call(
        body,
        grid_spec=pltpu.PrefetchScalarGridSpec(
            num_scalar_prefetch=1, grid=(h // th,),
            in_specs=[pl.BlockSpec((n4, th, cols), lambda i, s: (0, i, 0))],
            out_specs=pl.BlockSpec((None, th, cols), lambda i, s: (s[0], i, 0))),
        out_shape=SDS((2, h, cols), F32), name=name, compiler_params=_params())(_mesh_scalars(), parts)


def _adamw_math(w, g, m, v):
    m2 = ADAM_B1 * m + (1.0 - ADAM_B1) * g
    v2 = ADAM_B2 * v + (1.0 - ADAM_B2) * (g * g)
    m_hat = m2 / (1.0 - ADAM_B1 ** ADAM_STEP)
    v_hat = v2 / (1.0 - ADAM_B2 ** ADAM_STEP)
    delta = -ADAM_LR * (m_hat / (jnp.sqrt(v_hat) + ADAM_EPS) + ADAM_WD * w)
    return delta, m2, v2


def _row_tile(rows, cols):
    cap = max(8, (1 << 18) // cols)
    best = 8
    for cand in range(8, min(rows, cap) + 1, 8):
        if rows % cand == 0:
            best = cand
    return best


def _adamw_big(name, w, g_layers, m, v):
    layers, rows, cols = w.shape
    tr = _row_tile(rows, cols)

    def body(w_ref, m_ref, v_ref, *rest):
        g_refs, (g_o, d_o, m_o, v_o) = rest[:layers], rest[layers:]
        gv = g_refs[0][...]
        for layer in range(1, layers):
            gv = jnp.where(pl.program_id(0) == layer, g_refs[layer][...], gv)
        d, mm, vv = _adamw_math(w_ref[...], gv, m_ref[...], v_ref[...])
        g_o[...] = gv
        d_o[...] = d
        m_o[...] = mm
        v_o[...] = vv

    blk = pl.BlockSpec((None, tr, cols), lambda l, i: (l, i, 0))
    g_blk = pl.BlockSpec((tr, cols), lambda l, i: (i, 0))
    return tuple(pl.pallas_call(
        body, grid=(layers, rows // tr), in_specs=[blk] * 3 + [g_blk] * layers, out_specs=[blk] * 4,
        out_shape=[SDS((layers, rows, cols), F32)] * 4, name=name,
        compiler_params=_params())(w, m, v, *[g.reshape(rows, cols) for g in g_layers]))


def _adamw_small(ws, gs, ms, vs):
    n = len(ws)
    flat = []
    for group in (ws, gs, ms, vs):
        flat += [a.reshape(-1, a.shape[-1]) for a in group]

    def body(*refs):
        w_r, g_r, m_r, v_r = refs[:n], refs[n:2 * n], refs[2 * n:3 * n], refs[3 * n:4 * n]
        d_o, m_o, v_o = refs[4 * n:5 * n], refs[5 * n:6 * n], refs[6 * n:7 * n]
        for j in range(n):
            d, mm, vv = _adamw_math(w_r[j][...], g_r[j][...], m_r[j][...], v_r[j][...])
            d_o[j][...] = d
            m_o[j][...] = mm
            v_o[j][...] = vv

    shapes = [SDS(a.shape, F32) for a in flat[:n]]
    outs = pl.pallas_call(body, out_shape=shapes * 3, name="adamw_small", compiler_params=_params())(*flat)
    res = []
    for k in range(3):
        res.append([outs[k * n + j].reshape(ws[j].shape) for j in range(n)])
    return res


BIG = ("ab_w_in", "ab_w_out", "cd_w_in", "cd_w_out", "ffn_w_gate", "ffn_w_up", "ffn_w_down")
V_BLOCK = (2 * A_WIDTH + QK_COLS) // B_WIDTH


def _pad_rows(a, rows):
    return jnp.pad(a, ((0, rows - a.shape[0]), (0, 0)))


A_IN, A_OUT, C_IN, C_OUT = ("ab_w_in", 0), ("ab_w_out", 0), ("cd_w_in", 0), ("cd_w_out", 0)
G0, U0, D0 = ("ffn_w_gate", 0), ("ffn_w_up", 0), ("ffn_w_down", 0)
G1, U1, D1 = ("ffn_w_gate", 1), ("ffn_w_up", 1), ("ffn_w_down", 1)
UNITS = (A_IN, A_OUT, G0, U0, D0, C_IN, C_OUT, G1, U1, D1)
ROWS_MINOR = ("ffn_w_gate", "ffn_w_up")
SMALL_SHARDED = ("small", 0)
REPLICATED_UNIT = ("replicated", 0)


class _Exchange:
    def __init__(self, enabled):
        self.enabled = enabled
        self.w, self.grad, self.recv, self.half, self.land, self.done = {}, {}, {}, {}, {}, {}

    def full(self, unit):
        b = self.w[unit]
        return b.reshape(N_CHIPS, 1, 2 * b.shape[2], b.shape[3])

    def _ride(self, phases):
        rides, sinks = [], []
        for kind, units in phases:
            if kind == "send":
                rides.append(_ride_gather_send([self.w[u] for u in units]))
                sinks.append(self.w)
            elif kind == "pass":
                rides.append(_ride_gather_pass([self.w[u] for u in units]))
                sinks.append(self.w)
            elif kind == "swap":
                rides.append(_ride_swap([self.grad[u] for u in units]))
                sinks.append(self.recv)
            elif kind == "scatter":
                rides.append(_ride_scatter([self.half[u] for u in units], [self.land[u] for u in units]))
                sinks.append(self.land)
            else:
                rides.append(_ride_join([self.done[u] for u in units]))
                sinks.append(self.done)
        ride = functools.reduce(_ride_both, rides)

        def settle(res):
            n_bufs = sum(len(r.bufs) for r in rides)
            bufs, new = list(res[:n_bufs]), list(res[n_bufs:])
            for r, sink, (_, units) in zip(rides, sinks, phases):
                vals = [bufs.pop(0) for _ in r.bufs] + [new.pop(0) for _ in r.new_outs]
                for u, v in zip(units, vals):
                    sink[u] = v

        return ride, settle

    def run(self, fn, *args, phases=(), **kw):
        if not self.enabled or not phases:
            return fn(*args, **kw)
        ride, settle = self._ride(phases)
        out, res = fn(*args, ride=ride, **kw)
        settle(res)
        return out

    def alone(self, name, phases):
        if self.enabled:
            ride, settle = self._ride(phases)
            settle(_run_ride(name, ride))

    def pair_sum(self, units):
        if self.enabled:
            for u in units:
                dtype = F32 if u in (SMALL_SHARDED, REPLICATED_UNIT) else BF16
                self.half[u], self.land[u] = _add_own_half(f"pair_sum_{u[0]}_{u[1]}", self.grad[u], self.recv[u], dtype)

    def chip_sum(self, units):
        if self.enabled:
            for u in units:
                self.done[u] = _sum_chips(f"chip_sum_{u[0]}_{u[1]}", self.land[u])


def _local_step(x, target, ex, sp):
    t, d = x.shape
    tabs = _rope_tables(t)
    gains = jnp.concatenate([jnp.tile(sp["q_norm_g"][g], HEAD_DIM // 8) for g in range(N_DIL)]
                            + [jnp.tile(sp["k_norm_g"][g], HEAD_DIM // 8) for g in range(N_DIL)]).reshape(1, QK_COLS)
    bias_t = sp["sgu_bias"].T
    cw = _pad_rows(sp["conv_c_w"], 32)
    dw = _pad_rows(sp["conv_d_w"], 8)
    cb, clg, clb = (sp[k].reshape(1, C_WIDTH) for k in ("conv_c_b", "c_ln_g", "c_ln_b"))
    slg, slb = sp["sgu_norm_g"].reshape(1, A_WIDTH), sp["sgu_norm_b"].reshape(1, A_WIDTH)
    g_ab, g_cd = sp["ab_norm_g"].reshape(1, d), sp["cd_norm_g"].reshape(1, d)
    g_f0, g_f1 = sp["ffn_norm_g"][0:1], sp["ffn_norm_g"][1:2]
    run = ex.run

    def w2d(unit):
        return ex.full(unit).reshape(-1, d)

    h0 = _rms_fwd("rms_ab", x, g_ab)
    proj = run(_proj_in, "proj_ab", h0, ex.full(A_IN), 0, phases=[("send", [A_OUT, G0])])
    a_out = _mixer_a_fwd(proj, slg, slb, sp["sgu_w"], bias_t)
    qk, q1, q2, k1, k2 = run(_qk_fwd, proj, gains, tabs, phases=[("pass", [A_OUT, G0]), ("send", [U0])])
    regrouped_qk = {1: (q1, k1), 2: (q2, k2)}
    fwd_phases = ([("pass", [U0]), ("send", [D0])], [("pass", [D0]), ("send", [C_IN])],
                  [("pass", [C_IN]), ("send", [C_OUT])])
    qkv, o_list, l_list = [], [], []
    for g, rate in enumerate(DIL_RATES):
        if rate == 1:
            qk3, proj3 = qk.reshape(1, t, QK_COLS), proj.reshape(1, t, AB_IN)
            q, k, v = (qk3, g), (qk3, N_DIL + g), (proj3, V_BLOCK + g)
        else:
            vp, = _permute(f"regroup_v_{g}", [(proj, V_BLOCK + g)], rate)
            q, k, v = (regrouped_qk[g][0], 0), (regrouped_qk[g][1], 0), (vp, 0)
        qkv.append((q, k, v))
        o, l = run(_attn_fwd, f"attn_fwd_{g}", q, k, v, phases=fwd_phases[g])
        if rate == 1:
            o, l = o.reshape(t, B_WIDTH), l.reshape(t, B_WIDTH)
        o_list.append(o)
        l_list.append(l)
    cat, lse_tot, lse_1, lse_2 = _attn_merge(a_out, o_list, l_list)
    x1, hf0 = _proj_out("out_ab", cat, w2d(A_OUT), x, g_next=g_f0)
    fgate0, fup0, act0 = run(_ffn_in, "ffn_in_0", hf0, ex.full(G0), ex.full(U0), 0,
                           phases=[("pass", [C_OUT]), ("send", [G1, U1])])
    x2, h1 = run(_ffn_out, "ffn_out_0", act0, ex.full(D0), 0, x1, g_next=g_cd, phases=[("pass", [G1, U1]), ("send", [D1])])
    projcd = run(_proj_in, "proj_cd", h1, ex.full(C_IN), 0, phases=[("pass", [D1])])
    cat2, c1 = _mixer_cd_fwd(projcd, cw, cb, clg, clb, dw)
    x3, hf1 = _proj_out("out_cd", cat2, w2d(C_OUT), x2, g_next=g_f1)
    fgate1, fup1, act1 = _ffn_in("ffn_in_1", hf1, ex.full(G1), ex.full(U1), 0)
    dy, loss_acc, dy_b = _ffn_out("ffn_out_1", act1, ex.full(D1), 0, x3, target=target)
    loss = 0.5 * loss_acc[0, 0] / d

    late = [D1, G1, U1]
    dgate, dup = _ffn_dact("ffn_dact_1", dy_b, ex.full(D1), 0, fgate1, fup1)
    ex.grad[D1] = _wgrad_row_sharded("wgrad_down_1", act1, dy_b, True)
    ex.grad[G1] = _wgrad_row_sharded("wgrad_gate_1", dgate, hf1, True)
    ex.grad[U1] = _wgrad_row_sharded("wgrad_up_1", dup, hf1, True)
    g3, d_f1, g3_b = run(_dgrad_cols, "dgrad_ffn_1", [dgate, dup], [ex.full(G1), ex.full(U1)], 0, True, x3, g_f1, dy,
                         w_rows=True, phases=[("swap", late)])
    ex.pair_sum(late)

    dcat2 = _dgrad_rows("dgrad_out_cd", g3_b, w2d(C_OUT))
    ex.grad[C_OUT] = _wgrad_row_sharded("wgrad_out_cd", cat2, g3_b, False)
    dprojcd, d_cw, d_cb, d_clg, d_clb, d_dw = run(_mixer_cd_bwd, projcd, dcat2, c1, cw, clg, clb, dw, phases=[("scatter", late)])
    ex.chip_sum(late)
    ex.grad[C_IN] = _wgrad_col_sharded("wgrad_in_cd", h1, [dprojcd], False)[0]
    g2, d_cdn, g2_b = run(_dgrad_cols, "dgrad_in_cd", [dprojcd], [ex.full(C_IN)], 0, False, x2, g_cd, g3,
                          phases=[("join", late), ("swap", [C_OUT, C_IN])])
    ex.pair_sum([C_OUT, C_IN])

    dgate, dup = run(_ffn_dact, "ffn_dact_0", g2_b, ex.full(D0), 0, fgate0, fup0, phases=[("scatter", [C_OUT, C_IN])])
    ex.chip_sum([C_OUT, C_IN])
    ex.grad[D0] = _wgrad_row_sharded("wgrad_down_0", act0, g2_b, True)
    ex.grad[G0] = _wgrad_row_sharded("wgrad_gate_0", dgate, hf0, True)
    ex.grad[U0] = _wgrad_row_sharded("wgrad_up_0", dup, hf0, True)
    small = {"cd_norm_g": d_cdn, "conv_c_w": d_cw[:C_KERNEL], "conv_c_b": d_cb, "c_ln_g": d_clg, "c_ln_b": d_clb,
             "conv_d_w": d_dw[:D_KERNEL]}
    ex.grad[SMALL_SHARDED] = _split_full_small(small).reshape(N_CHIPS, 2, SHARDED_ROWS // 2, LANES)
    mid = [D0, G0, U0, SMALL_SHARDED]
    g1, d_f0, g1_b = run(_dgrad_cols, "dgrad_ffn_0", [dgate, dup], [ex.full(G0), ex.full(U0)], 0, True, x1, g_f0, g2,
                         w_rows=True, phases=[("join", [C_OUT, C_IN]), ("swap", mid)])
    ex.pair_sum(mid)

    dcat = _dgrad_rows("dgrad_out_ab", g1_b, w2d(A_OUT))
    ex.grad[A_OUT] = _wgrad_row_sharded("wgrad_out_ab", cat, g1_b, False)
    d_a, d_sw, d_sbt, d_slg, d_slb = _mixer_a_bwd(proj, dcat, slg, slb, sp["sgu_w"], bias_t)
    early = {"sgu_norm_g": d_slg, "sgu_norm_b": d_slb, "sgu_w": d_sw, "sgu_bias": d_sbt.T}
    ex.grad[REPLICATED_UNIT] = jnp.broadcast_to(
        _pack_replicated(early, REPLICATED_EARLY, REPLICATED_EARLY_ROWS).reshape(2, REPLICATED_EARLY_ROWS // 2, LANES),
        (N_CHIPS, 2, REPLICATED_EARLY_ROWS // 2, LANES))
    last = [A_OUT, REPLICATED_UNIT]
    dbb, dd, db_1, dd_1, db_2, dd_2 = _attn_bwd_prep(dcat, cat)
    regrouped_bwd = {1: (db_1, lse_1, dd_1), 2: (db_2, lse_2, dd_2)}
    bwd_phases = ([("scatter", [D0, SMALL_SHARDED])],
                  [("scatter", [G0]), ("join", [D0, SMALL_SHARDED]), ("swap", last)],
                  [("scatter", [U0]), ("join", [G0])])
    dqs, dks, dvs = [], [], []
    for g, rate in enumerate(DIL_RATES):
        q, k, v = qkv[g]
        if rate == 1:
            db3, l3, dd3 = (a.reshape(1, t, B_WIDTH) for a in (dbb, lse_tot, dd))
        else:
            db3, l3, dd3 = regrouped_bwd[g]
        if g == 1:
            ex.chip_sum([D0, SMALL_SHARDED])
        elif g == 2:
            ex.chip_sum([G0])
            ex.pair_sum(last)
        dq, dk, dv = run(_attn_bwd, f"attn_bwd_{g}", q, k, v, db3, l3, dd3, phases=bwd_phases[g])
        if rate == 1:
            dq, dk, dv = (a.reshape(t, B_WIDTH) for a in (dq, dk, dv))
        dqs.append(dq)
        dks.append(dk)
        dvs.append(dv)
    ex.chip_sum([U0])
    dproj, d_gains = run(_dproj_assemble, proj, d_a, dqs, dks, dvs, gains, tabs, phases=[("scatter", last), ("join", [U0])])
    ex.chip_sum(last)
    d_gains = _fold_heads(d_gains)[0].reshape(2, N_DIL, B_WIDTH)[:, :, :HEAD_DIM]
    ex.grad[A_IN] = _wgrad_col_sharded("wgrad_in_ab", h0, [dproj], False)[0]
    ex.alone("swap_last", [("swap", [A_IN])])
    ex.pair_sum([A_IN])
    gx, d_abn = run(_dgrad_cols, "dgrad_in_ab", [dproj], [ex.full(A_IN)], 0, False, x, g_ab, g1, bf16_copy=False,
                    phases=[("join", last), ("scatter", [A_IN])])
    ex.chip_sum([A_IN])
    ex.alone("join_last", [("join", [A_IN])])

    small.update({
        "ab_norm_g": d_abn, "sgu_norm_g": d_slg, "sgu_norm_b": d_slb, "sgu_w": d_sw, "sgu_bias": d_sbt.T,
        "q_norm_g": d_gains[0], "k_norm_g": d_gains[1], "ffn_norm_g": jnp.concatenate([d_f0, d_f1], axis=0),
    })
    return loss, gx, small


SHARDED_SMALL = ("cd_norm_g", "conv_c_w", "conv_c_b", "c_ln_g", "c_ln_b", "conv_d_w")
SHARDED_ROWS = 48
REPLICATED_EARLY = ("sgu_norm_g", "sgu_norm_b", "sgu_w", "sgu_bias")
REPLICATED_EARLY_ROWS = 528
REPLICATED_LATE = ("ab_norm_g", "q_norm_g", "k_norm_g", "ffn_norm_g", "loss")
REPLICATED_LATE_ROWS = 32
REPLICATED_SMALL = REPLICATED_EARLY + REPLICATED_LATE[:-1]


def _pack_sharded(parts):
    rows = [parts[k].reshape(-1, LANES) for k in SHARDED_SMALL]
    return _pad_rows(jnp.concatenate(rows, axis=0), SHARDED_ROWS)


def _split_full_small(small):
    per_chip = []
    for q in range(N_CHIPS):
        parts = {}
        for k in SHARDED_SMALL:
            a = small[k]
            a = a.reshape(-1, a.shape[-1])
            n = a.shape[-1] // N_CHIPS
            parts[k] = a[:, q * n:(q + 1) * n]
        per_chip.append(_pack_sharded(parts))
    return jnp.stack(per_chip)


def _unpack_sharded(pack, shapes):
    out, r = {}, 0
    for k in SHARDED_SMALL:
        n = math.prod(shapes[k]) // LANES
        out[k] = pack[r:r + n].reshape(shapes[k])
        r += n
    return out


def _gathered_small(packs, shapes):
    per_chip = [_unpack_sharded(packs[q], shapes) for q in range(N_CHIPS)]
    return {k: jnp.concatenate([pc[k] for pc in per_chip], axis=-1) for k in SHARDED_SMALL}


def _pack_replicated(small, names, total_rows):
    rows = []
    for k in names:
        a = small[k].reshape(-1)
        a = jnp.pad(a, (0, (-a.shape[0]) % LANES))
        rows.append(a.reshape(-1, LANES))
    return _pad_rows(jnp.concatenate(rows, axis=0), total_rows)


def _unpack_replicated(pack, shapes, names):
    out, r = {}, 0
    for k in names:
        size = math.prod(shapes[k])
        n = -(-size // LANES)
        out[k] = pack[r:r + n].reshape(-1)[:size].reshape(shapes[k])
        r += n
    return out


WEIGHT_ORDER = ("ab_norm_g", "ab_w_in", "sgu_norm_g", "sgu_norm_b", "sgu_w", "sgu_bias", "q_norm_g", "k_norm_g", "ab_w_out",
                "cd_norm_g", "cd_w_in", "conv_c_w", "conv_c_b", "c_ln_g", "c_ln_b", "conv_d_w", "cd_w_out", "ffn_norm_g",
                "ffn_w_gate", "ffn_w_up", "ffn_w_down")


def kernel(x, ab_norm_g, ab_w_in, sgu_norm_g, sgu_norm_b, sgu_w, sgu_bias, q_norm_g, k_norm_g, ab_w_out, cd_norm_g, cd_w_in, conv_c_w, conv_c_b, c_ln_g, c_ln_b, conv_d_w, cd_w_out, ffn_norm_g, ffn_w_gate, ffn_w_up, ffn_w_down, loss_target, m_ab_norm_g, m_ab_w_in, m_sgu_norm_g, m_sgu_norm_b, m_sgu_w, m_sgu_bias, m_q_norm_g, m_k_norm_g, m_ab_w_out, m_cd_norm_g, m_cd_w_in, m_conv_c_w, m_conv_c_b, m_c_ln_g, m_c_ln_b, m_conv_d_w, m_cd_w_out, m_ffn_norm_g, m_ffn_w_gate, m_ffn_w_up, m_ffn_w_down, v_ab_norm_g, v_ab_w_in, v_sgu_norm_g, v_sgu_norm_b, v_sgu_w, v_sgu_bias, v_q_norm_g, v_k_norm_g, v_ab_w_out, v_cd_norm_g, v_cd_w_in, v_conv_c_w, v_conv_c_b, v_c_ln_g, v_c_ln_b, v_conv_d_w, v_cd_w_out, v_ffn_norm_g, v_ffn_w_gate, v_ffn_w_up, v_ffn_w_down):
    args = dict(locals())
    ws = {k: args[k] for k in WEIGHT_ORDER}
    ms = {k: args["m_" + k] for k in WEIGHT_ORDER}
    vs = {k: args["v_" + k] for k in WEIGHT_ORDER}
    small_names = [k for k in WEIGHT_ORDER if k not in BIG]
    t, d = x.shape[1:]

    for group in (ws, ms, vs):
        for k in ROWS_MINOR:
            group[k] = jnp.swapaxes(group[k], 1, 2)
    ex = _Exchange(enabled=True)
    for name, layer in UNITS:
        ex.w[(name, layer)] = _stage_own(f"stage_{name}_{layer}", ws[name], layer, BF16)
    own_small = _pack_sharded({k: ws[k][0] for k in SHARDED_SMALL})
    ex.w[SMALL_SHARDED] = _stage_own("stage_small", own_small[None], 0, F32)
    ex.alone("gather_first", [("send", [A_IN, SMALL_SHARDED])])
    ex.alone("gather_first_pass", [("pass", [A_IN, SMALL_SHARDED])])
    sp = _gathered_small(ex.w[SMALL_SHARDED].reshape(N_CHIPS, SHARDED_ROWS, LANES), {k: ws[k].shape[1:] for k in SHARDED_SMALL})
    for k in REPLICATED_SMALL:
        sp[k] = ws[k] if k == "ffn_norm_g" else ws[k][0]

    loss, grad_x, g_small = _local_step(x.reshape(t, d), loss_target.reshape(t, d), ex, sp)

    shapes = {k: ws[k].shape for k in REPLICATED_SMALL}
    shapes["loss"] = (1,)
    g_small["loss"] = loss
    late = _all_reduce_small(_pack_replicated(g_small, REPLICATED_LATE, REPLICATED_LATE_ROWS))
    grad = _unpack_sharded(ex.done[SMALL_SHARDED].reshape(SHARDED_ROWS, LANES), {k: ws[k].shape for k in SHARDED_SMALL})
    grad.update(_unpack_replicated(ex.done[REPLICATED_UNIT].reshape(REPLICATED_EARLY_ROWS, LANES), shapes, REPLICATED_EARLY))
    grad.update(_unpack_replicated(late, shapes, REPLICATED_LATE))
    loss = grad.pop("loss")[0]

    delta, new_m, new_v = {}, {}, {}
    for k in BIG:
        g_layers = [ex.done[(k, layer)] for layer in range(ws[k].shape[0])]
        outs = _adamw_big("adamw_" + k, ws[k], g_layers, ms[k], vs[k])
        if k in ROWS_MINOR:
            outs = [jnp.swapaxes(o, 1, 2) for o in outs]
        grad[k], delta[k], new_m[k], new_v[k] = outs
    d_s, m_s, v_s = _adamw_small([ws[k] for k in small_names], [grad[k] for k in small_names],
                                 [ms[k] for k in small_names], [vs[k] for k in small_names])
    for j, k in enumerate(small_names):
        delta[k], new_m[k], new_v[k] = d_s[j], m_s[j], v_s[j]

    return (loss, grad_x[None], *[grad[k] for k in WEIGHT_ORDER], *[delta[k] for k in WEIGHT_ORDER],
            *[new_m[k] for k in WEIGHT_ORDER], *[new_v[k] for k in WEIGHT_ORDER])
```

```python
import functools
import math

import jax
import jax.numpy as jnp
from jax import lax
from jax.experimental import pallas as pl
from jax.experimental.pallas import tpu as pltpu

F32 = jnp.float32
BF16 = jnp.bfloat16
SDS = jax.ShapeDtypeStruct

N_CHIPS = 4
EPS = 1e-6
NEG_INF = -1e30
CHUNK = 128
A_GROUPS = 4
A_WIDTH = 512
N_DIL = 3
DIL_RATES = (1, 4, 16)
HEAD_DIM = 64
B_WIDTH = 512
ROPE_DIM = 16
ROPE_THETA = 500000.0
C_WIDTH = 512
C_KERNEL = 31
D_KERNEL = 3
HALO = 32
ATT_BLOCK = 128
LANES = 128

ADAM_LR = 0.001
ADAM_B1 = 0.9
ADAM_B2 = 0.999
ADAM_EPS = 1e-08
ADAM_WD = 0.01
ADAM_STEP = 10

VMEM_LIMIT = 56 * 1024 * 1024

NN = (((1,), (0,)), ((), ()))
NT = (((1,), (1,)), ((), ()))
TN = (((0,), (0,)), ((), ()))

TILES = {"proj_in": 1024, "proj_out": 1024, "ffn_in": 1024, "ffn_out": 512, "ffn_dact": 512, "dgrad_cols": 512,
         "dgrad_rows": 1024, "wgrad": 4096}


def _params(sem=None):
    return pltpu.CompilerParams(dimension_semantics=sem, vmem_limit_bytes=VMEM_LIMIT)


def _bf(v):
    return v if v.dtype == BF16 else v.astype(BF16)


def _dot(a, b, dims):
    return lax.dot_general(_bf(a), _bf(b), dims, preferred_element_type=F32)


def _dot_hi(a, b):
    return jnp.dot(a, b, precision=lax.Precision.HIGHEST, preferred_element_type=F32)


def _sigmoid(v):
    return 0.5 * jnp.tanh(0.5 * v) + 0.5


def _gelu(v):
    return 0.5 * v * (1.0 + lax.erf(v * (1.0 / math.sqrt(2.0))))


def _gelu_grad(v):
    cdf = 0.5 * (1.0 + lax.erf(v * (1.0 / math.sqrt(2.0))))
    return cdf + v * jnp.exp(-0.5 * v * v) * (1.0 / math.sqrt(2.0 * math.pi))


def _segment_mean_matrix(seg, scale=None):
    r = lax.broadcasted_iota(jnp.int32, (LANES, LANES), 0) // seg
    c = lax.broadcasted_iota(jnp.int32, (LANES, LANES), 1) // seg
    return jnp.where(r == c, (1.0 / seg) if scale is None else scale, 0.0).astype(BF16)


def _segment_dot(v, seg):
    hi = v.astype(BF16)
    lo = (v - hi.astype(F32)).astype(BF16)
    return jnp.dot(hi, seg, preferred_element_type=F32) + jnp.dot(lo, seg, preferred_element_type=F32)


MESH = pl.DeviceIdType.MESH
ANY = pl.BlockSpec(memory_space=pl.ANY)


def _position():
    x, y, c = lax.axis_index("x"), lax.axis_index("y"), lax.axis_index("c")
    others = [(1 - x, y), (x, 1 - y), (1 - x, 1 - y)]
    return x, y, c, 2 * x + y, others


class _Ride:
    def __init__(self, ins, bufs, new_outs, sem_shapes, start, finish):
        self.ins, self.bufs, self.new_outs, self.sem_shapes = list(ins), list(bufs), list(new_outs), list(sem_shapes)
        self.start, self.finish = start, finish


def _ride_both(a, b):
    na = (len(a.ins), len(a.bufs), len(a.new_outs), len(a.sem_shapes))

    def split(ins, bufs, new, sems):
        return ((ins[:na[0]], bufs[:na[1]], new[:na[2]], sems[:na[3]]), (ins[na[0]:], bufs[na[1]:], new[na[2]:], sems[na[3]:]))

    def start(*refs):
        ra, rb = split(*refs)
        a.start(*ra)
        b.start(*rb)

    def finish(*refs):
        ra, rb = split(*refs)
        a.finish(*ra)
        b.finish(*rb)

    return _Ride(a.ins + b.ins, a.bufs + b.bufs, a.new_outs + b.new_outs, a.sem_shapes + b.sem_shapes, start, finish)


def _call(body, *, grid, in_specs, out_specs, out_shape, operands, name, scratch_shapes=(), aliases=None, ride=None):
    if ride is None:
        return pl.pallas_call(body, grid=grid, in_specs=in_specs, out_specs=out_specs, out_shape=out_shape,
                              scratch_shapes=list(scratch_shapes), input_output_aliases=aliases or {}, name=name,
                              compiler_params=_params())(*operands)
    multi = isinstance(out_shape, (list, tuple))
    out_shapes = list(out_shape) if multi else [out_shape]
    o_specs = list(out_specs) if multi else [out_specs]
    n_in, n_out, n_scr = len(operands), len(out_shapes), len(scratch_shapes)
    n_ri, n_rb, n_rn = len(ride.ins), len(ride.bufs), len(ride.new_outs)

    def carrying(*refs):
        k = n_in
        r_ins = refs[k:k + n_ri]
        k += n_ri + n_rb
        outs = refs[k:k + n_out]
        k += n_out
        r_bufs = refs[k:k + n_rb]
        k += n_rb
        r_new = refs[k:k + n_rn]
        k += n_rn
        scratch = refs[k:k + n_scr]
        sems = refs[k + n_scr:]
        first, last = None, None
        for axis, size in enumerate(grid):
            pid = pl.program_id(axis)
            first = (pid == 0) if first is None else first & (pid == 0)
            last = (pid == size - 1) if last is None else last & (pid == size - 1)

        @pl.when(first)
        def _():
            ride.start(r_ins, r_bufs, r_new, sems)

        body(*refs[:n_in], *outs, *scratch)

        @pl.when(last)
        def _():
            ride.finish(r_ins, r_bufs, r_new, sems)

    all_aliases = dict(aliases or {})
    for j in range(n_rb):
        all_aliases[n_in + n_ri + j] = n_out + j
    res = pl.pallas_call(
        carrying, grid=grid, in_specs=list(in_specs) + [ANY] * (n_ri + n_rb), out_specs=o_specs + [ANY] * (n_rb + n_rn),
        out_shape=out_shapes + [SDS(b.shape, b.dtype) for b in ride.bufs] + ride.new_outs,
        scratch_shapes=list(scratch_shapes) + [pltpu.SemaphoreType.DMA(s) for s in ride.sem_shapes],
        input_output_aliases=all_aliases, name=name, compiler_params=_params())(*operands, *ride.ins, *ride.bufs)
    outs = res[:n_out]
    return (list(outs) if multi else outs[0]), list(res[n_out:])


def _run_ride(name, ride):
    n_ri, n_rb, n_rn = len(ride.ins), len(ride.bufs), len(ride.new_outs)

    def body(*refs):
        r_ins = refs[:n_ri]
        r_bufs = refs[n_ri + n_rb:n_ri + 2 * n_rb]
        r_new = refs[n_ri + 2 * n_rb:n_ri + 2 * n_rb + n_rn]
        sems = refs[n_ri + 2 * n_rb + n_rn:]
        ride.start(r_ins, r_bufs, r_new, sems)
        ride.finish(r_ins, r_bufs, r_new, sems)

    return list(pl.pallas_call(
        body, in_specs=[ANY] * (n_ri + n_rb), out_specs=[ANY] * (n_rb + n_rn),
        out_shape=[SDS(b.shape, b.dtype) for b in ride.bufs] + ride.new_outs,
        scratch_shapes=[pltpu.SemaphoreType.DMA(s) for s in ride.sem_shapes],
        input_output_aliases={n_ri + j: j for j in range(n_rb)}, name=name)(*ride.ins, *ride.bufs))


def _whole(ref, p):
    return ref[...]


def _slab(ref, p):
    return ref[p]


def _matmul(name, grid, pairs, extras, outs, dims, epi, *, slabs=1, n_acc=1, ride=None):
    n_pairs, n_ex, n_out = len(pairs), len(extras), len(outs)

    def body(*refs):
        ab = refs[:2 * n_pairs]
        ex = refs[2 * n_pairs:2 * n_pairs + n_ex]
        out_refs = refs[2 * n_pairs + n_ex:2 * n_pairs + n_ex + n_out]
        pids = tuple(pl.program_id(a) for a in range(len(grid)))
        parts = [None] * n_acc
        for p in range(slabs):
            for j, (_, _, a_pick, _, _, b_pick, acc) in enumerate(pairs):
                d = _dot(a_pick(ab[2 * j], p), b_pick(ab[2 * j + 1], p), dims)
                parts[acc] = d if parts[acc] is None else parts[acc] + d
        epi(parts, ex, out_refs, pids)

    operands, in_specs = [], []
    for a, a_spec, _, b, b_spec, _, _ in pairs:
        operands += [a, b]
        in_specs += [a_spec, b_spec]
    for e, e_spec in extras:
        operands.append(e)
        in_specs.append(e_spec)
    return _call(body, grid=grid, in_specs=in_specs, out_specs=[o[1] for o in outs], out_shape=[o[0] for o in outs],
                 operands=operands, name=name, ride=ride)


def _rms_rows(v, g):
    r = lax.rsqrt(jnp.mean(v * v, axis=-1, keepdims=True) + EPS)
    return v * r * g


def _rms_fwd(name, x, g):
    t, d = x.shape
    tm = 512

    def body(x_ref, g_ref, o_ref):
        o_ref[...] = _rms_rows(x_ref[...], g_ref[...]).astype(BF16)

    return pl.pallas_call(
        body, grid=(t // tm,),
        in_specs=[pl.BlockSpec((tm, d), lambda i: (i, 0)), pl.BlockSpec((1, d), lambda i: (0, 0))],
        out_specs=pl.BlockSpec((tm, d), lambda i: (i, 0)), out_shape=SDS((t, d), BF16), name=name,
        compiler_params=_params())(x, g)


def _epi_residual_norm(accs, ex, outs, pids):
    x_new = accs[0] + ex[0][...]
    outs[0][...] = x_new
    outs[1][...] = _rms_rows(x_new, ex[1][...]).astype(BF16)


def _epi_residual_loss(accs, ex, outs, pids):
    y = accs[0] + ex[0][...]
    err = y - ex[1][...]
    dy = err * (1.0 / err.shape[-1])
    outs[0][...] = dy
    outs[2][...] = dy.astype(BF16)

    @pl.when(pids[0] == 0)
    def _():
        outs[1][...] = jnp.zeros_like(outs[1])

    outs[1][...] += jnp.sum(err * err)


def _epi_rms_bwd(accs, ex, outs, pids):
    dh = accs[0]
    xv, g, res = ex[0][...], ex[1][...], ex[2][...]
    r = lax.rsqrt(jnp.mean(xv * xv, axis=-1, keepdims=True) + EPS)
    xh = xv * r
    dy = dh * g
    dx = res + r * (dy - xh * jnp.mean(dy * xh, axis=-1, keepdims=True))
    outs[0][...] = dx
    if len(outs) > 2:
        outs[2][...] = dx.astype(BF16)

    @pl.when(pids[0] == 0)
    def _():
        outs[1][...] = jnp.zeros_like(outs[1])

    outs[1][...] += jnp.sum(dh * xh, axis=0, keepdims=True)


def _row_spec(tm, d):
    return pl.BlockSpec((tm, d), lambda i, *_: (i, 0))


def _const_spec(shape):
    nd = len(shape)
    return pl.BlockSpec(shape, lambda *_: (0,) * nd)


def _proj_in(name, h, w, layer, ride=None):
    t, d = h.shape
    n4 = w.shape[-1]
    tm = TILES["proj_in"]

    def epi(accs, ex, outs, pids):
        outs[0][...] = accs[0].astype(BF16)

    res = _matmul(
        name, (N_CHIPS, t // tm),
        [(h, pl.BlockSpec((tm, d), lambda p, i: (i, 0)), _whole,
          w, pl.BlockSpec((None, None, d, n4), lambda p, i: (p, layer, 0, 0)), _whole, 0)],
        [], [(SDS((t, N_CHIPS * n4), BF16), pl.BlockSpec((tm, n4), lambda p, i: (i, p)))],
        NN, epi, ride=ride)
    return res[0] if ride is None else (res[0][0], res[1])


def _proj_out(name, a, w, x, g_next=None, target=None):
    t, k = a.shape
    d = w.shape[-1]
    tm = TILES["proj_out"]
    if target is None:
        extras = [(x, _row_spec(tm, d)), (g_next, _const_spec((1, d)))]
        outs = [(SDS((t, d), F32), _row_spec(tm, d)), (SDS((t, d), BF16), _row_spec(tm, d))]
        epi = _epi_residual_norm
    else:
        extras = [(x, _row_spec(tm, d)), (target, _row_spec(tm, d))]
        outs = [(SDS((t, d), F32), _row_spec(tm, d)), (SDS((8, LANES), F32), _const_spec((8, LANES))),
                (SDS((t, d), BF16), _row_spec(tm, d))]
        epi = _epi_residual_loss
    return _matmul(name, (t // tm,), [(a, _row_spec(tm, k), _whole, w, _const_spec((k, d)), _whole, 0)], extras, outs, NN, epi)


def _ffn_in(name, h, wg, wu, layer, ride=None):
    t, d = h.shape
    n4 = wg.shape[-2]
    tm = TILES["ffn_in"]

    def epi(accs, ex, outs, pids):
        gate, up = accs
        s = _sigmoid(gate)
        silu = gate * s
        outs[0][...] = (up * (s + silu - silu * s)).astype(BF16)
        outs[1][...] = silu.astype(BF16)
        outs[2][...] = (silu * up).astype(BF16)

    w_spec = pl.BlockSpec((None, None, n4, d), lambda p, i: (p, layer, 0, 0))
    h_spec = pl.BlockSpec((tm, d), lambda p, i: (i, 0))
    o = (SDS((N_CHIPS, t, n4), BF16), pl.BlockSpec((None, tm, n4), lambda p, i: (p, i, 0)))
    return _matmul(name, (N_CHIPS, t // tm),
                   [(h, h_spec, _whole, wg, w_spec, _whole, 0), (h, h_spec, _whole, wu, w_spec, _whole, 1)], [],
                   [o, o, o], NT, epi, n_acc=2, ride=ride)


def _ffn_out(name, act, wd, layer, x, g_next=None, target=None, ride=None):
    _, t, n4 = act.shape
    d = wd.shape[-1]
    tm = TILES["ffn_out"]
    xs = _row_spec(tm, d)
    if target is None:
        extras = [(x, xs), (g_next, _const_spec((1, d)))]
        outs = [(SDS((t, d), F32), xs), (SDS((t, d), BF16), xs)]
        epi = _epi_residual_norm
    else:
        extras = [(x, xs), (target, xs)]
        outs = [(SDS((t, d), F32), xs), (SDS((8, LANES), F32), _const_spec((8, LANES))), (SDS((t, d), BF16), xs)]
        epi = _epi_residual_loss
    return _matmul(
        name, (t // tm,),
        [(act, pl.BlockSpec((N_CHIPS, tm, n4), lambda i: (0, i, 0)), _slab,
          wd, pl.BlockSpec((N_CHIPS, None, n4, d), lambda i: (0, layer, 0, 0)), _slab, 0)],
        extras, outs, NN, epi, slabs=N_CHIPS, ride=ride)


def _ffn_dact(name, g, wd, layer, gate, up, ride=None):
    t, d = g.shape
    n4 = wd.shape[-2]
    tm = TILES["ffn_dact"]

    def body(g_ref, w_ref, gate_ref, up_ref, dgate_ref, dup_ref):
        gv = g_ref[...]
        for p in range(N_CHIPS):
            dact = _dot(gv, w_ref[p], NT)
            dgate_ref[p] = (dact * gate_ref[p].astype(F32)).astype(BF16)
            dup_ref[p] = (dact * up_ref[p].astype(F32)).astype(BF16)

    blk = pl.BlockSpec((N_CHIPS, tm, n4), lambda i: (0, i, 0))
    return _call(
        body, grid=(t // tm,),
        in_specs=[_row_spec(tm, d), pl.BlockSpec((N_CHIPS, None, n4, d), lambda i: (0, layer, 0, 0)), blk, blk],
        out_specs=[blk, blk], out_shape=[SDS((N_CHIPS, t, n4), BF16)] * 2, operands=[g, wd, gate, up], name=name, ride=ride)


def _copy_epi(accs, ex, outs, pids):
    for a, o in zip(accs, outs):
        o[...] = a.astype(o.dtype)


def _dgrad_cols(name, dz_list, w_list, layer, three_d, x, g, res, bf16_copy=True, w_rows=False, ride=None):
    t, d = x.shape
    n4 = w_list[0].shape[-2 if w_rows else -1]
    tm = TILES["dgrad_cols"]
    if three_d:
        zs, z_pick = pl.BlockSpec((N_CHIPS, tm, n4), lambda i: (0, i, 0)), _slab
    else:
        zs, z_pick = _row_spec(tm, N_CHIPS * n4), (lambda ref, p: ref[:, p * n4:(p + 1) * n4])
    ws = pl.BlockSpec((N_CHIPS, None) + ((n4, d) if w_rows else (d, n4)), lambda i: (0, layer, 0, 0))
    xs = _row_spec(tm, d)
    return _matmul(
        name, (t // tm,), [(dz, zs, z_pick, w, ws, _slab, 0) for dz, w in zip(dz_list, w_list)],
        [(x, xs), (g, _const_spec((1, d))), (res, xs)],
        [(SDS((t, d), F32), xs), (SDS((1, d), F32), _const_spec((1, d)))] + ([(SDS((t, d), BF16), xs)] if bf16_copy else []),
        NN if w_rows else NT, _epi_rms_bwd, slabs=N_CHIPS, ride=ride)


def _dgrad_rows(name, g, w):
    t, d = g.shape
    k = w.shape[0]
    tm = TILES["dgrad_rows"]
    return _matmul(name, (t // tm,), [(g, _row_spec(tm, d), _whole, w, _const_spec((k, d)), _whole, 0)], [],
                   [(SDS((t, k), F32), _row_spec(tm, k))], NT, _copy_epi)[0]


A_TILE = 256


def _a_common(p_ref, lg_ref, lb_ref):
    pv = p_ref[...].astype(F32)
    a = _gelu(pv)
    u, v = a[:, :A_WIDTH], a[:, A_WIDTH:]
    vc = v - jnp.mean(v, axis=-1, keepdims=True)
    rs = lax.rsqrt(jnp.mean(vc * vc, axis=-1, keepdims=True) + EPS)
    vhat = vc * rs
    vn = vhat * lg_ref[...] + lb_ref[...]
    return pv, u, vhat, rs, vn.astype(BF16)


def _tril_weights(w_ref, g):
    r = lax.broadcasted_iota(jnp.int32, (CHUNK, CHUNK), 0)
    c = lax.broadcasted_iota(jnp.int32, (CHUNK, CHUNK), 1)
    return jnp.where(c <= r, w_ref[g], 0.0).astype(BF16), c <= r


def _mixer_a_fwd(proj, lg, lb, w, bias_t):
    t = proj.shape[0]

    def body(p_ref, lg_ref, lb_ref, w_ref, bt_ref, o_ref):
        _, u, _, _, vnb = _a_common(p_ref, lg_ref, lb_ref)
        for g in range(A_GROUPS):
            wt, _ = _tril_weights(w_ref, g)
            cs = slice(g * CHUNK, (g + 1) * CHUNK)
            for ch in range(A_TILE // CHUNK):
                rs_ = slice(ch * CHUNK, (ch + 1) * CHUNK)
                mixed = _dot(wt, vnb[rs_, cs], NN) + bt_ref[:, g:g + 1]
                o_ref[rs_, cs] = (u[rs_, cs] * mixed).astype(BF16)

    return pl.pallas_call(
        body, grid=(t // A_TILE,),
        in_specs=[pl.BlockSpec((A_TILE, 2 * A_WIDTH), lambda i: (i, 0)), _const_spec((1, A_WIDTH)),
                  _const_spec((1, A_WIDTH)), _const_spec((A_GROUPS, CHUNK, CHUNK)), _const_spec((CHUNK, A_GROUPS))],
        out_specs=pl.BlockSpec((A_TILE, A_WIDTH), lambda i: (i, 0)), out_shape=SDS((t, A_WIDTH), BF16),
        name="mixer_a_fwd", compiler_params=_params())(proj, lg, lb, w, bias_t)


def _mixer_a_bwd(proj, dcat, lg, lb, w, bias_t):
    t = proj.shape[0]

    def body(p_ref, da_ref, lg_ref, lb_ref, w_ref, bt_ref, dp_ref, dw_ref, dbt_ref, dlg_ref, dlb_ref, du_scr, dvn_scr):
        @pl.when(pl.program_id(0) == 0)
        def _():
            dw_ref[...] = jnp.zeros_like(dw_ref)
            dbt_ref[...] = jnp.zeros_like(dbt_ref)
            dlg_ref[...] = jnp.zeros_like(dlg_ref)
            dlb_ref[...] = jnp.zeros_like(dlb_ref)

        pv, u, vhat, rs, vnb = _a_common(p_ref, lg_ref, lb_ref)
        da = da_ref[...]
        for g in range(A_GROUPS):
            wt, keep = _tril_weights(w_ref, g)
            cs = slice(g * CHUNK, (g + 1) * CHUNK)
            for ch in range(A_TILE // CHUNK):
                rs_ = slice(ch * CHUNK, (ch + 1) * CHUNK)
                vg = vnb[rs_, cs]
                mixed = _dot(wt, vg, NN) + bt_ref[:, g:g + 1]
                du_scr[rs_, cs] = da[rs_, cs] * mixed
                dmx = da[rs_, cs] * u[rs_, cs]
                dw_ref[g] += jnp.where(keep, _dot(dmx, vg, NT), 0.0)
                dvn_scr[rs_, cs] = _dot(wt, dmx, TN)
                dbt_ref[:, g:g + 1] += jnp.sum(dmx, axis=1, keepdims=True)
        dvn = dvn_scr[...]
        dlg_ref[...] += jnp.sum(dvn * vhat, axis=0, keepdims=True)
        dlb_ref[...] += jnp.sum(dvn, axis=0, keepdims=True)
        dvh = dvn * lg_ref[...]
        dv = rs * (dvh - jnp.mean(dvh, axis=-1, keepdims=True) - vhat * jnp.mean(dvh * vhat, axis=-1, keepdims=True))
        gp = _gelu_grad(pv)
        dp_ref[:, :A_WIDTH] = (du_scr[...] * gp[:, :A_WIDTH]).astype(BF16)
        dp_ref[:, A_WIDTH:] = (dv * gp[:, A_WIDTH:]).astype(BF16)

    return pl.pallas_call(
        body, grid=(t // A_TILE,),
        in_specs=[pl.BlockSpec((A_TILE, 2 * A_WIDTH), lambda i: (i, 0)), pl.BlockSpec((A_TILE, A_WIDTH), lambda i: (i, 0)),
                  _const_spec((1, A_WIDTH)), _const_spec((1, A_WIDTH)), _const_spec((A_GROUPS, CHUNK, CHUNK)),
                  _const_spec((CHUNK, A_GROUPS))],
        out_specs=[pl.BlockSpec((A_TILE, 2 * A_WIDTH), lambda i: (i, 0)), _const_spec((A_GROUPS, CHUNK, CHUNK)),
                   _const_spec((CHUNK, A_GROUPS)), _const_spec((1, A_WIDTH)), _const_spec((1, A_WIDTH))],
        out_shape=[SDS((t, 2 * A_WIDTH), BF16), SDS((A_GROUPS, CHUNK, CHUNK), F32), SDS((CHUNK, A_GROUPS), F32),
                   SDS((1, A_WIDTH), F32), SDS((1, A_WIDTH), F32)],
        scratch_shapes=[pltpu.VMEM((A_TILE, A_WIDTH), F32), pltpu.VMEM((A_TILE, A_WIDTH), F32)],
        name="mixer_a_bwd", compiler_params=_params())(proj, dcat, lg, lb, w, bias_t)


def _rope_tables(t):
    half = ROPE_DIM // 2
    inv_freq = ROPE_THETA ** (-jnp.arange(half, dtype=F32) * 2.0 / ROPE_DIM)
    ang = jnp.arange(t, dtype=F32)[:, None] * inv_freq[None, :]
    cos, sin = jnp.cos(ang), jnp.sin(ang)
    one = jnp.ones((t, HEAD_DIM - ROPE_DIM), F32)
    zero = jnp.zeros((t, HEAD_DIM - ROPE_DIM), F32)
    zh = jnp.zeros((t, half), F32)
    c = jnp.concatenate([cos, cos, one], axis=1)
    s1 = jnp.concatenate([-sin, zh, zero], axis=1)
    s2 = jnp.concatenate([zh, sin, zero], axis=1)
    return tuple(jnp.tile(a, (1, LANES // HEAD_DIM)) for a in (c, s1, s2))


QK_TILE = 512
QK_ROWS = 64
QK_COLS = 2 * N_DIL * B_WIDTH


CHUNKS = B_WIDTH // LANES


def _regroup_out(scr, first, out_ref, rate, tile):
    rows = tile // rate
    for rho in range(rate):
        for c in range(CHUNKS):
            out_ref[rho, :, c * LANES:(c + 1) * LANES] = scr[first + c, pl.ds(rho, rows, stride=rate), :].astype(out_ref.dtype)


def _regroup_in(x_ref, scr, rate, tile):
    rows = tile // rate
    for rho in range(rate):
        for c in range(CHUNKS):
            scr[c, pl.ds(rho, rows, stride=rate), :] = x_ref[rho, :, c * LANES:(c + 1) * LANES].astype(F32)


def _regrouped_spec(rate, tile):
    return pl.BlockSpec((rate, tile // rate, B_WIDTH), lambda i, *_: (0, i, 0))


def _qk_fwd(proj, gains, tabs, ride=None):
    t = proj.shape[0]
    col0 = 2 * A_WIDTH // 1024
    r1, r2 = DIL_RATES[1], DIL_RATES[2]

    def body(p_ref, g_ref, c_ref, s1_ref, s2_ref, o_ref, q1_ref, q2_ref, k1_ref, k2_ref, scr):
        seg = _segment_mean_matrix(HEAD_DIM)
        for r0 in range(0, QK_TILE, QK_ROWS):
            rows = slice(r0, r0 + QK_ROWS)
            c, s1, s2 = c_ref[rows, :], s1_ref[rows, :], s2_ref[rows, :]
            for ci in range(1024 // LANES):
                ls = slice(ci * LANES, (ci + 1) * LANES)
                xv = p_ref[rows, ls].astype(F32)
                r = lax.rsqrt(_segment_dot(xv * xv, seg) + EPS)
                y = xv * r * g_ref[:, ls]
                val = y * c + pltpu.roll(y, LANES - 8, axis=1) * s1 + pltpu.roll(y, 8, axis=1) * s2
                o_ref[rows, ls] = val.astype(BF16)
                scr[ci, rows, :] = val

        j = pl.program_id(1)

        @pl.when(j == 0)
        def _():
            _regroup_out(scr, CHUNKS, q1_ref, r1, QK_TILE)

        @pl.when(j == 1)
        def _():
            _regroup_out(scr, 0, q2_ref, r2, QK_TILE)

        @pl.when(j == 2)
        def _():
            _regroup_out(scr, 0, k1_ref, r1, QK_TILE)
            _regroup_out(scr, CHUNKS, k2_ref, r2, QK_TILE)

    tab = pl.BlockSpec((QK_TILE, LANES), lambda i, j: (i, 0))
    g1, g2 = SDS((r1, t // r1, B_WIDTH), BF16), SDS((r2, t // r2, B_WIDTH), BF16)
    s1_, s2_ = _regrouped_spec(r1, QK_TILE), _regrouped_spec(r2, QK_TILE)
    return _call(
        body, grid=(t // QK_TILE, QK_COLS // 1024),
        in_specs=[pl.BlockSpec((QK_TILE, 1024), lambda i, j: (i, col0 + j)), pl.BlockSpec((1, 1024), lambda i, j: (0, j)),
                  tab, tab, tab],
        out_specs=[pl.BlockSpec((QK_TILE, 1024), lambda i, j: (i, j)), s1_, s2_, s1_, s2_],
        out_shape=[SDS((t, QK_COLS), BF16), g1, g2, g1, g2],
        scratch_shapes=[pltpu.VMEM((2 * CHUNKS, QK_TILE, LANES), F32)],
        operands=[proj, gains, *tabs], name="qk_norm_rope_fwd", ride=ride)


PERM_TILE = 512


def _permute(name, items, rate):
    t = items[0][0].shape[0]
    n = len(items)

    def body(*refs):
        scr = refs[-1]
        for x_ref, o_ref in zip(refs[:n], refs[n:2 * n]):
            for ci in range(CHUNKS):
                scr[ci] = x_ref[:, ci * LANES:(ci + 1) * LANES].astype(F32)
            _regroup_out(scr, 0, o_ref, rate, PERM_TILE)

    return pl.pallas_call(
        body, grid=(t // PERM_TILE,),
        in_specs=[pl.BlockSpec((PERM_TILE, B_WIDTH), functools.partial(lambda cb, i: (i, cb), cb)) for _, cb in items],
        out_specs=[_regrouped_spec(rate, PERM_TILE) for _ in items],
        out_shape=[SDS((rate, t // rate, B_WIDTH), a.dtype) for a, _ in items],
        scratch_shapes=[pltpu.VMEM((CHUNKS, PERM_TILE, LANES), F32)],
        name=name, compiler_params=_params())(*[a for a, _ in items])


def _head_lane_mask(h):
    lane = lax.broadcasted_iota(jnp.int32, (1, LANES), 1)
    return (lane < HEAD_DIM) if h == 0 else (lane >= HEAD_DIM)


def _attn_fwd(name, q, k, v, ride=None):
    rate, length = q[0].shape[0], q[0].shape[1]
    nb = length // ATT_BLOCK
    scale = HEAD_DIM ** -0.5

    def body(q_ref, kc_ref, kp_ref, vc_ref, vp_ref, o_ref, l_ref):
        n = pl.program_id(1)
        qi = lax.broadcasted_iota(jnp.int32, (ATT_BLOCK, 2 * ATT_BLOCK), 0)
        cj = lax.broadcasted_iota(jnp.int32, (ATT_BLOCK, 2 * ATT_BLOCK), 1)
        has_prev = jnp.where(n > 0, 0, 2 * ATT_BLOCK)
        mask = ((cj < ATT_BLOCK) & (cj >= qi + has_prev)) | ((cj >= ATT_BLOCK) & (cj - ATT_BLOCK <= qi))
        heads = [(hp, h) for hp in range(CHUNKS) for h in range(2)]
        q2, k2, v2 = {}, {}, {}
        for hp in range(CHUNKS):
            ls = slice(hp * LANES, (hp + 1) * LANES)
            q2[hp] = q_ref[:, ls]
            k2[hp] = jnp.concatenate([kp_ref[:, ls], kc_ref[:, ls]], axis=0)
            v2[hp] = jnp.concatenate([vp_ref[:, ls], vc_ref[:, ls]], axis=0)
        scores = {}
        for hp, h in heads:
            scores[hp, h] = _dot(jnp.where(_head_lane_mask(h), q2[hp], jnp.zeros_like(q2[hp])), k2[hp], NT) * scale
        probs, lses = {}, {}
        for hp, h in heads:
            s = jnp.where(mask, scores[hp, h], NEG_INF)
            m = jnp.max(s, axis=1, keepdims=True)
            p = jnp.exp(s - m)
            den = jnp.sum(p, axis=1, keepdims=True)
            lses[hp, h] = m + jnp.log(den)
            probs[hp, h] = (p / den).astype(BF16)
        for hp in range(CHUNKS):
            ls = slice(hp * LANES, (hp + 1) * LANES)
            o_acc = None
            for h in range(2):
                o = _dot(probs[hp, h], jnp.where(_head_lane_mask(h), v2[hp], jnp.zeros_like(v2[hp])), NN)
                o_acc = o if o_acc is None else o_acc + o
            o_ref[:, ls] = o_acc
            zeros = jnp.zeros((ATT_BLOCK, LANES), F32)
            l_ref[:, ls] = jnp.where(_head_lane_mask(1), lses[hp, 1] + zeros, lses[hp, 0] + zeros)

    def cur(cb):
        return pl.BlockSpec((None, ATT_BLOCK, B_WIDTH), lambda r, n: (r, n, cb))

    def prev(cb):
        return pl.BlockSpec((None, ATT_BLOCK, B_WIDTH), lambda r, n: (r, jnp.maximum(n - 1, 0), cb))

    out = pl.BlockSpec((None, ATT_BLOCK, B_WIDTH), lambda r, n: (r, n, 0))
    return _call(
        body, grid=(rate, nb),
        in_specs=[cur(q[1]), cur(k[1]), prev(k[1]), cur(v[1]), prev(v[1])],
        out_specs=[out, out], out_shape=[SDS((rate, length, B_WIDTH), F32)] * 2,
        operands=[q[0], k[0], k[0], v[0], v[0]], name=name, ride=ride)


def _attn_merge(a_out, o_list, l_list):
    t = a_out.shape[0]
    tm = PERM_TILE
    r1, r2 = DIL_RATES[1], DIL_RATES[2]

    def body(a_ref, o0, o1, o2, l0, l1, l2, cat_ref, lt_ref, lt1_ref, lt2_ref, so1, so2, sl1, sl2, slt):
        _regroup_in(o1, so1, r1, tm)
        _regroup_in(l1, sl1, r1, tm)
        _regroup_in(o2, so2, r2, tm)
        _regroup_in(l2, sl2, r2, tm)
        cat_ref[:, :A_WIDTH] = a_ref[...]
        for c in range(CHUNKS):
            ls = slice(c * LANES, (c + 1) * LANES)
            lg = [l0[:, ls], sl1[c], sl2[c]]
            m = jnp.maximum(jnp.maximum(lg[0], lg[1]), lg[2])
            es = [jnp.exp(l - m) for l in lg]
            den = es[0] + es[1] + es[2]
            b = (es[0] * o0[:, ls] + es[1] * so1[c] + es[2] * so2[c]) / den
            cat_ref[:, A_WIDTH + c * LANES:A_WIDTH + (c + 1) * LANES] = b.astype(BF16)
            lt = m + jnp.log(den)
            lt_ref[:, ls] = lt
            slt[c] = lt
        _regroup_out(slt, 0, lt1_ref, r1, tm)
        _regroup_out(slt, 0, lt2_ref, r2, tm)

    blk = _row_spec(tm, B_WIDTH)
    g1, g2 = _regrouped_spec(r1, tm), _regrouped_spec(r2, tm)
    return pl.pallas_call(
        body, grid=(t // tm,), in_specs=[blk, blk, g1, g2, blk, g1, g2],
        out_specs=[_row_spec(tm, A_WIDTH + B_WIDTH), blk, g1, g2],
        out_shape=[SDS((t, A_WIDTH + B_WIDTH), BF16), SDS((t, B_WIDTH), F32), SDS((r1, t // r1, B_WIDTH), F32),
                   SDS((r2, t // r2, B_WIDTH), F32)],
        scratch_shapes=[pltpu.VMEM((CHUNKS, tm, LANES), F32)] * 5,
        name="attn_merge", compiler_params=_params())(a_out, *o_list, *l_list)


def _attn_bwd_prep(dcat, cat):
    t = dcat.shape[0]
    tm = PERM_TILE
    r1, r2 = DIL_RATES[1], DIL_RATES[2]

    def body(d_ref, b_ref, db_ref, dd_ref, db1_ref, dd1_ref, db2_ref, dd2_ref, sdb, sdd):
        seg = _segment_mean_matrix(HEAD_DIM, scale=1.0)
        for c in range(CHUNKS):
            ls = slice(c * LANES, (c + 1) * LANES)
            d = d_ref[:, ls]
            dsum = _segment_dot(d * b_ref[:, ls].astype(F32), seg)
            db_ref[:, ls] = d.astype(BF16)
            dd_ref[:, ls] = dsum
            sdb[c] = d
            sdd[c] = dsum
        _regroup_out(sdb, 0, db1_ref, r1, tm)
        _regroup_out(sdd, 0, dd1_ref, r1, tm)
        _regroup_out(sdb, 0, db2_ref, r2, tm)
        _regroup_out(sdd, 0, dd2_ref, r2, tm)

    right = pl.BlockSpec((tm, B_WIDTH), lambda i: (i, 1))
    blk = _row_spec(tm, B_WIDTH)
    g1, g2 = _regrouped_spec(r1, tm), _regrouped_spec(r2, tm)
    return pl.pallas_call(
        body, grid=(t // tm,), in_specs=[right, right], out_specs=[blk, blk, g1, g1, g2, g2],
        out_shape=[SDS((t, B_WIDTH), BF16), SDS((t, B_WIDTH), F32), SDS((r1, t // r1, B_WIDTH), BF16),
                   SDS((r1, t // r1, B_WIDTH), F32), SDS((r2, t // r2, B_WIDTH), BF16), SDS((r2, t // r2, B_WIDTH), F32)],
        scratch_shapes=[pltpu.VMEM((CHUNKS, tm, LANES), F32)] * 2,
        name="attn_bwd_prep", compiler_params=_params())(dcat, cat)


def _attn_bwd(name, q, k, v, db, lse, dd, ride=None):
    rate, length = db.shape[0], db.shape[1]
    nb = length // ATT_BLOCK
    scale = HEAD_DIM ** -0.5

    def body(qa_ref, qb_ref, k_ref, v_ref, dba_ref, dbb_ref, la_ref, lb_ref, da_ref, dbd_ref, dq_ref, dk_ref, dv_ref, carry):
        m = pl.program_id(1)

        @pl.when(m == 0)
        def _():
            carry[...] = jnp.zeros_like(carry)

        row = lax.broadcasted_iota(jnp.int32, (2 * ATT_BLOCK, ATT_BLOCK), 0)
        kj = lax.broadcasted_iota(jnp.int32, (2 * ATT_BLOCK, ATT_BLOCK), 1)
        no_next = jnp.where(m + 1 < nb, 0, 2 * ATT_BLOCK)
        mask = ((row < ATT_BLOCK) & (kj <= row)) | ((row >= ATT_BLOCK) & (kj >= row - ATT_BLOCK + no_next))
        heads = [(hp, h) for hp in range(CHUNKS) for h in range(2)]
        q2, db2, lse2, dd2, k2, v2 = {}, {}, {}, {}, {}, {}
        for hp in range(CHUNKS):
            ls = slice(hp * LANES, (hp + 1) * LANES)
            k2[hp], v2[hp] = k_ref[:, ls], v_ref[:, ls]
            q2[hp] = jnp.concatenate([qa_ref[:, ls], qb_ref[:, ls]], axis=0)
            db2[hp] = jnp.concatenate([dba_ref[:, ls], dbb_ref[:, ls]], axis=0)
            lse2[hp] = jnp.concatenate([la_ref[:, ls], lb_ref[:, ls]], axis=0)
            dd2[hp] = jnp.concatenate([da_ref[:, ls], dbd_ref[:, ls]], axis=0)
        km, scores, dps = {}, {}, {}
        for hp, h in heads:
            hm = _head_lane_mask(h)
            km[hp, h] = jnp.where(hm, k2[hp], jnp.zeros_like(k2[hp]))
            scores[hp, h] = _dot(q2[hp], km[hp, h], NT) * scale
            dps[hp, h] = _dot(db2[hp], jnp.where(hm, v2[hp], jnp.zeros_like(v2[hp])), NT)
        probs, dss = {}, {}
        for hp, h in heads:
            hm = _head_lane_mask(h)
            lse_col = jnp.max(jnp.where(hm, lse2[hp], NEG_INF), axis=1, keepdims=True)
            dd_col = jnp.max(jnp.where(hm, dd2[hp], NEG_INF), axis=1, keepdims=True)
            p = jnp.where(mask, jnp.exp(scores[hp, h] - lse_col), 0.0)
            probs[hp, h] = p.astype(BF16)
            dss[hp, h] = (p * (dps[hp, h] - dd_col) * scale).astype(BF16)
        for hp in range(CHUNKS):
            ls = slice(hp * LANES, (hp + 1) * LANES)
            dq_acc, dk_acc, dv_acc = None, None, None
            for h in range(2):
                hm = _head_lane_mask(h)
                dvc = _dot(probs[hp, h], jnp.where(hm, db2[hp], jnp.zeros_like(db2[hp])), TN)
                dqc = _dot(dss[hp, h], km[hp, h], NN)
                dkc = _dot(dss[hp, h], jnp.where(hm, q2[hp], jnp.zeros_like(q2[hp])), TN)
                dq_acc = dqc if dq_acc is None else dq_acc + dqc
                dk_acc = dkc if dk_acc is None else dk_acc + dkc
                dv_acc = dvc if dv_acc is None else dv_acc + dvc
            dq_ref[:, ls] = (dq_acc[:ATT_BLOCK] + carry[:, ls]).astype(BF16)
            carry[:, ls] = dq_acc[ATT_BLOCK:]
            dk_ref[:, ls] = dk_acc.astype(BF16)
            dv_ref[:, ls] = dv_acc.astype(BF16)

    def cur(cb):
        return pl.BlockSpec((None, ATT_BLOCK, B_WIDTH), lambda r, n: (r, n, cb))

    def nxt(cb):
        return pl.BlockSpec((None, ATT_BLOCK, B_WIDTH), lambda r, n: (r, jnp.minimum(n + 1, nb - 1), cb))

    out = cur(0)
    return _call(
        body, grid=(rate, nb),
        in_specs=[cur(q[1]), nxt(q[1]), cur(k[1]), cur(v[1]), cur(0), nxt(0), cur(0), nxt(0), cur(0), nxt(0)],
        out_specs=[out, out, out], out_shape=[SDS((rate, length, B_WIDTH), BF16)] * 3,
        scratch_shapes=[pltpu.VMEM((ATT_BLOCK, B_WIDTH), F32)],
        operands=[q[0], q[0], k[0], v[0], db, db, lse, lse, dd, dd], name=name, ride=ride)


AB_IN = 2 * A_WIDTH + 3 * N_DIL * B_WIDTH
ASM_TILE = 256


def _dproj_assemble(proj, d_a, dq, dk, dv, gains, tabs, ride=None):
    t = proj.shape[0]
    n_in = 3 * N_DIL

    def body(p_ref, da_ref, *rest):
        grads = rest[:n_in]
        g_ref, c_ref, s1_ref, s2_ref, o_ref, dg_ref = rest[n_in:n_in + 6]
        scratch = rest[n_in + 6:]

        @pl.when(pl.program_id(0) == 0)
        def _():
            dg_ref[...] = jnp.zeros_like(dg_ref)

        chunk = {}
        k_scr = 0
        for j in range(n_in):
            g = j % N_DIL
            if DIL_RATES[g] == 1:
                for ci in range(CHUNKS):
                    chunk[j, ci] = functools.partial(lambda r, ci: r[:, ci * LANES:(ci + 1) * LANES].astype(F32), grads[j], ci)
            else:
                scr = scratch[k_scr]
                k_scr += 1
                _regroup_in(grads[j], scr, DIL_RATES[g], ASM_TILE)
                for ci in range(CHUNKS):
                    chunk[j, ci] = functools.partial(lambda s, ci: s[ci], scr, ci)

        seg = _segment_mean_matrix(HEAD_DIM)
        c, s1, s2 = c_ref[...], s1_ref[...], s2_ref[...]
        o_ref[:, :2 * A_WIDTH] = da_ref[...]
        for jg in range(2 * N_DIL):
            for ci in range(CHUNKS):
                col = jg * B_WIDTH + ci * LANES
                src = slice(2 * A_WIDTH + col, 2 * A_WIDTH + col + LANES)
                xv = p_ref[:, src].astype(F32)
                r = lax.rsqrt(_segment_dot(xv * xv, seg) + EPS)
                xh = xv * r
                gain = g_ref[:, col:col + LANES]
                do = chunk[jg, ci]()
                dy = do * c + pltpu.roll(do * s1, 8, axis=1) + pltpu.roll(do * s2, LANES - 8, axis=1)
                dg_ref[:, col:col + LANES] += jnp.sum(dy * xh, axis=0, keepdims=True)
                dxh = dy * gain
                o_ref[:, src] = (r * (dxh - xh * _segment_dot(dxh * xh, seg))).astype(BF16)
        v0 = 2 * A_WIDTH + QK_COLS
        for g in range(N_DIL):
            for ci in range(CHUNKS):
                col = v0 + g * B_WIDTH + ci * LANES
                o_ref[:, col:col + LANES] = chunk[2 * N_DIL + g, ci]().astype(BF16)

    specs = [_row_spec(ASM_TILE, B_WIDTH) if r == 1 else _regrouped_spec(r, ASM_TILE) for r in DIL_RATES] * 3
    n_scr = 3 * sum(1 for r in DIL_RATES if r > 1)
    tab = _row_spec(ASM_TILE, LANES)
    return _call(
        body, grid=(t // ASM_TILE,),
        in_specs=[_row_spec(ASM_TILE, AB_IN), _row_spec(ASM_TILE, 2 * A_WIDTH)] + specs
        + [_const_spec((1, QK_COLS)), tab, tab, tab],
        out_specs=[_row_spec(ASM_TILE, AB_IN), _const_spec((1, QK_COLS))],
        out_shape=[SDS((t, AB_IN), BF16), SDS((1, QK_COLS), F32)],
        scratch_shapes=[pltpu.VMEM((CHUNKS, ASM_TILE, LANES), F32)] * n_scr,
        operands=[proj, d_a, *dq, *dk, *dv, gains, *tabs], name="dproj_assemble", ride=ride)


def _fold_heads(dg_lane):
    n = dg_lane.shape[1]

    def body(x_ref, o_ref):
        r = lax.broadcasted_iota(jnp.int32, (B_WIDTH, B_WIDTH), 0) % HEAD_DIM
        c = lax.broadcasted_iota(jnp.int32, (B_WIDTH, B_WIDTH), 1) % HEAD_DIM
        fold = jnp.where(r == c, 1.0, 0.0).astype(F32)
        for jg in range(n // B_WIDTH):
            ls = slice(jg * B_WIDTH, (jg + 1) * B_WIDTH)
            o_ref[:, ls] = _dot_hi(jnp.broadcast_to(x_ref[:, ls], (8, B_WIDTH)), fold)

    return pl.pallas_call(body, out_shape=SDS((8, n), F32), name="fold_heads", compiler_params=_params())(dg_lane)


CD_TILE = 256
TAP_ROWS = 64
CD_IN = 2 * C_WIDTH + 3 * 512


def _shifted_copies(src, dst, rows):
    dst[0, :rows] = src[...]
    for b in range(1, 8):
        dst[b, :rows - 8] = src[pl.ds(b, rows - 8), :]


def _rows_from(shifted, start, n, lanes=slice(None)):
    b = start % 8
    return shifted[b, pl.ds(start - b, n), lanes]


def _mixer_cd_fwd(proj, cw, cb, lg, lb, dw):
    t = proj.shape[0]
    per = CD_TILE // HALO

    def body(h_ref, m_ref, cw_ref, cb_ref, lg_ref, lb_ref, dw_ref, o_ref, c1_ref, c_scr, e_scr, c_sh):
        not_first = (pl.program_id(0) > 0).astype(F32)
        lanes = [slice(c * LANES, (c + 1) * LANES) for c in range(C_WIDTH // LANES)]

        def col(ref, part, ls):
            return ref[:, part * C_WIDTH + ls.start:part * C_WIDTH + ls.stop].astype(F32)

        for ls in lanes:
            c_scr[:HALO, ls] = col(h_ref, 0, ls) * _sigmoid(col(h_ref, 1, ls)) * not_first
            c_scr[HALO:, ls] = col(m_ref, 0, ls) * _sigmoid(col(m_ref, 1, ls))
            e_scr[:HALO, ls] = col(h_ref, 3, ls) * col(h_ref, 4, ls) * not_first
            e_scr[HALO:, ls] = col(m_ref, 3, ls) * col(m_ref, 4, ls)
        _shifted_copies(c_scr, c_sh, HALO + CD_TILE)
        for ls in lanes:
            for r0 in range(0, CD_TILE, TAP_ROWS):
                acc = jnp.zeros((TAP_ROWS, LANES), F32)
                for k in range(C_KERNEL):
                    acc = acc + cw_ref[k:k + 1, ls] * _rows_from(c_sh, r0 + HALO - (C_KERNEL - 1) + k, TAP_ROWS, ls)
                c1_ref[r0:r0 + TAP_ROWS, ls] = acc + cb_ref[:, ls]
        mean = sum(jnp.sum(c1_ref[:, ls], axis=-1, keepdims=True) for ls in lanes) * (1.0 / C_WIDTH)
        var = sum(jnp.sum((c1_ref[:, ls] - mean) ** 2, axis=-1, keepdims=True) for ls in lanes) * (1.0 / C_WIDTH)
        rs = lax.rsqrt(var + EPS)
        for ls in lanes:
            c2 = (c1_ref[:, ls] - mean) * rs * lg_ref[:, ls] + lb_ref[:, ls]
            o_ref[:, ls] = (c2 * _sigmoid(c2)).astype(BF16)
            d1 = jnp.zeros((CD_TILE, LANES), F32)
            for k in range(D_KERNEL):
                d1 = d1 + dw_ref[k:k + 1, ls] * e_scr[pl.ds(HALO - (D_KERNEL - 1) + k, CD_TILE), ls]
            o_ref[:, C_WIDTH + ls.start:C_WIDTH + ls.stop] = (col(m_ref, 2, ls) * d1).astype(BF16)

    return pl.pallas_call(
        body, grid=(t // CD_TILE,),
        in_specs=[pl.BlockSpec((HALO, CD_IN), lambda i: (jnp.maximum(i * per - 1, 0), 0)), _row_spec(CD_TILE, CD_IN),
                  _const_spec((32, C_WIDTH)), _const_spec((1, C_WIDTH)), _const_spec((1, C_WIDTH)), _const_spec((1, C_WIDTH)),
                  _const_spec((8, C_WIDTH))],
        out_specs=[_row_spec(CD_TILE, 2 * C_WIDTH), _row_spec(CD_TILE, C_WIDTH)],
        out_shape=[SDS((t, 2 * C_WIDTH), BF16), SDS((t, C_WIDTH), F32)],
        scratch_shapes=[pltpu.VMEM((HALO + CD_TILE, C_WIDTH), F32)] * 2 + [pltpu.VMEM((8, HALO + CD_TILE, C_WIDTH), F32)],
        name="mixer_cd_fwd", compiler_params=_params())(proj, proj, cw, cb, lg, lb, dw)


def _mixer_cd_bwd(proj, dcat, c1, cw, lg, lb, dw, ride=None):
    t = proj.shape[0]
    per = CD_TILE // HALO
    nt = t // CD_TILE
    ext = CD_TILE + HALO

    def body(hp_ref, m_ref, hn_ref, dm_ref, dn_ref, c1m_ref, c1n_ref, cw_ref, lg_ref, lb_ref, dw_ref,
             dp_ref, dcw_ref, dcb_ref, dlg_ref, dlb_ref, ddw_ref, c_scr, e_scr, dc1_scr, dd1_scr, c_sh, dc1_sh, dcw_acc,
             dvh_scr, vhat_scr):
        i = pl.program_id(0)

        @pl.when(i == 0)
        def _():
            for r in (dcw_acc, dcb_ref, dlg_ref, dlb_ref, ddw_ref):
                r[...] = jnp.zeros_like(r)

        not_first = (i > 0).astype(F32)
        not_last = (i < nt - 1).astype(F32)
        main = slice(HALO, HALO + CD_TILE)
        lanes = [slice(c * LANES, (c + 1) * LANES) for c in range(C_WIDTH // LANES)]

        def col(ref, part, ls):
            return ref[:, part * C_WIDTH + ls.start:part * C_WIDTH + ls.stop].astype(F32)

        for ls in lanes:
            c_scr[:HALO, ls] = col(hp_ref, 0, ls) * _sigmoid(col(hp_ref, 1, ls)) * not_first
            c_scr[main, ls] = col(m_ref, 0, ls) * _sigmoid(col(m_ref, 1, ls))
            c_scr[HALO + CD_TILE:, ls] = col(hn_ref, 0, ls) * _sigmoid(col(hn_ref, 1, ls)) * not_last
            e_scr[:HALO, ls] = col(hp_ref, 3, ls) * col(hp_ref, 4, ls) * not_first
            e_scr[main, ls] = col(m_ref, 3, ls) * col(m_ref, 4, ls)
            e_scr[HALO + CD_TILE:, ls] = col(hn_ref, 3, ls) * col(hn_ref, 4, ls) * not_last
        _shifted_copies(c_scr, c_sh, 2 * HALO + CD_TILE)

        def c1_of(ls):
            return jnp.concatenate([c1m_ref[:, ls], c1n_ref[:, ls]], axis=0)

        mean = sum(jnp.sum(c1_of(ls), axis=-1, keepdims=True) for ls in lanes) * (1.0 / C_WIDTH)
        var = sum(jnp.sum((c1_of(ls) - mean) ** 2, axis=-1, keepdims=True) for ls in lanes) * (1.0 / C_WIDTH)
        rs = lax.rsqrt(var + EPS)
        sum_dvh, sum_dvh_vhat = 0.0, 0.0
        for ls in lanes:
            vhat = (c1_of(ls) - mean) * rs
            c2 = vhat * lg_ref[:, ls] + lb_ref[:, ls]
            sig = _sigmoid(c2)
            dc = jnp.concatenate([dm_ref[:, ls], dn_ref[:, ls] * not_last], axis=0)
            dc2 = dc * (sig * (1.0 + c2 * (1.0 - sig)))
            dvh = dc2 * lg_ref[:, ls]
            sum_dvh = sum_dvh + jnp.sum(dvh, axis=-1, keepdims=True)
            sum_dvh_vhat = sum_dvh_vhat + jnp.sum(dvh * vhat, axis=-1, keepdims=True)
            dvh_scr[:, ls] = dvh
            vhat_scr[:, ls] = vhat
            dlg_ref[:, ls] += jnp.sum((dc2 * vhat)[:CD_TILE], axis=0, keepdims=True)
            dlb_ref[:, ls] += jnp.sum(dc2[:CD_TILE], axis=0, keepdims=True)
        for ls in lanes:
            dc1 = rs * (dvh_scr[:, ls] - sum_dvh * (1.0 / C_WIDTH) - vhat_scr[:, ls] * (sum_dvh_vhat * (1.0 / C_WIDTH)))
            dc1_scr[:, ls] = dc1
            dcb_ref[:, ls] += jnp.sum(dc1[:CD_TILE], axis=0, keepdims=True)
        _shifted_copies(dc1_scr, dc1_sh, ext)
        for ls in lanes:
            for r0 in range(0, CD_TILE, TAP_ROWS):
                rows = slice(r0, r0 + TAP_ROWS)
                dc1_m = dc1_scr[rows, ls]
                dc0 = jnp.zeros((TAP_ROWS, LANES), F32)
                for k in range(C_KERNEL):
                    dc0 = dc0 + cw_ref[k:k + 1, ls] * _rows_from(dc1_sh, r0 + C_KERNEL - 1 - k, TAP_ROWS, ls)
                    prod = dc1_m * _rows_from(c_sh, r0 + HALO - (C_KERNEL - 1) + k, TAP_ROWS, ls)
                    dcw_acc[k, :, ls] += prod.reshape(TAP_ROWS // 8, 8, LANES).sum(axis=0)
                g_m = m_ref[rows, C_WIDTH + ls.start:C_WIDTH + ls.stop].astype(F32)
                a_m = m_ref[rows, ls].astype(F32)
                sig_m = _sigmoid(g_m)
                dp_ref[rows, ls] = (dc0 * sig_m).astype(BF16)
                dp_ref[rows, C_WIDTH + ls.start:C_WIDTH + ls.stop] = (dc0 * a_m * sig_m * (1.0 - sig_m)).astype(BF16)

        @pl.when(i == nt - 1)
        def _():
            dcw_ref[...] = jnp.sum(dcw_acc[...], axis=1)

        for ls in lanes:
            wide = slice(C_WIDTH + ls.start, C_WIDTH + ls.stop)
            d1 = jnp.zeros((CD_TILE, LANES), F32)
            for k in range(D_KERNEL):
                d1 = d1 + dw_ref[k:k + 1, ls] * e_scr[pl.ds(HALO - (D_KERNEL - 1) + k, CD_TILE), ls]
            dd_m = dm_ref[:, wide]
            dd1 = jnp.concatenate([dd_m * col(m_ref, 2, ls), dn_ref[:, wide] * col(hn_ref, 2, ls) * not_last], axis=0)
            dd1_scr[:, ls] = dd1
            dp_ref[:, 2 * C_WIDTH + ls.start:2 * C_WIDTH + ls.stop] = (dd_m * d1).astype(BF16)
            de = jnp.zeros((CD_TILE, LANES), F32)
            for k in range(D_KERNEL):
                de = de + dw_ref[k:k + 1, ls] * dd1_scr[pl.ds(D_KERNEL - 1 - k, CD_TILE), ls]
                ddw_ref[k:k + 1, ls] += jnp.sum(dd1[:CD_TILE] * e_scr[pl.ds(HALO - (D_KERNEL - 1) + k, CD_TILE), ls], axis=0, keepdims=True)
            dp_ref[:, 3 * C_WIDTH + ls.start:3 * C_WIDTH + ls.stop] = (de * col(m_ref, 4, ls)).astype(BF16)
            dp_ref[:, 4 * C_WIDTH + ls.start:4 * C_WIDTH + ls.stop] = (de * col(m_ref, 3, ls)).astype(BF16)

    halo_prev = lambda i: (jnp.maximum(i * per - 1, 0), 0)
    halo_next = lambda i: (jnp.minimum((i + 1) * per, t // HALO - 1), 0)
    vec = _const_spec((1, C_WIDTH))
    return _call(
        body, grid=(nt,),
        in_specs=[pl.BlockSpec((HALO, CD_IN), halo_prev), _row_spec(CD_TILE, CD_IN), pl.BlockSpec((HALO, CD_IN), halo_next),
                  _row_spec(CD_TILE, 2 * C_WIDTH), pl.BlockSpec((HALO, 2 * C_WIDTH), halo_next),
                  _row_spec(CD_TILE, C_WIDTH), pl.BlockSpec((HALO, C_WIDTH), halo_next),
                  _const_spec((32, C_WIDTH)), vec, vec, _const_spec((8, C_WIDTH))],
        out_specs=[_row_spec(CD_TILE, CD_IN), _const_spec((32, C_WIDTH)), vec, vec, vec, _const_spec((8, C_WIDTH))],
        out_shape=[SDS((t, CD_IN), BF16), SDS((32, C_WIDTH), F32), SDS((1, C_WIDTH), F32), SDS((1, C_WIDTH), F32),
                   SDS((1, C_WIDTH), F32), SDS((8, C_WIDTH), F32)],
        scratch_shapes=[pltpu.VMEM((2 * HALO + CD_TILE, C_WIDTH), F32)] * 2 + [pltpu.VMEM((ext, C_WIDTH), F32)] * 2
        + [pltpu.VMEM((8, 2 * HALO + CD_TILE, C_WIDTH), F32), pltpu.VMEM((8, ext, C_WIDTH), F32),
           pltpu.VMEM((32, 8, C_WIDTH), F32)] + [pltpu.VMEM((ext, C_WIDTH), F32)] * 2,
        operands=[proj, proj, proj, dcat, dcat, c1, c1, cw, lg, lb, dw], name="mixer_cd_bwd", ride=ride)


def _wgrad(name, pairs, out_rc, t, ride):
    tk = TILES["wgrad"]
    assert tk == t, "the whole contraction has to fit one grid step"
    r, c = out_rc
    n = len(pairs)

    def body(*refs):
        ab, out_refs = refs[:2 * n], refs[2 * n:]
        for j in range(n):
            out_refs[j][...] = _dot(ab[2 * j][...], ab[2 * j + 1][...], TN).astype(BF16)

    operands, in_specs = [], []
    for lhs, lhs_spec, rhs, rhs_spec in pairs:
        operands += [lhs, rhs]
        in_specs += [lhs_spec, rhs_spec]
    res = _call(body, grid=(N_CHIPS, t // tk), in_specs=in_specs,
                out_specs=[pl.BlockSpec((None, r, c), lambda p, k: (p, 0, 0))] * n,
                out_shape=[SDS((N_CHIPS, r, c), BF16)] * n, operands=operands, name=name, ride=ride)
    outs, ride_res = (res, None) if ride is None else res
    outs = [o.reshape(N_CHIPS, 2, r // 2, c) for o in outs]
    return outs if ride is None else (outs, ride_res)


def _wgrad_col_sharded(name, h, dz_list, three_d, ride=None):
    t, d = h.shape
    tk = TILES["wgrad"]
    n4 = dz_list[0].shape[-1] if three_d else dz_list[0].shape[-1] // N_CHIPS
    hs = pl.BlockSpec((tk, d), lambda p, k: (k, 0))
    zs = pl.BlockSpec((None, tk, n4), lambda p, k: (p, k, 0)) if three_d else pl.BlockSpec((tk, n4), lambda p, k: (k, p))
    return _wgrad(name, [(h, hs, dz, zs) for dz in dz_list], (d, n4), t, ride)


def _wgrad_row_sharded(name, a, g, three_d, ride=None):
    many = isinstance(a, (list, tuple))
    a_list = list(a) if many else [a]
    t, d = g.shape
    tk = TILES["wgrad"]
    k4 = a_list[0].shape[-1] if three_d else a_list[0].shape[-1] // N_CHIPS
    a_spec = pl.BlockSpec((None, tk, k4), lambda p, k: (p, k, 0)) if three_d else pl.BlockSpec((tk, k4), lambda p, k: (k, p))
    gs = pl.BlockSpec((tk, d), lambda p, k: (k, 0))
    res = _wgrad(name, [(a_j, a_spec, g, gs) for a_j in a_list], (k4, d), t, ride)
    if many:
        return res
    return res[0] if ride is None else (res[0][0], res[1])


def _mesh_scalars():
    return jnp.stack([lax.axis_index("c"), 2 * lax.axis_index("x") + lax.axis_index("y")]).astype(jnp.int32)


def _stage_own(name, w, layer, dtype):
    layers, r, cols = w.shape
    h = r // 2

    def body(s_ref, x_ref, o_ref):
        o_ref[...] = x_ref[...].astype(dtype)

    return pl.pallas_call(
        body,
        grid_spec=pltpu.PrefetchScalarGridSpec(
            num_scalar_prefetch=1, grid=(2,),
            in_specs=[pl.BlockSpec((None, h, cols), lambda i, s: (2 * layer + i, 0, 0))],
            out_specs=pl.BlockSpec((None, None, h, cols), lambda i, s: (s[1], i, 0, 0))),
        out_shape=SDS((N_CHIPS, 2, h, cols), dtype), name=name,
        compiler_params=_params())(_mesh_scalars(), w.reshape(2 * layers, h, cols))


def _remote(src, dst, send_sem, recv_sem, device):
    return pltpu.make_async_remote_copy(src, dst, send_sem, recv_sem, device_id=device, device_id_type=MESH)


def _ride_gather_send(bufs):
    n = len(bufs)

    def each(b, sems, act):
        send, recv = sems
        x, y, c, p, others = _position()
        for t in range(n):
            for j, (qx, qy) in enumerate(others):
                act(b[t].at[p, c], b[t].at[2 * qx + qy, c], send.at[t, j], recv.at[t, j], (qx, qy, c))

    def start(ins, b, new, sems):
        each(b, sems, lambda mine, landed, s, r, dev: _remote(mine, mine, s, r, dev).start())

    def finish(ins, b, new, sems):
        def act(mine, landed, s, r, dev):
            _remote(mine, mine, s, r, dev).wait_send()
            _remote(landed, landed, s, r, dev).wait_recv()
        each(b, sems, act)

    return _Ride([], bufs, [], [(n, 3), (n, 3)], start, finish)


def _ride_gather_pass(bufs):
    n = len(bufs)

    def each(b, sems, act):
        send, recv = sems
        x, y, c, p, others = _position()
        for t in range(n):
            for j, (qx, qy) in enumerate(others):
                act(b[t].at[2 * qx + qy, c], b[t].at[2 * qx + qy, 1 - c], send.at[t, j], recv.at[t, j], (x, y, 1 - c))

    def start(ins, b, new, sems):
        each(b, sems, lambda landed, passed, s, r, dev: _remote(landed, landed, s, r, dev).start())

    def finish(ins, b, new, sems):
        def act(landed, passed, s, r, dev):
            _remote(landed, landed, s, r, dev).wait_send()
            _remote(passed, passed, s, r, dev).wait_recv()
        each(b, sems, act)

    return _Ride([], bufs, [], [(n, 3), (n, 3)], start, finish)


def _ride_gather(bufs):
    send, onward = _ride_gather_send(bufs), _ride_gather_pass(bufs)
    n_send = len(send.sem_shapes)

    def start(ins, b, new, sems):
        send.start(ins, b, new, sems[:n_send])

    def finish(ins, b, new, sems):
        send.finish(ins, b, new, sems[:n_send])
        onward.start(ins, b, new, sems[n_send:])
        onward.finish(ins, b, new, sems[n_send:])

    return _Ride([], bufs, [], send.sem_shapes + onward.sem_shapes, start, finish)


def _ride_swap(tensors):
    n = len(tensors)

    def each(ins, new, sems, act):
        send, recv = sems
        x, y, c, _, _ = _position()
        for t in range(n):
            act(_remote(ins[t].at[:, 1 - c], new[t], send.at[t], recv.at[t], (x, y, 1 - c)))

    def start(ins, b, new, sems):
        each(ins, new, sems, lambda cp: cp.start())

    def finish(ins, b, new, sems):
        each(ins, new, sems, lambda cp: cp.wait())

    return _Ride(tensors, [], [SDS((s.shape[0],) + s.shape[2:], s.dtype) for s in tensors], [(n,), (n,)], start, finish)


def _ride_scatter(tensors, landing):
    n = len(tensors)

    def each(ins, b, sems, act):
        send, recv = sems
        x, y, c, p, others = _position()
        for t in range(n):
            for j, (qx, qy) in enumerate(others):
                q = 2 * qx + qy
                act(ins[t].at[q], b[t].at[p], b[t].at[q], send.at[t, j], recv.at[t, j], (qx, qy, c))

    def start(ins, b, new, sems):
        each(ins, b, sems, lambda src, dst, landed, s, r, dev: _remote(src, dst, s, r, dev).start())

    def finish(ins, b, new, sems):
        def act(src, dst, landed, s, r, dev):
            _remote(src, dst, s, r, dev).wait_send()
            _remote(landed, landed, s, r, dev).wait_recv()
        each(ins, b, sems, act)

    return _Ride(tensors, landing, [], [(n, 3), (n, 3)], start, finish)


def _ride_join(bufs):
    n = len(bufs)

    def each(b, sems, act):
        send, recv = sems
        x, y, c, _, _ = _position()
        for t in range(n):
            act(b[t].at[c], b[t].at[1 - c], send.at[t], recv.at[t], (x, y, 1 - c))

    def start(ins, b, new, sems):
        each(b, sems, lambda mine, theirs, s, r, dev: _remote(mine, mine, s, r, dev).start())

    def finish(ins, b, new, sems):
        def act(mine, theirs, s, r, dev):
            _remote(mine, mine, s, r, dev).wait_send()
            _remote(theirs, theirs, s, r, dev).wait_recv()
        each(b, sems, act)

    return _Ride([], bufs, [], [(n,), (n,)], start, finish)


def _all_reduce_small(pack, ride=None):
    rows = pack.shape[0]
    n_dev = 2 * N_CHIPS
    n_rb = 0 if ride is None else len(ride.bufs)

    def body(x_ref, *rest):
        o_ref = rest[n_rb]
        r_bufs = rest[n_rb + 1:2 * n_rb + 1]
        land, send, recv = rest[2 * n_rb + 1:2 * n_rb + 4]
        r_sems = rest[2 * n_rb + 4:]
        if ride is not None:
            ride.start([], r_bufs, [], r_sems)
        x, y, c, p, _ = _position()
        me = 2 * p + c
        land[me] = x_ref[...]
        peers = [(dx, dy, dc) for dx in range(2) for dy in range(2) for dc in range(2) if (dx, dy, dc) != (0, 0, 0)]
        for j, (dx, dy, dc) in enumerate(peers):
            _remote(land.at[me], land.at[me], send.at[j], recv.at[j], (x ^ dx, y ^ dy, c ^ dc)).start()
        for j, (dx, dy, dc) in enumerate(peers):
            src = 4 * (x ^ dx) + 2 * (y ^ dy) + (c ^ dc)
            _remote(land.at[me], land.at[me], send.at[j], recv.at[j], (x ^ dx, y ^ dy, c ^ dc)).wait_send()
            _remote(land.at[src], land.at[src], send.at[j], recv.at[j], (x ^ dx, y ^ dy, c ^ dc)).wait_recv()
        acc = land[0]
        for dev in range(1, n_dev):
            acc = acc + land[dev]
        o_ref[...] = acc
        if ride is not None:
            ride.finish([], r_bufs, [], r_sems)

    bufs = [] if ride is None else ride.bufs
    sems = [] if ride is None else [pltpu.SemaphoreType.DMA(s) for s in ride.sem_shapes]
    res = pl.pallas_call(
        body, in_specs=[pl.BlockSpec(memory_space=pltpu.VMEM)] + [ANY] * n_rb,
        out_specs=[pl.BlockSpec(memory_space=pltpu.VMEM)] + [ANY] * n_rb,
        out_shape=[SDS((rows, LANES), F32)] + [SDS(b.shape, b.dtype) for b in bufs],
        scratch_shapes=[pltpu.VMEM((n_dev, rows, LANES), F32), pltpu.SemaphoreType.DMA((n_dev - 1,)),
                        pltpu.SemaphoreType.DMA((n_dev - 1,))] + sems,
        input_output_aliases={1 + j: 1 + j for j in range(n_rb)},
        name="all_reduce_small", compiler_params=_params())(pack, *bufs)
    return res[0], list(res[1:])


def _add_own_half(name, full, recv, out_dtype):
    n4, _, h, cols = full.shape

    def body(s_ref, a_ref, b_ref, o_ref, own_ref):
        v = (a_ref[...].astype(F32) + b_ref[...].astype(F32)).astype(out_dtype)
        o_ref[...] = v

        @pl.when(pl.program_id(0) == s_ref[1])
        def _():
            own_ref[...] = v

    return pl.pallas_call(
        body,
        grid_spec=pltpu.PrefetchScalarGridSpec(
            num_scalar_prefetch=1, grid=(n4,),
            in_specs=[pl.BlockSpec((None, None, h, cols), lambda q, s: (q, s[0], 0, 0)),
                      pl.BlockSpec((None, h, cols), lambda q, s: (q, 0, 0))],
            out_specs=[pl.BlockSpec((None, h, cols), lambda q, s: (q, 0, 0)),
                       pl.BlockSpec((None, h, cols), lambda q, s: (s[1], 0, 0))]),
        out_shape=[SDS((n4, h, cols), out_dtype)] * 2, name=name, compiler_params=_params())(_mesh_scalars(), full, recv)


def _sum_chips(name, parts):
    n4, h, cols = parts.shape
    th = h // 4 if h % 64 == 0 else h

    def body(s_ref, a_ref, o_ref):
        acc = a_ref[0].astype(F32)
        for q in range(1, n4):
            acc = acc + a_ref[q].astype(F32)
        o_ref[...] = acc

    return pl.pallas_call(
        body,
        grid_spec=pltpu.PrefetchScalarGridSpec(
            num_scalar_prefetch=1, grid=(h // th,),
            in_specs=[pl.BlockSpec((n4, th, cols), lambda i, s: (0, i, 0))],
            out_specs=pl.BlockSpec((None, th, cols), lambda i, s: (s[0], i, 0))),
        out_shape=SDS((2, h, cols), F32), name=name, compiler_params=_params())(_mesh_scalars(), parts)


def _adamw_math(w, g, m, v):
    m2 = ADAM_B1 * m + (1.0 - ADAM_B1) * g
    v2 = ADAM_B2 * v + (1.0 - ADAM_B2) * (g * g)
    m_hat = m2 / (1.0 - ADAM_B1 ** ADAM_STEP)
    v_hat = v2 / (1.0 - ADAM_B2 ** ADAM_STEP)
    delta = -ADAM_LR * (m_hat / (jnp.sqrt(v_hat) + ADAM_EPS) + ADAM_WD * w)
    return delta, m2, v2


def _row_tile(rows, cols):
    cap = max(8, (1 << 18) // cols)
    best = 8
    for cand in range(8, min(rows, cap) + 1, 8):
        if rows % cand == 0:
            best = cand
    return best


def _adamw_big(name, w, g_layers, m, v):
    layers, rows, cols = w.shape
    tr = _row_tile(rows, cols)

    def body(w_ref, m_ref, v_ref, *rest):
        g_refs, (g_o, d_o, m_o, v_o) = rest[:layers], rest[layers:]
        gv = g_refs[0][...]
        for layer in range(1, layers):
            gv = jnp.where(pl.program_id(0) == layer, g_refs[layer][...], gv)
        d, mm, vv = _adamw_math(w_ref[...], gv, m_ref[...], v_ref[...])
        g_o[...] = gv
        d_o[...] = d
        m_o[...] = mm
        v_o[...] = vv

    blk = pl.BlockSpec((None, tr, cols), lambda l, i: (l, i, 0))
    g_blk = pl.BlockSpec((tr, cols), lambda l, i: (i, 0))
    return tuple(pl.pallas_call(
        body, grid=(layers, rows // tr), in_specs=[blk] * 3 + [g_blk] * layers, out_specs=[blk] * 4,
        out_shape=[SDS((layers, rows, cols), F32)] * 4, name=name,
        compiler_params=_params())(w, m, v, *[g.reshape(rows, cols) for g in g_layers]))


def _adamw_small(ws, gs, ms, vs):
    n = len(ws)
    flat = []
    for group in (ws, gs, ms, vs):
        flat += [a.reshape(-1, a.shape[-1]) for a in group]

    def body(*refs):
        w_r, g_r, m_r, v_r = refs[:n], refs[n:2 * n], refs[2 * n:3 * n], refs[3 * n:4 * n]
        d_o, m_o, v_o = refs[4 * n:5 * n], refs[5 * n:6 * n], refs[6 * n:7 * n]
        for j in range(n):
            d, mm, vv = _adamw_math(w_r[j][...], g_r[j][...], m_r[j][...], v_r[j][...])
            d_o[j][...] = d
            m_o[j][...] = mm
            v_o[j][...] = vv

    shapes = [SDS(a.shape, F32) for a in flat[:n]]
    outs = pl.pallas_call(body, out_shape=shapes * 3, name="adamw_small", compiler_params=_params())(*flat)
    res = []
    for k in range(3):
        res.append([outs[k * n + j].reshape(ws[j].shape) for j in range(n)])
    return res


BIG = ("ab_w_in", "ab_w_out", "cd_w_in", "cd_w_out", "ffn_w_gate", "ffn_w_up", "ffn_w_down")
V_BLOCK = (2 * A_WIDTH + QK_COLS) // B_WIDTH


def _pad_rows(a, rows):
    return jnp.pad(a, ((0, rows - a.shape[0]), (0, 0)))


A_IN, A_OUT, C_IN, C_OUT = ("ab_w_in", 0), ("ab_w_out", 0), ("cd_w_in", 0), ("cd_w_out", 0)
G0, U0, D0 = ("ffn_w_gate", 0), ("ffn_w_up", 0), ("ffn_w_down", 0)
G1, U1, D1 = ("ffn_w_gate", 1), ("ffn_w_up", 1), ("ffn_w_down", 1)
UNITS = (A_IN, A_OUT, G0, U0, D0, C_IN, C_OUT, G1, U1, D1)
ROWS_MINOR = ("ffn_w_gate", "ffn_w_up")
SMALL_SHARDED = ("small", 0)
REPLICATED_UNIT = ("replicated", 0)


class _Exchange:
    def __init__(self, enabled):
        self.enabled = enabled
        self.w, self.grad, self.recv, self.half, self.land, self.done = {}, {}, {}, {}, {}, {}

    def full(self, unit):
        b = self.w[unit]
        return b.reshape(N_CHIPS, 1, 2 * b.shape[2], b.shape[3])

    def _ride(self, phases):
        rides, sinks = [], []
        for kind, units in phases:
            if kind == "send":
                rides.append(_ride_gather_send([self.w[u] for u in units]))
                sinks.append(self.w)
            elif kind == "pass":
                rides.append(_ride_gather_pass([self.w[u] for u in units]))
                sinks.append(self.w)
            elif kind == "gather":
                rides.append(_ride_gather([self.w[u] for u in units]))
                sinks.append(self.w)
            elif kind == "swap":
                rides.append(_ride_swap([self.grad[u] for u in units]))
                sinks.append(self.recv)
            elif kind == "scatter":
                rides.append(_ride_scatter([self.half[u] for u in units], [self.land[u] for u in units]))
                sinks.append(self.land)
            else:
                rides.append(_ride_join([self.done[u] for u in units]))
                sinks.append(self.done)
        ride = functools.reduce(_ride_both, rides)

        def settle(res):
            n_bufs = sum(len(r.bufs) for r in rides)
            bufs, new = list(res[:n_bufs]), list(res[n_bufs:])
            for r, sink, (_, units) in zip(rides, sinks, phases):
                vals = [bufs.pop(0) for _ in r.bufs] + [new.pop(0) for _ in r.new_outs]
                for u, v in zip(units, vals):
                    sink[u] = v

        return ride, settle

    def run(self, fn, *args, phases=(), **kw):
        if not self.enabled or not phases:
            return fn(*args, **kw)
        ride, settle = self._ride(phases)
        out, res = fn(*args, ride=ride, **kw)
        settle(res)
        return out

    def alone(self, name, phases):
        if self.enabled:
            ride, settle = self._ride(phases)
            settle(_run_ride(name, ride))

    def pair_sum(self, units):
        if self.enabled:
            for u in units:
                dtype = F32 if u in (SMALL_SHARDED, REPLICATED_UNIT) else BF16
                self.half[u], self.land[u] = _add_own_half(f"pair_sum_{u[0]}_{u[1]}", self.grad[u], self.recv[u], dtype)

    def chip_sum(self, units):
        if self.enabled:
            for u in units:
                self.done[u] = _sum_chips(f"chip_sum_{u[0]}_{u[1]}", self.land[u])


def _local_step(x, target, ex, sp):
    t, d = x.shape
    tabs = _rope_tables(t)
    gains = jnp.concatenate([jnp.tile(sp["q_norm_g"][g], HEAD_DIM // 8) for g in range(N_DIL)]
                            + [jnp.tile(sp["k_norm_g"][g], HEAD_DIM // 8) for g in range(N_DIL)]).reshape(1, QK_COLS)
    bias_t = sp["sgu_bias"].T
    cw = _pad_rows(sp["conv_c_w"], 32)
    dw = _pad_rows(sp["conv_d_w"], 8)
    cb, clg, clb = (sp[k].reshape(1, C_WIDTH) for k in ("conv_c_b", "c_ln_g", "c_ln_b"))
    slg, slb = sp["sgu_norm_g"].reshape(1, A_WIDTH), sp["sgu_norm_b"].reshape(1, A_WIDTH)
    g_ab, g_cd = sp["ab_norm_g"].reshape(1, d), sp["cd_norm_g"].reshape(1, d)
    g_f0, g_f1 = sp["ffn_norm_g"][0:1], sp["ffn_norm_g"][1:2]
    run = ex.run

    def w2d(unit):
        return ex.full(unit).reshape(-1, d)

    h0 = _rms_fwd("rms_ab", x, g_ab)
    proj = run(_proj_in, "proj_ab", h0, ex.full(A_IN), 0, phases=[("send", [A_OUT, G0])])
    a_out = _mixer_a_fwd(proj, slg, slb, sp["sgu_w"], bias_t)
    qk, q1, q2, k1, k2 = run(_qk_fwd, proj, gains, tabs, phases=[("pass", [A_OUT, G0]), ("send", [U0])])
    regrouped_qk = {1: (q1, k1), 2: (q2, k2)}
    fwd_phases = ([("pass", [U0]), ("send", [D0])], [("pass", [D0]), ("send", [C_IN])],
                  [("pass", [C_IN]), ("send", [C_OUT])])
    qkv, o_list, l_list = [], [], []
    for g, rate in enumerate(DIL_RATES):
        if rate == 1:
            qk3, proj3 = qk.reshape(1, t, QK_COLS), proj.reshape(1, t, AB_IN)
            q, k, v = (qk3, g), (qk3, N_DIL + g), (proj3, V_BLOCK + g)
        else:
            vp, = _permute(f"regroup_v_{g}", [(proj, V_BLOCK + g)], rate)
            q, k, v = (regrouped_qk[g][0], 0), (regrouped_qk[g][1], 0), (vp, 0)
        qkv.append((q, k, v))
        o, l = run(_attn_fwd, f"attn_fwd_{g}", q, k, v, phases=fwd_phases[g])
        if rate == 1:
            o, l = o.reshape(t, B_WIDTH), l.reshape(t, B_WIDTH)
        o_list.append(o)
        l_list.append(l)
    cat, lse_tot, lse_1, lse_2 = _attn_merge(a_out, o_list, l_list)
    x1, hf0 = _proj_out("out_ab", cat, w2d(A_OUT), x, g_next=g_f0)
    fgate0, fup0, act0 = run(_ffn_in, "ffn_in_0", hf0, ex.full(G0), ex.full(U0), 0,
                           phases=[("pass", [C_OUT]), ("send", [G1, U1])])
    x2, h1 = run(_ffn_out, "ffn_out_0", act0, ex.full(D0), 0, x1, g_next=g_cd, phases=[("pass", [G1, U1]), ("send", [D1])])
    projcd = run(_proj_in, "proj_cd", h1, ex.full(C_IN), 0, phases=[("pass", [D1])])
    cat2, c1 = _mixer_cd_fwd(projcd, cw, cb, clg, clb, dw)
    x3, hf1 = _proj_out("out_cd", cat2, w2d(C_OUT), x2, g_next=g_f1)
    fgate1, fup1, act1 = _ffn_in("ffn_in_1", hf1, ex.full(G1), ex.full(U1), 0)
    dy, loss_acc, dy_b = _ffn_out("ffn_out_1", act1, ex.full(D1), 0, x3, target=target)
    loss = 0.5 * loss_acc[0, 0] / d

    late = [D1, G1, U1]
    dgate, dup = _ffn_dact("ffn_dact_1", dy_b, ex.full(D1), 0, fgate1, fup1)
    ex.grad[D1] = _wgrad_row_sharded("wgrad_down_1", act1, dy_b, True)
    ex.grad[G1] = _wgrad_row_sharded("wgrad_gate_1", dgate, hf1, True)
    ex.grad[U1] = _wgrad_row_sharded("wgrad_up_1", dup, hf1, True)
    g3, d_f1, g3_b = run(_dgrad_cols, "dgrad_ffn_1", [dgate, dup], [ex.full(G1), ex.full(U1)], 0, True, x3, g_f1, dy,
                         w_rows=True, phases=[("swap", late)])
    ex.pair_sum(late)

    dcat2 = _dgrad_rows("dgrad_out_cd", g3_b, w2d(C_OUT))
    ex.grad[C_OUT] = _wgrad_row_sharded("wgrad_out_cd", cat2, g3_b, False)
    dprojcd, d_cw, d_cb, d_clg, d_clb, d_dw = run(_mixer_cd_bwd, projcd, dcat2, c1, cw, clg, clb, dw, phases=[("scatter", late)])
    ex.chip_sum(late)
    ex.grad[C_IN] = _wgrad_col_sharded("wgrad_in_cd", h1, [dprojcd], False)[0]
    g2, d_cdn, g2_b = run(_dgrad_cols, "dgrad_in_cd", [dprojcd], [ex.full(C_IN)], 0, False, x2, g_cd, g3,
                          phases=[("join", late), ("swap", [C_OUT, C_IN])])
    ex.pair_sum([C_OUT, C_IN])

    dgate, dup = run(_ffn_dact, "ffn_dact_0", g2_b, ex.full(D0), 0, fgate0, fup0, phases=[("scatter", [C_OUT, C_IN])])
    ex.chip_sum([C_OUT, C_IN])
    ex.grad[D0] = _wgrad_row_sharded("wgrad_down_0", act0, g2_b, True)
    ex.grad[G0] = _wgrad_row_sharded("wgrad_gate_0", dgate, hf0, True)
    ex.grad[U0] = _wgrad_row_sharded("wgrad_up_0", dup, hf0, True)
    small = {"cd_norm_g": d_cdn, "conv_c_w": d_cw[:C_KERNEL], "conv_c_b": d_cb, "c_ln_g": d_clg, "c_ln_b": d_clb,
             "conv_d_w": d_dw[:D_KERNEL]}
    ex.grad[SMALL_SHARDED] = _split_full_small(small).reshape(N_CHIPS, 2, SHARDED_ROWS // 2, LANES)
    mid = [D0, G0, U0, SMALL_SHARDED]
    g1, d_f0, g1_b = run(_dgrad_cols, "dgrad_ffn_0", [dgate, dup], [ex.full(G0), ex.full(U0)], 0, True, x1, g_f0, g2,
                         w_rows=True, phases=[("join", [C_OUT, C_IN]), ("swap", mid)])
    ex.pair_sum(mid)

    dcat = _dgrad_rows("dgrad_out_ab", g1_b, w2d(A_OUT))
    ex.grad[A_OUT] = _wgrad_row_sharded("wgrad_out_ab", cat, g1_b, False)
    d_a, d_sw, d_sbt, d_slg, d_slb = _mixer_a_bwd(proj, dcat, slg, slb, sp["sgu_w"], bias_t)
    early = {"sgu_norm_g": d_slg, "sgu_norm_b": d_slb, "sgu_w": d_sw, "sgu_bias": d_sbt.T}
    ex.grad[REPLICATED_UNIT] = jnp.broadcast_to(
        _pack_replicated(early, REPLICATED_EARLY, REPLICATED_EARLY_ROWS).reshape(2, REPLICATED_EARLY_ROWS // 2, LANES),
        (N_CHIPS, 2, REPLICATED_EARLY_ROWS // 2, LANES))
    last = [A_OUT, REPLICATED_UNIT]
    dbb, dd, db_1, dd_1, db_2, dd_2 = _attn_bwd_prep(dcat, cat)
    regrouped_bwd = {1: (db_1, lse_1, dd_1), 2: (db_2, lse_2, dd_2)}
    bwd_phases = ([("scatter", [D0, SMALL_SHARDED])],
                  [("scatter", [G0]), ("join", [D0, SMALL_SHARDED]), ("swap", last)],
                  [("scatter", [U0]), ("join", [G0])])
    dqs, dks, dvs = [], [], []
    for g, rate in enumerate(DIL_RATES):
        q, k, v = qkv[g]
        if rate == 1:
            db3, l3, dd3 = (a.reshape(1, t, B_WIDTH) for a in (dbb, lse_tot, dd))
        else:
            db3, l3, dd3 = regrouped_bwd[g]
        if g == 1:
            ex.chip_sum([D0, SMALL_SHARDED])
        elif g == 2:
            ex.chip_sum([G0])
            ex.pair_sum(last)
        dq, dk, dv = run(_attn_bwd, f"attn_bwd_{g}", q, k, v, db3, l3, dd3, phases=bwd_phases[g])
        if rate == 1:
            dq, dk, dv = (a.reshape(t, B_WIDTH) for a in (dq, dk, dv))
        dqs.append(dq)
        dks.append(dk)
        dvs.append(dv)
    ex.chip_sum([U0])
    dproj, d_gains = run(_dproj_assemble, proj, d_a, dqs, dks, dvs, gains, tabs, phases=[("scatter", last), ("join", [U0])])
    ex.chip_sum(last)
    d_gains = _fold_heads(d_gains)[0].reshape(2, N_DIL, B_WIDTH)[:, :, :HEAD_DIM]
    ex.grad[A_IN] = _wgrad_col_sharded("wgrad_in_ab", h0, [dproj], False)[0]
    ex.alone("swap_last", [("swap", [A_IN])])
    ex.pair_sum([A_IN])
    gx, d_abn = run(_dgrad_cols, "dgrad_in_ab", [dproj], [ex.full(A_IN)], 0, False, x, g_ab, g1, bf16_copy=False,
                    phases=[("join", last), ("scatter", [A_IN])])
    ex.chip_sum([A_IN])

    small.update({
        "ab_norm_g": d_abn, "sgu_norm_g": d_slg, "sgu_norm_b": d_slb, "sgu_w": d_sw, "sgu_bias": d_sbt.T,
        "q_norm_g": d_gains[0], "k_norm_g": d_gains[1], "ffn_norm_g": jnp.concatenate([d_f0, d_f1], axis=0),
    })
    return loss, gx, small


SHARDED_SMALL = ("cd_norm_g", "conv_c_w", "conv_c_b", "c_ln_g", "c_ln_b", "conv_d_w")
SHARDED_ROWS = 48
REPLICATED_EARLY = ("sgu_norm_g", "sgu_norm_b", "sgu_w", "sgu_bias")
REPLICATED_EARLY_ROWS = 528
REPLICATED_LATE = ("ab_norm_g", "q_norm_g", "k_norm_g", "ffn_norm_g", "loss")
REPLICATED_LATE_ROWS = 32
REPLICATED_SMALL = REPLICATED_EARLY + REPLICATED_LATE[:-1]


def _pack_sharded(parts):
    rows = [parts[k].reshape(-1, LANES) for k in SHARDED_SMALL]
    return _pad_rows(jnp.concatenate(rows, axis=0), SHARDED_ROWS)


def _split_full_small(small):
    per_chip = []
    for q in range(N_CHIPS):
        parts = {}
        for k in SHARDED_SMALL:
            a = small[k]
            a = a.reshape(-1, a.shape[-1])
            n = a.shape[-1] // N_CHIPS
            parts[k] = a[:, q * n:(q + 1) * n]
        per_chip.append(_pack_sharded(parts))
    return jnp.stack(per_chip)


def _unpack_sharded(pack, shapes):
    out, r = {}, 0
    for k in SHARDED_SMALL:
        n = math.prod(shapes[k]) // LANES
        out[k] = pack[r:r + n].reshape(shapes[k])
        r += n
    return out


def _gathered_small(packs, shapes):
    per_chip = [_unpack_sharded(packs[q], shapes) for q in range(N_CHIPS)]
    return {k: jnp.concatenate([pc[k] for pc in per_chip], axis=-1) for k in SHARDED_SMALL}


def _pack_replicated(small, names, total_rows):
    rows = []
    for k in names:
        a = small[k].reshape(-1)
        a = jnp.pad(a, (0, (-a.shape[0]) % LANES))
        rows.append(a.reshape(-1, LANES))
    return _pad_rows(jnp.concatenate(rows, axis=0), total_rows)


def _unpack_replicated(pack, shapes, names):
    out, r = {}, 0
    for k in names:
        size = math.prod(shapes[k])
        n = -(-size // LANES)
        out[k] = pack[r:r + n].reshape(-1)[:size].reshape(shapes[k])
        r += n
    return out


WEIGHT_ORDER = ("ab_norm_g", "ab_w_in", "sgu_norm_g", "sgu_norm_b", "sgu_w", "sgu_bias", "q_norm_g", "k_norm_g", "ab_w_out",
                "cd_norm_g", "cd_w_in", "conv_c_w", "conv_c_b", "c_ln_g", "c_ln_b", "conv_d_w", "cd_w_out", "ffn_norm_g",
                "ffn_w_gate", "ffn_w_up", "ffn_w_down")


def kernel(x, ab_norm_g, ab_w_in, sgu_norm_g, sgu_norm_b, sgu_w, sgu_bias, q_norm_g, k_norm_g, ab_w_out, cd_norm_g, cd_w_in, conv_c_w, conv_c_b, c_ln_g, c_ln_b, conv_d_w, cd_w_out, ffn_norm_g, ffn_w_gate, ffn_w_up, ffn_w_down, loss_target, m_ab_norm_g, m_ab_w_in, m_sgu_norm_g, m_sgu_norm_b, m_sgu_w, m_sgu_bias, m_q_norm_g, m_k_norm_g, m_ab_w_out, m_cd_norm_g, m_cd_w_in, m_conv_c_w, m_conv_c_b, m_c_ln_g, m_c_ln_b, m_conv_d_w, m_cd_w_out, m_ffn_norm_g, m_ffn_w_gate, m_ffn_w_up, m_ffn_w_down, v_ab_norm_g, v_ab_w_in, v_sgu_norm_g, v_sgu_norm_b, v_sgu_w, v_sgu_bias, v_q_norm_g, v_k_norm_g, v_ab_w_out, v_cd_norm_g, v_cd_w_in, v_conv_c_w, v_conv_c_b, v_c_ln_g, v_c_ln_b, v_conv_d_w, v_cd_w_out, v_ffn_norm_g, v_ffn_w_gate, v_ffn_w_up, v_ffn_w_down):
    args = dict(locals())
    ws = {k: args[k] for k in WEIGHT_ORDER}
    ms = {k: args["m_" + k] for k in WEIGHT_ORDER}
    vs = {k: args["v_" + k] for k in WEIGHT_ORDER}
    small_names = [k for k in WEIGHT_ORDER if k not in BIG]
    t, d = x.shape[1:]

    for group in (ws, ms, vs):
        for k in ROWS_MINOR:
            group[k] = jnp.swapaxes(group[k], 1, 2)
    ex = _Exchange(enabled=True)
    for name, layer in UNITS:
        ex.w[(name, layer)] = _stage_own(f"stage_{name}_{layer}", ws[name], layer, BF16)
    own_small = _pack_sharded({k: ws[k][0] for k in SHARDED_SMALL})
    ex.w[SMALL_SHARDED] = _stage_own("stage_small", own_small[None], 0, F32)
    ex.alone("gather_first", [("gather", [A_IN, SMALL_SHARDED])])
    sp = _gathered_small(ex.w[SMALL_SHARDED].reshape(N_CHIPS, SHARDED_ROWS, LANES), {k: ws[k].shape[1:] for k in SHARDED_SMALL})
    for k in REPLICATED_SMALL:
        sp[k] = ws[k] if k == "ffn_norm_g" else ws[k][0]

    loss, grad_x, g_small = _local_step(x.reshape(t, d), loss_target.reshape(t, d), ex, sp)

    shapes = {k: ws[k].shape for k in REPLICATED_SMALL}
    shapes["loss"] = (1,)
    g_small["loss"] = loss
    join_last, settle = ex._ride([("join", [A_IN])])
    late, joined = _all_reduce_small(_pack_replicated(g_small, REPLICATED_LATE, REPLICATED_LATE_ROWS), join_last)
    settle(joined)
    grad = _unpack_sharded(ex.done[SMALL_SHARDED].reshape(SHARDED_ROWS, LANES), {k: ws[k].shape for k in SHARDED_SMALL})
    grad.update(_unpack_replicated(ex.done[REPLICATED_UNIT].reshape(REPLICATED_EARLY_ROWS, LANES), shapes, REPLICATED_EARLY))
    grad.update(_unpack_replicated(late, shapes, REPLICATED_LATE))
    loss = grad.pop("loss")[0]

    delta, new_m, new_v = {}, {}, {}
    for k in BIG:
        g_layers = [ex.done[(k, layer)] for layer in range(ws[k].shape[0])]
        outs = _adamw_big("adamw_" + k, ws[k], g_layers, ms[k], vs[k])
        if k in ROWS_MINOR:
            outs = [jnp.swapaxes(o, 1, 2) for o in outs]
        grad[k], delta[k], new_m[k], new_v[k] = outs
    d_s, m_s, v_s = _adamw_small([ws[k] for k in small_names], [grad[k] for k in small_names],
                                 [ms[k] for k in small_names], [vs[k] for k in small_names])
    for j, k in enumerate(small_names):
        delta[k], new_m[k], new_v[k] = d_s[j], m_s[j], v_s[j]

    return (loss, grad_x[None], *[grad[k] for k in WEIGHT_ORDER], *[delta[k] for k in WEIGHT_ORDER],
            *[new_m[k] for k in WEIGHT_ORDER], *[new_v[k] for k in WEIGHT_ORDER])
```

```python
import functools
import math

import jax
import jax.numpy as jnp
from jax import lax
from jax.experimental import pallas as pl
from jax.experimental.pallas import tpu as pltpu

F32 = jnp.float32
BF16 = jnp.bfloat16
SDS = jax.ShapeDtypeStruct

N_CHIPS = 4
EPS = 1e-6
NEG_INF = -1e30
CHUNK = 128
A_GROUPS = 4
A_WIDTH = 512
N_DIL = 3
DIL_RATES = (1, 4, 16)
HEAD_DIM = 64
B_WIDTH = 512
ROPE_DIM = 16
ROPE_THETA = 500000.0
C_WIDTH = 512
C_KERNEL = 31
D_KERNEL = 3
HALO = 32
ATT_BLOCK = 128
LANES = 128

ADAM_LR = 0.001
ADAM_B1 = 0.9
ADAM_B2 = 0.999
ADAM_EPS = 1e-08
ADAM_WD = 0.01
ADAM_STEP = 10

VMEM_LIMIT = 56 * 1024 * 1024

NN = (((1,), (0,)), ((), ()))
NT = (((1,), (1,)), ((), ()))
TN = (((0,), (0,)), ((), ()))

TILES = {"proj_in": 1024, "proj_out": 1024, "ffn_in": 1024, "ffn_out": 512, "ffn_dact": 512, "dgrad_cols": 512,
         "dgrad_rows": 1024, "wgrad": 4096}


def _params(sem=None):
    return pltpu.CompilerParams(dimension_semantics=sem, vmem_limit_bytes=VMEM_LIMIT)


def _bf(v):
    return v if v.dtype == BF16 else v.astype(BF16)


def _dot(a, b, dims):
    return lax.dot_general(_bf(a), _bf(b), dims, preferred_element_type=F32)


def _dot_hi(a, b):
    return jnp.dot(a, b, precision=lax.Precision.HIGHEST, preferred_element_type=F32)


def _sigmoid(v):
    return 0.5 * jnp.tanh(0.5 * v) + 0.5


def _gelu(v):
    return 0.5 * v * (1.0 + lax.erf(v * (1.0 / math.sqrt(2.0))))


def _gelu_grad(v):
    cdf = 0.5 * (1.0 + lax.erf(v * (1.0 / math.sqrt(2.0))))
    return cdf + v * jnp.exp(-0.5 * v * v) * (1.0 / math.sqrt(2.0 * math.pi))


def _segment_mean_matrix(seg, scale=None):
    r = lax.broadcasted_iota(jnp.int32, (LANES, LANES), 0) // seg
    c = lax.broadcasted_iota(jnp.int32, (LANES, LANES), 1) // seg
    return jnp.where(r == c, (1.0 / seg) if scale is None else scale, 0.0).astype(BF16)


def _segment_dot(v, seg):
    hi = v.astype(BF16)
    lo = (v - hi.astype(F32)).astype(BF16)
    return jnp.dot(hi, seg, preferred_element_type=F32) + jnp.dot(lo, seg, preferred_element_type=F32)


MESH = pl.DeviceIdType.MESH
ANY = pl.BlockSpec(memory_space=pl.ANY)


def _position():
    x, y, c = lax.axis_index("x"), lax.axis_index("y"), lax.axis_index("c")
    others = [(1 - x, y), (x, 1 - y), (1 - x, 1 - y)]
    return x, y, c, 2 * x + y, others


class _Ride:
    def __init__(self, ins, bufs, new_outs, sem_shapes, start, finish):
        self.ins, self.bufs, self.new_outs, self.sem_shapes = list(ins), list(bufs), list(new_outs), list(sem_shapes)
        self.start, self.finish = start, finish


def _ride_both(a, b):
    na = (len(a.ins), len(a.bufs), len(a.new_outs), len(a.sem_shapes))

    def split(ins, bufs, new, sems):
        return ((ins[:na[0]], bufs[:na[1]], new[:na[2]], sems[:na[3]]), (ins[na[0]:], bufs[na[1]:], new[na[2]:], sems[na[3]:]))

    def start(*refs):
        ra, rb = split(*refs)
        a.start(*ra)
        b.start(*rb)

    def finish(*refs):
        ra, rb = split(*refs)
        a.finish(*ra)
        b.finish(*rb)

    return _Ride(a.ins + b.ins, a.bufs + b.bufs, a.new_outs + b.new_outs, a.sem_shapes + b.sem_shapes, start, finish)


def _call(body, *, grid, in_specs, out_specs, out_shape, operands, name, scratch_shapes=(), aliases=None, ride=None,
          prefetch=None):
    off = 0 if prefetch is None else 1
    lead = [] if prefetch is None else [prefetch]

    def launch(kernel_body, in_specs_, out_specs_, out_shape_, scratch_, aliases_, *args):
        if prefetch is None:
            return pl.pallas_call(kernel_body, grid=grid, in_specs=in_specs_, out_specs=out_specs_, out_shape=out_shape_,
                                  scratch_shapes=scratch_, input_output_aliases=aliases_, name=name,
                                  compiler_params=_params())(*args)
        spec = pltpu.PrefetchScalarGridSpec(num_scalar_prefetch=1, grid=grid, in_specs=in_specs_, out_specs=out_specs_,
                                            scratch_shapes=scratch_)
        return pl.pallas_call(kernel_body, grid_spec=spec, out_shape=out_shape_, input_output_aliases=aliases_, name=name,
                              compiler_params=_params())(*lead, *args)

    if ride is None:
        return launch(body, list(in_specs), out_specs, out_shape, list(scratch_shapes), dict(aliases or {}), *operands)
    multi = isinstance(out_shape, (list, tuple))
    out_shapes = list(out_shape) if multi else [out_shape]
    o_specs = list(out_specs) if multi else [out_specs]
    n_in, n_out, n_scr = off + len(operands), len(out_shapes), len(scratch_shapes)
    n_ri, n_rb, n_rn = len(ride.ins), len(ride.bufs), len(ride.new_outs)

    def carrying(*refs):
        k = n_in
        r_ins = refs[k:k + n_ri]
        k += n_ri + n_rb
        outs = refs[k:k + n_out]
        k += n_out
        r_bufs = refs[k:k + n_rb]
        k += n_rb
        r_new = refs[k:k + n_rn]
        k += n_rn
        scratch = refs[k:k + n_scr]
        sems = refs[k + n_scr:]
        first, last = None, None
        for axis, size in enumerate(grid):
            pid = pl.program_id(axis)
            first = (pid == 0) if first is None else first & (pid == 0)
            last = (pid == size - 1) if last is None else last & (pid == size - 1)

        @pl.when(first)
        def _():
            ride.start(r_ins, r_bufs, r_new, sems)

        body(*refs[:n_in], *outs, *scratch)

        @pl.when(last)
        def _():
            ride.finish(r_ins, r_bufs, r_new, sems)

    all_aliases = dict(aliases or {})
    for j in range(n_rb):
        all_aliases[n_in + n_ri + j] = n_out + j
    res = launch(
        carrying, list(in_specs) + [ANY] * (n_ri + n_rb), o_specs + [ANY] * (n_rb + n_rn),
        out_shapes + [SDS(b.shape, b.dtype) for b in ride.bufs] + ride.new_outs,
        list(scratch_shapes) + [pltpu.SemaphoreType.DMA(s) for s in ride.sem_shapes], all_aliases,
        *operands, *ride.ins, *ride.bufs)
    outs = res[:n_out]
    return (list(outs) if multi else outs[0]), list(res[n_out:])


def _run_ride(name, ride):
    n_ri, n_rb, n_rn = len(ride.ins), len(ride.bufs), len(ride.new_outs)

    def body(*refs):
        r_ins = refs[:n_ri]
        r_bufs = refs[n_ri + n_rb:n_ri + 2 * n_rb]
        r_new = refs[n_ri + 2 * n_rb:n_ri + 2 * n_rb + n_rn]
        sems = refs[n_ri + 2 * n_rb + n_rn:]
        ride.start(r_ins, r_bufs, r_new, sems)
        ride.finish(r_ins, r_bufs, r_new, sems)

    return list(pl.pallas_call(
        body, in_specs=[ANY] * (n_ri + n_rb), out_specs=[ANY] * (n_rb + n_rn),
        out_shape=[SDS(b.shape, b.dtype) for b in ride.bufs] + ride.new_outs,
        scratch_shapes=[pltpu.SemaphoreType.DMA(s) for s in ride.sem_shapes],
        input_output_aliases={n_ri + j: j for j in range(n_rb)}, name=name)(*ride.ins, *ride.bufs))


def _whole(ref, p):
    return ref[...]


def _slab(ref, p):
    return ref[p]


def _matmul(name, grid, pairs, extras, outs, dims, epi, *, slabs=1, n_acc=1, ride=None):
    n_pairs, n_ex, n_out = len(pairs), len(extras), len(outs)

    def body(*refs):
        ab = refs[:2 * n_pairs]
        ex = refs[2 * n_pairs:2 * n_pairs + n_ex]
        out_refs = refs[2 * n_pairs + n_ex:2 * n_pairs + n_ex + n_out]
        pids = tuple(pl.program_id(a) for a in range(len(grid)))
        parts = [None] * n_acc
        for p in range(slabs):
            for j, (_, _, a_pick, _, _, b_pick, acc) in enumerate(pairs):
                d = _dot(a_pick(ab[2 * j], p), b_pick(ab[2 * j + 1], p), dims)
                parts[acc] = d if parts[acc] is None else parts[acc] + d
        epi(parts, ex, out_refs, pids)

    operands, in_specs = [], []
    for a, a_spec, _, b, b_spec, _, _ in pairs:
        operands += [a, b]
        in_specs += [a_spec, b_spec]
    for e, e_spec in extras:
        operands.append(e)
        in_specs.append(e_spec)
    return _call(body, grid=grid, in_specs=in_specs, out_specs=[o[1] for o in outs], out_shape=[o[0] for o in outs],
                 operands=operands, name=name, ride=ride)


def _rms_rows(v, g):
    r = lax.rsqrt(jnp.mean(v * v, axis=-1, keepdims=True) + EPS)
    return v * r * g


def _rms_fwd(name, x, g):
    t, d = x.shape
    tm = 512

    def body(x_ref, g_ref, o_ref):
        o_ref[...] = _rms_rows(x_ref[...], g_ref[...]).astype(BF16)

    return pl.pallas_call(
        body, grid=(t // tm,),
        in_specs=[pl.BlockSpec((tm, d), lambda i: (i, 0)), pl.BlockSpec((1, d), lambda i: (0, 0))],
        out_specs=pl.BlockSpec((tm, d), lambda i: (i, 0)), out_shape=SDS((t, d), BF16), name=name,
        compiler_params=_params())(x, g)


def _epi_residual_norm(accs, ex, outs, pids):
    x_new = accs[0] + ex[0][...]
    outs[0][...] = x_new
    outs[1][...] = _rms_rows(x_new, ex[1][...]).astype(BF16)


def _epi_residual_loss(accs, ex, outs, pids):
    y = accs[0] + ex[0][...]
    err = y - ex[1][...]
    dy = err * (1.0 / err.shape[-1])
    outs[0][...] = dy
    outs[2][...] = dy.astype(BF16)

    @pl.when(pids[0] == 0)
    def _():
        outs[1][...] = jnp.zeros_like(outs[1])

    outs[1][...] += jnp.sum(err * err)


def _epi_rms_bwd(accs, ex, outs, pids):
    dh = accs[0]
    xv, g, res = ex[0][...], ex[1][...], ex[2][...]
    r = lax.rsqrt(jnp.mean(xv * xv, axis=-1, keepdims=True) + EPS)
    xh = xv * r
    dy = dh * g
    dx = res + r * (dy - xh * jnp.mean(dy * xh, axis=-1, keepdims=True))
    outs[0][...] = dx
    if len(outs) > 2:
        outs[2][...] = dx.astype(BF16)

    @pl.when(pids[0] == 0)
    def _():
        outs[1][...] = jnp.zeros_like(outs[1])

    outs[1][...] += jnp.sum(dh * xh, axis=0, keepdims=True)


def _row_spec(tm, d):
    return pl.BlockSpec((tm, d), lambda i, *_: (i, 0))


def _const_spec(shape):
    nd = len(shape)
    return pl.BlockSpec(shape, lambda *_: (0,) * nd)


def _proj_in(name, h, w, layer, ride=None):
    t, d = h.shape
    n4 = w.shape[-1]
    tm = TILES["proj_in"]

    def epi(accs, ex, outs, pids):
        outs[0][...] = accs[0].astype(BF16)

    res = _matmul(
        name, (N_CHIPS, t // tm),
        [(h, pl.BlockSpec((tm, d), lambda p, i: (i, 0)), _whole,
          w, pl.BlockSpec((None, None, d, n4), lambda p, i: (p, layer, 0, 0)), _whole, 0)],
        [], [(SDS((t, N_CHIPS * n4), BF16), pl.BlockSpec((tm, n4), lambda p, i: (i, p)))],
        NN, epi, ride=ride)
    return res[0] if ride is None else (res[0][0], res[1])


def _proj_out(name, a, w, x, g_next=None, target=None):
    t, k = a.shape
    d = w.shape[-1]
    tm = TILES["proj_out"]
    if target is None:
        extras = [(x, _row_spec(tm, d)), (g_next, _const_spec((1, d)))]
        outs = [(SDS((t, d), F32), _row_spec(tm, d)), (SDS((t, d), BF16), _row_spec(tm, d))]
        epi = _epi_residual_norm
    else:
        extras = [(x, _row_spec(tm, d)), (target, _row_spec(tm, d))]
        outs = [(SDS((t, d), F32), _row_spec(tm, d)), (SDS((8, LANES), F32), _const_spec((8, LANES))),
                (SDS((t, d), BF16), _row_spec(tm, d))]
        epi = _epi_residual_loss
    return _matmul(name, (t // tm,), [(a, _row_spec(tm, k), _whole, w, _const_spec((k, d)), _whole, 0)], extras, outs, NN, epi)


def _ffn_in(name, h, wg, wu, layer, ride=None):
    t, d = h.shape
    n4 = wg.shape[-2]
    tm = TILES["ffn_in"]

    def epi(accs, ex, outs, pids):
        gate, up = accs
        s = _sigmoid(gate)
        silu = gate * s
        outs[0][...] = (up * (s + silu - silu * s)).astype(BF16)
        outs[1][...] = silu.astype(BF16)
        outs[2][...] = (silu * up).astype(BF16)

    w_spec = pl.BlockSpec((None, None, n4, d), lambda p, i: (p, layer, 0, 0))
    h_spec = pl.BlockSpec((tm, d), lambda p, i: (i, 0))
    o = (SDS((N_CHIPS, t, n4), BF16), pl.BlockSpec((None, tm, n4), lambda p, i: (p, i, 0)))
    return _matmul(name, (N_CHIPS, t // tm),
                   [(h, h_spec, _whole, wg, w_spec, _whole, 0), (h, h_spec, _whole, wu, w_spec, _whole, 1)], [],
                   [o, o, o], NT, epi, n_acc=2, ride=ride)


def _ffn_out(name, act, wd, layer, x, g_next=None, target=None, ride=None):
    _, t, n4 = act.shape
    d = wd.shape[-1]
    tm = TILES["ffn_out"]
    xs = _row_spec(tm, d)
    if target is None:
        extras = [(x, xs), (g_next, _const_spec((1, d)))]
        outs = [(SDS((t, d), F32), xs), (SDS((t, d), BF16), xs)]
        epi = _epi_residual_norm
    else:
        extras = [(x, xs), (target, xs)]
        outs = [(SDS((t, d), F32), xs), (SDS((8, LANES), F32), _const_spec((8, LANES))), (SDS((t, d), BF16), xs)]
        epi = _epi_residual_loss
    return _matmul(
        name, (t // tm,),
        [(act, pl.BlockSpec((N_CHIPS, tm, n4), lambda i: (0, i, 0)), _slab,
          wd, pl.BlockSpec((N_CHIPS, None, n4, d), lambda i: (0, layer, 0, 0)), _slab, 0)],
        extras, outs, NN, epi, slabs=N_CHIPS, ride=ride)


def _ffn_dact(name, g, wd, layer, gate, up, ride=None):
    t, d = g.shape
    n4 = wd.shape[-2]
    tm = TILES["ffn_dact"]

    def body(g_ref, w_ref, gate_ref, up_ref, dgate_ref, dup_ref):
        gv = g_ref[...]
        for p in range(N_CHIPS):
            dact = _dot(gv, w_ref[p], NT)
            dgate_ref[p] = (dact * gate_ref[p].astype(F32)).astype(BF16)
            dup_ref[p] = (dact * up_ref[p].astype(F32)).astype(BF16)

    blk = pl.BlockSpec((N_CHIPS, tm, n4), lambda i: (0, i, 0))
    return _call(
        body, grid=(t // tm,),
        in_specs=[_row_spec(tm, d), pl.BlockSpec((N_CHIPS, None, n4, d), lambda i: (0, layer, 0, 0)), blk, blk],
        out_specs=[blk, blk], out_shape=[SDS((N_CHIPS, t, n4), BF16)] * 2, operands=[g, wd, gate, up], name=name, ride=ride)


def _copy_epi(accs, ex, outs, pids):
    for a, o in zip(accs, outs):
        o[...] = a.astype(o.dtype)


def _dgrad_cols(name, dz_list, w_list, layer, three_d, x, g, res, bf16_copy=True, w_rows=False, ride=None):
    t, d = x.shape
    n4 = w_list[0].shape[-2 if w_rows else -1]
    tm = TILES["dgrad_cols"]
    if three_d:
        zs, z_pick = pl.BlockSpec((N_CHIPS, tm, n4), lambda i: (0, i, 0)), _slab
    else:
        zs, z_pick = _row_spec(tm, N_CHIPS * n4), (lambda ref, p: ref[:, p * n4:(p + 1) * n4])
    ws = pl.BlockSpec((N_CHIPS, None) + ((n4, d) if w_rows else (d, n4)), lambda i: (0, layer, 0, 0))
    xs = _row_spec(tm, d)
    return _matmul(
        name, (t // tm,), [(dz, zs, z_pick, w, ws, _slab, 0) for dz, w in zip(dz_list, w_list)],
        [(x, xs), (g, _const_spec((1, d))), (res, xs)],
        [(SDS((t, d), F32), xs), (SDS((1, d), F32), _const_spec((1, d)))] + ([(SDS((t, d), BF16), xs)] if bf16_copy else []),
        NN if w_rows else NT, _epi_rms_bwd, slabs=N_CHIPS, ride=ride)


def _dgrad_rows(name, g, w):
    t, d = g.shape
    k = w.shape[0]
    tm = TILES["dgrad_rows"]
    return _matmul(name, (t // tm,), [(g, _row_spec(tm, d), _whole, w, _const_spec((k, d)), _whole, 0)], [],
                   [(SDS((t, k), F32), _row_spec(tm, k))], NT, _copy_epi)[0]


A_TILE = 256


def _a_common(p_ref, lg_ref, lb_ref):
    pv = p_ref[...].astype(F32)
    a = _gelu(pv)
    u, v = a[:, :A_WIDTH], a[:, A_WIDTH:]
    vc = v - jnp.mean(v, axis=-1, keepdims=True)
    rs = lax.rsqrt(jnp.mean(vc * vc, axis=-1, keepdims=True) + EPS)
    vhat = vc * rs
    vn = vhat * lg_ref[...] + lb_ref[...]
    return pv, u, vhat, rs, vn.astype(BF16)


def _tril_weights(w_ref, g):
    r = lax.broadcasted_iota(jnp.int32, (CHUNK, CHUNK), 0)
    c = lax.broadcasted_iota(jnp.int32, (CHUNK, CHUNK), 1)
    return jnp.where(c <= r, w_ref[g], 0.0).astype(BF16), c <= r


def _mixer_a_fwd(proj, lg, lb, w, bias_t):
    t = proj.shape[0]

    def body(p_ref, lg_ref, lb_ref, w_ref, bt_ref, o_ref):
        _, u, _, _, vnb = _a_common(p_ref, lg_ref, lb_ref)
        for g in range(A_GROUPS):
            wt, _ = _tril_weights(w_ref, g)
            cs = slice(g * CHUNK, (g + 1) * CHUNK)
            for ch in range(A_TILE // CHUNK):
                rs_ = slice(ch * CHUNK, (ch + 1) * CHUNK)
                mixed = _dot(wt, vnb[rs_, cs], NN) + bt_ref[:, g:g + 1]
                o_ref[rs_, cs] = (u[rs_, cs] * mixed).astype(BF16)

    return pl.pallas_call(
        body, grid=(t // A_TILE,),
        in_specs=[pl.BlockSpec((A_TILE, 2 * A_WIDTH), lambda i: (i, 0)), _const_spec((1, A_WIDTH)),
                  _const_spec((1, A_WIDTH)), _const_spec((A_GROUPS, CHUNK, CHUNK)), _const_spec((CHUNK, A_GROUPS))],
        out_specs=pl.BlockSpec((A_TILE, A_WIDTH), lambda i: (i, 0)), out_shape=SDS((t, A_WIDTH), BF16),
        name="mixer_a_fwd", compiler_params=_params())(proj, lg, lb, w, bias_t)


def _mixer_a_bwd(proj, dcat, lg, lb, w, bias_t):
    t = proj.shape[0]

    def body(p_ref, da_ref, lg_ref, lb_ref, w_ref, bt_ref, dp_ref, dw_ref, dbt_ref, dlg_ref, dlb_ref, du_scr, dvn_scr):
        @pl.when(pl.program_id(0) == 0)
        def _():
            dw_ref[...] = jnp.zeros_like(dw_ref)
            dbt_ref[...] = jnp.zeros_like(dbt_ref)
            dlg_ref[...] = jnp.zeros_like(dlg_ref)
            dlb_ref[...] = jnp.zeros_like(dlb_ref)

        pv, u, vhat, rs, vnb = _a_common(p_ref, lg_ref, lb_ref)
        da = da_ref[...]
        for g in range(A_GROUPS):
            wt, keep = _tril_weights(w_ref, g)
            cs = slice(g * CHUNK, (g + 1) * CHUNK)
            for ch in range(A_TILE // CHUNK):
                rs_ = slice(ch * CHUNK, (ch + 1) * CHUNK)
                vg = vnb[rs_, cs]
                mixed = _dot(wt, vg, NN) + bt_ref[:, g:g + 1]
                du_scr[rs_, cs] = da[rs_, cs] * mixed
                dmx = da[rs_, cs] * u[rs_, cs]
                dw_ref[g] += jnp.where(keep, _dot(dmx, vg, NT), 0.0)
                dvn_scr[rs_, cs] = _dot(wt, dmx, TN)
                dbt_ref[:, g:g + 1] += jnp.sum(dmx, axis=1, keepdims=True)
        dvn = dvn_scr[...]
        dlg_ref[...] += jnp.sum(dvn * vhat, axis=0, keepdims=True)
        dlb_ref[...] += jnp.sum(dvn, axis=0, keepdims=True)
        dvh = dvn * lg_ref[...]
        dv = rs * (dvh - jnp.mean(dvh, axis=-1, keepdims=True) - vhat * jnp.mean(dvh * vhat, axis=-1, keepdims=True))
        gp = _gelu_grad(pv)
        dp_ref[:, :A_WIDTH] = (du_scr[...] * gp[:, :A_WIDTH]).astype(BF16)
        dp_ref[:, A_WIDTH:] = (dv * gp[:, A_WIDTH:]).astype(BF16)

    return pl.pallas_call(
        body, grid=(t // A_TILE,),
        in_specs=[pl.BlockSpec((A_TILE, 2 * A_WIDTH), lambda i: (i, 0)), pl.BlockSpec((A_TILE, A_WIDTH), lambda i: (i, 0)),
                  _const_spec((1, A_WIDTH)), _const_spec((1, A_WIDTH)), _const_spec((A_GROUPS, CHUNK, CHUNK)),
                  _const_spec((CHUNK, A_GROUPS))],
        out_specs=[pl.BlockSpec((A_TILE, 2 * A_WIDTH), lambda i: (i, 0)), _const_spec((A_GROUPS, CHUNK, CHUNK)),
                   _const_spec((CHUNK, A_GROUPS)), _const_spec((1, A_WIDTH)), _const_spec((1, A_WIDTH))],
        out_shape=[SDS((t, 2 * A_WIDTH), BF16), SDS((A_GROUPS, CHUNK, CHUNK), F32), SDS((CHUNK, A_GROUPS), F32),
                   SDS((1, A_WIDTH), F32), SDS((1, A_WIDTH), F32)],
        scratch_shapes=[pltpu.VMEM((A_TILE, A_WIDTH), F32), pltpu.VMEM((A_TILE, A_WIDTH), F32)],
        name="mixer_a_bwd", compiler_params=_params())(proj, dcat, lg, lb, w, bias_t)


def _rope_tables(t):
    half = ROPE_DIM // 2
    inv_freq = ROPE_THETA ** (-jnp.arange(half, dtype=F32) * 2.0 / ROPE_DIM)
    ang = jnp.arange(t, dtype=F32)[:, None] * inv_freq[None, :]
    cos, sin = jnp.cos(ang), jnp.sin(ang)
    one = jnp.ones((t, HEAD_DIM - ROPE_DIM), F32)
    zero = jnp.zeros((t, HEAD_DIM - ROPE_DIM), F32)
    zh = jnp.zeros((t, half), F32)
    c = jnp.concatenate([cos, cos, one], axis=1)
    s1 = jnp.concatenate([-sin, zh, zero], axis=1)
    s2 = jnp.concatenate([zh, sin, zero], axis=1)
    return tuple(jnp.tile(a, (1, LANES // HEAD_DIM)) for a in (c, s1, s2))


QK_TILE = 512
QK_ROWS = 64
QK_COLS = 2 * N_DIL * B_WIDTH


CHUNKS = B_WIDTH // LANES


def _regroup_out(scr, first, out_ref, rate, tile):
    rows = tile // rate
    for rho in range(rate):
        for c in range(CHUNKS):
            out_ref[rho, :, c * LANES:(c + 1) * LANES] = scr[first + c, pl.ds(rho, rows, stride=rate), :].astype(out_ref.dtype)


def _regroup_in(x_ref, scr, rate, tile):
    rows = tile // rate
    for rho in range(rate):
        for c in range(CHUNKS):
            scr[c, pl.ds(rho, rows, stride=rate), :] = x_ref[rho, :, c * LANES:(c + 1) * LANES].astype(F32)


def _regrouped_spec(rate, tile):
    return pl.BlockSpec((rate, tile // rate, B_WIDTH), lambda i, *_: (0, i, 0))


def _qk_fwd(proj, gains, tabs, ride=None):
    t = proj.shape[0]
    col0 = 2 * A_WIDTH // 1024
    r1, r2 = DIL_RATES[1], DIL_RATES[2]

    def body(p_ref, g_ref, c_ref, s1_ref, s2_ref, o_ref, q1_ref, q2_ref, k1_ref, k2_ref, scr):
        seg = _segment_mean_matrix(HEAD_DIM)
        for r0 in range(0, QK_TILE, QK_ROWS):
            rows = slice(r0, r0 + QK_ROWS)
            c, s1, s2 = c_ref[rows, :], s1_ref[rows, :], s2_ref[rows, :]
            for ci in range(1024 // LANES):
                ls = slice(ci * LANES, (ci + 1) * LANES)
                xv = p_ref[rows, ls].astype(F32)
                r = lax.rsqrt(_segment_dot(xv * xv, seg) + EPS)
                y = xv * r * g_ref[:, ls]
                val = y * c + pltpu.roll(y, LANES - 8, axis=1) * s1 + pltpu.roll(y, 8, axis=1) * s2
                o_ref[rows, ls] = val.astype(BF16)
                scr[ci, rows, :] = val

        j = pl.program_id(1)

        @pl.when(j == 0)
        def _():
            _regroup_out(scr, CHUNKS, q1_ref, r1, QK_TILE)

        @pl.when(j == 1)
        def _():
            _regroup_out(scr, 0, q2_ref, r2, QK_TILE)

        @pl.when(j == 2)
        def _():
            _regroup_out(scr, 0, k1_ref, r1, QK_TILE)
            _regroup_out(scr, CHUNKS, k2_ref, r2, QK_TILE)

    tab = pl.BlockSpec((QK_TILE, LANES), lambda i, j: (i, 0))
    g1, g2 = SDS((r1, t // r1, B_WIDTH), BF16), SDS((r2, t // r2, B_WIDTH), BF16)
    s1_, s2_ = _regrouped_spec(r1, QK_TILE), _regrouped_spec(r2, QK_TILE)
    return _call(
        body, grid=(t // QK_TILE, QK_COLS // 1024),
        in_specs=[pl.BlockSpec((QK_TILE, 1024), lambda i, j: (i, col0 + j)), pl.BlockSpec((1, 1024), lambda i, j: (0, j)),
                  tab, tab, tab],
        out_specs=[pl.BlockSpec((QK_TILE, 1024), lambda i, j: (i, j)), s1_, s2_, s1_, s2_],
        out_shape=[SDS((t, QK_COLS), BF16), g1, g2, g1, g2],
        scratch_shapes=[pltpu.VMEM((2 * CHUNKS, QK_TILE, LANES), F32)],
        operands=[proj, gains, *tabs], name="qk_norm_rope_fwd", ride=ride)


PERM_TILE = 512


def _permute(name, items, rate):
    t = items[0][0].shape[0]
    n = len(items)

    def body(*refs):
        scr = refs[-1]
        for x_ref, o_ref in zip(refs[:n], refs[n:2 * n]):
            for ci in range(CHUNKS):
                scr[ci] = x_ref[:, ci * LANES:(ci + 1) * LANES].astype(F32)
            _regroup_out(scr, 0, o_ref, rate, PERM_TILE)

    return pl.pallas_call(
        body, grid=(t // PERM_TILE,),
        in_specs=[pl.BlockSpec((PERM_TILE, B_WIDTH), functools.partial(lambda cb, i: (i, cb), cb)) for _, cb in items],
        out_specs=[_regrouped_spec(rate, PERM_TILE) for _ in items],
        out_shape=[SDS((rate, t // rate, B_WIDTH), a.dtype) for a, _ in items],
        scratch_shapes=[pltpu.VMEM((CHUNKS, PERM_TILE, LANES), F32)],
        name=name, compiler_params=_params())(*[a for a, _ in items])


def _head_lane_mask(h):
    lane = lax.broadcasted_iota(jnp.int32, (1, LANES), 1)
    return (lane < HEAD_DIM) if h == 0 else (lane >= HEAD_DIM)


def _attn_fwd(name, q, k, v, ride=None):
    rate, length = q[0].shape[0], q[0].shape[1]
    nb = length // ATT_BLOCK
    scale = HEAD_DIM ** -0.5

    def body(q_ref, kc_ref, kp_ref, vc_ref, vp_ref, o_ref, l_ref):
        n = pl.program_id(1)
        qi = lax.broadcasted_iota(jnp.int32, (ATT_BLOCK, 2 * ATT_BLOCK), 0)
        cj = lax.broadcasted_iota(jnp.int32, (ATT_BLOCK, 2 * ATT_BLOCK), 1)
        has_prev = jnp.where(n > 0, 0, 2 * ATT_BLOCK)
        mask = ((cj < ATT_BLOCK) & (cj >= qi + has_prev)) | ((cj >= ATT_BLOCK) & (cj - ATT_BLOCK <= qi))
        heads = [(hp, h) for hp in range(CHUNKS) for h in range(2)]
        q2, k2, v2 = {}, {}, {}
        for hp in range(CHUNKS):
            ls = slice(hp * LANES, (hp + 1) * LANES)
            q2[hp] = q_ref[:, ls]
            k2[hp] = jnp.concatenate([kp_ref[:, ls], kc_ref[:, ls]], axis=0)
            v2[hp] = jnp.concatenate([vp_ref[:, ls], vc_ref[:, ls]], axis=0)
        scores = {}
        for hp, h in heads:
            scores[hp, h] = _dot(jnp.where(_head_lane_mask(h), q2[hp], jnp.zeros_like(q2[hp])), k2[hp], NT) * scale
        probs, lses = {}, {}
        for hp, h in heads:
            s = jnp.where(mask, scores[hp, h], NEG_INF)
            m = jnp.max(s, axis=1, keepdims=True)
            p = jnp.exp(s - m)
            den = jnp.sum(p, axis=1, keepdims=True)
            lses[hp, h] = m + jnp.log(den)
            probs[hp, h] = (p / den).astype(BF16)
        for hp in range(CHUNKS):
            ls = slice(hp * LANES, (hp + 1) * LANES)
            o_acc = None
            for h in range(2):
                o = _dot(probs[hp, h], jnp.where(_head_lane_mask(h), v2[hp], jnp.zeros_like(v2[hp])), NN)
                o_acc = o if o_acc is None else o_acc + o
            o_ref[:, ls] = o_acc
            zeros = jnp.zeros((ATT_BLOCK, LANES), F32)
            l_ref[:, ls] = jnp.where(_head_lane_mask(1), lses[hp, 1] + zeros, lses[hp, 0] + zeros)

    def cur(cb):
        return pl.BlockSpec((None, ATT_BLOCK, B_WIDTH), lambda r, n: (r, n, cb))

    def prev(cb):
        return pl.BlockSpec((None, ATT_BLOCK, B_WIDTH), lambda r, n: (r, jnp.maximum(n - 1, 0), cb))

    out = pl.BlockSpec((None, ATT_BLOCK, B_WIDTH), lambda r, n: (r, n, 0))
    return _call(
        body, grid=(rate, nb),
        in_specs=[cur(q[1]), cur(k[1]), prev(k[1]), cur(v[1]), prev(v[1])],
        out_specs=[out, out], out_shape=[SDS((rate, length, B_WIDTH), F32)] * 2,
        operands=[q[0], k[0], k[0], v[0], v[0]], name=name, ride=ride)


def _attn_merge(a_out, o_list, l_list):
    t = a_out.shape[0]
    tm = PERM_TILE
    r1, r2 = DIL_RATES[1], DIL_RATES[2]

    def body(a_ref, o0, o1, o2, l0, l1, l2, cat_ref, lt_ref, lt1_ref, lt2_ref, so1, so2, sl1, sl2, slt):
        _regroup_in(o1, so1, r1, tm)
        _regroup_in(l1, sl1, r1, tm)
        _regroup_in(o2, so2, r2, tm)
        _regroup_in(l2, sl2, r2, tm)
        cat_ref[:, :A_WIDTH] = a_ref[...]
        for c in range(CHUNKS):
            ls = slice(c * LANES, (c + 1) * LANES)
            lg = [l0[:, ls], sl1[c], sl2[c]]
            m = jnp.maximum(jnp.maximum(lg[0], lg[1]), lg[2])
            es = [jnp.exp(l - m) for l in lg]
            den = es[0] + es[1] + es[2]
            b = (es[0] * o0[:, ls] + es[1] * so1[c] + es[2] * so2[c]) / den
            cat_ref[:, A_WIDTH + c * LANES:A_WIDTH + (c + 1) * LANES] = b.astype(BF16)
            lt = m + jnp.log(den)
            lt_ref[:, ls] = lt
            slt[c] = lt
        _regroup_out(slt, 0, lt1_ref, r1, tm)
        _regroup_out(slt, 0, lt2_ref, r2, tm)

    blk = _row_spec(tm, B_WIDTH)
    g1, g2 = _regrouped_spec(r1, tm), _regrouped_spec(r2, tm)
    return pl.pallas_call(
        body, grid=(t // tm,), in_specs=[blk, blk, g1, g2, blk, g1, g2],
        out_specs=[_row_spec(tm, A_WIDTH + B_WIDTH), blk, g1, g2],
        out_shape=[SDS((t, A_WIDTH + B_WIDTH), BF16), SDS((t, B_WIDTH), F32), SDS((r1, t // r1, B_WIDTH), F32),
                   SDS((r2, t // r2, B_WIDTH), F32)],
        scratch_shapes=[pltpu.VMEM((CHUNKS, tm, LANES), F32)] * 5,
        name="attn_merge", compiler_params=_params())(a_out, *o_list, *l_list)


def _attn_bwd_prep(dcat, cat):
    t = dcat.shape[0]
    tm = PERM_TILE
    r1, r2 = DIL_RATES[1], DIL_RATES[2]

    def body(d_ref, b_ref, db_ref, dd_ref, db1_ref, dd1_ref, db2_ref, dd2_ref, sdb, sdd):
        seg = _segment_mean_matrix(HEAD_DIM, scale=1.0)
        for c in range(CHUNKS):
            ls = slice(c * LANES, (c + 1) * LANES)
            d = d_ref[:, ls]
            dsum = _segment_dot(d * b_ref[:, ls].astype(F32), seg)
            db_ref[:, ls] = d.astype(BF16)
            dd_ref[:, ls] = dsum
            sdb[c] = d
            sdd[c] = dsum
        _regroup_out(sdb, 0, db1_ref, r1, tm)
        _regroup_out(sdd, 0, dd1_ref, r1, tm)
        _regroup_out(sdb, 0, db2_ref, r2, tm)
        _regroup_out(sdd, 0, dd2_ref, r2, tm)

    right = pl.BlockSpec((tm, B_WIDTH), lambda i: (i, 1))
    blk = _row_spec(tm, B_WIDTH)
    g1, g2 = _regrouped_spec(r1, tm), _regrouped_spec(r2, tm)
    return pl.pallas_call(
        body, grid=(t // tm,), in_specs=[right, right], out_specs=[blk, blk, g1, g1, g2, g2],
        out_shape=[SDS((t, B_WIDTH), BF16), SDS((t, B_WIDTH), F32), SDS((r1, t // r1, B_WIDTH), BF16),
                   SDS((r1, t // r1, B_WIDTH), F32), SDS((r2, t // r2, B_WIDTH), BF16), SDS((r2, t // r2, B_WIDTH), F32)],
        scratch_shapes=[pltpu.VMEM((CHUNKS, tm, LANES), F32)] * 2,
        name="attn_bwd_prep", compiler_params=_params())(dcat, cat)


def _attn_bwd(name, q, k, v, db, lse, dd, ride=None):
    rate, length = db.shape[0], db.shape[1]
    nb = length // ATT_BLOCK
    scale = HEAD_DIM ** -0.5

    def body(qa_ref, qb_ref, k_ref, v_ref, dba_ref, dbb_ref, la_ref, lb_ref, da_ref, dbd_ref, dq_ref, dk_ref, dv_ref, carry):
        m = pl.program_id(1)

        @pl.when(m == 0)
        def _():
            carry[...] = jnp.zeros_like(carry)

        row = lax.broadcasted_iota(jnp.int32, (2 * ATT_BLOCK, ATT_BLOCK), 0)
        kj = lax.broadcasted_iota(jnp.int32, (2 * ATT_BLOCK, ATT_BLOCK), 1)
        no_next = jnp.where(m + 1 < nb, 0, 2 * ATT_BLOCK)
        mask = ((row < ATT_BLOCK) & (kj <= row)) | ((row >= ATT_BLOCK) & (kj >= row - ATT_BLOCK + no_next))
        heads = [(hp, h) for hp in range(CHUNKS) for h in range(2)]
        q2, db2, lse2, dd2, k2, v2 = {}, {}, {}, {}, {}, {}
        for hp in range(CHUNKS):
            ls = slice(hp * LANES, (hp + 1) * LANES)
            k2[hp], v2[hp] = k_ref[:, ls], v_ref[:, ls]
            q2[hp] = jnp.concatenate([qa_ref[:, ls], qb_ref[:, ls]], axis=0)
            db2[hp] = jnp.concatenate([dba_ref[:, ls], dbb_ref[:, ls]], axis=0)
            lse2[hp] = jnp.concatenate([la_ref[:, ls], lb_ref[:, ls]], axis=0)
            dd2[hp] = jnp.concatenate([da_ref[:, ls], dbd_ref[:, ls]], axis=0)
        km, scores, dps = {}, {}, {}
        for hp, h in heads:
            hm = _head_lane_mask(h)
            km[hp, h] = jnp.where(hm, k2[hp], jnp.zeros_like(k2[hp]))
            scores[hp, h] = _dot(q2[hp], km[hp, h], NT) * scale
            dps[hp, h] = _dot(db2[hp], jnp.where(hm, v2[hp], jnp.zeros_like(v2[hp])), NT)
        probs, dss = {}, {}
        for hp, h in heads:
            hm = _head_lane_mask(h)
            lse_col = jnp.max(jnp.where(hm, lse2[hp], NEG_INF), axis=1, keepdims=True)
            dd_col = jnp.max(jnp.where(hm, dd2[hp], NEG_INF), axis=1, keepdims=True)
            p = jnp.where(mask, jnp.exp(scores[hp, h] - lse_col), 0.0)
            probs[hp, h] = p.astype(BF16)
            dss[hp, h] = (p * (dps[hp, h] - dd_col) * scale).astype(BF16)
        for hp in range(CHUNKS):
            ls = slice(hp * LANES, (hp + 1) * LANES)
            dq_acc, dk_acc, dv_acc = None, None, None
            for h in range(2):
                hm = _head_lane_mask(h)
                dvc = _dot(probs[hp, h], jnp.where(hm, db2[hp], jnp.zeros_like(db2[hp])), TN)
                dqc = _dot(dss[hp, h], km[hp, h], NN)
                dkc = _dot(dss[hp, h], jnp.where(hm, q2[hp], jnp.zeros_like(q2[hp])), TN)
                dq_acc = dqc if dq_acc is None else dq_acc + dqc
                dk_acc = dkc if dk_acc is None else dk_acc + dkc
                dv_acc = dvc if dv_acc is None else dv_acc + dvc
            dq_ref[:, ls] = (dq_acc[:ATT_BLOCK] + carry[:, ls]).astype(BF16)
            carry[:, ls] = dq_acc[ATT_BLOCK:]
            dk_ref[:, ls] = dk_acc.astype(BF16)
            dv_ref[:, ls] = dv_acc.astype(BF16)

    def cur(cb):
        return pl.BlockSpec((None, ATT_BLOCK, B_WIDTH), lambda r, n: (r, n, cb))

    def nxt(cb):
        return pl.BlockSpec((None, ATT_BLOCK, B_WIDTH), lambda r, n: (r, jnp.minimum(n + 1, nb - 1), cb))

    out = cur(0)
    return _call(
        body, grid=(rate, nb),
        in_specs=[cur(q[1]), nxt(q[1]), cur(k[1]), cur(v[1]), cur(0), nxt(0), cur(0), nxt(0), cur(0), nxt(0)],
        out_specs=[out, out, out], out_shape=[SDS((rate, length, B_WIDTH), BF16)] * 3,
        scratch_shapes=[pltpu.VMEM((ATT_BLOCK, B_WIDTH), F32)],
        operands=[q[0], q[0], k[0], v[0], db, db, lse, lse, dd, dd], name=name, ride=ride)


AB_IN = 2 * A_WIDTH + 3 * N_DIL * B_WIDTH
ASM_TILE = 256


def _dproj_assemble(proj, d_a, dq, dk, dv, gains, tabs, ride=None):
    t = proj.shape[0]
    n_in = 3 * N_DIL

    def body(p_ref, da_ref, *rest):
        grads = rest[:n_in]
        g_ref, c_ref, s1_ref, s2_ref, o_ref, dg_ref = rest[n_in:n_in + 6]
        scratch = rest[n_in + 6:]

        @pl.when(pl.program_id(0) == 0)
        def _():
            dg_ref[...] = jnp.zeros_like(dg_ref)

        chunk = {}
        k_scr = 0
        for j in range(n_in):
            g = j % N_DIL
            if DIL_RATES[g] == 1:
                for ci in range(CHUNKS):
                    chunk[j, ci] = functools.partial(lambda r, ci: r[:, ci * LANES:(ci + 1) * LANES].astype(F32), grads[j], ci)
            else:
                scr = scratch[k_scr]
                k_scr += 1
                _regroup_in(grads[j], scr, DIL_RATES[g], ASM_TILE)
                for ci in range(CHUNKS):
                    chunk[j, ci] = functools.partial(lambda s, ci: s[ci], scr, ci)

        seg = _segment_mean_matrix(HEAD_DIM)
        c, s1, s2 = c_ref[...], s1_ref[...], s2_ref[...]
        o_ref[:, :2 * A_WIDTH] = da_ref[...]
        for jg in range(2 * N_DIL):
            for ci in range(CHUNKS):
                col = jg * B_WIDTH + ci * LANES
                src = slice(2 * A_WIDTH + col, 2 * A_WIDTH + col + LANES)
                xv = p_ref[:, src].astype(F32)
                r = lax.rsqrt(_segment_dot(xv * xv, seg) + EPS)
                xh = xv * r
                gain = g_ref[:, col:col + LANES]
                do = chunk[jg, ci]()
                dy = do * c + pltpu.roll(do * s1, 8, axis=1) + pltpu.roll(do * s2, LANES - 8, axis=1)
                dg_ref[:, col:col + LANES] += jnp.sum(dy * xh, axis=0, keepdims=True)
                dxh = dy * gain
                o_ref[:, src] = (r * (dxh - xh * _segment_dot(dxh * xh, seg))).astype(BF16)
        v0 = 2 * A_WIDTH + QK_COLS
        for g in range(N_DIL):
            for ci in range(CHUNKS):
                col = v0 + g * B_WIDTH + ci * LANES
                o_ref[:, col:col + LANES] = chunk[2 * N_DIL + g, ci]().astype(BF16)

    specs = [_row_spec(ASM_TILE, B_WIDTH) if r == 1 else _regrouped_spec(r, ASM_TILE) for r in DIL_RATES] * 3
    n_scr = 3 * sum(1 for r in DIL_RATES if r > 1)
    tab = _row_spec(ASM_TILE, LANES)
    return _call(
        body, grid=(t // ASM_TILE,),
        in_specs=[_row_spec(ASM_TILE, AB_IN), _row_spec(ASM_TILE, 2 * A_WIDTH)] + specs
        + [_const_spec((1, QK_COLS)), tab, tab, tab],
        out_specs=[_row_spec(ASM_TILE, AB_IN), _const_spec((1, QK_COLS))],
        out_shape=[SDS((t, AB_IN), BF16), SDS((1, QK_COLS), F32)],
        scratch_shapes=[pltpu.VMEM((CHUNKS, ASM_TILE, LANES), F32)] * n_scr,
        operands=[proj, d_a, *dq, *dk, *dv, gains, *tabs], name="dproj_assemble", ride=ride)


def _fold_heads(dg_lane):
    n = dg_lane.shape[1]

    def body(x_ref, o_ref):
        r = lax.broadcasted_iota(jnp.int32, (B_WIDTH, B_WIDTH), 0) % HEAD_DIM
        c = lax.broadcasted_iota(jnp.int32, (B_WIDTH, B_WIDTH), 1) % HEAD_DIM
        fold = jnp.where(r == c, 1.0, 0.0).astype(F32)
        for jg in range(n // B_WIDTH):
            ls = slice(jg * B_WIDTH, (jg + 1) * B_WIDTH)
            o_ref[:, ls] = _dot_hi(jnp.broadcast_to(x_ref[:, ls], (8, B_WIDTH)), fold)

    return pl.pallas_call(body, out_shape=SDS((8, n), F32), name="fold_heads", compiler_params=_params())(dg_lane)


CD_TILE = 256
TAP_ROWS = 64
CD_IN = 2 * C_WIDTH + 3 * 512


def _shifted_copies(src, dst, rows):
    dst[0, :rows] = src[...]
    for b in range(1, 8):
        dst[b, :rows - 8] = src[pl.ds(b, rows - 8), :]


def _rows_from(shifted, start, n, lanes=slice(None)):
    b = start % 8
    return shifted[b, pl.ds(start - b, n), lanes]


def _mixer_cd_fwd(proj, cw, cb, lg, lb, dw):
    t = proj.shape[0]
    per = CD_TILE // HALO

    def body(h_ref, m_ref, cw_ref, cb_ref, lg_ref, lb_ref, dw_ref, o_ref, c1_ref, c_scr, e_scr, c_sh):
        not_first = (pl.program_id(0) > 0).astype(F32)
        lanes = [slice(c * LANES, (c + 1) * LANES) for c in range(C_WIDTH // LANES)]

        def col(ref, part, ls):
            return ref[:, part * C_WIDTH + ls.start:part * C_WIDTH + ls.stop].astype(F32)

        for ls in lanes:
            c_scr[:HALO, ls] = col(h_ref, 0, ls) * _sigmoid(col(h_ref, 1, ls)) * not_first
            c_scr[HALO:, ls] = col(m_ref, 0, ls) * _sigmoid(col(m_ref, 1, ls))
            e_scr[:HALO, ls] = col(h_ref, 3, ls) * col(h_ref, 4, ls) * not_first
            e_scr[HALO:, ls] = col(m_ref, 3, ls) * col(m_ref, 4, ls)
        _shifted_copies(c_scr, c_sh, HALO + CD_TILE)
        for ls in lanes:
            for r0 in range(0, CD_TILE, TAP_ROWS):
                acc = jnp.zeros((TAP_ROWS, LANES), F32)
                for k in range(C_KERNEL):
                    acc = acc + cw_ref[k:k + 1, ls] * _rows_from(c_sh, r0 + HALO - (C_KERNEL - 1) + k, TAP_ROWS, ls)
                c1_ref[r0:r0 + TAP_ROWS, ls] = acc + cb_ref[:, ls]
        mean = sum(jnp.sum(c1_ref[:, ls], axis=-1, keepdims=True) for ls in lanes) * (1.0 / C_WIDTH)
        var = sum(jnp.sum((c1_ref[:, ls] - mean) ** 2, axis=-1, keepdims=True) for ls in lanes) * (1.0 / C_WIDTH)
        rs = lax.rsqrt(var + EPS)
        for ls in lanes:
            c2 = (c1_ref[:, ls] - mean) * rs * lg_ref[:, ls] + lb_ref[:, ls]
            o_ref[:, ls] = (c2 * _sigmoid(c2)).astype(BF16)
            d1 = jnp.zeros((CD_TILE, LANES), F32)
            for k in range(D_KERNEL):
                d1 = d1 + dw_ref[k:k + 1, ls] * e_scr[pl.ds(HALO - (D_KERNEL - 1) + k, CD_TILE), ls]
            o_ref[:, C_WIDTH + ls.start:C_WIDTH + ls.stop] = (col(m_ref, 2, ls) * d1).astype(BF16)

    return pl.pallas_call(
        body, grid=(t // CD_TILE,),
        in_specs=[pl.BlockSpec((HALO, CD_IN), lambda i: (jnp.maximum(i * per - 1, 0), 0)), _row_spec(CD_TILE, CD_IN),
                  _const_spec((32, C_WIDTH)), _const_spec((1, C_WIDTH)), _const_spec((1, C_WIDTH)), _const_spec((1, C_WIDTH)),
                  _const_spec((8, C_WIDTH))],
        out_specs=[_row_spec(CD_TILE, 2 * C_WIDTH), _row_spec(CD_TILE, C_WIDTH)],
        out_shape=[SDS((t, 2 * C_WIDTH), BF16), SDS((t, C_WIDTH), F32)],
        scratch_shapes=[pltpu.VMEM((HALO + CD_TILE, C_WIDTH), F32)] * 2 + [pltpu.VMEM((8, HALO + CD_TILE, C_WIDTH), F32)],
        name="mixer_cd_fwd", compiler_params=_params())(proj, proj, cw, cb, lg, lb, dw)


def _mixer_cd_bwd(proj, dcat, c1, cw, lg, lb, dw, ride=None):
    t = proj.shape[0]
    per = CD_TILE // HALO
    nt = t // CD_TILE
    ext = CD_TILE + HALO

    def body(hp_ref, m_ref, hn_ref, dm_ref, dn_ref, c1m_ref, c1n_ref, cw_ref, lg_ref, lb_ref, dw_ref,
             dp_ref, dcw_ref, dcb_ref, dlg_ref, dlb_ref, ddw_ref, c_scr, e_scr, dc1_scr, dd1_scr, c_sh, dc1_sh, dcw_acc,
             dvh_scr, vhat_scr):
        i = pl.program_id(0)

        @pl.when(i == 0)
        def _():
            for r in (dcw_acc, dcb_ref, dlg_ref, dlb_ref, ddw_ref):
                r[...] = jnp.zeros_like(r)

        not_first = (i > 0).astype(F32)
        not_last = (i < nt - 1).astype(F32)
        main = slice(HALO, HALO + CD_TILE)
        lanes = [slice(c * LANES, (c + 1) * LANES) for c in range(C_WIDTH // LANES)]

        def col(ref, part, ls):
            return ref[:, part * C_WIDTH + ls.start:part * C_WIDTH + ls.stop].astype(F32)

        for ls in lanes:
            c_scr[:HALO, ls] = col(hp_ref, 0, ls) * _sigmoid(col(hp_ref, 1, ls)) * not_first
            c_scr[main, ls] = col(m_ref, 0, ls) * _sigmoid(col(m_ref, 1, ls))
            c_scr[HALO + CD_TILE:, ls] = col(hn_ref, 0, ls) * _sigmoid(col(hn_ref, 1, ls)) * not_last
            e_scr[:HALO, ls] = col(hp_ref, 3, ls) * col(hp_ref, 4, ls) * not_first
            e_scr[main, ls] = col(m_ref, 3, ls) * col(m_ref, 4, ls)
            e_scr[HALO + CD_TILE:, ls] = col(hn_ref, 3, ls) * col(hn_ref, 4, ls) * not_last
        _shifted_copies(c_scr, c_sh, 2 * HALO + CD_TILE)

        def c1_of(ls):
            return jnp.concatenate([c1m_ref[:, ls], c1n_ref[:, ls]], axis=0)

        mean = sum(jnp.sum(c1_of(ls), axis=-1, keepdims=True) for ls in lanes) * (1.0 / C_WIDTH)
        var = sum(jnp.sum((c1_of(ls) - mean) ** 2, axis=-1, keepdims=True) for ls in lanes) * (1.0 / C_WIDTH)
        rs = lax.rsqrt(var + EPS)
        sum_dvh, sum_dvh_vhat = 0.0, 0.0
        for ls in lanes:
            vhat = (c1_of(ls) - mean) * rs
            c2 = vhat * lg_ref[:, ls] + lb_ref[:, ls]
            sig = _sigmoid(c2)
            dc = jnp.concatenate([dm_ref[:, ls], dn_ref[:, ls] * not_last], axis=0)
            dc2 = dc * (sig * (1.0 + c2 * (1.0 - sig)))
            dvh = dc2 * lg_ref[:, ls]
            sum_dvh = sum_dvh + jnp.sum(dvh, axis=-1, keepdims=True)
            sum_dvh_vhat = sum_dvh_vhat + jnp.sum(dvh * vhat, axis=-1, keepdims=True)
            dvh_scr[:, ls] = dvh
            vhat_scr[:, ls] = vhat
            dlg_ref[:, ls] += jnp.sum((dc2 * vhat)[:CD_TILE], axis=0, keepdims=True)
            dlb_ref[:, ls] += jnp.sum(dc2[:CD_TILE], axis=0, keepdims=True)
        for ls in lanes:
            dc1 = rs * (dvh_scr[:, ls] - sum_dvh * (1.0 / C_WIDTH) - vhat_scr[:, ls] * (sum_dvh_vhat * (1.0 / C_WIDTH)))
            dc1_scr[:, ls] = dc1
            dcb_ref[:, ls] += jnp.sum(dc1[:CD_TILE], axis=0, keepdims=True)
        _shifted_copies(dc1_scr, dc1_sh, ext)
        for ls in lanes:
            for r0 in range(0, CD_TILE, TAP_ROWS):
                rows = slice(r0, r0 + TAP_ROWS)
                dc1_m = dc1_scr[rows, ls]
                dc0 = jnp.zeros((TAP_ROWS, LANES), F32)
                for k in range(C_KERNEL):
                    dc0 = dc0 + cw_ref[k:k + 1, ls] * _rows_from(dc1_sh, r0 + C_KERNEL - 1 - k, TAP_ROWS, ls)
                    prod = dc1_m * _rows_from(c_sh, r0 + HALO - (C_KERNEL - 1) + k, TAP_ROWS, ls)
                    dcw_acc[k, :, ls] += prod.reshape(TAP_ROWS // 8, 8, LANES).sum(axis=0)
                g_m = m_ref[rows, C_WIDTH + ls.start:C_WIDTH + ls.stop].astype(F32)
                a_m = m_ref[rows, ls].astype(F32)
                sig_m = _sigmoid(g_m)
                dp_ref[rows, ls] = (dc0 * sig_m).astype(BF16)
                dp_ref[rows, C_WIDTH + ls.start:C_WIDTH + ls.stop] = (dc0 * a_m * sig_m * (1.0 - sig_m)).astype(BF16)

        @pl.when(i == nt - 1)
        def _():
            dcw_ref[...] = jnp.sum(dcw_acc[...], axis=1)

        for ls in lanes:
            wide = slice(C_WIDTH + ls.start, C_WIDTH + ls.stop)
            d1 = jnp.zeros((CD_TILE, LANES), F32)
            for k in range(D_KERNEL):
                d1 = d1 + dw_ref[k:k + 1, ls] * e_scr[pl.ds(HALO - (D_KERNEL - 1) + k, CD_TILE), ls]
            dd_m = dm_ref[:, wide]
            dd1 = jnp.concatenate([dd_m * col(m_ref, 2, ls), dn_ref[:, wide] * col(hn_ref, 2, ls) * not_last], axis=0)
            dd1_scr[:, ls] = dd1
            dp_ref[:, 2 * C_WIDTH + ls.start:2 * C_WIDTH + ls.stop] = (dd_m * d1).astype(BF16)
            de = jnp.zeros((CD_TILE, LANES), F32)
            for k in range(D_KERNEL):
                de = de + dw_ref[k:k + 1, ls] * dd1_scr[pl.ds(D_KERNEL - 1 - k, CD_TILE), ls]
                ddw_ref[k:k + 1, ls] += jnp.sum(dd1[:CD_TILE] * e_scr[pl.ds(HALO - (D_KERNEL - 1) + k, CD_TILE), ls], axis=0, keepdims=True)
            dp_ref[:, 3 * C_WIDTH + ls.start:3 * C_WIDTH + ls.stop] = (de * col(m_ref, 4, ls)).astype(BF16)
            dp_ref[:, 4 * C_WIDTH + ls.start:4 * C_WIDTH + ls.stop] = (de * col(m_ref, 3, ls)).astype(BF16)

    halo_prev = lambda i: (jnp.maximum(i * per - 1, 0), 0)
    halo_next = lambda i: (jnp.minimum((i + 1) * per, t // HALO - 1), 0)
    vec = _const_spec((1, C_WIDTH))
    return _call(
        body, grid=(nt,),
        in_specs=[pl.BlockSpec((HALO, CD_IN), halo_prev), _row_spec(CD_TILE, CD_IN), pl.BlockSpec((HALO, CD_IN), halo_next),
                  _row_spec(CD_TILE, 2 * C_WIDTH), pl.BlockSpec((HALO, 2 * C_WIDTH), halo_next),
                  _row_spec(CD_TILE, C_WIDTH), pl.BlockSpec((HALO, C_WIDTH), halo_next),
                  _const_spec((32, C_WIDTH)), vec, vec, _const_spec((8, C_WIDTH))],
        out_specs=[_row_spec(CD_TILE, CD_IN), _const_spec((32, C_WIDTH)), vec, vec, vec, _const_spec((8, C_WIDTH))],
        out_shape=[SDS((t, CD_IN), BF16), SDS((32, C_WIDTH), F32), SDS((1, C_WIDTH), F32), SDS((1, C_WIDTH), F32),
                   SDS((1, C_WIDTH), F32), SDS((8, C_WIDTH), F32)],
        scratch_shapes=[pltpu.VMEM((2 * HALO + CD_TILE, C_WIDTH), F32)] * 2 + [pltpu.VMEM((ext, C_WIDTH), F32)] * 2
        + [pltpu.VMEM((8, 2 * HALO + CD_TILE, C_WIDTH), F32), pltpu.VMEM((8, ext, C_WIDTH), F32),
           pltpu.VMEM((32, 8, C_WIDTH), F32)] + [pltpu.VMEM((ext, C_WIDTH), F32)] * 2,
        operands=[proj, proj, proj, dcat, dcat, c1, c1, cw, lg, lb, dw], name="mixer_cd_bwd", ride=ride)


def _wgrad(name, pairs, out_rc, t, ride):
    tk = TILES["wgrad"]
    assert tk == t, "the whole contraction has to fit one grid step"
    r, c = out_rc
    n = len(pairs)

    def body(*refs):
        ab, out_refs = refs[:2 * n], refs[2 * n:]
        for j in range(n):
            out_refs[j][...] = _dot(ab[2 * j][...], ab[2 * j + 1][...], TN).astype(BF16)

    operands, in_specs = [], []
    for lhs, lhs_spec, rhs, rhs_spec in pairs:
        operands += [lhs, rhs]
        in_specs += [lhs_spec, rhs_spec]
    res = _call(body, grid=(N_CHIPS, t // tk), in_specs=in_specs,
                out_specs=[pl.BlockSpec((None, r, c), lambda p, k: (p, 0, 0))] * n,
                out_shape=[SDS((N_CHIPS, r, c), BF16)] * n, operands=operands, name=name, ride=ride)
    outs, ride_res = (res, None) if ride is None else res
    outs = [o.reshape(N_CHIPS, 2, r // 2, c) for o in outs]
    return outs if ride is None else (outs, ride_res)


def _wgrad_col_sharded(name, h, dz_list, three_d, ride=None):
    t, d = h.shape
    tk = TILES["wgrad"]
    n4 = dz_list[0].shape[-1] if three_d else dz_list[0].shape[-1] // N_CHIPS
    hs = pl.BlockSpec((tk, d), lambda p, k: (k, 0))
    zs = pl.BlockSpec((None, tk, n4), lambda p, k: (p, k, 0)) if three_d else pl.BlockSpec((tk, n4), lambda p, k: (k, p))
    return _wgrad(name, [(h, hs, dz, zs) for dz in dz_list], (d, n4), t, ride)


def _wgrad_row_sharded(name, a, g, three_d, ride=None):
    many = isinstance(a, (list, tuple))
    a_list = list(a) if many else [a]
    t, d = g.shape
    tk = TILES["wgrad"]
    k4 = a_list[0].shape[-1] if three_d else a_list[0].shape[-1] // N_CHIPS
    a_spec = pl.BlockSpec((None, tk, k4), lambda p, k: (p, k, 0)) if three_d else pl.BlockSpec((tk, k4), lambda p, k: (k, p))
    gs = pl.BlockSpec((tk, d), lambda p, k: (k, 0))
    res = _wgrad(name, [(a_j, a_spec, g, gs) for a_j in a_list], (k4, d), t, ride)
    if many:
        return res
    return res[0] if ride is None else (res[0][0], res[1])


def _mesh_scalars():
    return jnp.stack([lax.axis_index("c"), 2 * lax.axis_index("x") + lax.axis_index("y")]).astype(jnp.int32)


def _stage_own(name, w, layer, dtype):
    layers, r, cols = w.shape
    h = r // 2

    def body(s_ref, x_ref, o_ref):
        o_ref[...] = x_ref[...].astype(dtype)

    return pl.pallas_call(
        body,
        grid_spec=pltpu.PrefetchScalarGridSpec(
            num_scalar_prefetch=1, grid=(2,),
            in_specs=[pl.BlockSpec((None, h, cols), lambda i, s: (2 * layer + i, 0, 0))],
            out_specs=pl.BlockSpec((None, None, h, cols), lambda i, s: (s[1], i, 0, 0))),
        out_shape=SDS((N_CHIPS, 2, h, cols), dtype), name=name,
        compiler_params=_params())(_mesh_scalars(), w.reshape(2 * layers, h, cols))


STAGE_STEPS = 4


def _stage_rest_and_norm(x, g, weights, ride=None):
    t, d = x.shape
    n = len(weights)
    views, in_specs, out_specs, out_shapes = [], [], [], []
    for w, layer in weights:
        layers, r, cols = w.shape
        sub = r // STAGE_STEPS
        views.append(w.reshape(layers * STAGE_STEPS, sub, cols))
        in_specs.append(pl.BlockSpec((None, sub, cols), functools.partial(lambda l, i, s: (STAGE_STEPS * l + i, 0, 0), layer)))
        out_specs.append(pl.BlockSpec((None, None, sub, cols), lambda i, s: (s[1], i // 2, i % 2, 0)))
        out_shapes.append(SDS((N_CHIPS, 2, r // 2, cols), BF16))

    def body(s_ref, x_ref, g_ref, *rest):
        w_refs, h_ref, o_refs = rest[:n], rest[n], rest[n + 1:]
        h_ref[...] = _rms_rows(x_ref[...], g_ref[...]).astype(BF16)
        for w_ref, o_ref in zip(w_refs, o_refs):
            o_ref[...] = w_ref[...].astype(BF16)

    tm = t // STAGE_STEPS
    res = _call(
        body, grid=(STAGE_STEPS,), in_specs=[pl.BlockSpec((tm, d), lambda i, s: (i, 0)), pl.BlockSpec((1, d), lambda i, s: (0, 0))] + in_specs,
        out_specs=[pl.BlockSpec((tm, d), lambda i, s: (i, 0))] + out_specs, out_shape=[SDS((t, d), BF16)] + out_shapes,
        operands=[x, g] + views, name="stage_and_norm", ride=ride, prefetch=_mesh_scalars())
    outs, ride_res = (res, None) if ride is None else res
    result = (outs[0], list(outs[1:]))
    return result if ride is None else (result, ride_res)


def _remote(src, dst, send_sem, recv_sem, device):
    return pltpu.make_async_remote_copy(src, dst, send_sem, recv_sem, device_id=device, device_id_type=MESH)


def _ride_gather_send(bufs):
    n = len(bufs)

    def each(b, sems, act):
        send, recv = sems
        x, y, c, p, others = _position()
        for t in range(n):
            for j, (qx, qy) in enumerate(others):
                act(b[t].at[p, c], b[t].at[2 * qx + qy, c], send.at[t, j], recv.at[t, j], (qx, qy, c))

    def start(ins, b, new, sems):
        each(b, sems, lambda mine, landed, s, r, dev: _remote(mine, mine, s, r, dev).start())

    def finish(ins, b, new, sems):
        def act(mine, landed, s, r, dev):
            _remote(mine, mine, s, r, dev).wait_send()
            _remote(landed, landed, s, r, dev).wait_recv()
        each(b, sems, act)

    return _Ride([], bufs, [], [(n, 3), (n, 3)], start, finish)


def _ride_gather_pass(bufs):
    n = len(bufs)

    def each(b, sems, act):
        send, recv = sems
        x, y, c, p, others = _position()
        for t in range(n):
            for j, (qx, qy) in enumerate(others):
                act(b[t].at[2 * qx + qy, c], b[t].at[2 * qx + qy, 1 - c], send.at[t, j], recv.at[t, j], (x, y, 1 - c))

    def start(ins, b, new, sems):
        each(b, sems, lambda landed, passed, s, r, dev: _remote(landed, landed, s, r, dev).start())

    def finish(ins, b, new, sems):
        def act(landed, passed, s, r, dev):
            _remote(landed, landed, s, r, dev).wait_send()
            _remote(passed, passed, s, r, dev).wait_recv()
        each(b, sems, act)

    return _Ride([], bufs, [], [(n, 3), (n, 3)], start, finish)


def _ride_gather(bufs):
    send, onward = _ride_gather_send(bufs), _ride_gather_pass(bufs)
    n_send = len(send.sem_shapes)

    def start(ins, b, new, sems):
        send.start(ins, b, new, sems[:n_send])

    def finish(ins, b, new, sems):
        send.finish(ins, b, new, sems[:n_send])
        onward.start(ins, b, new, sems[n_send:])
        onward.finish(ins, b, new, sems[n_send:])

    return _Ride([], bufs, [], send.sem_shapes + onward.sem_shapes, start, finish)


def _ride_swap(tensors):
    n = len(tensors)

    def each(ins, new, sems, act):
        send, recv = sems
        x, y, c, _, _ = _position()
        for t in range(n):
            act(_remote(ins[t].at[:, 1 - c], new[t], send.at[t], recv.at[t], (x, y, 1 - c)))

    def start(ins, b, new, sems):
        each(ins, new, sems, lambda cp: cp.start())

    def finish(ins, b, new, sems):
        each(ins, new, sems, lambda cp: cp.wait())

    return _Ride(tensors, [], [SDS((s.shape[0],) + s.shape[2:], s.dtype) for s in tensors], [(n,), (n,)], start, finish)


def _ride_scatter(tensors, landing):
    n = len(tensors)

    def each(ins, b, sems, act):
        send, recv = sems
        x, y, c, p, others = _position()
        for t in range(n):
            for j, (qx, qy) in enumerate(others):
                q = 2 * qx + qy
                act(ins[t].at[q], b[t].at[p], b[t].at[q], send.at[t, j], recv.at[t, j], (qx, qy, c))

    def start(ins, b, new, sems):
        each(ins, b, sems, lambda src, dst, landed, s, r, dev: _remote(src, dst, s, r, dev).start())

    def finish(ins, b, new, sems):
        def act(src, dst, landed, s, r, dev):
            _remote(src, dst, s, r, dev).wait_send()
            _remote(landed, landed, s, r, dev).wait_recv()
        each(ins, b, sems, act)

    return _Ride(tensors, landing, [], [(n, 3), (n, 3)], start, finish)


def _ride_join(bufs):
    n = len(bufs)

    def each(b, sems, act):
        send, recv = sems
        x, y, c, _, _ = _position()
        for t in range(n):
            act(b[t].at[c], b[t].at[1 - c], send.at[t], recv.at[t], (x, y, 1 - c))

    def start(ins, b, new, sems):
        each(b, sems, lambda mine, theirs, s, r, dev: _remote(mine, mine, s, r, dev).start())

    def finish(ins, b, new, sems):
        def act(mine, theirs, s, r, dev):
            _remote(mine, mine, s, r, dev).wait_send()
            _remote(theirs, theirs, s, r, dev).wait_recv()
        each(b, sems, act)

    return _Ride([], bufs, [], [(n,), (n,)], start, finish)


def _all_reduce_small(pack, ride=None):
    rows = pack.shape[0]
    n_dev = 2 * N_CHIPS
    n_rb = 0 if ride is None else len(ride.bufs)

    def body(x_ref, *rest):
        o_ref = rest[n_rb]
        r_bufs = rest[n_rb + 1:2 * n_rb + 1]
        land, send, recv = rest[2 * n_rb + 1:2 * n_rb + 4]
        r_sems = rest[2 * n_rb + 4:]
        if ride is not None:
            ride.start([], r_bufs, [], r_sems)
        x, y, c, p, _ = _position()
        me = 2 * p + c
        land[me] = x_ref[...]
        peers = [(dx, dy, dc) for dx in range(2) for dy in range(2) for dc in range(2) if (dx, dy, dc) != (0, 0, 0)]
        for j, (dx, dy, dc) in enumerate(peers):
            _remote(land.at[me], land.at[me], send.at[j], recv.at[j], (x ^ dx, y ^ dy, c ^ dc)).start()
        for j, (dx, dy, dc) in enumerate(peers):
            src = 4 * (x ^ dx) + 2 * (y ^ dy) + (c ^ dc)
            _remote(land.at[me], land.at[me], send.at[j], recv.at[j], (x ^ dx, y ^ dy, c ^ dc)).wait_send()
            _remote(land.at[src], land.at[src], send.at[j], recv.at[j], (x ^ dx, y ^ dy, c ^ dc)).wait_recv()
        acc = land[0]
        for dev in range(1, n_dev):
            acc = acc + land[dev]
        o_ref[...] = acc
        if ride is not None:
            ride.finish([], r_bufs, [], r_sems)

    bufs = [] if ride is None else ride.bufs
    sems = [] if ride is None else [pltpu.SemaphoreType.DMA(s) for s in ride.sem_shapes]
    res = pl.pallas_call(
        body, in_specs=[pl.BlockSpec(memory_space=pltpu.VMEM)] + [ANY] * n_rb,
        out_specs=[pl.BlockSpec(memory_space=pltpu.VMEM)] + [ANY] * n_rb,
        out_shape=[SDS((rows, LANES), F32)] + [SDS(b.shape, b.dtype) for b in bufs],
        scratch_shapes=[pltpu.VMEM((n_dev, rows, LANES), F32), pltpu.SemaphoreType.DMA((n_dev - 1,)),
                        pltpu.SemaphoreType.DMA((n_dev - 1,))] + sems,
        input_output_aliases={1 + j: 1 + j for j in range(n_rb)},
        name="all_reduce_small", compiler_params=_params())(pack, *bufs)
    return res[0], list(res[1:])


def _add_own_half(name, full, recv, out_dtype):
    n4, _, h, cols = full.shape

    def body(s_ref, a_ref, b_ref, o_ref, own_ref):
        v = (a_ref[...].astype(F32) + b_ref[...].astype(F32)).astype(out_dtype)
        o_ref[...] = v

        @pl.when(pl.program_id(0) == s_ref[1])
        def _():
            own_ref[...] = v

    return pl.pallas_call(
        body,
        grid_spec=pltpu.PrefetchScalarGridSpec(
            num_scalar_prefetch=1, grid=(n4,),
            in_specs=[pl.BlockSpec((None, None, h, cols), lambda q, s: (q, s[0], 0, 0)),
                      pl.BlockSpec((None, h, cols), lambda q, s: (q, 0, 0))],
            out_specs=[pl.BlockSpec((None, h, cols), lambda q, s: (q, 0, 0)),
                       pl.BlockSpec((None, h, cols), lambda q, s: (s[1], 0, 0))]),
        out_shape=[SDS((n4, h, cols), out_dtype)] * 2, name=name, compiler_params=_params())(_mesh_scalars(), full, recv)


def _sum_chips(name, parts):
    n4, h, cols = parts.shape
    th = h // 4 if h % 64 == 0 else h

    def body(s_ref, a_ref, o_ref):
        acc = a_ref[0].astype(F32)
        for q in range(1, n4):
            acc = acc + a_ref[q].astype(F32)
        o_ref[...] = acc

    return pl.pallas_call(
        body,
        grid_spec=pltpu.PrefetchScalarGridSpec(
            num_scalar_prefetch=1, grid=(h // th,),
            in_specs=[pl.BlockSpec((n4, th, cols), lambda i, s: (0, i, 0))],
            out_specs=pl.BlockSpec((None, th, cols), lambda i, s: (s[0], i, 0))),
        out_shape=SDS((2, h, cols), F32), name=name, compiler_params=_params())(_mesh_scalars(), parts)


def _adamw_math(w, g, m, v):
    m2 = ADAM_B1 * m + (1.0 - ADAM_B1) * g
    v2 = ADAM_B2 * v + (1.0 - ADAM_B2) * (g * g)
    m_hat = m2 / (1.0 - ADAM_B1 ** ADAM_STEP)
    v_hat = v2 / (1.0 - ADAM_B2 ** ADAM_STEP)
    delta = -ADAM_LR * (m_hat / (jnp.sqrt(v_hat) + ADAM_EPS) + ADAM_WD * w)
    return delta, m2, v2


def _row_tile(rows, cols):
    cap = max(8, (1 << 18) // cols)
    best = 8
    for cand in range(8, min(rows, cap) + 1, 8):
        if rows % cand == 0:
            best = cand
    return best


def _adamw_big(name, w, g_layers, m, v):
    layers, rows, cols = w.shape
    tr = _row_tile(rows, cols)

    def body(w_ref, m_ref, v_ref, *rest):
        g_refs, (g_o, d_o, m_o, v_o) = rest[:layers], rest[layers:]
        gv = g_refs[0][...]
        for layer in range(1, layers):
            gv = jnp.where(pl.program_id(0) == layer, g_refs[layer][...], gv)
        d, mm, vv = _adamw_math(w_ref[...], gv, m_ref[...], v_ref[...])
        g_o[...] = gv
        d_o[...] = d
        m_o[...] = mm
        v_o[...] = vv

    blk = pl.BlockSpec((None, tr, cols), lambda l, i: (l, i, 0))
    g_blk = pl.BlockSpec((tr, cols), lambda l, i: (i, 0))
    return tuple(pl.pallas_call(
        body, grid=(layers, rows // tr), in_specs=[blk] * 3 + [g_blk] * layers, out_specs=[blk] * 4,
        out_shape=[SDS((layers, rows, cols), F32)] * 4, name=name,
        compiler_params=_params())(w, m, v, *[g.reshape(rows, cols) for g in g_layers]))


def _adamw_small(ws, gs, ms, vs):
    n = len(ws)
    flat = []
    for group in (ws, gs, ms, vs):
        flat += [a.reshape(-1, a.shape[-1]) for a in group]

    def body(*refs):
        w_r, g_r, m_r, v_r = refs[:n], refs[n:2 * n], refs[2 * n:3 * n], refs[3 * n:4 * n]
        d_o, m_o, v_o = refs[4 * n:5 * n], refs[5 * n:6 * n], refs[6 * n:7 * n]
        for j in range(n):
            d, mm, vv = _adamw_math(w_r[j][...], g_r[j][...], m_r[j][...], v_r[j][...])
            d_o[j][...] = d
            m_o[j][...] = mm
            v_o[j][...] = vv

    shapes = [SDS(a.shape, F32) for a in flat[:n]]
    outs = pl.pallas_call(body, out_shape=shapes * 3, name="adamw_small", compiler_params=_params())(*flat)
    res = []
    for k in range(3):
        res.append([outs[k * n + j].reshape(ws[j].shape) for j in range(n)])
    return res


BIG = ("ab_w_in", "ab_w_out", "cd_w_in", "cd_w_out", "ffn_w_gate", "ffn_w_up", "ffn_w_down")
V_BLOCK = (2 * A_WIDTH + QK_COLS) // B_WIDTH


def _pad_rows(a, rows):
    return jnp.pad(a, ((0, rows - a.shape[0]), (0, 0)))


A_IN, A_OUT, C_IN, C_OUT = ("ab_w_in", 0), ("ab_w_out", 0), ("cd_w_in", 0), ("cd_w_out", 0)
G0, U0, D0 = ("ffn_w_gate", 0), ("ffn_w_up", 0), ("ffn_w_down", 0)
G1, U1, D1 = ("ffn_w_gate", 1), ("ffn_w_up", 1), ("ffn_w_down", 1)
UNITS = (A_IN, A_OUT, G0, U0, D0, C_IN, C_OUT, G1, U1, D1)
ROWS_MINOR = ("ffn_w_gate", "ffn_w_up")
SMALL_SHARDED = ("small", 0)
REPLICATED_UNIT = ("replicated", 0)


class _Exchange:
    def __init__(self, enabled):
        self.enabled = enabled
        self.w, self.grad, self.recv, self.half, self.land, self.done = {}, {}, {}, {}, {}, {}

    def full(self, unit):
        b = self.w[unit]
        return b.reshape(N_CHIPS, 1, 2 * b.shape[2], b.shape[3])

    def _ride(self, phases):
        rides, sinks = [], []
        for kind, units in phases:
            if kind == "send":
                rides.append(_ride_gather_send([self.w[u] for u in units]))
                sinks.append(self.w)
            elif kind == "pass":
                rides.append(_ride_gather_pass([self.w[u] for u in units]))
                sinks.append(self.w)
            elif kind == "gather":
                rides.append(_ride_gather([self.w[u] for u in units]))
                sinks.append(self.w)
            elif kind == "swap":
                rides.append(_ride_swap([self.grad[u] for u in units]))
                sinks.append(self.recv)
            elif kind == "scatter":
                rides.append(_ride_scatter([self.half[u] for u in units], [self.land[u] for u in units]))
                sinks.append(self.land)
            else:
                rides.append(_ride_join([self.done[u] for u in units]))
                sinks.append(self.done)
        ride = functools.reduce(_ride_both, rides)

        def settle(res):
            n_bufs = sum(len(r.bufs) for r in rides)
            bufs, new = list(res[:n_bufs]), list(res[n_bufs:])
            for r, sink, (_, units) in zip(rides, sinks, phases):
                vals = [bufs.pop(0) for _ in r.bufs] + [new.pop(0) for _ in r.new_outs]
                for u, v in zip(units, vals):
                    sink[u] = v

        return ride, settle

    def run(self, fn, *args, phases=(), **kw):
        if not self.enabled or not phases:
            return fn(*args, **kw)
        ride, settle = self._ride(phases)
        out, res = fn(*args, ride=ride, **kw)
        settle(res)
        return out

    def alone(self, name, phases):
        if self.enabled:
            ride, settle = self._ride(phases)
            settle(_run_ride(name, ride))

    def pair_sum(self, units):
        if self.enabled:
            for u in units:
                dtype = F32 if u in (SMALL_SHARDED, REPLICATED_UNIT) else BF16
                self.half[u], self.land[u] = _add_own_half(f"pair_sum_{u[0]}_{u[1]}", self.grad[u], self.recv[u], dtype)

    def chip_sum(self, units):
        if self.enabled:
            for u in units:
                self.done[u] = _sum_chips(f"chip_sum_{u[0]}_{u[1]}", self.land[u])


def _local_step(x, target, ex, sp, h0=None):
    t, d = x.shape
    tabs = _rope_tables(t)
    gains = jnp.concatenate([jnp.tile(sp["q_norm_g"][g], HEAD_DIM // 8) for g in range(N_DIL)]
                            + [jnp.tile(sp["k_norm_g"][g], HEAD_DIM // 8) for g in range(N_DIL)]).reshape(1, QK_COLS)
    bias_t = sp["sgu_bias"].T
    cw = _pad_rows(sp["conv_c_w"], 32)
    dw = _pad_rows(sp["conv_d_w"], 8)
    cb, clg, clb = (sp[k].reshape(1, C_WIDTH) for k in ("conv_c_b", "c_ln_g", "c_ln_b"))
    slg, slb = sp["sgu_norm_g"].reshape(1, A_WIDTH), sp["sgu_norm_b"].reshape(1, A_WIDTH)
    g_ab, g_cd = sp["ab_norm_g"].reshape(1, d), sp["cd_norm_g"].reshape(1, d)
    g_f0, g_f1 = sp["ffn_norm_g"][0:1], sp["ffn_norm_g"][1:2]
    run = ex.run

    def w2d(unit):
        return ex.full(unit).reshape(-1, d)

    if h0 is None:
        h0 = _rms_fwd("rms_ab", x, g_ab)
    proj = run(_proj_in, "proj_ab", h0, ex.full(A_IN), 0, phases=[("send", [A_OUT, G0])])
    a_out = _mixer_a_fwd(proj, slg, slb, sp["sgu_w"], bias_t)
    qk, q1, q2, k1, k2 = run(_qk_fwd, proj, gains, tabs, phases=[("pass", [A_OUT, G0]), ("send", [U0])])
    regrouped_qk = {1: (q1, k1), 2: (q2, k2)}
    fwd_phases = ([("pass", [U0]), ("send", [D0])], [("pass", [D0]), ("send", [C_IN])],
                  [("pass", [C_IN]), ("send", [C_OUT])])
    qkv, o_list, l_list = [], [], []
    for g, rate in enumerate(DIL_RATES):
        if rate == 1:
            qk3, proj3 = qk.reshape(1, t, QK_COLS), proj.reshape(1, t, AB_IN)
            q, k, v = (qk3, g), (qk3, N_DIL + g), (proj3, V_BLOCK + g)
        else:
            vp, = _permute(f"regroup_v_{g}", [(proj, V_BLOCK + g)], rate)
            q, k, v = (regrouped_qk[g][0], 0), (regrouped_qk[g][1], 0), (vp, 0)
        qkv.append((q, k, v))
        o, l = run(_attn_fwd, f"attn_fwd_{g}", q, k, v, phases=fwd_phases[g])
        if rate == 1:
            o, l = o.reshape(t, B_WIDTH), l.reshape(t, B_WIDTH)
        o_list.append(o)
        l_list.append(l)
    cat, lse_tot, lse_1, lse_2 = _attn_merge(a_out, o_list, l_list)
    x1, hf0 = _proj_out("out_ab", cat, w2d(A_OUT), x, g_next=g_f0)
    fgate0, fup0, act0 = run(_ffn_in, "ffn_in_0", hf0, ex.full(G0), ex.full(U0), 0,
                           phases=[("pass", [C_OUT]), ("send", [G1, U1])])
    x2, h1 = run(_ffn_out, "ffn_out_0", act0, ex.full(D0), 0, x1, g_next=g_cd, phases=[("pass", [G1, U1]), ("send", [D1])])
    projcd = run(_proj_in, "proj_cd", h1, ex.full(C_IN), 0, phases=[("pass", [D1])])
    cat2, c1 = _mixer_cd_fwd(projcd, cw, cb, clg, clb, dw)
    x3, hf1 = _proj_out("out_cd", cat2, w2d(C_OUT), x2, g_next=g_f1)
    fgate1, fup1, act1 = _ffn_in("ffn_in_1", hf1, ex.full(G1), ex.full(U1), 0)
    dy, loss_acc, dy_b = _ffn_out("ffn_out_1", act1, ex.full(D1), 0, x3, target=target)
    loss = 0.5 * loss_acc[0, 0] / d

    late = [D1, G1, U1]
    dgate, dup = _ffn_dact("ffn_dact_1", dy_b, ex.full(D1), 0, fgate1, fup1)
    ex.grad[D1] = _wgrad_row_sharded("wgrad_down_1", act1, dy_b, True)
    ex.grad[G1] = _wgrad_row_sharded("wgrad_gate_1", dgate, hf1, True)
    ex.grad[U1] = _wgrad_row_sharded("wgrad_up_1", dup, hf1, True)
    g3, d_f1, g3_b = run(_dgrad_cols, "dgrad_ffn_1", [dgate, dup], [ex.full(G1), ex.full(U1)], 0, True, x3, g_f1, dy,
                         w_rows=True, phases=[("swap", late)])
    ex.pair_sum(late)

    dcat2 = _dgrad_rows("dgrad_out_cd", g3_b, w2d(C_OUT))
    ex.grad[C_OUT] = _wgrad_row_sharded("wgrad_out_cd", cat2, g3_b, False)
    dprojcd, d_cw, d_cb, d_clg, d_clb, d_dw = run(_mixer_cd_bwd, projcd, dcat2, c1, cw, clg, clb, dw, phases=[("scatter", late)])
    ex.chip_sum(late)
    ex.grad[C_IN] = _wgrad_col_sharded("wgrad_in_cd", h1, [dprojcd], False)[0]
    g2, d_cdn, g2_b = run(_dgrad_cols, "dgrad_in_cd", [dprojcd], [ex.full(C_IN)], 0, False, x2, g_cd, g3,
                          phases=[("join", late), ("swap", [C_OUT, C_IN])])
    ex.pair_sum([C_OUT, C_IN])

    dgate, dup = run(_ffn_dact, "ffn_dact_0", g2_b, ex.full(D0), 0, fgate0, fup0, phases=[("scatter", [C_OUT, C_IN])])
    ex.chip_sum([C_OUT, C_IN])
    ex.grad[D0] = _wgrad_row_sharded("wgrad_down_0", act0, g2_b, True)
    ex.grad[G0] = _wgrad_row_sharded("wgrad_gate_0", dgate, hf0, True)
    ex.grad[U0] = _wgrad_row_sharded("wgrad_up_0", dup, hf0, True)
    small = {"cd_norm_g": d_cdn, "conv_c_w": d_cw[:C_KERNEL], "conv_c_b": d_cb, "c_ln_g": d_clg, "c_ln_b": d_clb,
             "conv_d_w": d_dw[:D_KERNEL]}
    ex.grad[SMALL_SHARDED] = _split_full_small(small).reshape(N_CHIPS, 2, SHARDED_ROWS // 2, LANES)
    mid = [D0, G0, U0, SMALL_SHARDED]
    g1, d_f0, g1_b = run(_dgrad_cols, "dgrad_ffn_0", [dgate, dup], [ex.full(G0), ex.full(U0)], 0, True, x1, g_f0, g2,
                         w_rows=True, phases=[("join", [C_OUT, C_IN]), ("swap", mid)])
    ex.pair_sum(mid)

    dcat = _dgrad_rows("dgrad_out_ab", g1_b, w2d(A_OUT))
    ex.grad[A_OUT] = _wgrad_row_sharded("wgrad_out_ab", cat, g1_b, False)
    d_a, d_sw, d_sbt, d_slg, d_slb = _mixer_a_bwd(proj, dcat, slg, slb, sp["sgu_w"], bias_t)
    early = {"sgu_norm_g": d_slg, "sgu_norm_b": d_slb, "sgu_w": d_sw, "sgu_bias": d_sbt.T}
    ex.grad[REPLICATED_UNIT] = jnp.broadcast_to(
        _pack_replicated(early, REPLICATED_EARLY, REPLICATED_EARLY_ROWS).reshape(2, REPLICATED_EARLY_ROWS // 2, LANES),
        (N_CHIPS, 2, REPLICATED_EARLY_ROWS // 2, LANES))
    last = [A_OUT, REPLICATED_UNIT]
    dbb, dd, db_1, dd_1, db_2, dd_2 = _attn_bwd_prep(dcat, cat)
    regrouped_bwd = {1: (db_1, lse_1, dd_1), 2: (db_2, lse_2, dd_2)}
    bwd_phases = ([("scatter", [D0, SMALL_SHARDED])],
                  [("scatter", [G0]), ("join", [D0, SMALL_SHARDED]), ("swap", last)],
                  [("scatter", [U0]), ("join", [G0])])
    dqs, dks, dvs = [], [], []
    for g, rate in enumerate(DIL_RATES):
        q, k, v = qkv[g]
        if rate == 1:
            db3, l3, dd3 = (a.reshape(1, t, B_WIDTH) for a in (dbb, lse_tot, dd))
        else:
            db3, l3, dd3 = regrouped_bwd[g]
        if g == 1:
            ex.chip_sum([D0, SMALL_SHARDED])
        elif g == 2:
            ex.chip_sum([G0])
            ex.pair_sum(last)
        dq, dk, dv = run(_attn_bwd, f"attn_bwd_{g}", q, k, v, db3, l3, dd3, phases=bwd_phases[g])
        if rate == 1:
            dq, dk, dv = (a.reshape(t, B_WIDTH) for a in (dq, dk, dv))
        dqs.append(dq)
        dks.append(dk)
        dvs.append(dv)
    ex.chip_sum([U0])
    dproj, d_gains = run(_dproj_assemble, proj, d_a, dqs, dks, dvs, gains, tabs, phases=[("scatter", last), ("join", [U0])])
    ex.chip_sum(last)
    d_gains = _fold_heads(d_gains)[0].reshape(2, N_DIL, B_WIDTH)[:, :, :HEAD_DIM]
    ex.grad[A_IN] = _wgrad_col_sharded("wgrad_in_ab", h0, [dproj], False)[0]
    ex.alone("swap_last", [("swap", [A_IN])])
    ex.pair_sum([A_IN])
    gx, d_abn = run(_dgrad_cols, "dgrad_in_ab", [dproj], [ex.full(A_IN)], 0, False, x, g_ab, g1, bf16_copy=False,
                    phases=[("join", last), ("scatter", [A_IN])])
    ex.chip_sum([A_IN])

    small.update({
        "ab_norm_g": d_abn, "sgu_norm_g": d_slg, "sgu_norm_b": d_slb, "sgu_w": d_sw, "sgu_bias": d_sbt.T,
        "q_norm_g": d_gains[0], "k_norm_g": d_gains[1], "ffn_norm_g": jnp.concatenate([d_f0, d_f1], axis=0),
    })
    return loss, gx, small


SHARDED_SMALL = ("cd_norm_g", "conv_c_w", "conv_c_b", "c_ln_g", "c_ln_b", "conv_d_w")
SHARDED_ROWS = 48
REPLICATED_EARLY = ("sgu_norm_g", "sgu_norm_b", "sgu_w", "sgu_bias")
REPLICATED_EARLY_ROWS = 528
REPLICATED_LATE = ("ab_norm_g", "q_norm_g", "k_norm_g", "ffn_norm_g", "loss")
REPLICATED_LATE_ROWS = 32
REPLICATED_SMALL = REPLICATED_EARLY + REPLICATED_LATE[:-1]


def _pack_sharded(parts):
    rows = [parts[k].reshape(-1, LANES) for k in SHARDED_SMALL]
    return _pad_rows(jnp.concatenate(rows, axis=0), SHARDED_ROWS)


def _split_full_small(small):
    per_chip = []
    for q in range(N_CHIPS):
        parts = {}
        for k in SHARDED_SMALL:
            a = small[k]
            a = a.reshape(-1, a.shape[-1])
            n = a.shape[-1] // N_CHIPS
            parts[k] = a[:, q * n:(q + 1) * n]
        per_chip.append(_pack_sharded(parts))
    return jnp.stack(per_chip)


def _unpack_sharded(pack, shapes):
    out, r = {}, 0
    for k in SHARDED_SMALL:
        n = math.prod(shapes[k]) // LANES
        out[k] = pack[r:r + n].reshape(shapes[k])
        r += n
    return out


def _gathered_small(packs, shapes):
    per_chip = [_unpack_sharded(packs[q], shapes) for q in range(N_CHIPS)]
    return {k: jnp.concatenate([pc[k] for pc in per_chip], axis=-1) for k in SHARDED_SMALL}


def _pack_replicated(small, names, total_rows):
    rows = []
    for k in names:
        a = small[k].reshape(-1)
        a = jnp.pad(a, (0, (-a.shape[0]) % LANES))
        rows.append(a.reshape(-1, LANES))
    return _pad_rows(jnp.concatenate(rows, axis=0), total_rows)


def _unpack_replicated(pack, shapes, names):
    out, r = {}, 0
    for k in names:
        size = math.prod(shapes[k])
        n = -(-size // LANES)
        out[k] = pack[r:r + n].reshape(-1)[:size].reshape(shapes[k])
        r += n
    return out


WEIGHT_ORDER = ("ab_norm_g", "ab_w_in", "sgu_norm_g", "sgu_norm_b", "sgu_w", "sgu_bias", "q_norm_g", "k_norm_g", "ab_w_out",
                "cd_norm_g", "cd_w_in", "conv_c_w", "conv_c_b", "c_ln_g", "c_ln_b", "conv_d_w", "cd_w_out", "ffn_norm_g",
                "ffn_w_gate", "ffn_w_up", "ffn_w_down")


def kernel(x, ab_norm_g, ab_w_in, sgu_norm_g, sgu_norm_b, sgu_w, sgu_bias, q_norm_g, k_norm_g, ab_w_out, cd_norm_g, cd_w_in, conv_c_w, conv_c_b, c_ln_g, c_ln_b, conv_d_w, cd_w_out, ffn_norm_g, ffn_w_gate, ffn_w_up, ffn_w_down, loss_target, m_ab_norm_g, m_ab_w_in, m_sgu_norm_g, m_sgu_norm_b, m_sgu_w, m_sgu_bias, m_q_norm_g, m_k_norm_g, m_ab_w_out, m_cd_norm_g, m_cd_w_in, m_conv_c_w, m_conv_c_b, m_c_ln_g, m_c_ln_b, m_conv_d_w, m_cd_w_out, m_ffn_norm_g, m_ffn_w_gate, m_ffn_w_up, m_ffn_w_down, v_ab_norm_g, v_ab_w_in, v_sgu_norm_g, v_sgu_norm_b, v_sgu_w, v_sgu_bias, v_q_norm_g, v_k_norm_g, v_ab_w_out, v_cd_norm_g, v_cd_w_in, v_conv_c_w, v_conv_c_b, v_c_ln_g, v_c_ln_b, v_conv_d_w, v_cd_w_out, v_ffn_norm_g, v_ffn_w_gate, v_ffn_w_up, v_ffn_w_down):
    args = dict(locals())
    ws = {k: args[k] for k in WEIGHT_ORDER}
    ms = {k: args["m_" + k] for k in WEIGHT_ORDER}
    vs = {k: args["v_" + k] for k in WEIGHT_ORDER}
    small_names = [k for k in WEIGHT_ORDER if k not in BIG]
    t, d = x.shape[1:]

    for group in (ws, ms, vs):
        for k in ROWS_MINOR:
            group[k] = jnp.swapaxes(group[k], 1, 2)
    ex = _Exchange(enabled=True)
    ex.w[A_IN] = _stage_own("stage_ab_w_in", ws["ab_w_in"], 0, BF16)
    own_small = _pack_sharded({k: ws[k][0] for k in SHARDED_SMALL})
    ex.w[SMALL_SHARDED] = _stage_own("stage_small", own_small[None], 0, F32)
    rest = [u for u in UNITS if u != A_IN]
    x2 = x.reshape(t, d)
    h0, staged = ex.run(_stage_rest_and_norm, x2, ws["ab_norm_g"], [(ws[name], layer) for name, layer in rest],
                        phases=[("gather", [A_IN, SMALL_SHARDED])])
    ex.w.update(zip(rest, staged))
    sp = _gathered_small(ex.w[SMALL_SHARDED].reshape(N_CHIPS, SHARDED_ROWS, LANES), {k: ws[k].shape[1:] for k in SHARDED_SMALL})
    for k in REPLICATED_SMALL:
        sp[k] = ws[k] if k == "ffn_norm_g" else ws[k][0]

    loss, grad_x, g_small = _local_step(x2, loss_target.reshape(t, d), ex, sp, h0)

    shapes = {k: ws[k].shape for k in REPLICATED_SMALL}
    shapes["loss"] = (1,)
    g_small["loss"] = loss
    join_last, settle = ex._ride([("join", [A_IN])])
    late, joined = _all_reduce_small(_pack_replicated(g_small, REPLICATED_LATE, REPLICATED_LATE_ROWS), join_last)
    settle(joined)
    grad = _unpack_sharded(ex.done[SMALL_SHARDED].reshape(SHARDED_ROWS, LANES), {k: ws[k].shape for k in SHARDED_SMALL})
    grad.update(_unpack_replicated(ex.done[REPLICATED_UNIT].reshape(REPLICATED_EARLY_ROWS, LANES), shapes, REPLICATED_EARLY))
    grad.update(_unpack_replicated(late, shapes, REPLICATED_LATE))
    loss = grad.pop("loss")[0]

    delta, new_m, new_v = {}, {}, {}
    for k in BIG:
        g_layers = [ex.done[(k, layer)] for layer in range(ws[k].shape[0])]
        outs = _adamw_big("adamw_" + k, ws[k], g_layers, ms[k], vs[k])
        if k in ROWS_MINOR:
            outs = [jnp.swapaxes(o, 1, 2) for o in outs]
        grad[k], delta[k], new_m[k], new_v[k] = outs
    d_s, m_s, v_s = _adamw_small([ws[k] for k in small_names], [grad[k] for k in small_names],
                                 [ms[k] for k in small_names], [vs[k] for k in small_names])
    for j, k in enumerate(small_names):
        delta[k], new_m[k], new_v[k] = d_s[j], m_s[j], v_s[j]

    return (loss, grad_x[None], *[grad[k] for k in WEIGHT_ORDER], *[delta[k] for k in WEIGHT_ORDER],
            *[new_m[k] for k in WEIGHT_ORDER], *[new_v[k] for k in WEIGHT_ORDER])
```

```python
import functools
import math

import jax
import jax.numpy as jnp
from jax import lax
from jax.experimental import pallas as pl
from jax.experimental.pallas import tpu as pltpu

F32 = jnp.float32
BF16 = jnp.bfloat16
SDS = jax.ShapeDtypeStruct

N_CHIPS = 4
EPS = 1e-6
NEG_INF = -1e30
CHUNK = 128
A_GROUPS = 4
A_WIDTH = 512
N_DIL = 3
DIL_RATES = (1, 4, 16)
HEAD_DIM = 64
B_WIDTH = 512
ROPE_DIM = 16
ROPE_THETA = 500000.0
C_WIDTH = 512
C_KERNEL = 31
D_KERNEL = 3
HALO = 32
ATT_BLOCK = 128
LANES = 128

ADAM_LR = 0.001
ADAM_B1 = 0.9
ADAM_B2 = 0.999
ADAM_EPS = 1e-08
ADAM_WD = 0.01
ADAM_STEP = 10

VMEM_LIMIT = 56 * 1024 * 1024

NN = (((1,), (0,)), ((), ()))
NT = (((1,), (1,)), ((), ()))
TN = (((0,), (0,)), ((), ()))

TILES = {"proj_in": 1024, "proj_out": 1024, "ffn_in": 1024, "ffn_out": 512, "ffn_dact": 512, "dgrad_cols": 512,
         "dgrad_rows": 1024, "wgrad": 4096}


def _params(sem=None, collective_id=None):
    return pltpu.CompilerParams(dimension_semantics=sem, vmem_limit_bytes=VMEM_LIMIT, collective_id=collective_id)


def _bf(v):
    return v if v.dtype == BF16 else v.astype(BF16)


def _dot(a, b, dims):
    return lax.dot_general(_bf(a), _bf(b), dims, preferred_element_type=F32)


def _dot_hi(a, b):
    return jnp.dot(a, b, precision=lax.Precision.HIGHEST, preferred_element_type=F32)


def _sigmoid(v):
    return 0.5 * jnp.tanh(0.5 * v) + 0.5


def _gelu(v):
    return 0.5 * v * (1.0 + lax.erf(v * (1.0 / math.sqrt(2.0))))


def _gelu_grad(v):
    cdf = 0.5 * (1.0 + lax.erf(v * (1.0 / math.sqrt(2.0))))
    return cdf + v * jnp.exp(-0.5 * v * v) * (1.0 / math.sqrt(2.0 * math.pi))


def _segment_mean_matrix(seg, scale=None):
    r = lax.broadcasted_iota(jnp.int32, (LANES, LANES), 0) // seg
    c = lax.broadcasted_iota(jnp.int32, (LANES, LANES), 1) // seg
    return jnp.where(r == c, (1.0 / seg) if scale is None else scale, 0.0).astype(BF16)


def _segment_dot(v, seg):
    hi = v.astype(BF16)
    lo = (v - hi.astype(F32)).astype(BF16)
    return jnp.dot(hi, seg, preferred_element_type=F32) + jnp.dot(lo, seg, preferred_element_type=F32)


MESH = pl.DeviceIdType.MESH
ANY = pl.BlockSpec(memory_space=pl.ANY)


def _position():
    x, y, c = lax.axis_index("x"), lax.axis_index("y"), lax.axis_index("c")
    others = [(1 - x, y), (x, 1 - y), (1 - x, 1 - y)]
    return x, y, c, 2 * x + y, others


class _Ride:
    def __init__(self, ins, bufs, new_outs, sem_shapes, start, finish, reach):
        self.ins, self.bufs, self.new_outs, self.sem_shapes = list(ins), list(bufs), list(new_outs), list(sem_shapes)
        self.start, self.finish = start, finish
        self.reach = frozenset(reach)

    def entry_barrier(self):
        x, y, c, _, others = _position()
        peers = ([(x, y, 1 - c)] if "sibling" in self.reach else []) + ([(qx, qy, c) for qx, qy in others] if "chips" in self.reach else [])
        barrier = pltpu.get_barrier_semaphore()
        for peer in peers:
            pl.semaphore_signal(barrier, inc=1, device_id=peer, device_id_type=MESH)
        pl.semaphore_wait(barrier, len(peers))

    @property
    def collective_id(self):
        return {frozenset(["sibling"]): 0, frozenset(["chips"]): 1, frozenset(["sibling", "chips"]): 2}[self.reach]


def _ride_both(a, b):
    na = (len(a.ins), len(a.bufs), len(a.new_outs), len(a.sem_shapes))

    def split(ins, bufs, new, sems):
        return ((ins[:na[0]], bufs[:na[1]], new[:na[2]], sems[:na[3]]), (ins[na[0]:], bufs[na[1]:], new[na[2]:], sems[na[3]:]))

    def start(*refs):
        ra, rb = split(*refs)
        a.start(*ra)
        b.start(*rb)

    def finish(*refs):
        ra, rb = split(*refs)
        a.finish(*ra)
        b.finish(*rb)

    return _Ride(a.ins + b.ins, a.bufs + b.bufs, a.new_outs + b.new_outs, a.sem_shapes + b.sem_shapes, start, finish,
                 a.reach | b.reach)


def _call(body, *, grid, in_specs, out_specs, out_shape, operands, name, scratch_shapes=(), aliases=None, ride=None,
          prefetch=None):
    off = 0 if prefetch is None else 1
    lead = [] if prefetch is None else [prefetch]

    params = _params(collective_id=None if ride is None else ride.collective_id)

    def launch(kernel_body, in_specs_, out_specs_, out_shape_, scratch_, aliases_, *args):
        if prefetch is None:
            return pl.pallas_call(kernel_body, grid=grid, in_specs=in_specs_, out_specs=out_specs_, out_shape=out_shape_,
                                  scratch_shapes=scratch_, input_output_aliases=aliases_, name=name,
                                  compiler_params=params)(*args)
        spec = pltpu.PrefetchScalarGridSpec(num_scalar_prefetch=1, grid=grid, in_specs=in_specs_, out_specs=out_specs_,
                                            scratch_shapes=scratch_)
        return pl.pallas_call(kernel_body, grid_spec=spec, out_shape=out_shape_, input_output_aliases=aliases_, name=name,
                              compiler_params=params)(*lead, *args)

    if ride is None:
        return launch(body, list(in_specs), out_specs, out_shape, list(scratch_shapes), dict(aliases or {}), *operands)
    multi = isinstance(out_shape, (list, tuple))
    out_shapes = list(out_shape) if multi else [out_shape]
    o_specs = list(out_specs) if multi else [out_specs]
    n_in, n_out, n_scr = off + len(operands), len(out_shapes), len(scratch_shapes)
    n_ri, n_rb, n_rn = len(ride.ins), len(ride.bufs), len(ride.new_outs)

    def carrying(*refs):
        k = n_in
        r_ins = refs[k:k + n_ri]
        k += n_ri + n_rb
        outs = refs[k:k + n_out]
        k += n_out
        r_bufs = refs[k:k + n_rb]
        k += n_rb
        r_new = refs[k:k + n_rn]
        k += n_rn
        scratch = refs[k:k + n_scr]
        sems = refs[k + n_scr:]
        first, last = None, None
        for axis, size in enumerate(grid):
            pid = pl.program_id(axis)
            first = (pid == 0) if first is None else first & (pid == 0)
            last = (pid == size - 1) if last is None else last & (pid == size - 1)

        @pl.when(first)
        def _():
            ride.entry_barrier()
            ride.start(r_ins, r_bufs, r_new, sems)

        body(*refs[:n_in], *outs, *scratch)

        @pl.when(last)
        def _():
            ride.finish(r_ins, r_bufs, r_new, sems)

    all_aliases = dict(aliases or {})
    for j in range(n_rb):
        all_aliases[n_in + n_ri + j] = n_out + j
    res = launch(
        carrying, list(in_specs) + [ANY] * (n_ri + n_rb), o_specs + [ANY] * (n_rb + n_rn),
        out_shapes + [SDS(b.shape, b.dtype) for b in ride.bufs] + ride.new_outs,
        list(scratch_shapes) + [pltpu.SemaphoreType.DMA(s) for s in ride.sem_shapes], all_aliases,
        *operands, *ride.ins, *ride.bufs)
    outs = res[:n_out]
    return (list(outs) if multi else outs[0]), list(res[n_out:])


def _run_ride(name, ride):
    n_ri, n_rb, n_rn = len(ride.ins), len(ride.bufs), len(ride.new_outs)

    def body(*refs):
        r_ins = refs[:n_ri]
        r_bufs = refs[n_ri + n_rb:n_ri + 2 * n_rb]
        r_new = refs[n_ri + 2 * n_rb:n_ri + 2 * n_rb + n_rn]
        sems = refs[n_ri + 2 * n_rb + n_rn:]
        ride.entry_barrier()
        ride.start(r_ins, r_bufs, r_new, sems)
        ride.finish(r_ins, r_bufs, r_new, sems)

    return list(pl.pallas_call(
        body, in_specs=[ANY] * (n_ri + n_rb), out_specs=[ANY] * (n_rb + n_rn),
        out_shape=[SDS(b.shape, b.dtype) for b in ride.bufs] + ride.new_outs,
        scratch_shapes=[pltpu.SemaphoreType.DMA(s) for s in ride.sem_shapes],
        input_output_aliases={n_ri + j: j for j in range(n_rb)}, name=name,
        compiler_params=pltpu.CompilerParams(collective_id=ride.collective_id))(*ride.ins, *ride.bufs))


def _whole(ref, p):
    return ref[...]


def _slab(ref, p):
    return ref[p]


def _matmul(name, grid, pairs, extras, outs, dims, epi, *, slabs=1, n_acc=1, ride=None):
    n_pairs, n_ex, n_out = len(pairs), len(extras), len(outs)

    def body(*refs):
        ab = refs[:2 * n_pairs]
        ex = refs[2 * n_pairs:2 * n_pairs + n_ex]
        out_refs = refs[2 * n_pairs + n_ex:2 * n_pairs + n_ex + n_out]
        pids = tuple(pl.program_id(a) for a in range(len(grid)))
        parts = [None] * n_acc
        for p in range(slabs):
            for j, (_, _, a_pick, _, _, b_pick, acc) in enumerate(pairs):
                d = _dot(a_pick(ab[2 * j], p), b_pick(ab[2 * j + 1], p), dims)
                parts[acc] = d if parts[acc] is None else parts[acc] + d
        epi(parts, ex, out_refs, pids)

    operands, in_specs = [], []
    for a, a_spec, _, b, b_spec, _, _ in pairs:
        operands += [a, b]
        in_specs += [a_spec, b_spec]
    for e, e_spec in extras:
        operands.append(e)
        in_specs.append(e_spec)
    return _call(body, grid=grid, in_specs=in_specs, out_specs=[o[1] for o in outs], out_shape=[o[0] for o in outs],
                 operands=operands, name=name, ride=ride)


def _rms_rows(v, g):
    r = lax.rsqrt(jnp.mean(v * v, axis=-1, keepdims=True) + EPS)
    return v * r * g


def _rms_fwd(name, x, g):
    t, d = x.shape
    tm = 512

    def body(x_ref, g_ref, o_ref):
        o_ref[...] = _rms_rows(x_ref[...], g_ref[...]).astype(BF16)

    return pl.pallas_call(
        body, grid=(t // tm,),
        in_specs=[pl.BlockSpec((tm, d), lambda i: (i, 0)), pl.BlockSpec((1, d), lambda i: (0, 0))],
        out_specs=pl.BlockSpec((tm, d), lambda i: (i, 0)), out_shape=SDS((t, d), BF16), name=name,
        compiler_params=_params())(x, g)


def _epi_residual_norm(accs, ex, outs, pids):
    x_new = accs[0] + ex[0][...]
    outs[0][...] = x_new
    outs[1][...] = _rms_rows(x_new, ex[1][...]).astype(BF16)


def _epi_residual_loss(accs, ex, outs, pids):
    y = accs[0] + ex[0][...]
    err = y - ex[1][...]
    dy = err * (1.0 / err.shape[-1])
    outs[0][...] = dy
    outs[2][...] = dy.astype(BF16)

    @pl.when(pids[0] == 0)
    def _():
        outs[1][...] = jnp.zeros_like(outs[1])

    outs[1][...] += jnp.sum(err * err)


def _epi_rms_bwd(accs, ex, outs, pids):
    dh = accs[0]
    xv, g, res = ex[0][...], ex[1][...], ex[2][...]
    r = lax.rsqrt(jnp.mean(xv * xv, axis=-1, keepdims=True) + EPS)
    xh = xv * r
    dy = dh * g
    dx = res + r * (dy - xh * jnp.mean(dy * xh, axis=-1, keepdims=True))
    outs[0][...] = dx
    if len(outs) > 2:
        outs[2][...] = dx.astype(BF16)

    @pl.when(pids[0] == 0)
    def _():
        outs[1][...] = jnp.zeros_like(outs[1])

    outs[1][...] += jnp.sum(dh * xh, axis=0, keepdims=True)


def _row_spec(tm, d):
    return pl.BlockSpec((tm, d), lambda i, *_: (i, 0))


def _const_spec(shape):
    nd = len(shape)
    return pl.BlockSpec(shape, lambda *_: (0,) * nd)


def _proj_in(name, h, w, layer, ride=None):
    t, d = h.shape
    n4 = w.shape[-1]
    tm = TILES["proj_in"]

    def epi(accs, ex, outs, pids):
        outs[0][...] = accs[0].astype(BF16)

    res = _matmul(
        name, (N_CHIPS, t // tm),
        [(h, pl.BlockSpec((tm, d), lambda p, i: (i, 0)), _whole,
          w, pl.BlockSpec((None, None, d, n4), lambda p, i: (p, layer, 0, 0)), _whole, 0)],
        [], [(SDS((t, N_CHIPS * n4), BF16), pl.BlockSpec((tm, n4), lambda p, i: (i, p)))],
        NN, epi, ride=ride)
    return res[0] if ride is None else (res[0][0], res[1])


def _proj_out(name, a, w, x, g_next=None, target=None):
    t, k = a.shape
    d = w.shape[-1]
    tm = TILES["proj_out"]
    if target is None:
        extras = [(x, _row_spec(tm, d)), (g_next, _const_spec((1, d)))]
        outs = [(SDS((t, d), F32), _row_spec(tm, d)), (SDS((t, d), BF16), _row_spec(tm, d))]
        epi = _epi_residual_norm
    else:
        extras = [(x, _row_spec(tm, d)), (target, _row_spec(tm, d))]
        outs = [(SDS((t, d), F32), _row_spec(tm, d)), (SDS((8, LANES), F32), _const_spec((8, LANES))),
                (SDS((t, d), BF16), _row_spec(tm, d))]
        epi = _epi_residual_loss
    return _matmul(name, (t // tm,), [(a, _row_spec(tm, k), _whole, w, _const_spec((k, d)), _whole, 0)], extras, outs, NN, epi)


def _ffn_in(name, h, wg, wu, layer, ride=None):
    t, d = h.shape
    n4 = wg.shape[-2]
    tm = TILES["ffn_in"]

    def epi(accs, ex, outs, pids):
        gate, up = accs
        s = _sigmoid(gate)
        silu = gate * s
        outs[0][...] = (up * (s + silu - silu * s)).astype(BF16)
        outs[1][...] = silu.astype(BF16)
        outs[2][...] = (silu * up).astype(BF16)

    w_spec = pl.BlockSpec((None, None, n4, d), lambda p, i: (p, layer, 0, 0))
    h_spec = pl.BlockSpec((tm, d), lambda p, i: (i, 0))
    o = (SDS((N_CHIPS, t, n4), BF16), pl.BlockSpec((None, tm, n4), lambda p, i: (p, i, 0)))
    return _matmul(name, (N_CHIPS, t // tm),
                   [(h, h_spec, _whole, wg, w_spec, _whole, 0), (h, h_spec, _whole, wu, w_spec, _whole, 1)], [],
                   [o, o, o], NT, epi, n_acc=2, ride=ride)


def _ffn_out(name, act, wd, layer, x, g_next=None, target=None, ride=None):
    _, t, n4 = act.shape
    d = wd.shape[-1]
    tm = TILES["ffn_out"]
    xs = _row_spec(tm, d)
    if target is None:
        extras = [(x, xs), (g_next, _const_spec((1, d)))]
        outs = [(SDS((t, d), F32), xs), (SDS((t, d), BF16), xs)]
        epi = _epi_residual_norm
    else:
        extras = [(x, xs), (target, xs)]
        outs = [(SDS((t, d), F32), xs), (SDS((8, LANES), F32), _const_spec((8, LANES))), (SDS((t, d), BF16), xs)]
        epi = _epi_residual_loss
    return _matmul(
        name, (t // tm,),
        [(act, pl.BlockSpec((N_CHIPS, tm, n4), lambda i: (0, i, 0)), _slab,
          wd, pl.BlockSpec((N_CHIPS, None, n4, d), lambda i: (0, layer, 0, 0)), _slab, 0)],
        extras, outs, NN, epi, slabs=N_CHIPS, ride=ride)


def _ffn_dact(name, g, wd, layer, gate, up, ride=None):
    t, d = g.shape
    n4 = wd.shape[-2]
    tm = TILES["ffn_dact"]

    def body(g_ref, w_ref, gate_ref, up_ref, dgate_ref, dup_ref):
        gv = g_ref[...]
        for p in range(N_CHIPS):
            dact = _dot(gv, w_ref[p], NT)
            dgate_ref[p] = (dact * gate_ref[p].astype(F32)).astype(BF16)
            dup_ref[p] = (dact * up_ref[p].astype(F32)).astype(BF16)

    blk = pl.BlockSpec((N_CHIPS, tm, n4), lambda i: (0, i, 0))
    return _call(
        body, grid=(t // tm,),
        in_specs=[_row_spec(tm, d), pl.BlockSpec((N_CHIPS, None, n4, d), lambda i: (0, layer, 0, 0)), blk, blk],
        out_specs=[blk, blk], out_shape=[SDS((N_CHIPS, t, n4), BF16)] * 2, operands=[g, wd, gate, up], name=name, ride=ride)


def _copy_epi(accs, ex, outs, pids):
    for a, o in zip(accs, outs):
        o[...] = a.astype(o.dtype)


def _dgrad_cols(name, dz_list, w_list, layer, three_d, x, g, res, bf16_copy=True, w_rows=False, ride=None):
    t, d = x.shape
    n4 = w_list[0].shape[-2 if w_rows else -1]
    tm = TILES["dgrad_cols"]
    if three_d:
        zs, z_pick = pl.BlockSpec((N_CHIPS, tm, n4), lambda i: (0, i, 0)), _slab
    else:
        zs, z_pick = _row_spec(tm, N_CHIPS * n4), (lambda ref, p: ref[:, p * n4:(p + 1) * n4])
    ws = pl.BlockSpec((N_CHIPS, None) + ((n4, d) if w_rows else (d, n4)), lambda i: (0, layer, 0, 0))
    xs = _row_spec(tm, d)
    return _matmul(
        name, (t // tm,), [(dz, zs, z_pick, w, ws, _slab, 0) for dz, w in zip(dz_list, w_list)],
        [(x, xs), (g, _const_spec((1, d))), (res, xs)],
        [(SDS((t, d), F32), xs), (SDS((1, d), F32), _const_spec((1, d)))] + ([(SDS((t, d), BF16), xs)] if bf16_copy else []),
        NN if w_rows else NT, _epi_rms_bwd, slabs=N_CHIPS, ride=ride)


def _dgrad_rows(name, g, w):
    t, d = g.shape
    k = w.shape[0]
    tm = TILES["dgrad_rows"]
    return _matmul(name, (t // tm,), [(g, _row_spec(tm, d), _whole, w, _const_spec((k, d)), _whole, 0)], [],
                   [(SDS((t, k), F32), _row_spec(tm, k))], NT, _copy_epi)[0]


A_TILE = 256


def _a_common(p_ref, lg_ref, lb_ref):
    pv = p_ref[...].astype(F32)
    a = _gelu(pv)
    u, v = a[:, :A_WIDTH], a[:, A_WIDTH:]
    vc = v - jnp.mean(v, axis=-1, keepdims=True)
    rs = lax.rsqrt(jnp.mean(vc * vc, axis=-1, keepdims=True) + EPS)
    vhat = vc * rs
    vn = vhat * lg_ref[...] + lb_ref[...]
    return pv, u, vhat, rs, vn.astype(BF16)


def _tril_weights(w_ref, g):
    r = lax.broadcasted_iota(jnp.int32, (CHUNK, CHUNK), 0)
    c = lax.broadcasted_iota(jnp.int32, (CHUNK, CHUNK), 1)
    return jnp.where(c <= r, w_ref[g], 0.0).astype(BF16), c <= r


def _mixer_a_fwd(proj, lg, lb, w, bias_t):
    t = proj.shape[0]

    def body(p_ref, lg_ref, lb_ref, w_ref, bt_ref, o_ref):
        _, u, _, _, vnb = _a_common(p_ref, lg_ref, lb_ref)
        for g in range(A_GROUPS):
            wt, _ = _tril_weights(w_ref, g)
            cs = slice(g * CHUNK, (g + 1) * CHUNK)
            for ch in range(A_TILE // CHUNK):
                rs_ = slice(ch * CHUNK, (ch + 1) * CHUNK)
                mixed = _dot(wt, vnb[rs_, cs], NN) + bt_ref[:, g:g + 1]
                o_ref[rs_, cs] = (u[rs_, cs] * mixed).astype(BF16)

    return pl.pallas_call(
        body, grid=(t // A_TILE,),
        in_specs=[pl.BlockSpec((A_TILE, 2 * A_WIDTH), lambda i: (i, 0)), _const_spec((1, A_WIDTH)),
                  _const_spec((1, A_WIDTH)), _const_spec((A_GROUPS, CHUNK, CHUNK)), _const_spec((CHUNK, A_GROUPS))],
        out_specs=pl.BlockSpec((A_TILE, A_WIDTH), lambda i: (i, 0)), out_shape=SDS((t, A_WIDTH), BF16),
        name="mixer_a_fwd", compiler_params=_params())(proj, lg, lb, w, bias_t)


def _mixer_a_bwd(proj, dcat, lg, lb, w, bias_t):
    t = proj.shape[0]

    def body(p_ref, da_ref, lg_ref, lb_ref, w_ref, bt_ref, dp_ref, dw_ref, dbt_ref, dlg_ref, dlb_ref, du_scr, dvn_scr):
        @pl.when(pl.program_id(0) == 0)
        def _():
            dw_ref[...] = jnp.zeros_like(dw_ref)
            dbt_ref[...] = jnp.zeros_like(dbt_ref)
            dlg_ref[...] = jnp.zeros_like(dlg_ref)
            dlb_ref[...] = jnp.zeros_like(dlb_ref)

        pv, u, vhat, rs, vnb = _a_common(p_ref, lg_ref, lb_ref)
        da = da_ref[...]
        for g in range(A_GROUPS):
            wt, keep = _tril_weights(w_ref, g)
            cs = slice(g * CHUNK, (g + 1) * CHUNK)
            for ch in range(A_TILE // CHUNK):
                rs_ = slice(ch * CHUNK, (ch + 1) * CHUNK)
                vg = vnb[rs_, cs]
                mixed = _dot(wt, vg, NN) + bt_ref[:, g:g + 1]
                du_scr[rs_, cs] = da[rs_, cs] * mixed
                dmx = da[rs_, cs] * u[rs_, cs]
                dw_ref[g] += jnp.where(keep, _dot(dmx, vg, NT), 0.0)
                dvn_scr[rs_, cs] = _dot(wt, dmx, TN)
                dbt_ref[:, g:g + 1] += jnp.sum(dmx, axis=1, keepdims=True)
        dvn = dvn_scr[...]
        dlg_ref[...] += jnp.sum(dvn * vhat, axis=0, keepdims=True)
        dlb_ref[...] += jnp.sum(dvn, axis=0, keepdims=True)
        dvh = dvn * lg_ref[...]
        dv = rs * (dvh - jnp.mean(dvh, axis=-1, keepdims=True) - vhat * jnp.mean(dvh * vhat, axis=-1, keepdims=True))
        gp = _gelu_grad(pv)
        dp_ref[:, :A_WIDTH] = (du_scr[...] * gp[:, :A_WIDTH]).astype(BF16)
        dp_ref[:, A_WIDTH:] = (dv * gp[:, A_WIDTH:]).astype(BF16)

    return pl.pallas_call(
        body, grid=(t // A_TILE,),
        in_specs=[pl.BlockSpec((A_TILE, 2 * A_WIDTH), lambda i: (i, 0)), pl.BlockSpec((A_TILE, A_WIDTH), lambda i: (i, 0)),
                  _const_spec((1, A_WIDTH)), _const_spec((1, A_WIDTH)), _const_spec((A_GROUPS, CHUNK, CHUNK)),
                  _const_spec((CHUNK, A_GROUPS))],
        out_specs=[pl.BlockSpec((A_TILE, 2 * A_WIDTH), lambda i: (i, 0)), _const_spec((A_GROUPS, CHUNK, CHUNK)),
                   _const_spec((CHUNK, A_GROUPS)), _const_spec((1, A_WIDTH)), _const_spec((1, A_WIDTH))],
        out_shape=[SDS((t, 2 * A_WIDTH), BF16), SDS((A_GROUPS, CHUNK, CHUNK), F32), SDS((CHUNK, A_GROUPS), F32),
                   SDS((1, A_WIDTH), F32), SDS((1, A_WIDTH), F32)],
        scratch_shapes=[pltpu.VMEM((A_TILE, A_WIDTH), F32), pltpu.VMEM((A_TILE, A_WIDTH), F32)],
        name="mixer_a_bwd", compiler_params=_params())(proj, dcat, lg, lb, w, bias_t)


def _rope_tables(t):
    half = ROPE_DIM // 2
    inv_freq = ROPE_THETA ** (-jnp.arange(half, dtype=F32) * 2.0 / ROPE_DIM)
    ang = jnp.arange(t, dtype=F32)[:, None] * inv_freq[None, :]
    cos, sin = jnp.cos(ang), jnp.sin(ang)
    one = jnp.ones((t, HEAD_DIM - ROPE_DIM), F32)
    zero = jnp.zeros((t, HEAD_DIM - ROPE_DIM), F32)
    zh = jnp.zeros((t, half), F32)
    c = jnp.concatenate([cos, cos, one], axis=1)
    s1 = jnp.concatenate([-sin, zh, zero], axis=1)
    s2 = jnp.concatenate([zh, sin, zero], axis=1)
    return tuple(jnp.tile(a, (1, LANES // HEAD_DIM)) for a in (c, s1, s2))


QK_TILE = 512
QK_ROWS = 64
QK_COLS = 2 * N_DIL * B_WIDTH


CHUNKS = B_WIDTH // LANES


def _regroup_out(scr, first, out_ref, rate, tile):
    rows = tile // rate
    for rho in range(rate):
        for c in range(CHUNKS):
            out_ref[rho, :, c * LANES:(c + 1) * LANES] = scr[first + c, pl.ds(rho, rows, stride=rate), :].astype(out_ref.dtype)


def _regroup_in(x_ref, scr, rate, tile):
    rows = tile // rate
    for rho in range(rate):
        for c in range(CHUNKS):
            scr[c, pl.ds(rho, rows, stride=rate), :] = x_ref[rho, :, c * LANES:(c + 1) * LANES].astype(F32)


def _regrouped_spec(rate, tile):
    return pl.BlockSpec((rate, tile // rate, B_WIDTH), lambda i, *_: (0, i, 0))


def _qk_fwd(proj, gains, tabs, ride=None):
    t = proj.shape[0]
    col0 = 2 * A_WIDTH // 1024
    r1, r2 = DIL_RATES[1], DIL_RATES[2]

    def body(p_ref, g_ref, c_ref, s1_ref, s2_ref, o_ref, q1_ref, q2_ref, k1_ref, k2_ref, scr):
        seg = _segment_mean_matrix(HEAD_DIM)
        for r0 in range(0, QK_TILE, QK_ROWS):
            rows = slice(r0, r0 + QK_ROWS)
            c, s1, s2 = c_ref[rows, :], s1_ref[rows, :], s2_ref[rows, :]
            for ci in range(1024 // LANES):
                ls = slice(ci * LANES, (ci + 1) * LANES)
                xv = p_ref[rows, ls].astype(F32)
                r = lax.rsqrt(_segment_dot(xv * xv, seg) + EPS)
                y = xv * r * g_ref[:, ls]
                val = y * c + pltpu.roll(y, LANES - 8, axis=1) * s1 + pltpu.roll(y, 8, axis=1) * s2
                o_ref[rows, ls] = val.astype(BF16)
                scr[ci, rows, :] = val

        j = pl.program_id(1)

        @pl.when(j == 0)
        def _():
            _regroup_out(scr, CHUNKS, q1_ref, r1, QK_TILE)

        @pl.when(j == 1)
        def _():
            _regroup_out(scr, 0, q2_ref, r2, QK_TILE)

        @pl.when(j == 2)
        def _():
            _regroup_out(scr, 0, k1_ref, r1, QK_TILE)
            _regroup_out(scr, CHUNKS, k2_ref, r2, QK_TILE)

    tab = pl.BlockSpec((QK_TILE, LANES), lambda i, j: (i, 0))
    g1, g2 = SDS((r1, t // r1, B_WIDTH), BF16), SDS((r2, t // r2, B_WIDTH), BF16)
    s1_, s2_ = _regrouped_spec(r1, QK_TILE), _regrouped_spec(r2, QK_TILE)
    return _call(
        body, grid=(t // QK_TILE, QK_COLS // 1024),
        in_specs=[pl.BlockSpec((QK_TILE, 1024), lambda i, j: (i, col0 + j)), pl.BlockSpec((1, 1024), lambda i, j: (0, j)),
                  tab, tab, tab],
        out_specs=[pl.BlockSpec((QK_TILE, 1024), lambda i, j: (i, j)), s1_, s2_, s1_, s2_],
        out_shape=[SDS((t, QK_COLS), BF16), g1, g2, g1, g2],
        scratch_shapes=[pltpu.VMEM((2 * CHUNKS, QK_TILE, LANES), F32)],
        operands=[proj, gains, *tabs], name="qk_norm_rope_fwd", ride=ride)


PERM_TILE = 512


def _permute(name, items, rate):
    t = items[0][0].shape[0]
    n = len(items)

    def body(*refs):
        scr = refs[-1]
        for x_ref, o_ref in zip(refs[:n], refs[n:2 * n]):
            for ci in range(CHUNKS):
                scr[ci] = x_ref[:, ci * LANES:(ci + 1) * LANES].astype(F32)
            _regroup_out(scr, 0, o_ref, rate, PERM_TILE)

    return pl.pallas_call(
        body, grid=(t // PERM_TILE,),
        in_specs=[pl.BlockSpec((PERM_TILE, B_WIDTH), functools.partial(lambda cb, i: (i, cb), cb)) for _, cb in items],
        out_specs=[_regrouped_spec(rate, PERM_TILE) for _ in items],
        out_shape=[SDS((rate, t // rate, B_WIDTH), a.dtype) for a, _ in items],
        scratch_shapes=[pltpu.VMEM((CHUNKS, PERM_TILE, LANES), F32)],
        name=name, compiler_params=_params())(*[a for a, _ in items])


def _head_lane_mask(h):
    lane = lax.broadcasted_iota(jnp.int32, (1, LANES), 1)
    return (lane < HEAD_DIM) if h == 0 else (lane >= HEAD_DIM)


def _attn_fwd(name, q, k, v, ride=None):
    rate, length = q[0].shape[0], q[0].shape[1]
    nb = length // ATT_BLOCK
    scale = HEAD_DIM ** -0.5

    def body(q_ref, kc_ref, kp_ref, vc_ref, vp_ref, o_ref, l_ref):
        n = pl.program_id(1)
        qi = lax.broadcasted_iota(jnp.int32, (ATT_BLOCK, 2 * ATT_BLOCK), 0)
        cj = lax.broadcasted_iota(jnp.int32, (ATT_BLOCK, 2 * ATT_BLOCK), 1)
        has_prev = jnp.where(n > 0, 0, 2 * ATT_BLOCK)
        mask = ((cj < ATT_BLOCK) & (cj >= qi + has_prev)) | ((cj >= ATT_BLOCK) & (cj - ATT_BLOCK <= qi))
        heads = [(hp, h) for hp in range(CHUNKS) for h in range(2)]
        q2, k2, v2 = {}, {}, {}
        for hp in range(CHUNKS):
            ls = slice(hp * LANES, (hp + 1) * LANES)
            q2[hp] = q_ref[:, ls]
            k2[hp] = jnp.concatenate([kp_ref[:, ls], kc_ref[:, ls]], axis=0)
            v2[hp] = jnp.concatenate([vp_ref[:, ls], vc_ref[:, ls]], axis=0)
        scores = {}
        for hp, h in heads:
            scores[hp, h] = _dot(jnp.where(_head_lane_mask(h), q2[hp], jnp.zeros_like(q2[hp])), k2[hp], NT) * scale
        probs, lses = {}, {}
        for hp, h in heads:
            s = jnp.where(mask, scores[hp, h], NEG_INF)
            m = jnp.max(s, axis=1, keepdims=True)
            p = jnp.exp(s - m)
            den = jnp.sum(p, axis=1, keepdims=True)
            lses[hp, h] = m + jnp.log(den)
            probs[hp, h] = (p / den).astype(BF16)
        for hp in range(CHUNKS):
            ls = slice(hp * LANES, (hp + 1) * LANES)
            o_acc = None
            for h in range(2):
                o = _dot(probs[hp, h], jnp.where(_head_lane_mask(h), v2[hp], jnp.zeros_like(v2[hp])), NN)
                o_acc = o if o_acc is None else o_acc + o
            o_ref[:, ls] = o_acc
            zeros = jnp.zeros((ATT_BLOCK, LANES), F32)
            l_ref[:, ls] = jnp.where(_head_lane_mask(1), lses[hp, 1] + zeros, lses[hp, 0] + zeros)

    def cur(cb):
        return pl.BlockSpec((None, ATT_BLOCK, B_WIDTH), lambda r, n: (r, n, cb))

    def prev(cb):
        return pl.BlockSpec((None, ATT_BLOCK, B_WIDTH), lambda r, n: (r, jnp.maximum(n - 1, 0), cb))

    out = pl.BlockSpec((None, ATT_BLOCK, B_WIDTH), lambda r, n: (r, n, 0))
    return _call(
        body, grid=(rate, nb),
        in_specs=[cur(q[1]), cur(k[1]), prev(k[1]), cur(v[1]), prev(v[1])],
        out_specs=[out, out], out_shape=[SDS((rate, length, B_WIDTH), F32)] * 2,
        operands=[q[0], k[0], k[0], v[0], v[0]], name=name, ride=ride)


def _attn_merge(a_out, o_list, l_list):
    t = a_out.shape[0]
    tm = PERM_TILE
    r1, r2 = DIL_RATES[1], DIL_RATES[2]

    def body(a_ref, o0, o1, o2, l0, l1, l2, cat_ref, lt_ref, lt1_ref, lt2_ref, so1, so2, sl1, sl2, slt):
        _regroup_in(o1, so1, r1, tm)
        _regroup_in(l1, sl1, r1, tm)
        _regroup_in(o2, so2, r2, tm)
        _regroup_in(l2, sl2, r2, tm)
        cat_ref[:, :A_WIDTH] = a_ref[...]
        for c in range(CHUNKS):
            ls = slice(c * LANES, (c + 1) * LANES)
            lg = [l0[:, ls], sl1[c], sl2[c]]
            m = jnp.maximum(jnp.maximum(lg[0], lg[1]), lg[2])
            es = [jnp.exp(l - m) for l in lg]
            den = es[0] + es[1] + es[2]
            b = (es[0] * o0[:, ls] + es[1] * so1[c] + es[2] * so2[c]) / den
            cat_ref[:, A_WIDTH + c * LANES:A_WIDTH + (c + 1) * LANES] = b.astype(BF16)
            lt = m + jnp.log(den)
            lt_ref[:, ls] = lt
            slt[c] = lt
        _regroup_out(slt, 0, lt1_ref, r1, tm)
        _regroup_out(slt, 0, lt2_ref, r2, tm)

    blk = _row_spec(tm, B_WIDTH)
    g1, g2 = _regrouped_spec(r1, tm), _regrouped_spec(r2, tm)
    return pl.pallas_call(
        body, grid=(t // tm,), in_specs=[blk, blk, g1, g2, blk, g1, g2],
        out_specs=[_row_spec(tm, A_WIDTH + B_WIDTH), blk, g1, g2],
        out_shape=[SDS((t, A_WIDTH + B_WIDTH), BF16), SDS((t, B_WIDTH), F32), SDS((r1, t // r1, B_WIDTH), F32),
                   SDS((r2, t // r2, B_WIDTH), F32)],
        scratch_shapes=[pltpu.VMEM((CHUNKS, tm, LANES), F32)] * 5,
        name="attn_merge", compiler_params=_params())(a_out, *o_list, *l_list)


def _attn_bwd_prep(dcat, cat):
    t = dcat.shape[0]
    tm = PERM_TILE
    r1, r2 = DIL_RATES[1], DIL_RATES[2]

    def body(d_ref, b_ref, db_ref, dd_ref, db1_ref, dd1_ref, db2_ref, dd2_ref, sdb, sdd):
        seg = _segment_mean_matrix(HEAD_DIM, scale=1.0)
        for c in range(CHUNKS):
            ls = slice(c * LANES, (c + 1) * LANES)
            d = d_ref[:, ls]
            dsum = _segment_dot(d * b_ref[:, ls].astype(F32), seg)
            db_ref[:, ls] = d.astype(BF16)
            dd_ref[:, ls] = dsum
            sdb[c] = d
            sdd[c] = dsum
        _regroup_out(sdb, 0, db1_ref, r1, tm)
        _regroup_out(sdd, 0, dd1_ref, r1, tm)
        _regroup_out(sdb, 0, db2_ref, r2, tm)
        _regroup_out(sdd, 0, dd2_ref, r2, tm)

    right = pl.BlockSpec((tm, B_WIDTH), lambda i: (i, 1))
    blk = _row_spec(tm, B_WIDTH)
    g1, g2 = _regrouped_spec(r1, tm), _regrouped_spec(r2, tm)
    return pl.pallas_call(
        body, grid=(t // tm,), in_specs=[right, right], out_specs=[blk, blk, g1, g1, g2, g2],
        out_shape=[SDS((t, B_WIDTH), BF16), SDS((t, B_WIDTH), F32), SDS((r1, t // r1, B_WIDTH), BF16),
                   SDS((r1, t // r1, B_WIDTH), F32), SDS((r2, t // r2, B_WIDTH), BF16), SDS((r2, t // r2, B_WIDTH), F32)],
        scratch_shapes=[pltpu.VMEM((CHUNKS, tm, LANES), F32)] * 2,
        name="attn_bwd_prep", compiler_params=_params())(dcat, cat)


def _attn_bwd(name, q, k, v, db, lse, dd, ride=None):
    rate, length = db.shape[0], db.shape[1]
    nb = length // ATT_BLOCK
    scale = HEAD_DIM ** -0.5

    def body(qa_ref, qb_ref, k_ref, v_ref, dba_ref, dbb_ref, la_ref, lb_ref, da_ref, dbd_ref, dq_ref, dk_ref, dv_ref, carry):
        m = pl.program_id(1)

        @pl.when(m == 0)
        def _():
            carry[...] = jnp.zeros_like(carry)

        row = lax.broadcasted_iota(jnp.int32, (2 * ATT_BLOCK, ATT_BLOCK), 0)
        kj = lax.broadcasted_iota(jnp.int32, (2 * ATT_BLOCK, ATT_BLOCK), 1)
        no_next = jnp.where(m + 1 < nb, 0, 2 * ATT_BLOCK)
        mask = ((row < ATT_BLOCK) & (kj <= row)) | ((row >= ATT_BLOCK) & (kj >= row - ATT_BLOCK + no_next))
        heads = [(hp, h) for hp in range(CHUNKS) for h in range(2)]
        q2, db2, lse2, dd2, k2, v2 = {}, {}, {}, {}, {}, {}
        for hp in range(CHUNKS):
            ls = slice(hp * LANES, (hp + 1) * LANES)
            k2[hp], v2[hp] = k_ref[:, ls], v_ref[:, ls]
            q2[hp] = jnp.concatenate([qa_ref[:, ls], qb_ref[:, ls]], axis=0)
            db2[hp] = jnp.concatenate([dba_ref[:, ls], dbb_ref[:, ls]], axis=0)
            lse2[hp] = jnp.concatenate([la_ref[:, ls], lb_ref[:, ls]], axis=0)
            dd2[hp] = jnp.concatenate([da_ref[:, ls], dbd_ref[:, ls]], axis=0)
        km, scores, dps = {}, {}, {}
        for hp, h in heads:
            hm = _head_lane_mask(h)
            km[hp, h] = jnp.where(hm, k2[hp], jnp.zeros_like(k2[hp]))
            scores[hp, h] = _dot(q2[hp], km[hp, h], NT) * scale
            dps[hp, h] = _dot(db2[hp], jnp.where(hm, v2[hp], jnp.zeros_like(v2[hp])), NT)
        probs, dss = {}, {}
        for hp, h in heads:
            hm = _head_lane_mask(h)
            lse_col = jnp.max(jnp.where(hm, lse2[hp], NEG_INF), axis=1, keepdims=True)
            dd_col = jnp.max(jnp.where(hm, dd2[hp], NEG_INF), axis=1, keepdims=True)
            p = jnp.where(mask, jnp.exp(scores[hp, h] - lse_col), 0.0)
            probs[hp, h] = p.astype(BF16)
            dss[hp, h] = (p * (dps[hp, h] - dd_col) * scale).astype(BF16)
        for hp in range(CHUNKS):
            ls = slice(hp * LANES, (hp + 1) * LANES)
            dq_acc, dk_acc, dv_acc = None, None, None
            for h in range(2):
                hm = _head_lane_mask(h)
                dvc = _dot(probs[hp, h], jnp.where(hm, db2[hp], jnp.zeros_like(db2[hp])), TN)
                dqc = _dot(dss[hp, h], km[hp, h], NN)
                dkc = _dot(dss[hp, h], jnp.where(hm, q2[hp], jnp.zeros_like(q2[hp])), TN)
                dq_acc = dqc if dq_acc is None else dq_acc + dqc
                dk_acc = dkc if dk_acc is None else dk_acc + dkc
                dv_acc = dvc if dv_acc is None else dv_acc + dvc
            dq_ref[:, ls] = (dq_acc[:ATT_BLOCK] + carry[:, ls]).astype(BF16)
            carry[:, ls] = dq_acc[ATT_BLOCK:]
            dk_ref[:, ls] = dk_acc.astype(BF16)
            dv_ref[:, ls] = dv_acc.astype(BF16)

    def cur(cb):
        return pl.BlockSpec((None, ATT_BLOCK, B_WIDTH), lambda r, n: (r, n, cb))

    def nxt(cb):
        return pl.BlockSpec((None, ATT_BLOCK, B_WIDTH), lambda r, n: (r, jnp.minimum(n + 1, nb - 1), cb))

    out = cur(0)
    return _call(
        body, grid=(rate, nb),
        in_specs=[cur(q[1]), nxt(q[1]), cur(k[1]), cur(v[1]), cur(0), nxt(0), cur(0), nxt(0), cur(0), nxt(0)],
        out_specs=[out, out, out], out_shape=[SDS((rate, length, B_WIDTH), BF16)] * 3,
        scratch_shapes=[pltpu.VMEM((ATT_BLOCK, B_WIDTH), F32)],
        operands=[q[0], q[0], k[0], v[0], db, db, lse, lse, dd, dd], name=name, ride=ride)


AB_IN = 2 * A_WIDTH + 3 * N_DIL * B_WIDTH
ASM_TILE = 256


def _dproj_assemble(proj, d_a, dq, dk, dv, gains, tabs, ride=None):
    t = proj.shape[0]
    n_in = 3 * N_DIL

    def body(p_ref, da_ref, *rest):
        grads = rest[:n_in]
        g_ref, c_ref, s1_ref, s2_ref, o_ref, dg_ref = rest[n_in:n_in + 6]
        scratch = rest[n_in + 6:]

        @pl.when(pl.program_id(0) == 0)
        def _():
            dg_ref[...] = jnp.zeros_like(dg_ref)

        chunk = {}
        k_scr = 0
        for j in range(n_in):
            g = j % N_DIL
            if DIL_RATES[g] == 1:
                for ci in range(CHUNKS):
                    chunk[j, ci] = functools.partial(lambda r, ci: r[:, ci * LANES:(ci + 1) * LANES].astype(F32), grads[j], ci)
            else:
                scr = scratch[k_scr]
                k_scr += 1
                _regroup_in(grads[j], scr, DIL_RATES[g], ASM_TILE)
                for ci in range(CHUNKS):
                    chunk[j, ci] = functools.partial(lambda s, ci: s[ci], scr, ci)

        seg = _segment_mean_matrix(HEAD_DIM)
        c, s1, s2 = c_ref[...], s1_ref[...], s2_ref[...]
        o_ref[:, :2 * A_WIDTH] = da_ref[...]
        for jg in range(2 * N_DIL):
            for ci in range(CHUNKS):
                col = jg * B_WIDTH + ci * LANES
                src = slice(2 * A_WIDTH + col, 2 * A_WIDTH + col + LANES)
                xv = p_ref[:, src].astype(F32)
                r = lax.rsqrt(_segment_dot(xv * xv, seg) + EPS)
                xh = xv * r
                gain = g_ref[:, col:col + LANES]
                do = chunk[jg, ci]()
                dy = do * c + pltpu.roll(do * s1, 8, axis=1) + pltpu.roll(do * s2, LANES - 8, axis=1)
                dg_ref[:, col:col + LANES] += jnp.sum(dy * xh, axis=0, keepdims=True)
                dxh = dy * gain
                o_ref[:, src] = (r * (dxh - xh * _segment_dot(dxh * xh, seg))).astype(BF16)
        v0 = 2 * A_WIDTH + QK_COLS
        for g in range(N_DIL):
            for ci in range(CHUNKS):
                col = v0 + g * B_WIDTH + ci * LANES
                o_ref[:, col:col + LANES] = chunk[2 * N_DIL + g, ci]().astype(BF16)

    specs = [_row_spec(ASM_TILE, B_WIDTH) if r == 1 else _regrouped_spec(r, ASM_TILE) for r in DIL_RATES] * 3
    n_scr = 3 * sum(1 for r in DIL_RATES if r > 1)
    tab = _row_spec(ASM_TILE, LANES)
    return _call(
        body, grid=(t // ASM_TILE,),
        in_specs=[_row_spec(ASM_TILE, AB_IN), _row_spec(ASM_TILE, 2 * A_WIDTH)] + specs
        + [_const_spec((1, QK_COLS)), tab, tab, tab],
        out_specs=[_row_spec(ASM_TILE, AB_IN), _const_spec((1, QK_COLS))],
        out_shape=[SDS((t, AB_IN), BF16), SDS((1, QK_COLS), F32)],
        scratch_shapes=[pltpu.VMEM((CHUNKS, ASM_TILE, LANES), F32)] * n_scr,
        operands=[proj, d_a, *dq, *dk, *dv, gains, *tabs], name="dproj_assemble", ride=ride)


def _fold_heads(dg_lane):
    n = dg_lane.shape[1]

    def body(x_ref, o_ref):
        r = lax.broadcasted_iota(jnp.int32, (B_WIDTH, B_WIDTH), 0) % HEAD_DIM
        c = lax.broadcasted_iota(jnp.int32, (B_WIDTH, B_WIDTH), 1) % HEAD_DIM
        fold = jnp.where(r == c, 1.0, 0.0).astype(F32)
        for jg in range(n // B_WIDTH):
            ls = slice(jg * B_WIDTH, (jg + 1) * B_WIDTH)
            o_ref[:, ls] = _dot_hi(jnp.broadcast_to(x_ref[:, ls], (8, B_WIDTH)), fold)

    return pl.pallas_call(body, out_shape=SDS((8, n), F32), name="fold_heads", compiler_params=_params())(dg_lane)


CD_TILE = 256
TAP_ROWS = 64
CD_IN = 2 * C_WIDTH + 3 * 512


def _shifted_copies(src, dst, rows):
    dst[0, :rows] = src[...]
    for b in range(1, 8):
        dst[b, :rows - 8] = src[pl.ds(b, rows - 8), :]


def _rows_from(shifted, start, n, lanes=slice(None)):
    b = start % 8
    return shifted[b, pl.ds(start - b, n), lanes]


def _mixer_cd_fwd(proj, cw, cb, lg, lb, dw):
    t = proj.shape[0]
    per = CD_TILE // HALO

    def body(h_ref, m_ref, cw_ref, cb_ref, lg_ref, lb_ref, dw_ref, o_ref, c1_ref, c_scr, e_scr, c_sh):
        not_first = (pl.program_id(0) > 0).astype(F32)
        lanes = [slice(c * LANES, (c + 1) * LANES) for c in range(C_WIDTH // LANES)]

        def col(ref, part, ls):
            return ref[:, part * C_WIDTH + ls.start:part * C_WIDTH + ls.stop].astype(F32)

        for ls in lanes:
            c_scr[:HALO, ls] = col(h_ref, 0, ls) * _sigmoid(col(h_ref, 1, ls)) * not_first
            c_scr[HALO:, ls] = col(m_ref, 0, ls) * _sigmoid(col(m_ref, 1, ls))
            e_scr[:HALO, ls] = col(h_ref, 3, ls) * col(h_ref, 4, ls) * not_first
            e_scr[HALO:, ls] = col(m_ref, 3, ls) * col(m_ref, 4, ls)
        _shifted_copies(c_scr, c_sh, HALO + CD_TILE)
        for ls in lanes:
            for r0 in range(0, CD_TILE, TAP_ROWS):
                acc = jnp.zeros((TAP_ROWS, LANES), F32)
                for k in range(C_KERNEL):
                    acc = acc + cw_ref[k:k + 1, ls] * _rows_from(c_sh, r0 + HALO - (C_KERNEL - 1) + k, TAP_ROWS, ls)
                c1_ref[r0:r0 + TAP_ROWS, ls] = acc + cb_ref[:, ls]
        mean = sum(jnp.sum(c1_ref[:, ls], axis=-1, keepdims=True) for ls in lanes) * (1.0 / C_WIDTH)
        var = sum(jnp.sum((c1_ref[:, ls] - mean) ** 2, axis=-1, keepdims=True) for ls in lanes) * (1.0 / C_WIDTH)
        rs = lax.rsqrt(var + EPS)
        for ls in lanes:
            c2 = (c1_ref[:, ls] - mean) * rs * lg_ref[:, ls] + lb_ref[:, ls]
            o_ref[:, ls] = (c2 * _sigmoid(c2)).astype(BF16)
            d1 = jnp.zeros((CD_TILE, LANES), F32)
            for k in range(D_KERNEL):
                d1 = d1 + dw_ref[k:k + 1, ls] * e_scr[pl.ds(HALO - (D_KERNEL - 1) + k, CD_TILE), ls]
            o_ref[:, C_WIDTH + ls.start:C_WIDTH + ls.stop] = (col(m_ref, 2, ls) * d1).astype(BF16)

    return pl.pallas_call(
        body, grid=(t // CD_TILE,),
        in_specs=[pl.BlockSpec((HALO, CD_IN), lambda i: (jnp.maximum(i * per - 1, 0), 0)), _row_spec(CD_TILE, CD_IN),
                  _const_spec((32, C_WIDTH)), _const_spec((1, C_WIDTH)), _const_spec((1, C_WIDTH)), _const_spec((1, C_WIDTH)),
                  _const_spec((8, C_WIDTH))],
        out_specs=[_row_spec(CD_TILE, 2 * C_WIDTH), _row_spec(CD_TILE, C_WIDTH)],
        out_shape=[SDS((t, 2 * C_WIDTH), BF16), SDS((t, C_WIDTH), F32)],
        scratch_shapes=[pltpu.VMEM((HALO + CD_TILE, C_WIDTH), F32)] * 2 + [pltpu.VMEM((8, HALO + CD_TILE, C_WIDTH), F32)],
        name="mixer_cd_fwd", compiler_params=_params())(proj, proj, cw, cb, lg, lb, dw)


def _mixer_cd_bwd(proj, dcat, c1, cw, lg, lb, dw, ride=None):
    t = proj.shape[0]
    per = CD_TILE // HALO
    nt = t // CD_TILE
    ext = CD_TILE + HALO

    def body(hp_ref, m_ref, hn_ref, dm_ref, dn_ref, c1m_ref, c1n_ref, cw_ref, lg_ref, lb_ref, dw_ref,
             dp_ref, dcw_ref, dcb_ref, dlg_ref, dlb_ref, ddw_ref, c_scr, e_scr, dc1_scr, dd1_scr, c_sh, dc1_sh, dcw_acc,
             dvh_scr, vhat_scr):
        i = pl.program_id(0)

        @pl.when(i == 0)
        def _():
            for r in (dcw_acc, dcb_ref, dlg_ref, dlb_ref, ddw_ref):
                r[...] = jnp.zeros_like(r)

        not_first = (i > 0).astype(F32)
        not_last = (i < nt - 1).astype(F32)
        main = slice(HALO, HALO + CD_TILE)
        lanes = [slice(c * LANES, (c + 1) * LANES) for c in range(C_WIDTH // LANES)]

        def col(ref, part, ls):
            return ref[:, part * C_WIDTH + ls.start:part * C_WIDTH + ls.stop].astype(F32)

        for ls in lanes:
            c_scr[:HALO, ls] = col(hp_ref, 0, ls) * _sigmoid(col(hp_ref, 1, ls)) * not_first
            c_scr[main, ls] = col(m_ref, 0, ls) * _sigmoid(col(m_ref, 1, ls))
            c_scr[HALO + CD_TILE:, ls] = col(hn_ref, 0, ls) * _sigmoid(col(hn_ref, 1, ls)) * not_last
            e_scr[:HALO, ls] = col(hp_ref, 3, ls) * col(hp_ref, 4, ls) * not_first
            e_scr[main, ls] = col(m_ref, 3, ls) * col(m_ref, 4, ls)
            e_scr[HALO + CD_TILE:, ls] = col(hn_ref, 3, ls) * col(hn_ref, 4, ls) * not_last
        _shifted_copies(c_scr, c_sh, 2 * HALO + CD_TILE)

        def c1_of(ls):
            return jnp.concatenate([c1m_ref[:, ls], c1n_ref[:, ls]], axis=0)

        mean = sum(jnp.sum(c1_of(ls), axis=-1, keepdims=True) for ls in lanes) * (1.0 / C_WIDTH)
        var = sum(jnp.sum((c1_of(ls) - mean) ** 2, axis=-1, keepdims=True) for ls in lanes) * (1.0 / C_WIDTH)
        rs = lax.rsqrt(var + EPS)
        sum_dvh, sum_dvh_vhat = 0.0, 0.0
        for ls in lanes:
            vhat = (c1_of(ls) - mean) * rs
            c2 = vhat * lg_ref[:, ls] + lb_ref[:, ls]
            sig = _sigmoid(c2)
            dc = jnp.concatenate([dm_ref[:, ls], dn_ref[:, ls] * not_last], axis=0)
            dc2 = dc * (sig * (1.0 + c2 * (1.0 - sig)))
            dvh = dc2 * lg_ref[:, ls]
            sum_dvh = sum_dvh + jnp.sum(dvh, axis=-1, keepdims=True)
            sum_dvh_vhat = sum_dvh_vhat + jnp.sum(dvh * vhat, axis=-1, keepdims=True)
            dvh_scr[:, ls] = dvh
            vhat_scr[:, ls] = vhat
            dlg_ref[:, ls] += jnp.sum((dc2 * vhat)[:CD_TILE], axis=0, keepdims=True)
            dlb_ref[:, ls] += jnp.sum(dc2[:CD_TILE], axis=0, keepdims=True)
        for ls in lanes:
            dc1 = rs * (dvh_scr[:, ls] - sum_dvh * (1.0 / C_WIDTH) - vhat_scr[:, ls] * (sum_dvh_vhat * (1.0 / C_WIDTH)))
            dc1_scr[:, ls] = dc1
            dcb_ref[:, ls] += jnp.sum(dc1[:CD_TILE], axis=0, keepdims=True)
        _shifted_copies(dc1_scr, dc1_sh, ext)
        for ls in lanes:
            for r0 in range(0, CD_TILE, TAP_ROWS):
                rows = slice(r0, r0 + TAP_ROWS)
                dc1_m = dc1_scr[rows, ls]
                dc0 = jnp.zeros((TAP_ROWS, LANES), F32)
                for k in range(C_KERNEL):
                    dc0 = dc0 + cw_ref[k:k + 1, ls] * _rows_from(dc1_sh, r0 + C_KERNEL - 1 - k, TAP_ROWS, ls)
                    prod = dc1_m * _rows_from(c_sh, r0 + HALO - (C_KERNEL - 1) + k, TAP_ROWS, ls)
                    dcw_acc[k, :, ls] += prod.reshape(TAP_ROWS // 8, 8, LANES).sum(axis=0)
                g_m = m_ref[rows, C_WIDTH + ls.start:C_WIDTH + ls.stop].astype(F32)
                a_m = m_ref[rows, ls].astype(F32)
                sig_m = _sigmoid(g_m)
                dp_ref[rows, ls] = (dc0 * sig_m).astype(BF16)
                dp_ref[rows, C_WIDTH + ls.start:C_WIDTH + ls.stop] = (dc0 * a_m * sig_m * (1.0 - sig_m)).astype(BF16)

        @pl.when(i == nt - 1)
        def _():
            dcw_ref[...] = jnp.sum(dcw_acc[...], axis=1)

        for ls in lanes:
            wide = slice(C_WIDTH + ls.start, C_WIDTH + ls.stop)
            d1 = jnp.zeros((CD_TILE, LANES), F32)
            for k in range(D_KERNEL):
                d1 = d1 + dw_ref[k:k + 1, ls] * e_scr[pl.ds(HALO - (D_KERNEL - 1) + k, CD_TILE), ls]
            dd_m = dm_ref[:, wide]
            dd1 = jnp.concatenate([dd_m * col(m_ref, 2, ls), dn_ref[:, wide] * col(hn_ref, 2, ls) * not_last], axis=0)
            dd1_scr[:, ls] = dd1
            dp_ref[:, 2 * C_WIDTH + ls.start:2 * C_WIDTH + ls.stop] = (dd_m * d1).astype(BF16)
            de = jnp.zeros((CD_TILE, LANES), F32)
            for k in range(D_KERNEL):
                de = de + dw_ref[k:k + 1, ls] * dd1_scr[pl.ds(D_KERNEL - 1 - k, CD_TILE), ls]
                ddw_ref[k:k + 1, ls] += jnp.sum(dd1[:CD_TILE] * e_scr[pl.ds(HALO - (D_KERNEL - 1) + k, CD_TILE), ls], axis=0, keepdims=True)
            dp_ref[:, 3 * C_WIDTH + ls.start:3 * C_WIDTH + ls.stop] = (de * col(m_ref, 4, ls)).astype(BF16)
            dp_ref[:, 4 * C_WIDTH + ls.start:4 * C_WIDTH + ls.stop] = (de * col(m_ref, 3, ls)).astype(BF16)

    halo_prev = lambda i: (jnp.maximum(i * per - 1, 0), 0)
    halo_next = lambda i: (jnp.minimum((i + 1) * per, t // HALO - 1), 0)
    vec = _const_spec((1, C_WIDTH))
    return _call(
        body, grid=(nt,),
        in_specs=[pl.BlockSpec((HALO, CD_IN), halo_prev), _row_spec(CD_TILE, CD_IN), pl.BlockSpec((HALO, CD_IN), halo_next),
                  _row_spec(CD_TILE, 2 * C_WIDTH), pl.BlockSpec((HALO, 2 * C_WIDTH), halo_next),
                  _row_spec(CD_TILE, C_WIDTH), pl.BlockSpec((HALO, C_WIDTH), halo_next),
                  _const_spec((32, C_WIDTH)), vec, vec, _const_spec((8, C_WIDTH))],
        out_specs=[_row_spec(CD_TILE, CD_IN), _const_spec((32, C_WIDTH)), vec, vec, vec, _const_spec((8, C_WIDTH))],
        out_shape=[SDS((t, CD_IN), BF16), SDS((32, C_WIDTH), F32), SDS((1, C_WIDTH), F32), SDS((1, C_WIDTH), F32),
                   SDS((1, C_WIDTH), F32), SDS((8, C_WIDTH), F32)],
        scratch_shapes=[pltpu.VMEM((2 * HALO + CD_TILE, C_WIDTH), F32)] * 2 + [pltpu.VMEM((ext, C_WIDTH), F32)] * 2
        + [pltpu.VMEM((8, 2 * HALO + CD_TILE, C_WIDTH), F32), pltpu.VMEM((8, ext, C_WIDTH), F32),
           pltpu.VMEM((32, 8, C_WIDTH), F32)] + [pltpu.VMEM((ext, C_WIDTH), F32)] * 2,
        operands=[proj, proj, proj, dcat, dcat, c1, c1, cw, lg, lb, dw], name="mixer_cd_bwd", ride=ride)


def _wgrad(name, pairs, out_rc, t, ride):
    tk = TILES["wgrad"]
    assert tk == t, "the whole contraction has to fit one grid step"
    r, c = out_rc
    n = len(pairs)

    def body(*refs):
        ab, out_refs = refs[:2 * n], refs[2 * n:]
        for j in range(n):
            out_refs[j][...] = _dot(ab[2 * j][...], ab[2 * j + 1][...], TN).astype(BF16)

    operands, in_specs = [], []
    for lhs, lhs_spec, rhs, rhs_spec in pairs:
        operands += [lhs, rhs]
        in_specs += [lhs_spec, rhs_spec]
    res = _call(body, grid=(N_CHIPS, t // tk), in_specs=in_specs,
                out_specs=[pl.BlockSpec((None, r, c), lambda p, k: (p, 0, 0))] * n,
                out_shape=[SDS((N_CHIPS, r, c), BF16)] * n, operands=operands, name=name, ride=ride)
    outs, ride_res = (res, None) if ride is None else res
    outs = [o.reshape(N_CHIPS, 2, r // 2, c) for o in outs]
    return outs if ride is None else (outs, ride_res)


def _wgrad_col_sharded(name, h, dz_list, three_d, ride=None):
    t, d = h.shape
    tk = TILES["wgrad"]
    n4 = dz_list[0].shape[-1] if three_d else dz_list[0].shape[-1] // N_CHIPS
    hs = pl.BlockSpec((tk, d), lambda p, k: (k, 0))
    zs = pl.BlockSpec((None, tk, n4), lambda p, k: (p, k, 0)) if three_d else pl.BlockSpec((tk, n4), lambda p, k: (k, p))
    return _wgrad(name, [(h, hs, dz, zs) for dz in dz_list], (d, n4), t, ride)


def _wgrad_row_sharded(name, a, g, three_d, ride=None):
    many = isinstance(a, (list, tuple))
    a_list = list(a) if many else [a]
    t, d = g.shape
    tk = TILES["wgrad"]
    k4 = a_list[0].shape[-1] if three_d else a_list[0].shape[-1] // N_CHIPS
    a_spec = pl.BlockSpec((None, tk, k4), lambda p, k: (p, k, 0)) if three_d else pl.BlockSpec((tk, k4), lambda p, k: (k, p))
    gs = pl.BlockSpec((tk, d), lambda p, k: (k, 0))
    res = _wgrad(name, [(a_j, a_spec, g, gs) for a_j in a_list], (k4, d), t, ride)
    if many:
        return res
    return res[0] if ride is None else (res[0][0], res[1])


def _mesh_scalars():
    return jnp.stack([lax.axis_index("c"), 2 * lax.axis_index("x") + lax.axis_index("y")]).astype(jnp.int32)


def _stage_own(name, w, layer, dtype):
    layers, r, cols = w.shape
    h = r // 2

    def body(s_ref, x_ref, o_ref):
        o_ref[...] = x_ref[...].astype(dtype)

    return pl.pallas_call(
        body,
        grid_spec=pltpu.PrefetchScalarGridSpec(
            num_scalar_prefetch=1, grid=(2,),
            in_specs=[pl.BlockSpec((None, h, cols), lambda i, s: (2 * layer + i, 0, 0))],
            out_specs=pl.BlockSpec((None, None, h, cols), lambda i, s: (s[1], i, 0, 0))),
        out_shape=SDS((N_CHIPS, 2, h, cols), dtype), name=name,
        compiler_params=_params())(_mesh_scalars(), w.reshape(2 * layers, h, cols))


STAGE_STEPS = 4


def _stage_rest_and_norm(x, g, weights, ride=None):
    t, d = x.shape
    n = len(weights)
    views, in_specs, out_specs, out_shapes = [], [], [], []
    for w, layer in weights:
        layers, r, cols = w.shape
        sub = r // STAGE_STEPS
        views.append(w.reshape(layers * STAGE_STEPS, sub, cols))
        in_specs.append(pl.BlockSpec((None, sub, cols), functools.partial(lambda l, i, s: (STAGE_STEPS * l + i, 0, 0), layer)))
        out_specs.append(pl.BlockSpec((None, None, sub, cols), lambda i, s: (s[1], i // 2, i % 2, 0)))
        out_shapes.append(SDS((N_CHIPS, 2, r // 2, cols), BF16))

    def body(s_ref, x_ref, g_ref, *rest):
        w_refs, h_ref, o_refs = rest[:n], rest[n], rest[n + 1:]
        h_ref[...] = _rms_rows(x_ref[...], g_ref[...]).astype(BF16)
        for w_ref, o_ref in zip(w_refs, o_refs):
            o_ref[...] = w_ref[...].astype(BF16)

    tm = t // STAGE_STEPS
    res = _call(
        body, grid=(STAGE_STEPS,), in_specs=[pl.BlockSpec((tm, d), lambda i, s: (i, 0)), pl.BlockSpec((1, d), lambda i, s: (0, 0))] + in_specs,
        out_specs=[pl.BlockSpec((tm, d), lambda i, s: (i, 0))] + out_specs, out_shape=[SDS((t, d), BF16)] + out_shapes,
        operands=[x, g] + views, name="stage_and_norm", ride=ride, prefetch=_mesh_scalars())
    outs, ride_res = (res, None) if ride is None else res
    result = (outs[0], list(outs[1:]))
    return result if ride is None else (result, ride_res)


def _remote(src, dst, send_sem, recv_sem, device):
    return pltpu.make_async_remote_copy(src, dst, send_sem, recv_sem, device_id=device, device_id_type=MESH)


def _ride_gather_send(bufs):
    n = len(bufs)

    def each(b, sems, act):
        send, recv = sems
        x, y, c, p, others = _position()
        for t in range(n):
            for j, (qx, qy) in enumerate(others):
                act(b[t].at[p, c], b[t].at[2 * qx + qy, c], send.at[t, j], recv.at[t, j], (qx, qy, c))

    def start(ins, b, new, sems):
        each(b, sems, lambda mine, landed, s, r, dev: _remote(mine, mine, s, r, dev).start())

    def finish(ins, b, new, sems):
        def act(mine, landed, s, r, dev):
            _remote(mine, mine, s, r, dev).wait_send()
            _remote(landed, landed, s, r, dev).wait_recv()
        each(b, sems, act)

    return _Ride([], bufs, [], [(n, 3), (n, 3)], start, finish, ["chips"])


def _ride_gather_pass(bufs):
    n = len(bufs)

    def each(b, sems, act):
        send, recv = sems
        x, y, c, p, others = _position()
        for t in range(n):
            for j, (qx, qy) in enumerate(others):
                act(b[t].at[2 * qx + qy, c], b[t].at[2 * qx + qy, 1 - c], send.at[t, j], recv.at[t, j], (x, y, 1 - c))

    def start(ins, b, new, sems):
        each(b, sems, lambda landed, passed, s, r, dev: _remote(landed, landed, s, r, dev).start())

    def finish(ins, b, new, sems):
        def act(landed, passed, s, r, dev):
            _remote(landed, landed, s, r, dev).wait_send()
            _remote(passed, passed, s, r, dev).wait_recv()
        each(b, sems, act)

    return _Ride([], bufs, [], [(n, 3), (n, 3)], start, finish, ["sibling"])


def _ride_gather(bufs):
    send, onward = _ride_gather_send(bufs), _ride_gather_pass(bufs)
    n_send = len(send.sem_shapes)

    def start(ins, b, new, sems):
        send.start(ins, b, new, sems[:n_send])

    def finish(ins, b, new, sems):
        send.finish(ins, b, new, sems[:n_send])
        onward.start(ins, b, new, sems[n_send:])
        onward.finish(ins, b, new, sems[n_send:])

    return _Ride([], bufs, [], send.sem_shapes + onward.sem_shapes, start, finish, ["sibling", "chips"])


def _ride_swap(tensors):
    n = len(tensors)

    def each(ins, new, sems, act):
        send, recv = sems
        x, y, c, _, _ = _position()
        for t in range(n):
            act(_remote(ins[t].at[:, 1 - c], new[t], send.at[t], recv.at[t], (x, y, 1 - c)))

    def start(ins, b, new, sems):
        each(ins, new, sems, lambda cp: cp.start())

    def finish(ins, b, new, sems):
        each(ins, new, sems, lambda cp: cp.wait())

    return _Ride(tensors, [], [SDS((s.shape[0],) + s.shape[2:], s.dtype) for s in tensors], [(n,), (n,)], start, finish,
                 ["sibling"])


def _ride_scatter(tensors, landing):
    n = len(tensors)

    def each(ins, b, sems, act):
        send, recv = sems
        x, y, c, p, others = _position()
        for t in range(n):
            for j, (qx, qy) in enumerate(others):
                q = 2 * qx + qy
                act(ins[t].at[q], b[t].at[p], b[t].at[q], send.at[t, j], recv.at[t, j], (qx, qy, c))

    def start(ins, b, new, sems):
        each(ins, b, sems, lambda src, dst, landed, s, r, dev: _remote(src, dst, s, r, dev).start())

    def finish(ins, b, new, sems):
        def act(src, dst, landed, s, r, dev):
            _remote(src, dst, s, r, dev).wait_send()
            _remote(landed, landed, s, r, dev).wait_recv()
        each(ins, b, sems, act)

    return _Ride(tensors, landing, [], [(n, 3), (n, 3)], start, finish, ["chips"])


def _ride_join(bufs):
    n = len(bufs)

    def each(b, sems, act):
        send, recv = sems
        x, y, c, _, _ = _position()
        for t in range(n):
            act(b[t].at[c], b[t].at[1 - c], send.at[t], recv.at[t], (x, y, 1 - c))

    def start(ins, b, new, sems):
        each(b, sems, lambda mine, theirs, s, r, dev: _remote(mine, mine, s, r, dev).start())

    def finish(ins, b, new, sems):
        def act(mine, theirs, s, r, dev):
            _remote(mine, mine, s, r, dev).wait_send()
            _remote(theirs, theirs, s, r, dev).wait_recv()
        each(b, sems, act)

    return _Ride([], bufs, [], [(n,), (n,)], start, finish, ["sibling"])


def _all_reduce_small(pack, ride=None):
    rows = pack.shape[0]
    n_dev = 2 * N_CHIPS
    n_rb = 0 if ride is None else len(ride.bufs)

    def body(x_ref, *rest):
        o_ref = rest[n_rb]
        r_bufs = rest[n_rb + 1:2 * n_rb + 1]
        land, send, recv = rest[2 * n_rb + 1:2 * n_rb + 4]
        r_sems = rest[2 * n_rb + 4:]
        if ride is not None:
            ride.start([], r_bufs, [], r_sems)
        x, y, c, p, _ = _position()
        me = 2 * p + c
        land[me] = x_ref[...]
        peers = [(dx, dy, dc) for dx in range(2) for dy in range(2) for dc in range(2) if (dx, dy, dc) != (0, 0, 0)]
        for j, (dx, dy, dc) in enumerate(peers):
            _remote(land.at[me], land.at[me], send.at[j], recv.at[j], (x ^ dx, y ^ dy, c ^ dc)).start()
        for j, (dx, dy, dc) in enumerate(peers):
            src = 4 * (x ^ dx) + 2 * (y ^ dy) + (c ^ dc)
            _remote(land.at[me], land.at[me], send.at[j], recv.at[j], (x ^ dx, y ^ dy, c ^ dc)).wait_send()
            _remote(land.at[src], land.at[src], send.at[j], recv.at[j], (x ^ dx, y ^ dy, c ^ dc)).wait_recv()
        acc = land[0]
        for dev in range(1, n_dev):
            acc = acc + land[dev]
        o_ref[...] = acc
        if ride is not None:
            ride.finish([], r_bufs, [], r_sems)

    bufs = [] if ride is None else ride.bufs
    sems = [] if ride is None else [pltpu.SemaphoreType.DMA(s) for s in ride.sem_shapes]
    res = pl.pallas_call(
        body, in_specs=[pl.BlockSpec(memory_space=pltpu.VMEM)] + [ANY] * n_rb,
        out_specs=[pl.BlockSpec(memory_space=pltpu.VMEM)] + [ANY] * n_rb,
        out_shape=[SDS((rows, LANES), F32)] + [SDS(b.shape, b.dtype) for b in bufs],
        scratch_shapes=[pltpu.VMEM((n_dev, rows, LANES), F32), pltpu.SemaphoreType.DMA((n_dev - 1,)),
                        pltpu.SemaphoreType.DMA((n_dev - 1,))] + sems,
        input_output_aliases={1 + j: 1 + j for j in range(n_rb)},
        name="all_reduce_small", compiler_params=_params())(pack, *bufs)
    return res[0], list(res[1:])


def _add_own_half(name, full, recv, out_dtype):
    n4, _, h, cols = full.shape

    def body(s_ref, a_ref, b_ref, o_ref, own_ref):
        v = (a_ref[...].astype(F32) + b_ref[...].astype(F32)).astype(out_dtype)
        o_ref[...] = v

        @pl.when(pl.program_id(0) == s_ref[1])
        def _():
            own_ref[...] = v

    return pl.pallas_call(
        body,
        grid_spec=pltpu.PrefetchScalarGridSpec(
            num_scalar_prefetch=1, grid=(n4,),
            in_specs=[pl.BlockSpec((None, None, h, cols), lambda q, s: (q, s[0], 0, 0)),
                      pl.BlockSpec((None, h, cols), lambda q, s: (q, 0, 0))],
            out_specs=[pl.BlockSpec((None, h, cols), lambda q, s: (q, 0, 0)),
                       pl.BlockSpec((None, h, cols), lambda q, s: (s[1], 0, 0))]),
        out_shape=[SDS((n4, h, cols), out_dtype)] * 2, name=name, compiler_params=_params())(_mesh_scalars(), full, recv)


def _sum_chips(name, parts):
    n4, h, cols = parts.shape
    th = h // 4 if h % 64 == 0 else h

    def body(s_ref, a_ref, o_ref):
        acc = a_ref[0].astype(F32)
        for q in range(1, n4):
            acc = acc + a_ref[q].astype(F32)
        o_ref[...] = acc

    return pl.pallas_call(
        body,
        grid_spec=pltpu.PrefetchScalarGridSpec(
            num_scalar_prefetch=1, grid=(h // th,),
            in_specs=[pl.BlockSpec((n4, th, cols), lambda i, s: (0, i, 0))],
            out_specs=pl.BlockSpec((None, th, cols), lambda i, s: (s[0], i, 0))),
        out_shape=SDS((2, h, cols), F32), name=name, compiler_params=_params())(_mesh_scalars(), parts)


def _adamw_math(w, g, m, v):
    m2 = ADAM_B1 * m + (1.0 - ADAM_B1) * g
    v2 = ADAM_B2 * v + (1.0 - ADAM_B2) * (g * g)
    m_hat = m2 / (1.0 - ADAM_B1 ** ADAM_STEP)
    v_hat = v2 / (1.0 - ADAM_B2 ** ADAM_STEP)
    delta = -ADAM_LR * (m_hat / (jnp.sqrt(v_hat) + ADAM_EPS) + ADAM_WD * w)
    return delta, m2, v2


def _row_tile(rows, cols):
    cap = max(8, (1 << 18) // cols)
    best = 8
    for cand in range(8, min(rows, cap) + 1, 8):
        if rows % cand == 0:
            best = cand
    return best


def _adamw_big(name, w, g_layers, m, v):
    layers, rows, cols = w.shape
    tr = _row_tile(rows, cols)

    def body(w_ref, m_ref, v_ref, *rest):
        g_refs, (g_o, d_o, m_o, v_o) = rest[:layers], rest[layers:]
        gv = g_refs[0][...]
        for layer in range(1, layers):
            gv = jnp.where(pl.program_id(0) == layer, g_refs[layer][...], gv)
        d, mm, vv = _adamw_math(w_ref[...], gv, m_ref[...], v_ref[...])
        g_o[...] = gv
        d_o[...] = d
        m_o[...] = mm
        v_o[...] = vv

    blk = pl.BlockSpec((None, tr, cols), lambda l, i: (l, i, 0))
    g_blk = pl.BlockSpec((tr, cols), lambda l, i: (i, 0))
    return tuple(pl.pallas_call(
        body, grid=(layers, rows // tr), in_specs=[blk] * 3 + [g_blk] * layers, out_specs=[blk] * 4,
        out_shape=[SDS((layers, rows, cols), F32)] * 4, name=name,
        compiler_params=_params())(w, m, v, *[g.reshape(rows, cols) for g in g_layers]))


def _adamw_small(ws, gs, ms, vs):
    n = len(ws)
    flat = []
    for group in (ws, gs, ms, vs):
        flat += [a.reshape(-1, a.shape[-1]) for a in group]

    def body(*refs):
        w_r, g_r, m_r, v_r = refs[:n], refs[n:2 * n], refs[2 * n:3 * n], refs[3 * n:4 * n]
        d_o, m_o, v_o = refs[4 * n:5 * n], refs[5 * n:6 * n], refs[6 * n:7 * n]
        for j in range(n):
            d, mm, vv = _adamw_math(w_r[j][...], g_r[j][...], m_r[j][...], v_r[j][...])
            d_o[j][...] = d
            m_o[j][...] = mm
            v_o[j][...] = vv

    shapes = [SDS(a.shape, F32) for a in flat[:n]]
    outs = pl.pallas_call(body, out_shape=shapes * 3, name="adamw_small", compiler_params=_params())(*flat)
    res = []
    for k in range(3):
        res.append([outs[k * n + j].reshape(ws[j].shape) for j in range(n)])
    return res


BIG = ("ab_w_in", "ab_w_out", "cd_w_in", "cd_w_out", "ffn_w_gate", "ffn_w_up", "ffn_w_down")
V_BLOCK = (2 * A_WIDTH + QK_COLS) // B_WIDTH


def _pad_rows(a, rows):
    return jnp.pad(a, ((0, rows - a.shape[0]), (0, 0)))


A_IN, A_OUT, C_IN, C_OUT = ("ab_w_in", 0), ("ab_w_out", 0), ("cd_w_in", 0), ("cd_w_out", 0)
G0, U0, D0 = ("ffn_w_gate", 0), ("ffn_w_up", 0), ("ffn_w_down", 0)
G1, U1, D1 = ("ffn_w_gate", 1), ("ffn_w_up", 1), ("ffn_w_down", 1)
UNITS = (A_IN, A_OUT, G0, U0, D0, C_IN, C_OUT, G1, U1, D1)
ROWS_MINOR = ("ffn_w_gate", "ffn_w_up")
SMALL_SHARDED = ("small", 0)
REPLICATED_UNIT = ("replicated", 0)


class _Exchange:
    def __init__(self, enabled):
        self.enabled = enabled
        self.w, self.grad, self.recv, self.half, self.land, self.done = {}, {}, {}, {}, {}, {}

    def full(self, unit):
        b = self.w[unit]
        return b.reshape(N_CHIPS, 1, 2 * b.shape[2], b.shape[3])

    def _ride(self, phases):
        rides, sinks = [], []
        for kind, units in phases:
            if kind == "send":
                rides.append(_ride_gather_send([self.w[u] for u in units]))
                sinks.append(self.w)
            elif kind == "pass":
                rides.append(_ride_gather_pass([self.w[u] for u in units]))
                sinks.append(self.w)
            elif kind == "gather":
                rides.append(_ride_gather([self.w[u] for u in units]))
                sinks.append(self.w)
            elif kind == "swap":
                rides.append(_ride_swap([self.grad[u] for u in units]))
                sinks.append(self.recv)
            elif kind == "scatter":
                rides.append(_ride_scatter([self.half[u] for u in units], [self.land[u] for u in units]))
                sinks.append(self.land)
            else:
                rides.append(_ride_join([self.done[u] for u in units]))
                sinks.append(self.done)
        ride = functools.reduce(_ride_both, rides)

        def settle(res):
            n_bufs = sum(len(r.bufs) for r in rides)
            bufs, new = list(res[:n_bufs]), list(res[n_bufs:])
            for r, sink, (_, units) in zip(rides, sinks, phases):
                vals = [bufs.pop(0) for _ in r.bufs] + [new.pop(0) for _ in r.new_outs]
                for u, v in zip(units, vals):
                    sink[u] = v

        return ride, settle

    def run(self, fn, *args, phases=(), **kw):
        if not self.enabled or not phases:
            return fn(*args, **kw)
        ride, settle = self._ride(phases)
        out, res = fn(*args, ride=ride, **kw)
        settle(res)
        return out

    def alone(self, name, phases):
        if self.enabled:
            ride, settle = self._ride(phases)
            settle(_run_ride(name, ride))

    def pair_sum(self, units):
        if self.enabled:
            for u in units:
                dtype = F32 if u in (SMALL_SHARDED, REPLICATED_UNIT) else BF16
                self.half[u], self.land[u] = _add_own_half(f"pair_sum_{u[0]}_{u[1]}", self.grad[u], self.recv[u], dtype)

    def chip_sum(self, units):
        if self.enabled:
            for u in units:
                self.done[u] = _sum_chips(f"chip_sum_{u[0]}_{u[1]}", self.land[u])


def _local_step(x, target, ex, sp, h0=None):
    t, d = x.shape
    tabs = _rope_tables(t)
    gains = jnp.concatenate([jnp.tile(sp["q_norm_g"][g], HEAD_DIM // 8) for g in range(N_DIL)]
                            + [jnp.tile(sp["k_norm_g"][g], HEAD_DIM // 8) for g in range(N_DIL)]).reshape(1, QK_COLS)
    bias_t = sp["sgu_bias"].T
    cw = _pad_rows(sp["conv_c_w"], 32)
    dw = _pad_rows(sp["conv_d_w"], 8)
    cb, clg, clb = (sp[k].reshape(1, C_WIDTH) for k in ("conv_c_b", "c_ln_g", "c_ln_b"))
    slg, slb = sp["sgu_norm_g"].reshape(1, A_WIDTH), sp["sgu_norm_b"].reshape(1, A_WIDTH)
    g_ab, g_cd = sp["ab_norm_g"].reshape(1, d), sp["cd_norm_g"].reshape(1, d)
    g_f0, g_f1 = sp["ffn_norm_g"][0:1], sp["ffn_norm_g"][1:2]
    run = ex.run

    def w2d(unit):
        return ex.full(unit).reshape(-1, d)

    if h0 is None:
        h0 = _rms_fwd("rms_ab", x, g_ab)
    proj = run(_proj_in, "proj_ab", h0, ex.full(A_IN), 0, phases=[("send", [A_OUT, G0])])
    a_out = _mixer_a_fwd(proj, slg, slb, sp["sgu_w"], bias_t)
    qk, q1, q2, k1, k2 = run(_qk_fwd, proj, gains, tabs, phases=[("pass", [A_OUT, G0]), ("send", [U0])])
    regrouped_qk = {1: (q1, k1), 2: (q2, k2)}
    fwd_phases = ([("pass", [U0]), ("send", [D0])], [("pass", [D0]), ("send", [C_IN])],
                  [("pass", [C_IN]), ("send", [C_OUT])])
    qkv, o_list, l_list = [], [], []
    for g, rate in enumerate(DIL_RATES):
        if rate == 1:
            qk3, proj3 = qk.reshape(1, t, QK_COLS), proj.reshape(1, t, AB_IN)
            q, k, v = (qk3, g), (qk3, N_DIL + g), (proj3, V_BLOCK + g)
        else:
            vp, = _permute(f"regroup_v_{g}", [(proj, V_BLOCK + g)], rate)
            q, k, v = (regrouped_qk[g][0], 0), (regrouped_qk[g][1], 0), (vp, 0)
        qkv.append((q, k, v))
        o, l = run(_attn_fwd, f"attn_fwd_{g}", q, k, v, phases=fwd_phases[g])
        if rate == 1:
            o, l = o.reshape(t, B_WIDTH), l.reshape(t, B_WIDTH)
        o_list.append(o)
        l_list.append(l)
    cat, lse_tot, lse_1, lse_2 = _attn_merge(a_out, o_list, l_list)
    x1, hf0 = _proj_out("out_ab", cat, w2d(A_OUT), x, g_next=g_f0)
    fgate0, fup0, act0 = run(_ffn_in, "ffn_in_0", hf0, ex.full(G0), ex.full(U0), 0,
                           phases=[("pass", [C_OUT]), ("send", [G1, U1])])
    x2, h1 = run(_ffn_out, "ffn_out_0", act0, ex.full(D0), 0, x1, g_next=g_cd, phases=[("pass", [G1, U1]), ("send", [D1])])
    projcd = run(_proj_in, "proj_cd", h1, ex.full(C_IN), 0, phases=[("pass", [D1])])
    cat2, c1 = _mixer_cd_fwd(projcd, cw, cb, clg, clb, dw)
    x3, hf1 = _proj_out("out_cd", cat2, w2d(C_OUT), x2, g_next=g_f1)
    fgate1, fup1, act1 = _ffn_in("ffn_in_1", hf1, ex.full(G1), ex.full(U1), 0)
    dy, loss_acc, dy_b = _ffn_out("ffn_out_1", act1, ex.full(D1), 0, x3, target=target)
    loss = 0.5 * loss_acc[0, 0] / d

    late = [D1, G1, U1]
    dgate, dup = _ffn_dact("ffn_dact_1", dy_b, ex.full(D1), 0, fgate1, fup1)
    ex.grad[D1] = _wgrad_row_sharded("wgrad_down_1", act1, dy_b, True)
    ex.grad[G1] = _wgrad_row_sharded("wgrad_gate_1", dgate, hf1, True)
    ex.grad[U1] = _wgrad_row_sharded("wgrad_up_1", dup, hf1, True)
    g3, d_f1, g3_b = run(_dgrad_cols, "dgrad_ffn_1", [dgate, dup], [ex.full(G1), ex.full(U1)], 0, True, x3, g_f1, dy,
                         w_rows=True, phases=[("swap", late)])
    ex.pair_sum(late)

    dcat2 = _dgrad_rows("dgrad_out_cd", g3_b, w2d(C_OUT))
    ex.grad[C_OUT] = _wgrad_row_sharded("wgrad_out_cd", cat2, g3_b, False)
    dprojcd, d_cw, d_cb, d_clg, d_clb, d_dw = run(_mixer_cd_bwd, projcd, dcat2, c1, cw, clg, clb, dw, phases=[("scatter", late)])
    ex.chip_sum(late)
    ex.grad[C_IN] = _wgrad_col_sharded("wgrad_in_cd", h1, [dprojcd], False)[0]
    g2, d_cdn, g2_b = run(_dgrad_cols, "dgrad_in_cd", [dprojcd], [ex.full(C_IN)], 0, False, x2, g_cd, g3,
                          phases=[("join", late), ("swap", [C_OUT, C_IN])])
    ex.pair_sum([C_OUT, C_IN])

    dgate, dup = run(_ffn_dact, "ffn_dact_0", g2_b, ex.full(D0), 0, fgate0, fup0, phases=[("scatter", [C_OUT, C_IN])])
    ex.chip_sum([C_OUT, C_IN])
    ex.grad[D0] = _wgrad_row_sharded("wgrad_down_0", act0, g2_b, True)
    ex.grad[G0] = _wgrad_row_sharded("wgrad_gate_0", dgate, hf0, True)
    ex.grad[U0] = _wgrad_row_sharded("wgrad_up_0", dup, hf0, True)
    small = {"cd_norm_g": d_cdn, "conv_c_w": d_cw[:C_KERNEL], "conv_c_b": d_cb, "c_ln_g": d_clg, "c_ln_b": d_clb,
             "conv_d_w": d_dw[:D_KERNEL]}
    ex.grad[SMALL_SHARDED] = _split_full_small(small).reshape(N_CHIPS, 2, SHARDED_ROWS // 2, LANES)
    mid = [D0, G0, U0, SMALL_SHARDED]
    g1, d_f0, g1_b = run(_dgrad_cols, "dgrad_ffn_0", [dgate, dup], [ex.full(G0), ex.full(U0)], 0, True, x1, g_f0, g2,
                         w_rows=True, phases=[("join", [C_OUT, C_IN]), ("swap", mid)])
    ex.pair_sum(mid)

    dcat = _dgrad_rows("dgrad_out_ab", g1_b, w2d(A_OUT))
    ex.grad[A_OUT] = _wgrad_row_sharded("wgrad_out_ab", cat, g1_b, False)
    d_a, d_sw, d_sbt, d_slg, d_slb = _mixer_a_bwd(proj, dcat, slg, slb, sp["sgu_w"], bias_t)
    early = {"sgu_norm_g": d_slg, "sgu_norm_b": d_slb, "sgu_w": d_sw, "sgu_bias": d_sbt.T}
    ex.grad[REPLICATED_UNIT] = jnp.broadcast_to(
        _pack_replicated(early, REPLICATED_EARLY, REPLICATED_EARLY_ROWS).reshape(2, REPLICATED_EARLY_ROWS // 2, LANES),
        (N_CHIPS, 2, REPLICATED_EARLY_ROWS // 2, LANES))
    last = [A_OUT, REPLICATED_UNIT]
    dbb, dd, db_1, dd_1, db_2, dd_2 = _attn_bwd_prep(dcat, cat)
    regrouped_bwd = {1: (db_1, lse_1, dd_1), 2: (db_2, lse_2, dd_2)}
    bwd_phases = ([("scatter", [D0, SMALL_SHARDED])],
                  [("scatter", [G0]), ("join", [D0, SMALL_SHARDED]), ("swap", last)],
                  [("scatter", [U0]), ("join", [G0])])
    dqs, dks, dvs = [], [], []
    for g, rate in enumerate(DIL_RATES):
        q, k, v = qkv[g]
        if rate == 1:
            db3, l3, dd3 = (a.reshape(1, t, B_WIDTH) for a in (dbb, lse_tot, dd))
        else:
            db3, l3, dd3 = regrouped_bwd[g]
        if g == 1:
            ex.chip_sum([D0, SMALL_SHARDED])
        elif g == 2:
            ex.chip_sum([G0])
            ex.pair_sum(last)
        dq, dk, dv = run(_attn_bwd, f"attn_bwd_{g}", q, k, v, db3, l3, dd3, phases=bwd_phases[g])
        if rate == 1:
            dq, dk, dv = (a.reshape(t, B_WIDTH) for a in (dq, dk, dv))
        dqs.append(dq)
        dks.append(dk)
        dvs.append(dv)
    ex.chip_sum([U0])
    dproj, d_gains = run(_dproj_assemble, proj, d_a, dqs, dks, dvs, gains, tabs, phases=[("scatter", last), ("join", [U0])])
    ex.chip_sum(last)
    d_gains = _fold_heads(d_gains)[0].reshape(2, N_DIL, B_WIDTH)[:, :, :HEAD_DIM]
    ex.grad[A_IN] = _wgrad_col_sharded("wgrad_in_ab", h0, [dproj], False)[0]
    ex.alone("swap_last", [("swap", [A_IN])])
    ex.pair_sum([A_IN])
    gx, d_abn = run(_dgrad_cols, "dgrad_in_ab", [dproj], [ex.full(A_IN)], 0, False, x, g_ab, g1, bf16_copy=False,
                    phases=[("join", last), ("scatter", [A_IN])])
    ex.chip_sum([A_IN])

    small.update({
        "ab_norm_g": d_abn, "sgu_norm_g": d_slg, "sgu_norm_b": d_slb, "sgu_w": d_sw, "sgu_bias": d_sbt.T,
        "q_norm_g": d_gains[0], "k_norm_g": d_gains[1], "ffn_norm_g": jnp.concatenate([d_f0, d_f1], axis=0),
    })
    return loss, gx, small


SHARDED_SMALL = ("cd_norm_g", "conv_c_w", "conv_c_b", "c_ln_g", "c_ln_b", "conv_d_w")
SHARDED_ROWS = 48
REPLICATED_EARLY = ("sgu_norm_g", "sgu_norm_b", "sgu_w", "sgu_bias")
REPLICATED_EARLY_ROWS = 528
REPLICATED_LATE = ("ab_norm_g", "q_norm_g", "k_norm_g", "ffn_norm_g", "loss")
REPLICATED_LATE_ROWS = 32
REPLICATED_SMALL = REPLICATED_EARLY + REPLICATED_LATE[:-1]


def _pack_sharded(parts):
    rows = [parts[k].reshape(-1, LANES) for k in SHARDED_SMALL]
    return _pad_rows(jnp.concatenate(rows, axis=0), SHARDED_ROWS)


def _split_full_small(small):
    per_chip = []
    for q in range(N_CHIPS):
        parts = {}
        for k in SHARDED_SMALL:
            a = small[k]
            a = a.reshape(-1, a.shape[-1])
            n = a.shape[-1] // N_CHIPS
            parts[k] = a[:, q * n:(q + 1) * n]
        per_chip.append(_pack_sharded(parts))
    return jnp.stack(per_chip)


def _unpack_sharded(pack, shapes):
    out, r = {}, 0
    for k in SHARDED_SMALL:
        n = math.prod(shapes[k]) // LANES
        out[k] = pack[r:r + n].reshape(shapes[k])
        r += n
    return out


def _gathered_small(packs, shapes):
    per_chip = [_unpack_sharded(packs[q], shapes) for q in range(N_CHIPS)]
    return {k: jnp.concatenate([pc[k] for pc in per_chip], axis=-1) for k in SHARDED_SMALL}


def _pack_replicated(small, names, total_rows):
    rows = []
    for k in names:
        a = small[k].reshape(-1)
        a = jnp.pad(a, (0, (-a.shape[0]) % LANES))
        rows.append(a.reshape(-1, LANES))
    return _pad_rows(jnp.concatenate(rows, axis=0), total_rows)


def _unpack_replicated(pack, shapes, names):
    out, r = {}, 0
    for k in names:
        size = math.prod(shapes[k])
        n = -(-size // LANES)
        out[k] = pack[r:r + n].reshape(-1)[:size].reshape(shapes[k])
        r += n
    return out


WEIGHT_ORDER = ("ab_norm_g", "ab_w_in", "sgu_norm_g", "sgu_norm_b", "sgu_w", "sgu_bias", "q_norm_g", "k_norm_g", "ab_w_out",
                "cd_norm_g", "cd_w_in", "conv_c_w", "conv_c_b", "c_ln_g", "c_ln_b", "conv_d_w", "cd_w_out", "ffn_norm_g",
                "ffn_w_gate", "ffn_w_up", "ffn_w_down")


def kernel(x, ab_norm_g, ab_w_in, sgu_norm_g, sgu_norm_b, sgu_w, sgu_bias, q_norm_g, k_norm_g, ab_w_out, cd_norm_g, cd_w_in, conv_c_w, conv_c_b, c_ln_g, c_ln_b, conv_d_w, cd_w_out, ffn_norm_g, ffn_w_gate, ffn_w_up, ffn_w_down, loss_target, m_ab_norm_g, m_ab_w_in, m_sgu_norm_g, m_sgu_norm_b, m_sgu_w, m_sgu_bias, m_q_norm_g, m_k_norm_g, m_ab_w_out, m_cd_norm_g, m_cd_w_in, m_conv_c_w, m_conv_c_b, m_c_ln_g, m_c_ln_b, m_conv_d_w, m_cd_w_out, m_ffn_norm_g, m_ffn_w_gate, m_ffn_w_up, m_ffn_w_down, v_ab_norm_g, v_ab_w_in, v_sgu_norm_g, v_sgu_norm_b, v_sgu_w, v_sgu_bias, v_q_norm_g, v_k_norm_g, v_ab_w_out, v_cd_norm_g, v_cd_w_in, v_conv_c_w, v_conv_c_b, v_c_ln_g, v_c_ln_b, v_conv_d_w, v_cd_w_out, v_ffn_norm_g, v_ffn_w_gate, v_ffn_w_up, v_ffn_w_down):
    args = dict(locals())
    ws = {k: args[k] for k in WEIGHT_ORDER}
    ms = {k: args["m_" + k] for k in WEIGHT_ORDER}
    vs = {k: args["v_" + k] for k in WEIGHT_ORDER}
    small_names = [k for k in WEIGHT_ORDER if k not in BIG]
    t, d = x.shape[1:]

    for group in (ws, ms, vs):
        for k in ROWS_MINOR:
            group[k] = jnp.swapaxes(group[k], 1, 2)
    ex = _Exchange(enabled=True)
    ex.w[A_IN] = _stage_own("stage_ab_w_in", ws["ab_w_in"], 0, BF16)
    own_small = _pack_sharded({k: ws[k][0] for k in SHARDED_SMALL})
    ex.w[SMALL_SHARDED] = _stage_own("stage_small", own_small[None], 0, F32)
    rest = [u for u in UNITS if u != A_IN]
    x2 = x.reshape(t, d)
    h0, staged = ex.run(_stage_rest_and_norm, x2, ws["ab_norm_g"], [(ws[name], layer) for name, layer in rest],
                        phases=[("gather", [A_IN, SMALL_SHARDED])])
    ex.w.update(zip(rest, staged))
    sp = _gathered_small(ex.w[SMALL_SHARDED].reshape(N_CHIPS, SHARDED_ROWS, LANES), {k: ws[k].shape[1:] for k in SHARDED_SMALL})
    for k in REPLICATED_SMALL:
        sp[k] = ws[k] if k == "ffn_norm_g" else ws[k][0]

    loss, grad_x, g_small = _local_step(x2, loss_target.reshape(t, d), ex, sp, h0)

    shapes = {k: ws[k].shape for k in REPLICATED_SMALL}
    shapes["loss"] = (1,)
    g_small["loss"] = loss
    join_last, settle = ex._ride([("join", [A_IN])])
    late, joined = _all_reduce_small(_pack_replicated(g_small, REPLICATED_LATE, REPLICATED_LATE_ROWS), join_last)
    settle(joined)
    grad = _unpack_sharded(ex.done[SMALL_SHARDED].reshape(SHARDED_ROWS, LANES), {k: ws[k].shape for k in SHARDED_SMALL})
    grad.update(_unpack_replicated(ex.done[REPLICATED_UNIT].reshape(REPLICATED_EARLY_ROWS, LANES), shapes, REPLICATED_EARLY))
    grad.update(_unpack_replicated(late, shapes, REPLICATED_LATE))
    loss = grad.pop("loss")[0]

    delta, new_m, new_v = {}, {}, {}
    for k in BIG:
        g_layers = [ex.done[(k, layer)] for layer in range(ws[k].shape[0])]
        outs = _adamw_big("adamw_" + k, ws[k], g_layers, ms[k], vs[k])
        if k in ROWS_MINOR:
            outs = [jnp.swapaxes(o, 1, 2) for o in outs]
        grad[k], delta[k], new_m[k], new_v[k] = outs
    d_s, m_s, v_s = _adamw_small([ws[k] for k in small_names], [grad[k] for k in small_names],
                                 [ms[k] for k in small_names], [vs[k] for k in small_names])
    for j, k in enumerate(small_names):
        delta[k], new_m[k], new_v[k] = d_s[j], m_s[j], v_s[j]

    return (loss, grad_x[None], *[grad[k] for k in WEIGHT_ORDER], *[delta[k] for k in WEIGHT_ORDER],
            *[new_m[k] for k in WEIGHT_ORDER], *[new_v[k] for k in WEIGHT_ORDER])
```

```python
import functools
import math

import jax
import jax.numpy as jnp
from jax import lax
from jax.experimental import pallas as pl
from jax.experimental.pallas import tpu as pltpu

F32 = jnp.float32
BF16 = jnp.bfloat16
SDS = jax.ShapeDtypeStruct

N_CHIPS = 4
EPS = 1e-6
NEG_INF = -1e30
CHUNK = 128
A_GROUPS = 4
A_WIDTH = 512
N_DIL = 3
DIL_RATES = (1, 4, 16)
HEAD_DIM = 64
B_WIDTH = 512
ROPE_DIM = 16
ROPE_THETA = 500000.0
C_WIDTH = 512
C_KERNEL = 31
D_KERNEL = 3
HALO = 32
ATT_BLOCK = 128
LANES = 128

ADAM_LR = 0.001
ADAM_B1 = 0.9
ADAM_B2 = 0.999
ADAM_EPS = 1e-08
ADAM_WD = 0.01
ADAM_STEP = 10

VMEM_LIMIT = 56 * 1024 * 1024

NN = (((1,), (0,)), ((), ()))
NT = (((1,), (1,)), ((), ()))
TN = (((0,), (0,)), ((), ()))

TILES = {"proj_in": 2048, "proj_out": 1024, "ffn_in": 1024, "ffn_out": 1024, "ffn_dact": 512, "dgrad_cols": 512,
         "dgrad_rows": 1024, "wgrad": 4096}


def _params(sem=None, collective_id=None):
    return pltpu.CompilerParams(dimension_semantics=sem, vmem_limit_bytes=VMEM_LIMIT, collective_id=collective_id)


def _bf(v):
    return v if v.dtype == BF16 else v.astype(BF16)


def _dot(a, b, dims):
    return lax.dot_general(_bf(a), _bf(b), dims, preferred_element_type=F32)


def _dot_hi(a, b):
    return jnp.dot(a, b, precision=lax.Precision.HIGHEST, preferred_element_type=F32)


def _sigmoid(v):
    return 0.5 * jnp.tanh(0.5 * v) + 0.5


def _gelu(v):
    return 0.5 * v * (1.0 + lax.erf(v * (1.0 / math.sqrt(2.0))))


def _gelu_grad(v):
    cdf = 0.5 * (1.0 + lax.erf(v * (1.0 / math.sqrt(2.0))))
    return cdf + v * jnp.exp(-0.5 * v * v) * (1.0 / math.sqrt(2.0 * math.pi))


def _segment_mean_matrix(seg, scale=None):
    r = lax.broadcasted_iota(jnp.int32, (LANES, LANES), 0) // seg
    c = lax.broadcasted_iota(jnp.int32, (LANES, LANES), 1) // seg
    return jnp.where(r == c, (1.0 / seg) if scale is None else scale, 0.0).astype(BF16)


def _segment_dot(v, seg):
    hi = v.astype(BF16)
    lo = (v - hi.astype(F32)).astype(BF16)
    return jnp.dot(hi, seg, preferred_element_type=F32) + jnp.dot(lo, seg, preferred_element_type=F32)


MESH = pl.DeviceIdType.MESH
ANY = pl.BlockSpec(memory_space=pl.ANY)


def _position():
    x, y, c = lax.axis_index("x"), lax.axis_index("y"), lax.axis_index("c")
    others = [(1 - x, y), (x, 1 - y), (1 - x, 1 - y)]
    return x, y, c, 2 * x + y, others


class _Ride:
    def __init__(self, ins, bufs, new_outs, sem_shapes, start, finish, reach):
        self.ins, self.bufs, self.new_outs, self.sem_shapes = list(ins), list(bufs), list(new_outs), list(sem_shapes)
        self.start, self.finish = start, finish
        self.reach = frozenset(reach)

    def entry_barrier(self):
        x, y, c, _, others = _position()
        peers = ([(x, y, 1 - c)] if "sibling" in self.reach else []) + ([(qx, qy, c) for qx, qy in others] if "chips" in self.reach else [])
        barrier = pltpu.get_barrier_semaphore()
        for peer in peers:
            pl.semaphore_signal(barrier, inc=1, device_id=peer, device_id_type=MESH)
        pl.semaphore_wait(barrier, len(peers))

    @property
    def collective_id(self):
        return {frozenset(["sibling"]): 0, frozenset(["chips"]): 1, frozenset(["sibling", "chips"]): 2}[self.reach]


def _ride_both(a, b):
    na = (len(a.ins), len(a.bufs), len(a.new_outs), len(a.sem_shapes))

    def split(ins, bufs, new, sems):
        return ((ins[:na[0]], bufs[:na[1]], new[:na[2]], sems[:na[3]]), (ins[na[0]:], bufs[na[1]:], new[na[2]:], sems[na[3]:]))

    def start(*refs):
        ra, rb = split(*refs)
        a.start(*ra)
        b.start(*rb)

    def finish(*refs):
        ra, rb = split(*refs)
        a.finish(*ra)
        b.finish(*rb)

    return _Ride(a.ins + b.ins, a.bufs + b.bufs, a.new_outs + b.new_outs, a.sem_shapes + b.sem_shapes, start, finish,
                 a.reach | b.reach)


def _call(body, *, grid, in_specs, out_specs, out_shape, operands, name, scratch_shapes=(), aliases=None, ride=None,
          prefetch=None):
    off = 0 if prefetch is None else 1
    lead = [] if prefetch is None else [prefetch]

    params = _params(collective_id=None if ride is None else ride.collective_id)

    def launch(kernel_body, in_specs_, out_specs_, out_shape_, scratch_, aliases_, *args):
        if prefetch is None:
            return pl.pallas_call(kernel_body, grid=grid, in_specs=in_specs_, out_specs=out_specs_, out_shape=out_shape_,
                                  scratch_shapes=scratch_, input_output_aliases=aliases_, name=name,
                                  compiler_params=params)(*args)
        spec = pltpu.PrefetchScalarGridSpec(num_scalar_prefetch=1, grid=grid, in_specs=in_specs_, out_specs=out_specs_,
                                            scratch_shapes=scratch_)
        return pl.pallas_call(kernel_body, grid_spec=spec, out_shape=out_shape_, input_output_aliases=aliases_, name=name,
                              compiler_params=params)(*lead, *args)

    if ride is None:
        return launch(body, list(in_specs), out_specs, out_shape, list(scratch_shapes), dict(aliases or {}), *operands)
    multi = isinstance(out_shape, (list, tuple))
    out_shapes = list(out_shape) if multi else [out_shape]
    o_specs = list(out_specs) if multi else [out_specs]
    n_in, n_out, n_scr = off + len(operands), len(out_shapes), len(scratch_shapes)
    n_ri, n_rb, n_rn = len(ride.ins), len(ride.bufs), len(ride.new_outs)

    def carrying(*refs):
        k = n_in
        r_ins = refs[k:k + n_ri]
        k += n_ri + n_rb
        outs = refs[k:k + n_out]
        k += n_out
        r_bufs = refs[k:k + n_rb]
        k += n_rb
        r_new = refs[k:k + n_rn]
        k += n_rn
        scratch = refs[k:k + n_scr]
        sems = refs[k + n_scr:]
        first, last = None, None
        for axis, size in enumerate(grid):
            pid = pl.program_id(axis)
            first = (pid == 0) if first is None else first & (pid == 0)
            last = (pid == size - 1) if last is None else last & (pid == size - 1)

        @pl.when(first)
        def _():
            ride.entry_barrier()
            ride.start(r_ins, r_bufs, r_new, sems)

        body(*refs[:n_in], *outs, *scratch)

        @pl.when(last)
        def _():
            ride.finish(r_ins, r_bufs, r_new, sems)

    all_aliases = dict(aliases or {})
    for j in range(n_rb):
        all_aliases[n_in + n_ri + j] = n_out + j
    res = launch(
        carrying, list(in_specs) + [ANY] * (n_ri + n_rb), o_specs + [ANY] * (n_rb + n_rn),
        out_shapes + [SDS(b.shape, b.dtype) for b in ride.bufs] + ride.new_outs,
        list(scratch_shapes) + [pltpu.SemaphoreType.DMA(s) for s in ride.sem_shapes], all_aliases,
        *operands, *ride.ins, *ride.bufs)
    outs = res[:n_out]
    return (list(outs) if multi else outs[0]), list(res[n_out:])


def _run_ride(name, ride):
    n_ri, n_rb, n_rn = len(ride.ins), len(ride.bufs), len(ride.new_outs)

    def body(*refs):
        r_ins = refs[:n_ri]
        r_bufs = refs[n_ri + n_rb:n_ri + 2 * n_rb]
        r_new = refs[n_ri + 2 * n_rb:n_ri + 2 * n_rb + n_rn]
        sems = refs[n_ri + 2 * n_rb + n_rn:]
        ride.entry_barrier()
        ride.start(r_ins, r_bufs, r_new, sems)
        ride.finish(r_ins, r_bufs, r_new, sems)

    return list(pl.pallas_call(
        body, in_specs=[ANY] * (n_ri + n_rb), out_specs=[ANY] * (n_rb + n_rn),
        out_shape=[SDS(b.shape, b.dtype) for b in ride.bufs] + ride.new_outs,
        scratch_shapes=[pltpu.SemaphoreType.DMA(s) for s in ride.sem_shapes],
        input_output_aliases={n_ri + j: j for j in range(n_rb)}, name=name,
        compiler_params=pltpu.CompilerParams(collective_id=ride.collective_id))(*ride.ins, *ride.bufs))


def _whole(ref, p):
    return ref[...]


def _slab(ref, p):
    return ref[p]


def _matmul(name, grid, pairs, extras, outs, dims, epi, *, slabs=1, n_acc=1, ride=None):
    n_pairs, n_ex, n_out = len(pairs), len(extras), len(outs)

    def body(*refs):
        ab = refs[:2 * n_pairs]
        ex = refs[2 * n_pairs:2 * n_pairs + n_ex]
        out_refs = refs[2 * n_pairs + n_ex:2 * n_pairs + n_ex + n_out]
        pids = tuple(pl.program_id(a) for a in range(len(grid)))
        parts = [None] * n_acc
        for p in range(slabs):
            for j, (_, _, a_pick, _, _, b_pick, acc) in enumerate(pairs):
                d = _dot(a_pick(ab[2 * j], p), b_pick(ab[2 * j + 1], p), dims)
                parts[acc] = d if parts[acc] is None else parts[acc] + d
        epi(parts, ex, out_refs, pids)

    operands, in_specs = [], []
    for a, a_spec, _, b, b_spec, _, _ in pairs:
        operands += [a, b]
        in_specs += [a_spec, b_spec]
    for e, e_spec in extras:
        operands.append(e)
        in_specs.append(e_spec)
    return _call(body, grid=grid, in_specs=in_specs, out_specs=[o[1] for o in outs], out_shape=[o[0] for o in outs],
                 operands=operands, name=name, ride=ride)


def _rms_rows(v, g):
    r = lax.rsqrt(jnp.mean(v * v, axis=-1, keepdims=True) + EPS)
    return v * r * g


def _rms_fwd(name, x, g):
    t, d = x.shape
    tm = 512

    def body(x_ref, g_ref, o_ref):
        o_ref[...] = _rms_rows(x_ref[...], g_ref[...]).astype(BF16)

    return pl.pallas_call(
        body, grid=(t // tm,),
        in_specs=[pl.BlockSpec((tm, d), lambda i: (i, 0)), pl.BlockSpec((1, d), lambda i: (0, 0))],
        out_specs=pl.BlockSpec((tm, d), lambda i: (i, 0)), out_shape=SDS((t, d), BF16), name=name,
        compiler_params=_params())(x, g)


def _epi_residual_norm(accs, ex, outs, pids):
    x_new = accs[0] + ex[0][...]
    outs[0][...] = x_new
    outs[1][...] = _rms_rows(x_new, ex[1][...]).astype(BF16)


def _epi_residual_loss(accs, ex, outs, pids):
    y = accs[0] + ex[0][...]
    err = y - ex[1][...]
    dy = err * (1.0 / err.shape[-1])
    outs[0][...] = dy
    outs[2][...] = dy.astype(BF16)

    @pl.when(pids[0] == 0)
    def _():
        outs[1][...] = jnp.zeros_like(outs[1])

    outs[1][...] += jnp.sum(err * err)


def _epi_rms_bwd(accs, ex, outs, pids):
    dh = accs[0]
    xv, g, res = ex[0][...], ex[1][...], ex[2][...]
    r = lax.rsqrt(jnp.mean(xv * xv, axis=-1, keepdims=True) + EPS)
    xh = xv * r
    dy = dh * g
    dx = res + r * (dy - xh * jnp.mean(dy * xh, axis=-1, keepdims=True))
    outs[0][...] = dx
    if len(outs) > 2:
        outs[2][...] = dx.astype(BF16)

    @pl.when(pids[0] == 0)
    def _():
        outs[1][...] = jnp.zeros_like(outs[1])

    outs[1][...] += jnp.sum(dh * xh, axis=0, keepdims=True)


def _row_spec(tm, d):
    return pl.BlockSpec((tm, d), lambda i, *_: (i, 0))


def _const_spec(shape):
    nd = len(shape)
    return pl.BlockSpec(shape, lambda *_: (0,) * nd)


def _proj_in(name, h, w, layer, ride=None):
    t, d = h.shape
    n4 = w.shape[-1]
    tm = TILES["proj_in"]

    def epi(accs, ex, outs, pids):
        outs[0][...] = accs[0].astype(BF16)

    res = _matmul(
        name, (N_CHIPS, t // tm),
        [(h, pl.BlockSpec((tm, d), lambda p, i: (i, 0)), _whole,
          w, pl.BlockSpec((None, None, d, n4), lambda p, i: (p, layer, 0, 0)), _whole, 0)],
        [], [(SDS((t, N_CHIPS * n4), BF16), pl.BlockSpec((tm, n4), lambda p, i: (i, p)))],
        NN, epi, ride=ride)
    return res[0] if ride is None else (res[0][0], res[1])


def _proj_out(name, a, w, x, g_next=None, target=None):
    t, k = a.shape
    d = w.shape[-1]
    tm = TILES["proj_out"]
    if target is None:
        extras = [(x, _row_spec(tm, d)), (g_next, _const_spec((1, d)))]
        outs = [(SDS((t, d), F32), _row_spec(tm, d)), (SDS((t, d), BF16), _row_spec(tm, d))]
        epi = _epi_residual_norm
    else:
        extras = [(x, _row_spec(tm, d)), (target, _row_spec(tm, d))]
        outs = [(SDS((t, d), F32), _row_spec(tm, d)), (SDS((8, LANES), F32), _const_spec((8, LANES))),
                (SDS((t, d), BF16), _row_spec(tm, d))]
        epi = _epi_residual_loss
    return _matmul(name, (t // tm,), [(a, _row_spec(tm, k), _whole, w, _const_spec((k, d)), _whole, 0)], extras, outs, NN, epi)


def _ffn_in(name, h, wg, wu, layer, ride=None):
    t, d = h.shape
    n4 = wg.shape[-2]
    tm = TILES["ffn_in"]

    def epi(accs, ex, outs, pids):
        gate, up = accs
        s = _sigmoid(gate)
        silu = gate * s
        outs[0][...] = (up * (s + silu - silu * s)).astype(BF16)
        outs[1][...] = silu.astype(BF16)
        outs[2][...] = (silu * up).astype(BF16)

    w_spec = pl.BlockSpec((None, None, n4, d), lambda p, i: (p, layer, 0, 0))
    h_spec = pl.BlockSpec((tm, d), lambda p, i: (i, 0))
    o = (SDS((N_CHIPS, t, n4), BF16), pl.BlockSpec((None, tm, n4), lambda p, i: (p, i, 0)))
    return _matmul(name, (N_CHIPS, t // tm),
                   [(h, h_spec, _whole, wg, w_spec, _whole, 0), (h, h_spec, _whole, wu, w_spec, _whole, 1)], [],
                   [o, o, o], NT, epi, n_acc=2, ride=ride)


def _ffn_out(name, act, wd, layer, x, g_next=None, target=None, ride=None):
    _, t, n4 = act.shape
    d = wd.shape[-1]
    tm = TILES["ffn_out"]
    xs = _row_spec(tm, d)
    if target is None:
        extras = [(x, xs), (g_next, _const_spec((1, d)))]
        outs = [(SDS((t, d), F32), xs), (SDS((t, d), BF16), xs)]
        epi = _epi_residual_norm
    else:
        extras = [(x, xs), (target, xs)]
        outs = [(SDS((t, d), F32), xs), (SDS((8, LANES), F32), _const_spec((8, LANES))), (SDS((t, d), BF16), xs)]
        epi = _epi_residual_loss
    return _matmul(
        name, (t // tm,),
        [(act, pl.BlockSpec((N_CHIPS, tm, n4), lambda i: (0, i, 0)), _slab,
          wd, pl.BlockSpec((N_CHIPS, None, n4, d), lambda i: (0, layer, 0, 0)), _slab, 0)],
        extras, outs, NN, epi, slabs=N_CHIPS, ride=ride)


def _ffn_dact(name, g, wd, layer, gate, up, ride=None):
    t, d = g.shape
    n4 = wd.shape[-2]
    tm = TILES["ffn_dact"]

    def body(g_ref, w_ref, gate_ref, up_ref, dgate_ref, dup_ref):
        gv = g_ref[...]
        for p in range(N_CHIPS):
            dact = _dot(gv, w_ref[p], NT)
            dgate_ref[p] = (dact * gate_ref[p].astype(F32)).astype(BF16)
            dup_ref[p] = (dact * up_ref[p].astype(F32)).astype(BF16)

    blk = pl.BlockSpec((N_CHIPS, tm, n4), lambda i: (0, i, 0))
    return _call(
        body, grid=(t // tm,),
        in_specs=[_row_spec(tm, d), pl.BlockSpec((N_CHIPS, None, n4, d), lambda i: (0, layer, 0, 0)), blk, blk],
        out_specs=[blk, blk], out_shape=[SDS((N_CHIPS, t, n4), BF16)] * 2, operands=[g, wd, gate, up], name=name, ride=ride)


def _copy_epi(accs, ex, outs, pids):
    for a, o in zip(accs, outs):
        o[...] = a.astype(o.dtype)


def _dgrad_cols(name, dz_list, w_list, layer, three_d, x, g, res, bf16_copy=True, w_rows=False, ride=None):
    t, d = x.shape
    n4 = w_list[0].shape[-2 if w_rows else -1]
    tm = TILES["dgrad_cols"]
    if three_d:
        zs, z_pick = pl.BlockSpec((N_CHIPS, tm, n4), lambda i: (0, i, 0)), _slab
    else:
        zs, z_pick = _row_spec(tm, N_CHIPS * n4), (lambda ref, p: ref[:, p * n4:(p + 1) * n4])
    ws = pl.BlockSpec((N_CHIPS, None) + ((n4, d) if w_rows else (d, n4)), lambda i: (0, layer, 0, 0))
    xs = _row_spec(tm, d)
    return _matmul(
        name, (t // tm,), [(dz, zs, z_pick, w, ws, _slab, 0) for dz, w in zip(dz_list, w_list)],
        [(x, xs), (g, _const_spec((1, d))), (res, xs)],
        [(SDS((t, d), F32), xs), (SDS((1, d), F32), _const_spec((1, d)))] + ([(SDS((t, d), BF16), xs)] if bf16_copy else []),
        NN if w_rows else NT, _epi_rms_bwd, slabs=N_CHIPS, ride=ride)


def _dgrad_rows(name, g, w):
    t, d = g.shape
    k = w.shape[0]
    tm = TILES["dgrad_rows"]
    return _matmul(name, (t // tm,), [(g, _row_spec(tm, d), _whole, w, _const_spec((k, d)), _whole, 0)], [],
                   [(SDS((t, k), F32), _row_spec(tm, k))], NT, _copy_epi)[0]


A_TILE = 256


def _a_common(p_ref, lg_ref, lb_ref):
    pv = p_ref[...].astype(F32)
    a = _gelu(pv)
    u, v = a[:, :A_WIDTH], a[:, A_WIDTH:]
    vc = v - jnp.mean(v, axis=-1, keepdims=True)
    rs = lax.rsqrt(jnp.mean(vc * vc, axis=-1, keepdims=True) + EPS)
    vhat = vc * rs
    vn = vhat * lg_ref[...] + lb_ref[...]
    return pv, u, vhat, rs, vn.astype(BF16)


def _tril_weights(w_ref, g):
    r = lax.broadcasted_iota(jnp.int32, (CHUNK, CHUNK), 0)
    c = lax.broadcasted_iota(jnp.int32, (CHUNK, CHUNK), 1)
    return jnp.where(c <= r, w_ref[g], 0.0).astype(BF16), c <= r


def _mixer_a_fwd(proj, lg, lb, w, bias_t):
    t = proj.shape[0]

    def body(p_ref, lg_ref, lb_ref, w_ref, bt_ref, o_ref):
        _, u, _, _, vnb = _a_common(p_ref, lg_ref, lb_ref)
        for g in range(A_GROUPS):
            wt, _ = _tril_weights(w_ref, g)
            cs = slice(g * CHUNK, (g + 1) * CHUNK)
            for ch in range(A_TILE // CHUNK):
                rs_ = slice(ch * CHUNK, (ch + 1) * CHUNK)
                mixed = _dot(wt, vnb[rs_, cs], NN) + bt_ref[:, g:g + 1]
                o_ref[rs_, cs] = (u[rs_, cs] * mixed).astype(BF16)

    return pl.pallas_call(
        body, grid=(t // A_TILE,),
        in_specs=[pl.BlockSpec((A_TILE, 2 * A_WIDTH), lambda i: (i, 0)), _const_spec((1, A_WIDTH)),
                  _const_spec((1, A_WIDTH)), _const_spec((A_GROUPS, CHUNK, CHUNK)), _const_spec((CHUNK, A_GROUPS))],
        out_specs=pl.BlockSpec((A_TILE, A_WIDTH), lambda i: (i, 0)), out_shape=SDS((t, A_WIDTH), BF16),
        name="mixer_a_fwd", compiler_params=_params())(proj, lg, lb, w, bias_t)


def _mixer_a_bwd(proj, dcat, lg, lb, w, bias_t):
    t = proj.shape[0]

    def body(p_ref, da_ref, lg_ref, lb_ref, w_ref, bt_ref, dp_ref, dw_ref, dbt_ref, dlg_ref, dlb_ref, du_scr, dvn_scr):
        @pl.when(pl.program_id(0) == 0)
        def _():
            dw_ref[...] = jnp.zeros_like(dw_ref)
            dbt_ref[...] = jnp.zeros_like(dbt_ref)
            dlg_ref[...] = jnp.zeros_like(dlg_ref)
            dlb_ref[...] = jnp.zeros_like(dlb_ref)

        pv, u, vhat, rs, vnb = _a_common(p_ref, lg_ref, lb_ref)
        da = da_ref[...]
        for g in range(A_GROUPS):
            wt, keep = _tril_weights(w_ref, g)
            cs = slice(g * CHUNK, (g + 1) * CHUNK)
            for ch in range(A_TILE // CHUNK):
                rs_ = slice(ch * CHUNK, (ch + 1) * CHUNK)
                vg = vnb[rs_, cs]
                mixed = _dot(wt, vg, NN) + bt_ref[:, g:g + 1]
                du_scr[rs_, cs] = da[rs_, cs] * mixed
                dmx = da[rs_, cs] * u[rs_, cs]
                dw_ref[g] += jnp.where(keep, _dot(dmx, vg, NT), 0.0)
                dvn_scr[rs_, cs] = _dot(wt, dmx, TN)
                dbt_ref[:, g:g + 1] += jnp.sum(dmx, axis=1, keepdims=True)
        dvn = dvn_scr[...]
        dlg_ref[...] += jnp.sum(dvn * vhat, axis=0, keepdims=True)
        dlb_ref[...] += jnp.sum(dvn, axis=0, keepdims=True)
        dvh = dvn * lg_ref[...]
        dv = rs * (dvh - jnp.mean(dvh, axis=-1, keepdims=True) - vhat * jnp.mean(dvh * vhat, axis=-1, keepdims=True))
        gp = _gelu_grad(pv)
        dp_ref[:, :A_WIDTH] = (du_scr[...] * gp[:, :A_WIDTH]).astype(BF16)
        dp_ref[:, A_WIDTH:] = (dv * gp[:, A_WIDTH:]).astype(BF16)

    return pl.pallas_call(
        body, grid=(t // A_TILE,),
        in_specs=[pl.BlockSpec((A_TILE, 2 * A_WIDTH), lambda i: (i, 0)), pl.BlockSpec((A_TILE, A_WIDTH), lambda i: (i, 0)),
                  _const_spec((1, A_WIDTH)), _const_spec((1, A_WIDTH)), _const_spec((A_GROUPS, CHUNK, CHUNK)),
                  _const_spec((CHUNK, A_GROUPS))],
        out_specs=[pl.BlockSpec((A_TILE, 2 * A_WIDTH), lambda i: (i, 0)), _const_spec((A_GROUPS, CHUNK, CHUNK)),
                   _const_spec((CHUNK, A_GROUPS)), _const_spec((1, A_WIDTH)), _const_spec((1, A_WIDTH))],
        out_shape=[SDS((t, 2 * A_WIDTH), BF16), SDS((A_GROUPS, CHUNK, CHUNK), F32), SDS((CHUNK, A_GROUPS), F32),
                   SDS((1, A_WIDTH), F32), SDS((1, A_WIDTH), F32)],
        scratch_shapes=[pltpu.VMEM((A_TILE, A_WIDTH), F32), pltpu.VMEM((A_TILE, A_WIDTH), F32)],
        name="mixer_a_bwd", compiler_params=_params())(proj, dcat, lg, lb, w, bias_t)


def _rope_tables(t):
    half = ROPE_DIM // 2
    inv_freq = ROPE_THETA ** (-jnp.arange(half, dtype=F32) * 2.0 / ROPE_DIM)
    ang = jnp.arange(t, dtype=F32)[:, None] * inv_freq[None, :]
    cos, sin = jnp.cos(ang), jnp.sin(ang)
    one = jnp.ones((t, HEAD_DIM - ROPE_DIM), F32)
    zero = jnp.zeros((t, HEAD_DIM - ROPE_DIM), F32)
    zh = jnp.zeros((t, half), F32)
    c = jnp.concatenate([cos, cos, one], axis=1)
    s1 = jnp.concatenate([-sin, zh, zero], axis=1)
    s2 = jnp.concatenate([zh, sin, zero], axis=1)
    return tuple(jnp.tile(a, (1, LANES // HEAD_DIM)) for a in (c, s1, s2))


QK_TILE = 512
QK_ROWS = 64
QK_COLS = 2 * N_DIL * B_WIDTH


CHUNKS = B_WIDTH // LANES


def _regroup_out(scr, first, out_ref, rate, tile):
    rows = tile // rate
    for rho in range(rate):
        for c in range(CHUNKS):
            out_ref[rho, :, c * LANES:(c + 1) * LANES] = scr[first + c, pl.ds(rho, rows, stride=rate), :].astype(out_ref.dtype)


def _regroup_in(x_ref, scr, rate, tile):
    rows = tile // rate
    for rho in range(rate):
        for c in range(CHUNKS):
            scr[c, pl.ds(rho, rows, stride=rate), :] = x_ref[rho, :, c * LANES:(c + 1) * LANES].astype(F32)


def _regrouped_spec(rate, tile):
    return pl.BlockSpec((rate, tile // rate, B_WIDTH), lambda i, *_: (0, i, 0))


def _qk_fwd(proj, gains, tabs, ride=None):
    t = proj.shape[0]
    col0 = 2 * A_WIDTH // 1024
    r1, r2 = DIL_RATES[1], DIL_RATES[2]

    def body(p_ref, g_ref, c_ref, s1_ref, s2_ref, o_ref, q1_ref, q2_ref, k1_ref, k2_ref, scr):
        seg = _segment_mean_matrix(HEAD_DIM)
        for r0 in range(0, QK_TILE, QK_ROWS):
            rows = slice(r0, r0 + QK_ROWS)
            c, s1, s2 = c_ref[rows, :], s1_ref[rows, :], s2_ref[rows, :]
            for ci in range(1024 // LANES):
                ls = slice(ci * LANES, (ci + 1) * LANES)
                xv = p_ref[rows, ls].astype(F32)
                r = lax.rsqrt(_segment_dot(xv * xv, seg) + EPS)
                y = xv * r * g_ref[:, ls]
                val = y * c + pltpu.roll(y, LANES - 8, axis=1) * s1 + pltpu.roll(y, 8, axis=1) * s2
                o_ref[rows, ls] = val.astype(BF16)
                scr[ci, rows, :] = val

        j = pl.program_id(1)

        @pl.when(j == 0)
        def _():
            _regroup_out(scr, CHUNKS, q1_ref, r1, QK_TILE)

        @pl.when(j == 1)
        def _():
            _regroup_out(scr, 0, q2_ref, r2, QK_TILE)

        @pl.when(j == 2)
        def _():
            _regroup_out(scr, 0, k1_ref, r1, QK_TILE)
            _regroup_out(scr, CHUNKS, k2_ref, r2, QK_TILE)

    tab = pl.BlockSpec((QK_TILE, LANES), lambda i, j: (i, 0))
    g1, g2 = SDS((r1, t // r1, B_WIDTH), BF16), SDS((r2, t // r2, B_WIDTH), BF16)
    s1_, s2_ = _regrouped_spec(r1, QK_TILE), _regrouped_spec(r2, QK_TILE)
    return _call(
        body, grid=(t // QK_TILE, QK_COLS // 1024),
        in_specs=[pl.BlockSpec((QK_TILE, 1024), lambda i, j: (i, col0 + j)), pl.BlockSpec((1, 1024), lambda i, j: (0, j)),
                  tab, tab, tab],
        out_specs=[pl.BlockSpec((QK_TILE, 1024), lambda i, j: (i, j)), s1_, s2_, s1_, s2_],
        out_shape=[SDS((t, QK_COLS), BF16), g1, g2, g1, g2],
        scratch_shapes=[pltpu.VMEM((2 * CHUNKS, QK_TILE, LANES), F32)],
        operands=[proj, gains, *tabs], name="qk_norm_rope_fwd", ride=ride)


PERM_TILE = 512


def _permute(name, items, rate):
    t = items[0][0].shape[0]
    n = len(items)

    def body(*refs):
        scr = refs[-1]
        for x_ref, o_ref in zip(refs[:n], refs[n:2 * n]):
            for ci in range(CHUNKS):
                scr[ci] = x_ref[:, ci * LANES:(ci + 1) * LANES].astype(F32)
            _regroup_out(scr, 0, o_ref, rate, PERM_TILE)

    return pl.pallas_call(
        body, grid=(t // PERM_TILE,),
        in_specs=[pl.BlockSpec((PERM_TILE, B_WIDTH), functools.partial(lambda cb, i: (i, cb), cb)) for _, cb in items],
        out_specs=[_regrouped_spec(rate, PERM_TILE) for _ in items],
        out_shape=[SDS((rate, t // rate, B_WIDTH), a.dtype) for a, _ in items],
        scratch_shapes=[pltpu.VMEM((CHUNKS, PERM_TILE, LANES), F32)],
        name=name, compiler_params=_params())(*[a for a, _ in items])


def _head_lane_mask(h):
    lane = lax.broadcasted_iota(jnp.int32, (1, LANES), 1)
    return (lane < HEAD_DIM) if h == 0 else (lane >= HEAD_DIM)


def _attn_fwd(name, q, k, v, ride=None):
    rate, length = q[0].shape[0], q[0].shape[1]
    nb = length // ATT_BLOCK
    scale = HEAD_DIM ** -0.5

    def body(q_ref, kc_ref, kp_ref, vc_ref, vp_ref, o_ref, l_ref):
        n = pl.program_id(1)
        qi = lax.broadcasted_iota(jnp.int32, (ATT_BLOCK, 2 * ATT_BLOCK), 0)
        cj = lax.broadcasted_iota(jnp.int32, (ATT_BLOCK, 2 * ATT_BLOCK), 1)
        has_prev = jnp.where(n > 0, 0, 2 * ATT_BLOCK)
        mask = ((cj < ATT_BLOCK) & (cj >= qi + has_prev)) | ((cj >= ATT_BLOCK) & (cj - ATT_BLOCK <= qi))
        heads = [(hp, h) for hp in range(CHUNKS) for h in range(2)]
        q2, k2, v2 = {}, {}, {}
        for hp in range(CHUNKS):
            ls = slice(hp * LANES, (hp + 1) * LANES)
            q2[hp] = q_ref[:, ls]
            k2[hp] = jnp.concatenate([kp_ref[:, ls], kc_ref[:, ls]], axis=0)
            v2[hp] = jnp.concatenate([vp_ref[:, ls], vc_ref[:, ls]], axis=0)
        scores = {}
        for hp, h in heads:
            scores[hp, h] = _dot(jnp.where(_head_lane_mask(h), q2[hp], jnp.zeros_like(q2[hp])), k2[hp], NT) * scale
        probs, lses = {}, {}
        for hp, h in heads:
            s = jnp.where(mask, scores[hp, h], NEG_INF)
            m = jnp.max(s, axis=1, keepdims=True)
            p = jnp.exp(s - m)
            den = jnp.sum(p, axis=1, keepdims=True)
            lses[hp, h] = m + jnp.log(den)
            probs[hp, h] = (p / den).astype(BF16)
        for hp in range(CHUNKS):
            ls = slice(hp * LANES, (hp + 1) * LANES)
            o_acc = None
            for h in range(2):
                o = _dot(probs[hp, h], jnp.where(_head_lane_mask(h), v2[hp], jnp.zeros_like(v2[hp])), NN)
                o_acc = o if o_acc is None else o_acc + o
            o_ref[:, ls] = o_acc
            zeros = jnp.zeros((ATT_BLOCK, LANES), F32)
            l_ref[:, ls] = jnp.where(_head_lane_mask(1), lses[hp, 1] + zeros, lses[hp, 0] + zeros)

    def cur(cb):
        return pl.BlockSpec((None, ATT_BLOCK, B_WIDTH), lambda r, n: (r, n, cb))

    def prev(cb):
        return pl.BlockSpec((None, ATT_BLOCK, B_WIDTH), lambda r, n: (r, jnp.maximum(n - 1, 0), cb))

    out = pl.BlockSpec((None, ATT_BLOCK, B_WIDTH), lambda r, n: (r, n, 0))
    return _call(
        body, grid=(rate, nb),
        in_specs=[cur(q[1]), cur(k[1]), prev(k[1]), cur(v[1]), prev(v[1])],
        out_specs=[out, out], out_shape=[SDS((rate, length, B_WIDTH), F32)] * 2,
        operands=[q[0], k[0], k[0], v[0], v[0]], name=name, ride=ride)


def _attn_merge(a_out, o_list, l_list):
    t = a_out.shape[0]
    tm = PERM_TILE
    r1, r2 = DIL_RATES[1], DIL_RATES[2]

    def body(a_ref, o0, o1, o2, l0, l1, l2, cat_ref, lt_ref, lt1_ref, lt2_ref, so1, so2, sl1, sl2, slt):
        _regroup_in(o1, so1, r1, tm)
        _regroup_in(l1, sl1, r1, tm)
        _regroup_in(o2, so2, r2, tm)
        _regroup_in(l2, sl2, r2, tm)
        cat_ref[:, :A_WIDTH] = a_ref[...]
        for c in range(CHUNKS):
            ls = slice(c * LANES, (c + 1) * LANES)
            lg = [l0[:, ls], sl1[c], sl2[c]]
            m = jnp.maximum(jnp.maximum(lg[0], lg[1]), lg[2])
            es = [jnp.exp(l - m) for l in lg]
            den = es[0] + es[1] + es[2]
            b = (es[0] * o0[:, ls] + es[1] * so1[c] + es[2] * so2[c]) / den
            cat_ref[:, A_WIDTH + c * LANES:A_WIDTH + (c + 1) * LANES] = b.astype(BF16)
            lt = m + jnp.log(den)
            lt_ref[:, ls] = lt
            slt[c] = lt
        _regroup_out(slt, 0, lt1_ref, r1, tm)
        _regroup_out(slt, 0, lt2_ref, r2, tm)

    blk = _row_spec(tm, B_WIDTH)
    g1, g2 = _regrouped_spec(r1, tm), _regrouped_spec(r2, tm)
    return pl.pallas_call(
        body, grid=(t // tm,), in_specs=[blk, blk, g1, g2, blk, g1, g2],
        out_specs=[_row_spec(tm, A_WIDTH + B_WIDTH), blk, g1, g2],
        out_shape=[SDS((t, A_WIDTH + B_WIDTH), BF16), SDS((t, B_WIDTH), F32), SDS((r1, t // r1, B_WIDTH), F32),
                   SDS((r2, t // r2, B_WIDTH), F32)],
        scratch_shapes=[pltpu.VMEM((CHUNKS, tm, LANES), F32)] * 5,
        name="attn_merge", compiler_params=_params())(a_out, *o_list, *l_list)


def _attn_bwd_prep(dcat, cat):
    t = dcat.shape[0]
    tm = PERM_TILE
    r1, r2 = DIL_RATES[1], DIL_RATES[2]

    def body(d_ref, b_ref, db_ref, dd_ref, db1_ref, dd1_ref, db2_ref, dd2_ref, sdb, sdd):
        seg = _segment_mean_matrix(HEAD_DIM, scale=1.0)
        for c in range(CHUNKS):
            ls = slice(c * LANES, (c + 1) * LANES)
            d = d_ref[:, ls]
            dsum = _segment_dot(d * b_ref[:, ls].astype(F32), seg)
            db_ref[:, ls] = d.astype(BF16)
            dd_ref[:, ls] = dsum
            sdb[c] = d
            sdd[c] = dsum
        _regroup_out(sdb, 0, db1_ref, r1, tm)
        _regroup_out(sdd, 0, dd1_ref, r1, tm)
        _regroup_out(sdb, 0, db2_ref, r2, tm)
        _regroup_out(sdd, 0, dd2_ref, r2, tm)

    right = pl.BlockSpec((tm, B_WIDTH), lambda i: (i, 1))
    blk = _row_spec(tm, B_WIDTH)
    g1, g2 = _regrouped_spec(r1, tm), _regrouped_spec(r2, tm)
    return pl.pallas_call(
        body, grid=(t // tm,), in_specs=[right, right], out_specs=[blk, blk, g1, g1, g2, g2],
        out_shape=[SDS((t, B_WIDTH), BF16), SDS((t, B_WIDTH), F32), SDS((r1, t // r1, B_WIDTH), BF16),
                   SDS((r1, t // r1, B_WIDTH), F32), SDS((r2, t // r2, B_WIDTH), BF16), SDS((r2, t // r2, B_WIDTH), F32)],
        scratch_shapes=[pltpu.VMEM((CHUNKS, tm, LANES), F32)] * 2,
        name="attn_bwd_prep", compiler_params=_params())(dcat, cat)


def _attn_bwd(name, q, k, v, db, lse, dd, ride=None):
    rate, length = db.shape[0], db.shape[1]
    nb = length // ATT_BLOCK
    scale = HEAD_DIM ** -0.5

    def body(qa_ref, qb_ref, k_ref, v_ref, dba_ref, dbb_ref, la_ref, lb_ref, da_ref, dbd_ref, dq_ref, dk_ref, dv_ref, carry):
        m = pl.program_id(1)

        @pl.when(m == 0)
        def _():
            carry[...] = jnp.zeros_like(carry)

        row = lax.broadcasted_iota(jnp.int32, (2 * ATT_BLOCK, ATT_BLOCK), 0)
        kj = lax.broadcasted_iota(jnp.int32, (2 * ATT_BLOCK, ATT_BLOCK), 1)
        no_next = jnp.where(m + 1 < nb, 0, 2 * ATT_BLOCK)
        mask = ((row < ATT_BLOCK) & (kj <= row)) | ((row >= ATT_BLOCK) & (kj >= row - ATT_BLOCK + no_next))
        heads = [(hp, h) for hp in range(CHUNKS) for h in range(2)]
        q2, db2, lse2, dd2, k2, v2 = {}, {}, {}, {}, {}, {}
        for hp in range(CHUNKS):
            ls = slice(hp * LANES, (hp + 1) * LANES)
            k2[hp], v2[hp] = k_ref[:, ls], v_ref[:, ls]
            q2[hp] = jnp.concatenate([qa_ref[:, ls], qb_ref[:, ls]], axis=0)
            db2[hp] = jnp.concatenate([dba_ref[:, ls], dbb_ref[:, ls]], axis=0)
            lse2[hp] = jnp.concatenate([la_ref[:, ls], lb_ref[:, ls]], axis=0)
            dd2[hp] = jnp.concatenate([da_ref[:, ls], dbd_ref[:, ls]], axis=0)
        km, scores, dps = {}, {}, {}
        for hp, h in heads:
            hm = _head_lane_mask(h)
            km[hp, h] = jnp.where(hm, k2[hp], jnp.zeros_like(k2[hp]))
            scores[hp, h] = _dot(q2[hp], km[hp, h], NT) * scale
            dps[hp, h] = _dot(db2[hp], jnp.where(hm, v2[hp], jnp.zeros_like(v2[hp])), NT)
        probs, dss = {}, {}
        for hp, h in heads:
            hm = _head_lane_mask(h)
            lse_col = jnp.max(jnp.where(hm, lse2[hp], NEG_INF), axis=1, keepdims=True)
            dd_col = jnp.max(jnp.where(hm, dd2[hp], NEG_INF), axis=1, keepdims=True)
            p = jnp.where(mask, jnp.exp(scores[hp, h] - lse_col), 0.0)
            probs[hp, h] = p.astype(BF16)
            dss[hp, h] = (p * (dps[hp, h] - dd_col) * scale).astype(BF16)
        for hp in range(CHUNKS):
            ls = slice(hp * LANES, (hp + 1) * LANES)
            dq_acc, dk_acc, dv_acc = None, None, None
            for h in range(2):
                hm = _head_lane_mask(h)
                dvc = _dot(probs[hp, h], jnp.where(hm, db2[hp], jnp.zeros_like(db2[hp])), TN)
                dqc = _dot(dss[hp, h], km[hp, h], NN)
                dkc = _dot(dss[hp, h], jnp.where(hm, q2[hp], jnp.zeros_like(q2[hp])), TN)
                dq_acc = dqc if dq_acc is None else dq_acc + dqc
                dk_acc = dkc if dk_acc is None else dk_acc + dkc
                dv_acc = dvc if dv_acc is None else dv_acc + dvc
            dq_ref[:, ls] = (dq_acc[:ATT_BLOCK] + carry[:, ls]).astype(BF16)
            carry[:, ls] = dq_acc[ATT_BLOCK:]
            dk_ref[:, ls] = dk_acc.astype(BF16)
            dv_ref[:, ls] = dv_acc.astype(BF16)

    def cur(cb):
        return pl.BlockSpec((None, ATT_BLOCK, B_WIDTH), lambda r, n: (r, n, cb))

    def nxt(cb):
        return pl.BlockSpec((None, ATT_BLOCK, B_WIDTH), lambda r, n: (r, jnp.minimum(n + 1, nb - 1), cb))

    out = cur(0)
    return _call(
        body, grid=(rate, nb),
        in_specs=[cur(q[1]), nxt(q[1]), cur(k[1]), cur(v[1]), cur(0), nxt(0), cur(0), nxt(0), cur(0), nxt(0)],
        out_specs=[out, out, out], out_shape=[SDS((rate, length, B_WIDTH), BF16)] * 3,
        scratch_shapes=[pltpu.VMEM((ATT_BLOCK, B_WIDTH), F32)],
        operands=[q[0], q[0], k[0], v[0], db, db, lse, lse, dd, dd], name=name, ride=ride)


AB_IN = 2 * A_WIDTH + 3 * N_DIL * B_WIDTH
ASM_TILE = 256


def _dproj_assemble(proj, d_a, dq, dk, dv, gains, tabs, ride=None):
    t = proj.shape[0]
    n_in = 3 * N_DIL

    def body(p_ref, da_ref, *rest):
        grads = rest[:n_in]
        g_ref, c_ref, s1_ref, s2_ref, o_ref, dg_ref = rest[n_in:n_in + 6]
        scratch = rest[n_in + 6:]

        @pl.when(pl.program_id(0) == 0)
        def _():
            dg_ref[...] = jnp.zeros_like(dg_ref)

        chunk = {}
        k_scr = 0
        for j in range(n_in):
            g = j % N_DIL
            if DIL_RATES[g] == 1:
                for ci in range(CHUNKS):
                    chunk[j, ci] = functools.partial(lambda r, ci: r[:, ci * LANES:(ci + 1) * LANES].astype(F32), grads[j], ci)
            else:
                scr = scratch[k_scr]
                k_scr += 1
                _regroup_in(grads[j], scr, DIL_RATES[g], ASM_TILE)
                for ci in range(CHUNKS):
                    chunk[j, ci] = functools.partial(lambda s, ci: s[ci], scr, ci)

        seg = _segment_mean_matrix(HEAD_DIM)
        c, s1, s2 = c_ref[...], s1_ref[...], s2_ref[...]
        o_ref[:, :2 * A_WIDTH] = da_ref[...]
        for jg in range(2 * N_DIL):
            for ci in range(CHUNKS):
                col = jg * B_WIDTH + ci * LANES
                src = slice(2 * A_WIDTH + col, 2 * A_WIDTH + col + LANES)
                xv = p_ref[:, src].astype(F32)
                r = lax.rsqrt(_segment_dot(xv * xv, seg) + EPS)
                xh = xv * r
                gain = g_ref[:, col:col + LANES]
                do = chunk[jg, ci]()
                dy = do * c + pltpu.roll(do * s1, 8, axis=1) + pltpu.roll(do * s2, LANES - 8, axis=1)
                dg_ref[:, col:col + LANES] += jnp.sum(dy * xh, axis=0, keepdims=True)
                dxh = dy * gain
                o_ref[:, src] = (r * (dxh - xh * _segment_dot(dxh * xh, seg))).astype(BF16)
        v0 = 2 * A_WIDTH + QK_COLS
        for g in range(N_DIL):
            for ci in range(CHUNKS):
                col = v0 + g * B_WIDTH + ci * LANES
                o_ref[:, col:col + LANES] = chunk[2 * N_DIL + g, ci]().astype(BF16)

    specs = [_row_spec(ASM_TILE, B_WIDTH) if r == 1 else _regrouped_spec(r, ASM_TILE) for r in DIL_RATES] * 3
    n_scr = 3 * sum(1 for r in DIL_RATES if r > 1)
    tab = _row_spec(ASM_TILE, LANES)
    return _call(
        body, grid=(t // ASM_TILE,),
        in_specs=[_row_spec(ASM_TILE, AB_IN), _row_spec(ASM_TILE, 2 * A_WIDTH)] + specs
        + [_const_spec((1, QK_COLS)), tab, tab, tab],
        out_specs=[_row_spec(ASM_TILE, AB_IN), _const_spec((1, QK_COLS))],
        out_shape=[SDS((t, AB_IN), BF16), SDS((1, QK_COLS), F32)],
        scratch_shapes=[pltpu.VMEM((CHUNKS, ASM_TILE, LANES), F32)] * n_scr,
        operands=[proj, d_a, *dq, *dk, *dv, gains, *tabs], name="dproj_assemble", ride=ride)


def _fold_heads(dg_lane):
    n = dg_lane.shape[1]

    def body(x_ref, o_ref):
        r = lax.broadcasted_iota(jnp.int32, (B_WIDTH, B_WIDTH), 0) % HEAD_DIM
        c = lax.broadcasted_iota(jnp.int32, (B_WIDTH, B_WIDTH), 1) % HEAD_DIM
        fold = jnp.where(r == c, 1.0, 0.0).astype(F32)
        for jg in range(n // B_WIDTH):
            ls = slice(jg * B_WIDTH, (jg + 1) * B_WIDTH)
            o_ref[:, ls] = _dot_hi(jnp.broadcast_to(x_ref[:, ls], (8, B_WIDTH)), fold)

    return pl.pallas_call(body, out_shape=SDS((8, n), F32), name="fold_heads", compiler_params=_params())(dg_lane)


CD_TILE = 256
TAP_ROWS = 64
CD_IN = 2 * C_WIDTH + 3 * 512


def _shifted_copies(src, dst, rows):
    dst[0, :rows] = src[...]
    for b in range(1, 8):
        dst[b, :rows - 8] = src[pl.ds(b, rows - 8), :]


def _rows_from(shifted, start, n, lanes=slice(None)):
    b = start % 8
    return shifted[b, pl.ds(start - b, n), lanes]


def _mixer_cd_fwd(proj, cw, cb, lg, lb, dw):
    t = proj.shape[0]
    per = CD_TILE // HALO

    def body(h_ref, m_ref, cw_ref, cb_ref, lg_ref, lb_ref, dw_ref, o_ref, c1_ref, c_scr, e_scr, c_sh):
        not_first = (pl.program_id(0) > 0).astype(F32)
        lanes = [slice(c * LANES, (c + 1) * LANES) for c in range(C_WIDTH // LANES)]

        def col(ref, part, ls):
            return ref[:, part * C_WIDTH + ls.start:part * C_WIDTH + ls.stop].astype(F32)

        for ls in lanes:
            c_scr[:HALO, ls] = col(h_ref, 0, ls) * _sigmoid(col(h_ref, 1, ls)) * not_first
            c_scr[HALO:, ls] = col(m_ref, 0, ls) * _sigmoid(col(m_ref, 1, ls))
            e_scr[:HALO, ls] = col(h_ref, 3, ls) * col(h_ref, 4, ls) * not_first
            e_scr[HALO:, ls] = col(m_ref, 3, ls) * col(m_ref, 4, ls)
        _shifted_copies(c_scr, c_sh, HALO + CD_TILE)
        for ls in lanes:
            for r0 in range(0, CD_TILE, TAP_ROWS):
                acc = jnp.zeros((TAP_ROWS, LANES), F32)
                for k in range(C_KERNEL):
                    acc = acc + cw_ref[k:k + 1, ls] * _rows_from(c_sh, r0 + HALO - (C_KERNEL - 1) + k, TAP_ROWS, ls)
                c1_ref[r0:r0 + TAP_ROWS, ls] = acc + cb_ref[:, ls]
        mean = sum(jnp.sum(c1_ref[:, ls], axis=-1, keepdims=True) for ls in lanes) * (1.0 / C_WIDTH)
        var = sum(jnp.sum((c1_ref[:, ls] - mean) ** 2, axis=-1, keepdims=True) for ls in lanes) * (1.0 / C_WIDTH)
        rs = lax.rsqrt(var + EPS)
        for ls in lanes:
            c2 = (c1_ref[:, ls] - mean) * rs * lg_ref[:, ls] + lb_ref[:, ls]
            o_ref[:, ls] = (c2 * _sigmoid(c2)).astype(BF16)
            d1 = jnp.zeros((CD_TILE, LANES), F32)
            for k in range(D_KERNEL):
                d1 = d1 + dw_ref[k:k + 1, ls] * e_scr[pl.ds(HALO - (D_KERNEL - 1) + k, CD_TILE), ls]
            o_ref[:, C_WIDTH + ls.start:C_WIDTH + ls.stop] = (col(m_ref, 2, ls) * d1).astype(BF16)

    return pl.pallas_call(
        body, grid=(t // CD_TILE,),
        in_specs=[pl.BlockSpec((HALO, CD_IN), lambda i: (jnp.maximum(i * per - 1, 0), 0)), _row_spec(CD_TILE, CD_IN),
                  _const_spec((32, C_WIDTH)), _const_spec((1, C_WIDTH)), _const_spec((1, C_WIDTH)), _const_spec((1, C_WIDTH)),
                  _const_spec((8, C_WIDTH))],
        out_specs=[_row_spec(CD_TILE, 2 * C_WIDTH), _row_spec(CD_TILE, C_WIDTH)],
        out_shape=[SDS((t, 2 * C_WIDTH), BF16), SDS((t, C_WIDTH), F32)],
        scratch_shapes=[pltpu.VMEM((HALO + CD_TILE, C_WIDTH), F32)] * 2 + [pltpu.VMEM((8, HALO + CD_TILE, C_WIDTH), F32)],
        name="mixer_cd_fwd", compiler_params=_params())(proj, proj, cw, cb, lg, lb, dw)


def _mixer_cd_bwd(proj, dcat, c1, cw, lg, lb, dw, ride=None):
    t = proj.shape[0]
    per = CD_TILE // HALO
    nt = t // CD_TILE
    ext = CD_TILE + HALO

    def body(hp_ref, m_ref, hn_ref, dm_ref, dn_ref, c1m_ref, c1n_ref, cw_ref, lg_ref, lb_ref, dw_ref,
             dp_ref, dcw_ref, dcb_ref, dlg_ref, dlb_ref, ddw_ref, c_scr, e_scr, dc1_scr, dd1_scr, c_sh, dc1_sh, dcw_acc,
             dvh_scr, vhat_scr):
        i = pl.program_id(0)

        @pl.when(i == 0)
        def _():
            for r in (dcw_acc, dcb_ref, dlg_ref, dlb_ref, ddw_ref):
                r[...] = jnp.zeros_like(r)

        not_first = (i > 0).astype(F32)
        not_last = (i < nt - 1).astype(F32)
        main = slice(HALO, HALO + CD_TILE)
        lanes = [slice(c * LANES, (c + 1) * LANES) for c in range(C_WIDTH // LANES)]

        def col(ref, part, ls):
            return ref[:, part * C_WIDTH + ls.start:part * C_WIDTH + ls.stop].astype(F32)

        for ls in lanes:
            c_scr[:HALO, ls] = col(hp_ref, 0, ls) * _sigmoid(col(hp_ref, 1, ls)) * not_first
            c_scr[main, ls] = col(m_ref, 0, ls) * _sigmoid(col(m_ref, 1, ls))
            c_scr[HALO + CD_TILE:, ls] = col(hn_ref, 0, ls) * _sigmoid(col(hn_ref, 1, ls)) * not_last
            e_scr[:HALO, ls] = col(hp_ref, 3, ls) * col(hp_ref, 4, ls) * not_first
            e_scr[main, ls] = col(m_ref, 3, ls) * col(m_ref, 4, ls)
            e_scr[HALO + CD_TILE:, ls] = col(hn_ref, 3, ls) * col(hn_ref, 4, ls) * not_last
        _shifted_copies(c_scr, c_sh, 2 * HALO + CD_TILE)

        def c1_of(ls):
            return jnp.concatenate([c1m_ref[:, ls], c1n_ref[:, ls]], axis=0)

        mean = sum(jnp.sum(c1_of(ls), axis=-1, keepdims=True) for ls in lanes) * (1.0 / C_WIDTH)
        var = sum(jnp.sum((c1_of(ls) - mean) ** 2, axis=-1, keepdims=True) for ls in lanes) * (1.0 / C_WIDTH)
        rs = lax.rsqrt(var + EPS)
        sum_dvh, sum_dvh_vhat = 0.0, 0.0
        for ls in lanes:
            vhat = (c1_of(ls) - mean) * rs
            c2 = vhat * lg_ref[:, ls] + lb_ref[:, ls]
            sig = _sigmoid(c2)
            dc = jnp.concatenate([dm_ref[:, ls], dn_ref[:, ls] * not_last], axis=0)
            dc2 = dc * (sig * (1.0 + c2 * (1.0 - sig)))
            dvh = dc2 * lg_ref[:, ls]
            sum_dvh = sum_dvh + jnp.sum(dvh, axis=-1, keepdims=True)
            sum_dvh_vhat = sum_dvh_vhat + jnp.sum(dvh * vhat, axis=-1, keepdims=True)
            dvh_scr[:, ls] = dvh
            vhat_scr[:, ls] = vhat
            dlg_ref[:, ls] += jnp.sum((dc2 * vhat)[:CD_TILE], axis=0, keepdims=True)
            dlb_ref[:, ls] += jnp.sum(dc2[:CD_TILE], axis=0, keepdims=True)
        for ls in lanes:
            dc1 = rs * (dvh_scr[:, ls] - sum_dvh * (1.0 / C_WIDTH) - vhat_scr[:, ls] * (sum_dvh_vhat * (1.0 / C_WIDTH)))
            dc1_scr[:, ls] = dc1
            dcb_ref[:, ls] += jnp.sum(dc1[:CD_TILE], axis=0, keepdims=True)
        _shifted_copies(dc1_scr, dc1_sh, ext)
        for ls in lanes:
            for r0 in range(0, CD_TILE, TAP_ROWS):
                rows = slice(r0, r0 + TAP_ROWS)
                dc1_m = dc1_scr[rows, ls]
                dc0 = jnp.zeros((TAP_ROWS, LANES), F32)
                for k in range(C_KERNEL):
                    dc0 = dc0 + cw_ref[k:k + 1, ls] * _rows_from(dc1_sh, r0 + C_KERNEL - 1 - k, TAP_ROWS, ls)
                    prod = dc1_m * _rows_from(c_sh, r0 + HALO - (C_KERNEL - 1) + k, TAP_ROWS, ls)
                    dcw_acc[k, :, ls] += prod.reshape(TAP_ROWS // 8, 8, LANES).sum(axis=0)
                g_m = m_ref[rows, C_WIDTH + ls.start:C_WIDTH + ls.stop].astype(F32)
                a_m = m_ref[rows, ls].astype(F32)
                sig_m = _sigmoid(g_m)
                dp_ref[rows, ls] = (dc0 * sig_m).astype(BF16)
                dp_ref[rows, C_WIDTH + ls.start:C_WIDTH + ls.stop] = (dc0 * a_m * sig_m * (1.0 - sig_m)).astype(BF16)

        @pl.when(i == nt - 1)
        def _():
            dcw_ref[...] = jnp.sum(dcw_acc[...], axis=1)

        for ls in lanes:
            wide = slice(C_WIDTH + ls.start, C_WIDTH + ls.stop)
            d1 = jnp.zeros((CD_TILE, LANES), F32)
            for k in range(D_KERNEL):
                d1 = d1 + dw_ref[k:k + 1, ls] * e_scr[pl.ds(HALO - (D_KERNEL - 1) + k, CD_TILE), ls]
            dd_m = dm_ref[:, wide]
            dd1 = jnp.concatenate([dd_m * col(m_ref, 2, ls), dn_ref[:, wide] * col(hn_ref, 2, ls) * not_last], axis=0)
            dd1_scr[:, ls] = dd1
            dp_ref[:, 2 * C_WIDTH + ls.start:2 * C_WIDTH + ls.stop] = (dd_m * d1).astype(BF16)
            de = jnp.zeros((CD_TILE, LANES), F32)
            for k in range(D_KERNEL):
                de = de + dw_ref[k:k + 1, ls] * dd1_scr[pl.ds(D_KERNEL - 1 - k, CD_TILE), ls]
                ddw_ref[k:k + 1, ls] += jnp.sum(dd1[:CD_TILE] * e_scr[pl.ds(HALO - (D_KERNEL - 1) + k, CD_TILE), ls], axis=0, keepdims=True)
            dp_ref[:, 3 * C_WIDTH + ls.start:3 * C_WIDTH + ls.stop] = (de * col(m_ref, 4, ls)).astype(BF16)
            dp_ref[:, 4 * C_WIDTH + ls.start:4 * C_WIDTH + ls.stop] = (de * col(m_ref, 3, ls)).astype(BF16)

    halo_prev = lambda i: (jnp.maximum(i * per - 1, 0), 0)
    halo_next = lambda i: (jnp.minimum((i + 1) * per, t // HALO - 1), 0)
    vec = _const_spec((1, C_WIDTH))
    return _call(
        body, grid=(nt,),
        in_specs=[pl.BlockSpec((HALO, CD_IN), halo_prev), _row_spec(CD_TILE, CD_IN), pl.BlockSpec((HALO, CD_IN), halo_next),
                  _row_spec(CD_TILE, 2 * C_WIDTH), pl.BlockSpec((HALO, 2 * C_WIDTH), halo_next),
                  _row_spec(CD_TILE, C_WIDTH), pl.BlockSpec((HALO, C_WIDTH), halo_next),
                  _const_spec((32, C_WIDTH)), vec, vec, _const_spec((8, C_WIDTH))],
        out_specs=[_row_spec(CD_TILE, CD_IN), _const_spec((32, C_WIDTH)), vec, vec, vec, _const_spec((8, C_WIDTH))],
        out_shape=[SDS((t, CD_IN), BF16), SDS((32, C_WIDTH), F32), SDS((1, C_WIDTH), F32), SDS((1, C_WIDTH), F32),
                   SDS((1, C_WIDTH), F32), SDS((8, C_WIDTH), F32)],
        scratch_shapes=[pltpu.VMEM((2 * HALO + CD_TILE, C_WIDTH), F32)] * 2 + [pltpu.VMEM((ext, C_WIDTH), F32)] * 2
        + [pltpu.VMEM((8, 2 * HALO + CD_TILE, C_WIDTH), F32), pltpu.VMEM((8, ext, C_WIDTH), F32),
           pltpu.VMEM((32, 8, C_WIDTH), F32)] + [pltpu.VMEM((ext, C_WIDTH), F32)] * 2,
        operands=[proj, proj, proj, dcat, dcat, c1, c1, cw, lg, lb, dw], name="mixer_cd_bwd", ride=ride)


def _wgrad(name, pairs, out_rc, t, ride):
    tk = TILES["wgrad"]
    assert tk == t, "the whole contraction has to fit one grid step"
    r, c = out_rc
    n = len(pairs)

    def body(*refs):
        ab, out_refs = refs[:2 * n], refs[2 * n:]
        for j in range(n):
            out_refs[j][...] = _dot(ab[2 * j][...], ab[2 * j + 1][...], TN).astype(BF16)

    operands, in_specs = [], []
    for lhs, lhs_spec, rhs, rhs_spec in pairs:
        operands += [lhs, rhs]
        in_specs += [lhs_spec, rhs_spec]
    res = _call(body, grid=(N_CHIPS, t // tk), in_specs=in_specs,
                out_specs=[pl.BlockSpec((None, r, c), lambda p, k: (p, 0, 0))] * n,
                out_shape=[SDS((N_CHIPS, r, c), BF16)] * n, operands=operands, name=name, ride=ride)
    outs, ride_res = (res, None) if ride is None else res
    outs = [o.reshape(N_CHIPS, 2, r // 2, c) for o in outs]
    return outs if ride is None else (outs, ride_res)


def _wgrad_col_sharded(name, h, dz_list, three_d, ride=None):
    t, d = h.shape
    tk = TILES["wgrad"]
    n4 = dz_list[0].shape[-1] if three_d else dz_list[0].shape[-1] // N_CHIPS
    hs = pl.BlockSpec((tk, d), lambda p, k: (k, 0))
    zs = pl.BlockSpec((None, tk, n4), lambda p, k: (p, k, 0)) if three_d else pl.BlockSpec((tk, n4), lambda p, k: (k, p))
    return _wgrad(name, [(h, hs, dz, zs) for dz in dz_list], (d, n4), t, ride)


def _wgrad_row_sharded(name, a, g, three_d, ride=None):
    many = isinstance(a, (list, tuple))
    a_list = list(a) if many else [a]
    t, d = g.shape
    tk = TILES["wgrad"]
    k4 = a_list[0].shape[-1] if three_d else a_list[0].shape[-1] // N_CHIPS
    a_spec = pl.BlockSpec((None, tk, k4), lambda p, k: (p, k, 0)) if three_d else pl.BlockSpec((tk, k4), lambda p, k: (k, p))
    gs = pl.BlockSpec((tk, d), lambda p, k: (k, 0))
    res = _wgrad(name, [(a_j, a_spec, g, gs) for a_j in a_list], (k4, d), t, ride)
    if many:
        return res
    return res[0] if ride is None else (res[0][0], res[1])


def _mesh_scalars():
    return jnp.stack([lax.axis_index("c"), 2 * lax.axis_index("x") + lax.axis_index("y")]).astype(jnp.int32)


def _stage_own(name, w, layer, dtype):
    layers, r, cols = w.shape
    h = r // 2

    def body(s_ref, x_ref, o_ref):
        o_ref[...] = x_ref[...].astype(dtype)

    return pl.pallas_call(
        body,
        grid_spec=pltpu.PrefetchScalarGridSpec(
            num_scalar_prefetch=1, grid=(2,),
            in_specs=[pl.BlockSpec((None, h, cols), lambda i, s: (2 * layer + i, 0, 0))],
            out_specs=pl.BlockSpec((None, None, h, cols), lambda i, s: (s[1], i, 0, 0))),
        out_shape=SDS((N_CHIPS, 2, h, cols), dtype), name=name,
        compiler_params=_params())(_mesh_scalars(), w.reshape(2 * layers, h, cols))


STAGE_STEPS = 4


def _stage_rest_and_norm(x, g, weights, ride=None):
    t, d = x.shape
    n = len(weights)
    views, in_specs, out_specs, out_shapes = [], [], [], []
    for w, layer in weights:
        layers, r, cols = w.shape
        sub = r // STAGE_STEPS
        views.append(w.reshape(layers * STAGE_STEPS, sub, cols))
        in_specs.append(pl.BlockSpec((None, sub, cols), functools.partial(lambda l, i, s: (STAGE_STEPS * l + i, 0, 0), layer)))
        out_specs.append(pl.BlockSpec((None, None, sub, cols), lambda i, s: (s[1], i // 2, i % 2, 0)))
        out_shapes.append(SDS((N_CHIPS, 2, r // 2, cols), BF16))

    def body(s_ref, x_ref, g_ref, *rest):
        w_refs, h_ref, o_refs = rest[:n], rest[n], rest[n + 1:]
        h_ref[...] = _rms_rows(x_ref[...], g_ref[...]).astype(BF16)
        for w_ref, o_ref in zip(w_refs, o_refs):
            o_ref[...] = w_ref[...].astype(BF16)

    tm = t // STAGE_STEPS
    res = _call(
        body, grid=(STAGE_STEPS,), in_specs=[pl.BlockSpec((tm, d), lambda i, s: (i, 0)), pl.BlockSpec((1, d), lambda i, s: (0, 0))] + in_specs,
        out_specs=[pl.BlockSpec((tm, d), lambda i, s: (i, 0))] + out_specs, out_shape=[SDS((t, d), BF16)] + out_shapes,
        operands=[x, g] + views, name="stage_and_norm", ride=ride, prefetch=_mesh_scalars())
    outs, ride_res = (res, None) if ride is None else res
    result = (outs[0], list(outs[1:]))
    return result if ride is None else (result, ride_res)


def _remote(src, dst, send_sem, recv_sem, device):
    return pltpu.make_async_remote_copy(src, dst, send_sem, recv_sem, device_id=device, device_id_type=MESH)


def _ride_gather_send(bufs):
    n = len(bufs)

    def each(b, sems, act):
        send, recv = sems
        x, y, c, p, others = _position()
        for t in range(n):
            for j, (qx, qy) in enumerate(others):
                act(b[t].at[p, c], b[t].at[2 * qx + qy, c], send.at[t, j], recv.at[t, j], (qx, qy, c))

    def start(ins, b, new, sems):
        each(b, sems, lambda mine, landed, s, r, dev: _remote(mine, mine, s, r, dev).start())

    def finish(ins, b, new, sems):
        def act(mine, landed, s, r, dev):
            _remote(mine, mine, s, r, dev).wait_send()
            _remote(landed, landed, s, r, dev).wait_recv()
        each(b, sems, act)

    return _Ride([], bufs, [], [(n, 3), (n, 3)], start, finish, ["chips"])


def _ride_gather_pass(bufs):
    n = len(bufs)

    def each(b, sems, act):
        send, recv = sems
        x, y, c, p, others = _position()
        for t in range(n):
            for j, (qx, qy) in enumerate(others):
                act(b[t].at[2 * qx + qy, c], b[t].at[2 * qx + qy, 1 - c], send.at[t, j], recv.at[t, j], (x, y, 1 - c))

    def start(ins, b, new, sems):
        each(b, sems, lambda landed, passed, s, r, dev: _remote(landed, landed, s, r, dev).start())

    def finish(ins, b, new, sems):
        def act(landed, passed, s, r, dev):
            _remote(landed, landed, s, r, dev).wait_send()
            _remote(passed, passed, s, r, dev).wait_recv()
        each(b, sems, act)

    return _Ride([], bufs, [], [(n, 3), (n, 3)], start, finish, ["sibling"])


def _ride_gather(bufs):
    send, onward = _ride_gather_send(bufs), _ride_gather_pass(bufs)
    n_send = len(send.sem_shapes)

    def start(ins, b, new, sems):
        send.start(ins, b, new, sems[:n_send])

    def finish(ins, b, new, sems):
        send.finish(ins, b, new, sems[:n_send])
        onward.start(ins, b, new, sems[n_send:])
        onward.finish(ins, b, new, sems[n_send:])

    return _Ride([], bufs, [], send.sem_shapes + onward.sem_shapes, start, finish, ["sibling", "chips"])


def _ride_swap(tensors):
    n = len(tensors)

    def each(ins, new, sems, act):
        send, recv = sems
        x, y, c, _, _ = _position()
        for t in range(n):
            act(_remote(ins[t].at[:, 1 - c], new[t], send.at[t], recv.at[t], (x, y, 1 - c)))

    def start(ins, b, new, sems):
        each(ins, new, sems, lambda cp: cp.start())

    def finish(ins, b, new, sems):
        each(ins, new, sems, lambda cp: cp.wait())

    return _Ride(tensors, [], [SDS((s.shape[0],) + s.shape[2:], s.dtype) for s in tensors], [(n,), (n,)], start, finish,
                 ["sibling"])


def _ride_scatter(tensors, landing):
    n = len(tensors)

    def each(ins, b, sems, act):
        send, recv = sems
        x, y, c, p, others = _position()
        for t in range(n):
            for j, (qx, qy) in enumerate(others):
                q = 2 * qx + qy
                act(ins[t].at[q], b[t].at[p], b[t].at[q], send.at[t, j], recv.at[t, j], (qx, qy, c))

    def start(ins, b, new, sems):
        each(ins, b, sems, lambda src, dst, landed, s, r, dev: _remote(src, dst, s, r, dev).start())

    def finish(ins, b, new, sems):
        def act(src, dst, landed, s, r, dev):
            _remote(src, dst, s, r, dev).wait_send()
            _remote(landed, landed, s, r, dev).wait_recv()
        each(ins, b, sems, act)

    return _Ride(tensors, landing, [], [(n, 3), (n, 3)], start, finish, ["chips"])


def _ride_join(bufs):
    n = len(bufs)

    def each(b, sems, act):
        send, recv = sems
        x, y, c, _, _ = _position()
        for t in range(n):
            act(b[t].at[c], b[t].at[1 - c], send.at[t], recv.at[t], (x, y, 1 - c))

    def start(ins, b, new, sems):
        each(b, sems, lambda mine, theirs, s, r, dev: _remote(mine, mine, s, r, dev).start())

    def finish(ins, b, new, sems):
        def act(mine, theirs, s, r, dev):
            _remote(mine, mine, s, r, dev).wait_send()
            _remote(theirs, theirs, s, r, dev).wait_recv()
        each(b, sems, act)

    return _Ride([], bufs, [], [(n,), (n,)], start, finish, ["sibling"])


def _all_reduce_small(pack, ride=None):
    rows = pack.shape[0]
    n_dev = 2 * N_CHIPS
    n_rb = 0 if ride is None else len(ride.bufs)

    def body(x_ref, *rest):
        o_ref = rest[n_rb]
        r_bufs = rest[n_rb + 1:2 * n_rb + 1]
        land, send, recv = rest[2 * n_rb + 1:2 * n_rb + 4]
        r_sems = rest[2 * n_rb + 4:]
        if ride is not None:
            ride.start([], r_bufs, [], r_sems)
        x, y, c, p, _ = _position()
        me = 2 * p + c
        land[me] = x_ref[...]
        peers = [(dx, dy, dc) for dx in range(2) for dy in range(2) for dc in range(2) if (dx, dy, dc) != (0, 0, 0)]
        for j, (dx, dy, dc) in enumerate(peers):
            _remote(land.at[me], land.at[me], send.at[j], recv.at[j], (x ^ dx, y ^ dy, c ^ dc)).start()
        for j, (dx, dy, dc) in enumerate(peers):
            src = 4 * (x ^ dx) + 2 * (y ^ dy) + (c ^ dc)
            _remote(land.at[me], land.at[me], send.at[j], recv.at[j], (x ^ dx, y ^ dy, c ^ dc)).wait_send()
            _remote(land.at[src], land.at[src], send.at[j], recv.at[j], (x ^ dx, y ^ dy, c ^ dc)).wait_recv()
        acc = land[0]
        for dev in range(1, n_dev):
            acc = acc + land[dev]
        o_ref[...] = acc
        if ride is not None:
            ride.finish([], r_bufs, [], r_sems)

    bufs = [] if ride is None else ride.bufs
    sems = [] if ride is None else [pltpu.SemaphoreType.DMA(s) for s in ride.sem_shapes]
    res = pl.pallas_call(
        body, in_specs=[pl.BlockSpec(memory_space=pltpu.VMEM)] + [ANY] * n_rb,
        out_specs=[pl.BlockSpec(memory_space=pltpu.VMEM)] + [ANY] * n_rb,
        out_shape=[SDS((rows, LANES), F32)] + [SDS(b.shape, b.dtype) for b in bufs],
        scratch_shapes=[pltpu.VMEM((n_dev, rows, LANES), F32), pltpu.SemaphoreType.DMA((n_dev - 1,)),
                        pltpu.SemaphoreType.DMA((n_dev - 1,))] + sems,
        input_output_aliases={1 + j: 1 + j for j in range(n_rb)},
        name="all_reduce_small", compiler_params=_params())(pack, *bufs)
    return res[0], list(res[1:])


def _add_own_half(name, full, recv, out_dtype):
    n4, _, h, cols = full.shape

    def body(s_ref, a_ref, b_ref, o_ref, own_ref):
        v = (a_ref[...].astype(F32) + b_ref[...].astype(F32)).astype(out_dtype)
        o_ref[...] = v

        @pl.when(pl.program_id(0) == s_ref[1])
        def _():
            own_ref[...] = v

    return pl.pallas_call(
        body,
        grid_spec=pltpu.PrefetchScalarGridSpec(
            num_scalar_prefetch=1, grid=(n4,),
            in_specs=[pl.BlockSpec((None, None, h, cols), lambda q, s: (q, s[0], 0, 0)),
                      pl.BlockSpec((None, h, cols), lambda q, s: (q, 0, 0))],
            out_specs=[pl.BlockSpec((None, h, cols), lambda q, s: (q, 0, 0)),
                       pl.BlockSpec((None, h, cols), lambda q, s: (s[1], 0, 0))]),
        out_shape=[SDS((n4, h, cols), out_dtype)] * 2, name=name, compiler_params=_params())(_mesh_scalars(), full, recv)


def _sum_chips(name, parts):
    n4, h, cols = parts.shape
    th = h // 4 if h % 64 == 0 else h

    def body(s_ref, a_ref, o_ref):
        acc = a_ref[0].astype(F32)
        for q in range(1, n4):
            acc = acc + a_ref[q].astype(F32)
        o_ref[...] = acc

    return pl.pallas_call(
        body,
        grid_spec=pltpu.PrefetchScalarGridSpec(
            num_scalar_prefetch=1, grid=(h // th,),
            in_specs=[pl.BlockSpec((n4, th, cols), lambda i, s: (0, i, 0))],
            out_specs=pl.BlockSpec((None, th, cols), lambda i, s: (s[0], i, 0))),
        out_shape=SDS((2, h, cols), F32), name=name, compiler_params=_params())(_mesh_scalars(), parts)


def _adamw_math(w, g, m, v):
    m2 = ADAM_B1 * m + (1.0 - ADAM_B1) * g
    v2 = ADAM_B2 * v + (1.0 - ADAM_B2) * (g * g)
    m_hat = m2 / (1.0 - ADAM_B1 ** ADAM_STEP)
    v_hat = v2 / (1.0 - ADAM_B2 ** ADAM_STEP)
    delta = -ADAM_LR * (m_hat / (jnp.sqrt(v_hat) + ADAM_EPS) + ADAM_WD * w)
    return delta, m2, v2


def _row_tile(rows, cols):
    cap = max(8, (1 << 18) // cols)
    best = 8
    for cand in range(8, min(rows, cap) + 1, 8):
        if rows % cand == 0:
            best = cand
    return best


def _adamw_big(name, w, g_layers, m, v):
    layers, rows, cols = w.shape
    tr = _row_tile(rows, cols)

    def body(w_ref, m_ref, v_ref, *rest):
        g_refs, (g_o, d_o, m_o, v_o) = rest[:layers], rest[layers:]
        gv = g_refs[0][...]
        for layer in range(1, layers):
            gv = jnp.where(pl.program_id(0) == layer, g_refs[layer][...], gv)
        d, mm, vv = _adamw_math(w_ref[...], gv, m_ref[...], v_ref[...])
        g_o[...] = gv
        d_o[...] = d
        m_o[...] = mm
        v_o[...] = vv

    blk = pl.BlockSpec((None, tr, cols), lambda l, i: (l, i, 0))
    g_blk = pl.BlockSpec((tr, cols), lambda l, i: (i, 0))
    return tuple(pl.pallas_call(
        body, grid=(layers, rows // tr), in_specs=[blk] * 3 + [g_blk] * layers, out_specs=[blk] * 4,
        out_shape=[SDS((layers, rows, cols), F32)] * 4, name=name,
        compiler_params=_params())(w, m, v, *[g.reshape(rows, cols) for g in g_layers]))


def _adamw_small(ws, gs, ms, vs):
    n = len(ws)
    flat = []
    for group in (ws, gs, ms, vs):
        flat += [a.reshape(-1, a.shape[-1]) for a in group]

    def body(*refs):
        w_r, g_r, m_r, v_r = refs[:n], refs[n:2 * n], refs[2 * n:3 * n], refs[3 * n:4 * n]
        d_o, m_o, v_o = refs[4 * n:5 * n], refs[5 * n:6 * n], refs[6 * n:7 * n]
        for j in range(n):
            d, mm, vv = _adamw_math(w_r[j][...], g_r[j][...], m_r[j][...], v_r[j][...])
            d_o[j][...] = d
            m_o[j][...] = mm
            v_o[j][...] = vv

    shapes = [SDS(a.shape, F32) for a in flat[:n]]
    outs = pl.pallas_call(body, out_shape=shapes * 3, name="adamw_small", compiler_params=_params())(*flat)
    res = []
    for k in range(3):
        res.append([outs[k * n + j].reshape(ws[j].shape) for j in range(n)])
    return res


BIG = ("ab_w_in", "ab_w_out", "cd_w_in", "cd_w_out", "ffn_w_gate", "ffn_w_up", "ffn_w_down")
V_BLOCK = (2 * A_WIDTH + QK_COLS) // B_WIDTH


def _pad_rows(a, rows):
    return jnp.pad(a, ((0, rows - a.shape[0]), (0, 0)))


A_IN, A_OUT, C_IN, C_OUT = ("ab_w_in", 0), ("ab_w_out", 0), ("cd_w_in", 0), ("cd_w_out", 0)
G0, U0, D0 = ("ffn_w_gate", 0), ("ffn_w_up", 0), ("ffn_w_down", 0)
G1, U1, D1 = ("ffn_w_gate", 1), ("ffn_w_up", 1), ("ffn_w_down", 1)
UNITS = (A_IN, A_OUT, G0, U0, D0, C_IN, C_OUT, G1, U1, D1)
ROWS_MINOR = ("ffn_w_gate", "ffn_w_up")
SMALL_SHARDED = ("small", 0)
REPLICATED_UNIT = ("replicated", 0)


class _Exchange:
    def __init__(self, enabled):
        self.enabled = enabled
        self.w, self.grad, self.recv, self.half, self.land, self.done = {}, {}, {}, {}, {}, {}

    def full(self, unit):
        b = self.w[unit]
        return b.reshape(N_CHIPS, 1, 2 * b.shape[2], b.shape[3])

    def ride_for(self, phases):
        rides, sinks = [], []
        for kind, units in phases:
            if kind == "send":
                rides.append(_ride_gather_send([self.w[u] for u in units]))
                sinks.append(self.w)
            elif kind == "pass":
                rides.append(_ride_gather_pass([self.w[u] for u in units]))
                sinks.append(self.w)
            elif kind == "gather":
                rides.append(_ride_gather([self.w[u] for u in units]))
                sinks.append(self.w)
            elif kind == "swap":
                rides.append(_ride_swap([self.grad[u] for u in units]))
                sinks.append(self.recv)
            elif kind == "scatter":
                rides.append(_ride_scatter([self.half[u] for u in units], [self.land[u] for u in units]))
                sinks.append(self.land)
            else:
                rides.append(_ride_join([self.done[u] for u in units]))
                sinks.append(self.done)
        ride = functools.reduce(_ride_both, rides)

        def settle(res):
            n_bufs = sum(len(r.bufs) for r in rides)
            bufs, new = list(res[:n_bufs]), list(res[n_bufs:])
            for r, sink, (_, units) in zip(rides, sinks, phases):
                vals = [bufs.pop(0) for _ in r.bufs] + [new.pop(0) for _ in r.new_outs]
                for u, v in zip(units, vals):
                    sink[u] = v

        return ride, settle

    def run(self, fn, *args, phases=(), **kw):
        if not self.enabled or not phases:
            return fn(*args, **kw)
        ride, settle = self.ride_for(phases)
        out, res = fn(*args, ride=ride, **kw)
        settle(res)
        return out

    def alone(self, name, phases):
        if self.enabled:
            ride, settle = self.ride_for(phases)
            settle(_run_ride(name, ride))

    def pair_sum(self, units):
        if self.enabled:
            for u in units:
                dtype = F32 if u in (SMALL_SHARDED, REPLICATED_UNIT) else BF16
                self.half[u], self.land[u] = _add_own_half(f"pair_sum_{u[0]}_{u[1]}", self.grad[u], self.recv[u], dtype)

    def chip_sum(self, units):
        if self.enabled:
            for u in units:
                self.done[u] = _sum_chips(f"chip_sum_{u[0]}_{u[1]}", self.land[u])


def _local_step(x, target, ex, sp, h0=None):
    t, d = x.shape
    tabs = _rope_tables(t)
    gains = jnp.concatenate([jnp.tile(sp["q_norm_g"][g], HEAD_DIM // 8) for g in range(N_DIL)]
                            + [jnp.tile(sp["k_norm_g"][g], HEAD_DIM // 8) for g in range(N_DIL)]).reshape(1, QK_COLS)
    bias_t = sp["sgu_bias"].T
    cw = _pad_rows(sp["conv_c_w"], 32)
    dw = _pad_rows(sp["conv_d_w"], 8)
    cb, clg, clb = (sp[k].reshape(1, C_WIDTH) for k in ("conv_c_b", "c_ln_g", "c_ln_b"))
    slg, slb = sp["sgu_norm_g"].reshape(1, A_WIDTH), sp["sgu_norm_b"].reshape(1, A_WIDTH)
    g_ab, g_cd = sp["ab_norm_g"].reshape(1, d), sp["cd_norm_g"].reshape(1, d)
    g_f0, g_f1 = sp["ffn_norm_g"][0:1], sp["ffn_norm_g"][1:2]
    run = ex.run

    def w2d(unit):
        return ex.full(unit).reshape(-1, d)

    if h0 is None:
        h0 = _rms_fwd("rms_ab", x, g_ab)
    proj = run(_proj_in, "proj_ab", h0, ex.full(A_IN), 0, phases=[("send", [A_OUT, G0])])
    a_out = _mixer_a_fwd(proj, slg, slb, sp["sgu_w"], bias_t)
    qk, q1, q2, k1, k2 = run(_qk_fwd, proj, gains, tabs, phases=[("pass", [A_OUT, G0]), ("send", [U0])])
    regrouped_qk = {1: (q1, k1), 2: (q2, k2)}
    fwd_phases = ([("pass", [U0]), ("send", [D0])], [("pass", [D0]), ("send", [C_IN])],
                  [("pass", [C_IN]), ("send", [C_OUT])])
    qkv, o_list, l_list = [], [], []
    for g, rate in enumerate(DIL_RATES):
        if rate == 1:
            qk3, proj3 = qk.reshape(1, t, QK_COLS), proj.reshape(1, t, AB_IN)
            q, k, v = (qk3, g), (qk3, N_DIL + g), (proj3, V_BLOCK + g)
        else:
            vp, = _permute(f"regroup_v_{g}", [(proj, V_BLOCK + g)], rate)
            q, k, v = (regrouped_qk[g][0], 0), (regrouped_qk[g][1], 0), (vp, 0)
        qkv.append((q, k, v))
        o, l = run(_attn_fwd, f"attn_fwd_{g}", q, k, v, phases=fwd_phases[g])
        if rate == 1:
            o, l = o.reshape(t, B_WIDTH), l.reshape(t, B_WIDTH)
        o_list.append(o)
        l_list.append(l)
    cat, lse_tot, lse_1, lse_2 = _attn_merge(a_out, o_list, l_list)
    x1, hf0 = _proj_out("out_ab", cat, w2d(A_OUT), x, g_next=g_f0)
    fgate0, fup0, act0 = run(_ffn_in, "ffn_in_0", hf0, ex.full(G0), ex.full(U0), 0,
                           phases=[("pass", [C_OUT]), ("send", [G1, U1])])
    x2, h1 = run(_ffn_out, "ffn_out_0", act0, ex.full(D0), 0, x1, g_next=g_cd, phases=[("pass", [G1, U1]), ("send", [D1])])
    projcd = run(_proj_in, "proj_cd", h1, ex.full(C_IN), 0, phases=[("pass", [D1])])
    cat2, c1 = _mixer_cd_fwd(projcd, cw, cb, clg, clb, dw)
    x3, hf1 = _proj_out("out_cd", cat2, w2d(C_OUT), x2, g_next=g_f1)
    fgate1, fup1, act1 = _ffn_in("ffn_in_1", hf1, ex.full(G1), ex.full(U1), 0)
    dy, loss_acc, dy_b = _ffn_out("ffn_out_1", act1, ex.full(D1), 0, x3, target=target)
    loss = 0.5 * loss_acc[0, 0] / d

    late = [D1, G1, U1]
    dgate, dup = _ffn_dact("ffn_dact_1", dy_b, ex.full(D1), 0, fgate1, fup1)
    ex.grad[D1] = _wgrad_row_sharded("wgrad_down_1", act1, dy_b, True)
    ex.grad[G1] = _wgrad_row_sharded("wgrad_gate_1", dgate, hf1, True)
    ex.grad[U1] = _wgrad_row_sharded("wgrad_up_1", dup, hf1, True)
    g3, d_f1, g3_b = run(_dgrad_cols, "dgrad_ffn_1", [dgate, dup], [ex.full(G1), ex.full(U1)], 0, True, x3, g_f1, dy,
                         w_rows=True, phases=[("swap", late)])
    ex.pair_sum(late)

    dcat2 = _dgrad_rows("dgrad_out_cd", g3_b, w2d(C_OUT))
    ex.grad[C_OUT] = _wgrad_row_sharded("wgrad_out_cd", cat2, g3_b, False)
    dprojcd, d_cw, d_cb, d_clg, d_clb, d_dw = run(_mixer_cd_bwd, projcd, dcat2, c1, cw, clg, clb, dw, phases=[("scatter", late)])
    ex.chip_sum(late)
    ex.grad[C_IN] = _wgrad_col_sharded("wgrad_in_cd", h1, [dprojcd], False)[0]
    g2, d_cdn, g2_b = run(_dgrad_cols, "dgrad_in_cd", [dprojcd], [ex.full(C_IN)], 0, False, x2, g_cd, g3,
                          phases=[("join", late), ("swap", [C_OUT, C_IN])])
    ex.pair_sum([C_OUT, C_IN])

    dgate, dup = run(_ffn_dact, "ffn_dact_0", g2_b, ex.full(D0), 0, fgate0, fup0, phases=[("scatter", [C_OUT, C_IN])])
    ex.chip_sum([C_OUT, C_IN])
    ex.grad[D0] = _wgrad_row_sharded("wgrad_down_0", act0, g2_b, True)
    ex.grad[G0] = _wgrad_row_sharded("wgrad_gate_0", dgate, hf0, True)
    ex.grad[U0] = _wgrad_row_sharded("wgrad_up_0", dup, hf0, True)
    small = {"cd_norm_g": d_cdn, "conv_c_w": d_cw[:C_KERNEL], "conv_c_b": d_cb, "c_ln_g": d_clg, "c_ln_b": d_clb,
             "conv_d_w": d_dw[:D_KERNEL]}
    ex.grad[SMALL_SHARDED] = _split_full_small(small).reshape(N_CHIPS, 2, SHARDED_ROWS // 2, LANES)
    mid = [D0, G0, U0, SMALL_SHARDED]
    g1, d_f0, g1_b = run(_dgrad_cols, "dgrad_ffn_0", [dgate, dup], [ex.full(G0), ex.full(U0)], 0, True, x1, g_f0, g2,
                         w_rows=True, phases=[("join", [C_OUT, C_IN]), ("swap", mid)])
    ex.pair_sum(mid)

    dcat = _dgrad_rows("dgrad_out_ab", g1_b, w2d(A_OUT))
    ex.grad[A_OUT] = _wgrad_row_sharded("wgrad_out_ab", cat, g1_b, False)
    d_a, d_sw, d_sbt, d_slg, d_slb = _mixer_a_bwd(proj, dcat, slg, slb, sp["sgu_w"], bias_t)
    early = {"sgu_norm_g": d_slg, "sgu_norm_b": d_slb, "sgu_w": d_sw, "sgu_bias": d_sbt.T}
    ex.grad[REPLICATED_UNIT] = jnp.broadcast_to(
        _pack_replicated(early, REPLICATED_EARLY, REPLICATED_EARLY_ROWS).reshape(2, REPLICATED_EARLY_ROWS // 2, LANES),
        (N_CHIPS, 2, REPLICATED_EARLY_ROWS // 2, LANES))
    last = [A_OUT, REPLICATED_UNIT]
    dbb, dd, db_1, dd_1, db_2, dd_2 = _attn_bwd_prep(dcat, cat)
    regrouped_bwd = {1: (db_1, lse_1, dd_1), 2: (db_2, lse_2, dd_2)}
    bwd_phases = ([("scatter", [D0, SMALL_SHARDED])],
                  [("scatter", [G0]), ("join", [D0, SMALL_SHARDED]), ("swap", last)],
                  [("scatter", [U0]), ("join", [G0])])
    dqs, dks, dvs = [], [], []
    for g, rate in enumerate(DIL_RATES):
        q, k, v = qkv[g]
        if rate == 1:
            db3, l3, dd3 = (a.reshape(1, t, B_WIDTH) for a in (dbb, lse_tot, dd))
        else:
            db3, l3, dd3 = regrouped_bwd[g]
        if g == 1:
            ex.chip_sum([D0, SMALL_SHARDED])
        elif g == 2:
            ex.chip_sum([G0])
            ex.pair_sum(last)
        dq, dk, dv = run(_attn_bwd, f"attn_bwd_{g}", q, k, v, db3, l3, dd3, phases=bwd_phases[g])
        if rate == 1:
            dq, dk, dv = (a.reshape(t, B_WIDTH) for a in (dq, dk, dv))
        dqs.append(dq)
        dks.append(dk)
        dvs.append(dv)
    ex.chip_sum([U0])
    dproj, d_gains = run(_dproj_assemble, proj, d_a, dqs, dks, dvs, gains, tabs, phases=[("scatter", last), ("join", [U0])])
    ex.chip_sum(last)
    d_gains = _fold_heads(d_gains)[0].reshape(2, N_DIL, B_WIDTH)[:, :, :HEAD_DIM]
    ex.grad[A_IN] = _wgrad_col_sharded("wgrad_in_ab", h0, [dproj], False)[0]
    ex.alone("swap_last", [("swap", [A_IN])])
    ex.pair_sum([A_IN])
    gx, d_abn = run(_dgrad_cols, "dgrad_in_ab", [dproj], [ex.full(A_IN)], 0, False, x, g_ab, g1, bf16_copy=False,
                    phases=[("join", last), ("scatter", [A_IN])])
    ex.chip_sum([A_IN])

    small.update({
        "ab_norm_g": d_abn, "sgu_norm_g": d_slg, "sgu_norm_b": d_slb, "sgu_w": d_sw, "sgu_bias": d_sbt.T,
        "q_norm_g": d_gains[0], "k_norm_g": d_gains[1], "ffn_norm_g": jnp.concatenate([d_f0, d_f1], axis=0),
    })
    return loss, gx, small


SHARDED_SMALL = ("cd_norm_g", "conv_c_w", "conv_c_b", "c_ln_g", "c_ln_b", "conv_d_w")
SHARDED_ROWS = 48
REPLICATED_EARLY = ("sgu_norm_g", "sgu_norm_b", "sgu_w", "sgu_bias")
REPLICATED_EARLY_ROWS = 528
REPLICATED_LATE = ("ab_norm_g", "q_norm_g", "k_norm_g", "ffn_norm_g", "loss")
REPLICATED_LATE_ROWS = 32
REPLICATED_SMALL = REPLICATED_EARLY + REPLICATED_LATE[:-1]


def _pack_sharded(parts):
    rows = [parts[k].reshape(-1, LANES) for k in SHARDED_SMALL]
    return _pad_rows(jnp.concatenate(rows, axis=0), SHARDED_ROWS)


def _split_full_small(small):
    per_chip = []
    for q in range(N_CHIPS):
        parts = {}
        for k in SHARDED_SMALL:
            a = small[k]
            a = a.reshape(-1, a.shape[-1])
            n = a.shape[-1] // N_CHIPS
            parts[k] = a[:, q * n:(q + 1) * n]
        per_chip.append(_pack_sharded(parts))
    return jnp.stack(per_chip)


def _unpack_sharded(pack, shapes):
    out, r = {}, 0
    for k in SHARDED_SMALL:
        n = math.prod(shapes[k]) // LANES
        out[k] = pack[r:r + n].reshape(shapes[k])
        r += n
    return out


def _gathered_small(packs, shapes):
    per_chip = [_unpack_sharded(packs[q], shapes) for q in range(N_CHIPS)]
    return {k: jnp.concatenate([pc[k] for pc in per_chip], axis=-1) for k in SHARDED_SMALL}


def _pack_replicated(small, names, total_rows):
    rows = []
    for k in names:
        a = small[k].reshape(-1)
        a = jnp.pad(a, (0, (-a.shape[0]) % LANES))
        rows.append(a.reshape(-1, LANES))
    return _pad_rows(jnp.concatenate(rows, axis=0), total_rows)


def _unpack_replicated(pack, shapes, names):
    out, r = {}, 0
    for k in names:
        size = math.prod(shapes[k])
        n = -(-size // LANES)
        out[k] = pack[r:r + n].reshape(-1)[:size].reshape(shapes[k])
        r += n
    return out


WEIGHT_ORDER = ("ab_norm_g", "ab_w_in", "sgu_norm_g", "sgu_norm_b", "sgu_w", "sgu_bias", "q_norm_g", "k_norm_g", "ab_w_out",
                "cd_norm_g", "cd_w_in", "conv_c_w", "conv_c_b", "c_ln_g", "c_ln_b", "conv_d_w", "cd_w_out", "ffn_norm_g",
                "ffn_w_gate", "ffn_w_up", "ffn_w_down")


def kernel(x, ab_norm_g, ab_w_in, sgu_norm_g, sgu_norm_b, sgu_w, sgu_bias, q_norm_g, k_norm_g, ab_w_out, cd_norm_g, cd_w_in, conv_c_w, conv_c_b, c_ln_g, c_ln_b, conv_d_w, cd_w_out, ffn_norm_g, ffn_w_gate, ffn_w_up, ffn_w_down, loss_target, m_ab_norm_g, m_ab_w_in, m_sgu_norm_g, m_sgu_norm_b, m_sgu_w, m_sgu_bias, m_q_norm_g, m_k_norm_g, m_ab_w_out, m_cd_norm_g, m_cd_w_in, m_conv_c_w, m_conv_c_b, m_c_ln_g, m_c_ln_b, m_conv_d_w, m_cd_w_out, m_ffn_norm_g, m_ffn_w_gate, m_ffn_w_up, m_ffn_w_down, v_ab_norm_g, v_ab_w_in, v_sgu_norm_g, v_sgu_norm_b, v_sgu_w, v_sgu_bias, v_q_norm_g, v_k_norm_g, v_ab_w_out, v_cd_norm_g, v_cd_w_in, v_conv_c_w, v_conv_c_b, v_c_ln_g, v_c_ln_b, v_conv_d_w, v_cd_w_out, v_ffn_norm_g, v_ffn_w_gate, v_ffn_w_up, v_ffn_w_down):
    args = dict(locals())
    ws = {k: args[k] for k in WEIGHT_ORDER}
    ms = {k: args["m_" + k] for k in WEIGHT_ORDER}
    vs = {k: args["v_" + k] for k in WEIGHT_ORDER}
    small_names = [k for k in WEIGHT_ORDER if k not in BIG]
    t, d = x.shape[1:]

    for group in (ws, ms, vs):
        for k in ROWS_MINOR:
            group[k] = jnp.swapaxes(group[k], 1, 2)
    ex = _Exchange(enabled=True)
    ex.w[A_IN] = _stage_own("stage_ab_w_in", ws["ab_w_in"], 0, BF16)
    own_small = _pack_sharded({k: ws[k][0] for k in SHARDED_SMALL})
    ex.w[SMALL_SHARDED] = _stage_own("stage_small", own_small[None], 0, F32)
    rest = [u for u in UNITS if u != A_IN]
    x2 = x.reshape(t, d)
    h0, staged = ex.run(_stage_rest_and_norm, x2, ws["ab_norm_g"], [(ws[name], layer) for name, layer in rest],
                        phases=[("gather", [A_IN, SMALL_SHARDED])])
    ex.w.update(zip(rest, staged))
    sp = _gathered_small(ex.w[SMALL_SHARDED].reshape(N_CHIPS, SHARDED_ROWS, LANES), {k: ws[k].shape[1:] for k in SHARDED_SMALL})
    for k in REPLICATED_SMALL:
        sp[k] = ws[k] if k == "ffn_norm_g" else ws[k][0]

    loss, grad_x, g_small = _local_step(x2, loss_target.reshape(t, d), ex, sp, h0)

    shapes = {k: ws[k].shape for k in REPLICATED_SMALL}
    shapes["loss"] = (1,)
    g_small["loss"] = loss
    join_last, settle = ex.ride_for([("join", [A_IN])])
    late, joined = _all_reduce_small(_pack_replicated(g_small, REPLICATED_LATE, REPLICATED_LATE_ROWS), join_last)
    settle(joined)
    grad = _unpack_sharded(ex.done[SMALL_SHARDED].reshape(SHARDED_ROWS, LANES), {k: ws[k].shape for k in SHARDED_SMALL})
    grad.update(_unpack_replicated(ex.done[REPLICATED_UNIT].reshape(REPLICATED_EARLY_ROWS, LANES), shapes, REPLICATED_EARLY))
    grad.update(_unpack_replicated(late, shapes, REPLICATED_LATE))
    loss = grad.pop("loss")[0]

    delta, new_m, new_v = {}, {}, {}
    for k in BIG:
        g_layers = [ex.done[(k, layer)] for layer in range(ws[k].shape[0])]
        outs = _adamw_big("adamw_" + k, ws[k], g_layers, ms[k], vs[k])
        if k in ROWS_MINOR:
            outs = [jnp.swapaxes(o, 1, 2) for o in outs]
        grad[k], delta[k], new_m[k], new_v[k] = outs
    d_s, m_s, v_s = _adamw_small([ws[k] for k in small_names], [grad[k] for k in small_names],
                                 [ms[k] for k in small_names], [vs[k] for k in small_names])
    for j, k in enumerate(small_names):
        delta[k], new_m[k], new_v[k] = d_s[j], m_s[j], v_s[j]

    return (loss, grad_x[None], *[grad[k] for k in WEIGHT_ORDER], *[delta[k] for k in WEIGHT_ORDER],
            *[new_m[k] for k in WEIGHT_ORDER], *[new_v[k] for k in WEIGHT_ORDER])
```

```python
import functools
import math

import jax
import jax.numpy as jnp
from jax import lax
from jax.experimental import pallas as pl
from jax.experimental.pallas import tpu as pltpu

F32 = jnp.float32
BF16 = jnp.bfloat16
SDS = jax.ShapeDtypeStruct

N_CHIPS = 4
EPS = 1e-6
NEG_INF = -1e30
CHUNK = 128
A_GROUPS = 4
A_WIDTH = 512
N_DIL = 3
DIL_RATES = (1, 4, 16)
HEAD_DIM = 64
B_WIDTH = 512
ROPE_DIM = 16
ROPE_THETA = 500000.0
C_WIDTH = 512
C_KERNEL = 31
D_KERNEL = 3
HALO = 32
ATT_BLOCK = 128
LANES = 128

ADAM_LR = 0.001
ADAM_B1 = 0.9
ADAM_B2 = 0.999
ADAM_EPS = 1e-08
ADAM_WD = 0.01
ADAM_STEP = 10

VMEM_LIMIT = 56 * 1024 * 1024

NN = (((1,), (0,)), ((), ()))
NT = (((1,), (1,)), ((), ()))
TN = (((0,), (0,)), ((), ()))

TILES = {"proj_in": 2048, "proj_out": 1024, "ffn_in": 1024, "ffn_out": 1024, "ffn_dact": 512, "dgrad_cols": 512,
         "dgrad_rows": 1024, "wgrad": 4096}


def _params(sem=None, collective_id=None):
    return pltpu.CompilerParams(dimension_semantics=sem, vmem_limit_bytes=VMEM_LIMIT, collective_id=collective_id)


def _bf(v):
    return v if v.dtype == BF16 else v.astype(BF16)


def _dot(a, b, dims):
    return lax.dot_general(_bf(a), _bf(b), dims, preferred_element_type=F32)


def _dot_hi(a, b):
    return jnp.dot(a, b, precision=lax.Precision.HIGHEST, preferred_element_type=F32)


def _sigmoid(v):
    return 0.5 * jnp.tanh(0.5 * v) + 0.5


def _gelu(v):
    return 0.5 * v * (1.0 + lax.erf(v * (1.0 / math.sqrt(2.0))))


def _gelu_grad(v):
    cdf = 0.5 * (1.0 + lax.erf(v * (1.0 / math.sqrt(2.0))))
    return cdf + v * jnp.exp(-0.5 * v * v) * (1.0 / math.sqrt(2.0 * math.pi))


def _segment_mean_matrix(seg, scale=None):
    r = lax.broadcasted_iota(jnp.int32, (LANES, LANES), 0) // seg
    c = lax.broadcasted_iota(jnp.int32, (LANES, LANES), 1) // seg
    return jnp.where(r == c, (1.0 / seg) if scale is None else scale, 0.0).astype(BF16)


def _segment_dot(v, seg):
    hi = v.astype(BF16)
    lo = (v - hi.astype(F32)).astype(BF16)
    return jnp.dot(hi, seg, preferred_element_type=F32) + jnp.dot(lo, seg, preferred_element_type=F32)


MESH = pl.DeviceIdType.MESH
ANY = pl.BlockSpec(memory_space=pl.ANY)


def _position():
    x, y, c = lax.axis_index("x"), lax.axis_index("y"), lax.axis_index("c")
    others = [(1 - x, y), (x, 1 - y), (1 - x, 1 - y)]
    return x, y, c, 2 * x + y, others


class _Ride:
    def __init__(self, ins, bufs, new_outs, sem_shapes, start, finish, reach):
        self.ins, self.bufs, self.new_outs, self.sem_shapes = list(ins), list(bufs), list(new_outs), list(sem_shapes)
        self.start, self.finish = start, finish
        self.reach = frozenset(reach)

    def entry_barrier(self):
        x, y, c, _, others = _position()
        peers = ([(x, y, 1 - c)] if "sibling" in self.reach else []) + ([(qx, qy, c) for qx, qy in others] if "chips" in self.reach else [])
        barrier = pltpu.get_barrier_semaphore()
        for peer in peers:
            pl.semaphore_signal(barrier, inc=1, device_id=peer, device_id_type=MESH)
        pl.semaphore_wait(barrier, len(peers))

    @property
    def collective_id(self):
        return {frozenset(["sibling"]): 0, frozenset(["chips"]): 1, frozenset(["sibling", "chips"]): 2}[self.reach]


def _ride_both(a, b):
    na = (len(a.ins), len(a.bufs), len(a.new_outs), len(a.sem_shapes))

    def split(ins, bufs, new, sems):
        return ((ins[:na[0]], bufs[:na[1]], new[:na[2]], sems[:na[3]]), (ins[na[0]:], bufs[na[1]:], new[na[2]:], sems[na[3]:]))

    def start(*refs):
        ra, rb = split(*refs)
        a.start(*ra)
        b.start(*rb)

    def finish(*refs):
        ra, rb = split(*refs)
        a.finish(*ra)
        b.finish(*rb)

    return _Ride(a.ins + b.ins, a.bufs + b.bufs, a.new_outs + b.new_outs, a.sem_shapes + b.sem_shapes, start, finish,
                 a.reach | b.reach)


def _call(body, *, grid, in_specs, out_specs, out_shape, operands, name, scratch_shapes=(), aliases=None, ride=None,
          prefetch=None):
    off = 0 if prefetch is None else 1
    lead = [] if prefetch is None else [prefetch]

    params = _params(collective_id=None if ride is None else ride.collective_id)

    def launch(kernel_body, in_specs_, out_specs_, out_shape_, scratch_, aliases_, *args):
        if prefetch is None:
            return pl.pallas_call(kernel_body, grid=grid, in_specs=in_specs_, out_specs=out_specs_, out_shape=out_shape_,
                                  scratch_shapes=scratch_, input_output_aliases=aliases_, name=name,
                                  compiler_params=params)(*args)
        spec = pltpu.PrefetchScalarGridSpec(num_scalar_prefetch=1, grid=grid, in_specs=in_specs_, out_specs=out_specs_,
                                            scratch_shapes=scratch_)
        return pl.pallas_call(kernel_body, grid_spec=spec, out_shape=out_shape_, input_output_aliases=aliases_, name=name,
                              compiler_params=params)(*lead, *args)

    if ride is None:
        return launch(body, list(in_specs), out_specs, out_shape, list(scratch_shapes), dict(aliases or {}), *operands)
    multi = isinstance(out_shape, (list, tuple))
    out_shapes = list(out_shape) if multi else [out_shape]
    o_specs = list(out_specs) if multi else [out_specs]
    n_in, n_out, n_scr = off + len(operands), len(out_shapes), len(scratch_shapes)
    n_ri, n_rb, n_rn = len(ride.ins), len(ride.bufs), len(ride.new_outs)

    def carrying(*refs):
        k = n_in
        r_ins = refs[k:k + n_ri]
        k += n_ri + n_rb
        outs = refs[k:k + n_out]
        k += n_out
        r_bufs = refs[k:k + n_rb]
        k += n_rb
        r_new = refs[k:k + n_rn]
        k += n_rn
        scratch = refs[k:k + n_scr]
        sems = refs[k + n_scr:]
        first, last = None, None
        for axis, size in enumerate(grid):
            pid = pl.program_id(axis)
            first = (pid == 0) if first is None else first & (pid == 0)
            last = (pid == size - 1) if last is None else last & (pid == size - 1)

        @pl.when(first)
        def _():
            ride.entry_barrier()
            ride.start(r_ins, r_bufs, r_new, sems)

        body(*refs[:n_in], *outs, *scratch)

        @pl.when(last)
        def _():
            ride.finish(r_ins, r_bufs, r_new, sems)

    all_aliases = dict(aliases or {})
    for j in range(n_rb):
        all_aliases[n_in + n_ri + j] = n_out + j
    res = launch(
        carrying, list(in_specs) + [ANY] * (n_ri + n_rb), o_specs + [ANY] * (n_rb + n_rn),
        out_shapes + [SDS(b.shape, b.dtype) for b in ride.bufs] + ride.new_outs,
        list(scratch_shapes) + [pltpu.SemaphoreType.DMA(s) for s in ride.sem_shapes], all_aliases,
        *operands, *ride.ins, *ride.bufs)
    outs = res[:n_out]
    return (list(outs) if multi else outs[0]), list(res[n_out:])


def _run_ride(name, ride):
    n_ri, n_rb, n_rn = len(ride.ins), len(ride.bufs), len(ride.new_outs)

    def body(*refs):
        r_ins = refs[:n_ri]
        r_bufs = refs[n_ri + n_rb:n_ri + 2 * n_rb]
        r_new = refs[n_ri + 2 * n_rb:n_ri + 2 * n_rb + n_rn]
        sems = refs[n_ri + 2 * n_rb + n_rn:]
        ride.entry_barrier()
        ride.start(r_ins, r_bufs, r_new, sems)
        ride.finish(r_ins, r_bufs, r_new, sems)

    return list(pl.pallas_call(
        body, in_specs=[ANY] * (n_ri + n_rb), out_specs=[ANY] * (n_rb + n_rn),
        out_shape=[SDS(b.shape, b.dtype) for b in ride.bufs] + ride.new_outs,
        scratch_shapes=[pltpu.SemaphoreType.DMA(s) for s in ride.sem_shapes],
        input_output_aliases={n_ri + j: j for j in range(n_rb)}, name=name,
        compiler_params=pltpu.CompilerParams(collective_id=ride.collective_id))(*ride.ins, *ride.bufs))


def _whole(ref, p):
    return ref[...]


def _slab(ref, p):
    return ref[p]


def _matmul(name, grid, pairs, extras, outs, dims, epi, *, slabs=1, n_acc=1, ride=None):
    n_pairs, n_ex, n_out = len(pairs), len(extras), len(outs)

    def body(*refs):
        ab = refs[:2 * n_pairs]
        ex = refs[2 * n_pairs:2 * n_pairs + n_ex]
        out_refs = refs[2 * n_pairs + n_ex:2 * n_pairs + n_ex + n_out]
        pids = tuple(pl.program_id(a) for a in range(len(grid)))
        parts = [None] * n_acc
        for p in range(slabs):
            for j, (_, _, a_pick, _, _, b_pick, acc) in enumerate(pairs):
                d = _dot(a_pick(ab[2 * j], p), b_pick(ab[2 * j + 1], p), dims)
                parts[acc] = d if parts[acc] is None else parts[acc] + d
        epi(parts, ex, out_refs, pids)

    operands, in_specs = [], []
    for a, a_spec, _, b, b_spec, _, _ in pairs:
        operands += [a, b]
        in_specs += [a_spec, b_spec]
    for e, e_spec in extras:
        operands.append(e)
        in_specs.append(e_spec)
    return _call(body, grid=grid, in_specs=in_specs, out_specs=[o[1] for o in outs], out_shape=[o[0] for o in outs],
                 operands=operands, name=name, ride=ride)


def _rms_rows(v, g):
    r = lax.rsqrt(jnp.mean(v * v, axis=-1, keepdims=True) + EPS)
    return v * r * g


def _rms_fwd(name, x, g):
    t, d = x.shape
    tm = 512

    def body(x_ref, g_ref, o_ref):
        o_ref[...] = _rms_rows(x_ref[...], g_ref[...]).astype(BF16)

    return pl.pallas_call(
        body, grid=(t // tm,),
        in_specs=[pl.BlockSpec((tm, d), lambda i: (i, 0)), pl.BlockSpec((1, d), lambda i: (0, 0))],
        out_specs=pl.BlockSpec((tm, d), lambda i: (i, 0)), out_shape=SDS((t, d), BF16), name=name,
        compiler_params=_params())(x, g)


def _epi_residual_norm(accs, ex, outs, pids):
    x_new = accs[0] + ex[0][...]
    outs[0][...] = x_new
    outs[1][...] = _rms_rows(x_new, ex[1][...]).astype(BF16)


def _epi_residual_loss(accs, ex, outs, pids):
    y = accs[0] + ex[0][...]
    err = y - ex[1][...]
    dy = err * (1.0 / err.shape[-1])
    outs[0][...] = dy
    outs[2][...] = dy.astype(BF16)

    @pl.when(pids[0] == 0)
    def _():
        outs[1][...] = jnp.zeros_like(outs[1])

    outs[1][...] += jnp.sum(err * err)


def _epi_rms_bwd(accs, ex, outs, pids):
    dh = accs[0]
    xv, g, res = ex[0][...], ex[1][...], ex[2][...]
    r = lax.rsqrt(jnp.mean(xv * xv, axis=-1, keepdims=True) + EPS)
    xh = xv * r
    dy = dh * g
    dx = res + r * (dy - xh * jnp.mean(dy * xh, axis=-1, keepdims=True))
    outs[0][...] = dx
    if len(outs) > 2:
        outs[2][...] = dx.astype(BF16)

    @pl.when(pids[0] == 0)
    def _():
        outs[1][...] = jnp.zeros_like(outs[1])

    outs[1][...] += jnp.sum(dh * xh, axis=0, keepdims=True)


def _row_spec(tm, d):
    return pl.BlockSpec((tm, d), lambda i, *_: (i, 0))


def _const_spec(shape):
    nd = len(shape)
    return pl.BlockSpec(shape, lambda *_: (0,) * nd)


def _proj_in(name, h, w, layer, ride=None):
    t, d = h.shape
    n4 = w.shape[-1]
    tm = TILES["proj_in"]

    def epi(accs, ex, outs, pids):
        outs[0][...] = accs[0].astype(BF16)

    res = _matmul(
        name, (N_CHIPS, t // tm),
        [(h, pl.BlockSpec((tm, d), lambda p, i: (i, 0)), _whole,
          w, pl.BlockSpec((None, None, d, n4), lambda p, i: (p, layer, 0, 0)), _whole, 0)],
        [], [(SDS((t, N_CHIPS * n4), BF16), pl.BlockSpec((tm, n4), lambda p, i: (i, p)))],
        NN, epi, ride=ride)
    return res[0] if ride is None else (res[0][0], res[1])


def _proj_out(name, a, w, x, g_next=None, target=None):
    t, k = a.shape
    d = w.shape[-1]
    tm = TILES["proj_out"]
    if target is None:
        extras = [(x, _row_spec(tm, d)), (g_next, _const_spec((1, d)))]
        outs = [(SDS((t, d), F32), _row_spec(tm, d)), (SDS((t, d), BF16), _row_spec(tm, d))]
        epi = _epi_residual_norm
    else:
        extras = [(x, _row_spec(tm, d)), (target, _row_spec(tm, d))]
        outs = [(SDS((t, d), F32), _row_spec(tm, d)), (SDS((8, LANES), F32), _const_spec((8, LANES))),
                (SDS((t, d), BF16), _row_spec(tm, d))]
        epi = _epi_residual_loss
    return _matmul(name, (t // tm,), [(a, _row_spec(tm, k), _whole, w, _const_spec((k, d)), _whole, 0)], extras, outs, NN, epi)


def _ffn_in(name, h, wg, wu, layer, ride=None):
    t, d = h.shape
    n4 = wg.shape[-2]
    tm = TILES["ffn_in"]

    def epi(accs, ex, outs, pids):
        gate, up = accs
        s = _sigmoid(gate)
        silu = gate * s
        outs[0][...] = (up * (s + silu - silu * s)).astype(BF16)
        outs[1][...] = silu.astype(BF16)
        outs[2][...] = (silu * up).astype(BF16)

    w_spec = pl.BlockSpec((None, None, n4, d), lambda p, i: (p, layer, 0, 0))
    h_spec = pl.BlockSpec((tm, d), lambda p, i: (i, 0))
    o = (SDS((N_CHIPS, t, n4), BF16), pl.BlockSpec((None, tm, n4), lambda p, i: (p, i, 0)))
    return _matmul(name, (N_CHIPS, t // tm),
                   [(h, h_spec, _whole, wg, w_spec, _whole, 0), (h, h_spec, _whole, wu, w_spec, _whole, 1)], [],
                   [o, o, o], NT, epi, n_acc=2, ride=ride)


def _ffn_out(name, act, wd, layer, x, g_next=None, target=None, ride=None):
    _, t, n4 = act.shape
    d = wd.shape[-1]
    tm = TILES["ffn_out"]
    xs = _row_spec(tm, d)
    if target is None:
        extras = [(x, xs), (g_next, _const_spec((1, d)))]
        outs = [(SDS((t, d), F32), xs), (SDS((t, d), BF16), xs)]
        epi = _epi_residual_norm
    else:
        extras = [(x, xs), (target, xs)]
        outs = [(SDS((t, d), F32), xs), (SDS((8, LANES), F32), _const_spec((8, LANES))), (SDS((t, d), BF16), xs)]
        epi = _epi_residual_loss
    return _matmul(
        name, (t // tm,),
        [(act, pl.BlockSpec((N_CHIPS, tm, n4), lambda i: (0, i, 0)), _slab,
          wd, pl.BlockSpec((N_CHIPS, None, n4, d), lambda i: (0, layer, 0, 0)), _slab, 0)],
        extras, outs, NN, epi, slabs=N_CHIPS, ride=ride)


def _ffn_dact(name, g, wd, layer, gate, up, ride=None):
    t, d = g.shape
    n4 = wd.shape[-2]
    tm = TILES["ffn_dact"]

    def body(g_ref, w_ref, gate_ref, up_ref, dgate_ref, dup_ref):
        gv = g_ref[...]
        for p in range(N_CHIPS):
            dact = _dot(gv, w_ref[p], NT)
            dgate_ref[p] = (dact * gate_ref[p].astype(F32)).astype(BF16)
            dup_ref[p] = (dact * up_ref[p].astype(F32)).astype(BF16)

    blk = pl.BlockSpec((N_CHIPS, tm, n4), lambda i: (0, i, 0))
    return _call(
        body, grid=(t // tm,),
        in_specs=[_row_spec(tm, d), pl.BlockSpec((N_CHIPS, None, n4, d), lambda i: (0, layer, 0, 0)), blk, blk],
        out_specs=[blk, blk], out_shape=[SDS((N_CHIPS, t, n4), BF16)] * 2, operands=[g, wd, gate, up], name=name, ride=ride)


def _copy_epi(accs, ex, outs, pids):
    for a, o in zip(accs, outs):
        o[...] = a.astype(o.dtype)


def _dgrad_cols(name, dz_list, w_list, layer, three_d, x, g, res, bf16_copy=True, w_rows=False, ride=None):
    t, d = x.shape
    n4 = w_list[0].shape[-2 if w_rows else -1]
    tm = TILES["dgrad_cols"]
    if three_d:
        zs, z_pick = pl.BlockSpec((N_CHIPS, tm, n4), lambda i: (0, i, 0)), _slab
    else:
        zs, z_pick = _row_spec(tm, N_CHIPS * n4), (lambda ref, p: ref[:, p * n4:(p + 1) * n4])
    ws = pl.BlockSpec((N_CHIPS, None) + ((n4, d) if w_rows else (d, n4)), lambda i: (0, layer, 0, 0))
    xs = _row_spec(tm, d)
    return _matmul(
        name, (t // tm,), [(dz, zs, z_pick, w, ws, _slab, 0) for dz, w in zip(dz_list, w_list)],
        [(x, xs), (g, _const_spec((1, d))), (res, xs)],
        [(SDS((t, d), F32), xs), (SDS((1, d), F32), _const_spec((1, d)))] + ([(SDS((t, d), BF16), xs)] if bf16_copy else []),
        NN if w_rows else NT, _epi_rms_bwd, slabs=N_CHIPS, ride=ride)


def _dgrad_rows(name, g, w):
    t, d = g.shape
    k = w.shape[0]
    tm = TILES["dgrad_rows"]
    return _matmul(name, (t // tm,), [(g, _row_spec(tm, d), _whole, w, _const_spec((k, d)), _whole, 0)], [],
                   [(SDS((t, k), F32), _row_spec(tm, k))], NT, _copy_epi)[0]


A_TILE = 256


def _a_common(p_ref, lg_ref, lb_ref):
    pv = p_ref[...].astype(F32)
    a = _gelu(pv)
    u, v = a[:, :A_WIDTH], a[:, A_WIDTH:]
    vc = v - jnp.mean(v, axis=-1, keepdims=True)
    rs = lax.rsqrt(jnp.mean(vc * vc, axis=-1, keepdims=True) + EPS)
    vhat = vc * rs
    vn = vhat * lg_ref[...] + lb_ref[...]
    return pv, u, vhat, rs, vn.astype(BF16)


def _tril_weights(w_ref, g):
    r = lax.broadcasted_iota(jnp.int32, (CHUNK, CHUNK), 0)
    c = lax.broadcasted_iota(jnp.int32, (CHUNK, CHUNK), 1)
    return jnp.where(c <= r, w_ref[g], 0.0).astype(BF16), c <= r


def _mixer_a_fwd(proj, lg, lb, w, bias_t):
    t = proj.shape[0]

    def body(p_ref, lg_ref, lb_ref, w_ref, bt_ref, o_ref):
        _, u, _, _, vnb = _a_common(p_ref, lg_ref, lb_ref)
        for g in range(A_GROUPS):
            wt, _ = _tril_weights(w_ref, g)
            cs = slice(g * CHUNK, (g + 1) * CHUNK)
            for ch in range(A_TILE // CHUNK):
                rs_ = slice(ch * CHUNK, (ch + 1) * CHUNK)
                mixed = _dot(wt, vnb[rs_, cs], NN) + bt_ref[:, g:g + 1]
                o_ref[rs_, cs] = (u[rs_, cs] * mixed).astype(BF16)

    return pl.pallas_call(
        body, grid=(t // A_TILE,),
        in_specs=[pl.BlockSpec((A_TILE, 2 * A_WIDTH), lambda i: (i, 0)), _const_spec((1, A_WIDTH)),
                  _const_spec((1, A_WIDTH)), _const_spec((A_GROUPS, CHUNK, CHUNK)), _const_spec((CHUNK, A_GROUPS))],
        out_specs=pl.BlockSpec((A_TILE, A_WIDTH), lambda i: (i, 0)), out_shape=SDS((t, A_WIDTH), BF16),
        name="mixer_a_fwd", compiler_params=_params())(proj, lg, lb, w, bias_t)


def _mixer_a_bwd(proj, dcat, lg, lb, w, bias_t):
    t = proj.shape[0]

    def body(p_ref, da_ref, lg_ref, lb_ref, w_ref, bt_ref, dp_ref, dw_ref, dbt_ref, dlg_ref, dlb_ref, du_scr, dvn_scr):
        @pl.when(pl.program_id(0) == 0)
        def _():
            dw_ref[...] = jnp.zeros_like(dw_ref)
            dbt_ref[...] = jnp.zeros_like(dbt_ref)
            dlg_ref[...] = jnp.zeros_like(dlg_ref)
            dlb_ref[...] = jnp.zeros_like(dlb_ref)

        pv, u, vhat, rs, vnb = _a_common(p_ref, lg_ref, lb_ref)
        da = da_ref[...]
        for g in range(A_GROUPS):
            wt, keep = _tril_weights(w_ref, g)
            cs = slice(g * CHUNK, (g + 1) * CHUNK)
            for ch in range(A_TILE // CHUNK):
                rs_ = slice(ch * CHUNK, (ch + 1) * CHUNK)
                vg = vnb[rs_, cs]
                mixed = _dot(wt, vg, NN) + bt_ref[:, g:g + 1]
                du_scr[rs_, cs] = da[rs_, cs] * mixed
                dmx = da[rs_, cs] * u[rs_, cs]
                dw_ref[g] += jnp.where(keep, _dot(dmx, vg, NT), 0.0)
                dvn_scr[rs_, cs] = _dot(wt, dmx, TN)
                dbt_ref[:, g:g + 1] += jnp.sum(dmx, axis=1, keepdims=True)
        dvn = dvn_scr[...]
        dlg_ref[...] += jnp.sum(dvn * vhat, axis=0, keepdims=True)
        dlb_ref[...] += jnp.sum(dvn, axis=0, keepdims=True)
        dvh = dvn * lg_ref[...]
        dv = rs * (dvh - jnp.mean(dvh, axis=-1, keepdims=True) - vhat * jnp.mean(dvh * vhat, axis=-1, keepdims=True))
        gp = _gelu_grad(pv)
        dp_ref[:, :A_WIDTH] = (du_scr[...] * gp[:, :A_WIDTH]).astype(BF16)
        dp_ref[:, A_WIDTH:] = (dv * gp[:, A_WIDTH:]).astype(BF16)

    return pl.pallas_call(
        body, grid=(t // A_TILE,),
        in_specs=[pl.BlockSpec((A_TILE, 2 * A_WIDTH), lambda i: (i, 0)), pl.BlockSpec((A_TILE, A_WIDTH), lambda i: (i, 0)),
                  _const_spec((1, A_WIDTH)), _const_spec((1, A_WIDTH)), _const_spec((A_GROUPS, CHUNK, CHUNK)),
                  _const_spec((CHUNK, A_GROUPS))],
        out_specs=[pl.BlockSpec((A_TILE, 2 * A_WIDTH), lambda i: (i, 0)), _const_spec((A_GROUPS, CHUNK, CHUNK)),
                   _const_spec((CHUNK, A_GROUPS)), _const_spec((1, A_WIDTH)), _const_spec((1, A_WIDTH))],
        out_shape=[SDS((t, 2 * A_WIDTH), BF16), SDS((A_GROUPS, CHUNK, CHUNK), F32), SDS((CHUNK, A_GROUPS), F32),
                   SDS((1, A_WIDTH), F32), SDS((1, A_WIDTH), F32)],
        scratch_shapes=[pltpu.VMEM((A_TILE, A_WIDTH), F32), pltpu.VMEM((A_TILE, A_WIDTH), F32)],
        name="mixer_a_bwd", compiler_params=_params())(proj, dcat, lg, lb, w, bias_t)


def _rope_tables(t):
    half = ROPE_DIM // 2
    inv_freq = ROPE_THETA ** (-jnp.arange(half, dtype=F32) * 2.0 / ROPE_DIM)
    ang = jnp.arange(t, dtype=F32)[:, None] * inv_freq[None, :]
    cos, sin = jnp.cos(ang), jnp.sin(ang)
    one = jnp.ones((t, HEAD_DIM - ROPE_DIM), F32)
    zero = jnp.zeros((t, HEAD_DIM - ROPE_DIM), F32)
    zh = jnp.zeros((t, half), F32)
    c = jnp.concatenate([cos, cos, one], axis=1)
    s1 = jnp.concatenate([-sin, zh, zero], axis=1)
    s2 = jnp.concatenate([zh, sin, zero], axis=1)
    return tuple(jnp.tile(a, (1, LANES // HEAD_DIM)) for a in (c, s1, s2))


QK_TILE = 512
QK_ROWS = 64
QK_COLS = 2 * N_DIL * B_WIDTH


CHUNKS = B_WIDTH // LANES


def _regroup_out(scr, first, out_ref, rate, tile):
    rows = tile // rate
    for rho in range(rate):
        for c in range(CHUNKS):
            out_ref[rho, :, c * LANES:(c + 1) * LANES] = scr[first + c, pl.ds(rho, rows, stride=rate), :].astype(out_ref.dtype)


def _regroup_in(x_ref, scr, rate, tile):
    rows = tile // rate
    for rho in range(rate):
        for c in range(CHUNKS):
            scr[c, pl.ds(rho, rows, stride=rate), :] = x_ref[rho, :, c * LANES:(c + 1) * LANES].astype(F32)


def _regrouped_spec(rate, tile):
    return pl.BlockSpec((rate, tile // rate, B_WIDTH), lambda i, *_: (0, i, 0))


def _qk_fwd(proj, gains, tabs, ride=None):
    t = proj.shape[0]
    col0 = 2 * A_WIDTH // 1024
    r1, r2 = DIL_RATES[1], DIL_RATES[2]

    def body(p_ref, g_ref, c_ref, s1_ref, s2_ref, o_ref, q1_ref, q2_ref, k1_ref, k2_ref, scr):
        seg = _segment_mean_matrix(HEAD_DIM)
        for r0 in range(0, QK_TILE, QK_ROWS):
            rows = slice(r0, r0 + QK_ROWS)
            c, s1, s2 = c_ref[rows, :], s1_ref[rows, :], s2_ref[rows, :]
            for ci in range(1024 // LANES):
                ls = slice(ci * LANES, (ci + 1) * LANES)
                xv = p_ref[rows, ls].astype(F32)
                r = lax.rsqrt(_segment_dot(xv * xv, seg) + EPS)
                y = xv * r * g_ref[:, ls]
                val = y * c + pltpu.roll(y, LANES - 8, axis=1) * s1 + pltpu.roll(y, 8, axis=1) * s2
                o_ref[rows, ls] = val.astype(BF16)
                scr[ci, rows, :] = val

        j = pl.program_id(1)

        @pl.when(j == 0)
        def _():
            _regroup_out(scr, CHUNKS, q1_ref, r1, QK_TILE)

        @pl.when(j == 1)
        def _():
            _regroup_out(scr, 0, q2_ref, r2, QK_TILE)

        @pl.when(j == 2)
        def _():
            _regroup_out(scr, 0, k1_ref, r1, QK_TILE)
            _regroup_out(scr, CHUNKS, k2_ref, r2, QK_TILE)

    tab = pl.BlockSpec((QK_TILE, LANES), lambda i, j: (i, 0))
    g1, g2 = SDS((r1, t // r1, B_WIDTH), BF16), SDS((r2, t // r2, B_WIDTH), BF16)
    s1_, s2_ = _regrouped_spec(r1, QK_TILE), _regrouped_spec(r2, QK_TILE)
    return _call(
        body, grid=(t // QK_TILE, QK_COLS // 1024),
        in_specs=[pl.BlockSpec((QK_TILE, 1024), lambda i, j: (i, col0 + j)), pl.BlockSpec((1, 1024), lambda i, j: (0, j)),
                  tab, tab, tab],
        out_specs=[pl.BlockSpec((QK_TILE, 1024), lambda i, j: (i, j)), s1_, s2_, s1_, s2_],
        out_shape=[SDS((t, QK_COLS), BF16), g1, g2, g1, g2],
        scratch_shapes=[pltpu.VMEM((2 * CHUNKS, QK_TILE, LANES), F32)],
        operands=[proj, gains, *tabs], name="qk_norm_rope_fwd", ride=ride)


PERM_TILE = 512


def _permute(name, items, rate):
    t = items[0][0].shape[0]
    n = len(items)

    def body(*refs):
        scr = refs[-1]
        for x_ref, o_ref in zip(refs[:n], refs[n:2 * n]):
            for ci in range(CHUNKS):
                scr[ci] = x_ref[:, ci * LANES:(ci + 1) * LANES].astype(F32)
            _regroup_out(scr, 0, o_ref, rate, PERM_TILE)

    return pl.pallas_call(
        body, grid=(t // PERM_TILE,),
        in_specs=[pl.BlockSpec((PERM_TILE, B_WIDTH), functools.partial(lambda cb, i: (i, cb), cb)) for _, cb in items],
        out_specs=[_regrouped_spec(rate, PERM_TILE) for _ in items],
        out_shape=[SDS((rate, t // rate, B_WIDTH), a.dtype) for a, _ in items],
        scratch_shapes=[pltpu.VMEM((CHUNKS, PERM_TILE, LANES), F32)],
        name=name, compiler_params=_params())(*[a for a, _ in items])


def _head_lane_mask(h):
    lane = lax.broadcasted_iota(jnp.int32, (1, LANES), 1)
    return (lane < HEAD_DIM) if h == 0 else (lane >= HEAD_DIM)


def _attn_fwd(name, q, k, v, ride=None):
    rate, length = q[0].shape[0], q[0].shape[1]
    nb = length // ATT_BLOCK
    scale = HEAD_DIM ** -0.5

    def body(q_ref, kc_ref, kp_ref, vc_ref, vp_ref, o_ref, l_ref):
        n = pl.program_id(1)
        qi = lax.broadcasted_iota(jnp.int32, (ATT_BLOCK, 2 * ATT_BLOCK), 0)
        cj = lax.broadcasted_iota(jnp.int32, (ATT_BLOCK, 2 * ATT_BLOCK), 1)
        has_prev = jnp.where(n > 0, 0, 2 * ATT_BLOCK)
        mask = ((cj < ATT_BLOCK) & (cj >= qi + has_prev)) | ((cj >= ATT_BLOCK) & (cj - ATT_BLOCK <= qi))
        heads = [(hp, h) for hp in range(CHUNKS) for h in range(2)]
        q2, k2, v2 = {}, {}, {}
        for hp in range(CHUNKS):
            ls = slice(hp * LANES, (hp + 1) * LANES)
            q2[hp] = q_ref[:, ls]
            k2[hp] = jnp.concatenate([kp_ref[:, ls], kc_ref[:, ls]], axis=0)
            v2[hp] = jnp.concatenate([vp_ref[:, ls], vc_ref[:, ls]], axis=0)
        scores = {}
        for hp, h in heads:
            scores[hp, h] = _dot(jnp.where(_head_lane_mask(h), q2[hp], jnp.zeros_like(q2[hp])), k2[hp], NT) * scale
        probs, lses = {}, {}
        for hp, h in heads:
            s = jnp.where(mask, scores[hp, h], NEG_INF)
            m = jnp.max(s, axis=1, keepdims=True)
            p = jnp.exp(s - m)
            den = jnp.sum(p, axis=1, keepdims=True)
            lses[hp, h] = m + jnp.log(den)
            probs[hp, h] = (p / den).astype(BF16)
        for hp in range(CHUNKS):
            ls = slice(hp * LANES, (hp + 1) * LANES)
            o_acc = None
            for h in range(2):
                o = _dot(probs[hp, h], jnp.where(_head_lane_mask(h), v2[hp], jnp.zeros_like(v2[hp])), NN)
                o_acc = o if o_acc is None else o_acc + o
            o_ref[:, ls] = o_acc
            zeros = jnp.zeros((ATT_BLOCK, LANES), F32)
            l_ref[:, ls] = jnp.where(_head_lane_mask(1), lses[hp, 1] + zeros, lses[hp, 0] + zeros)

    def cur(cb):
        return pl.BlockSpec((None, ATT_BLOCK, B_WIDTH), lambda r, n: (r, n, cb))

    def prev(cb):
        return pl.BlockSpec((None, ATT_BLOCK, B_WIDTH), lambda r, n: (r, jnp.maximum(n - 1, 0), cb))

    out = pl.BlockSpec((None, ATT_BLOCK, B_WIDTH), lambda r, n: (r, n, 0))
    return _call(
        body, grid=(rate, nb),
        in_specs=[cur(q[1]), cur(k[1]), prev(k[1]), cur(v[1]), prev(v[1])],
        out_specs=[out, out], out_shape=[SDS((rate, length, B_WIDTH), F32)] * 2,
        operands=[q[0], k[0], k[0], v[0], v[0]], name=name, ride=ride)


def _attn_merge(a_out, o_list, l_list):
    t = a_out.shape[0]
    tm = PERM_TILE
    r1, r2 = DIL_RATES[1], DIL_RATES[2]

    def body(a_ref, o0, o1, o2, l0, l1, l2, cat_ref, lt_ref, lt1_ref, lt2_ref, so1, so2, sl1, sl2, slt):
        _regroup_in(o1, so1, r1, tm)
        _regroup_in(l1, sl1, r1, tm)
        _regroup_in(o2, so2, r2, tm)
        _regroup_in(l2, sl2, r2, tm)
        cat_ref[:, :A_WIDTH] = a_ref[...]
        for c in range(CHUNKS):
            ls = slice(c * LANES, (c + 1) * LANES)
            lg = [l0[:, ls], sl1[c], sl2[c]]
            m = jnp.maximum(jnp.maximum(lg[0], lg[1]), lg[2])
            es = [jnp.exp(l - m) for l in lg]
            den = es[0] + es[1] + es[2]
            b = (es[0] * o0[:, ls] + es[1] * so1[c] + es[2] * so2[c]) / den
            cat_ref[:, A_WIDTH + c * LANES:A_WIDTH + (c + 1) * LANES] = b.astype(BF16)
            lt = m + jnp.log(den)
            lt_ref[:, ls] = lt
            slt[c] = lt
        _regroup_out(slt, 0, lt1_ref, r1, tm)
        _regroup_out(slt, 0, lt2_ref, r2, tm)

    blk = _row_spec(tm, B_WIDTH)
    g1, g2 = _regrouped_spec(r1, tm), _regrouped_spec(r2, tm)
    return pl.pallas_call(
        body, grid=(t // tm,), in_specs=[blk, blk, g1, g2, blk, g1, g2],
        out_specs=[_row_spec(tm, A_WIDTH + B_WIDTH), blk, g1, g2],
        out_shape=[SDS((t, A_WIDTH + B_WIDTH), BF16), SDS((t, B_WIDTH), F32), SDS((r1, t // r1, B_WIDTH), F32),
                   SDS((r2, t // r2, B_WIDTH), F32)],
        scratch_shapes=[pltpu.VMEM((CHUNKS, tm, LANES), F32)] * 5,
        name="attn_merge", compiler_params=_params())(a_out, *o_list, *l_list)


def _attn_bwd_prep(dcat, cat):
    t = dcat.shape[0]
    tm = PERM_TILE
    r1, r2 = DIL_RATES[1], DIL_RATES[2]

    def body(d_ref, b_ref, db_ref, dd_ref, db1_ref, dd1_ref, db2_ref, dd2_ref, sdb, sdd):
        seg = _segment_mean_matrix(HEAD_DIM, scale=1.0)
        for c in range(CHUNKS):
            ls = slice(c * LANES, (c + 1) * LANES)
            d = d_ref[:, ls]
            dsum = _segment_dot(d * b_ref[:, ls].astype(F32), seg)
            db_ref[:, ls] = d.astype(BF16)
            dd_ref[:, ls] = dsum
            sdb[c] = d
            sdd[c] = dsum
        _regroup_out(sdb, 0, db1_ref, r1, tm)
        _regroup_out(sdd, 0, dd1_ref, r1, tm)
        _regroup_out(sdb, 0, db2_ref, r2, tm)
        _regroup_out(sdd, 0, dd2_ref, r2, tm)

    right = pl.BlockSpec((tm, B_WIDTH), lambda i: (i, 1))
    blk = _row_spec(tm, B_WIDTH)
    g1, g2 = _regrouped_spec(r1, tm), _regrouped_spec(r2, tm)
    return pl.pallas_call(
        body, grid=(t // tm,), in_specs=[right, right], out_specs=[blk, blk, g1, g1, g2, g2],
        out_shape=[SDS((t, B_WIDTH), BF16), SDS((t, B_WIDTH), F32), SDS((r1, t // r1, B_WIDTH), BF16),
                   SDS((r1, t // r1, B_WIDTH), F32), SDS((r2, t // r2, B_WIDTH), BF16), SDS((r2, t // r2, B_WIDTH), F32)],
        scratch_shapes=[pltpu.VMEM((CHUNKS, tm, LANES), F32)] * 2,
        name="attn_bwd_prep", compiler_params=_params())(dcat, cat)


def _attn_bwd(name, q, k, v, db, lse, dd, ride=None):
    rate, length = db.shape[0], db.shape[1]
    nb = length // ATT_BLOCK
    scale = HEAD_DIM ** -0.5

    def body(qa_ref, qb_ref, k_ref, v_ref, dba_ref, dbb_ref, la_ref, lb_ref, da_ref, dbd_ref, dq_ref, dk_ref, dv_ref, carry):
        m = pl.program_id(1)

        @pl.when(m == 0)
        def _():
            carry[...] = jnp.zeros_like(carry)

        row = lax.broadcasted_iota(jnp.int32, (2 * ATT_BLOCK, ATT_BLOCK), 0)
        kj = lax.broadcasted_iota(jnp.int32, (2 * ATT_BLOCK, ATT_BLOCK), 1)
        no_next = jnp.where(m + 1 < nb, 0, 2 * ATT_BLOCK)
        mask = ((row < ATT_BLOCK) & (kj <= row)) | ((row >= ATT_BLOCK) & (kj >= row - ATT_BLOCK + no_next))
        heads = [(hp, h) for hp in range(CHUNKS) for h in range(2)]
        q2, db2, lse2, dd2, k2, v2 = {}, {}, {}, {}, {}, {}
        for hp in range(CHUNKS):
            ls = slice(hp * LANES, (hp + 1) * LANES)
            k2[hp], v2[hp] = k_ref[:, ls], v_ref[:, ls]
            q2[hp] = jnp.concatenate([qa_ref[:, ls], qb_ref[:, ls]], axis=0)
            db2[hp] = jnp.concatenate([dba_ref[:, ls], dbb_ref[:, ls]], axis=0)
            lse2[hp] = jnp.concatenate([la_ref[:, ls], lb_ref[:, ls]], axis=0)
            dd2[hp] = jnp.concatenate([da_ref[:, ls], dbd_ref[:, ls]], axis=0)
        km, scores, dps = {}, {}, {}
        for hp, h in heads:
            hm = _head_lane_mask(h)
            km[hp, h] = jnp.where(hm, k2[hp], jnp.zeros_like(k2[hp]))
            scores[hp, h] = _dot(q2[hp], km[hp, h], NT) * scale
            dps[hp, h] = _dot(db2[hp], jnp.where(hm, v2[hp], jnp.zeros_like(v2[hp])), NT)
        probs, dss = {}, {}
        for hp, h in heads:
            hm = _head_lane_mask(h)
            lse_col = jnp.max(jnp.where(hm, lse2[hp], NEG_INF), axis=1, keepdims=True)
            dd_col = jnp.max(jnp.where(hm, dd2[hp], NEG_INF), axis=1, keepdims=True)
            p = jnp.where(mask, jnp.exp(scores[hp, h] - lse_col), 0.0)
            probs[hp, h] = p.astype(BF16)
            dss[hp, h] = (p * (dps[hp, h] - dd_col) * scale).astype(BF16)
        for hp in range(CHUNKS):
            ls = slice(hp * LANES, (hp + 1) * LANES)
            dq_acc, dk_acc, dv_acc = None, None, None
            for h in range(2):
                hm = _head_lane_mask(h)
                dvc = _dot(probs[hp, h], jnp.where(hm, db2[hp], jnp.zeros_like(db2[hp])), TN)
                dqc = _dot(dss[hp, h], km[hp, h], NN)
                dkc = _dot(dss[hp, h], jnp.where(hm, q2[hp], jnp.zeros_like(q2[hp])), TN)
                dq_acc = dqc if dq_acc is None else dq_acc + dqc
                dk_acc = dkc if dk_acc is None else dk_acc + dkc
                dv_acc = dvc if dv_acc is None else dv_acc + dvc
            dq_ref[:, ls] = (dq_acc[:ATT_BLOCK] + carry[:, ls]).astype(BF16)
            carry[:, ls] = dq_acc[ATT_BLOCK:]
            dk_ref[:, ls] = dk_acc.astype(BF16)
            dv_ref[:, ls] = dv_acc.astype(BF16)

    def cur(cb):
        return pl.BlockSpec((None, ATT_BLOCK, B_WIDTH), lambda r, n: (r, n, cb))

    def nxt(cb):
        return pl.BlockSpec((None, ATT_BLOCK, B_WIDTH), lambda r, n: (r, jnp.minimum(n + 1, nb - 1), cb))

    out = cur(0)
    return _call(
        body, grid=(rate, nb),
        in_specs=[cur(q[1]), nxt(q[1]), cur(k[1]), cur(v[1]), cur(0), nxt(0), cur(0), nxt(0), cur(0), nxt(0)],
        out_specs=[out, out, out], out_shape=[SDS((rate, length, B_WIDTH), BF16)] * 3,
        scratch_shapes=[pltpu.VMEM((ATT_BLOCK, B_WIDTH), F32)],
        operands=[q[0], q[0], k[0], v[0], db, db, lse, lse, dd, dd], name=name, ride=ride)


AB_IN = 2 * A_WIDTH + 3 * N_DIL * B_WIDTH
ASM_TILE = 256


def _dproj_assemble(proj, d_a, dq, dk, dv, gains, tabs, ride=None):
    t = proj.shape[0]
    n_in = 3 * N_DIL

    def body(p_ref, da_ref, *rest):
        grads = rest[:n_in]
        g_ref, c_ref, s1_ref, s2_ref, o_ref, dg_ref = rest[n_in:n_in + 6]
        scratch = rest[n_in + 6:]

        @pl.when(pl.program_id(0) == 0)
        def _():
            dg_ref[...] = jnp.zeros_like(dg_ref)

        chunk = {}
        k_scr = 0
        for j in range(n_in):
            g = j % N_DIL
            if DIL_RATES[g] == 1:
                for ci in range(CHUNKS):
                    chunk[j, ci] = functools.partial(lambda r, ci: r[:, ci * LANES:(ci + 1) * LANES].astype(F32), grads[j], ci)
            else:
                scr = scratch[k_scr]
                k_scr += 1
                _regroup_in(grads[j], scr, DIL_RATES[g], ASM_TILE)
                for ci in range(CHUNKS):
                    chunk[j, ci] = functools.partial(lambda s, ci: s[ci], scr, ci)

        seg = _segment_mean_matrix(HEAD_DIM)
        c, s1, s2 = c_ref[...], s1_ref[...], s2_ref[...]
        o_ref[:, :2 * A_WIDTH] = da_ref[...]
        for jg in range(2 * N_DIL):
            for ci in range(CHUNKS):
                col = jg * B_WIDTH + ci * LANES
                src = slice(2 * A_WIDTH + col, 2 * A_WIDTH + col + LANES)
                xv = p_ref[:, src].astype(F32)
                r = lax.rsqrt(_segment_dot(xv * xv, seg) + EPS)
                xh = xv * r
                gain = g_ref[:, col:col + LANES]
                do = chunk[jg, ci]()
                dy = do * c + pltpu.roll(do * s1, 8, axis=1) + pltpu.roll(do * s2, LANES - 8, axis=1)
                dg_ref[:, col:col + LANES] += jnp.sum(dy * xh, axis=0, keepdims=True)
                dxh = dy * gain
                o_ref[:, src] = (r * (dxh - xh * _segment_dot(dxh * xh, seg))).astype(BF16)
        v0 = 2 * A_WIDTH + QK_COLS
        for g in range(N_DIL):
            for ci in range(CHUNKS):
                col = v0 + g * B_WIDTH + ci * LANES
                o_ref[:, col:col + LANES] = chunk[2 * N_DIL + g, ci]().astype(BF16)

    specs = [_row_spec(ASM_TILE, B_WIDTH) if r == 1 else _regrouped_spec(r, ASM_TILE) for r in DIL_RATES] * 3
    n_scr = 3 * sum(1 for r in DIL_RATES if r > 1)
    tab = _row_spec(ASM_TILE, LANES)
    return _call(
        body, grid=(t // ASM_TILE,),
        in_specs=[_row_spec(ASM_TILE, AB_IN), _row_spec(ASM_TILE, 2 * A_WIDTH)] + specs
        + [_const_spec((1, QK_COLS)), tab, tab, tab],
        out_specs=[_row_spec(ASM_TILE, AB_IN), _const_spec((1, QK_COLS))],
        out_shape=[SDS((t, AB_IN), BF16), SDS((1, QK_COLS), F32)],
        scratch_shapes=[pltpu.VMEM((CHUNKS, ASM_TILE, LANES), F32)] * n_scr,
        operands=[proj, d_a, *dq, *dk, *dv, gains, *tabs], name="dproj_assemble", ride=ride)


def _fold_heads(dg_lane):
    n = dg_lane.shape[1]

    def body(x_ref, o_ref):
        r = lax.broadcasted_iota(jnp.int32, (B_WIDTH, B_WIDTH), 0) % HEAD_DIM
        c = lax.broadcasted_iota(jnp.int32, (B_WIDTH, B_WIDTH), 1) % HEAD_DIM
        fold = jnp.where(r == c, 1.0, 0.0).astype(F32)
        for jg in range(n // B_WIDTH):
            ls = slice(jg * B_WIDTH, (jg + 1) * B_WIDTH)
            o_ref[:, ls] = _dot_hi(jnp.broadcast_to(x_ref[:, ls], (8, B_WIDTH)), fold)

    return pl.pallas_call(body, out_shape=SDS((8, n), F32), name="fold_heads", compiler_params=_params())(dg_lane)


CD_TILE = 256
TAP_ROWS = 64
CD_IN = 2 * C_WIDTH + 3 * 512


def _shifted_copies(src, dst, rows):
    dst[0, :rows] = src[...]
    for b in range(1, 8):
        dst[b, :rows - 8] = src[pl.ds(b, rows - 8), :]


def _rows_from(shifted, start, n, lanes=slice(None)):
    b = start % 8
    return shifted[b, pl.ds(start - b, n), lanes]


def _mixer_cd_fwd(proj, cw, cb, lg, lb, dw):
    t = proj.shape[0]
    per = CD_TILE // HALO

    def body(h_ref, m_ref, cw_ref, cb_ref, lg_ref, lb_ref, dw_ref, o_ref, c1_ref, c_scr, e_scr, c_sh):
        not_first = (pl.program_id(0) > 0).astype(F32)
        lanes = [slice(c * LANES, (c + 1) * LANES) for c in range(C_WIDTH // LANES)]

        def col(ref, part, ls):
            return ref[:, part * C_WIDTH + ls.start:part * C_WIDTH + ls.stop].astype(F32)

        for ls in lanes:
            c_scr[:HALO, ls] = col(h_ref, 0, ls) * _sigmoid(col(h_ref, 1, ls)) * not_first
            c_scr[HALO:, ls] = col(m_ref, 0, ls) * _sigmoid(col(m_ref, 1, ls))
            e_scr[:HALO, ls] = col(h_ref, 3, ls) * col(h_ref, 4, ls) * not_first
            e_scr[HALO:, ls] = col(m_ref, 3, ls) * col(m_ref, 4, ls)
        _shifted_copies(c_scr, c_sh, HALO + CD_TILE)
        for ls in lanes:
            for r0 in range(0, CD_TILE, TAP_ROWS):
                acc = jnp.zeros((TAP_ROWS, LANES), F32)
                for k in range(C_KERNEL):
                    acc = acc + cw_ref[k:k + 1, ls] * _rows_from(c_sh, r0 + HALO - (C_KERNEL - 1) + k, TAP_ROWS, ls)
                c1_ref[r0:r0 + TAP_ROWS, ls] = acc + cb_ref[:, ls]
        mean = sum(jnp.sum(c1_ref[:, ls], axis=-1, keepdims=True) for ls in lanes) * (1.0 / C_WIDTH)
        var = sum(jnp.sum((c1_ref[:, ls] - mean) ** 2, axis=-1, keepdims=True) for ls in lanes) * (1.0 / C_WIDTH)
        rs = lax.rsqrt(var + EPS)
        for ls in lanes:
            c2 = (c1_ref[:, ls] - mean) * rs * lg_ref[:, ls] + lb_ref[:, ls]
            o_ref[:, ls] = (c2 * _sigmoid(c2)).astype(BF16)
            d1 = jnp.zeros((CD_TILE, LANES), F32)
            for k in range(D_KERNEL):
                d1 = d1 + dw_ref[k:k + 1, ls] * e_scr[pl.ds(HALO - (D_KERNEL - 1) + k, CD_TILE), ls]
            o_ref[:, C_WIDTH + ls.start:C_WIDTH + ls.stop] = (col(m_ref, 2, ls) * d1).astype(BF16)

    return pl.pallas_call(
        body, grid=(t // CD_TILE,),
        in_specs=[pl.BlockSpec((HALO, CD_IN), lambda i: (jnp.maximum(i * per - 1, 0), 0)), _row_spec(CD_TILE, CD_IN),
                  _const_spec((32, C_WIDTH)), _const_spec((1, C_WIDTH)), _const_spec((1, C_WIDTH)), _const_spec((1, C_WIDTH)),
                  _const_spec((8, C_WIDTH))],
        out_specs=[_row_spec(CD_TILE, 2 * C_WIDTH), _row_spec(CD_TILE, C_WIDTH)],
        out_shape=[SDS((t, 2 * C_WIDTH), BF16), SDS((t, C_WIDTH), F32)],
        scratch_shapes=[pltpu.VMEM((HALO + CD_TILE, C_WIDTH), F32)] * 2 + [pltpu.VMEM((8, HALO + CD_TILE, C_WIDTH), F32)],
        name="mixer_cd_fwd", compiler_params=_params())(proj, proj, cw, cb, lg, lb, dw)


def _mixer_cd_bwd(proj, dcat, c1, cw, lg, lb, dw, ride=None):
    t = proj.shape[0]
    per = CD_TILE // HALO
    nt = t // CD_TILE
    ext = CD_TILE + HALO

    def body(hp_ref, m_ref, hn_ref, dm_ref, dn_ref, c1m_ref, c1n_ref, cw_ref, lg_ref, lb_ref, dw_ref,
             dp_ref, dcw_ref, dcb_ref, dlg_ref, dlb_ref, ddw_ref, c_scr, e_scr, dc1_scr, dd1_scr, c_sh, dc1_sh, dcw_acc,
             dvh_scr, vhat_scr):
        i = pl.program_id(0)

        @pl.when(i == 0)
        def _():
            for r in (dcw_acc, dcb_ref, dlg_ref, dlb_ref, ddw_ref):
                r[...] = jnp.zeros_like(r)

        not_first = (i > 0).astype(F32)
        not_last = (i < nt - 1).astype(F32)
        main = slice(HALO, HALO + CD_TILE)
        lanes = [slice(c * LANES, (c + 1) * LANES) for c in range(C_WIDTH // LANES)]

        def col(ref, part, ls):
            return ref[:, part * C_WIDTH + ls.start:part * C_WIDTH + ls.stop].astype(F32)

        for ls in lanes:
            c_scr[:HALO, ls] = col(hp_ref, 0, ls) * _sigmoid(col(hp_ref, 1, ls)) * not_first
            c_scr[main, ls] = col(m_ref, 0, ls) * _sigmoid(col(m_ref, 1, ls))
            c_scr[HALO + CD_TILE:, ls] = col(hn_ref, 0, ls) * _sigmoid(col(hn_ref, 1, ls)) * not_last
            e_scr[:HALO, ls] = col(hp_ref, 3, ls) * col(hp_ref, 4, ls) * not_first
            e_scr[main, ls] = col(m_ref, 3, ls) * col(m_ref, 4, ls)
            e_scr[HALO + CD_TILE:, ls] = col(hn_ref, 3, ls) * col(hn_ref, 4, ls) * not_last
        _shifted_copies(c_scr, c_sh, 2 * HALO + CD_TILE)

        def c1_of(ls):
            return jnp.concatenate([c1m_ref[:, ls], c1n_ref[:, ls]], axis=0)

        mean = sum(jnp.sum(c1_of(ls), axis=-1, keepdims=True) for ls in lanes) * (1.0 / C_WIDTH)
        var = sum(jnp.sum((c1_of(ls) - mean) ** 2, axis=-1, keepdims=True) for ls in lanes) * (1.0 / C_WIDTH)
        rs = lax.rsqrt(var + EPS)
        sum_dvh, sum_dvh_vhat = 0.0, 0.0
        for ls in lanes:
            vhat = (c1_of(ls) - mean) * rs
            c2 = vhat * lg_ref[:, ls] + lb_ref[:, ls]
            sig = _sigmoid(c2)
            dc = jnp.concatenate([dm_ref[:, ls], dn_ref[:, ls] * not_last], axis=0)
            dc2 = dc * (sig * (1.0 + c2 * (1.0 - sig)))
            dvh = dc2 * lg_ref[:, ls]
            sum_dvh = sum_dvh + jnp.sum(dvh, axis=-1, keepdims=True)
            sum_dvh_vhat = sum_dvh_vhat + jnp.sum(dvh * vhat, axis=-1, keepdims=True)
            dvh_scr[:, ls] = dvh
            vhat_scr[:, ls] = vhat
            dlg_ref[:, ls] += jnp.sum((dc2 * vhat)[:CD_TILE], axis=0, keepdims=True)
            dlb_ref[:, ls] += jnp.sum(dc2[:CD_TILE], axis=0, keepdims=True)
        for ls in lanes:
            dc1 = rs * (dvh_scr[:, ls] - sum_dvh * (1.0 / C_WIDTH) - vhat_scr[:, ls] * (sum_dvh_vhat * (1.0 / C_WIDTH)))
            dc1_scr[:, ls] = dc1
            dcb_ref[:, ls] += jnp.sum(dc1[:CD_TILE], axis=0, keepdims=True)
        _shifted_copies(dc1_scr, dc1_sh, ext)
        for ls in lanes:
            for r0 in range(0, CD_TILE, TAP_ROWS):
                rows = slice(r0, r0 + TAP_ROWS)
                dc1_m = dc1_scr[rows, ls]
                dc0 = jnp.zeros((TAP_ROWS, LANES), F32)
                for k in range(C_KERNEL):
                    dc0 = dc0 + cw_ref[k:k + 1, ls] * _rows_from(dc1_sh, r0 + C_KERNEL - 1 - k, TAP_ROWS, ls)
                    prod = dc1_m * _rows_from(c_sh, r0 + HALO - (C_KERNEL - 1) + k, TAP_ROWS, ls)
                    dcw_acc[k, :, ls] += prod.reshape(TAP_ROWS // 8, 8, LANES).sum(axis=0)
                g_m = m_ref[rows, C_WIDTH + ls.start:C_WIDTH + ls.stop].astype(F32)
                a_m = m_ref[rows, ls].astype(F32)
                sig_m = _sigmoid(g_m)
                dp_ref[rows, ls] = (dc0 * sig_m).astype(BF16)
                dp_ref[rows, C_WIDTH + ls.start:C_WIDTH + ls.stop] = (dc0 * a_m * sig_m * (1.0 - sig_m)).astype(BF16)

        @pl.when(i == nt - 1)
        def _():
            dcw_ref[...] = jnp.sum(dcw_acc[...], axis=1)

        for ls in lanes:
            wide = slice(C_WIDTH + ls.start, C_WIDTH + ls.stop)
            d1 = jnp.zeros((CD_TILE, LANES), F32)
            for k in range(D_KERNEL):
                d1 = d1 + dw_ref[k:k + 1, ls] * e_scr[pl.ds(HALO - (D_KERNEL - 1) + k, CD_TILE), ls]
            dd_m = dm_ref[:, wide]
            dd1 = jnp.concatenate([dd_m * col(m_ref, 2, ls), dn_ref[:, wide] * col(hn_ref, 2, ls) * not_last], axis=0)
            dd1_scr[:, ls] = dd1
            dp_ref[:, 2 * C_WIDTH + ls.start:2 * C_WIDTH + ls.stop] = (dd_m * d1).astype(BF16)
            de = jnp.zeros((CD_TILE, LANES), F32)
            for k in range(D_KERNEL):
                de = de + dw_ref[k:k + 1, ls] * dd1_scr[pl.ds(D_KERNEL - 1 - k, CD_TILE), ls]
                ddw_ref[k:k + 1, ls] += jnp.sum(dd1[:CD_TILE] * e_scr[pl.ds(HALO - (D_KERNEL - 1) + k, CD_TILE), ls], axis=0, keepdims=True)
            dp_ref[:, 3 * C_WIDTH + ls.start:3 * C_WIDTH + ls.stop] = (de * col(m_ref, 4, ls)).astype(BF16)
            dp_ref[:, 4 * C_WIDTH + ls.start:4 * C_WIDTH + ls.stop] = (de * col(m_ref, 3, ls)).astype(BF16)

    halo_prev = lambda i: (jnp.maximum(i * per - 1, 0), 0)
    halo_next = lambda i: (jnp.minimum((i + 1) * per, t // HALO - 1), 0)
    vec = _const_spec((1, C_WIDTH))
    return _call(
        body, grid=(nt,),
        in_specs=[pl.BlockSpec((HALO, CD_IN), halo_prev), _row_spec(CD_TILE, CD_IN), pl.BlockSpec((HALO, CD_IN), halo_next),
                  _row_spec(CD_TILE, 2 * C_WIDTH), pl.BlockSpec((HALO, 2 * C_WIDTH), halo_next),
                  _row_spec(CD_TILE, C_WIDTH), pl.BlockSpec((HALO, C_WIDTH), halo_next),
                  _const_spec((32, C_WIDTH)), vec, vec, _const_spec((8, C_WIDTH))],
        out_specs=[_row_spec(CD_TILE, CD_IN), _const_spec((32, C_WIDTH)), vec, vec, vec, _const_spec((8, C_WIDTH))],
        out_shape=[SDS((t, CD_IN), BF16), SDS((32, C_WIDTH), F32), SDS((1, C_WIDTH), F32), SDS((1, C_WIDTH), F32),
                   SDS((1, C_WIDTH), F32), SDS((8, C_WIDTH), F32)],
        scratch_shapes=[pltpu.VMEM((2 * HALO + CD_TILE, C_WIDTH), F32)] * 2 + [pltpu.VMEM((ext, C_WIDTH), F32)] * 2
        + [pltpu.VMEM((8, 2 * HALO + CD_TILE, C_WIDTH), F32), pltpu.VMEM((8, ext, C_WIDTH), F32),
           pltpu.VMEM((32, 8, C_WIDTH), F32)] + [pltpu.VMEM((ext, C_WIDTH), F32)] * 2,
        operands=[proj, proj, proj, dcat, dcat, c1, c1, cw, lg, lb, dw], name="mixer_cd_bwd", ride=ride)


def _wgrad(name, pairs, out_rc, t, ride):
    tk = TILES["wgrad"]
    assert tk == t, "the whole contraction has to fit one grid step"
    r, c = out_rc
    n = len(pairs)

    def body(*refs):
        ab, out_refs = refs[:2 * n], refs[2 * n:]
        for j in range(n):
            out_refs[j][...] = _dot(ab[2 * j][...], ab[2 * j + 1][...], TN).astype(BF16)

    operands, in_specs = [], []
    for lhs, lhs_spec, rhs, rhs_spec in pairs:
        operands += [lhs, rhs]
        in_specs += [lhs_spec, rhs_spec]
    res = _call(body, grid=(N_CHIPS, t // tk), in_specs=in_specs,
                out_specs=[pl.BlockSpec((None, r, c), lambda p, k: (p, 0, 0))] * n,
                out_shape=[SDS((N_CHIPS, r, c), BF16)] * n, operands=operands, name=name, ride=ride)
    outs, ride_res = (res, None) if ride is None else res
    outs = [o.reshape(N_CHIPS, 2, r // 2, c) for o in outs]
    return outs if ride is None else (outs, ride_res)


def _wgrad_col_sharded(name, h, dz_list, three_d, ride=None):
    t, d = h.shape
    tk = TILES["wgrad"]
    n4 = dz_list[0].shape[-1] if three_d else dz_list[0].shape[-1] // N_CHIPS
    hs = pl.BlockSpec((tk, d), lambda p, k: (k, 0))
    zs = pl.BlockSpec((None, tk, n4), lambda p, k: (p, k, 0)) if three_d else pl.BlockSpec((tk, n4), lambda p, k: (k, p))
    return _wgrad(name, [(h, hs, dz, zs) for dz in dz_list], (d, n4), t, ride)


def _wgrad_row_sharded(name, a, g, three_d, ride=None):
    many = isinstance(a, (list, tuple))
    a_list = list(a) if many else [a]
    t, d = g.shape
    tk = TILES["wgrad"]
    k4 = a_list[0].shape[-1] if three_d else a_list[0].shape[-1] // N_CHIPS
    a_spec = pl.BlockSpec((None, tk, k4), lambda p, k: (p, k, 0)) if three_d else pl.BlockSpec((tk, k4), lambda p, k: (k, p))
    gs = pl.BlockSpec((tk, d), lambda p, k: (k, 0))
    res = _wgrad(name, [(a_j, a_spec, g, gs) for a_j in a_list], (k4, d), t, ride)
    if many:
        return res
    return res[0] if ride is None else (res[0][0], res[1])


def _mesh_scalars():
    return jnp.stack([lax.axis_index("c"), 2 * lax.axis_index("x") + lax.axis_index("y")]).astype(jnp.int32)


def _stage_own(name, w, layer, dtype):
    layers, r, cols = w.shape
    h = r // 2

    def body(s_ref, x_ref, o_ref):
        o_ref[...] = x_ref[...].astype(dtype)

    return pl.pallas_call(
        body,
        grid_spec=pltpu.PrefetchScalarGridSpec(
            num_scalar_prefetch=1, grid=(2,),
            in_specs=[pl.BlockSpec((None, h, cols), lambda i, s: (2 * layer + i, 0, 0))],
            out_specs=pl.BlockSpec((None, None, h, cols), lambda i, s: (s[1], i, 0, 0))),
        out_shape=SDS((N_CHIPS, 2, h, cols), dtype), name=name,
        compiler_params=_params())(_mesh_scalars(), w.reshape(2 * layers, h, cols))


STAGE_STEPS = 4


def _stage_rest_and_norm(x, g, weights, ride=None):
    t, d = x.shape
    n = len(weights)
    views, in_specs, out_specs, out_shapes = [], [], [], []
    for w, layer in weights:
        layers, r, cols = w.shape
        sub = r // STAGE_STEPS
        views.append(w.reshape(layers * STAGE_STEPS, sub, cols))
        in_specs.append(pl.BlockSpec((None, sub, cols), functools.partial(lambda l, i, s: (STAGE_STEPS * l + i, 0, 0), layer)))
        out_specs.append(pl.BlockSpec((None, None, sub, cols), lambda i, s: (s[1], i // 2, i % 2, 0)))
        out_shapes.append(SDS((N_CHIPS, 2, r // 2, cols), BF16))

    def body(s_ref, x_ref, g_ref, *rest):
        w_refs, h_ref, o_refs = rest[:n], rest[n], rest[n + 1:]
        h_ref[...] = _rms_rows(x_ref[...], g_ref[...]).astype(BF16)
        for w_ref, o_ref in zip(w_refs, o_refs):
            o_ref[...] = w_ref[...].astype(BF16)

    tm = t // STAGE_STEPS
    res = _call(
        body, grid=(STAGE_STEPS,), in_specs=[pl.BlockSpec((tm, d), lambda i, s: (i, 0)), pl.BlockSpec((1, d), lambda i, s: (0, 0))] + in_specs,
        out_specs=[pl.BlockSpec((tm, d), lambda i, s: (i, 0))] + out_specs, out_shape=[SDS((t, d), BF16)] + out_shapes,
        operands=[x, g] + views, name="stage_and_norm", ride=ride, prefetch=_mesh_scalars())
    outs, ride_res = (res, None) if ride is None else res
    result = (outs[0], list(outs[1:]))
    return result if ride is None else (result, ride_res)


def _remote(src, dst, send_sem, recv_sem, device):
    return pltpu.make_async_remote_copy(src, dst, send_sem, recv_sem, device_id=device, device_id_type=MESH)


def _ride_gather_send(bufs):
    n = len(bufs)

    def each(b, sems, act):
        send, recv = sems
        x, y, c, p, others = _position()
        for t in range(n):
            for j, (qx, qy) in enumerate(others):
                act(b[t].at[p, c], b[t].at[2 * qx + qy, c], send.at[t, j], recv.at[t, j], (qx, qy, c))

    def start(ins, b, new, sems):
        each(b, sems, lambda mine, landed, s, r, dev: _remote(mine, mine, s, r, dev).start())

    def finish(ins, b, new, sems):
        def act(mine, landed, s, r, dev):
            _remote(mine, mine, s, r, dev).wait_send()
            _remote(landed, landed, s, r, dev).wait_recv()
        each(b, sems, act)

    return _Ride([], bufs, [], [(n, 3), (n, 3)], start, finish, ["chips"])


def _ride_gather_pass(bufs):
    n = len(bufs)

    def each(b, sems, act):
        send, recv = sems
        x, y, c, p, others = _position()
        for t in range(n):
            for j, (qx, qy) in enumerate(others):
                act(b[t].at[2 * qx + qy, c], b[t].at[2 * qx + qy, 1 - c], send.at[t, j], recv.at[t, j], (x, y, 1 - c))

    def start(ins, b, new, sems):
        each(b, sems, lambda landed, passed, s, r, dev: _remote(landed, landed, s, r, dev).start())

    def finish(ins, b, new, sems):
        def act(landed, passed, s, r, dev):
            _remote(landed, landed, s, r, dev).wait_send()
            _remote(passed, passed, s, r, dev).wait_recv()
        each(b, sems, act)

    return _Ride([], bufs, [], [(n, 3), (n, 3)], start, finish, ["sibling"])


def _ride_gather(bufs):
    send, onward = _ride_gather_send(bufs), _ride_gather_pass(bufs)
    n_send = len(send.sem_shapes)

    def start(ins, b, new, sems):
        send.start(ins, b, new, sems[:n_send])

    def finish(ins, b, new, sems):
        send.finish(ins, b, new, sems[:n_send])
        onward.start(ins, b, new, sems[n_send:])
        onward.finish(ins, b, new, sems[n_send:])

    return _Ride([], bufs, [], send.sem_shapes + onward.sem_shapes, start, finish, ["sibling", "chips"])


def _ride_swap(tensors):
    n = len(tensors)

    def each(ins, new, sems, act):
        send, recv = sems
        x, y, c, _, _ = _position()
        for t in range(n):
            act(_remote(ins[t].at[:, 1 - c], new[t], send.at[t], recv.at[t], (x, y, 1 - c)))

    def start(ins, b, new, sems):
        each(ins, new, sems, lambda cp: cp.start())

    def finish(ins, b, new, sems):
        each(ins, new, sems, lambda cp: cp.wait())

    return _Ride(tensors, [], [SDS((s.shape[0],) + s.shape[2:], s.dtype) for s in tensors], [(n,), (n,)], start, finish,
                 ["sibling"])


def _ride_scatter(tensors, landing):
    n = len(tensors)

    def each(ins, b, sems, act):
        send, recv = sems
        x, y, c, p, others = _position()
        for t in range(n):
            for j, (qx, qy) in enumerate(others):
                q = 2 * qx + qy
                act(ins[t].at[q], b[t].at[p], b[t].at[q], send.at[t, j], recv.at[t, j], (qx, qy, c))

    def start(ins, b, new, sems):
        each(ins, b, sems, lambda src, dst, landed, s, r, dev: _remote(src, dst, s, r, dev).start())

    def finish(ins, b, new, sems):
        def act(src, dst, landed, s, r, dev):
            _remote(src, dst, s, r, dev).wait_send()
            _remote(landed, landed, s, r, dev).wait_recv()
        each(ins, b, sems, act)

    return _Ride(tensors, landing, [], [(n, 3), (n, 3)], start, finish, ["chips"])


def _ride_join(bufs):
    n = len(bufs)

    def each(b, sems, act):
        send, recv = sems
        x, y, c, _, _ = _position()
        for t in range(n):
            act(b[t].at[c], b[t].at[1 - c], send.at[t], recv.at[t], (x, y, 1 - c))

    def start(ins, b, new, sems):
        each(b, sems, lambda mine, theirs, s, r, dev: _remote(mine, mine, s, r, dev).start())

    def finish(ins, b, new, sems):
        def act(mine, theirs, s, r, dev):
            _remote(mine, mine, s, r, dev).wait_send()
            _remote(theirs, theirs, s, r, dev).wait_recv()
        each(b, sems, act)

    return _Ride([], bufs, [], [(n,), (n,)], start, finish, ["sibling"])


def _all_reduce_small(pack, ride=None):
    rows = pack.shape[0]
    n_dev = 2 * N_CHIPS
    n_rb = 0 if ride is None else len(ride.bufs)

    def body(x_ref, *rest):
        o_ref = rest[n_rb]
        r_bufs = rest[n_rb + 1:2 * n_rb + 1]
        land, send, recv = rest[2 * n_rb + 1:2 * n_rb + 4]
        r_sems = rest[2 * n_rb + 4:]
        if ride is not None:
            ride.start([], r_bufs, [], r_sems)
        x, y, c, p, _ = _position()
        me = 2 * p + c
        land[me] = x_ref[...]
        peers = [(dx, dy, dc) for dx in range(2) for dy in range(2) for dc in range(2) if (dx, dy, dc) != (0, 0, 0)]
        for j, (dx, dy, dc) in enumerate(peers):
            _remote(land.at[me], land.at[me], send.at[j], recv.at[j], (x ^ dx, y ^ dy, c ^ dc)).start()
        for j, (dx, dy, dc) in enumerate(peers):
            src = 4 * (x ^ dx) + 2 * (y ^ dy) + (c ^ dc)
            _remote(land.at[me], land.at[me], send.at[j], recv.at[j], (x ^ dx, y ^ dy, c ^ dc)).wait_send()
            _remote(land.at[src], land.at[src], send.at[j], recv.at[j], (x ^ dx, y ^ dy, c ^ dc)).wait_recv()
        acc = land[0]
        for dev in range(1, n_dev):
            acc = acc + land[dev]
        o_ref[...] = acc
        if ride is not None:
            ride.finish([], r_bufs, [], r_sems)

    bufs = [] if ride is None else ride.bufs
    sems = [] if ride is None else [pltpu.SemaphoreType.DMA(s) for s in ride.sem_shapes]
    res = pl.pallas_call(
        body, in_specs=[pl.BlockSpec(memory_space=pltpu.VMEM)] + [ANY] * n_rb,
        out_specs=[pl.BlockSpec(memory_space=pltpu.VMEM)] + [ANY] * n_rb,
        out_shape=[SDS((rows, LANES), F32)] + [SDS(b.shape, b.dtype) for b in bufs],
        scratch_shapes=[pltpu.VMEM((n_dev, rows, LANES), F32), pltpu.SemaphoreType.DMA((n_dev - 1,)),
                        pltpu.SemaphoreType.DMA((n_dev - 1,))] + sems,
        input_output_aliases={1 + j: 1 + j for j in range(n_rb)},
        name="all_reduce_small", compiler_params=_params())(pack, *bufs)
    return res[0], list(res[1:])


def _add_own_half(name, full, recv, out_dtype):
    n4, _, h, cols = full.shape

    def body(s_ref, a_ref, b_ref, o_ref, own_ref):
        v = (a_ref[...].astype(F32) + b_ref[...].astype(F32)).astype(out_dtype)
        o_ref[...] = v

        @pl.when(pl.program_id(0) == s_ref[1])
        def _():
            own_ref[...] = v

    return pl.pallas_call(
        body,
        grid_spec=pltpu.PrefetchScalarGridSpec(
            num_scalar_prefetch=1, grid=(n4,),
            in_specs=[pl.BlockSpec((None, None, h, cols), lambda q, s: (q, s[0], 0, 0)),
                      pl.BlockSpec((None, h, cols), lambda q, s: (q, 0, 0))],
            out_specs=[pl.BlockSpec((None, h, cols), lambda q, s: (q, 0, 0)),
                       pl.BlockSpec((None, h, cols), lambda q, s: (s[1], 0, 0))]),
        out_shape=[SDS((n4, h, cols), out_dtype)] * 2, name=name, compiler_params=_params())(_mesh_scalars(), full, recv)


def _sum_chips(name, parts):
    n4, h, cols = parts.shape
    th = h // 4 if h % 64 == 0 else h

    def body(s_ref, a_ref, o_ref):
        acc = a_ref[0].astype(F32)
        for q in range(1, n4):
            acc = acc + a_ref[q].astype(F32)
        o_ref[...] = acc

    return pl.pallas_call(
        body,
        grid_spec=pltpu.PrefetchScalarGridSpec(
            num_scalar_prefetch=1, grid=(h // th,),
            in_specs=[pl.BlockSpec((n4, th, cols), lambda i, s: (0, i, 0))],
            out_specs=pl.BlockSpec((None, th, cols), lambda i, s: (s[0], i, 0))),
        out_shape=SDS((2, h, cols), F32), name=name, compiler_params=_params())(_mesh_scalars(), parts)


def _adamw_math(w, g, m, v):
    m2 = ADAM_B1 * m + (1.0 - ADAM_B1) * g
    v2 = ADAM_B2 * v + (1.0 - ADAM_B2) * (g * g)
    m_hat = m2 / (1.0 - ADAM_B1 ** ADAM_STEP)
    v_hat = v2 / (1.0 - ADAM_B2 ** ADAM_STEP)
    delta = -ADAM_LR * (m_hat / (jnp.sqrt(v_hat) + ADAM_EPS) + ADAM_WD * w)
    return delta, m2, v2


def _row_tile(rows, cols):
    cap = max(8, (1 << 18) // cols)
    best = 8
    for cand in range(8, min(rows, cap) + 1, 8):
        if rows % cand == 0:
            best = cand
    return best


def _adamw_big(name, w, g_layers, m, v):
    layers, rows, cols = w.shape
    tr = _row_tile(rows, cols)

    def body(w_ref, m_ref, v_ref, *rest):
        g_refs, (g_o, d_o, m_o, v_o) = rest[:layers], rest[layers:]
        gv = g_refs[0][...]
        for layer in range(1, layers):
            gv = jnp.where(pl.program_id(0) == layer, g_refs[layer][...], gv)
        d, mm, vv = _adamw_math(w_ref[...], gv, m_ref[...], v_ref[...])
        g_o[...] = gv
        d_o[...] = d
        m_o[...] = mm
        v_o[...] = vv

    blk = pl.BlockSpec((None, tr, cols), lambda l, i: (l, i, 0))
    g_blk = pl.BlockSpec((tr, cols), lambda l, i: (i, 0))
    return tuple(pl.pallas_call(
        body, grid=(layers, rows // tr), in_specs=[blk] * 3 + [g_blk] * layers, out_specs=[blk] * 4,
        out_shape=[SDS((layers, rows, cols), F32)] * 4, name=name,
        compiler_params=_params())(w, m, v, *[g.reshape(rows, cols) for g in g_layers]))


def _adamw_small(ws, gs, ms, vs):
    n = len(ws)
    flat = []
    for group in (ws, gs, ms, vs):
        flat += [a.reshape(-1, a.shape[-1]) for a in group]

    def body(*refs):
        w_r, g_r, m_r, v_r = refs[:n], refs[n:2 * n], refs[2 * n:3 * n], refs[3 * n:4 * n]
        d_o, m_o, v_o = refs[4 * n:5 * n], refs[5 * n:6 * n], refs[6 * n:7 * n]
        for j in range(n):
            d, mm, vv = _adamw_math(w_r[j][...], g_r[j][...], m_r[j][...], v_r[j][...])
            d_o[j][...] = d
            m_o[j][...] = mm
            v_o[j][...] = vv

    shapes = [SDS(a.shape, F32) for a in flat[:n]]
    outs = pl.pallas_call(body, out_shape=shapes * 3, name="adamw_small", compiler_params=_params())(*flat)
    res = []
    for k in range(3):
        res.append([outs[k * n + j].reshape(ws[j].shape) for j in range(n)])
    return res


BIG = ("ab_w_in", "ab_w_out", "cd_w_in", "cd_w_out", "ffn_w_gate", "ffn_w_up", "ffn_w_down")
V_BLOCK = (2 * A_WIDTH + QK_COLS) // B_WIDTH


def _pad_rows(a, rows):
    return jnp.pad(a, ((0, rows - a.shape[0]), (0, 0)))


A_IN, A_OUT, C_IN, C_OUT = ("ab_w_in", 0), ("ab_w_out", 0), ("cd_w_in", 0), ("cd_w_out", 0)
G0, U0, D0 = ("ffn_w_gate", 0), ("ffn_w_up", 0), ("ffn_w_down", 0)
G1, U1, D1 = ("ffn_w_gate", 1), ("ffn_w_up", 1), ("ffn_w_down", 1)
UNITS = (A_IN, A_OUT, G0, U0, D0, C_IN, C_OUT, G1, U1, D1)
ROWS_MINOR = ("ffn_w_gate", "ffn_w_up")
SMALL_SHARDED = ("small", 0)
REPLICATED_UNIT = ("replicated", 0)


class _Exchange:
    def __init__(self, enabled):
        self.enabled = enabled
        self.w, self.grad, self.recv, self.half, self.land, self.done = {}, {}, {}, {}, {}, {}

    def full(self, unit):
        b = self.w[unit]
        return b.reshape(N_CHIPS, 1, 2 * b.shape[2], b.shape[3])

    def ride_for(self, phases):
        rides, sinks = [], []
        for kind, units in phases:
            if kind == "send":
                rides.append(_ride_gather_send([self.w[u] for u in units]))
                sinks.append(self.w)
            elif kind == "pass":
                rides.append(_ride_gather_pass([self.w[u] for u in units]))
                sinks.append(self.w)
            elif kind == "gather":
                rides.append(_ride_gather([self.w[u] for u in units]))
                sinks.append(self.w)
            elif kind == "swap":
                rides.append(_ride_swap([self.grad[u] for u in units]))
                sinks.append(self.recv)
            elif kind == "scatter":
                rides.append(_ride_scatter([self.half[u] for u in units], [self.land[u] for u in units]))
                sinks.append(self.land)
            else:
                rides.append(_ride_join([self.done[u] for u in units]))
                sinks.append(self.done)
        ride = functools.reduce(_ride_both, rides)

        def settle(res):
            n_bufs = sum(len(r.bufs) for r in rides)
            bufs, new = list(res[:n_bufs]), list(res[n_bufs:])
            for r, sink, (_, units) in zip(rides, sinks, phases):
                vals = [bufs.pop(0) for _ in r.bufs] + [new.pop(0) for _ in r.new_outs]
                for u, v in zip(units, vals):
                    sink[u] = v

        return ride, settle

    def run(self, fn, *args, phases=(), **kw):
        if not self.enabled or not phases:
            return fn(*args, **kw)
        ride, settle = self.ride_for(phases)
        out, res = fn(*args, ride=ride, **kw)
        settle(res)
        return out

    def alone(self, name, phases):
        if self.enabled:
            ride, settle = self.ride_for(phases)
            settle(_run_ride(name, ride))

    def pair_sum(self, units):
        if self.enabled:
            for u in units:
                dtype = F32 if u in (SMALL_SHARDED, REPLICATED_UNIT) else BF16
                self.half[u], self.land[u] = _add_own_half(f"pair_sum_{u[0]}_{u[1]}", self.grad[u], self.recv[u], dtype)

    def chip_sum(self, units):
        if self.enabled:
            for u in units:
                self.done[u] = _sum_chips(f"chip_sum_{u[0]}_{u[1]}", self.land[u])


def _local_step(x, target, ex, sp, h0=None):
    t, d = x.shape
    tabs = _rope_tables(t)
    gains = jnp.concatenate([jnp.tile(sp["q_norm_g"][g], HEAD_DIM // 8) for g in range(N_DIL)]
                            + [jnp.tile(sp["k_norm_g"][g], HEAD_DIM // 8) for g in range(N_DIL)]).reshape(1, QK_COLS)
    bias_t = sp["sgu_bias"].T
    cw = _pad_rows(sp["conv_c_w"], 32)
    dw = _pad_rows(sp["conv_d_w"], 8)
    cb, clg, clb = (sp[k].reshape(1, C_WIDTH) for k in ("conv_c_b", "c_ln_g", "c_ln_b"))
    slg, slb = sp["sgu_norm_g"].reshape(1, A_WIDTH), sp["sgu_norm_b"].reshape(1, A_WIDTH)
    g_ab, g_cd = sp["ab_norm_g"].reshape(1, d), sp["cd_norm_g"].reshape(1, d)
    g_f0, g_f1 = sp["ffn_norm_g"][0:1], sp["ffn_norm_g"][1:2]
    run = ex.run

    def w2d(unit):
        return ex.full(unit).reshape(-1, d)

    if h0 is None:
        h0 = _rms_fwd("rms_ab", x, g_ab)
    proj = run(_proj_in, "proj_ab", h0, ex.full(A_IN), 0, phases=[("send", [A_OUT, G0])])
    a_out = _mixer_a_fwd(proj, slg, slb, sp["sgu_w"], bias_t)
    qk, q1, q2, k1, k2 = run(_qk_fwd, proj, gains, tabs, phases=[("pass", [A_OUT, G0]), ("send", [U0, C_OUT])])
    regrouped_qk = {1: (q1, k1), 2: (q2, k2)}
    fwd_phases = ([("pass", [U0, C_OUT]), ("send", [D0])], [("pass", [D0]), ("send", [C_IN])],
                  [("pass", [C_IN]), ("send", [D1])])
    qkv, o_list, l_list = [], [], []
    for g, rate in enumerate(DIL_RATES):
        if rate == 1:
            qk3, proj3 = qk.reshape(1, t, QK_COLS), proj.reshape(1, t, AB_IN)
            q, k, v = (qk3, g), (qk3, N_DIL + g), (proj3, V_BLOCK + g)
        else:
            vp, = _permute(f"regroup_v_{g}", [(proj, V_BLOCK + g)], rate)
            q, k, v = (regrouped_qk[g][0], 0), (regrouped_qk[g][1], 0), (vp, 0)
        qkv.append((q, k, v))
        o, l = run(_attn_fwd, f"attn_fwd_{g}", q, k, v, phases=fwd_phases[g])
        if rate == 1:
            o, l = o.reshape(t, B_WIDTH), l.reshape(t, B_WIDTH)
        o_list.append(o)
        l_list.append(l)
    cat, lse_tot, lse_1, lse_2 = _attn_merge(a_out, o_list, l_list)
    x1, hf0 = _proj_out("out_ab", cat, w2d(A_OUT), x, g_next=g_f0)
    fgate0, fup0, act0 = run(_ffn_in, "ffn_in_0", hf0, ex.full(G0), ex.full(U0), 0,
                           phases=[("pass", [D1]), ("send", [G1, U1])])
    x2, h1 = run(_ffn_out, "ffn_out_0", act0, ex.full(D0), 0, x1, g_next=g_cd, phases=[("pass", [G1, U1])])
    projcd = _proj_in("proj_cd", h1, ex.full(C_IN), 0)
    cat2, c1 = _mixer_cd_fwd(projcd, cw, cb, clg, clb, dw)
    x3, hf1 = _proj_out("out_cd", cat2, w2d(C_OUT), x2, g_next=g_f1)
    fgate1, fup1, act1 = _ffn_in("ffn_in_1", hf1, ex.full(G1), ex.full(U1), 0)
    dy, loss_acc, dy_b = _ffn_out("ffn_out_1", act1, ex.full(D1), 0, x3, target=target)
    loss = 0.5 * loss_acc[0, 0] / d

    late = [D1, G1, U1]
    dgate, dup = _ffn_dact("ffn_dact_1", dy_b, ex.full(D1), 0, fgate1, fup1)
    ex.grad[D1] = _wgrad_row_sharded("wgrad_down_1", act1, dy_b, True)
    ex.grad[G1] = _wgrad_row_sharded("wgrad_gate_1", dgate, hf1, True)
    ex.grad[U1] = _wgrad_row_sharded("wgrad_up_1", dup, hf1, True)
    g3, d_f1, g3_b = run(_dgrad_cols, "dgrad_ffn_1", [dgate, dup], [ex.full(G1), ex.full(U1)], 0, True, x3, g_f1, dy,
                         w_rows=True, phases=[("swap", late)])
    ex.pair_sum(late)

    dcat2 = _dgrad_rows("dgrad_out_cd", g3_b, w2d(C_OUT))
    ex.grad[C_OUT] = _wgrad_row_sharded("wgrad_out_cd", cat2, g3_b, False)
    dprojcd, d_cw, d_cb, d_clg, d_clb, d_dw = run(_mixer_cd_bwd, projcd, dcat2, c1, cw, clg, clb, dw, phases=[("scatter", late)])
    ex.chip_sum(late)
    ex.grad[C_IN] = _wgrad_col_sharded("wgrad_in_cd", h1, [dprojcd], False)[0]
    g2, d_cdn, g2_b = run(_dgrad_cols, "dgrad_in_cd", [dprojcd], [ex.full(C_IN)], 0, False, x2, g_cd, g3,
                          phases=[("join", late), ("swap", [C_OUT, C_IN])])
    ex.pair_sum([C_OUT, C_IN])

    dgate, dup = run(_ffn_dact, "ffn_dact_0", g2_b, ex.full(D0), 0, fgate0, fup0, phases=[("scatter", [C_OUT, C_IN])])
    ex.chip_sum([C_OUT, C_IN])
    ex.grad[D0] = _wgrad_row_sharded("wgrad_down_0", act0, g2_b, True)
    ex.grad[G0] = _wgrad_row_sharded("wgrad_gate_0", dgate, hf0, True)
    ex.grad[U0] = _wgrad_row_sharded("wgrad_up_0", dup, hf0, True)
    small = {"cd_norm_g": d_cdn, "conv_c_w": d_cw[:C_KERNEL], "conv_c_b": d_cb, "c_ln_g": d_clg, "c_ln_b": d_clb,
             "conv_d_w": d_dw[:D_KERNEL]}
    ex.grad[SMALL_SHARDED] = _split_full_small(small).reshape(N_CHIPS, 2, SHARDED_ROWS // 2, LANES)
    mid = [D0, G0, U0, SMALL_SHARDED]
    g1, d_f0, g1_b = run(_dgrad_cols, "dgrad_ffn_0", [dgate, dup], [ex.full(G0), ex.full(U0)], 0, True, x1, g_f0, g2,
                         w_rows=True, phases=[("join", [C_OUT, C_IN]), ("swap", mid)])
    ex.pair_sum(mid)

    dcat = _dgrad_rows("dgrad_out_ab", g1_b, w2d(A_OUT))
    ex.grad[A_OUT] = _wgrad_row_sharded("wgrad_out_ab", cat, g1_b, False)
    d_a, d_sw, d_sbt, d_slg, d_slb = _mixer_a_bwd(proj, dcat, slg, slb, sp["sgu_w"], bias_t)
    early = {"sgu_norm_g": d_slg, "sgu_norm_b": d_slb, "sgu_w": d_sw, "sgu_bias": d_sbt.T}
    ex.grad[REPLICATED_UNIT] = jnp.broadcast_to(
        _pack_replicated(early, REPLICATED_EARLY, REPLICATED_EARLY_ROWS).reshape(2, REPLICATED_EARLY_ROWS // 2, LANES),
        (N_CHIPS, 2, REPLICATED_EARLY_ROWS // 2, LANES))
    last = [A_OUT, REPLICATED_UNIT]
    dbb, dd, db_1, dd_1, db_2, dd_2 = _attn_bwd_prep(dcat, cat)
    regrouped_bwd = {1: (db_1, lse_1, dd_1), 2: (db_2, lse_2, dd_2)}
    bwd_phases = ([("scatter", [D0, SMALL_SHARDED])],
                  [("scatter", [G0]), ("join", [D0, SMALL_SHARDED]), ("swap", last)],
                  [("scatter", [U0]), ("join", [G0])])
    dqs, dks, dvs = [], [], []
    for g, rate in enumerate(DIL_RATES):
        q, k, v = qkv[g]
        if rate == 1:
            db3, l3, dd3 = (a.reshape(1, t, B_WIDTH) for a in (dbb, lse_tot, dd))
        else:
            db3, l3, dd3 = regrouped_bwd[g]
        if g == 1:
            ex.chip_sum([D0, SMALL_SHARDED])
        elif g == 2:
            ex.chip_sum([G0])
            ex.pair_sum(last)
        dq, dk, dv = run(_attn_bwd, f"attn_bwd_{g}", q, k, v, db3, l3, dd3, phases=bwd_phases[g])
        if rate == 1:
            dq, dk, dv = (a.reshape(t, B_WIDTH) for a in (dq, dk, dv))
        dqs.append(dq)
        dks.append(dk)
        dvs.append(dv)
    ex.chip_sum([U0])
    dproj, d_gains = run(_dproj_assemble, proj, d_a, dqs, dks, dvs, gains, tabs, phases=[("scatter", last), ("join", [U0])])
    ex.chip_sum(last)
    d_gains = _fold_heads(d_gains)[0].reshape(2, N_DIL, B_WIDTH)[:, :, :HEAD_DIM]
    ex.grad[A_IN] = _wgrad_col_sharded("wgrad_in_ab", h0, [dproj], False)[0]
    ex.alone("swap_last", [("swap", [A_IN])])
    ex.pair_sum([A_IN])
    gx, d_abn = run(_dgrad_cols, "dgrad_in_ab", [dproj], [ex.full(A_IN)], 0, False, x, g_ab, g1, bf16_copy=False,
                    phases=[("join", last), ("scatter", [A_IN])])
    ex.chip_sum([A_IN])

    small.update({
        "ab_norm_g": d_abn, "sgu_norm_g": d_slg, "sgu_norm_b": d_slb, "sgu_w": d_sw, "sgu_bias": d_sbt.T,
        "q_norm_g": d_gains[0], "k_norm_g": d_gains[1], "ffn_norm_g": jnp.concatenate([d_f0, d_f1], axis=0),
    })
    return loss, gx, small


SHARDED_SMALL = ("cd_norm_g", "conv_c_w", "conv_c_b", "c_ln_g", "c_ln_b", "conv_d_w")
SHARDED_ROWS = 48
REPLICATED_EARLY = ("sgu_norm_g", "sgu_norm_b", "sgu_w", "sgu_bias")
REPLICATED_EARLY_ROWS = 528
REPLICATED_LATE = ("ab_norm_g", "q_norm_g", "k_norm_g", "ffn_norm_g", "loss")
REPLICATED_LATE_ROWS = 32
REPLICATED_SMALL = REPLICATED_EARLY + REPLICATED_LATE[:-1]


def _pack_sharded(parts):
    rows = [parts[k].reshape(-1, LANES) for k in SHARDED_SMALL]
    return _pad_rows(jnp.concatenate(rows, axis=0), SHARDED_ROWS)


def _split_full_small(small):
    per_chip = []
    for q in range(N_CHIPS):
        parts = {}
        for k in SHARDED_SMALL:
            a = small[k]
            a = a.reshape(-1, a.shape[-1])
            n = a.shape[-1] // N_CHIPS
            parts[k] = a[:, q * n:(q + 1) * n]
        per_chip.append(_pack_sharded(parts))
    return jnp.stack(per_chip)


def _unpack_sharded(pack, shapes):
    out, r = {}, 0
    for k in SHARDED_SMALL:
        n = math.prod(shapes[k]) // LANES
        out[k] = pack[r:r + n].reshape(shapes[k])
        r += n
    return out


def _gathered_small(packs, shapes):
    per_chip = [_unpack_sharded(packs[q], shapes) for q in range(N_CHIPS)]
    return {k: jnp.concatenate([pc[k] for pc in per_chip], axis=-1) for k in SHARDED_SMALL}


def _pack_replicated(small, names, total_rows):
    rows = []
    for k in names:
        a = small[k].reshape(-1)
        a = jnp.pad(a, (0, (-a.shape[0]) % LANES))
        rows.append(a.reshape(-1, LANES))
    return _pad_rows(jnp.concatenate(rows, axis=0), total_rows)


def _unpack_replicated(pack, shapes, names):
    out, r = {}, 0
    for k in names:
        size = math.prod(shapes[k])
        n = -(-size // LANES)
        out[k] = pack[r:r + n].reshape(-1)[:size].reshape(shapes[k])
        r += n
    return out


WEIGHT_ORDER = ("ab_norm_g", "ab_w_in", "sgu_norm_g", "sgu_norm_b", "sgu_w", "sgu_bias", "q_norm_g", "k_norm_g", "ab_w_out",
                "cd_norm_g", "cd_w_in", "conv_c_w", "conv_c_b", "c_ln_g", "c_ln_b", "conv_d_w", "cd_w_out", "ffn_norm_g",
                "ffn_w_gate", "ffn_w_up", "ffn_w_down")


def kernel(x, ab_norm_g, ab_w_in, sgu_norm_g, sgu_norm_b, sgu_w, sgu_bias, q_norm_g, k_norm_g, ab_w_out, cd_norm_g, cd_w_in, conv_c_w, conv_c_b, c_ln_g, c_ln_b, conv_d_w, cd_w_out, ffn_norm_g, ffn_w_gate, ffn_w_up, ffn_w_down, loss_target, m_ab_norm_g, m_ab_w_in, m_sgu_norm_g, m_sgu_norm_b, m_sgu_w, m_sgu_bias, m_q_norm_g, m_k_norm_g, m_ab_w_out, m_cd_norm_g, m_cd_w_in, m_conv_c_w, m_conv_c_b, m_c_ln_g, m_c_ln_b, m_conv_d_w, m_cd_w_out, m_ffn_norm_g, m_ffn_w_gate, m_ffn_w_up, m_ffn_w_down, v_ab_norm_g, v_ab_w_in, v_sgu_norm_g, v_sgu_norm_b, v_sgu_w, v_sgu_bias, v_q_norm_g, v_k_norm_g, v_ab_w_out, v_cd_norm_g, v_cd_w_in, v_conv_c_w, v_conv_c_b, v_c_ln_g, v_c_ln_b, v_conv_d_w, v_cd_w_out, v_ffn_norm_g, v_ffn_w_gate, v_ffn_w_up, v_ffn_w_down):
    args = dict(locals())
    ws = {k: args[k] for k in WEIGHT_ORDER}
    ms = {k: args["m_" + k] for k in WEIGHT_ORDER}
    vs = {k: args["v_" + k] for k in WEIGHT_ORDER}
    small_names = [k for k in WEIGHT_ORDER if k not in BIG]
    t, d = x.shape[1:]

    for group in (ws, ms, vs):
        for k in ROWS_MINOR:
            group[k] = jnp.swapaxes(group[k], 1, 2)
    ex = _Exchange(enabled=True)
    ex.w[A_IN] = _stage_own("stage_ab_w_in", ws["ab_w_in"], 0, BF16)
    own_small = _pack_sharded({k: ws[k][0] for k in SHARDED_SMALL})
    ex.w[SMALL_SHARDED] = _stage_own("stage_small", own_small[None], 0, F32)
    rest = [u for u in UNITS if u != A_IN]
    x2 = x.reshape(t, d)
    h0, staged = ex.run(_stage_rest_and_norm, x2, ws["ab_norm_g"], [(ws[name], layer) for name, layer in rest],
                        phases=[("gather", [A_IN, SMALL_SHARDED])])
    ex.w.update(zip(rest, staged))
    sp = _gathered_small(ex.w[SMALL_SHARDED].reshape(N_CHIPS, SHARDED_ROWS, LANES), {k: ws[k].shape[1:] for k in SHARDED_SMALL})
    for k in REPLICATED_SMALL:
        sp[k] = ws[k] if k == "ffn_norm_g" else ws[k][0]

    loss, grad_x, g_small = _local_step(x2, loss_target.reshape(t, d), ex, sp, h0)

    shapes = {k: ws[k].shape for k in REPLICATED_SMALL}
    shapes["loss"] = (1,)
    g_small["loss"] = loss
    join_last, settle = ex.ride_for([("join", [A_IN])])
    late, joined = _all_reduce_small(_pack_replicated(g_small, REPLICATED_LATE, REPLICATED_LATE_ROWS), join_last)
    settle(joined)
    grad = _unpack_sharded(ex.done[SMALL_SHARDED].reshape(SHARDED_ROWS, LANES), {k: ws[k].shape for k in SHARDED_SMALL})
    grad.update(_unpack_replicated(ex.done[REPLICATED_UNIT].reshape(REPLICATED_EARLY_ROWS, LANES), shapes, REPLICATED_EARLY))
    grad.update(_unpack_replicated(late, shapes, REPLICATED_LATE))
    loss = grad.pop("loss")[0]

    delta, new_m, new_v = {}, {}, {}
    for k in BIG:
        g_layers = [ex.done[(k, layer)] for layer in range(ws[k].shape[0])]
        outs = _adamw_big("adamw_" + k, ws[k], g_layers, ms[k], vs[k])
        if k in ROWS_MINOR:
            outs = [jnp.swapaxes(o, 1, 2) for o in outs]
        grad[k], delta[k], new_m[k], new_v[k] = outs
    d_s, m_s, v_s = _adamw_small([ws[k] for k in small_names], [grad[k] for k in small_names],
                                 [ms[k] for k in small_names], [vs[k] for k in small_names])
    for j, k in enumerate(small_names):
        delta[k], new_m[k], new_v[k] = d_s[j], m_s[j], v_s[j]

    return (loss, grad_x[None], *[grad[k] for k in WEIGHT_ORDER], *[delta[k] for k in WEIGHT_ORDER],
            *[new_m[k] for k in WEIGHT_ORDER], *[new_v[k] for k in WEIGHT_ORDER])
```

```python
import functools
import math

import jax
import jax.numpy as jnp
from jax import lax
from jax.experimental import pallas as pl
from jax.experimental.pallas import tpu as pltpu

F32 = jnp.float32
BF16 = jnp.bfloat16
SDS = jax.ShapeDtypeStruct

N_CHIPS = 4
EPS = 1e-6
NEG_INF = -1e30
CHUNK = 128
A_GROUPS = 4
A_WIDTH = 512
N_DIL = 3
DIL_RATES = (1, 4, 16)
HEAD_DIM = 64
B_WIDTH = 512
ROPE_DIM = 16
ROPE_THETA = 500000.0
C_WIDTH = 512
C_KERNEL = 31
D_KERNEL = 3
HALO = 32
ATT_BLOCK = 128
LANES = 128

ADAM_LR = 0.001
ADAM_B1 = 0.9
ADAM_B2 = 0.999
ADAM_EPS = 1e-08
ADAM_WD = 0.01
ADAM_STEP = 10

VMEM_LIMIT = 56 * 1024 * 1024

NN = (((1,), (0,)), ((), ()))
NT = (((1,), (1,)), ((), ()))
TN = (((0,), (0,)), ((), ()))

TILES = {"proj_in": 2048, "proj_out": 1024, "ffn_in": 1024, "ffn_out": 1024, "ffn_dact": 512, "dgrad_cols": 512,
         "dgrad_rows": 1024, "wgrad": 4096}


def _params(sem=None, collective_id=None):
    return pltpu.CompilerParams(dimension_semantics=sem, vmem_limit_bytes=VMEM_LIMIT, collective_id=collective_id)


def _bf(v):
    return v if v.dtype == BF16 else v.astype(BF16)


def _dot(a, b, dims):
    return lax.dot_general(_bf(a), _bf(b), dims, preferred_element_type=F32)


def _dot_hi(a, b):
    return jnp.dot(a, b, precision=lax.Precision.HIGHEST, preferred_element_type=F32)


def _sigmoid(v):
    return 0.5 * jnp.tanh(0.5 * v) + 0.5


def _gelu(v):
    return 0.5 * v * (1.0 + lax.erf(v * (1.0 / math.sqrt(2.0))))


def _gelu_grad(v):
    cdf = 0.5 * (1.0 + lax.erf(v * (1.0 / math.sqrt(2.0))))
    return cdf + v * jnp.exp(-0.5 * v * v) * (1.0 / math.sqrt(2.0 * math.pi))


def _segment_mean_matrix(seg, scale=None):
    r = lax.broadcasted_iota(jnp.int32, (LANES, LANES), 0) // seg
    c = lax.broadcasted_iota(jnp.int32, (LANES, LANES), 1) // seg
    return jnp.where(r == c, (1.0 / seg) if scale is None else scale, 0.0).astype(BF16)


def _segment_dot(v, seg):
    hi = v.astype(BF16)
    lo = (v - hi.astype(F32)).astype(BF16)
    return jnp.dot(hi, seg, preferred_element_type=F32) + jnp.dot(lo, seg, preferred_element_type=F32)


MESH = pl.DeviceIdType.MESH
ANY = pl.BlockSpec(memory_space=pl.ANY)


def _position():
    x, y, c = lax.axis_index("x"), lax.axis_index("y"), lax.axis_index("c")
    others = [(1 - x, y), (x, 1 - y), (1 - x, 1 - y)]
    return x, y, c, 2 * x + y, others


class _Ride:
    def __init__(self, ins, bufs, new_outs, sem_shapes, start, finish, reach):
        self.ins, self.bufs, self.new_outs, self.sem_shapes = list(ins), list(bufs), list(new_outs), list(sem_shapes)
        self.start, self.finish = start, finish
        self.reach = frozenset(reach)

    def entry_barrier(self):
        x, y, c, _, others = _position()
        peers = ([(x, y, 1 - c)] if "sibling" in self.reach else []) + ([(qx, qy, c) for qx, qy in others] if "chips" in self.reach else [])
        barrier = pltpu.get_barrier_semaphore()
        for peer in peers:
            pl.semaphore_signal(barrier, inc=1, device_id=peer, device_id_type=MESH)
        pl.semaphore_wait(barrier, len(peers))

    @property
    def collective_id(self):
        return {frozenset(["sibling"]): 0, frozenset(["chips"]): 1, frozenset(["sibling", "chips"]): 2}[self.reach]


def _ride_both(a, b):
    na = (len(a.ins), len(a.bufs), len(a.new_outs), len(a.sem_shapes))

    def split(ins, bufs, new, sems):
        return ((ins[:na[0]], bufs[:na[1]], new[:na[2]], sems[:na[3]]), (ins[na[0]:], bufs[na[1]:], new[na[2]:], sems[na[3]:]))

    def start(*refs):
        ra, rb = split(*refs)
        a.start(*ra)
        b.start(*rb)

    def finish(*refs):
        ra, rb = split(*refs)
        a.finish(*ra)
        b.finish(*rb)

    return _Ride(a.ins + b.ins, a.bufs + b.bufs, a.new_outs + b.new_outs, a.sem_shapes + b.sem_shapes, start, finish,
                 a.reach | b.reach)


def _call(body, *, grid, in_specs, out_specs, out_shape, operands, name, scratch_shapes=(), aliases=None, ride=None,
          prefetch=None):
    off = 0 if prefetch is None else 1
    lead = [] if prefetch is None else [prefetch]

    params = _params(collective_id=None if ride is None else ride.collective_id)

    def launch(kernel_body, in_specs_, out_specs_, out_shape_, scratch_, aliases_, *args):
        if prefetch is None:
            return pl.pallas_call(kernel_body, grid=grid, in_specs=in_specs_, out_specs=out_specs_, out_shape=out_shape_,
                                  scratch_shapes=scratch_, input_output_aliases=aliases_, name=name,
                                  compiler_params=params)(*args)
        spec = pltpu.PrefetchScalarGridSpec(num_scalar_prefetch=1, grid=grid, in_specs=in_specs_, out_specs=out_specs_,
                                            scratch_shapes=scratch_)
        return pl.pallas_call(kernel_body, grid_spec=spec, out_shape=out_shape_, input_output_aliases=aliases_, name=name,
                              compiler_params=params)(*lead, *args)

    if ride is None:
        return launch(body, list(in_specs), out_specs, out_shape, list(scratch_shapes), dict(aliases or {}), *operands)
    multi = isinstance(out_shape, (list, tuple))
    out_shapes = list(out_shape) if multi else [out_shape]
    o_specs = list(out_specs) if multi else [out_specs]
    n_in, n_out, n_scr = off + len(operands), len(out_shapes), len(scratch_shapes)
    n_ri, n_rb, n_rn = len(ride.ins), len(ride.bufs), len(ride.new_outs)

    def carrying(*refs):
        k = n_in
        r_ins = refs[k:k + n_ri]
        k += n_ri + n_rb
        outs = refs[k:k + n_out]
        k += n_out
        r_bufs = refs[k:k + n_rb]
        k += n_rb
        r_new = refs[k:k + n_rn]
        k += n_rn
        scratch = refs[k:k + n_scr]
        sems = refs[k + n_scr:]
        first, last = None, None
        for axis, size in enumerate(grid):
            pid = pl.program_id(axis)
            first = (pid == 0) if first is None else first & (pid == 0)
            last = (pid == size - 1) if last is None else last & (pid == size - 1)

        @pl.when(first)
        def _():
            ride.entry_barrier()
            ride.start(r_ins, r_bufs, r_new, sems)

        body(*refs[:n_in], *outs, *scratch)

        @pl.when(last)
        def _():
            ride.finish(r_ins, r_bufs, r_new, sems)

    all_aliases = dict(aliases or {})
    for j in range(n_rb):
        all_aliases[n_in + n_ri + j] = n_out + j
    res = launch(
        carrying, list(in_specs) + [ANY] * (n_ri + n_rb), o_specs + [ANY] * (n_rb + n_rn),
        out_shapes + [SDS(b.shape, b.dtype) for b in ride.bufs] + ride.new_outs,
        list(scratch_shapes) + [pltpu.SemaphoreType.DMA(s) for s in ride.sem_shapes], all_aliases,
        *operands, *ride.ins, *ride.bufs)
    outs = res[:n_out]
    return (list(outs) if multi else outs[0]), list(res[n_out:])


def _run_ride(name, ride):
    n_ri, n_rb, n_rn = len(ride.ins), len(ride.bufs), len(ride.new_outs)

    def body(*refs):
        r_ins = refs[:n_ri]
        r_bufs = refs[n_ri + n_rb:n_ri + 2 * n_rb]
        r_new = refs[n_ri + 2 * n_rb:n_ri + 2 * n_rb + n_rn]
        sems = refs[n_ri + 2 * n_rb + n_rn:]
        ride.entry_barrier()
        ride.start(r_ins, r_bufs, r_new, sems)
        ride.finish(r_ins, r_bufs, r_new, sems)

    return list(pl.pallas_call(
        body, in_specs=[ANY] * (n_ri + n_rb), out_specs=[ANY] * (n_rb + n_rn),
        out_shape=[SDS(b.shape, b.dtype) for b in ride.bufs] + ride.new_outs,
        scratch_shapes=[pltpu.SemaphoreType.DMA(s) for s in ride.sem_shapes],
        input_output_aliases={n_ri + j: j for j in range(n_rb)}, name=name,
        compiler_params=pltpu.CompilerParams(collective_id=ride.collective_id))(*ride.ins, *ride.bufs))


def _whole(ref, p):
    return ref[...]


def _slab(ref, p):
    return ref[p]


def _matmul(name, grid, pairs, extras, outs, dims, epi, *, slabs=1, n_acc=1, ride=None):
    n_pairs, n_ex, n_out = len(pairs), len(extras), len(outs)

    def body(*refs):
        ab = refs[:2 * n_pairs]
        ex = refs[2 * n_pairs:2 * n_pairs + n_ex]
        out_refs = refs[2 * n_pairs + n_ex:2 * n_pairs + n_ex + n_out]
        pids = tuple(pl.program_id(a) for a in range(len(grid)))
        parts = [None] * n_acc
        for p in range(slabs):
            for j, (_, _, a_pick, _, _, b_pick, acc) in enumerate(pairs):
                d = _dot(a_pick(ab[2 * j], p), b_pick(ab[2 * j + 1], p), dims)
                parts[acc] = d if parts[acc] is None else parts[acc] + d
        epi(parts, ex, out_refs, pids)

    operands, in_specs = [], []
    for a, a_spec, _, b, b_spec, _, _ in pairs:
        operands += [a, b]
        in_specs += [a_spec, b_spec]
    for e, e_spec in extras:
        operands.append(e)
        in_specs.append(e_spec)
    return _call(body, grid=grid, in_specs=in_specs, out_specs=[o[1] for o in outs], out_shape=[o[0] for o in outs],
                 operands=operands, name=name, ride=ride)


def _rms_rows(v, g):
    r = lax.rsqrt(jnp.mean(v * v, axis=-1, keepdims=True) + EPS)
    return v * r * g


def _rms_fwd(name, x, g):
    t, d = x.shape
    tm = 512

    def body(x_ref, g_ref, o_ref):
        o_ref[...] = _rms_rows(x_ref[...], g_ref[...]).astype(BF16)

    return pl.pallas_call(
        body, grid=(t // tm,),
        in_specs=[pl.BlockSpec((tm, d), lambda i: (i, 0)), pl.BlockSpec((1, d), lambda i: (0, 0))],
        out_specs=pl.BlockSpec((tm, d), lambda i: (i, 0)), out_shape=SDS((t, d), BF16), name=name,
        compiler_params=_params())(x, g)


def _epi_residual_norm(accs, ex, outs, pids):
    x_new = accs[0] + ex[0][...]
    outs[0][...] = x_new
    outs[1][...] = _rms_rows(x_new, ex[1][...]).astype(BF16)


def _epi_residual_loss(accs, ex, outs, pids):
    y = accs[0] + ex[0][...]
    err = y - ex[1][...]
    dy = err * (1.0 / err.shape[-1])
    outs[0][...] = dy
    outs[2][...] = dy.astype(BF16)

    @pl.when(pids[0] == 0)
    def _():
        outs[1][...] = jnp.zeros_like(outs[1])

    outs[1][...] += jnp.sum(err * err)


def _epi_rms_bwd(accs, ex, outs, pids):
    dh = accs[0]
    xv, g, res = ex[0][...], ex[1][...], ex[2][...]
    r = lax.rsqrt(jnp.mean(xv * xv, axis=-1, keepdims=True) + EPS)
    xh = xv * r
    dy = dh * g
    dx = res + r * (dy - xh * jnp.mean(dy * xh, axis=-1, keepdims=True))
    outs[0][...] = dx
    if len(outs) > 2:
        outs[2][...] = dx.astype(BF16)

    @pl.when(pids[0] == 0)
    def _():
        outs[1][...] = jnp.zeros_like(outs[1])

    outs[1][...] += jnp.sum(dh * xh, axis=0, keepdims=True)


def _row_spec(tm, d):
    return pl.BlockSpec((tm, d), lambda i, *_: (i, 0))


def _const_spec(shape):
    nd = len(shape)
    return pl.BlockSpec(shape, lambda *_: (0,) * nd)


def _proj_in(name, h, w, layer, ride=None):
    t, d = h.shape
    n4 = w.shape[-1]
    tm = TILES["proj_in"]

    def epi(accs, ex, outs, pids):
        outs[0][...] = accs[0].astype(BF16)

    res = _matmul(
        name, (N_CHIPS, t // tm),
        [(h, pl.BlockSpec((tm, d), lambda p, i: (i, 0)), _whole,
          w, pl.BlockSpec((None, None, d, n4), lambda p, i: (p, layer, 0, 0)), _whole, 0)],
        [], [(SDS((t, N_CHIPS * n4), BF16), pl.BlockSpec((tm, n4), lambda p, i: (i, p)))],
        NN, epi, ride=ride)
    return res[0] if ride is None else (res[0][0], res[1])


def _proj_in_near(name, h, w, ride=None):
    t, d = h.shape
    n4 = w.shape[-1]
    tm = TILES["proj_in"]

    def shard(j, s):
        return j + (j >= N_CHIPS - 1 - s[1]).astype(jnp.int32)

    def body(s_ref, h_ref, w_ref, o_ref):
        o_ref[...] = _dot(h_ref[...], w_ref[...], NN).astype(BF16)

    return _call(
        body, grid=(N_CHIPS - 1, t // tm),
        in_specs=[pl.BlockSpec((tm, d), lambda j, i, s: (i, 0)),
                  pl.BlockSpec((None, None, d, n4), lambda j, i, s: (shard(j, s), 0, 0, 0))],
        out_specs=pl.BlockSpec((tm, n4), lambda j, i, s: (i, shard(j, s))), out_shape=SDS((t, N_CHIPS * n4), BF16),
        operands=[h, w], name=name, ride=ride, prefetch=_mesh_scalars())


def _proj_in_far(name, h, slab, proj, w):
    t, d = h.shape
    n4 = slab.shape[-1]
    tm = TILES["proj_in"]

    def body(s_ref, h_ref, slab_ref, proj_in, w_in, o_ref, w_ref):
        shard = slab_ref[...]
        o_ref[...] = _dot(h_ref[...], shard, NN).astype(BF16)
        w_ref[...] = shard

    proj, w_full = _call(
        body, grid=(t // tm,),
        in_specs=[pl.BlockSpec((tm, d), lambda i, s: (i, 0)), pl.BlockSpec((d, n4), lambda i, s: (0, 0)), ANY, ANY],
        out_specs=[pl.BlockSpec((tm, n4), lambda i, s: (i, N_CHIPS - 1 - s[1])),
                   pl.BlockSpec((None, d, n4), lambda i, s: (N_CHIPS - 1 - s[1], 0, 0))],
        out_shape=[SDS(proj.shape, BF16), SDS((N_CHIPS, d, n4), BF16)],
        operands=[h, slab.reshape(d, n4), proj, w.reshape(N_CHIPS, d, n4)], aliases={3: 0, 4: 1}, name=name,
        prefetch=_mesh_scalars())
    return proj, w_full.reshape(w.shape)


def _proj_out(name, a, w, x, g_next=None, target=None):
    t, k = a.shape
    d = w.shape[-1]
    tm = TILES["proj_out"]
    if target is None:
        extras = [(x, _row_spec(tm, d)), (g_next, _const_spec((1, d)))]
        outs = [(SDS((t, d), F32), _row_spec(tm, d)), (SDS((t, d), BF16), _row_spec(tm, d))]
        epi = _epi_residual_norm
    else:
        extras = [(x, _row_spec(tm, d)), (target, _row_spec(tm, d))]
        outs = [(SDS((t, d), F32), _row_spec(tm, d)), (SDS((8, LANES), F32), _const_spec((8, LANES))),
                (SDS((t, d), BF16), _row_spec(tm, d))]
        epi = _epi_residual_loss
    return _matmul(name, (t // tm,), [(a, _row_spec(tm, k), _whole, w, _const_spec((k, d)), _whole, 0)], extras, outs, NN, epi)


def _ffn_in(name, h, wg, wu, layer, ride=None):
    t, d = h.shape
    n4 = wg.shape[-2]
    tm = TILES["ffn_in"]

    def epi(accs, ex, outs, pids):
        gate, up = accs
        s = _sigmoid(gate)
        silu = gate * s
        outs[0][...] = (up * (s + silu - silu * s)).astype(BF16)
        outs[1][...] = silu.astype(BF16)
        outs[2][...] = (silu * up).astype(BF16)

    w_spec = pl.BlockSpec((None, None, n4, d), lambda p, i: (p, layer, 0, 0))
    h_spec = pl.BlockSpec((tm, d), lambda p, i: (i, 0))
    o = (SDS((N_CHIPS, t, n4), BF16), pl.BlockSpec((None, tm, n4), lambda p, i: (p, i, 0)))
    return _matmul(name, (N_CHIPS, t // tm),
                   [(h, h_spec, _whole, wg, w_spec, _whole, 0), (h, h_spec, _whole, wu, w_spec, _whole, 1)], [],
                   [o, o, o], NT, epi, n_acc=2, ride=ride)


def _ffn_out(name, act, wd, layer, x, g_next=None, target=None, ride=None):
    _, t, n4 = act.shape
    d = wd.shape[-1]
    tm = TILES["ffn_out"]
    xs = _row_spec(tm, d)
    if target is None:
        extras = [(x, xs), (g_next, _const_spec((1, d)))]
        outs = [(SDS((t, d), F32), xs), (SDS((t, d), BF16), xs)]
        epi = _epi_residual_norm
    else:
        extras = [(x, xs), (target, xs)]
        outs = [(SDS((t, d), F32), xs), (SDS((8, LANES), F32), _const_spec((8, LANES))), (SDS((t, d), BF16), xs)]
        epi = _epi_residual_loss
    return _matmul(
        name, (t // tm,),
        [(act, pl.BlockSpec((N_CHIPS, tm, n4), lambda i: (0, i, 0)), _slab,
          wd, pl.BlockSpec((N_CHIPS, None, n4, d), lambda i: (0, layer, 0, 0)), _slab, 0)],
        extras, outs, NN, epi, slabs=N_CHIPS, ride=ride)


def _ffn_dact(name, g, wd, layer, gate, up, ride=None):
    t, d = g.shape
    n4 = wd.shape[-2]
    tm = TILES["ffn_dact"]

    def body(g_ref, w_ref, gate_ref, up_ref, dgate_ref, dup_ref):
        gv = g_ref[...]
        for p in range(N_CHIPS):
            dact = _dot(gv, w_ref[p], NT)
            dgate_ref[p] = (dact * gate_ref[p].astype(F32)).astype(BF16)
            dup_ref[p] = (dact * up_ref[p].astype(F32)).astype(BF16)

    blk = pl.BlockSpec((N_CHIPS, tm, n4), lambda i: (0, i, 0))
    return _call(
        body, grid=(t // tm,),
        in_specs=[_row_spec(tm, d), pl.BlockSpec((N_CHIPS, None, n4, d), lambda i: (0, layer, 0, 0)), blk, blk],
        out_specs=[blk, blk], out_shape=[SDS((N_CHIPS, t, n4), BF16)] * 2, operands=[g, wd, gate, up], name=name, ride=ride)


def _copy_epi(accs, ex, outs, pids):
    for a, o in zip(accs, outs):
        o[...] = a.astype(o.dtype)


def _dgrad_cols(name, dz_list, w_list, layer, three_d, x, g, res, bf16_copy=True, w_rows=False, ride=None):
    t, d = x.shape
    n4 = w_list[0].shape[-2 if w_rows else -1]
    tm = TILES["dgrad_cols"]
    if three_d:
        zs, z_pick = pl.BlockSpec((N_CHIPS, tm, n4), lambda i: (0, i, 0)), _slab
    else:
        zs, z_pick = _row_spec(tm, N_CHIPS * n4), (lambda ref, p: ref[:, p * n4:(p + 1) * n4])
    ws = pl.BlockSpec((N_CHIPS, None) + ((n4, d) if w_rows else (d, n4)), lambda i: (0, layer, 0, 0))
    xs = _row_spec(tm, d)
    return _matmul(
        name, (t // tm,), [(dz, zs, z_pick, w, ws, _slab, 0) for dz, w in zip(dz_list, w_list)],
        [(x, xs), (g, _const_spec((1, d))), (res, xs)],
        [(SDS((t, d), F32), xs), (SDS((1, d), F32), _const_spec((1, d)))] + ([(SDS((t, d), BF16), xs)] if bf16_copy else []),
        NN if w_rows else NT, _epi_rms_bwd, slabs=N_CHIPS, ride=ride)


def _dgrad_rows(name, g, w):
    t, d = g.shape
    k = w.shape[0]
    tm = TILES["dgrad_rows"]
    return _matmul(name, (t // tm,), [(g, _row_spec(tm, d), _whole, w, _const_spec((k, d)), _whole, 0)], [],
                   [(SDS((t, k), F32), _row_spec(tm, k))], NT, _copy_epi)[0]


A_TILE = 256


def _a_common(p_ref, lg_ref, lb_ref):
    pv = p_ref[...].astype(F32)
    a = _gelu(pv)
    u, v = a[:, :A_WIDTH], a[:, A_WIDTH:]
    vc = v - jnp.mean(v, axis=-1, keepdims=True)
    rs = lax.rsqrt(jnp.mean(vc * vc, axis=-1, keepdims=True) + EPS)
    vhat = vc * rs
    vn = vhat * lg_ref[...] + lb_ref[...]
    return pv, u, vhat, rs, vn.astype(BF16)


def _tril_weights(w_ref, g):
    r = lax.broadcasted_iota(jnp.int32, (CHUNK, CHUNK), 0)
    c = lax.broadcasted_iota(jnp.int32, (CHUNK, CHUNK), 1)
    return jnp.where(c <= r, w_ref[g], 0.0).astype(BF16), c <= r


def _mixer_a_fwd(proj, lg, lb, w, bias_t):
    t = proj.shape[0]

    def body(p_ref, lg_ref, lb_ref, w_ref, bt_ref, o_ref):
        _, u, _, _, vnb = _a_common(p_ref, lg_ref, lb_ref)
        for g in range(A_GROUPS):
            wt, _ = _tril_weights(w_ref, g)
            cs = slice(g * CHUNK, (g + 1) * CHUNK)
            for ch in range(A_TILE // CHUNK):
                rs_ = slice(ch * CHUNK, (ch + 1) * CHUNK)
                mixed = _dot(wt, vnb[rs_, cs], NN) + bt_ref[:, g:g + 1]
                o_ref[rs_, cs] = (u[rs_, cs] * mixed).astype(BF16)

    return pl.pallas_call(
        body, grid=(t // A_TILE,),
        in_specs=[pl.BlockSpec((A_TILE, 2 * A_WIDTH), lambda i: (i, 0)), _const_spec((1, A_WIDTH)),
                  _const_spec((1, A_WIDTH)), _const_spec((A_GROUPS, CHUNK, CHUNK)), _const_spec((CHUNK, A_GROUPS))],
        out_specs=pl.BlockSpec((A_TILE, A_WIDTH), lambda i: (i, 0)), out_shape=SDS((t, A_WIDTH), BF16),
        name="mixer_a_fwd", compiler_params=_params())(proj, lg, lb, w, bias_t)


def _mixer_a_bwd(proj, dcat, lg, lb, w, bias_t):
    t = proj.shape[0]

    def body(p_ref, da_ref, lg_ref, lb_ref, w_ref, bt_ref, dp_ref, dw_ref, dbt_ref, dlg_ref, dlb_ref, du_scr, dvn_scr):
        @pl.when(pl.program_id(0) == 0)
        def _():
            dw_ref[...] = jnp.zeros_like(dw_ref)
            dbt_ref[...] = jnp.zeros_like(dbt_ref)
            dlg_ref[...] = jnp.zeros_like(dlg_ref)
            dlb_ref[...] = jnp.zeros_like(dlb_ref)

        pv, u, vhat, rs, vnb = _a_common(p_ref, lg_ref, lb_ref)
        da = da_ref[...]
        for g in range(A_GROUPS):
            wt, keep = _tril_weights(w_ref, g)
            cs = slice(g * CHUNK, (g + 1) * CHUNK)
            for ch in range(A_TILE // CHUNK):
                rs_ = slice(ch * CHUNK, (ch + 1) * CHUNK)
                vg = vnb[rs_, cs]
                mixed = _dot(wt, vg, NN) + bt_ref[:, g:g + 1]
                du_scr[rs_, cs] = da[rs_, cs] * mixed
                dmx = da[rs_, cs] * u[rs_, cs]
                dw_ref[g] += jnp.where(keep, _dot(dmx, vg, NT), 0.0)
                dvn_scr[rs_, cs] = _dot(wt, dmx, TN)
                dbt_ref[:, g:g + 1] += jnp.sum(dmx, axis=1, keepdims=True)
        dvn = dvn_scr[...]
        dlg_ref[...] += jnp.sum(dvn * vhat, axis=0, keepdims=True)
        dlb_ref[...] += jnp.sum(dvn, axis=0, keepdims=True)
        dvh = dvn * lg_ref[...]
        dv = rs * (dvh - jnp.mean(dvh, axis=-1, keepdims=True) - vhat * jnp.mean(dvh * vhat, axis=-1, keepdims=True))
        gp = _gelu_grad(pv)
        dp_ref[:, :A_WIDTH] = (du_scr[...] * gp[:, :A_WIDTH]).astype(BF16)
        dp_ref[:, A_WIDTH:] = (dv * gp[:, A_WIDTH:]).astype(BF16)

    return pl.pallas_call(
        body, grid=(t // A_TILE,),
        in_specs=[pl.BlockSpec((A_TILE, 2 * A_WIDTH), lambda i: (i, 0)), pl.BlockSpec((A_TILE, A_WIDTH), lambda i: (i, 0)),
                  _const_spec((1, A_WIDTH)), _const_spec((1, A_WIDTH)), _const_spec((A_GROUPS, CHUNK, CHUNK)),
                  _const_spec((CHUNK, A_GROUPS))],
        out_specs=[pl.BlockSpec((A_TILE, 2 * A_WIDTH), lambda i: (i, 0)), _const_spec((A_GROUPS, CHUNK, CHUNK)),
                   _const_spec((CHUNK, A_GROUPS)), _const_spec((1, A_WIDTH)), _const_spec((1, A_WIDTH))],
        out_shape=[SDS((t, 2 * A_WIDTH), BF16), SDS((A_GROUPS, CHUNK, CHUNK), F32), SDS((CHUNK, A_GROUPS), F32),
                   SDS((1, A_WIDTH), F32), SDS((1, A_WIDTH), F32)],
        scratch_shapes=[pltpu.VMEM((A_TILE, A_WIDTH), F32), pltpu.VMEM((A_TILE, A_WIDTH), F32)],
        name="mixer_a_bwd", compiler_params=_params())(proj, dcat, lg, lb, w, bias_t)


def _rope_tables(t):
    half = ROPE_DIM // 2
    inv_freq = ROPE_THETA ** (-jnp.arange(half, dtype=F32) * 2.0 / ROPE_DIM)
    ang = jnp.arange(t, dtype=F32)[:, None] * inv_freq[None, :]
    cos, sin = jnp.cos(ang), jnp.sin(ang)
    one = jnp.ones((t, HEAD_DIM - ROPE_DIM), F32)
    zero = jnp.zeros((t, HEAD_DIM - ROPE_DIM), F32)
    zh = jnp.zeros((t, half), F32)
    c = jnp.concatenate([cos, cos, one], axis=1)
    s1 = jnp.concatenate([-sin, zh, zero], axis=1)
    s2 = jnp.concatenate([zh, sin, zero], axis=1)
    return tuple(jnp.tile(a, (1, LANES // HEAD_DIM)) for a in (c, s1, s2))


QK_TILE = 512
QK_ROWS = 64
QK_COLS = 2 * N_DIL * B_WIDTH


CHUNKS = B_WIDTH // LANES


def _regroup_out(scr, first, out_ref, rate, tile):
    rows = tile // rate
    for rho in range(rate):
        for c in range(CHUNKS):
            out_ref[rho, :, c * LANES:(c + 1) * LANES] = scr[first + c, pl.ds(rho, rows, stride=rate), :].astype(out_ref.dtype)


def _regroup_in(x_ref, scr, rate, tile):
    rows = tile // rate
    for rho in range(rate):
        for c in range(CHUNKS):
            scr[c, pl.ds(rho, rows, stride=rate), :] = x_ref[rho, :, c * LANES:(c + 1) * LANES].astype(F32)


def _regrouped_spec(rate, tile):
    return pl.BlockSpec((rate, tile // rate, B_WIDTH), lambda i, *_: (0, i, 0))


def _qk_fwd(proj, gains, tabs, ride=None):
    t = proj.shape[0]
    col0 = 2 * A_WIDTH // 1024
    r1, r2 = DIL_RATES[1], DIL_RATES[2]

    def body(p_ref, g_ref, c_ref, s1_ref, s2_ref, o_ref, q1_ref, q2_ref, k1_ref, k2_ref, scr):
        seg = _segment_mean_matrix(HEAD_DIM)
        for r0 in range(0, QK_TILE, QK_ROWS):
            rows = slice(r0, r0 + QK_ROWS)
            c, s1, s2 = c_ref[rows, :], s1_ref[rows, :], s2_ref[rows, :]
            for ci in range(1024 // LANES):
                ls = slice(ci * LANES, (ci + 1) * LANES)
                xv = p_ref[rows, ls].astype(F32)
                r = lax.rsqrt(_segment_dot(xv * xv, seg) + EPS)
                y = xv * r * g_ref[:, ls]
                val = y * c + pltpu.roll(y, LANES - 8, axis=1) * s1 + pltpu.roll(y, 8, axis=1) * s2
                o_ref[rows, ls] = val.astype(BF16)
                scr[ci, rows, :] = val

        j = pl.program_id(1)

        @pl.when(j == 0)
        def _():
            _regroup_out(scr, CHUNKS, q1_ref, r1, QK_TILE)

        @pl.when(j == 1)
        def _():
            _regroup_out(scr, 0, q2_ref, r2, QK_TILE)

        @pl.when(j == 2)
        def _():
            _regroup_out(scr, 0, k1_ref, r1, QK_TILE)
            _regroup_out(scr, CHUNKS, k2_ref, r2, QK_TILE)

    tab = pl.BlockSpec((QK_TILE, LANES), lambda i, j: (i, 0))
    g1, g2 = SDS((r1, t // r1, B_WIDTH), BF16), SDS((r2, t // r2, B_WIDTH), BF16)
    s1_, s2_ = _regrouped_spec(r1, QK_TILE), _regrouped_spec(r2, QK_TILE)
    return _call(
        body, grid=(t // QK_TILE, QK_COLS // 1024),
        in_specs=[pl.BlockSpec((QK_TILE, 1024), lambda i, j: (i, col0 + j)), pl.BlockSpec((1, 1024), lambda i, j: (0, j)),
                  tab, tab, tab],
        out_specs=[pl.BlockSpec((QK_TILE, 1024), lambda i, j: (i, j)), s1_, s2_, s1_, s2_],
        out_shape=[SDS((t, QK_COLS), BF16), g1, g2, g1, g2],
        scratch_shapes=[pltpu.VMEM((2 * CHUNKS, QK_TILE, LANES), F32)],
        operands=[proj, gains, *tabs], name="qk_norm_rope_fwd", ride=ride)


PERM_TILE = 512


def _permute(name, items, rate):
    t = items[0][0].shape[0]
    n = len(items)

    def body(*refs):
        scr = refs[-1]
        for x_ref, o_ref in zip(refs[:n], refs[n:2 * n]):
            for ci in range(CHUNKS):
                scr[ci] = x_ref[:, ci * LANES:(ci + 1) * LANES].astype(F32)
            _regroup_out(scr, 0, o_ref, rate, PERM_TILE)

    return pl.pallas_call(
        body, grid=(t // PERM_TILE,),
        in_specs=[pl.BlockSpec((PERM_TILE, B_WIDTH), functools.partial(lambda cb, i: (i, cb), cb)) for _, cb in items],
        out_specs=[_regrouped_spec(rate, PERM_TILE) for _ in items],
        out_shape=[SDS((rate, t // rate, B_WIDTH), a.dtype) for a, _ in items],
        scratch_shapes=[pltpu.VMEM((CHUNKS, PERM_TILE, LANES), F32)],
        name=name, compiler_params=_params())(*[a for a, _ in items])


def _head_lane_mask(h):
    lane = lax.broadcasted_iota(jnp.int32, (1, LANES), 1)
    return (lane < HEAD_DIM) if h == 0 else (lane >= HEAD_DIM)


def _attn_fwd(name, q, k, v, ride=None):
    rate, length = q[0].shape[0], q[0].shape[1]
    nb = length // ATT_BLOCK
    scale = HEAD_DIM ** -0.5

    def body(q_ref, kc_ref, kp_ref, vc_ref, vp_ref, o_ref, l_ref):
        n = pl.program_id(1)
        qi = lax.broadcasted_iota(jnp.int32, (ATT_BLOCK, 2 * ATT_BLOCK), 0)
        cj = lax.broadcasted_iota(jnp.int32, (ATT_BLOCK, 2 * ATT_BLOCK), 1)
        has_prev = jnp.where(n > 0, 0, 2 * ATT_BLOCK)
        mask = ((cj < ATT_BLOCK) & (cj >= qi + has_prev)) | ((cj >= ATT_BLOCK) & (cj - ATT_BLOCK <= qi))
        heads = [(hp, h) for hp in range(CHUNKS) for h in range(2)]
        q2, k2, v2 = {}, {}, {}
        for hp in range(CHUNKS):
            ls = slice(hp * LANES, (hp + 1) * LANES)
            q2[hp] = q_ref[:, ls]
            k2[hp] = jnp.concatenate([kp_ref[:, ls], kc_ref[:, ls]], axis=0)
            v2[hp] = jnp.concatenate([vp_ref[:, ls], vc_ref[:, ls]], axis=0)
        scores = {}
        for hp, h in heads:
            scores[hp, h] = _dot(jnp.where(_head_lane_mask(h), q2[hp], jnp.zeros_like(q2[hp])), k2[hp], NT) * scale
        probs, lses = {}, {}
        for hp, h in heads:
            s = jnp.where(mask, scores[hp, h], NEG_INF)
            m = jnp.max(s, axis=1, keepdims=True)
            p = jnp.exp(s - m)
            den = jnp.sum(p, axis=1, keepdims=True)
            lses[hp, h] = m + jnp.log(den)
            probs[hp, h] = (p / den).astype(BF16)
        for hp in range(CHUNKS):
            ls = slice(hp * LANES, (hp + 1) * LANES)
            o_acc = None
            for h in range(2):
                o = _dot(probs[hp, h], jnp.where(_head_lane_mask(h), v2[hp], jnp.zeros_like(v2[hp])), NN)
                o_acc = o if o_acc is None else o_acc + o
            o_ref[:, ls] = o_acc
            zeros = jnp.zeros((ATT_BLOCK, LANES), F32)
            l_ref[:, ls] = jnp.where(_head_lane_mask(1), lses[hp, 1] + zeros, lses[hp, 0] + zeros)

    def cur(cb):
        return pl.BlockSpec((None, ATT_BLOCK, B_WIDTH), lambda r, n: (r, n, cb))

    def prev(cb):
        return pl.BlockSpec((None, ATT_BLOCK, B_WIDTH), lambda r, n: (r, jnp.maximum(n - 1, 0), cb))

    out = pl.BlockSpec((None, ATT_BLOCK, B_WIDTH), lambda r, n: (r, n, 0))
    return _call(
        body, grid=(rate, nb),
        in_specs=[cur(q[1]), cur(k[1]), prev(k[1]), cur(v[1]), prev(v[1])],
        out_specs=[out, out], out_shape=[SDS((rate, length, B_WIDTH), F32)] * 2,
        operands=[q[0], k[0], k[0], v[0], v[0]], name=name, ride=ride)


def _attn_merge(a_out, o_list, l_list):
    t = a_out.shape[0]
    tm = PERM_TILE
    r1, r2 = DIL_RATES[1], DIL_RATES[2]

    def body(a_ref, o0, o1, o2, l0, l1, l2, cat_ref, lt_ref, lt1_ref, lt2_ref, so1, so2, sl1, sl2, slt):
        _regroup_in(o1, so1, r1, tm)
        _regroup_in(l1, sl1, r1, tm)
        _regroup_in(o2, so2, r2, tm)
        _regroup_in(l2, sl2, r2, tm)
        cat_ref[:, :A_WIDTH] = a_ref[...]
        for c in range(CHUNKS):
            ls = slice(c * LANES, (c + 1) * LANES)
            lg = [l0[:, ls], sl1[c], sl2[c]]
            m = jnp.maximum(jnp.maximum(lg[0], lg[1]), lg[2])
            es = [jnp.exp(l - m) for l in lg]
            den = es[0] + es[1] + es[2]
            b = (es[0] * o0[:, ls] + es[1] * so1[c] + es[2] * so2[c]) / den
            cat_ref[:, A_WIDTH + c * LANES:A_WIDTH + (c + 1) * LANES] = b.astype(BF16)
            lt = m + jnp.log(den)
            lt_ref[:, ls] = lt
            slt[c] = lt
        _regroup_out(slt, 0, lt1_ref, r1, tm)
        _regroup_out(slt, 0, lt2_ref, r2, tm)

    blk = _row_spec(tm, B_WIDTH)
    g1, g2 = _regrouped_spec(r1, tm), _regrouped_spec(r2, tm)
    return pl.pallas_call(
        body, grid=(t // tm,), in_specs=[blk, blk, g1, g2, blk, g1, g2],
        out_specs=[_row_spec(tm, A_WIDTH + B_WIDTH), blk, g1, g2],
        out_shape=[SDS((t, A_WIDTH + B_WIDTH), BF16), SDS((t, B_WIDTH), F32), SDS((r1, t // r1, B_WIDTH), F32),
                   SDS((r2, t // r2, B_WIDTH), F32)],
        scratch_shapes=[pltpu.VMEM((CHUNKS, tm, LANES), F32)] * 5,
        name="attn_merge", compiler_params=_params())(a_out, *o_list, *l_list)


def _attn_bwd_prep(dcat, cat):
    t = dcat.shape[0]
    tm = PERM_TILE
    r1, r2 = DIL_RATES[1], DIL_RATES[2]

    def body(d_ref, b_ref, db_ref, dd_ref, db1_ref, dd1_ref, db2_ref, dd2_ref, sdb, sdd):
        seg = _segment_mean_matrix(HEAD_DIM, scale=1.0)
        for c in range(CHUNKS):
            ls = slice(c * LANES, (c + 1) * LANES)
            d = d_ref[:, ls]
            dsum = _segment_dot(d * b_ref[:, ls].astype(F32), seg)
            db_ref[:, ls] = d.astype(BF16)
            dd_ref[:, ls] = dsum
            sdb[c] = d
            sdd[c] = dsum
        _regroup_out(sdb, 0, db1_ref, r1, tm)
        _regroup_out(sdd, 0, dd1_ref, r1, tm)
        _regroup_out(sdb, 0, db2_ref, r2, tm)
        _regroup_out(sdd, 0, dd2_ref, r2, tm)

    right = pl.BlockSpec((tm, B_WIDTH), lambda i: (i, 1))
    blk = _row_spec(tm, B_WIDTH)
    g1, g2 = _regrouped_spec(r1, tm), _regrouped_spec(r2, tm)
    return pl.pallas_call(
        body, grid=(t // tm,), in_specs=[right, right], out_specs=[blk, blk, g1, g1, g2, g2],
        out_shape=[SDS((t, B_WIDTH), BF16), SDS((t, B_WIDTH), F32), SDS((r1, t // r1, B_WIDTH), BF16),
                   SDS((r1, t // r1, B_WIDTH), F32), SDS((r2, t // r2, B_WIDTH), BF16), SDS((r2, t // r2, B_WIDTH), F32)],
        scratch_shapes=[pltpu.VMEM((CHUNKS, tm, LANES), F32)] * 2,
        name="attn_bwd_prep", compiler_params=_params())(dcat, cat)


def _attn_bwd(name, q, k, v, db, lse, dd, ride=None):
    rate, length = db.shape[0], db.shape[1]
    nb = length // ATT_BLOCK
    scale = HEAD_DIM ** -0.5

    def body(qa_ref, qb_ref, k_ref, v_ref, dba_ref, dbb_ref, la_ref, lb_ref, da_ref, dbd_ref, dq_ref, dk_ref, dv_ref, carry):
        m = pl.program_id(1)

        @pl.when(m == 0)
        def _():
            carry[...] = jnp.zeros_like(carry)

        row = lax.broadcasted_iota(jnp.int32, (2 * ATT_BLOCK, ATT_BLOCK), 0)
        kj = lax.broadcasted_iota(jnp.int32, (2 * ATT_BLOCK, ATT_BLOCK), 1)
        no_next = jnp.where(m + 1 < nb, 0, 2 * ATT_BLOCK)
        mask = ((row < ATT_BLOCK) & (kj <= row)) | ((row >= ATT_BLOCK) & (kj >= row - ATT_BLOCK + no_next))
        heads = [(hp, h) for hp in range(CHUNKS) for h in range(2)]
        q2, db2, lse2, dd2, k2, v2 = {}, {}, {}, {}, {}, {}
        for hp in range(CHUNKS):
            ls = slice(hp * LANES, (hp + 1) * LANES)
            k2[hp], v2[hp] = k_ref[:, ls], v_ref[:, ls]
            q2[hp] = jnp.concatenate([qa_ref[:, ls], qb_ref[:, ls]], axis=0)
            db2[hp] = jnp.concatenate([dba_ref[:, ls], dbb_ref[:, ls]], axis=0)
            lse2[hp] = jnp.concatenate([la_ref[:, ls], lb_ref[:, ls]], axis=0)
            dd2[hp] = jnp.concatenate([da_ref[:, ls], dbd_ref[:, ls]], axis=0)
        km, scores, dps = {}, {}, {}
        for hp, h in heads:
            hm = _head_lane_mask(h)
            km[hp, h] = jnp.where(hm, k2[hp], jnp.zeros_like(k2[hp]))
            scores[hp, h] = _dot(q2[hp], km[hp, h], NT) * scale
            dps[hp, h] = _dot(db2[hp], jnp.where(hm, v2[hp], jnp.zeros_like(v2[hp])), NT)
        probs, dss = {}, {}
        for hp, h in heads:
            hm = _head_lane_mask(h)
            lse_col = jnp.max(jnp.where(hm, lse2[hp], NEG_INF), axis=1, keepdims=True)
            dd_col = jnp.max(jnp.where(hm, dd2[hp], NEG_INF), axis=1, keepdims=True)
            p = jnp.where(mask, jnp.exp(scores[hp, h] - lse_col), 0.0)
            probs[hp, h] = p.astype(BF16)
            dss[hp, h] = (p * (dps[hp, h] - dd_col) * scale).astype(BF16)
        for hp in range(CHUNKS):
            ls = slice(hp * LANES, (hp + 1) * LANES)
            dq_acc, dk_acc, dv_acc = None, None, None
            for h in range(2):
                hm = _head_lane_mask(h)
                dvc = _dot(probs[hp, h], jnp.where(hm, db2[hp], jnp.zeros_like(db2[hp])), TN)
                dqc = _dot(dss[hp, h], km[hp, h], NN)
                dkc = _dot(dss[hp, h], jnp.where(hm, q2[hp], jnp.zeros_like(q2[hp])), TN)
                dq_acc = dqc if dq_acc is None else dq_acc + dqc
                dk_acc = dkc if dk_acc is None else dk_acc + dkc
                dv_acc = dvc if dv_acc is None else dv_acc + dvc
            dq_ref[:, ls] = (dq_acc[:ATT_BLOCK] + carry[:, ls]).astype(BF16)
            carry[:, ls] = dq_acc[ATT_BLOCK:]
            dk_ref[:, ls] = dk_acc.astype(BF16)
            dv_ref[:, ls] = dv_acc.astype(BF16)

    def cur(cb):
        return pl.BlockSpec((None, ATT_BLOCK, B_WIDTH), lambda r, n: (r, n, cb))

    def nxt(cb):
        return pl.BlockSpec((None, ATT_BLOCK, B_WIDTH), lambda r, n: (r, jnp.minimum(n + 1, nb - 1), cb))

    out = cur(0)
    return _call(
        body, grid=(rate, nb),
        in_specs=[cur(q[1]), nxt(q[1]), cur(k[1]), cur(v[1]), cur(0), nxt(0), cur(0), nxt(0), cur(0), nxt(0)],
        out_specs=[out, out, out], out_shape=[SDS((rate, length, B_WIDTH), BF16)] * 3,
        scratch_shapes=[pltpu.VMEM((ATT_BLOCK, B_WIDTH), F32)],
        operands=[q[0], q[0], k[0], v[0], db, db, lse, lse, dd, dd], name=name, ride=ride)


AB_IN = 2 * A_WIDTH + 3 * N_DIL * B_WIDTH
ASM_TILE = 256


def _dproj_assemble(proj, d_a, dq, dk, dv, gains, tabs, ride=None):
    t = proj.shape[0]
    n_in = 3 * N_DIL

    def body(p_ref, da_ref, *rest):
        grads = rest[:n_in]
        g_ref, c_ref, s1_ref, s2_ref, o_ref, dg_ref = rest[n_in:n_in + 6]
        scratch = rest[n_in + 6:]

        @pl.when(pl.program_id(0) == 0)
        def _():
            dg_ref[...] = jnp.zeros_like(dg_ref)

        chunk = {}
        k_scr = 0
        for j in range(n_in):
            g = j % N_DIL
            if DIL_RATES[g] == 1:
                for ci in range(CHUNKS):
                    chunk[j, ci] = functools.partial(lambda r, ci: r[:, ci * LANES:(ci + 1) * LANES].astype(F32), grads[j], ci)
            else:
                scr = scratch[k_scr]
                k_scr += 1
                _regroup_in(grads[j], scr, DIL_RATES[g], ASM_TILE)
                for ci in range(CHUNKS):
                    chunk[j, ci] = functools.partial(lambda s, ci: s[ci], scr, ci)

        seg = _segment_mean_matrix(HEAD_DIM)
        c, s1, s2 = c_ref[...], s1_ref[...], s2_ref[...]
        o_ref[:, :2 * A_WIDTH] = da_ref[...]
        for jg in range(2 * N_DIL):
            for ci in range(CHUNKS):
                col = jg * B_WIDTH + ci * LANES
                src = slice(2 * A_WIDTH + col, 2 * A_WIDTH + col + LANES)
                xv = p_ref[:, src].astype(F32)
                r = lax.rsqrt(_segment_dot(xv * xv, seg) + EPS)
                xh = xv * r
                gain = g_ref[:, col:col + LANES]
                do = chunk[jg, ci]()
                dy = do * c + pltpu.roll(do * s1, 8, axis=1) + pltpu.roll(do * s2, LANES - 8, axis=1)
                dg_ref[:, col:col + LANES] += jnp.sum(dy * xh, axis=0, keepdims=True)
                dxh = dy * gain
                o_ref[:, src] = (r * (dxh - xh * _segment_dot(dxh * xh, seg))).astype(BF16)
        v0 = 2 * A_WIDTH + QK_COLS
        for g in range(N_DIL):
            for ci in range(CHUNKS):
                col = v0 + g * B_WIDTH + ci * LANES
                o_ref[:, col:col + LANES] = chunk[2 * N_DIL + g, ci]().astype(BF16)

    specs = [_row_spec(ASM_TILE, B_WIDTH) if r == 1 else _regrouped_spec(r, ASM_TILE) for r in DIL_RATES] * 3
    n_scr = 3 * sum(1 for r in DIL_RATES if r > 1)
    tab = _row_spec(ASM_TILE, LANES)
    return _call(
        body, grid=(t // ASM_TILE,),
        in_specs=[_row_spec(ASM_TILE, AB_IN), _row_spec(ASM_TILE, 2 * A_WIDTH)] + specs
        + [_const_spec((1, QK_COLS)), tab, tab, tab],
        out_specs=[_row_spec(ASM_TILE, AB_IN), _const_spec((1, QK_COLS))],
        out_shape=[SDS((t, AB_IN), BF16), SDS((1, QK_COLS), F32)],
        scratch_shapes=[pltpu.VMEM((CHUNKS, ASM_TILE, LANES), F32)] * n_scr,
        operands=[proj, d_a, *dq, *dk, *dv, gains, *tabs], name="dproj_assemble", ride=ride)


def _fold_heads(dg_lane):
    n = dg_lane.shape[1]

    def body(x_ref, o_ref):
        r = lax.broadcasted_iota(jnp.int32, (B_WIDTH, B_WIDTH), 0) % HEAD_DIM
        c = lax.broadcasted_iota(jnp.int32, (B_WIDTH, B_WIDTH), 1) % HEAD_DIM
        fold = jnp.where(r == c, 1.0, 0.0).astype(F32)
        for jg in range(n // B_WIDTH):
            ls = slice(jg * B_WIDTH, (jg + 1) * B_WIDTH)
            o_ref[:, ls] = _dot_hi(jnp.broadcast_to(x_ref[:, ls], (8, B_WIDTH)), fold)

    return pl.pallas_call(body, out_shape=SDS((8, n), F32), name="fold_heads", compiler_params=_params())(dg_lane)


CD_TILE = 256
TAP_ROWS = 64
CD_IN = 2 * C_WIDTH + 3 * 512


def _shifted_copies(src, dst, rows):
    dst[0, :rows] = src[...]
    for b in range(1, 8):
        dst[b, :rows - 8] = src[pl.ds(b, rows - 8), :]


def _rows_from(shifted, start, n, lanes=slice(None)):
    b = start % 8
    return shifted[b, pl.ds(start - b, n), lanes]


def _mixer_cd_fwd(proj, cw, cb, lg, lb, dw):
    t = proj.shape[0]
    per = CD_TILE // HALO

    def body(h_ref, m_ref, cw_ref, cb_ref, lg_ref, lb_ref, dw_ref, o_ref, c1_ref, c_scr, e_scr, c_sh):
        not_first = (pl.program_id(0) > 0).astype(F32)
        lanes = [slice(c * LANES, (c + 1) * LANES) for c in range(C_WIDTH // LANES)]

        def col(ref, part, ls):
            return ref[:, part * C_WIDTH + ls.start:part * C_WIDTH + ls.stop].astype(F32)

        for ls in lanes:
            c_scr[:HALO, ls] = col(h_ref, 0, ls) * _sigmoid(col(h_ref, 1, ls)) * not_first
            c_scr[HALO:, ls] = col(m_ref, 0, ls) * _sigmoid(col(m_ref, 1, ls))
            e_scr[:HALO, ls] = col(h_ref, 3, ls) * col(h_ref, 4, ls) * not_first
            e_scr[HALO:, ls] = col(m_ref, 3, ls) * col(m_ref, 4, ls)
        _shifted_copies(c_scr, c_sh, HALO + CD_TILE)
        for ls in lanes:
            for r0 in range(0, CD_TILE, TAP_ROWS):
                acc = jnp.zeros((TAP_ROWS, LANES), F32)
                for k in range(C_KERNEL):
                    acc = acc + cw_ref[k:k + 1, ls] * _rows_from(c_sh, r0 + HALO - (C_KERNEL - 1) + k, TAP_ROWS, ls)
                c1_ref[r0:r0 + TAP_ROWS, ls] = acc + cb_ref[:, ls]
        mean = sum(jnp.sum(c1_ref[:, ls], axis=-1, keepdims=True) for ls in lanes) * (1.0 / C_WIDTH)
        var = sum(jnp.sum((c1_ref[:, ls] - mean) ** 2, axis=-1, keepdims=True) for ls in lanes) * (1.0 / C_WIDTH)
        rs = lax.rsqrt(var + EPS)
        for ls in lanes:
            c2 = (c1_ref[:, ls] - mean) * rs * lg_ref[:, ls] + lb_ref[:, ls]
            o_ref[:, ls] = (c2 * _sigmoid(c2)).astype(BF16)
            d1 = jnp.zeros((CD_TILE, LANES), F32)
            for k in range(D_KERNEL):
                d1 = d1 + dw_ref[k:k + 1, ls] * e_scr[pl.ds(HALO - (D_KERNEL - 1) + k, CD_TILE), ls]
            o_ref[:, C_WIDTH + ls.start:C_WIDTH + ls.stop] = (col(m_ref, 2, ls) * d1).astype(BF16)

    return pl.pallas_call(
        body, grid=(t // CD_TILE,),
        in_specs=[pl.BlockSpec((HALO, CD_IN), lambda i: (jnp.maximum(i * per - 1, 0), 0)), _row_spec(CD_TILE, CD_IN),
                  _const_spec((32, C_WIDTH)), _const_spec((1, C_WIDTH)), _const_spec((1, C_WIDTH)), _const_spec((1, C_WIDTH)),
                  _const_spec((8, C_WIDTH))],
        out_specs=[_row_spec(CD_TILE, 2 * C_WIDTH), _row_spec(CD_TILE, C_WIDTH)],
        out_shape=[SDS((t, 2 * C_WIDTH), BF16), SDS((t, C_WIDTH), F32)],
        scratch_shapes=[pltpu.VMEM((HALO + CD_TILE, C_WIDTH), F32)] * 2 + [pltpu.VMEM((8, HALO + CD_TILE, C_WIDTH), F32)],
        name="mixer_cd_fwd", compiler_params=_params())(proj, proj, cw, cb, lg, lb, dw)


def _mixer_cd_bwd(proj, dcat, c1, cw, lg, lb, dw, ride=None):
    t = proj.shape[0]
    per = CD_TILE // HALO
    nt = t // CD_TILE
    ext = CD_TILE + HALO

    def body(hp_ref, m_ref, hn_ref, dm_ref, dn_ref, c1m_ref, c1n_ref, cw_ref, lg_ref, lb_ref, dw_ref,
             dp_ref, dcw_ref, dcb_ref, dlg_ref, dlb_ref, ddw_ref, c_scr, e_scr, dc1_scr, dd1_scr, c_sh, dc1_sh, dcw_acc,
             dvh_scr, vhat_scr):
        i = pl.program_id(0)

        @pl.when(i == 0)
        def _():
            for r in (dcw_acc, dcb_ref, dlg_ref, dlb_ref, ddw_ref):
                r[...] = jnp.zeros_like(r)

        not_first = (i > 0).astype(F32)
        not_last = (i < nt - 1).astype(F32)
        main = slice(HALO, HALO + CD_TILE)
        lanes = [slice(c * LANES, (c + 1) * LANES) for c in range(C_WIDTH // LANES)]

        def col(ref, part, ls):
            return ref[:, part * C_WIDTH + ls.start:part * C_WIDTH + ls.stop].astype(F32)

        for ls in lanes:
            c_scr[:HALO, ls] = col(hp_ref, 0, ls) * _sigmoid(col(hp_ref, 1, ls)) * not_first
            c_scr[main, ls] = col(m_ref, 0, ls) * _sigmoid(col(m_ref, 1, ls))
            c_scr[HALO + CD_TILE:, ls] = col(hn_ref, 0, ls) * _sigmoid(col(hn_ref, 1, ls)) * not_last
            e_scr[:HALO, ls] = col(hp_ref, 3, ls) * col(hp_ref, 4, ls) * not_first
            e_scr[main, ls] = col(m_ref, 3, ls) * col(m_ref, 4, ls)
            e_scr[HALO + CD_TILE:, ls] = col(hn_ref, 3, ls) * col(hn_ref, 4, ls) * not_last
        _shifted_copies(c_scr, c_sh, 2 * HALO + CD_TILE)

        def c1_of(ls):
            return jnp.concatenate([c1m_ref[:, ls], c1n_ref[:, ls]], axis=0)

        mean = sum(jnp.sum(c1_of(ls), axis=-1, keepdims=True) for ls in lanes) * (1.0 / C_WIDTH)
        var = sum(jnp.sum((c1_of(ls) - mean) ** 2, axis=-1, keepdims=True) for ls in lanes) * (1.0 / C_WIDTH)
        rs = lax.rsqrt(var + EPS)
        sum_dvh, sum_dvh_vhat = 0.0, 0.0
        for ls in lanes:
            vhat = (c1_of(ls) - mean) * rs
            c2 = vhat * lg_ref[:, ls] + lb_ref[:, ls]
            sig = _sigmoid(c2)
            dc = jnp.concatenate([dm_ref[:, ls], dn_ref[:, ls] * not_last], axis=0)
            dc2 = dc * (sig * (1.0 + c2 * (1.0 - sig)))
            dvh = dc2 * lg_ref[:, ls]
            sum_dvh = sum_dvh + jnp.sum(dvh, axis=-1, keepdims=True)
            sum_dvh_vhat = sum_dvh_vhat + jnp.sum(dvh * vhat, axis=-1, keepdims=True)
            dvh_scr[:, ls] = dvh
            vhat_scr[:, ls] = vhat
            dlg_ref[:, ls] += jnp.sum((dc2 * vhat)[:CD_TILE], axis=0, keepdims=True)
            dlb_ref[:, ls] += jnp.sum(dc2[:CD_TILE], axis=0, keepdims=True)
        for ls in lanes:
            dc1 = rs * (dvh_scr[:, ls] - sum_dvh * (1.0 / C_WIDTH) - vhat_scr[:, ls] * (sum_dvh_vhat * (1.0 / C_WIDTH)))
            dc1_scr[:, ls] = dc1
            dcb_ref[:, ls] += jnp.sum(dc1[:CD_TILE], axis=0, keepdims=True)
        _shifted_copies(dc1_scr, dc1_sh, ext)
        for ls in lanes:
            for r0 in range(0, CD_TILE, TAP_ROWS):
                rows = slice(r0, r0 + TAP_ROWS)
                dc1_m = dc1_scr[rows, ls]
                dc0 = jnp.zeros((TAP_ROWS, LANES), F32)
                for k in range(C_KERNEL):
                    dc0 = dc0 + cw_ref[k:k + 1, ls] * _rows_from(dc1_sh, r0 + C_KERNEL - 1 - k, TAP_ROWS, ls)
                    prod = dc1_m * _rows_from(c_sh, r0 + HALO - (C_KERNEL - 1) + k, TAP_ROWS, ls)
                    dcw_acc[k, :, ls] += prod.reshape(TAP_ROWS // 8, 8, LANES).sum(axis=0)
                g_m = m_ref[rows, C_WIDTH + ls.start:C_WIDTH + ls.stop].astype(F32)
                a_m = m_ref[rows, ls].astype(F32)
                sig_m = _sigmoid(g_m)
                dp_ref[rows, ls] = (dc0 * sig_m).astype(BF16)
                dp_ref[rows, C_WIDTH + ls.start:C_WIDTH + ls.stop] = (dc0 * a_m * sig_m * (1.0 - sig_m)).astype(BF16)

        @pl.when(i == nt - 1)
        def _():
            dcw_ref[...] = jnp.sum(dcw_acc[...], axis=1)

        for ls in lanes:
            wide = slice(C_WIDTH + ls.start, C_WIDTH + ls.stop)
            d1 = jnp.zeros((CD_TILE, LANES), F32)
            for k in range(D_KERNEL):
                d1 = d1 + dw_ref[k:k + 1, ls] * e_scr[pl.ds(HALO - (D_KERNEL - 1) + k, CD_TILE), ls]
            dd_m = dm_ref[:, wide]
            dd1 = jnp.concatenate([dd_m * col(m_ref, 2, ls), dn_ref[:, wide] * col(hn_ref, 2, ls) * not_last], axis=0)
            dd1_scr[:, ls] = dd1
            dp_ref[:, 2 * C_WIDTH + ls.start:2 * C_WIDTH + ls.stop] = (dd_m * d1).astype(BF16)
            de = jnp.zeros((CD_TILE, LANES), F32)
            for k in range(D_KERNEL):
                de = de + dw_ref[k:k + 1, ls] * dd1_scr[pl.ds(D_KERNEL - 1 - k, CD_TILE), ls]
                ddw_ref[k:k + 1, ls] += jnp.sum(dd1[:CD_TILE] * e_scr[pl.ds(HALO - (D_KERNEL - 1) + k, CD_TILE), ls], axis=0, keepdims=True)
            dp_ref[:, 3 * C_WIDTH + ls.start:3 * C_WIDTH + ls.stop] = (de * col(m_ref, 4, ls)).astype(BF16)
            dp_ref[:, 4 * C_WIDTH + ls.start:4 * C_WIDTH + ls.stop] = (de * col(m_ref, 3, ls)).astype(BF16)

    halo_prev = lambda i: (jnp.maximum(i * per - 1, 0), 0)
    halo_next = lambda i: (jnp.minimum((i + 1) * per, t // HALO - 1), 0)
    vec = _const_spec((1, C_WIDTH))
    return _call(
        body, grid=(nt,),
        in_specs=[pl.BlockSpec((HALO, CD_IN), halo_prev), _row_spec(CD_TILE, CD_IN), pl.BlockSpec((HALO, CD_IN), halo_next),
                  _row_spec(CD_TILE, 2 * C_WIDTH), pl.BlockSpec((HALO, 2 * C_WIDTH), halo_next),
                  _row_spec(CD_TILE, C_WIDTH), pl.BlockSpec((HALO, C_WIDTH), halo_next),
                  _const_spec((32, C_WIDTH)), vec, vec, _const_spec((8, C_WIDTH))],
        out_specs=[_row_spec(CD_TILE, CD_IN), _const_spec((32, C_WIDTH)), vec, vec, vec, _const_spec((8, C_WIDTH))],
        out_shape=[SDS((t, CD_IN), BF16), SDS((32, C_WIDTH), F32), SDS((1, C_WIDTH), F32), SDS((1, C_WIDTH), F32),
                   SDS((1, C_WIDTH), F32), SDS((8, C_WIDTH), F32)],
        scratch_shapes=[pltpu.VMEM((2 * HALO + CD_TILE, C_WIDTH), F32)] * 2 + [pltpu.VMEM((ext, C_WIDTH), F32)] * 2
        + [pltpu.VMEM((8, 2 * HALO + CD_TILE, C_WIDTH), F32), pltpu.VMEM((8, ext, C_WIDTH), F32),
           pltpu.VMEM((32, 8, C_WIDTH), F32)] + [pltpu.VMEM((ext, C_WIDTH), F32)] * 2,
        operands=[proj, proj, proj, dcat, dcat, c1, c1, cw, lg, lb, dw], name="mixer_cd_bwd", ride=ride)


def _wgrad(name, pairs, out_rc, t, ride):
    tk = TILES["wgrad"]
    assert tk == t, "the whole contraction has to fit one grid step"
    r, c = out_rc
    n = len(pairs)

    def body(*refs):
        ab, out_refs = refs[:2 * n], refs[2 * n:]
        for j in range(n):
            out_refs[j][...] = _dot(ab[2 * j][...], ab[2 * j + 1][...], TN).astype(BF16)

    operands, in_specs = [], []
    for lhs, lhs_spec, rhs, rhs_spec in pairs:
        operands += [lhs, rhs]
        in_specs += [lhs_spec, rhs_spec]
    res = _call(body, grid=(N_CHIPS, t // tk), in_specs=in_specs,
                out_specs=[pl.BlockSpec((None, r, c), lambda p, k: (p, 0, 0))] * n,
                out_shape=[SDS((N_CHIPS, r, c), BF16)] * n, operands=operands, name=name, ride=ride)
    outs, ride_res = (res, None) if ride is None else res
    outs = [o.reshape(N_CHIPS, 2, r // 2, c) for o in outs]
    return outs if ride is None else (outs, ride_res)


def _wgrad_col_sharded(name, h, dz_list, three_d, ride=None):
    t, d = h.shape
    tk = TILES["wgrad"]
    n4 = dz_list[0].shape[-1] if three_d else dz_list[0].shape[-1] // N_CHIPS
    hs = pl.BlockSpec((tk, d), lambda p, k: (k, 0))
    zs = pl.BlockSpec((None, tk, n4), lambda p, k: (p, k, 0)) if three_d else pl.BlockSpec((tk, n4), lambda p, k: (k, p))
    return _wgrad(name, [(h, hs, dz, zs) for dz in dz_list], (d, n4), t, ride)


def _wgrad_row_sharded(name, a, g, three_d, ride=None):
    many = isinstance(a, (list, tuple))
    a_list = list(a) if many else [a]
    t, d = g.shape
    tk = TILES["wgrad"]
    k4 = a_list[0].shape[-1] if three_d else a_list[0].shape[-1] // N_CHIPS
    a_spec = pl.BlockSpec((None, tk, k4), lambda p, k: (p, k, 0)) if three_d else pl.BlockSpec((tk, k4), lambda p, k: (k, p))
    gs = pl.BlockSpec((tk, d), lambda p, k: (k, 0))
    res = _wgrad(name, [(a_j, a_spec, g, gs) for a_j in a_list], (k4, d), t, ride)
    if many:
        return res
    return res[0] if ride is None else (res[0][0], res[1])


def _mesh_scalars():
    return jnp.stack([lax.axis_index("c"), 2 * lax.axis_index("x") + lax.axis_index("y")]).astype(jnp.int32)


def _stage_own(name, w, layer, dtype, far_slab=False):
    layers, r, cols = w.shape
    h = r // 2

    def body(s_ref, x_ref, o_ref, *unwritten):
        o_ref[...] = x_ref[...].astype(dtype)

    out_specs = [pl.BlockSpec((None, None, h, cols), lambda i, s: (s[1], i, 0, 0))] + [ANY] * far_slab
    out_shape = [SDS((N_CHIPS, 2, h, cols), dtype)] + [SDS((2, h, cols), dtype)] * far_slab
    res = pl.pallas_call(
        body,
        grid_spec=pltpu.PrefetchScalarGridSpec(
            num_scalar_prefetch=1, grid=(2,),
            in_specs=[pl.BlockSpec((None, h, cols), lambda i, s: (2 * layer + i, 0, 0))], out_specs=out_specs),
        out_shape=out_shape, name=name,
        compiler_params=_params())(_mesh_scalars(), w.reshape(2 * layers, h, cols))
    return tuple(res) if far_slab else res[0]


STAGE_STEPS = 4


def _stage_rest_and_norm(x, g, weights, ride=None):
    t, d = x.shape
    n = len(weights)
    views, in_specs, out_specs, out_shapes = [], [], [], []
    for w, layer in weights:
        layers, r, cols = w.shape
        sub = r // STAGE_STEPS
        views.append(w.reshape(layers * STAGE_STEPS, sub, cols))
        in_specs.append(pl.BlockSpec((None, sub, cols), functools.partial(lambda l, i, s: (STAGE_STEPS * l + i, 0, 0), layer)))
        out_specs.append(pl.BlockSpec((None, None, sub, cols), lambda i, s: (s[1], i // 2, i % 2, 0)))
        out_shapes.append(SDS((N_CHIPS, 2, r // 2, cols), BF16))

    def body(s_ref, x_ref, g_ref, *rest):
        w_refs, h_ref, o_refs = rest[:n], rest[n], rest[n + 1:]
        h_ref[...] = _rms_rows(x_ref[...], g_ref[...]).astype(BF16)
        for w_ref, o_ref in zip(w_refs, o_refs):
            o_ref[...] = w_ref[...].astype(BF16)

    tm = t // STAGE_STEPS
    res = _call(
        body, grid=(STAGE_STEPS,), in_specs=[pl.BlockSpec((tm, d), lambda i, s: (i, 0)), pl.BlockSpec((1, d), lambda i, s: (0, 0))] + in_specs,
        out_specs=[pl.BlockSpec((tm, d), lambda i, s: (i, 0))] + out_specs, out_shape=[SDS((t, d), BF16)] + out_shapes,
        operands=[x, g] + views, name="stage_and_norm", ride=ride, prefetch=_mesh_scalars())
    outs, ride_res = (res, None) if ride is None else res
    result = (outs[0], list(outs[1:]))
    return result if ride is None else (result, ride_res)


def _remote(src, dst, send_sem, recv_sem, device):
    return pltpu.make_async_remote_copy(src, dst, send_sem, recv_sem, device_id=device, device_id_type=MESH)


ALL_PEERS = (0, 1, 2)
NEIGHBOURS = (0, 1)


def _ride_gather_send(bufs, peers=ALL_PEERS):
    n = len(bufs)

    def each(b, sems, act):
        send, recv = sems
        x, y, c, p, others = _position()
        for t in range(n):
            for j in peers:
                qx, qy = others[j]
                act(b[t].at[p, c], b[t].at[2 * qx + qy, c], send.at[t, j], recv.at[t, j], (qx, qy, c))

    def start(ins, b, new, sems):
        each(b, sems, lambda mine, landed, s, r, dev: _remote(mine, mine, s, r, dev).start())

    def finish(ins, b, new, sems):
        def act(mine, landed, s, r, dev):
            _remote(mine, mine, s, r, dev).wait_send()
            _remote(landed, landed, s, r, dev).wait_recv()
        each(b, sems, act)

    return _Ride([], bufs, [], [(n, 3), (n, 3)], start, finish, ["chips"])


def _ride_gather_pass(bufs, peers=ALL_PEERS):
    n = len(bufs)

    def each(b, sems, act):
        send, recv = sems
        x, y, c, p, others = _position()
        for t in range(n):
            for j in peers:
                qx, qy = others[j]
                act(b[t].at[2 * qx + qy, c], b[t].at[2 * qx + qy, 1 - c], send.at[t, j], recv.at[t, j], (x, y, 1 - c))

    def start(ins, b, new, sems):
        each(b, sems, lambda landed, passed, s, r, dev: _remote(landed, landed, s, r, dev).start())

    def finish(ins, b, new, sems):
        def act(landed, passed, s, r, dev):
            _remote(landed, landed, s, r, dev).wait_send()
            _remote(passed, passed, s, r, dev).wait_recv()
        each(b, sems, act)

    return _Ride([], bufs, [], [(n, 3), (n, 3)], start, finish, ["sibling"])


def _ride_gather(bufs, peers=ALL_PEERS):
    send, onward = _ride_gather_send(bufs, peers), _ride_gather_pass(bufs, peers)
    n_send = len(send.sem_shapes)

    def start(ins, b, new, sems):
        send.start(ins, b, new, sems[:n_send])

    def finish(ins, b, new, sems):
        send.finish(ins, b, new, sems[:n_send])
        onward.start(ins, b, new, sems[n_send:])
        onward.finish(ins, b, new, sems[n_send:])

    return _Ride([], bufs, [], send.sem_shapes + onward.sem_shapes, start, finish, ["sibling", "chips"])


def _ride_gather_far(sources, slabs):
    n = len(slabs)

    def hops(ins, b, sems):
        x, y, c, p, others = _position()
        qx, qy = others[2]
        for t in range(n):
            far = (ins[t].at[p, c], b[t].at[c], sems[0].at[t], sems[1].at[t], (qx, qy, c))
            onward = (b[t].at[c], b[t].at[1 - c], sems[2].at[t], sems[3].at[t], (x, y, 1 - c))
            yield far, onward

    def wait(mine, landed, s, r, dev):
        _remote(mine, mine, s, r, dev).wait_send()
        _remote(landed, landed, s, r, dev).wait_recv()

    def start(ins, b, new, sems):
        for (mine, landed, s, r, dev), _ in hops(ins, b, sems):
            _remote(mine, landed, s, r, dev).start()

    def finish(ins, b, new, sems):
        for far, _ in hops(ins, b, sems):
            wait(*far)
        for _, (landed, passed, s, r, dev) in hops(ins, b, sems):
            _remote(landed, landed, s, r, dev).start()
        for _, onward in hops(ins, b, sems):
            wait(*onward)

    return _Ride(sources, slabs, [], [(n,)] * 4, start, finish, ["sibling", "chips"])


def _ride_swap(tensors):
    n = len(tensors)

    def each(ins, new, sems, act):
        send, recv = sems
        x, y, c, _, _ = _position()
        for t in range(n):
            act(_remote(ins[t].at[:, 1 - c], new[t], send.at[t], recv.at[t], (x, y, 1 - c)))

    def start(ins, b, new, sems):
        each(ins, new, sems, lambda cp: cp.start())

    def finish(ins, b, new, sems):
        each(ins, new, sems, lambda cp: cp.wait())

    return _Ride(tensors, [], [SDS((s.shape[0],) + s.shape[2:], s.dtype) for s in tensors], [(n,), (n,)], start, finish,
                 ["sibling"])


def _ride_scatter(tensors, landing):
    n = len(tensors)

    def each(ins, b, sems, act):
        send, recv = sems
        x, y, c, p, others = _position()
        for t in range(n):
            for j, (qx, qy) in enumerate(others):
                q = 2 * qx + qy
                act(ins[t].at[q], b[t].at[p], b[t].at[q], send.at[t, j], recv.at[t, j], (qx, qy, c))

    def start(ins, b, new, sems):
        each(ins, b, sems, lambda src, dst, landed, s, r, dev: _remote(src, dst, s, r, dev).start())

    def finish(ins, b, new, sems):
        def act(src, dst, landed, s, r, dev):
            _remote(src, dst, s, r, dev).wait_send()
            _remote(landed, landed, s, r, dev).wait_recv()
        each(ins, b, sems, act)

    return _Ride(tensors, landing, [], [(n, 3), (n, 3)], start, finish, ["chips"])


def _ride_join(bufs):
    n = len(bufs)

    def each(b, sems, act):
        send, recv = sems
        x, y, c, _, _ = _position()
        for t in range(n):
            act(b[t].at[c], b[t].at[1 - c], send.at[t], recv.at[t], (x, y, 1 - c))

    def start(ins, b, new, sems):
        each(b, sems, lambda mine, theirs, s, r, dev: _remote(mine, mine, s, r, dev).start())

    def finish(ins, b, new, sems):
        def act(mine, theirs, s, r, dev):
            _remote(mine, mine, s, r, dev).wait_send()
            _remote(theirs, theirs, s, r, dev).wait_recv()
        each(b, sems, act)

    return _Ride([], bufs, [], [(n,), (n,)], start, finish, ["sibling"])


def _all_reduce_small(pack, ride=None):
    rows = pack.shape[0]
    n_dev = 2 * N_CHIPS
    n_rb = 0 if ride is None else len(ride.bufs)

    def body(x_ref, *rest):
        o_ref = rest[n_rb]
        r_bufs = rest[n_rb + 1:2 * n_rb + 1]
        land, send, recv = rest[2 * n_rb + 1:2 * n_rb + 4]
        r_sems = rest[2 * n_rb + 4:]
        if ride is not None:
            ride.start([], r_bufs, [], r_sems)
        x, y, c, p, _ = _position()
        me = 2 * p + c
        land[me] = x_ref[...]
        peers = [(dx, dy, dc) for dx in range(2) for dy in range(2) for dc in range(2) if (dx, dy, dc) != (0, 0, 0)]
        for j, (dx, dy, dc) in enumerate(peers):
            _remote(land.at[me], land.at[me], send.at[j], recv.at[j], (x ^ dx, y ^ dy, c ^ dc)).start()
        for j, (dx, dy, dc) in enumerate(peers):
            src = 4 * (x ^ dx) + 2 * (y ^ dy) + (c ^ dc)
            _remote(land.at[me], land.at[me], send.at[j], recv.at[j], (x ^ dx, y ^ dy, c ^ dc)).wait_send()
            _remote(land.at[src], land.at[src], send.at[j], recv.at[j], (x ^ dx, y ^ dy, c ^ dc)).wait_recv()
        acc = land[0]
        for dev in range(1, n_dev):
            acc = acc + land[dev]
        o_ref[...] = acc
        if ride is not None:
            ride.finish([], r_bufs, [], r_sems)

    bufs = [] if ride is None else ride.bufs
    sems = [] if ride is None else [pltpu.SemaphoreType.DMA(s) for s in ride.sem_shapes]
    res = pl.pallas_call(
        body, in_specs=[pl.BlockSpec(memory_space=pltpu.VMEM)] + [ANY] * n_rb,
        out_specs=[pl.BlockSpec(memory_space=pltpu.VMEM)] + [ANY] * n_rb,
        out_shape=[SDS((rows, LANES), F32)] + [SDS(b.shape, b.dtype) for b in bufs],
        scratch_shapes=[pltpu.VMEM((n_dev, rows, LANES), F32), pltpu.SemaphoreType.DMA((n_dev - 1,)),
                        pltpu.SemaphoreType.DMA((n_dev - 1,))] + sems,
        input_output_aliases={1 + j: 1 + j for j in range(n_rb)},
        name="all_reduce_small", compiler_params=_params())(pack, *bufs)
    return res[0], list(res[1:])


def _add_own_half(name, full, recv, out_dtype):
    n4, _, h, cols = full.shape

    def body(s_ref, a_ref, b_ref, o_ref, own_ref):
        v = (a_ref[...].astype(F32) + b_ref[...].astype(F32)).astype(out_dtype)
        o_ref[...] = v

        @pl.when(pl.program_id(0) == s_ref[1])
        def _():
            own_ref[...] = v

    return pl.pallas_call(
        body,
        grid_spec=pltpu.PrefetchScalarGridSpec(
            num_scalar_prefetch=1, grid=(n4,),
            in_specs=[pl.BlockSpec((None, None, h, cols), lambda q, s: (q, s[0], 0, 0)),
                      pl.BlockSpec((None, h, cols), lambda q, s: (q, 0, 0))],
            out_specs=[pl.BlockSpec((None, h, cols), lambda q, s: (q, 0, 0)),
                       pl.BlockSpec((None, h, cols), lambda q, s: (s[1], 0, 0))]),
        out_shape=[SDS((n4, h, cols), out_dtype)] * 2, name=name, compiler_params=_params())(_mesh_scalars(), full, recv)


def _sum_chips(name, parts):
    n4, h, cols = parts.shape
    th = h // 4 if h % 64 == 0 else h

    def body(s_ref, a_ref, o_ref):
        acc = a_ref[0].astype(F32)
        for q in range(1, n4):
            acc = acc + a_ref[q].astype(F32)
        o_ref[...] = acc

    return pl.pallas_call(
        body,
        grid_spec=pltpu.PrefetchScalarGridSpec(
            num_scalar_prefetch=1, grid=(h // th,),
            in_specs=[pl.BlockSpec((n4, th, cols), lambda i, s: (0, i, 0))],
            out_specs=pl.BlockSpec((None, th, cols), lambda i, s: (s[0], i, 0))),
        out_shape=SDS((2, h, cols), F32), name=name, compiler_params=_params())(_mesh_scalars(), parts)


def _adamw_math(w, g, m, v):
    m2 = ADAM_B1 * m + (1.0 - ADAM_B1) * g
    v2 = ADAM_B2 * v + (1.0 - ADAM_B2) * (g * g)
    m_hat = m2 / (1.0 - ADAM_B1 ** ADAM_STEP)
    v_hat = v2 / (1.0 - ADAM_B2 ** ADAM_STEP)
    delta = -ADAM_LR * (m_hat / (jnp.sqrt(v_hat) + ADAM_EPS) + ADAM_WD * w)
    return delta, m2, v2


def _row_tile(rows, cols):
    cap = max(8, (1 << 18) // cols)
    best = 8
    for cand in range(8, min(rows, cap) + 1, 8):
        if rows % cand == 0:
            best = cand
    return best


def _adamw_big(name, w, g_layers, m, v):
    layers, rows, cols = w.shape
    tr = _row_tile(rows, cols)

    def body(w_ref, m_ref, v_ref, *rest):
        g_refs, (g_o, d_o, m_o, v_o) = rest[:layers], rest[layers:]
        gv = g_refs[0][...]
        for layer in range(1, layers):
            gv = jnp.where(pl.program_id(0) == layer, g_refs[layer][...], gv)
        d, mm, vv = _adamw_math(w_ref[...], gv, m_ref[...], v_ref[...])
        g_o[...] = gv
        d_o[...] = d
        m_o[...] = mm
        v_o[...] = vv

    blk = pl.BlockSpec((None, tr, cols), lambda l, i: (l, i, 0))
    g_blk = pl.BlockSpec((tr, cols), lambda l, i: (i, 0))
    return tuple(pl.pallas_call(
        body, grid=(layers, rows // tr), in_specs=[blk] * 3 + [g_blk] * layers, out_specs=[blk] * 4,
        out_shape=[SDS((layers, rows, cols), F32)] * 4, name=name,
        compiler_params=_params())(w, m, v, *[g.reshape(rows, cols) for g in g_layers]))


def _adamw_small(ws, gs, ms, vs):
    n = len(ws)
    flat = []
    for group in (ws, gs, ms, vs):
        flat += [a.reshape(-1, a.shape[-1]) for a in group]

    def body(*refs):
        w_r, g_r, m_r, v_r = refs[:n], refs[n:2 * n], refs[2 * n:3 * n], refs[3 * n:4 * n]
        d_o, m_o, v_o = refs[4 * n:5 * n], refs[5 * n:6 * n], refs[6 * n:7 * n]
        for j in range(n):
            d, mm, vv = _adamw_math(w_r[j][...], g_r[j][...], m_r[j][...], v_r[j][...])
            d_o[j][...] = d
            m_o[j][...] = mm
            v_o[j][...] = vv

    shapes = [SDS(a.shape, F32) for a in flat[:n]]
    outs = pl.pallas_call(body, out_shape=shapes * 3, name="adamw_small", compiler_params=_params())(*flat)
    res = []
    for k in range(3):
        res.append([outs[k * n + j].reshape(ws[j].shape) for j in range(n)])
    return res


BIG = ("ab_w_in", "ab_w_out", "cd_w_in", "cd_w_out", "ffn_w_gate", "ffn_w_up", "ffn_w_down")
V_BLOCK = (2 * A_WIDTH + QK_COLS) // B_WIDTH


def _pad_rows(a, rows):
    return jnp.pad(a, ((0, rows - a.shape[0]), (0, 0)))


A_IN, A_OUT, C_IN, C_OUT = ("ab_w_in", 0), ("ab_w_out", 0), ("cd_w_in", 0), ("cd_w_out", 0)
G0, U0, D0 = ("ffn_w_gate", 0), ("ffn_w_up", 0), ("ffn_w_down", 0)
G1, U1, D1 = ("ffn_w_gate", 1), ("ffn_w_up", 1), ("ffn_w_down", 1)
UNITS = (A_IN, A_OUT, G0, U0, D0, C_IN, C_OUT, G1, U1, D1)
ROWS_MINOR = ("ffn_w_gate", "ffn_w_up")
SMALL_SHARDED = ("small", 0)
REPLICATED_UNIT = ("replicated", 0)


class _Exchange:
    def __init__(self, enabled):
        self.enabled = enabled
        self.w, self.grad, self.recv, self.half, self.land, self.done = {}, {}, {}, {}, {}, {}
        self.far = {}

    def full(self, unit):
        b = self.w[unit]
        return b.reshape(N_CHIPS, 1, 2 * b.shape[2], b.shape[3])

    def ride_for(self, phases):
        rides, sinks = [], []
        for kind, units in phases:
            if kind == "send":
                rides.append(_ride_gather_send([self.w[u] for u in units]))
                sinks.append(self.w)
            elif kind == "pass":
                rides.append(_ride_gather_pass([self.w[u] for u in units]))
                sinks.append(self.w)
            elif kind == "gather":
                rides.append(_ride_gather([self.w[u] for u in units]))
                sinks.append(self.w)
            elif kind == "gather_near":
                rides.append(_ride_gather([self.w[u] for u in units], NEIGHBOURS))
                sinks.append(self.w)
            elif kind == "gather_far":
                rides.append(_ride_gather_far([self.w[u] for u in units], [self.far[u] for u in units]))
                sinks.append(self.far)
            elif kind == "swap":
                rides.append(_ride_swap([self.grad[u] for u in units]))
                sinks.append(self.recv)
            elif kind == "scatter":
                rides.append(_ride_scatter([self.half[u] for u in units], [self.land[u] for u in units]))
                sinks.append(self.land)
            else:
                rides.append(_ride_join([self.done[u] for u in units]))
                sinks.append(self.done)
        ride = functools.reduce(_ride_both, rides)

        def settle(res):
            n_bufs = sum(len(r.bufs) for r in rides)
            bufs, new = list(res[:n_bufs]), list(res[n_bufs:])
            for r, sink, (_, units) in zip(rides, sinks, phases):
                vals = [bufs.pop(0) for _ in r.bufs] + [new.pop(0) for _ in r.new_outs]
                for u, v in zip(units, vals):
                    sink[u] = v

        return ride, settle

    def run(self, fn, *args, phases=(), **kw):
        if not self.enabled or not phases:
            return fn(*args, **kw)
        ride, settle = self.ride_for(phases)
        out, res = fn(*args, ride=ride, **kw)
        settle(res)
        return out

    def alone(self, name, phases):
        if self.enabled:
            ride, settle = self.ride_for(phases)
            settle(_run_ride(name, ride))

    def pair_sum(self, units):
        if self.enabled:
            for u in units:
                dtype = F32 if u in (SMALL_SHARDED, REPLICATED_UNIT) else BF16
                self.half[u], self.land[u] = _add_own_half(f"pair_sum_{u[0]}_{u[1]}", self.grad[u], self.recv[u], dtype)

    def chip_sum(self, units):
        if self.enabled:
            for u in units:
                self.done[u] = _sum_chips(f"chip_sum_{u[0]}_{u[1]}", self.land[u])


def _local_step(x, target, ex, sp, h0=None):
    t, d = x.shape
    tabs = _rope_tables(t)
    gains = jnp.concatenate([jnp.tile(sp["q_norm_g"][g], HEAD_DIM // 8) for g in range(N_DIL)]
                            + [jnp.tile(sp["k_norm_g"][g], HEAD_DIM // 8) for g in range(N_DIL)]).reshape(1, QK_COLS)
    bias_t = sp["sgu_bias"].T
    cw = _pad_rows(sp["conv_c_w"], 32)
    dw = _pad_rows(sp["conv_d_w"], 8)
    cb, clg, clb = (sp[k].reshape(1, C_WIDTH) for k in ("conv_c_b", "c_ln_g", "c_ln_b"))
    slg, slb = sp["sgu_norm_g"].reshape(1, A_WIDTH), sp["sgu_norm_b"].reshape(1, A_WIDTH)
    g_ab, g_cd = sp["ab_norm_g"].reshape(1, d), sp["cd_norm_g"].reshape(1, d)
    g_f0, g_f1 = sp["ffn_norm_g"][0:1], sp["ffn_norm_g"][1:2]
    run = ex.run

    def w2d(unit):
        return ex.full(unit).reshape(-1, d)

    if h0 is None:
        h0 = _rms_fwd("rms_ab", x, g_ab)
    if ex.enabled:
        proj = run(_proj_in_near, "proj_ab_near", h0, ex.full(A_IN), phases=[("gather_far", [A_IN]), ("send", [A_OUT])])
        proj, ex.w[A_IN] = _proj_in_far("proj_ab_far", h0, ex.far[A_IN], proj, ex.w[A_IN])
    else:
        proj = _proj_in("proj_ab", h0, ex.full(A_IN), 0)
    a_out = _mixer_a_fwd(proj, slg, slb, sp["sgu_w"], bias_t)
    qk, q1, q2, k1, k2 = run(_qk_fwd, proj, gains, tabs, phases=[("pass", [A_OUT]), ("send", [G0, C_OUT])])
    regrouped_qk = {1: (q1, k1), 2: (q2, k2)}
    fwd_phases = ([("pass", [G0, C_OUT]), ("send", [U0])], [("pass", [U0]), ("send", [D0])],
                  [("pass", [D0]), ("send", [C_IN])])
    qkv, o_list, l_list = [], [], []
    for g, rate in enumerate(DIL_RATES):
        if rate == 1:
            qk3, proj3 = qk.reshape(1, t, QK_COLS), proj.reshape(1, t, AB_IN)
            q, k, v = (qk3, g), (qk3, N_DIL + g), (proj3, V_BLOCK + g)
        else:
            vp, = _permute(f"regroup_v_{g}", [(proj, V_BLOCK + g)], rate)
            q, k, v = (regrouped_qk[g][0], 0), (regrouped_qk[g][1], 0), (vp, 0)
        qkv.append((q, k, v))
        o, l = run(_attn_fwd, f"attn_fwd_{g}", q, k, v, phases=fwd_phases[g])
        if rate == 1:
            o, l = o.reshape(t, B_WIDTH), l.reshape(t, B_WIDTH)
        o_list.append(o)
        l_list.append(l)
    cat, lse_tot, lse_1, lse_2 = _attn_merge(a_out, o_list, l_list)
    x1, hf0 = _proj_out("out_ab", cat, w2d(A_OUT), x, g_next=g_f0)
    fgate0, fup0, act0 = run(_ffn_in, "ffn_in_0", hf0, ex.full(G0), ex.full(U0), 0,
                           phases=[("pass", [C_IN]), ("send", [D1, G1])])
    x2, h1 = run(_ffn_out, "ffn_out_0", act0, ex.full(D0), 0, x1, g_next=g_cd, phases=[("pass", [D1, G1]), ("send", [U1])])
    projcd = run(_proj_in, "proj_cd", h1, ex.full(C_IN), 0, phases=[("pass", [U1])])
    cat2, c1 = _mixer_cd_fwd(projcd, cw, cb, clg, clb, dw)
    x3, hf1 = _proj_out("out_cd", cat2, w2d(C_OUT), x2, g_next=g_f1)
    fgate1, fup1, act1 = _ffn_in("ffn_in_1", hf1, ex.full(G1), ex.full(U1), 0)
    dy, loss_acc, dy_b = _ffn_out("ffn_out_1", act1, ex.full(D1), 0, x3, target=target)
    loss = 0.5 * loss_acc[0, 0] / d

    late = [D1, G1, U1]
    dgate, dup = _ffn_dact("ffn_dact_1", dy_b, ex.full(D1), 0, fgate1, fup1)
    ex.grad[D1] = _wgrad_row_sharded("wgrad_down_1", act1, dy_b, True)
    ex.grad[G1] = _wgrad_row_sharded("wgrad_gate_1", dgate, hf1, True)
    ex.grad[U1] = _wgrad_row_sharded("wgrad_up_1", dup, hf1, True)
    g3, d_f1, g3_b = run(_dgrad_cols, "dgrad_ffn_1", [dgate, dup], [ex.full(G1), ex.full(U1)], 0, True, x3, g_f1, dy,
                         w_rows=True, phases=[("swap", late)])
    ex.pair_sum(late)

    dcat2 = _dgrad_rows("dgrad_out_cd", g3_b, w2d(C_OUT))
    ex.grad[C_OUT] = _wgrad_row_sharded("wgrad_out_cd", cat2, g3_b, False)
    dprojcd, d_cw, d_cb, d_clg, d_clb, d_dw = run(_mixer_cd_bwd, projcd, dcat2, c1, cw, clg, clb, dw, phases=[("scatter", late)])
    ex.chip_sum(late)
    ex.grad[C_IN] = _wgrad_col_sharded("wgrad_in_cd", h1, [dprojcd], False)[0]
    g2, d_cdn, g2_b = run(_dgrad_cols, "dgrad_in_cd", [dprojcd], [ex.full(C_IN)], 0, False, x2, g_cd, g3,
                          phases=[("join", late), ("swap", [C_OUT, C_IN])])
    ex.pair_sum([C_OUT, C_IN])

    dgate, dup = run(_ffn_dact, "ffn_dact_0", g2_b, ex.full(D0), 0, fgate0, fup0, phases=[("scatter", [C_OUT, C_IN])])
    ex.chip_sum([C_OUT, C_IN])
    ex.grad[D0] = _wgrad_row_sharded("wgrad_down_0", act0, g2_b, True)
    ex.grad[G0] = _wgrad_row_sharded("wgrad_gate_0", dgate, hf0, True)
    ex.grad[U0] = _wgrad_row_sharded("wgrad_up_0", dup, hf0, True)
    small = {"cd_norm_g": d_cdn, "conv_c_w": d_cw[:C_KERNEL], "conv_c_b": d_cb, "c_ln_g": d_clg, "c_ln_b": d_clb,
             "conv_d_w": d_dw[:D_KERNEL]}
    ex.grad[SMALL_SHARDED] = _split_full_small(small).reshape(N_CHIPS, 2, SHARDED_ROWS // 2, LANES)
    mid = [D0, G0, U0, SMALL_SHARDED]
    g1, d_f0, g1_b = run(_dgrad_cols, "dgrad_ffn_0", [dgate, dup], [ex.full(G0), ex.full(U0)], 0, True, x1, g_f0, g2,
                         w_rows=True, phases=[("join", [C_OUT, C_IN]), ("swap", mid)])
    ex.pair_sum(mid)

    dcat = _dgrad_rows("dgrad_out_ab", g1_b, w2d(A_OUT))
    ex.grad[A_OUT] = _wgrad_row_sharded("wgrad_out_ab", cat, g1_b, False)
    d_a, d_sw, d_sbt, d_slg, d_slb = _mixer_a_bwd(proj, dcat, slg, slb, sp["sgu_w"], bias_t)
    early = {"sgu_norm_g": d_slg, "sgu_norm_b": d_slb, "sgu_w": d_sw, "sgu_bias": d_sbt.T}
    ex.grad[REPLICATED_UNIT] = jnp.broadcast_to(
        _pack_replicated(early, REPLICATED_EARLY, REPLICATED_EARLY_ROWS).reshape(2, REPLICATED_EARLY_ROWS // 2, LANES),
        (N_CHIPS, 2, REPLICATED_EARLY_ROWS // 2, LANES))
    last = [A_OUT, REPLICATED_UNIT]
    dbb, dd, db_1, dd_1, db_2, dd_2 = _attn_bwd_prep(dcat, cat)
    regrouped_bwd = {1: (db_1, lse_1, dd_1), 2: (db_2, lse_2, dd_2)}
    bwd_phases = ([("scatter", [D0, SMALL_SHARDED])],
                  [("scatter", [G0]), ("join", [D0, SMALL_SHARDED]), ("swap", last)],
                  [("scatter", [U0]), ("join", [G0])])
    dqs, dks, dvs = [], [], []
    for g, rate in enumerate(DIL_RATES):
        q, k, v = qkv[g]
        if rate == 1:
            db3, l3, dd3 = (a.reshape(1, t, B_WIDTH) for a in (dbb, lse_tot, dd))
        else:
            db3, l3, dd3 = regrouped_bwd[g]
        if g == 1:
            ex.chip_sum([D0, SMALL_SHARDED])
        elif g == 2:
            ex.chip_sum([G0])
            ex.pair_sum(last)
        dq, dk, dv = run(_attn_bwd, f"attn_bwd_{g}", q, k, v, db3, l3, dd3, phases=bwd_phases[g])
        if rate == 1:
            dq, dk, dv = (a.reshape(t, B_WIDTH) for a in (dq, dk, dv))
        dqs.append(dq)
        dks.append(dk)
        dvs.append(dv)
    ex.chip_sum([U0])
    dproj, d_gains = run(_dproj_assemble, proj, d_a, dqs, dks, dvs, gains, tabs, phases=[("scatter", last), ("join", [U0])])
    ex.chip_sum(last)
    d_gains = _fold_heads(d_gains)[0].reshape(2, N_DIL, B_WIDTH)[:, :, :HEAD_DIM]
    ex.grad[A_IN] = _wgrad_col_sharded("wgrad_in_ab", h0, [dproj], False)[0]
    ex.alone("swap_last", [("swap", [A_IN])])
    ex.pair_sum([A_IN])
    gx, d_abn = run(_dgrad_cols, "dgrad_in_ab", [dproj], [ex.full(A_IN)], 0, False, x, g_ab, g1, bf16_copy=False,
                    phases=[("join", last), ("scatter", [A_IN])])
    ex.chip_sum([A_IN])

    small.update({
        "ab_norm_g": d_abn, "sgu_norm_g": d_slg, "sgu_norm_b": d_slb, "sgu_w": d_sw, "sgu_bias": d_sbt.T,
        "q_norm_g": d_gains[0], "k_norm_g": d_gains[1], "ffn_norm_g": jnp.concatenate([d_f0, d_f1], axis=0),
    })
    return loss, gx, small


SHARDED_SMALL = ("cd_norm_g", "conv_c_w", "conv_c_b", "c_ln_g", "c_ln_b", "conv_d_w")
SHARDED_ROWS = 48
REPLICATED_EARLY = ("sgu_norm_g", "sgu_norm_b", "sgu_w", "sgu_bias")
REPLICATED_EARLY_ROWS = 528
REPLICATED_LATE = ("ab_norm_g", "q_norm_g", "k_norm_g", "ffn_norm_g", "loss")
REPLICATED_LATE_ROWS = 32
REPLICATED_SMALL = REPLICATED_EARLY + REPLICATED_LATE[:-1]


def _pack_sharded(parts):
    rows = [parts[k].reshape(-1, LANES) for k in SHARDED_SMALL]
    return _pad_rows(jnp.concatenate(rows, axis=0), SHARDED_ROWS)


def _split_full_small(small):
    per_chip = []
    for q in range(N_CHIPS):
        parts = {}
        for k in SHARDED_SMALL:
            a = small[k]
            a = a.reshape(-1, a.shape[-1])
            n = a.shape[-1] // N_CHIPS
            parts[k] = a[:, q * n:(q + 1) * n]
        per_chip.append(_pack_sharded(parts))
    return jnp.stack(per_chip)


def _unpack_sharded(pack, shapes):
    out, r = {}, 0
    for k in SHARDED_SMALL:
        n = math.prod(shapes[k]) // LANES
        out[k] = pack[r:r + n].reshape(shapes[k])
        r += n
    return out


def _gathered_small(packs, shapes):
    per_chip = [_unpack_sharded(packs[q], shapes) for q in range(N_CHIPS)]
    return {k: jnp.concatenate([pc[k] for pc in per_chip], axis=-1) for k in SHARDED_SMALL}


def _pack_replicated(small, names, total_rows):
    rows = []
    for k in names:
        a = small[k].reshape(-1)
        a = jnp.pad(a, (0, (-a.shape[0]) % LANES))
        rows.append(a.reshape(-1, LANES))
    return _pad_rows(jnp.concatenate(rows, axis=0), total_rows)


def _unpack_replicated(pack, shapes, names):
    out, r = {}, 0
    for k in names:
        size = math.prod(shapes[k])
        n = -(-size // LANES)
        out[k] = pack[r:r + n].reshape(-1)[:size].reshape(shapes[k])
        r += n
    return out


WEIGHT_ORDER = ("ab_norm_g", "ab_w_in", "sgu_norm_g", "sgu_norm_b", "sgu_w", "sgu_bias", "q_norm_g", "k_norm_g", "ab_w_out",
                "cd_norm_g", "cd_w_in", "conv_c_w", "conv_c_b", "c_ln_g", "c_ln_b", "conv_d_w", "cd_w_out", "ffn_norm_g",
                "ffn_w_gate", "ffn_w_up", "ffn_w_down")


def kernel(x, ab_norm_g, ab_w_in, sgu_norm_g, sgu_norm_b, sgu_w, sgu_bias, q_norm_g, k_norm_g, ab_w_out, cd_norm_g, cd_w_in, conv_c_w, conv_c_b, c_ln_g, c_ln_b, conv_d_w, cd_w_out, ffn_norm_g, ffn_w_gate, ffn_w_up, ffn_w_down, loss_target, m_ab_norm_g, m_ab_w_in, m_sgu_norm_g, m_sgu_norm_b, m_sgu_w, m_sgu_bias, m_q_norm_g, m_k_norm_g, m_ab_w_out, m_cd_norm_g, m_cd_w_in, m_conv_c_w, m_conv_c_b, m_c_ln_g, m_c_ln_b, m_conv_d_w, m_cd_w_out, m_ffn_norm_g, m_ffn_w_gate, m_ffn_w_up, m_ffn_w_down, v_ab_norm_g, v_ab_w_in, v_sgu_norm_g, v_sgu_norm_b, v_sgu_w, v_sgu_bias, v_q_norm_g, v_k_norm_g, v_ab_w_out, v_cd_norm_g, v_cd_w_in, v_conv_c_w, v_conv_c_b, v_c_ln_g, v_c_ln_b, v_conv_d_w, v_cd_w_out, v_ffn_norm_g, v_ffn_w_gate, v_ffn_w_up, v_ffn_w_down):
    args = dict(locals())
    ws = {k: args[k] for k in WEIGHT_ORDER}
    ms = {k: args["m_" + k] for k in WEIGHT_ORDER}
    vs = {k: args["v_" + k] for k in WEIGHT_ORDER}
    small_names = [k for k in WEIGHT_ORDER if k not in BIG]
    t, d = x.shape[1:]

    for group in (ws, ms, vs):
        for k in ROWS_MINOR:
            group[k] = jnp.swapaxes(group[k], 1, 2)
    ex = _Exchange(enabled=True)
    ex.w[A_IN], ex.far[A_IN] = _stage_own("stage_ab_w_in", ws["ab_w_in"], 0, BF16, far_slab=True)
    own_small = _pack_sharded({k: ws[k][0] for k in SHARDED_SMALL})
    ex.w[SMALL_SHARDED] = _stage_own("stage_small", own_small[None], 0, F32)
    rest = [u for u in UNITS if u != A_IN]
    x2 = x.reshape(t, d)
    h0, staged = ex.run(_stage_rest_and_norm, x2, ws["ab_norm_g"], [(ws[name], layer) for name, layer in rest],
                        phases=[("gather_near", [A_IN]), ("gather", [SMALL_SHARDED])])
    ex.w.update(zip(rest, staged))
    sp = _gathered_small(ex.w[SMALL_SHARDED].reshape(N_CHIPS, SHARDED_ROWS, LANES), {k: ws[k].shape[1:] for k in SHARDED_SMALL})
    for k in REPLICATED_SMALL:
        sp[k] = ws[k] if k == "ffn_norm_g" else ws[k][0]

    loss, grad_x, g_small = _local_step(x2, loss_target.reshape(t, d), ex, sp, h0)

    shapes = {k: ws[k].shape for k in REPLICATED_SMALL}
    shapes["loss"] = (1,)
    g_small["loss"] = loss
    join_last, settle = ex.ride_for([("join", [A_IN])])
    late, joined = _all_reduce_small(_pack_replicated(g_small, REPLICATED_LATE, REPLICATED_LATE_ROWS), join_last)
    settle(joined)
    grad = _unpack_sharded(ex.done[SMALL_SHARDED].reshape(SHARDED_ROWS, LANES), {k: ws[k].shape for k in SHARDED_SMALL})
    grad.update(_unpack_replicated(ex.done[REPLICATED_UNIT].reshape(REPLICATED_EARLY_ROWS, LANES), shapes, REPLICATED_EARLY))
    grad.update(_unpack_replicated(late, shapes, REPLICATED_LATE))
    loss = grad.pop("loss")[0]

    delta, new_m, new_v = {}, {}, {}
    for k in BIG:
        g_layers = [ex.done[(k, layer)] for layer in range(ws[k].shape[0])]
        outs = _adamw_big("adamw_" + k, ws[k], g_layers, ms[k], vs[k])
        if k in ROWS_MINOR:
            outs = [jnp.swapaxes(o, 1, 2) for o in outs]
        grad[k], delta[k], new_m[k], new_v[k] = outs
    d_s, m_s, v_s = _adamw_small([ws[k] for k in small_names], [grad[k] for k in small_names],
                                 [ms[k] for k in small_names], [vs[k] for k in small_names])
    for j, k in enumerate(small_names):
        delta[k], new_m[k], new_v[k] = d_s[j], m_s[j], v_s[j]

    return (loss, grad_x[None], *[grad[k] for k in WEIGHT_ORDER], *[delta[k] for k in WEIGHT_ORDER],
            *[new_m[k] for k in WEIGHT_ORDER], *[new_v[k] for k in WEIGHT_ORDER])
```

```python
import functools
import math

import jax
import jax.numpy as jnp
from jax import lax
from jax.experimental import pallas as pl
from jax.experimental.pallas import tpu as pltpu

F32 = jnp.float32
BF16 = jnp.bfloat16
SDS = jax.ShapeDtypeStruct

N_CHIPS = 4
EPS = 1e-6
NEG_INF = -1e30
CHUNK = 128
A_GROUPS = 4
A_WIDTH = 512
N_DIL = 3
DIL_RATES = (1, 4, 16)
HEAD_DIM = 64
B_WIDTH = 512
ROPE_DIM = 16
ROPE_THETA = 500000.0
C_WIDTH = 512
C_KERNEL = 31
D_KERNEL = 3
HALO = 32
ATT_BLOCK = 128
LANES = 128

ADAM_LR = 0.001
ADAM_B1 = 0.9
ADAM_B2 = 0.999
ADAM_EPS = 1e-08
ADAM_WD = 0.01
ADAM_STEP = 10

VMEM_LIMIT = 56 * 1024 * 1024

NN = (((1,), (0,)), ((), ()))
NT = (((1,), (1,)), ((), ()))
TN = (((0,), (0,)), ((), ()))

TILES = {"proj_in": 2048, "proj_out": 1024, "ffn_in": 1024, "ffn_out": 1024, "ffn_dact": 512, "dgrad_cols": 512,
         "dgrad_rows": 1024, "wgrad": 4096}


def _params(sem=None, collective_id=None):
    return pltpu.CompilerParams(dimension_semantics=sem, vmem_limit_bytes=VMEM_LIMIT, collective_id=collective_id)


def _bf(v):
    return v if v.dtype == BF16 else v.astype(BF16)


def _dot(a, b, dims):
    return lax.dot_general(_bf(a), _bf(b), dims, preferred_element_type=F32)


def _dot_hi(a, b):
    return jnp.dot(a, b, precision=lax.Precision.HIGHEST, preferred_element_type=F32)


def _sigmoid(v):
    return 0.5 * jnp.tanh(0.5 * v) + 0.5


def _gelu(v):
    return 0.5 * v * (1.0 + lax.erf(v * (1.0 / math.sqrt(2.0))))


def _gelu_grad(v):
    cdf = 0.5 * (1.0 + lax.erf(v * (1.0 / math.sqrt(2.0))))
    return cdf + v * jnp.exp(-0.5 * v * v) * (1.0 / math.sqrt(2.0 * math.pi))


def _segment_mean_matrix(seg, scale=None):
    r = lax.broadcasted_iota(jnp.int32, (LANES, LANES), 0) // seg
    c = lax.broadcasted_iota(jnp.int32, (LANES, LANES), 1) // seg
    return jnp.where(r == c, (1.0 / seg) if scale is None else scale, 0.0).astype(BF16)


def _segment_dot(v, seg):
    hi = v.astype(BF16)
    lo = (v - hi.astype(F32)).astype(BF16)
    return jnp.dot(hi, seg, preferred_element_type=F32) + jnp.dot(lo, seg, preferred_element_type=F32)


MESH = pl.DeviceIdType.MESH
ANY = pl.BlockSpec(memory_space=pl.ANY)


def _position():
    x, y, c = lax.axis_index("x"), lax.axis_index("y"), lax.axis_index("c")
    others = [(1 - x, y), (x, 1 - y), (1 - x, 1 - y)]
    return x, y, c, 2 * x + y, others


class _Ride:
    def __init__(self, ins, bufs, new_outs, sem_shapes, start, finish, reach):
        self.ins, self.bufs, self.new_outs, self.sem_shapes = list(ins), list(bufs), list(new_outs), list(sem_shapes)
        self.start, self.finish = start, finish
        self.reach = frozenset(reach)

    def entry_barrier(self):
        x, y, c, _, others = _position()
        peers = ([(x, y, 1 - c)] if "sibling" in self.reach else []) + ([(qx, qy, c) for qx, qy in others] if "chips" in self.reach else [])
        barrier = pltpu.get_barrier_semaphore()
        for peer in peers:
            pl.semaphore_signal(barrier, inc=1, device_id=peer, device_id_type=MESH)
        pl.semaphore_wait(barrier, len(peers))

    @property
    def collective_id(self):
        return {frozenset(["sibling"]): 0, frozenset(["chips"]): 1, frozenset(["sibling", "chips"]): 2}[self.reach]


def _ride_both(a, b):
    na = (len(a.ins), len(a.bufs), len(a.new_outs), len(a.sem_shapes))

    def split(ins, bufs, new, sems):
        return ((ins[:na[0]], bufs[:na[1]], new[:na[2]], sems[:na[3]]), (ins[na[0]:], bufs[na[1]:], new[na[2]:], sems[na[3]:]))

    def start(*refs):
        ra, rb = split(*refs)
        a.start(*ra)
        b.start(*rb)

    def finish(*refs):
        ra, rb = split(*refs)
        a.finish(*ra)
        b.finish(*rb)

    return _Ride(a.ins + b.ins, a.bufs + b.bufs, a.new_outs + b.new_outs, a.sem_shapes + b.sem_shapes, start, finish,
                 a.reach | b.reach)


def _call(body, *, grid, in_specs, out_specs, out_shape, operands, name, scratch_shapes=(), aliases=None, ride=None,
          prefetch=None):
    off = 0 if prefetch is None else 1
    lead = [] if prefetch is None else [prefetch]

    params = _params(collective_id=None if ride is None else ride.collective_id)

    def launch(kernel_body, in_specs_, out_specs_, out_shape_, scratch_, aliases_, *args):
        if prefetch is None:
            return pl.pallas_call(kernel_body, grid=grid, in_specs=in_specs_, out_specs=out_specs_, out_shape=out_shape_,
                                  scratch_shapes=scratch_, input_output_aliases=aliases_, name=name,
                                  compiler_params=params)(*args)
        spec = pltpu.PrefetchScalarGridSpec(num_scalar_prefetch=1, grid=grid, in_specs=in_specs_, out_specs=out_specs_,
                                            scratch_shapes=scratch_)
        return pl.pallas_call(kernel_body, grid_spec=spec, out_shape=out_shape_, input_output_aliases=aliases_, name=name,
                              compiler_params=params)(*lead, *args)

    if ride is None:
        return launch(body, list(in_specs), out_specs, out_shape, list(scratch_shapes), dict(aliases or {}), *operands)
    multi = isinstance(out_shape, (list, tuple))
    out_shapes = list(out_shape) if multi else [out_shape]
    o_specs = list(out_specs) if multi else [out_specs]
    n_in, n_out, n_scr = off + len(operands), len(out_shapes), len(scratch_shapes)
    n_ri, n_rb, n_rn = len(ride.ins), len(ride.bufs), len(ride.new_outs)

    def carrying(*refs):
        k = n_in
        r_ins = refs[k:k + n_ri]
        k += n_ri + n_rb
        outs = refs[k:k + n_out]
        k += n_out
        r_bufs = refs[k:k + n_rb]
        k += n_rb
        r_new = refs[k:k + n_rn]
        k += n_rn
        scratch = refs[k:k + n_scr]
        sems = refs[k + n_scr:]
        first, last = None, None
        for axis, size in enumerate(grid):
            pid = pl.program_id(axis)
            first = (pid == 0) if first is None else first & (pid == 0)
            last = (pid == size - 1) if last is None else last & (pid == size - 1)

        @pl.when(first)
        def _():
            ride.entry_barrier()
            ride.start(r_ins, r_bufs, r_new, sems)

        body(*refs[:n_in], *outs, *scratch)

        @pl.when(last)
        def _():
            ride.finish(r_ins, r_bufs, r_new, sems)

    all_aliases = dict(aliases or {})
    for j in range(n_rb):
        all_aliases[n_in + n_ri + j] = n_out + j
    res = launch(
        carrying, list(in_specs) + [ANY] * (n_ri + n_rb), o_specs + [ANY] * (n_rb + n_rn),
        out_shapes + [SDS(b.shape, b.dtype) for b in ride.bufs] + ride.new_outs,
        list(scratch_shapes) + [pltpu.SemaphoreType.DMA(s) for s in ride.sem_shapes], all_aliases,
        *operands, *ride.ins, *ride.bufs)
    outs = res[:n_out]
    return (list(outs) if multi else outs[0]), list(res[n_out:])


def _run_ride(name, ride):
    n_ri, n_rb, n_rn = len(ride.ins), len(ride.bufs), len(ride.new_outs)

    def body(*refs):
        r_ins = refs[:n_ri]
        r_bufs = refs[n_ri + n_rb:n_ri + 2 * n_rb]
        r_new = refs[n_ri + 2 * n_rb:n_ri + 2 * n_rb + n_rn]
        sems = refs[n_ri + 2 * n_rb + n_rn:]
        ride.entry_barrier()
        ride.start(r_ins, r_bufs, r_new, sems)
        ride.finish(r_ins, r_bufs, r_new, sems)

    return list(pl.pallas_call(
        body, in_specs=[ANY] * (n_ri + n_rb), out_specs=[ANY] * (n_rb + n_rn),
        out_shape=[SDS(b.shape, b.dtype) for b in ride.bufs] + ride.new_outs,
        scratch_shapes=[pltpu.SemaphoreType.DMA(s) for s in ride.sem_shapes],
        input_output_aliases={n_ri + j: j for j in range(n_rb)}, name=name,
        compiler_params=pltpu.CompilerParams(collective_id=ride.collective_id))(*ride.ins, *ride.bufs))


def _whole(ref, p):
    return ref[...]


def _slab(ref, p):
    return ref[p]


def _matmul(name, grid, pairs, extras, outs, dims, epi, *, slabs=1, n_acc=1, ride=None):
    n_pairs, n_ex, n_out = len(pairs), len(extras), len(outs)

    def body(*refs):
        ab = refs[:2 * n_pairs]
        ex = refs[2 * n_pairs:2 * n_pairs + n_ex]
        out_refs = refs[2 * n_pairs + n_ex:2 * n_pairs + n_ex + n_out]
        pids = tuple(pl.program_id(a) for a in range(len(grid)))
        parts = [None] * n_acc
        for p in range(slabs):
            for j, (_, _, a_pick, _, _, b_pick, acc) in enumerate(pairs):
                d = _dot(a_pick(ab[2 * j], p), b_pick(ab[2 * j + 1], p), dims)
                parts[acc] = d if parts[acc] is None else parts[acc] + d
        epi(parts, ex, out_refs, pids)

    operands, in_specs = [], []
    for a, a_spec, _, b, b_spec, _, _ in pairs:
        operands += [a, b]
        in_specs += [a_spec, b_spec]
    for e, e_spec in extras:
        operands.append(e)
        in_specs.append(e_spec)
    return _call(body, grid=grid, in_specs=in_specs, out_specs=[o[1] for o in outs], out_shape=[o[0] for o in outs],
                 operands=operands, name=name, ride=ride)


def _rms_rows(v, g):
    r = lax.rsqrt(jnp.mean(v * v, axis=-1, keepdims=True) + EPS)
    return v * r * g


def _rms_fwd(name, x, g):
    t, d = x.shape
    tm = 512

    def body(x_ref, g_ref, o_ref):
        o_ref[...] = _rms_rows(x_ref[...], g_ref[...]).astype(BF16)

    return pl.pallas_call(
        body, grid=(t // tm,),
        in_specs=[pl.BlockSpec((tm, d), lambda i: (i, 0)), pl.BlockSpec((1, d), lambda i: (0, 0))],
        out_specs=pl.BlockSpec((tm, d), lambda i: (i, 0)), out_shape=SDS((t, d), BF16), name=name,
        compiler_params=_params())(x, g)


def _epi_residual_norm(accs, ex, outs, pids):
    x_new = accs[0] + ex[0][...]
    outs[0][...] = x_new
    outs[1][...] = _rms_rows(x_new, ex[1][...]).astype(BF16)


def _epi_residual_loss(accs, ex, outs, pids):
    y = accs[0] + ex[0][...]
    err = y - ex[1][...]
    dy = err * (1.0 / err.shape[-1])
    outs[0][...] = dy
    outs[2][...] = dy.astype(BF16)

    @pl.when(pids[0] == 0)
    def _():
        outs[1][...] = jnp.zeros_like(outs[1])

    outs[1][...] += jnp.sum(err * err)


def _epi_rms_bwd(accs, ex, outs, pids):
    dh = accs[0]
    xv, g, res = ex[0][...], ex[1][...], ex[2][...]
    r = lax.rsqrt(jnp.mean(xv * xv, axis=-1, keepdims=True) + EPS)
    xh = xv * r
    dy = dh * g
    dx = res + r * (dy - xh * jnp.mean(dy * xh, axis=-1, keepdims=True))
    outs[0][...] = dx
    if len(outs) > 2:
        outs[2][...] = dx.astype(BF16)

    @pl.when(pids[0] == 0)
    def _():
        outs[1][...] = jnp.zeros_like(outs[1])

    outs[1][...] += jnp.sum(dh * xh, axis=0, keepdims=True)


def _row_spec(tm, d):
    return pl.BlockSpec((tm, d), lambda i, *_: (i, 0))


def _const_spec(shape):
    nd = len(shape)
    return pl.BlockSpec(shape, lambda *_: (0,) * nd)


def _proj_in(name, h, w, layer, ride=None):
    t, d = h.shape
    n4 = w.shape[-1]
    tm = TILES["proj_in"]

    def epi(accs, ex, outs, pids):
        outs[0][...] = accs[0].astype(BF16)

    res = _matmul(
        name, (N_CHIPS, t // tm),
        [(h, pl.BlockSpec((tm, d), lambda p, i: (i, 0)), _whole,
          w, pl.BlockSpec((None, None, d, n4), lambda p, i: (p, layer, 0, 0)), _whole, 0)],
        [], [(SDS((t, N_CHIPS * n4), BF16), pl.BlockSpec((tm, n4), lambda p, i: (i, p)))],
        NN, epi, ride=ride)
    return res[0] if ride is None else (res[0][0], res[1])


def _proj_in_near(name, h, w, ride=None):
    t, d = h.shape
    n4 = w.shape[-1]
    tm = TILES["proj_in"]

    def shard(j, s):
        return j + (j >= N_CHIPS - 1 - s[1]).astype(jnp.int32)

    def body(s_ref, h_ref, w_ref, o_ref):
        o_ref[...] = _dot(h_ref[...], w_ref[...], NN).astype(BF16)

    return _call(
        body, grid=(N_CHIPS - 1, t // tm),
        in_specs=[pl.BlockSpec((tm, d), lambda j, i, s: (i, 0)),
                  pl.BlockSpec((None, None, d, n4), lambda j, i, s: (shard(j, s), 0, 0, 0))],
        out_specs=pl.BlockSpec((tm, n4), lambda j, i, s: (i, shard(j, s))), out_shape=SDS((t, N_CHIPS * n4), BF16),
        operands=[h, w], name=name, ride=ride, prefetch=_mesh_scalars())


def _proj_in_far(name, h, slab, proj, w):
    t, d = h.shape
    n4 = slab.shape[-1]
    tm = TILES["proj_in"]

    def body(s_ref, h_ref, slab_ref, proj_in, w_in, o_ref, w_ref):
        shard = slab_ref[...]
        o_ref[...] = _dot(h_ref[...], shard, NN).astype(BF16)
        w_ref[...] = shard

    proj, w_full = _call(
        body, grid=(t // tm,),
        in_specs=[pl.BlockSpec((tm, d), lambda i, s: (i, 0)), pl.BlockSpec((d, n4), lambda i, s: (0, 0)), ANY, ANY],
        out_specs=[pl.BlockSpec((tm, n4), lambda i, s: (i, N_CHIPS - 1 - s[1])),
                   pl.BlockSpec((None, d, n4), lambda i, s: (N_CHIPS - 1 - s[1], 0, 0))],
        out_shape=[SDS(proj.shape, BF16), SDS((N_CHIPS, d, n4), BF16)],
        operands=[h, slab.reshape(d, n4), proj, w.reshape(N_CHIPS, d, n4)], aliases={3: 0, 4: 1}, name=name,
        prefetch=_mesh_scalars())
    return proj, w_full.reshape(w.shape)


def _proj_out(name, a, w, x, g_next=None, target=None):
    t, k = a.shape
    d = w.shape[-1]
    tm = TILES["proj_out"]
    if target is None:
        extras = [(x, _row_spec(tm, d)), (g_next, _const_spec((1, d)))]
        outs = [(SDS((t, d), F32), _row_spec(tm, d)), (SDS((t, d), BF16), _row_spec(tm, d))]
        epi = _epi_residual_norm
    else:
        extras = [(x, _row_spec(tm, d)), (target, _row_spec(tm, d))]
        outs = [(SDS((t, d), F32), _row_spec(tm, d)), (SDS((8, LANES), F32), _const_spec((8, LANES))),
                (SDS((t, d), BF16), _row_spec(tm, d))]
        epi = _epi_residual_loss
    return _matmul(name, (t // tm,), [(a, _row_spec(tm, k), _whole, w, _const_spec((k, d)), _whole, 0)], extras, outs, NN, epi)


def _ffn_in(name, h, wg, wu, layer, ride=None):
    t, d = h.shape
    n4 = wg.shape[-2]
    tm = TILES["ffn_in"]

    def epi(accs, ex, outs, pids):
        gate, up = accs
        s = _sigmoid(gate)
        silu = gate * s
        outs[0][...] = (up * (s + silu - silu * s)).astype(BF16)
        outs[1][...] = silu.astype(BF16)
        outs[2][...] = (silu * up).astype(BF16)

    w_spec = pl.BlockSpec((None, None, n4, d), lambda p, i: (p, layer, 0, 0))
    h_spec = pl.BlockSpec((tm, d), lambda p, i: (i, 0))
    o = (SDS((N_CHIPS, t, n4), BF16), pl.BlockSpec((None, tm, n4), lambda p, i: (p, i, 0)))
    return _matmul(name, (N_CHIPS, t // tm),
                   [(h, h_spec, _whole, wg, w_spec, _whole, 0), (h, h_spec, _whole, wu, w_spec, _whole, 1)], [],
                   [o, o, o], NT, epi, n_acc=2, ride=ride)


def _ffn_out(name, act, wd, layer, x, g_next=None, target=None, ride=None):
    _, t, n4 = act.shape
    d = wd.shape[-1]
    tm = TILES["ffn_out"]
    xs = _row_spec(tm, d)
    if target is None:
        extras = [(x, xs), (g_next, _const_spec((1, d)))]
        outs = [(SDS((t, d), F32), xs), (SDS((t, d), BF16), xs)]
        epi = _epi_residual_norm
    else:
        extras = [(x, xs), (target, xs)]
        outs = [(SDS((t, d), F32), xs), (SDS((8, LANES), F32), _const_spec((8, LANES))), (SDS((t, d), BF16), xs)]
        epi = _epi_residual_loss
    return _matmul(
        name, (t // tm,),
        [(act, pl.BlockSpec((N_CHIPS, tm, n4), lambda i: (0, i, 0)), _slab,
          wd, pl.BlockSpec((N_CHIPS, None, n4, d), lambda i: (0, layer, 0, 0)), _slab, 0)],
        extras, outs, NN, epi, slabs=N_CHIPS, ride=ride)


def _ffn_dact(name, g, wd, layer, gate, up, ride=None):
    t, d = g.shape
    n4 = wd.shape[-2]
    tm = TILES["ffn_dact"]

    def body(g_ref, w_ref, gate_ref, up_ref, dgate_ref, dup_ref):
        gv = g_ref[...]
        for p in range(N_CHIPS):
            dact = _dot(gv, w_ref[p], NT)
            dgate_ref[p] = (dact * gate_ref[p].astype(F32)).astype(BF16)
            dup_ref[p] = (dact * up_ref[p].astype(F32)).astype(BF16)

    blk = pl.BlockSpec((N_CHIPS, tm, n4), lambda i: (0, i, 0))
    return _call(
        body, grid=(t // tm,),
        in_specs=[_row_spec(tm, d), pl.BlockSpec((N_CHIPS, None, n4, d), lambda i: (0, layer, 0, 0)), blk, blk],
        out_specs=[blk, blk], out_shape=[SDS((N_CHIPS, t, n4), BF16)] * 2, operands=[g, wd, gate, up], name=name, ride=ride)


def _copy_epi(accs, ex, outs, pids):
    for a, o in zip(accs, outs):
        o[...] = a.astype(o.dtype)


def _dgrad_cols(name, dz_list, w_list, layer, three_d, x, g, res, bf16_copy=True, w_rows=False, ride=None):
    t, d = x.shape
    n4 = w_list[0].shape[-2 if w_rows else -1]
    tm = TILES["dgrad_cols"]
    if three_d:
        zs, z_pick = pl.BlockSpec((N_CHIPS, tm, n4), lambda i: (0, i, 0)), _slab
    else:
        zs, z_pick = _row_spec(tm, N_CHIPS * n4), (lambda ref, p: ref[:, p * n4:(p + 1) * n4])
    ws = pl.BlockSpec((N_CHIPS, None) + ((n4, d) if w_rows else (d, n4)), lambda i: (0, layer, 0, 0))
    xs = _row_spec(tm, d)
    return _matmul(
        name, (t // tm,), [(dz, zs, z_pick, w, ws, _slab, 0) for dz, w in zip(dz_list, w_list)],
        [(x, xs), (g, _const_spec((1, d))), (res, xs)],
        [(SDS((t, d), F32), xs), (SDS((1, d), F32), _const_spec((1, d)))] + ([(SDS((t, d), BF16), xs)] if bf16_copy else []),
        NN if w_rows else NT, _epi_rms_bwd, slabs=N_CHIPS, ride=ride)


def _dgrad_rows(name, g, w):
    t, d = g.shape
    k = w.shape[0]
    tm = TILES["dgrad_rows"]
    return _matmul(name, (t // tm,), [(g, _row_spec(tm, d), _whole, w, _const_spec((k, d)), _whole, 0)], [],
                   [(SDS((t, k), F32), _row_spec(tm, k))], NT, _copy_epi)[0]


A_TILE = 256


def _a_common(p_ref, lg_ref, lb_ref):
    pv = p_ref[...].astype(F32)
    a = _gelu(pv)
    u, v = a[:, :A_WIDTH], a[:, A_WIDTH:]
    vc = v - jnp.mean(v, axis=-1, keepdims=True)
    rs = lax.rsqrt(jnp.mean(vc * vc, axis=-1, keepdims=True) + EPS)
    vhat = vc * rs
    vn = vhat * lg_ref[...] + lb_ref[...]
    return pv, u, vhat, rs, vn.astype(BF16)


def _tril_weights(w_ref, g):
    r = lax.broadcasted_iota(jnp.int32, (CHUNK, CHUNK), 0)
    c = lax.broadcasted_iota(jnp.int32, (CHUNK, CHUNK), 1)
    return jnp.where(c <= r, w_ref[g], 0.0).astype(BF16), c <= r


def _mixer_a_fwd(proj, lg, lb, w, bias_t):
    t = proj.shape[0]

    def body(p_ref, lg_ref, lb_ref, w_ref, bt_ref, o_ref):
        _, u, _, _, vnb = _a_common(p_ref, lg_ref, lb_ref)
        for g in range(A_GROUPS):
            wt, _ = _tril_weights(w_ref, g)
            cs = slice(g * CHUNK, (g + 1) * CHUNK)
            for ch in range(A_TILE // CHUNK):
                rs_ = slice(ch * CHUNK, (ch + 1) * CHUNK)
                mixed = _dot(wt, vnb[rs_, cs], NN) + bt_ref[:, g:g + 1]
                o_ref[rs_, cs] = (u[rs_, cs] * mixed).astype(BF16)

    return pl.pallas_call(
        body, grid=(t // A_TILE,),
        in_specs=[pl.BlockSpec((A_TILE, 2 * A_WIDTH), lambda i: (i, 0)), _const_spec((1, A_WIDTH)),
                  _const_spec((1, A_WIDTH)), _const_spec((A_GROUPS, CHUNK, CHUNK)), _const_spec((CHUNK, A_GROUPS))],
        out_specs=pl.BlockSpec((A_TILE, A_WIDTH), lambda i: (i, 0)), out_shape=SDS((t, A_WIDTH), BF16),
        name="mixer_a_fwd", compiler_params=_params())(proj, lg, lb, w, bias_t)


def _mixer_a_bwd(proj, dcat, lg, lb, w, bias_t):
    t = proj.shape[0]

    def body(p_ref, da_ref, lg_ref, lb_ref, w_ref, bt_ref, dp_ref, dw_ref, dbt_ref, dlg_ref, dlb_ref, du_scr, dvn_scr):
        @pl.when(pl.program_id(0) == 0)
        def _():
            dw_ref[...] = jnp.zeros_like(dw_ref)
            dbt_ref[...] = jnp.zeros_like(dbt_ref)
            dlg_ref[...] = jnp.zeros_like(dlg_ref)
            dlb_ref[...] = jnp.zeros_like(dlb_ref)

        pv, u, vhat, rs, vnb = _a_common(p_ref, lg_ref, lb_ref)
        da = da_ref[...]
        for g in range(A_GROUPS):
            wt, keep = _tril_weights(w_ref, g)
            cs = slice(g * CHUNK, (g + 1) * CHUNK)
            for ch in range(A_TILE // CHUNK):
                rs_ = slice(ch * CHUNK, (ch + 1) * CHUNK)
                vg = vnb[rs_, cs]
                mixed = _dot(wt, vg, NN) + bt_ref[:, g:g + 1]
                du_scr[rs_, cs] = da[rs_, cs] * mixed
                dmx = da[rs_, cs] * u[rs_, cs]
                dw_ref[g] += jnp.where(keep, _dot(dmx, vg, NT), 0.0)
                dvn_scr[rs_, cs] = _dot(wt, dmx, TN)
                dbt_ref[:, g:g + 1] += jnp.sum(dmx, axis=1, keepdims=True)
        dvn = dvn_scr[...]
        dlg_ref[...] += jnp.sum(dvn * vhat, axis=0, keepdims=True)
        dlb_ref[...] += jnp.sum(dvn, axis=0, keepdims=True)
        dvh = dvn * lg_ref[...]
        dv = rs * (dvh - jnp.mean(dvh, axis=-1, keepdims=True) - vhat * jnp.mean(dvh * vhat, axis=-1, keepdims=True))
        gp = _gelu_grad(pv)
        dp_ref[:, :A_WIDTH] = (du_scr[...] * gp[:, :A_WIDTH]).astype(BF16)
        dp_ref[:, A_WIDTH:] = (dv * gp[:, A_WIDTH:]).astype(BF16)

    return pl.pallas_call(
        body, grid=(t // A_TILE,),
        in_specs=[pl.BlockSpec((A_TILE, 2 * A_WIDTH), lambda i: (i, 0)), pl.BlockSpec((A_TILE, A_WIDTH), lambda i: (i, 0)),
                  _const_spec((1, A_WIDTH)), _const_spec((1, A_WIDTH)), _const_spec((A_GROUPS, CHUNK, CHUNK)),
                  _const_spec((CHUNK, A_GROUPS))],
        out_specs=[pl.BlockSpec((A_TILE, 2 * A_WIDTH), lambda i: (i, 0)), _const_spec((A_GROUPS, CHUNK, CHUNK)),
                   _const_spec((CHUNK, A_GROUPS)), _const_spec((1, A_WIDTH)), _const_spec((1, A_WIDTH))],
        out_shape=[SDS((t, 2 * A_WIDTH), BF16), SDS((A_GROUPS, CHUNK, CHUNK), F32), SDS((CHUNK, A_GROUPS), F32),
                   SDS((1, A_WIDTH), F32), SDS((1, A_WIDTH), F32)],
        scratch_shapes=[pltpu.VMEM((A_TILE, A_WIDTH), F32), pltpu.VMEM((A_TILE, A_WIDTH), F32)],
        name="mixer_a_bwd", compiler_params=_params())(proj, dcat, lg, lb, w, bias_t)


def _rope_tables(t):
    half = ROPE_DIM // 2
    inv_freq = ROPE_THETA ** (-jnp.arange(half, dtype=F32) * 2.0 / ROPE_DIM)
    ang = jnp.arange(t, dtype=F32)[:, None] * inv_freq[None, :]
    cos, sin = jnp.cos(ang), jnp.sin(ang)
    one = jnp.ones((t, HEAD_DIM - ROPE_DIM), F32)
    zero = jnp.zeros((t, HEAD_DIM - ROPE_DIM), F32)
    zh = jnp.zeros((t, half), F32)
    c = jnp.concatenate([cos, cos, one], axis=1)
    s1 = jnp.concatenate([-sin, zh, zero], axis=1)
    s2 = jnp.concatenate([zh, sin, zero], axis=1)
    return tuple(jnp.tile(a, (1, LANES // HEAD_DIM)) for a in (c, s1, s2))


QK_TILE = 512
QK_ROWS = 64
QK_COLS = 2 * N_DIL * B_WIDTH


CHUNKS = B_WIDTH // LANES


def _regroup_out(scr, first, out_ref, rate, tile):
    rows = tile // rate
    for rho in range(rate):
        for c in range(CHUNKS):
            out_ref[rho, :, c * LANES:(c + 1) * LANES] = scr[first + c, pl.ds(rho, rows, stride=rate), :].astype(out_ref.dtype)


def _regroup_in(x_ref, scr, rate, tile):
    rows = tile // rate
    for rho in range(rate):
        for c in range(CHUNKS):
            scr[c, pl.ds(rho, rows, stride=rate), :] = x_ref[rho, :, c * LANES:(c + 1) * LANES].astype(F32)


def _regrouped_spec(rate, tile):
    return pl.BlockSpec((rate, tile // rate, B_WIDTH), lambda i, *_: (0, i, 0))


def _qk_fwd(proj, gains, tabs, ride=None):
    t = proj.shape[0]
    col0 = 2 * A_WIDTH // 1024
    r1, r2 = DIL_RATES[1], DIL_RATES[2]

    def body(p_ref, g_ref, c_ref, s1_ref, s2_ref, o_ref, q1_ref, q2_ref, k1_ref, k2_ref, scr):
        seg = _segment_mean_matrix(HEAD_DIM)
        for r0 in range(0, QK_TILE, QK_ROWS):
            rows = slice(r0, r0 + QK_ROWS)
            c, s1, s2 = c_ref[rows, :], s1_ref[rows, :], s2_ref[rows, :]
            for ci in range(1024 // LANES):
                ls = slice(ci * LANES, (ci + 1) * LANES)
                xv = p_ref[rows, ls].astype(F32)
                r = lax.rsqrt(_segment_dot(xv * xv, seg) + EPS)
                y = xv * r * g_ref[:, ls]
                val = y * c + pltpu.roll(y, LANES - 8, axis=1) * s1 + pltpu.roll(y, 8, axis=1) * s2
                o_ref[rows, ls] = val.astype(BF16)
                scr[ci, rows, :] = val

        j = pl.program_id(1)

        @pl.when(j == 0)
        def _():
            _regroup_out(scr, CHUNKS, q1_ref, r1, QK_TILE)

        @pl.when(j == 1)
        def _():
            _regroup_out(scr, 0, q2_ref, r2, QK_TILE)

        @pl.when(j == 2)
        def _():
            _regroup_out(scr, 0, k1_ref, r1, QK_TILE)
            _regroup_out(scr, CHUNKS, k2_ref, r2, QK_TILE)

    tab = pl.BlockSpec((QK_TILE, LANES), lambda i, j: (i, 0))
    g1, g2 = SDS((r1, t // r1, B_WIDTH), BF16), SDS((r2, t // r2, B_WIDTH), BF16)
    s1_, s2_ = _regrouped_spec(r1, QK_TILE), _regrouped_spec(r2, QK_TILE)
    return _call(
        body, grid=(t // QK_TILE, QK_COLS // 1024),
        in_specs=[pl.BlockSpec((QK_TILE, 1024), lambda i, j: (i, col0 + j)), pl.BlockSpec((1, 1024), lambda i, j: (0, j)),
                  tab, tab, tab],
        out_specs=[pl.BlockSpec((QK_TILE, 1024), lambda i, j: (i, j)), s1_, s2_, s1_, s2_],
        out_shape=[SDS((t, QK_COLS), BF16), g1, g2, g1, g2],
        scratch_shapes=[pltpu.VMEM((2 * CHUNKS, QK_TILE, LANES), F32)],
        operands=[proj, gains, *tabs], name="qk_norm_rope_fwd", ride=ride)


PERM_TILE = 512


def _permute(name, items, rate):
    t = items[0][0].shape[0]
    n = len(items)

    def body(*refs):
        scr = refs[-1]
        for x_ref, o_ref in zip(refs[:n], refs[n:2 * n]):
            for ci in range(CHUNKS):
                scr[ci] = x_ref[:, ci * LANES:(ci + 1) * LANES].astype(F32)
            _regroup_out(scr, 0, o_ref, rate, PERM_TILE)

    return pl.pallas_call(
        body, grid=(t // PERM_TILE,),
        in_specs=[pl.BlockSpec((PERM_TILE, B_WIDTH), functools.partial(lambda cb, i: (i, cb), cb)) for _, cb in items],
        out_specs=[_regrouped_spec(rate, PERM_TILE) for _ in items],
        out_shape=[SDS((rate, t // rate, B_WIDTH), a.dtype) for a, _ in items],
        scratch_shapes=[pltpu.VMEM((CHUNKS, PERM_TILE, LANES), F32)],
        name=name, compiler_params=_params())(*[a for a, _ in items])


def _head_lane_mask(h):
    lane = lax.broadcasted_iota(jnp.int32, (1, LANES), 1)
    return (lane < HEAD_DIM) if h == 0 else (lane >= HEAD_DIM)


def _attn_fwd(name, q, k, v, ride=None):
    rate, length = q[0].shape[0], q[0].shape[1]
    nb = length // ATT_BLOCK
    scale = HEAD_DIM ** -0.5

    def body(q_ref, kc_ref, kp_ref, vc_ref, vp_ref, o_ref, l_ref):
        n = pl.program_id(1)
        qi = lax.broadcasted_iota(jnp.int32, (ATT_BLOCK, 2 * ATT_BLOCK), 0)
        cj = lax.broadcasted_iota(jnp.int32, (ATT_BLOCK, 2 * ATT_BLOCK), 1)
        has_prev = jnp.where(n > 0, 0, 2 * ATT_BLOCK)
        mask = ((cj < ATT_BLOCK) & (cj >= qi + has_prev)) | ((cj >= ATT_BLOCK) & (cj - ATT_BLOCK <= qi))
        heads = [(hp, h) for hp in range(CHUNKS) for h in range(2)]
        q2, k2, v2 = {}, {}, {}
        for hp in range(CHUNKS):
            ls = slice(hp * LANES, (hp + 1) * LANES)
            q2[hp] = q_ref[:, ls]
            k2[hp] = jnp.concatenate([kp_ref[:, ls], kc_ref[:, ls]], axis=0)
            v2[hp] = jnp.concatenate([vp_ref[:, ls], vc_ref[:, ls]], axis=0)
        scores = {}
        for hp, h in heads:
            scores[hp, h] = _dot(jnp.where(_head_lane_mask(h), q2[hp], jnp.zeros_like(q2[hp])), k2[hp], NT) * scale
        probs, lses = {}, {}
        for hp, h in heads:
            s = jnp.where(mask, scores[hp, h], NEG_INF)
            m = jnp.max(s, axis=1, keepdims=True)
            p = jnp.exp(s - m)
            den = jnp.sum(p, axis=1, keepdims=True)
            lses[hp, h] = m + jnp.log(den)
            probs[hp, h] = (p / den).astype(BF16)
        for hp in range(CHUNKS):
            ls = slice(hp * LANES, (hp + 1) * LANES)
            o_acc = None
            for h in range(2):
                o = _dot(probs[hp, h], jnp.where(_head_lane_mask(h), v2[hp], jnp.zeros_like(v2[hp])), NN)
                o_acc = o if o_acc is None else o_acc + o
            o_ref[:, ls] = o_acc
            zeros = jnp.zeros((ATT_BLOCK, LANES), F32)
            l_ref[:, ls] = jnp.where(_head_lane_mask(1), lses[hp, 1] + zeros, lses[hp, 0] + zeros)

    def cur(cb):
        return pl.BlockSpec((None, ATT_BLOCK, B_WIDTH), lambda r, n: (r, n, cb))

    def prev(cb):
        return pl.BlockSpec((None, ATT_BLOCK, B_WIDTH), lambda r, n: (r, jnp.maximum(n - 1, 0), cb))

    out = pl.BlockSpec((None, ATT_BLOCK, B_WIDTH), lambda r, n: (r, n, 0))
    return _call(
        body, grid=(rate, nb),
        in_specs=[cur(q[1]), cur(k[1]), prev(k[1]), cur(v[1]), prev(v[1])],
        out_specs=[out, out], out_shape=[SDS((rate, length, B_WIDTH), F32)] * 2,
        operands=[q[0], k[0], k[0], v[0], v[0]], name=name, ride=ride)


def _attn_merge(a_out, o_list, l_list):
    t = a_out.shape[0]
    tm = PERM_TILE
    r1, r2 = DIL_RATES[1], DIL_RATES[2]

    def body(a_ref, o0, o1, o2, l0, l1, l2, cat_ref, lt_ref, lt1_ref, lt2_ref, so1, so2, sl1, sl2, slt):
        _regroup_in(o1, so1, r1, tm)
        _regroup_in(l1, sl1, r1, tm)
        _regroup_in(o2, so2, r2, tm)
        _regroup_in(l2, sl2, r2, tm)
        cat_ref[:, :A_WIDTH] = a_ref[...]
        for c in range(CHUNKS):
            ls = slice(c * LANES, (c + 1) * LANES)
            lg = [l0[:, ls], sl1[c], sl2[c]]
            m = jnp.maximum(jnp.maximum(lg[0], lg[1]), lg[2])
            es = [jnp.exp(l - m) for l in lg]
            den = es[0] + es[1] + es[2]
            b = (es[0] * o0[:, ls] + es[1] * so1[c] + es[2] * so2[c]) / den
            cat_ref[:, A_WIDTH + c * LANES:A_WIDTH + (c + 1) * LANES] = b.astype(BF16)
            lt = m + jnp.log(den)
            lt_ref[:, ls] = lt
            slt[c] = lt
        _regroup_out(slt, 0, lt1_ref, r1, tm)
        _regroup_out(slt, 0, lt2_ref, r2, tm)

    blk = _row_spec(tm, B_WIDTH)
    g1, g2 = _regrouped_spec(r1, tm), _regrouped_spec(r2, tm)
    return pl.pallas_call(
        body, grid=(t // tm,), in_specs=[blk, blk, g1, g2, blk, g1, g2],
        out_specs=[_row_spec(tm, A_WIDTH + B_WIDTH), blk, g1, g2],
        out_shape=[SDS((t, A_WIDTH + B_WIDTH), BF16), SDS((t, B_WIDTH), F32), SDS((r1, t // r1, B_WIDTH), F32),
                   SDS((r2, t // r2, B_WIDTH), F32)],
        scratch_shapes=[pltpu.VMEM((CHUNKS, tm, LANES), F32)] * 5,
        name="attn_merge", compiler_params=_params())(a_out, *o_list, *l_list)


def _attn_bwd_prep(dcat, cat):
    t = dcat.shape[0]
    tm = PERM_TILE
    r1, r2 = DIL_RATES[1], DIL_RATES[2]

    def body(d_ref, b_ref, db_ref, dd_ref, db1_ref, dd1_ref, db2_ref, dd2_ref, sdb, sdd):
        seg = _segment_mean_matrix(HEAD_DIM, scale=1.0)
        for c in range(CHUNKS):
            ls = slice(c * LANES, (c + 1) * LANES)
            d = d_ref[:, ls]
            dsum = _segment_dot(d * b_ref[:, ls].astype(F32), seg)
            db_ref[:, ls] = d.astype(BF16)
            dd_ref[:, ls] = dsum
            sdb[c] = d
            sdd[c] = dsum
        _regroup_out(sdb, 0, db1_ref, r1, tm)
        _regroup_out(sdd, 0, dd1_ref, r1, tm)
        _regroup_out(sdb, 0, db2_ref, r2, tm)
        _regroup_out(sdd, 0, dd2_ref, r2, tm)

    right = pl.BlockSpec((tm, B_WIDTH), lambda i: (i, 1))
    blk = _row_spec(tm, B_WIDTH)
    g1, g2 = _regrouped_spec(r1, tm), _regrouped_spec(r2, tm)
    return pl.pallas_call(
        body, grid=(t // tm,), in_specs=[right, right], out_specs=[blk, blk, g1, g1, g2, g2],
        out_shape=[SDS((t, B_WIDTH), BF16), SDS((t, B_WIDTH), F32), SDS((r1, t // r1, B_WIDTH), BF16),
                   SDS((r1, t // r1, B_WIDTH), F32), SDS((r2, t // r2, B_WIDTH), BF16), SDS((r2, t // r2, B_WIDTH), F32)],
        scratch_shapes=[pltpu.VMEM((CHUNKS, tm, LANES), F32)] * 2,
        name="attn_bwd_prep", compiler_params=_params())(dcat, cat)


def _attn_bwd(name, q, k, v, db, lse, dd, ride=None):
    rate, length = db.shape[0], db.shape[1]
    nb = length // ATT_BLOCK
    scale = HEAD_DIM ** -0.5

    def body(qa_ref, qb_ref, k_ref, v_ref, dba_ref, dbb_ref, la_ref, lb_ref, da_ref, dbd_ref, dq_ref, dk_ref, dv_ref, carry):
        m = pl.program_id(1)

        @pl.when(m == 0)
        def _():
            carry[...] = jnp.zeros_like(carry)

        row = lax.broadcasted_iota(jnp.int32, (2 * ATT_BLOCK, ATT_BLOCK), 0)
        kj = lax.broadcasted_iota(jnp.int32, (2 * ATT_BLOCK, ATT_BLOCK), 1)
        no_next = jnp.where(m + 1 < nb, 0, 2 * ATT_BLOCK)
        mask = ((row < ATT_BLOCK) & (kj <= row)) | ((row >= ATT_BLOCK) & (kj >= row - ATT_BLOCK + no_next))
        heads = [(hp, h) for hp in range(CHUNKS) for h in range(2)]
        q2, db2, lse2, dd2, k2, v2 = {}, {}, {}, {}, {}, {}
        for hp in range(CHUNKS):
            ls = slice(hp * LANES, (hp + 1) * LANES)
            k2[hp], v2[hp] = k_ref[:, ls], v_ref[:, ls]
            q2[hp] = jnp.concatenate([qa_ref[:, ls], qb_ref[:, ls]], axis=0)
            db2[hp] = jnp.concatenate([dba_ref[:, ls], dbb_ref[:, ls]], axis=0)
            lse2[hp] = jnp.concatenate([la_ref[:, ls], lb_ref[:, ls]], axis=0)
            dd2[hp] = jnp.concatenate([da_ref[:, ls], dbd_ref[:, ls]], axis=0)
        km, scores, dps = {}, {}, {}
        for hp, h in heads:
            hm = _head_lane_mask(h)
            km[hp, h] = jnp.where(hm, k2[hp], jnp.zeros_like(k2[hp]))
            scores[hp, h] = _dot(q2[hp], km[hp, h], NT) * scale
            dps[hp, h] = _dot(db2[hp], jnp.where(hm, v2[hp], jnp.zeros_like(v2[hp])), NT)
        probs, dss = {}, {}
        for hp, h in heads:
            hm = _head_lane_mask(h)
            lse_col = jnp.max(jnp.where(hm, lse2[hp], NEG_INF), axis=1, keepdims=True)
            dd_col = jnp.max(jnp.where(hm, dd2[hp], NEG_INF), axis=1, keepdims=True)
            p = jnp.where(mask, jnp.exp(scores[hp, h] - lse_col), 0.0)
            probs[hp, h] = p.astype(BF16)
            dss[hp, h] = (p * (dps[hp, h] - dd_col) * scale).astype(BF16)
        for hp in range(CHUNKS):
            ls = slice(hp * LANES, (hp + 1) * LANES)
            dq_acc, dk_acc, dv_acc = None, None, None
            for h in range(2):
                hm = _head_lane_mask(h)
                dvc = _dot(probs[hp, h], jnp.where(hm, db2[hp], jnp.zeros_like(db2[hp])), TN)
                dqc = _dot(dss[hp, h], km[hp, h], NN)
                dkc = _dot(dss[hp, h], jnp.where(hm, q2[hp], jnp.zeros_like(q2[hp])), TN)
                dq_acc = dqc if dq_acc is None else dq_acc + dqc
                dk_acc = dkc if dk_acc is None else dk_acc + dkc
                dv_acc = dvc if dv_acc is None else dv_acc + dvc
            dq_ref[:, ls] = (dq_acc[:ATT_BLOCK] + carry[:, ls]).astype(BF16)
            carry[:, ls] = dq_acc[ATT_BLOCK:]
            dk_ref[:, ls] = dk_acc.astype(BF16)
            dv_ref[:, ls] = dv_acc.astype(BF16)

    def cur(cb):
        return pl.BlockSpec((None, ATT_BLOCK, B_WIDTH), lambda r, n: (r, n, cb))

    def nxt(cb):
        return pl.BlockSpec((None, ATT_BLOCK, B_WIDTH), lambda r, n: (r, jnp.minimum(n + 1, nb - 1), cb))

    out = cur(0)
    return _call(
        body, grid=(rate, nb),
        in_specs=[cur(q[1]), nxt(q[1]), cur(k[1]), cur(v[1]), cur(0), nxt(0), cur(0), nxt(0), cur(0), nxt(0)],
        out_specs=[out, out, out], out_shape=[SDS((rate, length, B_WIDTH), BF16)] * 3,
        scratch_shapes=[pltpu.VMEM((ATT_BLOCK, B_WIDTH), F32)],
        operands=[q[0], q[0], k[0], v[0], db, db, lse, lse, dd, dd], name=name, ride=ride)


AB_IN = 2 * A_WIDTH + 3 * N_DIL * B_WIDTH
ASM_TILE = 256


def _dproj_assemble(proj, d_a, dq, dk, dv, gains, tabs, ride=None):
    t = proj.shape[0]
    n_in = 3 * N_DIL

    def body(p_ref, da_ref, *rest):
        grads = rest[:n_in]
        g_ref, c_ref, s1_ref, s2_ref, o_ref, dg_ref = rest[n_in:n_in + 6]
        scratch = rest[n_in + 6:]

        @pl.when(pl.program_id(0) == 0)
        def _():
            dg_ref[...] = jnp.zeros_like(dg_ref)

        chunk = {}
        k_scr = 0
        for j in range(n_in):
            g = j % N_DIL
            if DIL_RATES[g] == 1:
                for ci in range(CHUNKS):
                    chunk[j, ci] = functools.partial(lambda r, ci: r[:, ci * LANES:(ci + 1) * LANES].astype(F32), grads[j], ci)
            else:
                scr = scratch[k_scr]
                k_scr += 1
                _regroup_in(grads[j], scr, DIL_RATES[g], ASM_TILE)
                for ci in range(CHUNKS):
                    chunk[j, ci] = functools.partial(lambda s, ci: s[ci], scr, ci)

        seg = _segment_mean_matrix(HEAD_DIM)
        c, s1, s2 = c_ref[...], s1_ref[...], s2_ref[...]
        o_ref[:, :2 * A_WIDTH] = da_ref[...]
        for jg in range(2 * N_DIL):
            for ci in range(CHUNKS):
                col = jg * B_WIDTH + ci * LANES
                src = slice(2 * A_WIDTH + col, 2 * A_WIDTH + col + LANES)
                xv = p_ref[:, src].astype(F32)
                r = lax.rsqrt(_segment_dot(xv * xv, seg) + EPS)
                xh = xv * r
                gain = g_ref[:, col:col + LANES]
                do = chunk[jg, ci]()
                dy = do * c + pltpu.roll(do * s1, 8, axis=1) + pltpu.roll(do * s2, LANES - 8, axis=1)
                dg_ref[:, col:col + LANES] += jnp.sum(dy * xh, axis=0, keepdims=True)
                dxh = dy * gain
                o_ref[:, src] = (r * (dxh - xh * _segment_dot(dxh * xh, seg))).astype(BF16)
        v0 = 2 * A_WIDTH + QK_COLS
        for g in range(N_DIL):
            for ci in range(CHUNKS):
                col = v0 + g * B_WIDTH + ci * LANES
                o_ref[:, col:col + LANES] = chunk[2 * N_DIL + g, ci]().astype(BF16)

    specs = [_row_spec(ASM_TILE, B_WIDTH) if r == 1 else _regrouped_spec(r, ASM_TILE) for r in DIL_RATES] * 3
    n_scr = 3 * sum(1 for r in DIL_RATES if r > 1)
    tab = _row_spec(ASM_TILE, LANES)
    return _call(
        body, grid=(t // ASM_TILE,),
        in_specs=[_row_spec(ASM_TILE, AB_IN), _row_spec(ASM_TILE, 2 * A_WIDTH)] + specs
        + [_const_spec((1, QK_COLS)), tab, tab, tab],
        out_specs=[_row_spec(ASM_TILE, AB_IN), _const_spec((1, QK_COLS))],
        out_shape=[SDS((t, AB_IN), BF16), SDS((1, QK_COLS), F32)],
        scratch_shapes=[pltpu.VMEM((CHUNKS, ASM_TILE, LANES), F32)] * n_scr,
        operands=[proj, d_a, *dq, *dk, *dv, gains, *tabs], name="dproj_assemble", ride=ride)


def _fold_heads(dg_lane):
    n = dg_lane.shape[1]

    def body(x_ref, o_ref):
        r = lax.broadcasted_iota(jnp.int32, (B_WIDTH, B_WIDTH), 0) % HEAD_DIM
        c = lax.broadcasted_iota(jnp.int32, (B_WIDTH, B_WIDTH), 1) % HEAD_DIM
        fold = jnp.where(r == c, 1.0, 0.0).astype(F32)
        for jg in range(n // B_WIDTH):
            ls = slice(jg * B_WIDTH, (jg + 1) * B_WIDTH)
            o_ref[:, ls] = _dot_hi(jnp.broadcast_to(x_ref[:, ls], (8, B_WIDTH)), fold)

    return pl.pallas_call(body, out_shape=SDS((8, n), F32), name="fold_heads", compiler_params=_params())(dg_lane)


CD_TILE = 256
TAP_ROWS = 64
CD_IN = 2 * C_WIDTH + 3 * 512


def _shifted_copies(src, dst, rows):
    dst[0, :rows] = src[...]
    for b in range(1, 8):
        dst[b, :rows - 8] = src[pl.ds(b, rows - 8), :]


def _rows_from(shifted, start, n, lanes=slice(None)):
    b = start % 8
    return shifted[b, pl.ds(start - b, n), lanes]


def _mixer_cd_fwd(proj, cw, cb, lg, lb, dw):
    t = proj.shape[0]
    per = CD_TILE // HALO

    def body(h_ref, m_ref, cw_ref, cb_ref, lg_ref, lb_ref, dw_ref, o_ref, c1_ref, c_scr, e_scr, c_sh):
        not_first = (pl.program_id(0) > 0).astype(F32)
        lanes = [slice(c * LANES, (c + 1) * LANES) for c in range(C_WIDTH // LANES)]

        def col(ref, part, ls):
            return ref[:, part * C_WIDTH + ls.start:part * C_WIDTH + ls.stop].astype(F32)

        for ls in lanes:
            c_scr[:HALO, ls] = col(h_ref, 0, ls) * _sigmoid(col(h_ref, 1, ls)) * not_first
            c_scr[HALO:, ls] = col(m_ref, 0, ls) * _sigmoid(col(m_ref, 1, ls))
            e_scr[:HALO, ls] = col(h_ref, 3, ls) * col(h_ref, 4, ls) * not_first
            e_scr[HALO:, ls] = col(m_ref, 3, ls) * col(m_ref, 4, ls)
        _shifted_copies(c_scr, c_sh, HALO + CD_TILE)
        for ls in lanes:
            for r0 in range(0, CD_TILE, TAP_ROWS):
                acc = jnp.zeros((TAP_ROWS, LANES), F32)
                for k in range(C_KERNEL):
                    acc = acc + cw_ref[k:k + 1, ls] * _rows_from(c_sh, r0 + HALO - (C_KERNEL - 1) + k, TAP_ROWS, ls)
                c1_ref[r0:r0 + TAP_ROWS, ls] = acc + cb_ref[:, ls]
        mean = sum(jnp.sum(c1_ref[:, ls], axis=-1, keepdims=True) for ls in lanes) * (1.0 / C_WIDTH)
        var = sum(jnp.sum((c1_ref[:, ls] - mean) ** 2, axis=-1, keepdims=True) for ls in lanes) * (1.0 / C_WIDTH)
        rs = lax.rsqrt(var + EPS)
        for ls in lanes:
            c2 = (c1_ref[:, ls] - mean) * rs * lg_ref[:, ls] + lb_ref[:, ls]
            o_ref[:, ls] = (c2 * _sigmoid(c2)).astype(BF16)
            d1 = jnp.zeros((CD_TILE, LANES), F32)
            for k in range(D_KERNEL):
                d1 = d1 + dw_ref[k:k + 1, ls] * e_scr[pl.ds(HALO - (D_KERNEL - 1) + k, CD_TILE), ls]
            o_ref[:, C_WIDTH + ls.start:C_WIDTH + ls.stop] = (col(m_ref, 2, ls) * d1).astype(BF16)

    return pl.pallas_call(
        body, grid=(t // CD_TILE,),
        in_specs=[pl.BlockSpec((HALO, CD_IN), lambda i: (jnp.maximum(i * per - 1, 0), 0)), _row_spec(CD_TILE, CD_IN),
                  _const_spec((32, C_WIDTH)), _const_spec((1, C_WIDTH)), _const_spec((1, C_WIDTH)), _const_spec((1, C_WIDTH)),
                  _const_spec((8, C_WIDTH))],
        out_specs=[_row_spec(CD_TILE, 2 * C_WIDTH), _row_spec(CD_TILE, C_WIDTH)],
        out_shape=[SDS((t, 2 * C_WIDTH), BF16), SDS((t, C_WIDTH), F32)],
        scratch_shapes=[pltpu.VMEM((HALO + CD_TILE, C_WIDTH), F32)] * 2 + [pltpu.VMEM((8, HALO + CD_TILE, C_WIDTH), F32)],
        name="mixer_cd_fwd", compiler_params=_params())(proj, proj, cw, cb, lg, lb, dw)


def _mixer_cd_bwd(proj, dcat, c1, cw, lg, lb, dw, ride=None):
    t = proj.shape[0]
    per = CD_TILE // HALO
    nt = t // CD_TILE
    ext = CD_TILE + HALO

    def body(hp_ref, m_ref, hn_ref, dm_ref, dn_ref, c1m_ref, c1n_ref, cw_ref, lg_ref, lb_ref, dw_ref,
             dp_ref, dcw_ref, dcb_ref, dlg_ref, dlb_ref, ddw_ref, c_scr, e_scr, dc1_scr, dd1_scr, c_sh, dc1_sh, dcw_acc,
             dvh_scr, vhat_scr):
        i = pl.program_id(0)

        @pl.when(i == 0)
        def _():
            for r in (dcw_acc, dcb_ref, dlg_ref, dlb_ref, ddw_ref):
                r[...] = jnp.zeros_like(r)

        not_first = (i > 0).astype(F32)
        not_last = (i < nt - 1).astype(F32)
        main = slice(HALO, HALO + CD_TILE)
        lanes = [slice(c * LANES, (c + 1) * LANES) for c in range(C_WIDTH // LANES)]

        def col(ref, part, ls):
            return ref[:, part * C_WIDTH + ls.start:part * C_WIDTH + ls.stop].astype(F32)

        for ls in lanes:
            c_scr[:HALO, ls] = col(hp_ref, 0, ls) * _sigmoid(col(hp_ref, 1, ls)) * not_first
            c_scr[main, ls] = col(m_ref, 0, ls) * _sigmoid(col(m_ref, 1, ls))
            c_scr[HALO + CD_TILE:, ls] = col(hn_ref, 0, ls) * _sigmoid(col(hn_ref, 1, ls)) * not_last
            e_scr[:HALO, ls] = col(hp_ref, 3, ls) * col(hp_ref, 4, ls) * not_first
            e_scr[main, ls] = col(m_ref, 3, ls) * col(m_ref, 4, ls)
            e_scr[HALO + CD_TILE:, ls] = col(hn_ref, 3, ls) * col(hn_ref, 4, ls) * not_last
        _shifted_copies(c_scr, c_sh, 2 * HALO + CD_TILE)

        def c1_of(ls):
            return jnp.concatenate([c1m_ref[:, ls], c1n_ref[:, ls]], axis=0)

        mean = sum(jnp.sum(c1_of(ls), axis=-1, keepdims=True) for ls in lanes) * (1.0 / C_WIDTH)
        var = sum(jnp.sum((c1_of(ls) - mean) ** 2, axis=-1, keepdims=True) for ls in lanes) * (1.0 / C_WIDTH)
        rs = lax.rsqrt(var + EPS)
        sum_dvh, sum_dvh_vhat = 0.0, 0.0
        for ls in lanes:
            vhat = (c1_of(ls) - mean) * rs
            c2 = vhat * lg_ref[:, ls] + lb_ref[:, ls]
            sig = _sigmoid(c2)
            dc = jnp.concatenate([dm_ref[:, ls], dn_ref[:, ls] * not_last], axis=0)
            dc2 = dc * (sig * (1.0 + c2 * (1.0 - sig)))
            dvh = dc2 * lg_ref[:, ls]
            sum_dvh = sum_dvh + jnp.sum(dvh, axis=-1, keepdims=True)
            sum_dvh_vhat = sum_dvh_vhat + jnp.sum(dvh * vhat, axis=-1, keepdims=True)
            dvh_scr[:, ls] = dvh
            vhat_scr[:, ls] = vhat
            dlg_ref[:, ls] += jnp.sum((dc2 * vhat)[:CD_TILE], axis=0, keepdims=True)
            dlb_ref[:, ls] += jnp.sum(dc2[:CD_TILE], axis=0, keepdims=True)
        for ls in lanes:
            dc1 = rs * (dvh_scr[:, ls] - sum_dvh * (1.0 / C_WIDTH) - vhat_scr[:, ls] * (sum_dvh_vhat * (1.0 / C_WIDTH)))
            dc1_scr[:, ls] = dc1
            dcb_ref[:, ls] += jnp.sum(dc1[:CD_TILE], axis=0, keepdims=True)
        _shifted_copies(dc1_scr, dc1_sh, ext)
        for ls in lanes:
            for r0 in range(0, CD_TILE, TAP_ROWS):
                rows = slice(r0, r0 + TAP_ROWS)
                dc1_m = dc1_scr[rows, ls]
                dc0 = jnp.zeros((TAP_ROWS, LANES), F32)
                for k in range(C_KERNEL):
                    dc0 = dc0 + cw_ref[k:k + 1, ls] * _rows_from(dc1_sh, r0 + C_KERNEL - 1 - k, TAP_ROWS, ls)
                    prod = dc1_m * _rows_from(c_sh, r0 + HALO - (C_KERNEL - 1) + k, TAP_ROWS, ls)
                    dcw_acc[k, :, ls] += prod.reshape(TAP_ROWS // 8, 8, LANES).sum(axis=0)
                g_m = m_ref[rows, C_WIDTH + ls.start:C_WIDTH + ls.stop].astype(F32)
                a_m = m_ref[rows, ls].astype(F32)
                sig_m = _sigmoid(g_m)
                dp_ref[rows, ls] = (dc0 * sig_m).astype(BF16)
                dp_ref[rows, C_WIDTH + ls.start:C_WIDTH + ls.stop] = (dc0 * a_m * sig_m * (1.0 - sig_m)).astype(BF16)

        @pl.when(i == nt - 1)
        def _():
            dcw_ref[...] = jnp.sum(dcw_acc[...], axis=1)

        for ls in lanes:
            wide = slice(C_WIDTH + ls.start, C_WIDTH + ls.stop)
            d1 = jnp.zeros((CD_TILE, LANES), F32)
            for k in range(D_KERNEL):
                d1 = d1 + dw_ref[k:k + 1, ls] * e_scr[pl.ds(HALO - (D_KERNEL - 1) + k, CD_TILE), ls]
            dd_m = dm_ref[:, wide]
            dd1 = jnp.concatenate([dd_m * col(m_ref, 2, ls), dn_ref[:, wide] * col(hn_ref, 2, ls) * not_last], axis=0)
            dd1_scr[:, ls] = dd1
            dp_ref[:, 2 * C_WIDTH + ls.start:2 * C_WIDTH + ls.stop] = (dd_m * d1).astype(BF16)
            de = jnp.zeros((CD_TILE, LANES), F32)
            for k in range(D_KERNEL):
                de = de + dw_ref[k:k + 1, ls] * dd1_scr[pl.ds(D_KERNEL - 1 - k, CD_TILE), ls]
                ddw_ref[k:k + 1, ls] += jnp.sum(dd1[:CD_TILE] * e_scr[pl.ds(HALO - (D_KERNEL - 1) + k, CD_TILE), ls], axis=0, keepdims=True)
            dp_ref[:, 3 * C_WIDTH + ls.start:3 * C_WIDTH + ls.stop] = (de * col(m_ref, 4, ls)).astype(BF16)
            dp_ref[:, 4 * C_WIDTH + ls.start:4 * C_WIDTH + ls.stop] = (de * col(m_ref, 3, ls)).astype(BF16)

    halo_prev = lambda i: (jnp.maximum(i * per - 1, 0), 0)
    halo_next = lambda i: (jnp.minimum((i + 1) * per, t // HALO - 1), 0)
    vec = _const_spec((1, C_WIDTH))
    return _call(
        body, grid=(nt,),
        in_specs=[pl.BlockSpec((HALO, CD_IN), halo_prev), _row_spec(CD_TILE, CD_IN), pl.BlockSpec((HALO, CD_IN), halo_next),
                  _row_spec(CD_TILE, 2 * C_WIDTH), pl.BlockSpec((HALO, 2 * C_WIDTH), halo_next),
                  _row_spec(CD_TILE, C_WIDTH), pl.BlockSpec((HALO, C_WIDTH), halo_next),
                  _const_spec((32, C_WIDTH)), vec, vec, _const_spec((8, C_WIDTH))],
        out_specs=[_row_spec(CD_TILE, CD_IN), _const_spec((32, C_WIDTH)), vec, vec, vec, _const_spec((8, C_WIDTH))],
        out_shape=[SDS((t, CD_IN), BF16), SDS((32, C_WIDTH), F32), SDS((1, C_WIDTH), F32), SDS((1, C_WIDTH), F32),
                   SDS((1, C_WIDTH), F32), SDS((8, C_WIDTH), F32)],
        scratch_shapes=[pltpu.VMEM((2 * HALO + CD_TILE, C_WIDTH), F32)] * 2 + [pltpu.VMEM((ext, C_WIDTH), F32)] * 2
        + [pltpu.VMEM((8, 2 * HALO + CD_TILE, C_WIDTH), F32), pltpu.VMEM((8, ext, C_WIDTH), F32),
           pltpu.VMEM((32, 8, C_WIDTH), F32)] + [pltpu.VMEM((ext, C_WIDTH), F32)] * 2,
        operands=[proj, proj, proj, dcat, dcat, c1, c1, cw, lg, lb, dw], name="mixer_cd_bwd", ride=ride)


def _wgrad(name, pairs, out_rc, t, ride):
    tk = TILES["wgrad"]
    assert tk == t, "the whole contraction has to fit one grid step"
    r, c = out_rc
    n = len(pairs)

    def body(*refs):
        ab, out_refs = refs[:2 * n], refs[2 * n:]
        for j in range(n):
            out_refs[j][...] = _dot(ab[2 * j][...], ab[2 * j + 1][...], TN).astype(BF16)

    operands, in_specs = [], []
    for lhs, lhs_spec, rhs, rhs_spec in pairs:
        operands += [lhs, rhs]
        in_specs += [lhs_spec, rhs_spec]
    res = _call(body, grid=(N_CHIPS, t // tk), in_specs=in_specs,
                out_specs=[pl.BlockSpec((None, r, c), lambda p, k: (p, 0, 0))] * n,
                out_shape=[SDS((N_CHIPS, r, c), BF16)] * n, operands=operands, name=name, ride=ride)
    outs, ride_res = (res, None) if ride is None else res
    outs = [o.reshape(N_CHIPS, 2, r // 2, c) for o in outs]
    return outs if ride is None else (outs, ride_res)


def _wgrad_col_sharded(name, h, dz_list, three_d, ride=None):
    t, d = h.shape
    tk = TILES["wgrad"]
    n4 = dz_list[0].shape[-1] if three_d else dz_list[0].shape[-1] // N_CHIPS
    hs = pl.BlockSpec((tk, d), lambda p, k: (k, 0))
    zs = pl.BlockSpec((None, tk, n4), lambda p, k: (p, k, 0)) if three_d else pl.BlockSpec((tk, n4), lambda p, k: (k, p))
    return _wgrad(name, [(h, hs, dz, zs) for dz in dz_list], (d, n4), t, ride)


def _wgrad_row_sharded(name, a, g, three_d, ride=None):
    many = isinstance(a, (list, tuple))
    a_list = list(a) if many else [a]
    t, d = g.shape
    tk = TILES["wgrad"]
    k4 = a_list[0].shape[-1] if three_d else a_list[0].shape[-1] // N_CHIPS
    a_spec = pl.BlockSpec((None, tk, k4), lambda p, k: (p, k, 0)) if three_d else pl.BlockSpec((tk, k4), lambda p, k: (k, p))
    gs = pl.BlockSpec((tk, d), lambda p, k: (k, 0))
    res = _wgrad(name, [(a_j, a_spec, g, gs) for a_j in a_list], (k4, d), t, ride)
    if many:
        return res
    return res[0] if ride is None else (res[0][0], res[1])


def _mesh_scalars():
    return jnp.stack([lax.axis_index("c"), 2 * lax.axis_index("x") + lax.axis_index("y")]).astype(jnp.int32)


def _stage_own(name, w, layer, dtype, far_slab=False):
    layers, r, cols = w.shape
    h = r // 2

    def body(s_ref, x_ref, o_ref, *unwritten):
        o_ref[...] = x_ref[...].astype(dtype)

    out_specs = [pl.BlockSpec((None, None, h, cols), lambda i, s: (s[1], i, 0, 0))] + [ANY] * far_slab
    out_shape = [SDS((N_CHIPS, 2, h, cols), dtype)] + [SDS((2, h, cols), dtype)] * far_slab
    res = pl.pallas_call(
        body,
        grid_spec=pltpu.PrefetchScalarGridSpec(
            num_scalar_prefetch=1, grid=(2,),
            in_specs=[pl.BlockSpec((None, h, cols), lambda i, s: (2 * layer + i, 0, 0))], out_specs=out_specs),
        out_shape=out_shape, name=name,
        compiler_params=_params())(_mesh_scalars(), w.reshape(2 * layers, h, cols))
    return tuple(res) if far_slab else res[0]


STAGE_STEPS = 4


def _stage_rest_and_norm(x, g, weights, ride=None):
    t, d = x.shape
    n = len(weights)
    views, in_specs, out_specs, out_shapes = [], [], [], []
    for w, layer in weights:
        layers, r, cols = w.shape
        sub = r // STAGE_STEPS
        views.append(w.reshape(layers * STAGE_STEPS, sub, cols))
        in_specs.append(pl.BlockSpec((None, sub, cols), functools.partial(lambda l, i, s: (STAGE_STEPS * l + i, 0, 0), layer)))
        out_specs.append(pl.BlockSpec((None, None, sub, cols), lambda i, s: (s[1], i // 2, i % 2, 0)))
        out_shapes.append(SDS((N_CHIPS, 2, r // 2, cols), BF16))

    def body(s_ref, x_ref, g_ref, *rest):
        w_refs, h_ref, o_refs = rest[:n], rest[n], rest[n + 1:]
        h_ref[...] = _rms_rows(x_ref[...], g_ref[...]).astype(BF16)
        for w_ref, o_ref in zip(w_refs, o_refs):
            o_ref[...] = w_ref[...].astype(BF16)

    tm = t // STAGE_STEPS
    res = _call(
        body, grid=(STAGE_STEPS,), in_specs=[pl.BlockSpec((tm, d), lambda i, s: (i, 0)), pl.BlockSpec((1, d), lambda i, s: (0, 0))] + in_specs,
        out_specs=[pl.BlockSpec((tm, d), lambda i, s: (i, 0))] + out_specs, out_shape=[SDS((t, d), BF16)] + out_shapes,
        operands=[x, g] + views, name="stage_and_norm", ride=ride, prefetch=_mesh_scalars())
    outs, ride_res = (res, None) if ride is None else res
    result = (outs[0], list(outs[1:]))
    return result if ride is None else (result, ride_res)


def _remote(src, dst, send_sem, recv_sem, device):
    return pltpu.make_async_remote_copy(src, dst, send_sem, recv_sem, device_id=device, device_id_type=MESH)


ALL_PEERS = (0, 1, 2)
NEIGHBOURS = (0, 1)


def _ride_gather_send(bufs, peers=ALL_PEERS):
    n = len(bufs)

    def each(b, sems, act):
        send, recv = sems
        x, y, c, p, others = _position()
        for t in range(n):
            for j in peers:
                qx, qy = others[j]
                act(b[t].at[p, c], b[t].at[2 * qx + qy, c], send.at[t, j], recv.at[t, j], (qx, qy, c))

    def start(ins, b, new, sems):
        each(b, sems, lambda mine, landed, s, r, dev: _remote(mine, mine, s, r, dev).start())

    def finish(ins, b, new, sems):
        def act(mine, landed, s, r, dev):
            _remote(mine, mine, s, r, dev).wait_send()
            _remote(landed, landed, s, r, dev).wait_recv()
        each(b, sems, act)

    return _Ride([], bufs, [], [(n, 3), (n, 3)], start, finish, ["chips"])


def _ride_gather_pass(bufs, peers=ALL_PEERS):
    n = len(bufs)

    def each(b, sems, act):
        send, recv = sems
        x, y, c, p, others = _position()
        for t in range(n):
            for j in peers:
                qx, qy = others[j]
                act(b[t].at[2 * qx + qy, c], b[t].at[2 * qx + qy, 1 - c], send.at[t, j], recv.at[t, j], (x, y, 1 - c))

    def start(ins, b, new, sems):
        each(b, sems, lambda landed, passed, s, r, dev: _remote(landed, landed, s, r, dev).start())

    def finish(ins, b, new, sems):
        def act(landed, passed, s, r, dev):
            _remote(landed, landed, s, r, dev).wait_send()
            _remote(passed, passed, s, r, dev).wait_recv()
        each(b, sems, act)

    return _Ride([], bufs, [], [(n, 3), (n, 3)], start, finish, ["sibling"])


def _ride_gather(bufs, peers=ALL_PEERS):
    send, onward = _ride_gather_send(bufs, peers), _ride_gather_pass(bufs, peers)
    n_send = len(send.sem_shapes)

    def start(ins, b, new, sems):
        send.start(ins, b, new, sems[:n_send])

    def finish(ins, b, new, sems):
        send.finish(ins, b, new, sems[:n_send])
        onward.start(ins, b, new, sems[n_send:])
        onward.finish(ins, b, new, sems[n_send:])

    return _Ride([], bufs, [], send.sem_shapes + onward.sem_shapes, start, finish, ["sibling", "chips"])


def _ride_gather_far(sources, slabs):
    n = len(slabs)

    def hops(ins, b, sems):
        x, y, c, p, others = _position()
        qx, qy = others[2]
        for t in range(n):
            far = (ins[t].at[p, c], b[t].at[c], sems[0].at[t], sems[1].at[t], (qx, qy, c))
            onward = (b[t].at[c], b[t].at[1 - c], sems[2].at[t], sems[3].at[t], (x, y, 1 - c))
            yield far, onward

    def wait(mine, landed, s, r, dev):
        _remote(mine, mine, s, r, dev).wait_send()
        _remote(landed, landed, s, r, dev).wait_recv()

    def start(ins, b, new, sems):
        for (mine, landed, s, r, dev), _ in hops(ins, b, sems):
            _remote(mine, landed, s, r, dev).start()

    def finish(ins, b, new, sems):
        for far, _ in hops(ins, b, sems):
            wait(*far)
        for _, (landed, passed, s, r, dev) in hops(ins, b, sems):
            _remote(landed, landed, s, r, dev).start()
        for _, onward in hops(ins, b, sems):
            wait(*onward)

    return _Ride(sources, slabs, [], [(n,)] * 4, start, finish, ["sibling", "chips"])


def _ride_swap(tensors):
    n = len(tensors)

    def each(ins, new, sems, act):
        send, recv = sems
        x, y, c, _, _ = _position()
        for t in range(n):
            act(_remote(ins[t].at[:, 1 - c], new[t], send.at[t], recv.at[t], (x, y, 1 - c)))

    def start(ins, b, new, sems):
        each(ins, new, sems, lambda cp: cp.start())

    def finish(ins, b, new, sems):
        each(ins, new, sems, lambda cp: cp.wait())

    return _Ride(tensors, [], [SDS((s.shape[0],) + s.shape[2:], s.dtype) for s in tensors], [(n,), (n,)], start, finish,
                 ["sibling"])


def _ride_scatter(tensors, landing):
    n = len(tensors)

    def each(ins, b, sems, act):
        send, recv = sems
        x, y, c, p, others = _position()
        for t in range(n):
            for j, (qx, qy) in enumerate(others):
                q = 2 * qx + qy
                act(ins[t].at[q], b[t].at[p], b[t].at[q], send.at[t, j], recv.at[t, j], (qx, qy, c))

    def start(ins, b, new, sems):
        each(ins, b, sems, lambda src, dst, landed, s, r, dev: _remote(src, dst, s, r, dev).start())

    def finish(ins, b, new, sems):
        def act(src, dst, landed, s, r, dev):
            _remote(src, dst, s, r, dev).wait_send()
            _remote(landed, landed, s, r, dev).wait_recv()
        each(ins, b, sems, act)

    return _Ride(tensors, landing, [], [(n, 3), (n, 3)], start, finish, ["chips"])


def _ride_join(bufs):
    n = len(bufs)

    def each(b, sems, act):
        send, recv = sems
        x, y, c, _, _ = _position()
        for t in range(n):
            act(b[t].at[c], b[t].at[1 - c], send.at[t], recv.at[t], (x, y, 1 - c))

    def start(ins, b, new, sems):
        each(b, sems, lambda mine, theirs, s, r, dev: _remote(mine, mine, s, r, dev).start())

    def finish(ins, b, new, sems):
        def act(mine, theirs, s, r, dev):
            _remote(mine, mine, s, r, dev).wait_send()
            _remote(theirs, theirs, s, r, dev).wait_recv()
        each(b, sems, act)

    return _Ride([], bufs, [], [(n,), (n,)], start, finish, ["sibling"])


def _all_reduce_small(pack, ride=None):
    rows = pack.shape[0]
    n_dev = 2 * N_CHIPS
    n_rb = 0 if ride is None else len(ride.bufs)

    def body(x_ref, *rest):
        o_ref = rest[n_rb]
        r_bufs = rest[n_rb + 1:2 * n_rb + 1]
        land, send, recv = rest[2 * n_rb + 1:2 * n_rb + 4]
        r_sems = rest[2 * n_rb + 4:]
        if ride is not None:
            ride.start([], r_bufs, [], r_sems)
        x, y, c, p, _ = _position()
        me = 2 * p + c
        land[me] = x_ref[...]
        peers = [(dx, dy, dc) for dx in range(2) for dy in range(2) for dc in range(2) if (dx, dy, dc) != (0, 0, 0)]
        for j, (dx, dy, dc) in enumerate(peers):
            _remote(land.at[me], land.at[me], send.at[j], recv.at[j], (x ^ dx, y ^ dy, c ^ dc)).start()
        for j, (dx, dy, dc) in enumerate(peers):
            src = 4 * (x ^ dx) + 2 * (y ^ dy) + (c ^ dc)
            _remote(land.at[me], land.at[me], send.at[j], recv.at[j], (x ^ dx, y ^ dy, c ^ dc)).wait_send()
            _remote(land.at[src], land.at[src], send.at[j], recv.at[j], (x ^ dx, y ^ dy, c ^ dc)).wait_recv()
        acc = land[0]
        for dev in range(1, n_dev):
            acc = acc + land[dev]
        o_ref[...] = acc
        if ride is not None:
            ride.finish([], r_bufs, [], r_sems)

    bufs = [] if ride is None else ride.bufs
    sems = [] if ride is None else [pltpu.SemaphoreType.DMA(s) for s in ride.sem_shapes]
    res = pl.pallas_call(
        body, in_specs=[pl.BlockSpec(memory_space=pltpu.VMEM)] + [ANY] * n_rb,
        out_specs=[pl.BlockSpec(memory_space=pltpu.VMEM)] + [ANY] * n_rb,
        out_shape=[SDS((rows, LANES), F32)] + [SDS(b.shape, b.dtype) for b in bufs],
        scratch_shapes=[pltpu.VMEM((n_dev, rows, LANES), F32), pltpu.SemaphoreType.DMA((n_dev - 1,)),
                        pltpu.SemaphoreType.DMA((n_dev - 1,))] + sems,
        input_output_aliases={1 + j: 1 + j for j in range(n_rb)},
        name="all_reduce_small", compiler_params=_params())(pack, *bufs)
    return res[0], list(res[1:])


def _add_own_half(name, fulls, recvs, out_dtypes):
    n = len(fulls)
    in_specs, out_specs, out_shapes = [], [], []
    for full, dtype in zip(fulls, out_dtypes):
        n4, _, h, cols = full.shape
        in_specs += [pl.BlockSpec((None, None, h, cols), lambda q, s: (q, s[0], 0, 0)),
                     pl.BlockSpec((None, h, cols), lambda q, s: (q, 0, 0))]
        out_specs += [pl.BlockSpec((None, h, cols), lambda q, s: (q, 0, 0)),
                      pl.BlockSpec((None, h, cols), lambda q, s: (s[1], 0, 0))]
        out_shapes += [SDS((n4, h, cols), dtype)] * 2

    def body(s_ref, *refs):
        ins, outs = refs[:2 * n], refs[2 * n:]
        for j in range(n):
            o_ref, own_ref = outs[2 * j], outs[2 * j + 1]
            v = (ins[2 * j][...].astype(F32) + ins[2 * j + 1][...].astype(F32)).astype(o_ref.dtype)
            o_ref[...] = v

            @pl.when(pl.program_id(0) == s_ref[1])
            def _():
                own_ref[...] = v

    res = pl.pallas_call(
        body,
        grid_spec=pltpu.PrefetchScalarGridSpec(num_scalar_prefetch=1, grid=(N_CHIPS,), in_specs=in_specs, out_specs=out_specs),
        out_shape=out_shapes, name=name, compiler_params=_params())(
            _mesh_scalars(), *[a for pair in zip(fulls, recvs) for a in pair])
    return [(res[2 * j], res[2 * j + 1]) for j in range(n)]


def _sum_chips(name, parts_list):
    steps = 4 if all(parts.shape[1] % 64 == 0 for parts in parts_list) else 1
    in_specs, out_specs, out_shapes = [], [], []
    for parts in parts_list:
        n4, h, cols = parts.shape
        in_specs.append(pl.BlockSpec((n4, h // steps, cols), lambda i, s: (0, i, 0)))
        out_specs.append(pl.BlockSpec((None, h // steps, cols), lambda i, s: (s[0], i, 0)))
        out_shapes.append(SDS((2, h, cols), F32))
    n = len(parts_list)

    def body(s_ref, *refs):
        for a_ref, o_ref in zip(refs[:n], refs[n:]):
            acc = a_ref[0].astype(F32)
            for q in range(1, N_CHIPS):
                acc = acc + a_ref[q].astype(F32)
            o_ref[...] = acc

    return pl.pallas_call(
        body,
        grid_spec=pltpu.PrefetchScalarGridSpec(num_scalar_prefetch=1, grid=(steps,), in_specs=in_specs, out_specs=out_specs),
        out_shape=out_shapes, name=name, compiler_params=_params())(_mesh_scalars(), *parts_list)


def _adamw_math(w, g, m, v):
    m2 = ADAM_B1 * m + (1.0 - ADAM_B1) * g
    v2 = ADAM_B2 * v + (1.0 - ADAM_B2) * (g * g)
    m_hat = m2 / (1.0 - ADAM_B1 ** ADAM_STEP)
    v_hat = v2 / (1.0 - ADAM_B2 ** ADAM_STEP)
    delta = -ADAM_LR * (m_hat / (jnp.sqrt(v_hat) + ADAM_EPS) + ADAM_WD * w)
    return delta, m2, v2


def _row_tile(rows, cols):
    cap = max(8, (1 << 18) // cols)
    best = 8
    for cand in range(8, min(rows, cap) + 1, 8):
        if rows % cand == 0:
            best = cand
    return best


def _adamw_big(name, w, g_layers, m, v):
    layers, rows, cols = w.shape
    tr = _row_tile(rows, cols)

    def body(w_ref, m_ref, v_ref, *rest):
        g_refs, (g_o, d_o, m_o, v_o) = rest[:layers], rest[layers:]
        gv = g_refs[0][...]
        for layer in range(1, layers):
            gv = jnp.where(pl.program_id(0) == layer, g_refs[layer][...], gv)
        d, mm, vv = _adamw_math(w_ref[...], gv, m_ref[...], v_ref[...])
        g_o[...] = gv
        d_o[...] = d
        m_o[...] = mm
        v_o[...] = vv

    blk = pl.BlockSpec((None, tr, cols), lambda l, i: (l, i, 0))
    g_blk = pl.BlockSpec((tr, cols), lambda l, i: (i, 0))
    return tuple(pl.pallas_call(
        body, grid=(layers, rows // tr), in_specs=[blk] * 3 + [g_blk] * layers, out_specs=[blk] * 4,
        out_shape=[SDS((layers, rows, cols), F32)] * 4, name=name,
        compiler_params=_params())(w, m, v, *[g.reshape(rows, cols) for g in g_layers]))


def _adamw_small(ws, gs, ms, vs):
    n = len(ws)
    flat = []
    for group in (ws, gs, ms, vs):
        flat += [a.reshape(-1, a.shape[-1]) for a in group]

    def body(*refs):
        w_r, g_r, m_r, v_r = refs[:n], refs[n:2 * n], refs[2 * n:3 * n], refs[3 * n:4 * n]
        d_o, m_o, v_o = refs[4 * n:5 * n], refs[5 * n:6 * n], refs[6 * n:7 * n]
        for j in range(n):
            d, mm, vv = _adamw_math(w_r[j][...], g_r[j][...], m_r[j][...], v_r[j][...])
            d_o[j][...] = d
            m_o[j][...] = mm
            v_o[j][...] = vv

    shapes = [SDS(a.shape, F32) for a in flat[:n]]
    outs = pl.pallas_call(body, out_shape=shapes * 3, name="adamw_small", compiler_params=_params())(*flat)
    res = []
    for k in range(3):
        res.append([outs[k * n + j].reshape(ws[j].shape) for j in range(n)])
    return res


BIG = ("ab_w_in", "ab_w_out", "cd_w_in", "cd_w_out", "ffn_w_gate", "ffn_w_up", "ffn_w_down")
V_BLOCK = (2 * A_WIDTH + QK_COLS) // B_WIDTH


def _pad_rows(a, rows):
    return jnp.pad(a, ((0, rows - a.shape[0]), (0, 0)))


A_IN, A_OUT, C_IN, C_OUT = ("ab_w_in", 0), ("ab_w_out", 0), ("cd_w_in", 0), ("cd_w_out", 0)
G0, U0, D0 = ("ffn_w_gate", 0), ("ffn_w_up", 0), ("ffn_w_down", 0)
G1, U1, D1 = ("ffn_w_gate", 1), ("ffn_w_up", 1), ("ffn_w_down", 1)
UNITS = (A_IN, A_OUT, G0, U0, D0, C_IN, C_OUT, G1, U1, D1)
ROWS_MINOR = ("ffn_w_gate", "ffn_w_up")
SMALL_SHARDED = ("small", 0)
REPLICATED_UNIT = ("replicated", 0)


class _Exchange:
    def __init__(self, enabled):
        self.enabled = enabled
        self.w, self.grad, self.recv, self.half, self.land, self.done = {}, {}, {}, {}, {}, {}
        self.far = {}

    def full(self, unit):
        b = self.w[unit]
        return b.reshape(N_CHIPS, 1, 2 * b.shape[2], b.shape[3])

    def ride_for(self, phases):
        rides, sinks = [], []
        for kind, units in phases:
            if kind == "send":
                rides.append(_ride_gather_send([self.w[u] for u in units]))
                sinks.append(self.w)
            elif kind == "pass":
                rides.append(_ride_gather_pass([self.w[u] for u in units]))
                sinks.append(self.w)
            elif kind == "gather":
                rides.append(_ride_gather([self.w[u] for u in units]))
                sinks.append(self.w)
            elif kind == "gather_near":
                rides.append(_ride_gather([self.w[u] for u in units], NEIGHBOURS))
                sinks.append(self.w)
            elif kind == "gather_far":
                rides.append(_ride_gather_far([self.w[u] for u in units], [self.far[u] for u in units]))
                sinks.append(self.far)
            elif kind == "swap":
                rides.append(_ride_swap([self.grad[u] for u in units]))
                sinks.append(self.recv)
            elif kind == "scatter":
                rides.append(_ride_scatter([self.half[u] for u in units], [self.land[u] for u in units]))
                sinks.append(self.land)
            else:
                rides.append(_ride_join([self.done[u] for u in units]))
                sinks.append(self.done)
        ride = functools.reduce(_ride_both, rides)

        def settle(res):
            n_bufs = sum(len(r.bufs) for r in rides)
            bufs, new = list(res[:n_bufs]), list(res[n_bufs:])
            for r, sink, (_, units) in zip(rides, sinks, phases):
                vals = [bufs.pop(0) for _ in r.bufs] + [new.pop(0) for _ in r.new_outs]
                for u, v in zip(units, vals):
                    sink[u] = v

        return ride, settle

    def run(self, fn, *args, phases=(), **kw):
        if not self.enabled or not phases:
            return fn(*args, **kw)
        ride, settle = self.ride_for(phases)
        out, res = fn(*args, ride=ride, **kw)
        settle(res)
        return out

    def alone(self, name, phases):
        if self.enabled:
            ride, settle = self.ride_for(phases)
            settle(_run_ride(name, ride))

    def pair_sum(self, units):
        if self.enabled:
            dtypes = [F32 if u in (SMALL_SHARDED, REPLICATED_UNIT) else BF16 for u in units]
            res = _add_own_half(f"pair_sum_{units[0][0]}_{units[0][1]}", [self.grad[u] for u in units],
                                [self.recv[u] for u in units], dtypes)
            for u, (half, land) in zip(units, res):
                self.half[u], self.land[u] = half, land

    def chip_sum(self, units):
        if self.enabled:
            res = _sum_chips(f"chip_sum_{units[0][0]}_{units[0][1]}", [self.land[u] for u in units])
            self.done.update(zip(units, res))


def _local_step(x, target, ex, sp, h0=None):
    t, d = x.shape
    tabs = _rope_tables(t)
    gains = jnp.concatenate([jnp.tile(sp["q_norm_g"][g], HEAD_DIM // 8) for g in range(N_DIL)]
                            + [jnp.tile(sp["k_norm_g"][g], HEAD_DIM // 8) for g in range(N_DIL)]).reshape(1, QK_COLS)
    bias_t = sp["sgu_bias"].T
    cw = _pad_rows(sp["conv_c_w"], 32)
    dw = _pad_rows(sp["conv_d_w"], 8)
    cb, clg, clb = (sp[k].reshape(1, C_WIDTH) for k in ("conv_c_b", "c_ln_g", "c_ln_b"))
    slg, slb = sp["sgu_norm_g"].reshape(1, A_WIDTH), sp["sgu_norm_b"].reshape(1, A_WIDTH)
    g_ab, g_cd = sp["ab_norm_g"].reshape(1, d), sp["cd_norm_g"].reshape(1, d)
    g_f0, g_f1 = sp["ffn_norm_g"][0:1], sp["ffn_norm_g"][1:2]
    run = ex.run

    def w2d(unit):
        return ex.full(unit).reshape(-1, d)

    if h0 is None:
        h0 = _rms_fwd("rms_ab", x, g_ab)
    if ex.enabled:
        proj = run(_proj_in_near, "proj_ab_near", h0, ex.full(A_IN), phases=[("gather_far", [A_IN]), ("send", [A_OUT])])
        proj, ex.w[A_IN] = _proj_in_far("proj_ab_far", h0, ex.far[A_IN], proj, ex.w[A_IN])
    else:
        proj = _proj_in("proj_ab", h0, ex.full(A_IN), 0)
    a_out = _mixer_a_fwd(proj, slg, slb, sp["sgu_w"], bias_t)
    qk, q1, q2, k1, k2 = run(_qk_fwd, proj, gains, tabs, phases=[("pass", [A_OUT]), ("send", [G0, C_OUT])])
    regrouped_qk = {1: (q1, k1), 2: (q2, k2)}
    fwd_phases = ([("pass", [G0, C_OUT]), ("send", [U0])], [("pass", [U0]), ("send", [D0])],
                  [("pass", [D0]), ("send", [C_IN])])
    qkv, o_list, l_list = [], [], []
    for g, rate in enumerate(DIL_RATES):
        if rate == 1:
            qk3, proj3 = qk.reshape(1, t, QK_COLS), proj.reshape(1, t, AB_IN)
            q, k, v = (qk3, g), (qk3, N_DIL + g), (proj3, V_BLOCK + g)
        else:
            vp, = _permute(f"regroup_v_{g}", [(proj, V_BLOCK + g)], rate)
            q, k, v = (regrouped_qk[g][0], 0), (regrouped_qk[g][1], 0), (vp, 0)
        qkv.append((q, k, v))
        o, l = run(_attn_fwd, f"attn_fwd_{g}", q, k, v, phases=fwd_phases[g])
        if rate == 1:
            o, l = o.reshape(t, B_WIDTH), l.reshape(t, B_WIDTH)
        o_list.append(o)
        l_list.append(l)
    cat, lse_tot, lse_1, lse_2 = _attn_merge(a_out, o_list, l_list)
    x1, hf0 = _proj_out("out_ab", cat, w2d(A_OUT), x, g_next=g_f0)
    fgate0, fup0, act0 = run(_ffn_in, "ffn_in_0", hf0, ex.full(G0), ex.full(U0), 0,
                           phases=[("pass", [C_IN]), ("send", [D1, G1])])
    x2, h1 = run(_ffn_out, "ffn_out_0", act0, ex.full(D0), 0, x1, g_next=g_cd, phases=[("pass", [D1, G1]), ("send", [U1])])
    projcd = run(_proj_in, "proj_cd", h1, ex.full(C_IN), 0, phases=[("pass", [U1])])
    cat2, c1 = _mixer_cd_fwd(projcd, cw, cb, clg, clb, dw)
    x3, hf1 = _proj_out("out_cd", cat2, w2d(C_OUT), x2, g_next=g_f1)
    fgate1, fup1, act1 = _ffn_in("ffn_in_1", hf1, ex.full(G1), ex.full(U1), 0)
    dy, loss_acc, dy_b = _ffn_out("ffn_out_1", act1, ex.full(D1), 0, x3, target=target)
    loss = 0.5 * loss_acc[0, 0] / d

    late = [D1, G1, U1]
    dgate, dup = _ffn_dact("ffn_dact_1", dy_b, ex.full(D1), 0, fgate1, fup1)
    ex.grad[D1] = _wgrad_row_sharded("wgrad_down_1", act1, dy_b, True)
    ex.grad[G1] = _wgrad_row_sharded("wgrad_gate_1", dgate, hf1, True)
    ex.grad[U1] = _wgrad_row_sharded("wgrad_up_1", dup, hf1, True)
    g3, d_f1, g3_b = run(_dgrad_cols, "dgrad_ffn_1", [dgate, dup], [ex.full(G1), ex.full(U1)], 0, True, x3, g_f1, dy,
                         w_rows=True, phases=[("swap", late)])
    ex.pair_sum(late)

    dcat2 = _dgrad_rows("dgrad_out_cd", g3_b, w2d(C_OUT))
    ex.grad[C_OUT] = _wgrad_row_sharded("wgrad_out_cd", cat2, g3_b, False)
    dprojcd, d_cw, d_cb, d_clg, d_clb, d_dw = run(_mixer_cd_bwd, projcd, dcat2, c1, cw, clg, clb, dw, phases=[("scatter", late)])
    ex.chip_sum(late)
    ex.grad[C_IN] = _wgrad_col_sharded("wgrad_in_cd", h1, [dprojcd], False)[0]
    g2, d_cdn, g2_b = run(_dgrad_cols, "dgrad_in_cd", [dprojcd], [ex.full(C_IN)], 0, False, x2, g_cd, g3,
                          phases=[("join", late), ("swap", [C_OUT, C_IN])])
    ex.pair_sum([C_OUT, C_IN])

    dgate, dup = run(_ffn_dact, "ffn_dact_0", g2_b, ex.full(D0), 0, fgate0, fup0, phases=[("scatter", [C_OUT, C_IN])])
    ex.chip_sum([C_OUT, C_IN])
    ex.grad[D0] = _wgrad_row_sharded("wgrad_down_0", act0, g2_b, True)
    ex.grad[G0] = _wgrad_row_sharded("wgrad_gate_0", dgate, hf0, True)
    ex.grad[U0] = _wgrad_row_sharded("wgrad_up_0", dup, hf0, True)
    small = {"cd_norm_g": d_cdn, "conv_c_w": d_cw[:C_KERNEL], "conv_c_b": d_cb, "c_ln_g": d_clg, "c_ln_b": d_clb,
             "conv_d_w": d_dw[:D_KERNEL]}
    ex.grad[SMALL_SHARDED] = _split_full_small(small).reshape(N_CHIPS, 2, SHARDED_ROWS // 2, LANES)
    mid = [D0, G0, U0, SMALL_SHARDED]
    g1, d_f0, g1_b = run(_dgrad_cols, "dgrad_ffn_0", [dgate, dup], [ex.full(G0), ex.full(U0)], 0, True, x1, g_f0, g2,
                         w_rows=True, phases=[("join", [C_OUT, C_IN]), ("swap", mid)])
    ex.pair_sum(mid)

    dcat = _dgrad_rows("dgrad_out_ab", g1_b, w2d(A_OUT))
    ex.grad[A_OUT] = _wgrad_row_sharded("wgrad_out_ab", cat, g1_b, False)
    d_a, d_sw, d_sbt, d_slg, d_slb = _mixer_a_bwd(proj, dcat, slg, slb, sp["sgu_w"], bias_t)
    early = {"sgu_norm_g": d_slg, "sgu_norm_b": d_slb, "sgu_w": d_sw, "sgu_bias": d_sbt.T}
    ex.grad[REPLICATED_UNIT] = jnp.broadcast_to(
        _pack_replicated(early, REPLICATED_EARLY, REPLICATED_EARLY_ROWS).reshape(2, REPLICATED_EARLY_ROWS // 2, LANES),
        (N_CHIPS, 2, REPLICATED_EARLY_ROWS // 2, LANES))
    last = [A_OUT, REPLICATED_UNIT]
    dbb, dd, db_1, dd_1, db_2, dd_2 = _attn_bwd_prep(dcat, cat)
    regrouped_bwd = {1: (db_1, lse_1, dd_1), 2: (db_2, lse_2, dd_2)}
    bwd_phases = ([("scatter", [D0, SMALL_SHARDED])],
                  [("scatter", [G0]), ("join", [D0, SMALL_SHARDED]), ("swap", last)],
                  [("scatter", [U0]), ("join", [G0])])
    dqs, dks, dvs = [], [], []
    for g, rate in enumerate(DIL_RATES):
        q, k, v = qkv[g]
        if rate == 1:
            db3, l3, dd3 = (a.reshape(1, t, B_WIDTH) for a in (dbb, lse_tot, dd))
        else:
            db3, l3, dd3 = regrouped_bwd[g]
        if g == 1:
            ex.chip_sum([D0, SMALL_SHARDED])
        elif g == 2:
            ex.chip_sum([G0])
            ex.pair_sum(last)
        dq, dk, dv = run(_attn_bwd, f"attn_bwd_{g}", q, k, v, db3, l3, dd3, phases=bwd_phases[g])
        if rate == 1:
            dq, dk, dv = (a.reshape(t, B_WIDTH) for a in (dq, dk, dv))
        dqs.append(dq)
        dks.append(dk)
        dvs.append(dv)
    ex.chip_sum([U0])
    dproj, d_gains = run(_dproj_assemble, proj, d_a, dqs, dks, dvs, gains, tabs, phases=[("scatter", last), ("join", [U0])])
    ex.chip_sum(last)
    d_gains = _fold_heads(d_gains)[0].reshape(2, N_DIL, B_WIDTH)[:, :, :HEAD_DIM]
    ex.grad[A_IN] = _wgrad_col_sharded("wgrad_in_ab", h0, [dproj], False)[0]
    ex.alone("swap_last", [("swap", [A_IN])])
    ex.pair_sum([A_IN])
    gx, d_abn = run(_dgrad_cols, "dgrad_in_ab", [dproj], [ex.full(A_IN)], 0, False, x, g_ab, g1, bf16_copy=False,
                    phases=[("join", last), ("scatter", [A_IN])])
    ex.chip_sum([A_IN])

    small.update({
        "ab_norm_g": d_abn, "sgu_norm_g": d_slg, "sgu_norm_b": d_slb, "sgu_w": d_sw, "sgu_bias": d_sbt.T,
        "q_norm_g": d_gains[0], "k_norm_g": d_gains[1], "ffn_norm_g": jnp.concatenate([d_f0, d_f1], axis=0),
    })
    return loss, gx, small


SHARDED_SMALL = ("cd_norm_g", "conv_c_w", "conv_c_b", "c_ln_g", "c_ln_b", "conv_d_w")
SHARDED_ROWS = 48
REPLICATED_EARLY = ("sgu_norm_g", "sgu_norm_b", "sgu_w", "sgu_bias")
REPLICATED_EARLY_ROWS = 528
REPLICATED_LATE = ("ab_norm_g", "q_norm_g", "k_norm_g", "ffn_norm_g", "loss")
REPLICATED_LATE_ROWS = 32
REPLICATED_SMALL = REPLICATED_EARLY + REPLICATED_LATE[:-1]


def _pack_sharded(parts):
    rows = [parts[k].reshape(-1, LANES) for k in SHARDED_SMALL]
    return _pad_rows(jnp.concatenate(rows, axis=0), SHARDED_ROWS)


def _split_full_small(small):
    per_chip = []
    for q in range(N_CHIPS):
        parts = {}
        for k in SHARDED_SMALL:
            a = small[k]
            a = a.reshape(-1, a.shape[-1])
            n = a.shape[-1] // N_CHIPS
            parts[k] = a[:, q * n:(q + 1) * n]
        per_chip.append(_pack_sharded(parts))
    return jnp.stack(per_chip)


def _unpack_sharded(pack, shapes):
    out, r = {}, 0
    for k in SHARDED_SMALL:
        n = math.prod(shapes[k]) // LANES
        out[k] = pack[r:r + n].reshape(shapes[k])
        r += n
    return out


def _gathered_small(packs, shapes):
    per_chip = [_unpack_sharded(packs[q], shapes) for q in range(N_CHIPS)]
    return {k: jnp.concatenate([pc[k] for pc in per_chip], axis=-1) for k in SHARDED_SMALL}


def _pack_replicated(small, names, total_rows):
    rows = []
    for k in names:
        a = small[k].reshape(-1)
        a = jnp.pad(a, (0, (-a.shape[0]) % LANES))
        rows.append(a.reshape(-1, LANES))
    return _pad_rows(jnp.concatenate(rows, axis=0), total_rows)


def _unpack_replicated(pack, shapes, names):
    out, r = {}, 0
    for k in names:
        size = math.prod(shapes[k])
        n = -(-size // LANES)
        out[k] = pack[r:r + n].reshape(-1)[:size].reshape(shapes[k])
        r += n
    return out


WEIGHT_ORDER = ("ab_norm_g", "ab_w_in", "sgu_norm_g", "sgu_norm_b", "sgu_w", "sgu_bias", "q_norm_g", "k_norm_g", "ab_w_out",
                "cd_norm_g", "cd_w_in", "conv_c_w", "conv_c_b", "c_ln_g", "c_ln_b", "conv_d_w", "cd_w_out", "ffn_norm_g",
                "ffn_w_gate", "ffn_w_up", "ffn_w_down")


def kernel(x, ab_norm_g, ab_w_in, sgu_norm_g, sgu_norm_b, sgu_w, sgu_bias, q_norm_g, k_norm_g, ab_w_out, cd_norm_g, cd_w_in, conv_c_w, conv_c_b, c_ln_g, c_ln_b, conv_d_w, cd_w_out, ffn_norm_g, ffn_w_gate, ffn_w_up, ffn_w_down, loss_target, m_ab_norm_g, m_ab_w_in, m_sgu_norm_g, m_sgu_norm_b, m_sgu_w, m_sgu_bias, m_q_norm_g, m_k_norm_g, m_ab_w_out, m_cd_norm_g, m_cd_w_in, m_conv_c_w, m_conv_c_b, m_c_ln_g, m_c_ln_b, m_conv_d_w, m_cd_w_out, m_ffn_norm_g, m_ffn_w_gate, m_ffn_w_up, m_ffn_w_down, v_ab_norm_g, v_ab_w_in, v_sgu_norm_g, v_sgu_norm_b, v_sgu_w, v_sgu_bias, v_q_norm_g, v_k_norm_g, v_ab_w_out, v_cd_norm_g, v_cd_w_in, v_conv_c_w, v_conv_c_b, v_c_ln_g, v_c_ln_b, v_conv_d_w, v_cd_w_out, v_ffn_norm_g, v_ffn_w_gate, v_ffn_w_up, v_ffn_w_down):
    args = dict(locals())
    ws = {k: args[k] for k in WEIGHT_ORDER}
    ms = {k: args["m_" + k] for k in WEIGHT_ORDER}
    vs = {k: args["v_" + k] for k in WEIGHT_ORDER}
    small_names = [k for k in WEIGHT_ORDER if k not in BIG]
    t, d = x.shape[1:]

    for group in (ws, ms, vs):
        for k in ROWS_MINOR:
            group[k] = jnp.swapaxes(group[k], 1, 2)
    ex = _Exchange(enabled=True)
    ex.w[A_IN], ex.far[A_IN] = _stage_own("stage_ab_w_in", ws["ab_w_in"], 0, BF16, far_slab=True)
    own_small = _pack_sharded({k: ws[k][0] for k in SHARDED_SMALL})
    ex.w[SMALL_SHARDED] = _stage_own("stage_small", own_small[None], 0, F32)
    rest = [u for u in UNITS if u != A_IN]
    x2 = x.reshape(t, d)
    h0, staged = ex.run(_stage_rest_and_norm, x2, ws["ab_norm_g"], [(ws[name], layer) for name, layer in rest],
                        phases=[("gather_near", [A_IN]), ("gather", [SMALL_SHARDED])])
    ex.w.update(zip(rest, staged))
    sp = _gathered_small(ex.w[SMALL_SHARDED].reshape(N_CHIPS, SHARDED_ROWS, LANES), {k: ws[k].shape[1:] for k in SHARDED_SMALL})
    for k in REPLICATED_SMALL:
        sp[k] = ws[k] if k == "ffn_norm_g" else ws[k][0]

    loss, grad_x, g_small = _local_step(x2, loss_target.reshape(t, d), ex, sp, h0)

    shapes = {k: ws[k].shape for k in REPLICATED_SMALL}
    shapes["loss"] = (1,)
    g_small["loss"] = loss
    join_last, settle = ex.ride_for([("join", [A_IN])])
    late, joined = _all_reduce_small(_pack_replicated(g_small, REPLICATED_LATE, REPLICATED_LATE_ROWS), join_last)
    settle(joined)
    grad = _unpack_sharded(ex.done[SMALL_SHARDED].reshape(SHARDED_ROWS, LANES), {k: ws[k].shape for k in SHARDED_SMALL})
    grad.update(_unpack_replicated(ex.done[REPLICATED_UNIT].reshape(REPLICATED_EARLY_ROWS, LANES), shapes, REPLICATED_EARLY))
    grad.update(_unpack_replicated(late, shapes, REPLICATED_LATE))
    loss = grad.pop("loss")[0]

    delta, new_m, new_v = {}, {}, {}
    for k in BIG:
        g_layers = [ex.done[(k, layer)] for layer in range(ws[k].shape[0])]
        outs = _adamw_big("adamw_" + k, ws[k], g_layers, ms[k], vs[k])
        if k in ROWS_MINOR:
            outs = [jnp.swapaxes(o, 1, 2) for o in outs]
        grad[k], delta[k], new_m[k], new_v[k] = outs
    d_s, m_s, v_s = _adamw_small([ws[k] for k in small_names], [grad[k] for k in small_names],
                                 [ms[k] for k in small_names], [vs[k] for k in small_names])
    for j, k in enumerate(small_names):
        delta[k], new_m[k], new_v[k] = d_s[j], m_s[j], v_s[j]

    return (loss, grad_x[None], *[grad[k] for k in WEIGHT_ORDER], *[delta[k] for k in WEIGHT_ORDER],
            *[new_m[k] for k in WEIGHT_ORDER], *[new_v[k] for k in WEIGHT_ORDER])
```

```python
import functools
import math

import jax
import jax.numpy as jnp
from jax import lax
from jax.experimental import pallas as pl
from jax.experimental.pallas import tpu as pltpu

F32 = jnp.float32
BF16 = jnp.bfloat16
SDS = jax.ShapeDtypeStruct

N_CHIPS = 4
EPS = 1e-6
NEG_INF = -1e30
CHUNK = 128
A_GROUPS = 4
A_WIDTH = 512
N_DIL = 3
DIL_RATES = (1, 4, 16)
HEAD_DIM = 64
B_WIDTH = 512
ROPE_DIM = 16
ROPE_THETA = 500000.0
C_WIDTH = 512
C_KERNEL = 31
D_KERNEL = 3
HALO = 32
ATT_BLOCK = 128
LANES = 128

ADAM_LR = 0.001
ADAM_B1 = 0.9
ADAM_B2 = 0.999
ADAM_EPS = 1e-08
ADAM_WD = 0.01
ADAM_STEP = 10

VMEM_LIMIT = 56 * 1024 * 1024

NN = (((1,), (0,)), ((), ()))
NT = (((1,), (1,)), ((), ()))
TN = (((0,), (0,)), ((), ()))

TILES = {"proj_in": 2048, "proj_out": 1024, "ffn_in": 1024, "ffn_out": 1024, "ffn_dact": 512, "dgrad_cols": 512,
         "dgrad_rows": 1024, "wgrad": 4096}


def _params(sem=None, collective_id=None):
    return pltpu.CompilerParams(dimension_semantics=sem, vmem_limit_bytes=VMEM_LIMIT, collective_id=collective_id)


def _bf(v):
    return v if v.dtype == BF16 else v.astype(BF16)


def _dot(a, b, dims):
    return lax.dot_general(_bf(a), _bf(b), dims, preferred_element_type=F32)


def _dot_hi(a, b):
    return jnp.dot(a, b, precision=lax.Precision.HIGHEST, preferred_element_type=F32)


def _sigmoid(v):
    return 0.5 * jnp.tanh(0.5 * v) + 0.5


def _gelu(v):
    return 0.5 * v * (1.0 + lax.erf(v * (1.0 / math.sqrt(2.0))))


def _gelu_grad(v):
    cdf = 0.5 * (1.0 + lax.erf(v * (1.0 / math.sqrt(2.0))))
    return cdf + v * jnp.exp(-0.5 * v * v) * (1.0 / math.sqrt(2.0 * math.pi))


def _segment_mean_matrix(seg, scale=None):
    r = lax.broadcasted_iota(jnp.int32, (LANES, LANES), 0) // seg
    c = lax.broadcasted_iota(jnp.int32, (LANES, LANES), 1) // seg
    return jnp.where(r == c, (1.0 / seg) if scale is None else scale, 0.0).astype(BF16)


def _segment_dot(v, seg):
    hi = v.astype(BF16)
    lo = (v - hi.astype(F32)).astype(BF16)
    return jnp.dot(hi, seg, preferred_element_type=F32) + jnp.dot(lo, seg, preferred_element_type=F32)


MESH = pl.DeviceIdType.MESH
ANY = pl.BlockSpec(memory_space=pl.ANY)


def _position():
    x, y, c = lax.axis_index("x"), lax.axis_index("y"), lax.axis_index("c")
    others = [(1 - x, y), (x, 1 - y), (1 - x, 1 - y)]
    return x, y, c, 2 * x + y, others


class _Ride:
    def __init__(self, ins, bufs, new_outs, sem_shapes, start, finish, reach):
        self.ins, self.bufs, self.new_outs, self.sem_shapes = list(ins), list(bufs), list(new_outs), list(sem_shapes)
        self.start, self.finish = start, finish
        self.reach = frozenset(reach)

    def entry_barrier(self):
        x, y, c, _, others = _position()
        peers = ([(x, y, 1 - c)] if "sibling" in self.reach else []) + ([(qx, qy, c) for qx, qy in others] if "chips" in self.reach else [])
        barrier = pltpu.get_barrier_semaphore()
        for peer in peers:
            pl.semaphore_signal(barrier, inc=1, device_id=peer, device_id_type=MESH)
        pl.semaphore_wait(barrier, len(peers))

    @property
    def collective_id(self):
        return {frozenset(["sibling"]): 0, frozenset(["chips"]): 1, frozenset(["sibling", "chips"]): 2}[self.reach]


def _ride_both(a, b):
    na = (len(a.ins), len(a.bufs), len(a.new_outs), len(a.sem_shapes))

    def split(ins, bufs, new, sems):
        return ((ins[:na[0]], bufs[:na[1]], new[:na[2]], sems[:na[3]]), (ins[na[0]:], bufs[na[1]:], new[na[2]:], sems[na[3]:]))

    def start(*refs):
        ra, rb = split(*refs)
        a.start(*ra)
        b.start(*rb)

    def finish(*refs):
        ra, rb = split(*refs)
        a.finish(*ra)
        b.finish(*rb)

    return _Ride(a.ins + b.ins, a.bufs + b.bufs, a.new_outs + b.new_outs, a.sem_shapes + b.sem_shapes, start, finish,
                 a.reach | b.reach)


def _call(body, *, grid, in_specs, out_specs, out_shape, operands, name, scratch_shapes=(), aliases=None, ride=None,
          prefetch=None):
    off = 0 if prefetch is None else 1
    lead = [] if prefetch is None else [prefetch]

    params = _params(collective_id=None if ride is None else ride.collective_id)

    def launch(kernel_body, in_specs_, out_specs_, out_shape_, scratch_, aliases_, *args):
        if prefetch is None:
            return pl.pallas_call(kernel_body, grid=grid, in_specs=in_specs_, out_specs=out_specs_, out_shape=out_shape_,
                                  scratch_shapes=scratch_, input_output_aliases=aliases_, name=name,
                                  compiler_params=params)(*args)
        spec = pltpu.PrefetchScalarGridSpec(num_scalar_prefetch=1, grid=grid, in_specs=in_specs_, out_specs=out_specs_,
                                            scratch_shapes=scratch_)
        return pl.pallas_call(kernel_body, grid_spec=spec, out_shape=out_shape_, input_output_aliases=aliases_, name=name,
                              compiler_params=params)(*lead, *args)

    if ride is None:
        return launch(body, list(in_specs), out_specs, out_shape, list(scratch_shapes), dict(aliases or {}), *operands)
    multi = isinstance(out_shape, (list, tuple))
    out_shapes = list(out_shape) if multi else [out_shape]
    o_specs = list(out_specs) if multi else [out_specs]
    n_in, n_out, n_scr = off + len(operands), len(out_shapes), len(scratch_shapes)
    n_ri, n_rb, n_rn = len(ride.ins), len(ride.bufs), len(ride.new_outs)

    def carrying(*refs):
        k = n_in
        r_ins = refs[k:k + n_ri]
        k += n_ri + n_rb
        outs = refs[k:k + n_out]
        k += n_out
        r_bufs = refs[k:k + n_rb]
        k += n_rb
        r_new = refs[k:k + n_rn]
        k += n_rn
        scratch = refs[k:k + n_scr]
        sems = refs[k + n_scr:]
        first, last = None, None
        for axis, size in enumerate(grid):
            pid = pl.program_id(axis)
            first = (pid == 0) if first is None else first & (pid == 0)
            last = (pid == size - 1) if last is None else last & (pid == size - 1)

        @pl.when(first)
        def _():
            ride.entry_barrier()
            ride.start(r_ins, r_bufs, r_new, sems)

        body(*refs[:n_in], *outs, *scratch)

        @pl.when(last)
        def _():
            ride.finish(r_ins, r_bufs, r_new, sems)

    all_aliases = dict(aliases or {})
    for j in range(n_rb):
        all_aliases[n_in + n_ri + j] = n_out + j
    res = launch(
        carrying, list(in_specs) + [ANY] * (n_ri + n_rb), o_specs + [ANY] * (n_rb + n_rn),
        out_shapes + [SDS(b.shape, b.dtype) for b in ride.bufs] + ride.new_outs,
        list(scratch_shapes) + [pltpu.SemaphoreType.DMA(s) for s in ride.sem_shapes], all_aliases,
        *operands, *ride.ins, *ride.bufs)
    outs = res[:n_out]
    return (list(outs) if multi else outs[0]), list(res[n_out:])


def _run_ride(name, ride):
    n_ri, n_rb, n_rn = len(ride.ins), len(ride.bufs), len(ride.new_outs)

    def body(*refs):
        r_ins = refs[:n_ri]
        r_bufs = refs[n_ri + n_rb:n_ri + 2 * n_rb]
        r_new = refs[n_ri + 2 * n_rb:n_ri + 2 * n_rb + n_rn]
        sems = refs[n_ri + 2 * n_rb + n_rn:]
        ride.entry_barrier()
        ride.start(r_ins, r_bufs, r_new, sems)
        ride.finish(r_ins, r_bufs, r_new, sems)

    return list(pl.pallas_call(
        body, in_specs=[ANY] * (n_ri + n_rb), out_specs=[ANY] * (n_rb + n_rn),
        out_shape=[SDS(b.shape, b.dtype) for b in ride.bufs] + ride.new_outs,
        scratch_shapes=[pltpu.SemaphoreType.DMA(s) for s in ride.sem_shapes],
        input_output_aliases={n_ri + j: j for j in range(n_rb)}, name=name,
        compiler_params=pltpu.CompilerParams(collective_id=ride.collective_id))(*ride.ins, *ride.bufs))


def _whole(ref, p):
    return ref[...]


def _slab(ref, p):
    return ref[p]


def _matmul(name, grid, pairs, extras, outs, dims, epi, *, slabs=1, n_acc=1, ride=None):
    n_pairs, n_ex, n_out = len(pairs), len(extras), len(outs)

    def body(*refs):
        ab = refs[:2 * n_pairs]
        ex = refs[2 * n_pairs:2 * n_pairs + n_ex]
        out_refs = refs[2 * n_pairs + n_ex:2 * n_pairs + n_ex + n_out]
        pids = tuple(pl.program_id(a) for a in range(len(grid)))
        parts = [None] * n_acc
        for p in range(slabs):
            for j, (_, _, a_pick, _, _, b_pick, acc) in enumerate(pairs):
                d = _dot(a_pick(ab[2 * j], p), b_pick(ab[2 * j + 1], p), dims)
                parts[acc] = d if parts[acc] is None else parts[acc] + d
        epi(parts, ex, out_refs, pids)

    operands, in_specs = [], []
    for a, a_spec, _, b, b_spec, _, _ in pairs:
        operands += [a, b]
        in_specs += [a_spec, b_spec]
    for e, e_spec in extras:
        operands.append(e)
        in_specs.append(e_spec)
    return _call(body, grid=grid, in_specs=in_specs, out_specs=[o[1] for o in outs], out_shape=[o[0] for o in outs],
                 operands=operands, name=name, ride=ride)


def _rms_rows(v, g):
    r = lax.rsqrt(jnp.mean(v * v, axis=-1, keepdims=True) + EPS)
    return v * r * g


def _rms_fwd(name, x, g):
    t, d = x.shape
    tm = 512

    def body(x_ref, g_ref, o_ref):
        o_ref[...] = _rms_rows(x_ref[...], g_ref[...]).astype(BF16)

    return pl.pallas_call(
        body, grid=(t // tm,),
        in_specs=[pl.BlockSpec((tm, d), lambda i: (i, 0)), pl.BlockSpec((1, d), lambda i: (0, 0))],
        out_specs=pl.BlockSpec((tm, d), lambda i: (i, 0)), out_shape=SDS((t, d), BF16), name=name,
        compiler_params=_params())(x, g)


def _epi_residual_norm(accs, ex, outs, pids):
    x_new = accs[0] + ex[0][...]
    outs[0][...] = x_new
    outs[1][...] = _rms_rows(x_new, ex[1][...]).astype(BF16)


def _epi_residual_loss(accs, ex, outs, pids):
    y = accs[0] + ex[0][...]
    err = y - ex[1][...]
    dy = err * (1.0 / err.shape[-1])
    outs[0][...] = dy
    outs[2][...] = dy.astype(BF16)

    @pl.when(pids[0] == 0)
    def _():
        outs[1][...] = jnp.zeros_like(outs[1])

    outs[1][...] += jnp.sum(err * err)


def _epi_rms_bwd(accs, ex, outs, pids):
    dh = accs[0]
    xv, g, res = ex[0][...], ex[1][...], ex[2][...]
    r = lax.rsqrt(jnp.mean(xv * xv, axis=-1, keepdims=True) + EPS)
    xh = xv * r
    dy = dh * g
    dx = res + r * (dy - xh * jnp.mean(dy * xh, axis=-1, keepdims=True))
    outs[0][...] = dx
    if len(outs) > 2:
        outs[2][...] = dx.astype(BF16)

    @pl.when(pids[0] == 0)
    def _():
        outs[1][...] = jnp.zeros_like(outs[1])

    outs[1][...] += jnp.sum(dh * xh, axis=0, keepdims=True)


def _row_spec(tm, d):
    return pl.BlockSpec((tm, d), lambda i, *_: (i, 0))


def _const_spec(shape):
    nd = len(shape)
    return pl.BlockSpec(shape, lambda *_: (0,) * nd)


def _proj_in(name, h, w, layer, ride=None):
    t, d = h.shape
    n4 = w.shape[-1]
    tm = TILES["proj_in"]

    def epi(accs, ex, outs, pids):
        outs[0][...] = accs[0].astype(BF16)

    res = _matmul(
        name, (N_CHIPS, t // tm),
        [(h, pl.BlockSpec((tm, d), lambda p, i: (i, 0)), _whole,
          w, pl.BlockSpec((None, None, d, n4), lambda p, i: (p, layer, 0, 0)), _whole, 0)],
        [], [(SDS((t, N_CHIPS * n4), BF16), pl.BlockSpec((tm, n4), lambda p, i: (i, p)))],
        NN, epi, ride=ride)
    return res[0] if ride is None else (res[0][0], res[1])


def _proj_in_near(name, h, w, ride=None):
    t, d = h.shape
    n4 = w.shape[-1]
    tm = TILES["proj_in"]

    def shard(j, s):
        return j + (j >= N_CHIPS - 1 - s[1]).astype(jnp.int32)

    def body(s_ref, h_ref, w_ref, o_ref):
        o_ref[...] = _dot(h_ref[...], w_ref[...], NN).astype(BF16)

    return _call(
        body, grid=(N_CHIPS - 1, t // tm),
        in_specs=[pl.BlockSpec((tm, d), lambda j, i, s: (i, 0)),
                  pl.BlockSpec((None, None, d, n4), lambda j, i, s: (shard(j, s), 0, 0, 0))],
        out_specs=pl.BlockSpec((tm, n4), lambda j, i, s: (i, shard(j, s))), out_shape=SDS((t, N_CHIPS * n4), BF16),
        operands=[h, w], name=name, ride=ride, prefetch=_mesh_scalars())


def _proj_in_far(name, h, slab, proj, w):
    t, d = h.shape
    n4 = slab.shape[-1]
    tm = TILES["proj_in"]

    def body(s_ref, h_ref, slab_ref, proj_in, w_in, o_ref, w_ref):
        shard = slab_ref[...]
        o_ref[...] = _dot(h_ref[...], shard, NN).astype(BF16)
        w_ref[...] = shard

    proj, w_full = _call(
        body, grid=(t // tm,),
        in_specs=[pl.BlockSpec((tm, d), lambda i, s: (i, 0)), pl.BlockSpec((d, n4), lambda i, s: (0, 0)), ANY, ANY],
        out_specs=[pl.BlockSpec((tm, n4), lambda i, s: (i, N_CHIPS - 1 - s[1])),
                   pl.BlockSpec((None, d, n4), lambda i, s: (N_CHIPS - 1 - s[1], 0, 0))],
        out_shape=[SDS(proj.shape, BF16), SDS((N_CHIPS, d, n4), BF16)],
        operands=[h, slab.reshape(d, n4), proj, w.reshape(N_CHIPS, d, n4)], aliases={3: 0, 4: 1}, name=name,
        prefetch=_mesh_scalars())
    return proj, w_full.reshape(w.shape)


def _proj_out(name, a, w, x, g_next=None, target=None):
    t, k = a.shape
    d = w.shape[-1]
    tm = TILES["proj_out"]
    if target is None:
        extras = [(x, _row_spec(tm, d)), (g_next, _const_spec((1, d)))]
        outs = [(SDS((t, d), F32), _row_spec(tm, d)), (SDS((t, d), BF16), _row_spec(tm, d))]
        epi = _epi_residual_norm
    else:
        extras = [(x, _row_spec(tm, d)), (target, _row_spec(tm, d))]
        outs = [(SDS((t, d), F32), _row_spec(tm, d)), (SDS((8, LANES), F32), _const_spec((8, LANES))),
                (SDS((t, d), BF16), _row_spec(tm, d))]
        epi = _epi_residual_loss
    return _matmul(name, (t // tm,), [(a, _row_spec(tm, k), _whole, w, _const_spec((k, d)), _whole, 0)], extras, outs, NN, epi)


def _ffn_in(name, h, wg, wu, layer, ride=None):
    t, d = h.shape
    n4 = wg.shape[-2]
    tm = TILES["ffn_in"]

    def epi(accs, ex, outs, pids):
        gate, up = accs
        s = _sigmoid(gate)
        silu = gate * s
        outs[0][...] = (up * (s + silu - silu * s)).astype(BF16)
        outs[1][...] = silu.astype(BF16)
        outs[2][...] = (silu * up).astype(BF16)

    w_spec = pl.BlockSpec((None, None, n4, d), lambda p, i: (p, layer, 0, 0))
    h_spec = pl.BlockSpec((tm, d), lambda p, i: (i, 0))
    o = (SDS((N_CHIPS, t, n4), BF16), pl.BlockSpec((None, tm, n4), lambda p, i: (p, i, 0)))
    return _matmul(name, (N_CHIPS, t // tm),
                   [(h, h_spec, _whole, wg, w_spec, _whole, 0), (h, h_spec, _whole, wu, w_spec, _whole, 1)], [],
                   [o, o, o], NT, epi, n_acc=2, ride=ride)


def _ffn_out(name, act, wd, layer, x, g_next=None, target=None, ride=None):
    _, t, n4 = act.shape
    d = wd.shape[-1]
    tm = TILES["ffn_out"]
    xs = _row_spec(tm, d)
    if target is None:
        extras = [(x, xs), (g_next, _const_spec((1, d)))]
        outs = [(SDS((t, d), F32), xs), (SDS((t, d), BF16), xs)]
        epi = _epi_residual_norm
    else:
        extras = [(x, xs), (target, xs)]
        outs = [(SDS((t, d), F32), xs), (SDS((8, LANES), F32), _const_spec((8, LANES))), (SDS((t, d), BF16), xs)]
        epi = _epi_residual_loss
    return _matmul(
        name, (t // tm,),
        [(act, pl.BlockSpec((N_CHIPS, tm, n4), lambda i: (0, i, 0)), _slab,
          wd, pl.BlockSpec((N_CHIPS, None, n4, d), lambda i: (0, layer, 0, 0)), _slab, 0)],
        extras, outs, NN, epi, slabs=N_CHIPS, ride=ride)


def _ffn_dact(name, g, wd, layer, gate, up, ride=None):
    t, d = g.shape
    n4 = wd.shape[-2]
    tm = TILES["ffn_dact"]

    def body(g_ref, w_ref, gate_ref, up_ref, dgate_ref, dup_ref):
        gv = g_ref[...]
        for p in range(N_CHIPS):
            dact = _dot(gv, w_ref[p], NT)
            dgate_ref[p] = (dact * gate_ref[p].astype(F32)).astype(BF16)
            dup_ref[p] = (dact * up_ref[p].astype(F32)).astype(BF16)

    blk = pl.BlockSpec((N_CHIPS, tm, n4), lambda i: (0, i, 0))
    return _call(
        body, grid=(t // tm,),
        in_specs=[_row_spec(tm, d), pl.BlockSpec((N_CHIPS, None, n4, d), lambda i: (0, layer, 0, 0)), blk, blk],
        out_specs=[blk, blk], out_shape=[SDS((N_CHIPS, t, n4), BF16)] * 2, operands=[g, wd, gate, up], name=name, ride=ride)


def _copy_epi(accs, ex, outs, pids):
    for a, o in zip(accs, outs):
        o[...] = a.astype(o.dtype)


def _dgrad_cols(name, dz_list, w_list, layer, three_d, x, g, res, bf16_copy=True, w_rows=False, ride=None):
    t, d = x.shape
    n4 = w_list[0].shape[-2 if w_rows else -1]
    tm = TILES["dgrad_cols"]
    if three_d:
        zs, z_pick = pl.BlockSpec((N_CHIPS, tm, n4), lambda i: (0, i, 0)), _slab
    else:
        zs, z_pick = _row_spec(tm, N_CHIPS * n4), (lambda ref, p: ref[:, p * n4:(p + 1) * n4])
    ws = pl.BlockSpec((N_CHIPS, None) + ((n4, d) if w_rows else (d, n4)), lambda i: (0, layer, 0, 0))
    xs = _row_spec(tm, d)
    return _matmul(
        name, (t // tm,), [(dz, zs, z_pick, w, ws, _slab, 0) for dz, w in zip(dz_list, w_list)],
        [(x, xs), (g, _const_spec((1, d))), (res, xs)],
        [(SDS((t, d), F32), xs), (SDS((1, d), F32), _const_spec((1, d)))] + ([(SDS((t, d), BF16), xs)] if bf16_copy else []),
        NN if w_rows else NT, _epi_rms_bwd, slabs=N_CHIPS, ride=ride)


def _dgrad_rows(name, g, w):
    t, d = g.shape
    k = w.shape[0]
    tm = TILES["dgrad_rows"]
    return _matmul(name, (t // tm,), [(g, _row_spec(tm, d), _whole, w, _const_spec((k, d)), _whole, 0)], [],
                   [(SDS((t, k), F32), _row_spec(tm, k))], NT, _copy_epi)[0]


A_TILE = 256


def _a_common(p_ref, lg_ref, lb_ref):
    pv = p_ref[...].astype(F32)
    a = _gelu(pv)
    u, v = a[:, :A_WIDTH], a[:, A_WIDTH:]
    vc = v - jnp.mean(v, axis=-1, keepdims=True)
    rs = lax.rsqrt(jnp.mean(vc * vc, axis=-1, keepdims=True) + EPS)
    vhat = vc * rs
    vn = vhat * lg_ref[...] + lb_ref[...]
    return pv, u, vhat, rs, vn.astype(BF16)


def _tril_weights(w_ref, g):
    r = lax.broadcasted_iota(jnp.int32, (CHUNK, CHUNK), 0)
    c = lax.broadcasted_iota(jnp.int32, (CHUNK, CHUNK), 1)
    return jnp.where(c <= r, w_ref[g], 0.0).astype(BF16), c <= r


def _mixer_a_fwd(proj, lg, lb, w, bias_t):
    t = proj.shape[0]

    def body(p_ref, lg_ref, lb_ref, w_ref, bt_ref, o_ref):
        _, u, _, _, vnb = _a_common(p_ref, lg_ref, lb_ref)
        for g in range(A_GROUPS):
            wt, _ = _tril_weights(w_ref, g)
            cs = slice(g * CHUNK, (g + 1) * CHUNK)
            for ch in range(A_TILE // CHUNK):
                rs_ = slice(ch * CHUNK, (ch + 1) * CHUNK)
                mixed = _dot(wt, vnb[rs_, cs], NN) + bt_ref[:, g:g + 1]
                o_ref[rs_, cs] = (u[rs_, cs] * mixed).astype(BF16)

    return pl.pallas_call(
        body, grid=(t // A_TILE,),
        in_specs=[pl.BlockSpec((A_TILE, 2 * A_WIDTH), lambda i: (i, 0)), _const_spec((1, A_WIDTH)),
                  _const_spec((1, A_WIDTH)), _const_spec((A_GROUPS, CHUNK, CHUNK)), _const_spec((CHUNK, A_GROUPS))],
        out_specs=pl.BlockSpec((A_TILE, A_WIDTH), lambda i: (i, 0)), out_shape=SDS((t, A_WIDTH), BF16),
        name="mixer_a_fwd", compiler_params=_params())(proj, lg, lb, w, bias_t)


def _mixer_a_bwd(proj, dcat, lg, lb, w, bias_t):
    t = proj.shape[0]

    def body(p_ref, da_ref, lg_ref, lb_ref, w_ref, bt_ref, dp_ref, dw_ref, dbt_ref, dlg_ref, dlb_ref, du_scr, dvn_scr):
        @pl.when(pl.program_id(0) == 0)
        def _():
            dw_ref[...] = jnp.zeros_like(dw_ref)
            dbt_ref[...] = jnp.zeros_like(dbt_ref)
            dlg_ref[...] = jnp.zeros_like(dlg_ref)
            dlb_ref[...] = jnp.zeros_like(dlb_ref)

        pv, u, vhat, rs, vnb = _a_common(p_ref, lg_ref, lb_ref)
        da = da_ref[...]
        for g in range(A_GROUPS):
            wt, keep = _tril_weights(w_ref, g)
            cs = slice(g * CHUNK, (g + 1) * CHUNK)
            for ch in range(A_TILE // CHUNK):
                rs_ = slice(ch * CHUNK, (ch + 1) * CHUNK)
                vg = vnb[rs_, cs]
                mixed = _dot(wt, vg, NN) + bt_ref[:, g:g + 1]
                du_scr[rs_, cs] = da[rs_, cs] * mixed
                dmx = da[rs_, cs] * u[rs_, cs]
                dw_ref[g] += jnp.where(keep, _dot(dmx, vg, NT), 0.0)
                dvn_scr[rs_, cs] = _dot(wt, dmx, TN)
                dbt_ref[:, g:g + 1] += jnp.sum(dmx, axis=1, keepdims=True)
        dvn = dvn_scr[...]
        dlg_ref[...] += jnp.sum(dvn * vhat, axis=0, keepdims=True)
        dlb_ref[...] += jnp.sum(dvn, axis=0, keepdims=True)
        dvh = dvn * lg_ref[...]
        dv = rs * (dvh - jnp.mean(dvh, axis=-1, keepdims=True) - vhat * jnp.mean(dvh * vhat, axis=-1, keepdims=True))
        gp = _gelu_grad(pv)
        dp_ref[:, :A_WIDTH] = (du_scr[...] * gp[:, :A_WIDTH]).astype(BF16)
        dp_ref[:, A_WIDTH:] = (dv * gp[:, A_WIDTH:]).astype(BF16)

    return pl.pallas_call(
        body, grid=(t // A_TILE,),
        in_specs=[pl.BlockSpec((A_TILE, 2 * A_WIDTH), lambda i: (i, 0)), pl.BlockSpec((A_TILE, A_WIDTH), lambda i: (i, 0)),
                  _const_spec((1, A_WIDTH)), _const_spec((1, A_WIDTH)), _const_spec((A_GROUPS, CHUNK, CHUNK)),
                  _const_spec((CHUNK, A_GROUPS))],
        out_specs=[pl.BlockSpec((A_TILE, 2 * A_WIDTH), lambda i: (i, 0)), _const_spec((A_GROUPS, CHUNK, CHUNK)),
                   _const_spec((CHUNK, A_GROUPS)), _const_spec((1, A_WIDTH)), _const_spec((1, A_WIDTH))],
        out_shape=[SDS((t, 2 * A_WIDTH), BF16), SDS((A_GROUPS, CHUNK, CHUNK), F32), SDS((CHUNK, A_GROUPS), F32),
                   SDS((1, A_WIDTH), F32), SDS((1, A_WIDTH), F32)],
        scratch_shapes=[pltpu.VMEM((A_TILE, A_WIDTH), F32), pltpu.VMEM((A_TILE, A_WIDTH), F32)],
        name="mixer_a_bwd", compiler_params=_params())(proj, dcat, lg, lb, w, bias_t)


def _rope_tables(t):
    half = ROPE_DIM // 2
    inv_freq = ROPE_THETA ** (-jnp.arange(half, dtype=F32) * 2.0 / ROPE_DIM)
    ang = jnp.arange(t, dtype=F32)[:, None] * inv_freq[None, :]
    cos, sin = jnp.cos(ang), jnp.sin(ang)
    one = jnp.ones((t, HEAD_DIM - ROPE_DIM), F32)
    zero = jnp.zeros((t, HEAD_DIM - ROPE_DIM), F32)
    zh = jnp.zeros((t, half), F32)
    c = jnp.concatenate([cos, cos, one], axis=1)
    s1 = jnp.concatenate([-sin, zh, zero], axis=1)
    s2 = jnp.concatenate([zh, sin, zero], axis=1)
    return tuple(jnp.tile(a, (1, LANES // HEAD_DIM)) for a in (c, s1, s2))


QK_TILE = 512
QK_ROWS = 64
QK_COLS = 2 * N_DIL * B_WIDTH


CHUNKS = B_WIDTH // LANES


def _regroup_out(scr, first, out_ref, rate, tile):
    rows = tile // rate
    for rho in range(rate):
        for c in range(CHUNKS):
            out_ref[rho, :, c * LANES:(c + 1) * LANES] = scr[first + c, pl.ds(rho, rows, stride=rate), :].astype(out_ref.dtype)


def _regroup_in(x_ref, scr, rate, tile):
    rows = tile // rate
    for rho in range(rate):
        for c in range(CHUNKS):
            scr[c, pl.ds(rho, rows, stride=rate), :] = x_ref[rho, :, c * LANES:(c + 1) * LANES].astype(F32)


def _regrouped_spec(rate, tile):
    return pl.BlockSpec((rate, tile // rate, B_WIDTH), lambda i, *_: (0, i, 0))


def _qk_fwd(proj, gains, tabs, ride=None):
    t = proj.shape[0]
    col0 = 2 * A_WIDTH // 1024
    r1, r2 = DIL_RATES[1], DIL_RATES[2]

    def body(p_ref, g_ref, c_ref, s1_ref, s2_ref, o_ref, q1_ref, q2_ref, k1_ref, k2_ref, scr):
        seg = _segment_mean_matrix(HEAD_DIM)
        for r0 in range(0, QK_TILE, QK_ROWS):
            rows = slice(r0, r0 + QK_ROWS)
            c, s1, s2 = c_ref[rows, :], s1_ref[rows, :], s2_ref[rows, :]
            for ci in range(1024 // LANES):
                ls = slice(ci * LANES, (ci + 1) * LANES)
                xv = p_ref[rows, ls].astype(F32)
                r = lax.rsqrt(_segment_dot(xv * xv, seg) + EPS)
                y = xv * r * g_ref[:, ls]
                val = y * c + pltpu.roll(y, LANES - 8, axis=1) * s1 + pltpu.roll(y, 8, axis=1) * s2
                o_ref[rows, ls] = val.astype(BF16)
                scr[ci, rows, :] = val

        j = pl.program_id(1)

        @pl.when(j == 0)
        def _():
            _regroup_out(scr, CHUNKS, q1_ref, r1, QK_TILE)

        @pl.when(j == 1)
        def _():
            _regroup_out(scr, 0, q2_ref, r2, QK_TILE)

        @pl.when(j == 2)
        def _():
            _regroup_out(scr, 0, k1_ref, r1, QK_TILE)
            _regroup_out(scr, CHUNKS, k2_ref, r2, QK_TILE)

    tab = pl.BlockSpec((QK_TILE, LANES), lambda i, j: (i, 0))
    g1, g2 = SDS((r1, t // r1, B_WIDTH), BF16), SDS((r2, t // r2, B_WIDTH), BF16)
    s1_, s2_ = _regrouped_spec(r1, QK_TILE), _regrouped_spec(r2, QK_TILE)
    return _call(
        body, grid=(t // QK_TILE, QK_COLS // 1024),
        in_specs=[pl.BlockSpec((QK_TILE, 1024), lambda i, j: (i, col0 + j)), pl.BlockSpec((1, 1024), lambda i, j: (0, j)),
                  tab, tab, tab],
        out_specs=[pl.BlockSpec((QK_TILE, 1024), lambda i, j: (i, j)), s1_, s2_, s1_, s2_],
        out_shape=[SDS((t, QK_COLS), BF16), g1, g2, g1, g2],
        scratch_shapes=[pltpu.VMEM((2 * CHUNKS, QK_TILE, LANES), F32)],
        operands=[proj, gains, *tabs], name="qk_norm_rope_fwd", ride=ride)


PERM_TILE = 512


def _permute(name, items, rate):
    t = items[0][0].shape[0]
    n = len(items)

    def body(*refs):
        scr = refs[-1]
        for x_ref, o_ref in zip(refs[:n], refs[n:2 * n]):
            for ci in range(CHUNKS):
                scr[ci] = x_ref[:, ci * LANES:(ci + 1) * LANES].astype(F32)
            _regroup_out(scr, 0, o_ref, rate, PERM_TILE)

    return pl.pallas_call(
        body, grid=(t // PERM_TILE,),
        in_specs=[pl.BlockSpec((PERM_TILE, B_WIDTH), functools.partial(lambda cb, i: (i, cb), cb)) for _, cb in items],
        out_specs=[_regrouped_spec(rate, PERM_TILE) for _ in items],
        out_shape=[SDS((rate, t // rate, B_WIDTH), a.dtype) for a, _ in items],
        scratch_shapes=[pltpu.VMEM((CHUNKS, PERM_TILE, LANES), F32)],
        name=name, compiler_params=_params())(*[a for a, _ in items])


def _head_lane_mask(h):
    lane = lax.broadcasted_iota(jnp.int32, (1, LANES), 1)
    return (lane < HEAD_DIM) if h == 0 else (lane >= HEAD_DIM)


def _attn_fwd(name, q, k, v, ride=None):
    rate, length = q[0].shape[0], q[0].shape[1]
    nb = length // ATT_BLOCK
    scale = HEAD_DIM ** -0.5

    def body(q_ref, kc_ref, kp_ref, vc_ref, vp_ref, o_ref, l_ref):
        n = pl.program_id(1)
        qi = lax.broadcasted_iota(jnp.int32, (ATT_BLOCK, 2 * ATT_BLOCK), 0)
        cj = lax.broadcasted_iota(jnp.int32, (ATT_BLOCK, 2 * ATT_BLOCK), 1)
        has_prev = jnp.where(n > 0, 0, 2 * ATT_BLOCK)
        mask = ((cj < ATT_BLOCK) & (cj >= qi + has_prev)) | ((cj >= ATT_BLOCK) & (cj - ATT_BLOCK <= qi))
        heads = [(hp, h) for hp in range(CHUNKS) for h in range(2)]
        q2, k2, v2 = {}, {}, {}
        for hp in range(CHUNKS):
            ls = slice(hp * LANES, (hp + 1) * LANES)
            q2[hp] = q_ref[:, ls]
            k2[hp] = jnp.concatenate([kp_ref[:, ls], kc_ref[:, ls]], axis=0)
            v2[hp] = jnp.concatenate([vp_ref[:, ls], vc_ref[:, ls]], axis=0)
        scores = {}
        for hp, h in heads:
            scores[hp, h] = _dot(jnp.where(_head_lane_mask(h), q2[hp], jnp.zeros_like(q2[hp])), k2[hp], NT) * scale
        probs, lses = {}, {}
        for hp, h in heads:
            s = jnp.where(mask, scores[hp, h], NEG_INF)
            m = jnp.max(s, axis=1, keepdims=True)
            p = jnp.exp(s - m)
            den = jnp.sum(p, axis=1, keepdims=True)
            lses[hp, h] = m + jnp.log(den)
            probs[hp, h] = (p / den).astype(BF16)
        for hp in range(CHUNKS):
            ls = slice(hp * LANES, (hp + 1) * LANES)
            o_acc = None
            for h in range(2):
                o = _dot(probs[hp, h], jnp.where(_head_lane_mask(h), v2[hp], jnp.zeros_like(v2[hp])), NN)
                o_acc = o if o_acc is None else o_acc + o
            o_ref[:, ls] = o_acc
            zeros = jnp.zeros((ATT_BLOCK, LANES), F32)
            l_ref[:, ls] = jnp.where(_head_lane_mask(1), lses[hp, 1] + zeros, lses[hp, 0] + zeros)

    def cur(cb):
        return pl.BlockSpec((None, ATT_BLOCK, B_WIDTH), lambda r, n: (r, n, cb))

    def prev(cb):
        return pl.BlockSpec((None, ATT_BLOCK, B_WIDTH), lambda r, n: (r, jnp.maximum(n - 1, 0), cb))

    out = pl.BlockSpec((None, ATT_BLOCK, B_WIDTH), lambda r, n: (r, n, 0))
    return _call(
        body, grid=(rate, nb),
        in_specs=[cur(q[1]), cur(k[1]), prev(k[1]), cur(v[1]), prev(v[1])],
        out_specs=[out, out], out_shape=[SDS((rate, length, B_WIDTH), F32)] * 2,
        operands=[q[0], k[0], k[0], v[0], v[0]], name=name, ride=ride)


def _attn_merge(a_out, o_list, l_list):
    t = a_out.shape[0]
    tm = PERM_TILE
    r1, r2 = DIL_RATES[1], DIL_RATES[2]

    def body(a_ref, o0, o1, o2, l0, l1, l2, cat_ref, lt_ref, lt1_ref, lt2_ref, so1, so2, sl1, sl2, slt):
        _regroup_in(o1, so1, r1, tm)
        _regroup_in(l1, sl1, r1, tm)
        _regroup_in(o2, so2, r2, tm)
        _regroup_in(l2, sl2, r2, tm)
        cat_ref[:, :A_WIDTH] = a_ref[...]
        for c in range(CHUNKS):
            ls = slice(c * LANES, (c + 1) * LANES)
            lg = [l0[:, ls], sl1[c], sl2[c]]
            m = jnp.maximum(jnp.maximum(lg[0], lg[1]), lg[2])
            es = [jnp.exp(l - m) for l in lg]
            den = es[0] + es[1] + es[2]
            b = (es[0] * o0[:, ls] + es[1] * so1[c] + es[2] * so2[c]) / den
            cat_ref[:, A_WIDTH + c * LANES:A_WIDTH + (c + 1) * LANES] = b.astype(BF16)
            lt = m + jnp.log(den)
            lt_ref[:, ls] = lt
            slt[c] = lt
        _regroup_out(slt, 0, lt1_ref, r1, tm)
        _regroup_out(slt, 0, lt2_ref, r2, tm)

    blk = _row_spec(tm, B_WIDTH)
    g1, g2 = _regrouped_spec(r1, tm), _regrouped_spec(r2, tm)
    return pl.pallas_call(
        body, grid=(t // tm,), in_specs=[blk, blk, g1, g2, blk, g1, g2],
        out_specs=[_row_spec(tm, A_WIDTH + B_WIDTH), blk, g1, g2],
        out_shape=[SDS((t, A_WIDTH + B_WIDTH), BF16), SDS((t, B_WIDTH), F32), SDS((r1, t // r1, B_WIDTH), F32),
                   SDS((r2, t // r2, B_WIDTH), F32)],
        scratch_shapes=[pltpu.VMEM((CHUNKS, tm, LANES), F32)] * 5,
        name="attn_merge", compiler_params=_params())(a_out, *o_list, *l_list)


def _attn_bwd_prep(dcat, cat):
    t = dcat.shape[0]
    tm = PERM_TILE
    r1, r2 = DIL_RATES[1], DIL_RATES[2]

    def body(d_ref, b_ref, db_ref, dd_ref, db1_ref, dd1_ref, db2_ref, dd2_ref, sdb, sdd):
        seg = _segment_mean_matrix(HEAD_DIM, scale=1.0)
        for c in range(CHUNKS):
            ls = slice(c * LANES, (c + 1) * LANES)
            d = d_ref[:, ls]
            dsum = _segment_dot(d * b_ref[:, ls].astype(F32), seg)
            db_ref[:, ls] = d.astype(BF16)
            dd_ref[:, ls] = dsum
            sdb[c] = d
            sdd[c] = dsum
        _regroup_out(sdb, 0, db1_ref, r1, tm)
        _regroup_out(sdd, 0, dd1_ref, r1, tm)
        _regroup_out(sdb, 0, db2_ref, r2, tm)
        _regroup_out(sdd, 0, dd2_ref, r2, tm)

    right = pl.BlockSpec((tm, B_WIDTH), lambda i: (i, 1))
    blk = _row_spec(tm, B_WIDTH)
    g1, g2 = _regrouped_spec(r1, tm), _regrouped_spec(r2, tm)
    return pl.pallas_call(
        body, grid=(t // tm,), in_specs=[right, right], out_specs=[blk, blk, g1, g1, g2, g2],
        out_shape=[SDS((t, B_WIDTH), BF16), SDS((t, B_WIDTH), F32), SDS((r1, t // r1, B_WIDTH), BF16),
                   SDS((r1, t // r1, B_WIDTH), F32), SDS((r2, t // r2, B_WIDTH), BF16), SDS((r2, t // r2, B_WIDTH), F32)],
        scratch_shapes=[pltpu.VMEM((CHUNKS, tm, LANES), F32)] * 2,
        name="attn_bwd_prep", compiler_params=_params())(dcat, cat)


def _attn_bwd(name, q, k, v, db, lse, dd, ride=None):
    rate, length = db.shape[0], db.shape[1]
    nb = length // ATT_BLOCK
    scale = HEAD_DIM ** -0.5

    def body(qa_ref, qb_ref, k_ref, v_ref, dba_ref, dbb_ref, la_ref, lb_ref, da_ref, dbd_ref, dq_ref, dk_ref, dv_ref, carry):
        m = pl.program_id(1)

        @pl.when(m == 0)
        def _():
            carry[...] = jnp.zeros_like(carry)

        row = lax.broadcasted_iota(jnp.int32, (2 * ATT_BLOCK, ATT_BLOCK), 0)
        kj = lax.broadcasted_iota(jnp.int32, (2 * ATT_BLOCK, ATT_BLOCK), 1)
        no_next = jnp.where(m + 1 < nb, 0, 2 * ATT_BLOCK)
        mask = ((row < ATT_BLOCK) & (kj <= row)) | ((row >= ATT_BLOCK) & (kj >= row - ATT_BLOCK + no_next))
        heads = [(hp, h) for hp in range(CHUNKS) for h in range(2)]
        q2, db2, lse2, dd2, k2, v2 = {}, {}, {}, {}, {}, {}
        for hp in range(CHUNKS):
            ls = slice(hp * LANES, (hp + 1) * LANES)
            k2[hp], v2[hp] = k_ref[:, ls], v_ref[:, ls]
            q2[hp] = jnp.concatenate([qa_ref[:, ls], qb_ref[:, ls]], axis=0)
            db2[hp] = jnp.concatenate([dba_ref[:, ls], dbb_ref[:, ls]], axis=0)
            lse2[hp] = jnp.concatenate([la_ref[:, ls], lb_ref[:, ls]], axis=0)
            dd2[hp] = jnp.concatenate([da_ref[:, ls], dbd_ref[:, ls]], axis=0)
        km, scores, dps = {}, {}, {}
        for hp, h in heads:
            hm = _head_lane_mask(h)
            km[hp, h] = jnp.where(hm, k2[hp], jnp.zeros_like(k2[hp]))
            scores[hp, h] = _dot(q2[hp], km[hp, h], NT) * scale
            dps[hp, h] = _dot(db2[hp], jnp.where(hm, v2[hp], jnp.zeros_like(v2[hp])), NT)
        probs, dss = {}, {}
        for hp, h in heads:
            hm = _head_lane_mask(h)
            lse_col = jnp.max(jnp.where(hm, lse2[hp], NEG_INF), axis=1, keepdims=True)
            dd_col = jnp.max(jnp.where(hm, dd2[hp], NEG_INF), axis=1, keepdims=True)
            p = jnp.where(mask, jnp.exp(scores[hp, h] - lse_col), 0.0)
            probs[hp, h] = p.astype(BF16)
            dss[hp, h] = (p * (dps[hp, h] - dd_col) * scale).astype(BF16)
        for hp in range(CHUNKS):
            ls = slice(hp * LANES, (hp + 1) * LANES)
            dq_acc, dk_acc, dv_acc = None, None, None
            for h in range(2):
                hm = _head_lane_mask(h)
                dvc = _dot(probs[hp, h], jnp.where(hm, db2[hp], jnp.zeros_like(db2[hp])), TN)
                dqc = _dot(dss[hp, h], km[hp, h], NN)
                dkc = _dot(dss[hp, h], jnp.where(hm, q2[hp], jnp.zeros_like(q2[hp])), TN)
                dq_acc = dqc if dq_acc is None else dq_acc + dqc
                dk_acc = dkc if dk_acc is None else dk_acc + dkc
                dv_acc = dvc if dv_acc is None else dv_acc + dvc
            dq_ref[:, ls] = (dq_acc[:ATT_BLOCK] + carry[:, ls]).astype(BF16)
            carry[:, ls] = dq_acc[ATT_BLOCK:]
            dk_ref[:, ls] = dk_acc.astype(BF16)
            dv_ref[:, ls] = dv_acc.astype(BF16)

    def cur(cb):
        return pl.BlockSpec((None, ATT_BLOCK, B_WIDTH), lambda r, n: (r, n, cb))

    def nxt(cb):
        return pl.BlockSpec((None, ATT_BLOCK, B_WIDTH), lambda r, n: (r, jnp.minimum(n + 1, nb - 1), cb))

    out = cur(0)
    return _call(
        body, grid=(rate, nb),
        in_specs=[cur(q[1]), nxt(q[1]), cur(k[1]), cur(v[1]), cur(0), nxt(0), cur(0), nxt(0), cur(0), nxt(0)],
        out_specs=[out, out, out], out_shape=[SDS((rate, length, B_WIDTH), BF16)] * 3,
        scratch_shapes=[pltpu.VMEM((ATT_BLOCK, B_WIDTH), F32)],
        operands=[q[0], q[0], k[0], v[0], db, db, lse, lse, dd, dd], name=name, ride=ride)


AB_IN = 2 * A_WIDTH + 3 * N_DIL * B_WIDTH
ASM_TILE = 256


def _dproj_assemble(proj, d_a, dq, dk, dv, gains, tabs, ride=None):
    t = proj.shape[0]
    n_in = 3 * N_DIL

    def body(p_ref, da_ref, *rest):
        grads = rest[:n_in]
        g_ref, c_ref, s1_ref, s2_ref, o_ref, dg_ref = rest[n_in:n_in + 6]
        scratch = rest[n_in + 6:]

        @pl.when(pl.program_id(0) == 0)
        def _():
            dg_ref[...] = jnp.zeros_like(dg_ref)

        chunk = {}
        k_scr = 0
        for j in range(n_in):
            g = j % N_DIL
            if DIL_RATES[g] == 1:
                for ci in range(CHUNKS):
                    chunk[j, ci] = functools.partial(lambda r, ci: r[:, ci * LANES:(ci + 1) * LANES].astype(F32), grads[j], ci)
            else:
                scr = scratch[k_scr]
                k_scr += 1
                _regroup_in(grads[j], scr, DIL_RATES[g], ASM_TILE)
                for ci in range(CHUNKS):
                    chunk[j, ci] = functools.partial(lambda s, ci: s[ci], scr, ci)

        seg = _segment_mean_matrix(HEAD_DIM)
        c, s1, s2 = c_ref[...], s1_ref[...], s2_ref[...]
        o_ref[:, :2 * A_WIDTH] = da_ref[...]
        for jg in range(2 * N_DIL):
            for ci in range(CHUNKS):
                col = jg * B_WIDTH + ci * LANES
                src = slice(2 * A_WIDTH + col, 2 * A_WIDTH + col + LANES)
                xv = p_ref[:, src].astype(F32)
                r = lax.rsqrt(_segment_dot(xv * xv, seg) + EPS)
                xh = xv * r
                gain = g_ref[:, col:col + LANES]
                do = chunk[jg, ci]()
                dy = do * c + pltpu.roll(do * s1, 8, axis=1) + pltpu.roll(do * s2, LANES - 8, axis=1)
                dg_ref[:, col:col + LANES] += jnp.sum(dy * xh, axis=0, keepdims=True)
                dxh = dy * gain
                o_ref[:, src] = (r * (dxh - xh * _segment_dot(dxh * xh, seg))).astype(BF16)
        v0 = 2 * A_WIDTH + QK_COLS
        for g in range(N_DIL):
            for ci in range(CHUNKS):
                col = v0 + g * B_WIDTH + ci * LANES
                o_ref[:, col:col + LANES] = chunk[2 * N_DIL + g, ci]().astype(BF16)

    specs = [_row_spec(ASM_TILE, B_WIDTH) if r == 1 else _regrouped_spec(r, ASM_TILE) for r in DIL_RATES] * 3
    n_scr = 3 * sum(1 for r in DIL_RATES if r > 1)
    tab = _row_spec(ASM_TILE, LANES)
    return _call(
        body, grid=(t // ASM_TILE,),
        in_specs=[_row_spec(ASM_TILE, AB_IN), _row_spec(ASM_TILE, 2 * A_WIDTH)] + specs
        + [_const_spec((1, QK_COLS)), tab, tab, tab],
        out_specs=[_row_spec(ASM_TILE, AB_IN), _const_spec((1, QK_COLS))],
        out_shape=[SDS((t, AB_IN), BF16), SDS((1, QK_COLS), F32)],
        scratch_shapes=[pltpu.VMEM((CHUNKS, ASM_TILE, LANES), F32)] * n_scr,
        operands=[proj, d_a, *dq, *dk, *dv, gains, *tabs], name="dproj_assemble", ride=ride)


def _fold_heads(dg_lane):
    n = dg_lane.shape[1]

    def body(x_ref, o_ref):
        r = lax.broadcasted_iota(jnp.int32, (B_WIDTH, B_WIDTH), 0) % HEAD_DIM
        c = lax.broadcasted_iota(jnp.int32, (B_WIDTH, B_WIDTH), 1) % HEAD_DIM
        fold = jnp.where(r == c, 1.0, 0.0).astype(F32)
        for jg in range(n // B_WIDTH):
            ls = slice(jg * B_WIDTH, (jg + 1) * B_WIDTH)
            o_ref[:, ls] = _dot_hi(jnp.broadcast_to(x_ref[:, ls], (8, B_WIDTH)), fold)

    return pl.pallas_call(body, out_shape=SDS((8, n), F32), name="fold_heads", compiler_params=_params())(dg_lane)


CD_TILE = 256
TAP_ROWS = 64
CD_IN = 2 * C_WIDTH + 3 * 512


def _shifted_copies(src, dst, rows):
    dst[0, :rows] = src[...]
    for b in range(1, 8):
        dst[b, :rows - 8] = src[pl.ds(b, rows - 8), :]


def _rows_from(shifted, start, n, lanes=slice(None)):
    b = start % 8
    return shifted[b, pl.ds(start - b, n), lanes]


def _mixer_cd_fwd(proj, cw, cb, lg, lb, dw):
    t = proj.shape[0]
    per = CD_TILE // HALO

    def body(h_ref, m_ref, cw_ref, cb_ref, lg_ref, lb_ref, dw_ref, o_ref, c1_ref, c_scr, e_scr, c_sh):
        not_first = (pl.program_id(0) > 0).astype(F32)
        lanes = [slice(c * LANES, (c + 1) * LANES) for c in range(C_WIDTH // LANES)]

        def col(ref, part, ls):
            return ref[:, part * C_WIDTH + ls.start:part * C_WIDTH + ls.stop].astype(F32)

        for ls in lanes:
            c_scr[:HALO, ls] = col(h_ref, 0, ls) * _sigmoid(col(h_ref, 1, ls)) * not_first
            c_scr[HALO:, ls] = col(m_ref, 0, ls) * _sigmoid(col(m_ref, 1, ls))
            e_scr[:HALO, ls] = col(h_ref, 3, ls) * col(h_ref, 4, ls) * not_first
            e_scr[HALO:, ls] = col(m_ref, 3, ls) * col(m_ref, 4, ls)
        _shifted_copies(c_scr, c_sh, HALO + CD_TILE)
        for ls in lanes:
            for r0 in range(0, CD_TILE, TAP_ROWS):
                acc = jnp.zeros((TAP_ROWS, LANES), F32)
                for k in range(C_KERNEL):
                    acc = acc + cw_ref[k:k + 1, ls] * _rows_from(c_sh, r0 + HALO - (C_KERNEL - 1) + k, TAP_ROWS, ls)
                c1_ref[r0:r0 + TAP_ROWS, ls] = acc + cb_ref[:, ls]
        mean = sum(jnp.sum(c1_ref[:, ls], axis=-1, keepdims=True) for ls in lanes) * (1.0 / C_WIDTH)
        var = sum(jnp.sum((c1_ref[:, ls] - mean) ** 2, axis=-1, keepdims=True) for ls in lanes) * (1.0 / C_WIDTH)
        rs = lax.rsqrt(var + EPS)
        for ls in lanes:
            c2 = (c1_ref[:, ls] - mean) * rs * lg_ref[:, ls] + lb_ref[:, ls]
            o_ref[:, ls] = (c2 * _sigmoid(c2)).astype(BF16)
            d1 = jnp.zeros((CD_TILE, LANES), F32)
            for k in range(D_KERNEL):
                d1 = d1 + dw_ref[k:k + 1, ls] * e_scr[pl.ds(HALO - (D_KERNEL - 1) + k, CD_TILE), ls]
            o_ref[:, C_WIDTH + ls.start:C_WIDTH + ls.stop] = (col(m_ref, 2, ls) * d1).astype(BF16)

    return pl.pallas_call(
        body, grid=(t // CD_TILE,),
        in_specs=[pl.BlockSpec((HALO, CD_IN), lambda i: (jnp.maximum(i * per - 1, 0), 0)), _row_spec(CD_TILE, CD_IN),
                  _const_spec((32, C_WIDTH)), _const_spec((1, C_WIDTH)), _const_spec((1, C_WIDTH)), _const_spec((1, C_WIDTH)),
                  _const_spec((8, C_WIDTH))],
        out_specs=[_row_spec(CD_TILE, 2 * C_WIDTH), _row_spec(CD_TILE, C_WIDTH)],
        out_shape=[SDS((t, 2 * C_WIDTH), BF16), SDS((t, C_WIDTH), F32)],
        scratch_shapes=[pltpu.VMEM((HALO + CD_TILE, C_WIDTH), F32)] * 2 + [pltpu.VMEM((8, HALO + CD_TILE, C_WIDTH), F32)],
        name="mixer_cd_fwd", compiler_params=_params())(proj, proj, cw, cb, lg, lb, dw)


def _mixer_cd_bwd(proj, dcat, c1, cw, lg, lb, dw, ride=None):
    t = proj.shape[0]
    per = CD_TILE // HALO
    nt = t // CD_TILE
    ext = CD_TILE + HALO

    def body(hp_ref, m_ref, hn_ref, dm_ref, dn_ref, c1m_ref, c1n_ref, cw_ref, lg_ref, lb_ref, dw_ref,
             dp_ref, dcw_ref, dcb_ref, dlg_ref, dlb_ref, ddw_ref, c_scr, e_scr, dc1_scr, dd1_scr, c_sh, dc1_sh, dcw_acc,
             dvh_scr, vhat_scr):
        i = pl.program_id(0)

        @pl.when(i == 0)
        def _():
            for r in (dcw_acc, dcb_ref, dlg_ref, dlb_ref, ddw_ref):
                r[...] = jnp.zeros_like(r)

        not_first = (i > 0).astype(F32)
        not_last = (i < nt - 1).astype(F32)
        main = slice(HALO, HALO + CD_TILE)
        lanes = [slice(c * LANES, (c + 1) * LANES) for c in range(C_WIDTH // LANES)]

        def col(ref, part, ls):
            return ref[:, part * C_WIDTH + ls.start:part * C_WIDTH + ls.stop].astype(F32)

        for ls in lanes:
            c_scr[:HALO, ls] = col(hp_ref, 0, ls) * _sigmoid(col(hp_ref, 1, ls)) * not_first
            c_scr[main, ls] = col(m_ref, 0, ls) * _sigmoid(col(m_ref, 1, ls))
            c_scr[HALO + CD_TILE:, ls] = col(hn_ref, 0, ls) * _sigmoid(col(hn_ref, 1, ls)) * not_last
            e_scr[:HALO, ls] = col(hp_ref, 3, ls) * col(hp_ref, 4, ls) * not_first
            e_scr[main, ls] = col(m_ref, 3, ls) * col(m_ref, 4, ls)
            e_scr[HALO + CD_TILE:, ls] = col(hn_ref, 3, ls) * col(hn_ref, 4, ls) * not_last
        _shifted_copies(c_scr, c_sh, 2 * HALO + CD_TILE)

        def c1_of(ls):
            return jnp.concatenate([c1m_ref[:, ls], c1n_ref[:, ls]], axis=0)

        mean = sum(jnp.sum(c1_of(ls), axis=-1, keepdims=True) for ls in lanes) * (1.0 / C_WIDTH)
        var = sum(jnp.sum((c1_of(ls) - mean) ** 2, axis=-1, keepdims=True) for ls in lanes) * (1.0 / C_WIDTH)
        rs = lax.rsqrt(var + EPS)
        sum_dvh, sum_dvh_vhat = 0.0, 0.0
        for ls in lanes:
            vhat = (c1_of(ls) - mean) * rs
            c2 = vhat * lg_ref[:, ls] + lb_ref[:, ls]
            sig = _sigmoid(c2)
            dc = jnp.concatenate([dm_ref[:, ls], dn_ref[:, ls] * not_last], axis=0)
            dc2 = dc * (sig * (1.0 + c2 * (1.0 - sig)))
            dvh = dc2 * lg_ref[:, ls]
            sum_dvh = sum_dvh + jnp.sum(dvh, axis=-1, keepdims=True)
            sum_dvh_vhat = sum_dvh_vhat + jnp.sum(dvh * vhat, axis=-1, keepdims=True)
            dvh_scr[:, ls] = dvh
            vhat_scr[:, ls] = vhat
            dlg_ref[:, ls] += jnp.sum((dc2 * vhat)[:CD_TILE], axis=0, keepdims=True)
            dlb_ref[:, ls] += jnp.sum(dc2[:CD_TILE], axis=0, keepdims=True)
        for ls in lanes:
            dc1 = rs * (dvh_scr[:, ls] - sum_dvh * (1.0 / C_WIDTH) - vhat_scr[:, ls] * (sum_dvh_vhat * (1.0 / C_WIDTH)))
            dc1_scr[:, ls] = dc1
            dcb_ref[:, ls] += jnp.sum(dc1[:CD_TILE], axis=0, keepdims=True)
        _shifted_copies(dc1_scr, dc1_sh, ext)
        for ls in lanes:
            for r0 in range(0, CD_TILE, TAP_ROWS):
                rows = slice(r0, r0 + TAP_ROWS)
                dc1_m = dc1_scr[rows, ls]
                dc0 = jnp.zeros((TAP_ROWS, LANES), F32)
                for k in range(C_KERNEL):
                    dc0 = dc0 + cw_ref[k:k + 1, ls] * _rows_from(dc1_sh, r0 + C_KERNEL - 1 - k, TAP_ROWS, ls)
                    prod = dc1_m * _rows_from(c_sh, r0 + HALO - (C_KERNEL - 1) + k, TAP_ROWS, ls)
                    dcw_acc[k, :, ls] += prod.reshape(TAP_ROWS // 8, 8, LANES).sum(axis=0)
                g_m = m_ref[rows, C_WIDTH + ls.start:C_WIDTH + ls.stop].astype(F32)
                a_m = m_ref[rows, ls].astype(F32)
                sig_m = _sigmoid(g_m)
                dp_ref[rows, ls] = (dc0 * sig_m).astype(BF16)
                dp_ref[rows, C_WIDTH + ls.start:C_WIDTH + ls.stop] = (dc0 * a_m * sig_m * (1.0 - sig_m)).astype(BF16)

        @pl.when(i == nt - 1)
        def _():
            dcw_ref[...] = jnp.sum(dcw_acc[...], axis=1)

        for ls in lanes:
            wide = slice(C_WIDTH + ls.start, C_WIDTH + ls.stop)
            d1 = jnp.zeros((CD_TILE, LANES), F32)
            for k in range(D_KERNEL):
                d1 = d1 + dw_ref[k:k + 1, ls] * e_scr[pl.ds(HALO - (D_KERNEL - 1) + k, CD_TILE), ls]
            dd_m = dm_ref[:, wide]
            dd1 = jnp.concatenate([dd_m * col(m_ref, 2, ls), dn_ref[:, wide] * col(hn_ref, 2, ls) * not_last], axis=0)
            dd1_scr[:, ls] = dd1
            dp_ref[:, 2 * C_WIDTH + ls.start:2 * C_WIDTH + ls.stop] = (dd_m * d1).astype(BF16)
            de = jnp.zeros((CD_TILE, LANES), F32)
            for k in range(D_KERNEL):
                de = de + dw_ref[k:k + 1, ls] * dd1_scr[pl.ds(D_KERNEL - 1 - k, CD_TILE), ls]
                ddw_ref[k:k + 1, ls] += jnp.sum(dd1[:CD_TILE] * e_scr[pl.ds(HALO - (D_KERNEL - 1) + k, CD_TILE), ls], axis=0, keepdims=True)
            dp_ref[:, 3 * C_WIDTH + ls.start:3 * C_WIDTH + ls.stop] = (de * col(m_ref, 4, ls)).astype(BF16)
            dp_ref[:, 4 * C_WIDTH + ls.start:4 * C_WIDTH + ls.stop] = (de * col(m_ref, 3, ls)).astype(BF16)

    halo_prev = lambda i: (jnp.maximum(i * per - 1, 0), 0)
    halo_next = lambda i: (jnp.minimum((i + 1) * per, t // HALO - 1), 0)
    vec = _const_spec((1, C_WIDTH))
    return _call(
        body, grid=(nt,),
        in_specs=[pl.BlockSpec((HALO, CD_IN), halo_prev), _row_spec(CD_TILE, CD_IN), pl.BlockSpec((HALO, CD_IN), halo_next),
                  _row_spec(CD_TILE, 2 * C_WIDTH), pl.BlockSpec((HALO, 2 * C_WIDTH), halo_next),
                  _row_spec(CD_TILE, C_WIDTH), pl.BlockSpec((HALO, C_WIDTH), halo_next),
                  _const_spec((32, C_WIDTH)), vec, vec, _const_spec((8, C_WIDTH))],
        out_specs=[_row_spec(CD_TILE, CD_IN), _const_spec((32, C_WIDTH)), vec, vec, vec, _const_spec((8, C_WIDTH))],
        out_shape=[SDS((t, CD_IN), BF16), SDS((32, C_WIDTH), F32), SDS((1, C_WIDTH), F32), SDS((1, C_WIDTH), F32),
                   SDS((1, C_WIDTH), F32), SDS((8, C_WIDTH), F32)],
        scratch_shapes=[pltpu.VMEM((2 * HALO + CD_TILE, C_WIDTH), F32)] * 2 + [pltpu.VMEM((ext, C_WIDTH), F32)] * 2
        + [pltpu.VMEM((8, 2 * HALO + CD_TILE, C_WIDTH), F32), pltpu.VMEM((8, ext, C_WIDTH), F32),
           pltpu.VMEM((32, 8, C_WIDTH), F32)] + [pltpu.VMEM((ext, C_WIDTH), F32)] * 2,
        operands=[proj, proj, proj, dcat, dcat, c1, c1, cw, lg, lb, dw], name="mixer_cd_bwd", ride=ride)


def _wgrad(name, pairs, out_rc, t, ride):
    tk = TILES["wgrad"]
    assert tk == t, "the whole contraction has to fit one grid step"
    r, c = out_rc
    n = len(pairs)

    def body(*refs):
        ab, out_refs = refs[:2 * n], refs[2 * n:]
        for j in range(n):
            out_refs[j][...] = _dot(ab[2 * j][...], ab[2 * j + 1][...], TN).astype(BF16)

    operands, in_specs = [], []
    for lhs, lhs_spec, rhs, rhs_spec in pairs:
        operands += [lhs, rhs]
        in_specs += [lhs_spec, rhs_spec]
    res = _call(body, grid=(N_CHIPS, t // tk), in_specs=in_specs,
                out_specs=[pl.BlockSpec((None, r, c), lambda p, k: (p, 0, 0))] * n,
                out_shape=[SDS((N_CHIPS, r, c), BF16)] * n, operands=operands, name=name, ride=ride)
    outs, ride_res = (res, None) if ride is None else res
    outs = [o.reshape(N_CHIPS, 2, r // 2, c) for o in outs]
    return outs if ride is None else (outs, ride_res)


def _wgrad_col_sharded(name, h, dz_list, three_d, ride=None):
    t, d = h.shape
    tk = TILES["wgrad"]
    n4 = dz_list[0].shape[-1] if three_d else dz_list[0].shape[-1] // N_CHIPS
    hs = pl.BlockSpec((tk, d), lambda p, k: (k, 0))
    zs = pl.BlockSpec((None, tk, n4), lambda p, k: (p, k, 0)) if three_d else pl.BlockSpec((tk, n4), lambda p, k: (k, p))
    return _wgrad(name, [(h, hs, dz, zs) for dz in dz_list], (d, n4), t, ride)


def _wgrad_row_sharded(name, a, g, three_d, ride=None):
    many = isinstance(a, (list, tuple))
    a_list = list(a) if many else [a]
    t, d = g.shape
    tk = TILES["wgrad"]
    k4 = a_list[0].shape[-1] if three_d else a_list[0].shape[-1] // N_CHIPS
    a_spec = pl.BlockSpec((None, tk, k4), lambda p, k: (p, k, 0)) if three_d else pl.BlockSpec((tk, k4), lambda p, k: (k, p))
    gs = pl.BlockSpec((tk, d), lambda p, k: (k, 0))
    res = _wgrad(name, [(a_j, a_spec, g, gs) for a_j in a_list], (k4, d), t, ride)
    if many:
        return res
    return res[0] if ride is None else (res[0][0], res[1])


def _mesh_scalars():
    return jnp.stack([lax.axis_index("c"), 2 * lax.axis_index("x") + lax.axis_index("y")]).astype(jnp.int32)


def _stage_own(name, w, layer, dtype, far_slab=False):
    layers, r, cols = w.shape
    h = r // 2

    def body(s_ref, x_ref, o_ref, *unwritten):
        o_ref[...] = x_ref[...].astype(dtype)

    out_specs = [pl.BlockSpec((None, None, h, cols), lambda i, s: (s[1], i, 0, 0))] + [ANY] * far_slab
    out_shape = [SDS((N_CHIPS, 2, h, cols), dtype)] + [SDS((2, h, cols), dtype)] * far_slab
    res = pl.pallas_call(
        body,
        grid_spec=pltpu.PrefetchScalarGridSpec(
            num_scalar_prefetch=1, grid=(2,),
            in_specs=[pl.BlockSpec((None, h, cols), lambda i, s: (2 * layer + i, 0, 0))], out_specs=out_specs),
        out_shape=out_shape, name=name,
        compiler_params=_params())(_mesh_scalars(), w.reshape(2 * layers, h, cols))
    return tuple(res) if far_slab else res[0]


STAGE_STEPS = 4


def _stage_rest_and_norm(x, g, weights, ride=None):
    t, d = x.shape
    n = len(weights)
    views, in_specs, out_specs, out_shapes = [], [], [], []
    for w, layer in weights:
        layers, r, cols = w.shape
        sub = r // STAGE_STEPS
        views.append(w.reshape(layers * STAGE_STEPS, sub, cols))
        in_specs.append(pl.BlockSpec((None, sub, cols), functools.partial(lambda l, i, s: (STAGE_STEPS * l + i, 0, 0), layer)))
        out_specs.append(pl.BlockSpec((None, None, sub, cols), lambda i, s: (s[1], i // 2, i % 2, 0)))
        out_shapes.append(SDS((N_CHIPS, 2, r // 2, cols), BF16))

    def body(s_ref, x_ref, g_ref, *rest):
        w_refs, h_ref, o_refs = rest[:n], rest[n], rest[n + 1:]
        h_ref[...] = _rms_rows(x_ref[...], g_ref[...]).astype(BF16)
        for w_ref, o_ref in zip(w_refs, o_refs):
            o_ref[...] = w_ref[...].astype(BF16)

    tm = t // STAGE_STEPS
    res = _call(
        body, grid=(STAGE_STEPS,), in_specs=[pl.BlockSpec((tm, d), lambda i, s: (i, 0)), pl.BlockSpec((1, d), lambda i, s: (0, 0))] + in_specs,
        out_specs=[pl.BlockSpec((tm, d), lambda i, s: (i, 0))] + out_specs, out_shape=[SDS((t, d), BF16)] + out_shapes,
        operands=[x, g] + views, name="stage_and_norm", ride=ride, prefetch=_mesh_scalars())
    outs, ride_res = (res, None) if ride is None else res
    result = (outs[0], list(outs[1:]))
    return result if ride is None else (result, ride_res)


def _remote(src, dst, send_sem, recv_sem, device):
    return pltpu.make_async_remote_copy(src, dst, send_sem, recv_sem, device_id=device, device_id_type=MESH)


ALL_PEERS = (0, 1, 2)
NEIGHBOURS = (0, 1)


def _ride_gather_send(bufs, peers=ALL_PEERS):
    n = len(bufs)

    def each(b, sems, act):
        send, recv = sems
        x, y, c, p, others = _position()
        for t in range(n):
            for j in peers:
                qx, qy = others[j]
                act(b[t].at[p, c], b[t].at[2 * qx + qy, c], send.at[t, j], recv.at[t, j], (qx, qy, c))

    def start(ins, b, new, sems):
        each(b, sems, lambda mine, landed, s, r, dev: _remote(mine, mine, s, r, dev).start())

    def finish(ins, b, new, sems):
        def act(mine, landed, s, r, dev):
            _remote(mine, mine, s, r, dev).wait_send()
            _remote(landed, landed, s, r, dev).wait_recv()
        each(b, sems, act)

    return _Ride([], bufs, [], [(n, 3), (n, 3)], start, finish, ["chips"])


def _ride_gather_pass(bufs, peers=ALL_PEERS):
    n = len(bufs)

    def each(b, sems, act):
        send, recv = sems
        x, y, c, p, others = _position()
        for t in range(n):
            for j in peers:
                qx, qy = others[j]
                act(b[t].at[2 * qx + qy, c], b[t].at[2 * qx + qy, 1 - c], send.at[t, j], recv.at[t, j], (x, y, 1 - c))

    def start(ins, b, new, sems):
        each(b, sems, lambda landed, passed, s, r, dev: _remote(landed, landed, s, r, dev).start())

    def finish(ins, b, new, sems):
        def act(landed, passed, s, r, dev):
            _remote(landed, landed, s, r, dev).wait_send()
            _remote(passed, passed, s, r, dev).wait_recv()
        each(b, sems, act)

    return _Ride([], bufs, [], [(n, 3), (n, 3)], start, finish, ["sibling"])


def _ride_gather(bufs, peers=ALL_PEERS):
    send, onward = _ride_gather_send(bufs, peers), _ride_gather_pass(bufs, peers)
    n_send = len(send.sem_shapes)

    def start(ins, b, new, sems):
        send.start(ins, b, new, sems[:n_send])

    def finish(ins, b, new, sems):
        send.finish(ins, b, new, sems[:n_send])
        onward.start(ins, b, new, sems[n_send:])
        onward.finish(ins, b, new, sems[n_send:])

    return _Ride([], bufs, [], send.sem_shapes + onward.sem_shapes, start, finish, ["sibling", "chips"])


def _ride_gather_far(sources, slabs):
    n = len(slabs)

    def hops(ins, b, sems):
        x, y, c, p, others = _position()
        qx, qy = others[2]
        for t in range(n):
            far = (ins[t].at[p, c], b[t].at[c], sems[0].at[t], sems[1].at[t], (qx, qy, c))
            onward = (b[t].at[c], b[t].at[1 - c], sems[2].at[t], sems[3].at[t], (x, y, 1 - c))
            yield far, onward

    def wait(mine, landed, s, r, dev):
        _remote(mine, mine, s, r, dev).wait_send()
        _remote(landed, landed, s, r, dev).wait_recv()

    def start(ins, b, new, sems):
        for (mine, landed, s, r, dev), _ in hops(ins, b, sems):
            _remote(mine, landed, s, r, dev).start()

    def finish(ins, b, new, sems):
        for far, _ in hops(ins, b, sems):
            wait(*far)
        for _, (landed, passed, s, r, dev) in hops(ins, b, sems):
            _remote(landed, landed, s, r, dev).start()
        for _, onward in hops(ins, b, sems):
            wait(*onward)

    return _Ride(sources, slabs, [], [(n,)] * 4, start, finish, ["sibling", "chips"])


def _ride_swap(tensors):
    n = len(tensors)

    def each(ins, new, sems, act):
        send, recv = sems
        x, y, c, _, _ = _position()
        for t in range(n):
            act(_remote(ins[t].at[:, 1 - c], new[t], send.at[t], recv.at[t], (x, y, 1 - c)))

    def start(ins, b, new, sems):
        each(ins, new, sems, lambda cp: cp.start())

    def finish(ins, b, new, sems):
        each(ins, new, sems, lambda cp: cp.wait())

    return _Ride(tensors, [], [SDS((s.shape[0],) + s.shape[2:], s.dtype) for s in tensors], [(n,), (n,)], start, finish,
                 ["sibling"])


def _ride_scatter(tensors, landing):
    n = len(tensors)

    def each(ins, b, sems, act):
        send, recv = sems
        x, y, c, p, others = _position()
        for t in range(n):
            for j, (qx, qy) in enumerate(others):
                q = 2 * qx + qy
                act(ins[t].at[q], b[t].at[p], b[t].at[q], send.at[t, j], recv.at[t, j], (qx, qy, c))

    def start(ins, b, new, sems):
        each(ins, b, sems, lambda src, dst, landed, s, r, dev: _remote(src, dst, s, r, dev).start())

    def finish(ins, b, new, sems):
        def act(src, dst, landed, s, r, dev):
            _remote(src, dst, s, r, dev).wait_send()
            _remote(landed, landed, s, r, dev).wait_recv()
        each(ins, b, sems, act)

    return _Ride(tensors, landing, [], [(n, 3), (n, 3)], start, finish, ["chips"])


def _ride_join(bufs):
    n = len(bufs)

    def each(b, sems, act):
        send, recv = sems
        x, y, c, _, _ = _position()
        for t in range(n):
            act(b[t].at[c], b[t].at[1 - c], send.at[t], recv.at[t], (x, y, 1 - c))

    def start(ins, b, new, sems):
        each(b, sems, lambda mine, theirs, s, r, dev: _remote(mine, mine, s, r, dev).start())

    def finish(ins, b, new, sems):
        def act(mine, theirs, s, r, dev):
            _remote(mine, mine, s, r, dev).wait_send()
            _remote(theirs, theirs, s, r, dev).wait_recv()
        each(b, sems, act)

    return _Ride([], bufs, [], [(n,), (n,)], start, finish, ["sibling"])


def _all_reduce_small(pack, ride=None):
    rows = pack.shape[0]
    n_dev = 2 * N_CHIPS
    n_rb = 0 if ride is None else len(ride.bufs)

    def body(x_ref, *rest):
        o_ref = rest[n_rb]
        r_bufs = rest[n_rb + 1:2 * n_rb + 1]
        land, send, recv = rest[2 * n_rb + 1:2 * n_rb + 4]
        r_sems = rest[2 * n_rb + 4:]
        if ride is not None:
            ride.start([], r_bufs, [], r_sems)
        x, y, c, p, _ = _position()
        me = 2 * p + c
        land[me] = x_ref[...]
        peers = [(dx, dy, dc) for dx in range(2) for dy in range(2) for dc in range(2) if (dx, dy, dc) != (0, 0, 0)]
        for j, (dx, dy, dc) in enumerate(peers):
            _remote(land.at[me], land.at[me], send.at[j], recv.at[j], (x ^ dx, y ^ dy, c ^ dc)).start()
        for j, (dx, dy, dc) in enumerate(peers):
            src = 4 * (x ^ dx) + 2 * (y ^ dy) + (c ^ dc)
            _remote(land.at[me], land.at[me], send.at[j], recv.at[j], (x ^ dx, y ^ dy, c ^ dc)).wait_send()
            _remote(land.at[src], land.at[src], send.at[j], recv.at[j], (x ^ dx, y ^ dy, c ^ dc)).wait_recv()
        acc = land[0]
        for dev in range(1, n_dev):
            acc = acc + land[dev]
        o_ref[...] = acc
        if ride is not None:
            ride.finish([], r_bufs, [], r_sems)

    bufs = [] if ride is None else ride.bufs
    sems = [] if ride is None else [pltpu.SemaphoreType.DMA(s) for s in ride.sem_shapes]
    res = pl.pallas_call(
        body, in_specs=[pl.BlockSpec(memory_space=pltpu.VMEM)] + [ANY] * n_rb,
        out_specs=[pl.BlockSpec(memory_space=pltpu.VMEM)] + [ANY] * n_rb,
        out_shape=[SDS((rows, LANES), F32)] + [SDS(b.shape, b.dtype) for b in bufs],
        scratch_shapes=[pltpu.VMEM((n_dev, rows, LANES), F32), pltpu.SemaphoreType.DMA((n_dev - 1,)),
                        pltpu.SemaphoreType.DMA((n_dev - 1,))] + sems,
        input_output_aliases={1 + j: 1 + j for j in range(n_rb)},
        name="all_reduce_small", compiler_params=_params())(pack, *bufs)
    return res[0], list(res[1:])


def _add_own_half(name, fulls, recvs, out_dtypes):
    n = len(fulls)
    in_specs, out_specs, out_shapes = [], [], []
    for full, dtype in zip(fulls, out_dtypes):
        n4, _, h, cols = full.shape
        in_specs += [pl.BlockSpec((None, None, h, cols), lambda q, s: (q, s[0], 0, 0)),
                     pl.BlockSpec((None, h, cols), lambda q, s: (q, 0, 0))]
        out_specs += [pl.BlockSpec((None, h, cols), lambda q, s: (q, 0, 0)),
                      pl.BlockSpec((None, h, cols), lambda q, s: (s[1], 0, 0))]
        out_shapes += [SDS((n4, h, cols), dtype)] * 2

    def body(s_ref, *refs):
        ins, outs = refs[:2 * n], refs[2 * n:]
        for j in range(n):
            o_ref, own_ref = outs[2 * j], outs[2 * j + 1]
            v = (ins[2 * j][...].astype(F32) + ins[2 * j + 1][...].astype(F32)).astype(o_ref.dtype)
            o_ref[...] = v

            @pl.when(pl.program_id(0) == s_ref[1])
            def _():
                own_ref[...] = v

    res = pl.pallas_call(
        body,
        grid_spec=pltpu.PrefetchScalarGridSpec(num_scalar_prefetch=1, grid=(N_CHIPS,), in_specs=in_specs, out_specs=out_specs),
        out_shape=out_shapes, name=name, compiler_params=_params())(
            _mesh_scalars(), *[a for pair in zip(fulls, recvs) for a in pair])
    return [(res[2 * j], res[2 * j + 1]) for j in range(n)]


def _sum_chips(name, parts_list):
    steps = 4 if all(parts.shape[1] % 64 == 0 for parts in parts_list) else 1
    in_specs, out_specs, out_shapes = [], [], []
    for parts in parts_list:
        n4, h, cols = parts.shape
        in_specs.append(pl.BlockSpec((n4, h // steps, cols), lambda i, s: (0, i, 0)))
        out_specs.append(pl.BlockSpec((None, h // steps, cols), lambda i, s: (s[0], i, 0)))
        out_shapes.append(SDS((2, h, cols), F32))
    n = len(parts_list)

    def body(s_ref, *refs):
        for a_ref, o_ref in zip(refs[:n], refs[n:]):
            acc = a_ref[0].astype(F32)
            for q in range(1, N_CHIPS):
                acc = acc + a_ref[q].astype(F32)
            o_ref[...] = acc

    return pl.pallas_call(
        body,
        grid_spec=pltpu.PrefetchScalarGridSpec(num_scalar_prefetch=1, grid=(steps,), in_specs=in_specs, out_specs=out_specs),
        out_shape=out_shapes, name=name, compiler_params=_params())(_mesh_scalars(), *parts_list)


def _adamw_math(w, g, m, v):
    m2 = ADAM_B1 * m + (1.0 - ADAM_B1) * g
    v2 = ADAM_B2 * v + (1.0 - ADAM_B2) * (g * g)
    m_hat = m2 / (1.0 - ADAM_B1 ** ADAM_STEP)
    v_hat = v2 / (1.0 - ADAM_B2 ** ADAM_STEP)
    delta = -ADAM_LR * (m_hat / (jnp.sqrt(v_hat) + ADAM_EPS) + ADAM_WD * w)
    return delta, m2, v2


def _row_tile(rows, cols):
    cap = max(8, (1 << 18) // cols)
    best = 8
    for cand in range(8, min(rows, cap) + 1, 8):
        if rows % cand == 0:
            best = cand
    return best


ADAMW_STEPS = 8


def _adamw_big(name, ws, g_layers_list, ms, vs):
    layers = ws[0].shape[0]
    n, per_in = len(ws), 3 + layers
    in_specs, out_specs, out_shapes, operands = [], [], [], []
    for w, g_layers, m, v in zip(ws, g_layers_list, ms, vs):
        assert w.shape[0] == layers and w.shape[1] % (8 * ADAMW_STEPS) == 0
        _, rows, cols = w.shape
        tr = rows // ADAMW_STEPS
        blk = pl.BlockSpec((None, tr, cols), lambda l, i: (l, i, 0))
        in_specs += [blk] * 3 + [pl.BlockSpec((tr, cols), lambda l, i: (i, 0))] * layers
        out_specs += [blk] * 4
        out_shapes += [SDS((layers, rows, cols), F32)] * 4
        operands += [w, m, v] + [g.reshape(rows, cols) for g in g_layers]

    def body(*refs):
        ins, outs = refs[:n * per_in], refs[n * per_in:]
        for j in range(n):
            w_ref, m_ref, v_ref = ins[j * per_in:j * per_in + 3]
            g_refs = ins[j * per_in + 3:(j + 1) * per_in]
            g_o, d_o, m_o, v_o = outs[4 * j:4 * j + 4]
            gv = g_refs[0][...]
            for layer in range(1, layers):
                gv = jnp.where(pl.program_id(0) == layer, g_refs[layer][...], gv)
            d, mm, vv = _adamw_math(w_ref[...], gv, m_ref[...], v_ref[...])
            g_o[...] = gv
            d_o[...] = d
            m_o[...] = mm
            v_o[...] = vv

    res = pl.pallas_call(
        body, grid=(layers, ADAMW_STEPS), in_specs=in_specs, out_specs=out_specs, out_shape=out_shapes, name=name,
        compiler_params=_params())(*operands)
    return [tuple(res[4 * j:4 * j + 4]) for j in range(n)]


def _adamw_small(ws, gs, ms, vs):
    n = len(ws)
    flat = []
    for group in (ws, gs, ms, vs):
        flat += [a.reshape(-1, a.shape[-1]) for a in group]

    def body(*refs):
        w_r, g_r, m_r, v_r = refs[:n], refs[n:2 * n], refs[2 * n:3 * n], refs[3 * n:4 * n]
        d_o, m_o, v_o = refs[4 * n:5 * n], refs[5 * n:6 * n], refs[6 * n:7 * n]
        for j in range(n):
            d, mm, vv = _adamw_math(w_r[j][...], g_r[j][...], m_r[j][...], v_r[j][...])
            d_o[j][...] = d
            m_o[j][...] = mm
            v_o[j][...] = vv

    shapes = [SDS(a.shape, F32) for a in flat[:n]]
    outs = pl.pallas_call(body, out_shape=shapes * 3, name="adamw_small", compiler_params=_params())(*flat)
    res = []
    for k in range(3):
        res.append([outs[k * n + j].reshape(ws[j].shape) for j in range(n)])
    return res


BIG = ("ab_w_in", "ab_w_out", "cd_w_in", "cd_w_out", "ffn_w_gate", "ffn_w_up", "ffn_w_down")
BIG_BY_LAYERS = (BIG[:4], BIG[4:])
V_BLOCK = (2 * A_WIDTH + QK_COLS) // B_WIDTH


def _pad_rows(a, rows):
    return jnp.pad(a, ((0, rows - a.shape[0]), (0, 0)))


A_IN, A_OUT, C_IN, C_OUT = ("ab_w_in", 0), ("ab_w_out", 0), ("cd_w_in", 0), ("cd_w_out", 0)
G0, U0, D0 = ("ffn_w_gate", 0), ("ffn_w_up", 0), ("ffn_w_down", 0)
G1, U1, D1 = ("ffn_w_gate", 1), ("ffn_w_up", 1), ("ffn_w_down", 1)
UNITS = (A_IN, A_OUT, G0, U0, D0, C_IN, C_OUT, G1, U1, D1)
ROWS_MINOR = ("ffn_w_gate", "ffn_w_up")
SMALL_SHARDED = ("small", 0)
REPLICATED_UNIT = ("replicated", 0)


class _Exchange:
    def __init__(self, enabled):
        self.enabled = enabled
        self.w, self.grad, self.recv, self.half, self.land, self.done = {}, {}, {}, {}, {}, {}
        self.far = {}

    def full(self, unit):
        b = self.w[unit]
        return b.reshape(N_CHIPS, 1, 2 * b.shape[2], b.shape[3])

    def ride_for(self, phases):
        rides, sinks = [], []
        for kind, units in phases:
            if kind == "send":
                rides.append(_ride_gather_send([self.w[u] for u in units]))
                sinks.append(self.w)
            elif kind == "pass":
                rides.append(_ride_gather_pass([self.w[u] for u in units]))
                sinks.append(self.w)
            elif kind == "gather":
                rides.append(_ride_gather([self.w[u] for u in units]))
                sinks.append(self.w)
            elif kind == "gather_near":
                rides.append(_ride_gather([self.w[u] for u in units], NEIGHBOURS))
                sinks.append(self.w)
            elif kind == "gather_far":
                rides.append(_ride_gather_far([self.w[u] for u in units], [self.far[u] for u in units]))
                sinks.append(self.far)
            elif kind == "swap":
                rides.append(_ride_swap([self.grad[u] for u in units]))
                sinks.append(self.recv)
            elif kind == "scatter":
                rides.append(_ride_scatter([self.half[u] for u in units], [self.land[u] for u in units]))
                sinks.append(self.land)
            else:
                rides.append(_ride_join([self.done[u] for u in units]))
                sinks.append(self.done)
        ride = functools.reduce(_ride_both, rides)

        def settle(res):
            n_bufs = sum(len(r.bufs) for r in rides)
            bufs, new = list(res[:n_bufs]), list(res[n_bufs:])
            for r, sink, (_, units) in zip(rides, sinks, phases):
                vals = [bufs.pop(0) for _ in r.bufs] + [new.pop(0) for _ in r.new_outs]
                for u, v in zip(units, vals):
                    sink[u] = v

        return ride, settle

    def run(self, fn, *args, phases=(), **kw):
        if not self.enabled or not phases:
            return fn(*args, **kw)
        ride, settle = self.ride_for(phases)
        out, res = fn(*args, ride=ride, **kw)
        settle(res)
        return out

    def alone(self, name, phases):
        if self.enabled:
            ride, settle = self.ride_for(phases)
            settle(_run_ride(name, ride))

    def pair_sum(self, units):
        if self.enabled:
            dtypes = [F32 if u in (SMALL_SHARDED, REPLICATED_UNIT) else BF16 for u in units]
            res = _add_own_half(f"pair_sum_{units[0][0]}_{units[0][1]}", [self.grad[u] for u in units],
                                [self.recv[u] for u in units], dtypes)
            for u, (half, land) in zip(units, res):
                self.half[u], self.land[u] = half, land

    def chip_sum(self, units):
        if self.enabled:
            res = _sum_chips(f"chip_sum_{units[0][0]}_{units[0][1]}", [self.land[u] for u in units])
            self.done.update(zip(units, res))


def _local_step(x, target, ex, sp, h0=None):
    t, d = x.shape
    tabs = _rope_tables(t)
    gains = jnp.concatenate([jnp.tile(sp["q_norm_g"][g], HEAD_DIM // 8) for g in range(N_DIL)]
                            + [jnp.tile(sp["k_norm_g"][g], HEAD_DIM // 8) for g in range(N_DIL)]).reshape(1, QK_COLS)
    bias_t = sp["sgu_bias"].T
    cw = _pad_rows(sp["conv_c_w"], 32)
    dw = _pad_rows(sp["conv_d_w"], 8)
    cb, clg, clb = (sp[k].reshape(1, C_WIDTH) for k in ("conv_c_b", "c_ln_g", "c_ln_b"))
    slg, slb = sp["sgu_norm_g"].reshape(1, A_WIDTH), sp["sgu_norm_b"].reshape(1, A_WIDTH)
    g_ab, g_cd = sp["ab_norm_g"].reshape(1, d), sp["cd_norm_g"].reshape(1, d)
    g_f0, g_f1 = sp["ffn_norm_g"][0:1], sp["ffn_norm_g"][1:2]
    run = ex.run

    def w2d(unit):
        return ex.full(unit).reshape(-1, d)

    if h0 is None:
        h0 = _rms_fwd("rms_ab", x, g_ab)
    if ex.enabled:
        proj = run(_proj_in_near, "proj_ab_near", h0, ex.full(A_IN), phases=[("gather_far", [A_IN]), ("send", [A_OUT])])
        proj, ex.w[A_IN] = _proj_in_far("proj_ab_far", h0, ex.far[A_IN], proj, ex.w[A_IN])
    else:
        proj = _proj_in("proj_ab", h0, ex.full(A_IN), 0)
    a_out = _mixer_a_fwd(proj, slg, slb, sp["sgu_w"], bias_t)
    qk, q1, q2, k1, k2 = run(_qk_fwd, proj, gains, tabs, phases=[("pass", [A_OUT]), ("send", [G0, C_OUT])])
    regrouped_qk = {1: (q1, k1), 2: (q2, k2)}
    fwd_phases = ([("pass", [G0, C_OUT]), ("send", [U0])], [("pass", [U0]), ("send", [D0])],
                  [("pass", [D0]), ("send", [C_IN])])
    qkv, o_list, l_list = [], [], []
    for g, rate in enumerate(DIL_RATES):
        if rate == 1:
            qk3, proj3 = qk.reshape(1, t, QK_COLS), proj.reshape(1, t, AB_IN)
            q, k, v = (qk3, g), (qk3, N_DIL + g), (proj3, V_BLOCK + g)
        else:
            vp, = _permute(f"regroup_v_{g}", [(proj, V_BLOCK + g)], rate)
            q, k, v = (regrouped_qk[g][0], 0), (regrouped_qk[g][1], 0), (vp, 0)
        qkv.append((q, k, v))
        o, l = run(_attn_fwd, f"attn_fwd_{g}", q, k, v, phases=fwd_phases[g])
        if rate == 1:
            o, l = o.reshape(t, B_WIDTH), l.reshape(t, B_WIDTH)
        o_list.append(o)
        l_list.append(l)
    cat, lse_tot, lse_1, lse_2 = _attn_merge(a_out, o_list, l_list)
    x1, hf0 = _proj_out("out_ab", cat, w2d(A_OUT), x, g_next=g_f0)
    fgate0, fup0, act0 = run(_ffn_in, "ffn_in_0", hf0, ex.full(G0), ex.full(U0), 0,
                           phases=[("pass", [C_IN]), ("send", [D1, G1])])
    x2, h1 = run(_ffn_out, "ffn_out_0", act0, ex.full(D0), 0, x1, g_next=g_cd, phases=[("pass", [D1, G1]), ("send", [U1])])
    projcd = run(_proj_in, "proj_cd", h1, ex.full(C_IN), 0, phases=[("pass", [U1])])
    cat2, c1 = _mixer_cd_fwd(projcd, cw, cb, clg, clb, dw)
    x3, hf1 = _proj_out("out_cd", cat2, w2d(C_OUT), x2, g_next=g_f1)
    fgate1, fup1, act1 = _ffn_in("ffn_in_1", hf1, ex.full(G1), ex.full(U1), 0)
    dy, loss_acc, dy_b = _ffn_out("ffn_out_1", act1, ex.full(D1), 0, x3, target=target)
    loss = 0.5 * loss_acc[0, 0] / d

    late = [D1, G1, U1]
    dgate, dup = _ffn_dact("ffn_dact_1", dy_b, ex.full(D1), 0, fgate1, fup1)
    ex.grad[D1] = _wgrad_row_sharded("wgrad_down_1", act1, dy_b, True)
    ex.grad[G1] = _wgrad_row_sharded("wgrad_gate_1", dgate, hf1, True)
    ex.grad[U1] = _wgrad_row_sharded("wgrad_up_1", dup, hf1, True)
    g3, d_f1, g3_b = run(_dgrad_cols, "dgrad_ffn_1", [dgate, dup], [ex.full(G1), ex.full(U1)], 0, True, x3, g_f1, dy,
                         w_rows=True, phases=[("swap", late)])
    ex.pair_sum(late)

    dcat2 = _dgrad_rows("dgrad_out_cd", g3_b, w2d(C_OUT))
    ex.grad[C_OUT] = _wgrad_row_sharded("wgrad_out_cd", cat2, g3_b, False)
    dprojcd, d_cw, d_cb, d_clg, d_clb, d_dw = run(_mixer_cd_bwd, projcd, dcat2, c1, cw, clg, clb, dw, phases=[("scatter", late)])
    ex.chip_sum(late)
    ex.grad[C_IN] = _wgrad_col_sharded("wgrad_in_cd", h1, [dprojcd], False)[0]
    g2, d_cdn, g2_b = run(_dgrad_cols, "dgrad_in_cd", [dprojcd], [ex.full(C_IN)], 0, False, x2, g_cd, g3,
                          phases=[("join", late), ("swap", [C_OUT, C_IN])])
    ex.pair_sum([C_OUT, C_IN])

    dgate, dup = run(_ffn_dact, "ffn_dact_0", g2_b, ex.full(D0), 0, fgate0, fup0, phases=[("scatter", [C_OUT, C_IN])])
    ex.chip_sum([C_OUT, C_IN])
    ex.grad[D0] = _wgrad_row_sharded("wgrad_down_0", act0, g2_b, True)
    ex.grad[G0] = _wgrad_row_sharded("wgrad_gate_0", dgate, hf0, True)
    ex.grad[U0] = _wgrad_row_sharded("wgrad_up_0", dup, hf0, True)
    small = {"cd_norm_g": d_cdn, "conv_c_w": d_cw[:C_KERNEL], "conv_c_b": d_cb, "c_ln_g": d_clg, "c_ln_b": d_clb,
             "conv_d_w": d_dw[:D_KERNEL]}
    ex.grad[SMALL_SHARDED] = _split_full_small(small).reshape(N_CHIPS, 2, SHARDED_ROWS // 2, LANES)
    mid = [D0, G0, U0, SMALL_SHARDED]
    g1, d_f0, g1_b = run(_dgrad_cols, "dgrad_ffn_0", [dgate, dup], [ex.full(G0), ex.full(U0)], 0, True, x1, g_f0, g2,
                         w_rows=True, phases=[("join", [C_OUT, C_IN]), ("swap", mid)])
    ex.pair_sum(mid)

    dcat = _dgrad_rows("dgrad_out_ab", g1_b, w2d(A_OUT))
    ex.grad[A_OUT] = _wgrad_row_sharded("wgrad_out_ab", cat, g1_b, False)
    d_a, d_sw, d_sbt, d_slg, d_slb = _mixer_a_bwd(proj, dcat, slg, slb, sp["sgu_w"], bias_t)
    early = {"sgu_norm_g": d_slg, "sgu_norm_b": d_slb, "sgu_w": d_sw, "sgu_bias": d_sbt.T}
    ex.grad[REPLICATED_UNIT] = jnp.broadcast_to(
        _pack_replicated(early, REPLICATED_EARLY, REPLICATED_EARLY_ROWS).reshape(2, REPLICATED_EARLY_ROWS // 2, LANES),
        (N_CHIPS, 2, REPLICATED_EARLY_ROWS // 2, LANES))
    last = [A_OUT, REPLICATED_UNIT]
    dbb, dd, db_1, dd_1, db_2, dd_2 = _attn_bwd_prep(dcat, cat)
    regrouped_bwd = {1: (db_1, lse_1, dd_1), 2: (db_2, lse_2, dd_2)}
    bwd_phases = ([("scatter", [D0, SMALL_SHARDED])],
                  [("scatter", [G0]), ("join", [D0, SMALL_SHARDED]), ("swap", last)],
                  [("scatter", [U0]), ("join", [G0])])
    dqs, dks, dvs = [], [], []
    for g, rate in enumerate(DIL_RATES):
        q, k, v = qkv[g]
        if rate == 1:
            db3, l3, dd3 = (a.reshape(1, t, B_WIDTH) for a in (dbb, lse_tot, dd))
        else:
            db3, l3, dd3 = regrouped_bwd[g]
        if g == 1:
            ex.chip_sum([D0, SMALL_SHARDED])
        elif g == 2:
            ex.chip_sum([G0])
            ex.pair_sum(last)
        dq, dk, dv = run(_attn_bwd, f"attn_bwd_{g}", q, k, v, db3, l3, dd3, phases=bwd_phases[g])
        if rate == 1:
            dq, dk, dv = (a.reshape(t, B_WIDTH) for a in (dq, dk, dv))
        dqs.append(dq)
        dks.append(dk)
        dvs.append(dv)
    ex.chip_sum([U0])
    dproj, d_gains = run(_dproj_assemble, proj, d_a, dqs, dks, dvs, gains, tabs, phases=[("scatter", last), ("join", [U0])])
    ex.chip_sum(last)
    d_gains = _fold_heads(d_gains)[0].reshape(2, N_DIL, B_WIDTH)[:, :, :HEAD_DIM]
    ex.grad[A_IN] = _wgrad_col_sharded("wgrad_in_ab", h0, [dproj], False)[0]
    ex.alone("swap_last", [("swap", [A_IN])])
    ex.pair_sum([A_IN])
    gx, d_abn = run(_dgrad_cols, "dgrad_in_ab", [dproj], [ex.full(A_IN)], 0, False, x, g_ab, g1, bf16_copy=False,
                    phases=[("join", last), ("scatter", [A_IN])])
    ex.chip_sum([A_IN])

    small.update({
        "ab_norm_g": d_abn, "sgu_norm_g": d_slg, "sgu_norm_b": d_slb, "sgu_w": d_sw, "sgu_bias": d_sbt.T,
        "q_norm_g": d_gains[0], "k_norm_g": d_gains[1], "ffn_norm_g": jnp.concatenate([d_f0, d_f1], axis=0),
    })
    return loss, gx, small


SHARDED_SMALL = ("cd_norm_g", "conv_c_w", "conv_c_b", "c_ln_g", "c_ln_b", "conv_d_w")
SHARDED_ROWS = 48
REPLICATED_EARLY = ("sgu_norm_g", "sgu_norm_b", "sgu_w", "sgu_bias")
REPLICATED_EARLY_ROWS = 528
REPLICATED_LATE = ("ab_norm_g", "q_norm_g", "k_norm_g", "ffn_norm_g", "loss")
REPLICATED_LATE_ROWS = 32
REPLICATED_SMALL = REPLICATED_EARLY + REPLICATED_LATE[:-1]


def _pack_sharded(parts):
    rows = [parts[k].reshape(-1, LANES) for k in SHARDED_SMALL]
    return _pad_rows(jnp.concatenate(rows, axis=0), SHARDED_ROWS)


def _split_full_small(small):
    per_chip = []
    for q in range(N_CHIPS):
        parts = {}
        for k in SHARDED_SMALL:
            a = small[k]
            a = a.reshape(-1, a.shape[-1])
            n = a.shape[-1] // N_CHIPS
            parts[k] = a[:, q * n:(q + 1) * n]
        per_chip.append(_pack_sharded(parts))
    return jnp.stack(per_chip)


def _unpack_sharded(pack, shapes):
    out, r = {}, 0
    for k in SHARDED_SMALL:
        n = math.prod(shapes[k]) // LANES
        out[k] = pack[r:r + n].reshape(shapes[k])
        r += n
    return out


def _gathered_small(packs, shapes):
    per_chip = [_unpack_sharded(packs[q], shapes) for q in range(N_CHIPS)]
    return {k: jnp.concatenate([pc[k] for pc in per_chip], axis=-1) for k in SHARDED_SMALL}


def _pack_replicated(small, names, total_rows):
    rows = []
    for k in names:
        a = small[k].reshape(-1)
        a = jnp.pad(a, (0, (-a.shape[0]) % LANES))
        rows.append(a.reshape(-1, LANES))
    return _pad_rows(jnp.concatenate(rows, axis=0), total_rows)


def _unpack_replicated(pack, shapes, names):
    out, r = {}, 0
    for k in names:
        size = math.prod(shapes[k])
        n = -(-size // LANES)
        out[k] = pack[r:r + n].reshape(-1)[:size].reshape(shapes[k])
        r += n
    return out


WEIGHT_ORDER = ("ab_norm_g", "ab_w_in", "sgu_norm_g", "sgu_norm_b", "sgu_w", "sgu_bias", "q_norm_g", "k_norm_g", "ab_w_out",
                "cd_norm_g", "cd_w_in", "conv_c_w", "conv_c_b", "c_ln_g", "c_ln_b", "conv_d_w", "cd_w_out", "ffn_norm_g",
                "ffn_w_gate", "ffn_w_up", "ffn_w_down")


def kernel(x, ab_norm_g, ab_w_in, sgu_norm_g, sgu_norm_b, sgu_w, sgu_bias, q_norm_g, k_norm_g, ab_w_out, cd_norm_g, cd_w_in, conv_c_w, conv_c_b, c_ln_g, c_ln_b, conv_d_w, cd_w_out, ffn_norm_g, ffn_w_gate, ffn_w_up, ffn_w_down, loss_target, m_ab_norm_g, m_ab_w_in, m_sgu_norm_g, m_sgu_norm_b, m_sgu_w, m_sgu_bias, m_q_norm_g, m_k_norm_g, m_ab_w_out, m_cd_norm_g, m_cd_w_in, m_conv_c_w, m_conv_c_b, m_c_ln_g, m_c_ln_b, m_conv_d_w, m_cd_w_out, m_ffn_norm_g, m_ffn_w_gate, m_ffn_w_up, m_ffn_w_down, v_ab_norm_g, v_ab_w_in, v_sgu_norm_g, v_sgu_norm_b, v_sgu_w, v_sgu_bias, v_q_norm_g, v_k_norm_g, v_ab_w_out, v_cd_norm_g, v_cd_w_in, v_conv_c_w, v_conv_c_b, v_c_ln_g, v_c_ln_b, v_conv_d_w, v_cd_w_out, v_ffn_norm_g, v_ffn_w_gate, v_ffn_w_up, v_ffn_w_down):
    args = dict(locals())
    ws = {k: args[k] for k in WEIGHT_ORDER}
    ms = {k: args["m_" + k] for k in WEIGHT_ORDER}
    vs = {k: args["v_" + k] for k in WEIGHT_ORDER}
    small_names = [k for k in WEIGHT_ORDER if k not in BIG]
    t, d = x.shape[1:]

    for group in (ws, ms, vs):
        for k in ROWS_MINOR:
            group[k] = jnp.swapaxes(group[k], 1, 2)
    ex = _Exchange(enabled=True)
    ex.w[A_IN], ex.far[A_IN] = _stage_own("stage_ab_w_in", ws["ab_w_in"], 0, BF16, far_slab=True)
    own_small = _pack_sharded({k: ws[k][0] for k in SHARDED_SMALL})
    ex.w[SMALL_SHARDED] = _stage_own("stage_small", own_small[None], 0, F32)
    rest = [u for u in UNITS if u != A_IN]
    x2 = x.reshape(t, d)
    h0, staged = ex.run(_stage_rest_and_norm, x2, ws["ab_norm_g"], [(ws[name], layer) for name, layer in rest],
                        phases=[("gather_near", [A_IN]), ("gather", [SMALL_SHARDED])])
    ex.w.update(zip(rest, staged))
    sp = _gathered_small(ex.w[SMALL_SHARDED].reshape(N_CHIPS, SHARDED_ROWS, LANES), {k: ws[k].shape[1:] for k in SHARDED_SMALL})
    for k in REPLICATED_SMALL:
        sp[k] = ws[k] if k == "ffn_norm_g" else ws[k][0]

    loss, grad_x, g_small = _local_step(x2, loss_target.reshape(t, d), ex, sp, h0)

    shapes = {k: ws[k].shape for k in REPLICATED_SMALL}
    shapes["loss"] = (1,)
    g_small["loss"] = loss
    join_last, settle = ex.ride_for([("join", [A_IN])])
    late, joined = _all_reduce_small(_pack_replicated(g_small, REPLICATED_LATE, REPLICATED_LATE_ROWS), join_last)
    settle(joined)
    grad = _unpack_sharded(ex.done[SMALL_SHARDED].reshape(SHARDED_ROWS, LANES), {k: ws[k].shape for k in SHARDED_SMALL})
    grad.update(_unpack_replicated(ex.done[REPLICATED_UNIT].reshape(REPLICATED_EARLY_ROWS, LANES), shapes, REPLICATED_EARLY))
    grad.update(_unpack_replicated(late, shapes, REPLICATED_LATE))
    loss = grad.pop("loss")[0]

    delta, new_m, new_v = {}, {}, {}
    for group in BIG_BY_LAYERS:
        g_layers = [[ex.done[(k, layer)] for layer in range(ws[k].shape[0])] for k in group]
        results = _adamw_big("adamw_" + group[0], [ws[k] for k in group], g_layers, [ms[k] for k in group], [vs[k] for k in group])
        for k, outs in zip(group, results):
            if k in ROWS_MINOR:
                outs = [jnp.swapaxes(o, 1, 2) for o in outs]
            grad[k], delta[k], new_m[k], new_v[k] = outs
    d_s, m_s, v_s = _adamw_small([ws[k] for k in small_names], [grad[k] for k in small_names],
                                 [ms[k] for k in small_names], [vs[k] for k in small_names])
    for j, k in enumerate(small_names):
        delta[k], new_m[k], new_v[k] = d_s[j], m_s[j], v_s[j]

    return (loss, grad_x[None], *[grad[k] for k in WEIGHT_ORDER], *[delta[k] for k in WEIGHT_ORDER],
            *[new_m[k] for k in WEIGHT_ORDER], *[new_v[k] for k in WEIGHT_ORDER])
```

```python
import functools
import math

import jax
import jax.numpy as jnp
from jax import lax
from jax.experimental import pallas as pl
from jax.experimental.pallas import tpu as pltpu

F32 = jnp.float32
BF16 = jnp.bfloat16
SDS = jax.ShapeDtypeStruct

N_CHIPS = 4
EPS = 1e-6
NEG_INF = -1e30
CHUNK = 128
A_GROUPS = 4
A_WIDTH = 512
N_DIL = 3
DIL_RATES = (1, 4, 16)
HEAD_DIM = 64
B_WIDTH = 512
ROPE_DIM = 16
ROPE_THETA = 500000.0
C_WIDTH = 512
C_KERNEL = 31
D_KERNEL = 3
HALO = 32
ATT_BLOCK = 128
LANES = 128

ADAM_LR = 0.001
ADAM_B1 = 0.9
ADAM_B2 = 0.999
ADAM_EPS = 1e-08
ADAM_WD = 0.01
ADAM_STEP = 10

VMEM_LIMIT = 56 * 1024 * 1024

NN = (((1,), (0,)), ((), ()))
NT = (((1,), (1,)), ((), ()))
TN = (((0,), (0,)), ((), ()))

TILES = {"proj_in": 2048, "proj_out": 1024, "ffn_in": 1024, "ffn_out": 1024, "ffn_dact": 512, "dgrad_cols": 512,
         "dgrad_rows": 1024, "wgrad": 4096}


def _params(sem=None, collective_id=None):
    return pltpu.CompilerParams(dimension_semantics=sem, vmem_limit_bytes=VMEM_LIMIT, collective_id=collective_id)


def _bf(v):
    return v if v.dtype == BF16 else v.astype(BF16)


def _dot(a, b, dims):
    return lax.dot_general(_bf(a), _bf(b), dims, preferred_element_type=F32)


def _dot_hi(a, b):
    return jnp.dot(a, b, precision=lax.Precision.HIGHEST, preferred_element_type=F32)


def _sigmoid(v):
    return 0.5 * jnp.tanh(0.5 * v) + 0.5


def _gelu(v):
    return 0.5 * v * (1.0 + lax.erf(v * (1.0 / math.sqrt(2.0))))


def _gelu_grad(v):
    cdf = 0.5 * (1.0 + lax.erf(v * (1.0 / math.sqrt(2.0))))
    return cdf + v * jnp.exp(-0.5 * v * v) * (1.0 / math.sqrt(2.0 * math.pi))


def _segment_mean_matrix(seg, scale=None):
    r = lax.broadcasted_iota(jnp.int32, (LANES, LANES), 0) // seg
    c = lax.broadcasted_iota(jnp.int32, (LANES, LANES), 1) // seg
    return jnp.where(r == c, (1.0 / seg) if scale is None else scale, 0.0).astype(BF16)


def _segment_dot(v, seg):
    hi = v.astype(BF16)
    lo = (v - hi.astype(F32)).astype(BF16)
    return jnp.dot(hi, seg, preferred_element_type=F32) + jnp.dot(lo, seg, preferred_element_type=F32)


MESH = pl.DeviceIdType.MESH
ANY = pl.BlockSpec(memory_space=pl.ANY)


def _position():
    x, y, c = lax.axis_index("x"), lax.axis_index("y"), lax.axis_index("c")
    others = [(1 - x, y), (x, 1 - y), (1 - x, 1 - y)]
    return x, y, c, 2 * x + y, others


class _Ride:
    def __init__(self, ins, bufs, new_outs, sem_shapes, start, finish, reach):
        self.ins, self.bufs, self.new_outs, self.sem_shapes = list(ins), list(bufs), list(new_outs), list(sem_shapes)
        self.start, self.finish = start, finish
        self.reach = frozenset(reach)

    def entry_barrier(self):
        x, y, c, _, others = _position()
        peers = ([(x, y, 1 - c)] if "sibling" in self.reach else []) + ([(qx, qy, c) for qx, qy in others] if "chips" in self.reach else [])
        barrier = pltpu.get_barrier_semaphore()
        for peer in peers:
            pl.semaphore_signal(barrier, inc=1, device_id=peer, device_id_type=MESH)
        pl.semaphore_wait(barrier, len(peers))

    @property
    def collective_id(self):
        return {frozenset(["sibling"]): 0, frozenset(["chips"]): 1, frozenset(["sibling", "chips"]): 2}[self.reach]


def _ride_both(a, b):
    na = (len(a.ins), len(a.bufs), len(a.new_outs), len(a.sem_shapes))

    def split(ins, bufs, new, sems):
        return ((ins[:na[0]], bufs[:na[1]], new[:na[2]], sems[:na[3]]), (ins[na[0]:], bufs[na[1]:], new[na[2]:], sems[na[3]:]))

    def start(*refs):
        ra, rb = split(*refs)
        a.start(*ra)
        b.start(*rb)

    def finish(*refs):
        ra, rb = split(*refs)
        a.finish(*ra)
        b.finish(*rb)

    return _Ride(a.ins + b.ins, a.bufs + b.bufs, a.new_outs + b.new_outs, a.sem_shapes + b.sem_shapes, start, finish,
                 a.reach | b.reach)


def _call(body, *, grid, in_specs, out_specs, out_shape, operands, name, scratch_shapes=(), aliases=None, ride=None,
          prefetch=None):
    off = 0 if prefetch is None else 1
    lead = [] if prefetch is None else [prefetch]

    params = _params(collective_id=None if ride is None else ride.collective_id)

    def launch(kernel_body, in_specs_, out_specs_, out_shape_, scratch_, aliases_, *args):
        if prefetch is None:
            return pl.pallas_call(kernel_body, grid=grid, in_specs=in_specs_, out_specs=out_specs_, out_shape=out_shape_,
                                  scratch_shapes=scratch_, input_output_aliases=aliases_, name=name,
                                  compiler_params=params)(*args)
        spec = pltpu.PrefetchScalarGridSpec(num_scalar_prefetch=1, grid=grid, in_specs=in_specs_, out_specs=out_specs_,
                                            scratch_shapes=scratch_)
        return pl.pallas_call(kernel_body, grid_spec=spec, out_shape=out_shape_, input_output_aliases=aliases_, name=name,
                              compiler_params=params)(*lead, *args)

    if ride is None:
        return launch(body, list(in_specs), out_specs, out_shape, list(scratch_shapes), dict(aliases or {}), *operands)
    multi = isinstance(out_shape, (list, tuple))
    out_shapes = list(out_shape) if multi else [out_shape]
    o_specs = list(out_specs) if multi else [out_specs]
    n_in, n_out, n_scr = off + len(operands), len(out_shapes), len(scratch_shapes)
    n_ri, n_rb, n_rn = len(ride.ins), len(ride.bufs), len(ride.new_outs)

    def carrying(*refs):
        k = n_in
        r_ins = refs[k:k + n_ri]
        k += n_ri + n_rb
        outs = refs[k:k + n_out]
        k += n_out
        r_bufs = refs[k:k + n_rb]
        k += n_rb
        r_new = refs[k:k + n_rn]
        k += n_rn
        scratch = refs[k:k + n_scr]
        sems = refs[k + n_scr:]
        first, last = None, None
        for axis, size in enumerate(grid):
            pid = pl.program_id(axis)
            first = (pid == 0) if first is None else first & (pid == 0)
            last = (pid == size - 1) if last is None else last & (pid == size - 1)

        @pl.when(first)
        def _():
            ride.entry_barrier()
            ride.start(r_ins, r_bufs, r_new, sems)

        body(*refs[:n_in], *outs, *scratch)

        @pl.when(last)
        def _():
            ride.finish(r_ins, r_bufs, r_new, sems)

    all_aliases = dict(aliases or {})
    for j in range(n_rb):
        all_aliases[n_in + n_ri + j] = n_out + j
    res = launch(
        carrying, list(in_specs) + [ANY] * (n_ri + n_rb), o_specs + [ANY] * (n_rb + n_rn),
        out_shapes + [SDS(b.shape, b.dtype) for b in ride.bufs] + ride.new_outs,
        list(scratch_shapes) + [pltpu.SemaphoreType.DMA(s) for s in ride.sem_shapes], all_aliases,
        *operands, *ride.ins, *ride.bufs)
    outs = res[:n_out]
    return (list(outs) if multi else outs[0]), list(res[n_out:])


def _run_ride(name, ride):
    n_ri, n_rb, n_rn = len(ride.ins), len(ride.bufs), len(ride.new_outs)

    def body(*refs):
        r_ins = refs[:n_ri]
        r_bufs = refs[n_ri + n_rb:n_ri + 2 * n_rb]
        r_new = refs[n_ri + 2 * n_rb:n_ri + 2 * n_rb + n_rn]
        sems = refs[n_ri + 2 * n_rb + n_rn:]
        ride.entry_barrier()
        ride.start(r_ins, r_bufs, r_new, sems)
        ride.finish(r_ins, r_bufs, r_new, sems)

    return list(pl.pallas_call(
        body, in_specs=[ANY] * (n_ri + n_rb), out_specs=[ANY] * (n_rb + n_rn),
        out_shape=[SDS(b.shape, b.dtype) for b in ride.bufs] + ride.new_outs,
        scratch_shapes=[pltpu.SemaphoreType.DMA(s) for s in ride.sem_shapes],
        input_output_aliases={n_ri + j: j for j in range(n_rb)}, name=name,
        compiler_params=pltpu.CompilerParams(collective_id=ride.collective_id))(*ride.ins, *ride.bufs))


def _whole(ref, p):
    return ref[...]


def _slab(ref, p):
    return ref[p]


def _matmul(name, grid, pairs, extras, outs, dims, epi, *, slabs=1, n_acc=1, ride=None):
    n_pairs, n_ex, n_out = len(pairs), len(extras), len(outs)

    def body(*refs):
        ab = refs[:2 * n_pairs]
        ex = refs[2 * n_pairs:2 * n_pairs + n_ex]
        out_refs = refs[2 * n_pairs + n_ex:2 * n_pairs + n_ex + n_out]
        pids = tuple(pl.program_id(a) for a in range(len(grid)))
        parts = [None] * n_acc
        for p in range(slabs):
            for j, (_, _, a_pick, _, _, b_pick, acc) in enumerate(pairs):
                d = _dot(a_pick(ab[2 * j], p), b_pick(ab[2 * j + 1], p), dims)
                parts[acc] = d if parts[acc] is None else parts[acc] + d
        epi(parts, ex, out_refs, pids)

    operands, in_specs = [], []
    for a, a_spec, _, b, b_spec, _, _ in pairs:
        operands += [a, b]
        in_specs += [a_spec, b_spec]
    for e, e_spec in extras:
        operands.append(e)
        in_specs.append(e_spec)
    return _call(body, grid=grid, in_specs=in_specs, out_specs=[o[1] for o in outs], out_shape=[o[0] for o in outs],
                 operands=operands, name=name, ride=ride)


def _rms_rows(v, g):
    r = lax.rsqrt(jnp.mean(v * v, axis=-1, keepdims=True) + EPS)
    return v * r * g


def _rms_fwd(name, x, g):
    t, d = x.shape
    tm = 512

    def body(x_ref, g_ref, o_ref):
        o_ref[...] = _rms_rows(x_ref[...], g_ref[...]).astype(BF16)

    return pl.pallas_call(
        body, grid=(t // tm,),
        in_specs=[pl.BlockSpec((tm, d), lambda i: (i, 0)), pl.BlockSpec((1, d), lambda i: (0, 0))],
        out_specs=pl.BlockSpec((tm, d), lambda i: (i, 0)), out_shape=SDS((t, d), BF16), name=name,
        compiler_params=_params())(x, g)


def _epi_residual_norm(accs, ex, outs, pids):
    x_new = accs[0] + ex[0][...]
    outs[0][...] = x_new
    outs[1][...] = _rms_rows(x_new, ex[1][...]).astype(BF16)


def _epi_residual_loss(accs, ex, outs, pids):
    y = accs[0] + ex[0][...]
    err = y - ex[1][...]
    dy = err * (1.0 / err.shape[-1])
    outs[0][...] = dy
    outs[2][...] = dy.astype(BF16)

    @pl.when(pids[0] == 0)
    def _():
        outs[1][...] = jnp.zeros_like(outs[1])

    outs[1][...] += jnp.sum(err * err)


def _epi_rms_bwd(accs, ex, outs, pids):
    dh = accs[0]
    xv, g, res = ex[0][...], ex[1][...], ex[2][...]
    r = lax.rsqrt(jnp.mean(xv * xv, axis=-1, keepdims=True) + EPS)
    xh = xv * r
    dy = dh * g
    dx = res + r * (dy - xh * jnp.mean(dy * xh, axis=-1, keepdims=True))
    outs[0][...] = dx
    if len(outs) > 2:
        outs[2][...] = dx.astype(BF16)

    @pl.when(pids[0] == 0)
    def _():
        outs[1][...] = jnp.zeros_like(outs[1])

    outs[1][...] += jnp.sum(dh * xh, axis=0, keepdims=True)


def _row_spec(tm, d):
    return pl.BlockSpec((tm, d), lambda i, *_: (i, 0))


def _const_spec(shape):
    nd = len(shape)
    return pl.BlockSpec(shape, lambda *_: (0,) * nd)


def _proj_in(name, h, w, layer, ride=None):
    t, d = h.shape
    n4 = w.shape[-1]
    tm = TILES["proj_in"]

    def epi(accs, ex, outs, pids):
        outs[0][...] = accs[0].astype(BF16)

    res = _matmul(
        name, (N_CHIPS, t // tm),
        [(h, pl.BlockSpec((tm, d), lambda p, i: (i, 0)), _whole,
          w, pl.BlockSpec((None, None, d, n4), lambda p, i: (p, layer, 0, 0)), _whole, 0)],
        [], [(SDS((t, N_CHIPS * n4), BF16), pl.BlockSpec((tm, n4), lambda p, i: (i, p)))],
        NN, epi, ride=ride)
    return res[0] if ride is None else (res[0][0], res[1])


def _proj_in_near(name, h, w, ride=None):
    t, d = h.shape
    n4 = w.shape[-1]
    tm = TILES["proj_in"]

    def shard(j, s):
        return j + (j >= N_CHIPS - 1 - s[1]).astype(jnp.int32)

    def body(s_ref, h_ref, w_ref, o_ref):
        o_ref[...] = _dot(h_ref[...], w_ref[...], NN).astype(BF16)

    return _call(
        body, grid=(N_CHIPS - 1, t // tm),
        in_specs=[pl.BlockSpec((tm, d), lambda j, i, s: (i, 0)),
                  pl.BlockSpec((None, None, d, n4), lambda j, i, s: (shard(j, s), 0, 0, 0))],
        out_specs=pl.BlockSpec((tm, n4), lambda j, i, s: (i, shard(j, s))), out_shape=SDS((t, N_CHIPS * n4), BF16),
        operands=[h, w], name=name, ride=ride, prefetch=_mesh_scalars())


def _proj_in_far(name, h, slab, proj, w):
    t, d = h.shape
    n4 = slab.shape[-1]
    tm = TILES["proj_in"]

    def body(s_ref, h_ref, slab_ref, proj_in, w_in, o_ref, w_ref):
        shard = slab_ref[...]
        o_ref[...] = _dot(h_ref[...], shard, NN).astype(BF16)
        w_ref[...] = shard

    proj, w_full = _call(
        body, grid=(t // tm,),
        in_specs=[pl.BlockSpec((tm, d), lambda i, s: (i, 0)), pl.BlockSpec((d, n4), lambda i, s: (0, 0)), ANY, ANY],
        out_specs=[pl.BlockSpec((tm, n4), lambda i, s: (i, N_CHIPS - 1 - s[1])),
                   pl.BlockSpec((None, d, n4), lambda i, s: (N_CHIPS - 1 - s[1], 0, 0))],
        out_shape=[SDS(proj.shape, BF16), SDS((N_CHIPS, d, n4), BF16)],
        operands=[h, slab.reshape(d, n4), proj, w.reshape(N_CHIPS, d, n4)], aliases={3: 0, 4: 1}, name=name,
        prefetch=_mesh_scalars())
    return proj, w_full.reshape(w.shape)


def _proj_out(name, a, w, x, g_next=None, target=None):
    t, k = a.shape
    d = w.shape[-1]
    tm = TILES["proj_out"]
    if target is None:
        extras = [(x, _row_spec(tm, d)), (g_next, _const_spec((1, d)))]
        outs = [(SDS((t, d), F32), _row_spec(tm, d)), (SDS((t, d), BF16), _row_spec(tm, d))]
        epi = _epi_residual_norm
    else:
        extras = [(x, _row_spec(tm, d)), (target, _row_spec(tm, d))]
        outs = [(SDS((t, d), F32), _row_spec(tm, d)), (SDS((8, LANES), F32), _const_spec((8, LANES))),
                (SDS((t, d), BF16), _row_spec(tm, d))]
        epi = _epi_residual_loss
    return _matmul(name, (t // tm,), [(a, _row_spec(tm, k), _whole, w, _const_spec((k, d)), _whole, 0)], extras, outs, NN, epi)


def _ffn_in(name, h, wg, wu, layer, ride=None):
    t, d = h.shape
    n4 = wg.shape[-2]
    tm = TILES["ffn_in"]

    def epi(accs, ex, outs, pids):
        gate, up = accs
        s = _sigmoid(gate)
        silu = gate * s
        outs[0][...] = (up * (s + silu - silu * s)).astype(BF16)
        outs[1][...] = silu.astype(BF16)
        outs[2][...] = (silu * up).astype(BF16)

    w_spec = pl.BlockSpec((None, None, n4, d), lambda p, i: (p, layer, 0, 0))
    h_spec = pl.BlockSpec((tm, d), lambda p, i: (i, 0))
    o = (SDS((N_CHIPS, t, n4), BF16), pl.BlockSpec((None, tm, n4), lambda p, i: (p, i, 0)))
    return _matmul(name, (N_CHIPS, t // tm),
                   [(h, h_spec, _whole, wg, w_spec, _whole, 0), (h, h_spec, _whole, wu, w_spec, _whole, 1)], [],
                   [o, o, o], NT, epi, n_acc=2, ride=ride)


def _ffn_out(name, act, wd, layer, x, g_next=None, target=None, ride=None):
    _, t, n4 = act.shape
    d = wd.shape[-1]
    tm = TILES["ffn_out"]
    xs = _row_spec(tm, d)
    if target is None:
        extras = [(x, xs), (g_next, _const_spec((1, d)))]
        outs = [(SDS((t, d), F32), xs), (SDS((t, d), BF16), xs)]
        epi = _epi_residual_norm
    else:
        extras = [(x, xs), (target, xs)]
        outs = [(SDS((t, d), F32), xs), (SDS((8, LANES), F32), _const_spec((8, LANES))), (SDS((t, d), BF16), xs)]
        epi = _epi_residual_loss
    return _matmul(
        name, (t // tm,),
        [(act, pl.BlockSpec((N_CHIPS, tm, n4), lambda i: (0, i, 0)), _slab,
          wd, pl.BlockSpec((N_CHIPS, None, n4, d), lambda i: (0, layer, 0, 0)), _slab, 0)],
        extras, outs, NN, epi, slabs=N_CHIPS, ride=ride)


def _ffn_dact(name, g, wd, layer, gate, up, ride=None):
    t, d = g.shape
    n4 = wd.shape[-2]
    tm = TILES["ffn_dact"]

    def body(g_ref, w_ref, gate_ref, up_ref, dgate_ref, dup_ref):
        gv = g_ref[...]
        for p in range(N_CHIPS):
            dact = _dot(gv, w_ref[p], NT)
            dgate_ref[p] = (dact * gate_ref[p].astype(F32)).astype(BF16)
            dup_ref[p] = (dact * up_ref[p].astype(F32)).astype(BF16)

    blk = pl.BlockSpec((N_CHIPS, tm, n4), lambda i: (0, i, 0))
    return _call(
        body, grid=(t // tm,),
        in_specs=[_row_spec(tm, d), pl.BlockSpec((N_CHIPS, None, n4, d), lambda i: (0, layer, 0, 0)), blk, blk],
        out_specs=[blk, blk], out_shape=[SDS((N_CHIPS, t, n4), BF16)] * 2, operands=[g, wd, gate, up], name=name, ride=ride)


def _copy_epi(accs, ex, outs, pids):
    for a, o in zip(accs, outs):
        o[...] = a.astype(o.dtype)


def _dgrad_cols(name, dz_list, w_list, layer, three_d, x, g, res, bf16_copy=True, w_rows=False, ride=None):
    t, d = x.shape
    n4 = w_list[0].shape[-2 if w_rows else -1]
    tm = TILES["dgrad_cols"]
    if three_d:
        zs, z_pick = pl.BlockSpec((N_CHIPS, tm, n4), lambda i: (0, i, 0)), _slab
    else:
        zs, z_pick = _row_spec(tm, N_CHIPS * n4), (lambda ref, p: ref[:, p * n4:(p + 1) * n4])
    ws = pl.BlockSpec((N_CHIPS, None) + ((n4, d) if w_rows else (d, n4)), lambda i: (0, layer, 0, 0))
    xs = _row_spec(tm, d)
    return _matmul(
        name, (t // tm,), [(dz, zs, z_pick, w, ws, _slab, 0) for dz, w in zip(dz_list, w_list)],
        [(x, xs), (g, _const_spec((1, d))), (res, xs)],
        [(SDS((t, d), F32), xs), (SDS((1, d), F32), _const_spec((1, d)))] + ([(SDS((t, d), BF16), xs)] if bf16_copy else []),
        NN if w_rows else NT, _epi_rms_bwd, slabs=N_CHIPS, ride=ride)


def _dgrad_rows(name, g, w):
    t, d = g.shape
    k = w.shape[0]
    tm = TILES["dgrad_rows"]
    return _matmul(name, (t // tm,), [(g, _row_spec(tm, d), _whole, w, _const_spec((k, d)), _whole, 0)], [],
                   [(SDS((t, k), F32), _row_spec(tm, k))], NT, _copy_epi)[0]


A_TILE = 256


def _a_common(p_ref, lg_ref, lb_ref):
    pv = p_ref[...].astype(F32)
    a = _gelu(pv)
    u, v = a[:, :A_WIDTH], a[:, A_WIDTH:]
    vc = v - jnp.mean(v, axis=-1, keepdims=True)
    rs = lax.rsqrt(jnp.mean(vc * vc, axis=-1, keepdims=True) + EPS)
    vhat = vc * rs
    vn = vhat * lg_ref[...] + lb_ref[...]
    return pv, u, vhat, rs, vn.astype(BF16)


def _tril_weights(w_ref, g):
    r = lax.broadcasted_iota(jnp.int32, (CHUNK, CHUNK), 0)
    c = lax.broadcasted_iota(jnp.int32, (CHUNK, CHUNK), 1)
    return jnp.where(c <= r, w_ref[g], 0.0).astype(BF16), c <= r


def _mixer_a_fwd(proj, lg, lb, w, bias_t):
    t = proj.shape[0]

    def body(p_ref, lg_ref, lb_ref, w_ref, bt_ref, o_ref):
        _, u, _, _, vnb = _a_common(p_ref, lg_ref, lb_ref)
        for g in range(A_GROUPS):
            wt, _ = _tril_weights(w_ref, g)
            cs = slice(g * CHUNK, (g + 1) * CHUNK)
            for ch in range(A_TILE // CHUNK):
                rs_ = slice(ch * CHUNK, (ch + 1) * CHUNK)
                mixed = _dot(wt, vnb[rs_, cs], NN) + bt_ref[:, g:g + 1]
                o_ref[rs_, cs] = (u[rs_, cs] * mixed).astype(BF16)

    return pl.pallas_call(
        body, grid=(t // A_TILE,),
        in_specs=[pl.BlockSpec((A_TILE, 2 * A_WIDTH), lambda i: (i, 0)), _const_spec((1, A_WIDTH)),
                  _const_spec((1, A_WIDTH)), _const_spec((A_GROUPS, CHUNK, CHUNK)), _const_spec((CHUNK, A_GROUPS))],
        out_specs=pl.BlockSpec((A_TILE, A_WIDTH), lambda i: (i, 0)), out_shape=SDS((t, A_WIDTH), BF16),
        name="mixer_a_fwd", compiler_params=_params())(proj, lg, lb, w, bias_t)


def _mixer_a_bwd(proj, dcat, lg, lb, w, bias_t):
    t = proj.shape[0]

    def body(p_ref, da_ref, lg_ref, lb_ref, w_ref, bt_ref, dp_ref, dw_ref, dbt_ref, dlg_ref, dlb_ref, du_scr, dvn_scr):
        @pl.when(pl.program_id(0) == 0)
        def _():
            dw_ref[...] = jnp.zeros_like(dw_ref)
            dbt_ref[...] = jnp.zeros_like(dbt_ref)
            dlg_ref[...] = jnp.zeros_like(dlg_ref)
            dlb_ref[...] = jnp.zeros_like(dlb_ref)

        pv, u, vhat, rs, vnb = _a_common(p_ref, lg_ref, lb_ref)
        da = da_ref[...]
        for g in range(A_GROUPS):
            wt, keep = _tril_weights(w_ref, g)
            cs = slice(g * CHUNK, (g + 1) * CHUNK)
            for ch in range(A_TILE // CHUNK):
                rs_ = slice(ch * CHUNK, (ch + 1) * CHUNK)
                vg = vnb[rs_, cs]
                mixed = _dot(wt, vg, NN) + bt_ref[:, g:g + 1]
                du_scr[rs_, cs] = da[rs_, cs] * mixed
                dmx = da[rs_, cs] * u[rs_, cs]
                dw_ref[g] += jnp.where(keep, _dot(dmx, vg, NT), 0.0)
                dvn_scr[rs_, cs] = _dot(wt, dmx, TN)
                dbt_ref[:, g:g + 1] += jnp.sum(dmx, axis=1, keepdims=True)
        dvn = dvn_scr[...]
        dlg_ref[...] += jnp.sum(dvn * vhat, axis=0, keepdims=True)
        dlb_ref[...] += jnp.sum(dvn, axis=0, keepdims=True)
        dvh = dvn * lg_ref[...]
        dv = rs * (dvh - jnp.mean(dvh, axis=-1, keepdims=True) - vhat * jnp.mean(dvh * vhat, axis=-1, keepdims=True))
        gp = _gelu_grad(pv)
        dp_ref[:, :A_WIDTH] = (du_scr[...] * gp[:, :A_WIDTH]).astype(BF16)
        dp_ref[:, A_WIDTH:] = (dv * gp[:, A_WIDTH:]).astype(BF16)

    return pl.pallas_call(
        body, grid=(t // A_TILE,),
        in_specs=[pl.BlockSpec((A_TILE, 2 * A_WIDTH), lambda i: (i, 0)), pl.BlockSpec((A_TILE, A_WIDTH), lambda i: (i, 0)),
                  _const_spec((1, A_WIDTH)), _const_spec((1, A_WIDTH)), _const_spec((A_GROUPS, CHUNK, CHUNK)),
                  _const_spec((CHUNK, A_GROUPS))],
        out_specs=[pl.BlockSpec((A_TILE, 2 * A_WIDTH), lambda i: (i, 0)), _const_spec((A_GROUPS, CHUNK, CHUNK)),
                   _const_spec((CHUNK, A_GROUPS)), _const_spec((1, A_WIDTH)), _const_spec((1, A_WIDTH))],
        out_shape=[SDS((t, 2 * A_WIDTH), BF16), SDS((A_GROUPS, CHUNK, CHUNK), F32), SDS((CHUNK, A_GROUPS), F32),
                   SDS((1, A_WIDTH), F32), SDS((1, A_WIDTH), F32)],
        scratch_shapes=[pltpu.VMEM((A_TILE, A_WIDTH), F32), pltpu.VMEM((A_TILE, A_WIDTH), F32)],
        name="mixer_a_bwd", compiler_params=_params())(proj, dcat, lg, lb, w, bias_t)


def _rope_tables(t):
    half = ROPE_DIM // 2
    inv_freq = ROPE_THETA ** (-jnp.arange(half, dtype=F32) * 2.0 / ROPE_DIM)
    ang = jnp.arange(t, dtype=F32)[:, None] * inv_freq[None, :]
    cos, sin = jnp.cos(ang), jnp.sin(ang)
    one = jnp.ones((t, HEAD_DIM - ROPE_DIM), F32)
    zero = jnp.zeros((t, HEAD_DIM - ROPE_DIM), F32)
    zh = jnp.zeros((t, half), F32)
    c = jnp.concatenate([cos, cos, one], axis=1)
    s1 = jnp.concatenate([-sin, zh, zero], axis=1)
    s2 = jnp.concatenate([zh, sin, zero], axis=1)
    return tuple(jnp.tile(a, (1, LANES // HEAD_DIM)) for a in (c, s1, s2))


QK_TILE = 512
QK_ROWS = 64
QK_COLS = 2 * N_DIL * B_WIDTH


CHUNKS = B_WIDTH // LANES


def _regroup_out(scr, first, out_ref, rate, tile):
    rows = tile // rate
    for rho in range(rate):
        for c in range(CHUNKS):
            out_ref[rho, :, c * LANES:(c + 1) * LANES] = scr[first + c, pl.ds(rho, rows, stride=rate), :].astype(out_ref.dtype)


def _regroup_in(x_ref, scr, rate, tile):
    rows = tile // rate
    for rho in range(rate):
        for c in range(CHUNKS):
            scr[c, pl.ds(rho, rows, stride=rate), :] = x_ref[rho, :, c * LANES:(c + 1) * LANES].astype(F32)


def _regrouped_spec(rate, tile):
    return pl.BlockSpec((rate, tile // rate, B_WIDTH), lambda i, *_: (0, i, 0))


def _qk_fwd(proj, gains, tabs, ride=None):
    t = proj.shape[0]
    col0 = 2 * A_WIDTH // 1024
    r1, r2 = DIL_RATES[1], DIL_RATES[2]

    def body(p_ref, g_ref, c_ref, s1_ref, s2_ref, o_ref, q1_ref, q2_ref, k1_ref, k2_ref, scr):
        seg = _segment_mean_matrix(HEAD_DIM)
        for r0 in range(0, QK_TILE, QK_ROWS):
            rows = slice(r0, r0 + QK_ROWS)
            c, s1, s2 = c_ref[rows, :], s1_ref[rows, :], s2_ref[rows, :]
            for ci in range(1024 // LANES):
                ls = slice(ci * LANES, (ci + 1) * LANES)
                xv = p_ref[rows, ls].astype(F32)
                r = lax.rsqrt(_segment_dot(xv * xv, seg) + EPS)
                y = xv * r * g_ref[:, ls]
                val = y * c + pltpu.roll(y, LANES - 8, axis=1) * s1 + pltpu.roll(y, 8, axis=1) * s2
                o_ref[rows, ls] = val.astype(BF16)
                scr[ci, rows, :] = val

        j = pl.program_id(1)

        @pl.when(j == 0)
        def _():
            _regroup_out(scr, CHUNKS, q1_ref, r1, QK_TILE)

        @pl.when(j == 1)
        def _():
            _regroup_out(scr, 0, q2_ref, r2, QK_TILE)

        @pl.when(j == 2)
        def _():
            _regroup_out(scr, 0, k1_ref, r1, QK_TILE)
            _regroup_out(scr, CHUNKS, k2_ref, r2, QK_TILE)

    tab = pl.BlockSpec((QK_TILE, LANES), lambda i, j: (i, 0))
    g1, g2 = SDS((r1, t // r1, B_WIDTH), BF16), SDS((r2, t // r2, B_WIDTH), BF16)
    s1_, s2_ = _regrouped_spec(r1, QK_TILE), _regrouped_spec(r2, QK_TILE)
    return _call(
        body, grid=(t // QK_TILE, QK_COLS // 1024),
        in_specs=[pl.BlockSpec((QK_TILE, 1024), lambda i, j: (i, col0 + j)), pl.BlockSpec((1, 1024), lambda i, j: (0, j)),
                  tab, tab, tab],
        out_specs=[pl.BlockSpec((QK_TILE, 1024), lambda i, j: (i, j)), s1_, s2_, s1_, s2_],
        out_shape=[SDS((t, QK_COLS), BF16), g1, g2, g1, g2],
        scratch_shapes=[pltpu.VMEM((2 * CHUNKS, QK_TILE, LANES), F32)],
        operands=[proj, gains, *tabs], name="qk_norm_rope_fwd", ride=ride)


PERM_TILE = 512


def _permute(name, items, rate):
    t = items[0][0].shape[0]
    n = len(items)

    def body(*refs):
        scr = refs[-1]
        for x_ref, o_ref in zip(refs[:n], refs[n:2 * n]):
            for ci in range(CHUNKS):
                scr[ci] = x_ref[:, ci * LANES:(ci + 1) * LANES].astype(F32)
            _regroup_out(scr, 0, o_ref, rate, PERM_TILE)

    return pl.pallas_call(
        body, grid=(t // PERM_TILE,),
        in_specs=[pl.BlockSpec((PERM_TILE, B_WIDTH), functools.partial(lambda cb, i: (i, cb), cb)) for _, cb in items],
        out_specs=[_regrouped_spec(rate, PERM_TILE) for _ in items],
        out_shape=[SDS((rate, t // rate, B_WIDTH), a.dtype) for a, _ in items],
        scratch_shapes=[pltpu.VMEM((CHUNKS, PERM_TILE, LANES), F32)],
        name=name, compiler_params=_params())(*[a for a, _ in items])


def _head_lane_mask(h):
    lane = lax.broadcasted_iota(jnp.int32, (1, LANES), 1)
    return (lane < HEAD_DIM) if h == 0 else (lane >= HEAD_DIM)


def _attn_fwd(name, q, k, v, ride=None):
    rate, length = q[0].shape[0], q[0].shape[1]
    nb = length // ATT_BLOCK
    scale = HEAD_DIM ** -0.5

    def body(q_ref, kc_ref, kp_ref, vc_ref, vp_ref, o_ref, l_ref):
        n = pl.program_id(1)
        qi = lax.broadcasted_iota(jnp.int32, (ATT_BLOCK, 2 * ATT_BLOCK), 0)
        cj = lax.broadcasted_iota(jnp.int32, (ATT_BLOCK, 2 * ATT_BLOCK), 1)
        has_prev = jnp.where(n > 0, 0, 2 * ATT_BLOCK)
        mask = ((cj < ATT_BLOCK) & (cj >= qi + has_prev)) | ((cj >= ATT_BLOCK) & (cj - ATT_BLOCK <= qi))
        heads = [(hp, h) for hp in range(CHUNKS) for h in range(2)]
        q2, k2, v2 = {}, {}, {}
        for hp in range(CHUNKS):
            ls = slice(hp * LANES, (hp + 1) * LANES)
            q2[hp] = q_ref[:, ls]
            k2[hp] = jnp.concatenate([kp_ref[:, ls], kc_ref[:, ls]], axis=0)
            v2[hp] = jnp.concatenate([vp_ref[:, ls], vc_ref[:, ls]], axis=0)
        scores = {}
        for hp, h in heads:
            scores[hp, h] = _dot(jnp.where(_head_lane_mask(h), q2[hp], jnp.zeros_like(q2[hp])), k2[hp], NT) * scale
        probs, lses = {}, {}
        for hp, h in heads:
            s = jnp.where(mask, scores[hp, h], NEG_INF)
            m = jnp.max(s, axis=1, keepdims=True)
            p = jnp.exp(s - m)
            den = jnp.sum(p, axis=1, keepdims=True)
            lses[hp, h] = m + jnp.log(den)
            probs[hp, h] = (p / den).astype(BF16)
        for hp in range(CHUNKS):
            ls = slice(hp * LANES, (hp + 1) * LANES)
            o_acc = None
            for h in range(2):
                o = _dot(probs[hp, h], jnp.where(_head_lane_mask(h), v2[hp], jnp.zeros_like(v2[hp])), NN)
                o_acc = o if o_acc is None else o_acc + o
            o_ref[:, ls] = o_acc
            zeros = jnp.zeros((ATT_BLOCK, LANES), F32)
            l_ref[:, ls] = jnp.where(_head_lane_mask(1), lses[hp, 1] + zeros, lses[hp, 0] + zeros)

    def cur(cb):
        return pl.BlockSpec((None, ATT_BLOCK, B_WIDTH), lambda r, n: (r, n, cb))

    def prev(cb):
        return pl.BlockSpec((None, ATT_BLOCK, B_WIDTH), lambda r, n: (r, jnp.maximum(n - 1, 0), cb))

    out = pl.BlockSpec((None, ATT_BLOCK, B_WIDTH), lambda r, n: (r, n, 0))
    return _call(
        body, grid=(rate, nb),
        in_specs=[cur(q[1]), cur(k[1]), prev(k[1]), cur(v[1]), prev(v[1])],
        out_specs=[out, out], out_shape=[SDS((rate, length, B_WIDTH), F32)] * 2,
        operands=[q[0], k[0], k[0], v[0], v[0]], name=name, ride=ride)


def _attn_merge(a_out, o_list, l_list):
    t = a_out.shape[0]
    tm = PERM_TILE
    r1, r2 = DIL_RATES[1], DIL_RATES[2]

    def body(a_ref, o0, o1, o2, l0, l1, l2, cat_ref, lt_ref, lt1_ref, lt2_ref, so1, so2, sl1, sl2, slt):
        _regroup_in(o1, so1, r1, tm)
        _regroup_in(l1, sl1, r1, tm)
        _regroup_in(o2, so2, r2, tm)
        _regroup_in(l2, sl2, r2, tm)
        cat_ref[:, :A_WIDTH] = a_ref[...]
        for c in range(CHUNKS):
            ls = slice(c * LANES, (c + 1) * LANES)
            lg = [l0[:, ls], sl1[c], sl2[c]]
            m = jnp.maximum(jnp.maximum(lg[0], lg[1]), lg[2])
            es = [jnp.exp(l - m) for l in lg]
            den = es[0] + es[1] + es[2]
            b = (es[0] * o0[:, ls] + es[1] * so1[c] + es[2] * so2[c]) / den
            cat_ref[:, A_WIDTH + c * LANES:A_WIDTH + (c + 1) * LANES] = b.astype(BF16)
            lt = m + jnp.log(den)
            lt_ref[:, ls] = lt
            slt[c] = lt
        _regroup_out(slt, 0, lt1_ref, r1, tm)
        _regroup_out(slt, 0, lt2_ref, r2, tm)

    blk = _row_spec(tm, B_WIDTH)
    g1, g2 = _regrouped_spec(r1, tm), _regrouped_spec(r2, tm)
    return pl.pallas_call(
        body, grid=(t // tm,), in_specs=[blk, blk, g1, g2, blk, g1, g2],
        out_specs=[_row_spec(tm, A_WIDTH + B_WIDTH), blk, g1, g2],
        out_shape=[SDS((t, A_WIDTH + B_WIDTH), BF16), SDS((t, B_WIDTH), F32), SDS((r1, t // r1, B_WIDTH), F32),
                   SDS((r2, t // r2, B_WIDTH), F32)],
        scratch_shapes=[pltpu.VMEM((CHUNKS, tm, LANES), F32)] * 5,
        name="attn_merge", compiler_params=_params())(a_out, *o_list, *l_list)


def _attn_bwd_prep(dcat, cat):
    t = dcat.shape[0]
    tm = PERM_TILE
    r1, r2 = DIL_RATES[1], DIL_RATES[2]

    def body(d_ref, b_ref, db_ref, dd_ref, db1_ref, dd1_ref, db2_ref, dd2_ref, sdb, sdd):
        seg = _segment_mean_matrix(HEAD_DIM, scale=1.0)
        for c in range(CHUNKS):
            ls = slice(c * LANES, (c + 1) * LANES)
            d = d_ref[:, ls]
            dsum = _segment_dot(d * b_ref[:, ls].astype(F32), seg)
            db_ref[:, ls] = d.astype(BF16)
            dd_ref[:, ls] = dsum
            sdb[c] = d
            sdd[c] = dsum
        _regroup_out(sdb, 0, db1_ref, r1, tm)
        _regroup_out(sdd, 0, dd1_ref, r1, tm)
        _regroup_out(sdb, 0, db2_ref, r2, tm)
        _regroup_out(sdd, 0, dd2_ref, r2, tm)

    right = pl.BlockSpec((tm, B_WIDTH), lambda i: (i, 1))
    blk = _row_spec(tm, B_WIDTH)
    g1, g2 = _regrouped_spec(r1, tm), _regrouped_spec(r2, tm)
    return pl.pallas_call(
        body, grid=(t // tm,), in_specs=[right, right], out_specs=[blk, blk, g1, g1, g2, g2],
        out_shape=[SDS((t, B_WIDTH), BF16), SDS((t, B_WIDTH), F32), SDS((r1, t // r1, B_WIDTH), BF16),
                   SDS((r1, t // r1, B_WIDTH), F32), SDS((r2, t // r2, B_WIDTH), BF16), SDS((r2, t // r2, B_WIDTH), F32)],
        scratch_shapes=[pltpu.VMEM((CHUNKS, tm, LANES), F32)] * 2,
        name="attn_bwd_prep", compiler_params=_params())(dcat, cat)


def _attn_bwd(name, q, k, v, db, lse, dd, ride=None):
    rate, length = db.shape[0], db.shape[1]
    nb = length // ATT_BLOCK
    scale = HEAD_DIM ** -0.5

    def body(qa_ref, qb_ref, k_ref, v_ref, dba_ref, dbb_ref, la_ref, lb_ref, da_ref, dbd_ref, dq_ref, dk_ref, dv_ref, carry):
        m = pl.program_id(1)

        @pl.when(m == 0)
        def _():
            carry[...] = jnp.zeros_like(carry)

        row = lax.broadcasted_iota(jnp.int32, (2 * ATT_BLOCK, ATT_BLOCK), 0)
        kj = lax.broadcasted_iota(jnp.int32, (2 * ATT_BLOCK, ATT_BLOCK), 1)
        no_next = jnp.where(m + 1 < nb, 0, 2 * ATT_BLOCK)
        mask = ((row < ATT_BLOCK) & (kj <= row)) | ((row >= ATT_BLOCK) & (kj >= row - ATT_BLOCK + no_next))
        heads = [(hp, h) for hp in range(CHUNKS) for h in range(2)]
        q2, db2, lse2, dd2, k2, v2 = {}, {}, {}, {}, {}, {}
        for hp in range(CHUNKS):
            ls = slice(hp * LANES, (hp + 1) * LANES)
            k2[hp], v2[hp] = k_ref[:, ls], v_ref[:, ls]
            q2[hp] = jnp.concatenate([qa_ref[:, ls], qb_ref[:, ls]], axis=0)
            db2[hp] = jnp.concatenate([dba_ref[:, ls], dbb_ref[:, ls]], axis=0)
            lse2[hp] = jnp.concatenate([la_ref[:, ls], lb_ref[:, ls]], axis=0)
            dd2[hp] = jnp.concatenate([da_ref[:, ls], dbd_ref[:, ls]], axis=0)
        km, scores, dps = {}, {}, {}
        for hp, h in heads:
            hm = _head_lane_mask(h)
            km[hp, h] = jnp.where(hm, k2[hp], jnp.zeros_like(k2[hp]))
            scores[hp, h] = _dot(q2[hp], km[hp, h], NT) * scale
            dps[hp, h] = _dot(db2[hp], jnp.where(hm, v2[hp], jnp.zeros_like(v2[hp])), NT)
        probs, dss = {}, {}
        for hp, h in heads:
            hm = _head_lane_mask(h)
            lse_col = jnp.max(jnp.where(hm, lse2[hp], NEG_INF), axis=1, keepdims=True)
            dd_col = jnp.max(jnp.where(hm, dd2[hp], NEG_INF), axis=1, keepdims=True)
            p = jnp.where(mask, jnp.exp(scores[hp, h] - lse_col), 0.0)
            probs[hp, h] = p.astype(BF16)
            dss[hp, h] = (p * (dps[hp, h] - dd_col) * scale).astype(BF16)
        for hp in range(CHUNKS):
            ls = slice(hp * LANES, (hp + 1) * LANES)
            dq_acc, dk_acc, dv_acc = None, None, None
            for h in range(2):
                hm = _head_lane_mask(h)
                dvc = _dot(probs[hp, h], jnp.where(hm, db2[hp], jnp.zeros_like(db2[hp])), TN)
                dqc = _dot(dss[hp, h], km[hp, h], NN)
                dkc = _dot(dss[hp, h], jnp.where(hm, q2[hp], jnp.zeros_like(q2[hp])), TN)
                dq_acc = dqc if dq_acc is None else dq_acc + dqc
                dk_acc = dkc if dk_acc is None else dk_acc + dkc
                dv_acc = dvc if dv_acc is None else dv_acc + dvc
            dq_ref[:, ls] = (dq_acc[:ATT_BLOCK] + carry[:, ls]).astype(BF16)
            carry[:, ls] = dq_acc[ATT_BLOCK:]
            dk_ref[:, ls] = dk_acc.astype(BF16)
            dv_ref[:, ls] = dv_acc.astype(BF16)

    def cur(cb):
        return pl.BlockSpec((None, ATT_BLOCK, B_WIDTH), lambda r, n: (r, n, cb))

    def nxt(cb):
        return pl.BlockSpec((None, ATT_BLOCK, B_WIDTH), lambda r, n: (r, jnp.minimum(n + 1, nb - 1), cb))

    out = cur(0)
    return _call(
        body, grid=(rate, nb),
        in_specs=[cur(q[1]), nxt(q[1]), cur(k[1]), cur(v[1]), cur(0), nxt(0), cur(0), nxt(0), cur(0), nxt(0)],
        out_specs=[out, out, out], out_shape=[SDS((rate, length, B_WIDTH), BF16)] * 3,
        scratch_shapes=[pltpu.VMEM((ATT_BLOCK, B_WIDTH), F32)],
        operands=[q[0], q[0], k[0], v[0], db, db, lse, lse, dd, dd], name=name, ride=ride)


AB_IN = 2 * A_WIDTH + 3 * N_DIL * B_WIDTH
ASM_TILE = 256


def _dproj_assemble(proj, d_a, dq, dk, dv, gains, tabs, ride=None):
    t = proj.shape[0]
    n_in = 3 * N_DIL

    def body(p_ref, da_ref, *rest):
        grads = rest[:n_in]
        g_ref, c_ref, s1_ref, s2_ref, o_ref, dg_ref = rest[n_in:n_in + 6]
        scratch = rest[n_in + 6:]

        @pl.when(pl.program_id(0) == 0)
        def _():
            dg_ref[...] = jnp.zeros_like(dg_ref)

        chunk = {}
        k_scr = 0
        for j in range(n_in):
            g = j % N_DIL
            if DIL_RATES[g] == 1:
                for ci in range(CHUNKS):
                    chunk[j, ci] = functools.partial(lambda r, ci: r[:, ci * LANES:(ci + 1) * LANES].astype(F32), grads[j], ci)
            else:
                scr = scratch[k_scr]
                k_scr += 1
                _regroup_in(grads[j], scr, DIL_RATES[g], ASM_TILE)
                for ci in range(CHUNKS):
                    chunk[j, ci] = functools.partial(lambda s, ci: s[ci], scr, ci)

        seg = _segment_mean_matrix(HEAD_DIM)
        c, s1, s2 = c_ref[...], s1_ref[...], s2_ref[...]
        o_ref[:, :2 * A_WIDTH] = da_ref[...]
        for jg in range(2 * N_DIL):
            for ci in range(CHUNKS):
                col = jg * B_WIDTH + ci * LANES
                src = slice(2 * A_WIDTH + col, 2 * A_WIDTH + col + LANES)
                xv = p_ref[:, src].astype(F32)
                r = lax.rsqrt(_segment_dot(xv * xv, seg) + EPS)
                xh = xv * r
                gain = g_ref[:, col:col + LANES]
                do = chunk[jg, ci]()
                dy = do * c + pltpu.roll(do * s1, 8, axis=1) + pltpu.roll(do * s2, LANES - 8, axis=1)
                dg_ref[:, col:col + LANES] += jnp.sum(dy * xh, axis=0, keepdims=True)
                dxh = dy * gain
                o_ref[:, src] = (r * (dxh - xh * _segment_dot(dxh * xh, seg))).astype(BF16)
        v0 = 2 * A_WIDTH + QK_COLS
        for g in range(N_DIL):
            for ci in range(CHUNKS):
                col = v0 + g * B_WIDTH + ci * LANES
                o_ref[:, col:col + LANES] = chunk[2 * N_DIL + g, ci]().astype(BF16)

    specs = [_row_spec(ASM_TILE, B_WIDTH) if r == 1 else _regrouped_spec(r, ASM_TILE) for r in DIL_RATES] * 3
    n_scr = 3 * sum(1 for r in DIL_RATES if r > 1)
    tab = _row_spec(ASM_TILE, LANES)
    return _call(
        body, grid=(t // ASM_TILE,),
        in_specs=[_row_spec(ASM_TILE, AB_IN), _row_spec(ASM_TILE, 2 * A_WIDTH)] + specs
        + [_const_spec((1, QK_COLS)), tab, tab, tab],
        out_specs=[_row_spec(ASM_TILE, AB_IN), _const_spec((1, QK_COLS))],
        out_shape=[SDS((t, AB_IN), BF16), SDS((1, QK_COLS), F32)],
        scratch_shapes=[pltpu.VMEM((CHUNKS, ASM_TILE, LANES), F32)] * n_scr,
        operands=[proj, d_a, *dq, *dk, *dv, gains, *tabs], name="dproj_assemble", ride=ride)


def _fold_heads(dg_lane):
    n = dg_lane.shape[1]

    def body(x_ref, o_ref):
        r = lax.broadcasted_iota(jnp.int32, (B_WIDTH, B_WIDTH), 0) % HEAD_DIM
        c = lax.broadcasted_iota(jnp.int32, (B_WIDTH, B_WIDTH), 1) % HEAD_DIM
        fold = jnp.where(r == c, 1.0, 0.0).astype(F32)
        for jg in range(n // B_WIDTH):
            ls = slice(jg * B_WIDTH, (jg + 1) * B_WIDTH)
            o_ref[:, ls] = _dot_hi(jnp.broadcast_to(x_ref[:, ls], (8, B_WIDTH)), fold)

    return pl.pallas_call(body, out_shape=SDS((8, n), F32), name="fold_heads", compiler_params=_params())(dg_lane)


CD_TILE = 256
TAP_ROWS = 64
CD_IN = 2 * C_WIDTH + 3 * 512


def _shifted_copies(src, dst, rows):
    dst[0, :rows] = src[...]
    for b in range(1, 8):
        dst[b, :rows - 8] = src[pl.ds(b, rows - 8), :]


def _rows_from(shifted, start, n, lanes=slice(None)):
    b = start % 8
    return shifted[b, pl.ds(start - b, n), lanes]


def _mixer_cd_fwd(proj, cw, cb, lg, lb, dw):
    t = proj.shape[0]
    per = CD_TILE // HALO

    def body(h_ref, m_ref, cw_ref, cb_ref, lg_ref, lb_ref, dw_ref, o_ref, c1_ref, c_scr, e_scr, c_sh):
        not_first = (pl.program_id(0) > 0).astype(F32)
        lanes = [slice(c * LANES, (c + 1) * LANES) for c in range(C_WIDTH // LANES)]

        def col(ref, part, ls):
            return ref[:, part * C_WIDTH + ls.start:part * C_WIDTH + ls.stop].astype(F32)

        for ls in lanes:
            c_scr[:HALO, ls] = col(h_ref, 0, ls) * _sigmoid(col(h_ref, 1, ls)) * not_first
            c_scr[HALO:, ls] = col(m_ref, 0, ls) * _sigmoid(col(m_ref, 1, ls))
            e_scr[:HALO, ls] = col(h_ref, 3, ls) * col(h_ref, 4, ls) * not_first
            e_scr[HALO:, ls] = col(m_ref, 3, ls) * col(m_ref, 4, ls)
        _shifted_copies(c_scr, c_sh, HALO + CD_TILE)
        for ls in lanes:
            for r0 in range(0, CD_TILE, TAP_ROWS):
                acc = jnp.zeros((TAP_ROWS, LANES), F32)
                for k in range(C_KERNEL):
                    acc = acc + cw_ref[k:k + 1, ls] * _rows_from(c_sh, r0 + HALO - (C_KERNEL - 1) + k, TAP_ROWS, ls)
                c1_ref[r0:r0 + TAP_ROWS, ls] = acc + cb_ref[:, ls]
        mean = sum(jnp.sum(c1_ref[:, ls], axis=-1, keepdims=True) for ls in lanes) * (1.0 / C_WIDTH)
        var = sum(jnp.sum((c1_ref[:, ls] - mean) ** 2, axis=-1, keepdims=True) for ls in lanes) * (1.0 / C_WIDTH)
        rs = lax.rsqrt(var + EPS)
        for ls in lanes:
            c2 = (c1_ref[:, ls] - mean) * rs * lg_ref[:, ls] + lb_ref[:, ls]
            o_ref[:, ls] = (c2 * _sigmoid(c2)).astype(BF16)
            d1 = jnp.zeros((CD_TILE, LANES), F32)
            for k in range(D_KERNEL):
                d1 = d1 + dw_ref[k:k + 1, ls] * e_scr[pl.ds(HALO - (D_KERNEL - 1) + k, CD_TILE), ls]
            o_ref[:, C_WIDTH + ls.start:C_WIDTH + ls.stop] = (col(m_ref, 2, ls) * d1).astype(BF16)

    return pl.pallas_call(
        body, grid=(t // CD_TILE,),
        in_specs=[pl.BlockSpec((HALO, CD_IN), lambda i: (jnp.maximum(i * per - 1, 0), 0)), _row_spec(CD_TILE, CD_IN),
                  _const_spec((32, C_WIDTH)), _const_spec((1, C_WIDTH)), _const_spec((1, C_WIDTH)), _const_spec((1, C_WIDTH)),
                  _const_spec((8, C_WIDTH))],
        out_specs=[_row_spec(CD_TILE, 2 * C_WIDTH), _row_spec(CD_TILE, C_WIDTH)],
        out_shape=[SDS((t, 2 * C_WIDTH), BF16), SDS((t, C_WIDTH), F32)],
        scratch_shapes=[pltpu.VMEM((HALO + CD_TILE, C_WIDTH), F32)] * 2 + [pltpu.VMEM((8, HALO + CD_TILE, C_WIDTH), F32)],
        name="mixer_cd_fwd", compiler_params=_params())(proj, proj, cw, cb, lg, lb, dw)


def _mixer_cd_bwd(proj, dcat, c1, cw, lg, lb, dw, ride=None):
    t = proj.shape[0]
    per = CD_TILE // HALO
    nt = t // CD_TILE
    ext = CD_TILE + HALO

    def body(hp_ref, m_ref, hn_ref, dm_ref, dn_ref, c1m_ref, c1n_ref, cw_ref, lg_ref, lb_ref, dw_ref,
             dp_ref, dcw_ref, dcb_ref, dlg_ref, dlb_ref, ddw_ref, c_scr, e_scr, dc1_scr, dd1_scr, c_sh, dc1_sh, dcw_acc,
             dvh_scr, vhat_scr):
        i = pl.program_id(0)

        @pl.when(i == 0)
        def _():
            for r in (dcw_acc, dcb_ref, dlg_ref, dlb_ref, ddw_ref):
                r[...] = jnp.zeros_like(r)

        not_first = (i > 0).astype(F32)
        not_last = (i < nt - 1).astype(F32)
        main = slice(HALO, HALO + CD_TILE)
        lanes = [slice(c * LANES, (c + 1) * LANES) for c in range(C_WIDTH // LANES)]

        def col(ref, part, ls):
            return ref[:, part * C_WIDTH + ls.start:part * C_WIDTH + ls.stop].astype(F32)

        for ls in lanes:
            c_scr[:HALO, ls] = col(hp_ref, 0, ls) * _sigmoid(col(hp_ref, 1, ls)) * not_first
            c_scr[main, ls] = col(m_ref, 0, ls) * _sigmoid(col(m_ref, 1, ls))
            c_scr[HALO + CD_TILE:, ls] = col(hn_ref, 0, ls) * _sigmoid(col(hn_ref, 1, ls)) * not_last
            e_scr[:HALO, ls] = col(hp_ref, 3, ls) * col(hp_ref, 4, ls) * not_first
            e_scr[main, ls] = col(m_ref, 3, ls) * col(m_ref, 4, ls)
            e_scr[HALO + CD_TILE:, ls] = col(hn_ref, 3, ls) * col(hn_ref, 4, ls) * not_last
        _shifted_copies(c_scr, c_sh, 2 * HALO + CD_TILE)

        def c1_of(ls):
            return jnp.concatenate([c1m_ref[:, ls], c1n_ref[:, ls]], axis=0)

        mean = sum(jnp.sum(c1_of(ls), axis=-1, keepdims=True) for ls in lanes) * (1.0 / C_WIDTH)
        var = sum(jnp.sum((c1_of(ls) - mean) ** 2, axis=-1, keepdims=True) for ls in lanes) * (1.0 / C_WIDTH)
        rs = lax.rsqrt(var + EPS)
        sum_dvh, sum_dvh_vhat = 0.0, 0.0
        for ls in lanes:
            vhat = (c1_of(ls) - mean) * rs
            c2 = vhat * lg_ref[:, ls] + lb_ref[:, ls]
            sig = _sigmoid(c2)
            dc = jnp.concatenate([dm_ref[:, ls], dn_ref[:, ls] * not_last], axis=0)
            dc2 = dc * (sig * (1.0 + c2 * (1.0 - sig)))
            dvh = dc2 * lg_ref[:, ls]
            sum_dvh = sum_dvh + jnp.sum(dvh, axis=-1, keepdims=True)
            sum_dvh_vhat = sum_dvh_vhat + jnp.sum(dvh * vhat, axis=-1, keepdims=True)
            dvh_scr[:, ls] = dvh
            vhat_scr[:, ls] = vhat
            dlg_ref[:, ls] += jnp.sum((dc2 * vhat)[:CD_TILE], axis=0, keepdims=True)
            dlb_ref[:, ls] += jnp.sum(dc2[:CD_TILE], axis=0, keepdims=True)
        for ls in lanes:
            dc1 = rs * (dvh_scr[:, ls] - sum_dvh * (1.0 / C_WIDTH) - vhat_scr[:, ls] * (sum_dvh_vhat * (1.0 / C_WIDTH)))
            dc1_scr[:, ls] = dc1
            dcb_ref[:, ls] += jnp.sum(dc1[:CD_TILE], axis=0, keepdims=True)
        _shifted_copies(dc1_scr, dc1_sh, ext)
        for ls in lanes:
            for r0 in range(0, CD_TILE, TAP_ROWS):
                rows = slice(r0, r0 + TAP_ROWS)
                dc1_m = dc1_scr[rows, ls]
                dc0 = jnp.zeros((TAP_ROWS, LANES), F32)
                for k in range(C_KERNEL):
                    dc0 = dc0 + cw_ref[k:k + 1, ls] * _rows_from(dc1_sh, r0 + C_KERNEL - 1 - k, TAP_ROWS, ls)
                    prod = dc1_m * _rows_from(c_sh, r0 + HALO - (C_KERNEL - 1) + k, TAP_ROWS, ls)
                    dcw_acc[k, :, ls] += prod.reshape(TAP_ROWS // 8, 8, LANES).sum(axis=0)
                g_m = m_ref[rows, C_WIDTH + ls.start:C_WIDTH + ls.stop].astype(F32)
                a_m = m_ref[rows, ls].astype(F32)
                sig_m = _sigmoid(g_m)
                dp_ref[rows, ls] = (dc0 * sig_m).astype(BF16)
                dp_ref[rows, C_WIDTH + ls.start:C_WIDTH + ls.stop] = (dc0 * a_m * sig_m * (1.0 - sig_m)).astype(BF16)

        @pl.when(i == nt - 1)
        def _():
            dcw_ref[...] = jnp.sum(dcw_acc[...], axis=1)

        for ls in lanes:
            wide = slice(C_WIDTH + ls.start, C_WIDTH + ls.stop)
            d1 = jnp.zeros((CD_TILE, LANES), F32)
            for k in range(D_KERNEL):
                d1 = d1 + dw_ref[k:k + 1, ls] * e_scr[pl.ds(HALO - (D_KERNEL - 1) + k, CD_TILE), ls]
            dd_m = dm_ref[:, wide]
            dd1 = jnp.concatenate([dd_m * col(m_ref, 2, ls), dn_ref[:, wide] * col(hn_ref, 2, ls) * not_last], axis=0)
            dd1_scr[:, ls] = dd1
            dp_ref[:, 2 * C_WIDTH + ls.start:2 * C_WIDTH + ls.stop] = (dd_m * d1).astype(BF16)
            de = jnp.zeros((CD_TILE, LANES), F32)
            for k in range(D_KERNEL):
                de = de + dw_ref[k:k + 1, ls] * dd1_scr[pl.ds(D_KERNEL - 1 - k, CD_TILE), ls]
                ddw_ref[k:k + 1, ls] += jnp.sum(dd1[:CD_TILE] * e_scr[pl.ds(HALO - (D_KERNEL - 1) + k, CD_TILE), ls], axis=0, keepdims=True)
            dp_ref[:, 3 * C_WIDTH + ls.start:3 * C_WIDTH + ls.stop] = (de * col(m_ref, 4, ls)).astype(BF16)
            dp_ref[:, 4 * C_WIDTH + ls.start:4 * C_WIDTH + ls.stop] = (de * col(m_ref, 3, ls)).astype(BF16)

    halo_prev = lambda i: (jnp.maximum(i * per - 1, 0), 0)
    halo_next = lambda i: (jnp.minimum((i + 1) * per, t // HALO - 1), 0)
    vec = _const_spec((1, C_WIDTH))
    return _call(
        body, grid=(nt,),
        in_specs=[pl.BlockSpec((HALO, CD_IN), halo_prev), _row_spec(CD_TILE, CD_IN), pl.BlockSpec((HALO, CD_IN), halo_next),
                  _row_spec(CD_TILE, 2 * C_WIDTH), pl.BlockSpec((HALO, 2 * C_WIDTH), halo_next),
                  _row_spec(CD_TILE, C_WIDTH), pl.BlockSpec((HALO, C_WIDTH), halo_next),
                  _const_spec((32, C_WIDTH)), vec, vec, _const_spec((8, C_WIDTH))],
        out_specs=[_row_spec(CD_TILE, CD_IN), _const_spec((32, C_WIDTH)), vec, vec, vec, _const_spec((8, C_WIDTH))],
        out_shape=[SDS((t, CD_IN), BF16), SDS((32, C_WIDTH), F32), SDS((1, C_WIDTH), F32), SDS((1, C_WIDTH), F32),
                   SDS((1, C_WIDTH), F32), SDS((8, C_WIDTH), F32)],
        scratch_shapes=[pltpu.VMEM((2 * HALO + CD_TILE, C_WIDTH), F32)] * 2 + [pltpu.VMEM((ext, C_WIDTH), F32)] * 2
        + [pltpu.VMEM((8, 2 * HALO + CD_TILE, C_WIDTH), F32), pltpu.VMEM((8, ext, C_WIDTH), F32),
           pltpu.VMEM((32, 8, C_WIDTH), F32)] + [pltpu.VMEM((ext, C_WIDTH), F32)] * 2,
        operands=[proj, proj, proj, dcat, dcat, c1, c1, cw, lg, lb, dw], name="mixer_cd_bwd", ride=ride)


def _wgrad(name, pairs, out_rc, t, ride):
    tk = TILES["wgrad"]
    assert tk == t, "the whole contraction has to fit one grid step"
    r, c = out_rc
    n = len(pairs)

    operands, in_specs, where = [], [], []
    for lhs, lhs_spec, rhs, rhs_spec in pairs:
        at = []
        for array, spec in ((lhs, lhs_spec), (rhs, rhs_spec)):
            seen = [k for k, o in enumerate(operands) if o is array]
            if not seen:
                operands.append(array)
                in_specs.append(spec)
            at.append(seen[0] if seen else len(operands) - 1)
        where.append(at)
    n_in = len(operands)

    def body(*refs):
        ins, out_refs = refs[:n_in], refs[n_in:]
        for j, (lhs_at, rhs_at) in enumerate(where):
            out_refs[j][...] = _dot(ins[lhs_at][...], ins[rhs_at][...], TN).astype(BF16)

    res = _call(body, grid=(N_CHIPS, t // tk), in_specs=in_specs,
                out_specs=[pl.BlockSpec((None, r, c), lambda p, k: (p, 0, 0))] * n,
                out_shape=[SDS((N_CHIPS, r, c), BF16)] * n, operands=operands, name=name, ride=ride)
    outs, ride_res = (res, None) if ride is None else res
    outs = [o.reshape(N_CHIPS, 2, r // 2, c) for o in outs]
    return outs if ride is None else (outs, ride_res)


def _wgrad_col_sharded(name, h, dz_list, three_d, ride=None):
    t, d = h.shape
    tk = TILES["wgrad"]
    n4 = dz_list[0].shape[-1] if three_d else dz_list[0].shape[-1] // N_CHIPS
    hs = pl.BlockSpec((tk, d), lambda p, k: (k, 0))
    zs = pl.BlockSpec((None, tk, n4), lambda p, k: (p, k, 0)) if three_d else pl.BlockSpec((tk, n4), lambda p, k: (k, p))
    return _wgrad(name, [(h, hs, dz, zs) for dz in dz_list], (d, n4), t, ride)


def _wgrad_row_sharded(name, a, g, three_d, ride=None):
    many = isinstance(a, (list, tuple))
    a_list = list(a) if many else [a]
    t, d = g.shape
    tk = TILES["wgrad"]
    k4 = a_list[0].shape[-1] if three_d else a_list[0].shape[-1] // N_CHIPS
    a_spec = pl.BlockSpec((None, tk, k4), lambda p, k: (p, k, 0)) if three_d else pl.BlockSpec((tk, k4), lambda p, k: (k, p))
    gs = pl.BlockSpec((tk, d), lambda p, k: (k, 0))
    res = _wgrad(name, [(a_j, a_spec, g, gs) for a_j in a_list], (k4, d), t, ride)
    if many:
        return res
    return res[0] if ride is None else (res[0][0], res[1])


def _mesh_scalars():
    return jnp.stack([lax.axis_index("c"), 2 * lax.axis_index("x") + lax.axis_index("y")]).astype(jnp.int32)


def _stage_own(name, w, layer, dtype, far_slab=False):
    layers, r, cols = w.shape
    h = r // 2

    def body(s_ref, x_ref, o_ref, *unwritten):
        o_ref[...] = x_ref[...].astype(dtype)

    out_specs = [pl.BlockSpec((None, None, h, cols), lambda i, s: (s[1], i, 0, 0))] + [ANY] * far_slab
    out_shape = [SDS((N_CHIPS, 2, h, cols), dtype)] + [SDS((2, h, cols), dtype)] * far_slab
    res = pl.pallas_call(
        body,
        grid_spec=pltpu.PrefetchScalarGridSpec(
            num_scalar_prefetch=1, grid=(2,),
            in_specs=[pl.BlockSpec((None, h, cols), lambda i, s: (2 * layer + i, 0, 0))], out_specs=out_specs),
        out_shape=out_shape, name=name,
        compiler_params=_params())(_mesh_scalars(), w.reshape(2 * layers, h, cols))
    return tuple(res) if far_slab else res[0]


STAGE_STEPS = 4


def _stage_rest_and_norm(x, g, weights, ride=None):
    t, d = x.shape
    n = len(weights)
    views, in_specs, out_specs, out_shapes = [], [], [], []
    for w, layer in weights:
        layers, r, cols = w.shape
        sub = r // STAGE_STEPS
        views.append(w.reshape(layers * STAGE_STEPS, sub, cols))
        in_specs.append(pl.BlockSpec((None, sub, cols), functools.partial(lambda l, i, s: (STAGE_STEPS * l + i, 0, 0), layer)))
        out_specs.append(pl.BlockSpec((None, None, sub, cols), lambda i, s: (s[1], i // 2, i % 2, 0)))
        out_shapes.append(SDS((N_CHIPS, 2, r // 2, cols), BF16))

    def body(s_ref, x_ref, g_ref, *rest):
        w_refs, h_ref, o_refs = rest[:n], rest[n], rest[n + 1:]
        h_ref[...] = _rms_rows(x_ref[...], g_ref[...]).astype(BF16)
        for w_ref, o_ref in zip(w_refs, o_refs):
            o_ref[...] = w_ref[...].astype(BF16)

    tm = t // STAGE_STEPS
    res = _call(
        body, grid=(STAGE_STEPS,), in_specs=[pl.BlockSpec((tm, d), lambda i, s: (i, 0)), pl.BlockSpec((1, d), lambda i, s: (0, 0))] + in_specs,
        out_specs=[pl.BlockSpec((tm, d), lambda i, s: (i, 0))] + out_specs, out_shape=[SDS((t, d), BF16)] + out_shapes,
        operands=[x, g] + views, name="stage_and_norm", ride=ride, prefetch=_mesh_scalars())
    outs, ride_res = (res, None) if ride is None else res
    result = (outs[0], list(outs[1:]))
    return result if ride is None else (result, ride_res)


def _remote(src, dst, send_sem, recv_sem, device):
    return pltpu.make_async_remote_copy(src, dst, send_sem, recv_sem, device_id=device, device_id_type=MESH)


ALL_PEERS = (0, 1, 2)
NEIGHBOURS = (0, 1)


def _ride_gather_send(bufs, peers=ALL_PEERS):
    n = len(bufs)

    def each(b, sems, act):
        send, recv = sems
        x, y, c, p, others = _position()
        for t in range(n):
            for j in peers:
                qx, qy = others[j]
                act(b[t].at[p, c], b[t].at[2 * qx + qy, c], send.at[t, j], recv.at[t, j], (qx, qy, c))

    def start(ins, b, new, sems):
        each(b, sems, lambda mine, landed, s, r, dev: _remote(mine, mine, s, r, dev).start())

    def finish(ins, b, new, sems):
        def act(mine, landed, s, r, dev):
            _remote(mine, mine, s, r, dev).wait_send()
            _remote(landed, landed, s, r, dev).wait_recv()
        each(b, sems, act)

    return _Ride([], bufs, [], [(n, 3), (n, 3)], start, finish, ["chips"])


def _ride_gather_pass(bufs, peers=ALL_PEERS):
    n = len(bufs)

    def each(b, sems, act):
        send, recv = sems
        x, y, c, p, others = _position()
        for t in range(n):
            for j in peers:
                qx, qy = others[j]
                act(b[t].at[2 * qx + qy, c], b[t].at[2 * qx + qy, 1 - c], send.at[t, j], recv.at[t, j], (x, y, 1 - c))

    def start(ins, b, new, sems):
        each(b, sems, lambda landed, passed, s, r, dev: _remote(landed, landed, s, r, dev).start())

    def finish(ins, b, new, sems):
        def act(landed, passed, s, r, dev):
            _remote(landed, landed, s, r, dev).wait_send()
            _remote(passed, passed, s, r, dev).wait_recv()
        each(b, sems, act)

    return _Ride([], bufs, [], [(n, 3), (n, 3)], start, finish, ["sibling"])


def _ride_gather(bufs, peers=ALL_PEERS):
    send, onward = _ride_gather_send(bufs, peers), _ride_gather_pass(bufs, peers)
    n_send = len(send.sem_shapes)

    def start(ins, b, new, sems):
        send.start(ins, b, new, sems[:n_send])

    def finish(ins, b, new, sems):
        send.finish(ins, b, new, sems[:n_send])
        onward.start(ins, b, new, sems[n_send:])
        onward.finish(ins, b, new, sems[n_send:])

    return _Ride([], bufs, [], send.sem_shapes + onward.sem_shapes, start, finish, ["sibling", "chips"])


def _ride_gather_far(sources, slabs):
    n = len(slabs)

    def hops(ins, b, sems):
        x, y, c, p, others = _position()
        qx, qy = others[2]
        for t in range(n):
            far = (ins[t].at[p, c], b[t].at[c], sems[0].at[t], sems[1].at[t], (qx, qy, c))
            onward = (b[t].at[c], b[t].at[1 - c], sems[2].at[t], sems[3].at[t], (x, y, 1 - c))
            yield far, onward

    def wait(mine, landed, s, r, dev):
        _remote(mine, mine, s, r, dev).wait_send()
        _remote(landed, landed, s, r, dev).wait_recv()

    def start(ins, b, new, sems):
        for (mine, landed, s, r, dev), _ in hops(ins, b, sems):
            _remote(mine, landed, s, r, dev).start()

    def finish(ins, b, new, sems):
        for far, _ in hops(ins, b, sems):
            wait(*far)
        for _, (landed, passed, s, r, dev) in hops(ins, b, sems):
            _remote(landed, landed, s, r, dev).start()
        for _, onward in hops(ins, b, sems):
            wait(*onward)

    return _Ride(sources, slabs, [], [(n,)] * 4, start, finish, ["sibling", "chips"])


def _ride_swap(tensors):
    n = len(tensors)

    def each(ins, new, sems, act):
        send, recv = sems
        x, y, c, _, _ = _position()
        for t in range(n):
            act(_remote(ins[t].at[:, 1 - c], new[t], send.at[t], recv.at[t], (x, y, 1 - c)))

    def start(ins, b, new, sems):
        each(ins, new, sems, lambda cp: cp.start())

    def finish(ins, b, new, sems):
        each(ins, new, sems, lambda cp: cp.wait())

    return _Ride(tensors, [], [SDS((s.shape[0],) + s.shape[2:], s.dtype) for s in tensors], [(n,), (n,)], start, finish,
                 ["sibling"])


def _ride_scatter(tensors, landing):
    n = len(tensors)

    def each(ins, b, sems, act):
        send, recv = sems
        x, y, c, p, others = _position()
        for t in range(n):
            for j, (qx, qy) in enumerate(others):
                q = 2 * qx + qy
                act(ins[t].at[q], b[t].at[p], b[t].at[q], send.at[t, j], recv.at[t, j], (qx, qy, c))

    def start(ins, b, new, sems):
        each(ins, b, sems, lambda src, dst, landed, s, r, dev: _remote(src, dst, s, r, dev).start())

    def finish(ins, b, new, sems):
        def act(src, dst, landed, s, r, dev):
            _remote(src, dst, s, r, dev).wait_send()
            _remote(landed, landed, s, r, dev).wait_recv()
        each(ins, b, sems, act)

    return _Ride(tensors, landing, [], [(n, 3), (n, 3)], start, finish, ["chips"])


def _ride_join(bufs):
    n = len(bufs)

    def each(b, sems, act):
        send, recv = sems
        x, y, c, _, _ = _position()
        for t in range(n):
            act(b[t].at[c], b[t].at[1 - c], send.at[t], recv.at[t], (x, y, 1 - c))

    def start(ins, b, new, sems):
        each(b, sems, lambda mine, theirs, s, r, dev: _remote(mine, mine, s, r, dev).start())

    def finish(ins, b, new, sems):
        def act(mine, theirs, s, r, dev):
            _remote(mine, mine, s, r, dev).wait_send()
            _remote(theirs, theirs, s, r, dev).wait_recv()
        each(b, sems, act)

    return _Ride([], bufs, [], [(n,), (n,)], start, finish, ["sibling"])


def _all_reduce_small(pack, ride=None):
    rows = pack.shape[0]
    n_dev = 2 * N_CHIPS
    n_rb = 0 if ride is None else len(ride.bufs)

    def body(x_ref, *rest):
        o_ref = rest[n_rb]
        r_bufs = rest[n_rb + 1:2 * n_rb + 1]
        land, send, recv = rest[2 * n_rb + 1:2 * n_rb + 4]
        r_sems = rest[2 * n_rb + 4:]
        if ride is not None:
            ride.start([], r_bufs, [], r_sems)
        x, y, c, p, _ = _position()
        me = 2 * p + c
        land[me] = x_ref[...]
        peers = [(dx, dy, dc) for dx in range(2) for dy in range(2) for dc in range(2) if (dx, dy, dc) != (0, 0, 0)]
        for j, (dx, dy, dc) in enumerate(peers):
            _remote(land.at[me], land.at[me], send.at[j], recv.at[j], (x ^ dx, y ^ dy, c ^ dc)).start()
        for j, (dx, dy, dc) in enumerate(peers):
            src = 4 * (x ^ dx) + 2 * (y ^ dy) + (c ^ dc)
            _remote(land.at[me], land.at[me], send.at[j], recv.at[j], (x ^ dx, y ^ dy, c ^ dc)).wait_send()
            _remote(land.at[src], land.at[src], send.at[j], recv.at[j], (x ^ dx, y ^ dy, c ^ dc)).wait_recv()
        acc = land[0]
        for dev in range(1, n_dev):
            acc = acc + land[dev]
        o_ref[...] = acc
        if ride is not None:
            ride.finish([], r_bufs, [], r_sems)

    bufs = [] if ride is None else ride.bufs
    sems = [] if ride is None else [pltpu.SemaphoreType.DMA(s) for s in ride.sem_shapes]
    res = pl.pallas_call(
        body, in_specs=[pl.BlockSpec(memory_space=pltpu.VMEM)] + [ANY] * n_rb,
        out_specs=[pl.BlockSpec(memory_space=pltpu.VMEM)] + [ANY] * n_rb,
        out_shape=[SDS((rows, LANES), F32)] + [SDS(b.shape, b.dtype) for b in bufs],
        scratch_shapes=[pltpu.VMEM((n_dev, rows, LANES), F32), pltpu.SemaphoreType.DMA((n_dev - 1,)),
                        pltpu.SemaphoreType.DMA((n_dev - 1,))] + sems,
        input_output_aliases={1 + j: 1 + j for j in range(n_rb)},
        name="all_reduce_small", compiler_params=_params())(pack, *bufs)
    return res[0], list(res[1:])


def _add_own_half(name, fulls, recvs, out_dtypes):
    n = len(fulls)
    in_specs, out_specs, out_shapes = [], [], []
    for full, dtype in zip(fulls, out_dtypes):
        n4, _, h, cols = full.shape
        in_specs += [pl.BlockSpec((None, None, h, cols), lambda q, s: (q, s[0], 0, 0)),
                     pl.BlockSpec((None, h, cols), lambda q, s: (q, 0, 0))]
        out_specs += [pl.BlockSpec((None, h, cols), lambda q, s: (q, 0, 0)),
                      pl.BlockSpec((None, h, cols), lambda q, s: (s[1], 0, 0))]
        out_shapes += [SDS((n4, h, cols), dtype)] * 2

    def body(s_ref, *refs):
        ins, outs = refs[:2 * n], refs[2 * n:]
        for j in range(n):
            o_ref, own_ref = outs[2 * j], outs[2 * j + 1]
            v = (ins[2 * j][...].astype(F32) + ins[2 * j + 1][...].astype(F32)).astype(o_ref.dtype)
            o_ref[...] = v

            @pl.when(pl.program_id(0) == s_ref[1])
            def _():
                own_ref[...] = v

    res = pl.pallas_call(
        body,
        grid_spec=pltpu.PrefetchScalarGridSpec(num_scalar_prefetch=1, grid=(N_CHIPS,), in_specs=in_specs, out_specs=out_specs),
        out_shape=out_shapes, name=name, compiler_params=_params())(
            _mesh_scalars(), *[a for pair in zip(fulls, recvs) for a in pair])
    return [(res[2 * j], res[2 * j + 1]) for j in range(n)]


def _sum_chips(name, parts_list):
    steps = 4 if all(parts.shape[1] % 64 == 0 for parts in parts_list) else 1
    in_specs, out_specs, out_shapes = [], [], []
    for parts in parts_list:
        n4, h, cols = parts.shape
        in_specs.append(pl.BlockSpec((n4, h // steps, cols), lambda i, s: (0, i, 0)))
        out_specs.append(pl.BlockSpec((None, h // steps, cols), lambda i, s: (s[0], i, 0)))
        out_shapes.append(SDS((2, h, cols), F32))
    n = len(parts_list)

    def body(s_ref, *refs):
        for a_ref, o_ref in zip(refs[:n], refs[n:]):
            acc = a_ref[0].astype(F32)
            for q in range(1, N_CHIPS):
                acc = acc + a_ref[q].astype(F32)
            o_ref[...] = acc

    return pl.pallas_call(
        body,
        grid_spec=pltpu.PrefetchScalarGridSpec(num_scalar_prefetch=1, grid=(steps,), in_specs=in_specs, out_specs=out_specs),
        out_shape=out_shapes, name=name, compiler_params=_params())(_mesh_scalars(), *parts_list)


def _adamw_math(w, g, m, v):
    m2 = ADAM_B1 * m + (1.0 - ADAM_B1) * g
    v2 = ADAM_B2 * v + (1.0 - ADAM_B2) * (g * g)
    m_hat = m2 / (1.0 - ADAM_B1 ** ADAM_STEP)
    v_hat = v2 / (1.0 - ADAM_B2 ** ADAM_STEP)
    delta = -ADAM_LR * (m_hat / (jnp.sqrt(v_hat) + ADAM_EPS) + ADAM_WD * w)
    return delta, m2, v2


ADAMW_STEPS = 8


def _adamw_big(name, ws, g_layers_list, ms, vs):
    layers = ws[0].shape[0]
    n, per_in = len(ws), 3 + layers
    in_specs, out_specs, out_shapes, operands = [], [], [], []
    for w, g_layers, m, v in zip(ws, g_layers_list, ms, vs):
        assert w.shape[0] == layers and w.shape[1] % (8 * ADAMW_STEPS) == 0
        _, rows, cols = w.shape
        tr = rows // ADAMW_STEPS
        blk = pl.BlockSpec((None, tr, cols), lambda l, i: (l, i, 0))
        in_specs += [blk] * 3 + [pl.BlockSpec((tr, cols), lambda l, i: (i, 0))] * layers
        out_specs += [blk] * 4
        out_shapes += [SDS((layers, rows, cols), F32)] * 4
        operands += [w, m, v] + [g.reshape(rows, cols) for g in g_layers]

    def body(*refs):
        ins, outs = refs[:n * per_in], refs[n * per_in:]
        for j in range(n):
            w_ref, m_ref, v_ref = ins[j * per_in:j * per_in + 3]
            g_refs = ins[j * per_in + 3:(j + 1) * per_in]
            g_o, d_o, m_o, v_o = outs[4 * j:4 * j + 4]
            gv = g_refs[0][...]
            for layer in range(1, layers):
                gv = jnp.where(pl.program_id(0) == layer, g_refs[layer][...], gv)
            d, mm, vv = _adamw_math(w_ref[...], gv, m_ref[...], v_ref[...])
            g_o[...] = gv
            d_o[...] = d
            m_o[...] = mm
            v_o[...] = vv

    res = pl.pallas_call(
        body, grid=(layers, ADAMW_STEPS), in_specs=in_specs, out_specs=out_specs, out_shape=out_shapes, name=name,
        compiler_params=_params())(*operands)
    return [tuple(res[4 * j:4 * j + 4]) for j in range(n)]


def _adamw_small(ws, gs, ms, vs):
    n = len(ws)
    flat = []
    for group in (ws, gs, ms, vs):
        flat += [a.reshape(-1, a.shape[-1]) for a in group]

    def body(*refs):
        w_r, g_r, m_r, v_r = refs[:n], refs[n:2 * n], refs[2 * n:3 * n], refs[3 * n:4 * n]
        d_o, m_o, v_o = refs[4 * n:5 * n], refs[5 * n:6 * n], refs[6 * n:7 * n]
        for j in range(n):
            d, mm, vv = _adamw_math(w_r[j][...], g_r[j][...], m_r[j][...], v_r[j][...])
            d_o[j][...] = d
            m_o[j][...] = mm
            v_o[j][...] = vv

    shapes = [SDS(a.shape, F32) for a in flat[:n]]
    outs = pl.pallas_call(body, out_shape=shapes * 3, name="adamw_small", compiler_params=_params())(*flat)
    res = []
    for k in range(3):
        res.append([outs[k * n + j].reshape(ws[j].shape) for j in range(n)])
    return res


BIG = ("ab_w_in", "ab_w_out", "cd_w_in", "cd_w_out", "ffn_w_gate", "ffn_w_up", "ffn_w_down")
BIG_BY_LAYERS = (BIG[:4], BIG[4:])
V_BLOCK = (2 * A_WIDTH + QK_COLS) // B_WIDTH


def _pad_rows(a, rows):
    return jnp.pad(a, ((0, rows - a.shape[0]), (0, 0)))


A_IN, A_OUT, C_IN, C_OUT = ("ab_w_in", 0), ("ab_w_out", 0), ("cd_w_in", 0), ("cd_w_out", 0)
G0, U0, D0 = ("ffn_w_gate", 0), ("ffn_w_up", 0), ("ffn_w_down", 0)
G1, U1, D1 = ("ffn_w_gate", 1), ("ffn_w_up", 1), ("ffn_w_down", 1)
UNITS = (A_IN, A_OUT, G0, U0, D0, C_IN, C_OUT, G1, U1, D1)
ROWS_MINOR = ("ffn_w_gate", "ffn_w_up")
SMALL_SHARDED = ("small", 0)
REPLICATED_UNIT = ("replicated", 0)


class _Exchange:
    def __init__(self, enabled):
        self.enabled = enabled
        self.w, self.grad, self.recv, self.half, self.land, self.done = {}, {}, {}, {}, {}, {}
        self.far = {}

    def full(self, unit):
        b = self.w[unit]
        return b.reshape(N_CHIPS, 1, 2 * b.shape[2], b.shape[3])

    def ride_for(self, phases):
        rides, sinks = [], []
        for kind, units in phases:
            if kind == "send":
                rides.append(_ride_gather_send([self.w[u] for u in units]))
                sinks.append(self.w)
            elif kind == "pass":
                rides.append(_ride_gather_pass([self.w[u] for u in units]))
                sinks.append(self.w)
            elif kind == "gather":
                rides.append(_ride_gather([self.w[u] for u in units]))
                sinks.append(self.w)
            elif kind == "gather_near":
                rides.append(_ride_gather([self.w[u] for u in units], NEIGHBOURS))
                sinks.append(self.w)
            elif kind == "gather_far":
                rides.append(_ride_gather_far([self.w[u] for u in units], [self.far[u] for u in units]))
                sinks.append(self.far)
            elif kind == "swap":
                rides.append(_ride_swap([self.grad[u] for u in units]))
                sinks.append(self.recv)
            elif kind == "scatter":
                rides.append(_ride_scatter([self.half[u] for u in units], [self.land[u] for u in units]))
                sinks.append(self.land)
            else:
                rides.append(_ride_join([self.done[u] for u in units]))
                sinks.append(self.done)
        ride = functools.reduce(_ride_both, rides)

        def settle(res):
            n_bufs = sum(len(r.bufs) for r in rides)
            bufs, new = list(res[:n_bufs]), list(res[n_bufs:])
            for r, sink, (_, units) in zip(rides, sinks, phases):
                vals = [bufs.pop(0) for _ in r.bufs] + [new.pop(0) for _ in r.new_outs]
                for u, v in zip(units, vals):
                    sink[u] = v

        return ride, settle

    def run(self, fn, *args, phases=(), **kw):
        if not self.enabled or not phases:
            return fn(*args, **kw)
        ride, settle = self.ride_for(phases)
        out, res = fn(*args, ride=ride, **kw)
        settle(res)
        return out

    def alone(self, name, phases):
        if self.enabled:
            ride, settle = self.ride_for(phases)
            settle(_run_ride(name, ride))

    def pair_sum(self, units):
        if self.enabled:
            dtypes = [F32 if u in (SMALL_SHARDED, REPLICATED_UNIT) else BF16 for u in units]
            res = _add_own_half(f"pair_sum_{units[0][0]}_{units[0][1]}", [self.grad[u] for u in units],
                                [self.recv[u] for u in units], dtypes)
            for u, (half, land) in zip(units, res):
                self.half[u], self.land[u] = half, land

    def chip_sum(self, units):
        if self.enabled:
            res = _sum_chips(f"chip_sum_{units[0][0]}_{units[0][1]}", [self.land[u] for u in units])
            self.done.update(zip(units, res))


def _local_step(x, target, ex, sp, h0=None):
    t, d = x.shape
    tabs = _rope_tables(t)
    gains = jnp.concatenate([jnp.tile(sp["q_norm_g"][g], HEAD_DIM // 8) for g in range(N_DIL)]
                            + [jnp.tile(sp["k_norm_g"][g], HEAD_DIM // 8) for g in range(N_DIL)]).reshape(1, QK_COLS)
    bias_t = sp["sgu_bias"].T
    cw = _pad_rows(sp["conv_c_w"], 32)
    dw = _pad_rows(sp["conv_d_w"], 8)
    cb, clg, clb = (sp[k].reshape(1, C_WIDTH) for k in ("conv_c_b", "c_ln_g", "c_ln_b"))
    slg, slb = sp["sgu_norm_g"].reshape(1, A_WIDTH), sp["sgu_norm_b"].reshape(1, A_WIDTH)
    g_ab, g_cd = sp["ab_norm_g"].reshape(1, d), sp["cd_norm_g"].reshape(1, d)
    g_f0, g_f1 = sp["ffn_norm_g"][0:1], sp["ffn_norm_g"][1:2]
    run = ex.run

    def w2d(unit):
        return ex.full(unit).reshape(-1, d)

    if h0 is None:
        h0 = _rms_fwd("rms_ab", x, g_ab)
    if ex.enabled:
        proj = run(_proj_in_near, "proj_ab_near", h0, ex.full(A_IN), phases=[("gather_far", [A_IN]), ("send", [A_OUT])])
        proj, ex.w[A_IN] = _proj_in_far("proj_ab_far", h0, ex.far[A_IN], proj, ex.w[A_IN])
    else:
        proj = _proj_in("proj_ab", h0, ex.full(A_IN), 0)
    a_out = _mixer_a_fwd(proj, slg, slb, sp["sgu_w"], bias_t)
    qk, q1, q2, k1, k2 = run(_qk_fwd, proj, gains, tabs, phases=[("pass", [A_OUT]), ("send", [G0, C_OUT])])
    regrouped_qk = {1: (q1, k1), 2: (q2, k2)}
    fwd_phases = ([("pass", [G0, C_OUT]), ("send", [U0])], [("pass", [U0]), ("send", [D0])],
                  [("pass", [D0]), ("send", [C_IN])])
    qkv, o_list, l_list = [], [], []
    for g, rate in enumerate(DIL_RATES):
        if rate == 1:
            qk3, proj3 = qk.reshape(1, t, QK_COLS), proj.reshape(1, t, AB_IN)
            q, k, v = (qk3, g), (qk3, N_DIL + g), (proj3, V_BLOCK + g)
        else:
            vp, = _permute(f"regroup_v_{g}", [(proj, V_BLOCK + g)], rate)
            q, k, v = (regrouped_qk[g][0], 0), (regrouped_qk[g][1], 0), (vp, 0)
        qkv.append((q, k, v))
        o, l = run(_attn_fwd, f"attn_fwd_{g}", q, k, v, phases=fwd_phases[g])
        if rate == 1:
            o, l = o.reshape(t, B_WIDTH), l.reshape(t, B_WIDTH)
        o_list.append(o)
        l_list.append(l)
    cat, lse_tot, lse_1, lse_2 = _attn_merge(a_out, o_list, l_list)
    x1, hf0 = _proj_out("out_ab", cat, w2d(A_OUT), x, g_next=g_f0)
    fgate0, fup0, act0 = run(_ffn_in, "ffn_in_0", hf0, ex.full(G0), ex.full(U0), 0,
                           phases=[("pass", [C_IN]), ("send", [D1, G1])])
    x2, h1 = run(_ffn_out, "ffn_out_0", act0, ex.full(D0), 0, x1, g_next=g_cd, phases=[("pass", [D1, G1]), ("send", [U1])])
    projcd = run(_proj_in, "proj_cd", h1, ex.full(C_IN), 0, phases=[("pass", [U1])])
    cat2, c1 = _mixer_cd_fwd(projcd, cw, cb, clg, clb, dw)
    x3, hf1 = _proj_out("out_cd", cat2, w2d(C_OUT), x2, g_next=g_f1)
    fgate1, fup1, act1 = _ffn_in("ffn_in_1", hf1, ex.full(G1), ex.full(U1), 0)
    dy, loss_acc, dy_b = _ffn_out("ffn_out_1", act1, ex.full(D1), 0, x3, target=target)
    loss = 0.5 * loss_acc[0, 0] / d

    late = [D1, G1, U1]
    dgate, dup = _ffn_dact("ffn_dact_1", dy_b, ex.full(D1), 0, fgate1, fup1)
    ex.grad[D1] = _wgrad_row_sharded("wgrad_down_1", act1, dy_b, True)
    ex.grad[G1], ex.grad[U1] = _wgrad_row_sharded("wgrad_gate_up_1", [dgate, dup], hf1, True)
    g3, d_f1, g3_b = run(_dgrad_cols, "dgrad_ffn_1", [dgate, dup], [ex.full(G1), ex.full(U1)], 0, True, x3, g_f1, dy,
                         w_rows=True, phases=[("swap", late)])
    ex.pair_sum(late)

    dcat2 = _dgrad_rows("dgrad_out_cd", g3_b, w2d(C_OUT))
    ex.grad[C_OUT] = _wgrad_row_sharded("wgrad_out_cd", cat2, g3_b, False)
    dprojcd, d_cw, d_cb, d_clg, d_clb, d_dw = run(_mixer_cd_bwd, projcd, dcat2, c1, cw, clg, clb, dw, phases=[("scatter", late)])
    ex.chip_sum(late)
    ex.grad[C_IN] = _wgrad_col_sharded("wgrad_in_cd", h1, [dprojcd], False)[0]
    g2, d_cdn, g2_b = run(_dgrad_cols, "dgrad_in_cd", [dprojcd], [ex.full(C_IN)], 0, False, x2, g_cd, g3,
                          phases=[("join", late), ("swap", [C_OUT, C_IN])])
    ex.pair_sum([C_OUT, C_IN])

    dgate, dup = run(_ffn_dact, "ffn_dact_0", g2_b, ex.full(D0), 0, fgate0, fup0, phases=[("scatter", [C_OUT, C_IN])])
    ex.chip_sum([C_OUT, C_IN])
    ex.grad[D0] = _wgrad_row_sharded("wgrad_down_0", act0, g2_b, True)
    ex.grad[G0], ex.grad[U0] = _wgrad_row_sharded("wgrad_gate_up_0", [dgate, dup], hf0, True)
    small = {"cd_norm_g": d_cdn, "conv_c_w": d_cw[:C_KERNEL], "conv_c_b": d_cb, "c_ln_g": d_clg, "c_ln_b": d_clb,
             "conv_d_w": d_dw[:D_KERNEL]}
    ex.grad[SMALL_SHARDED] = _split_full_small(small).reshape(N_CHIPS, 2, SHARDED_ROWS // 2, LANES)
    mid = [D0, G0, U0, SMALL_SHARDED]
    g1, d_f0, g1_b = run(_dgrad_cols, "dgrad_ffn_0", [dgate, dup], [ex.full(G0), ex.full(U0)], 0, True, x1, g_f0, g2,
                         w_rows=True, phases=[("join", [C_OUT, C_IN]), ("swap", mid)])
    ex.pair_sum(mid)

    dcat = _dgrad_rows("dgrad_out_ab", g1_b, w2d(A_OUT))
    ex.grad[A_OUT] = _wgrad_row_sharded("wgrad_out_ab", cat, g1_b, False)
    d_a, d_sw, d_sbt, d_slg, d_slb = _mixer_a_bwd(proj, dcat, slg, slb, sp["sgu_w"], bias_t)
    early = {"sgu_norm_g": d_slg, "sgu_norm_b": d_slb, "sgu_w": d_sw, "sgu_bias": d_sbt.T}
    ex.grad[REPLICATED_UNIT] = jnp.broadcast_to(
        _pack_replicated(early, REPLICATED_EARLY, REPLICATED_EARLY_ROWS).reshape(2, REPLICATED_EARLY_ROWS // 2, LANES),
        (N_CHIPS, 2, REPLICATED_EARLY_ROWS // 2, LANES))
    last = [A_OUT, REPLICATED_UNIT]
    dbb, dd, db_1, dd_1, db_2, dd_2 = _attn_bwd_prep(dcat, cat)
    regrouped_bwd = {1: (db_1, lse_1, dd_1), 2: (db_2, lse_2, dd_2)}
    bwd_phases = ([("scatter", [D0, SMALL_SHARDED])],
                  [("scatter", [G0]), ("join", [D0, SMALL_SHARDED]), ("swap", last)],
                  [("scatter", [U0])])
    dqs, dks, dvs = [], [], []
    for g, rate in enumerate(DIL_RATES):
        q, k, v = qkv[g]
        if rate == 1:
            db3, l3, dd3 = (a.reshape(1, t, B_WIDTH) for a in (dbb, lse_tot, dd))
        else:
            db3, l3, dd3 = regrouped_bwd[g]
        if g == 1:
            ex.chip_sum([D0, SMALL_SHARDED])
        elif g == 2:
            ex.pair_sum(last)
        dq, dk, dv = run(_attn_bwd, f"attn_bwd_{g}", q, k, v, db3, l3, dd3, phases=bwd_phases[g])
        if rate == 1:
            dq, dk, dv = (a.reshape(t, B_WIDTH) for a in (dq, dk, dv))
        dqs.append(dq)
        dks.append(dk)
        dvs.append(dv)
    ex.chip_sum([G0, U0])
    dproj, d_gains = run(_dproj_assemble, proj, d_a, dqs, dks, dvs, gains, tabs, phases=[("scatter", last), ("join", [G0, U0])])
    ex.chip_sum(last)
    d_gains = _fold_heads(d_gains)[0].reshape(2, N_DIL, B_WIDTH)[:, :, :HEAD_DIM]
    ex.grad[A_IN] = _wgrad_col_sharded("wgrad_in_ab", h0, [dproj], False)[0]
    ex.alone("swap_last", [("swap", [A_IN])])
    ex.pair_sum([A_IN])
    gx, d_abn = run(_dgrad_cols, "dgrad_in_ab", [dproj], [ex.full(A_IN)], 0, False, x, g_ab, g1, bf16_copy=False,
                    phases=[("join", last), ("scatter", [A_IN])])
    ex.chip_sum([A_IN])

    small.update({
        "ab_norm_g": d_abn, "sgu_norm_g": d_slg, "sgu_norm_b": d_slb, "sgu_w": d_sw, "sgu_bias": d_sbt.T,
        "q_norm_g": d_gains[0], "k_norm_g": d_gains[1], "ffn_norm_g": jnp.concatenate([d_f0, d_f1], axis=0),
    })
    return loss, gx, small


SHARDED_SMALL = ("cd_norm_g", "conv_c_w", "conv_c_b", "c_ln_g", "c_ln_b", "conv_d_w")
SHARDED_ROWS = 48
REPLICATED_EARLY = ("sgu_norm_g", "sgu_norm_b", "sgu_w", "sgu_bias")
REPLICATED_EARLY_ROWS = 528
REPLICATED_LATE = ("ab_norm_g", "q_norm_g", "k_norm_g", "ffn_norm_g", "loss")
REPLICATED_LATE_ROWS = 32
REPLICATED_SMALL = REPLICATED_EARLY + REPLICATED_LATE[:-1]


def _pack_sharded(parts):
    rows = [parts[k].reshape(-1, LANES) for k in SHARDED_SMALL]
    return _pad_rows(jnp.concatenate(rows, axis=0), SHARDED_ROWS)


def _split_full_small(small):
    per_chip = []
    for q in range(N_CHIPS):
        parts = {}
        for k in SHARDED_SMALL:
            a = small[k]
            a = a.reshape(-1, a.shape[-1])
            n = a.shape[-1] // N_CHIPS
            parts[k] = a[:, q * n:(q + 1) * n]
        per_chip.append(_pack_sharded(parts))
    return jnp.stack(per_chip)


def _unpack_sharded(pack, shapes):
    out, r = {}, 0
    for k in SHARDED_SMALL:
        n = math.prod(shapes[k]) // LANES
        out[k] = pack[r:r + n].reshape(shapes[k])
        r += n
    return out


def _gathered_small(packs, shapes):
    per_chip = [_unpack_sharded(packs[q], shapes) for q in range(N_CHIPS)]
    return {k: jnp.concatenate([pc[k] for pc in per_chip], axis=-1) for k in SHARDED_SMALL}


def _pack_replicated(small, names, total_rows):
    rows = []
    for k in names:
        a = small[k].reshape(-1)
        a = jnp.pad(a, (0, (-a.shape[0]) % LANES))
        rows.append(a.reshape(-1, LANES))
    return _pad_rows(jnp.concatenate(rows, axis=0), total_rows)


def _unpack_replicated(pack, shapes, names):
    out, r = {}, 0
    for k in names:
        size = math.prod(shapes[k])
        n = -(-size // LANES)
        out[k] = pack[r:r + n].reshape(-1)[:size].reshape(shapes[k])
        r += n
    return out


WEIGHT_ORDER = ("ab_norm_g", "ab_w_in", "sgu_norm_g", "sgu_norm_b", "sgu_w", "sgu_bias", "q_norm_g", "k_norm_g", "ab_w_out",
                "cd_norm_g", "cd_w_in", "conv_c_w", "conv_c_b", "c_ln_g", "c_ln_b", "conv_d_w", "cd_w_out", "ffn_norm_g",
                "ffn_w_gate", "ffn_w_up", "ffn_w_down")


def kernel(x, ab_norm_g, ab_w_in, sgu_norm_g, sgu_norm_b, sgu_w, sgu_bias, q_norm_g, k_norm_g, ab_w_out, cd_norm_g, cd_w_in, conv_c_w, conv_c_b, c_ln_g, c_ln_b, conv_d_w, cd_w_out, ffn_norm_g, ffn_w_gate, ffn_w_up, ffn_w_down, loss_target, m_ab_norm_g, m_ab_w_in, m_sgu_norm_g, m_sgu_norm_b, m_sgu_w, m_sgu_bias, m_q_norm_g, m_k_norm_g, m_ab_w_out, m_cd_norm_g, m_cd_w_in, m_conv_c_w, m_conv_c_b, m_c_ln_g, m_c_ln_b, m_conv_d_w, m_cd_w_out, m_ffn_norm_g, m_ffn_w_gate, m_ffn_w_up, m_ffn_w_down, v_ab_norm_g, v_ab_w_in, v_sgu_norm_g, v_sgu_norm_b, v_sgu_w, v_sgu_bias, v_q_norm_g, v_k_norm_g, v_ab_w_out, v_cd_norm_g, v_cd_w_in, v_conv_c_w, v_conv_c_b, v_c_ln_g, v_c_ln_b, v_conv_d_w, v_cd_w_out, v_ffn_norm_g, v_ffn_w_gate, v_ffn_w_up, v_ffn_w_down):
    args = dict(locals())
    ws = {k: args[k] for k in WEIGHT_ORDER}
    ms = {k: args["m_" + k] for k in WEIGHT_ORDER}
    vs = {k: args["v_" + k] for k in WEIGHT_ORDER}
    small_names = [k for k in WEIGHT_ORDER if k not in BIG]
    t, d = x.shape[1:]

    for group in (ws, ms, vs):
        for k in ROWS_MINOR:
            group[k] = jnp.swapaxes(group[k], 1, 2)
    ex = _Exchange(enabled=True)
    ex.w[A_IN], ex.far[A_IN] = _stage_own("stage_ab_w_in", ws["ab_w_in"], 0, BF16, far_slab=True)
    own_small = _pack_sharded({k: ws[k][0] for k in SHARDED_SMALL})
    ex.w[SMALL_SHARDED] = _stage_own("stage_small", own_small[None], 0, F32)
    rest = [u for u in UNITS if u != A_IN]
    x2 = x.reshape(t, d)
    h0, staged = ex.run(_stage_rest_and_norm, x2, ws["ab_norm_g"], [(ws[name], layer) for name, layer in rest],
                        phases=[("gather_near", [A_IN]), ("gather", [SMALL_SHARDED])])
    ex.w.update(zip(rest, staged))
    sp = _gathered_small(ex.w[SMALL_SHARDED].reshape(N_CHIPS, SHARDED_ROWS, LANES), {k: ws[k].shape[1:] for k in SHARDED_SMALL})
    for k in REPLICATED_SMALL:
        sp[k] = ws[k] if k == "ffn_norm_g" else ws[k][0]

    loss, grad_x, g_small = _local_step(x2, loss_target.reshape(t, d), ex, sp, h0)

    shapes = {k: ws[k].shape for k in REPLICATED_SMALL}
    shapes["loss"] = (1,)
    g_small["loss"] = loss
    join_last, settle = ex.ride_for([("join", [A_IN])])
    late, joined = _all_reduce_small(_pack_replicated(g_small, REPLICATED_LATE, REPLICATED_LATE_ROWS), join_last)
    settle(joined)
    grad = _unpack_sharded(ex.done[SMALL_SHARDED].reshape(SHARDED_ROWS, LANES), {k: ws[k].shape for k in SHARDED_SMALL})
    grad.update(_unpack_replicated(ex.done[REPLICATED_UNIT].reshape(REPLICATED_EARLY_ROWS, LANES), shapes, REPLICATED_EARLY))
    grad.update(_unpack_replicated(late, shapes, REPLICATED_LATE))
    loss = grad.pop("loss")[0]

    delta, new_m, new_v = {}, {}, {}
    for group in BIG_BY_LAYERS:
        g_layers = [[ex.done[(k, layer)] for layer in range(ws[k].shape[0])] for k in group]
        results = _adamw_big("adamw_" + group[0], [ws[k] for k in group], g_layers, [ms[k] for k in group], [vs[k] for k in group])
        for k, outs in zip(group, results):
            if k in ROWS_MINOR:
                outs = [jnp.swapaxes(o, 1, 2) for o in outs]
            grad[k], delta[k], new_m[k], new_v[k] = outs
    d_s, m_s, v_s = _adamw_small([ws[k] for k in small_names], [grad[k] for k in small_names],
                                 [ms[k] for k in small_names], [vs[k] for k in small_names])
    for j, k in enumerate(small_names):
        delta[k], new_m[k], new_v[k] = d_s[j], m_s[j], v_s[j]

    return (loss, grad_x[None], *[grad[k] for k in WEIGHT_ORDER], *[delta[k] for k in WEIGHT_ORDER],
            *[new_m[k] for k in WEIGHT_ORDER], *[new_v[k] for k in WEIGHT_ORDER])
```

```python
import functools
import math

import jax
import jax.numpy as jnp
from jax import lax
from jax.experimental import pallas as pl
from jax.experimental.pallas import tpu as pltpu

F32 = jnp.float32
BF16 = jnp.bfloat16
SDS = jax.ShapeDtypeStruct

N_CHIPS = 4
EPS = 1e-6
NEG_INF = -1e30
CHUNK = 128
A_GROUPS = 4
A_WIDTH = 512
N_DIL = 3
DIL_RATES = (1, 4, 16)
HEAD_DIM = 64
B_WIDTH = 512
ROPE_DIM = 16
ROPE_THETA = 500000.0
C_WIDTH = 512
C_KERNEL = 31
D_KERNEL = 3
HALO = 32
ATT_BLOCK = 128
LANES = 128

ADAM_LR = 0.001
ADAM_B1 = 0.9
ADAM_B2 = 0.999
ADAM_EPS = 1e-08
ADAM_WD = 0.01
ADAM_STEP = 10

VMEM_LIMIT = 56 * 1024 * 1024

NN = (((1,), (0,)), ((), ()))
NT = (((1,), (1,)), ((), ()))
TN = (((0,), (0,)), ((), ()))

TILES = {"proj_in": 2048, "proj_out": 1024, "ffn_in": 1024, "ffn_out": 1024, "ffn_dact": 512, "dgrad_cols": 512,
         "dgrad_rows": 1024, "wgrad": 4096}


def _params(sem=None, collective_id=None):
    return pltpu.CompilerParams(dimension_semantics=sem, vmem_limit_bytes=VMEM_LIMIT, collective_id=collective_id)


def _bf(v):
    return v if v.dtype == BF16 else v.astype(BF16)


def _dot(a, b, dims):
    return lax.dot_general(_bf(a), _bf(b), dims, preferred_element_type=F32)


def _dot_hi(a, b):
    return jnp.dot(a, b, precision=lax.Precision.HIGHEST, preferred_element_type=F32)


def _sigmoid(v):
    return 0.5 * jnp.tanh(0.5 * v) + 0.5


def _gelu(v):
    return 0.5 * v * (1.0 + lax.erf(v * (1.0 / math.sqrt(2.0))))


def _gelu_grad(v):
    cdf = 0.5 * (1.0 + lax.erf(v * (1.0 / math.sqrt(2.0))))
    return cdf + v * jnp.exp(-0.5 * v * v) * (1.0 / math.sqrt(2.0 * math.pi))


def _segment_mean_matrix(seg, scale=None):
    r = lax.broadcasted_iota(jnp.int32, (LANES, LANES), 0) // seg
    c = lax.broadcasted_iota(jnp.int32, (LANES, LANES), 1) // seg
    return jnp.where(r == c, (1.0 / seg) if scale is None else scale, 0.0).astype(BF16)


def _segment_dot(v, seg):
    hi = v.astype(BF16)
    lo = (v - hi.astype(F32)).astype(BF16)
    return jnp.dot(hi, seg, preferred_element_type=F32) + jnp.dot(lo, seg, preferred_element_type=F32)


MESH = pl.DeviceIdType.MESH
ANY = pl.BlockSpec(memory_space=pl.ANY)


def _position():
    x, y, c = lax.axis_index("x"), lax.axis_index("y"), lax.axis_index("c")
    others = [(1 - x, y), (x, 1 - y), (1 - x, 1 - y)]
    return x, y, c, 2 * x + y, others


class _Ride:
    def __init__(self, ins, bufs, new_outs, sem_shapes, start, finish, reach):
        self.ins, self.bufs, self.new_outs, self.sem_shapes = list(ins), list(bufs), list(new_outs), list(sem_shapes)
        self.start, self.finish = start, finish
        self.reach = frozenset(reach)

    def entry_barrier(self):
        x, y, c, _, others = _position()
        peers = ([(x, y, 1 - c)] if "sibling" in self.reach else []) + ([(qx, qy, c) for qx, qy in others] if "chips" in self.reach else [])
        barrier = pltpu.get_barrier_semaphore()
        for peer in peers:
            pl.semaphore_signal(barrier, inc=1, device_id=peer, device_id_type=MESH)
        pl.semaphore_wait(barrier, len(peers))

    @property
    def collective_id(self):
        return {frozenset(["sibling"]): 0, frozenset(["chips"]): 1, frozenset(["sibling", "chips"]): 2}[self.reach]


def _ride_both(a, b):
    na = (len(a.ins), len(a.bufs), len(a.new_outs), len(a.sem_shapes))

    def split(ins, bufs, new, sems):
        return ((ins[:na[0]], bufs[:na[1]], new[:na[2]], sems[:na[3]]), (ins[na[0]:], bufs[na[1]:], new[na[2]:], sems[na[3]:]))

    def start(*refs):
        ra, rb = split(*refs)
        a.start(*ra)
        b.start(*rb)

    def finish(*refs):
        ra, rb = split(*refs)
        a.finish(*ra)
        b.finish(*rb)

    return _Ride(a.ins + b.ins, a.bufs + b.bufs, a.new_outs + b.new_outs, a.sem_shapes + b.sem_shapes, start, finish,
                 a.reach | b.reach)


def _call(body, *, grid, in_specs, out_specs, out_shape, operands, name, scratch_shapes=(), aliases=None, ride=None,
          prefetch=None):
    off = 0 if prefetch is None else 1
    lead = [] if prefetch is None else [prefetch]

    params = _params(collective_id=None if ride is None else ride.collective_id)

    def launch(kernel_body, in_specs_, out_specs_, out_shape_, scratch_, aliases_, *args):
        if prefetch is None:
            return pl.pallas_call(kernel_body, grid=grid, in_specs=in_specs_, out_specs=out_specs_, out_shape=out_shape_,
                                  scratch_shapes=scratch_, input_output_aliases=aliases_, name=name,
                                  compiler_params=params)(*args)
        spec = pltpu.PrefetchScalarGridSpec(num_scalar_prefetch=1, grid=grid, in_specs=in_specs_, out_specs=out_specs_,
                                            scratch_shapes=scratch_)
        return pl.pallas_call(kernel_body, grid_spec=spec, out_shape=out_shape_, input_output_aliases=aliases_, name=name,
                              compiler_params=params)(*lead, *args)

    if ride is None:
        return launch(body, list(in_specs), out_specs, out_shape, list(scratch_shapes), dict(aliases or {}), *operands)
    multi = isinstance(out_shape, (list, tuple))
    out_shapes = list(out_shape) if multi else [out_shape]
    o_specs = list(out_specs) if multi else [out_specs]
    n_in, n_out, n_scr = off + len(operands), len(out_shapes), len(scratch_shapes)
    n_ri, n_rb, n_rn = len(ride.ins), len(ride.bufs), len(ride.new_outs)

    def carrying(*refs):
        k = n_in
        r_ins = refs[k:k + n_ri]
        k += n_ri + n_rb
        outs = refs[k:k + n_out]
        k += n_out
        r_bufs = refs[k:k + n_rb]
        k += n_rb
        r_new = refs[k:k + n_rn]
        k += n_rn
        scratch = refs[k:k + n_scr]
        sems = refs[k + n_scr:]
        first, last = None, None
        for axis, size in enumerate(grid):
            pid = pl.program_id(axis)
            first = (pid == 0) if first is None else first & (pid == 0)
            last = (pid == size - 1) if last is None else last & (pid == size - 1)

        @pl.when(first)
        def _():
            ride.entry_barrier()
            ride.start(r_ins, r_bufs, r_new, sems)

        body(*refs[:n_in], *outs, *scratch)

        @pl.when(last)
        def _():
            ride.finish(r_ins, r_bufs, r_new, sems)

    all_aliases = dict(aliases or {})
    for j in range(n_rb):
        all_aliases[n_in + n_ri + j] = n_out + j
    res = launch(
        carrying, list(in_specs) + [ANY] * (n_ri + n_rb), o_specs + [ANY] * (n_rb + n_rn),
        out_shapes + [SDS(b.shape, b.dtype) for b in ride.bufs] + ride.new_outs,
        list(scratch_shapes) + [pltpu.SemaphoreType.DMA(s) for s in ride.sem_shapes], all_aliases,
        *operands, *ride.ins, *ride.bufs)
    outs = res[:n_out]
    return (list(outs) if multi else outs[0]), list(res[n_out:])


def _run_ride(name, ride):
    n_ri, n_rb, n_rn = len(ride.ins), len(ride.bufs), len(ride.new_outs)

    def body(*refs):
        r_ins = refs[:n_ri]
        r_bufs = refs[n_ri + n_rb:n_ri + 2 * n_rb]
        r_new = refs[n_ri + 2 * n_rb:n_ri + 2 * n_rb + n_rn]
        sems = refs[n_ri + 2 * n_rb + n_rn:]
        ride.entry_barrier()
        ride.start(r_ins, r_bufs, r_new, sems)
        ride.finish(r_ins, r_bufs, r_new, sems)

    return list(pl.pallas_call(
        body, in_specs=[ANY] * (n_ri + n_rb), out_specs=[ANY] * (n_rb + n_rn),
        out_shape=[SDS(b.shape, b.dtype) for b in ride.bufs] + ride.new_outs,
        scratch_shapes=[pltpu.SemaphoreType.DMA(s) for s in ride.sem_shapes],
        input_output_aliases={n_ri + j: j for j in range(n_rb)}, name=name,
        compiler_params=pltpu.CompilerParams(collective_id=ride.collective_id))(*ride.ins, *ride.bufs))


def _whole(ref, p):
    return ref[...]


def _slab(ref, p):
    return ref[p]


def _matmul(name, grid, pairs, extras, outs, dims, epi, *, slabs=1, n_acc=1, ride=None):
    n_pairs, n_ex, n_out = len(pairs), len(extras), len(outs)

    def body(*refs):
        ab = refs[:2 * n_pairs]
        ex = refs[2 * n_pairs:2 * n_pairs + n_ex]
        out_refs = refs[2 * n_pairs + n_ex:2 * n_pairs + n_ex + n_out]
        pids = tuple(pl.program_id(a) for a in range(len(grid)))
        parts = [None] * n_acc
        for p in range(slabs):
            for j, (_, _, a_pick, _, _, b_pick, acc) in enumerate(pairs):
                d = _dot(a_pick(ab[2 * j], p), b_pick(ab[2 * j + 1], p), dims)
                parts[acc] = d if parts[acc] is None else parts[acc] + d
        epi(parts, ex, out_refs, pids)

    operands, in_specs = [], []
    for a, a_spec, _, b, b_spec, _, _ in pairs:
        operands += [a, b]
        in_specs += [a_spec, b_spec]
    for e, e_spec in extras:
        operands.append(e)
        in_specs.append(e_spec)
    return _call(body, grid=grid, in_specs=in_specs, out_specs=[o[1] for o in outs], out_shape=[o[0] for o in outs],
                 operands=operands, name=name, ride=ride)


def _rms_rows(v, g):
    r = lax.rsqrt(jnp.mean(v * v, axis=-1, keepdims=True) + EPS)
    return v * r * g


def _rms_fwd(name, x, g):
    t, d = x.shape
    tm = 512

    def body(x_ref, g_ref, o_ref):
        o_ref[...] = _rms_rows(x_ref[...], g_ref[...]).astype(BF16)

    return pl.pallas_call(
        body, grid=(t // tm,),
        in_specs=[pl.BlockSpec((tm, d), lambda i: (i, 0)), pl.BlockSpec((1, d), lambda i: (0, 0))],
        out_specs=pl.BlockSpec((tm, d), lambda i: (i, 0)), out_shape=SDS((t, d), BF16), name=name,
        compiler_params=_params())(x, g)


def _epi_residual_norm(accs, ex, outs, pids):
    x_new = accs[0] + ex[0][...]
    outs[0][...] = x_new
    outs[1][...] = _rms_rows(x_new, ex[1][...]).astype(BF16)


def _epi_residual_loss(accs, ex, outs, pids):
    y = accs[0] + ex[0][...]
    err = y - ex[1][...]
    dy = err * (1.0 / err.shape[-1])
    outs[0][...] = dy
    outs[2][...] = dy.astype(BF16)

    @pl.when(pids[0] == 0)
    def _():
        outs[1][...] = jnp.zeros_like(outs[1])

    outs[1][...] += jnp.sum(err * err)


def _epi_rms_bwd(accs, ex, outs, pids):
    dh = accs[0]
    xv, g, res = ex[0][...], ex[1][...], ex[2][...]
    r = lax.rsqrt(jnp.mean(xv * xv, axis=-1, keepdims=True) + EPS)
    xh = xv * r
    dy = dh * g
    dx = res + r * (dy - xh * jnp.mean(dy * xh, axis=-1, keepdims=True))
    outs[0][...] = dx
    if len(outs) > 2:
        outs[2][...] = dx.astype(BF16)

    @pl.when(pids[0] == 0)
    def _():
        outs[1][...] = jnp.zeros_like(outs[1])

    outs[1][...] += jnp.sum(dh * xh, axis=0, keepdims=True)


def _row_spec(tm, d):
    return pl.BlockSpec((tm, d), lambda i, *_: (i, 0))


def _const_spec(shape):
    nd = len(shape)
    return pl.BlockSpec(shape, lambda *_: (0,) * nd)


def _proj_in(name, h, w, layer, ride=None):
    t, d = h.shape
    n4 = w.shape[-1]
    tm = TILES["proj_in"]

    def epi(accs, ex, outs, pids):
        outs[0][...] = accs[0].astype(BF16)

    res = _matmul(
        name, (N_CHIPS, t // tm),
        [(h, pl.BlockSpec((tm, d), lambda p, i: (i, 0)), _whole,
          w, pl.BlockSpec((None, None, d, n4), lambda p, i: (p, layer, 0, 0)), _whole, 0)],
        [], [(SDS((t, N_CHIPS * n4), BF16), pl.BlockSpec((tm, n4), lambda p, i: (i, p)))],
        NN, epi, ride=ride)
    return res[0] if ride is None else (res[0][0], res[1])


def _proj_in_near(name, h, w, ride=None):
    t, d = h.shape
    n4 = w.shape[-1]
    tm = TILES["proj_in"]

    def shard(j, s):
        return j + (j >= N_CHIPS - 1 - s[1]).astype(jnp.int32)

    def body(s_ref, h_ref, w_ref, o_ref):
        o_ref[...] = _dot(h_ref[...], w_ref[...], NN).astype(BF16)

    return _call(
        body, grid=(N_CHIPS - 1, t // tm),
        in_specs=[pl.BlockSpec((tm, d), lambda j, i, s: (i, 0)),
                  pl.BlockSpec((None, None, d, n4), lambda j, i, s: (shard(j, s), 0, 0, 0))],
        out_specs=pl.BlockSpec((tm, n4), lambda j, i, s: (i, shard(j, s))), out_shape=SDS((t, N_CHIPS * n4), BF16),
        operands=[h, w], name=name, ride=ride, prefetch=_mesh_scalars())


def _proj_in_far(name, h, slab, proj, w):
    t, d = h.shape
    n4 = slab.shape[-1]
    tm = TILES["proj_in"]

    def body(s_ref, h_ref, slab_ref, proj_in, w_in, o_ref, w_ref):
        shard = slab_ref[...]
        o_ref[...] = _dot(h_ref[...], shard, NN).astype(BF16)
        w_ref[...] = shard

    proj, w_full = _call(
        body, grid=(t // tm,),
        in_specs=[pl.BlockSpec((tm, d), lambda i, s: (i, 0)), pl.BlockSpec((d, n4), lambda i, s: (0, 0)), ANY, ANY],
        out_specs=[pl.BlockSpec((tm, n4), lambda i, s: (i, N_CHIPS - 1 - s[1])),
                   pl.BlockSpec((None, d, n4), lambda i, s: (N_CHIPS - 1 - s[1], 0, 0))],
        out_shape=[SDS(proj.shape, BF16), SDS((N_CHIPS, d, n4), BF16)],
        operands=[h, slab.reshape(d, n4), proj, w.reshape(N_CHIPS, d, n4)], aliases={3: 0, 4: 1}, name=name,
        prefetch=_mesh_scalars())
    return proj, w_full.reshape(w.shape)


def _proj_out(name, a, w, x, g_next=None, target=None):
    t, k = a.shape
    d = w.shape[-1]
    tm = TILES["proj_out"]
    if target is None:
        extras = [(x, _row_spec(tm, d)), (g_next, _const_spec((1, d)))]
        outs = [(SDS((t, d), F32), _row_spec(tm, d)), (SDS((t, d), BF16), _row_spec(tm, d))]
        epi = _epi_residual_norm
    else:
        extras = [(x, _row_spec(tm, d)), (target, _row_spec(tm, d))]
        outs = [(SDS((t, d), F32), _row_spec(tm, d)), (SDS((8, LANES), F32), _const_spec((8, LANES))),
                (SDS((t, d), BF16), _row_spec(tm, d))]
        epi = _epi_residual_loss
    return _matmul(name, (t // tm,), [(a, _row_spec(tm, k), _whole, w, _const_spec((k, d)), _whole, 0)], extras, outs, NN, epi)


def _ffn_in(name, h, wg, wu, layer, ride=None):
    t, d = h.shape
    n4 = wg.shape[-2]
    tm = TILES["ffn_in"]

    def epi(accs, ex, outs, pids):
        gate, up = accs
        s = _sigmoid(gate)
        silu = gate * s
        outs[0][...] = (up * (s + silu - silu * s)).astype(BF16)
        outs[1][...] = silu.astype(BF16)
        outs[2][...] = (silu * up).astype(BF16)

    w_spec = pl.BlockSpec((None, None, n4, d), lambda p, i: (p, layer, 0, 0))
    h_spec = pl.BlockSpec((tm, d), lambda p, i: (i, 0))
    o = (SDS((N_CHIPS, t, n4), BF16), pl.BlockSpec((None, tm, n4), lambda p, i: (p, i, 0)))
    return _matmul(name, (N_CHIPS, t // tm),
                   [(h, h_spec, _whole, wg, w_spec, _whole, 0), (h, h_spec, _whole, wu, w_spec, _whole, 1)], [],
                   [o, o, o], NT, epi, n_acc=2, ride=ride)


def _ffn_out(name, act, wd, layer, x, g_next=None, target=None, ride=None):
    _, t, n4 = act.shape
    d = wd.shape[-1]
    tm = TILES["ffn_out"]
    xs = _row_spec(tm, d)
    if target is None:
        extras = [(x, xs), (g_next, _const_spec((1, d)))]
        outs = [(SDS((t, d), F32), xs), (SDS((t, d), BF16), xs)]
        epi = _epi_residual_norm
    else:
        extras = [(x, xs), (target, xs)]
        outs = [(SDS((t, d), F32), xs), (SDS((8, LANES), F32), _const_spec((8, LANES))), (SDS((t, d), BF16), xs)]
        epi = _epi_residual_loss
    return _matmul(
        name, (t // tm,),
        [(act, pl.BlockSpec((N_CHIPS, tm, n4), lambda i: (0, i, 0)), _slab,
          wd, pl.BlockSpec((N_CHIPS, None, n4, d), lambda i: (0, layer, 0, 0)), _slab, 0)],
        extras, outs, NN, epi, slabs=N_CHIPS, ride=ride)


def _ffn_dact(name, g, wd, layer, gate, up, ride=None):
    t, d = g.shape
    n4 = wd.shape[-2]
    tm = TILES["ffn_dact"]

    def body(g_ref, w_ref, gate_ref, up_ref, dgate_ref, dup_ref):
        gv = g_ref[...]
        for p in range(N_CHIPS):
            dact = _dot(gv, w_ref[p], NT)
            dgate_ref[p] = (dact * gate_ref[p].astype(F32)).astype(BF16)
            dup_ref[p] = (dact * up_ref[p].astype(F32)).astype(BF16)

    blk = pl.BlockSpec((N_CHIPS, tm, n4), lambda i: (0, i, 0))
    return _call(
        body, grid=(t // tm,),
        in_specs=[_row_spec(tm, d), pl.BlockSpec((N_CHIPS, None, n4, d), lambda i: (0, layer, 0, 0)), blk, blk],
        out_specs=[blk, blk], out_shape=[SDS((N_CHIPS, t, n4), BF16)] * 2, operands=[g, wd, gate, up], name=name, ride=ride)


def _copy_epi(accs, ex, outs, pids):
    for a, o in zip(accs, outs):
        o[...] = a.astype(o.dtype)


def _dgrad_cols(name, dz_list, w_list, layer, three_d, x, g, res, bf16_copy=True, w_rows=False, ride=None):
    t, d = x.shape
    n4 = w_list[0].shape[-2 if w_rows else -1]
    tm = TILES["dgrad_cols"]
    if three_d:
        zs, z_pick = pl.BlockSpec((N_CHIPS, tm, n4), lambda i: (0, i, 0)), _slab
    else:
        zs, z_pick = _row_spec(tm, N_CHIPS * n4), (lambda ref, p: ref[:, p * n4:(p + 1) * n4])
    ws = pl.BlockSpec((N_CHIPS, None) + ((n4, d) if w_rows else (d, n4)), lambda i: (0, layer, 0, 0))
    xs = _row_spec(tm, d)
    return _matmul(
        name, (t // tm,), [(dz, zs, z_pick, w, ws, _slab, 0) for dz, w in zip(dz_list, w_list)],
        [(x, xs), (g, _const_spec((1, d))), (res, xs)],
        [(SDS((t, d), F32), xs), (SDS((1, d), F32), _const_spec((1, d)))] + ([(SDS((t, d), BF16), xs)] if bf16_copy else []),
        NN if w_rows else NT, _epi_rms_bwd, slabs=N_CHIPS, ride=ride)


def _dgrad_rows(name, g, w):
    t, d = g.shape
    k = w.shape[0]
    tm = TILES["dgrad_rows"]
    return _matmul(name, (t // tm,), [(g, _row_spec(tm, d), _whole, w, _const_spec((k, d)), _whole, 0)], [],
                   [(SDS((t, k), F32), _row_spec(tm, k))], NT, _copy_epi)[0]


A_TILE = 256


def _a_common(p_ref, lg_ref, lb_ref):
    pv = p_ref[...].astype(F32)
    a = _gelu(pv)
    u, v = a[:, :A_WIDTH], a[:, A_WIDTH:]
    vc = v - jnp.mean(v, axis=-1, keepdims=True)
    rs = lax.rsqrt(jnp.mean(vc * vc, axis=-1, keepdims=True) + EPS)
    vhat = vc * rs
    vn = vhat * lg_ref[...] + lb_ref[...]
    return pv, u, vhat, rs, vn.astype(BF16)


def _tril_weights(w_ref, g):
    r = lax.broadcasted_iota(jnp.int32, (CHUNK, CHUNK), 0)
    c = lax.broadcasted_iota(jnp.int32, (CHUNK, CHUNK), 1)
    return jnp.where(c <= r, w_ref[g], 0.0).astype(BF16), c <= r


def _mixer_a_fwd(proj, lg, lb, w, bias_t):
    t = proj.shape[0]

    def body(p_ref, lg_ref, lb_ref, w_ref, bt_ref, o_ref):
        _, u, _, _, vnb = _a_common(p_ref, lg_ref, lb_ref)
        for g in range(A_GROUPS):
            wt, _ = _tril_weights(w_ref, g)
            cs = slice(g * CHUNK, (g + 1) * CHUNK)
            for ch in range(A_TILE // CHUNK):
                rs_ = slice(ch * CHUNK, (ch + 1) * CHUNK)
                mixed = _dot(wt, vnb[rs_, cs], NN) + bt_ref[:, g:g + 1]
                o_ref[rs_, cs] = (u[rs_, cs] * mixed).astype(BF16)

    return pl.pallas_call(
        body, grid=(t // A_TILE,),
        in_specs=[pl.BlockSpec((A_TILE, 2 * A_WIDTH), lambda i: (i, 0)), _const_spec((1, A_WIDTH)),
                  _const_spec((1, A_WIDTH)), _const_spec((A_GROUPS, CHUNK, CHUNK)), _const_spec((CHUNK, A_GROUPS))],
        out_specs=pl.BlockSpec((A_TILE, A_WIDTH), lambda i: (i, 0)), out_shape=SDS((t, A_WIDTH), BF16),
        name="mixer_a_fwd", compiler_params=_params())(proj, lg, lb, w, bias_t)


def _mixer_a_bwd(proj, dcat, lg, lb, w, bias_t):
    t = proj.shape[0]

    def body(p_ref, da_ref, lg_ref, lb_ref, w_ref, bt_ref, dp_ref, dw_ref, dbt_ref, dlg_ref, dlb_ref, du_scr, dvn_scr):
        @pl.when(pl.program_id(0) == 0)
        def _():
            dw_ref[...] = jnp.zeros_like(dw_ref)
            dbt_ref[...] = jnp.zeros_like(dbt_ref)
            dlg_ref[...] = jnp.zeros_like(dlg_ref)
            dlb_ref[...] = jnp.zeros_like(dlb_ref)

        pv, u, vhat, rs, vnb = _a_common(p_ref, lg_ref, lb_ref)
        da = da_ref[...]
        for g in range(A_GROUPS):
            wt, keep = _tril_weights(w_ref, g)
            cs = slice(g * CHUNK, (g + 1) * CHUNK)
            for ch in range(A_TILE // CHUNK):
                rs_ = slice(ch * CHUNK, (ch + 1) * CHUNK)
                vg = vnb[rs_, cs]
                mixed = _dot(wt, vg, NN) + bt_ref[:, g:g + 1]
                du_scr[rs_, cs] = da[rs_, cs] * mixed
                dmx = da[rs_, cs] * u[rs_, cs]
                dw_ref[g] += jnp.where(keep, _dot(dmx, vg, NT), 0.0)
                dvn_scr[rs_, cs] = _dot(wt, dmx, TN)
                dbt_ref[:, g:g + 1] += jnp.sum(dmx, axis=1, keepdims=True)
        dvn = dvn_scr[...]
        dlg_ref[...] += jnp.sum(dvn * vhat, axis=0, keepdims=True)
        dlb_ref[...] += jnp.sum(dvn, axis=0, keepdims=True)
        dvh = dvn * lg_ref[...]
        dv = rs * (dvh - jnp.mean(dvh, axis=-1, keepdims=True) - vhat * jnp.mean(dvh * vhat, axis=-1, keepdims=True))
        gp = _gelu_grad(pv)
        dp_ref[:, :A_WIDTH] = (du_scr[...] * gp[:, :A_WIDTH]).astype(BF16)
        dp_ref[:, A_WIDTH:] = (dv * gp[:, A_WIDTH:]).astype(BF16)

    return pl.pallas_call(
        body, grid=(t // A_TILE,),
        in_specs=[pl.BlockSpec((A_TILE, 2 * A_WIDTH), lambda i: (i, 0)), pl.BlockSpec((A_TILE, A_WIDTH), lambda i: (i, 0)),
                  _const_spec((1, A_WIDTH)), _const_spec((1, A_WIDTH)), _const_spec((A_GROUPS, CHUNK, CHUNK)),
                  _const_spec((CHUNK, A_GROUPS))],
        out_specs=[pl.BlockSpec((A_TILE, 2 * A_WIDTH), lambda i: (i, 0)), _const_spec((A_GROUPS, CHUNK, CHUNK)),
                   _const_spec((CHUNK, A_GROUPS)), _const_spec((1, A_WIDTH)), _const_spec((1, A_WIDTH))],
        out_shape=[SDS((t, 2 * A_WIDTH), BF16), SDS((A_GROUPS, CHUNK, CHUNK), F32), SDS((CHUNK, A_GROUPS), F32),
                   SDS((1, A_WIDTH), F32), SDS((1, A_WIDTH), F32)],
        scratch_shapes=[pltpu.VMEM((A_TILE, A_WIDTH), F32), pltpu.VMEM((A_TILE, A_WIDTH), F32)],
        name="mixer_a_bwd", compiler_params=_params())(proj, dcat, lg, lb, w, bias_t)


def _rope_tables(t):
    half = ROPE_DIM // 2
    inv_freq = ROPE_THETA ** (-jnp.arange(half, dtype=F32) * 2.0 / ROPE_DIM)
    ang = jnp.arange(t, dtype=F32)[:, None] * inv_freq[None, :]
    cos, sin = jnp.cos(ang), jnp.sin(ang)
    one = jnp.ones((t, HEAD_DIM - ROPE_DIM), F32)
    zero = jnp.zeros((t, HEAD_DIM - ROPE_DIM), F32)
    zh = jnp.zeros((t, half), F32)
    c = jnp.concatenate([cos, cos, one], axis=1)
    s1 = jnp.concatenate([-sin, zh, zero], axis=1)
    s2 = jnp.concatenate([zh, sin, zero], axis=1)
    return tuple(jnp.tile(a, (1, LANES // HEAD_DIM)) for a in (c, s1, s2))


QK_TILE = 512
QK_ROWS = 64
QK_COLS = 2 * N_DIL * B_WIDTH


CHUNKS = B_WIDTH // LANES


def _regroup_out(scr, first, out_ref, rate, tile):
    rows = tile // rate
    for rho in range(rate):
        for c in range(CHUNKS):
            out_ref[rho, :, c * LANES:(c + 1) * LANES] = scr[first + c, pl.ds(rho, rows, stride=rate), :].astype(out_ref.dtype)


def _regroup_in(x_ref, scr, rate, tile):
    rows = tile // rate
    for rho in range(rate):
        for c in range(CHUNKS):
            scr[c, pl.ds(rho, rows, stride=rate), :] = x_ref[rho, :, c * LANES:(c + 1) * LANES].astype(F32)


def _regrouped_spec(rate, tile):
    return pl.BlockSpec((rate, tile // rate, B_WIDTH), lambda i, *_: (0, i, 0))


def _qk_fwd(proj, gains, tabs, ride=None):
    t = proj.shape[0]
    col0 = 2 * A_WIDTH // 1024
    r1, r2 = DIL_RATES[1], DIL_RATES[2]

    def body(p_ref, g_ref, c_ref, s1_ref, s2_ref, o_ref, q1_ref, q2_ref, k1_ref, k2_ref, scr):
        seg = _segment_mean_matrix(HEAD_DIM)
        for r0 in range(0, QK_TILE, QK_ROWS):
            rows = slice(r0, r0 + QK_ROWS)
            c, s1, s2 = c_ref[rows, :], s1_ref[rows, :], s2_ref[rows, :]
            for ci in range(1024 // LANES):
                ls = slice(ci * LANES, (ci + 1) * LANES)
                xv = p_ref[rows, ls].astype(F32)
                r = lax.rsqrt(_segment_dot(xv * xv, seg) + EPS)
                y = xv * r * g_ref[:, ls]
                val = y * c + pltpu.roll(y, LANES - 8, axis=1) * s1 + pltpu.roll(y, 8, axis=1) * s2
                o_ref[rows, ls] = val.astype(BF16)
                scr[ci, rows, :] = val

        j = pl.program_id(1)

        @pl.when(j == 0)
        def _():
            _regroup_out(scr, CHUNKS, q1_ref, r1, QK_TILE)

        @pl.when(j == 1)
        def _():
            _regroup_out(scr, 0, q2_ref, r2, QK_TILE)

        @pl.when(j == 2)
        def _():
            _regroup_out(scr, 0, k1_ref, r1, QK_TILE)
            _regroup_out(scr, CHUNKS, k2_ref, r2, QK_TILE)

    tab = pl.BlockSpec((QK_TILE, LANES), lambda i, j: (i, 0))
    g1, g2 = SDS((r1, t // r1, B_WIDTH), BF16), SDS((r2, t // r2, B_WIDTH), BF16)
    s1_, s2_ = _regrouped_spec(r1, QK_TILE), _regrouped_spec(r2, QK_TILE)
    return _call(
        body, grid=(t // QK_TILE, QK_COLS // 1024),
        in_specs=[pl.BlockSpec((QK_TILE, 1024), lambda i, j: (i, col0 + j)), pl.BlockSpec((1, 1024), lambda i, j: (0, j)),
                  tab, tab, tab],
        out_specs=[pl.BlockSpec((QK_TILE, 1024), lambda i, j: (i, j)), s1_, s2_, s1_, s2_],
        out_shape=[SDS((t, QK_COLS), BF16), g1, g2, g1, g2],
        scratch_shapes=[pltpu.VMEM((2 * CHUNKS, QK_TILE, LANES), F32)],
        operands=[proj, gains, *tabs], name="qk_norm_rope_fwd", ride=ride)


PERM_TILE = 512


def _permute(name, items):
    t = items[0][0].shape[0]
    n = len(items)

    def body(*refs):
        scr = refs[-1]
        for x_ref, o_ref, (_, _, rate) in zip(refs[:n], refs[n:2 * n], items):
            for ci in range(CHUNKS):
                scr[ci] = x_ref[:, ci * LANES:(ci + 1) * LANES].astype(F32)
            _regroup_out(scr, 0, o_ref, rate, PERM_TILE)

    return pl.pallas_call(
        body, grid=(t // PERM_TILE,),
        in_specs=[pl.BlockSpec((PERM_TILE, B_WIDTH), functools.partial(lambda cb, i: (i, cb), cb)) for _, cb, _ in items],
        out_specs=[_regrouped_spec(rate, PERM_TILE) for _, _, rate in items],
        out_shape=[SDS((rate, t // rate, B_WIDTH), a.dtype) for a, _, rate in items],
        scratch_shapes=[pltpu.VMEM((CHUNKS, PERM_TILE, LANES), F32)],
        name=name, compiler_params=_params())(*[a for a, _, _ in items])


def _head_lane_mask(h):
    lane = lax.broadcasted_iota(jnp.int32, (1, LANES), 1)
    return (lane < HEAD_DIM) if h == 0 else (lane >= HEAD_DIM)


def _attn_fwd(name, q, k, v, ride=None):
    rate, length = q[0].shape[0], q[0].shape[1]
    nb = length // ATT_BLOCK
    scale = HEAD_DIM ** -0.5

    def body(q_ref, kc_ref, kp_ref, vc_ref, vp_ref, o_ref, l_ref):
        n = pl.program_id(1)
        qi = lax.broadcasted_iota(jnp.int32, (ATT_BLOCK, 2 * ATT_BLOCK), 0)
        cj = lax.broadcasted_iota(jnp.int32, (ATT_BLOCK, 2 * ATT_BLOCK), 1)
        has_prev = jnp.where(n > 0, 0, 2 * ATT_BLOCK)
        mask = ((cj < ATT_BLOCK) & (cj >= qi + has_prev)) | ((cj >= ATT_BLOCK) & (cj - ATT_BLOCK <= qi))
        heads = [(hp, h) for hp in range(CHUNKS) for h in range(2)]
        q2, k2, v2 = {}, {}, {}
        for hp in range(CHUNKS):
            ls = slice(hp * LANES, (hp + 1) * LANES)
            q2[hp] = q_ref[:, ls]
            k2[hp] = jnp.concatenate([kp_ref[:, ls], kc_ref[:, ls]], axis=0)
            v2[hp] = jnp.concatenate([vp_ref[:, ls], vc_ref[:, ls]], axis=0)
        scores = {}
        for hp, h in heads:
            scores[hp, h] = _dot(jnp.where(_head_lane_mask(h), q2[hp], jnp.zeros_like(q2[hp])), k2[hp], NT) * scale
        probs, lses = {}, {}
        for hp, h in heads:
            s = jnp.where(mask, scores[hp, h], NEG_INF)
            m = jnp.max(s, axis=1, keepdims=True)
            p = jnp.exp(s - m)
            den = jnp.sum(p, axis=1, keepdims=True)
            lses[hp, h] = m + jnp.log(den)
            probs[hp, h] = (p / den).astype(BF16)
        for hp in range(CHUNKS):
            ls = slice(hp * LANES, (hp + 1) * LANES)
            o_acc = None
            for h in range(2):
                o = _dot(probs[hp, h], jnp.where(_head_lane_mask(h), v2[hp], jnp.zeros_like(v2[hp])), NN)
                o_acc = o if o_acc is None else o_acc + o
            o_ref[:, ls] = o_acc
            zeros = jnp.zeros((ATT_BLOCK, LANES), F32)
            l_ref[:, ls] = jnp.where(_head_lane_mask(1), lses[hp, 1] + zeros, lses[hp, 0] + zeros)

    def cur(cb):
        return pl.BlockSpec((None, ATT_BLOCK, B_WIDTH), lambda r, n: (r, n, cb))

    def prev(cb):
        return pl.BlockSpec((None, ATT_BLOCK, B_WIDTH), lambda r, n: (r, jnp.maximum(n - 1, 0), cb))

    out = pl.BlockSpec((None, ATT_BLOCK, B_WIDTH), lambda r, n: (r, n, 0))
    return _call(
        body, grid=(rate, nb),
        in_specs=[cur(q[1]), cur(k[1]), prev(k[1]), cur(v[1]), prev(v[1])],
        out_specs=[out, out], out_shape=[SDS((rate, length, B_WIDTH), F32)] * 2,
        operands=[q[0], k[0], k[0], v[0], v[0]], name=name, ride=ride)


def _attn_merge(a_out, o_list, l_list):
    t = a_out.shape[0]
    tm = PERM_TILE
    r1, r2 = DIL_RATES[1], DIL_RATES[2]

    def body(a_ref, o0, o1, o2, l0, l1, l2, cat_ref, lt_ref, lt1_ref, lt2_ref, so1, so2, sl1, sl2, slt):
        _regroup_in(o1, so1, r1, tm)
        _regroup_in(l1, sl1, r1, tm)
        _regroup_in(o2, so2, r2, tm)
        _regroup_in(l2, sl2, r2, tm)
        cat_ref[:, :A_WIDTH] = a_ref[...]
        for c in range(CHUNKS):
            ls = slice(c * LANES, (c + 1) * LANES)
            lg = [l0[:, ls], sl1[c], sl2[c]]
            m = jnp.maximum(jnp.maximum(lg[0], lg[1]), lg[2])
            es = [jnp.exp(l - m) for l in lg]
            den = es[0] + es[1] + es[2]
            b = (es[0] * o0[:, ls] + es[1] * so1[c] + es[2] * so2[c]) / den
            cat_ref[:, A_WIDTH + c * LANES:A_WIDTH + (c + 1) * LANES] = b.astype(BF16)
            lt = m + jnp.log(den)
            lt_ref[:, ls] = lt
            slt[c] = lt
        _regroup_out(slt, 0, lt1_ref, r1, tm)
        _regroup_out(slt, 0, lt2_ref, r2, tm)

    blk = _row_spec(tm, B_WIDTH)
    g1, g2 = _regrouped_spec(r1, tm), _regrouped_spec(r2, tm)
    return pl.pallas_call(
        body, grid=(t // tm,), in_specs=[blk, blk, g1, g2, blk, g1, g2],
        out_specs=[_row_spec(tm, A_WIDTH + B_WIDTH), blk, g1, g2],
        out_shape=[SDS((t, A_WIDTH + B_WIDTH), BF16), SDS((t, B_WIDTH), F32), SDS((r1, t // r1, B_WIDTH), F32),
                   SDS((r2, t // r2, B_WIDTH), F32)],
        scratch_shapes=[pltpu.VMEM((CHUNKS, tm, LANES), F32)] * 5,
        name="attn_merge", compiler_params=_params())(a_out, *o_list, *l_list)


def _attn_bwd_prep(dcat, cat):
    t = dcat.shape[0]
    tm = PERM_TILE
    r1, r2 = DIL_RATES[1], DIL_RATES[2]

    def body(d_ref, b_ref, db_ref, dd_ref, db1_ref, dd1_ref, db2_ref, dd2_ref, sdb, sdd):
        seg = _segment_mean_matrix(HEAD_DIM, scale=1.0)
        for c in range(CHUNKS):
            ls = slice(c * LANES, (c + 1) * LANES)
            d = d_ref[:, ls]
            dsum = _segment_dot(d * b_ref[:, ls].astype(F32), seg)
            db_ref[:, ls] = d.astype(BF16)
            dd_ref[:, ls] = dsum
            sdb[c] = d
            sdd[c] = dsum
        _regroup_out(sdb, 0, db1_ref, r1, tm)
        _regroup_out(sdd, 0, dd1_ref, r1, tm)
        _regroup_out(sdb, 0, db2_ref, r2, tm)
        _regroup_out(sdd, 0, dd2_ref, r2, tm)

    right = pl.BlockSpec((tm, B_WIDTH), lambda i: (i, 1))
    blk = _row_spec(tm, B_WIDTH)
    g1, g2 = _regrouped_spec(r1, tm), _regrouped_spec(r2, tm)
    return pl.pallas_call(
        body, grid=(t // tm,), in_specs=[right, right], out_specs=[blk, blk, g1, g1, g2, g2],
        out_shape=[SDS((t, B_WIDTH), BF16), SDS((t, B_WIDTH), F32), SDS((r1, t // r1, B_WIDTH), BF16),
                   SDS((r1, t // r1, B_WIDTH), F32), SDS((r2, t // r2, B_WIDTH), BF16), SDS((r2, t // r2, B_WIDTH), F32)],
        scratch_shapes=[pltpu.VMEM((CHUNKS, tm, LANES), F32)] * 2,
        name="attn_bwd_prep", compiler_params=_params())(dcat, cat)


def _attn_bwd(name, q, k, v, db, lse, dd, ride=None):
    rate, length = db.shape[0], db.shape[1]
    nb = length // ATT_BLOCK
    scale = HEAD_DIM ** -0.5

    def body(qa_ref, qb_ref, k_ref, v_ref, dba_ref, dbb_ref, la_ref, lb_ref, da_ref, dbd_ref, dq_ref, dk_ref, dv_ref, carry):
        m = pl.program_id(1)

        @pl.when(m == 0)
        def _():
            carry[...] = jnp.zeros_like(carry)

        row = lax.broadcasted_iota(jnp.int32, (2 * ATT_BLOCK, ATT_BLOCK), 0)
        kj = lax.broadcasted_iota(jnp.int32, (2 * ATT_BLOCK, ATT_BLOCK), 1)
        no_next = jnp.where(m + 1 < nb, 0, 2 * ATT_BLOCK)
        mask = ((row < ATT_BLOCK) & (kj <= row)) | ((row >= ATT_BLOCK) & (kj >= row - ATT_BLOCK + no_next))
        heads = [(hp, h) for hp in range(CHUNKS) for h in range(2)]
        q2, db2, lse2, dd2, k2, v2 = {}, {}, {}, {}, {}, {}
        for hp in range(CHUNKS):
            ls = slice(hp * LANES, (hp + 1) * LANES)
            k2[hp], v2[hp] = k_ref[:, ls], v_ref[:, ls]
            q2[hp] = jnp.concatenate([qa_ref[:, ls], qb_ref[:, ls]], axis=0)
            db2[hp] = jnp.concatenate([dba_ref[:, ls], dbb_ref[:, ls]], axis=0)
            lse2[hp] = jnp.concatenate([la_ref[:, ls], lb_ref[:, ls]], axis=0)
            dd2[hp] = jnp.concatenate([da_ref[:, ls], dbd_ref[:, ls]], axis=0)
        km, scores, dps = {}, {}, {}
        for hp, h in heads:
            hm = _head_lane_mask(h)
            km[hp, h] = jnp.where(hm, k2[hp], jnp.zeros_like(k2[hp]))
            scores[hp, h] = _dot(q2[hp], km[hp, h], NT) * scale
            dps[hp, h] = _dot(db2[hp], jnp.where(hm, v2[hp], jnp.zeros_like(v2[hp])), NT)
        probs, dss = {}, {}
        for hp, h in heads:
            hm = _head_lane_mask(h)
            lse_col = jnp.max(jnp.where(hm, lse2[hp], NEG_INF), axis=1, keepdims=True)
            dd_col = jnp.max(jnp.where(hm, dd2[hp], NEG_INF), axis=1, keepdims=True)
            p = jnp.where(mask, jnp.exp(scores[hp, h] - lse_col), 0.0)
            probs[hp, h] = p.astype(BF16)
            dss[hp, h] = (p * (dps[hp, h] - dd_col) * scale).astype(BF16)
        for hp in range(CHUNKS):
            ls = slice(hp * LANES, (hp + 1) * LANES)
            dq_acc, dk_acc, dv_acc = None, None, None
            for h in range(2):
                hm = _head_lane_mask(h)
                dvc = _dot(probs[hp, h], jnp.where(hm, db2[hp], jnp.zeros_like(db2[hp])), TN)
                dqc = _dot(dss[hp, h], km[hp, h], NN)
                dkc = _dot(dss[hp, h], jnp.where(hm, q2[hp], jnp.zeros_like(q2[hp])), TN)
                dq_acc = dqc if dq_acc is None else dq_acc + dqc
                dk_acc = dkc if dk_acc is None else dk_acc + dkc
                dv_acc = dvc if dv_acc is None else dv_acc + dvc
            dq_ref[:, ls] = (dq_acc[:ATT_BLOCK] + carry[:, ls]).astype(BF16)
            carry[:, ls] = dq_acc[ATT_BLOCK:]
            dk_ref[:, ls] = dk_acc.astype(BF16)
            dv_ref[:, ls] = dv_acc.astype(BF16)

    def cur(cb):
        return pl.BlockSpec((None, ATT_BLOCK, B_WIDTH), lambda r, n: (r, n, cb))

    def nxt(cb):
        return pl.BlockSpec((None, ATT_BLOCK, B_WIDTH), lambda r, n: (r, jnp.minimum(n + 1, nb - 1), cb))

    out = cur(0)
    return _call(
        body, grid=(rate, nb),
        in_specs=[cur(q[1]), nxt(q[1]), cur(k[1]), cur(v[1]), cur(0), nxt(0), cur(0), nxt(0), cur(0), nxt(0)],
        out_specs=[out, out, out], out_shape=[SDS((rate, length, B_WIDTH), BF16)] * 3,
        scratch_shapes=[pltpu.VMEM((ATT_BLOCK, B_WIDTH), F32)],
        operands=[q[0], q[0], k[0], v[0], db, db, lse, lse, dd, dd], name=name, ride=ride)


AB_IN = 2 * A_WIDTH + 3 * N_DIL * B_WIDTH
ASM_TILE = 256


def _dproj_assemble(proj, d_a, dq, dk, dv, gains, tabs, ride=None):
    t = proj.shape[0]
    n_in = 3 * N_DIL

    def body(p_ref, da_ref, *rest):
        grads = rest[:n_in]
        g_ref, c_ref, s1_ref, s2_ref, o_ref, dg_ref = rest[n_in:n_in + 6]
        scratch = rest[n_in + 6:]

        @pl.when(pl.program_id(0) == 0)
        def _():
            dg_ref[...] = jnp.zeros_like(dg_ref)

        chunk = {}
        k_scr = 0
        for j in range(n_in):
            g = j % N_DIL
            if DIL_RATES[g] == 1:
                for ci in range(CHUNKS):
                    chunk[j, ci] = functools.partial(lambda r, ci: r[:, ci * LANES:(ci + 1) * LANES].astype(F32), grads[j], ci)
            else:
                scr = scratch[k_scr]
                k_scr += 1
                _regroup_in(grads[j], scr, DIL_RATES[g], ASM_TILE)
                for ci in range(CHUNKS):
                    chunk[j, ci] = functools.partial(lambda s, ci: s[ci], scr, ci)

        seg = _segment_mean_matrix(HEAD_DIM)
        c, s1, s2 = c_ref[...], s1_ref[...], s2_ref[...]
        o_ref[:, :2 * A_WIDTH] = da_ref[...]
        for jg in range(2 * N_DIL):
            for ci in range(CHUNKS):
                col = jg * B_WIDTH + ci * LANES
                src = slice(2 * A_WIDTH + col, 2 * A_WIDTH + col + LANES)
                xv = p_ref[:, src].astype(F32)
                r = lax.rsqrt(_segment_dot(xv * xv, seg) + EPS)
                xh = xv * r
                gain = g_ref[:, col:col + LANES]
                do = chunk[jg, ci]()
                dy = do * c + pltpu.roll(do * s1, 8, axis=1) + pltpu.roll(do * s2, LANES - 8, axis=1)
                dg_ref[:, col:col + LANES] += jnp.sum(dy * xh, axis=0, keepdims=True)
                dxh = dy * gain
                o_ref[:, src] = (r * (dxh - xh * _segment_dot(dxh * xh, seg))).astype(BF16)
        v0 = 2 * A_WIDTH + QK_COLS
        for g in range(N_DIL):
            for ci in range(CHUNKS):
                col = v0 + g * B_WIDTH + ci * LANES
                o_ref[:, col:col + LANES] = chunk[2 * N_DIL + g, ci]().astype(BF16)

    specs = [_row_spec(ASM_TILE, B_WIDTH) if r == 1 else _regrouped_spec(r, ASM_TILE) for r in DIL_RATES] * 3
    n_scr = 3 * sum(1 for r in DIL_RATES if r > 1)
    tab = _row_spec(ASM_TILE, LANES)
    return _call(
        body, grid=(t // ASM_TILE,),
        in_specs=[_row_spec(ASM_TILE, AB_IN), _row_spec(ASM_TILE, 2 * A_WIDTH)] + specs
        + [_const_spec((1, QK_COLS)), tab, tab, tab],
        out_specs=[_row_spec(ASM_TILE, AB_IN), _const_spec((1, QK_COLS))],
        out_shape=[SDS((t, AB_IN), BF16), SDS((1, QK_COLS), F32)],
        scratch_shapes=[pltpu.VMEM((CHUNKS, ASM_TILE, LANES), F32)] * n_scr,
        operands=[proj, d_a, *dq, *dk, *dv, gains, *tabs], name="dproj_assemble", ride=ride)


def _fold_heads(dg_lane):
    n = dg_lane.shape[1]

    def body(x_ref, o_ref):
        r = lax.broadcasted_iota(jnp.int32, (B_WIDTH, B_WIDTH), 0) % HEAD_DIM
        c = lax.broadcasted_iota(jnp.int32, (B_WIDTH, B_WIDTH), 1) % HEAD_DIM
        fold = jnp.where(r == c, 1.0, 0.0).astype(F32)
        for jg in range(n // B_WIDTH):
            ls = slice(jg * B_WIDTH, (jg + 1) * B_WIDTH)
            o_ref[:, ls] = _dot_hi(jnp.broadcast_to(x_ref[:, ls], (8, B_WIDTH)), fold)

    return pl.pallas_call(body, out_shape=SDS((8, n), F32), name="fold_heads", compiler_params=_params())(dg_lane)


CD_TILE = 256
TAP_ROWS = 64
CD_IN = 2 * C_WIDTH + 3 * 512


def _shifted_copies(src, dst, rows):
    dst[0, :rows] = src[...]
    for b in range(1, 8):
        dst[b, :rows - 8] = src[pl.ds(b, rows - 8), :]


def _rows_from(shifted, start, n, lanes=slice(None)):
    b = start % 8
    return shifted[b, pl.ds(start - b, n), lanes]


def _mixer_cd_fwd(proj, cw, cb, lg, lb, dw):
    t = proj.shape[0]
    per = CD_TILE // HALO

    def body(h_ref, m_ref, cw_ref, cb_ref, lg_ref, lb_ref, dw_ref, o_ref, c1_ref, c_scr, e_scr, c_sh):
        not_first = (pl.program_id(0) > 0).astype(F32)
        lanes = [slice(c * LANES, (c + 1) * LANES) for c in range(C_WIDTH // LANES)]

        def col(ref, part, ls):
            return ref[:, part * C_WIDTH + ls.start:part * C_WIDTH + ls.stop].astype(F32)

        for ls in lanes:
            c_scr[:HALO, ls] = col(h_ref, 0, ls) * _sigmoid(col(h_ref, 1, ls)) * not_first
            c_scr[HALO:, ls] = col(m_ref, 0, ls) * _sigmoid(col(m_ref, 1, ls))
            e_scr[:HALO, ls] = col(h_ref, 3, ls) * col(h_ref, 4, ls) * not_first
            e_scr[HALO:, ls] = col(m_ref, 3, ls) * col(m_ref, 4, ls)
        _shifted_copies(c_scr, c_sh, HALO + CD_TILE)
        for ls in lanes:
            for r0 in range(0, CD_TILE, TAP_ROWS):
                acc = jnp.zeros((TAP_ROWS, LANES), F32)
                for k in range(C_KERNEL):
                    acc = acc + cw_ref[k:k + 1, ls] * _rows_from(c_sh, r0 + HALO - (C_KERNEL - 1) + k, TAP_ROWS, ls)
                c1_ref[r0:r0 + TAP_ROWS, ls] = acc + cb_ref[:, ls]
        mean = sum(jnp.sum(c1_ref[:, ls], axis=-1, keepdims=True) for ls in lanes) * (1.0 / C_WIDTH)
        var = sum(jnp.sum((c1_ref[:, ls] - mean) ** 2, axis=-1, keepdims=True) for ls in lanes) * (1.0 / C_WIDTH)
        rs = lax.rsqrt(var + EPS)
        for ls in lanes:
            c2 = (c1_ref[:, ls] - mean) * rs * lg_ref[:, ls] + lb_ref[:, ls]
            o_ref[:, ls] = (c2 * _sigmoid(c2)).astype(BF16)
            d1 = jnp.zeros((CD_TILE, LANES), F32)
            for k in range(D_KERNEL):
                d1 = d1 + dw_ref[k:k + 1, ls] * e_scr[pl.ds(HALO - (D_KERNEL - 1) + k, CD_TILE), ls]
            o_ref[:, C_WIDTH + ls.start:C_WIDTH + ls.stop] = (col(m_ref, 2, ls) * d1).astype(BF16)

    return pl.pallas_call(
        body, grid=(t // CD_TILE,),
        in_specs=[pl.BlockSpec((HALO, CD_IN), lambda i: (jnp.maximum(i * per - 1, 0), 0)), _row_spec(CD_TILE, CD_IN),
                  _const_spec((32, C_WIDTH)), _const_spec((1, C_WIDTH)), _const_spec((1, C_WIDTH)), _const_spec((1, C_WIDTH)),
                  _const_spec((8, C_WIDTH))],
        out_specs=[_row_spec(CD_TILE, 2 * C_WIDTH), _row_spec(CD_TILE, C_WIDTH)],
        out_shape=[SDS((t, 2 * C_WIDTH), BF16), SDS((t, C_WIDTH), F32)],
        scratch_shapes=[pltpu.VMEM((HALO + CD_TILE, C_WIDTH), F32)] * 2 + [pltpu.VMEM((8, HALO + CD_TILE, C_WIDTH), F32)],
        name="mixer_cd_fwd", compiler_params=_params())(proj, proj, cw, cb, lg, lb, dw)


def _mixer_cd_bwd(proj, dcat, c1, cw, lg, lb, dw, ride=None):
    t = proj.shape[0]
    per = CD_TILE // HALO
    nt = t // CD_TILE
    ext = CD_TILE + HALO

    def body(hp_ref, m_ref, hn_ref, dm_ref, dn_ref, c1m_ref, c1n_ref, cw_ref, lg_ref, lb_ref, dw_ref,
             dp_ref, dcw_ref, dcb_ref, dlg_ref, dlb_ref, ddw_ref, c_scr, e_scr, dc1_scr, dd1_scr, c_sh, dc1_sh, dcw_acc,
             dvh_scr, vhat_scr):
        i = pl.program_id(0)

        @pl.when(i == 0)
        def _():
            for r in (dcw_acc, dcb_ref, dlg_ref, dlb_ref, ddw_ref):
                r[...] = jnp.zeros_like(r)

        not_first = (i > 0).astype(F32)
        not_last = (i < nt - 1).astype(F32)
        main = slice(HALO, HALO + CD_TILE)
        lanes = [slice(c * LANES, (c + 1) * LANES) for c in range(C_WIDTH // LANES)]

        def col(ref, part, ls):
            return ref[:, part * C_WIDTH + ls.start:part * C_WIDTH + ls.stop].astype(F32)

        for ls in lanes:
            c_scr[:HALO, ls] = col(hp_ref, 0, ls) * _sigmoid(col(hp_ref, 1, ls)) * not_first
            c_scr[main, ls] = col(m_ref, 0, ls) * _sigmoid(col(m_ref, 1, ls))
            c_scr[HALO + CD_TILE:, ls] = col(hn_ref, 0, ls) * _sigmoid(col(hn_ref, 1, ls)) * not_last
            e_scr[:HALO, ls] = col(hp_ref, 3, ls) * col(hp_ref, 4, ls) * not_first
            e_scr[main, ls] = col(m_ref, 3, ls) * col(m_ref, 4, ls)
            e_scr[HALO + CD_TILE:, ls] = col(hn_ref, 3, ls) * col(hn_ref, 4, ls) * not_last
        _shifted_copies(c_scr, c_sh, 2 * HALO + CD_TILE)

        def c1_of(ls):
            return jnp.concatenate([c1m_ref[:, ls], c1n_ref[:, ls]], axis=0)

        mean = sum(jnp.sum(c1_of(ls), axis=-1, keepdims=True) for ls in lanes) * (1.0 / C_WIDTH)
        var = sum(jnp.sum((c1_of(ls) - mean) ** 2, axis=-1, keepdims=True) for ls in lanes) * (1.0 / C_WIDTH)
        rs = lax.rsqrt(var + EPS)
        sum_dvh, sum_dvh_vhat = 0.0, 0.0
        for ls in lanes:
            vhat = (c1_of(ls) - mean) * rs
            c2 = vhat * lg_ref[:, ls] + lb_ref[:, ls]
            sig = _sigmoid(c2)
            dc = jnp.concatenate([dm_ref[:, ls], dn_ref[:, ls] * not_last], axis=0)
            dc2 = dc * (sig * (1.0 + c2 * (1.0 - sig)))
            dvh = dc2 * lg_ref[:, ls]
            sum_dvh = sum_dvh + jnp.sum(dvh, axis=-1, keepdims=True)
            sum_dvh_vhat = sum_dvh_vhat + jnp.sum(dvh * vhat, axis=-1, keepdims=True)
            dvh_scr[:, ls] = dvh
            vhat_scr[:, ls] = vhat
            dlg_ref[:, ls] += jnp.sum((dc2 * vhat)[:CD_TILE], axis=0, keepdims=True)
            dlb_ref[:, ls] += jnp.sum(dc2[:CD_TILE], axis=0, keepdims=True)
        for ls in lanes:
            dc1 = rs * (dvh_scr[:, ls] - sum_dvh * (1.0 / C_WIDTH) - vhat_scr[:, ls] * (sum_dvh_vhat * (1.0 / C_WIDTH)))
            dc1_scr[:, ls] = dc1
            dcb_ref[:, ls] += jnp.sum(dc1[:CD_TILE], axis=0, keepdims=True)
        _shifted_copies(dc1_scr, dc1_sh, ext)
        for ls in lanes:
            for r0 in range(0, CD_TILE, TAP_ROWS):
                rows = slice(r0, r0 + TAP_ROWS)
                dc1_m = dc1_scr[rows, ls]
                dc0 = jnp.zeros((TAP_ROWS, LANES), F32)
                for k in range(C_KERNEL):
                    dc0 = dc0 + cw_ref[k:k + 1, ls] * _rows_from(dc1_sh, r0 + C_KERNEL - 1 - k, TAP_ROWS, ls)
                    prod = dc1_m * _rows_from(c_sh, r0 + HALO - (C_KERNEL - 1) + k, TAP_ROWS, ls)
                    dcw_acc[k, :, ls] += prod.reshape(TAP_ROWS // 8, 8, LANES).sum(axis=0)
                g_m = m_ref[rows, C_WIDTH + ls.start:C_WIDTH + ls.stop].astype(F32)
                a_m = m_ref[rows, ls].astype(F32)
                sig_m = _sigmoid(g_m)
                dp_ref[rows, ls] = (dc0 * sig_m).astype(BF16)
                dp_ref[rows, C_WIDTH + ls.start:C_WIDTH + ls.stop] = (dc0 * a_m * sig_m * (1.0 - sig_m)).astype(BF16)

        @pl.when(i == nt - 1)
        def _():
            dcw_ref[...] = jnp.sum(dcw_acc[...], axis=1)

        for ls in lanes:
            wide = slice(C_WIDTH + ls.start, C_WIDTH + ls.stop)
            d1 = jnp.zeros((CD_TILE, LANES), F32)
            for k in range(D_KERNEL):
                d1 = d1 + dw_ref[k:k + 1, ls] * e_scr[pl.ds(HALO - (D_KERNEL - 1) + k, CD_TILE), ls]
            dd_m = dm_ref[:, wide]
            dd1 = jnp.concatenate([dd_m * col(m_ref, 2, ls), dn_ref[:, wide] * col(hn_ref, 2, ls) * not_last], axis=0)
            dd1_scr[:, ls] = dd1
            dp_ref[:, 2 * C_WIDTH + ls.start:2 * C_WIDTH + ls.stop] = (dd_m * d1).astype(BF16)
            de = jnp.zeros((CD_TILE, LANES), F32)
            for k in range(D_KERNEL):
                de = de + dw_ref[k:k + 1, ls] * dd1_scr[pl.ds(D_KERNEL - 1 - k, CD_TILE), ls]
                ddw_ref[k:k + 1, ls] += jnp.sum(dd1[:CD_TILE] * e_scr[pl.ds(HALO - (D_KERNEL - 1) + k, CD_TILE), ls], axis=0, keepdims=True)
            dp_ref[:, 3 * C_WIDTH + ls.start:3 * C_WIDTH + ls.stop] = (de * col(m_ref, 4, ls)).astype(BF16)
            dp_ref[:, 4 * C_WIDTH + ls.start:4 * C_WIDTH + ls.stop] = (de * col(m_ref, 3, ls)).astype(BF16)

    halo_prev = lambda i: (jnp.maximum(i * per - 1, 0), 0)
    halo_next = lambda i: (jnp.minimum((i + 1) * per, t // HALO - 1), 0)
    vec = _const_spec((1, C_WIDTH))
    return _call(
        body, grid=(nt,),
        in_specs=[pl.BlockSpec((HALO, CD_IN), halo_prev), _row_spec(CD_TILE, CD_IN), pl.BlockSpec((HALO, CD_IN), halo_next),
                  _row_spec(CD_TILE, 2 * C_WIDTH), pl.BlockSpec((HALO, 2 * C_WIDTH), halo_next),
                  _row_spec(CD_TILE, C_WIDTH), pl.BlockSpec((HALO, C_WIDTH), halo_next),
                  _const_spec((32, C_WIDTH)), vec, vec, _const_spec((8, C_WIDTH))],
        out_specs=[_row_spec(CD_TILE, CD_IN), _const_spec((32, C_WIDTH)), vec, vec, vec, _const_spec((8, C_WIDTH))],
        out_shape=[SDS((t, CD_IN), BF16), SDS((32, C_WIDTH), F32), SDS((1, C_WIDTH), F32), SDS((1, C_WIDTH), F32),
                   SDS((1, C_WIDTH), F32), SDS((8, C_WIDTH), F32)],
        scratch_shapes=[pltpu.VMEM((2 * HALO + CD_TILE, C_WIDTH), F32)] * 2 + [pltpu.VMEM((ext, C_WIDTH), F32)] * 2
        + [pltpu.VMEM((8, 2 * HALO + CD_TILE, C_WIDTH), F32), pltpu.VMEM((8, ext, C_WIDTH), F32),
           pltpu.VMEM((32, 8, C_WIDTH), F32)] + [pltpu.VMEM((ext, C_WIDTH), F32)] * 2,
        operands=[proj, proj, proj, dcat, dcat, c1, c1, cw, lg, lb, dw], name="mixer_cd_bwd", ride=ride)


def _wgrad(name, pairs, out_rc, t, ride):
    tk = TILES["wgrad"]
    assert tk == t, "the whole contraction has to fit one grid step"
    r, c = out_rc
    n = len(pairs)

    operands, in_specs, where = [], [], []
    for lhs, lhs_spec, rhs, rhs_spec in pairs:
        at = []
        for array, spec in ((lhs, lhs_spec), (rhs, rhs_spec)):
            seen = [k for k, o in enumerate(operands) if o is array]
            if not seen:
                operands.append(array)
                in_specs.append(spec)
            at.append(seen[0] if seen else len(operands) - 1)
        where.append(at)
    n_in = len(operands)

    def body(*refs):
        ins, out_refs = refs[:n_in], refs[n_in:]
        for j, (lhs_at, rhs_at) in enumerate(where):
            out_refs[j][...] = _dot(ins[lhs_at][...], ins[rhs_at][...], TN).astype(BF16)

    res = _call(body, grid=(N_CHIPS, t // tk), in_specs=in_specs,
                out_specs=[pl.BlockSpec((None, r, c), lambda p, k: (p, 0, 0))] * n,
                out_shape=[SDS((N_CHIPS, r, c), BF16)] * n, operands=operands, name=name, ride=ride)
    outs, ride_res = (res, None) if ride is None else res
    outs = [o.reshape(N_CHIPS, 2, r // 2, c) for o in outs]
    return outs if ride is None else (outs, ride_res)


def _wgrad_col_sharded(name, h, dz_list, three_d, ride=None):
    t, d = h.shape
    tk = TILES["wgrad"]
    n4 = dz_list[0].shape[-1] if three_d else dz_list[0].shape[-1] // N_CHIPS
    hs = pl.BlockSpec((tk, d), lambda p, k: (k, 0))
    zs = pl.BlockSpec((None, tk, n4), lambda p, k: (p, k, 0)) if three_d else pl.BlockSpec((tk, n4), lambda p, k: (k, p))
    return _wgrad(name, [(h, hs, dz, zs) for dz in dz_list], (d, n4), t, ride)


def _wgrad_row_sharded(name, a, g, three_d, ride=None):
    many = isinstance(a, (list, tuple))
    a_list = list(a) if many else [a]
    t, d = g.shape
    tk = TILES["wgrad"]
    k4 = a_list[0].shape[-1] if three_d else a_list[0].shape[-1] // N_CHIPS
    a_spec = pl.BlockSpec((None, tk, k4), lambda p, k: (p, k, 0)) if three_d else pl.BlockSpec((tk, k4), lambda p, k: (k, p))
    gs = pl.BlockSpec((tk, d), lambda p, k: (k, 0))
    res = _wgrad(name, [(a_j, a_spec, g, gs) for a_j in a_list], (k4, d), t, ride)
    if many:
        return res
    return res[0] if ride is None else (res[0][0], res[1])


def _mesh_scalars():
    return jnp.stack([lax.axis_index("c"), 2 * lax.axis_index("x") + lax.axis_index("y")]).astype(jnp.int32)


def _stage_own(name, w, layer, dtype, far_slab=False):
    layers, r, cols = w.shape
    h = r // 2

    def body(s_ref, x_ref, o_ref, *unwritten):
        o_ref[...] = x_ref[...].astype(dtype)

    out_specs = [pl.BlockSpec((None, None, h, cols), lambda i, s: (s[1], i, 0, 0))] + [ANY] * far_slab
    out_shape = [SDS((N_CHIPS, 2, h, cols), dtype)] + [SDS((2, h, cols), dtype)] * far_slab
    res = pl.pallas_call(
        body,
        grid_spec=pltpu.PrefetchScalarGridSpec(
            num_scalar_prefetch=1, grid=(2,),
            in_specs=[pl.BlockSpec((None, h, cols), lambda i, s: (2 * layer + i, 0, 0))], out_specs=out_specs),
        out_shape=out_shape, name=name,
        compiler_params=_params())(_mesh_scalars(), w.reshape(2 * layers, h, cols))
    return tuple(res) if far_slab else res[0]


STAGE_STEPS = 4


def _stage_rest_and_norm(x, g, weights, ride=None):
    t, d = x.shape
    n = len(weights)
    views, in_specs, out_specs, out_shapes = [], [], [], []
    for w, layer in weights:
        layers, r, cols = w.shape
        sub = r // STAGE_STEPS
        views.append(w.reshape(layers * STAGE_STEPS, sub, cols))
        in_specs.append(pl.BlockSpec((None, sub, cols), functools.partial(lambda l, i, s: (STAGE_STEPS * l + i, 0, 0), layer)))
        out_specs.append(pl.BlockSpec((None, None, sub, cols), lambda i, s: (s[1], i // 2, i % 2, 0)))
        out_shapes.append(SDS((N_CHIPS, 2, r // 2, cols), BF16))

    def body(s_ref, x_ref, g_ref, *rest):
        w_refs, h_ref, o_refs = rest[:n], rest[n], rest[n + 1:]
        h_ref[...] = _rms_rows(x_ref[...], g_ref[...]).astype(BF16)
        for w_ref, o_ref in zip(w_refs, o_refs):
            o_ref[...] = w_ref[...].astype(BF16)

    tm = t // STAGE_STEPS
    res = _call(
        body, grid=(STAGE_STEPS,), in_specs=[pl.BlockSpec((tm, d), lambda i, s: (i, 0)), pl.BlockSpec((1, d), lambda i, s: (0, 0))] + in_specs,
        out_specs=[pl.BlockSpec((tm, d), lambda i, s: (i, 0))] + out_specs, out_shape=[SDS((t, d), BF16)] + out_shapes,
        operands=[x, g] + views, name="stage_and_norm", ride=ride, prefetch=_mesh_scalars())
    outs, ride_res = (res, None) if ride is None else res
    result = (outs[0], list(outs[1:]))
    return result if ride is None else (result, ride_res)


def _remote(src, dst, send_sem, recv_sem, device):
    return pltpu.make_async_remote_copy(src, dst, send_sem, recv_sem, device_id=device, device_id_type=MESH)


ALL_PEERS = (0, 1, 2)
NEIGHBOURS = (0, 1)


def _ride_gather_send(bufs, peers=ALL_PEERS):
    n = len(bufs)

    def each(b, sems, act):
        send, recv = sems
        x, y, c, p, others = _position()
        for t in range(n):
            for j in peers:
                qx, qy = others[j]
                act(b[t].at[p, c], b[t].at[2 * qx + qy, c], send.at[t, j], recv.at[t, j], (qx, qy, c))

    def start(ins, b, new, sems):
        each(b, sems, lambda mine, landed, s, r, dev: _remote(mine, mine, s, r, dev).start())

    def finish(ins, b, new, sems):
        def act(mine, landed, s, r, dev):
            _remote(mine, mine, s, r, dev).wait_send()
            _remote(landed, landed, s, r, dev).wait_recv()
        each(b, sems, act)

    return _Ride([], bufs, [], [(n, 3), (n, 3)], start, finish, ["chips"])


def _ride_gather_pass(bufs, peers=ALL_PEERS):
    n = len(bufs)

    def each(b, sems, act):
        send, recv = sems
        x, y, c, p, others = _position()
        for t in range(n):
            for j in peers:
                qx, qy = others[j]
                act(b[t].at[2 * qx + qy, c], b[t].at[2 * qx + qy, 1 - c], send.at[t, j], recv.at[t, j], (x, y, 1 - c))

    def start(ins, b, new, sems):
        each(b, sems, lambda landed, passed, s, r, dev: _remote(landed, landed, s, r, dev).start())

    def finish(ins, b, new, sems):
        def act(landed, passed, s, r, dev):
            _remote(landed, landed, s, r, dev).wait_send()
            _remote(passed, passed, s, r, dev).wait_recv()
        each(b, sems, act)

    return _Ride([], bufs, [], [(n, 3), (n, 3)], start, finish, ["sibling"])


def _ride_gather(bufs, peers=ALL_PEERS):
    send, onward = _ride_gather_send(bufs, peers), _ride_gather_pass(bufs, peers)
    n_send = len(send.sem_shapes)

    def start(ins, b, new, sems):
        send.start(ins, b, new, sems[:n_send])

    def finish(ins, b, new, sems):
        send.finish(ins, b, new, sems[:n_send])
        onward.start(ins, b, new, sems[n_send:])
        onward.finish(ins, b, new, sems[n_send:])

    return _Ride([], bufs, [], send.sem_shapes + onward.sem_shapes, start, finish, ["sibling", "chips"])


def _ride_gather_far(sources, slabs):
    n = len(slabs)

    def hops(ins, b, sems):
        x, y, c, p, others = _position()
        qx, qy = others[2]
        for t in range(n):
            far = (ins[t].at[p, c], b[t].at[c], sems[0].at[t], sems[1].at[t], (qx, qy, c))
            onward = (b[t].at[c], b[t].at[1 - c], sems[2].at[t], sems[3].at[t], (x, y, 1 - c))
            yield far, onward

    def wait(mine, landed, s, r, dev):
        _remote(mine, mine, s, r, dev).wait_send()
        _remote(landed, landed, s, r, dev).wait_recv()

    def start(ins, b, new, sems):
        for (mine, landed, s, r, dev), _ in hops(ins, b, sems):
            _remote(mine, landed, s, r, dev).start()

    def finish(ins, b, new, sems):
        for far, _ in hops(ins, b, sems):
            wait(*far)
        for _, (landed, passed, s, r, dev) in hops(ins, b, sems):
            _remote(landed, landed, s, r, dev).start()
        for _, onward in hops(ins, b, sems):
            wait(*onward)

    return _Ride(sources, slabs, [], [(n,)] * 4, start, finish, ["sibling", "chips"])


def _ride_swap(tensors):
    n = len(tensors)

    def each(ins, new, sems, act):
        send, recv = sems
        x, y, c, _, _ = _position()
        for t in range(n):
            act(_remote(ins[t].at[:, 1 - c], new[t], send.at[t], recv.at[t], (x, y, 1 - c)))

    def start(ins, b, new, sems):
        each(ins, new, sems, lambda cp: cp.start())

    def finish(ins, b, new, sems):
        each(ins, new, sems, lambda cp: cp.wait())

    return _Ride(tensors, [], [SDS((s.shape[0],) + s.shape[2:], s.dtype) for s in tensors], [(n,), (n,)], start, finish,
                 ["sibling"])


def _ride_scatter(tensors, landing):
    n = len(tensors)

    def each(ins, b, sems, act):
        send, recv = sems
        x, y, c, p, others = _position()
        for t in range(n):
            for j, (qx, qy) in enumerate(others):
                q = 2 * qx + qy
                act(ins[t].at[q], b[t].at[p], b[t].at[q], send.at[t, j], recv.at[t, j], (qx, qy, c))

    def start(ins, b, new, sems):
        each(ins, b, sems, lambda src, dst, landed, s, r, dev: _remote(src, dst, s, r, dev).start())

    def finish(ins, b, new, sems):
        def act(src, dst, landed, s, r, dev):
            _remote(src, dst, s, r, dev).wait_send()
            _remote(landed, landed, s, r, dev).wait_recv()
        each(ins, b, sems, act)

    return _Ride(tensors, landing, [], [(n, 3), (n, 3)], start, finish, ["chips"])


def _ride_join(bufs):
    n = len(bufs)

    def each(b, sems, act):
        send, recv = sems
        x, y, c, _, _ = _position()
        for t in range(n):
            act(b[t].at[c], b[t].at[1 - c], send.at[t], recv.at[t], (x, y, 1 - c))

    def start(ins, b, new, sems):
        each(b, sems, lambda mine, theirs, s, r, dev: _remote(mine, mine, s, r, dev).start())

    def finish(ins, b, new, sems):
        def act(mine, theirs, s, r, dev):
            _remote(mine, mine, s, r, dev).wait_send()
            _remote(theirs, theirs, s, r, dev).wait_recv()
        each(b, sems, act)

    return _Ride([], bufs, [], [(n,), (n,)], start, finish, ["sibling"])


def _all_reduce_small(pack, ride=None):
    rows = pack.shape[0]
    n_dev = 2 * N_CHIPS
    n_rb = 0 if ride is None else len(ride.bufs)

    def body(x_ref, *rest):
        o_ref = rest[n_rb]
        r_bufs = rest[n_rb + 1:2 * n_rb + 1]
        land, send, recv = rest[2 * n_rb + 1:2 * n_rb + 4]
        r_sems = rest[2 * n_rb + 4:]
        if ride is not None:
            ride.start([], r_bufs, [], r_sems)
        x, y, c, p, _ = _position()
        me = 2 * p + c
        land[me] = x_ref[...]
        peers = [(dx, dy, dc) for dx in range(2) for dy in range(2) for dc in range(2) if (dx, dy, dc) != (0, 0, 0)]
        for j, (dx, dy, dc) in enumerate(peers):
            _remote(land.at[me], land.at[me], send.at[j], recv.at[j], (x ^ dx, y ^ dy, c ^ dc)).start()
        for j, (dx, dy, dc) in enumerate(peers):
            src = 4 * (x ^ dx) + 2 * (y ^ dy) + (c ^ dc)
            _remote(land.at[me], land.at[me], send.at[j], recv.at[j], (x ^ dx, y ^ dy, c ^ dc)).wait_send()
            _remote(land.at[src], land.at[src], send.at[j], recv.at[j], (x ^ dx, y ^ dy, c ^ dc)).wait_recv()
        acc = land[0]
        for dev in range(1, n_dev):
            acc = acc + land[dev]
        o_ref[...] = acc
        if ride is not None:
            ride.finish([], r_bufs, [], r_sems)

    bufs = [] if ride is None else ride.bufs
    sems = [] if ride is None else [pltpu.SemaphoreType.DMA(s) for s in ride.sem_shapes]
    res = pl.pallas_call(
        body, in_specs=[pl.BlockSpec(memory_space=pltpu.VMEM)] + [ANY] * n_rb,
        out_specs=[pl.BlockSpec(memory_space=pltpu.VMEM)] + [ANY] * n_rb,
        out_shape=[SDS((rows, LANES), F32)] + [SDS(b.shape, b.dtype) for b in bufs],
        scratch_shapes=[pltpu.VMEM((n_dev, rows, LANES), F32), pltpu.SemaphoreType.DMA((n_dev - 1,)),
                        pltpu.SemaphoreType.DMA((n_dev - 1,))] + sems,
        input_output_aliases={1 + j: 1 + j for j in range(n_rb)},
        name="all_reduce_small", compiler_params=_params())(pack, *bufs)
    return res[0], list(res[1:])


def _add_own_half(name, fulls, recvs, out_dtypes):
    n = len(fulls)
    in_specs, out_specs, out_shapes = [], [], []
    for full, dtype in zip(fulls, out_dtypes):
        n4, _, h, cols = full.shape
        in_specs += [pl.BlockSpec((None, None, h, cols), lambda q, s: (q, s[0], 0, 0)),
                     pl.BlockSpec((None, h, cols), lambda q, s: (q, 0, 0))]
        out_specs += [pl.BlockSpec((None, h, cols), lambda q, s: (q, 0, 0)),
                      pl.BlockSpec((None, h, cols), lambda q, s: (s[1], 0, 0))]
        out_shapes += [SDS((n4, h, cols), dtype)] * 2

    def body(s_ref, *refs):
        ins, outs = refs[:2 * n], refs[2 * n:]
        for j in range(n):
            o_ref, own_ref = outs[2 * j], outs[2 * j + 1]
            v = (ins[2 * j][...].astype(F32) + ins[2 * j + 1][...].astype(F32)).astype(o_ref.dtype)
            o_ref[...] = v

            @pl.when(pl.program_id(0) == s_ref[1])
            def _():
                own_ref[...] = v

    res = pl.pallas_call(
        body,
        grid_spec=pltpu.PrefetchScalarGridSpec(num_scalar_prefetch=1, grid=(N_CHIPS,), in_specs=in_specs, out_specs=out_specs),
        out_shape=out_shapes, name=name, compiler_params=_params())(
            _mesh_scalars(), *[a for pair in zip(fulls, recvs) for a in pair])
    return [(res[2 * j], res[2 * j + 1]) for j in range(n)]


def _sum_chips(name, parts_list):
    steps = 4 if all(parts.shape[1] % 64 == 0 for parts in parts_list) else 1
    in_specs, out_specs, out_shapes = [], [], []
    for parts in parts_list:
        n4, h, cols = parts.shape
        in_specs.append(pl.BlockSpec((n4, h // steps, cols), lambda i, s: (0, i, 0)))
        out_specs.append(pl.BlockSpec((None, h // steps, cols), lambda i, s: (s[0], i, 0)))
        out_shapes.append(SDS((2, h, cols), F32))
    n = len(parts_list)

    def body(s_ref, *refs):
        for a_ref, o_ref in zip(refs[:n], refs[n:]):
            acc = a_ref[0].astype(F32)
            for q in range(1, N_CHIPS):
                acc = acc + a_ref[q].astype(F32)
            o_ref[...] = acc

    return pl.pallas_call(
        body,
        grid_spec=pltpu.PrefetchScalarGridSpec(num_scalar_prefetch=1, grid=(steps,), in_specs=in_specs, out_specs=out_specs),
        out_shape=out_shapes, name=name, compiler_params=_params())(_mesh_scalars(), *parts_list)


def _adamw_math(w, g, m, v):
    m2 = ADAM_B1 * m + (1.0 - ADAM_B1) * g
    v2 = ADAM_B2 * v + (1.0 - ADAM_B2) * (g * g)
    m_hat = m2 / (1.0 - ADAM_B1 ** ADAM_STEP)
    v_hat = v2 / (1.0 - ADAM_B2 ** ADAM_STEP)
    delta = -ADAM_LR * (m_hat / (jnp.sqrt(v_hat) + ADAM_EPS) + ADAM_WD * w)
    return delta, m2, v2


ADAMW_STEPS = 8


def _adamw_big(name, ws, g_layers_list, ms, vs):
    layers = ws[0].shape[0]
    n, per_in = len(ws), 3 + layers
    in_specs, out_specs, out_shapes, operands = [], [], [], []
    for w, g_layers, m, v in zip(ws, g_layers_list, ms, vs):
        assert w.shape[0] == layers and w.shape[1] % (8 * ADAMW_STEPS) == 0
        _, rows, cols = w.shape
        tr = rows // ADAMW_STEPS
        blk = pl.BlockSpec((None, tr, cols), lambda l, i: (l, i, 0))
        in_specs += [blk] * 3 + [pl.BlockSpec((tr, cols), lambda l, i: (i, 0))] * layers
        out_specs += [blk] * 4
        out_shapes += [SDS((layers, rows, cols), F32)] * 4
        operands += [w, m, v] + [g.reshape(rows, cols) for g in g_layers]

    def body(*refs):
        ins, outs = refs[:n * per_in], refs[n * per_in:]
        for j in range(n):
            w_ref, m_ref, v_ref = ins[j * per_in:j * per_in + 3]
            g_refs = ins[j * per_in + 3:(j + 1) * per_in]
            g_o, d_o, m_o, v_o = outs[4 * j:4 * j + 4]
            gv = g_refs[0][...]
            for layer in range(1, layers):
                gv = jnp.where(pl.program_id(0) == layer, g_refs[layer][...], gv)
            d, mm, vv = _adamw_math(w_ref[...], gv, m_ref[...], v_ref[...])
            g_o[...] = gv
            d_o[...] = d
            m_o[...] = mm
            v_o[...] = vv

    res = pl.pallas_call(
        body, grid=(layers, ADAMW_STEPS), in_specs=in_specs, out_specs=out_specs, out_shape=out_shapes, name=name,
        compiler_params=_params())(*operands)
    return [tuple(res[4 * j:4 * j + 4]) for j in range(n)]


def _adamw_small(ws, gs, ms, vs):
    n = len(ws)
    flat = []
    for group in (ws, gs, ms, vs):
        flat += [a.reshape(-1, a.shape[-1]) for a in group]

    def body(*refs):
        w_r, g_r, m_r, v_r = refs[:n], refs[n:2 * n], refs[2 * n:3 * n], refs[3 * n:4 * n]
        d_o, m_o, v_o = refs[4 * n:5 * n], refs[5 * n:6 * n], refs[6 * n:7 * n]
        for j in range(n):
            d, mm, vv = _adamw_math(w_r[j][...], g_r[j][...], m_r[j][...], v_r[j][...])
            d_o[j][...] = d
            m_o[j][...] = mm
            v_o[j][...] = vv

    shapes = [SDS(a.shape, F32) for a in flat[:n]]
    outs = pl.pallas_call(body, out_shape=shapes * 3, name="adamw_small", compiler_params=_params())(*flat)
    res = []
    for k in range(3):
        res.append([outs[k * n + j].reshape(ws[j].shape) for j in range(n)])
    return res


BIG = ("ab_w_in", "ab_w_out", "cd_w_in", "cd_w_out", "ffn_w_gate", "ffn_w_up", "ffn_w_down")
BIG_BY_LAYERS = (BIG[:4], BIG[4:])
V_BLOCK = (2 * A_WIDTH + QK_COLS) // B_WIDTH


def _pad_rows(a, rows):
    return jnp.pad(a, ((0, rows - a.shape[0]), (0, 0)))


A_IN, A_OUT, C_IN, C_OUT = ("ab_w_in", 0), ("ab_w_out", 0), ("cd_w_in", 0), ("cd_w_out", 0)
G0, U0, D0 = ("ffn_w_gate", 0), ("ffn_w_up", 0), ("ffn_w_down", 0)
G1, U1, D1 = ("ffn_w_gate", 1), ("ffn_w_up", 1), ("ffn_w_down", 1)
UNITS = (A_IN, A_OUT, G0, U0, D0, C_IN, C_OUT, G1, U1, D1)
ROWS_MINOR = ("ffn_w_gate", "ffn_w_up")
SMALL_SHARDED = ("small", 0)
REPLICATED_UNIT = ("replicated", 0)


class _Exchange:
    def __init__(self, enabled):
        self.enabled = enabled
        self.w, self.grad, self.recv, self.half, self.land, self.done = {}, {}, {}, {}, {}, {}
        self.far = {}

    def full(self, unit):
        b = self.w[unit]
        return b.reshape(N_CHIPS, 1, 2 * b.shape[2], b.shape[3])

    def ride_for(self, phases):
        rides, sinks = [], []
        for kind, units in phases:
            if kind == "send":
                rides.append(_ride_gather_send([self.w[u] for u in units]))
                sinks.append(self.w)
            elif kind == "pass":
                rides.append(_ride_gather_pass([self.w[u] for u in units]))
                sinks.append(self.w)
            elif kind == "gather":
                rides.append(_ride_gather([self.w[u] for u in units]))
                sinks.append(self.w)
            elif kind == "gather_near":
                rides.append(_ride_gather([self.w[u] for u in units], NEIGHBOURS))
                sinks.append(self.w)
            elif kind == "gather_far":
                rides.append(_ride_gather_far([self.w[u] for u in units], [self.far[u] for u in units]))
                sinks.append(self.far)
            elif kind == "swap":
                rides.append(_ride_swap([self.grad[u] for u in units]))
                sinks.append(self.recv)
            elif kind == "scatter":
                rides.append(_ride_scatter([self.half[u] for u in units], [self.land[u] for u in units]))
                sinks.append(self.land)
            else:
                rides.append(_ride_join([self.done[u] for u in units]))
                sinks.append(self.done)
        ride = functools.reduce(_ride_both, rides)

        def settle(res):
            n_bufs = sum(len(r.bufs) for r in rides)
            bufs, new = list(res[:n_bufs]), list(res[n_bufs:])
            for r, sink, (_, units) in zip(rides, sinks, phases):
                vals = [bufs.pop(0) for _ in r.bufs] + [new.pop(0) for _ in r.new_outs]
                for u, v in zip(units, vals):
                    sink[u] = v

        return ride, settle

    def run(self, fn, *args, phases=(), **kw):
        if not self.enabled or not phases:
            return fn(*args, **kw)
        ride, settle = self.ride_for(phases)
        out, res = fn(*args, ride=ride, **kw)
        settle(res)
        return out

    def alone(self, name, phases):
        if self.enabled:
            ride, settle = self.ride_for(phases)
            settle(_run_ride(name, ride))

    def pair_sum(self, units):
        if self.enabled:
            dtypes = [F32 if u in (SMALL_SHARDED, REPLICATED_UNIT) else BF16 for u in units]
            res = _add_own_half(f"pair_sum_{units[0][0]}_{units[0][1]}", [self.grad[u] for u in units],
                                [self.recv[u] for u in units], dtypes)
            for u, (half, land) in zip(units, res):
                self.half[u], self.land[u] = half, land

    def chip_sum(self, units):
        if self.enabled:
            res = _sum_chips(f"chip_sum_{units[0][0]}_{units[0][1]}", [self.land[u] for u in units])
            self.done.update(zip(units, res))


def _local_step(x, target, ex, sp, h0=None):
    t, d = x.shape
    tabs = _rope_tables(t)
    gains = jnp.concatenate([jnp.tile(sp["q_norm_g"][g], HEAD_DIM // 8) for g in range(N_DIL)]
                            + [jnp.tile(sp["k_norm_g"][g], HEAD_DIM // 8) for g in range(N_DIL)]).reshape(1, QK_COLS)
    bias_t = sp["sgu_bias"].T
    cw = _pad_rows(sp["conv_c_w"], 32)
    dw = _pad_rows(sp["conv_d_w"], 8)
    cb, clg, clb = (sp[k].reshape(1, C_WIDTH) for k in ("conv_c_b", "c_ln_g", "c_ln_b"))
    slg, slb = sp["sgu_norm_g"].reshape(1, A_WIDTH), sp["sgu_norm_b"].reshape(1, A_WIDTH)
    g_ab, g_cd = sp["ab_norm_g"].reshape(1, d), sp["cd_norm_g"].reshape(1, d)
    g_f0, g_f1 = sp["ffn_norm_g"][0:1], sp["ffn_norm_g"][1:2]
    run = ex.run

    def w2d(unit):
        return ex.full(unit).reshape(-1, d)

    if h0 is None:
        h0 = _rms_fwd("rms_ab", x, g_ab)
    if ex.enabled:
        proj = run(_proj_in_near, "proj_ab_near", h0, ex.full(A_IN), phases=[("gather_far", [A_IN]), ("send", [A_OUT])])
        proj, ex.w[A_IN] = _proj_in_far("proj_ab_far", h0, ex.far[A_IN], proj, ex.w[A_IN])
    else:
        proj = _proj_in("proj_ab", h0, ex.full(A_IN), 0)
    a_out = _mixer_a_fwd(proj, slg, slb, sp["sgu_w"], bias_t)
    qk, q1, q2, k1, k2 = run(_qk_fwd, proj, gains, tabs, phases=[("pass", [A_OUT]), ("send", [G0, C_OUT])])
    regrouped_qk = {1: (q1, k1), 2: (q2, k2)}
    fwd_phases = ([("pass", [G0, C_OUT]), ("send", [U0])], [("pass", [U0]), ("send", [D0])],
                  [("pass", [D0]), ("send", [C_IN])])
    qkv, o_list, l_list = [], [], []
    dilated = [g for g, rate in enumerate(DIL_RATES) if rate != 1]
    regrouped_v = dict(zip(dilated, _permute("regroup_v", [(proj, V_BLOCK + g, DIL_RATES[g]) for g in dilated])))
    for g, rate in enumerate(DIL_RATES):
        if rate == 1:
            qk3, proj3 = qk.reshape(1, t, QK_COLS), proj.reshape(1, t, AB_IN)
            q, k, v = (qk3, g), (qk3, N_DIL + g), (proj3, V_BLOCK + g)
        else:
            q, k, v = (regrouped_qk[g][0], 0), (regrouped_qk[g][1], 0), (regrouped_v[g], 0)
        qkv.append((q, k, v))
        o, l = run(_attn_fwd, f"attn_fwd_{g}", q, k, v, phases=fwd_phases[g])
        if rate == 1:
            o, l = o.reshape(t, B_WIDTH), l.reshape(t, B_WIDTH)
        o_list.append(o)
        l_list.append(l)
    cat, lse_tot, lse_1, lse_2 = _attn_merge(a_out, o_list, l_list)
    x1, hf0 = _proj_out("out_ab", cat, w2d(A_OUT), x, g_next=g_f0)
    fgate0, fup0, act0 = run(_ffn_in, "ffn_in_0", hf0, ex.full(G0), ex.full(U0), 0,
                           phases=[("pass", [C_IN]), ("send", [D1, G1])])
    x2, h1 = run(_ffn_out, "ffn_out_0", act0, ex.full(D0), 0, x1, g_next=g_cd, phases=[("pass", [D1, G1]), ("send", [U1])])
    projcd = run(_proj_in, "proj_cd", h1, ex.full(C_IN), 0, phases=[("pass", [U1])])
    cat2, c1 = _mixer_cd_fwd(projcd, cw, cb, clg, clb, dw)
    x3, hf1 = _proj_out("out_cd", cat2, w2d(C_OUT), x2, g_next=g_f1)
    fgate1, fup1, act1 = _ffn_in("ffn_in_1", hf1, ex.full(G1), ex.full(U1), 0)
    dy, loss_acc, dy_b = _ffn_out("ffn_out_1", act1, ex.full(D1), 0, x3, target=target)
    loss = 0.5 * loss_acc[0, 0] / d

    late = [D1, G1, U1]
    dgate, dup = _ffn_dact("ffn_dact_1", dy_b, ex.full(D1), 0, fgate1, fup1)
    ex.grad[D1] = _wgrad_row_sharded("wgrad_down_1", act1, dy_b, True)
    ex.grad[G1], ex.grad[U1] = _wgrad_row_sharded("wgrad_gate_up_1", [dgate, dup], hf1, True)
    g3, d_f1, g3_b = run(_dgrad_cols, "dgrad_ffn_1", [dgate, dup], [ex.full(G1), ex.full(U1)], 0, True, x3, g_f1, dy,
                         w_rows=True, phases=[("swap", late)])
    ex.pair_sum(late)

    dcat2 = _dgrad_rows("dgrad_out_cd", g3_b, w2d(C_OUT))
    ex.grad[C_OUT] = _wgrad_row_sharded("wgrad_out_cd", cat2, g3_b, False)
    dprojcd, d_cw, d_cb, d_clg, d_clb, d_dw = run(_mixer_cd_bwd, projcd, dcat2, c1, cw, clg, clb, dw, phases=[("scatter", late)])
    ex.grad[C_IN] = _wgrad_col_sharded("wgrad_in_cd", h1, [dprojcd], False)[0]
    g2, d_cdn, g2_b = run(_dgrad_cols, "dgrad_in_cd", [dprojcd], [ex.full(C_IN)], 0, False, x2, g_cd, g3,
                          phases=[("swap", [C_OUT, C_IN])])
    ex.pair_sum([C_OUT, C_IN])

    dgate, dup = run(_ffn_dact, "ffn_dact_0", g2_b, ex.full(D0), 0, fgate0, fup0, phases=[("scatter", [C_OUT, C_IN])])
    ex.chip_sum(late + [C_OUT, C_IN])
    ex.grad[D0] = _wgrad_row_sharded("wgrad_down_0", act0, g2_b, True)
    ex.grad[G0], ex.grad[U0] = _wgrad_row_sharded("wgrad_gate_up_0", [dgate, dup], hf0, True)
    small = {"cd_norm_g": d_cdn, "conv_c_w": d_cw[:C_KERNEL], "conv_c_b": d_cb, "c_ln_g": d_clg, "c_ln_b": d_clb,
             "conv_d_w": d_dw[:D_KERNEL]}
    ex.grad[SMALL_SHARDED] = _split_full_small(small).reshape(N_CHIPS, 2, SHARDED_ROWS // 2, LANES)
    mid = [D0, G0, U0, SMALL_SHARDED]
    g1, d_f0, g1_b = run(_dgrad_cols, "dgrad_ffn_0", [dgate, dup], [ex.full(G0), ex.full(U0)], 0, True, x1, g_f0, g2,
                         w_rows=True, phases=[("join", late + [C_OUT, C_IN]), ("swap", mid)])
    ex.pair_sum(mid)

    dcat = _dgrad_rows("dgrad_out_ab", g1_b, w2d(A_OUT))
    ex.grad[A_OUT] = _wgrad_row_sharded("wgrad_out_ab", cat, g1_b, False)
    d_a, d_sw, d_sbt, d_slg, d_slb = _mixer_a_bwd(proj, dcat, slg, slb, sp["sgu_w"], bias_t)
    early = {"sgu_norm_g": d_slg, "sgu_norm_b": d_slb, "sgu_w": d_sw, "sgu_bias": d_sbt.T}
    ex.grad[REPLICATED_UNIT] = jnp.broadcast_to(
        _pack_replicated(early, REPLICATED_EARLY, REPLICATED_EARLY_ROWS).reshape(2, REPLICATED_EARLY_ROWS // 2, LANES),
        (N_CHIPS, 2, REPLICATED_EARLY_ROWS // 2, LANES))
    last = [A_OUT, REPLICATED_UNIT]
    dbb, dd, db_1, dd_1, db_2, dd_2 = _attn_bwd_prep(dcat, cat)
    regrouped_bwd = {1: (db_1, lse_1, dd_1), 2: (db_2, lse_2, dd_2)}
    bwd_phases = ([("scatter", [D0, SMALL_SHARDED])],
                  [("scatter", [G0]), ("swap", last)],
                  [("scatter", [U0])])
    dqs, dks, dvs = [], [], []
    for g, rate in enumerate(DIL_RATES):
        q, k, v = qkv[g]
        if rate == 1:
            db3, l3, dd3 = (a.reshape(1, t, B_WIDTH) for a in (dbb, lse_tot, dd))
        else:
            db3, l3, dd3 = regrouped_bwd[g]
        if g == 2:
            ex.pair_sum(last)
        dq, dk, dv = run(_attn_bwd, f"attn_bwd_{g}", q, k, v, db3, l3, dd3, phases=bwd_phases[g])
        if rate == 1:
            dq, dk, dv = (a.reshape(t, B_WIDTH) for a in (dq, dk, dv))
        dqs.append(dq)
        dks.append(dk)
        dvs.append(dv)
    ex.chip_sum([D0, SMALL_SHARDED, G0, U0])
    dproj, d_gains = run(_dproj_assemble, proj, d_a, dqs, dks, dvs, gains, tabs, phases=[("scatter", last), ("join", [D0, SMALL_SHARDED, G0, U0])])
    ex.chip_sum(last)
    d_gains = _fold_heads(d_gains)[0].reshape(2, N_DIL, B_WIDTH)[:, :, :HEAD_DIM]
    ex.grad[A_IN] = _wgrad_col_sharded("wgrad_in_ab", h0, [dproj], False)[0]
    ex.alone("swap_last", [("swap", [A_IN])])
    ex.pair_sum([A_IN])
    gx, d_abn = run(_dgrad_cols, "dgrad_in_ab", [dproj], [ex.full(A_IN)], 0, False, x, g_ab, g1, bf16_copy=False,
                    phases=[("join", last), ("scatter", [A_IN])])
    ex.chip_sum([A_IN])

    small.update({
        "ab_norm_g": d_abn, "sgu_norm_g": d_slg, "sgu_norm_b": d_slb, "sgu_w": d_sw, "sgu_bias": d_sbt.T,
        "q_norm_g": d_gains[0], "k_norm_g": d_gains[1], "ffn_norm_g": jnp.concatenate([d_f0, d_f1], axis=0),
    })
    return loss, gx, small


SHARDED_SMALL = ("cd_norm_g", "conv_c_w", "conv_c_b", "c_ln_g", "c_ln_b", "conv_d_w")
SHARDED_ROWS = 48
REPLICATED_EARLY = ("sgu_norm_g", "sgu_norm_b", "sgu_w", "sgu_bias")
REPLICATED_EARLY_ROWS = 528
REPLICATED_LATE = ("ab_norm_g", "q_norm_g", "k_norm_g", "ffn_norm_g", "loss")
REPLICATED_LATE_ROWS = 32
REPLICATED_SMALL = REPLICATED_EARLY + REPLICATED_LATE[:-1]


def _pack_sharded(parts):
    rows = [parts[k].reshape(-1, LANES) for k in SHARDED_SMALL]
    return _pad_rows(jnp.concatenate(rows, axis=0), SHARDED_ROWS)


def _split_full_small(small):
    per_chip = []
    for q in range(N_CHIPS):
        parts = {}
        for k in SHARDED_SMALL:
            a = small[k]
            a = a.reshape(-1, a.shape[-1])
            n = a.shape[-1] // N_CHIPS
            parts[k] = a[:, q * n:(q + 1) * n]
        per_chip.append(_pack_sharded(parts))
    return jnp.stack(per_chip)


def _unpack_sharded(pack, shapes):
    out, r = {}, 0
    for k in SHARDED_SMALL:
        n = math.prod(shapes[k]) // LANES
        out[k] = pack[r:r + n].reshape(shapes[k])
        r += n
    return out


def _gathered_small(packs, shapes):
    per_chip = [_unpack_sharded(packs[q], shapes) for q in range(N_CHIPS)]
    return {k: jnp.concatenate([pc[k] for pc in per_chip], axis=-1) for k in SHARDED_SMALL}


def _pack_replicated(small, names, total_rows):
    rows = []
    for k in names:
        a = small[k].reshape(-1)
        a = jnp.pad(a, (0, (-a.shape[0]) % LANES))
        rows.append(a.reshape(-1, LANES))
    return _pad_rows(jnp.concatenate(rows, axis=0), total_rows)


def _unpack_replicated(pack, shapes, names):
    out, r = {}, 0
    for k in names:
        size = math.prod(shapes[k])
        n = -(-size // LANES)
        out[k] = pack[r:r + n].reshape(-1)[:size].reshape(shapes[k])
        r += n
    return out


WEIGHT_ORDER = ("ab_norm_g", "ab_w_in", "sgu_norm_g", "sgu_norm_b", "sgu_w", "sgu_bias", "q_norm_g", "k_norm_g", "ab_w_out",
                "cd_norm_g", "cd_w_in", "conv_c_w", "conv_c_b", "c_ln_g", "c_ln_b", "conv_d_w", "cd_w_out", "ffn_norm_g",
                "ffn_w_gate", "ffn_w_up", "ffn_w_down")


def kernel(x, ab_norm_g, ab_w_in, sgu_norm_g, sgu_norm_b, sgu_w, sgu_bias, q_norm_g, k_norm_g, ab_w_out, cd_norm_g, cd_w_in, conv_c_w, conv_c_b, c_ln_g, c_ln_b, conv_d_w, cd_w_out, ffn_norm_g, ffn_w_gate, ffn_w_up, ffn_w_down, loss_target, m_ab_norm_g, m_ab_w_in, m_sgu_norm_g, m_sgu_norm_b, m_sgu_w, m_sgu_bias, m_q_norm_g, m_k_norm_g, m_ab_w_out, m_cd_norm_g, m_cd_w_in, m_conv_c_w, m_conv_c_b, m_c_ln_g, m_c_ln_b, m_conv_d_w, m_cd_w_out, m_ffn_norm_g, m_ffn_w_gate, m_ffn_w_up, m_ffn_w_down, v_ab_norm_g, v_ab_w_in, v_sgu_norm_g, v_sgu_norm_b, v_sgu_w, v_sgu_bias, v_q_norm_g, v_k_norm_g, v_ab_w_out, v_cd_norm_g, v_cd_w_in, v_conv_c_w, v_conv_c_b, v_c_ln_g, v_c_ln_b, v_conv_d_w, v_cd_w_out, v_ffn_norm_g, v_ffn_w_gate, v_ffn_w_up, v_ffn_w_down):
    args = dict(locals())
    ws = {k: args[k] for k in WEIGHT_ORDER}
    ms = {k: args["m_" + k] for k in WEIGHT_ORDER}
    vs = {k: args["v_" + k] for k in WEIGHT_ORDER}
    small_names = [k for k in WEIGHT_ORDER if k not in BIG]
    t, d = x.shape[1:]

    for group in (ws, ms, vs):
        for k in ROWS_MINOR:
            group[k] = jnp.swapaxes(group[k], 1, 2)
    ex = _Exchange(enabled=True)
    ex.w[A_IN], ex.far[A_IN] = _stage_own("stage_ab_w_in", ws["ab_w_in"], 0, BF16, far_slab=True)
    own_small = _pack_sharded({k: ws[k][0] for k in SHARDED_SMALL})
    ex.w[SMALL_SHARDED] = _stage_own("stage_small", own_small[None], 0, F32)
    rest = [u for u in UNITS if u != A_IN]
    x2 = x.reshape(t, d)
    h0, staged = ex.run(_stage_rest_and_norm, x2, ws["ab_norm_g"], [(ws[name], layer) for name, layer in rest],
                        phases=[("gather_near", [A_IN]), ("gather", [SMALL_SHARDED])])
    ex.w.update(zip(rest, staged))
    sp = _gathered_small(ex.w[SMALL_SHARDED].reshape(N_CHIPS, SHARDED_ROWS, LANES), {k: ws[k].shape[1:] for k in SHARDED_SMALL})
    for k in REPLICATED_SMALL:
        sp[k] = ws[k] if k == "ffn_norm_g" else ws[k][0]

    loss, grad_x, g_small = _local_step(x2, loss_target.reshape(t, d), ex, sp, h0)

    shapes = {k: ws[k].shape for k in REPLICATED_SMALL}
    shapes["loss"] = (1,)
    g_small["loss"] = loss
    join_last, settle = ex.ride_for([("join", [A_IN])])
    late, joined = _all_reduce_small(_pack_replicated(g_small, REPLICATED_LATE, REPLICATED_LATE_ROWS), join_last)
    settle(joined)
    grad = _unpack_sharded(ex.done[SMALL_SHARDED].reshape(SHARDED_ROWS, LANES), {k: ws[k].shape for k in SHARDED_SMALL})
    grad.update(_unpack_replicated(ex.done[REPLICATED_UNIT].reshape(REPLICATED_EARLY_ROWS, LANES), shapes, REPLICATED_EARLY))
    grad.update(_unpack_replicated(late, shapes, REPLICATED_LATE))
    loss = grad.pop("loss")[0]

    delta, new_m, new_v = {}, {}, {}
    for group in BIG_BY_LAYERS:
        g_layers = [[ex.done[(k, layer)] for layer in range(ws[k].shape[0])] for k in group]
        results = _adamw_big("adamw_" + group[0], [ws[k] for k in group], g_layers, [ms[k] for k in group], [vs[k] for k in group])
        for k, outs in zip(group, results):
            if k in ROWS_MINOR:
                outs = [jnp.swapaxes(o, 1, 2) for o in outs]
            grad[k], delta[k], new_m[k], new_v[k] = outs
    d_s, m_s, v_s = _adamw_small([ws[k] for k in small_names], [grad[k] for k in small_names],
                                 [ms[k] for k in small_names], [vs[k] for k in small_names])
    for j, k in enumerate(small_names):
        delta[k], new_m[k], new_v[k] = d_s[j], m_s[j], v_s[j]

    return (loss, grad_x[None], *[grad[k] for k in WEIGHT_ORDER], *[delta[k] for k in WEIGHT_ORDER],
            *[new_m[k] for k in WEIGHT_ORDER], *[new_v[k] for k in WEIGHT_ORDER])
```

```python
import functools
import math

import jax
import jax.numpy as jnp
from jax import lax
from jax.experimental import pallas as pl
from jax.experimental.pallas import tpu as pltpu

F32 = jnp.float32
BF16 = jnp.bfloat16
SDS = jax.ShapeDtypeStruct

N_CHIPS = 4
EPS = 1e-6
NEG_INF = -1e30
CHUNK = 128
A_GROUPS = 4
A_WIDTH = 512
N_DIL = 3
DIL_RATES = (1, 4, 16)
HEAD_DIM = 64
B_WIDTH = 512
ROPE_DIM = 16
ROPE_THETA = 500000.0
C_WIDTH = 512
C_KERNEL = 31
D_KERNEL = 3
HALO = 32
ATT_BLOCK = 128
LANES = 128

ADAM_LR = 0.001
ADAM_B1 = 0.9
ADAM_B2 = 0.999
ADAM_EPS = 1e-08
ADAM_WD = 0.01
ADAM_STEP = 10

VMEM_LIMIT = 56 * 1024 * 1024

NN = (((1,), (0,)), ((), ()))
NT = (((1,), (1,)), ((), ()))
TN = (((0,), (0,)), ((), ()))

TILES = {"proj_in": 2048, "proj_out": 1024, "ffn_in": 1024, "ffn_out": 1024, "ffn_dact": 512, "dgrad_cols": 512,
         "dgrad_rows": 1024, "wgrad": 4096}


def _params(sem=None, collective_id=None):
    return pltpu.CompilerParams(dimension_semantics=sem, vmem_limit_bytes=VMEM_LIMIT, collective_id=collective_id)


def _bf(v):
    return v if v.dtype == BF16 else v.astype(BF16)


def _dot(a, b, dims):
    return lax.dot_general(_bf(a), _bf(b), dims, preferred_element_type=F32)


def _dot_hi(a, b):
    return jnp.dot(a, b, precision=lax.Precision.HIGHEST, preferred_element_type=F32)


def _sigmoid(v):
    return 0.5 * jnp.tanh(0.5 * v) + 0.5


def _gelu(v):
    return 0.5 * v * (1.0 + lax.erf(v * (1.0 / math.sqrt(2.0))))


def _gelu_grad(v):
    cdf = 0.5 * (1.0 + lax.erf(v * (1.0 / math.sqrt(2.0))))
    return cdf + v * jnp.exp(-0.5 * v * v) * (1.0 / math.sqrt(2.0 * math.pi))


def _segment_mean_matrix(seg, scale=None):
    r = lax.broadcasted_iota(jnp.int32, (LANES, LANES), 0) // seg
    c = lax.broadcasted_iota(jnp.int32, (LANES, LANES), 1) // seg
    return jnp.where(r == c, (1.0 / seg) if scale is None else scale, 0.0).astype(BF16)


def _segment_dot(v, seg):
    hi = v.astype(BF16)
    lo = (v - hi.astype(F32)).astype(BF16)
    return jnp.dot(hi, seg, preferred_element_type=F32) + jnp.dot(lo, seg, preferred_element_type=F32)


MESH = pl.DeviceIdType.MESH
ANY = pl.BlockSpec(memory_space=pl.ANY)


def _position():
    x, y, c = lax.axis_index("x"), lax.axis_index("y"), lax.axis_index("c")
    others = [(1 - x, y), (x, 1 - y), (1 - x, 1 - y)]
    return x, y, c, 2 * x + y, others


class _Ride:
    def __init__(self, ins, bufs, new_outs, sem_shapes, start, finish, reach):
        self.ins, self.bufs, self.new_outs, self.sem_shapes = list(ins), list(bufs), list(new_outs), list(sem_shapes)
        self.start, self.finish = start, finish
        self.reach = frozenset(reach)

    def entry_barrier(self):
        x, y, c, _, others = _position()
        peers = ([(x, y, 1 - c)] if "sibling" in self.reach else []) + ([(qx, qy, c) for qx, qy in others] if "chips" in self.reach else [])
        barrier = pltpu.get_barrier_semaphore()
        for peer in peers:
            pl.semaphore_signal(barrier, inc=1, device_id=peer, device_id_type=MESH)
        pl.semaphore_wait(barrier, len(peers))

    @property
    def collective_id(self):
        return {frozenset(["sibling"]): 0, frozenset(["chips"]): 1, frozenset(["sibling", "chips"]): 2}[self.reach]


def _ride_both(a, b):
    na = (len(a.ins), len(a.bufs), len(a.new_outs), len(a.sem_shapes))

    def split(ins, bufs, new, sems):
        return ((ins[:na[0]], bufs[:na[1]], new[:na[2]], sems[:na[3]]), (ins[na[0]:], bufs[na[1]:], new[na[2]:], sems[na[3]:]))

    def start(*refs):
        ra, rb = split(*refs)
        a.start(*ra)
        b.start(*rb)

    def finish(*refs):
        ra, rb = split(*refs)
        a.finish(*ra)
        b.finish(*rb)

    return _Ride(a.ins + b.ins, a.bufs + b.bufs, a.new_outs + b.new_outs, a.sem_shapes + b.sem_shapes, start, finish,
                 a.reach | b.reach)


def _call(body, *, grid, in_specs, out_specs, out_shape, operands, name, scratch_shapes=(), aliases=None, ride=None,
          prefetch=None):
    off = 0 if prefetch is None else 1
    lead = [] if prefetch is None else [prefetch]

    params = _params(collective_id=None if ride is None else ride.collective_id)

    def launch(kernel_body, in_specs_, out_specs_, out_shape_, scratch_, aliases_, *args):
        if prefetch is None:
            return pl.pallas_call(kernel_body, grid=grid, in_specs=in_specs_, out_specs=out_specs_, out_shape=out_shape_,
                                  scratch_shapes=scratch_, input_output_aliases=aliases_, name=name,
                                  compiler_params=params)(*args)
        spec = pltpu.PrefetchScalarGridSpec(num_scalar_prefetch=1, grid=grid, in_specs=in_specs_, out_specs=out_specs_,
                                            scratch_shapes=scratch_)
        return pl.pallas_call(kernel_body, grid_spec=spec, out_shape=out_shape_, input_output_aliases=aliases_, name=name,
                              compiler_params=params)(*lead, *args)

    if ride is None:
        return launch(body, list(in_specs), out_specs, out_shape, list(scratch_shapes), dict(aliases or {}), *operands)
    multi = isinstance(out_shape, (list, tuple))
    out_shapes = list(out_shape) if multi else [out_shape]
    o_specs = list(out_specs) if multi else [out_specs]
    n_in, n_out, n_scr = off + len(operands), len(out_shapes), len(scratch_shapes)
    n_ri, n_rb, n_rn = len(ride.ins), len(ride.bufs), len(ride.new_outs)

    def carrying(*refs):
        k = n_in
        r_ins = refs[k:k + n_ri]
        k += n_ri + n_rb
        outs = refs[k:k + n_out]
        k += n_out
        r_bufs = refs[k:k + n_rb]
        k += n_rb
        r_new = refs[k:k + n_rn]
        k += n_rn
        scratch = refs[k:k + n_scr]
        sems = refs[k + n_scr:]
        first, last = None, None
        for axis, size in enumerate(grid):
            pid = pl.program_id(axis)
            first = (pid == 0) if first is None else first & (pid == 0)
            last = (pid == size - 1) if last is None else last & (pid == size - 1)

        @pl.when(first)
        def _():
            ride.entry_barrier()
            ride.start(r_ins, r_bufs, r_new, sems)

        body(*refs[:n_in], *outs, *scratch)

        @pl.when(last)
        def _():
            ride.finish(r_ins, r_bufs, r_new, sems)

    all_aliases = dict(aliases or {})
    for j in range(n_rb):
        all_aliases[n_in + n_ri + j] = n_out + j
    res = launch(
        carrying, list(in_specs) + [ANY] * (n_ri + n_rb), o_specs + [ANY] * (n_rb + n_rn),
        out_shapes + [SDS(b.shape, b.dtype) for b in ride.bufs] + ride.new_outs,
        list(scratch_shapes) + [pltpu.SemaphoreType.DMA(s) for s in ride.sem_shapes], all_aliases,
        *operands, *ride.ins, *ride.bufs)
    outs = res[:n_out]
    return (list(outs) if multi else outs[0]), list(res[n_out:])


def _whole(ref, p):
    return ref[...]


def _slab(ref, p):
    return ref[p]


def _matmul(name, grid, pairs, extras, outs, dims, epi, *, slabs=1, n_acc=1, ride=None):
    n_pairs, n_ex, n_out = len(pairs), len(extras), len(outs)

    def body(*refs):
        ab = refs[:2 * n_pairs]
        ex = refs[2 * n_pairs:2 * n_pairs + n_ex]
        out_refs = refs[2 * n_pairs + n_ex:2 * n_pairs + n_ex + n_out]
        pids = tuple(pl.program_id(a) for a in range(len(grid)))
        parts = [None] * n_acc
        for p in range(slabs):
            for j, (_, _, a_pick, _, _, b_pick, acc) in enumerate(pairs):
                d = _dot(a_pick(ab[2 * j], p), b_pick(ab[2 * j + 1], p), dims)
                parts[acc] = d if parts[acc] is None else parts[acc] + d
        epi(parts, ex, out_refs, pids)

    operands, in_specs = [], []
    for a, a_spec, _, b, b_spec, _, _ in pairs:
        operands += [a, b]
        in_specs += [a_spec, b_spec]
    for e, e_spec in extras:
        operands.append(e)
        in_specs.append(e_spec)
    return _call(body, grid=grid, in_specs=in_specs, out_specs=[o[1] for o in outs], out_shape=[o[0] for o in outs],
                 operands=operands, name=name, ride=ride)


def _rms_rows(v, g):
    r = lax.rsqrt(jnp.mean(v * v, axis=-1, keepdims=True) + EPS)
    return v * r * g


def _rms_fwd(name, x, g):
    t, d = x.shape
    tm = 512

    def body(x_ref, g_ref, o_ref):
        o_ref[...] = _rms_rows(x_ref[...], g_ref[...]).astype(BF16)

    return pl.pallas_call(
        body, grid=(t // tm,),
        in_specs=[pl.BlockSpec((tm, d), lambda i: (i, 0)), pl.BlockSpec((1, d), lambda i: (0, 0))],
        out_specs=pl.BlockSpec((tm, d), lambda i: (i, 0)), out_shape=SDS((t, d), BF16), name=name,
        compiler_params=_params())(x, g)


def _epi_residual_norm(accs, ex, outs, pids):
    x_new = accs[0] + ex[0][...]
    outs[0][...] = x_new
    outs[1][...] = _rms_rows(x_new, ex[1][...]).astype(BF16)


def _epi_residual_loss(accs, ex, outs, pids):
    y = accs[0] + ex[0][...]
    err = y - ex[1][...]
    dy = err * (1.0 / err.shape[-1])
    outs[0][...] = dy
    outs[2][...] = dy.astype(BF16)

    @pl.when(pids[0] == 0)
    def _():
        outs[1][...] = jnp.zeros_like(outs[1])

    outs[1][...] += jnp.sum(err * err)


def _epi_rms_bwd(accs, ex, outs, pids):
    dh = accs[0]
    xv, g, res = ex[0][...], ex[1][...], ex[2][...]
    r = lax.rsqrt(jnp.mean(xv * xv, axis=-1, keepdims=True) + EPS)
    xh = xv * r
    dy = dh * g
    dx = res + r * (dy - xh * jnp.mean(dy * xh, axis=-1, keepdims=True))
    outs[0][...] = dx
    if len(outs) > 2:
        outs[2][...] = dx.astype(BF16)

    @pl.when(pids[0] == 0)
    def _():
        outs[1][...] = jnp.zeros_like(outs[1])

    outs[1][...] += jnp.sum(dh * xh, axis=0, keepdims=True)


def _row_spec(tm, d):
    return pl.BlockSpec((tm, d), lambda i, *_: (i, 0))


def _const_spec(shape):
    nd = len(shape)
    return pl.BlockSpec(shape, lambda *_: (0,) * nd)


def _proj_in(name, h, w, layer, ride=None):
    t, d = h.shape
    n4 = w.shape[-1]
    tm = TILES["proj_in"]

    def epi(accs, ex, outs, pids):
        outs[0][...] = accs[0].astype(BF16)

    res = _matmul(
        name, (N_CHIPS, t // tm),
        [(h, pl.BlockSpec((tm, d), lambda p, i: (i, 0)), _whole,
          w, pl.BlockSpec((None, None, d, n4), lambda p, i: (p, layer, 0, 0)), _whole, 0)],
        [], [(SDS((t, N_CHIPS * n4), BF16), pl.BlockSpec((tm, n4), lambda p, i: (i, p)))],
        NN, epi, ride=ride)
    return res[0] if ride is None else (res[0][0], res[1])


def _proj_in_near(name, h, w, ride=None):
    t, d = h.shape
    n4 = w.shape[-1]
    tm = TILES["proj_in"]

    def shard(j, s):
        return j + (j >= N_CHIPS - 1 - s[1]).astype(jnp.int32)

    def body(s_ref, h_ref, w_ref, o_ref):
        o_ref[...] = _dot(h_ref[...], w_ref[...], NN).astype(BF16)

    return _call(
        body, grid=(N_CHIPS - 1, t // tm),
        in_specs=[pl.BlockSpec((tm, d), lambda j, i, s: (i, 0)),
                  pl.BlockSpec((None, None, d, n4), lambda j, i, s: (shard(j, s), 0, 0, 0))],
        out_specs=pl.BlockSpec((tm, n4), lambda j, i, s: (i, shard(j, s))), out_shape=SDS((t, N_CHIPS * n4), BF16),
        operands=[h, w], name=name, ride=ride, prefetch=_mesh_scalars())


def _proj_in_far(name, h, slab, proj, w):
    t, d = h.shape
    n4 = slab.shape[-1]
    tm = TILES["proj_in"]

    def body(s_ref, h_ref, slab_ref, proj_in, w_in, o_ref, w_ref):
        shard = slab_ref[...]
        o_ref[...] = _dot(h_ref[...], shard, NN).astype(BF16)
        w_ref[...] = shard

    proj, w_full = _call(
        body, grid=(t // tm,),
        in_specs=[pl.BlockSpec((tm, d), lambda i, s: (i, 0)), pl.BlockSpec((d, n4), lambda i, s: (0, 0)), ANY, ANY],
        out_specs=[pl.BlockSpec((tm, n4), lambda i, s: (i, N_CHIPS - 1 - s[1])),
                   pl.BlockSpec((None, d, n4), lambda i, s: (N_CHIPS - 1 - s[1], 0, 0))],
        out_shape=[SDS(proj.shape, BF16), SDS((N_CHIPS, d, n4), BF16)],
        operands=[h, slab.reshape(d, n4), proj, w.reshape(N_CHIPS, d, n4)], aliases={3: 0, 4: 1}, name=name,
        prefetch=_mesh_scalars())
    return proj, w_full.reshape(w.shape)


def _proj_out(name, a, w, x, g_next=None, target=None):
    t, k = a.shape
    d = w.shape[-1]
    tm = TILES["proj_out"]
    if target is None:
        extras = [(x, _row_spec(tm, d)), (g_next, _const_spec((1, d)))]
        outs = [(SDS((t, d), F32), _row_spec(tm, d)), (SDS((t, d), BF16), _row_spec(tm, d))]
        epi = _epi_residual_norm
    else:
        extras = [(x, _row_spec(tm, d)), (target, _row_spec(tm, d))]
        outs = [(SDS((t, d), F32), _row_spec(tm, d)), (SDS((8, LANES), F32), _const_spec((8, LANES))),
                (SDS((t, d), BF16), _row_spec(tm, d))]
        epi = _epi_residual_loss
    return _matmul(name, (t // tm,), [(a, _row_spec(tm, k), _whole, w, _const_spec((k, d)), _whole, 0)], extras, outs, NN, epi)


def _ffn_in(name, h, wg, wu, layer, ride=None):
    t, d = h.shape
    n4 = wg.shape[-2]
    tm = TILES["ffn_in"]

    def epi(accs, ex, outs, pids):
        gate, up = accs
        s = _sigmoid(gate)
        silu = gate * s
        outs[0][...] = (up * (s + silu - silu * s)).astype(BF16)
        outs[1][...] = silu.astype(BF16)
        outs[2][...] = (silu * up).astype(BF16)

    w_spec = pl.BlockSpec((None, None, n4, d), lambda p, i: (p, layer, 0, 0))
    h_spec = pl.BlockSpec((tm, d), lambda p, i: (i, 0))
    o = (SDS((N_CHIPS, t, n4), BF16), pl.BlockSpec((None, tm, n4), lambda p, i: (p, i, 0)))
    return _matmul(name, (N_CHIPS, t // tm),
                   [(h, h_spec, _whole, wg, w_spec, _whole, 0), (h, h_spec, _whole, wu, w_spec, _whole, 1)], [],
                   [o, o, o], NT, epi, n_acc=2, ride=ride)


def _ffn_out(name, act, wd, layer, x, g_next=None, target=None, ride=None):
    _, t, n4 = act.shape
    d = wd.shape[-1]
    tm = TILES["ffn_out"]
    xs = _row_spec(tm, d)
    if target is None:
        extras = [(x, xs), (g_next, _const_spec((1, d)))]
        outs = [(SDS((t, d), F32), xs), (SDS((t, d), BF16), xs)]
        epi = _epi_residual_norm
    else:
        extras = [(x, xs), (target, xs)]
        outs = [(SDS((t, d), F32), xs), (SDS((8, LANES), F32), _const_spec((8, LANES))), (SDS((t, d), BF16), xs)]
        epi = _epi_residual_loss
    return _matmul(
        name, (t // tm,),
        [(act, pl.BlockSpec((N_CHIPS, tm, n4), lambda i: (0, i, 0)), _slab,
          wd, pl.BlockSpec((N_CHIPS, None, n4, d), lambda i: (0, layer, 0, 0)), _slab, 0)],
        extras, outs, NN, epi, slabs=N_CHIPS, ride=ride)


def _ffn_dact(name, g, wd, layer, gate, up, ride=None):
    t, d = g.shape
    n4 = wd.shape[-2]
    tm = TILES["ffn_dact"]

    def body(g_ref, w_ref, gate_ref, up_ref, dgate_ref, dup_ref):
        gv = g_ref[...]
        for p in range(N_CHIPS):
            dact = _dot(gv, w_ref[p], NT)
            dgate_ref[p] = (dact * gate_ref[p].astype(F32)).astype(BF16)
            dup_ref[p] = (dact * up_ref[p].astype(F32)).astype(BF16)

    blk = pl.BlockSpec((N_CHIPS, tm, n4), lambda i: (0, i, 0))
    return _call(
        body, grid=(t // tm,),
        in_specs=[_row_spec(tm, d), pl.BlockSpec((N_CHIPS, None, n4, d), lambda i: (0, layer, 0, 0)), blk, blk],
        out_specs=[blk, blk], out_shape=[SDS((N_CHIPS, t, n4), BF16)] * 2, operands=[g, wd, gate, up], name=name, ride=ride)


def _copy_epi(accs, ex, outs, pids):
    for a, o in zip(accs, outs):
        o[...] = a.astype(o.dtype)


def _dgrad_cols(name, dz_list, w_list, layer, three_d, x, g, res, bf16_copy=True, w_rows=False, ride=None):
    t, d = x.shape
    n4 = w_list[0].shape[-2 if w_rows else -1]
    tm = TILES["dgrad_cols"]
    if three_d:
        zs, z_pick = pl.BlockSpec((N_CHIPS, tm, n4), lambda i: (0, i, 0)), _slab
    else:
        zs, z_pick = _row_spec(tm, N_CHIPS * n4), (lambda ref, p: ref[:, p * n4:(p + 1) * n4])
    ws = pl.BlockSpec((N_CHIPS, None) + ((n4, d) if w_rows else (d, n4)), lambda i: (0, layer, 0, 0))
    xs = _row_spec(tm, d)
    return _matmul(
        name, (t // tm,), [(dz, zs, z_pick, w, ws, _slab, 0) for dz, w in zip(dz_list, w_list)],
        [(x, xs), (g, _const_spec((1, d))), (res, xs)],
        [(SDS((t, d), F32), xs), (SDS((1, d), F32), _const_spec((1, d)))] + ([(SDS((t, d), BF16), xs)] if bf16_copy else []),
        NN if w_rows else NT, _epi_rms_bwd, slabs=N_CHIPS, ride=ride)


def _dgrad_rows(name, g, w):
    t, d = g.shape
    k = w.shape[0]
    tm = TILES["dgrad_rows"]
    return _matmul(name, (t // tm,), [(g, _row_spec(tm, d), _whole, w, _const_spec((k, d)), _whole, 0)], [],
                   [(SDS((t, k), F32), _row_spec(tm, k))], NT, _copy_epi)[0]


A_TILE = 256


def _a_common(p_ref, lg_ref, lb_ref):
    pv = p_ref[...].astype(F32)
    a = _gelu(pv)
    u, v = a[:, :A_WIDTH], a[:, A_WIDTH:]
    vc = v - jnp.mean(v, axis=-1, keepdims=True)
    rs = lax.rsqrt(jnp.mean(vc * vc, axis=-1, keepdims=True) + EPS)
    vhat = vc * rs
    vn = vhat * lg_ref[...] + lb_ref[...]
    return pv, u, vhat, rs, vn.astype(BF16)


def _tril_weights(w_ref, g):
    r = lax.broadcasted_iota(jnp.int32, (CHUNK, CHUNK), 0)
    c = lax.broadcasted_iota(jnp.int32, (CHUNK, CHUNK), 1)
    return jnp.where(c <= r, w_ref[g], 0.0).astype(BF16), c <= r


def _mixer_a_fwd(proj, lg, lb, w, bias_t):
    t = proj.shape[0]

    def body(p_ref, lg_ref, lb_ref, w_ref, bt_ref, o_ref):
        _, u, _, _, vnb = _a_common(p_ref, lg_ref, lb_ref)
        for g in range(A_GROUPS):
            wt, _ = _tril_weights(w_ref, g)
            cs = slice(g * CHUNK, (g + 1) * CHUNK)
            for ch in range(A_TILE // CHUNK):
                rs_ = slice(ch * CHUNK, (ch + 1) * CHUNK)
                mixed = _dot(wt, vnb[rs_, cs], NN) + bt_ref[:, g:g + 1]
                o_ref[rs_, cs] = (u[rs_, cs] * mixed).astype(BF16)

    return pl.pallas_call(
        body, grid=(t // A_TILE,),
        in_specs=[pl.BlockSpec((A_TILE, 2 * A_WIDTH), lambda i: (i, 0)), _const_spec((1, A_WIDTH)),
                  _const_spec((1, A_WIDTH)), _const_spec((A_GROUPS, CHUNK, CHUNK)), _const_spec((CHUNK, A_GROUPS))],
        out_specs=pl.BlockSpec((A_TILE, A_WIDTH), lambda i: (i, 0)), out_shape=SDS((t, A_WIDTH), BF16),
        name="mixer_a_fwd", compiler_params=_params())(proj, lg, lb, w, bias_t)


def _mixer_a_bwd(proj, dcat, lg, lb, w, bias_t):
    t = proj.shape[0]

    def body(p_ref, da_ref, lg_ref, lb_ref, w_ref, bt_ref, dp_ref, dw_ref, dbt_ref, dlg_ref, dlb_ref, du_scr, dvn_scr):
        @pl.when(pl.program_id(0) == 0)
        def _():
            dw_ref[...] = jnp.zeros_like(dw_ref)
            dbt_ref[...] = jnp.zeros_like(dbt_ref)
            dlg_ref[...] = jnp.zeros_like(dlg_ref)
            dlb_ref[...] = jnp.zeros_like(dlb_ref)

        pv, u, vhat, rs, vnb = _a_common(p_ref, lg_ref, lb_ref)
        da = da_ref[...]
        for g in range(A_GROUPS):
            wt, keep = _tril_weights(w_ref, g)
            cs = slice(g * CHUNK, (g + 1) * CHUNK)
            for ch in range(A_TILE // CHUNK):
                rs_ = slice(ch * CHUNK, (ch + 1) * CHUNK)
                vg = vnb[rs_, cs]
                mixed = _dot(wt, vg, NN) + bt_ref[:, g:g + 1]
                du_scr[rs_, cs] = da[rs_, cs] * mixed
                dmx = da[rs_, cs] * u[rs_, cs]
                dw_ref[g] += jnp.where(keep, _dot(dmx, vg, NT), 0.0)
                dvn_scr[rs_, cs] = _dot(wt, dmx, TN)
                dbt_ref[:, g:g + 1] += jnp.sum(dmx, axis=1, keepdims=True)
        dvn = dvn_scr[...]
        dlg_ref[...] += jnp.sum(dvn * vhat, axis=0, keepdims=True)
        dlb_ref[...] += jnp.sum(dvn, axis=0, keepdims=True)
        dvh = dvn * lg_ref[...]
        dv = rs * (dvh - jnp.mean(dvh, axis=-1, keepdims=True) - vhat * jnp.mean(dvh * vhat, axis=-1, keepdims=True))
        gp = _gelu_grad(pv)
        dp_ref[:, :A_WIDTH] = (du_scr[...] * gp[:, :A_WIDTH]).astype(BF16)
        dp_ref[:, A_WIDTH:] = (dv * gp[:, A_WIDTH:]).astype(BF16)

    return pl.pallas_call(
        body, grid=(t // A_TILE,),
        in_specs=[pl.BlockSpec((A_TILE, 2 * A_WIDTH), lambda i: (i, 0)), pl.BlockSpec((A_TILE, A_WIDTH), lambda i: (i, 0)),
                  _const_spec((1, A_WIDTH)), _const_spec((1, A_WIDTH)), _const_spec((A_GROUPS, CHUNK, CHUNK)),
                  _const_spec((CHUNK, A_GROUPS))],
        out_specs=[pl.BlockSpec((A_TILE, 2 * A_WIDTH), lambda i: (i, 0)), _const_spec((A_GROUPS, CHUNK, CHUNK)),
                   _const_spec((CHUNK, A_GROUPS)), _const_spec((1, A_WIDTH)), _const_spec((1, A_WIDTH))],
        out_shape=[SDS((t, 2 * A_WIDTH), BF16), SDS((A_GROUPS, CHUNK, CHUNK), F32), SDS((CHUNK, A_GROUPS), F32),
                   SDS((1, A_WIDTH), F32), SDS((1, A_WIDTH), F32)],
        scratch_shapes=[pltpu.VMEM((A_TILE, A_WIDTH), F32), pltpu.VMEM((A_TILE, A_WIDTH), F32)],
        name="mixer_a_bwd", compiler_params=_params())(proj, dcat, lg, lb, w, bias_t)


def _rope_tables(t):
    half = ROPE_DIM // 2
    inv_freq = ROPE_THETA ** (-jnp.arange(half, dtype=F32) * 2.0 / ROPE_DIM)
    ang = jnp.arange(t, dtype=F32)[:, None] * inv_freq[None, :]
    cos, sin = jnp.cos(ang), jnp.sin(ang)
    one = jnp.ones((t, HEAD_DIM - ROPE_DIM), F32)
    zero = jnp.zeros((t, HEAD_DIM - ROPE_DIM), F32)
    zh = jnp.zeros((t, half), F32)
    c = jnp.concatenate([cos, cos, one], axis=1)
    s1 = jnp.concatenate([-sin, zh, zero], axis=1)
    s2 = jnp.concatenate([zh, sin, zero], axis=1)
    return tuple(jnp.tile(a, (1, LANES // HEAD_DIM)) for a in (c, s1, s2))


QK_TILE = 512
QK_ROWS = 64
QK_COLS = 2 * N_DIL * B_WIDTH


CHUNKS = B_WIDTH // LANES


def _regroup_out(scr, first, out_ref, rate, tile):
    rows = tile // rate
    for rho in range(rate):
        for c in range(CHUNKS):
            out_ref[rho, :, c * LANES:(c + 1) * LANES] = scr[first + c, pl.ds(rho, rows, stride=rate), :].astype(out_ref.dtype)


def _regroup_in(x_ref, scr, rate, tile):
    rows = tile // rate
    for rho in range(rate):
        for c in range(CHUNKS):
            scr[c, pl.ds(rho, rows, stride=rate), :] = x_ref[rho, :, c * LANES:(c + 1) * LANES].astype(F32)


def _regrouped_spec(rate, tile):
    return pl.BlockSpec((rate, tile // rate, B_WIDTH), lambda i, *_: (0, i, 0))


def _qk_fwd(proj, gains, tabs, ride=None):
    t = proj.shape[0]
    col0 = 2 * A_WIDTH // 1024
    r1, r2 = DIL_RATES[1], DIL_RATES[2]

    def body(p_ref, g_ref, c_ref, s1_ref, s2_ref, o_ref, q1_ref, q2_ref, k1_ref, k2_ref, scr):
        seg = _segment_mean_matrix(HEAD_DIM)
        for r0 in range(0, QK_TILE, QK_ROWS):
            rows = slice(r0, r0 + QK_ROWS)
            c, s1, s2 = c_ref[rows, :], s1_ref[rows, :], s2_ref[rows, :]
            for ci in range(1024 // LANES):
                ls = slice(ci * LANES, (ci + 1) * LANES)
                xv = p_ref[rows, ls].astype(F32)
                r = lax.rsqrt(_segment_dot(xv * xv, seg) + EPS)
                y = xv * r * g_ref[:, ls]
                val = y * c + pltpu.roll(y, LANES - 8, axis=1) * s1 + pltpu.roll(y, 8, axis=1) * s2
                o_ref[rows, ls] = val.astype(BF16)
                scr[ci, rows, :] = val

        j = pl.program_id(1)

        @pl.when(j == 0)
        def _():
            _regroup_out(scr, CHUNKS, q1_ref, r1, QK_TILE)

        @pl.when(j == 1)
        def _():
            _regroup_out(scr, 0, q2_ref, r2, QK_TILE)

        @pl.when(j == 2)
        def _():
            _regroup_out(scr, 0, k1_ref, r1, QK_TILE)
            _regroup_out(scr, CHUNKS, k2_ref, r2, QK_TILE)

    tab = pl.BlockSpec((QK_TILE, LANES), lambda i, j: (i, 0))
    g1, g2 = SDS((r1, t // r1, B_WIDTH), BF16), SDS((r2, t // r2, B_WIDTH), BF16)
    s1_, s2_ = _regrouped_spec(r1, QK_TILE), _regrouped_spec(r2, QK_TILE)
    return _call(
        body, grid=(t // QK_TILE, QK_COLS // 1024),
        in_specs=[pl.BlockSpec((QK_TILE, 1024), lambda i, j: (i, col0 + j)), pl.BlockSpec((1, 1024), lambda i, j: (0, j)),
                  tab, tab, tab],
        out_specs=[pl.BlockSpec((QK_TILE, 1024), lambda i, j: (i, j)), s1_, s2_, s1_, s2_],
        out_shape=[SDS((t, QK_COLS), BF16), g1, g2, g1, g2],
        scratch_shapes=[pltpu.VMEM((2 * CHUNKS, QK_TILE, LANES), F32)],
        operands=[proj, gains, *tabs], name="qk_norm_rope_fwd", ride=ride)


PERM_TILE = 512


def _permute(name, items):
    t = items[0][0].shape[0]
    n = len(items)

    def body(*refs):
        scr = refs[-1]
        for x_ref, o_ref, (_, _, rate) in zip(refs[:n], refs[n:2 * n], items):
            for ci in range(CHUNKS):
                scr[ci] = x_ref[:, ci * LANES:(ci + 1) * LANES].astype(F32)
            _regroup_out(scr, 0, o_ref, rate, PERM_TILE)

    return pl.pallas_call(
        body, grid=(t // PERM_TILE,),
        in_specs=[pl.BlockSpec((PERM_TILE, B_WIDTH), functools.partial(lambda cb, i: (i, cb), cb)) for _, cb, _ in items],
        out_specs=[_regrouped_spec(rate, PERM_TILE) for _, _, rate in items],
        out_shape=[SDS((rate, t // rate, B_WIDTH), a.dtype) for a, _, rate in items],
        scratch_shapes=[pltpu.VMEM((CHUNKS, PERM_TILE, LANES), F32)],
        name=name, compiler_params=_params())(*[a for a, _, _ in items])


def _head_lane_mask(h):
    lane = lax.broadcasted_iota(jnp.int32, (1, LANES), 1)
    return (lane < HEAD_DIM) if h == 0 else (lane >= HEAD_DIM)


def _attn_fwd(name, q, k, v, ride=None):
    rate, length = q[0].shape[0], q[0].shape[1]
    nb = length // ATT_BLOCK
    scale = HEAD_DIM ** -0.5

    def body(q_ref, kc_ref, kp_ref, vc_ref, vp_ref, o_ref, l_ref):
        n = pl.program_id(1)
        qi = lax.broadcasted_iota(jnp.int32, (ATT_BLOCK, 2 * ATT_BLOCK), 0)
        cj = lax.broadcasted_iota(jnp.int32, (ATT_BLOCK, 2 * ATT_BLOCK), 1)
        has_prev = jnp.where(n > 0, 0, 2 * ATT_BLOCK)
        mask = ((cj < ATT_BLOCK) & (cj >= qi + has_prev)) | ((cj >= ATT_BLOCK) & (cj - ATT_BLOCK <= qi))
        heads = [(hp, h) for hp in range(CHUNKS) for h in range(2)]
        q2, k2, v2 = {}, {}, {}
        for hp in range(CHUNKS):
            ls = slice(hp * LANES, (hp + 1) * LANES)
            q2[hp] = q_ref[:, ls]
            k2[hp] = jnp.concatenate([kp_ref[:, ls], kc_ref[:, ls]], axis=0)
            v2[hp] = jnp.concatenate([vp_ref[:, ls], vc_ref[:, ls]], axis=0)
        scores = {}
        for hp, h in heads:
            scores[hp, h] = _dot(jnp.where(_head_lane_mask(h), q2[hp], jnp.zeros_like(q2[hp])), k2[hp], NT) * scale
        probs, lses = {}, {}
        for hp, h in heads:
            s = jnp.where(mask, scores[hp, h], NEG_INF)
            m = jnp.max(s, axis=1, keepdims=True)
            p = jnp.exp(s - m)
            den = jnp.sum(p, axis=1, keepdims=True)
            lses[hp, h] = m + jnp.log(den)
            probs[hp, h] = (p / den).astype(BF16)
        for hp in range(CHUNKS):
            ls = slice(hp * LANES, (hp + 1) * LANES)
            o_acc = None
            for h in range(2):
                o = _dot(probs[hp, h], jnp.where(_head_lane_mask(h), v2[hp], jnp.zeros_like(v2[hp])), NN)
                o_acc = o if o_acc is None else o_acc + o
            o_ref[:, ls] = o_acc
            zeros = jnp.zeros((ATT_BLOCK, LANES), F32)
            l_ref[:, ls] = jnp.where(_head_lane_mask(1), lses[hp, 1] + zeros, lses[hp, 0] + zeros)

    def cur(cb):
        return pl.BlockSpec((None, ATT_BLOCK, B_WIDTH), lambda r, n: (r, n, cb))

    def prev(cb):
        return pl.BlockSpec((None, ATT_BLOCK, B_WIDTH), lambda r, n: (r, jnp.maximum(n - 1, 0), cb))

    out = pl.BlockSpec((None, ATT_BLOCK, B_WIDTH), lambda r, n: (r, n, 0))
    return _call(
        body, grid=(rate, nb),
        in_specs=[cur(q[1]), cur(k[1]), prev(k[1]), cur(v[1]), prev(v[1])],
        out_specs=[out, out], out_shape=[SDS((rate, length, B_WIDTH), F32)] * 2,
        operands=[q[0], k[0], k[0], v[0], v[0]], name=name, ride=ride)


def _attn_merge(a_out, o_list, l_list):
    t = a_out.shape[0]
    tm = PERM_TILE
    r1, r2 = DIL_RATES[1], DIL_RATES[2]

    def body(a_ref, o0, o1, o2, l0, l1, l2, cat_ref, lt_ref, lt1_ref, lt2_ref, so1, so2, sl1, sl2, slt):
        _regroup_in(o1, so1, r1, tm)
        _regroup_in(l1, sl1, r1, tm)
        _regroup_in(o2, so2, r2, tm)
        _regroup_in(l2, sl2, r2, tm)
        cat_ref[:, :A_WIDTH] = a_ref[...]
        for c in range(CHUNKS):
            ls = slice(c * LANES, (c + 1) * LANES)
            lg = [l0[:, ls], sl1[c], sl2[c]]
            m = jnp.maximum(jnp.maximum(lg[0], lg[1]), lg[2])
            es = [jnp.exp(l - m) for l in lg]
            den = es[0] + es[1] + es[2]
            b = (es[0] * o0[:, ls] + es[1] * so1[c] + es[2] * so2[c]) / den
            cat_ref[:, A_WIDTH + c * LANES:A_WIDTH + (c + 1) * LANES] = b.astype(BF16)
            lt = m + jnp.log(den)
            lt_ref[:, ls] = lt
            slt[c] = lt
        _regroup_out(slt, 0, lt1_ref, r1, tm)
        _regroup_out(slt, 0, lt2_ref, r2, tm)

    blk = _row_spec(tm, B_WIDTH)
    g1, g2 = _regrouped_spec(r1, tm), _regrouped_spec(r2, tm)
    return pl.pallas_call(
        body, grid=(t // tm,), in_specs=[blk, blk, g1, g2, blk, g1, g2],
        out_specs=[_row_spec(tm, A_WIDTH + B_WIDTH), blk, g1, g2],
        out_shape=[SDS((t, A_WIDTH + B_WIDTH), BF16), SDS((t, B_WIDTH), F32), SDS((r1, t // r1, B_WIDTH), F32),
                   SDS((r2, t // r2, B_WIDTH), F32)],
        scratch_shapes=[pltpu.VMEM((CHUNKS, tm, LANES), F32)] * 5,
        name="attn_merge", compiler_params=_params())(a_out, *o_list, *l_list)


def _attn_bwd_prep(dcat, cat):
    t = dcat.shape[0]
    tm = PERM_TILE
    r1, r2 = DIL_RATES[1], DIL_RATES[2]

    def body(d_ref, b_ref, db_ref, dd_ref, db1_ref, dd1_ref, db2_ref, dd2_ref, sdb, sdd):
        seg = _segment_mean_matrix(HEAD_DIM, scale=1.0)
        for c in range(CHUNKS):
            ls = slice(c * LANES, (c + 1) * LANES)
            d = d_ref[:, ls]
            dsum = _segment_dot(d * b_ref[:, ls].astype(F32), seg)
            db_ref[:, ls] = d.astype(BF16)
            dd_ref[:, ls] = dsum
            sdb[c] = d
            sdd[c] = dsum
        _regroup_out(sdb, 0, db1_ref, r1, tm)
        _regroup_out(sdd, 0, dd1_ref, r1, tm)
        _regroup_out(sdb, 0, db2_ref, r2, tm)
        _regroup_out(sdd, 0, dd2_ref, r2, tm)

    right = pl.BlockSpec((tm, B_WIDTH), lambda i: (i, 1))
    blk = _row_spec(tm, B_WIDTH)
    g1, g2 = _regrouped_spec(r1, tm), _regrouped_spec(r2, tm)
    return pl.pallas_call(
        body, grid=(t // tm,), in_specs=[right, right], out_specs=[blk, blk, g1, g1, g2, g2],
        out_shape=[SDS((t, B_WIDTH), BF16), SDS((t, B_WIDTH), F32), SDS((r1, t // r1, B_WIDTH), BF16),
                   SDS((r1, t // r1, B_WIDTH), F32), SDS((r2, t // r2, B_WIDTH), BF16), SDS((r2, t // r2, B_WIDTH), F32)],
        scratch_shapes=[pltpu.VMEM((CHUNKS, tm, LANES), F32)] * 2,
        name="attn_bwd_prep", compiler_params=_params())(dcat, cat)


def _attn_bwd(name, q, k, v, db, lse, dd, ride=None):
    rate, length = db.shape[0], db.shape[1]
    nb = length // ATT_BLOCK
    scale = HEAD_DIM ** -0.5

    def body(qa_ref, qb_ref, k_ref, v_ref, dba_ref, dbb_ref, la_ref, lb_ref, da_ref, dbd_ref, dq_ref, dk_ref, dv_ref, carry):
        m = pl.program_id(1)

        @pl.when(m == 0)
        def _():
            carry[...] = jnp.zeros_like(carry)

        row = lax.broadcasted_iota(jnp.int32, (2 * ATT_BLOCK, ATT_BLOCK), 0)
        kj = lax.broadcasted_iota(jnp.int32, (2 * ATT_BLOCK, ATT_BLOCK), 1)
        no_next = jnp.where(m + 1 < nb, 0, 2 * ATT_BLOCK)
        mask = ((row < ATT_BLOCK) & (kj <= row)) | ((row >= ATT_BLOCK) & (kj >= row - ATT_BLOCK + no_next))
        heads = [(hp, h) for hp in range(CHUNKS) for h in range(2)]
        q2, db2, lse2, dd2, k2, v2 = {}, {}, {}, {}, {}, {}
        for hp in range(CHUNKS):
            ls = slice(hp * LANES, (hp + 1) * LANES)
            k2[hp], v2[hp] = k_ref[:, ls], v_ref[:, ls]
            q2[hp] = jnp.concatenate([qa_ref[:, ls], qb_ref[:, ls]], axis=0)
            db2[hp] = jnp.concatenate([dba_ref[:, ls], dbb_ref[:, ls]], axis=0)
            lse2[hp] = jnp.concatenate([la_ref[:, ls], lb_ref[:, ls]], axis=0)
            dd2[hp] = jnp.concatenate([da_ref[:, ls], dbd_ref[:, ls]], axis=0)
        km, scores, dps = {}, {}, {}
        for hp, h in heads:
            hm = _head_lane_mask(h)
            km[hp, h] = jnp.where(hm, k2[hp], jnp.zeros_like(k2[hp]))
            scores[hp, h] = _dot(q2[hp], km[hp, h], NT) * scale
            dps[hp, h] = _dot(db2[hp], jnp.where(hm, v2[hp], jnp.zeros_like(v2[hp])), NT)
        probs, dss = {}, {}
        for hp, h in heads:
            hm = _head_lane_mask(h)
            lse_col = jnp.max(jnp.where(hm, lse2[hp], NEG_INF), axis=1, keepdims=True)
            dd_col = jnp.max(jnp.where(hm, dd2[hp], NEG_INF), axis=1, keepdims=True)
            p = jnp.where(mask, jnp.exp(scores[hp, h] - lse_col), 0.0)
            probs[hp, h] = p.astype(BF16)
            dss[hp, h] = (p * (dps[hp, h] - dd_col) * scale).astype(BF16)
        for hp in range(CHUNKS):
            ls = slice(hp * LANES, (hp + 1) * LANES)
            dq_acc, dk_acc, dv_acc = None, None, None
            for h in range(2):
                hm = _head_lane_mask(h)
                dvc = _dot(probs[hp, h], jnp.where(hm, db2[hp], jnp.zeros_like(db2[hp])), TN)
                dqc = _dot(dss[hp, h], km[hp, h], NN)
                dkc = _dot(dss[hp, h], jnp.where(hm, q2[hp], jnp.zeros_like(q2[hp])), TN)
                dq_acc = dqc if dq_acc is None else dq_acc + dqc
                dk_acc = dkc if dk_acc is None else dk_acc + dkc
                dv_acc = dvc if dv_acc is None else dv_acc + dvc
            dq_ref[:, ls] = (dq_acc[:ATT_BLOCK] + carry[:, ls]).astype(BF16)
            carry[:, ls] = dq_acc[ATT_BLOCK:]
            dk_ref[:, ls] = dk_acc.astype(BF16)
            dv_ref[:, ls] = dv_acc.astype(BF16)

    def cur(cb):
        return pl.BlockSpec((None, ATT_BLOCK, B_WIDTH), lambda r, n: (r, n, cb))

    def nxt(cb):
        return pl.BlockSpec((None, ATT_BLOCK, B_WIDTH), lambda r, n: (r, jnp.minimum(n + 1, nb - 1), cb))

    out = cur(0)
    return _call(
        body, grid=(rate, nb),
        in_specs=[cur(q[1]), nxt(q[1]), cur(k[1]), cur(v[1]), cur(0), nxt(0), cur(0), nxt(0), cur(0), nxt(0)],
        out_specs=[out, out, out], out_shape=[SDS((rate, length, B_WIDTH), BF16)] * 3,
        scratch_shapes=[pltpu.VMEM((ATT_BLOCK, B_WIDTH), F32)],
        operands=[q[0], q[0], k[0], v[0], db, db, lse, lse, dd, dd], name=name, ride=ride)


AB_IN = 2 * A_WIDTH + 3 * N_DIL * B_WIDTH
ASM_TILE = 256


def _dproj_assemble(proj, d_a, dq, dk, dv, gains, tabs, ride=None):
    t = proj.shape[0]
    n_in = 3 * N_DIL

    def body(p_ref, da_ref, *rest):
        grads = rest[:n_in]
        g_ref, c_ref, s1_ref, s2_ref, o_ref, dg_ref = rest[n_in:n_in + 6]
        scratch = rest[n_in + 6:]

        @pl.when(pl.program_id(0) == 0)
        def _():
            dg_ref[...] = jnp.zeros_like(dg_ref)

        chunk = {}
        k_scr = 0
        for j in range(n_in):
            g = j % N_DIL
            if DIL_RATES[g] == 1:
                for ci in range(CHUNKS):
                    chunk[j, ci] = functools.partial(lambda r, ci: r[:, ci * LANES:(ci + 1) * LANES].astype(F32), grads[j], ci)
            else:
                scr = scratch[k_scr]
                k_scr += 1
                _regroup_in(grads[j], scr, DIL_RATES[g], ASM_TILE)
                for ci in range(CHUNKS):
                    chunk[j, ci] = functools.partial(lambda s, ci: s[ci], scr, ci)

        seg = _segment_mean_matrix(HEAD_DIM)
        c, s1, s2 = c_ref[...], s1_ref[...], s2_ref[...]
        o_ref[:, :2 * A_WIDTH] = da_ref[...]
        for jg in range(2 * N_DIL):
            for ci in range(CHUNKS):
                col = jg * B_WIDTH + ci * LANES
                src = slice(2 * A_WIDTH + col, 2 * A_WIDTH + col + LANES)
                xv = p_ref[:, src].astype(F32)
                r = lax.rsqrt(_segment_dot(xv * xv, seg) + EPS)
                xh = xv * r
                gain = g_ref[:, col:col + LANES]
                do = chunk[jg, ci]()
                dy = do * c + pltpu.roll(do * s1, 8, axis=1) + pltpu.roll(do * s2, LANES - 8, axis=1)
                dg_ref[:, col:col + LANES] += jnp.sum(dy * xh, axis=0, keepdims=True)
                dxh = dy * gain
                o_ref[:, src] = (r * (dxh - xh * _segment_dot(dxh * xh, seg))).astype(BF16)
        v0 = 2 * A_WIDTH + QK_COLS
        for g in range(N_DIL):
            for ci in range(CHUNKS):
                col = v0 + g * B_WIDTH + ci * LANES
                o_ref[:, col:col + LANES] = chunk[2 * N_DIL + g, ci]().astype(BF16)

    specs = [_row_spec(ASM_TILE, B_WIDTH) if r == 1 else _regrouped_spec(r, ASM_TILE) for r in DIL_RATES] * 3
    n_scr = 3 * sum(1 for r in DIL_RATES if r > 1)
    tab = _row_spec(ASM_TILE, LANES)
    return _call(
        body, grid=(t // ASM_TILE,),
        in_specs=[_row_spec(ASM_TILE, AB_IN), _row_spec(ASM_TILE, 2 * A_WIDTH)] + specs
        + [_const_spec((1, QK_COLS)), tab, tab, tab],
        out_specs=[_row_spec(ASM_TILE, AB_IN), _const_spec((1, QK_COLS))],
        out_shape=[SDS((t, AB_IN), BF16), SDS((1, QK_COLS), F32)],
        scratch_shapes=[pltpu.VMEM((CHUNKS, ASM_TILE, LANES), F32)] * n_scr,
        operands=[proj, d_a, *dq, *dk, *dv, gains, *tabs], name="dproj_assemble", ride=ride)


def _fold_heads(dg_lane, ride=None):
    n = dg_lane.shape[1]

    def body(x_ref, o_ref):
        r = lax.broadcasted_iota(jnp.int32, (B_WIDTH, B_WIDTH), 0) % HEAD_DIM
        c = lax.broadcasted_iota(jnp.int32, (B_WIDTH, B_WIDTH), 1) % HEAD_DIM
        fold = jnp.where(r == c, 1.0, 0.0).astype(F32)
        for jg in range(n // B_WIDTH):
            ls = slice(jg * B_WIDTH, (jg + 1) * B_WIDTH)
            o_ref[:, ls] = _dot_hi(jnp.broadcast_to(x_ref[:, ls], (8, B_WIDTH)), fold)

    return _call(body, grid=(1,), in_specs=[_const_spec((1, n))], out_specs=_const_spec((8, n)), out_shape=SDS((8, n), F32),
                 operands=[dg_lane], name="fold_heads", ride=ride)


CD_TILE = 256
TAP_ROWS = 64
CD_IN = 2 * C_WIDTH + 3 * 512


def _shifted_copies(src, dst, rows):
    dst[0, :rows] = src[...]
    for b in range(1, 8):
        dst[b, :rows - 8] = src[pl.ds(b, rows - 8), :]


def _rows_from(shifted, start, n, lanes=slice(None)):
    b = start % 8
    return shifted[b, pl.ds(start - b, n), lanes]


def _mixer_cd_fwd(proj, cw, cb, lg, lb, dw):
    t = proj.shape[0]
    per = CD_TILE // HALO

    def body(h_ref, m_ref, cw_ref, cb_ref, lg_ref, lb_ref, dw_ref, o_ref, c1_ref, c_scr, e_scr, c_sh):
        not_first = (pl.program_id(0) > 0).astype(F32)
        lanes = [slice(c * LANES, (c + 1) * LANES) for c in range(C_WIDTH // LANES)]

        def col(ref, part, ls):
            return ref[:, part * C_WIDTH + ls.start:part * C_WIDTH + ls.stop].astype(F32)

        for ls in lanes:
            c_scr[:HALO, ls] = col(h_ref, 0, ls) * _sigmoid(col(h_ref, 1, ls)) * not_first
            c_scr[HALO:, ls] = col(m_ref, 0, ls) * _sigmoid(col(m_ref, 1, ls))
            e_scr[:HALO, ls] = col(h_ref, 3, ls) * col(h_ref, 4, ls) * not_first
            e_scr[HALO:, ls] = col(m_ref, 3, ls) * col(m_ref, 4, ls)
        _shifted_copies(c_scr, c_sh, HALO + CD_TILE)
        for ls in lanes:
            for r0 in range(0, CD_TILE, TAP_ROWS):
                acc = jnp.zeros((TAP_ROWS, LANES), F32)
                for k in range(C_KERNEL):
                    acc = acc + cw_ref[k:k + 1, ls] * _rows_from(c_sh, r0 + HALO - (C_KERNEL - 1) + k, TAP_ROWS, ls)
                c1_ref[r0:r0 + TAP_ROWS, ls] = acc + cb_ref[:, ls]
        mean = sum(jnp.sum(c1_ref[:, ls], axis=-1, keepdims=True) for ls in lanes) * (1.0 / C_WIDTH)
        var = sum(jnp.sum((c1_ref[:, ls] - mean) ** 2, axis=-1, keepdims=True) for ls in lanes) * (1.0 / C_WIDTH)
        rs = lax.rsqrt(var + EPS)
        for ls in lanes:
            c2 = (c1_ref[:, ls] - mean) * rs * lg_ref[:, ls] + lb_ref[:, ls]
            o_ref[:, ls] = (c2 * _sigmoid(c2)).astype(BF16)
            d1 = jnp.zeros((CD_TILE, LANES), F32)
            for k in range(D_KERNEL):
                d1 = d1 + dw_ref[k:k + 1, ls] * e_scr[pl.ds(HALO - (D_KERNEL - 1) + k, CD_TILE), ls]
            o_ref[:, C_WIDTH + ls.start:C_WIDTH + ls.stop] = (col(m_ref, 2, ls) * d1).astype(BF16)

    return pl.pallas_call(
        body, grid=(t // CD_TILE,),
        in_specs=[pl.BlockSpec((HALO, CD_IN), lambda i: (jnp.maximum(i * per - 1, 0), 0)), _row_spec(CD_TILE, CD_IN),
                  _const_spec((32, C_WIDTH)), _const_spec((1, C_WIDTH)), _const_spec((1, C_WIDTH)), _const_spec((1, C_WIDTH)),
                  _const_spec((8, C_WIDTH))],
        out_specs=[_row_spec(CD_TILE, 2 * C_WIDTH), _row_spec(CD_TILE, C_WIDTH)],
        out_shape=[SDS((t, 2 * C_WIDTH), BF16), SDS((t, C_WIDTH), F32)],
        scratch_shapes=[pltpu.VMEM((HALO + CD_TILE, C_WIDTH), F32)] * 2 + [pltpu.VMEM((8, HALO + CD_TILE, C_WIDTH), F32)],
        name="mixer_cd_fwd", compiler_params=_params())(proj, proj, cw, cb, lg, lb, dw)


def _mixer_cd_bwd(proj, dcat, c1, cw, lg, lb, dw, ride=None):
    t = proj.shape[0]
    per = CD_TILE // HALO
    nt = t // CD_TILE
    ext = CD_TILE + HALO

    def body(hp_ref, m_ref, hn_ref, dm_ref, dn_ref, c1m_ref, c1n_ref, cw_ref, lg_ref, lb_ref, dw_ref,
             dp_ref, dcw_ref, dcb_ref, dlg_ref, dlb_ref, ddw_ref, c_scr, e_scr, dc1_scr, dd1_scr, c_sh, dc1_sh, dcw_acc,
             dvh_scr, vhat_scr):
        i = pl.program_id(0)

        @pl.when(i == 0)
        def _():
            for r in (dcw_acc, dcb_ref, dlg_ref, dlb_ref, ddw_ref):
                r[...] = jnp.zeros_like(r)

        not_first = (i > 0).astype(F32)
        not_last = (i < nt - 1).astype(F32)
        main = slice(HALO, HALO + CD_TILE)
        lanes = [slice(c * LANES, (c + 1) * LANES) for c in range(C_WIDTH // LANES)]

        def col(ref, part, ls):
            return ref[:, part * C_WIDTH + ls.start:part * C_WIDTH + ls.stop].astype(F32)

        for ls in lanes:
            c_scr[:HALO, ls] = col(hp_ref, 0, ls) * _sigmoid(col(hp_ref, 1, ls)) * not_first
            c_scr[main, ls] = col(m_ref, 0, ls) * _sigmoid(col(m_ref, 1, ls))
            c_scr[HALO + CD_TILE:, ls] = col(hn_ref, 0, ls) * _sigmoid(col(hn_ref, 1, ls)) * not_last
            e_scr[:HALO, ls] = col(hp_ref, 3, ls) * col(hp_ref, 4, ls) * not_first
            e_scr[main, ls] = col(m_ref, 3, ls) * col(m_ref, 4, ls)
            e_scr[HALO + CD_TILE:, ls] = col(hn_ref, 3, ls) * col(hn_ref, 4, ls) * not_last
        _shifted_copies(c_scr, c_sh, 2 * HALO + CD_TILE)

        def c1_of(ls):
            return jnp.concatenate([c1m_ref[:, ls], c1n_ref[:, ls]], axis=0)

        mean = sum(jnp.sum(c1_of(ls), axis=-1, keepdims=True) for ls in lanes) * (1.0 / C_WIDTH)
        var = sum(jnp.sum((c1_of(ls) - mean) ** 2, axis=-1, keepdims=True) for ls in lanes) * (1.0 / C_WIDTH)
        rs = lax.rsqrt(var + EPS)
        sum_dvh, sum_dvh_vhat = 0.0, 0.0
        for ls in lanes:
            vhat = (c1_of(ls) - mean) * rs
            c2 = vhat * lg_ref[:, ls] + lb_ref[:, ls]
            sig = _sigmoid(c2)
            dc = jnp.concatenate([dm_ref[:, ls], dn_ref[:, ls] * not_last], axis=0)
            dc2 = dc * (sig * (1.0 + c2 * (1.0 - sig)))
            dvh = dc2 * lg_ref[:, ls]
            sum_dvh = sum_dvh + jnp.sum(dvh, axis=-1, keepdims=True)
            sum_dvh_vhat = sum_dvh_vhat + jnp.sum(dvh * vhat, axis=-1, keepdims=True)
            dvh_scr[:, ls] = dvh
            vhat_scr[:, ls] = vhat
            dlg_ref[:, ls] += jnp.sum((dc2 * vhat)[:CD_TILE], axis=0, keepdims=True)
            dlb_ref[:, ls] += jnp.sum(dc2[:CD_TILE], axis=0, keepdims=True)
        for ls in lanes:
            dc1 = rs * (dvh_scr[:, ls] - sum_dvh * (1.0 / C_WIDTH) - vhat_scr[:, ls] * (sum_dvh_vhat * (1.0 / C_WIDTH)))
            dc1_scr[:, ls] = dc1
            dcb_ref[:, ls] += jnp.sum(dc1[:CD_TILE], axis=0, keepdims=True)
        _shifted_copies(dc1_scr, dc1_sh, ext)
        for ls in lanes:
            for r0 in range(0, CD_TILE, TAP_ROWS):
                rows = slice(r0, r0 + TAP_ROWS)
                dc1_m = dc1_scr[rows, ls]
                dc0 = jnp.zeros((TAP_ROWS, LANES), F32)
                for k in range(C_KERNEL):
                    dc0 = dc0 + cw_ref[k:k + 1, ls] * _rows_from(dc1_sh, r0 + C_KERNEL - 1 - k, TAP_ROWS, ls)
                    prod = dc1_m * _rows_from(c_sh, r0 + HALO - (C_KERNEL - 1) + k, TAP_ROWS, ls)
                    dcw_acc[k, :, ls] += prod.reshape(TAP_ROWS // 8, 8, LANES).sum(axis=0)
                g_m = m_ref[rows, C_WIDTH + ls.start:C_WIDTH + ls.stop].astype(F32)
                a_m = m_ref[rows, ls].astype(F32)
                sig_m = _sigmoid(g_m)
                dp_ref[rows, ls] = (dc0 * sig_m).astype(BF16)
                dp_ref[rows, C_WIDTH + ls.start:C_WIDTH + ls.stop] = (dc0 * a_m * sig_m * (1.0 - sig_m)).astype(BF16)

        @pl.when(i == nt - 1)
        def _():
            dcw_ref[...] = jnp.sum(dcw_acc[...], axis=1)

        for ls in lanes:
            wide = slice(C_WIDTH + ls.start, C_WIDTH + ls.stop)
            d1 = jnp.zeros((CD_TILE, LANES), F32)
            for k in range(D_KERNEL):
                d1 = d1 + dw_ref[k:k + 1, ls] * e_scr[pl.ds(HALO - (D_KERNEL - 1) + k, CD_TILE), ls]
            dd_m = dm_ref[:, wide]
            dd1 = jnp.concatenate([dd_m * col(m_ref, 2, ls), dn_ref[:, wide] * col(hn_ref, 2, ls) * not_last], axis=0)
            dd1_scr[:, ls] = dd1
            dp_ref[:, 2 * C_WIDTH + ls.start:2 * C_WIDTH + ls.stop] = (dd_m * d1).astype(BF16)
            de = jnp.zeros((CD_TILE, LANES), F32)
            for k in range(D_KERNEL):
                de = de + dw_ref[k:k + 1, ls] * dd1_scr[pl.ds(D_KERNEL - 1 - k, CD_TILE), ls]
                ddw_ref[k:k + 1, ls] += jnp.sum(dd1[:CD_TILE] * e_scr[pl.ds(HALO - (D_KERNEL - 1) + k, CD_TILE), ls], axis=0, keepdims=True)
            dp_ref[:, 3 * C_WIDTH + ls.start:3 * C_WIDTH + ls.stop] = (de * col(m_ref, 4, ls)).astype(BF16)
            dp_ref[:, 4 * C_WIDTH + ls.start:4 * C_WIDTH + ls.stop] = (de * col(m_ref, 3, ls)).astype(BF16)

    halo_prev = lambda i: (jnp.maximum(i * per - 1, 0), 0)
    halo_next = lambda i: (jnp.minimum((i + 1) * per, t // HALO - 1), 0)
    vec = _const_spec((1, C_WIDTH))
    return _call(
        body, grid=(nt,),
        in_specs=[pl.BlockSpec((HALO, CD_IN), halo_prev), _row_spec(CD_TILE, CD_IN), pl.BlockSpec((HALO, CD_IN), halo_next),
                  _row_spec(CD_TILE, 2 * C_WIDTH), pl.BlockSpec((HALO, 2 * C_WIDTH), halo_next),
                  _row_spec(CD_TILE, C_WIDTH), pl.BlockSpec((HALO, C_WIDTH), halo_next),
                  _const_spec((32, C_WIDTH)), vec, vec, _const_spec((8, C_WIDTH))],
        out_specs=[_row_spec(CD_TILE, CD_IN), _const_spec((32, C_WIDTH)), vec, vec, vec, _const_spec((8, C_WIDTH))],
        out_shape=[SDS((t, CD_IN), BF16), SDS((32, C_WIDTH), F32), SDS((1, C_WIDTH), F32), SDS((1, C_WIDTH), F32),
                   SDS((1, C_WIDTH), F32), SDS((8, C_WIDTH), F32)],
        scratch_shapes=[pltpu.VMEM((2 * HALO + CD_TILE, C_WIDTH), F32)] * 2 + [pltpu.VMEM((ext, C_WIDTH), F32)] * 2
        + [pltpu.VMEM((8, 2 * HALO + CD_TILE, C_WIDTH), F32), pltpu.VMEM((8, ext, C_WIDTH), F32),
           pltpu.VMEM((32, 8, C_WIDTH), F32)] + [pltpu.VMEM((ext, C_WIDTH), F32)] * 2,
        operands=[proj, proj, proj, dcat, dcat, c1, c1, cw, lg, lb, dw], name="mixer_cd_bwd", ride=ride)


def _wgrad(name, pairs, out_rc, t, ride):
    tk = TILES["wgrad"]
    assert tk == t, "the whole contraction has to fit one grid step"
    r, c = out_rc
    n = len(pairs)

    operands, in_specs, where = [], [], []
    for lhs, lhs_spec, rhs, rhs_spec in pairs:
        at = []
        for array, spec in ((lhs, lhs_spec), (rhs, rhs_spec)):
            seen = [k for k, o in enumerate(operands) if o is array]
            if not seen:
                operands.append(array)
                in_specs.append(spec)
            at.append(seen[0] if seen else len(operands) - 1)
        where.append(at)
    n_in = len(operands)

    def body(*refs):
        ins, out_refs = refs[:n_in], refs[n_in:]
        for j, (lhs_at, rhs_at) in enumerate(where):
            out_refs[j][...] = _dot(ins[lhs_at][...], ins[rhs_at][...], TN).astype(BF16)

    res = _call(body, grid=(N_CHIPS, t // tk), in_specs=in_specs,
                out_specs=[pl.BlockSpec((None, r, c), lambda p, k: (p, 0, 0))] * n,
                out_shape=[SDS((N_CHIPS, r, c), BF16)] * n, operands=operands, name=name, ride=ride)
    outs, ride_res = (res, None) if ride is None else res
    outs = [o.reshape(N_CHIPS, 2, r // 2, c) for o in outs]
    return outs if ride is None else (outs, ride_res)


def _wgrad_col_sharded(name, h, dz_list, three_d, ride=None):
    t, d = h.shape
    tk = TILES["wgrad"]
    n4 = dz_list[0].shape[-1] if three_d else dz_list[0].shape[-1] // N_CHIPS
    hs = pl.BlockSpec((tk, d), lambda p, k: (k, 0))
    zs = pl.BlockSpec((None, tk, n4), lambda p, k: (p, k, 0)) if three_d else pl.BlockSpec((tk, n4), lambda p, k: (k, p))
    return _wgrad(name, [(h, hs, dz, zs) for dz in dz_list], (d, n4), t, ride)


def _wgrad_row_sharded(name, a, g, three_d, ride=None):
    many = isinstance(a, (list, tuple))
    a_list = list(a) if many else [a]
    t, d = g.shape
    tk = TILES["wgrad"]
    k4 = a_list[0].shape[-1] if three_d else a_list[0].shape[-1] // N_CHIPS
    a_spec = pl.BlockSpec((None, tk, k4), lambda p, k: (p, k, 0)) if three_d else pl.BlockSpec((tk, k4), lambda p, k: (k, p))
    gs = pl.BlockSpec((tk, d), lambda p, k: (k, 0))
    res = _wgrad(name, [(a_j, a_spec, g, gs) for a_j in a_list], (k4, d), t, ride)
    if many:
        return res
    return res[0] if ride is None else (res[0][0], res[1])


def _mesh_scalars():
    return jnp.stack([lax.axis_index("c"), 2 * lax.axis_index("x") + lax.axis_index("y")]).astype(jnp.int32)


def _stage_own(name, w, layer, dtype, far_slab=False):
    layers, r, cols = w.shape
    h = r // 2

    def body(s_ref, x_ref, o_ref, *unwritten):
        o_ref[...] = x_ref[...].astype(dtype)

    out_specs = [pl.BlockSpec((None, None, h, cols), lambda i, s: (s[1], i, 0, 0))] + [ANY] * far_slab
    out_shape = [SDS((N_CHIPS, 2, h, cols), dtype)] + [SDS((2, h, cols), dtype)] * far_slab
    res = pl.pallas_call(
        body,
        grid_spec=pltpu.PrefetchScalarGridSpec(
            num_scalar_prefetch=1, grid=(2,),
            in_specs=[pl.BlockSpec((None, h, cols), lambda i, s: (2 * layer + i, 0, 0))], out_specs=out_specs),
        out_shape=out_shape, name=name,
        compiler_params=_params())(_mesh_scalars(), w.reshape(2 * layers, h, cols))
    return tuple(res) if far_slab else res[0]


STAGE_STEPS = 4


def _stage_rest_and_norm(x, g, weights, ride=None):
    t, d = x.shape
    n = len(weights)
    views, in_specs, out_specs, out_shapes = [], [], [], []
    for w, layer in weights:
        layers, r, cols = w.shape
        sub = r // STAGE_STEPS
        views.append(w.reshape(layers * STAGE_STEPS, sub, cols))
        in_specs.append(pl.BlockSpec((None, sub, cols), functools.partial(lambda l, i, s: (STAGE_STEPS * l + i, 0, 0), layer)))
        out_specs.append(pl.BlockSpec((None, None, sub, cols), lambda i, s: (s[1], i // 2, i % 2, 0)))
        out_shapes.append(SDS((N_CHIPS, 2, r // 2, cols), BF16))

    def body(s_ref, x_ref, g_ref, *rest):
        w_refs, h_ref, o_refs = rest[:n], rest[n], rest[n + 1:]
        h_ref[...] = _rms_rows(x_ref[...], g_ref[...]).astype(BF16)
        for w_ref, o_ref in zip(w_refs, o_refs):
            o_ref[...] = w_ref[...].astype(BF16)

    tm = t // STAGE_STEPS
    res = _call(
        body, grid=(STAGE_STEPS,), in_specs=[pl.BlockSpec((tm, d), lambda i, s: (i, 0)), pl.BlockSpec((1, d), lambda i, s: (0, 0))] + in_specs,
        out_specs=[pl.BlockSpec((tm, d), lambda i, s: (i, 0))] + out_specs, out_shape=[SDS((t, d), BF16)] + out_shapes,
        operands=[x, g] + views, name="stage_and_norm", ride=ride, prefetch=_mesh_scalars())
    outs, ride_res = (res, None) if ride is None else res
    result = (outs[0], list(outs[1:]))
    return result if ride is None else (result, ride_res)


def _remote(src, dst, send_sem, recv_sem, device):
    return pltpu.make_async_remote_copy(src, dst, send_sem, recv_sem, device_id=device, device_id_type=MESH)


ALL_PEERS = (0, 1, 2)
NEIGHBOURS = (0, 1)


def _ride_gather_send(bufs, peers=ALL_PEERS):
    n = len(bufs)

    def each(b, sems, act):
        send, recv = sems
        x, y, c, p, others = _position()
        for t in range(n):
            for j in peers:
                qx, qy = others[j]
                act(b[t].at[p, c], b[t].at[2 * qx + qy, c], send.at[t, j], recv.at[t, j], (qx, qy, c))

    def start(ins, b, new, sems):
        each(b, sems, lambda mine, landed, s, r, dev: _remote(mine, mine, s, r, dev).start())

    def finish(ins, b, new, sems):
        def act(mine, landed, s, r, dev):
            _remote(mine, mine, s, r, dev).wait_send()
            _remote(landed, landed, s, r, dev).wait_recv()
        each(b, sems, act)

    return _Ride([], bufs, [], [(n, 3), (n, 3)], start, finish, ["chips"])


def _ride_gather_pass(bufs, peers=ALL_PEERS):
    n = len(bufs)

    def each(b, sems, act):
        send, recv = sems
        x, y, c, p, others = _position()
        for t in range(n):
            for j in peers:
                qx, qy = others[j]
                act(b[t].at[2 * qx + qy, c], b[t].at[2 * qx + qy, 1 - c], send.at[t, j], recv.at[t, j], (x, y, 1 - c))

    def start(ins, b, new, sems):
        each(b, sems, lambda landed, passed, s, r, dev: _remote(landed, landed, s, r, dev).start())

    def finish(ins, b, new, sems):
        def act(landed, passed, s, r, dev):
            _remote(landed, landed, s, r, dev).wait_send()
            _remote(passed, passed, s, r, dev).wait_recv()
        each(b, sems, act)

    return _Ride([], bufs, [], [(n, 3), (n, 3)], start, finish, ["sibling"])


def _ride_gather(bufs, peers=ALL_PEERS):
    send, onward = _ride_gather_send(bufs, peers), _ride_gather_pass(bufs, peers)
    n_send = len(send.sem_shapes)

    def start(ins, b, new, sems):
        send.start(ins, b, new, sems[:n_send])

    def finish(ins, b, new, sems):
        send.finish(ins, b, new, sems[:n_send])
        onward.start(ins, b, new, sems[n_send:])
        onward.finish(ins, b, new, sems[n_send:])

    return _Ride([], bufs, [], send.sem_shapes + onward.sem_shapes, start, finish, ["sibling", "chips"])


def _ride_gather_far(sources, slabs):
    n = len(slabs)

    def hops(ins, b, sems):
        x, y, c, p, others = _position()
        qx, qy = others[2]
        for t in range(n):
            far = (ins[t].at[p, c], b[t].at[c], sems[0].at[t], sems[1].at[t], (qx, qy, c))
            onward = (b[t].at[c], b[t].at[1 - c], sems[2].at[t], sems[3].at[t], (x, y, 1 - c))
            yield far, onward

    def wait(mine, landed, s, r, dev):
        _remote(mine, mine, s, r, dev).wait_send()
        _remote(landed, landed, s, r, dev).wait_recv()

    def start(ins, b, new, sems):
        for (mine, landed, s, r, dev), _ in hops(ins, b, sems):
            _remote(mine, landed, s, r, dev).start()

    def finish(ins, b, new, sems):
        for far, _ in hops(ins, b, sems):
            wait(*far)
        for _, (landed, passed, s, r, dev) in hops(ins, b, sems):
            _remote(landed, landed, s, r, dev).start()
        for _, onward in hops(ins, b, sems):
            wait(*onward)

    return _Ride(sources, slabs, [], [(n,)] * 4, start, finish, ["sibling", "chips"])


def _ride_swap(tensors):
    n = len(tensors)

    def each(ins, new, sems, act):
        send, recv = sems
        x, y, c, _, _ = _position()
        for t in range(n):
            act(_remote(ins[t].at[:, 1 - c], new[t], send.at[t], recv.at[t], (x, y, 1 - c)))

    def start(ins, b, new, sems):
        each(ins, new, sems, lambda cp: cp.start())

    def finish(ins, b, new, sems):
        each(ins, new, sems, lambda cp: cp.wait())

    return _Ride(tensors, [], [SDS((s.shape[0],) + s.shape[2:], s.dtype) for s in tensors], [(n,), (n,)], start, finish,
                 ["sibling"])


def _ride_scatter(tensors, landing):
    n = len(tensors)

    def each(ins, b, sems, act):
        send, recv = sems
        x, y, c, p, others = _position()
        for t in range(n):
            for j, (qx, qy) in enumerate(others):
                q = 2 * qx + qy
                act(ins[t].at[q], b[t].at[p], b[t].at[q], send.at[t, j], recv.at[t, j], (qx, qy, c))

    def start(ins, b, new, sems):
        each(ins, b, sems, lambda src, dst, landed, s, r, dev: _remote(src, dst, s, r, dev).start())

    def finish(ins, b, new, sems):
        def act(src, dst, landed, s, r, dev):
            _remote(src, dst, s, r, dev).wait_send()
            _remote(landed, landed, s, r, dev).wait_recv()
        each(ins, b, sems, act)

    return _Ride(tensors, landing, [], [(n, 3), (n, 3)], start, finish, ["chips"])


def _ride_join(bufs):
    n = len(bufs)

    def each(b, sems, act):
        send, recv = sems
        x, y, c, _, _ = _position()
        for t in range(n):
            act(b[t].at[c], b[t].at[1 - c], send.at[t], recv.at[t], (x, y, 1 - c))

    def start(ins, b, new, sems):
        each(b, sems, lambda mine, theirs, s, r, dev: _remote(mine, mine, s, r, dev).start())

    def finish(ins, b, new, sems):
        def act(mine, theirs, s, r, dev):
            _remote(mine, mine, s, r, dev).wait_send()
            _remote(theirs, theirs, s, r, dev).wait_recv()
        each(b, sems, act)

    return _Ride([], bufs, [], [(n,), (n,)], start, finish, ["sibling"])


def _all_reduce_small(pack, ride=None):
    rows = pack.shape[0]
    n_dev = 2 * N_CHIPS
    n_rb = 0 if ride is None else len(ride.bufs)

    def body(x_ref, *rest):
        o_ref = rest[n_rb]
        r_bufs = rest[n_rb + 1:2 * n_rb + 1]
        land, send, recv = rest[2 * n_rb + 1:2 * n_rb + 4]
        r_sems = rest[2 * n_rb + 4:]
        if ride is not None:
            ride.start([], r_bufs, [], r_sems)
        x, y, c, p, _ = _position()
        me = 2 * p + c
        land[me] = x_ref[...]
        peers = [(dx, dy, dc) for dx in range(2) for dy in range(2) for dc in range(2) if (dx, dy, dc) != (0, 0, 0)]
        for j, (dx, dy, dc) in enumerate(peers):
            _remote(land.at[me], land.at[me], send.at[j], recv.at[j], (x ^ dx, y ^ dy, c ^ dc)).start()
        for j, (dx, dy, dc) in enumerate(peers):
            src = 4 * (x ^ dx) + 2 * (y ^ dy) + (c ^ dc)
            _remote(land.at[me], land.at[me], send.at[j], recv.at[j], (x ^ dx, y ^ dy, c ^ dc)).wait_send()
            _remote(land.at[src], land.at[src], send.at[j], recv.at[j], (x ^ dx, y ^ dy, c ^ dc)).wait_recv()
        acc = land[0]
        for dev in range(1, n_dev):
            acc = acc + land[dev]
        o_ref[...] = acc
        if ride is not None:
            ride.finish([], r_bufs, [], r_sems)

    bufs = [] if ride is None else ride.bufs
    sems = [] if ride is None else [pltpu.SemaphoreType.DMA(s) for s in ride.sem_shapes]
    res = pl.pallas_call(
        body, in_specs=[pl.BlockSpec(memory_space=pltpu.VMEM)] + [ANY] * n_rb,
        out_specs=[pl.BlockSpec(memory_space=pltpu.VMEM)] + [ANY] * n_rb,
        out_shape=[SDS((rows, LANES), F32)] + [SDS(b.shape, b.dtype) for b in bufs],
        scratch_shapes=[pltpu.VMEM((n_dev, rows, LANES), F32), pltpu.SemaphoreType.DMA((n_dev - 1,)),
                        pltpu.SemaphoreType.DMA((n_dev - 1,))] + sems,
        input_output_aliases={1 + j: 1 + j for j in range(n_rb)},
        name="all_reduce_small", compiler_params=_params())(pack, *bufs)
    return res[0], list(res[1:])


def _add_own_half(name, fulls, recvs, out_dtypes):
    n = len(fulls)
    in_specs, out_specs, out_shapes = [], [], []
    for full, dtype in zip(fulls, out_dtypes):
        n4, _, h, cols = full.shape
        in_specs += [pl.BlockSpec((None, None, h, cols), lambda q, s: (q, s[0], 0, 0)),
                     pl.BlockSpec((None, h, cols), lambda q, s: (q, 0, 0))]
        out_specs += [pl.BlockSpec((None, h, cols), lambda q, s: (q, 0, 0)),
                      pl.BlockSpec((None, h, cols), lambda q, s: (s[1], 0, 0))]
        out_shapes += [SDS((n4, h, cols), dtype)] * 2

    def body(s_ref, *refs):
        ins, outs = refs[:2 * n], refs[2 * n:]
        for j in range(n):
            o_ref, own_ref = outs[2 * j], outs[2 * j + 1]
            v = (ins[2 * j][...].astype(F32) + ins[2 * j + 1][...].astype(F32)).astype(o_ref.dtype)
            o_ref[...] = v

            @pl.when(pl.program_id(0) == s_ref[1])
            def _():
                own_ref[...] = v

    res = pl.pallas_call(
        body,
        grid_spec=pltpu.PrefetchScalarGridSpec(num_scalar_prefetch=1, grid=(N_CHIPS,), in_specs=in_specs, out_specs=out_specs),
        out_shape=out_shapes, name=name, compiler_params=_params())(
            _mesh_scalars(), *[a for pair in zip(fulls, recvs) for a in pair])
    return [(res[2 * j], res[2 * j + 1]) for j in range(n)]


def _sum_chips(name, parts_list):
    steps = 4 if all(parts.shape[1] % 64 == 0 for parts in parts_list) else 1
    in_specs, out_specs, out_shapes = [], [], []
    for parts in parts_list:
        n4, h, cols = parts.shape
        in_specs.append(pl.BlockSpec((n4, h // steps, cols), lambda i, s: (0, i, 0)))
        out_specs.append(pl.BlockSpec((None, h // steps, cols), lambda i, s: (s[0], i, 0)))
        out_shapes.append(SDS((2, h, cols), F32))
    n = len(parts_list)

    def body(s_ref, *refs):
        for a_ref, o_ref in zip(refs[:n], refs[n:]):
            acc = a_ref[0].astype(F32)
            for q in range(1, N_CHIPS):
                acc = acc + a_ref[q].astype(F32)
            o_ref[...] = acc

    return pl.pallas_call(
        body,
        grid_spec=pltpu.PrefetchScalarGridSpec(num_scalar_prefetch=1, grid=(steps,), in_specs=in_specs, out_specs=out_specs),
        out_shape=out_shapes, name=name, compiler_params=_params())(_mesh_scalars(), *parts_list)


def _adamw_math(w, g, m, v):
    m2 = ADAM_B1 * m + (1.0 - ADAM_B1) * g
    v2 = ADAM_B2 * v + (1.0 - ADAM_B2) * (g * g)
    m_hat = m2 / (1.0 - ADAM_B1 ** ADAM_STEP)
    v_hat = v2 / (1.0 - ADAM_B2 ** ADAM_STEP)
    delta = -ADAM_LR * (m_hat / (jnp.sqrt(v_hat) + ADAM_EPS) + ADAM_WD * w)
    return delta, m2, v2


ADAMW_STEPS = 8


def _adamw_big(name, ws, g_layers_list, ms, vs):
    layers = ws[0].shape[0]
    n, per_in = len(ws), 3 + layers
    in_specs, out_specs, out_shapes, operands = [], [], [], []
    for w, g_layers, m, v in zip(ws, g_layers_list, ms, vs):
        assert w.shape[0] == layers and w.shape[1] % (8 * ADAMW_STEPS) == 0
        _, rows, cols = w.shape
        tr = rows // ADAMW_STEPS
        blk = pl.BlockSpec((None, tr, cols), lambda l, i: (l, i, 0))
        in_specs += [blk] * 3 + [pl.BlockSpec((tr, cols), lambda l, i: (i, 0))] * layers
        out_specs += [blk] * 4
        out_shapes += [SDS((layers, rows, cols), F32)] * 4
        operands += [w, m, v] + [g.reshape(rows, cols) for g in g_layers]

    def body(*refs):
        ins, outs = refs[:n * per_in], refs[n * per_in:]
        for j in range(n):
            w_ref, m_ref, v_ref = ins[j * per_in:j * per_in + 3]
            g_refs = ins[j * per_in + 3:(j + 1) * per_in]
            g_o, d_o, m_o, v_o = outs[4 * j:4 * j + 4]
            gv = g_refs[0][...]
            for layer in range(1, layers):
                gv = jnp.where(pl.program_id(0) == layer, g_refs[layer][...], gv)
            d, mm, vv = _adamw_math(w_ref[...], gv, m_ref[...], v_ref[...])
            g_o[...] = gv
            d_o[...] = d
            m_o[...] = mm
            v_o[...] = vv

    res = pl.pallas_call(
        body, grid=(layers, ADAMW_STEPS), in_specs=in_specs, out_specs=out_specs, out_shape=out_shapes, name=name,
        compiler_params=_params())(*operands)
    return [tuple(res[4 * j:4 * j + 4]) for j in range(n)]


def _adamw_small(ws, gs, ms, vs):
    n = len(ws)
    flat = []
    for group in (ws, gs, ms, vs):
        flat += [a.reshape(-1, a.shape[-1]) for a in group]

    def body(*refs):
        w_r, g_r, m_r, v_r = refs[:n], refs[n:2 * n], refs[2 * n:3 * n], refs[3 * n:4 * n]
        d_o, m_o, v_o = refs[4 * n:5 * n], refs[5 * n:6 * n], refs[6 * n:7 * n]
        for j in range(n):
            d, mm, vv = _adamw_math(w_r[j][...], g_r[j][...], m_r[j][...], v_r[j][...])
            d_o[j][...] = d
            m_o[j][...] = mm
            v_o[j][...] = vv

    shapes = [SDS(a.shape, F32) for a in flat[:n]]
    outs = pl.pallas_call(body, out_shape=shapes * 3, name="adamw_small", compiler_params=_params())(*flat)
    res = []
    for k in range(3):
        res.append([outs[k * n + j].reshape(ws[j].shape) for j in range(n)])
    return res


BIG = ("ab_w_in", "ab_w_out", "cd_w_in", "cd_w_out", "ffn_w_gate", "ffn_w_up", "ffn_w_down")
BIG_BY_LAYERS = (BIG[:4], BIG[4:])
V_BLOCK = (2 * A_WIDTH + QK_COLS) // B_WIDTH


def _pad_rows(a, rows):
    return jnp.pad(a, ((0, rows - a.shape[0]), (0, 0)))


A_IN, A_OUT, C_IN, C_OUT = ("ab_w_in", 0), ("ab_w_out", 0), ("cd_w_in", 0), ("cd_w_out", 0)
G0, U0, D0 = ("ffn_w_gate", 0), ("ffn_w_up", 0), ("ffn_w_down", 0)
G1, U1, D1 = ("ffn_w_gate", 1), ("ffn_w_up", 1), ("ffn_w_down", 1)
UNITS = (A_IN, A_OUT, G0, U0, D0, C_IN, C_OUT, G1, U1, D1)
ROWS_MINOR = ("ffn_w_gate", "ffn_w_up")
SMALL_SHARDED = ("small", 0)
REPLICATED_UNIT = ("replicated", 0)


class _Exchange:
    def __init__(self, enabled):
        self.enabled = enabled
        self.w, self.grad, self.recv, self.half, self.land, self.done = {}, {}, {}, {}, {}, {}
        self.far = {}

    def full(self, unit):
        b = self.w[unit]
        return b.reshape(N_CHIPS, 1, 2 * b.shape[2], b.shape[3])

    def ride_for(self, phases):
        rides, sinks = [], []
        for kind, units in phases:
            if kind == "send":
                rides.append(_ride_gather_send([self.w[u] for u in units]))
                sinks.append(self.w)
            elif kind == "pass":
                rides.append(_ride_gather_pass([self.w[u] for u in units]))
                sinks.append(self.w)
            elif kind == "gather":
                rides.append(_ride_gather([self.w[u] for u in units]))
                sinks.append(self.w)
            elif kind == "gather_near":
                rides.append(_ride_gather([self.w[u] for u in units], NEIGHBOURS))
                sinks.append(self.w)
            elif kind == "gather_far":
                rides.append(_ride_gather_far([self.w[u] for u in units], [self.far[u] for u in units]))
                sinks.append(self.far)
            elif kind == "swap":
                rides.append(_ride_swap([self.grad[u] for u in units]))
                sinks.append(self.recv)
            elif kind == "scatter":
                rides.append(_ride_scatter([self.half[u] for u in units], [self.land[u] for u in units]))
                sinks.append(self.land)
            else:
                rides.append(_ride_join([self.done[u] for u in units]))
                sinks.append(self.done)
        ride = functools.reduce(_ride_both, rides)

        def settle(res):
            n_bufs = sum(len(r.bufs) for r in rides)
            bufs, new = list(res[:n_bufs]), list(res[n_bufs:])
            for r, sink, (_, units) in zip(rides, sinks, phases):
                vals = [bufs.pop(0) for _ in r.bufs] + [new.pop(0) for _ in r.new_outs]
                for u, v in zip(units, vals):
                    sink[u] = v

        return ride, settle

    def run(self, fn, *args, phases=(), **kw):
        if not self.enabled or not phases:
            return fn(*args, **kw)
        ride, settle = self.ride_for(phases)
        out, res = fn(*args, ride=ride, **kw)
        settle(res)
        return out

    def pair_sum(self, units):
        if self.enabled:
            dtypes = [F32 if u in (SMALL_SHARDED, REPLICATED_UNIT) else BF16 for u in units]
            res = _add_own_half(f"pair_sum_{units[0][0]}_{units[0][1]}", [self.grad[u] for u in units],
                                [self.recv[u] for u in units], dtypes)
            for u, (half, land) in zip(units, res):
                self.half[u], self.land[u] = half, land

    def chip_sum(self, units):
        if self.enabled:
            res = _sum_chips(f"chip_sum_{units[0][0]}_{units[0][1]}", [self.land[u] for u in units])
            self.done.update(zip(units, res))


def _local_step(x, target, ex, sp, h0=None):
    t, d = x.shape
    tabs = _rope_tables(t)
    gains = jnp.concatenate([jnp.tile(sp["q_norm_g"][g], HEAD_DIM // 8) for g in range(N_DIL)]
                            + [jnp.tile(sp["k_norm_g"][g], HEAD_DIM // 8) for g in range(N_DIL)]).reshape(1, QK_COLS)
    bias_t = sp["sgu_bias"].T
    cw = _pad_rows(sp["conv_c_w"], 32)
    dw = _pad_rows(sp["conv_d_w"], 8)
    cb, clg, clb = (sp[k].reshape(1, C_WIDTH) for k in ("conv_c_b", "c_ln_g", "c_ln_b"))
    slg, slb = sp["sgu_norm_g"].reshape(1, A_WIDTH), sp["sgu_norm_b"].reshape(1, A_WIDTH)
    g_ab, g_cd = sp["ab_norm_g"].reshape(1, d), sp["cd_norm_g"].reshape(1, d)
    g_f0, g_f1 = sp["ffn_norm_g"][0:1], sp["ffn_norm_g"][1:2]
    run = ex.run

    def w2d(unit):
        return ex.full(unit).reshape(-1, d)

    if h0 is None:
        h0 = _rms_fwd("rms_ab", x, g_ab)
    if ex.enabled:
        proj = run(_proj_in_near, "proj_ab_near", h0, ex.full(A_IN), phases=[("gather_far", [A_IN]), ("send", [A_OUT])])
        proj, ex.w[A_IN] = _proj_in_far("proj_ab_far", h0, ex.far[A_IN], proj, ex.w[A_IN])
    else:
        proj = _proj_in("proj_ab", h0, ex.full(A_IN), 0)
    a_out = _mixer_a_fwd(proj, slg, slb, sp["sgu_w"], bias_t)
    qk, q1, q2, k1, k2 = run(_qk_fwd, proj, gains, tabs, phases=[("pass", [A_OUT]), ("send", [G0, C_OUT])])
    regrouped_qk = {1: (q1, k1), 2: (q2, k2)}
    fwd_phases = ([("pass", [G0, C_OUT]), ("send", [U0])], [("pass", [U0]), ("send", [D0])],
                  [("pass", [D0]), ("send", [C_IN])])
    qkv, o_list, l_list = [], [], []
    dilated = [g for g, rate in enumerate(DIL_RATES) if rate != 1]
    regrouped_v = dict(zip(dilated, _permute("regroup_v", [(proj, V_BLOCK + g, DIL_RATES[g]) for g in dilated])))
    for g, rate in enumerate(DIL_RATES):
        if rate == 1:
            qk3, proj3 = qk.reshape(1, t, QK_COLS), proj.reshape(1, t, AB_IN)
            q, k, v = (qk3, g), (qk3, N_DIL + g), (proj3, V_BLOCK + g)
        else:
            q, k, v = (regrouped_qk[g][0], 0), (regrouped_qk[g][1], 0), (regrouped_v[g], 0)
        qkv.append((q, k, v))
        o, l = run(_attn_fwd, f"attn_fwd_{g}", q, k, v, phases=fwd_phases[g])
        if rate == 1:
            o, l = o.reshape(t, B_WIDTH), l.reshape(t, B_WIDTH)
        o_list.append(o)
        l_list.append(l)
    cat, lse_tot, lse_1, lse_2 = _attn_merge(a_out, o_list, l_list)
    x1, hf0 = _proj_out("out_ab", cat, w2d(A_OUT), x, g_next=g_f0)
    fgate0, fup0, act0 = run(_ffn_in, "ffn_in_0", hf0, ex.full(G0), ex.full(U0), 0,
                           phases=[("pass", [C_IN]), ("send", [D1, G1])])
    x2, h1 = run(_ffn_out, "ffn_out_0", act0, ex.full(D0), 0, x1, g_next=g_cd, phases=[("pass", [D1, G1]), ("send", [U1])])
    projcd = run(_proj_in, "proj_cd", h1, ex.full(C_IN), 0, phases=[("pass", [U1])])
    cat2, c1 = _mixer_cd_fwd(projcd, cw, cb, clg, clb, dw)
    x3, hf1 = _proj_out("out_cd", cat2, w2d(C_OUT), x2, g_next=g_f1)
    fgate1, fup1, act1 = _ffn_in("ffn_in_1", hf1, ex.full(G1), ex.full(U1), 0)
    dy, loss_acc, dy_b = _ffn_out("ffn_out_1", act1, ex.full(D1), 0, x3, target=target)
    loss = 0.5 * loss_acc[0, 0] / d

    late = [D1, G1, U1]
    dgate, dup = _ffn_dact("ffn_dact_1", dy_b, ex.full(D1), 0, fgate1, fup1)
    ex.grad[D1] = _wgrad_row_sharded("wgrad_down_1", act1, dy_b, True)
    ex.grad[G1], ex.grad[U1] = _wgrad_row_sharded("wgrad_gate_up_1", [dgate, dup], hf1, True)
    g3, d_f1, g3_b = run(_dgrad_cols, "dgrad_ffn_1", [dgate, dup], [ex.full(G1), ex.full(U1)], 0, True, x3, g_f1, dy,
                         w_rows=True, phases=[("swap", late)])
    ex.pair_sum(late)

    dcat2 = _dgrad_rows("dgrad_out_cd", g3_b, w2d(C_OUT))
    ex.grad[C_OUT] = _wgrad_row_sharded("wgrad_out_cd", cat2, g3_b, False)
    dprojcd, d_cw, d_cb, d_clg, d_clb, d_dw = run(_mixer_cd_bwd, projcd, dcat2, c1, cw, clg, clb, dw, phases=[("scatter", late)])
    ex.grad[C_IN] = _wgrad_col_sharded("wgrad_in_cd", h1, [dprojcd], False)[0]
    g2, d_cdn, g2_b = run(_dgrad_cols, "dgrad_in_cd", [dprojcd], [ex.full(C_IN)], 0, False, x2, g_cd, g3,
                          phases=[("swap", [C_OUT, C_IN])])
    ex.pair_sum([C_OUT, C_IN])

    dgate, dup = run(_ffn_dact, "ffn_dact_0", g2_b, ex.full(D0), 0, fgate0, fup0, phases=[("scatter", [C_OUT, C_IN])])
    ex.chip_sum(late + [C_OUT, C_IN])
    ex.grad[D0] = _wgrad_row_sharded("wgrad_down_0", act0, g2_b, True)
    ex.grad[G0], ex.grad[U0] = _wgrad_row_sharded("wgrad_gate_up_0", [dgate, dup], hf0, True)
    small = {"cd_norm_g": d_cdn, "conv_c_w": d_cw[:C_KERNEL], "conv_c_b": d_cb, "c_ln_g": d_clg, "c_ln_b": d_clb,
             "conv_d_w": d_dw[:D_KERNEL]}
    ex.grad[SMALL_SHARDED] = _split_full_small(small).reshape(N_CHIPS, 2, SHARDED_ROWS // 2, LANES)
    mid = [D0, G0, U0, SMALL_SHARDED]
    g1, d_f0, g1_b = run(_dgrad_cols, "dgrad_ffn_0", [dgate, dup], [ex.full(G0), ex.full(U0)], 0, True, x1, g_f0, g2,
                         w_rows=True, phases=[("join", late + [C_OUT, C_IN]), ("swap", mid)])
    ex.pair_sum(mid)

    dcat = _dgrad_rows("dgrad_out_ab", g1_b, w2d(A_OUT))
    ex.grad[A_OUT] = _wgrad_row_sharded("wgrad_out_ab", cat, g1_b, False)
    d_a, d_sw, d_sbt, d_slg, d_slb = _mixer_a_bwd(proj, dcat, slg, slb, sp["sgu_w"], bias_t)
    early = {"sgu_norm_g": d_slg, "sgu_norm_b": d_slb, "sgu_w": d_sw, "sgu_bias": d_sbt.T}
    ex.grad[REPLICATED_UNIT] = jnp.broadcast_to(
        _pack_replicated(early, REPLICATED_EARLY, REPLICATED_EARLY_ROWS).reshape(2, REPLICATED_EARLY_ROWS // 2, LANES),
        (N_CHIPS, 2, REPLICATED_EARLY_ROWS // 2, LANES))
    last = [A_OUT, REPLICATED_UNIT]
    dbb, dd, db_1, dd_1, db_2, dd_2 = _attn_bwd_prep(dcat, cat)
    regrouped_bwd = {1: (db_1, lse_1, dd_1), 2: (db_2, lse_2, dd_2)}
    bwd_phases = ([("scatter", [D0, SMALL_SHARDED])],
                  [("scatter", [G0]), ("swap", last)],
                  [("scatter", [U0])])
    dqs, dks, dvs = [], [], []
    for g, rate in enumerate(DIL_RATES):
        q, k, v = qkv[g]
        if rate == 1:
            db3, l3, dd3 = (a.reshape(1, t, B_WIDTH) for a in (dbb, lse_tot, dd))
        else:
            db3, l3, dd3 = regrouped_bwd[g]
        if g == 2:
            ex.pair_sum(last)
        dq, dk, dv = run(_attn_bwd, f"attn_bwd_{g}", q, k, v, db3, l3, dd3, phases=bwd_phases[g])
        if rate == 1:
            dq, dk, dv = (a.reshape(t, B_WIDTH) for a in (dq, dk, dv))
        dqs.append(dq)
        dks.append(dk)
        dvs.append(dv)
    ex.chip_sum([D0, SMALL_SHARDED, G0, U0])
    dproj, d_gains = run(_dproj_assemble, proj, d_a, dqs, dks, dvs, gains, tabs, phases=[("scatter", last), ("join", [D0, SMALL_SHARDED, G0, U0])])
    ex.chip_sum(last)
    ex.grad[A_IN] = _wgrad_col_sharded("wgrad_in_ab", h0, [dproj], False)[0]
    d_gains = run(_fold_heads, d_gains, phases=[("swap", [A_IN])])[0].reshape(2, N_DIL, B_WIDTH)[:, :, :HEAD_DIM]
    ex.pair_sum([A_IN])
    gx, d_abn = run(_dgrad_cols, "dgrad_in_ab", [dproj], [ex.full(A_IN)], 0, False, x, g_ab, g1, bf16_copy=False,
                    phases=[("join", last), ("scatter", [A_IN])])
    ex.chip_sum([A_IN])

    small.update({
        "ab_norm_g": d_abn, "sgu_norm_g": d_slg, "sgu_norm_b": d_slb, "sgu_w": d_sw, "sgu_bias": d_sbt.T,
        "q_norm_g": d_gains[0], "k_norm_g": d_gains[1], "ffn_norm_g": jnp.concatenate([d_f0, d_f1], axis=0),
    })
    return loss, gx, small


SHARDED_SMALL = ("cd_norm_g", "conv_c_w", "conv_c_b", "c_ln_g", "c_ln_b", "conv_d_w")
SHARDED_ROWS = 48
REPLICATED_EARLY = ("sgu_norm_g", "sgu_norm_b", "sgu_w", "sgu_bias")
REPLICATED_EARLY_ROWS = 528
REPLICATED_LATE = ("ab_norm_g", "q_norm_g", "k_norm_g", "ffn_norm_g", "loss")
REPLICATED_LATE_ROWS = 32
REPLICATED_SMALL = REPLICATED_EARLY + REPLICATED_LATE[:-1]


def _pack_sharded(parts):
    rows = [parts[k].reshape(-1, LANES) for k in SHARDED_SMALL]
    return _pad_rows(jnp.concatenate(rows, axis=0), SHARDED_ROWS)


def _split_full_small(small):
    per_chip = []
    for q in range(N_CHIPS):
        parts = {}
        for k in SHARDED_SMALL:
            a = small[k]
            a = a.reshape(-1, a.shape[-1])
            n = a.shape[-1] // N_CHIPS
            parts[k] = a[:, q * n:(q + 1) * n]
        per_chip.append(_pack_sharded(parts))
    return jnp.stack(per_chip)


def _unpack_sharded(pack, shapes):
    out, r = {}, 0
    for k in SHARDED_SMALL:
        n = math.prod(shapes[k]) // LANES
        out[k] = pack[r:r + n].reshape(shapes[k])
        r += n
    return out


def _gathered_small(packs, shapes):
    per_chip = [_unpack_sharded(packs[q], shapes) for q in range(N_CHIPS)]
    return {k: jnp.concatenate([pc[k] for pc in per_chip], axis=-1) for k in SHARDED_SMALL}


def _pack_replicated(small, names, total_rows):
    rows = []
    for k in names:
        a = small[k].reshape(-1)
        a = jnp.pad(a, (0, (-a.shape[0]) % LANES))
        rows.append(a.reshape(-1, LANES))
    return _pad_rows(jnp.concatenate(rows, axis=0), total_rows)


def _unpack_replicated(pack, shapes, names):
    out, r = {}, 0
    for k in names:
        size = math.prod(shapes[k])
        n = -(-size // LANES)
        out[k] = pack[r:r + n].reshape(-1)[:size].reshape(shapes[k])
        r += n
    return out


WEIGHT_ORDER = ("ab_norm_g", "ab_w_in", "sgu_norm_g", "sgu_norm_b", "sgu_w", "sgu_bias", "q_norm_g", "k_norm_g", "ab_w_out",
                "cd_norm_g", "cd_w_in", "conv_c_w", "conv_c_b", "c_ln_g", "c_ln_b", "conv_d_w", "cd_w_out", "ffn_norm_g",
                "ffn_w_gate", "ffn_w_up", "ffn_w_down")


def kernel(x, ab_norm_g, ab_w_in, sgu_norm_g, sgu_norm_b, sgu_w, sgu_bias, q_norm_g, k_norm_g, ab_w_out, cd_norm_g, cd_w_in, conv_c_w, conv_c_b, c_ln_g, c_ln_b, conv_d_w, cd_w_out, ffn_norm_g, ffn_w_gate, ffn_w_up, ffn_w_down, loss_target, m_ab_norm_g, m_ab_w_in, m_sgu_norm_g, m_sgu_norm_b, m_sgu_w, m_sgu_bias, m_q_norm_g, m_k_norm_g, m_ab_w_out, m_cd_norm_g, m_cd_w_in, m_conv_c_w, m_conv_c_b, m_c_ln_g, m_c_ln_b, m_conv_d_w, m_cd_w_out, m_ffn_norm_g, m_ffn_w_gate, m_ffn_w_up, m_ffn_w_down, v_ab_norm_g, v_ab_w_in, v_sgu_norm_g, v_sgu_norm_b, v_sgu_w, v_sgu_bias, v_q_norm_g, v_k_norm_g, v_ab_w_out, v_cd_norm_g, v_cd_w_in, v_conv_c_w, v_conv_c_b, v_c_ln_g, v_c_ln_b, v_conv_d_w, v_cd_w_out, v_ffn_norm_g, v_ffn_w_gate, v_ffn_w_up, v_ffn_w_down):
    args = dict(locals())
    ws = {k: args[k] for k in WEIGHT_ORDER}
    ms = {k: args["m_" + k] for k in WEIGHT_ORDER}
    vs = {k: args["v_" + k] for k in WEIGHT_ORDER}
    small_names = [k for k in WEIGHT_ORDER if k not in BIG]
    t, d = x.shape[1:]

    for group in (ws, ms, vs):
        for k in ROWS_MINOR:
            group[k] = jnp.swapaxes(group[k], 1, 2)
    ex = _Exchange(enabled=True)
    ex.w[A_IN], ex.far[A_IN] = _stage_own("stage_ab_w_in", ws["ab_w_in"], 0, BF16, far_slab=True)
    own_small = _pack_sharded({k: ws[k][0] for k in SHARDED_SMALL})
    ex.w[SMALL_SHARDED] = _stage_own("stage_small", own_small[None], 0, F32)
    rest = [u for u in UNITS if u != A_IN]
    x2 = x.reshape(t, d)
    h0, staged = ex.run(_stage_rest_and_norm, x2, ws["ab_norm_g"], [(ws[name], layer) for name, layer in rest],
                        phases=[("gather_near", [A_IN]), ("gather", [SMALL_SHARDED])])
    ex.w.update(zip(rest, staged))
    sp = _gathered_small(ex.w[SMALL_SHARDED].reshape(N_CHIPS, SHARDED_ROWS, LANES), {k: ws[k].shape[1:] for k in SHARDED_SMALL})
    for k in REPLICATED_SMALL:
        sp[k] = ws[k] if k == "ffn_norm_g" else ws[k][0]

    loss, grad_x, g_small = _local_step(x2, loss_target.reshape(t, d), ex, sp, h0)

    shapes = {k: ws[k].shape for k in REPLICATED_SMALL}
    shapes["loss"] = (1,)
    g_small["loss"] = loss
    join_last, settle = ex.ride_for([("join", [A_IN])])
    late, joined = _all_reduce_small(_pack_replicated(g_small, REPLICATED_LATE, REPLICATED_LATE_ROWS), join_last)
    settle(joined)
    grad = _unpack_sharded(ex.done[SMALL_SHARDED].reshape(SHARDED_ROWS, LANES), {k: ws[k].shape for k in SHARDED_SMALL})
    grad.update(_unpack_replicated(ex.done[REPLICATED_UNIT].reshape(REPLICATED_EARLY_ROWS, LANES), shapes, REPLICATED_EARLY))
    grad.update(_unpack_replicated(late, shapes, REPLICATED_LATE))
    loss = grad.pop("loss")[0]

    delta, new_m, new_v = {}, {}, {}
    for group in BIG_BY_LAYERS:
        g_layers = [[ex.done[(k, layer)] for layer in range(ws[k].shape[0])] for k in group]
        results = _adamw_big("adamw_" + group[0], [ws[k] for k in group], g_layers, [ms[k] for k in group], [vs[k] for k in group])
        for k, outs in zip(group, results):
            if k in ROWS_MINOR:
                outs = [jnp.swapaxes(o, 1, 2) for o in outs]
            grad[k], delta[k], new_m[k], new_v[k] = outs
    d_s, m_s, v_s = _adamw_small([ws[k] for k in small_names], [grad[k] for k in small_names],
                                 [ms[k] for k in small_names], [vs[k] for k in small_names])
    for j, k in enumerate(small_names):
        delta[k], new_m[k], new_v[k] = d_s[j], m_s[j], v_s[j]

    return (loss, grad_x[None], *[grad[k] for k in WEIGHT_ORDER], *[delta[k] for k in WEIGHT_ORDER],
            *[new_m[k] for k in WEIGHT_ORDER], *[new_v[k] for k in WEIGHT_ORDER])
```

```python
import functools
import math

import jax
import jax.numpy as jnp
from jax import lax
from jax.experimental import pallas as pl
from jax.experimental.pallas import tpu as pltpu

F32 = jnp.float32
BF16 = jnp.bfloat16
SDS = jax.ShapeDtypeStruct

N_CHIPS = 4
EPS = 1e-6
NEG_INF = -1e30
CHUNK = 128
A_GROUPS = 4
A_WIDTH = 512
N_DIL = 3
DIL_RATES = (1, 4, 16)
HEAD_DIM = 64
B_WIDTH = 512
ROPE_DIM = 16
ROPE_THETA = 500000.0
C_WIDTH = 512
C_KERNEL = 31
D_KERNEL = 3
HALO = 32
ATT_BLOCK = 128
LANES = 128

ADAM_LR = 0.001
ADAM_B1 = 0.9
ADAM_B2 = 0.999
ADAM_EPS = 1e-08
ADAM_WD = 0.01
ADAM_STEP = 10

VMEM_LIMIT = 56 * 1024 * 1024

NN = (((1,), (0,)), ((), ()))
NT = (((1,), (1,)), ((), ()))
TN = (((0,), (0,)), ((), ()))

TILES = {"proj_in": 2048, "proj_out": 1024, "ffn_in": 1024, "ffn_out": 1024, "ffn_dact": 512, "dgrad_cols": 512,
         "dgrad_rows": 1024, "wgrad": 4096}


def _params(sem=None, collective_id=None):
    return pltpu.CompilerParams(dimension_semantics=sem, vmem_limit_bytes=VMEM_LIMIT, collective_id=collective_id)


def _bf(v):
    return v if v.dtype == BF16 else v.astype(BF16)


def _dot(a, b, dims):
    return lax.dot_general(_bf(a), _bf(b), dims, preferred_element_type=F32)


def _dot_hi(a, b):
    return jnp.dot(a, b, precision=lax.Precision.HIGHEST, preferred_element_type=F32)


def _sigmoid(v):
    return 0.5 * jnp.tanh(0.5 * v) + 0.5


def _gelu(v):
    return 0.5 * v * (1.0 + lax.erf(v * (1.0 / math.sqrt(2.0))))


def _gelu_grad(v):
    cdf = 0.5 * (1.0 + lax.erf(v * (1.0 / math.sqrt(2.0))))
    return cdf + v * jnp.exp(-0.5 * v * v) * (1.0 / math.sqrt(2.0 * math.pi))


def _segment_mean_matrix(seg, scale=None):
    r = lax.broadcasted_iota(jnp.int32, (LANES, LANES), 0) // seg
    c = lax.broadcasted_iota(jnp.int32, (LANES, LANES), 1) // seg
    return jnp.where(r == c, (1.0 / seg) if scale is None else scale, 0.0).astype(BF16)


def _segment_dot(v, seg):
    hi = v.astype(BF16)
    lo = (v - hi.astype(F32)).astype(BF16)
    return jnp.dot(hi, seg, preferred_element_type=F32) + jnp.dot(lo, seg, preferred_element_type=F32)


MESH = pl.DeviceIdType.MESH
ANY = pl.BlockSpec(memory_space=pl.ANY)


def _position():
    x, y, c = lax.axis_index("x"), lax.axis_index("y"), lax.axis_index("c")
    others = [(1 - x, y), (x, 1 - y), (1 - x, 1 - y)]
    return x, y, c, 2 * x + y, others


class _Ride:
    def __init__(self, ins, bufs, new_outs, sem_shapes, start, finish, reach):
        self.ins, self.bufs, self.new_outs, self.sem_shapes = list(ins), list(bufs), list(new_outs), list(sem_shapes)
        self.start, self.finish = start, finish
        self.reach = frozenset(reach)

    def entry_barrier(self):
        x, y, c, _, others = _position()
        peers = ([(x, y, 1 - c)] if "sibling" in self.reach else []) + ([(qx, qy, c) for qx, qy in others] if "chips" in self.reach else [])
        barrier = pltpu.get_barrier_semaphore()
        for peer in peers:
            pl.semaphore_signal(barrier, inc=1, device_id=peer, device_id_type=MESH)
        pl.semaphore_wait(barrier, len(peers))

    @property
    def collective_id(self):
        return {frozenset(["sibling"]): 0, frozenset(["chips"]): 1, frozenset(["sibling", "chips"]): 2}[self.reach]


def _ride_both(a, b):
    na = (len(a.ins), len(a.bufs), len(a.new_outs), len(a.sem_shapes))

    def split(ins, bufs, new, sems):
        return ((ins[:na[0]], bufs[:na[1]], new[:na[2]], sems[:na[3]]), (ins[na[0]:], bufs[na[1]:], new[na[2]:], sems[na[3]:]))

    def start(*refs):
        ra, rb = split(*refs)
        a.start(*ra)
        b.start(*rb)

    def finish(*refs):
        ra, rb = split(*refs)
        a.finish(*ra)
        b.finish(*rb)

    return _Ride(a.ins + b.ins, a.bufs + b.bufs, a.new_outs + b.new_outs, a.sem_shapes + b.sem_shapes, start, finish,
                 a.reach | b.reach)


def _call(body, *, grid, in_specs, out_specs, out_shape, operands, name, scratch_shapes=(), aliases=None, ride=None,
          prefetch=None):
    off = 0 if prefetch is None else 1
    lead = [] if prefetch is None else [prefetch]

    params = _params(collective_id=None if ride is None else ride.collective_id)

    def launch(kernel_body, in_specs_, out_specs_, out_shape_, scratch_, aliases_, *args):
        if prefetch is None:
            return pl.pallas_call(kernel_body, grid=grid, in_specs=in_specs_, out_specs=out_specs_, out_shape=out_shape_,
                                  scratch_shapes=scratch_, input_output_aliases=aliases_, name=name,
                                  compiler_params=params)(*args)
        spec = pltpu.PrefetchScalarGridSpec(num_scalar_prefetch=1, grid=grid, in_specs=in_specs_, out_specs=out_specs_,
                                            scratch_shapes=scratch_)
        return pl.pallas_call(kernel_body, grid_spec=spec, out_shape=out_shape_, input_output_aliases=aliases_, name=name,
                              compiler_params=params)(*lead, *args)

    if ride is None:
        return launch(body, list(in_specs), out_specs, out_shape, list(scratch_shapes), dict(aliases or {}), *operands)
    multi = isinstance(out_shape, (list, tuple))
    out_shapes = list(out_shape) if multi else [out_shape]
    o_specs = list(out_specs) if multi else [out_specs]
    n_in, n_out, n_scr = off + len(operands), len(out_shapes), len(scratch_shapes)
    n_ri, n_rb, n_rn = len(ride.ins), len(ride.bufs), len(ride.new_outs)

    def carrying(*refs):
        k = n_in
        r_ins = refs[k:k + n_ri]
        k += n_ri + n_rb
        outs = refs[k:k + n_out]
        k += n_out
        r_bufs = refs[k:k + n_rb]
        k += n_rb
        r_new = refs[k:k + n_rn]
        k += n_rn
        scratch = refs[k:k + n_scr]
        sems = refs[k + n_scr:]
        first, last = None, None
        for axis, size in enumerate(grid):
            pid = pl.program_id(axis)
            first = (pid == 0) if first is None else first & (pid == 0)
            last = (pid == size - 1) if last is None else last & (pid == size - 1)

        @pl.when(first)
        def _():
            ride.entry_barrier()
            ride.start(r_ins, r_bufs, r_new, sems)

        body(*refs[:n_in], *outs, *scratch)

        @pl.when(last)
        def _():
            ride.finish(r_ins, r_bufs, r_new, sems)

    all_aliases = dict(aliases or {})
    for j in range(n_rb):
        all_aliases[n_in + n_ri + j] = n_out + j
    res = launch(
        carrying, list(in_specs) + [ANY] * (n_ri + n_rb), o_specs + [ANY] * (n_rb + n_rn),
        out_shapes + [SDS(b.shape, b.dtype) for b in ride.bufs] + ride.new_outs,
        list(scratch_shapes) + [pltpu.SemaphoreType.DMA(s) for s in ride.sem_shapes], all_aliases,
        *operands, *ride.ins, *ride.bufs)
    outs = res[:n_out]
    return (list(outs) if multi else outs[0]), list(res[n_out:])


def _run_ride(name, ride):
    n_ri, n_rb, n_rn = len(ride.ins), len(ride.bufs), len(ride.new_outs)

    def body(*refs):
        r_ins = refs[:n_ri]
        r_bufs = refs[n_ri + n_rb:n_ri + 2 * n_rb]
        r_new = refs[n_ri + 2 * n_rb:n_ri + 2 * n_rb + n_rn]
        sems = refs[n_ri + 2 * n_rb + n_rn:]
        ride.entry_barrier()
        ride.start(r_ins, r_bufs, r_new, sems)
        ride.finish(r_ins, r_bufs, r_new, sems)

    return list(pl.pallas_call(
        body, in_specs=[ANY] * (n_ri + n_rb), out_specs=[ANY] * (n_rb + n_rn),
        out_shape=[SDS(b.shape, b.dtype) for b in ride.bufs] + ride.new_outs,
        scratch_shapes=[pltpu.SemaphoreType.DMA(s) for s in ride.sem_shapes],
        input_output_aliases={n_ri + j: j for j in range(n_rb)}, name=name,
        compiler_params=pltpu.CompilerParams(collective_id=ride.collective_id))(*ride.ins, *ride.bufs))


def _whole(ref, p):
    return ref[...]


def _slab(ref, p):
    return ref[p]


def _matmul(name, grid, pairs, extras, outs, dims, epi, *, slabs=1, n_acc=1, ride=None):
    n_pairs, n_ex, n_out = len(pairs), len(extras), len(outs)

    def body(*refs):
        ab = refs[:2 * n_pairs]
        ex = refs[2 * n_pairs:2 * n_pairs + n_ex]
        out_refs = refs[2 * n_pairs + n_ex:2 * n_pairs + n_ex + n_out]
        pids = tuple(pl.program_id(a) for a in range(len(grid)))
        parts = [None] * n_acc
        for p in range(slabs):
            for j, (_, _, a_pick, _, _, b_pick, acc) in enumerate(pairs):
                d = _dot(a_pick(ab[2 * j], p), b_pick(ab[2 * j + 1], p), dims)
                parts[acc] = d if parts[acc] is None else parts[acc] + d
        epi(parts, ex, out_refs, pids)

    operands, in_specs = [], []
    for a, a_spec, _, b, b_spec, _, _ in pairs:
        operands += [a, b]
        in_specs += [a_spec, b_spec]
    for e, e_spec in extras:
        operands.append(e)
        in_specs.append(e_spec)
    return _call(body, grid=grid, in_specs=in_specs, out_specs=[o[1] for o in outs], out_shape=[o[0] for o in outs],
                 operands=operands, name=name, ride=ride)


def _rms_rows(v, g):
    r = lax.rsqrt(jnp.mean(v * v, axis=-1, keepdims=True) + EPS)
    return v * r * g


def _rms_fwd(name, x, g):
    t, d = x.shape
    tm = 512

    def body(x_ref, g_ref, o_ref):
        o_ref[...] = _rms_rows(x_ref[...], g_ref[...]).astype(BF16)

    return pl.pallas_call(
        body, grid=(t // tm,),
        in_specs=[pl.BlockSpec((tm, d), lambda i: (i, 0)), pl.BlockSpec((1, d), lambda i: (0, 0))],
        out_specs=pl.BlockSpec((tm, d), lambda i: (i, 0)), out_shape=SDS((t, d), BF16), name=name,
        compiler_params=_params())(x, g)


def _epi_residual_norm(accs, ex, outs, pids):
    x_new = accs[0] + ex[0][...]
    outs[0][...] = x_new
    outs[1][...] = _rms_rows(x_new, ex[1][...]).astype(BF16)


def _epi_residual_loss(accs, ex, outs, pids):
    y = accs[0] + ex[0][...]
    err = y - ex[1][...]
    dy = err * (1.0 / err.shape[-1])
    outs[0][...] = dy
    outs[2][...] = dy.astype(BF16)

    @pl.when(pids[0] == 0)
    def _():
        outs[1][...] = jnp.zeros_like(outs[1])

    outs[1][...] += jnp.sum(err * err)


def _epi_rms_bwd(accs, ex, outs, pids):
    dh = accs[0]
    xv, g, res = ex[0][...], ex[1][...], ex[2][...]
    r = lax.rsqrt(jnp.mean(xv * xv, axis=-1, keepdims=True) + EPS)
    xh = xv * r
    dy = dh * g
    dx = res + r * (dy - xh * jnp.mean(dy * xh, axis=-1, keepdims=True))
    outs[0][...] = dx
    if len(outs) > 2:
        outs[2][...] = dx.astype(BF16)

    @pl.when(pids[0] == 0)
    def _():
        outs[1][...] = jnp.zeros_like(outs[1])

    outs[1][...] += jnp.sum(dh * xh, axis=0, keepdims=True)


def _row_spec(tm, d):
    return pl.BlockSpec((tm, d), lambda i, *_: (i, 0))


def _const_spec(shape):
    nd = len(shape)
    return pl.BlockSpec(shape, lambda *_: (0,) * nd)


def _proj_in(name, h, w, layer, ride=None):
    t, d = h.shape
    n4 = w.shape[-1]
    tm = TILES["proj_in"]

    def epi(accs, ex, outs, pids):
        outs[0][...] = accs[0].astype(BF16)

    res = _matmul(
        name, (N_CHIPS, t // tm),
        [(h, pl.BlockSpec((tm, d), lambda p, i: (i, 0)), _whole,
          w, pl.BlockSpec((None, None, d, n4), lambda p, i: (p, layer, 0, 0)), _whole, 0)],
        [], [(SDS((t, N_CHIPS * n4), BF16), pl.BlockSpec((tm, n4), lambda p, i: (i, p)))],
        NN, epi, ride=ride)
    return res[0] if ride is None else (res[0][0], res[1])


def _proj_in_near(name, h, w, ride=None):
    t, d = h.shape
    n4 = w.shape[-1]
    tm = TILES["proj_in"]

    def shard(j, s):
        return j + (j >= N_CHIPS - 1 - s[1]).astype(jnp.int32)

    def body(s_ref, h_ref, w_ref, o_ref):
        o_ref[...] = _dot(h_ref[...], w_ref[...], NN).astype(BF16)

    return _call(
        body, grid=(N_CHIPS - 1, t // tm),
        in_specs=[pl.BlockSpec((tm, d), lambda j, i, s: (i, 0)),
                  pl.BlockSpec((None, None, d, n4), lambda j, i, s: (shard(j, s), 0, 0, 0))],
        out_specs=pl.BlockSpec((tm, n4), lambda j, i, s: (i, shard(j, s))), out_shape=SDS((t, N_CHIPS * n4), BF16),
        operands=[h, w], name=name, ride=ride, prefetch=_mesh_scalars())


def _proj_in_far(name, h, slab, proj, w):
    t, d = h.shape
    n4 = slab.shape[-1]
    tm = TILES["proj_in"]

    def body(s_ref, h_ref, slab_ref, proj_in, w_in, o_ref, w_ref):
        shard = slab_ref[...]
        o_ref[...] = _dot(h_ref[...], shard, NN).astype(BF16)
        w_ref[...] = shard

    proj, w_full = _call(
        body, grid=(t // tm,),
        in_specs=[pl.BlockSpec((tm, d), lambda i, s: (i, 0)), pl.BlockSpec((d, n4), lambda i, s: (0, 0)), ANY, ANY],
        out_specs=[pl.BlockSpec((tm, n4), lambda i, s: (i, N_CHIPS - 1 - s[1])),
                   pl.BlockSpec((None, d, n4), lambda i, s: (N_CHIPS - 1 - s[1], 0, 0))],
        out_shape=[SDS(proj.shape, BF16), SDS((N_CHIPS, d, n4), BF16)],
        operands=[h, slab.reshape(d, n4), proj, w.reshape(N_CHIPS, d, n4)], aliases={3: 0, 4: 1}, name=name,
        prefetch=_mesh_scalars())
    return proj, w_full.reshape(w.shape)


def _proj_out(name, a, w, x, g_next=None, target=None):
    t, k = a.shape
    d = w.shape[-1]
    tm = TILES["proj_out"]
    if target is None:
        extras = [(x, _row_spec(tm, d)), (g_next, _const_spec((1, d)))]
        outs = [(SDS((t, d), F32), _row_spec(tm, d)), (SDS((t, d), BF16), _row_spec(tm, d))]
        epi = _epi_residual_norm
    else:
        extras = [(x, _row_spec(tm, d)), (target, _row_spec(tm, d))]
        outs = [(SDS((t, d), F32), _row_spec(tm, d)), (SDS((8, LANES), F32), _const_spec((8, LANES))),
                (SDS((t, d), BF16), _row_spec(tm, d))]
        epi = _epi_residual_loss
    return _matmul(name, (t // tm,), [(a, _row_spec(tm, k), _whole, w, _const_spec((k, d)), _whole, 0)], extras, outs, NN, epi)


def _ffn_in(name, h, wg, wu, layer, ride=None):
    t, d = h.shape
    n4 = wg.shape[-2]
    tm = TILES["ffn_in"]

    def epi(accs, ex, outs, pids):
        gate, up = accs
        s = _sigmoid(gate)
        silu = gate * s
        outs[0][...] = (up * (s + silu - silu * s)).astype(BF16)
        outs[1][...] = silu.astype(BF16)
        outs[2][...] = (silu * up).astype(BF16)

    w_spec = pl.BlockSpec((None, None, n4, d), lambda p, i: (p, layer, 0, 0))
    h_spec = pl.BlockSpec((tm, d), lambda p, i: (i, 0))
    o = (SDS((N_CHIPS, t, n4), BF16), pl.BlockSpec((None, tm, n4), lambda p, i: (p, i, 0)))
    return _matmul(name, (N_CHIPS, t // tm),
                   [(h, h_spec, _whole, wg, w_spec, _whole, 0), (h, h_spec, _whole, wu, w_spec, _whole, 1)], [],
                   [o, o, o], NT, epi, n_acc=2, ride=ride)


def _ffn_out(name, act, wd, layer, x, g_next=None, target=None, ride=None):
    _, t, n4 = act.shape
    d = wd.shape[-1]
    tm = TILES["ffn_out"]
    xs = _row_spec(tm, d)
    if target is None:
        extras = [(x, xs), (g_next, _const_spec((1, d)))]
        outs = [(SDS((t, d), F32), xs), (SDS((t, d), BF16), xs)]
        epi = _epi_residual_norm
    else:
        extras = [(x, xs), (target, xs)]
        outs = [(SDS((t, d), F32), xs), (SDS((8, LANES), F32), _const_spec((8, LANES))), (SDS((t, d), BF16), xs)]
        epi = _epi_residual_loss
    return _matmul(
        name, (t // tm,),
        [(act, pl.BlockSpec((N_CHIPS, tm, n4), lambda i: (0, i, 0)), _slab,
          wd, pl.BlockSpec((N_CHIPS, None, n4, d), lambda i: (0, layer, 0, 0)), _slab, 0)],
        extras, outs, NN, epi, slabs=N_CHIPS, ride=ride)


DACT_SLOTS = 3


def _ffn_dact(name, g, wd, layer, gate, up, ride=None):
    t, d = g.shape
    n4 = wd.shape[-2]
    tm = TILES["ffn_dact"]
    steps = t // tm
    assert steps >= DACT_SLOTS

    def body(g_ref, w_ref, gate_hbm, up_hbm, dgate_ref, dup_ref, gate_buf, up_buf, sems):
        i = pl.program_id(0)

        def fetches(step, slot):
            first_row = step * tm
            rows = pl.ds(first_row if isinstance(first_row, int) else pl.multiple_of(first_row, tm), tm)
            return (pltpu.make_async_copy(gate_hbm.at[:, rows, :], gate_buf.at[slot], sems.at[0, slot]),
                    pltpu.make_async_copy(up_hbm.at[:, rows, :], up_buf.at[slot], sems.at[1, slot]))

        @pl.when(i == 0)
        def _():
            for first in range(DACT_SLOTS - 1):
                for fetch in fetches(first, first):
                    fetch.start()

        for slot in range(DACT_SLOTS):
            @pl.when(i % DACT_SLOTS == slot)
            def _():
                ahead = i + DACT_SLOTS - 1

                @pl.when(ahead < steps)
                def _():
                    for fetch in fetches(ahead, (slot + DACT_SLOTS - 1) % DACT_SLOTS):
                        fetch.start()

                for fetch in fetches(i, slot):
                    fetch.wait()
                gv = g_ref[...]
                for p in range(N_CHIPS):
                    dact = _dot(gv, w_ref[p], NT)
                    dgate_ref[p] = (dact * gate_buf[slot, p].astype(F32)).astype(BF16)
                    dup_ref[p] = (dact * up_buf[slot, p].astype(F32)).astype(BF16)

    blk = pl.BlockSpec((N_CHIPS, tm, n4), lambda i: (0, i, 0))
    ring = pltpu.VMEM((DACT_SLOTS, N_CHIPS, tm, n4), BF16)
    return _call(
        body, grid=(steps,),
        in_specs=[_row_spec(tm, d), pl.BlockSpec((N_CHIPS, None, n4, d), lambda i: (0, layer, 0, 0)), ANY, ANY],
        out_specs=[blk, blk], out_shape=[SDS((N_CHIPS, t, n4), BF16)] * 2, operands=[g, wd, gate, up], name=name,
        scratch_shapes=[ring, ring, pltpu.SemaphoreType.DMA((2, DACT_SLOTS))], ride=ride)


def _copy_epi(accs, ex, outs, pids):
    for a, o in zip(accs, outs):
        o[...] = a.astype(o.dtype)


def _dgrad_cols(name, dz_list, w_list, layer, three_d, x, g, res, bf16_copy=True, w_rows=False, ride=None):
    t, d = x.shape
    n4 = w_list[0].shape[-2 if w_rows else -1]
    tm = TILES["dgrad_cols"]
    if three_d:
        zs, z_pick = pl.BlockSpec((N_CHIPS, tm, n4), lambda i: (0, i, 0)), _slab
    else:
        zs, z_pick = _row_spec(tm, N_CHIPS * n4), (lambda ref, p: ref[:, p * n4:(p + 1) * n4])
    ws = pl.BlockSpec((N_CHIPS, None) + ((n4, d) if w_rows else (d, n4)), lambda i: (0, layer, 0, 0))
    xs = _row_spec(tm, d)
    return _matmul(
        name, (t // tm,), [(dz, zs, z_pick, w, ws, _slab, 0) for dz, w in zip(dz_list, w_list)],
        [(x, xs), (g, _const_spec((1, d))), (res, xs)],
        [(SDS((t, d), F32), xs), (SDS((1, d), F32), _const_spec((1, d)))] + ([(SDS((t, d), BF16), xs)] if bf16_copy else []),
        NN if w_rows else NT, _epi_rms_bwd, slabs=N_CHIPS, ride=ride)


def _dgrad_rows(name, g, w):
    t, d = g.shape
    k = w.shape[0]
    tm = TILES["dgrad_rows"]
    return _matmul(name, (t // tm,), [(g, _row_spec(tm, d), _whole, w, _const_spec((k, d)), _whole, 0)], [],
                   [(SDS((t, k), F32), _row_spec(tm, k))], NT, _copy_epi)[0]


A_TILE = 256


def _a_common(p_ref, lg_ref, lb_ref):
    pv = p_ref[...].astype(F32)
    a = _gelu(pv)
    u, v = a[:, :A_WIDTH], a[:, A_WIDTH:]
    vc = v - jnp.mean(v, axis=-1, keepdims=True)
    rs = lax.rsqrt(jnp.mean(vc * vc, axis=-1, keepdims=True) + EPS)
    vhat = vc * rs
    vn = vhat * lg_ref[...] + lb_ref[...]
    return pv, u, vhat, rs, vn.astype(BF16)


def _tril_weights(w_ref, g):
    r = lax.broadcasted_iota(jnp.int32, (CHUNK, CHUNK), 0)
    c = lax.broadcasted_iota(jnp.int32, (CHUNK, CHUNK), 1)
    return jnp.where(c <= r, w_ref[g], 0.0).astype(BF16), c <= r


def _mixer_a_fwd(proj, lg, lb, w, bias_t):
    t = proj.shape[0]

    def body(p_ref, lg_ref, lb_ref, w_ref, bt_ref, o_ref):
        _, u, _, _, vnb = _a_common(p_ref, lg_ref, lb_ref)
        for g in range(A_GROUPS):
            wt, _ = _tril_weights(w_ref, g)
            cs = slice(g * CHUNK, (g + 1) * CHUNK)
            for ch in range(A_TILE // CHUNK):
                rs_ = slice(ch * CHUNK, (ch + 1) * CHUNK)
                mixed = _dot(wt, vnb[rs_, cs], NN) + bt_ref[:, g:g + 1]
                o_ref[rs_, cs] = (u[rs_, cs] * mixed).astype(BF16)

    return pl.pallas_call(
        body, grid=(t // A_TILE,),
        in_specs=[pl.BlockSpec((A_TILE, 2 * A_WIDTH), lambda i: (i, 0)), _const_spec((1, A_WIDTH)),
                  _const_spec((1, A_WIDTH)), _const_spec((A_GROUPS, CHUNK, CHUNK)), _const_spec((CHUNK, A_GROUPS))],
        out_specs=pl.BlockSpec((A_TILE, A_WIDTH), lambda i: (i, 0)), out_shape=SDS((t, A_WIDTH), BF16),
        name="mixer_a_fwd", compiler_params=_params())(proj, lg, lb, w, bias_t)


def _mixer_a_bwd(proj, dcat, lg, lb, w, bias_t):
    t = proj.shape[0]

    def body(p_ref, da_ref, lg_ref, lb_ref, w_ref, bt_ref, dp_ref, dw_ref, dbt_ref, dlg_ref, dlb_ref, du_scr, dvn_scr):
        @pl.when(pl.program_id(0) == 0)
        def _():
            dw_ref[...] = jnp.zeros_like(dw_ref)
            dbt_ref[...] = jnp.zeros_like(dbt_ref)
            dlg_ref[...] = jnp.zeros_like(dlg_ref)
            dlb_ref[...] = jnp.zeros_like(dlb_ref)

        pv, u, vhat, rs, vnb = _a_common(p_ref, lg_ref, lb_ref)
        da = da_ref[...]
        for g in range(A_GROUPS):
            wt, keep = _tril_weights(w_ref, g)
            cs = slice(g * CHUNK, (g + 1) * CHUNK)
            for ch in range(A_TILE // CHUNK):
                rs_ = slice(ch * CHUNK, (ch + 1) * CHUNK)
                vg = vnb[rs_, cs]
                mixed = _dot(wt, vg, NN) + bt_ref[:, g:g + 1]
                du_scr[rs_, cs] = da[rs_, cs] * mixed
                dmx = da[rs_, cs] * u[rs_, cs]
                dw_ref[g] += jnp.where(keep, _dot(dmx, vg, NT), 0.0)
                dvn_scr[rs_, cs] = _dot(wt, dmx, TN)
                dbt_ref[:, g:g + 1] += jnp.sum(dmx, axis=1, keepdims=True)
        dvn = dvn_scr[...]
        dlg_ref[...] += jnp.sum(dvn * vhat, axis=0, keepdims=True)
        dlb_ref[...] += jnp.sum(dvn, axis=0, keepdims=True)
        dvh = dvn * lg_ref[...]
        dv = rs * (dvh - jnp.mean(dvh, axis=-1, keepdims=True) - vhat * jnp.mean(dvh * vhat, axis=-1, keepdims=True))
        gp = _gelu_grad(pv)
        dp_ref[:, :A_WIDTH] = (du_scr[...] * gp[:, :A_WIDTH]).astype(BF16)
        dp_ref[:, A_WIDTH:] = (dv * gp[:, A_WIDTH:]).astype(BF16)

    return pl.pallas_call(
        body, grid=(t // A_TILE,),
        in_specs=[pl.BlockSpec((A_TILE, 2 * A_WIDTH), lambda i: (i, 0)), pl.BlockSpec((A_TILE, A_WIDTH), lambda i: (i, 0)),
                  _const_spec((1, A_WIDTH)), _const_spec((1, A_WIDTH)), _const_spec((A_GROUPS, CHUNK, CHUNK)),
                  _const_spec((CHUNK, A_GROUPS))],
        out_specs=[pl.BlockSpec((A_TILE, 2 * A_WIDTH), lambda i: (i, 0)), _const_spec((A_GROUPS, CHUNK, CHUNK)),
                   _const_spec((CHUNK, A_GROUPS)), _const_spec((1, A_WIDTH)), _const_spec((1, A_WIDTH))],
        out_shape=[SDS((t, 2 * A_WIDTH), BF16), SDS((A_GROUPS, CHUNK, CHUNK), F32), SDS((CHUNK, A_GROUPS), F32),
                   SDS((1, A_WIDTH), F32), SDS((1, A_WIDTH), F32)],
        scratch_shapes=[pltpu.VMEM((A_TILE, A_WIDTH), F32), pltpu.VMEM((A_TILE, A_WIDTH), F32)],
        name="mixer_a_bwd", compiler_params=_params())(proj, dcat, lg, lb, w, bias_t)


def _rope_tables(t):
    half = ROPE_DIM // 2
    inv_freq = ROPE_THETA ** (-jnp.arange(half, dtype=F32) * 2.0 / ROPE_DIM)
    ang = jnp.arange(t, dtype=F32)[:, None] * inv_freq[None, :]
    cos, sin = jnp.cos(ang), jnp.sin(ang)
    one = jnp.ones((t, HEAD_DIM - ROPE_DIM), F32)
    zero = jnp.zeros((t, HEAD_DIM - ROPE_DIM), F32)
    zh = jnp.zeros((t, half), F32)
    c = jnp.concatenate([cos, cos, one], axis=1)
    s1 = jnp.concatenate([-sin, zh, zero], axis=1)
    s2 = jnp.concatenate([zh, sin, zero], axis=1)
    return tuple(jnp.tile(a, (1, LANES // HEAD_DIM)) for a in (c, s1, s2))


QK_TILE = 512
QK_ROWS = 64
QK_COLS = 2 * N_DIL * B_WIDTH


CHUNKS = B_WIDTH // LANES


def _regroup_out(scr, first, out_ref, rate, tile):
    rows = tile // rate
    for rho in range(rate):
        for c in range(CHUNKS):
            out_ref[rho, :, c * LANES:(c + 1) * LANES] = scr[first + c, pl.ds(rho, rows, stride=rate), :].astype(out_ref.dtype)


def _regroup_in(x_ref, scr, rate, tile):
    rows = tile // rate
    for rho in range(rate):
        for c in range(CHUNKS):
            scr[c, pl.ds(rho, rows, stride=rate), :] = x_ref[rho, :, c * LANES:(c + 1) * LANES].astype(F32)


def _regrouped_spec(rate, tile):
    return pl.BlockSpec((rate, tile // rate, B_WIDTH), lambda i, *_: (0, i, 0))


def _qk_fwd(proj, gains, tabs, ride=None):
    t = proj.shape[0]
    col0 = 2 * A_WIDTH // 1024
    r1, r2 = DIL_RATES[1], DIL_RATES[2]

    def body(p_ref, g_ref, c_ref, s1_ref, s2_ref, o_ref, q1_ref, q2_ref, k1_ref, k2_ref, scr):
        seg = _segment_mean_matrix(HEAD_DIM)
        for r0 in range(0, QK_TILE, QK_ROWS):
            rows = slice(r0, r0 + QK_ROWS)
            c, s1, s2 = c_ref[rows, :], s1_ref[rows, :], s2_ref[rows, :]
            for ci in range(1024 // LANES):
                ls = slice(ci * LANES, (ci + 1) * LANES)
                xv = p_ref[rows, ls].astype(F32)
                r = lax.rsqrt(_segment_dot(xv * xv, seg) + EPS)
                y = xv * r * g_ref[:, ls]
                val = y * c + pltpu.roll(y, LANES - 8, axis=1) * s1 + pltpu.roll(y, 8, axis=1) * s2
                o_ref[rows, ls] = val.astype(BF16)
                scr[ci, rows, :] = val

        j = pl.program_id(1)

        @pl.when(j == 0)
        def _():
            _regroup_out(scr, CHUNKS, q1_ref, r1, QK_TILE)

        @pl.when(j == 1)
        def _():
            _regroup_out(scr, 0, q2_ref, r2, QK_TILE)

        @pl.when(j == 2)
        def _():
            _regroup_out(scr, 0, k1_ref, r1, QK_TILE)
            _regroup_out(scr, CHUNKS, k2_ref, r2, QK_TILE)

    tab = pl.BlockSpec((QK_TILE, LANES), lambda i, j: (i, 0))
    g1, g2 = SDS((r1, t // r1, B_WIDTH), BF16), SDS((r2, t // r2, B_WIDTH), BF16)
    s1_, s2_ = _regrouped_spec(r1, QK_TILE), _regrouped_spec(r2, QK_TILE)
    return _call(
        body, grid=(t // QK_TILE, QK_COLS // 1024),
        in_specs=[pl.BlockSpec((QK_TILE, 1024), lambda i, j: (i, col0 + j)), pl.BlockSpec((1, 1024), lambda i, j: (0, j)),
                  tab, tab, tab],
        out_specs=[pl.BlockSpec((QK_TILE, 1024), lambda i, j: (i, j)), s1_, s2_, s1_, s2_],
        out_shape=[SDS((t, QK_COLS), BF16), g1, g2, g1, g2],
        scratch_shapes=[pltpu.VMEM((2 * CHUNKS, QK_TILE, LANES), F32)],
        operands=[proj, gains, *tabs], name="qk_norm_rope_fwd", ride=ride)


PERM_TILE = 512


def _permute(name, items):
    t = items[0][0].shape[0]
    n = len(items)

    def body(*refs):
        scr = refs[-1]
        for x_ref, o_ref, (_, _, rate) in zip(refs[:n], refs[n:2 * n], items):
            for ci in range(CHUNKS):
                scr[ci] = x_ref[:, ci * LANES:(ci + 1) * LANES].astype(F32)
            _regroup_out(scr, 0, o_ref, rate, PERM_TILE)

    return pl.pallas_call(
        body, grid=(t // PERM_TILE,),
        in_specs=[pl.BlockSpec((PERM_TILE, B_WIDTH), functools.partial(lambda cb, i: (i, cb), cb)) for _, cb, _ in items],
        out_specs=[_regrouped_spec(rate, PERM_TILE) for _, _, rate in items],
        out_shape=[SDS((rate, t // rate, B_WIDTH), a.dtype) for a, _, rate in items],
        scratch_shapes=[pltpu.VMEM((CHUNKS, PERM_TILE, LANES), F32)],
        name=name, compiler_params=_params())(*[a for a, _, _ in items])


def _head_lane_mask(h):
    lane = lax.broadcasted_iota(jnp.int32, (1, LANES), 1)
    return (lane < HEAD_DIM) if h == 0 else (lane >= HEAD_DIM)


def _attn_fwd(name, q, k, v, ride=None):
    rate, length = q[0].shape[0], q[0].shape[1]
    nb = length // ATT_BLOCK
    scale = HEAD_DIM ** -0.5

    def body(q_ref, kc_ref, kp_ref, vc_ref, vp_ref, o_ref, l_ref):
        n = pl.program_id(1)
        qi = lax.broadcasted_iota(jnp.int32, (ATT_BLOCK, 2 * ATT_BLOCK), 0)
        cj = lax.broadcasted_iota(jnp.int32, (ATT_BLOCK, 2 * ATT_BLOCK), 1)
        has_prev = jnp.where(n > 0, 0, 2 * ATT_BLOCK)
        mask = ((cj < ATT_BLOCK) & (cj >= qi + has_prev)) | ((cj >= ATT_BLOCK) & (cj - ATT_BLOCK <= qi))
        heads = [(hp, h) for hp in range(CHUNKS) for h in range(2)]
        q2, k2, v2 = {}, {}, {}
        for hp in range(CHUNKS):
            ls = slice(hp * LANES, (hp + 1) * LANES)
            q2[hp] = q_ref[:, ls]
            k2[hp] = jnp.concatenate([kp_ref[:, ls], kc_ref[:, ls]], axis=0)
            v2[hp] = jnp.concatenate([vp_ref[:, ls], vc_ref[:, ls]], axis=0)
        scores = {}
        for hp, h in heads:
            scores[hp, h] = _dot(jnp.where(_head_lane_mask(h), q2[hp], jnp.zeros_like(q2[hp])), k2[hp], NT) * scale
        probs, lses = {}, {}
        for hp, h in heads:
            s = jnp.where(mask, scores[hp, h], NEG_INF)
            m = jnp.max(s, axis=1, keepdims=True)
            p = jnp.exp(s - m)
            den = jnp.sum(p, axis=1, keepdims=True)
            lses[hp, h] = m + jnp.log(den)
            probs[hp, h] = (p / den).astype(BF16)
        for hp in range(CHUNKS):
            ls = slice(hp * LANES, (hp + 1) * LANES)
            o_acc = None
            for h in range(2):
                o = _dot(probs[hp, h], jnp.where(_head_lane_mask(h), v2[hp], jnp.zeros_like(v2[hp])), NN)
                o_acc = o if o_acc is None else o_acc + o
            o_ref[:, ls] = o_acc
            zeros = jnp.zeros((ATT_BLOCK, LANES), F32)
            l_ref[:, ls] = jnp.where(_head_lane_mask(1), lses[hp, 1] + zeros, lses[hp, 0] + zeros)

    def cur(cb):
        return pl.BlockSpec((None, ATT_BLOCK, B_WIDTH), lambda r, n: (r, n, cb))

    def prev(cb):
        return pl.BlockSpec((None, ATT_BLOCK, B_WIDTH), lambda r, n: (r, jnp.maximum(n - 1, 0), cb))

    out = pl.BlockSpec((None, ATT_BLOCK, B_WIDTH), lambda r, n: (r, n, 0))
    return _call(
        body, grid=(rate, nb),
        in_specs=[cur(q[1]), cur(k[1]), prev(k[1]), cur(v[1]), prev(v[1])],
        out_specs=[out, out], out_shape=[SDS((rate, length, B_WIDTH), F32)] * 2,
        operands=[q[0], k[0], k[0], v[0], v[0]], name=name, ride=ride)


def _attn_merge(a_out, o_list, l_list):
    t = a_out.shape[0]
    tm = PERM_TILE
    r1, r2 = DIL_RATES[1], DIL_RATES[2]

    def body(a_ref, o0, o1, o2, l0, l1, l2, cat_ref, lt_ref, lt1_ref, lt2_ref, so1, so2, sl1, sl2, slt):
        _regroup_in(o1, so1, r1, tm)
        _regroup_in(l1, sl1, r1, tm)
        _regroup_in(o2, so2, r2, tm)
        _regroup_in(l2, sl2, r2, tm)
        cat_ref[:, :A_WIDTH] = a_ref[...]
        for c in range(CHUNKS):
            ls = slice(c * LANES, (c + 1) * LANES)
            lg = [l0[:, ls], sl1[c], sl2[c]]
            m = jnp.maximum(jnp.maximum(lg[0], lg[1]), lg[2])
            es = [jnp.exp(l - m) for l in lg]
            den = es[0] + es[1] + es[2]
            b = (es[0] * o0[:, ls] + es[1] * so1[c] + es[2] * so2[c]) / den
            cat_ref[:, A_WIDTH + c * LANES:A_WIDTH + (c + 1) * LANES] = b.astype(BF16)
            lt = m + jnp.log(den)
            lt_ref[:, ls] = lt
            slt[c] = lt
        _regroup_out(slt, 0, lt1_ref, r1, tm)
        _regroup_out(slt, 0, lt2_ref, r2, tm)

    blk = _row_spec(tm, B_WIDTH)
    g1, g2 = _regrouped_spec(r1, tm), _regrouped_spec(r2, tm)
    return pl.pallas_call(
        body, grid=(t // tm,), in_specs=[blk, blk, g1, g2, blk, g1, g2],
        out_specs=[_row_spec(tm, A_WIDTH + B_WIDTH), blk, g1, g2],
        out_shape=[SDS((t, A_WIDTH + B_WIDTH), BF16), SDS((t, B_WIDTH), F32), SDS((r1, t // r1, B_WIDTH), F32),
                   SDS((r2, t // r2, B_WIDTH), F32)],
        scratch_shapes=[pltpu.VMEM((CHUNKS, tm, LANES), F32)] * 5,
        name="attn_merge", compiler_params=_params())(a_out, *o_list, *l_list)


def _attn_bwd_prep(dcat, cat):
    t = dcat.shape[0]
    tm = PERM_TILE
    r1, r2 = DIL_RATES[1], DIL_RATES[2]

    def body(d_ref, b_ref, db_ref, dd_ref, db1_ref, dd1_ref, db2_ref, dd2_ref, sdb, sdd):
        seg = _segment_mean_matrix(HEAD_DIM, scale=1.0)
        for c in range(CHUNKS):
            ls = slice(c * LANES, (c + 1) * LANES)
            d = d_ref[:, ls]
            dsum = _segment_dot(d * b_ref[:, ls].astype(F32), seg)
            db_ref[:, ls] = d.astype(BF16)
            dd_ref[:, ls] = dsum
            sdb[c] = d
            sdd[c] = dsum
        _regroup_out(sdb, 0, db1_ref, r1, tm)
        _regroup_out(sdd, 0, dd1_ref, r1, tm)
        _regroup_out(sdb, 0, db2_ref, r2, tm)
        _regroup_out(sdd, 0, dd2_ref, r2, tm)

    right = pl.BlockSpec((tm, B_WIDTH), lambda i: (i, 1))
    blk = _row_spec(tm, B_WIDTH)
    g1, g2 = _regrouped_spec(r1, tm), _regrouped_spec(r2, tm)
    return pl.pallas_call(
        body, grid=(t // tm,), in_specs=[right, right], out_specs=[blk, blk, g1, g1, g2, g2],
        out_shape=[SDS((t, B_WIDTH), BF16), SDS((t, B_WIDTH), F32), SDS((r1, t // r1, B_WIDTH), BF16),
                   SDS((r1, t // r1, B_WIDTH), F32), SDS((r2, t // r2, B_WIDTH), BF16), SDS((r2, t // r2, B_WIDTH), F32)],
        scratch_shapes=[pltpu.VMEM((CHUNKS, tm, LANES), F32)] * 2,
        name="attn_bwd_prep", compiler_params=_params())(dcat, cat)


def _attn_bwd(name, q, k, v, db, lse, dd, ride=None):
    rate, length = db.shape[0], db.shape[1]
    nb = length // ATT_BLOCK
    scale = HEAD_DIM ** -0.5

    def body(qa_ref, qb_ref, k_ref, v_ref, dba_ref, dbb_ref, la_ref, lb_ref, da_ref, dbd_ref, dq_ref, dk_ref, dv_ref, carry):
        m = pl.program_id(1)

        @pl.when(m == 0)
        def _():
            carry[...] = jnp.zeros_like(carry)

        row = lax.broadcasted_iota(jnp.int32, (2 * ATT_BLOCK, ATT_BLOCK), 0)
        kj = lax.broadcasted_iota(jnp.int32, (2 * ATT_BLOCK, ATT_BLOCK), 1)
        no_next = jnp.where(m + 1 < nb, 0, 2 * ATT_BLOCK)
        mask = ((row < ATT_BLOCK) & (kj <= row)) | ((row >= ATT_BLOCK) & (kj >= row - ATT_BLOCK + no_next))
        heads = [(hp, h) for hp in range(CHUNKS) for h in range(2)]
        q2, db2, lse2, dd2, k2, v2 = {}, {}, {}, {}, {}, {}
        for hp in range(CHUNKS):
            ls = slice(hp * LANES, (hp + 1) * LANES)
            k2[hp], v2[hp] = k_ref[:, ls], v_ref[:, ls]
            q2[hp] = jnp.concatenate([qa_ref[:, ls], qb_ref[:, ls]], axis=0)
            db2[hp] = jnp.concatenate([dba_ref[:, ls], dbb_ref[:, ls]], axis=0)
            lse2[hp] = jnp.concatenate([la_ref[:, ls], lb_ref[:, ls]], axis=0)
            dd2[hp] = jnp.concatenate([da_ref[:, ls], dbd_ref[:, ls]], axis=0)
        km, scores, dps = {}, {}, {}
        for hp, h in heads:
            hm = _head_lane_mask(h)
            km[hp, h] = jnp.where(hm, k2[hp], jnp.zeros_like(k2[hp]))
            scores[hp, h] = _dot(q2[hp], km[hp, h], NT) * scale
            dps[hp, h] = _dot(db2[hp], jnp.where(hm, v2[hp], jnp.zeros_like(v2[hp])), NT)
        probs, dss = {}, {}
        for hp, h in heads:
            hm = _head_lane_mask(h)
            lse_col = jnp.max(jnp.where(hm, lse2[hp], NEG_INF), axis=1, keepdims=True)
            dd_col = jnp.max(jnp.where(hm, dd2[hp], NEG_INF), axis=1, keepdims=True)
            p = jnp.where(mask, jnp.exp(scores[hp, h] - lse_col), 0.0)
            probs[hp, h] = p.astype(BF16)
            dss[hp, h] = (p * (dps[hp, h] - dd_col) * scale).astype(BF16)
        for hp in range(CHUNKS):
            ls = slice(hp * LANES, (hp + 1) * LANES)
            dq_acc, dk_acc, dv_acc = None, None, None
            for h in range(2):
                hm = _head_lane_mask(h)
                dvc = _dot(probs[hp, h], jnp.where(hm, db2[hp], jnp.zeros_like(db2[hp])), TN)
                dqc = _dot(dss[hp, h], km[hp, h], NN)
                dkc = _dot(dss[hp, h], jnp.where(hm, q2[hp], jnp.zeros_like(q2[hp])), TN)
                dq_acc = dqc if dq_acc is None else dq_acc + dqc
                dk_acc = dkc if dk_acc is None else dk_acc + dkc
                dv_acc = dvc if dv_acc is None else dv_acc + dvc
            dq_ref[:, ls] = (dq_acc[:ATT_BLOCK] + carry[:, ls]).astype(BF16)
            carry[:, ls] = dq_acc[ATT_BLOCK:]
            dk_ref[:, ls] = dk_acc.astype(BF16)
            dv_ref[:, ls] = dv_acc.astype(BF16)

    def cur(cb):
        return pl.BlockSpec((None, ATT_BLOCK, B_WIDTH), lambda r, n: (r, n, cb))

    def nxt(cb):
        return pl.BlockSpec((None, ATT_BLOCK, B_WIDTH), lambda r, n: (r, jnp.minimum(n + 1, nb - 1), cb))

    out = cur(0)
    return _call(
        body, grid=(rate, nb),
        in_specs=[cur(q[1]), nxt(q[1]), cur(k[1]), cur(v[1]), cur(0), nxt(0), cur(0), nxt(0), cur(0), nxt(0)],
        out_specs=[out, out, out], out_shape=[SDS((rate, length, B_WIDTH), BF16)] * 3,
        scratch_shapes=[pltpu.VMEM((ATT_BLOCK, B_WIDTH), F32)],
        operands=[q[0], q[0], k[0], v[0], db, db, lse, lse, dd, dd], name=name, ride=ride)


AB_IN = 2 * A_WIDTH + 3 * N_DIL * B_WIDTH
ASM_TILE = 256


def _dproj_assemble(proj, d_a, dq, dk, dv, gains, tabs, ride=None):
    t = proj.shape[0]
    n_in = 3 * N_DIL

    def body(p_ref, da_ref, *rest):
        grads = rest[:n_in]
        g_ref, c_ref, s1_ref, s2_ref, o_ref, dg_ref = rest[n_in:n_in + 6]
        scratch = rest[n_in + 6:]

        @pl.when(pl.program_id(0) == 0)
        def _():
            dg_ref[...] = jnp.zeros_like(dg_ref)

        chunk = {}
        k_scr = 0
        for j in range(n_in):
            g = j % N_DIL
            if DIL_RATES[g] == 1:
                for ci in range(CHUNKS):
                    chunk[j, ci] = functools.partial(lambda r, ci: r[:, ci * LANES:(ci + 1) * LANES].astype(F32), grads[j], ci)
            else:
                scr = scratch[k_scr]
                k_scr += 1
                _regroup_in(grads[j], scr, DIL_RATES[g], ASM_TILE)
                for ci in range(CHUNKS):
                    chunk[j, ci] = functools.partial(lambda s, ci: s[ci], scr, ci)

        seg = _segment_mean_matrix(HEAD_DIM)
        c, s1, s2 = c_ref[...], s1_ref[...], s2_ref[...]
        o_ref[:, :2 * A_WIDTH] = da_ref[...]
        for jg in range(2 * N_DIL):
            for ci in range(CHUNKS):
                col = jg * B_WIDTH + ci * LANES
                src = slice(2 * A_WIDTH + col, 2 * A_WIDTH + col + LANES)
                xv = p_ref[:, src].astype(F32)
                r = lax.rsqrt(_segment_dot(xv * xv, seg) + EPS)
                xh = xv * r
                gain = g_ref[:, col:col + LANES]
                do = chunk[jg, ci]()
                dy = do * c + pltpu.roll(do * s1, 8, axis=1) + pltpu.roll(do * s2, LANES - 8, axis=1)
                dg_ref[:, col:col + LANES] += jnp.sum(dy * xh, axis=0, keepdims=True)
                dxh = dy * gain
                o_ref[:, src] = (r * (dxh - xh * _segment_dot(dxh * xh, seg))).astype(BF16)
        v0 = 2 * A_WIDTH + QK_COLS
        for g in range(N_DIL):
            for ci in range(CHUNKS):
                col = v0 + g * B_WIDTH + ci * LANES
                o_ref[:, col:col + LANES] = chunk[2 * N_DIL + g, ci]().astype(BF16)

    specs = [_row_spec(ASM_TILE, B_WIDTH) if r == 1 else _regrouped_spec(r, ASM_TILE) for r in DIL_RATES] * 3
    n_scr = 3 * sum(1 for r in DIL_RATES if r > 1)
    tab = _row_spec(ASM_TILE, LANES)
    return _call(
        body, grid=(t // ASM_TILE,),
        in_specs=[_row_spec(ASM_TILE, AB_IN), _row_spec(ASM_TILE, 2 * A_WIDTH)] + specs
        + [_const_spec((1, QK_COLS)), tab, tab, tab],
        out_specs=[_row_spec(ASM_TILE, AB_IN), _const_spec((1, QK_COLS))],
        out_shape=[SDS((t, AB_IN), BF16), SDS((1, QK_COLS), F32)],
        scratch_shapes=[pltpu.VMEM((CHUNKS, ASM_TILE, LANES), F32)] * n_scr,
        operands=[proj, d_a, *dq, *dk, *dv, gains, *tabs], name="dproj_assemble", ride=ride)


def _fold_heads(dg_lane):
    n = dg_lane.shape[1]

    def body(x_ref, o_ref):
        r = lax.broadcasted_iota(jnp.int32, (B_WIDTH, B_WIDTH), 0) % HEAD_DIM
        c = lax.broadcasted_iota(jnp.int32, (B_WIDTH, B_WIDTH), 1) % HEAD_DIM
        fold = jnp.where(r == c, 1.0, 0.0).astype(F32)
        for jg in range(n // B_WIDTH):
            ls = slice(jg * B_WIDTH, (jg + 1) * B_WIDTH)
            o_ref[:, ls] = _dot_hi(jnp.broadcast_to(x_ref[:, ls], (8, B_WIDTH)), fold)

    return pl.pallas_call(body, out_shape=SDS((8, n), F32), name="fold_heads", compiler_params=_params())(dg_lane)


CD_TILE = 256
TAP_ROWS = 64
CD_IN = 2 * C_WIDTH + 3 * 512


def _shifted_copies(src, dst, rows):
    dst[0, :rows] = src[...]
    for b in range(1, 8):
        dst[b, :rows - 8] = src[pl.ds(b, rows - 8), :]


def _rows_from(shifted, start, n, lanes=slice(None)):
    b = start % 8
    return shifted[b, pl.ds(start - b, n), lanes]


def _mixer_cd_fwd(proj, cw, cb, lg, lb, dw):
    t = proj.shape[0]
    per = CD_TILE // HALO

    def body(h_ref, m_ref, cw_ref, cb_ref, lg_ref, lb_ref, dw_ref, o_ref, c1_ref, c_scr, e_scr, c_sh):
        not_first = (pl.program_id(0) > 0).astype(F32)
        lanes = [slice(c * LANES, (c + 1) * LANES) for c in range(C_WIDTH // LANES)]

        def col(ref, part, ls):
            return ref[:, part * C_WIDTH + ls.start:part * C_WIDTH + ls.stop].astype(F32)

        for ls in lanes:
            c_scr[:HALO, ls] = col(h_ref, 0, ls) * _sigmoid(col(h_ref, 1, ls)) * not_first
            c_scr[HALO:, ls] = col(m_ref, 0, ls) * _sigmoid(col(m_ref, 1, ls))
            e_scr[:HALO, ls] = col(h_ref, 3, ls) * col(h_ref, 4, ls) * not_first
            e_scr[HALO:, ls] = col(m_ref, 3, ls) * col(m_ref, 4, ls)
        _shifted_copies(c_scr, c_sh, HALO + CD_TILE)
        for ls in lanes:
            for r0 in range(0, CD_TILE, TAP_ROWS):
                acc = jnp.zeros((TAP_ROWS, LANES), F32)
                for k in range(C_KERNEL):
                    acc = acc + cw_ref[k:k + 1, ls] * _rows_from(c_sh, r0 + HALO - (C_KERNEL - 1) + k, TAP_ROWS, ls)
                c1_ref[r0:r0 + TAP_ROWS, ls] = acc + cb_ref[:, ls]
        mean = sum(jnp.sum(c1_ref[:, ls], axis=-1, keepdims=True) for ls in lanes) * (1.0 / C_WIDTH)
        var = sum(jnp.sum((c1_ref[:, ls] - mean) ** 2, axis=-1, keepdims=True) for ls in lanes) * (1.0 / C_WIDTH)
        rs = lax.rsqrt(var + EPS)
        for ls in lanes:
            c2 = (c1_ref[:, ls] - mean) * rs * lg_ref[:, ls] + lb_ref[:, ls]
            o_ref[:, ls] = (c2 * _sigmoid(c2)).astype(BF16)
            d1 = jnp.zeros((CD_TILE, LANES), F32)
            for k in range(D_KERNEL):
                d1 = d1 + dw_ref[k:k + 1, ls] * e_scr[pl.ds(HALO - (D_KERNEL - 1) + k, CD_TILE), ls]
            o_ref[:, C_WIDTH + ls.start:C_WIDTH + ls.stop] = (col(m_ref, 2, ls) * d1).astype(BF16)

    return pl.pallas_call(
        body, grid=(t // CD_TILE,),
        in_specs=[pl.BlockSpec((HALO, CD_IN), lambda i: (jnp.maximum(i * per - 1, 0), 0)), _row_spec(CD_TILE, CD_IN),
                  _const_spec((32, C_WIDTH)), _const_spec((1, C_WIDTH)), _const_spec((1, C_WIDTH)), _const_spec((1, C_WIDTH)),
                  _const_spec((8, C_WIDTH))],
        out_specs=[_row_spec(CD_TILE, 2 * C_WIDTH), _row_spec(CD_TILE, C_WIDTH)],
        out_shape=[SDS((t, 2 * C_WIDTH), BF16), SDS((t, C_WIDTH), F32)],
        scratch_shapes=[pltpu.VMEM((HALO + CD_TILE, C_WIDTH), F32)] * 2 + [pltpu.VMEM((8, HALO + CD_TILE, C_WIDTH), F32)],
        name="mixer_cd_fwd", compiler_params=_params())(proj, proj, cw, cb, lg, lb, dw)


def _mixer_cd_bwd(proj, dcat, c1, cw, lg, lb, dw, ride=None):
    t = proj.shape[0]
    per = CD_TILE // HALO
    nt = t // CD_TILE
    ext = CD_TILE + HALO

    def body(hp_ref, m_ref, hn_ref, dm_ref, dn_ref, c1m_ref, c1n_ref, cw_ref, lg_ref, lb_ref, dw_ref,
             dp_ref, dcw_ref, dcb_ref, dlg_ref, dlb_ref, ddw_ref, c_scr, e_scr, dc1_scr, dd1_scr, c_sh, dc1_sh, dcw_acc,
             dvh_scr, vhat_scr):
        i = pl.program_id(0)

        @pl.when(i == 0)
        def _():
            for r in (dcw_acc, dcb_ref, dlg_ref, dlb_ref, ddw_ref):
                r[...] = jnp.zeros_like(r)

        not_first = (i > 0).astype(F32)
        not_last = (i < nt - 1).astype(F32)
        main = slice(HALO, HALO + CD_TILE)
        lanes = [slice(c * LANES, (c + 1) * LANES) for c in range(C_WIDTH // LANES)]

        def col(ref, part, ls):
            return ref[:, part * C_WIDTH + ls.start:part * C_WIDTH + ls.stop].astype(F32)

        for ls in lanes:
            c_scr[:HALO, ls] = col(hp_ref, 0, ls) * _sigmoid(col(hp_ref, 1, ls)) * not_first
            c_scr[main, ls] = col(m_ref, 0, ls) * _sigmoid(col(m_ref, 1, ls))
            c_scr[HALO + CD_TILE:, ls] = col(hn_ref, 0, ls) * _sigmoid(col(hn_ref, 1, ls)) * not_last
            e_scr[:HALO, ls] = col(hp_ref, 3, ls) * col(hp_ref, 4, ls) * not_first
            e_scr[main, ls] = col(m_ref, 3, ls) * col(m_ref, 4, ls)
            e_scr[HALO + CD_TILE:, ls] = col(hn_ref, 3, ls) * col(hn_ref, 4, ls) * not_last
        _shifted_copies(c_scr, c_sh, 2 * HALO + CD_TILE)

        def c1_of(ls):
            return jnp.concatenate([c1m_ref[:, ls], c1n_ref[:, ls]], axis=0)

        mean = sum(jnp.sum(c1_of(ls), axis=-1, keepdims=True) for ls in lanes) * (1.0 / C_WIDTH)
        var = sum(jnp.sum((c1_of(ls) - mean) ** 2, axis=-1, keepdims=True) for ls in lanes) * (1.0 / C_WIDTH)
        rs = lax.rsqrt(var + EPS)
        sum_dvh, sum_dvh_vhat = 0.0, 0.0
        for ls in lanes:
            vhat = (c1_of(ls) - mean) * rs
            c2 = vhat * lg_ref[:, ls] + lb_ref[:, ls]
            sig = _sigmoid(c2)
            dc = jnp.concatenate([dm_ref[:, ls], dn_ref[:, ls] * not_last], axis=0)
            dc2 = dc * (sig * (1.0 + c2 * (1.0 - sig)))
            dvh = dc2 * lg_ref[:, ls]
            sum_dvh = sum_dvh + jnp.sum(dvh, axis=-1, keepdims=True)
            sum_dvh_vhat = sum_dvh_vhat + jnp.sum(dvh * vhat, axis=-1, keepdims=True)
            dvh_scr[:, ls] = dvh
            vhat_scr[:, ls] = vhat
            dlg_ref[:, ls] += jnp.sum((dc2 * vhat)[:CD_TILE], axis=0, keepdims=True)
            dlb_ref[:, ls] += jnp.sum(dc2[:CD_TILE], axis=0, keepdims=True)
        for ls in lanes:
            dc1 = rs * (dvh_scr[:, ls] - sum_dvh * (1.0 / C_WIDTH) - vhat_scr[:, ls] * (sum_dvh_vhat * (1.0 / C_WIDTH)))
            dc1_scr[:, ls] = dc1
            dcb_ref[:, ls] += jnp.sum(dc1[:CD_TILE], axis=0, keepdims=True)
        _shifted_copies(dc1_scr, dc1_sh, ext)
        for ls in lanes:
            for r0 in range(0, CD_TILE, TAP_ROWS):
                rows = slice(r0, r0 + TAP_ROWS)
                dc1_m = dc1_scr[rows, ls]
                dc0 = jnp.zeros((TAP_ROWS, LANES), F32)
                for k in range(C_KERNEL):
                    dc0 = dc0 + cw_ref[k:k + 1, ls] * _rows_from(dc1_sh, r0 + C_KERNEL - 1 - k, TAP_ROWS, ls)
                    prod = dc1_m * _rows_from(c_sh, r0 + HALO - (C_KERNEL - 1) + k, TAP_ROWS, ls)
                    dcw_acc[k, :, ls] += prod.reshape(TAP_ROWS // 8, 8, LANES).sum(axis=0)
                g_m = m_ref[rows, C_WIDTH + ls.start:C_WIDTH + ls.stop].astype(F32)
                a_m = m_ref[rows, ls].astype(F32)
                sig_m = _sigmoid(g_m)
                dp_ref[rows, ls] = (dc0 * sig_m).astype(BF16)
                dp_ref[rows, C_WIDTH + ls.start:C_WIDTH + ls.stop] = (dc0 * a_m * sig_m * (1.0 - sig_m)).astype(BF16)

        @pl.when(i == nt - 1)
        def _():
            dcw_ref[...] = jnp.sum(dcw_acc[...], axis=1)

        for ls in lanes:
            wide = slice(C_WIDTH + ls.start, C_WIDTH + ls.stop)
            d1 = jnp.zeros((CD_TILE, LANES), F32)
            for k in range(D_KERNEL):
                d1 = d1 + dw_ref[k:k + 1, ls] * e_scr[pl.ds(HALO - (D_KERNEL - 1) + k, CD_TILE), ls]
            dd_m = dm_ref[:, wide]
            dd1 = jnp.concatenate([dd_m * col(m_ref, 2, ls), dn_ref[:, wide] * col(hn_ref, 2, ls) * not_last], axis=0)
            dd1_scr[:, ls] = dd1
            dp_ref[:, 2 * C_WIDTH + ls.start:2 * C_WIDTH + ls.stop] = (dd_m * d1).astype(BF16)
            de = jnp.zeros((CD_TILE, LANES), F32)
            for k in range(D_KERNEL):
                de = de + dw_ref[k:k + 1, ls] * dd1_scr[pl.ds(D_KERNEL - 1 - k, CD_TILE), ls]
                ddw_ref[k:k + 1, ls] += jnp.sum(dd1[:CD_TILE] * e_scr[pl.ds(HALO - (D_KERNEL - 1) + k, CD_TILE), ls], axis=0, keepdims=True)
            dp_ref[:, 3 * C_WIDTH + ls.start:3 * C_WIDTH + ls.stop] = (de * col(m_ref, 4, ls)).astype(BF16)
            dp_ref[:, 4 * C_WIDTH + ls.start:4 * C_WIDTH + ls.stop] = (de * col(m_ref, 3, ls)).astype(BF16)

    halo_prev = lambda i: (jnp.maximum(i * per - 1, 0), 0)
    halo_next = lambda i: (jnp.minimum((i + 1) * per, t // HALO - 1), 0)
    vec = _const_spec((1, C_WIDTH))
    return _call(
        body, grid=(nt,),
        in_specs=[pl.BlockSpec((HALO, CD_IN), halo_prev), _row_spec(CD_TILE, CD_IN), pl.BlockSpec((HALO, CD_IN), halo_next),
                  _row_spec(CD_TILE, 2 * C_WIDTH), pl.BlockSpec((HALO, 2 * C_WIDTH), halo_next),
                  _row_spec(CD_TILE, C_WIDTH), pl.BlockSpec((HALO, C_WIDTH), halo_next),
                  _const_spec((32, C_WIDTH)), vec, vec, _const_spec((8, C_WIDTH))],
        out_specs=[_row_spec(CD_TILE, CD_IN), _const_spec((32, C_WIDTH)), vec, vec, vec, _const_spec((8, C_WIDTH))],
        out_shape=[SDS((t, CD_IN), BF16), SDS((32, C_WIDTH), F32), SDS((1, C_WIDTH), F32), SDS((1, C_WIDTH), F32),
                   SDS((1, C_WIDTH), F32), SDS((8, C_WIDTH), F32)],
        scratch_shapes=[pltpu.VMEM((2 * HALO + CD_TILE, C_WIDTH), F32)] * 2 + [pltpu.VMEM((ext, C_WIDTH), F32)] * 2
        + [pltpu.VMEM((8, 2 * HALO + CD_TILE, C_WIDTH), F32), pltpu.VMEM((8, ext, C_WIDTH), F32),
           pltpu.VMEM((32, 8, C_WIDTH), F32)] + [pltpu.VMEM((ext, C_WIDTH), F32)] * 2,
        operands=[proj, proj, proj, dcat, dcat, c1, c1, cw, lg, lb, dw], name="mixer_cd_bwd", ride=ride)


def _wgrad(name, pairs, out_rc, t, ride):
    tk = TILES["wgrad"]
    assert tk == t, "the whole contraction has to fit one grid step"
    r, c = out_rc
    n = len(pairs)

    operands, in_specs, where = [], [], []
    for lhs, lhs_spec, rhs, rhs_spec in pairs:
        at = []
        for array, spec in ((lhs, lhs_spec), (rhs, rhs_spec)):
            seen = [k for k, o in enumerate(operands) if o is array]
            if not seen:
                operands.append(array)
                in_specs.append(spec)
            at.append(seen[0] if seen else len(operands) - 1)
        where.append(at)
    n_in = len(operands)

    def body(*refs):
        ins, out_refs = refs[:n_in], refs[n_in:]
        for j, (lhs_at, rhs_at) in enumerate(where):
            out_refs[j][...] = _dot(ins[lhs_at][...], ins[rhs_at][...], TN).astype(BF16)

    res = _call(body, grid=(N_CHIPS, t // tk), in_specs=in_specs,
                out_specs=[pl.BlockSpec((None, r, c), lambda p, k: (p, 0, 0))] * n,
                out_shape=[SDS((N_CHIPS, r, c), BF16)] * n, operands=operands, name=name, ride=ride)
    outs, ride_res = (res, None) if ride is None else res
    outs = [o.reshape(N_CHIPS, 2, r // 2, c) for o in outs]
    return outs if ride is None else (outs, ride_res)


def _wgrad_col_sharded(name, h, dz_list, three_d, ride=None):
    t, d = h.shape
    tk = TILES["wgrad"]
    n4 = dz_list[0].shape[-1] if three_d else dz_list[0].shape[-1] // N_CHIPS
    hs = pl.BlockSpec((tk, d), lambda p, k: (k, 0))
    zs = pl.BlockSpec((None, tk, n4), lambda p, k: (p, k, 0)) if three_d else pl.BlockSpec((tk, n4), lambda p, k: (k, p))
    return _wgrad(name, [(h, hs, dz, zs) for dz in dz_list], (d, n4), t, ride)


def _wgrad_row_sharded(name, a, g, three_d, ride=None):
    many = isinstance(a, (list, tuple))
    a_list = list(a) if many else [a]
    t, d = g.shape
    tk = TILES["wgrad"]
    k4 = a_list[0].shape[-1] if three_d else a_list[0].shape[-1] // N_CHIPS
    a_spec = pl.BlockSpec((None, tk, k4), lambda p, k: (p, k, 0)) if three_d else pl.BlockSpec((tk, k4), lambda p, k: (k, p))
    gs = pl.BlockSpec((tk, d), lambda p, k: (k, 0))
    res = _wgrad(name, [(a_j, a_spec, g, gs) for a_j in a_list], (k4, d), t, ride)
    if many:
        return res
    return res[0] if ride is None else (res[0][0], res[1])


def _mesh_scalars():
    return jnp.stack([lax.axis_index("c"), 2 * lax.axis_index("x") + lax.axis_index("y")]).astype(jnp.int32)


def _stage_own(name, w, layer, dtype, far_slab=False):
    layers, r, cols = w.shape
    h = r // 2

    def body(s_ref, x_ref, o_ref, *unwritten):
        o_ref[...] = x_ref[...].astype(dtype)

    out_specs = [pl.BlockSpec((None, None, h, cols), lambda i, s: (s[1], i, 0, 0))] + [ANY] * far_slab
    out_shape = [SDS((N_CHIPS, 2, h, cols), dtype)] + [SDS((2, h, cols), dtype)] * far_slab
    res = pl.pallas_call(
        body,
        grid_spec=pltpu.PrefetchScalarGridSpec(
            num_scalar_prefetch=1, grid=(2,),
            in_specs=[pl.BlockSpec((None, h, cols), lambda i, s: (2 * layer + i, 0, 0))], out_specs=out_specs),
        out_shape=out_shape, name=name,
        compiler_params=_params())(_mesh_scalars(), w.reshape(2 * layers, h, cols))
    return tuple(res) if far_slab else res[0]


STAGE_STEPS = 4


def _stage_rest_and_norm(x, g, weights, ride=None):
    t, d = x.shape
    n = len(weights)
    views, in_specs, out_specs, out_shapes = [], [], [], []
    for w, layer in weights:
        layers, r, cols = w.shape
        sub = r // STAGE_STEPS
        views.append(w.reshape(layers * STAGE_STEPS, sub, cols))
        in_specs.append(pl.BlockSpec((None, sub, cols), functools.partial(lambda l, i, s: (STAGE_STEPS * l + i, 0, 0), layer)))
        out_specs.append(pl.BlockSpec((None, None, sub, cols), lambda i, s: (s[1], i // 2, i % 2, 0)))
        out_shapes.append(SDS((N_CHIPS, 2, r // 2, cols), BF16))

    def body(s_ref, x_ref, g_ref, *rest):
        w_refs, h_ref, o_refs = rest[:n], rest[n], rest[n + 1:]
        h_ref[...] = _rms_rows(x_ref[...], g_ref[...]).astype(BF16)
        for w_ref, o_ref in zip(w_refs, o_refs):
            o_ref[...] = w_ref[...].astype(BF16)

    tm = t // STAGE_STEPS
    res = _call(
        body, grid=(STAGE_STEPS,), in_specs=[pl.BlockSpec((tm, d), lambda i, s: (i, 0)), pl.BlockSpec((1, d), lambda i, s: (0, 0))] + in_specs,
        out_specs=[pl.BlockSpec((tm, d), lambda i, s: (i, 0))] + out_specs, out_shape=[SDS((t, d), BF16)] + out_shapes,
        operands=[x, g] + views, name="stage_and_norm", ride=ride, prefetch=_mesh_scalars())
    outs, ride_res = (res, None) if ride is None else res
    result = (outs[0], list(outs[1:]))
    return result if ride is None else (result, ride_res)


def _remote(src, dst, send_sem, recv_sem, device):
    return pltpu.make_async_remote_copy(src, dst, send_sem, recv_sem, device_id=device, device_id_type=MESH)


ALL_PEERS = (0, 1, 2)
NEIGHBOURS = (0, 1)


def _ride_gather_send(bufs, peers=ALL_PEERS):
    n = len(bufs)

    def each(b, sems, act):
        send, recv = sems
        x, y, c, p, others = _position()
        for t in range(n):
            for j in peers:
                qx, qy = others[j]
                act(b[t].at[p, c], b[t].at[2 * qx + qy, c], send.at[t, j], recv.at[t, j], (qx, qy, c))

    def start(ins, b, new, sems):
        each(b, sems, lambda mine, landed, s, r, dev: _remote(mine, mine, s, r, dev).start())

    def finish(ins, b, new, sems):
        def act(mine, landed, s, r, dev):
            _remote(mine, mine, s, r, dev).wait_send()
            _remote(landed, landed, s, r, dev).wait_recv()
        each(b, sems, act)

    return _Ride([], bufs, [], [(n, 3), (n, 3)], start, finish, ["chips"])


def _ride_gather_pass(bufs, peers=ALL_PEERS):
    n = len(bufs)

    def each(b, sems, act):
        send, recv = sems
        x, y, c, p, others = _position()
        for t in range(n):
            for j in peers:
                qx, qy = others[j]
                act(b[t].at[2 * qx + qy, c], b[t].at[2 * qx + qy, 1 - c], send.at[t, j], recv.at[t, j], (x, y, 1 - c))

    def start(ins, b, new, sems):
        each(b, sems, lambda landed, passed, s, r, dev: _remote(landed, landed, s, r, dev).start())

    def finish(ins, b, new, sems):
        def act(landed, passed, s, r, dev):
            _remote(landed, landed, s, r, dev).wait_send()
            _remote(passed, passed, s, r, dev).wait_recv()
        each(b, sems, act)

    return _Ride([], bufs, [], [(n, 3), (n, 3)], start, finish, ["sibling"])


def _ride_gather(bufs, peers=ALL_PEERS):
    send, onward = _ride_gather_send(bufs, peers), _ride_gather_pass(bufs, peers)
    n_send = len(send.sem_shapes)

    def start(ins, b, new, sems):
        send.start(ins, b, new, sems[:n_send])

    def finish(ins, b, new, sems):
        send.finish(ins, b, new, sems[:n_send])
        onward.start(ins, b, new, sems[n_send:])
        onward.finish(ins, b, new, sems[n_send:])

    return _Ride([], bufs, [], send.sem_shapes + onward.sem_shapes, start, finish, ["sibling", "chips"])


def _ride_gather_far(sources, slabs):
    n = len(slabs)

    def hops(ins, b, sems):
        x, y, c, p, others = _position()
        qx, qy = others[2]
        for t in range(n):
            far = (ins[t].at[p, c], b[t].at[c], sems[0].at[t], sems[1].at[t], (qx, qy, c))
            onward = (b[t].at[c], b[t].at[1 - c], sems[2].at[t], sems[3].at[t], (x, y, 1 - c))
            yield far, onward

    def wait(mine, landed, s, r, dev):
        _remote(mine, mine, s, r, dev).wait_send()
        _remote(landed, landed, s, r, dev).wait_recv()

    def start(ins, b, new, sems):
        for (mine, landed, s, r, dev), _ in hops(ins, b, sems):
            _remote(mine, landed, s, r, dev).start()

    def finish(ins, b, new, sems):
        for far, _ in hops(ins, b, sems):
            wait(*far)
        for _, (landed, passed, s, r, dev) in hops(ins, b, sems):
            _remote(landed, landed, s, r, dev).start()
        for _, onward in hops(ins, b, sems):
            wait(*onward)

    return _Ride(sources, slabs, [], [(n,)] * 4, start, finish, ["sibling", "chips"])


def _ride_swap(tensors):
    n = len(tensors)

    def each(ins, new, sems, act):
        send, recv = sems
        x, y, c, _, _ = _position()
        for t in range(n):
            act(_remote(ins[t].at[:, 1 - c], new[t], send.at[t], recv.at[t], (x, y, 1 - c)))

    def start(ins, b, new, sems):
        each(ins, new, sems, lambda cp: cp.start())

    def finish(ins, b, new, sems):
        each(ins, new, sems, lambda cp: cp.wait())

    return _Ride(tensors, [], [SDS((s.shape[0],) + s.shape[2:], s.dtype) for s in tensors], [(n,), (n,)], start, finish,
                 ["sibling"])


def _ride_scatter(tensors, landing):
    n = len(tensors)

    def each(ins, b, sems, act):
        send, recv = sems
        x, y, c, p, others = _position()
        for t in range(n):
            for j, (qx, qy) in enumerate(others):
                q = 2 * qx + qy
                act(ins[t].at[q], b[t].at[p], b[t].at[q], send.at[t, j], recv.at[t, j], (qx, qy, c))

    def start(ins, b, new, sems):
        each(ins, b, sems, lambda src, dst, landed, s, r, dev: _remote(src, dst, s, r, dev).start())

    def finish(ins, b, new, sems):
        def act(src, dst, landed, s, r, dev):
            _remote(src, dst, s, r, dev).wait_send()
            _remote(landed, landed, s, r, dev).wait_recv()
        each(ins, b, sems, act)

    return _Ride(tensors, landing, [], [(n, 3), (n, 3)], start, finish, ["chips"])


def _ride_join(bufs):
    n = len(bufs)

    def each(b, sems, act):
        send, recv = sems
        x, y, c, _, _ = _position()
        for t in range(n):
            act(b[t].at[c], b[t].at[1 - c], send.at[t], recv.at[t], (x, y, 1 - c))

    def start(ins, b, new, sems):
        each(b, sems, lambda mine, theirs, s, r, dev: _remote(mine, mine, s, r, dev).start())

    def finish(ins, b, new, sems):
        def act(mine, theirs, s, r, dev):
            _remote(mine, mine, s, r, dev).wait_send()
            _remote(theirs, theirs, s, r, dev).wait_recv()
        each(b, sems, act)

    return _Ride([], bufs, [], [(n,), (n,)], start, finish, ["sibling"])


def _all_reduce_small(pack, ride=None):
    rows = pack.shape[0]
    n_dev = 2 * N_CHIPS
    n_rb = 0 if ride is None else len(ride.bufs)

    def body(x_ref, *rest):
        o_ref = rest[n_rb]
        r_bufs = rest[n_rb + 1:2 * n_rb + 1]
        land, send, recv = rest[2 * n_rb + 1:2 * n_rb + 4]
        r_sems = rest[2 * n_rb + 4:]
        if ride is not None:
            ride.start([], r_bufs, [], r_sems)
        x, y, c, p, _ = _position()
        me = 2 * p + c
        land[me] = x_ref[...]
        peers = [(dx, dy, dc) for dx in range(2) for dy in range(2) for dc in range(2) if (dx, dy, dc) != (0, 0, 0)]
        for j, (dx, dy, dc) in enumerate(peers):
            _remote(land.at[me], land.at[me], send.at[j], recv.at[j], (x ^ dx, y ^ dy, c ^ dc)).start()
        for j, (dx, dy, dc) in enumerate(peers):
            src = 4 * (x ^ dx) + 2 * (y ^ dy) + (c ^ dc)
            _remote(land.at[me], land.at[me], send.at[j], recv.at[j], (x ^ dx, y ^ dy, c ^ dc)).wait_send()
            _remote(land.at[src], land.at[src], send.at[j], recv.at[j], (x ^ dx, y ^ dy, c ^ dc)).wait_recv()
        acc = land[0]
        for dev in range(1, n_dev):
            acc = acc + land[dev]
        o_ref[...] = acc
        if ride is not None:
            ride.finish([], r_bufs, [], r_sems)

    bufs = [] if ride is None else ride.bufs
    sems = [] if ride is None else [pltpu.SemaphoreType.DMA(s) for s in ride.sem_shapes]
    res = pl.pallas_call(
        body, in_specs=[pl.BlockSpec(memory_space=pltpu.VMEM)] + [ANY] * n_rb,
        out_specs=[pl.BlockSpec(memory_space=pltpu.VMEM)] + [ANY] * n_rb,
        out_shape=[SDS((rows, LANES), F32)] + [SDS(b.shape, b.dtype) for b in bufs],
        scratch_shapes=[pltpu.VMEM((n_dev, rows, LANES), F32), pltpu.SemaphoreType.DMA((n_dev - 1,)),
                        pltpu.SemaphoreType.DMA((n_dev - 1,))] + sems,
        input_output_aliases={1 + j: 1 + j for j in range(n_rb)},
        name="all_reduce_small", compiler_params=_params())(pack, *bufs)
    return res[0], list(res[1:])


def _add_own_half(name, fulls, recvs, out_dtypes):
    n = len(fulls)
    in_specs, out_specs, out_shapes = [], [], []
    for full, dtype in zip(fulls, out_dtypes):
        n4, _, h, cols = full.shape
        in_specs += [pl.BlockSpec((None, None, h, cols), lambda q, s: (q, s[0], 0, 0)),
                     pl.BlockSpec((None, h, cols), lambda q, s: (q, 0, 0))]
        out_specs += [pl.BlockSpec((None, h, cols), lambda q, s: (q, 0, 0)),
                      pl.BlockSpec((None, h, cols), lambda q, s: (s[1], 0, 0))]
        out_shapes += [SDS((n4, h, cols), dtype)] * 2

    def body(s_ref, *refs):
        ins, outs = refs[:2 * n], refs[2 * n:]
        for j in range(n):
            o_ref, own_ref = outs[2 * j], outs[2 * j + 1]
            v = (ins[2 * j][...].astype(F32) + ins[2 * j + 1][...].astype(F32)).astype(o_ref.dtype)
            o_ref[...] = v

            @pl.when(pl.program_id(0) == s_ref[1])
            def _():
                own_ref[...] = v

    res = pl.pallas_call(
        body,
        grid_spec=pltpu.PrefetchScalarGridSpec(num_scalar_prefetch=1, grid=(N_CHIPS,), in_specs=in_specs, out_specs=out_specs),
        out_shape=out_shapes, name=name, compiler_params=_params())(
            _mesh_scalars(), *[a for pair in zip(fulls, recvs) for a in pair])
    return [(res[2 * j], res[2 * j + 1]) for j in range(n)]


def _sum_chips(name, parts_list):
    steps = 4 if all(parts.shape[1] % 64 == 0 for parts in parts_list) else 1
    in_specs, out_specs, out_shapes = [], [], []
    for parts in parts_list:
        n4, h, cols = parts.shape
        in_specs.append(pl.BlockSpec((n4, h // steps, cols), lambda i, s: (0, i, 0)))
        out_specs.append(pl.BlockSpec((None, h // steps, cols), lambda i, s: (s[0], i, 0)))
        out_shapes.append(SDS((2, h, cols), F32))
    n = len(parts_list)

    def body(s_ref, *refs):
        for a_ref, o_ref in zip(refs[:n], refs[n:]):
            acc = a_ref[0].astype(F32)
            for q in range(1, N_CHIPS):
                acc = acc + a_ref[q].astype(F32)
            o_ref[...] = acc

    return pl.pallas_call(
        body,
        grid_spec=pltpu.PrefetchScalarGridSpec(num_scalar_prefetch=1, grid=(steps,), in_specs=in_specs, out_specs=out_specs),
        out_shape=out_shapes, name=name, compiler_params=_params())(_mesh_scalars(), *parts_list)


def _adamw_math(w, g, m, v):
    m2 = ADAM_B1 * m + (1.0 - ADAM_B1) * g
    v2 = ADAM_B2 * v + (1.0 - ADAM_B2) * (g * g)
    m_hat = m2 / (1.0 - ADAM_B1 ** ADAM_STEP)
    v_hat = v2 / (1.0 - ADAM_B2 ** ADAM_STEP)
    delta = -ADAM_LR * (m_hat / (jnp.sqrt(v_hat) + ADAM_EPS) + ADAM_WD * w)
    return delta, m2, v2


ADAMW_STEPS = 8


def _adamw_big(name, ws, g_layers_list, ms, vs):
    layers = ws[0].shape[0]
    n, per_in = len(ws), 3 + layers
    in_specs, out_specs, out_shapes, operands = [], [], [], []
    for w, g_layers, m, v in zip(ws, g_layers_list, ms, vs):
        assert w.shape[0] == layers and w.shape[1] % (8 * ADAMW_STEPS) == 0
        _, rows, cols = w.shape
        tr = rows // ADAMW_STEPS
        blk = pl.BlockSpec((None, tr, cols), lambda l, i: (l, i, 0))
        in_specs += [blk] * 3 + [pl.BlockSpec((tr, cols), lambda l, i: (i, 0))] * layers
        out_specs += [blk] * 4
        out_shapes += [SDS((layers, rows, cols), F32)] * 4
        operands += [w, m, v] + [g.reshape(rows, cols) for g in g_layers]

    def body(*refs):
        ins, outs = refs[:n * per_in], refs[n * per_in:]
        for j in range(n):
            w_ref, m_ref, v_ref = ins[j * per_in:j * per_in + 3]
            g_refs = ins[j * per_in + 3:(j + 1) * per_in]
            g_o, d_o, m_o, v_o = outs[4 * j:4 * j + 4]
            gv = g_refs[0][...]
            for layer in range(1, layers):
                gv = jnp.where(pl.program_id(0) == layer, g_refs[layer][...], gv)
            d, mm, vv = _adamw_math(w_ref[...], gv, m_ref[...], v_ref[...])
            g_o[...] = gv
            d_o[...] = d
            m_o[...] = mm
            v_o[...] = vv

    res = pl.pallas_call(
        body, grid=(layers, ADAMW_STEPS), in_specs=in_specs, out_specs=out_specs, out_shape=out_shapes, name=name,
        compiler_params=_params())(*operands)
    return [tuple(res[4 * j:4 * j + 4]) for j in range(n)]


def _adamw_small(ws, gs, ms, vs):
    n = len(ws)
    flat = []
    for group in (ws, gs, ms, vs):
        flat += [a.reshape(-1, a.shape[-1]) for a in group]

    def body(*refs):
        w_r, g_r, m_r, v_r = refs[:n], refs[n:2 * n], refs[2 * n:3 * n], refs[3 * n:4 * n]
        d_o, m_o, v_o = refs[4 * n:5 * n], refs[5 * n:6 * n], refs[6 * n:7 * n]
        for j in range(n):
            d, mm, vv = _adamw_math(w_r[j][...], g_r[j][...], m_r[j][...], v_r[j][...])
            d_o[j][...] = d
            m_o[j][...] = mm
            v_o[j][...] = vv

    shapes = [SDS(a.shape, F32) for a in flat[:n]]
    outs = pl.pallas_call(body, out_shape=shapes * 3, name="adamw_small", compiler_params=_params())(*flat)
    res = []
    for k in range(3):
        res.append([outs[k * n + j].reshape(ws[j].shape) for j in range(n)])
    return res


BIG = ("ab_w_in", "ab_w_out", "cd_w_in", "cd_w_out", "ffn_w_gate", "ffn_w_up", "ffn_w_down")
BIG_BY_LAYERS = (BIG[:4], BIG[4:])
V_BLOCK = (2 * A_WIDTH + QK_COLS) // B_WIDTH


def _pad_rows(a, rows):
    return jnp.pad(a, ((0, rows - a.shape[0]), (0, 0)))


A_IN, A_OUT, C_IN, C_OUT = ("ab_w_in", 0), ("ab_w_out", 0), ("cd_w_in", 0), ("cd_w_out", 0)
G0, U0, D0 = ("ffn_w_gate", 0), ("ffn_w_up", 0), ("ffn_w_down", 0)
G1, U1, D1 = ("ffn_w_gate", 1), ("ffn_w_up", 1), ("ffn_w_down", 1)
UNITS = (A_IN, A_OUT, G0, U0, D0, C_IN, C_OUT, G1, U1, D1)
ROWS_MINOR = ("ffn_w_gate", "ffn_w_up")
SMALL_SHARDED = ("small", 0)
REPLICATED_UNIT = ("replicated", 0)


class _Exchange:
    def __init__(self, enabled):
        self.enabled = enabled
        self.w, self.grad, self.recv, self.half, self.land, self.done = {}, {}, {}, {}, {}, {}
        self.far = {}

    def full(self, unit):
        b = self.w[unit]
        return b.reshape(N_CHIPS, 1, 2 * b.shape[2], b.shape[3])

    def ride_for(self, phases):
        rides, sinks = [], []
        for kind, units in phases:
            if kind == "send":
                rides.append(_ride_gather_send([self.w[u] for u in units]))
                sinks.append(self.w)
            elif kind == "pass":
                rides.append(_ride_gather_pass([self.w[u] for u in units]))
                sinks.append(self.w)
            elif kind == "gather":
                rides.append(_ride_gather([self.w[u] for u in units]))
                sinks.append(self.w)
            elif kind == "gather_near":
                rides.append(_ride_gather([self.w[u] for u in units], NEIGHBOURS))
                sinks.append(self.w)
            elif kind == "gather_far":
                rides.append(_ride_gather_far([self.w[u] for u in units], [self.far[u] for u in units]))
                sinks.append(self.far)
            elif kind == "swap":
                rides.append(_ride_swap([self.grad[u] for u in units]))
                sinks.append(self.recv)
            elif kind == "scatter":
                rides.append(_ride_scatter([self.half[u] for u in units], [self.land[u] for u in units]))
                sinks.append(self.land)
            else:
                rides.append(_ride_join([self.done[u] for u in units]))
                sinks.append(self.done)
        ride = functools.reduce(_ride_both, rides)

        def settle(res):
            n_bufs = sum(len(r.bufs) for r in rides)
            bufs, new = list(res[:n_bufs]), list(res[n_bufs:])
            for r, sink, (_, units) in zip(rides, sinks, phases):
                vals = [bufs.pop(0) for _ in r.bufs] + [new.pop(0) for _ in r.new_outs]
                for u, v in zip(units, vals):
                    sink[u] = v

        return ride, settle

    def run(self, fn, *args, phases=(), **kw):
        if not self.enabled or not phases:
            return fn(*args, **kw)
        ride, settle = self.ride_for(phases)
        out, res = fn(*args, ride=ride, **kw)
        settle(res)
        return out

    def alone(self, name, phases):
        if self.enabled:
            ride, settle = self.ride_for(phases)
            settle(_run_ride(name, ride))

    def pair_sum(self, units):
        if self.enabled:
            dtypes = [F32 if u in (SMALL_SHARDED, REPLICATED_UNIT) else BF16 for u in units]
            res = _add_own_half(f"pair_sum_{units[0][0]}_{units[0][1]}", [self.grad[u] for u in units],
                                [self.recv[u] for u in units], dtypes)
            for u, (half, land) in zip(units, res):
                self.half[u], self.land[u] = half, land

    def chip_sum(self, units):
        if self.enabled:
            res = _sum_chips(f"chip_sum_{units[0][0]}_{units[0][1]}", [self.land[u] for u in units])
            self.done.update(zip(units, res))


def _local_step(x, target, ex, sp, h0=None):
    t, d = x.shape
    tabs = _rope_tables(t)
    gains = jnp.concatenate([jnp.tile(sp["q_norm_g"][g], HEAD_DIM // 8) for g in range(N_DIL)]
                            + [jnp.tile(sp["k_norm_g"][g], HEAD_DIM // 8) for g in range(N_DIL)]).reshape(1, QK_COLS)
    bias_t = sp["sgu_bias"].T
    cw = _pad_rows(sp["conv_c_w"], 32)
    dw = _pad_rows(sp["conv_d_w"], 8)
    cb, clg, clb = (sp[k].reshape(1, C_WIDTH) for k in ("conv_c_b", "c_ln_g", "c_ln_b"))
    slg, slb = sp["sgu_norm_g"].reshape(1, A_WIDTH), sp["sgu_norm_b"].reshape(1, A_WIDTH)
    g_ab, g_cd = sp["ab_norm_g"].reshape(1, d), sp["cd_norm_g"].reshape(1, d)
    g_f0, g_f1 = sp["ffn_norm_g"][0:1], sp["ffn_norm_g"][1:2]
    run = ex.run

    def w2d(unit):
        return ex.full(unit).reshape(-1, d)

    if h0 is None:
        h0 = _rms_fwd("rms_ab", x, g_ab)
    if ex.enabled:
        proj = run(_proj_in_near, "proj_ab_near", h0, ex.full(A_IN), phases=[("gather_far", [A_IN]), ("send", [A_OUT])])
        proj, ex.w[A_IN] = _proj_in_far("proj_ab_far", h0, ex.far[A_IN], proj, ex.w[A_IN])
    else:
        proj = _proj_in("proj_ab", h0, ex.full(A_IN), 0)
    a_out = _mixer_a_fwd(proj, slg, slb, sp["sgu_w"], bias_t)
    qk, q1, q2, k1, k2 = run(_qk_fwd, proj, gains, tabs, phases=[("pass", [A_OUT]), ("send", [G0, C_OUT])])
    regrouped_qk = {1: (q1, k1), 2: (q2, k2)}
    fwd_phases = ([("pass", [G0, C_OUT]), ("send", [U0])], [("pass", [U0]), ("send", [D0])],
                  [("pass", [D0]), ("send", [C_IN])])
    qkv, o_list, l_list = [], [], []
    dilated = [g for g, rate in enumerate(DIL_RATES) if rate != 1]
    regrouped_v = dict(zip(dilated, _permute("regroup_v", [(proj, V_BLOCK + g, DIL_RATES[g]) for g in dilated])))
    for g, rate in enumerate(DIL_RATES):
        if rate == 1:
            qk3, proj3 = qk.reshape(1, t, QK_COLS), proj.reshape(1, t, AB_IN)
            q, k, v = (qk3, g), (qk3, N_DIL + g), (proj3, V_BLOCK + g)
        else:
            q, k, v = (regrouped_qk[g][0], 0), (regrouped_qk[g][1], 0), (regrouped_v[g], 0)
        qkv.append((q, k, v))
        o, l = run(_attn_fwd, f"attn_fwd_{g}", q, k, v, phases=fwd_phases[g])
        if rate == 1:
            o, l = o.reshape(t, B_WIDTH), l.reshape(t, B_WIDTH)
        o_list.append(o)
        l_list.append(l)
    cat, lse_tot, lse_1, lse_2 = _attn_merge(a_out, o_list, l_list)
    x1, hf0 = _proj_out("out_ab", cat, w2d(A_OUT), x, g_next=g_f0)
    fgate0, fup0, act0 = run(_ffn_in, "ffn_in_0", hf0, ex.full(G0), ex.full(U0), 0,
                           phases=[("pass", [C_IN]), ("send", [D1, G1])])
    x2, h1 = run(_ffn_out, "ffn_out_0", act0, ex.full(D0), 0, x1, g_next=g_cd, phases=[("pass", [D1, G1]), ("send", [U1])])
    projcd = run(_proj_in, "proj_cd", h1, ex.full(C_IN), 0, phases=[("pass", [U1])])
    cat2, c1 = _mixer_cd_fwd(projcd, cw, cb, clg, clb, dw)
    x3, hf1 = _proj_out("out_cd", cat2, w2d(C_OUT), x2, g_next=g_f1)
    fgate1, fup1, act1 = _ffn_in("ffn_in_1", hf1, ex.full(G1), ex.full(U1), 0)
    dy, loss_acc, dy_b = _ffn_out("ffn_out_1", act1, ex.full(D1), 0, x3, target=target)
    loss = 0.5 * loss_acc[0, 0] / d

    late = [D1, G1, U1]
    dgate, dup = _ffn_dact("ffn_dact_1", dy_b, ex.full(D1), 0, fgate1, fup1)
    ex.grad[D1] = _wgrad_row_sharded("wgrad_down_1", act1, dy_b, True)
    ex.grad[G1], ex.grad[U1] = _wgrad_row_sharded("wgrad_gate_up_1", [dgate, dup], hf1, True)
    g3, d_f1, g3_b = run(_dgrad_cols, "dgrad_ffn_1", [dgate, dup], [ex.full(G1), ex.full(U1)], 0, True, x3, g_f1, dy,
                         w_rows=True, phases=[("swap", late)])
    ex.pair_sum(late)

    dcat2 = _dgrad_rows("dgrad_out_cd", g3_b, w2d(C_OUT))
    ex.grad[C_OUT] = _wgrad_row_sharded("wgrad_out_cd", cat2, g3_b, False)
    dprojcd, d_cw, d_cb, d_clg, d_clb, d_dw = run(_mixer_cd_bwd, projcd, dcat2, c1, cw, clg, clb, dw, phases=[("scatter", late)])
    ex.grad[C_IN] = _wgrad_col_sharded("wgrad_in_cd", h1, [dprojcd], False)[0]
    g2, d_cdn, g2_b = run(_dgrad_cols, "dgrad_in_cd", [dprojcd], [ex.full(C_IN)], 0, False, x2, g_cd, g3,
                          phases=[("swap", [C_OUT, C_IN])])
    ex.pair_sum([C_OUT, C_IN])

    dgate, dup = run(_ffn_dact, "ffn_dact_0", g2_b, ex.full(D0), 0, fgate0, fup0, phases=[("scatter", [C_OUT, C_IN])])
    ex.chip_sum(late + [C_OUT, C_IN])
    ex.grad[D0] = _wgrad_row_sharded("wgrad_down_0", act0, g2_b, True)
    ex.grad[G0], ex.grad[U0] = _wgrad_row_sharded("wgrad_gate_up_0", [dgate, dup], hf0, True)
    small = {"cd_norm_g": d_cdn, "conv_c_w": d_cw[:C_KERNEL], "conv_c_b": d_cb, "c_ln_g": d_clg, "c_ln_b": d_clb,
             "conv_d_w": d_dw[:D_KERNEL]}
    ex.grad[SMALL_SHARDED] = _split_full_small(small).reshape(N_CHIPS, 2, SHARDED_ROWS // 2, LANES)
    mid = [D0, G0, U0, SMALL_SHARDED]
    g1, d_f0, g1_b = run(_dgrad_cols, "dgrad_ffn_0", [dgate, dup], [ex.full(G0), ex.full(U0)], 0, True, x1, g_f0, g2,
                         w_rows=True, phases=[("join", late + [C_OUT, C_IN]), ("swap", mid)])
    ex.pair_sum(mid)

    dcat = _dgrad_rows("dgrad_out_ab", g1_b, w2d(A_OUT))
    ex.grad[A_OUT] = _wgrad_row_sharded("wgrad_out_ab", cat, g1_b, False)
    d_a, d_sw, d_sbt, d_slg, d_slb = _mixer_a_bwd(proj, dcat, slg, slb, sp["sgu_w"], bias_t)
    early = {"sgu_norm_g": d_slg, "sgu_norm_b": d_slb, "sgu_w": d_sw, "sgu_bias": d_sbt.T}
    ex.grad[REPLICATED_UNIT] = jnp.broadcast_to(
        _pack_replicated(early, REPLICATED_EARLY, REPLICATED_EARLY_ROWS).reshape(2, REPLICATED_EARLY_ROWS // 2, LANES),
        (N_CHIPS, 2, REPLICATED_EARLY_ROWS // 2, LANES))
    last = [A_OUT, REPLICATED_UNIT]
    dbb, dd, db_1, dd_1, db_2, dd_2 = _attn_bwd_prep(dcat, cat)
    regrouped_bwd = {1: (db_1, lse_1, dd_1), 2: (db_2, lse_2, dd_2)}
    bwd_phases = ([("scatter", [D0, SMALL_SHARDED])],
                  [("scatter", [G0]), ("swap", last)],
                  [("scatter", [U0])])
    dqs, dks, dvs = [], [], []
    for g, rate in enumerate(DIL_RATES):
        q, k, v = qkv[g]
        if rate == 1:
            db3, l3, dd3 = (a.reshape(1, t, B_WIDTH) for a in (dbb, lse_tot, dd))
        else:
            db3, l3, dd3 = regrouped_bwd[g]
        if g == 2:
            ex.pair_sum(last)
        dq, dk, dv = run(_attn_bwd, f"attn_bwd_{g}", q, k, v, db3, l3, dd3, phases=bwd_phases[g])
        if rate == 1:
            dq, dk, dv = (a.reshape(t, B_WIDTH) for a in (dq, dk, dv))
        dqs.append(dq)
        dks.append(dk)
        dvs.append(dv)
    ex.chip_sum([D0, SMALL_SHARDED, G0, U0])
    dproj, d_gains = run(_dproj_assemble, proj, d_a, dqs, dks, dvs, gains, tabs, phases=[("scatter", last), ("join", [D0, SMALL_SHARDED, G0, U0])])
    ex.chip_sum(last)
    d_gains = _fold_heads(d_gains)[0].reshape(2, N_DIL, B_WIDTH)[:, :, :HEAD_DIM]
    ex.grad[A_IN] = _wgrad_col_sharded("wgrad_in_ab", h0, [dproj], False)[0]
    ex.alone("swap_last", [("swap", [A_IN])])
    ex.pair_sum([A_IN])
    gx, d_abn = run(_dgrad_cols, "dgrad_in_ab", [dproj], [ex.full(A_IN)], 0, False, x, g_ab, g1, bf16_copy=False,
                    phases=[("join", last), ("scatter", [A_IN])])
    ex.chip_sum([A_IN])

    small.update({
        "ab_norm_g": d_abn, "sgu_norm_g": d_slg, "sgu_norm_b": d_slb, "sgu_w": d_sw, "sgu_bias": d_sbt.T,
        "q_norm_g": d_gains[0], "k_norm_g": d_gains[1], "ffn_norm_g": jnp.concatenate([d_f0, d_f1], axis=0),
    })
    return loss, gx, small


SHARDED_SMALL = ("cd_norm_g", "conv_c_w", "conv_c_b", "c_ln_g", "c_ln_b", "conv_d_w")
SHARDED_ROWS = 48
REPLICATED_EARLY = ("sgu_norm_g", "sgu_norm_b", "sgu_w", "sgu_bias")
REPLICATED_EARLY_ROWS = 528
REPLICATED_LATE = ("ab_norm_g", "q_norm_g", "k_norm_g", "ffn_norm_g", "loss")
REPLICATED_LATE_ROWS = 32
REPLICATED_SMALL = REPLICATED_EARLY + REPLICATED_LATE[:-1]


def _pack_sharded(parts):
    rows = [parts[k].reshape(-1, LANES) for k in SHARDED_SMALL]
    return _pad_rows(jnp.concatenate(rows, axis=0), SHARDED_ROWS)


def _split_full_small(small):
    per_chip = []
    for q in range(N_CHIPS):
        parts = {}
        for k in SHARDED_SMALL:
            a = small[k]
            a = a.reshape(-1, a.shape[-1])
            n = a.shape[-1] // N_CHIPS
            parts[k] = a[:, q * n:(q + 1) * n]
        per_chip.append(_pack_sharded(parts))
    return jnp.stack(per_chip)


def _unpack_sharded(pack, shapes):
    out, r = {}, 0
    for k in SHARDED_SMALL:
        n = math.prod(shapes[k]) // LANES
        out[k] = pack[r:r + n].reshape(shapes[k])
        r += n
    return out


def _gathered_small(packs, shapes):
    per_chip = [_unpack_sharded(packs[q], shapes) for q in range(N_CHIPS)]
    return {k: jnp.concatenate([pc[k] for pc in per_chip], axis=-1) for k in SHARDED_SMALL}


def _pack_replicated(small, names, total_rows):
    rows = []
    for k in names:
        a = small[k].reshape(-1)
        a = jnp.pad(a, (0, (-a.shape[0]) % LANES))
        rows.append(a.reshape(-1, LANES))
    return _pad_rows(jnp.concatenate(rows, axis=0), total_rows)


def _unpack_replicated(pack, shapes, names):
    out, r = {}, 0
    for k in names:
        size = math.prod(shapes[k])
        n = -(-size // LANES)
        out[k] = pack[r:r + n].reshape(-1)[:size].reshape(shapes[k])
        r += n
    return out


WEIGHT_ORDER = ("ab_norm_g", "ab_w_in", "sgu_norm_g", "sgu_norm_b", "sgu_w", "sgu_bias", "q_norm_g", "k_norm_g", "ab_w_out",
                "cd_norm_g", "cd_w_in", "conv_c_w", "conv_c_b", "c_ln_g", "c_ln_b", "conv_d_w", "cd_w_out", "ffn_norm_g",
                "ffn_w_gate", "ffn_w_up", "ffn_w_down")


def kernel(x, ab_norm_g, ab_w_in, sgu_norm_g, sgu_norm_b, sgu_w, sgu_bias, q_norm_g, k_norm_g, ab_w_out, cd_norm_g, cd_w_in, conv_c_w, conv_c_b, c_ln_g, c_ln_b, conv_d_w, cd_w_out, ffn_norm_g, ffn_w_gate, ffn_w_up, ffn_w_down, loss_target, m_ab_norm_g, m_ab_w_in, m_sgu_norm_g, m_sgu_norm_b, m_sgu_w, m_sgu_bias, m_q_norm_g, m_k_norm_g, m_ab_w_out, m_cd_norm_g, m_cd_w_in, m_conv_c_w, m_conv_c_b, m_c_ln_g, m_c_ln_b, m_conv_d_w, m_cd_w_out, m_ffn_norm_g, m_ffn_w_gate, m_ffn_w_up, m_ffn_w_down, v_ab_norm_g, v_ab_w_in, v_sgu_norm_g, v_sgu_norm_b, v_sgu_w, v_sgu_bias, v_q_norm_g, v_k_norm_g, v_ab_w_out, v_cd_norm_g, v_cd_w_in, v_conv_c_w, v_conv_c_b, v_c_ln_g, v_c_ln_b, v_conv_d_w, v_cd_w_out, v_ffn_norm_g, v_ffn_w_gate, v_ffn_w_up, v_ffn_w_down):
    args = dict(locals())
    ws = {k: args[k] for k in WEIGHT_ORDER}
    ms = {k: args["m_" + k] for k in WEIGHT_ORDER}
    vs = {k: args["v_" + k] for k in WEIGHT_ORDER}
    small_names = [k for k in WEIGHT_ORDER if k not in BIG]
    t, d = x.shape[1:]

    for group in (ws, ms, vs):
        for k in ROWS_MINOR:
            group[k] = jnp.swapaxes(group[k], 1, 2)
    ex = _Exchange(enabled=True)
    ex.w[A_IN], ex.far[A_IN] = _stage_own("stage_ab_w_in", ws["ab_w_in"], 0, BF16, far_slab=True)
    own_small = _pack_sharded({k: ws[k][0] for k in SHARDED_SMALL})
    ex.w[SMALL_SHARDED] = _stage_own("stage_small", own_small[None], 0, F32)
    rest = [u for u in UNITS if u != A_IN]
    x2 = x.reshape(t, d)
    h0, staged = ex.run(_stage_rest_and_norm, x2, ws["ab_norm_g"], [(ws[name], layer) for name, layer in rest],
                        phases=[("gather_near", [A_IN]), ("gather", [SMALL_SHARDED])])
    ex.w.update(zip(rest, staged))
    sp = _gathered_small(ex.w[SMALL_SHARDED].reshape(N_CHIPS, SHARDED_ROWS, LANES), {k: ws[k].shape[1:] for k in SHARDED_SMALL})
    for k in REPLICATED_SMALL:
        sp[k] = ws[k] if k == "ffn_norm_g" else ws[k][0]

    loss, grad_x, g_small = _local_step(x2, loss_target.reshape(t, d), ex, sp, h0)

    shapes = {k: ws[k].shape for k in REPLICATED_SMALL}
    shapes["loss"] = (1,)
    g_small["loss"] = loss
    join_last, settle = ex.ride_for([("join", [A_IN])])
    late, joined = _all_reduce_small(_pack_replicated(g_small, REPLICATED_LATE, REPLICATED_LATE_ROWS), join_last)
    settle(joined)
    grad = _unpack_sharded(ex.done[SMALL_SHARDED].reshape(SHARDED_ROWS, LANES), {k: ws[k].shape for k in SHARDED_SMALL})
    grad.update(_unpack_replicated(ex.done[REPLICATED_UNIT].reshape(REPLICATED_EARLY_ROWS, LANES), shapes, REPLICATED_EARLY))
    grad.update(_unpack_replicated(late, shapes, REPLICATED_LATE))
    loss = grad.pop("loss")[0]

    delta, new_m, new_v = {}, {}, {}
    for group in BIG_BY_LAYERS:
        g_layers = [[ex.done[(k, layer)] for layer in range(ws[k].shape[0])] for k in group]
        results = _adamw_big("adamw_" + group[0], [ws[k] for k in group], g_layers, [ms[k] for k in group], [vs[k] for k in group])
        for k, outs in zip(group, results):
            if k in ROWS_MINOR:
                outs = [jnp.swapaxes(o, 1, 2) for o in outs]
            grad[k], delta[k], new_m[k], new_v[k] = outs
    d_s, m_s, v_s = _adamw_small([ws[k] for k in small_names], [grad[k] for k in small_names],
                                 [ms[k] for k in small_names], [vs[k] for k in small_names])
    for j, k in enumerate(small_names):
        delta[k], new_m[k], new_v[k] = d_s[j], m_s[j], v_s[j]

    return (loss, grad_x[None], *[grad[k] for k in WEIGHT_ORDER], *[delta[k] for k in WEIGHT_ORDER],
            *[new_m[k] for k in WEIGHT_ORDER], *[new_v[k] for k in WEIGHT_ORDER])
```

```python
import functools
import math

import jax
import jax.numpy as jnp
from jax import lax
from jax.experimental import pallas as pl
from jax.experimental.pallas import tpu as pltpu

F32 = jnp.float32
BF16 = jnp.bfloat16
SDS = jax.ShapeDtypeStruct

N_CHIPS = 4
EPS = 1e-6
NEG_INF = -1e30
CHUNK = 128
A_GROUPS = 4
A_WIDTH = 512
N_DIL = 3
DIL_RATES = (1, 4, 16)
HEAD_DIM = 64
B_WIDTH = 512
ROPE_DIM = 16
ROPE_THETA = 500000.0
C_WIDTH = 512
C_KERNEL = 31
D_KERNEL = 3
HALO = 32
ATT_BLOCK = 128
LANES = 128

ADAM_LR = 0.001
ADAM_B1 = 0.9
ADAM_B2 = 0.999
ADAM_EPS = 1e-08
ADAM_WD = 0.01
ADAM_STEP = 10

VMEM_LIMIT = 56 * 1024 * 1024

NN = (((1,), (0,)), ((), ()))
NT = (((1,), (1,)), ((), ()))
TN = (((0,), (0,)), ((), ()))

TILES = {"proj_in": 2048, "proj_out": 1024, "ffn_in": 1024, "ffn_out": 1024, "ffn_dact": 512, "dgrad_cols": 512,
         "dgrad_rows": 1024, "wgrad": 4096}


def _params(sem=None, collective_id=None):
    return pltpu.CompilerParams(dimension_semantics=sem, vmem_limit_bytes=VMEM_LIMIT, collective_id=collective_id)


def _bf(v):
    return v if v.dtype == BF16 else v.astype(BF16)


def _dot(a, b, dims):
    return lax.dot_general(_bf(a), _bf(b), dims, preferred_element_type=F32)


def _dot_hi(a, b):
    return jnp.dot(a, b, precision=lax.Precision.HIGHEST, preferred_element_type=F32)


def _sigmoid(v):
    return 0.5 * jnp.tanh(0.5 * v) + 0.5


def _gelu(v):
    return 0.5 * v * (1.0 + lax.erf(v * (1.0 / math.sqrt(2.0))))


def _gelu_grad(v):
    cdf = 0.5 * (1.0 + lax.erf(v * (1.0 / math.sqrt(2.0))))
    return cdf + v * jnp.exp(-0.5 * v * v) * (1.0 / math.sqrt(2.0 * math.pi))


def _segment_mean_matrix(seg, scale=None):
    r = lax.broadcasted_iota(jnp.int32, (LANES, LANES), 0) // seg
    c = lax.broadcasted_iota(jnp.int32, (LANES, LANES), 1) // seg
    return jnp.where(r == c, (1.0 / seg) if scale is None else scale, 0.0).astype(BF16)


def _segment_dot(v, seg):
    hi = v.astype(BF16)
    lo = (v - hi.astype(F32)).astype(BF16)
    return jnp.dot(hi, seg, preferred_element_type=F32) + jnp.dot(lo, seg, preferred_element_type=F32)


MESH = pl.DeviceIdType.MESH
ANY = pl.BlockSpec(memory_space=pl.ANY)


def _position():
    x, y, c = lax.axis_index("x"), lax.axis_index("y"), lax.axis_index("c")
    others = [(1 - x, y), (x, 1 - y), (1 - x, 1 - y)]
    return x, y, c, 2 * x + y, others


class _Ride:
    def __init__(self, ins, bufs, new_outs, sem_shapes, start, finish, reach):
        self.ins, self.bufs, self.new_outs, self.sem_shapes = list(ins), list(bufs), list(new_outs), list(sem_shapes)
        self.start, self.finish = start, finish
        self.reach = frozenset(reach)

    def entry_barrier(self):
        x, y, c, _, others = _position()
        peers = ([(x, y, 1 - c)] if "sibling" in self.reach else []) + ([(qx, qy, c) for qx, qy in others] if "chips" in self.reach else [])
        barrier = pltpu.get_barrier_semaphore()
        for peer in peers:
            pl.semaphore_signal(barrier, inc=1, device_id=peer, device_id_type=MESH)
        pl.semaphore_wait(barrier, len(peers))

    @property
    def collective_id(self):
        return {frozenset(["sibling"]): 0, frozenset(["chips"]): 1, frozenset(["sibling", "chips"]): 2}[self.reach]


def _ride_both(a, b):
    na = (len(a.ins), len(a.bufs), len(a.new_outs), len(a.sem_shapes))

    def split(ins, bufs, new, sems):
        return ((ins[:na[0]], bufs[:na[1]], new[:na[2]], sems[:na[3]]), (ins[na[0]:], bufs[na[1]:], new[na[2]:], sems[na[3]:]))

    def start(*refs):
        ra, rb = split(*refs)
        a.start(*ra)
        b.start(*rb)

    def finish(*refs):
        ra, rb = split(*refs)
        a.finish(*ra)
        b.finish(*rb)

    return _Ride(a.ins + b.ins, a.bufs + b.bufs, a.new_outs + b.new_outs, a.sem_shapes + b.sem_shapes, start, finish,
                 a.reach | b.reach)


def _call(body, *, grid, in_specs, out_specs, out_shape, operands, name, scratch_shapes=(), aliases=None, ride=None,
          prefetch=None):
    off = 0 if prefetch is None else 1
    lead = [] if prefetch is None else [prefetch]

    params = _params(collective_id=None if ride is None else ride.collective_id)

    def launch(kernel_body, in_specs_, out_specs_, out_shape_, scratch_, aliases_, *args):
        if prefetch is None:
            return pl.pallas_call(kernel_body, grid=grid, in_specs=in_specs_, out_specs=out_specs_, out_shape=out_shape_,
                                  scratch_shapes=scratch_, input_output_aliases=aliases_, name=name,
                                  compiler_params=params)(*args)
        spec = pltpu.PrefetchScalarGridSpec(num_scalar_prefetch=1, grid=grid, in_specs=in_specs_, out_specs=out_specs_,
                                            scratch_shapes=scratch_)
        return pl.pallas_call(kernel_body, grid_spec=spec, out_shape=out_shape_, input_output_aliases=aliases_, name=name,
                              compiler_params=params)(*lead, *args)

    if ride is None:
        return launch(body, list(in_specs), out_specs, out_shape, list(scratch_shapes), dict(aliases or {}), *operands)
    multi = isinstance(out_shape, (list, tuple))
    out_shapes = list(out_shape) if multi else [out_shape]
    o_specs = list(out_specs) if multi else [out_specs]
    n_in, n_out, n_scr = off + len(operands), len(out_shapes), len(scratch_shapes)
    n_ri, n_rb, n_rn = len(ride.ins), len(ride.bufs), len(ride.new_outs)

    def carrying(*refs):
        k = n_in
        r_ins = refs[k:k + n_ri]
        k += n_ri + n_rb
        outs = refs[k:k + n_out]
        k += n_out
        r_bufs = refs[k:k + n_rb]
        k += n_rb
        r_new = refs[k:k + n_rn]
        k += n_rn
        scratch = refs[k:k + n_scr]
        sems = refs[k + n_scr:]
        first, last = None, None
        for axis, size in enumerate(grid):
            pid = pl.program_id(axis)
            first = (pid == 0) if first is None else first & (pid == 0)
            last = (pid == size - 1) if last is None else last & (pid == size - 1)

        @pl.when(first)
        def _():
            ride.entry_barrier()
            ride.start(r_ins, r_bufs, r_new, sems)

        body(*refs[:n_in], *outs, *scratch)

        @pl.when(last)
        def _():
            ride.finish(r_ins, r_bufs, r_new, sems)

    all_aliases = dict(aliases or {})
    for j in range(n_rb):
        all_aliases[n_in + n_ri + j] = n_out + j
    res = launch(
        carrying, list(in_specs) + [ANY] * (n_ri + n_rb), o_specs + [ANY] * (n_rb + n_rn),
        out_shapes + [SDS(b.shape, b.dtype) for b in ride.bufs] + ride.new_outs,
        list(scratch_shapes) + [pltpu.SemaphoreType.DMA(s) for s in ride.sem_shapes], all_aliases,
        *operands, *ride.ins, *ride.bufs)
    outs = res[:n_out]
    return (list(outs) if multi else outs[0]), list(res[n_out:])


def _run_ride(name, ride):
    n_ri, n_rb, n_rn = len(ride.ins), len(ride.bufs), len(ride.new_outs)

    def body(*refs):
        r_ins = refs[:n_ri]
        r_bufs = refs[n_ri + n_rb:n_ri + 2 * n_rb]
        r_new = refs[n_ri + 2 * n_rb:n_ri + 2 * n_rb + n_rn]
        sems = refs[n_ri + 2 * n_rb + n_rn:]
        ride.entry_barrier()
        ride.start(r_ins, r_bufs, r_new, sems)
        ride.finish(r_ins, r_bufs, r_new, sems)

    return list(pl.pallas_call(
        body, in_specs=[ANY] * (n_ri + n_rb), out_specs=[ANY] * (n_rb + n_rn),
        out_shape=[SDS(b.shape, b.dtype) for b in ride.bufs] + ride.new_outs,
        scratch_shapes=[pltpu.SemaphoreType.DMA(s) for s in ride.sem_shapes],
        input_output_aliases={n_ri + j: j for j in range(n_rb)}, name=name,
        compiler_params=pltpu.CompilerParams(collective_id=ride.collective_id))(*ride.ins, *ride.bufs))


def _whole(ref, p):
    return ref[...]


def _slab(ref, p):
    return ref[p]


def _matmul(name, grid, pairs, extras, outs, dims, epi, *, slabs=1, n_acc=1, ride=None):
    n_pairs, n_ex, n_out = len(pairs), len(extras), len(outs)

    def body(*refs):
        ab = refs[:2 * n_pairs]
        ex = refs[2 * n_pairs:2 * n_pairs + n_ex]
        out_refs = refs[2 * n_pairs + n_ex:2 * n_pairs + n_ex + n_out]
        pids = tuple(pl.program_id(a) for a in range(len(grid)))
        parts = [None] * n_acc
        for p in range(slabs):
            for j, (_, _, a_pick, _, _, b_pick, acc) in enumerate(pairs):
                d = _dot(a_pick(ab[2 * j], p), b_pick(ab[2 * j + 1], p), dims)
                parts[acc] = d if parts[acc] is None else parts[acc] + d
        epi(parts, ex, out_refs, pids)

    operands, in_specs = [], []
    for a, a_spec, _, b, b_spec, _, _ in pairs:
        operands += [a, b]
        in_specs += [a_spec, b_spec]
    for e, e_spec in extras:
        operands.append(e)
        in_specs.append(e_spec)
    return _call(body, grid=grid, in_specs=in_specs, out_specs=[o[1] for o in outs], out_shape=[o[0] for o in outs],
                 operands=operands, name=name, ride=ride)


def _rms_rows(v, g):
    r = lax.rsqrt(jnp.mean(v * v, axis=-1, keepdims=True) + EPS)
    return v * r * g


def _rms_fwd(name, x, g):
    t, d = x.shape
    tm = 512

    def body(x_ref, g_ref, o_ref):
        o_ref[...] = _rms_rows(x_ref[...], g_ref[...]).astype(BF16)

    return pl.pallas_call(
        body, grid=(t // tm,),
        in_specs=[pl.BlockSpec((tm, d), lambda i: (i, 0)), pl.BlockSpec((1, d), lambda i: (0, 0))],
        out_specs=pl.BlockSpec((tm, d), lambda i: (i, 0)), out_shape=SDS((t, d), BF16), name=name,
        compiler_params=_params())(x, g)


def _epi_residual_norm(accs, ex, outs, pids):
    x_new = accs[0] + ex[0][...]
    outs[0][...] = x_new
    outs[1][...] = _rms_rows(x_new, ex[1][...]).astype(BF16)


def _epi_residual_loss(accs, ex, outs, pids):
    y = accs[0] + ex[0][...]
    err = y - ex[1][...]
    dy = err * (1.0 / err.shape[-1])
    outs[0][...] = dy
    outs[2][...] = dy.astype(BF16)

    @pl.when(pids[0] == 0)
    def _():
        outs[1][...] = jnp.zeros_like(outs[1])

    outs[1][...] += jnp.sum(err * err)


def _epi_rms_bwd(accs, ex, outs, pids):
    dh = accs[0]
    xv, g, res = ex[0][...], ex[1][...], ex[2][...]
    r = lax.rsqrt(jnp.mean(xv * xv, axis=-1, keepdims=True) + EPS)
    xh = xv * r
    dy = dh * g
    dx = res + r * (dy - xh * jnp.mean(dy * xh, axis=-1, keepdims=True))
    outs[0][...] = dx
    if len(outs) > 2:
        outs[2][...] = dx.astype(BF16)

    @pl.when(pids[0] == 0)
    def _():
        outs[1][...] = jnp.zeros_like(outs[1])

    outs[1][...] += jnp.sum(dh * xh, axis=0, keepdims=True)


def _row_spec(tm, d):
    return pl.BlockSpec((tm, d), lambda i, *_: (i, 0))


def _const_spec(shape):
    nd = len(shape)
    return pl.BlockSpec(shape, lambda *_: (0,) * nd)


def _proj_in(name, h, w, layer, ride=None):
    t, d = h.shape
    n4 = w.shape[-1]
    tm = TILES["proj_in"]

    def epi(accs, ex, outs, pids):
        outs[0][...] = accs[0].astype(BF16)

    res = _matmul(
        name, (N_CHIPS, t // tm),
        [(h, pl.BlockSpec((tm, d), lambda p, i: (i, 0)), _whole,
          w, pl.BlockSpec((None, None, d, n4), lambda p, i: (p, layer, 0, 0)), _whole, 0)],
        [], [(SDS((t, N_CHIPS * n4), BF16), pl.BlockSpec((tm, n4), lambda p, i: (i, p)))],
        NN, epi, ride=ride)
    return res[0] if ride is None else (res[0][0], res[1])


def _proj_in_near(name, h, w, ride=None):
    t, d = h.shape
    n4 = w.shape[-1]
    tm = TILES["proj_in"]

    def shard(j, s):
        return j + (j >= N_CHIPS - 1 - s[1]).astype(jnp.int32)

    def body(s_ref, h_ref, w_ref, o_ref):
        o_ref[...] = _dot(h_ref[...], w_ref[...], NN).astype(BF16)

    return _call(
        body, grid=(N_CHIPS - 1, t // tm),
        in_specs=[pl.BlockSpec((tm, d), lambda j, i, s: (i, 0)),
                  pl.BlockSpec((None, None, d, n4), lambda j, i, s: (shard(j, s), 0, 0, 0))],
        out_specs=pl.BlockSpec((tm, n4), lambda j, i, s: (i, shard(j, s))), out_shape=SDS((t, N_CHIPS * n4), BF16),
        operands=[h, w], name=name, ride=ride, prefetch=_mesh_scalars())


def _proj_in_far(name, h, slab, proj, w):
    t, d = h.shape
    n4 = slab.shape[-1]
    tm = TILES["proj_in"]

    def body(s_ref, h_ref, slab_ref, proj_in, w_in, o_ref, w_ref):
        shard = slab_ref[...]
        o_ref[...] = _dot(h_ref[...], shard, NN).astype(BF16)
        w_ref[...] = shard

    proj, w_full = _call(
        body, grid=(t // tm,),
        in_specs=[pl.BlockSpec((tm, d), lambda i, s: (i, 0)), pl.BlockSpec((d, n4), lambda i, s: (0, 0)), ANY, ANY],
        out_specs=[pl.BlockSpec((tm, n4), lambda i, s: (i, N_CHIPS - 1 - s[1])),
                   pl.BlockSpec((None, d, n4), lambda i, s: (N_CHIPS - 1 - s[1], 0, 0))],
        out_shape=[SDS(proj.shape, BF16), SDS((N_CHIPS, d, n4), BF16)],
        operands=[h, slab.reshape(d, n4), proj, w.reshape(N_CHIPS, d, n4)], aliases={3: 0, 4: 1}, name=name,
        prefetch=_mesh_scalars())
    return proj, w_full.reshape(w.shape)


def _proj_out(name, a, w, x, g_next=None, target=None):
    t, k = a.shape
    d = w.shape[-1]
    tm = TILES["proj_out"]
    if target is None:
        extras = [(x, _row_spec(tm, d)), (g_next, _const_spec((1, d)))]
        outs = [(SDS((t, d), F32), _row_spec(tm, d)), (SDS((t, d), BF16), _row_spec(tm, d))]
        epi = _epi_residual_norm
    else:
        extras = [(x, _row_spec(tm, d)), (target, _row_spec(tm, d))]
        outs = [(SDS((t, d), F32), _row_spec(tm, d)), (SDS((8, LANES), F32), _const_spec((8, LANES))),
                (SDS((t, d), BF16), _row_spec(tm, d))]
        epi = _epi_residual_loss
    return _matmul(name, (t // tm,), [(a, _row_spec(tm, k), _whole, w, _const_spec((k, d)), _whole, 0)], extras, outs, NN, epi)


def _ffn_in(name, h, wg, wu, layer, ride=None):
    t, d = h.shape
    n4 = wg.shape[-2]
    tm = TILES["ffn_in"]

    def epi(accs, ex, outs, pids):
        gate, up = accs
        s = _sigmoid(gate)
        silu = gate * s
        outs[0][...] = (up * (s + silu - silu * s)).astype(BF16)
        outs[1][...] = silu.astype(BF16)
        outs[2][...] = (silu * up).astype(BF16)

    w_spec = pl.BlockSpec((None, None, n4, d), lambda p, i: (p, layer, 0, 0))
    h_spec = pl.BlockSpec((tm, d), lambda p, i: (i, 0))
    o = (SDS((N_CHIPS, t, n4), BF16), pl.BlockSpec((None, tm, n4), lambda p, i: (p, i, 0)))
    return _matmul(name, (N_CHIPS, t // tm),
                   [(h, h_spec, _whole, wg, w_spec, _whole, 0), (h, h_spec, _whole, wu, w_spec, _whole, 1)], [],
                   [o, o, o], NT, epi, n_acc=2, ride=ride)


def _ffn_out(name, act, wd, layer, x, g_next=None, target=None, ride=None):
    _, t, n4 = act.shape
    d = wd.shape[-1]
    tm = TILES["ffn_out"]
    xs = _row_spec(tm, d)
    if target is None:
        extras = [(x, xs), (g_next, _const_spec((1, d)))]
        outs = [(SDS((t, d), F32), xs), (SDS((t, d), BF16), xs)]
        epi = _epi_residual_norm
    else:
        extras = [(x, xs), (target, xs)]
        outs = [(SDS((t, d), F32), xs), (SDS((8, LANES), F32), _const_spec((8, LANES))), (SDS((t, d), BF16), xs)]
        epi = _epi_residual_loss
    return _matmul(
        name, (t // tm,),
        [(act, pl.BlockSpec((N_CHIPS, tm, n4), lambda i: (0, i, 0)), _slab,
          wd, pl.BlockSpec((N_CHIPS, None, n4, d), lambda i: (0, layer, 0, 0)), _slab, 0)],
        extras, outs, NN, epi, slabs=N_CHIPS, ride=ride)


DACT_SLOTS = 3


def _ffn_dact(name, g, wd, layer, gate, up, ride=None):
    t, d = g.shape
    n4 = wd.shape[-2]
    tm = TILES["ffn_dact"]
    steps = t // tm
    assert steps >= DACT_SLOTS

    def body(g_ref, w_ref, gate_hbm, up_hbm, dgate_ref, dup_ref, gate_buf, up_buf, sems):
        i = pl.program_id(0)

        def fetches(step, slot):
            first_row = step * tm
            rows = pl.ds(first_row if isinstance(first_row, int) else pl.multiple_of(first_row, tm), tm)
            return (pltpu.make_async_copy(gate_hbm.at[:, rows, :], gate_buf.at[slot], sems.at[0, slot]),
                    pltpu.make_async_copy(up_hbm.at[:, rows, :], up_buf.at[slot], sems.at[1, slot]))

        @pl.when(i == 0)
        def _():
            for first in range(DACT_SLOTS - 1):
                for fetch in fetches(first, first):
                    fetch.start()

        for slot in range(DACT_SLOTS):
            @pl.when(i % DACT_SLOTS == slot)
            def _():
                ahead = i + DACT_SLOTS - 1

                @pl.when(ahead < steps)
                def _():
                    for fetch in fetches(ahead, (slot + DACT_SLOTS - 1) % DACT_SLOTS):
                        fetch.start()

                for fetch in fetches(i, slot):
                    fetch.wait()
                gv = g_ref[...]
                for p in range(N_CHIPS):
                    dact = _dot(gv, w_ref[p], NT)
                    dgate_ref[p] = (dact * gate_buf[slot, p].astype(F32)).astype(BF16)
                    dup_ref[p] = (dact * up_buf[slot, p].astype(F32)).astype(BF16)

    blk = pl.BlockSpec((N_CHIPS, tm, n4), lambda i: (0, i, 0))
    ring = pltpu.VMEM((DACT_SLOTS, N_CHIPS, tm, n4), BF16)
    return _call(
        body, grid=(steps,),
        in_specs=[_row_spec(tm, d), pl.BlockSpec((N_CHIPS, None, n4, d), lambda i: (0, layer, 0, 0)), ANY, ANY],
        out_specs=[blk, blk], out_shape=[SDS((N_CHIPS, t, n4), BF16)] * 2, operands=[g, wd, gate, up], name=name,
        scratch_shapes=[ring, ring, pltpu.SemaphoreType.DMA((2, DACT_SLOTS))], ride=ride)


def _copy_epi(accs, ex, outs, pids):
    for a, o in zip(accs, outs):
        o[...] = a.astype(o.dtype)


def _dgrad_cols(name, dz_list, w_list, layer, three_d, x, g, res, bf16_copy=True, w_rows=False, ride=None):
    t, d = x.shape
    n4 = w_list[0].shape[-2 if w_rows else -1]
    tm = TILES["dgrad_cols"]
    if three_d:
        zs, z_pick = pl.BlockSpec((N_CHIPS, tm, n4), lambda i: (0, i, 0)), _slab
    else:
        zs, z_pick = _row_spec(tm, N_CHIPS * n4), (lambda ref, p: ref[:, p * n4:(p + 1) * n4])
    ws = pl.BlockSpec((N_CHIPS, None) + ((n4, d) if w_rows else (d, n4)), lambda i: (0, layer, 0, 0))
    xs = _row_spec(tm, d)
    return _matmul(
        name, (t // tm,), [(dz, zs, z_pick, w, ws, _slab, 0) for dz, w in zip(dz_list, w_list)],
        [(x, xs), (g, _const_spec((1, d))), (res, xs)],
        [(SDS((t, d), F32), xs), (SDS((1, d), F32), _const_spec((1, d)))] + ([(SDS((t, d), BF16), xs)] if bf16_copy else []),
        NN if w_rows else NT, _epi_rms_bwd, slabs=N_CHIPS, ride=ride)


def _dgrad_rows(name, g, w):
    t, d = g.shape
    k = w.shape[0]
    tm = TILES["dgrad_rows"]
    return _matmul(name, (t // tm,), [(g, _row_spec(tm, d), _whole, w, _const_spec((k, d)), _whole, 0)], [],
                   [(SDS((t, k), F32), _row_spec(tm, k))], NT, _copy_epi)[0]


A_TILE = 256


def _a_common(p_ref, lg_ref, lb_ref):
    pv = p_ref[...].astype(F32)
    a = _gelu(pv)
    u, v = a[:, :A_WIDTH], a[:, A_WIDTH:]
    vc = v - jnp.mean(v, axis=-1, keepdims=True)
    rs = lax.rsqrt(jnp.mean(vc * vc, axis=-1, keepdims=True) + EPS)
    vhat = vc * rs
    vn = vhat * lg_ref[...] + lb_ref[...]
    return pv, u, vhat, rs, vn.astype(BF16)


def _tril_weights(w_ref, g):
    r = lax.broadcasted_iota(jnp.int32, (CHUNK, CHUNK), 0)
    c = lax.broadcasted_iota(jnp.int32, (CHUNK, CHUNK), 1)
    return jnp.where(c <= r, w_ref[g], 0.0).astype(BF16), c <= r


def _mixer_a_fwd(proj, lg, lb, w, bias_t):
    t = proj.shape[0]

    def body(p_ref, lg_ref, lb_ref, w_ref, bt_ref, o_ref):
        _, u, _, _, vnb = _a_common(p_ref, lg_ref, lb_ref)
        for g in range(A_GROUPS):
            wt, _ = _tril_weights(w_ref, g)
            cs = slice(g * CHUNK, (g + 1) * CHUNK)
            for ch in range(A_TILE // CHUNK):
                rs_ = slice(ch * CHUNK, (ch + 1) * CHUNK)
                mixed = _dot(wt, vnb[rs_, cs], NN) + bt_ref[:, g:g + 1]
                o_ref[rs_, cs] = (u[rs_, cs] * mixed).astype(BF16)

    return pl.pallas_call(
        body, grid=(t // A_TILE,),
        in_specs=[pl.BlockSpec((A_TILE, 2 * A_WIDTH), lambda i: (i, 0)), _const_spec((1, A_WIDTH)),
                  _const_spec((1, A_WIDTH)), _const_spec((A_GROUPS, CHUNK, CHUNK)), _const_spec((CHUNK, A_GROUPS))],
        out_specs=pl.BlockSpec((A_TILE, A_WIDTH), lambda i: (i, 0)), out_shape=SDS((t, A_WIDTH), BF16),
        name="mixer_a_fwd", compiler_params=_params())(proj, lg, lb, w, bias_t)


def _mixer_a_bwd(proj, dcat, lg, lb, w, bias_t):
    t = proj.shape[0]

    def body(p_ref, da_ref, lg_ref, lb_ref, w_ref, bt_ref, dp_ref, dw_ref, dbt_ref, dlg_ref, dlb_ref, du_scr, dvn_scr):
        @pl.when(pl.program_id(0) == 0)
        def _():
            dw_ref[...] = jnp.zeros_like(dw_ref)
            dbt_ref[...] = jnp.zeros_like(dbt_ref)
            dlg_ref[...] = jnp.zeros_like(dlg_ref)
            dlb_ref[...] = jnp.zeros_like(dlb_ref)

        pv, u, vhat, rs, vnb = _a_common(p_ref, lg_ref, lb_ref)
        da = da_ref[...]
        for g in range(A_GROUPS):
            wt, keep = _tril_weights(w_ref, g)
            cs = slice(g * CHUNK, (g + 1) * CHUNK)
            for ch in range(A_TILE // CHUNK):
                rs_ = slice(ch * CHUNK, (ch + 1) * CHUNK)
                vg = vnb[rs_, cs]
                mixed = _dot(wt, vg, NN) + bt_ref[:, g:g + 1]
                du_scr[rs_, cs] = da[rs_, cs] * mixed
                dmx = da[rs_, cs] * u[rs_, cs]
                dw_ref[g] += jnp.where(keep, _dot(dmx, vg, NT), 0.0)
                dvn_scr[rs_, cs] = _dot(wt, dmx, TN)
                dbt_ref[:, g:g + 1] += jnp.sum(dmx, axis=1, keepdims=True)
        dvn = dvn_scr[...]
        dlg_ref[...] += jnp.sum(dvn * vhat, axis=0, keepdims=True)
        dlb_ref[...] += jnp.sum(dvn, axis=0, keepdims=True)
        dvh = dvn * lg_ref[...]
        dv = rs * (dvh - jnp.mean(dvh, axis=-1, keepdims=True) - vhat * jnp.mean(dvh * vhat, axis=-1, keepdims=True))
        gp = _gelu_grad(pv)
        dp_ref[:, :A_WIDTH] = (du_scr[...] * gp[:, :A_WIDTH]).astype(BF16)
        dp_ref[:, A_WIDTH:] = (dv * gp[:, A_WIDTH:]).astype(BF16)

    return pl.pallas_call(
        body, grid=(t // A_TILE,),
        in_specs=[pl.BlockSpec((A_TILE, 2 * A_WIDTH), lambda i: (i, 0)), pl.BlockSpec((A_TILE, A_WIDTH), lambda i: (i, 0)),
                  _const_spec((1, A_WIDTH)), _const_spec((1, A_WIDTH)), _const_spec((A_GROUPS, CHUNK, CHUNK)),
                  _const_spec((CHUNK, A_GROUPS))],
        out_specs=[pl.BlockSpec((A_TILE, 2 * A_WIDTH), lambda i: (i, 0)), _const_spec((A_GROUPS, CHUNK, CHUNK)),
                   _const_spec((CHUNK, A_GROUPS)), _const_spec((1, A_WIDTH)), _const_spec((1, A_WIDTH))],
        out_shape=[SDS((t, 2 * A_WIDTH), BF16), SDS((A_GROUPS, CHUNK, CHUNK), F32), SDS((CHUNK, A_GROUPS), F32),
                   SDS((1, A_WIDTH), F32), SDS((1, A_WIDTH), F32)],
        scratch_shapes=[pltpu.VMEM((A_TILE, A_WIDTH), F32), pltpu.VMEM((A_TILE, A_WIDTH), F32)],
        name="mixer_a_bwd", compiler_params=_params())(proj, dcat, lg, lb, w, bias_t)


def _rope_tables(t):
    half = ROPE_DIM // 2
    inv_freq = ROPE_THETA ** (-jnp.arange(half, dtype=F32) * 2.0 / ROPE_DIM)
    ang = jnp.arange(t, dtype=F32)[:, None] * inv_freq[None, :]
    cos, sin = jnp.cos(ang), jnp.sin(ang)
    one = jnp.ones((t, HEAD_DIM - ROPE_DIM), F32)
    zero = jnp.zeros((t, HEAD_DIM - ROPE_DIM), F32)
    zh = jnp.zeros((t, half), F32)
    c = jnp.concatenate([cos, cos, one], axis=1)
    s1 = jnp.concatenate([-sin, zh, zero], axis=1)
    s2 = jnp.concatenate([zh, sin, zero], axis=1)
    return tuple(jnp.tile(a, (1, LANES // HEAD_DIM)) for a in (c, s1, s2))


QK_TILE = 512
QK_ROWS = 64
QK_COLS = 2 * N_DIL * B_WIDTH


CHUNKS = B_WIDTH // LANES


def _regroup_out(scr, first, out_ref, rate, tile):
    rows = tile // rate
    for rho in range(rate):
        for c in range(CHUNKS):
            out_ref[rho, :, c * LANES:(c + 1) * LANES] = scr[first + c, pl.ds(rho, rows, stride=rate), :].astype(out_ref.dtype)


def _regroup_in(x_ref, scr, rate, tile):
    rows = tile // rate
    for rho in range(rate):
        for c in range(CHUNKS):
            scr[c, pl.ds(rho, rows, stride=rate), :] = x_ref[rho, :, c * LANES:(c + 1) * LANES].astype(F32)


def _regrouped_spec(rate, tile):
    return pl.BlockSpec((rate, tile // rate, B_WIDTH), lambda i, *_: (0, i, 0))


def _qk_fwd(proj, gains, tabs, ride=None):
    t = proj.shape[0]
    col0 = 2 * A_WIDTH // 1024
    r1, r2 = DIL_RATES[1], DIL_RATES[2]

    def body(p_ref, g_ref, c_ref, s1_ref, s2_ref, o_ref, q1_ref, q2_ref, k1_ref, k2_ref, scr):
        seg = _segment_mean_matrix(HEAD_DIM)
        for r0 in range(0, QK_TILE, QK_ROWS):
            rows = slice(r0, r0 + QK_ROWS)
            c, s1, s2 = c_ref[rows, :], s1_ref[rows, :], s2_ref[rows, :]
            for ci in range(1024 // LANES):
                ls = slice(ci * LANES, (ci + 1) * LANES)
                xv = p_ref[rows, ls].astype(F32)
                r = lax.rsqrt(_segment_dot(xv * xv, seg) + EPS)
                y = xv * r * g_ref[:, ls]
                val = y * c + pltpu.roll(y, LANES - 8, axis=1) * s1 + pltpu.roll(y, 8, axis=1) * s2
                o_ref[rows, ls] = val.astype(BF16)
                scr[ci, rows, :] = val

        j = pl.program_id(1)

        @pl.when(j == 0)
        def _():
            _regroup_out(scr, CHUNKS, q1_ref, r1, QK_TILE)

        @pl.when(j == 1)
        def _():
            _regroup_out(scr, 0, q2_ref, r2, QK_TILE)

        @pl.when(j == 2)
        def _():
            _regroup_out(scr, 0, k1_ref, r1, QK_TILE)
            _regroup_out(scr, CHUNKS, k2_ref, r2, QK_TILE)

    tab = pl.BlockSpec((QK_TILE, LANES), lambda i, j: (i, 0))
    g1, g2 = SDS((r1, t // r1, B_WIDTH), BF16), SDS((r2, t // r2, B_WIDTH), BF16)
    s1_, s2_ = _regrouped_spec(r1, QK_TILE), _regrouped_spec(r2, QK_TILE)
    return _call(
        body, grid=(t // QK_TILE, QK_COLS // 1024),
        in_specs=[pl.BlockSpec((QK_TILE, 1024), lambda i, j: (i, col0 + j)), pl.BlockSpec((1, 1024), lambda i, j: (0, j)),
                  tab, tab, tab],
        out_specs=[pl.BlockSpec((QK_TILE, 1024), lambda i, j: (i, j)), s1_, s2_, s1_, s2_],
        out_shape=[SDS((t, QK_COLS), BF16), g1, g2, g1, g2],
        scratch_shapes=[pltpu.VMEM((2 * CHUNKS, QK_TILE, LANES), F32)],
        operands=[proj, gains, *tabs], name="qk_norm_rope_fwd", ride=ride)


PERM_TILE = 512


def _permute(name, items):
    t = items[0][0].shape[0]
    n = len(items)

    def body(*refs):
        scr = refs[-1]
        for x_ref, o_ref, (_, _, rate) in zip(refs[:n], refs[n:2 * n], items):
            for ci in range(CHUNKS):
                scr[ci] = x_ref[:, ci * LANES:(ci + 1) * LANES].astype(F32)
            _regroup_out(scr, 0, o_ref, rate, PERM_TILE)

    return pl.pallas_call(
        body, grid=(t // PERM_TILE,),
        in_specs=[pl.BlockSpec((PERM_TILE, B_WIDTH), functools.partial(lambda cb, i: (i, cb), cb)) for _, cb, _ in items],
        out_specs=[_regrouped_spec(rate, PERM_TILE) for _, _, rate in items],
        out_shape=[SDS((rate, t // rate, B_WIDTH), a.dtype) for a, _, rate in items],
        scratch_shapes=[pltpu.VMEM((CHUNKS, PERM_TILE, LANES), F32)],
        name=name, compiler_params=_params())(*[a for a, _, _ in items])


def _head_lane_mask(h):
    lane = lax.broadcasted_iota(jnp.int32, (1, LANES), 1)
    return (lane < HEAD_DIM) if h == 0 else (lane >= HEAD_DIM)


def _attn_fwd(name, q, k, v, ride=None):
    rate, length = q[0].shape[0], q[0].shape[1]
    nb = length // ATT_BLOCK
    scale = HEAD_DIM ** -0.5

    def body(q_ref, kc_ref, kp_ref, vc_ref, vp_ref, o_ref, l_ref):
        n = pl.program_id(1)
        qi = lax.broadcasted_iota(jnp.int32, (ATT_BLOCK, 2 * ATT_BLOCK), 0)
        cj = lax.broadcasted_iota(jnp.int32, (ATT_BLOCK, 2 * ATT_BLOCK), 1)
        has_prev = jnp.where(n > 0, 0, 2 * ATT_BLOCK)
        mask = ((cj < ATT_BLOCK) & (cj >= qi + has_prev)) | ((cj >= ATT_BLOCK) & (cj - ATT_BLOCK <= qi))
        heads = [(hp, h) for hp in range(CHUNKS) for h in range(2)]
        q2, k2, v2 = {}, {}, {}
        for hp in range(CHUNKS):
            ls = slice(hp * LANES, (hp + 1) * LANES)
            q2[hp] = q_ref[:, ls]
            k2[hp] = jnp.concatenate([kp_ref[:, ls], kc_ref[:, ls]], axis=0)
            v2[hp] = jnp.concatenate([vp_ref[:, ls], vc_ref[:, ls]], axis=0)
        scores = {}
        for hp, h in heads:
            scores[hp, h] = _dot(jnp.where(_head_lane_mask(h), q2[hp], jnp.zeros_like(q2[hp])), k2[hp], NT) * scale
        probs, lses = {}, {}
        for hp, h in heads:
            s = jnp.where(mask, scores[hp, h], NEG_INF)
            m = jnp.max(s, axis=1, keepdims=True)
            p = jnp.exp(s - m)
            den = jnp.sum(p, axis=1, keepdims=True)
            lses[hp, h] = m + jnp.log(den)
            probs[hp, h] = (p / den).astype(BF16)
        for hp in range(CHUNKS):
            ls = slice(hp * LANES, (hp + 1) * LANES)
            o_acc = None
            for h in range(2):
                o = _dot(probs[hp, h], jnp.where(_head_lane_mask(h), v2[hp], jnp.zeros_like(v2[hp])), NN)
                o_acc = o if o_acc is None else o_acc + o
            o_ref[:, ls] = o_acc
            zeros = jnp.zeros((ATT_BLOCK, LANES), F32)
            l_ref[:, ls] = jnp.where(_head_lane_mask(1), lses[hp, 1] + zeros, lses[hp, 0] + zeros)

    def cur(cb):
        return pl.BlockSpec((None, ATT_BLOCK, B_WIDTH), lambda r, n: (r, n, cb))

    def prev(cb):
        return pl.BlockSpec((None, ATT_BLOCK, B_WIDTH), lambda r, n: (r, jnp.maximum(n - 1, 0), cb))

    out = pl.BlockSpec((None, ATT_BLOCK, B_WIDTH), lambda r, n: (r, n, 0))
    return _call(
        body, grid=(rate, nb),
        in_specs=[cur(q[1]), cur(k[1]), prev(k[1]), cur(v[1]), prev(v[1])],
        out_specs=[out, out], out_shape=[SDS((rate, length, B_WIDTH), F32)] * 2,
        operands=[q[0], k[0], k[0], v[0], v[0]], name=name, ride=ride)


def _attn_merge(a_out, o_list, l_list):
    t = a_out.shape[0]
    tm = PERM_TILE
    r1, r2 = DIL_RATES[1], DIL_RATES[2]

    def body(a_ref, o0, o1, o2, l0, l1, l2, cat_ref, lt_ref, lt1_ref, lt2_ref, so1, so2, sl1, sl2, slt):
        _regroup_in(o1, so1, r1, tm)
        _regroup_in(l1, sl1, r1, tm)
        _regroup_in(o2, so2, r2, tm)
        _regroup_in(l2, sl2, r2, tm)
        cat_ref[:, :A_WIDTH] = a_ref[...]
        for c in range(CHUNKS):
            ls = slice(c * LANES, (c + 1) * LANES)
            lg = [l0[:, ls], sl1[c], sl2[c]]
            m = jnp.maximum(jnp.maximum(lg[0], lg[1]), lg[2])
            es = [jnp.exp(l - m) for l in lg]
            den = es[0] + es[1] + es[2]
            b = (es[0] * o0[:, ls] + es[1] * so1[c] + es[2] * so2[c]) / den
            cat_ref[:, A_WIDTH + c * LANES:A_WIDTH + (c + 1) * LANES] = b.astype(BF16)
            lt = m + jnp.log(den)
            lt_ref[:, ls] = lt
            slt[c] = lt
        _regroup_out(slt, 0, lt1_ref, r1, tm)
        _regroup_out(slt, 0, lt2_ref, r2, tm)

    blk = _row_spec(tm, B_WIDTH)
    g1, g2 = _regrouped_spec(r1, tm), _regrouped_spec(r2, tm)
    return pl.pallas_call(
        body, grid=(t // tm,), in_specs=[blk, blk, g1, g2, blk, g1, g2],
        out_specs=[_row_spec(tm, A_WIDTH + B_WIDTH), blk, g1, g2],
        out_shape=[SDS((t, A_WIDTH + B_WIDTH), BF16), SDS((t, B_WIDTH), F32), SDS((r1, t // r1, B_WIDTH), F32),
                   SDS((r2, t // r2, B_WIDTH), F32)],
        scratch_shapes=[pltpu.VMEM((CHUNKS, tm, LANES), F32)] * 5,
        name="attn_merge", compiler_params=_params())(a_out, *o_list, *l_list)


def _attn_bwd_prep(dcat, cat):
    t = dcat.shape[0]
    tm = PERM_TILE
    r1, r2 = DIL_RATES[1], DIL_RATES[2]

    def body(d_ref, b_ref, db_ref, dd_ref, db1_ref, dd1_ref, db2_ref, dd2_ref, sdb, sdd):
        seg = _segment_mean_matrix(HEAD_DIM, scale=1.0)
        for c in range(CHUNKS):
            ls = slice(c * LANES, (c + 1) * LANES)
            d = d_ref[:, ls]
            dsum = _segment_dot(d * b_ref[:, ls].astype(F32), seg)
            db_ref[:, ls] = d.astype(BF16)
            dd_ref[:, ls] = dsum
            sdb[c] = d
            sdd[c] = dsum
        _regroup_out(sdb, 0, db1_ref, r1, tm)
        _regroup_out(sdd, 0, dd1_ref, r1, tm)
        _regroup_out(sdb, 0, db2_ref, r2, tm)
        _regroup_out(sdd, 0, dd2_ref, r2, tm)

    right = pl.BlockSpec((tm, B_WIDTH), lambda i: (i, 1))
    blk = _row_spec(tm, B_WIDTH)
    g1, g2 = _regrouped_spec(r1, tm), _regrouped_spec(r2, tm)
    return pl.pallas_call(
        body, grid=(t // tm,), in_specs=[right, right], out_specs=[blk, blk, g1, g1, g2, g2],
        out_shape=[SDS((t, B_WIDTH), BF16), SDS((t, B_WIDTH), F32), SDS((r1, t // r1, B_WIDTH), BF16),
                   SDS((r1, t // r1, B_WIDTH), F32), SDS((r2, t // r2, B_WIDTH), BF16), SDS((r2, t // r2, B_WIDTH), F32)],
        scratch_shapes=[pltpu.VMEM((CHUNKS, tm, LANES), F32)] * 2,
        name="attn_bwd_prep", compiler_params=_params())(dcat, cat)


def _attn_bwd(name, q, k, v, db, lse, dd, ride=None):
    rate, length = db.shape[0], db.shape[1]
    nb = length // ATT_BLOCK
    scale = HEAD_DIM ** -0.5

    def body(qa_ref, qb_ref, k_ref, v_ref, dba_ref, dbb_ref, la_ref, lb_ref, da_ref, dbd_ref, dq_ref, dk_ref, dv_ref, carry):
        m = pl.program_id(1)

        @pl.when(m == 0)
        def _():
            carry[...] = jnp.zeros_like(carry)

        row = lax.broadcasted_iota(jnp.int32, (2 * ATT_BLOCK, ATT_BLOCK), 0)
        kj = lax.broadcasted_iota(jnp.int32, (2 * ATT_BLOCK, ATT_BLOCK), 1)
        no_next = jnp.where(m + 1 < nb, 0, 2 * ATT_BLOCK)
        mask = ((row < ATT_BLOCK) & (kj <= row)) | ((row >= ATT_BLOCK) & (kj >= row - ATT_BLOCK + no_next))
        heads = [(hp, h) for hp in range(CHUNKS) for h in range(2)]
        q2, db2, lse2, dd2, k2, v2 = {}, {}, {}, {}, {}, {}
        for hp in range(CHUNKS):
            ls = slice(hp * LANES, (hp + 1) * LANES)
            k2[hp], v2[hp] = k_ref[:, ls], v_ref[:, ls]
            q2[hp] = jnp.concatenate([qa_ref[:, ls], qb_ref[:, ls]], axis=0)
            db2[hp] = jnp.concatenate([dba_ref[:, ls], dbb_ref[:, ls]], axis=0)
            lse2[hp] = jnp.concatenate([la_ref[:, ls], lb_ref[:, ls]], axis=0)
            dd2[hp] = jnp.concatenate([da_ref[:, ls], dbd_ref[:, ls]], axis=0)
        km, scores, dps = {}, {}, {}
        for hp, h in heads:
            hm = _head_lane_mask(h)
            km[hp, h] = jnp.where(hm, k2[hp], jnp.zeros_like(k2[hp]))
            scores[hp, h] = _dot(q2[hp], km[hp, h], NT) * scale
            dps[hp, h] = _dot(db2[hp], jnp.where(hm, v2[hp], jnp.zeros_like(v2[hp])), NT)
        probs, dss = {}, {}
        for hp, h in heads:
            hm = _head_lane_mask(h)
            lse_col = jnp.max(jnp.where(hm, lse2[hp], NEG_INF), axis=1, keepdims=True)
            dd_col = jnp.max(jnp.where(hm, dd2[hp], NEG_INF), axis=1, keepdims=True)
            p = jnp.where(mask, jnp.exp(scores[hp, h] - lse_col), 0.0)
            probs[hp, h] = p.astype(BF16)
            dss[hp, h] = (p * (dps[hp, h] - dd_col) * scale).astype(BF16)
        for hp in range(CHUNKS):
            ls = slice(hp * LANES, (hp + 1) * LANES)
            dq_acc, dk_acc, dv_acc = None, None, None
            for h in range(2):
                hm = _head_lane_mask(h)
                dvc = _dot(probs[hp, h], jnp.where(hm, db2[hp], jnp.zeros_like(db2[hp])), TN)
                dqc = _dot(dss[hp, h], km[hp, h], NN)
                dkc = _dot(dss[hp, h], jnp.where(hm, q2[hp], jnp.zeros_like(q2[hp])), TN)
                dq_acc = dqc if dq_acc is None else dq_acc + dqc
                dk_acc = dkc if dk_acc is None else dk_acc + dkc
                dv_acc = dvc if dv_acc is None else dv_acc + dvc
            dq_ref[:, ls] = (dq_acc[:ATT_BLOCK] + carry[:, ls]).astype(BF16)
            carry[:, ls] = dq_acc[ATT_BLOCK:]
            dk_ref[:, ls] = dk_acc.astype(BF16)
            dv_ref[:, ls] = dv_acc.astype(BF16)

    def cur(cb):
        return pl.BlockSpec((None, ATT_BLOCK, B_WIDTH), lambda r, n: (r, n, cb))

    def nxt(cb):
        return pl.BlockSpec((None, ATT_BLOCK, B_WIDTH), lambda r, n: (r, jnp.minimum(n + 1, nb - 1), cb))

    out = cur(0)
    return _call(
        body, grid=(rate, nb),
        in_specs=[cur(q[1]), nxt(q[1]), cur(k[1]), cur(v[1]), cur(0), nxt(0), cur(0), nxt(0), cur(0), nxt(0)],
        out_specs=[out, out, out], out_shape=[SDS((rate, length, B_WIDTH), BF16)] * 3,
        scratch_shapes=[pltpu.VMEM((ATT_BLOCK, B_WIDTH), F32)],
        operands=[q[0], q[0], k[0], v[0], db, db, lse, lse, dd, dd], name=name, ride=ride)


AB_IN = 2 * A_WIDTH + 3 * N_DIL * B_WIDTH
ASM_TILE = 256


def _dproj_assemble(proj, d_a, dq, dk, dv, gains, tabs, ride=None):
    t = proj.shape[0]
    n_in = 3 * N_DIL

    def body(p_ref, da_ref, *rest):
        grads = rest[:n_in]
        g_ref, c_ref, s1_ref, s2_ref, o_ref, dg_ref = rest[n_in:n_in + 6]
        scratch = rest[n_in + 6:]

        @pl.when(pl.program_id(0) == 0)
        def _():
            dg_ref[...] = jnp.zeros_like(dg_ref)

        chunk = {}
        k_scr = 0
        for j in range(n_in):
            g = j % N_DIL
            if DIL_RATES[g] == 1:
                for ci in range(CHUNKS):
                    chunk[j, ci] = functools.partial(lambda r, ci: r[:, ci * LANES:(ci + 1) * LANES].astype(F32), grads[j], ci)
            else:
                scr = scratch[k_scr]
                k_scr += 1
                _regroup_in(grads[j], scr, DIL_RATES[g], ASM_TILE)
                for ci in range(CHUNKS):
                    chunk[j, ci] = functools.partial(lambda s, ci: s[ci], scr, ci)

        seg = _segment_mean_matrix(HEAD_DIM)
        c, s1, s2 = c_ref[...], s1_ref[...], s2_ref[...]
        o_ref[:, :2 * A_WIDTH] = da_ref[...]
        for jg in range(2 * N_DIL):
            for ci in range(CHUNKS):
                col = jg * B_WIDTH + ci * LANES
                src = slice(2 * A_WIDTH + col, 2 * A_WIDTH + col + LANES)
                xv = p_ref[:, src].astype(F32)
                r = lax.rsqrt(_segment_dot(xv * xv, seg) + EPS)
                xh = xv * r
                gain = g_ref[:, col:col + LANES]
                do = chunk[jg, ci]()
                dy = do * c + pltpu.roll(do * s1, 8, axis=1) + pltpu.roll(do * s2, LANES - 8, axis=1)
                dg_ref[:, col:col + LANES] += jnp.sum(dy * xh, axis=0, keepdims=True)
                dxh = dy * gain
                o_ref[:, src] = (r * (dxh - xh * _segment_dot(dxh * xh, seg))).astype(BF16)
        v0 = 2 * A_WIDTH + QK_COLS
        for g in range(N_DIL):
            for ci in range(CHUNKS):
                col = v0 + g * B_WIDTH + ci * LANES
                o_ref[:, col:col + LANES] = chunk[2 * N_DIL + g, ci]().astype(BF16)

    specs = [_row_spec(ASM_TILE, B_WIDTH) if r == 1 else _regrouped_spec(r, ASM_TILE) for r in DIL_RATES] * 3
    n_scr = 3 * sum(1 for r in DIL_RATES if r > 1)
    tab = _row_spec(ASM_TILE, LANES)
    return _call(
        body, grid=(t // ASM_TILE,),
        in_specs=[_row_spec(ASM_TILE, AB_IN), _row_spec(ASM_TILE, 2 * A_WIDTH)] + specs
        + [_const_spec((1, QK_COLS)), tab, tab, tab],
        out_specs=[_row_spec(ASM_TILE, AB_IN), _const_spec((1, QK_COLS))],
        out_shape=[SDS((t, AB_IN), BF16), SDS((1, QK_COLS), F32)],
        scratch_shapes=[pltpu.VMEM((CHUNKS, ASM_TILE, LANES), F32)] * n_scr,
        operands=[proj, d_a, *dq, *dk, *dv, gains, *tabs], name="dproj_assemble", ride=ride)


def _fold_heads(dg_lane):
    n = dg_lane.shape[1]

    def body(x_ref, o_ref):
        r = lax.broadcasted_iota(jnp.int32, (B_WIDTH, B_WIDTH), 0) % HEAD_DIM
        c = lax.broadcasted_iota(jnp.int32, (B_WIDTH, B_WIDTH), 1) % HEAD_DIM
        fold = jnp.where(r == c, 1.0, 0.0).astype(F32)
        for jg in range(n // B_WIDTH):
            ls = slice(jg * B_WIDTH, (jg + 1) * B_WIDTH)
            o_ref[:, ls] = _dot_hi(jnp.broadcast_to(x_ref[:, ls], (8, B_WIDTH)), fold)

    return pl.pallas_call(body, out_shape=SDS((8, n), F32), name="fold_heads", compiler_params=_params())(dg_lane)


CD_TILE = 256
TAP_ROWS = 64
CD_IN = 2 * C_WIDTH + 3 * 512


def _shifted_copies(src, dst, rows):
    dst[0, :rows] = src[...]
    for b in range(1, 8):
        dst[b, :rows - 8] = src[pl.ds(b, rows - 8), :]


def _rows_from(shifted, start, n, lanes=slice(None)):
    b = start % 8
    return shifted[b, pl.ds(start - b, n), lanes]


def _mixer_cd_fwd(proj, cw, cb, lg, lb, dw):
    t = proj.shape[0]
    per = CD_TILE // HALO

    def body(h_ref, m_ref, cw_ref, cb_ref, lg_ref, lb_ref, dw_ref, o_ref, c1_ref, c_scr, e_scr, c_sh):
        not_first = (pl.program_id(0) > 0).astype(F32)
        lanes = [slice(c * LANES, (c + 1) * LANES) for c in range(C_WIDTH // LANES)]

        def col(ref, part, ls):
            return ref[:, part * C_WIDTH + ls.start:part * C_WIDTH + ls.stop].astype(F32)

        for ls in lanes:
            c_scr[:HALO, ls] = col(h_ref, 0, ls) * _sigmoid(col(h_ref, 1, ls)) * not_first
            c_scr[HALO:, ls] = col(m_ref, 0, ls) * _sigmoid(col(m_ref, 1, ls))
            e_scr[:HALO, ls] = col(h_ref, 3, ls) * col(h_ref, 4, ls) * not_first
            e_scr[HALO:, ls] = col(m_ref, 3, ls) * col(m_ref, 4, ls)
        _shifted_copies(c_scr, c_sh, HALO + CD_TILE)
        for ls in lanes:
            for r0 in range(0, CD_TILE, TAP_ROWS):
                acc = jnp.zeros((TAP_ROWS, LANES), F32)
                for k in range(C_KERNEL):
                    acc = acc + cw_ref[k:k + 1, ls] * _rows_from(c_sh, r0 + HALO - (C_KERNEL - 1) + k, TAP_ROWS, ls)
                c1_ref[r0:r0 + TAP_ROWS, ls] = acc + cb_ref[:, ls]
        mean = sum(jnp.sum(c1_ref[:, ls], axis=-1, keepdims=True) for ls in lanes) * (1.0 / C_WIDTH)
        var = sum(jnp.sum((c1_ref[:, ls] - mean) ** 2, axis=-1, keepdims=True) for ls in lanes) * (1.0 / C_WIDTH)
        rs = lax.rsqrt(var + EPS)
        for ls in lanes:
            c2 = (c1_ref[:, ls] - mean) * rs * lg_ref[:, ls] + lb_ref[:, ls]
            o_ref[:, ls] = (c2 * _sigmoid(c2)).astype(BF16)
            d1 = jnp.zeros((CD_TILE, LANES), F32)
            for k in range(D_KERNEL):
                d1 = d1 + dw_ref[k:k + 1, ls] * e_scr[pl.ds(HALO - (D_KERNEL - 1) + k, CD_TILE), ls]
            o_ref[:, C_WIDTH + ls.start:C_WIDTH + ls.stop] = (col(m_ref, 2, ls) * d1).astype(BF16)

    return pl.pallas_call(
        body, grid=(t // CD_TILE,),
        in_specs=[pl.BlockSpec((HALO, CD_IN), lambda i: (jnp.maximum(i * per - 1, 0), 0)), _row_spec(CD_TILE, CD_IN),
                  _const_spec((32, C_WIDTH)), _const_spec((1, C_WIDTH)), _const_spec((1, C_WIDTH)), _const_spec((1, C_WIDTH)),
                  _const_spec((8, C_WIDTH))],
        out_specs=[_row_spec(CD_TILE, 2 * C_WIDTH), _row_spec(CD_TILE, C_WIDTH)],
        out_shape=[SDS((t, 2 * C_WIDTH), BF16), SDS((t, C_WIDTH), F32)],
        scratch_shapes=[pltpu.VMEM((HALO + CD_TILE, C_WIDTH), F32)] * 2 + [pltpu.VMEM((8, HALO + CD_TILE, C_WIDTH), F32)],
        name="mixer_cd_fwd", compiler_params=_params())(proj, proj, cw, cb, lg, lb, dw)


def _mixer_cd_bwd(proj, dcat, c1, cw, lg, lb, dw, ride=None):
    t = proj.shape[0]
    per = CD_TILE // HALO
    nt = t // CD_TILE
    ext = CD_TILE + HALO

    def body(hp_ref, m_ref, hn_ref, dm_ref, dn_ref, c1m_ref, c1n_ref, cw_ref, lg_ref, lb_ref, dw_ref,
             dp_ref, dcw_ref, dcb_ref, dlg_ref, dlb_ref, ddw_ref, c_scr, e_scr, dc1_scr, dd1_scr, c_sh, dc1_sh, dcw_acc,
             dvh_scr, vhat_scr):
        i = pl.program_id(0)

        @pl.when(i == 0)
        def _():
            for r in (dcw_acc, dcb_ref, dlg_ref, dlb_ref, ddw_ref):
                r[...] = jnp.zeros_like(r)

        not_first = (i > 0).astype(F32)
        not_last = (i < nt - 1).astype(F32)
        main = slice(HALO, HALO + CD_TILE)
        lanes = [slice(c * LANES, (c + 1) * LANES) for c in range(C_WIDTH // LANES)]

        def col(ref, part, ls):
            return ref[:, part * C_WIDTH + ls.start:part * C_WIDTH + ls.stop].astype(F32)

        for ls in lanes:
            c_scr[:HALO, ls] = col(hp_ref, 0, ls) * _sigmoid(col(hp_ref, 1, ls)) * not_first
            c_scr[main, ls] = col(m_ref, 0, ls) * _sigmoid(col(m_ref, 1, ls))
            c_scr[HALO + CD_TILE:, ls] = col(hn_ref, 0, ls) * _sigmoid(col(hn_ref, 1, ls)) * not_last
            e_scr[:HALO, ls] = col(hp_ref, 3, ls) * col(hp_ref, 4, ls) * not_first
            e_scr[main, ls] = col(m_ref, 3, ls) * col(m_ref, 4, ls)
            e_scr[HALO + CD_TILE:, ls] = col(hn_ref, 3, ls) * col(hn_ref, 4, ls) * not_last
        _shifted_copies(c_scr, c_sh, 2 * HALO + CD_TILE)

        def c1_of(ls):
            return jnp.concatenate([c1m_ref[:, ls], c1n_ref[:, ls]], axis=0)

        mean = sum(jnp.sum(c1_of(ls), axis=-1, keepdims=True) for ls in lanes) * (1.0 / C_WIDTH)
        var = sum(jnp.sum((c1_of(ls) - mean) ** 2, axis=-1, keepdims=True) for ls in lanes) * (1.0 / C_WIDTH)
        rs = lax.rsqrt(var + EPS)
        sum_dvh, sum_dvh_vhat = 0.0, 0.0
        for ls in lanes:
            vhat = (c1_of(ls) - mean) * rs
            c2 = vhat * lg_ref[:, ls] + lb_ref[:, ls]
            sig = _sigmoid(c2)
            dc = jnp.concatenate([dm_ref[:, ls], dn_ref[:, ls] * not_last], axis=0)
            dc2 = dc * (sig * (1.0 + c2 * (1.0 - sig)))
            dvh = dc2 * lg_ref[:, ls]
            sum_dvh = sum_dvh + jnp.sum(dvh, axis=-1, keepdims=True)
            sum_dvh_vhat = sum_dvh_vhat + jnp.sum(dvh * vhat, axis=-1, keepdims=True)
            dvh_scr[:, ls] = dvh
            vhat_scr[:, ls] = vhat
            dlg_ref[:, ls] += jnp.sum((dc2 * vhat)[:CD_TILE], axis=0, keepdims=True)
            dlb_ref[:, ls] += jnp.sum(dc2[:CD_TILE], axis=0, keepdims=True)
        for ls in lanes:
            dc1 = rs * (dvh_scr[:, ls] - sum_dvh * (1.0 / C_WIDTH) - vhat_scr[:, ls] * (sum_dvh_vhat * (1.0 / C_WIDTH)))
            dc1_scr[:, ls] = dc1
            dcb_ref[:, ls] += jnp.sum(dc1[:CD_TILE], axis=0, keepdims=True)
        _shifted_copies(dc1_scr, dc1_sh, ext)
        for ls in lanes:
            for r0 in range(0, CD_TILE, TAP_ROWS):
                rows = slice(r0, r0 + TAP_ROWS)
                dc1_m = dc1_scr[rows, ls]
                dc0 = jnp.zeros((TAP_ROWS, LANES), F32)
                for k in range(C_KERNEL):
                    dc0 = dc0 + cw_ref[k:k + 1, ls] * _rows_from(dc1_sh, r0 + C_KERNEL - 1 - k, TAP_ROWS, ls)
                    prod = dc1_m * _rows_from(c_sh, r0 + HALO - (C_KERNEL - 1) + k, TAP_ROWS, ls)
                    dcw_acc[k, :, ls] += prod.reshape(TAP_ROWS // 8, 8, LANES).sum(axis=0)
                g_m = m_ref[rows, C_WIDTH + ls.start:C_WIDTH + ls.stop].astype(F32)
                a_m = m_ref[rows, ls].astype(F32)
                sig_m = _sigmoid(g_m)
                dp_ref[rows, ls] = (dc0 * sig_m).astype(BF16)
                dp_ref[rows, C_WIDTH + ls.start:C_WIDTH + ls.stop] = (dc0 * a_m * sig_m * (1.0 - sig_m)).astype(BF16)

        @pl.when(i == nt - 1)
        def _():
            dcw_ref[...] = jnp.sum(dcw_acc[...], axis=1)

        for ls in lanes:
            wide = slice(C_WIDTH + ls.start, C_WIDTH + ls.stop)
            d1 = jnp.zeros((CD_TILE, LANES), F32)
            for k in range(D_KERNEL):
                d1 = d1 + dw_ref[k:k + 1, ls] * e_scr[pl.ds(HALO - (D_KERNEL - 1) + k, CD_TILE), ls]
            dd_m = dm_ref[:, wide]
            dd1 = jnp.concatenate([dd_m * col(m_ref, 2, ls), dn_ref[:, wide] * col(hn_ref, 2, ls) * not_last], axis=0)
            dd1_scr[:, ls] = dd1
            dp_ref[:, 2 * C_WIDTH + ls.start:2 * C_WIDTH + ls.stop] = (dd_m * d1).astype(BF16)
            de = jnp.zeros((CD_TILE, LANES), F32)
            for k in range(D_KERNEL):
                de = de + dw_ref[k:k + 1, ls] * dd1_scr[pl.ds(D_KERNEL - 1 - k, CD_TILE), ls]
                ddw_ref[k:k + 1, ls] += jnp.sum(dd1[:CD_TILE] * e_scr[pl.ds(HALO - (D_KERNEL - 1) + k, CD_TILE), ls], axis=0, keepdims=True)
            dp_ref[:, 3 * C_WIDTH + ls.start:3 * C_WIDTH + ls.stop] = (de * col(m_ref, 4, ls)).astype(BF16)
            dp_ref[:, 4 * C_WIDTH + ls.start:4 * C_WIDTH + ls.stop] = (de * col(m_ref, 3, ls)).astype(BF16)

    halo_prev = lambda i: (jnp.maximum(i * per - 1, 0), 0)
    halo_next = lambda i: (jnp.minimum((i + 1) * per, t // HALO - 1), 0)
    vec = _const_spec((1, C_WIDTH))
    return _call(
        body, grid=(nt,),
        in_specs=[pl.BlockSpec((HALO, CD_IN), halo_prev), _row_spec(CD_TILE, CD_IN), pl.BlockSpec((HALO, CD_IN), halo_next),
                  _row_spec(CD_TILE, 2 * C_WIDTH), pl.BlockSpec((HALO, 2 * C_WIDTH), halo_next),
                  _row_spec(CD_TILE, C_WIDTH), pl.BlockSpec((HALO, C_WIDTH), halo_next),
                  _const_spec((32, C_WIDTH)), vec, vec, _const_spec((8, C_WIDTH))],
        out_specs=[_row_spec(CD_TILE, CD_IN), _const_spec((32, C_WIDTH)), vec, vec, vec, _const_spec((8, C_WIDTH))],
        out_shape=[SDS((t, CD_IN), BF16), SDS((32, C_WIDTH), F32), SDS((1, C_WIDTH), F32), SDS((1, C_WIDTH), F32),
                   SDS((1, C_WIDTH), F32), SDS((8, C_WIDTH), F32)],
        scratch_shapes=[pltpu.VMEM((2 * HALO + CD_TILE, C_WIDTH), F32)] * 2 + [pltpu.VMEM((ext, C_WIDTH), F32)] * 2
        + [pltpu.VMEM((8, 2 * HALO + CD_TILE, C_WIDTH), F32), pltpu.VMEM((8, ext, C_WIDTH), F32),
           pltpu.VMEM((32, 8, C_WIDTH), F32)] + [pltpu.VMEM((ext, C_WIDTH), F32)] * 2,
        operands=[proj, proj, proj, dcat, dcat, c1, c1, cw, lg, lb, dw], name="mixer_cd_bwd", ride=ride)


def _wgrad(name, pairs, out_rc, t, ride, resident):
    tk = TILES["wgrad"]
    assert tk == t, "the whole contraction has to fit one grid step"
    r, c = out_rc
    n = len(pairs)

    operands, in_specs, where = [], [], []
    for lhs, lhs_spec, rhs, rhs_spec in pairs:
        at = []
        for array, spec in ((lhs, lhs_spec), (rhs, rhs_spec)):
            seen = [k for k, o in enumerate(operands) if o is array]
            if not seen:
                operands.append(array)
                in_specs.append(spec)
            at.append(seen[0] if seen else len(operands) - 1)
        where.append(at)
    n_in = len(operands)
    kept = [k for k, o in enumerate(operands) if o is resident][0]
    in_specs[kept] = ANY
    half = t // 2

    def body(*refs):
        ins, out_refs, (kept_buf, sems) = refs[:n_in], refs[n_in:n_in + n], refs[n_in + n:]
        first = (pl.program_id(0) == 0) & (pl.program_id(1) == 0)

        def fetch(k):
            rows = pl.ds(k * half, half)
            return pltpu.make_async_copy(ins[kept].at[rows, :], kept_buf.at[rows, :], sems.at[k])

        @pl.when(first)
        def _():
            fetch(0).start()
            fetch(1).start()

        sums = [None] * n
        for k in range(2):
            @pl.when(first)
            def _():
                fetch(k).wait()

            rows = slice(k * half, (k + 1) * half)
            for j, (lhs_at, rhs_at) in enumerate(where):
                lhs, rhs = ((kept_buf if at == kept else ins[at])[rows, :] for at in (lhs_at, rhs_at))
                d = _dot(lhs, rhs, TN)
                sums[j] = d if sums[j] is None else sums[j] + d
        for j in range(n):
            out_refs[j][...] = sums[j].astype(BF16)

    res = _call(body, grid=(N_CHIPS, t // tk), in_specs=in_specs,
                out_specs=[pl.BlockSpec((None, r, c), lambda p, k: (p, 0, 0))] * n,
                out_shape=[SDS((N_CHIPS, r, c), BF16)] * n, operands=operands, name=name,
                scratch_shapes=[pltpu.VMEM(resident.shape, resident.dtype), pltpu.SemaphoreType.DMA((2,))], ride=ride)
    outs, ride_res = (res, None) if ride is None else res
    outs = [o.reshape(N_CHIPS, 2, r // 2, c) for o in outs]
    return outs if ride is None else (outs, ride_res)


def _wgrad_col_sharded(name, h, dz_list, three_d, ride=None):
    t, d = h.shape
    tk = TILES["wgrad"]
    n4 = dz_list[0].shape[-1] if three_d else dz_list[0].shape[-1] // N_CHIPS
    hs = pl.BlockSpec((tk, d), lambda p, k: (k, 0))
    zs = pl.BlockSpec((None, tk, n4), lambda p, k: (p, k, 0)) if three_d else pl.BlockSpec((tk, n4), lambda p, k: (k, p))
    return _wgrad(name, [(h, hs, dz, zs) for dz in dz_list], (d, n4), t, ride, h)


def _wgrad_row_sharded(name, a, g, three_d, ride=None):
    many = isinstance(a, (list, tuple))
    a_list = list(a) if many else [a]
    t, d = g.shape
    tk = TILES["wgrad"]
    k4 = a_list[0].shape[-1] if three_d else a_list[0].shape[-1] // N_CHIPS
    a_spec = pl.BlockSpec((None, tk, k4), lambda p, k: (p, k, 0)) if three_d else pl.BlockSpec((tk, k4), lambda p, k: (k, p))
    gs = pl.BlockSpec((tk, d), lambda p, k: (k, 0))
    res = _wgrad(name, [(a_j, a_spec, g, gs) for a_j in a_list], (k4, d), t, ride, g)
    if many:
        return res
    return res[0] if ride is None else (res[0][0], res[1])


def _mesh_scalars():
    return jnp.stack([lax.axis_index("c"), 2 * lax.axis_index("x") + lax.axis_index("y")]).astype(jnp.int32)


def _stage_own(name, w, layer, dtype, far_slab=False):
    layers, r, cols = w.shape
    h = r // 2

    def body(s_ref, x_ref, o_ref, *unwritten):
        o_ref[...] = x_ref[...].astype(dtype)

    out_specs = [pl.BlockSpec((None, None, h, cols), lambda i, s: (s[1], i, 0, 0))] + [ANY] * far_slab
    out_shape = [SDS((N_CHIPS, 2, h, cols), dtype)] + [SDS((2, h, cols), dtype)] * far_slab
    res = pl.pallas_call(
        body,
        grid_spec=pltpu.PrefetchScalarGridSpec(
            num_scalar_prefetch=1, grid=(2,),
            in_specs=[pl.BlockSpec((None, h, cols), lambda i, s: (2 * layer + i, 0, 0))], out_specs=out_specs),
        out_shape=out_shape, name=name,
        compiler_params=_params())(_mesh_scalars(), w.reshape(2 * layers, h, cols))
    return tuple(res) if far_slab else res[0]


STAGE_STEPS = 4


def _stage_rest_and_norm(x, g, weights, ride=None):
    t, d = x.shape
    n = len(weights)
    views, in_specs, out_specs, out_shapes = [], [], [], []
    for w, layer in weights:
        layers, r, cols = w.shape
        sub = r // STAGE_STEPS
        views.append(w.reshape(layers * STAGE_STEPS, sub, cols))
        in_specs.append(pl.BlockSpec((None, sub, cols), functools.partial(lambda l, i, s: (STAGE_STEPS * l + i, 0, 0), layer)))
        out_specs.append(pl.BlockSpec((None, None, sub, cols), lambda i, s: (s[1], i // 2, i % 2, 0)))
        out_shapes.append(SDS((N_CHIPS, 2, r // 2, cols), BF16))

    def body(s_ref, x_ref, g_ref, *rest):
        w_refs, h_ref, o_refs = rest[:n], rest[n], rest[n + 1:]
        h_ref[...] = _rms_rows(x_ref[...], g_ref[...]).astype(BF16)
        for w_ref, o_ref in zip(w_refs, o_refs):
            o_ref[...] = w_ref[...].astype(BF16)

    tm = t // STAGE_STEPS
    res = _call(
        body, grid=(STAGE_STEPS,), in_specs=[pl.BlockSpec((tm, d), lambda i, s: (i, 0)), pl.BlockSpec((1, d), lambda i, s: (0, 0))] + in_specs,
        out_specs=[pl.BlockSpec((tm, d), lambda i, s: (i, 0))] + out_specs, out_shape=[SDS((t, d), BF16)] + out_shapes,
        operands=[x, g] + views, name="stage_and_norm", ride=ride, prefetch=_mesh_scalars())
    outs, ride_res = (res, None) if ride is None else res
    result = (outs[0], list(outs[1:]))
    return result if ride is None else (result, ride_res)


def _remote(src, dst, send_sem, recv_sem, device):
    return pltpu.make_async_remote_copy(src, dst, send_sem, recv_sem, device_id=device, device_id_type=MESH)


ALL_PEERS = (0, 1, 2)
NEIGHBOURS = (0, 1)


def _ride_gather_send(bufs, peers=ALL_PEERS):
    n = len(bufs)

    def each(b, sems, act):
        send, recv = sems
        x, y, c, p, others = _position()
        for t in range(n):
            for j in peers:
                qx, qy = others[j]
                act(b[t].at[p, c], b[t].at[2 * qx + qy, c], send.at[t, j], recv.at[t, j], (qx, qy, c))

    def start(ins, b, new, sems):
        each(b, sems, lambda mine, landed, s, r, dev: _remote(mine, mine, s, r, dev).start())

    def finish(ins, b, new, sems):
        def act(mine, landed, s, r, dev):
            _remote(mine, mine, s, r, dev).wait_send()
            _remote(landed, landed, s, r, dev).wait_recv()
        each(b, sems, act)

    return _Ride([], bufs, [], [(n, 3), (n, 3)], start, finish, ["chips"])


def _ride_gather_pass(bufs, peers=ALL_PEERS):
    n = len(bufs)

    def each(b, sems, act):
        send, recv = sems
        x, y, c, p, others = _position()
        for t in range(n):
            for j in peers:
                qx, qy = others[j]
                act(b[t].at[2 * qx + qy, c], b[t].at[2 * qx + qy, 1 - c], send.at[t, j], recv.at[t, j], (x, y, 1 - c))

    def start(ins, b, new, sems):
        each(b, sems, lambda landed, passed, s, r, dev: _remote(landed, landed, s, r, dev).start())

    def finish(ins, b, new, sems):
        def act(landed, passed, s, r, dev):
            _remote(landed, landed, s, r, dev).wait_send()
            _remote(passed, passed, s, r, dev).wait_recv()
        each(b, sems, act)

    return _Ride([], bufs, [], [(n, 3), (n, 3)], start, finish, ["sibling"])


def _ride_gather(bufs, peers=ALL_PEERS):
    send, onward = _ride_gather_send(bufs, peers), _ride_gather_pass(bufs, peers)
    n_send = len(send.sem_shapes)

    def start(ins, b, new, sems):
        send.start(ins, b, new, sems[:n_send])

    def finish(ins, b, new, sems):
        send.finish(ins, b, new, sems[:n_send])
        onward.start(ins, b, new, sems[n_send:])
        onward.finish(ins, b, new, sems[n_send:])

    return _Ride([], bufs, [], send.sem_shapes + onward.sem_shapes, start, finish, ["sibling", "chips"])


def _ride_gather_far(sources, slabs):
    n = len(slabs)

    def hops(ins, b, sems):
        x, y, c, p, others = _position()
        qx, qy = others[2]
        for t in range(n):
            far = (ins[t].at[p, c], b[t].at[c], sems[0].at[t], sems[1].at[t], (qx, qy, c))
            onward = (b[t].at[c], b[t].at[1 - c], sems[2].at[t], sems[3].at[t], (x, y, 1 - c))
            yield far, onward

    def wait(mine, landed, s, r, dev):
        _remote(mine, mine, s, r, dev).wait_send()
        _remote(landed, landed, s, r, dev).wait_recv()

    def start(ins, b, new, sems):
        for (mine, landed, s, r, dev), _ in hops(ins, b, sems):
            _remote(mine, landed, s, r, dev).start()

    def finish(ins, b, new, sems):
        for far, _ in hops(ins, b, sems):
            wait(*far)
        for _, (landed, passed, s, r, dev) in hops(ins, b, sems):
            _remote(landed, landed, s, r, dev).start()
        for _, onward in hops(ins, b, sems):
            wait(*onward)

    return _Ride(sources, slabs, [], [(n,)] * 4, start, finish, ["sibling", "chips"])


def _ride_swap(tensors):
    n = len(tensors)

    def each(ins, new, sems, act):
        send, recv = sems
        x, y, c, _, _ = _position()
        for t in range(n):
            act(_remote(ins[t].at[:, 1 - c], new[t], send.at[t], recv.at[t], (x, y, 1 - c)))

    def start(ins, b, new, sems):
        each(ins, new, sems, lambda cp: cp.start())

    def finish(ins, b, new, sems):
        each(ins, new, sems, lambda cp: cp.wait())

    return _Ride(tensors, [], [SDS((s.shape[0],) + s.shape[2:], s.dtype) for s in tensors], [(n,), (n,)], start, finish,
                 ["sibling"])


def _ride_scatter(tensors, landing):
    n = len(tensors)

    def each(ins, b, sems, act):
        send, recv = sems
        x, y, c, p, others = _position()
        for t in range(n):
            for j, (qx, qy) in enumerate(others):
                q = 2 * qx + qy
                act(ins[t].at[q], b[t].at[p], b[t].at[q], send.at[t, j], recv.at[t, j], (qx, qy, c))

    def start(ins, b, new, sems):
        each(ins, b, sems, lambda src, dst, landed, s, r, dev: _remote(src, dst, s, r, dev).start())

    def finish(ins, b, new, sems):
        def act(src, dst, landed, s, r, dev):
            _remote(src, dst, s, r, dev).wait_send()
            _remote(landed, landed, s, r, dev).wait_recv()
        each(ins, b, sems, act)

    return _Ride(tensors, landing, [], [(n, 3), (n, 3)], start, finish, ["chips"])


def _ride_join(bufs):
    n = len(bufs)

    def each(b, sems, act):
        send, recv = sems
        x, y, c, _, _ = _position()
        for t in range(n):
            act(b[t].at[c], b[t].at[1 - c], send.at[t], recv.at[t], (x, y, 1 - c))

    def start(ins, b, new, sems):
        each(b, sems, lambda mine, theirs, s, r, dev: _remote(mine, mine, s, r, dev).start())

    def finish(ins, b, new, sems):
        def act(mine, theirs, s, r, dev):
            _remote(mine, mine, s, r, dev).wait_send()
            _remote(theirs, theirs, s, r, dev).wait_recv()
        each(b, sems, act)

    return _Ride([], bufs, [], [(n,), (n,)], start, finish, ["sibling"])


def _all_reduce_small(pack, ride=None):
    rows = pack.shape[0]
    n_dev = 2 * N_CHIPS
    n_rb = 0 if ride is None else len(ride.bufs)

    def body(x_ref, *rest):
        o_ref = rest[n_rb]
        r_bufs = rest[n_rb + 1:2 * n_rb + 1]
        land, send, recv = rest[2 * n_rb + 1:2 * n_rb + 4]
        r_sems = rest[2 * n_rb + 4:]
        if ride is not None:
            ride.start([], r_bufs, [], r_sems)
        x, y, c, p, _ = _position()
        me = 2 * p + c
        land[me] = x_ref[...]
        peers = [(dx, dy, dc) for dx in range(2) for dy in range(2) for dc in range(2) if (dx, dy, dc) != (0, 0, 0)]
        for j, (dx, dy, dc) in enumerate(peers):
            _remote(land.at[me], land.at[me], send.at[j], recv.at[j], (x ^ dx, y ^ dy, c ^ dc)).start()
        for j, (dx, dy, dc) in enumerate(peers):
            src = 4 * (x ^ dx) + 2 * (y ^ dy) + (c ^ dc)
            _remote(land.at[me], land.at[me], send.at[j], recv.at[j], (x ^ dx, y ^ dy, c ^ dc)).wait_send()
            _remote(land.at[src], land.at[src], send.at[j], recv.at[j], (x ^ dx, y ^ dy, c ^ dc)).wait_recv()
        acc = land[0]
        for dev in range(1, n_dev):
            acc = acc + land[dev]
        o_ref[...] = acc
        if ride is not None:
            ride.finish([], r_bufs, [], r_sems)

    bufs = [] if ride is None else ride.bufs
    sems = [] if ride is None else [pltpu.SemaphoreType.DMA(s) for s in ride.sem_shapes]
    res = pl.pallas_call(
        body, in_specs=[pl.BlockSpec(memory_space=pltpu.VMEM)] + [ANY] * n_rb,
        out_specs=[pl.BlockSpec(memory_space=pltpu.VMEM)] + [ANY] * n_rb,
        out_shape=[SDS((rows, LANES), F32)] + [SDS(b.shape, b.dtype) for b in bufs],
        scratch_shapes=[pltpu.VMEM((n_dev, rows, LANES), F32), pltpu.SemaphoreType.DMA((n_dev - 1,)),
                        pltpu.SemaphoreType.DMA((n_dev - 1,))] + sems,
        input_output_aliases={1 + j: 1 + j for j in range(n_rb)},
        name="all_reduce_small", compiler_params=_params())(pack, *bufs)
    return res[0], list(res[1:])


def _add_own_half(name, fulls, recvs, out_dtypes):
    n = len(fulls)
    in_specs, out_specs, out_shapes = [], [], []
    for full, dtype in zip(fulls, out_dtypes):
        n4, _, h, cols = full.shape
        in_specs += [pl.BlockSpec((None, None, h, cols), lambda q, s: (q, s[0], 0, 0)),
                     pl.BlockSpec((None, h, cols), lambda q, s: (q, 0, 0))]
        out_specs += [pl.BlockSpec((None, h, cols), lambda q, s: (q, 0, 0)),
                      pl.BlockSpec((None, h, cols), lambda q, s: (s[1], 0, 0))]
        out_shapes += [SDS((n4, h, cols), dtype)] * 2

    def body(s_ref, *refs):
        ins, outs = refs[:2 * n], refs[2 * n:]
        for j in range(n):
            o_ref, own_ref = outs[2 * j], outs[2 * j + 1]
            v = (ins[2 * j][...].astype(F32) + ins[2 * j + 1][...].astype(F32)).astype(o_ref.dtype)
            o_ref[...] = v

            @pl.when(pl.program_id(0) == s_ref[1])
            def _():
                own_ref[...] = v

    res = pl.pallas_call(
        body,
        grid_spec=pltpu.PrefetchScalarGridSpec(num_scalar_prefetch=1, grid=(N_CHIPS,), in_specs=in_specs, out_specs=out_specs),
        out_shape=out_shapes, name=name, compiler_params=_params())(
            _mesh_scalars(), *[a for pair in zip(fulls, recvs) for a in pair])
    return [(res[2 * j], res[2 * j + 1]) for j in range(n)]


def _sum_chips(name, parts_list):
    steps = 4 if all(parts.shape[1] % 64 == 0 for parts in parts_list) else 1
    in_specs, out_specs, out_shapes = [], [], []
    for parts in parts_list:
        n4, h, cols = parts.shape
        in_specs.append(pl.BlockSpec((n4, h // steps, cols), lambda i, s: (0, i, 0)))
        out_specs.append(pl.BlockSpec((None, h // steps, cols), lambda i, s: (s[0], i, 0)))
        out_shapes.append(SDS((2, h, cols), F32))
    n = len(parts_list)

    def body(s_ref, *refs):
        for a_ref, o_ref in zip(refs[:n], refs[n:]):
            acc = a_ref[0].astype(F32)
            for q in range(1, N_CHIPS):
                acc = acc + a_ref[q].astype(F32)
            o_ref[...] = acc

    return pl.pallas_call(
        body,
        grid_spec=pltpu.PrefetchScalarGridSpec(num_scalar_prefetch=1, grid=(steps,), in_specs=in_specs, out_specs=out_specs),
        out_shape=out_shapes, name=name, compiler_params=_params())(_mesh_scalars(), *parts_list)


def _adamw_math(w, g, m, v):
    m2 = ADAM_B1 * m + (1.0 - ADAM_B1) * g
    v2 = ADAM_B2 * v + (1.0 - ADAM_B2) * (g * g)
    m_hat = m2 / (1.0 - ADAM_B1 ** ADAM_STEP)
    v_hat = v2 / (1.0 - ADAM_B2 ** ADAM_STEP)
    delta = -ADAM_LR * (m_hat / (jnp.sqrt(v_hat) + ADAM_EPS) + ADAM_WD * w)
    return delta, m2, v2


ADAMW_STEPS = 8


def _adamw_big(name, ws, g_layers_list, ms, vs):
    layers = ws[0].shape[0]
    n, per_in = len(ws), 3 + layers
    in_specs, out_specs, out_shapes, operands = [], [], [], []
    for w, g_layers, m, v in zip(ws, g_layers_list, ms, vs):
        assert w.shape[0] == layers and w.shape[1] % (8 * ADAMW_STEPS) == 0
        _, rows, cols = w.shape
        tr = rows // ADAMW_STEPS
        blk = pl.BlockSpec((None, tr, cols), lambda l, i: (l, i, 0))
        in_specs += [blk] * 3 + [pl.BlockSpec((tr, cols), lambda l, i: (i, 0))] * layers
        out_specs += [blk] * 4
        out_shapes += [SDS((layers, rows, cols), F32)] * 4
        operands += [w, m, v] + [g.reshape(rows, cols) for g in g_layers]

    def body(*refs):
        ins, outs = refs[:n * per_in], refs[n * per_in:]
        for j in range(n):
            w_ref, m_ref, v_ref = ins[j * per_in:j * per_in + 3]
            g_refs = ins[j * per_in + 3:(j + 1) * per_in]
            g_o, d_o, m_o, v_o = outs[4 * j:4 * j + 4]
            gv = g_refs[0][...]
            for layer in range(1, layers):
                gv = jnp.where(pl.program_id(0) == layer, g_refs[layer][...], gv)
            d, mm, vv = _adamw_math(w_ref[...], gv, m_ref[...], v_ref[...])
            g_o[...] = gv
            d_o[...] = d
            m_o[...] = mm
            v_o[...] = vv

    res = pl.pallas_call(
        body, grid=(layers, ADAMW_STEPS), in_specs=in_specs, out_specs=out_specs, out_shape=out_shapes, name=name,
        compiler_params=_params())(*operands)
    return [tuple(res[4 * j:4 * j + 4]) for j in range(n)]


def _adamw_small(ws, gs, ms, vs):
    n = len(ws)
    flat = []
    for group in (ws, gs, ms, vs):
        flat += [a.reshape(-1, a.shape[-1]) for a in group]

    def body(*refs):
        w_r, g_r, m_r, v_r = refs[:n], refs[n:2 * n], refs[2 * n:3 * n], refs[3 * n:4 * n]
        d_o, m_o, v_o = refs[4 * n:5 * n], refs[5 * n:6 * n], refs[6 * n:7 * n]
        for j in range(n):
            d, mm, vv = _adamw_math(w_r[j][...], g_r[j][...], m_r[j][...], v_r[j][...])
            d_o[j][...] = d
            m_o[j][...] = mm
            v_o[j][...] = vv

    shapes = [SDS(a.shape, F32) for a in flat[:n]]
    outs = pl.pallas_call(body, out_shape=shapes * 3, name="adamw_small", compiler_params=_params())(*flat)
    res = []
    for k in range(3):
        res.append([outs[k * n + j].reshape(ws[j].shape) for j in range(n)])
    return res


BIG = ("ab_w_in", "ab_w_out", "cd_w_in", "cd_w_out", "ffn_w_gate", "ffn_w_up", "ffn_w_down")
BIG_BY_LAYERS = (BIG[:4], BIG[4:])
V_BLOCK = (2 * A_WIDTH + QK_COLS) // B_WIDTH


def _pad_rows(a, rows):
    return jnp.pad(a, ((0, rows - a.shape[0]), (0, 0)))


A_IN, A_OUT, C_IN, C_OUT = ("ab_w_in", 0), ("ab_w_out", 0), ("cd_w_in", 0), ("cd_w_out", 0)
G0, U0, D0 = ("ffn_w_gate", 0), ("ffn_w_up", 0), ("ffn_w_down", 0)
G1, U1, D1 = ("ffn_w_gate", 1), ("ffn_w_up", 1), ("ffn_w_down", 1)
UNITS = (A_IN, A_OUT, G0, U0, D0, C_IN, C_OUT, G1, U1, D1)
ROWS_MINOR = ("ffn_w_gate", "ffn_w_up")
SMALL_SHARDED = ("small", 0)
REPLICATED_UNIT = ("replicated", 0)


class _Exchange:
    def __init__(self, enabled):
        self.enabled = enabled
        self.w, self.grad, self.recv, self.half, self.land, self.done = {}, {}, {}, {}, {}, {}
        self.far = {}

    def full(self, unit):
        b = self.w[unit]
        return b.reshape(N_CHIPS, 1, 2 * b.shape[2], b.shape[3])

    def ride_for(self, phases):
        rides, sinks = [], []
        for kind, units in phases:
            if kind == "send":
                rides.append(_ride_gather_send([self.w[u] for u in units]))
                sinks.append(self.w)
            elif kind == "pass":
                rides.append(_ride_gather_pass([self.w[u] for u in units]))
                sinks.append(self.w)
            elif kind == "gather":
                rides.append(_ride_gather([self.w[u] for u in units]))
                sinks.append(self.w)
            elif kind == "gather_near":
                rides.append(_ride_gather([self.w[u] for u in units], NEIGHBOURS))
                sinks.append(self.w)
            elif kind == "gather_far":
                rides.append(_ride_gather_far([self.w[u] for u in units], [self.far[u] for u in units]))
                sinks.append(self.far)
            elif kind == "swap":
                rides.append(_ride_swap([self.grad[u] for u in units]))
                sinks.append(self.recv)
            elif kind == "scatter":
                rides.append(_ride_scatter([self.half[u] for u in units], [self.land[u] for u in units]))
                sinks.append(self.land)
            else:
                rides.append(_ride_join([self.done[u] for u in units]))
                sinks.append(self.done)
        ride = functools.reduce(_ride_both, rides)

        def settle(res):
            n_bufs = sum(len(r.bufs) for r in rides)
            bufs, new = list(res[:n_bufs]), list(res[n_bufs:])
            for r, sink, (_, units) in zip(rides, sinks, phases):
                vals = [bufs.pop(0) for _ in r.bufs] + [new.pop(0) for _ in r.new_outs]
                for u, v in zip(units, vals):
                    sink[u] = v

        return ride, settle

    def run(self, fn, *args, phases=(), **kw):
        if not self.enabled or not phases:
            return fn(*args, **kw)
        ride, settle = self.ride_for(phases)
        out, res = fn(*args, ride=ride, **kw)
        settle(res)
        return out

    def alone(self, name, phases):
        if self.enabled:
            ride, settle = self.ride_for(phases)
            settle(_run_ride(name, ride))

    def pair_sum(self, units):
        if self.enabled:
            dtypes = [F32 if u in (SMALL_SHARDED, REPLICATED_UNIT) else BF16 for u in units]
            res = _add_own_half(f"pair_sum_{units[0][0]}_{units[0][1]}", [self.grad[u] for u in units],
                                [self.recv[u] for u in units], dtypes)
            for u, (half, land) in zip(units, res):
                self.half[u], self.land[u] = half, land

    def chip_sum(self, units):
        if self.enabled:
            res = _sum_chips(f"chip_sum_{units[0][0]}_{units[0][1]}", [self.land[u] for u in units])
            self.done.update(zip(units, res))


def _local_step(x, target, ex, sp, h0=None):
    t, d = x.shape
    tabs = _rope_tables(t)
    gains = jnp.concatenate([jnp.tile(sp["q_norm_g"][g], HEAD_DIM // 8) for g in range(N_DIL)]
                            + [jnp.tile(sp["k_norm_g"][g], HEAD_DIM // 8) for g in range(N_DIL)]).reshape(1, QK_COLS)
    bias_t = sp["sgu_bias"].T
    cw = _pad_rows(sp["conv_c_w"], 32)
    dw = _pad_rows(sp["conv_d_w"], 8)
    cb, clg, clb = (sp[k].reshape(1, C_WIDTH) for k in ("conv_c_b", "c_ln_g", "c_ln_b"))
    slg, slb = sp["sgu_norm_g"].reshape(1, A_WIDTH), sp["sgu_norm_b"].reshape(1, A_WIDTH)
    g_ab, g_cd = sp["ab_norm_g"].reshape(1, d), sp["cd_norm_g"].reshape(1, d)
    g_f0, g_f1 = sp["ffn_norm_g"][0:1], sp["ffn_norm_g"][1:2]
    run = ex.run

    def w2d(unit):
        return ex.full(unit).reshape(-1, d)

    if h0 is None:
        h0 = _rms_fwd("rms_ab", x, g_ab)
    if ex.enabled:
        proj = run(_proj_in_near, "proj_ab_near", h0, ex.full(A_IN), phases=[("gather_far", [A_IN]), ("send", [A_OUT])])
        proj, ex.w[A_IN] = _proj_in_far("proj_ab_far", h0, ex.far[A_IN], proj, ex.w[A_IN])
    else:
        proj = _proj_in("proj_ab", h0, ex.full(A_IN), 0)
    a_out = _mixer_a_fwd(proj, slg, slb, sp["sgu_w"], bias_t)
    qk, q1, q2, k1, k2 = run(_qk_fwd, proj, gains, tabs, phases=[("pass", [A_OUT]), ("send", [G0, C_OUT])])
    regrouped_qk = {1: (q1, k1), 2: (q2, k2)}
    fwd_phases = ([("pass", [G0, C_OUT]), ("send", [U0])], [("pass", [U0]), ("send", [D0])],
                  [("pass", [D0]), ("send", [C_IN])])
    qkv, o_list, l_list = [], [], []
    dilated = [g for g, rate in enumerate(DIL_RATES) if rate != 1]
    regrouped_v = dict(zip(dilated, _permute("regroup_v", [(proj, V_BLOCK + g, DIL_RATES[g]) for g in dilated])))
    for g, rate in enumerate(DIL_RATES):
        if rate == 1:
            qk3, proj3 = qk.reshape(1, t, QK_COLS), proj.reshape(1, t, AB_IN)
            q, k, v = (qk3, g), (qk3, N_DIL + g), (proj3, V_BLOCK + g)
        else:
            q, k, v = (regrouped_qk[g][0], 0), (regrouped_qk[g][1], 0), (regrouped_v[g], 0)
        qkv.append((q, k, v))
        o, l = run(_attn_fwd, f"attn_fwd_{g}", q, k, v, phases=fwd_phases[g])
        if rate == 1:
            o, l = o.reshape(t, B_WIDTH), l.reshape(t, B_WIDTH)
        o_list.append(o)
        l_list.append(l)
    cat, lse_tot, lse_1, lse_2 = _attn_merge(a_out, o_list, l_list)
    x1, hf0 = _proj_out("out_ab", cat, w2d(A_OUT), x, g_next=g_f0)
    fgate0, fup0, act0 = run(_ffn_in, "ffn_in_0", hf0, ex.full(G0), ex.full(U0), 0,
                           phases=[("pass", [C_IN]), ("send", [D1, G1])])
    x2, h1 = run(_ffn_out, "ffn_out_0", act0, ex.full(D0), 0, x1, g_next=g_cd, phases=[("pass", [D1, G1]), ("send", [U1])])
    projcd = run(_proj_in, "proj_cd", h1, ex.full(C_IN), 0, phases=[("pass", [U1])])
    cat2, c1 = _mixer_cd_fwd(projcd, cw, cb, clg, clb, dw)
    x3, hf1 = _proj_out("out_cd", cat2, w2d(C_OUT), x2, g_next=g_f1)
    fgate1, fup1, act1 = _ffn_in("ffn_in_1", hf1, ex.full(G1), ex.full(U1), 0)
    dy, loss_acc, dy_b = _ffn_out("ffn_out_1", act1, ex.full(D1), 0, x3, target=target)
    loss = 0.5 * loss_acc[0, 0] / d

    late = [D1, G1, U1]
    dgate, dup = _ffn_dact("ffn_dact_1", dy_b, ex.full(D1), 0, fgate1, fup1)
    ex.grad[D1] = _wgrad_row_sharded("wgrad_down_1", act1, dy_b, True)
    ex.grad[G1], ex.grad[U1] = _wgrad_row_sharded("wgrad_gate_up_1", [dgate, dup], hf1, True)
    g3, d_f1, g3_b = run(_dgrad_cols, "dgrad_ffn_1", [dgate, dup], [ex.full(G1), ex.full(U1)], 0, True, x3, g_f1, dy,
                         w_rows=True, phases=[("swap", late)])
    ex.pair_sum(late)

    dcat2 = _dgrad_rows("dgrad_out_cd", g3_b, w2d(C_OUT))
    ex.grad[C_OUT] = _wgrad_row_sharded("wgrad_out_cd", cat2, g3_b, False)
    dprojcd, d_cw, d_cb, d_clg, d_clb, d_dw = run(_mixer_cd_bwd, projcd, dcat2, c1, cw, clg, clb, dw, phases=[("scatter", late)])
    ex.grad[C_IN] = _wgrad_col_sharded("wgrad_in_cd", h1, [dprojcd], False)[0]
    g2, d_cdn, g2_b = run(_dgrad_cols, "dgrad_in_cd", [dprojcd], [ex.full(C_IN)], 0, False, x2, g_cd, g3,
                          phases=[("swap", [C_OUT, C_IN])])
    ex.pair_sum([C_OUT, C_IN])

    dgate, dup = run(_ffn_dact, "ffn_dact_0", g2_b, ex.full(D0), 0, fgate0, fup0, phases=[("scatter", [C_OUT, C_IN])])
    ex.chip_sum(late + [C_OUT, C_IN])
    ex.grad[D0] = _wgrad_row_sharded("wgrad_down_0", act0, g2_b, True)
    ex.grad[G0], ex.grad[U0] = _wgrad_row_sharded("wgrad_gate_up_0", [dgate, dup], hf0, True)
    small = {"cd_norm_g": d_cdn, "conv_c_w": d_cw[:C_KERNEL], "conv_c_b": d_cb, "c_ln_g": d_clg, "c_ln_b": d_clb,
             "conv_d_w": d_dw[:D_KERNEL]}
    ex.grad[SMALL_SHARDED] = _split_full_small(small).reshape(N_CHIPS, 2, SHARDED_ROWS // 2, LANES)
    mid = [D0, G0, U0, SMALL_SHARDED]
    g1, d_f0, g1_b = run(_dgrad_cols, "dgrad_ffn_0", [dgate, dup], [ex.full(G0), ex.full(U0)], 0, True, x1, g_f0, g2,
                         w_rows=True, phases=[("join", late + [C_OUT, C_IN]), ("swap", mid)])
    ex.pair_sum(mid)

    dcat = _dgrad_rows("dgrad_out_ab", g1_b, w2d(A_OUT))
    ex.grad[A_OUT] = _wgrad_row_sharded("wgrad_out_ab", cat, g1_b, False)
    d_a, d_sw, d_sbt, d_slg, d_slb = _mixer_a_bwd(proj, dcat, slg, slb, sp["sgu_w"], bias_t)
    early = {"sgu_norm_g": d_slg, "sgu_norm_b": d_slb, "sgu_w": d_sw, "sgu_bias": d_sbt.T}
    ex.grad[REPLICATED_UNIT] = jnp.broadcast_to(
        _pack_replicated(early, REPLICATED_EARLY, REPLICATED_EARLY_ROWS).reshape(2, REPLICATED_EARLY_ROWS // 2, LANES),
        (N_CHIPS, 2, REPLICATED_EARLY_ROWS // 2, LANES))
    last = [A_OUT, REPLICATED_UNIT]
    dbb, dd, db_1, dd_1, db_2, dd_2 = _attn_bwd_prep(dcat, cat)
    regrouped_bwd = {1: (db_1, lse_1, dd_1), 2: (db_2, lse_2, dd_2)}
    bwd_phases = ([("scatter", [D0, SMALL_SHARDED])],
                  [("scatter", [G0]), ("swap", last)],
                  [("scatter", [U0])])
    dqs, dks, dvs = [], [], []
    for g, rate in enumerate(DIL_RATES):
        q, k, v = qkv[g]
        if rate == 1:
            db3, l3, dd3 = (a.reshape(1, t, B_WIDTH) for a in (dbb, lse_tot, dd))
        else:
            db3, l3, dd3 = regrouped_bwd[g]
        if g == 2:
            ex.pair_sum(last)
        dq, dk, dv = run(_attn_bwd, f"attn_bwd_{g}", q, k, v, db3, l3, dd3, phases=bwd_phases[g])
        if rate == 1:
            dq, dk, dv = (a.reshape(t, B_WIDTH) for a in (dq, dk, dv))
        dqs.append(dq)
        dks.append(dk)
        dvs.append(dv)
    ex.chip_sum([D0, SMALL_SHARDED, G0, U0])
    dproj, d_gains = run(_dproj_assemble, proj, d_a, dqs, dks, dvs, gains, tabs, phases=[("scatter", last), ("join", [D0, SMALL_SHARDED, G0, U0])])
    ex.chip_sum(last)
    d_gains = _fold_heads(d_gains)[0].reshape(2, N_DIL, B_WIDTH)[:, :, :HEAD_DIM]
    ex.grad[A_IN] = _wgrad_col_sharded("wgrad_in_ab", h0, [dproj], False)[0]
    ex.alone("swap_last", [("swap", [A_IN])])
    ex.pair_sum([A_IN])
    gx, d_abn = run(_dgrad_cols, "dgrad_in_ab", [dproj], [ex.full(A_IN)], 0, False, x, g_ab, g1, bf16_copy=False,
                    phases=[("join", last), ("scatter", [A_IN])])
    ex.chip_sum([A_IN])

    small.update({
        "ab_norm_g": d_abn, "sgu_norm_g": d_slg, "sgu_norm_b": d_slb, "sgu_w": d_sw, "sgu_bias": d_sbt.T,
        "q_norm_g": d_gains[0], "k_norm_g": d_gains[1], "ffn_norm_g": jnp.concatenate([d_f0, d_f1], axis=0),
    })
    return loss, gx, small


SHARDED_SMALL = ("cd_norm_g", "conv_c_w", "conv_c_b", "c_ln_g", "c_ln_b", "conv_d_w")
SHARDED_ROWS = 48
REPLICATED_EARLY = ("sgu_norm_g", "sgu_norm_b", "sgu_w", "sgu_bias")
REPLICATED_EARLY_ROWS = 528
REPLICATED_LATE = ("ab_norm_g", "q_norm_g", "k_norm_g", "ffn_norm_g", "loss")
REPLICATED_LATE_ROWS = 32
REPLICATED_SMALL = REPLICATED_EARLY + REPLICATED_LATE[:-1]


def _pack_sharded(parts):
    rows = [parts[k].reshape(-1, LANES) for k in SHARDED_SMALL]
    return _pad_rows(jnp.concatenate(rows, axis=0), SHARDED_ROWS)


def _split_full_small(small):
    per_chip = []
    for q in range(N_CHIPS):
        parts = {}
        for k in SHARDED_SMALL:
            a = small[k]
            a = a.reshape(-1, a.shape[-1])
            n = a.shape[-1] // N_CHIPS
            parts[k] = a[:, q * n:(q + 1) * n]
        per_chip.append(_pack_sharded(parts))
    return jnp.stack(per_chip)


def _unpack_sharded(pack, shapes):
    out, r = {}, 0
    for k in SHARDED_SMALL:
        n = math.prod(shapes[k]) // LANES
        out[k] = pack[r:r + n].reshape(shapes[k])
        r += n
    return out


def _gathered_small(packs, shapes):
    per_chip = [_unpack_sharded(packs[q], shapes) for q in range(N_CHIPS)]
    return {k: jnp.concatenate([pc[k] for pc in per_chip], axis=-1) for k in SHARDED_SMALL}


def _pack_replicated(small, names, total_rows):
    rows = []
    for k in names:
        a = small[k].reshape(-1)
        a = jnp.pad(a, (0, (-a.shape[0]) % LANES))
        rows.append(a.reshape(-1, LANES))
    return _pad_rows(jnp.concatenate(rows, axis=0), total_rows)


def _unpack_replicated(pack, shapes, names):
    out, r = {}, 0
    for k in names:
        size = math.prod(shapes[k])
        n = -(-size // LANES)
        out[k] = pack[r:r + n].reshape(-1)[:size].reshape(shapes[k])
        r += n
    return out


WEIGHT_ORDER = ("ab_norm_g", "ab_w_in", "sgu_norm_g", "sgu_norm_b", "sgu_w", "sgu_bias", "q_norm_g", "k_norm_g", "ab_w_out",
                "cd_norm_g", "cd_w_in", "conv_c_w", "conv_c_b", "c_ln_g", "c_ln_b", "conv_d_w", "cd_w_out", "ffn_norm_g",
                "ffn_w_gate", "ffn_w_up", "ffn_w_down")


def kernel(x, ab_norm_g, ab_w_in, sgu_norm_g, sgu_norm_b, sgu_w, sgu_bias, q_norm_g, k_norm_g, ab_w_out, cd_norm_g, cd_w_in, conv_c_w, conv_c_b, c_ln_g, c_ln_b, conv_d_w, cd_w_out, ffn_norm_g, ffn_w_gate, ffn_w_up, ffn_w_down, loss_target, m_ab_norm_g, m_ab_w_in, m_sgu_norm_g, m_sgu_norm_b, m_sgu_w, m_sgu_bias, m_q_norm_g, m_k_norm_g, m_ab_w_out, m_cd_norm_g, m_cd_w_in, m_conv_c_w, m_conv_c_b, m_c_ln_g, m_c_ln_b, m_conv_d_w, m_cd_w_out, m_ffn_norm_g, m_ffn_w_gate, m_ffn_w_up, m_ffn_w_down, v_ab_norm_g, v_ab_w_in, v_sgu_norm_g, v_sgu_norm_b, v_sgu_w, v_sgu_bias, v_q_norm_g, v_k_norm_g, v_ab_w_out, v_cd_norm_g, v_cd_w_in, v_conv_c_w, v_conv_c_b, v_c_ln_g, v_c_ln_b, v_conv_d_w, v_cd_w_out, v_ffn_norm_g, v_ffn_w_gate, v_ffn_w_up, v_ffn_w_down):
    args = dict(locals())
    ws = {k: args[k] for k in WEIGHT_ORDER}
    ms = {k: args["m_" + k] for k in WEIGHT_ORDER}
    vs = {k: args["v_" + k] for k in WEIGHT_ORDER}
    small_names = [k for k in WEIGHT_ORDER if k not in BIG]
    t, d = x.shape[1:]

    for group in (ws, ms, vs):
        for k in ROWS_MINOR:
            group[k] = jnp.swapaxes(group[k], 1, 2)
    ex = _Exchange(enabled=True)
    ex.w[A_IN], ex.far[A_IN] = _stage_own("stage_ab_w_in", ws["ab_w_in"], 0, BF16, far_slab=True)
    own_small = _pack_sharded({k: ws[k][0] for k in SHARDED_SMALL})
    ex.w[SMALL_SHARDED] = _stage_own("stage_small", own_small[None], 0, F32)
    rest = [u for u in UNITS if u != A_IN]
    x2 = x.reshape(t, d)
    h0, staged = ex.run(_stage_rest_and_norm, x2, ws["ab_norm_g"], [(ws[name], layer) for name, layer in rest],
                        phases=[("gather_near", [A_IN]), ("gather", [SMALL_SHARDED])])
    ex.w.update(zip(rest, staged))
    sp = _gathered_small(ex.w[SMALL_SHARDED].reshape(N_CHIPS, SHARDED_ROWS, LANES), {k: ws[k].shape[1:] for k in SHARDED_SMALL})
    for k in REPLICATED_SMALL:
        sp[k] = ws[k] if k == "ffn_norm_g" else ws[k][0]

    loss, grad_x, g_small = _local_step(x2, loss_target.reshape(t, d), ex, sp, h0)

    shapes = {k: ws[k].shape for k in REPLICATED_SMALL}
    shapes["loss"] = (1,)
    g_small["loss"] = loss
    join_last, settle = ex.ride_for([("join", [A_IN])])
    late, joined = _all_reduce_small(_pack_replicated(g_small, REPLICATED_LATE, REPLICATED_LATE_ROWS), join_last)
    settle(joined)
    grad = _unpack_sharded(ex.done[SMALL_SHARDED].reshape(SHARDED_ROWS, LANES), {k: ws[k].shape for k in SHARDED_SMALL})
    grad.update(_unpack_replicated(ex.done[REPLICATED_UNIT].reshape(REPLICATED_EARLY_ROWS, LANES), shapes, REPLICATED_EARLY))
    grad.update(_unpack_replicated(late, shapes, REPLICATED_LATE))
    loss = grad.pop("loss")[0]

    delta, new_m, new_v = {}, {}, {}
    for group in BIG_BY_LAYERS:
        g_layers = [[ex.done[(k, layer)] for layer in range(ws[k].shape[0])] for k in group]
        results = _adamw_big("adamw_" + group[0], [ws[k] for k in group], g_layers, [ms[k] for k in group], [vs[k] for k in group])
        for k, outs in zip(group, results):
            if k in ROWS_MINOR:
                outs = [jnp.swapaxes(o, 1, 2) for o in outs]
            grad[k], delta[k], new_m[k], new_v[k] = outs
    d_s, m_s, v_s = _adamw_small([ws[k] for k in small_names], [grad[k] for k in small_names],
                                 [ms[k] for k in small_names], [vs[k] for k in small_names])
    for j, k in enumerate(small_names):
        delta[k], new_m[k], new_v[k] = d_s[j], m_s[j], v_s[j]

    return (loss, grad_x[None], *[grad[k] for k in WEIGHT_ORDER], *[delta[k] for k in WEIGHT_ORDER],
            *[new_m[k] for k in WEIGHT_ORDER], *[new_v[k] for k in WEIGHT_ORDER])
```

```python
import functools
import math

import jax
import jax.numpy as jnp
from jax import lax
from jax.experimental import pallas as pl
from jax.experimental.pallas import tpu as pltpu

F32 = jnp.float32
BF16 = jnp.bfloat16
SDS = jax.ShapeDtypeStruct

N_CHIPS = 4
EPS = 1e-6
NEG_INF = -1e30
CHUNK = 128
A_GROUPS = 4
A_WIDTH = 512
N_DIL = 3
DIL_RATES = (1, 4, 16)
HEAD_DIM = 64
B_WIDTH = 512
ROPE_DIM = 16
ROPE_THETA = 500000.0
C_WIDTH = 512
C_KERNEL = 31
D_KERNEL = 3
HALO = 32
ATT_BLOCK = 128
LANES = 128

ADAM_LR = 0.001
ADAM_B1 = 0.9
ADAM_B2 = 0.999
ADAM_EPS = 1e-08
ADAM_WD = 0.01
ADAM_STEP = 10

VMEM_LIMIT = 56 * 1024 * 1024

NN = (((1,), (0,)), ((), ()))
NT = (((1,), (1,)), ((), ()))
TN = (((0,), (0,)), ((), ()))

TILES = {"proj_in": 2048, "proj_out": 1024, "ffn_in": 1024, "ffn_out": 1024, "ffn_dact": 512, "dgrad_cols": 512,
         "dgrad_rows": 1024, "wgrad": 4096}


def _params(sem=None, collective_id=None):
    return pltpu.CompilerParams(dimension_semantics=sem, vmem_limit_bytes=VMEM_LIMIT, collective_id=collective_id)


def _bf(v):
    return v if v.dtype == BF16 else v.astype(BF16)


def _dot(a, b, dims):
    return lax.dot_general(_bf(a), _bf(b), dims, preferred_element_type=F32)


def _dot_hi(a, b):
    return jnp.dot(a, b, precision=lax.Precision.HIGHEST, preferred_element_type=F32)


def _sigmoid(v):
    return 0.5 * jnp.tanh(0.5 * v) + 0.5


def _gelu(v):
    return 0.5 * v * (1.0 + lax.erf(v * (1.0 / math.sqrt(2.0))))


def _gelu_grad(v):
    cdf = 0.5 * (1.0 + lax.erf(v * (1.0 / math.sqrt(2.0))))
    return cdf + v * jnp.exp(-0.5 * v * v) * (1.0 / math.sqrt(2.0 * math.pi))


def _segment_mean_matrix(seg, scale=None):
    r = lax.broadcasted_iota(jnp.int32, (LANES, LANES), 0) // seg
    c = lax.broadcasted_iota(jnp.int32, (LANES, LANES), 1) // seg
    return jnp.where(r == c, (1.0 / seg) if scale is None else scale, 0.0).astype(BF16)


def _segment_dot(v, seg):
    hi = v.astype(BF16)
    lo = (v - hi.astype(F32)).astype(BF16)
    return jnp.dot(hi, seg, preferred_element_type=F32) + jnp.dot(lo, seg, preferred_element_type=F32)


MESH = pl.DeviceIdType.MESH
ANY = pl.BlockSpec(memory_space=pl.ANY)


def _position():
    x, y, c = lax.axis_index("x"), lax.axis_index("y"), lax.axis_index("c")
    others = [(1 - x, y), (x, 1 - y), (1 - x, 1 - y)]
    return x, y, c, 2 * x + y, others


class _Ride:
    def __init__(self, ins, bufs, new_outs, sem_shapes, start, finish, reach):
        self.ins, self.bufs, self.new_outs, self.sem_shapes = list(ins), list(bufs), list(new_outs), list(sem_shapes)
        self.start, self.finish = start, finish
        self.reach = frozenset(reach)

    def entry_barrier(self):
        x, y, c, _, others = _position()
        peers = ([(x, y, 1 - c)] if "sibling" in self.reach else []) + ([(qx, qy, c) for qx, qy in others] if "chips" in self.reach else [])
        barrier = pltpu.get_barrier_semaphore()
        for peer in peers:
            pl.semaphore_signal(barrier, inc=1, device_id=peer, device_id_type=MESH)
        pl.semaphore_wait(barrier, len(peers))

    @property
    def collective_id(self):
        return {frozenset(["sibling"]): 0, frozenset(["chips"]): 1, frozenset(["sibling", "chips"]): 2}[self.reach]


def _ride_both(a, b):
    na = (len(a.ins), len(a.bufs), len(a.new_outs), len(a.sem_shapes))

    def split(ins, bufs, new, sems):
        return ((ins[:na[0]], bufs[:na[1]], new[:na[2]], sems[:na[3]]), (ins[na[0]:], bufs[na[1]:], new[na[2]:], sems[na[3]:]))

    def start(*refs):
        ra, rb = split(*refs)
        a.start(*ra)
        b.start(*rb)

    def finish(*refs):
        ra, rb = split(*refs)
        a.finish(*ra)
        b.finish(*rb)

    return _Ride(a.ins + b.ins, a.bufs + b.bufs, a.new_outs + b.new_outs, a.sem_shapes + b.sem_shapes, start, finish,
                 a.reach | b.reach)


def _call(body, *, grid, in_specs, out_specs, out_shape, operands, name, scratch_shapes=(), aliases=None, ride=None,
          prefetch=None):
    off = 0 if prefetch is None else 1
    lead = [] if prefetch is None else [prefetch]

    params = _params(collective_id=None if ride is None else ride.collective_id)

    def launch(kernel_body, in_specs_, out_specs_, out_shape_, scratch_, aliases_, *args):
        if prefetch is None:
            return pl.pallas_call(kernel_body, grid=grid, in_specs=in_specs_, out_specs=out_specs_, out_shape=out_shape_,
                                  scratch_shapes=scratch_, input_output_aliases=aliases_, name=name,
                                  compiler_params=params)(*args)
        spec = pltpu.PrefetchScalarGridSpec(num_scalar_prefetch=1, grid=grid, in_specs=in_specs_, out_specs=out_specs_,
                                            scratch_shapes=scratch_)
        return pl.pallas_call(kernel_body, grid_spec=spec, out_shape=out_shape_, input_output_aliases=aliases_, name=name,
                              compiler_params=params)(*lead, *args)

    if ride is None:
        return launch(body, list(in_specs), out_specs, out_shape, list(scratch_shapes), dict(aliases or {}), *operands)
    multi = isinstance(out_shape, (list, tuple))
    out_shapes = list(out_shape) if multi else [out_shape]
    o_specs = list(out_specs) if multi else [out_specs]
    n_in, n_out, n_scr = off + len(operands), len(out_shapes), len(scratch_shapes)
    n_ri, n_rb, n_rn = len(ride.ins), len(ride.bufs), len(ride.new_outs)

    def carrying(*refs):
        k = n_in
        r_ins = refs[k:k + n_ri]
        k += n_ri + n_rb
        outs = refs[k:k + n_out]
        k += n_out
        r_bufs = refs[k:k + n_rb]
        k += n_rb
        r_new = refs[k:k + n_rn]
        k += n_rn
        scratch = refs[k:k + n_scr]
        sems = refs[k + n_scr:]
        first, last = None, None
        for axis, size in enumerate(grid):
            pid = pl.program_id(axis)
            first = (pid == 0) if first is None else first & (pid == 0)
            last = (pid == size - 1) if last is None else last & (pid == size - 1)

        @pl.when(first)
        def _():
            ride.entry_barrier()
            ride.start(r_ins, r_bufs, r_new, sems)

        body(*refs[:n_in], *outs, *scratch)

        @pl.when(last)
        def _():
            ride.finish(r_ins, r_bufs, r_new, sems)

    all_aliases = dict(aliases or {})
    for j in range(n_rb):
        all_aliases[n_in + n_ri + j] = n_out + j
    res = launch(
        carrying, list(in_specs) + [ANY] * (n_ri + n_rb), o_specs + [ANY] * (n_rb + n_rn),
        out_shapes + [SDS(b.shape, b.dtype) for b in ride.bufs] + ride.new_outs,
        list(scratch_shapes) + [pltpu.SemaphoreType.DMA(s) for s in ride.sem_shapes], all_aliases,
        *operands, *ride.ins, *ride.bufs)
    outs = res[:n_out]
    return (list(outs) if multi else outs[0]), list(res[n_out:])


def _run_ride(name, ride):
    n_ri, n_rb, n_rn = len(ride.ins), len(ride.bufs), len(ride.new_outs)

    def body(*refs):
        r_ins = refs[:n_ri]
        r_bufs = refs[n_ri + n_rb:n_ri + 2 * n_rb]
        r_new = refs[n_ri + 2 * n_rb:n_ri + 2 * n_rb + n_rn]
        sems = refs[n_ri + 2 * n_rb + n_rn:]
        ride.entry_barrier()
        ride.start(r_ins, r_bufs, r_new, sems)
        ride.finish(r_ins, r_bufs, r_new, sems)

    return list(pl.pallas_call(
        body, in_specs=[ANY] * (n_ri + n_rb), out_specs=[ANY] * (n_rb + n_rn),
        out_shape=[SDS(b.shape, b.dtype) for b in ride.bufs] + ride.new_outs,
        scratch_shapes=[pltpu.SemaphoreType.DMA(s) for s in ride.sem_shapes],
        input_output_aliases={n_ri + j: j for j in range(n_rb)}, name=name,
        compiler_params=pltpu.CompilerParams(collective_id=ride.collective_id))(*ride.ins, *ride.bufs))


def _whole(ref, p):
    return ref[...]


def _slab(ref, p):
    return ref[p]


def _matmul(name, grid, pairs, extras, outs, dims, epi, *, slabs=1, n_acc=1, ride=None):
    n_pairs, n_ex, n_out = len(pairs), len(extras), len(outs)

    def body(*refs):
        ab = refs[:2 * n_pairs]
        ex = refs[2 * n_pairs:2 * n_pairs + n_ex]
        out_refs = refs[2 * n_pairs + n_ex:2 * n_pairs + n_ex + n_out]
        pids = tuple(pl.program_id(a) for a in range(len(grid)))
        parts = [None] * n_acc
        for p in range(slabs):
            for j, (_, _, a_pick, _, _, b_pick, acc) in enumerate(pairs):
                d = _dot(a_pick(ab[2 * j], p), b_pick(ab[2 * j + 1], p), dims)
                parts[acc] = d if parts[acc] is None else parts[acc] + d
        epi(parts, ex, out_refs, pids)

    operands, in_specs = [], []
    for a, a_spec, _, b, b_spec, _, _ in pairs:
        operands += [a, b]
        in_specs += [a_spec, b_spec]
    for e, e_spec in extras:
        operands.append(e)
        in_specs.append(e_spec)
    return _call(body, grid=grid, in_specs=in_specs, out_specs=[o[1] for o in outs], out_shape=[o[0] for o in outs],
                 operands=operands, name=name, ride=ride)


def _rms_rows(v, g):
    r = lax.rsqrt(jnp.mean(v * v, axis=-1, keepdims=True) + EPS)
    return v * r * g


def _rms_fwd(name, x, g):
    t, d = x.shape
    tm = 512

    def body(x_ref, g_ref, o_ref):
        o_ref[...] = _rms_rows(x_ref[...], g_ref[...]).astype(BF16)

    return pl.pallas_call(
        body, grid=(t // tm,),
        in_specs=[pl.BlockSpec((tm, d), lambda i: (i, 0)), pl.BlockSpec((1, d), lambda i: (0, 0))],
        out_specs=pl.BlockSpec((tm, d), lambda i: (i, 0)), out_shape=SDS((t, d), BF16), name=name,
        compiler_params=_params())(x, g)


def _epi_residual_norm(accs, ex, outs, pids):
    x_new = accs[0] + ex[0][...]
    outs[0][...] = x_new
    outs[1][...] = _rms_rows(x_new, ex[1][...]).astype(BF16)


def _epi_residual_loss(accs, ex, outs, pids):
    y = accs[0] + ex[0][...]
    err = y - ex[1][...]
    dy = err * (1.0 / err.shape[-1])
    outs[0][...] = dy
    outs[2][...] = dy.astype(BF16)

    @pl.when(pids[0] == 0)
    def _():
        outs[1][...] = jnp.zeros_like(outs[1])

    outs[1][...] += jnp.sum(err * err)


def _epi_rms_bwd(accs, ex, outs, pids):
    dh = accs[0]
    xv, g, res = ex[0][...], ex[1][...], ex[2][...]
    r = lax.rsqrt(jnp.mean(xv * xv, axis=-1, keepdims=True) + EPS)
    xh = xv * r
    dy = dh * g
    dx = res + r * (dy - xh * jnp.mean(dy * xh, axis=-1, keepdims=True))
    outs[0][...] = dx
    if len(outs) > 2:
        outs[2][...] = dx.astype(BF16)

    @pl.when(pids[0] == 0)
    def _():
        outs[1][...] = jnp.zeros_like(outs[1])

    outs[1][...] += jnp.sum(dh * xh, axis=0, keepdims=True)


def _row_spec(tm, d):
    return pl.BlockSpec((tm, d), lambda i, *_: (i, 0))


def _const_spec(shape):
    nd = len(shape)
    return pl.BlockSpec(shape, lambda *_: (0,) * nd)


def _proj_in(name, h, w, layer, ride=None):
    t, d = h.shape
    n4 = w.shape[-1]
    tm = TILES["proj_in"]

    def epi(accs, ex, outs, pids):
        outs[0][...] = accs[0].astype(BF16)

    res = _matmul(
        name, (N_CHIPS, t // tm),
        [(h, pl.BlockSpec((tm, d), lambda p, i: (i, 0)), _whole,
          w, pl.BlockSpec((None, None, d, n4), lambda p, i: (p, layer, 0, 0)), _whole, 0)],
        [], [(SDS((t, N_CHIPS * n4), BF16), pl.BlockSpec((tm, n4), lambda p, i: (i, p)))],
        NN, epi, ride=ride)
    return res[0] if ride is None else (res[0][0], res[1])


def _proj_in_near(name, h, w, ride=None):
    t, d = h.shape
    n4 = w.shape[-1]
    tm = TILES["proj_in"]

    def shard(j, s):
        return j + (j >= N_CHIPS - 1 - s[1]).astype(jnp.int32)

    def body(s_ref, h_ref, w_ref, o_ref):
        o_ref[...] = _dot(h_ref[...], w_ref[...], NN).astype(BF16)

    return _call(
        body, grid=(N_CHIPS - 1, t // tm),
        in_specs=[pl.BlockSpec((tm, d), lambda j, i, s: (i, 0)),
                  pl.BlockSpec((None, None, d, n4), lambda j, i, s: (shard(j, s), 0, 0, 0))],
        out_specs=pl.BlockSpec((tm, n4), lambda j, i, s: (i, shard(j, s))), out_shape=SDS((t, N_CHIPS * n4), BF16),
        operands=[h, w], name=name, ride=ride, prefetch=_mesh_scalars())


def _proj_in_far(name, h, slab, proj, w):
    t, d = h.shape
    n4 = slab.shape[-1]
    tm = TILES["proj_in"]

    def body(s_ref, h_ref, slab_ref, proj_in, w_in, o_ref, w_ref):
        shard = slab_ref[...]
        o_ref[...] = _dot(h_ref[...], shard, NN).astype(BF16)
        w_ref[...] = shard

    proj, w_full = _call(
        body, grid=(t // tm,),
        in_specs=[pl.BlockSpec((tm, d), lambda i, s: (i, 0)), pl.BlockSpec((d, n4), lambda i, s: (0, 0)), ANY, ANY],
        out_specs=[pl.BlockSpec((tm, n4), lambda i, s: (i, N_CHIPS - 1 - s[1])),
                   pl.BlockSpec((None, d, n4), lambda i, s: (N_CHIPS - 1 - s[1], 0, 0))],
        out_shape=[SDS(proj.shape, BF16), SDS((N_CHIPS, d, n4), BF16)],
        operands=[h, slab.reshape(d, n4), proj, w.reshape(N_CHIPS, d, n4)], aliases={3: 0, 4: 1}, name=name,
        prefetch=_mesh_scalars())
    return proj, w_full.reshape(w.shape)


def _proj_out(name, a, w, x, g_next=None, target=None):
    t, k = a.shape
    d = w.shape[-1]
    tm = TILES["proj_out"]
    if target is None:
        extras = [(x, _row_spec(tm, d)), (g_next, _const_spec((1, d)))]
        outs = [(SDS((t, d), F32), _row_spec(tm, d)), (SDS((t, d), BF16), _row_spec(tm, d))]
        epi = _epi_residual_norm
    else:
        extras = [(x, _row_spec(tm, d)), (target, _row_spec(tm, d))]
        outs = [(SDS((t, d), F32), _row_spec(tm, d)), (SDS((8, LANES), F32), _const_spec((8, LANES))),
                (SDS((t, d), BF16), _row_spec(tm, d))]
        epi = _epi_residual_loss
    return _matmul(name, (t // tm,), [(a, _row_spec(tm, k), _whole, w, _const_spec((k, d)), _whole, 0)], extras, outs, NN, epi)


def _ffn_in(name, h, wg, wu, layer, ride=None):
    t, d = h.shape
    n4 = wg.shape[-2]
    tm = TILES["ffn_in"]

    def epi(accs, ex, outs, pids):
        gate, up = accs
        s = _sigmoid(gate)
        silu = gate * s
        outs[0][...] = (up * (s + silu - silu * s)).astype(BF16)
        outs[1][...] = silu.astype(BF16)
        outs[2][...] = (silu * up).astype(BF16)

    w_spec = pl.BlockSpec((None, None, n4, d), lambda p, i: (p, layer, 0, 0))
    h_spec = pl.BlockSpec((tm, d), lambda p, i: (i, 0))
    o = (SDS((N_CHIPS, t, n4), BF16), pl.BlockSpec((None, tm, n4), lambda p, i: (p, i, 0)))
    return _matmul(name, (N_CHIPS, t // tm),
                   [(h, h_spec, _whole, wg, w_spec, _whole, 0), (h, h_spec, _whole, wu, w_spec, _whole, 1)], [],
                   [o, o, o], NT, epi, n_acc=2, ride=ride)


def _ffn_out(name, act, wd, layer, x, g_next=None, target=None, ride=None):
    _, t, n4 = act.shape
    d = wd.shape[-1]
    tm = TILES["ffn_out"]
    xs = _row_spec(tm, d)
    if target is None:
        extras = [(x, xs), (g_next, _const_spec((1, d)))]
        outs = [(SDS((t, d), F32), xs), (SDS((t, d), BF16), xs)]
        epi = _epi_residual_norm
    else:
        extras = [(x, xs), (target, xs)]
        outs = [(SDS((t, d), F32), xs), (SDS((8, LANES), F32), _const_spec((8, LANES))), (SDS((t, d), BF16), xs)]
        epi = _epi_residual_loss
    return _matmul(
        name, (t // tm,),
        [(act, pl.BlockSpec((N_CHIPS, tm, n4), lambda i: (0, i, 0)), _slab,
          wd, pl.BlockSpec((N_CHIPS, None, n4, d), lambda i: (0, layer, 0, 0)), _slab, 0)],
        extras, outs, NN, epi, slabs=N_CHIPS, ride=ride)


DACT_SLOTS = 3


def _ffn_dact(name, g, wd, layer, gate, up, ride=None):
    t, d = g.shape
    n4 = wd.shape[-2]
    tm = TILES["ffn_dact"]
    steps = t // tm
    assert steps >= DACT_SLOTS

    def body(g_ref, w_ref, gate_hbm, up_hbm, dgate_ref, dup_ref, gate_buf, up_buf, sems):
        i = pl.program_id(0)

        def fetches(step, slot):
            first_row = step * tm
            rows = pl.ds(first_row if isinstance(first_row, int) else pl.multiple_of(first_row, tm), tm)
            return (pltpu.make_async_copy(gate_hbm.at[:, rows, :], gate_buf.at[slot], sems.at[0, slot]),
                    pltpu.make_async_copy(up_hbm.at[:, rows, :], up_buf.at[slot], sems.at[1, slot]))

        @pl.when(i == 0)
        def _():
            for first in range(DACT_SLOTS - 1):
                for fetch in fetches(first, first):
                    fetch.start()

        for slot in range(DACT_SLOTS):
            @pl.when(i % DACT_SLOTS == slot)
            def _():
                ahead = i + DACT_SLOTS - 1

                @pl.when(ahead < steps)
                def _():
                    for fetch in fetches(ahead, (slot + DACT_SLOTS - 1) % DACT_SLOTS):
                        fetch.start()

                for fetch in fetches(i, slot):
                    fetch.wait()
                gv = g_ref[...]
                for p in range(N_CHIPS):
                    dact = _dot(gv, w_ref[p], NT)
                    dgate_ref[p] = (dact * gate_buf[slot, p].astype(F32)).astype(BF16)
                    dup_ref[p] = (dact * up_buf[slot, p].astype(F32)).astype(BF16)

    blk = pl.BlockSpec((N_CHIPS, tm, n4), lambda i: (0, i, 0))
    ring = pltpu.VMEM((DACT_SLOTS, N_CHIPS, tm, n4), BF16)
    return _call(
        body, grid=(steps,),
        in_specs=[_row_spec(tm, d), pl.BlockSpec((N_CHIPS, None, n4, d), lambda i: (0, layer, 0, 0)), ANY, ANY],
        out_specs=[blk, blk], out_shape=[SDS((N_CHIPS, t, n4), BF16)] * 2, operands=[g, wd, gate, up], name=name,
        scratch_shapes=[ring, ring, pltpu.SemaphoreType.DMA((2, DACT_SLOTS))], ride=ride)


def _copy_epi(accs, ex, outs, pids):
    for a, o in zip(accs, outs):
        o[...] = a.astype(o.dtype)


def _dgrad_cols(name, dz_list, w_list, layer, three_d, x, g, res, bf16_copy=True, w_rows=False, ride=None):
    t, d = x.shape
    n4 = w_list[0].shape[-2 if w_rows else -1]
    tm = TILES["dgrad_cols"]
    if three_d:
        zs, z_pick = pl.BlockSpec((N_CHIPS, tm, n4), lambda i: (0, i, 0)), _slab
    else:
        zs, z_pick = _row_spec(tm, N_CHIPS * n4), (lambda ref, p: ref[:, p * n4:(p + 1) * n4])
    ws = pl.BlockSpec((N_CHIPS, None) + ((n4, d) if w_rows else (d, n4)), lambda i: (0, layer, 0, 0))
    xs = _row_spec(tm, d)
    return _matmul(
        name, (t // tm,), [(dz, zs, z_pick, w, ws, _slab, 0) for dz, w in zip(dz_list, w_list)],
        [(x, xs), (g, _const_spec((1, d))), (res, xs)],
        [(SDS((t, d), F32), xs), (SDS((1, d), F32), _const_spec((1, d)))] + ([(SDS((t, d), BF16), xs)] if bf16_copy else []),
        NN if w_rows else NT, _epi_rms_bwd, slabs=N_CHIPS, ride=ride)


def _dgrad_rows(name, g, w):
    t, d = g.shape
    k = w.shape[0]
    tm = TILES["dgrad_rows"]
    return _matmul(name, (t // tm,), [(g, _row_spec(tm, d), _whole, w, _const_spec((k, d)), _whole, 0)], [],
                   [(SDS((t, k), F32), _row_spec(tm, k))], NT, _copy_epi)[0]


A_TILE = 256


def _a_common(p_ref, lg_ref, lb_ref):
    pv = p_ref[...].astype(F32)
    a = _gelu(pv)
    u, v = a[:, :A_WIDTH], a[:, A_WIDTH:]
    vc = v - jnp.mean(v, axis=-1, keepdims=True)
    rs = lax.rsqrt(jnp.mean(vc * vc, axis=-1, keepdims=True) + EPS)
    vhat = vc * rs
    vn = vhat * lg_ref[...] + lb_ref[...]
    return pv, u, vhat, rs, vn.astype(BF16)


def _tril_weights(w_ref, g):
    r = lax.broadcasted_iota(jnp.int32, (CHUNK, CHUNK), 0)
    c = lax.broadcasted_iota(jnp.int32, (CHUNK, CHUNK), 1)
    return jnp.where(c <= r, w_ref[g], 0.0).astype(BF16), c <= r


def _mixer_a_fwd(proj, lg, lb, w, bias_t):
    t = proj.shape[0]

    def body(p_ref, lg_ref, lb_ref, w_ref, bt_ref, o_ref):
        _, u, _, _, vnb = _a_common(p_ref, lg_ref, lb_ref)
        for g in range(A_GROUPS):
            wt, _ = _tril_weights(w_ref, g)
            cs = slice(g * CHUNK, (g + 1) * CHUNK)
            for ch in range(A_TILE // CHUNK):
                rs_ = slice(ch * CHUNK, (ch + 1) * CHUNK)
                mixed = _dot(wt, vnb[rs_, cs], NN) + bt_ref[:, g:g + 1]
                o_ref[rs_, cs] = (u[rs_, cs] * mixed).astype(BF16)

    return pl.pallas_call(
        body, grid=(t // A_TILE,),
        in_specs=[pl.BlockSpec((A_TILE, 2 * A_WIDTH), lambda i: (i, 0)), _const_spec((1, A_WIDTH)),
                  _const_spec((1, A_WIDTH)), _const_spec((A_GROUPS, CHUNK, CHUNK)), _const_spec((CHUNK, A_GROUPS))],
        out_specs=pl.BlockSpec((A_TILE, A_WIDTH), lambda i: (i, 0)), out_shape=SDS((t, A_WIDTH), BF16),
        name="mixer_a_fwd", compiler_params=_params())(proj, lg, lb, w, bias_t)


def _mixer_a_bwd(proj, dcat, lg, lb, w, bias_t):
    t = proj.shape[0]

    def body(p_ref, da_ref, lg_ref, lb_ref, w_ref, bt_ref, dp_ref, dw_ref, dbt_ref, dlg_ref, dlb_ref, du_scr, dvn_scr):
        @pl.when(pl.program_id(0) == 0)
        def _():
            dw_ref[...] = jnp.zeros_like(dw_ref)
            dbt_ref[...] = jnp.zeros_like(dbt_ref)
            dlg_ref[...] = jnp.zeros_like(dlg_ref)
            dlb_ref[...] = jnp.zeros_like(dlb_ref)

        pv, u, vhat, rs, vnb = _a_common(p_ref, lg_ref, lb_ref)
        da = da_ref[...]
        for g in range(A_GROUPS):
            wt, keep = _tril_weights(w_ref, g)
            cs = slice(g * CHUNK, (g + 1) * CHUNK)
            for ch in range(A_TILE // CHUNK):
                rs_ = slice(ch * CHUNK, (ch + 1) * CHUNK)
                vg = vnb[rs_, cs]
                mixed = _dot(wt, vg, NN) + bt_ref[:, g:g + 1]
                du_scr[rs_, cs] = da[rs_, cs] * mixed
                dmx = da[rs_, cs] * u[rs_, cs]
                dw_ref[g] += jnp.where(keep, _dot(dmx, vg, NT), 0.0)
                dvn_scr[rs_, cs] = _dot(wt, dmx, TN)
                dbt_ref[:, g:g + 1] += jnp.sum(dmx, axis=1, keepdims=True)
        dvn = dvn_scr[...]
        dlg_ref[...] += jnp.sum(dvn * vhat, axis=0, keepdims=True)
        dlb_ref[...] += jnp.sum(dvn, axis=0, keepdims=True)
        dvh = dvn * lg_ref[...]
        dv = rs * (dvh - jnp.mean(dvh, axis=-1, keepdims=True) - vhat * jnp.mean(dvh * vhat, axis=-1, keepdims=True))
        gp = _gelu_grad(pv)
        dp_ref[:, :A_WIDTH] = (du_scr[...] * gp[:, :A_WIDTH]).astype(BF16)
        dp_ref[:, A_WIDTH:] = (dv * gp[:, A_WIDTH:]).astype(BF16)

    return pl.pallas_call(
        body, grid=(t // A_TILE,),
        in_specs=[pl.BlockSpec((A_TILE, 2 * A_WIDTH), lambda i: (i, 0)), pl.BlockSpec((A_TILE, A_WIDTH), lambda i: (i, 0)),
                  _const_spec((1, A_WIDTH)), _const_spec((1, A_WIDTH)), _const_spec((A_GROUPS, CHUNK, CHUNK)),
                  _const_spec((CHUNK, A_GROUPS))],
        out_specs=[pl.BlockSpec((A_TILE, 2 * A_WIDTH), lambda i: (i, 0)), _const_spec((A_GROUPS, CHUNK, CHUNK)),
                   _const_spec((CHUNK, A_GROUPS)), _const_spec((1, A_WIDTH)), _const_spec((1, A_WIDTH))],
        out_shape=[SDS((t, 2 * A_WIDTH), BF16), SDS((A_GROUPS, CHUNK, CHUNK), F32), SDS((CHUNK, A_GROUPS), F32),
                   SDS((1, A_WIDTH), F32), SDS((1, A_WIDTH), F32)],
        scratch_shapes=[pltpu.VMEM((A_TILE, A_WIDTH), F32), pltpu.VMEM((A_TILE, A_WIDTH), F32)],
        name="mixer_a_bwd", compiler_params=_params())(proj, dcat, lg, lb, w, bias_t)


def _rope_tables(t):
    half = ROPE_DIM // 2
    inv_freq = ROPE_THETA ** (-jnp.arange(half, dtype=F32) * 2.0 / ROPE_DIM)
    ang = jnp.arange(t, dtype=F32)[:, None] * inv_freq[None, :]
    cos, sin = jnp.cos(ang), jnp.sin(ang)
    one = jnp.ones((t, HEAD_DIM - ROPE_DIM), F32)
    zero = jnp.zeros((t, HEAD_DIM - ROPE_DIM), F32)
    zh = jnp.zeros((t, half), F32)
    c = jnp.concatenate([cos, cos, one], axis=1)
    s1 = jnp.concatenate([-sin, zh, zero], axis=1)
    s2 = jnp.concatenate([zh, sin, zero], axis=1)
    return tuple(jnp.tile(a, (1, LANES // HEAD_DIM)) for a in (c, s1, s2))


QK_TILE = 512
QK_ROWS = 64
QK_COLS = 2 * N_DIL * B_WIDTH


CHUNKS = B_WIDTH // LANES


def _regroup_out(scr, first, out_ref, rate, tile):
    rows = tile // rate
    for rho in range(rate):
        for c in range(CHUNKS):
            out_ref[rho, :, c * LANES:(c + 1) * LANES] = scr[first + c, pl.ds(rho, rows, stride=rate), :].astype(out_ref.dtype)


def _regroup_in(x_ref, scr, rate, tile):
    rows = tile // rate
    for rho in range(rate):
        for c in range(CHUNKS):
            scr[c, pl.ds(rho, rows, stride=rate), :] = x_ref[rho, :, c * LANES:(c + 1) * LANES].astype(F32)


def _regrouped_spec(rate, tile):
    return pl.BlockSpec((rate, tile // rate, B_WIDTH), lambda i, *_: (0, i, 0))


def _qk_fwd(proj, gains, tabs, ride=None):
    t = proj.shape[0]
    col0 = 2 * A_WIDTH // 1024
    r1, r2 = DIL_RATES[1], DIL_RATES[2]

    def body(p_ref, g_ref, c_ref, s1_ref, s2_ref, o_ref, q1_ref, q2_ref, k1_ref, k2_ref, scr):
        seg = _segment_mean_matrix(HEAD_DIM)
        for r0 in range(0, QK_TILE, QK_ROWS):
            rows = slice(r0, r0 + QK_ROWS)
            c, s1, s2 = c_ref[rows, :], s1_ref[rows, :], s2_ref[rows, :]
            for ci in range(1024 // LANES):
                ls = slice(ci * LANES, (ci + 1) * LANES)
                xv = p_ref[rows, ls].astype(F32)
                r = lax.rsqrt(_segment_dot(xv * xv, seg) + EPS)
                y = xv * r * g_ref[:, ls]
                val = y * c + pltpu.roll(y, LANES - 8, axis=1) * s1 + pltpu.roll(y, 8, axis=1) * s2
                o_ref[rows, ls] = val.astype(BF16)
                scr[ci, rows, :] = val

        j = pl.program_id(1)

        @pl.when(j == 0)
        def _():
            _regroup_out(scr, CHUNKS, q1_ref, r1, QK_TILE)

        @pl.when(j == 1)
        def _():
            _regroup_out(scr, 0, q2_ref, r2, QK_TILE)

        @pl.when(j == 2)
        def _():
            _regroup_out(scr, 0, k1_ref, r1, QK_TILE)
            _regroup_out(scr, CHUNKS, k2_ref, r2, QK_TILE)

    tab = pl.BlockSpec((QK_TILE, LANES), lambda i, j: (i, 0))
    g1, g2 = SDS((r1, t // r1, B_WIDTH), BF16), SDS((r2, t // r2, B_WIDTH), BF16)
    s1_, s2_ = _regrouped_spec(r1, QK_TILE), _regrouped_spec(r2, QK_TILE)
    return _call(
        body, grid=(t // QK_TILE, QK_COLS // 1024),
        in_specs=[pl.BlockSpec((QK_TILE, 1024), lambda i, j: (i, col0 + j)), pl.BlockSpec((1, 1024), lambda i, j: (0, j)),
                  tab, tab, tab],
        out_specs=[pl.BlockSpec((QK_TILE, 1024), lambda i, j: (i, j)), s1_, s2_, s1_, s2_],
        out_shape=[SDS((t, QK_COLS), BF16), g1, g2, g1, g2],
        scratch_shapes=[pltpu.VMEM((2 * CHUNKS, QK_TILE, LANES), F32)],
        operands=[proj, gains, *tabs], name="qk_norm_rope_fwd", ride=ride)


PERM_TILE = 512


def _permute(name, items):
    t = items[0][0].shape[0]
    n = len(items)

    def body(*refs):
        scr = refs[-1]
        for x_ref, o_ref, (_, _, rate) in zip(refs[:n], refs[n:2 * n], items):
            for ci in range(CHUNKS):
                scr[ci] = x_ref[:, ci * LANES:(ci + 1) * LANES].astype(F32)
            _regroup_out(scr, 0, o_ref, rate, PERM_TILE)

    return pl.pallas_call(
        body, grid=(t // PERM_TILE,),
        in_specs=[pl.BlockSpec((PERM_TILE, B_WIDTH), functools.partial(lambda cb, i: (i, cb), cb)) for _, cb, _ in items],
        out_specs=[_regrouped_spec(rate, PERM_TILE) for _, _, rate in items],
        out_shape=[SDS((rate, t // rate, B_WIDTH), a.dtype) for a, _, rate in items],
        scratch_shapes=[pltpu.VMEM((CHUNKS, PERM_TILE, LANES), F32)],
        name=name, compiler_params=_params())(*[a for a, _, _ in items])


def _head_lane_mask(h):
    lane = lax.broadcasted_iota(jnp.int32, (1, LANES), 1)
    return (lane < HEAD_DIM) if h == 0 else (lane >= HEAD_DIM)


def _attn_fwd(name, q, k, v, ride=None):
    rate, length = q[0].shape[0], q[0].shape[1]
    nb = length // ATT_BLOCK
    scale = HEAD_DIM ** -0.5

    def body(q_ref, kc_ref, kp_ref, vc_ref, vp_ref, o_ref, l_ref):
        n = pl.program_id(1)
        qi = lax.broadcasted_iota(jnp.int32, (ATT_BLOCK, 2 * ATT_BLOCK), 0)
        cj = lax.broadcasted_iota(jnp.int32, (ATT_BLOCK, 2 * ATT_BLOCK), 1)
        has_prev = jnp.where(n > 0, 0, 2 * ATT_BLOCK)
        mask = ((cj < ATT_BLOCK) & (cj >= qi + has_prev)) | ((cj >= ATT_BLOCK) & (cj - ATT_BLOCK <= qi))
        heads = [(hp, h) for hp in range(CHUNKS) for h in range(2)]
        q2, k2, v2 = {}, {}, {}
        for hp in range(CHUNKS):
            ls = slice(hp * LANES, (hp + 1) * LANES)
            q2[hp] = q_ref[:, ls]
            k2[hp] = jnp.concatenate([kp_ref[:, ls], kc_ref[:, ls]], axis=0)
            v2[hp] = jnp.concatenate([vp_ref[:, ls], vc_ref[:, ls]], axis=0)
        scores = {}
        for hp, h in heads:
            scores[hp, h] = _dot(jnp.where(_head_lane_mask(h), q2[hp], jnp.zeros_like(q2[hp])), k2[hp], NT) * scale
        probs, lses = {}, {}
        for hp, h in heads:
            s = jnp.where(mask, scores[hp, h], NEG_INF)
            m = jnp.max(s, axis=1, keepdims=True)
            p = jnp.exp(s - m)
            den = jnp.sum(p, axis=1, keepdims=True)
            lses[hp, h] = m + jnp.log(den)
            probs[hp, h] = (p / den).astype(BF16)
        for hp in range(CHUNKS):
            ls = slice(hp * LANES, (hp + 1) * LANES)
            o_acc = None
            for h in range(2):
                o = _dot(probs[hp, h], jnp.where(_head_lane_mask(h), v2[hp], jnp.zeros_like(v2[hp])), NN)
                o_acc = o if o_acc is None else o_acc + o
            o_ref[:, ls] = o_acc
            zeros = jnp.zeros((ATT_BLOCK, LANES), F32)
            l_ref[:, ls] = jnp.where(_head_lane_mask(1), lses[hp, 1] + zeros, lses[hp, 0] + zeros)

    def cur(cb):
        return pl.BlockSpec((None, ATT_BLOCK, B_WIDTH), lambda r, n: (r, n, cb))

    def prev(cb):
        return pl.BlockSpec((None, ATT_BLOCK, B_WIDTH), lambda r, n: (r, jnp.maximum(n - 1, 0), cb))

    out = pl.BlockSpec((None, ATT_BLOCK, B_WIDTH), lambda r, n: (r, n, 0))
    return _call(
        body, grid=(rate, nb),
        in_specs=[cur(q[1]), cur(k[1]), prev(k[1]), cur(v[1]), prev(v[1])],
        out_specs=[out, out], out_shape=[SDS((rate, length, B_WIDTH), F32)] * 2,
        operands=[q[0], k[0], k[0], v[0], v[0]], name=name, ride=ride)


def _attn_merge(a_out, o_list, l_list):
    t = a_out.shape[0]
    tm = PERM_TILE
    r1, r2 = DIL_RATES[1], DIL_RATES[2]

    def body(a_ref, o0, o1, o2, l0, l1, l2, cat_ref, lt_ref, lt1_ref, lt2_ref, so1, so2, sl1, sl2, slt):
        _regroup_in(o1, so1, r1, tm)
        _regroup_in(l1, sl1, r1, tm)
        _regroup_in(o2, so2, r2, tm)
        _regroup_in(l2, sl2, r2, tm)
        cat_ref[:, :A_WIDTH] = a_ref[...]
        for c in range(CHUNKS):
            ls = slice(c * LANES, (c + 1) * LANES)
            lg = [l0[:, ls], sl1[c], sl2[c]]
            m = jnp.maximum(jnp.maximum(lg[0], lg[1]), lg[2])
            es = [jnp.exp(l - m) for l in lg]
            den = es[0] + es[1] + es[2]
            b = (es[0] * o0[:, ls] + es[1] * so1[c] + es[2] * so2[c]) / den
            cat_ref[:, A_WIDTH + c * LANES:A_WIDTH + (c + 1) * LANES] = b.astype(BF16)
            lt = m + jnp.log(den)
            lt_ref[:, ls] = lt
            slt[c] = lt
        _regroup_out(slt, 0, lt1_ref, r1, tm)
        _regroup_out(slt, 0, lt2_ref, r2, tm)

    blk = _row_spec(tm, B_WIDTH)
    g1, g2 = _regrouped_spec(r1, tm), _regrouped_spec(r2, tm)
    return pl.pallas_call(
        body, grid=(t // tm,), in_specs=[blk, blk, g1, g2, blk, g1, g2],
        out_specs=[_row_spec(tm, A_WIDTH + B_WIDTH), blk, g1, g2],
        out_shape=[SDS((t, A_WIDTH + B_WIDTH), BF16), SDS((t, B_WIDTH), F32), SDS((r1, t // r1, B_WIDTH), F32),
                   SDS((r2, t // r2, B_WIDTH), F32)],
        scratch_shapes=[pltpu.VMEM((CHUNKS, tm, LANES), F32)] * 5,
        name="attn_merge", compiler_params=_params())(a_out, *o_list, *l_list)


def _attn_bwd_prep(dcat, cat):
    t = dcat.shape[0]
    tm = PERM_TILE
    r1, r2 = DIL_RATES[1], DIL_RATES[2]

    def body(d_ref, b_ref, db_ref, dd_ref, db1_ref, dd1_ref, db2_ref, dd2_ref, sdb, sdd):
        seg = _segment_mean_matrix(HEAD_DIM, scale=1.0)
        for c in range(CHUNKS):
            ls = slice(c * LANES, (c + 1) * LANES)
            d = d_ref[:, ls]
            dsum = _segment_dot(d * b_ref[:, ls].astype(F32), seg)
            db_ref[:, ls] = d.astype(BF16)
            dd_ref[:, ls] = dsum
            sdb[c] = d
            sdd[c] = dsum
        _regroup_out(sdb, 0, db1_ref, r1, tm)
        _regroup_out(sdd, 0, dd1_ref, r1, tm)
        _regroup_out(sdb, 0, db2_ref, r2, tm)
        _regroup_out(sdd, 0, dd2_ref, r2, tm)

    right = pl.BlockSpec((tm, B_WIDTH), lambda i: (i, 1))
    blk = _row_spec(tm, B_WIDTH)
    g1, g2 = _regrouped_spec(r1, tm), _regrouped_spec(r2, tm)
    return pl.pallas_call(
        body, grid=(t // tm,), in_specs=[right, right], out_specs=[blk, blk, g1, g1, g2, g2],
        out_shape=[SDS((t, B_WIDTH), BF16), SDS((t, B_WIDTH), F32), SDS((r1, t // r1, B_WIDTH), BF16),
                   SDS((r1, t // r1, B_WIDTH), F32), SDS((r2, t // r2, B_WIDTH), BF16), SDS((r2, t // r2, B_WIDTH), F32)],
        scratch_shapes=[pltpu.VMEM((CHUNKS, tm, LANES), F32)] * 2,
        name="attn_bwd_prep", compiler_params=_params())(dcat, cat)


def _attn_bwd(name, q, k, v, db, lse, dd, ride=None):
    rate, length = db.shape[0], db.shape[1]
    nb = length // ATT_BLOCK
    scale = HEAD_DIM ** -0.5

    def body(qa_ref, qb_ref, k_ref, v_ref, dba_ref, dbb_ref, la_ref, lb_ref, da_ref, dbd_ref, dq_ref, dk_ref, dv_ref, carry):
        m = pl.program_id(1)

        @pl.when(m == 0)
        def _():
            carry[...] = jnp.zeros_like(carry)

        row = lax.broadcasted_iota(jnp.int32, (2 * ATT_BLOCK, ATT_BLOCK), 0)
        kj = lax.broadcasted_iota(jnp.int32, (2 * ATT_BLOCK, ATT_BLOCK), 1)
        no_next = jnp.where(m + 1 < nb, 0, 2 * ATT_BLOCK)
        mask = ((row < ATT_BLOCK) & (kj <= row)) | ((row >= ATT_BLOCK) & (kj >= row - ATT_BLOCK + no_next))
        heads = [(hp, h) for hp in range(CHUNKS) for h in range(2)]
        q2, db2, lse2, dd2, k2, v2 = {}, {}, {}, {}, {}, {}
        for hp in range(CHUNKS):
            ls = slice(hp * LANES, (hp + 1) * LANES)
            k2[hp], v2[hp] = k_ref[:, ls], v_ref[:, ls]
            q2[hp] = jnp.concatenate([qa_ref[:, ls], qb_ref[:, ls]], axis=0)
            db2[hp] = jnp.concatenate([dba_ref[:, ls], dbb_ref[:, ls]], axis=0)
            lse2[hp] = jnp.concatenate([la_ref[:, ls], lb_ref[:, ls]], axis=0)
            dd2[hp] = jnp.concatenate([da_ref[:, ls], dbd_ref[:, ls]], axis=0)
        km, scores, dps = {}, {}, {}
        for hp, h in heads:
            hm = _head_lane_mask(h)
            km[hp, h] = jnp.where(hm, k2[hp], jnp.zeros_like(k2[hp]))
            scores[hp, h] = _dot(q2[hp], km[hp, h], NT) * scale
            dps[hp, h] = _dot(db2[hp], jnp.where(hm, v2[hp], jnp.zeros_like(v2[hp])), NT)
        probs, dss = {}, {}
        for hp, h in heads:
            hm = _head_lane_mask(h)
            lse_col = jnp.max(jnp.where(hm, lse2[hp], NEG_INF), axis=1, keepdims=True)
            dd_col = jnp.max(jnp.where(hm, dd2[hp], NEG_INF), axis=1, keepdims=True)
            p = jnp.where(mask, jnp.exp(scores[hp, h] - lse_col), 0.0)
            probs[hp, h] = p.astype(BF16)
            dss[hp, h] = (p * (dps[hp, h] - dd_col) * scale).astype(BF16)
        for hp in range(CHUNKS):
            ls = slice(hp * LANES, (hp + 1) * LANES)
            dq_acc, dk_acc, dv_acc = None, None, None
            for h in range(2):
                hm = _head_lane_mask(h)
                dvc = _dot(probs[hp, h], jnp.where(hm, db2[hp], jnp.zeros_like(db2[hp])), TN)
                dqc = _dot(dss[hp, h], km[hp, h], NN)
                dkc = _dot(dss[hp, h], jnp.where(hm, q2[hp], jnp.zeros_like(q2[hp])), TN)
                dq_acc = dqc if dq_acc is None else dq_acc + dqc
                dk_acc = dkc if dk_acc is None else dk_acc + dkc
                dv_acc = dvc if dv_acc is None else dv_acc + dvc
            dq_ref[:, ls] = (dq_acc[:ATT_BLOCK] + carry[:, ls]).astype(BF16)
            carry[:, ls] = dq_acc[ATT_BLOCK:]
            dk_ref[:, ls] = dk_acc.astype(BF16)
            dv_ref[:, ls] = dv_acc.astype(BF16)

    def cur(cb):
        return pl.BlockSpec((None, ATT_BLOCK, B_WIDTH), lambda r, n: (r, n, cb))

    def nxt(cb):
        return pl.BlockSpec((None, ATT_BLOCK, B_WIDTH), lambda r, n: (r, jnp.minimum(n + 1, nb - 1), cb))

    out = cur(0)
    return _call(
        body, grid=(rate, nb),
        in_specs=[cur(q[1]), nxt(q[1]), cur(k[1]), cur(v[1]), cur(0), nxt(0), cur(0), nxt(0), cur(0), nxt(0)],
        out_specs=[out, out, out], out_shape=[SDS((rate, length, B_WIDTH), BF16)] * 3,
        scratch_shapes=[pltpu.VMEM((ATT_BLOCK, B_WIDTH), F32)],
        operands=[q[0], q[0], k[0], v[0], db, db, lse, lse, dd, dd], name=name, ride=ride)


AB_IN = 2 * A_WIDTH + 3 * N_DIL * B_WIDTH
ASM_TILE = 256


def _dproj_assemble(proj, d_a, dq, dk, dv, gains, tabs, ride=None):
    t = proj.shape[0]
    n_in = 3 * N_DIL

    def body(p_ref, da_ref, *rest):
        grads = rest[:n_in]
        g_ref, c_ref, s1_ref, s2_ref, o_ref, dg_ref = rest[n_in:n_in + 6]
        scratch = rest[n_in + 6:]

        @pl.when(pl.program_id(0) == 0)
        def _():
            dg_ref[...] = jnp.zeros_like(dg_ref)

        chunk = {}
        k_scr = 0
        for j in range(n_in):
            g = j % N_DIL
            if DIL_RATES[g] == 1:
                for ci in range(CHUNKS):
                    chunk[j, ci] = functools.partial(lambda r, ci: r[:, ci * LANES:(ci + 1) * LANES].astype(F32), grads[j], ci)
            else:
                scr = scratch[k_scr]
                k_scr += 1
                _regroup_in(grads[j], scr, DIL_RATES[g], ASM_TILE)
                for ci in range(CHUNKS):
                    chunk[j, ci] = functools.partial(lambda s, ci: s[ci], scr, ci)

        seg = _segment_mean_matrix(HEAD_DIM)
        c, s1, s2 = c_ref[...], s1_ref[...], s2_ref[...]
        o_ref[:, :2 * A_WIDTH] = da_ref[...]
        for jg in range(2 * N_DIL):
            for ci in range(CHUNKS):
                col = jg * B_WIDTH + ci * LANES
                src = slice(2 * A_WIDTH + col, 2 * A_WIDTH + col + LANES)
                xv = p_ref[:, src].astype(F32)
                r = lax.rsqrt(_segment_dot(xv * xv, seg) + EPS)
                xh = xv * r
                gain = g_ref[:, col:col + LANES]
                do = chunk[jg, ci]()
                dy = do * c + pltpu.roll(do * s1, 8, axis=1) + pltpu.roll(do * s2, LANES - 8, axis=1)
                dg_ref[:, col:col + LANES] += jnp.sum(dy * xh, axis=0, keepdims=True)
                dxh = dy * gain
                o_ref[:, src] = (r * (dxh - xh * _segment_dot(dxh * xh, seg))).astype(BF16)
        v0 = 2 * A_WIDTH + QK_COLS
        for g in range(N_DIL):
            for ci in range(CHUNKS):
                col = v0 + g * B_WIDTH + ci * LANES
                o_ref[:, col:col + LANES] = chunk[2 * N_DIL + g, ci]().astype(BF16)

    specs = [_row_spec(ASM_TILE, B_WIDTH) if r == 1 else _regrouped_spec(r, ASM_TILE) for r in DIL_RATES] * 3
    n_scr = 3 * sum(1 for r in DIL_RATES if r > 1)
    tab = _row_spec(ASM_TILE, LANES)
    return _call(
        body, grid=(t // ASM_TILE,),
        in_specs=[_row_spec(ASM_TILE, AB_IN), _row_spec(ASM_TILE, 2 * A_WIDTH)] + specs
        + [_const_spec((1, QK_COLS)), tab, tab, tab],
        out_specs=[_row_spec(ASM_TILE, AB_IN), _const_spec((1, QK_COLS))],
        out_shape=[SDS((t, AB_IN), BF16), SDS((1, QK_COLS), F32)],
        scratch_shapes=[pltpu.VMEM((CHUNKS, ASM_TILE, LANES), F32)] * n_scr,
        operands=[proj, d_a, *dq, *dk, *dv, gains, *tabs], name="dproj_assemble", ride=ride)


def _fold_heads(dg_lane):
    n = dg_lane.shape[1]

    def body(x_ref, o_ref):
        r = lax.broadcasted_iota(jnp.int32, (B_WIDTH, B_WIDTH), 0) % HEAD_DIM
        c = lax.broadcasted_iota(jnp.int32, (B_WIDTH, B_WIDTH), 1) % HEAD_DIM
        fold = jnp.where(r == c, 1.0, 0.0).astype(F32)
        for jg in range(n // B_WIDTH):
            ls = slice(jg * B_WIDTH, (jg + 1) * B_WIDTH)
            o_ref[:, ls] = _dot_hi(jnp.broadcast_to(x_ref[:, ls], (8, B_WIDTH)), fold)

    return pl.pallas_call(body, out_shape=SDS((8, n), F32), name="fold_heads", compiler_params=_params())(dg_lane)


CD_TILE = 256
TAP_ROWS = 64
CD_IN = 2 * C_WIDTH + 3 * 512


def _shifted_copies(src, dst, rows):
    dst[0, :rows] = src[...]
    for b in range(1, 8):
        dst[b, :rows - 8] = src[pl.ds(b, rows - 8), :]


def _rows_from(shifted, start, n, lanes=slice(None)):
    b = start % 8
    return shifted[b, pl.ds(start - b, n), lanes]


def _mixer_cd_fwd(proj, cw, cb, lg, lb, dw):
    t = proj.shape[0]
    per = CD_TILE // HALO

    def body(h_ref, m_ref, cw_ref, cb_ref, lg_ref, lb_ref, dw_ref, o_ref, c1_ref, c_scr, e_scr, c_sh):
        not_first = (pl.program_id(0) > 0).astype(F32)
        lanes = [slice(c * LANES, (c + 1) * LANES) for c in range(C_WIDTH // LANES)]

        def col(ref, part, ls):
            return ref[:, part * C_WIDTH + ls.start:part * C_WIDTH + ls.stop].astype(F32)

        for ls in lanes:
            c_scr[:HALO, ls] = col(h_ref, 0, ls) * _sigmoid(col(h_ref, 1, ls)) * not_first
            c_scr[HALO:, ls] = col(m_ref, 0, ls) * _sigmoid(col(m_ref, 1, ls))
            e_scr[:HALO, ls] = col(h_ref, 3, ls) * col(h_ref, 4, ls) * not_first
            e_scr[HALO:, ls] = col(m_ref, 3, ls) * col(m_ref, 4, ls)
        _shifted_copies(c_scr, c_sh, HALO + CD_TILE)
        for ls in lanes:
            for r0 in range(0, CD_TILE, TAP_ROWS):
                acc = jnp.zeros((TAP_ROWS, LANES), F32)
                for k in range(C_KERNEL):
                    acc = acc + cw_ref[k:k + 1, ls] * _rows_from(c_sh, r0 + HALO - (C_KERNEL - 1) + k, TAP_ROWS, ls)
                c1_ref[r0:r0 + TAP_ROWS, ls] = acc + cb_ref[:, ls]
        mean = sum(jnp.sum(c1_ref[:, ls], axis=-1, keepdims=True) for ls in lanes) * (1.0 / C_WIDTH)
        var = sum(jnp.sum((c1_ref[:, ls] - mean) ** 2, axis=-1, keepdims=True) for ls in lanes) * (1.0 / C_WIDTH)
        rs = lax.rsqrt(var + EPS)
        for ls in lanes:
            c2 = (c1_ref[:, ls] - mean) * rs * lg_ref[:, ls] + lb_ref[:, ls]
            o_ref[:, ls] = (c2 * _sigmoid(c2)).astype(BF16)
            d1 = jnp.zeros((CD_TILE, LANES), F32)
            for k in range(D_KERNEL):
                d1 = d1 + dw_ref[k:k + 1, ls] * e_scr[pl.ds(HALO - (D_KERNEL - 1) + k, CD_TILE), ls]
            o_ref[:, C_WIDTH + ls.start:C_WIDTH + ls.stop] = (col(m_ref, 2, ls) * d1).astype(BF16)

    return pl.pallas_call(
        body, grid=(t // CD_TILE,),
        in_specs=[pl.BlockSpec((HALO, CD_IN), lambda i: (jnp.maximum(i * per - 1, 0), 0)), _row_spec(CD_TILE, CD_IN),
                  _const_spec((32, C_WIDTH)), _const_spec((1, C_WIDTH)), _const_spec((1, C_WIDTH)), _const_spec((1, C_WIDTH)),
                  _const_spec((8, C_WIDTH))],
        out_specs=[_row_spec(CD_TILE, 2 * C_WIDTH), _row_spec(CD_TILE, C_WIDTH)],
        out_shape=[SDS((t, 2 * C_WIDTH), BF16), SDS((t, C_WIDTH), F32)],
        scratch_shapes=[pltpu.VMEM((HALO + CD_TILE, C_WIDTH), F32)] * 2 + [pltpu.VMEM((8, HALO + CD_TILE, C_WIDTH), F32)],
        name="mixer_cd_fwd", compiler_params=_params())(proj, proj, cw, cb, lg, lb, dw)


def _mixer_cd_bwd(proj, dcat, c1, cw, lg, lb, dw, ride=None):
    t = proj.shape[0]
    per = CD_TILE // HALO
    nt = t // CD_TILE
    ext = CD_TILE + HALO

    def body(hp_ref, m_ref, hn_ref, dm_ref, dn_ref, c1m_ref, c1n_ref, cw_ref, lg_ref, lb_ref, dw_ref,
             dp_ref, dcw_ref, dcb_ref, dlg_ref, dlb_ref, ddw_ref, c_scr, e_scr, dc1_scr, dd1_scr, c_sh, dc1_sh, dcw_acc,
             dvh_scr, vhat_scr):
        i = pl.program_id(0)

        @pl.when(i == 0)
        def _():
            for r in (dcw_acc, dcb_ref, dlg_ref, dlb_ref, ddw_ref):
                r[...] = jnp.zeros_like(r)

        not_first = (i > 0).astype(F32)
        not_last = (i < nt - 1).astype(F32)
        main = slice(HALO, HALO + CD_TILE)
        lanes = [slice(c * LANES, (c + 1) * LANES) for c in range(C_WIDTH // LANES)]

        def col(ref, part, ls):
            return ref[:, part * C_WIDTH + ls.start:part * C_WIDTH + ls.stop].astype(F32)

        for ls in lanes:
            c_scr[:HALO, ls] = col(hp_ref, 0, ls) * _sigmoid(col(hp_ref, 1, ls)) * not_first
            c_scr[main, ls] = col(m_ref, 0, ls) * _sigmoid(col(m_ref, 1, ls))
            c_scr[HALO + CD_TILE:, ls] = col(hn_ref, 0, ls) * _sigmoid(col(hn_ref, 1, ls)) * not_last
            e_scr[:HALO, ls] = col(hp_ref, 3, ls) * col(hp_ref, 4, ls) * not_first
            e_scr[main, ls] = col(m_ref, 3, ls) * col(m_ref, 4, ls)
            e_scr[HALO + CD_TILE:, ls] = col(hn_ref, 3, ls) * col(hn_ref, 4, ls) * not_last
        _shifted_copies(c_scr, c_sh, 2 * HALO + CD_TILE)

        def c1_of(ls):
            return jnp.concatenate([c1m_ref[:, ls], c1n_ref[:, ls]], axis=0)

        mean = sum(jnp.sum(c1_of(ls), axis=-1, keepdims=True) for ls in lanes) * (1.0 / C_WIDTH)
        var = sum(jnp.sum((c1_of(ls) - mean) ** 2, axis=-1, keepdims=True) for ls in lanes) * (1.0 / C_WIDTH)
        rs = lax.rsqrt(var + EPS)
        sum_dvh, sum_dvh_vhat = 0.0, 0.0
        for ls in lanes:
            vhat = (c1_of(ls) - mean) * rs
            c2 = vhat * lg_ref[:, ls] + lb_ref[:, ls]
            sig = _sigmoid(c2)
            dc = jnp.concatenate([dm_ref[:, ls], dn_ref[:, ls] * not_last], axis=0)
            dc2 = dc * (sig * (1.0 + c2 * (1.0 - sig)))
            dvh = dc2 * lg_ref[:, ls]
            sum_dvh = sum_dvh + jnp.sum(dvh, axis=-1, keepdims=True)
            sum_dvh_vhat = sum_dvh_vhat + jnp.sum(dvh * vhat, axis=-1, keepdims=True)
            dvh_scr[:, ls] = dvh
            vhat_scr[:, ls] = vhat
            dlg_ref[:, ls] += jnp.sum((dc2 * vhat)[:CD_TILE], axis=0, keepdims=True)
            dlb_ref[:, ls] += jnp.sum(dc2[:CD_TILE], axis=0, keepdims=True)
        for ls in lanes:
            dc1 = rs * (dvh_scr[:, ls] - sum_dvh * (1.0 / C_WIDTH) - vhat_scr[:, ls] * (sum_dvh_vhat * (1.0 / C_WIDTH)))
            dc1_scr[:, ls] = dc1
            dcb_ref[:, ls] += jnp.sum(dc1[:CD_TILE], axis=0, keepdims=True)
        _shifted_copies(dc1_scr, dc1_sh, ext)
        for ls in lanes:
            for r0 in range(0, CD_TILE, TAP_ROWS):
                rows = slice(r0, r0 + TAP_ROWS)
                dc1_m = dc1_scr[rows, ls]
                dc0 = jnp.zeros((TAP_ROWS, LANES), F32)
                for k in range(C_KERNEL):
                    dc0 = dc0 + cw_ref[k:k + 1, ls] * _rows_from(dc1_sh, r0 + C_KERNEL - 1 - k, TAP_ROWS, ls)
                    prod = dc1_m * _rows_from(c_sh, r0 + HALO - (C_KERNEL - 1) + k, TAP_ROWS, ls)
                    dcw_acc[k, :, ls] += prod.reshape(TAP_ROWS // 8, 8, LANES).sum(axis=0)
                g_m = m_ref[rows, C_WIDTH + ls.start:C_WIDTH + ls.stop].astype(F32)
                a_m = m_ref[rows, ls].astype(F32)
                sig_m = _sigmoid(g_m)
                dp_ref[rows, ls] = (dc0 * sig_m).astype(BF16)
                dp_ref[rows, C_WIDTH + ls.start:C_WIDTH + ls.stop] = (dc0 * a_m * sig_m * (1.0 - sig_m)).astype(BF16)

        @pl.when(i == nt - 1)
        def _():
            dcw_ref[...] = jnp.sum(dcw_acc[...], axis=1)

        for ls in lanes:
            wide = slice(C_WIDTH + ls.start, C_WIDTH + ls.stop)
            d1 = jnp.zeros((CD_TILE, LANES), F32)
            for k in range(D_KERNEL):
                d1 = d1 + dw_ref[k:k + 1, ls] * e_scr[pl.ds(HALO - (D_KERNEL - 1) + k, CD_TILE), ls]
            dd_m = dm_ref[:, wide]
            dd1 = jnp.concatenate([dd_m * col(m_ref, 2, ls), dn_ref[:, wide] * col(hn_ref, 2, ls) * not_last], axis=0)
            dd1_scr[:, ls] = dd1
            dp_ref[:, 2 * C_WIDTH + ls.start:2 * C_WIDTH + ls.stop] = (dd_m * d1).astype(BF16)
            de = jnp.zeros((CD_TILE, LANES), F32)
            for k in range(D_KERNEL):
                de = de + dw_ref[k:k + 1, ls] * dd1_scr[pl.ds(D_KERNEL - 1 - k, CD_TILE), ls]
                ddw_ref[k:k + 1, ls] += jnp.sum(dd1[:CD_TILE] * e_scr[pl.ds(HALO - (D_KERNEL - 1) + k, CD_TILE), ls], axis=0, keepdims=True)
            dp_ref[:, 3 * C_WIDTH + ls.start:3 * C_WIDTH + ls.stop] = (de * col(m_ref, 4, ls)).astype(BF16)
            dp_ref[:, 4 * C_WIDTH + ls.start:4 * C_WIDTH + ls.stop] = (de * col(m_ref, 3, ls)).astype(BF16)

    halo_prev = lambda i: (jnp.maximum(i * per - 1, 0), 0)
    halo_next = lambda i: (jnp.minimum((i + 1) * per, t // HALO - 1), 0)
    vec = _const_spec((1, C_WIDTH))
    return _call(
        body, grid=(nt,),
        in_specs=[pl.BlockSpec((HALO, CD_IN), halo_prev), _row_spec(CD_TILE, CD_IN), pl.BlockSpec((HALO, CD_IN), halo_next),
                  _row_spec(CD_TILE, 2 * C_WIDTH), pl.BlockSpec((HALO, 2 * C_WIDTH), halo_next),
                  _row_spec(CD_TILE, C_WIDTH), pl.BlockSpec((HALO, C_WIDTH), halo_next),
                  _const_spec((32, C_WIDTH)), vec, vec, _const_spec((8, C_WIDTH))],
        out_specs=[_row_spec(CD_TILE, CD_IN), _const_spec((32, C_WIDTH)), vec, vec, vec, _const_spec((8, C_WIDTH))],
        out_shape=[SDS((t, CD_IN), BF16), SDS((32, C_WIDTH), F32), SDS((1, C_WIDTH), F32), SDS((1, C_WIDTH), F32),
                   SDS((1, C_WIDTH), F32), SDS((8, C_WIDTH), F32)],
        scratch_shapes=[pltpu.VMEM((2 * HALO + CD_TILE, C_WIDTH), F32)] * 2 + [pltpu.VMEM((ext, C_WIDTH), F32)] * 2
        + [pltpu.VMEM((8, 2 * HALO + CD_TILE, C_WIDTH), F32), pltpu.VMEM((8, ext, C_WIDTH), F32),
           pltpu.VMEM((32, 8, C_WIDTH), F32)] + [pltpu.VMEM((ext, C_WIDTH), F32)] * 2,
        operands=[proj, proj, proj, dcat, dcat, c1, c1, cw, lg, lb, dw], name="mixer_cd_bwd", ride=ride)


def _wgrad(name, pairs, out_rc, t, ride):
    tk = TILES["wgrad"]
    assert tk == t, "the whole contraction has to fit one grid step"
    r, c = out_rc
    n = len(pairs)

    operands, in_specs, where = [], [], []
    for lhs, lhs_spec, rhs, rhs_spec in pairs:
        at = []
        for array, spec in ((lhs, lhs_spec), (rhs, rhs_spec)):
            seen = [k for k, o in enumerate(operands) if o is array]
            if not seen:
                operands.append(array)
                in_specs.append(spec)
            at.append(seen[0] if seen else len(operands) - 1)
        where.append(at)
    n_in = len(operands)

    def body(*refs):
        ins, out_refs = refs[:n_in], refs[n_in:]
        for j, (lhs_at, rhs_at) in enumerate(where):
            out_refs[j][...] = _dot(ins[lhs_at][...], ins[rhs_at][...], TN).astype(BF16)

    res = _call(body, grid=(N_CHIPS, t // tk), in_specs=in_specs,
                out_specs=[pl.BlockSpec((None, r, c), lambda p, k: (p, 0, 0))] * n,
                out_shape=[SDS((N_CHIPS, r, c), BF16)] * n, operands=operands, name=name, ride=ride)
    outs, ride_res = (res, None) if ride is None else res
    outs = [o.reshape(N_CHIPS, 2, r // 2, c) for o in outs]
    return outs if ride is None else (outs, ride_res)


def _wgrad_col_sharded(name, h, dz_list, three_d, ride=None):
    t, d = h.shape
    tk = TILES["wgrad"]
    n4 = dz_list[0].shape[-1] if three_d else dz_list[0].shape[-1] // N_CHIPS
    hs = pl.BlockSpec((tk, d), lambda p, k: (k, 0))
    zs = pl.BlockSpec((None, tk, n4), lambda p, k: (p, k, 0)) if three_d else pl.BlockSpec((tk, n4), lambda p, k: (k, p))
    return _wgrad(name, [(h, hs, dz, zs) for dz in dz_list], (d, n4), t, ride)


def _wgrad_row_sharded(name, a, g, three_d, ride=None):
    many = isinstance(a, (list, tuple))
    a_list = list(a) if many else [a]
    t, d = g.shape
    tk = TILES["wgrad"]
    k4 = a_list[0].shape[-1] if three_d else a_list[0].shape[-1] // N_CHIPS
    a_spec = pl.BlockSpec((None, tk, k4), lambda p, k: (p, k, 0)) if three_d else pl.BlockSpec((tk, k4), lambda p, k: (k, p))
    gs = pl.BlockSpec((tk, d), lambda p, k: (k, 0))
    res = _wgrad(name, [(a_j, a_spec, g, gs) for a_j in a_list], (k4, d), t, ride)
    if many:
        return res
    return res[0] if ride is None else (res[0][0], res[1])


def _mesh_scalars():
    return jnp.stack([lax.axis_index("c"), 2 * lax.axis_index("x") + lax.axis_index("y")]).astype(jnp.int32)


def _stage_own(name, items):
    n = len(items)
    views, in_specs, out_specs, out_shape = [], [], [], []
    for w, layer, dtype in items:
        layers, r, cols = w.shape
        h = r // 2
        views.append(w.reshape(2 * layers, h, cols))
        in_specs.append(pl.BlockSpec((None, h, cols), functools.partial(lambda l, i, s: (2 * l + i, 0, 0), layer)))
        out_specs.append(pl.BlockSpec((None, None, h, cols), lambda i, s: (s[1], i, 0, 0)))
        out_shape.append(SDS((N_CHIPS, 2, h, cols), dtype))
    out_specs.append(ANY)
    out_shape.append(SDS(out_shape[0].shape[1:], out_shape[0].dtype))

    def body(s_ref, *refs):
        for x_ref, o_ref in zip(refs[:n], refs[n:2 * n]):
            o_ref[...] = x_ref[...].astype(o_ref.dtype)

    return pl.pallas_call(
        body,
        grid_spec=pltpu.PrefetchScalarGridSpec(num_scalar_prefetch=1, grid=(2,), in_specs=in_specs, out_specs=out_specs),
        out_shape=out_shape, name=name, compiler_params=_params())(_mesh_scalars(), *views)


STAGE_STEPS = 4


def _stage_rest_and_norm(x, g, weights, ride=None):
    t, d = x.shape
    n = len(weights)
    views, in_specs, out_specs, out_shapes = [], [], [], []
    for w, layer in weights:
        layers, r, cols = w.shape
        sub = r // STAGE_STEPS
        views.append(w.reshape(layers * STAGE_STEPS, sub, cols))
        in_specs.append(pl.BlockSpec((None, sub, cols), functools.partial(lambda l, i, s: (STAGE_STEPS * l + i, 0, 0), layer)))
        out_specs.append(pl.BlockSpec((None, None, sub, cols), lambda i, s: (s[1], i // 2, i % 2, 0)))
        out_shapes.append(SDS((N_CHIPS, 2, r // 2, cols), BF16))

    def body(s_ref, x_ref, g_ref, *rest):
        w_refs, h_ref, o_refs = rest[:n], rest[n], rest[n + 1:]
        h_ref[...] = _rms_rows(x_ref[...], g_ref[...]).astype(BF16)
        for w_ref, o_ref in zip(w_refs, o_refs):
            o_ref[...] = w_ref[...].astype(BF16)

    tm = t // STAGE_STEPS
    res = _call(
        body, grid=(STAGE_STEPS,), in_specs=[pl.BlockSpec((tm, d), lambda i, s: (i, 0)), pl.BlockSpec((1, d), lambda i, s: (0, 0))] + in_specs,
        out_specs=[pl.BlockSpec((tm, d), lambda i, s: (i, 0))] + out_specs, out_shape=[SDS((t, d), BF16)] + out_shapes,
        operands=[x, g] + views, name="stage_and_norm", ride=ride, prefetch=_mesh_scalars())
    outs, ride_res = (res, None) if ride is None else res
    result = (outs[0], list(outs[1:]))
    return result if ride is None else (result, ride_res)


def _remote(src, dst, send_sem, recv_sem, device):
    return pltpu.make_async_remote_copy(src, dst, send_sem, recv_sem, device_id=device, device_id_type=MESH)


ALL_PEERS = (0, 1, 2)
NEIGHBOURS = (0, 1)


def _ride_gather_send(bufs, peers=ALL_PEERS):
    n = len(bufs)

    def each(b, sems, act):
        send, recv = sems
        x, y, c, p, others = _position()
        for t in range(n):
            for j in peers:
                qx, qy = others[j]
                act(b[t].at[p, c], b[t].at[2 * qx + qy, c], send.at[t, j], recv.at[t, j], (qx, qy, c))

    def start(ins, b, new, sems):
        each(b, sems, lambda mine, landed, s, r, dev: _remote(mine, mine, s, r, dev).start())

    def finish(ins, b, new, sems):
        def act(mine, landed, s, r, dev):
            _remote(mine, mine, s, r, dev).wait_send()
            _remote(landed, landed, s, r, dev).wait_recv()
        each(b, sems, act)

    return _Ride([], bufs, [], [(n, 3), (n, 3)], start, finish, ["chips"])


def _ride_gather_pass(bufs, peers=ALL_PEERS):
    n = len(bufs)

    def each(b, sems, act):
        send, recv = sems
        x, y, c, p, others = _position()
        for t in range(n):
            for j in peers:
                qx, qy = others[j]
                act(b[t].at[2 * qx + qy, c], b[t].at[2 * qx + qy, 1 - c], send.at[t, j], recv.at[t, j], (x, y, 1 - c))

    def start(ins, b, new, sems):
        each(b, sems, lambda landed, passed, s, r, dev: _remote(landed, landed, s, r, dev).start())

    def finish(ins, b, new, sems):
        def act(landed, passed, s, r, dev):
            _remote(landed, landed, s, r, dev).wait_send()
            _remote(passed, passed, s, r, dev).wait_recv()
        each(b, sems, act)

    return _Ride([], bufs, [], [(n, 3), (n, 3)], start, finish, ["sibling"])


def _ride_gather(bufs, peers=ALL_PEERS):
    send, onward = _ride_gather_send(bufs, peers), _ride_gather_pass(bufs, peers)
    n_send = len(send.sem_shapes)

    def start(ins, b, new, sems):
        send.start(ins, b, new, sems[:n_send])

    def finish(ins, b, new, sems):
        send.finish(ins, b, new, sems[:n_send])
        onward.start(ins, b, new, sems[n_send:])
        onward.finish(ins, b, new, sems[n_send:])

    return _Ride([], bufs, [], send.sem_shapes + onward.sem_shapes, start, finish, ["sibling", "chips"])


def _ride_gather_far(sources, slabs):
    n = len(slabs)

    def hops(ins, b, sems):
        x, y, c, p, others = _position()
        qx, qy = others[2]
        for t in range(n):
            far = (ins[t].at[p, c], b[t].at[c], sems[0].at[t], sems[1].at[t], (qx, qy, c))
            onward = (b[t].at[c], b[t].at[1 - c], sems[2].at[t], sems[3].at[t], (x, y, 1 - c))
            yield far, onward

    def wait(mine, landed, s, r, dev):
        _remote(mine, mine, s, r, dev).wait_send()
        _remote(landed, landed, s, r, dev).wait_recv()

    def start(ins, b, new, sems):
        for (mine, landed, s, r, dev), _ in hops(ins, b, sems):
            _remote(mine, landed, s, r, dev).start()

    def finish(ins, b, new, sems):
        for far, _ in hops(ins, b, sems):
            wait(*far)
        for _, (landed, passed, s, r, dev) in hops(ins, b, sems):
            _remote(landed, landed, s, r, dev).start()
        for _, onward in hops(ins, b, sems):
            wait(*onward)

    return _Ride(sources, slabs, [], [(n,)] * 4, start, finish, ["sibling", "chips"])


def _ride_swap(tensors):
    n = len(tensors)

    def each(ins, new, sems, act):
        send, recv = sems
        x, y, c, _, _ = _position()
        for t in range(n):
            act(_remote(ins[t].at[:, 1 - c], new[t], send.at[t], recv.at[t], (x, y, 1 - c)))

    def start(ins, b, new, sems):
        each(ins, new, sems, lambda cp: cp.start())

    def finish(ins, b, new, sems):
        each(ins, new, sems, lambda cp: cp.wait())

    return _Ride(tensors, [], [SDS((s.shape[0],) + s.shape[2:], s.dtype) for s in tensors], [(n,), (n,)], start, finish,
                 ["sibling"])


def _ride_scatter(tensors, landing):
    n = len(tensors)

    def each(ins, b, sems, act):
        send, recv = sems
        x, y, c, p, others = _position()
        for t in range(n):
            for j, (qx, qy) in enumerate(others):
                q = 2 * qx + qy
                act(ins[t].at[q], b[t].at[p], b[t].at[q], send.at[t, j], recv.at[t, j], (qx, qy, c))

    def start(ins, b, new, sems):
        each(ins, b, sems, lambda src, dst, landed, s, r, dev: _remote(src, dst, s, r, dev).start())

    def finish(ins, b, new, sems):
        def act(src, dst, landed, s, r, dev):
            _remote(src, dst, s, r, dev).wait_send()
            _remote(landed, landed, s, r, dev).wait_recv()
        each(ins, b, sems, act)

    return _Ride(tensors, landing, [], [(n, 3), (n, 3)], start, finish, ["chips"])


def _ride_join(bufs):
    n = len(bufs)

    def each(b, sems, act):
        send, recv = sems
        x, y, c, _, _ = _position()
        for t in range(n):
            act(b[t].at[c], b[t].at[1 - c], send.at[t], recv.at[t], (x, y, 1 - c))

    def start(ins, b, new, sems):
        each(b, sems, lambda mine, theirs, s, r, dev: _remote(mine, mine, s, r, dev).start())

    def finish(ins, b, new, sems):
        def act(mine, theirs, s, r, dev):
            _remote(mine, mine, s, r, dev).wait_send()
            _remote(theirs, theirs, s, r, dev).wait_recv()
        each(b, sems, act)

    return _Ride([], bufs, [], [(n,), (n,)], start, finish, ["sibling"])


def _all_reduce_small(pack, ride=None):
    rows = pack.shape[0]
    n_dev = 2 * N_CHIPS
    n_rb = 0 if ride is None else len(ride.bufs)

    def body(x_ref, *rest):
        o_ref = rest[n_rb]
        r_bufs = rest[n_rb + 1:2 * n_rb + 1]
        land, send, recv = rest[2 * n_rb + 1:2 * n_rb + 4]
        r_sems = rest[2 * n_rb + 4:]
        if ride is not None:
            ride.start([], r_bufs, [], r_sems)
        x, y, c, p, _ = _position()
        me = 2 * p + c
        land[me] = x_ref[...]
        peers = [(dx, dy, dc) for dx in range(2) for dy in range(2) for dc in range(2) if (dx, dy, dc) != (0, 0, 0)]
        for j, (dx, dy, dc) in enumerate(peers):
            _remote(land.at[me], land.at[me], send.at[j], recv.at[j], (x ^ dx, y ^ dy, c ^ dc)).start()
        for j, (dx, dy, dc) in enumerate(peers):
            src = 4 * (x ^ dx) + 2 * (y ^ dy) + (c ^ dc)
            _remote(land.at[me], land.at[me], send.at[j], recv.at[j], (x ^ dx, y ^ dy, c ^ dc)).wait_send()
            _remote(land.at[src], land.at[src], send.at[j], recv.at[j], (x ^ dx, y ^ dy, c ^ dc)).wait_recv()
        acc = land[0]
        for dev in range(1, n_dev):
            acc = acc + land[dev]
        o_ref[...] = acc
        if ride is not None:
            ride.finish([], r_bufs, [], r_sems)

    bufs = [] if ride is None else ride.bufs
    sems = [] if ride is None else [pltpu.SemaphoreType.DMA(s) for s in ride.sem_shapes]
    res = pl.pallas_call(
        body, in_specs=[pl.BlockSpec(memory_space=pltpu.VMEM)] + [ANY] * n_rb,
        out_specs=[pl.BlockSpec(memory_space=pltpu.VMEM)] + [ANY] * n_rb,
        out_shape=[SDS((rows, LANES), F32)] + [SDS(b.shape, b.dtype) for b in bufs],
        scratch_shapes=[pltpu.VMEM((n_dev, rows, LANES), F32), pltpu.SemaphoreType.DMA((n_dev - 1,)),
                        pltpu.SemaphoreType.DMA((n_dev - 1,))] + sems,
        input_output_aliases={1 + j: 1 + j for j in range(n_rb)},
        name="all_reduce_small", compiler_params=_params())(pack, *bufs)
    return res[0], list(res[1:])


def _add_own_half(name, fulls, recvs, out_dtypes):
    n = len(fulls)
    in_specs, out_specs, out_shapes = [], [], []
    for full, dtype in zip(fulls, out_dtypes):
        n4, _, h, cols = full.shape
        in_specs += [pl.BlockSpec((None, None, h, cols), lambda q, s: (q, s[0], 0, 0)),
                     pl.BlockSpec((None, h, cols), lambda q, s: (q, 0, 0))]
        out_specs += [pl.BlockSpec((None, h, cols), lambda q, s: (q, 0, 0)),
                      pl.BlockSpec((None, h, cols), lambda q, s: (s[1], 0, 0))]
        out_shapes += [SDS((n4, h, cols), dtype)] * 2

    def body(s_ref, *refs):
        ins, outs = refs[:2 * n], refs[2 * n:]
        for j in range(n):
            o_ref, own_ref = outs[2 * j], outs[2 * j + 1]
            v = (ins[2 * j][...].astype(F32) + ins[2 * j + 1][...].astype(F32)).astype(o_ref.dtype)
            o_ref[...] = v

            @pl.when(pl.program_id(0) == s_ref[1])
            def _():
                own_ref[...] = v

    res = pl.pallas_call(
        body,
        grid_spec=pltpu.PrefetchScalarGridSpec(num_scalar_prefetch=1, grid=(N_CHIPS,), in_specs=in_specs, out_specs=out_specs),
        out_shape=out_shapes, name=name, compiler_params=_params())(
            _mesh_scalars(), *[a for pair in zip(fulls, recvs) for a in pair])
    return [(res[2 * j], res[2 * j + 1]) for j in range(n)]


def _sum_chips(name, parts_list):
    steps = 4 if all(parts.shape[1] % 64 == 0 for parts in parts_list) else 1
    in_specs, out_specs, out_shapes = [], [], []
    for parts in parts_list:
        n4, h, cols = parts.shape
        in_specs.append(pl.BlockSpec((n4, h // steps, cols), lambda i, s: (0, i, 0)))
        out_specs.append(pl.BlockSpec((None, h // steps, cols), lambda i, s: (s[0], i, 0)))
        out_shapes.append(SDS((2, h, cols), F32))
    n = len(parts_list)

    def body(s_ref, *refs):
        for a_ref, o_ref in zip(refs[:n], refs[n:]):
            acc = a_ref[0].astype(F32)
            for q in range(1, N_CHIPS):
                acc = acc + a_ref[q].astype(F32)
            o_ref[...] = acc

    return pl.pallas_call(
        body,
        grid_spec=pltpu.PrefetchScalarGridSpec(num_scalar_prefetch=1, grid=(steps,), in_specs=in_specs, out_specs=out_specs),
        out_shape=out_shapes, name=name, compiler_params=_params())(_mesh_scalars(), *parts_list)


def _adamw_math(w, g, m, v):
    m2 = ADAM_B1 * m + (1.0 - ADAM_B1) * g
    v2 = ADAM_B2 * v + (1.0 - ADAM_B2) * (g * g)
    m_hat = m2 / (1.0 - ADAM_B1 ** ADAM_STEP)
    v_hat = v2 / (1.0 - ADAM_B2 ** ADAM_STEP)
    delta = -ADAM_LR * (m_hat / (jnp.sqrt(v_hat) + ADAM_EPS) + ADAM_WD * w)
    return delta, m2, v2


ADAMW_STEPS = 8


def _adamw_big(name, ws, g_layers_list, ms, vs):
    layers = ws[0].shape[0]
    n, per_in = len(ws), 3 + layers
    in_specs, out_specs, out_shapes, operands = [], [], [], []
    for w, g_layers, m, v in zip(ws, g_layers_list, ms, vs):
        assert w.shape[0] == layers and w.shape[1] % (8 * ADAMW_STEPS) == 0
        _, rows, cols = w.shape
        tr = rows // ADAMW_STEPS
        blk = pl.BlockSpec((None, tr, cols), lambda l, i: (l, i, 0))
        in_specs += [blk] * 3 + [pl.BlockSpec((tr, cols), lambda l, i: (i, 0))] * layers
        out_specs += [blk] * 4
        out_shapes += [SDS((layers, rows, cols), F32)] * 4
        operands += [w, m, v] + [g.reshape(rows, cols) for g in g_layers]

    def body(*refs):
        ins, outs = refs[:n * per_in], refs[n * per_in:]
        for j in range(n):
            w_ref, m_ref, v_ref = ins[j * per_in:j * per_in + 3]
            g_refs = ins[j * per_in + 3:(j + 1) * per_in]
            g_o, d_o, m_o, v_o = outs[4 * j:4 * j + 4]
            gv = g_refs[0][...]
            for layer in range(1, layers):
                gv = jnp.where(pl.program_id(0) == layer, g_refs[layer][...], gv)
            d, mm, vv = _adamw_math(w_ref[...], gv, m_ref[...], v_ref[...])
            g_o[...] = gv
            d_o[...] = d
            m_o[...] = mm
            v_o[...] = vv

    res = pl.pallas_call(
        body, grid=(layers, ADAMW_STEPS), in_specs=in_specs, out_specs=out_specs, out_shape=out_shapes, name=name,
        compiler_params=_params())(*operands)
    return [tuple(res[4 * j:4 * j + 4]) for j in range(n)]


def _adamw_small(ws, gs, ms, vs):
    n = len(ws)
    flat = []
    for group in (ws, gs, ms, vs):
        flat += [a.reshape(-1, a.shape[-1]) for a in group]

    def body(*refs):
        w_r, g_r, m_r, v_r = refs[:n], refs[n:2 * n], refs[2 * n:3 * n], refs[3 * n:4 * n]
        d_o, m_o, v_o = refs[4 * n:5 * n], refs[5 * n:6 * n], refs[6 * n:7 * n]
        for j in range(n):
            d, mm, vv = _adamw_math(w_r[j][...], g_r[j][...], m_r[j][...], v_r[j][...])
            d_o[j][...] = d
            m_o[j][...] = mm
            v_o[j][...] = vv

    shapes = [SDS(a.shape, F32) for a in flat[:n]]
    outs = pl.pallas_call(body, out_shape=shapes * 3, name="adamw_small", compiler_params=_params())(*flat)
    res = []
    for k in range(3):
        res.append([outs[k * n + j].reshape(ws[j].shape) for j in range(n)])
    return res


BIG = ("ab_w_in", "ab_w_out", "cd_w_in", "cd_w_out", "ffn_w_gate", "ffn_w_up", "ffn_w_down")
BIG_BY_LAYERS = (BIG[:4], BIG[4:])
V_BLOCK = (2 * A_WIDTH + QK_COLS) // B_WIDTH


def _pad_rows(a, rows):
    return jnp.pad(a, ((0, rows - a.shape[0]), (0, 0)))


A_IN, A_OUT, C_IN, C_OUT = ("ab_w_in", 0), ("ab_w_out", 0), ("cd_w_in", 0), ("cd_w_out", 0)
G0, U0, D0 = ("ffn_w_gate", 0), ("ffn_w_up", 0), ("ffn_w_down", 0)
G1, U1, D1 = ("ffn_w_gate", 1), ("ffn_w_up", 1), ("ffn_w_down", 1)
UNITS = (A_IN, A_OUT, G0, U0, D0, C_IN, C_OUT, G1, U1, D1)
ROWS_MINOR = ("ffn_w_gate", "ffn_w_up")
SMALL_SHARDED = ("small", 0)
REPLICATED_UNIT = ("replicated", 0)


class _Exchange:
    def __init__(self, enabled):
        self.enabled = enabled
        self.w, self.grad, self.recv, self.half, self.land, self.done = {}, {}, {}, {}, {}, {}
        self.far = {}

    def full(self, unit):
        b = self.w[unit]
        return b.reshape(N_CHIPS, 1, 2 * b.shape[2], b.shape[3])

    def ride_for(self, phases):
        rides, sinks = [], []
        for kind, units in phases:
            if kind == "send":
                rides.append(_ride_gather_send([self.w[u] for u in units]))
                sinks.append(self.w)
            elif kind == "pass":
                rides.append(_ride_gather_pass([self.w[u] for u in units]))
                sinks.append(self.w)
            elif kind == "gather":
                rides.append(_ride_gather([self.w[u] for u in units]))
                sinks.append(self.w)
            elif kind == "gather_near":
                rides.append(_ride_gather([self.w[u] for u in units], NEIGHBOURS))
                sinks.append(self.w)
            elif kind == "gather_far":
                rides.append(_ride_gather_far([self.w[u] for u in units], [self.far[u] for u in units]))
                sinks.append(self.far)
            elif kind == "swap":
                rides.append(_ride_swap([self.grad[u] for u in units]))
                sinks.append(self.recv)
            elif kind == "scatter":
                rides.append(_ride_scatter([self.half[u] for u in units], [self.land[u] for u in units]))
                sinks.append(self.land)
            else:
                rides.append(_ride_join([self.done[u] for u in units]))
                sinks.append(self.done)
        ride = functools.reduce(_ride_both, rides)

        def settle(res):
            n_bufs = sum(len(r.bufs) for r in rides)
            bufs, new = list(res[:n_bufs]), list(res[n_bufs:])
            for r, sink, (_, units) in zip(rides, sinks, phases):
                vals = [bufs.pop(0) for _ in r.bufs] + [new.pop(0) for _ in r.new_outs]
                for u, v in zip(units, vals):
                    sink[u] = v

        return ride, settle

    def run(self, fn, *args, phases=(), **kw):
        if not self.enabled or not phases:
            return fn(*args, **kw)
        ride, settle = self.ride_for(phases)
        out, res = fn(*args, ride=ride, **kw)
        settle(res)
        return out

    def alone(self, name, phases):
        if self.enabled:
            ride, settle = self.ride_for(phases)
            settle(_run_ride(name, ride))

    def pair_sum(self, units):
        if self.enabled:
            dtypes = [F32 if u in (SMALL_SHARDED, REPLICATED_UNIT) else BF16 for u in units]
            res = _add_own_half(f"pair_sum_{units[0][0]}_{units[0][1]}", [self.grad[u] for u in units],
                                [self.recv[u] for u in units], dtypes)
            for u, (half, land) in zip(units, res):
                self.half[u], self.land[u] = half, land

    def chip_sum(self, units):
        if self.enabled:
            res = _sum_chips(f"chip_sum_{units[0][0]}_{units[0][1]}", [self.land[u] for u in units])
            self.done.update(zip(units, res))


def _local_step(x, target, ex, sp, h0=None):
    t, d = x.shape
    tabs = _rope_tables(t)
    gains = jnp.concatenate([jnp.tile(sp["q_norm_g"][g], HEAD_DIM // 8) for g in range(N_DIL)]
                            + [jnp.tile(sp["k_norm_g"][g], HEAD_DIM // 8) for g in range(N_DIL)]).reshape(1, QK_COLS)
    bias_t = sp["sgu_bias"].T
    cw = _pad_rows(sp["conv_c_w"], 32)
    dw = _pad_rows(sp["conv_d_w"], 8)
    cb, clg, clb = (sp[k].reshape(1, C_WIDTH) for k in ("conv_c_b", "c_ln_g", "c_ln_b"))
    slg, slb = sp["sgu_norm_g"].reshape(1, A_WIDTH), sp["sgu_norm_b"].reshape(1, A_WIDTH)
    g_ab, g_cd = sp["ab_norm_g"].reshape(1, d), sp["cd_norm_g"].reshape(1, d)
    g_f0, g_f1 = sp["ffn_norm_g"][0:1], sp["ffn_norm_g"][1:2]
    run = ex.run

    def w2d(unit):
        return ex.full(unit).reshape(-1, d)

    if h0 is None:
        h0 = _rms_fwd("rms_ab", x, g_ab)
    if ex.enabled:
        proj = run(_proj_in_near, "proj_ab_near", h0, ex.full(A_IN), phases=[("gather_far", [A_IN]), ("send", [A_OUT])])
        proj, ex.w[A_IN] = _proj_in_far("proj_ab_far", h0, ex.far[A_IN], proj, ex.w[A_IN])
    else:
        proj = _proj_in("proj_ab", h0, ex.full(A_IN), 0)
    a_out = _mixer_a_fwd(proj, slg, slb, sp["sgu_w"], bias_t)
    qk, q1, q2, k1, k2 = run(_qk_fwd, proj, gains, tabs, phases=[("pass", [A_OUT]), ("send", [G0, C_OUT])])
    regrouped_qk = {1: (q1, k1), 2: (q2, k2)}
    fwd_phases = ([("pass", [G0, C_OUT]), ("send", [U0])], [("pass", [U0]), ("send", [D0])],
                  [("pass", [D0]), ("send", [C_IN])])
    qkv, o_list, l_list = [], [], []
    dilated = [g for g, rate in enumerate(DIL_RATES) if rate != 1]
    regrouped_v = dict(zip(dilated, _permute("regroup_v", [(proj, V_BLOCK + g, DIL_RATES[g]) for g in dilated])))
    for g, rate in enumerate(DIL_RATES):
        if rate == 1:
            qk3, proj3 = qk.reshape(1, t, QK_COLS), proj.reshape(1, t, AB_IN)
            q, k, v = (qk3, g), (qk3, N_DIL + g), (proj3, V_BLOCK + g)
        else:
            q, k, v = (regrouped_qk[g][0], 0), (regrouped_qk[g][1], 0), (regrouped_v[g], 0)
        qkv.append((q, k, v))
        o, l = run(_attn_fwd, f"attn_fwd_{g}", q, k, v, phases=fwd_phases[g])
        if rate == 1:
            o, l = o.reshape(t, B_WIDTH), l.reshape(t, B_WIDTH)
        o_list.append(o)
        l_list.append(l)
    cat, lse_tot, lse_1, lse_2 = _attn_merge(a_out, o_list, l_list)
    x1, hf0 = _proj_out("out_ab", cat, w2d(A_OUT), x, g_next=g_f0)
    fgate0, fup0, act0 = run(_ffn_in, "ffn_in_0", hf0, ex.full(G0), ex.full(U0), 0,
                           phases=[("pass", [C_IN]), ("send", [D1, G1])])
    x2, h1 = run(_ffn_out, "ffn_out_0", act0, ex.full(D0), 0, x1, g_next=g_cd, phases=[("pass", [D1, G1]), ("send", [U1])])
    projcd = run(_proj_in, "proj_cd", h1, ex.full(C_IN), 0, phases=[("pass", [U1])])
    cat2, c1 = _mixer_cd_fwd(projcd, cw, cb, clg, clb, dw)
    x3, hf1 = _proj_out("out_cd", cat2, w2d(C_OUT), x2, g_next=g_f1)
    fgate1, fup1, act1 = _ffn_in("ffn_in_1", hf1, ex.full(G1), ex.full(U1), 0)
    dy, loss_acc, dy_b = _ffn_out("ffn_out_1", act1, ex.full(D1), 0, x3, target=target)
    loss = 0.5 * loss_acc[0, 0] / d

    late = [D1, G1, U1]
    dgate, dup = _ffn_dact("ffn_dact_1", dy_b, ex.full(D1), 0, fgate1, fup1)
    ex.grad[D1] = _wgrad_row_sharded("wgrad_down_1", act1, dy_b, True)
    ex.grad[G1], ex.grad[U1] = _wgrad_row_sharded("wgrad_gate_up_1", [dgate, dup], hf1, True)
    g3, d_f1, g3_b = run(_dgrad_cols, "dgrad_ffn_1", [dgate, dup], [ex.full(G1), ex.full(U1)], 0, True, x3, g_f1, dy,
                         w_rows=True, phases=[("swap", late)])
    ex.pair_sum(late)

    dcat2 = _dgrad_rows("dgrad_out_cd", g3_b, w2d(C_OUT))
    ex.grad[C_OUT] = _wgrad_row_sharded("wgrad_out_cd", cat2, g3_b, False)
    dprojcd, d_cw, d_cb, d_clg, d_clb, d_dw = run(_mixer_cd_bwd, projcd, dcat2, c1, cw, clg, clb, dw, phases=[("scatter", late)])
    ex.grad[C_IN] = _wgrad_col_sharded("wgrad_in_cd", h1, [dprojcd], False)[0]
    g2, d_cdn, g2_b = run(_dgrad_cols, "dgrad_in_cd", [dprojcd], [ex.full(C_IN)], 0, False, x2, g_cd, g3,
                          phases=[("swap", [C_OUT, C_IN])])
    ex.pair_sum([C_OUT, C_IN])

    dgate, dup = run(_ffn_dact, "ffn_dact_0", g2_b, ex.full(D0), 0, fgate0, fup0, phases=[("scatter", [C_OUT, C_IN])])
    ex.chip_sum(late + [C_OUT, C_IN])
    ex.grad[D0] = _wgrad_row_sharded("wgrad_down_0", act0, g2_b, True)
    ex.grad[G0], ex.grad[U0] = _wgrad_row_sharded("wgrad_gate_up_0", [dgate, dup], hf0, True)
    small = {"cd_norm_g": d_cdn, "conv_c_w": d_cw[:C_KERNEL], "conv_c_b": d_cb, "c_ln_g": d_clg, "c_ln_b": d_clb,
             "conv_d_w": d_dw[:D_KERNEL]}
    ex.grad[SMALL_SHARDED] = _split_full_small(small).reshape(N_CHIPS, 2, SHARDED_ROWS // 2, LANES)
    mid = [D0, G0, U0, SMALL_SHARDED]
    g1, d_f0, g1_b = run(_dgrad_cols, "dgrad_ffn_0", [dgate, dup], [ex.full(G0), ex.full(U0)], 0, True, x1, g_f0, g2,
                         w_rows=True, phases=[("join", late + [C_OUT, C_IN]), ("swap", mid)])
    ex.pair_sum(mid)

    dcat = _dgrad_rows("dgrad_out_ab", g1_b, w2d(A_OUT))
    ex.grad[A_OUT] = _wgrad_row_sharded("wgrad_out_ab", cat, g1_b, False)
    d_a, d_sw, d_sbt, d_slg, d_slb = _mixer_a_bwd(proj, dcat, slg, slb, sp["sgu_w"], bias_t)
    early = {"sgu_norm_g": d_slg, "sgu_norm_b": d_slb, "sgu_w": d_sw, "sgu_bias": d_sbt.T}
    ex.grad[REPLICATED_UNIT] = jnp.broadcast_to(
        _pack_replicated(early, REPLICATED_EARLY, REPLICATED_EARLY_ROWS).reshape(2, REPLICATED_EARLY_ROWS // 2, LANES),
        (N_CHIPS, 2, REPLICATED_EARLY_ROWS // 2, LANES))
    last = [A_OUT, REPLICATED_UNIT]
    dbb, dd, db_1, dd_1, db_2, dd_2 = _attn_bwd_prep(dcat, cat)
    regrouped_bwd = {1: (db_1, lse_1, dd_1), 2: (db_2, lse_2, dd_2)}
    bwd_phases = ([("scatter", [D0, SMALL_SHARDED])],
                  [("scatter", [G0]), ("swap", last)],
                  [("scatter", [U0])])
    dqs, dks, dvs = [], [], []
    for g, rate in enumerate(DIL_RATES):
        q, k, v = qkv[g]
        if rate == 1:
            db3, l3, dd3 = (a.reshape(1, t, B_WIDTH) for a in (dbb, lse_tot, dd))
        else:
            db3, l3, dd3 = regrouped_bwd[g]
        if g == 2:
            ex.pair_sum(last)
        dq, dk, dv = run(_attn_bwd, f"attn_bwd_{g}", q, k, v, db3, l3, dd3, phases=bwd_phases[g])
        if rate == 1:
            dq, dk, dv = (a.reshape(t, B_WIDTH) for a in (dq, dk, dv))
        dqs.append(dq)
        dks.append(dk)
        dvs.append(dv)
    ex.chip_sum([D0, SMALL_SHARDED, G0, U0])
    dproj, d_gains = run(_dproj_assemble, proj, d_a, dqs, dks, dvs, gains, tabs, phases=[("scatter", last), ("join", [D0, SMALL_SHARDED, G0, U0])])
    ex.chip_sum(last)
    d_gains = _fold_heads(d_gains)[0].reshape(2, N_DIL, B_WIDTH)[:, :, :HEAD_DIM]
    ex.grad[A_IN] = _wgrad_col_sharded("wgrad_in_ab", h0, [dproj], False)[0]
    ex.alone("swap_last", [("swap", [A_IN])])
    ex.pair_sum([A_IN])
    gx, d_abn = run(_dgrad_cols, "dgrad_in_ab", [dproj], [ex.full(A_IN)], 0, False, x, g_ab, g1, bf16_copy=False,
                    phases=[("join", last), ("scatter", [A_IN])])
    ex.chip_sum([A_IN])

    small.update({
        "ab_norm_g": d_abn, "sgu_norm_g": d_slg, "sgu_norm_b": d_slb, "sgu_w": d_sw, "sgu_bias": d_sbt.T,
        "q_norm_g": d_gains[0], "k_norm_g": d_gains[1], "ffn_norm_g": jnp.concatenate([d_f0, d_f1], axis=0),
    })
    return loss, gx, small


SHARDED_SMALL = ("cd_norm_g", "conv_c_w", "conv_c_b", "c_ln_g", "c_ln_b", "conv_d_w")
SHARDED_ROWS = 48
REPLICATED_EARLY = ("sgu_norm_g", "sgu_norm_b", "sgu_w", "sgu_bias")
REPLICATED_EARLY_ROWS = 528
REPLICATED_LATE = ("ab_norm_g", "q_norm_g", "k_norm_g", "ffn_norm_g", "loss")
REPLICATED_LATE_ROWS = 32
REPLICATED_SMALL = REPLICATED_EARLY + REPLICATED_LATE[:-1]


def _pack_sharded(parts):
    rows = [parts[k].reshape(-1, LANES) for k in SHARDED_SMALL]
    return _pad_rows(jnp.concatenate(rows, axis=0), SHARDED_ROWS)


def _split_full_small(small):
    per_chip = []
    for q in range(N_CHIPS):
        parts = {}
        for k in SHARDED_SMALL:
            a = small[k]
            a = a.reshape(-1, a.shape[-1])
            n = a.shape[-1] // N_CHIPS
            parts[k] = a[:, q * n:(q + 1) * n]
        per_chip.append(_pack_sharded(parts))
    return jnp.stack(per_chip)


def _unpack_sharded(pack, shapes):
    out, r = {}, 0
    for k in SHARDED_SMALL:
        n = math.prod(shapes[k]) // LANES
        out[k] = pack[r:r + n].reshape(shapes[k])
        r += n
    return out


def _gathered_small(packs, shapes):
    per_chip = [_unpack_sharded(packs[q], shapes) for q in range(N_CHIPS)]
    return {k: jnp.concatenate([pc[k] for pc in per_chip], axis=-1) for k in SHARDED_SMALL}


def _pack_replicated(small, names, total_rows):
    rows = []
    for k in names:
        a = small[k].reshape(-1)
        a = jnp.pad(a, (0, (-a.shape[0]) % LANES))
        rows.append(a.reshape(-1, LANES))
    return _pad_rows(jnp.concatenate(rows, axis=0), total_rows)


def _unpack_replicated(pack, shapes, names):
    out, r = {}, 0
    for k in names:
        size = math.prod(shapes[k])
        n = -(-size // LANES)
        out[k] = pack[r:r + n].reshape(-1)[:size].reshape(shapes[k])
        r += n
    return out


WEIGHT_ORDER = ("ab_norm_g", "ab_w_in", "sgu_norm_g", "sgu_norm_b", "sgu_w", "sgu_bias", "q_norm_g", "k_norm_g", "ab_w_out",
                "cd_norm_g", "cd_w_in", "conv_c_w", "conv_c_b", "c_ln_g", "c_ln_b", "conv_d_w", "cd_w_out", "ffn_norm_g",
                "ffn_w_gate", "ffn_w_up", "ffn_w_down")


def kernel(x, ab_norm_g, ab_w_in, sgu_norm_g, sgu_norm_b, sgu_w, sgu_bias, q_norm_g, k_norm_g, ab_w_out, cd_norm_g, cd_w_in, conv_c_w, conv_c_b, c_ln_g, c_ln_b, conv_d_w, cd_w_out, ffn_norm_g, ffn_w_gate, ffn_w_up, ffn_w_down, loss_target, m_ab_norm_g, m_ab_w_in, m_sgu_norm_g, m_sgu_norm_b, m_sgu_w, m_sgu_bias, m_q_norm_g, m_k_norm_g, m_ab_w_out, m_cd_norm_g, m_cd_w_in, m_conv_c_w, m_conv_c_b, m_c_ln_g, m_c_ln_b, m_conv_d_w, m_cd_w_out, m_ffn_norm_g, m_ffn_w_gate, m_ffn_w_up, m_ffn_w_down, v_ab_norm_g, v_ab_w_in, v_sgu_norm_g, v_sgu_norm_b, v_sgu_w, v_sgu_bias, v_q_norm_g, v_k_norm_g, v_ab_w_out, v_cd_norm_g, v_cd_w_in, v_conv_c_w, v_conv_c_b, v_c_ln_g, v_c_ln_b, v_conv_d_w, v_cd_w_out, v_ffn_norm_g, v_ffn_w_gate, v_ffn_w_up, v_ffn_w_down):
    args = dict(locals())
    ws = {k: args[k] for k in WEIGHT_ORDER}
    ms = {k: args["m_" + k] for k in WEIGHT_ORDER}
    vs = {k: args["v_" + k] for k in WEIGHT_ORDER}
    small_names = [k for k in WEIGHT_ORDER if k not in BIG]
    t, d = x.shape[1:]

    for group in (ws, ms, vs):
        for k in ROWS_MINOR:
            group[k] = jnp.swapaxes(group[k], 1, 2)
    ex = _Exchange(enabled=True)
    own_small = _pack_sharded({k: ws[k][0] for k in SHARDED_SMALL})
    ex.w[A_IN], ex.w[SMALL_SHARDED], ex.far[A_IN] = _stage_own(
        "stage_own", [(ws["ab_w_in"], 0, BF16), (own_small[None], 0, F32)])
    rest = [u for u in UNITS if u != A_IN]
    x2 = x.reshape(t, d)
    h0, staged = ex.run(_stage_rest_and_norm, x2, ws["ab_norm_g"], [(ws[name], layer) for name, layer in rest],
                        phases=[("gather_near", [A_IN]), ("gather", [SMALL_SHARDED])])
    ex.w.update(zip(rest, staged))
    sp = _gathered_small(ex.w[SMALL_SHARDED].reshape(N_CHIPS, SHARDED_ROWS, LANES), {k: ws[k].shape[1:] for k in SHARDED_SMALL})
    for k in REPLICATED_SMALL:
        sp[k] = ws[k] if k == "ffn_norm_g" else ws[k][0]

    loss, grad_x, g_small = _local_step(x2, loss_target.reshape(t, d), ex, sp, h0)

    shapes = {k: ws[k].shape for k in REPLICATED_SMALL}
    shapes["loss"] = (1,)
    g_small["loss"] = loss
    join_last, settle = ex.ride_for([("join", [A_IN])])
    late, joined = _all_reduce_small(_pack_replicated(g_small, REPLICATED_LATE, REPLICATED_LATE_ROWS), join_last)
    settle(joined)
    grad = _unpack_sharded(ex.done[SMALL_SHARDED].reshape(SHARDED_ROWS, LANES), {k: ws[k].shape for k in SHARDED_SMALL})
    grad.update(_unpack_replicated(ex.done[REPLICATED_UNIT].reshape(REPLICATED_EARLY_ROWS, LANES), shapes, REPLICATED_EARLY))
    grad.update(_unpack_replicated(late, shapes, REPLICATED_LATE))
    loss = grad.pop("loss")[0]

    delta, new_m, new_v = {}, {}, {}
    for group in BIG_BY_LAYERS:
        g_layers = [[ex.done[(k, layer)] for layer in range(ws[k].shape[0])] for k in group]
        results = _adamw_big("adamw_" + group[0], [ws[k] for k in group], g_layers, [ms[k] for k in group], [vs[k] for k in group])
        for k, outs in zip(group, results):
            if k in ROWS_MINOR:
                outs = [jnp.swapaxes(o, 1, 2) for o in outs]
            grad[k], delta[k], new_m[k], new_v[k] = outs
    d_s, m_s, v_s = _adamw_small([ws[k] for k in small_names], [grad[k] for k in small_names],
                                 [ms[k] for k in small_names], [vs[k] for k in small_names])
    for j, k in enumerate(small_names):
        delta[k], new_m[k], new_v[k] = d_s[j], m_s[j], v_s[j]

    return (loss, grad_x[None], *[grad[k] for k in WEIGHT_ORDER], *[delta[k] for k in WEIGHT_ORDER],
            *[new_m[k] for k in WEIGHT_ORDER], *[new_v[k] for k in WEIGHT_ORDER])
```
